```python
import math
import jax, jax.numpy as jnp
from jax import lax
import numpy as np

D_MODEL = 1024
BATCH = 16
SEQ = 2048
DEPTH = 4

N_MIXERS = 2
N_FOX = (DEPTH + 1) // 2
N_MLA = DEPTH // 2
FOX_HEADS = 16
FOX_HEAD_DIM = D_MODEL // FOX_HEADS
MLA_HEADS = 16
MLA_NOPE_DIM = D_MODEL // MLA_HEADS
MLA_ROPE_DIM = MLA_NOPE_DIM // 2
MLA_V_DIM = D_MODEL // MLA_HEADS
MLA_Q_RANK = D_MODEL // 4
MLA_KV_RANK = D_MODEL // 8
D_FF = 4 * D_MODEL
Q_BLOCK = 128
ROPE_THETA = 10000.0
NORM_EPS = 1e-6
N_MOD = 6

kernel_name = "hybrid_fox_mla_adaln_trunk"


def rms_norm(x, g):
    x32 = x.astype(jnp.float32)
    y = x32 * lax.rsqrt(jnp.mean(x32 * x32, axis=-1, keepdims=True) + NORM_EPS)
    return y.astype(x.dtype) * g


def causal_block_attention(logits_fn, v):
    S = v.shape[1]
    outs = []
    for qb in range(S // Q_BLOCK):
        q0, q1 = qb * Q_BLOCK, (qb + 1) * Q_BLOCK
        logits = logits_fn(q0, q1)
        allowed = jnp.arange(q1)[None, :] <= jnp.arange(q0, q1)[:, None]
        logits = jnp.where(allowed, logits, -jnp.inf)
        p = jax.nn.softmax(logits, axis=-1).astype(v.dtype)
        outs.append(jnp.einsum('bhqk,bkhd->bqhd', p, v[:, :q1]))
    return jnp.concatenate(outs, axis=1)


def rope_cos_sin(positions, dim):
    inv_freq = ROPE_THETA ** (-jnp.arange(0, dim, 2, dtype=jnp.float32) / dim)
    ang = positions.astype(jnp.float32)[..., None] * inv_freq
    return jnp.cos(ang), jnp.sin(ang)


def apply_rope(x, cos, sin):
    half = x.shape[-1] // 2
    x1, x2 = x[..., :half], x[..., half:]
    cos = cos.astype(x.dtype)
    sin = sin.astype(x.dtype)
    return jnp.concatenate([x1 * cos - x2 * sin, x2 * cos + x1 * sin], axis=-1)


def fox_mixer(h, w_in, b_f, w_out):
    B, S, _ = h.shape
    proj = h @ w_in
    q = proj[..., :D_MODEL].reshape(B, S, FOX_HEADS, FOX_HEAD_DIM)
    k = proj[..., D_MODEL:2 * D_MODEL].reshape(B, S, FOX_HEADS, FOX_HEAD_DIM)
    v = proj[..., 2 * D_MODEL:3 * D_MODEL].reshape(B, S, FOX_HEADS, FOX_HEAD_DIM)
    log_f = jax.nn.log_sigmoid((proj[..., 3 * D_MODEL:] + b_f).astype(jnp.float32))
    F = jnp.cumsum(log_f, axis=1).transpose(0, 2, 1)
    scale = FOX_HEAD_DIM ** -0.5

    def logits_fn(q0, q1):
        s = jnp.einsum('bqhd,bkhd->bhqk', q[:, q0:q1], k[:, :q1],
                       preferred_element_type=jnp.float32)
        return s * scale + (F[:, :, q0:q1, None] - F[:, :, None, :q1])

    o = causal_block_attention(logits_fn, v)
    return o.reshape(B, S, D_MODEL) @ w_out


def mla_mixer(h, cos, sin, w_dq, q_norm_g, w_uq, w_dkv, kv_norm_g, w_ukv, w_out):
    B, S, _ = h.shape
    cq = rms_norm(h @ w_dq, q_norm_g)
    q = (cq @ w_uq).reshape(B, S, MLA_HEADS, MLA_NOPE_DIM + MLA_ROPE_DIM)
    q_nope = q[..., :MLA_NOPE_DIM]
    q_rope = apply_rope(q[..., MLA_NOPE_DIM:], cos[:, :, None, :], sin[:, :, None, :])
    dkv = h @ w_dkv
    ckv = rms_norm(dkv[..., :MLA_KV_RANK], kv_norm_g)
    k_rope = apply_rope(dkv[..., MLA_KV_RANK:], cos, sin)
    kv = (ckv @ w_ukv).reshape(B, S, MLA_HEADS, MLA_NOPE_DIM + MLA_V_DIM)
    k_nope = kv[..., :MLA_NOPE_DIM]
    v = kv[..., MLA_NOPE_DIM:]
    scale = (MLA_NOPE_DIM + MLA_ROPE_DIM) ** -0.5

    def logits_fn(q0, q1):
        s = jnp.einsum('bqhd,bkhd->bhqk', q_nope[:, q0:q1], k_nope[:, :q1],
                       preferred_element_type=jnp.float32)
        s = s + jnp.einsum('bqhd,bkd->bhqk', q_rope[:, q0:q1], k_rope[:, :q1],
                           preferred_element_type=jnp.float32)
        return s * scale

    o = causal_block_attention(logits_fn, v)
    return o.reshape(B, S, MLA_HEADS * MLA_V_DIM) @ w_out


def sq_relu_mlp(h, w1, w2):
    a = jax.nn.relu(h @ w1)
    return (a * a) @ w2


def _fwd_setup_inputs(seed: int = 0) -> dict:
    key = jax.random.key(seed)
    ks = iter(jax.random.split(key, 32))
    D = D_MODEL

    def nrm(shape, fan_in, mult=1.0):
        return jax.random.normal(next(ks), shape, jnp.float32) * (mult * fan_in ** -0.5)

    def gain(shape):
        return 1.0 + 0.02 * jax.random.normal(next(ks), shape, jnp.float32)

    x = jax.random.normal(next(ks), (BATCH, SEQ, D), jnp.float32)
    c = jax.random.normal(next(ks), (BATCH, D), jnp.float32)
    offs = jax.random.randint(next(ks), (BATCH, 1), 0, 4096, dtype=jnp.int32)
    positions = (jnp.arange(SEQ, dtype=jnp.int32)[None, :] + offs).astype(jnp.int32)
    return {
        "x": x,
        "c": c,
        "positions": positions,
        "ada_w": nrm((DEPTH, D, N_MOD * D), D),
        "ada_b": 0.02 * jax.random.normal(next(ks), (DEPTH, N_MOD * D), jnp.float32),
        "norm_mix_g": gain((DEPTH, D)),
        "norm_mlp_g": gain((DEPTH, D)),
        "fox_w_in": nrm((N_FOX, D, 3 * D + FOX_HEADS), D),
        "fox_b_f": jax.random.uniform(next(ks), (N_FOX, FOX_HEADS), jnp.float32, 2.0, 6.0),
        "fox_w_out": nrm((N_FOX, D, D), D),
        "mla_w_dq": nrm((N_MLA, D, MLA_Q_RANK), D),
        "mla_q_norm_g": gain((N_MLA, MLA_Q_RANK)),
        "mla_w_uq": nrm((N_MLA, MLA_Q_RANK, MLA_HEADS * (MLA_NOPE_DIM + MLA_ROPE_DIM)), MLA_Q_RANK),
        "mla_w_dkv": nrm((N_MLA, D, MLA_KV_RANK + MLA_ROPE_DIM), D),
        "mla_kv_norm_g": gain((N_MLA, MLA_KV_RANK)),
        "mla_w_ukv": nrm((N_MLA, MLA_KV_RANK, MLA_HEADS * (MLA_NOPE_DIM + MLA_V_DIM)), MLA_KV_RANK),
        "mla_w_out": nrm((N_MLA, MLA_HEADS * MLA_V_DIM, D), MLA_HEADS * MLA_V_DIM),
        "mlp_w1": nrm((DEPTH, D, D_FF), D),
        "mlp_w2": nrm((DEPTH, D_FF, D), D_FF),
        "final_norm_g": gain((D,)),
    }


def _fwd_reference(x, c, positions, ada_w, ada_b, norm_mix_g, norm_mlp_g,
              fox_w_in, fox_b_f, fox_w_out,
              mla_w_dq, mla_q_norm_g, mla_w_uq, mla_w_dkv, mla_kv_norm_g, mla_w_ukv, mla_w_out,
              mlp_w1, mlp_w2, final_norm_g):
    cos, sin = rope_cos_sin(positions, MLA_ROPE_DIM)
    c_act = jax.nn.silu(c)
    for i in range(DEPTH):
        mod = (c_act @ ada_w[i] + ada_b[i])[:, None, :]
        sh_m, sc_m, g_m, sh_f, sc_f, g_f = jnp.split(mod, N_MOD, axis=-1)
        h = rms_norm(x, norm_mix_g[i]) * (1 + sc_m) + sh_m
        j = i // N_MIXERS
        if i % N_MIXERS == 0:
            y = fox_mixer(h, fox_w_in[j], fox_b_f[j], fox_w_out[j])
        else:
            y = mla_mixer(h, cos, sin, mla_w_dq[j], mla_q_norm_g[j], mla_w_uq[j],
                          mla_w_dkv[j], mla_kv_norm_g[j], mla_w_ukv[j], mla_w_out[j])
        x = x + g_m * y
        h = rms_norm(x, norm_mlp_g[i]) * (1 + sc_f) + sh_f
        x = x + g_f * sq_relu_mlp(h, mlp_w1[i], mlp_w2[i])
    return rms_norm(x, final_norm_g)


import jax as _jax
import jax.numpy as _jnp

TWIN_FORMAT = 'train_step'
FWD_PARAMS = ['x', 'c', 'positions', 'ada_w', 'ada_b', 'norm_mix_g', 'norm_mlp_g', 'fox_w_in', 'fox_b_f', 'fox_w_out', 'mla_w_dq', 'mla_q_norm_g', 'mla_w_uq', 'mla_w_dkv', 'mla_kv_norm_g', 'mla_w_ukv', 'mla_w_out', 'mlp_w1', 'mlp_w2', 'final_norm_g']
TWIN_WEIGHTS = ['ada_w', 'ada_b', 'norm_mix_g', 'norm_mlp_g', 'fox_w_in', 'fox_b_f', 'fox_w_out', 'mla_w_dq', 'mla_q_norm_g', 'mla_w_uq', 'mla_w_dkv', 'mla_kv_norm_g', 'mla_w_ukv', 'mla_w_out', 'mlp_w1', 'mlp_w2', 'final_norm_g']
TWIN_DIFF_INPUT = 'x'
TWIN_INPUTS = ['x', 'c', 'positions', 'ada_w', 'ada_b', 'norm_mix_g', 'norm_mlp_g', 'fox_w_in', 'fox_b_f', 'fox_w_out', 'mla_w_dq', 'mla_q_norm_g', 'mla_w_uq', 'mla_w_dkv', 'mla_kv_norm_g', 'mla_w_ukv', 'mla_w_out', 'mlp_w1', 'mlp_w2', 'final_norm_g', 'loss_target', 'm_ada_w', 'm_ada_b', 'm_norm_mix_g', 'm_norm_mlp_g', 'm_fox_w_in', 'm_fox_b_f', 'm_fox_w_out', 'm_mla_w_dq', 'm_mla_q_norm_g', 'm_mla_w_uq', 'm_mla_w_dkv', 'm_mla_kv_norm_g', 'm_mla_w_ukv', 'm_mla_w_out', 'm_mlp_w1', 'm_mlp_w2', 'm_final_norm_g', 'v_ada_w', 'v_ada_b', 'v_norm_mix_g', 'v_norm_mlp_g', 'v_fox_w_in', 'v_fox_b_f', 'v_fox_w_out', 'v_mla_w_dq', 'v_mla_q_norm_g', 'v_mla_w_uq', 'v_mla_w_dkv', 'v_mla_kv_norm_g', 'v_mla_w_ukv', 'v_mla_w_out', 'v_mlp_w1', 'v_mlp_w2', 'v_final_norm_g']
TWIN_OUTPUTS = ['loss', 'grad_x', 'grad_ada_w', 'grad_ada_b', 'grad_norm_mix_g', 'grad_norm_mlp_g', 'grad_fox_w_in', 'grad_fox_b_f', 'grad_fox_w_out', 'grad_mla_w_dq', 'grad_mla_q_norm_g', 'grad_mla_w_uq', 'grad_mla_w_dkv', 'grad_mla_kv_norm_g', 'grad_mla_w_ukv', 'grad_mla_w_out', 'grad_mlp_w1', 'grad_mlp_w2', 'grad_final_norm_g', 'delta_ada_w', 'delta_ada_b', 'delta_norm_mix_g', 'delta_norm_mlp_g', 'delta_fox_w_in', 'delta_fox_b_f', 'delta_fox_w_out', 'delta_mla_w_dq', 'delta_mla_q_norm_g', 'delta_mla_w_uq', 'delta_mla_w_dkv', 'delta_mla_kv_norm_g', 'delta_mla_w_ukv', 'delta_mla_w_out', 'delta_mlp_w1', 'delta_mlp_w2', 'delta_final_norm_g', 'new_m_ada_w', 'new_m_ada_b', 'new_m_norm_mix_g', 'new_m_norm_mlp_g', 'new_m_fox_w_in', 'new_m_fox_b_f', 'new_m_fox_w_out', 'new_m_mla_w_dq', 'new_m_mla_q_norm_g', 'new_m_mla_w_uq', 'new_m_mla_w_dkv', 'new_m_mla_kv_norm_g', 'new_m_mla_w_ukv', 'new_m_mla_w_out', 'new_m_mlp_w1', 'new_m_mlp_w2', 'new_m_final_norm_g', 'new_v_ada_w', 'new_v_ada_b', 'new_v_norm_mix_g', 'new_v_norm_mlp_g', 'new_v_fox_w_in', 'new_v_fox_b_f', 'new_v_fox_w_out', 'new_v_mla_w_dq', 'new_v_mla_q_norm_g', 'new_v_mla_w_uq', 'new_v_mla_w_dkv', 'new_v_mla_kv_norm_g', 'new_v_mla_w_ukv', 'new_v_mla_w_out', 'new_v_mlp_w1', 'new_v_mlp_w2', 'new_v_final_norm_g']
TWIN_LEAF_KINDS = {'loss': 'loss', 'grad_x': 'grad_x', 'grad_ada_w': 'grad_w', 'grad_ada_b': 'grad_w', 'grad_norm_mix_g': 'grad_w', 'grad_norm_mlp_g': 'grad_w', 'grad_fox_w_in': 'grad_w', 'grad_fox_b_f': 'grad_w', 'grad_fox_w_out': 'grad_w', 'grad_mla_w_dq': 'grad_w', 'grad_mla_q_norm_g': 'grad_w', 'grad_mla_w_uq': 'grad_w', 'grad_mla_w_dkv': 'grad_w', 'grad_mla_kv_norm_g': 'grad_w', 'grad_mla_w_ukv': 'grad_w', 'grad_mla_w_out': 'grad_w', 'grad_mlp_w1': 'grad_w', 'grad_mlp_w2': 'grad_w', 'grad_final_norm_g': 'grad_w', 'delta_ada_w': 'delta_w', 'delta_ada_b': 'delta_w', 'delta_norm_mix_g': 'delta_w', 'delta_norm_mlp_g': 'delta_w', 'delta_fox_w_in': 'delta_w', 'delta_fox_b_f': 'delta_w', 'delta_fox_w_out': 'delta_w', 'delta_mla_w_dq': 'delta_w', 'delta_mla_q_norm_g': 'delta_w', 'delta_mla_w_uq': 'delta_w', 'delta_mla_w_dkv': 'delta_w', 'delta_mla_kv_norm_g': 'delta_w', 'delta_mla_w_ukv': 'delta_w', 'delta_mla_w_out': 'delta_w', 'delta_mlp_w1': 'delta_w', 'delta_mlp_w2': 'delta_w', 'delta_final_norm_g': 'delta_w', 'new_m_ada_w': 'new_m', 'new_m_ada_b': 'new_m', 'new_m_norm_mix_g': 'new_m', 'new_m_norm_mlp_g': 'new_m', 'new_m_fox_w_in': 'new_m', 'new_m_fox_b_f': 'new_m', 'new_m_fox_w_out': 'new_m', 'new_m_mla_w_dq': 'new_m', 'new_m_mla_q_norm_g': 'new_m', 'new_m_mla_w_uq': 'new_m', 'new_m_mla_w_dkv': 'new_m', 'new_m_mla_kv_norm_g': 'new_m', 'new_m_mla_w_ukv': 'new_m', 'new_m_mla_w_out': 'new_m', 'new_m_mlp_w1': 'new_m', 'new_m_mlp_w2': 'new_m', 'new_m_final_norm_g': 'new_m', 'new_v_ada_w': 'new_v', 'new_v_ada_b': 'new_v', 'new_v_norm_mix_g': 'new_v', 'new_v_norm_mlp_g': 'new_v', 'new_v_fox_w_in': 'new_v', 'new_v_fox_b_f': 'new_v', 'new_v_fox_w_out': 'new_v', 'new_v_mla_w_dq': 'new_v', 'new_v_mla_q_norm_g': 'new_v', 'new_v_mla_w_uq': 'new_v', 'new_v_mla_w_dkv': 'new_v', 'new_v_mla_kv_norm_g': 'new_v', 'new_v_mla_w_ukv': 'new_v', 'new_v_mla_w_out': 'new_v', 'new_v_mlp_w1': 'new_v', 'new_v_mlp_w2': 'new_v', 'new_v_final_norm_g': 'new_v'}


def _forward(args):
    return _fwd_reference(*[args[k] for k in FWD_PARAMS])


def _output_shape():
    out = _jax.eval_shape(lambda: _forward(_fwd_setup_inputs(0)))
    return out.shape, out.dtype

N_MICROBATCH = 1
ADAM_LR = 0.001
ADAM_B1 = 0.9
ADAM_B2 = 0.999
ADAM_EPS = 1e-08
ADAM_WD = 0.01
ADAM_STEP = 10
PER_EXAMPLE_BATCH_AXIS = {'x': 0, 'c': 0, 'positions': 0, 'loss_target': 0}
SHARED_INPUTS = []
_WEIGHT_DTYPES = {'ada_w': _jnp.float32, 'ada_b': _jnp.float32, 'norm_mix_g': _jnp.float32, 'norm_mlp_g': _jnp.float32, 'fox_w_in': _jnp.float32, 'fox_b_f': _jnp.float32, 'fox_w_out': _jnp.float32, 'mla_w_dq': _jnp.float32, 'mla_q_norm_g': _jnp.float32, 'mla_w_uq': _jnp.float32, 'mla_w_dkv': _jnp.float32, 'mla_kv_norm_g': _jnp.float32, 'mla_w_ukv': _jnp.float32, 'mla_w_out': _jnp.float32, 'mlp_w1': _jnp.float32, 'mlp_w2': _jnp.float32, 'final_norm_g': _jnp.float32}
MOMENT_SCALE = {'ada_w': 8.406224e-02, 'ada_b': 1.420003e-01, 'norm_mix_g': 5.317092e-02, 'norm_mlp_g': 1.183146e-01, 'fox_w_in': 5.686259e-02, 'fox_b_f': 1.323384e-01, 'fox_w_out': 8.472550e-02, 'mla_w_dq': 1.276956e-02, 'mla_q_norm_g': 1.333140e-02, 'mla_w_uq': 5.152231e-03, 'mla_w_dkv': 1.415652e-01, 'mla_kv_norm_g': 1.670908e-01, 'mla_w_ukv': 3.799369e-02, 'mla_w_out': 5.574292e-02, 'mlp_w1': 7.154538e-02, 'mlp_w2': 1.407458e-01, 'final_norm_g': 3.475129e+01}


def _to_microbatches(a, axis):
    t = _jnp.moveaxis(a, axis, 0)
    t = t.reshape((N_MICROBATCH, t.shape[0] // N_MICROBATCH) + t.shape[1:])
    return _jnp.moveaxis(t, 1, axis + 1)


def setup_inputs(seed: int = 0) -> dict:
    inp = _fwd_setup_inputs(seed)
    key = _jax.random.fold_in(_jax.random.key(seed), 7919)
    shape, _ = _output_shape()
    out = dict(inp)
    out["loss_target"] = _jax.random.normal(_jax.random.fold_in(key, 0), shape, _jnp.float32)
    for i, name in enumerate(TWIN_WEIGHTS):
        w = inp[name].astype(_jnp.float32)
        if MOMENT_SCALE is None:
            s = _jnp.sqrt(_jnp.mean(_jnp.square(w)) + 1e-30)
        else:
            s = MOMENT_SCALE[name]
        km, kv = _jax.random.split(_jax.random.fold_in(key, i + 1))
        out[name] = w
        out["m_" + name] = s * _jax.random.normal(km, w.shape, _jnp.float32)
        out["v_" + name] = (s * s) * _jax.random.uniform(kv, w.shape, _jnp.float32, 0.5, 1.5)
    if N_MICROBATCH > 1:
        for name, axis in PER_EXAMPLE_BATCH_AXIS.items():
            out[name] = _to_microbatches(out[name], axis)
    return {'x': out['x'], 'c': out['c'], 'positions': out['positions'], 'ada_w': out['ada_w'], 'ada_b': out['ada_b'], 'norm_mix_g': out['norm_mix_g'], 'norm_mlp_g': out['norm_mlp_g'], 'fox_w_in': out['fox_w_in'], 'fox_b_f': out['fox_b_f'], 'fox_w_out': out['fox_w_out'], 'mla_w_dq': out['mla_w_dq'], 'mla_q_norm_g': out['mla_q_norm_g'], 'mla_w_uq': out['mla_w_uq'], 'mla_w_dkv': out['mla_w_dkv'], 'mla_kv_norm_g': out['mla_kv_norm_g'], 'mla_w_ukv': out['mla_w_ukv'], 'mla_w_out': out['mla_w_out'], 'mlp_w1': out['mlp_w1'], 'mlp_w2': out['mlp_w2'], 'final_norm_g': out['final_norm_g'], 'loss_target': out['loss_target'], 'm_ada_w': out['m_ada_w'], 'm_ada_b': out['m_ada_b'], 'm_norm_mix_g': out['m_norm_mix_g'], 'm_norm_mlp_g': out['m_norm_mlp_g'], 'm_fox_w_in': out['m_fox_w_in'], 'm_fox_b_f': out['m_fox_b_f'], 'm_fox_w_out': out['m_fox_w_out'], 'm_mla_w_dq': out['m_mla_w_dq'], 'm_mla_q_norm_g': out['m_mla_q_norm_g'], 'm_mla_w_uq': out['m_mla_w_uq'], 'm_mla_w_dkv': out['m_mla_w_dkv'], 'm_mla_kv_norm_g': out['m_mla_kv_norm_g'], 'm_mla_w_ukv': out['m_mla_w_ukv'], 'm_mla_w_out': out['m_mla_w_out'], 'm_mlp_w1': out['m_mlp_w1'], 'm_mlp_w2': out['m_mlp_w2'], 'm_final_norm_g': out['m_final_norm_g'], 'v_ada_w': out['v_ada_w'], 'v_ada_b': out['v_ada_b'], 'v_norm_mix_g': out['v_norm_mix_g'], 'v_norm_mlp_g': out['v_norm_mlp_g'], 'v_fox_w_in': out['v_fox_w_in'], 'v_fox_b_f': out['v_fox_b_f'], 'v_fox_w_out': out['v_fox_w_out'], 'v_mla_w_dq': out['v_mla_w_dq'], 'v_mla_q_norm_g': out['v_mla_q_norm_g'], 'v_mla_w_uq': out['v_mla_w_uq'], 'v_mla_w_dkv': out['v_mla_w_dkv'], 'v_mla_kv_norm_g': out['v_mla_kv_norm_g'], 'v_mla_w_ukv': out['v_mla_w_ukv'], 'v_mla_w_out': out['v_mla_w_out'], 'v_mlp_w1': out['v_mlp_w1'], 'v_mlp_w2': out['v_mlp_w2'], 'v_final_norm_g': out['v_final_norm_g']}


def _loss(weights, diff, rest, loss_target):
    with _jax.named_scope("forward"):
        args = {**rest, TWIN_DIFF_INPUT: diff, **{k: w.astype(_WEIGHT_DTYPES[k]) for k, w in weights.items()}}
        y = _forward(args)
    with _jax.named_scope("loss_head"):
        err = _jnp.square(y.astype(_jnp.float32) - loss_target)
        return 0.5 * _jnp.sum(_jnp.mean(err, axis=-1)) if err.ndim else 0.5 * err


def _adamw(w, g, m, v):
    m = ADAM_B1 * m + (1.0 - ADAM_B1) * g
    v = ADAM_B2 * v + (1.0 - ADAM_B2) * _jnp.square(g)
    m_hat = m / (1.0 - ADAM_B1 ** ADAM_STEP)
    v_hat = v / (1.0 - ADAM_B2 ** ADAM_STEP)
    delta = -ADAM_LR * (m_hat / (_jnp.sqrt(v_hat) + ADAM_EPS) + ADAM_WD * w)
    return delta, m, v


def reference(x, c, positions, ada_w, ada_b, norm_mix_g, norm_mlp_g, fox_w_in, fox_b_f, fox_w_out, mla_w_dq, mla_q_norm_g, mla_w_uq, mla_w_dkv, mla_kv_norm_g, mla_w_ukv, mla_w_out, mlp_w1, mlp_w2, final_norm_g, loss_target, m_ada_w, m_ada_b, m_norm_mix_g, m_norm_mlp_g, m_fox_w_in, m_fox_b_f, m_fox_w_out, m_mla_w_dq, m_mla_q_norm_g, m_mla_w_uq, m_mla_w_dkv, m_mla_kv_norm_g, m_mla_w_ukv, m_mla_w_out, m_mlp_w1, m_mlp_w2, m_final_norm_g, v_ada_w, v_ada_b, v_norm_mix_g, v_norm_mlp_g, v_fox_w_in, v_fox_b_f, v_fox_w_out, v_mla_w_dq, v_mla_q_norm_g, v_mla_w_uq, v_mla_w_dkv, v_mla_kv_norm_g, v_mla_w_ukv, v_mla_w_out, v_mlp_w1, v_mlp_w2, v_final_norm_g):
    given = dict(x=x, c=c, positions=positions, ada_w=ada_w, ada_b=ada_b, norm_mix_g=norm_mix_g, norm_mlp_g=norm_mlp_g, fox_w_in=fox_w_in, fox_b_f=fox_b_f, fox_w_out=fox_w_out, mla_w_dq=mla_w_dq, mla_q_norm_g=mla_q_norm_g, mla_w_uq=mla_w_uq, mla_w_dkv=mla_w_dkv, mla_kv_norm_g=mla_kv_norm_g, mla_w_ukv=mla_w_ukv, mla_w_out=mla_w_out, mlp_w1=mlp_w1, mlp_w2=mlp_w2, final_norm_g=final_norm_g, loss_target=loss_target, m_ada_w=m_ada_w, m_ada_b=m_ada_b, m_norm_mix_g=m_norm_mix_g, m_norm_mlp_g=m_norm_mlp_g, m_fox_w_in=m_fox_w_in, m_fox_b_f=m_fox_b_f, m_fox_w_out=m_fox_w_out, m_mla_w_dq=m_mla_w_dq, m_mla_q_norm_g=m_mla_q_norm_g, m_mla_w_uq=m_mla_w_uq, m_mla_w_dkv=m_mla_w_dkv, m_mla_kv_norm_g=m_mla_kv_norm_g, m_mla_w_ukv=m_mla_w_ukv, m_mla_w_out=m_mla_w_out, m_mlp_w1=m_mlp_w1, m_mlp_w2=m_mlp_w2, m_final_norm_g=m_final_norm_g, v_ada_w=v_ada_w, v_ada_b=v_ada_b, v_norm_mix_g=v_norm_mix_g, v_norm_mlp_g=v_norm_mlp_g, v_fox_w_in=v_fox_w_in, v_fox_b_f=v_fox_b_f, v_fox_w_out=v_fox_w_out, v_mla_w_dq=v_mla_w_dq, v_mla_q_norm_g=v_mla_q_norm_g, v_mla_w_uq=v_mla_w_uq, v_mla_w_dkv=v_mla_w_dkv, v_mla_kv_norm_g=v_mla_kv_norm_g, v_mla_w_ukv=v_mla_w_ukv, v_mla_w_out=v_mla_w_out, v_mlp_w1=v_mlp_w1, v_mlp_w2=v_mlp_w2, v_final_norm_g=v_final_norm_g)
    weights = {n: given[n] for n in TWIN_WEIGHTS}
    shared = {n: given[n] for n in SHARED_INPUTS}
    per_example = {n: given[n] for n in ['x', 'c', 'positions']}
    grad_fn = _jax.value_and_grad(_loss, argnums=(0, 1))

    def one_microbatch(ex, loss_target):
        ex = dict(ex)
        diff = ex.pop(TWIN_DIFF_INPUT)
        return grad_fn(weights, diff, {**shared, **ex}, loss_target)

    if N_MICROBATCH == 1:
        loss, (grad_w, grad_x) = one_microbatch(per_example, given["loss_target"])
    else:
        def body(carry, xs):
            loss_sum, grad_sum = carry
            l_k, (gw_k, gx_k) = one_microbatch(xs[0], xs[1])
            with _jax.named_scope("update"):
                return (loss_sum + l_k, _jax.tree.map(_jnp.add, grad_sum, gw_k)), gx_k

        init = (_jnp.zeros((), _jnp.float32), _jax.tree.map(_jnp.zeros_like, weights))
        (loss, grad_w), grad_x = _jax.lax.scan(body, init, (per_example, given["loss_target"]))
    with _jax.named_scope("update"):
        delta_w, new_m, new_v = {}, {}, {}
        for n in TWIN_WEIGHTS:
            delta_w[n], new_m[n], new_v[n] = _adamw(weights[n], grad_w[n], given["m_" + n], given["v_" + n])
    return (loss, grad_x, *[grad_w[n] for n in TWIN_WEIGHTS], *[delta_w[n] for n in TWIN_WEIGHTS],
            *[new_m[n] for n in TWIN_WEIGHTS], *[new_v[n] for n in TWIN_WEIGHTS])
```

```python
import functools

import numpy as np
import jax
import jax.numpy as jnp
from jax import lax
from jax.experimental import pallas as pl
from jax.experimental.pallas import tpu as pltpu

F32 = jnp.float32
BF16 = jnp.bfloat16
MESH_ID = pl.DeviceIdType.MESH

NORM_EPS = 1e-6
ROPE_THETA = 10000.0
HEAD_DIM = 64
ROPE_DIM = 32
KV_RANK = 128
FOX_EXTRA = 6
PAIR_Q = 256
PAIR_KV = 384
LANES = 128
ADAM_LR = 0.001
ADAM_B1 = 0.9
ADAM_B2 = 0.999
ADAM_EPS = 1e-08
ADAM_WD = 0.01
ADAM_STEP = 10
VMEM_LIMIT_V7X = 48 * 1024 * 1024
NEG_BIG = -1e30
PACK_W = 1024

BIG_WEIGHTS = (("fox_w_in", 2), ("fox_w_out", 1), ("mla_w_dq", 1), ("mla_w_uq", 2), ("mla_w_dkv", 1),
               ("mla_w_ukv", 2), ("mla_w_out", 1), ("mlp_w1", 2), ("mlp_w2", 1))


def _cparams(sem=None):
    return pltpu.CompilerParams(dimension_semantics=sem, vmem_limit_bytes=VMEM_LIMIT_V7X)


def _tile(n, want):
    if n <= want:
        return n
    for t in range(want - want % LANES, 0, -LANES):
        if n % t == 0:
            return t
    raise ValueError((n, want))


def _mm(a, b, mode, *, name, out_dtypes=(F32,), epilogue=None, extras=(), rowvecs=(), tables=(),
        seq=None, a_off=0, a_sz=None, tm=512, tn=512, tk=1024):
    if mode == "nn":
        M, K, N = a.shape[0], b.shape[0], b.shape[1]
    elif mode == "nt":
        M, K, N = a.shape[0], b.shape[1], b.shape[0]
    else:
        K, N = b.shape
        M = a.shape[1] - a_off if a_sz is None else a_sz
    if a_sz is not None and mode != "tn":
        assert a_sz == K
    tm, tn = _tile(seq if rowvecs else M, tm), _tile(N, tn)
    tk = _tile(K, 512 if mode == "tn" else tk)
    nk = K // tk
    ne, nr, nt_ = len(extras), len(rowvecs), len(tables)
    no = len(out_dtypes)

    if mode == "tn":
        assert a_off % tm == 0
        a_spec = pl.BlockSpec((tk, tm), lambda i, j, k: (k, i + a_off // tm))
        dims = (((0,), (0,)), ((), ()))
    else:
        assert a_off % tk == 0
        a_spec = pl.BlockSpec((tm, tk), lambda i, j, k: (i, k + a_off // tk))
        dims = (((1,), (0,)), ((), ())) if mode == "nn" else (((1,), (1,)), ((), ()))
    if mode == "nt":
        b_spec = pl.BlockSpec((tn, tk), lambda i, j, k: (j, k))
    else:
        b_spec = pl.BlockSpec((tk, tn), lambda i, j, k: (k, j))
    in_specs = [a_spec, b_spec]
    in_specs += [pl.BlockSpec((tm, tn), lambda i, j, k: (i, j)) for _ in extras]
    if rowvecs:
        assert seq % tm == 0
        per = seq // tm
        in_specs += [pl.BlockSpec((None, 1, tn), lambda i, j, k: (i // per, 0, j)) for _ in rowvecs]
    in_specs += [pl.BlockSpec((tm, LANES), lambda i, j, k: (i, 0)) for _ in tables]
    out_specs = [pl.BlockSpec((tm, tn), lambda i, j, k: (i, j)) for _ in out_dtypes]
    out_shape = [jax.ShapeDtypeStruct((M, N), d) for d in out_dtypes]

    def body(*refs):
        a_ref, b_ref = refs[0], refs[1]
        side = refs[2:2 + ne + nr + nt_]
        outs = refs[2 + ne + nr + nt_:2 + ne + nr + nt_ + no]

        def finish(acc):
            res = (acc,) if epilogue is None else epilogue(acc, *[r[...] for r in side])
            for o_ref, r in zip(outs, res):
                o_ref[...] = r.astype(o_ref.dtype)

        part = lax.dot_general(a_ref[...].astype(BF16), b_ref[...].astype(BF16), dims,
                               preferred_element_type=F32)
        if nk == 1:
            finish(part)
        else:
            acc_ref = refs[-1]
            k = pl.program_id(2)

            @pl.when(k == 0)
            def _():
                acc_ref[...] = part

            @pl.when(k > 0)
            def _():
                acc_ref[...] += part

            @pl.when(k == nk - 1)
            def _():
                finish(acc_ref[...])

    res = pl.pallas_call(
        body, name=name, grid=(M // tm, N // tn, nk), in_specs=in_specs, out_specs=out_specs,
        out_shape=out_shape, scratch_shapes=[pltpu.VMEM((tm, tn), F32)] if nk > 1 else [],
        compiler_params=_cparams(("parallel", "parallel", "arbitrary")),
    )(a, b, *extras, *rowvecs, *tables)
    return res[0] if no == 1 else tuple(res)


def _rope128(x, cos_t, sin_s):
    lane = lax.broadcasted_iota(jnp.int32, x.shape, 1)
    first = (lane % ROPE_DIM) < (ROPE_DIM // 2)
    swapped = jnp.where(first, pltpu.roll(x, LANES - ROPE_DIM // 2, 1), pltpu.roll(x, ROPE_DIM // 2, 1))
    return x * cos_t + swapped * sin_s


def _rope_pairs(acc, cos_t, sin_s, sign):
    parts = []
    for p in range(acc.shape[1] // PAIR_Q):
        parts.append(acc[:, p * PAIR_Q:p * PAIR_Q + LANES])
        parts.append(_rope128(acc[:, p * PAIR_Q + LANES:(p + 1) * PAIR_Q], cos_t, sign * sin_s))
    return jnp.concatenate(parts, axis=1)


def _rope_tables(pos_f, inv_freq_row, sign_row):
    T = pos_f.shape[0]
    tt = _tile(T, 512)

    def body(p_ref, f_ref, s_ref, cos_ref, sin_ref):
        ang = p_ref[...] * f_ref[...]
        cos_ref[...] = jnp.cos(ang)
        sin_ref[...] = jnp.sin(ang) * s_ref[...]

    return pl.pallas_call(
        body, name="rope_tables", grid=(T // tt,),
        in_specs=[pl.BlockSpec((tt, 1), lambda i: (i, 0)), pl.BlockSpec((1, LANES), lambda i: (0, 0)),
                  pl.BlockSpec((1, LANES), lambda i: (0, 0))],
        out_specs=[pl.BlockSpec((tt, LANES), lambda i: (i, 0))] * 2,
        out_shape=[jax.ShapeDtypeStruct((T, LANES), F32)] * 2,
        compiler_params=_cparams(("parallel",)),
    )(pos_f, inv_freq_row, sign_row)


def _unrope(dqx, cos_t, sin_s):
    T, W = dqx.shape
    tt = _tile(T, 512)

    def body(d_ref, c_ref, s_ref, o_ref):
        o_ref[...] = _rope_pairs(d_ref[...].astype(F32), c_ref[...], s_ref[...], -1.0).astype(BF16)

    return pl.pallas_call(
        body, name="mla_unrope", grid=(T // tt,),
        in_specs=[pl.BlockSpec((tt, W), lambda i: (i, 0)), pl.BlockSpec((tt, LANES), lambda i: (i, 0)),
                  pl.BlockSpec((tt, LANES), lambda i: (i, 0))],
        out_specs=pl.BlockSpec((tt, W), lambda i: (i, 0)),
        out_shape=jax.ShapeDtypeStruct((T, W), BF16),
        compiler_params=_cparams(("parallel",)),
    )(dqx, cos_t, sin_s)


def _row_specs(tt, D, per, n):
    return [pl.BlockSpec((None, 1, D), lambda i: (i // per, 0, 0)) for _ in range(n)]


def _norm_mod(x, gain, sc, sh, *, S, name):
    T, D = x.shape
    tt = _tile(S, 512)
    per = S // tt

    def body(x_ref, g_ref, sc_ref, sh_ref, h_ref):
        xv = x_ref[...]
        r = lax.rsqrt(jnp.mean(xv * xv, axis=-1, keepdims=True) + NORM_EPS)
        h_ref[...] = ((xv * r) * g_ref[...] * (1.0 + sc_ref[...]) + sh_ref[...]).astype(BF16)

    return pl.pallas_call(
        body, name=name, grid=(T // tt,),
        in_specs=[pl.BlockSpec((tt, D), lambda i: (i, 0)), pl.BlockSpec((1, D), lambda i: (0, 0))]
        + _row_specs(tt, D, per, 2),
        out_specs=pl.BlockSpec((tt, D), lambda i: (i, 0)),
        out_shape=jax.ShapeDtypeStruct((T, D), BF16),
        compiler_params=_cparams(("parallel",)),
    )(x, gain, sc, sh)


def _norm_mod_bwd(x, dh, dres, gain, sc, *, S, name):
    T, D = x.shape
    B = T // S
    tt = _tile(S, 512)
    per = S // tt

    def body(x_ref, dh_ref, dres_ref, g_ref, sc_ref, dx_ref, dsh_ref, dsc_ref, dg_ref):
        i = pl.program_id(0)
        xv = x_ref[...]
        dhv = dh_ref[...].astype(F32)
        r = lax.rsqrt(jnp.mean(xv * xv, axis=-1, keepdims=True) + NORM_EPS)
        n = xv * r
        g = g_ref[...]
        one_sc = 1.0 + sc_ref[...]
        dn = dhv * (g * one_sc)
        dx_ref[...] = dres_ref[...] + r * (dn - n * jnp.mean(dn * n, axis=-1, keepdims=True))
        dhn = dhv * n

        @pl.when(i % per == 0)
        def _():
            dsh_ref[...] = jnp.zeros_like(dsh_ref)
            dsc_ref[...] = jnp.zeros_like(dsc_ref)

        @pl.when(i == 0)
        def _():
            dg_ref[...] = jnp.zeros_like(dg_ref)

        dsh_ref[...] += jnp.sum(dhv, axis=0, keepdims=True)
        dsc_ref[...] += jnp.sum(dhn, axis=0, keepdims=True) * g
        dg_ref[...] += jnp.sum(dhn, axis=0, keepdims=True) * one_sc

    return pl.pallas_call(
        body, name=name, grid=(T // tt,),
        in_specs=[pl.BlockSpec((tt, D), lambda i: (i, 0))] * 3 + [pl.BlockSpec((1, D), lambda i: (0, 0))]
        + _row_specs(tt, D, per, 1),
        out_specs=[pl.BlockSpec((tt, D), lambda i: (i, 0))] + _row_specs(tt, D, per, 2)
        + [pl.BlockSpec((1, D), lambda i: (0, 0))],
        out_shape=[jax.ShapeDtypeStruct((T, D), F32), jax.ShapeDtypeStruct((B, 1, D), F32),
                   jax.ShapeDtypeStruct((B, 1, D), F32), jax.ShapeDtypeStruct((1, D), F32)],
        compiler_params=_cparams(("arbitrary",)),
    )(x, dh, dres, gain, sc)


def _gate_bwd(dx, y, g, *, S, name):
    T, D = dx.shape
    B = T // S
    tt = _tile(S, 512)
    per = S // tt

    def body(dx_ref, y_ref, g_ref, dy_ref, dg_ref):
        i = pl.program_id(0)
        dxv = dx_ref[...]
        dy_ref[...] = (dxv * g_ref[...]).astype(BF16)

        @pl.when(i % per == 0)
        def _():
            dg_ref[...] = jnp.zeros_like(dg_ref)

        dg_ref[...] += jnp.sum(dxv * y_ref[...], axis=0, keepdims=True)

    return pl.pallas_call(
        body, name=name, grid=(T // tt,),
        in_specs=[pl.BlockSpec((tt, D), lambda i: (i, 0))] * 2 + _row_specs(tt, D, per, 1),
        out_specs=[pl.BlockSpec((tt, D), lambda i: (i, 0))] + _row_specs(tt, D, per, 1),
        out_shape=[jax.ShapeDtypeStruct((T, D), BF16), jax.ShapeDtypeStruct((B, 1, D), F32)],
        compiler_params=_cparams(("arbitrary",)),
    )(dx, y, g)


def _final_loss(x, target, gain):
    T, D = x.shape
    tt = _tile(T, 512)

    def body(x_ref, t_ref, g_ref, dx_ref, dg_ref, loss_ref):
        i = pl.program_id(0)
        xv = x_ref[...]
        r = lax.rsqrt(jnp.mean(xv * xv, axis=-1, keepdims=True) + NORM_EPS)
        n = xv * r
        g = g_ref[...]
        err = n * g - t_ref[...]
        dy = err * (1.0 / D)
        dn = dy * g
        dx_ref[...] = r * (dn - n * jnp.mean(dn * n, axis=-1, keepdims=True))

        @pl.when(i == 0)
        def _():
            dg_ref[...] = jnp.zeros_like(dg_ref)
            loss_ref[...] = jnp.zeros_like(loss_ref)

        dg_ref[...] += jnp.sum(dy * n, axis=0, keepdims=True)
        loss_ref[...] += jnp.sum(jnp.sum(err * err, axis=-1, keepdims=True), axis=0, keepdims=True) * (0.5 / D)

    return pl.pallas_call(
        body, name="final_loss", grid=(T // tt,),
        in_specs=[pl.BlockSpec((tt, D), lambda i: (i, 0))] * 2 + [pl.BlockSpec((1, D), lambda i: (0, 0))],
        out_specs=[pl.BlockSpec((tt, D), lambda i: (i, 0)), pl.BlockSpec((1, D), lambda i: (0, 0)),
                   pl.BlockSpec((1, LANES), lambda i: (0, 0))],
        out_shape=[jax.ShapeDtypeStruct((T, D), F32), jax.ShapeDtypeStruct((1, D), F32),
                   jax.ShapeDtypeStruct((1, LANES), F32)],
        compiler_params=_cparams(("arbitrary",)),
    )(x, target, gain)


def _head_masks(ew):
    lane = lax.broadcasted_iota(jnp.int32, (1, PAIR_Q), 1)
    m0 = (lane < HEAD_DIM) | ((lane >= LANES) & (lane < LANES + ew))
    m1 = ((lane >= HEAD_DIM) & (lane < LANES)) | ((lane >= LANES + ew) & (lane < LANES + 2 * ew))
    return m0, m1


def _dot_nt(a, b):
    return lax.dot_general(a, b, (((1,), (1,)), ((), ())), preferred_element_type=F32)


def _dot_tn(a, b):
    return lax.dot_general(a, b, (((0,), (0,)), ((), ())), preferred_element_type=F32)


def _attn_fwd(qx, kvx, *, S, scale, ew, name):
    T = qx.shape[0]
    P = qx.shape[1] // PAIR_Q
    B = T // S
    tq = _tile(S, 256)
    nq = S // tq

    def body(q_ref, kv_ref, o_ref, lse_ref, m_sc, l_sc, acc_sc):
        qi = pl.program_id(2)
        q = q_ref[...]
        masks = _head_masks(ew)
        qh = [jnp.where(m, q, jnp.zeros_like(q)) for m in masks]
        m_sc[...] = jnp.full(m_sc.shape, NEG_BIG, F32)
        l_sc[...] = jnp.zeros_like(l_sc)
        acc_sc[...] = jnp.zeros_like(acc_sc)

        def step(kj, diag):
            rows = pl.ds(pl.multiple_of(kj * tq, tq), tq)
            k = kv_ref[rows, 0:PAIR_Q]
            v = kv_ref[rows, PAIR_Q:PAIR_KV]
            for h in range(2):
                s = _dot_nt(qh[h], k)
                if scale != 1.0:
                    s = s * scale
                if diag:
                    row = lax.broadcasted_iota(jnp.int32, s.shape, 0)
                    col = lax.broadcasted_iota(jnp.int32, s.shape, 1)
                    s = jnp.where(col <= row, s, NEG_BIG)
                m_prev = m_sc[h]
                m_new = jnp.maximum(m_prev, jnp.max(s, axis=1, keepdims=True))
                alpha = jnp.exp(m_prev - m_new)
                p = jnp.exp(s - m_new[:, 0:1])
                l_sc[h] = alpha * l_sc[h] + jnp.sum(p, axis=1, keepdims=True)
                acc_sc[h] = alpha * acc_sc[h] + jnp.dot(p.astype(BF16), v, preferred_element_type=F32)
                m_sc[h] = m_new

        def loop_body(kj, carry):
            step(kj, False)
            return carry

        lax.fori_loop(0, qi, loop_body, 0)
        step(qi, True)
        lane = lax.broadcasted_iota(jnp.int32, (tq, LANES), 1)
        lo = lane < HEAD_DIM
        o_ref[...] = jnp.where(lo, acc_sc[0] / l_sc[0], acc_sc[1] / l_sc[1]).astype(BF16)
        lse_ref[...] = jnp.where(lo, m_sc[0] + jnp.log(l_sc[0]), m_sc[1] + jnp.log(l_sc[1]))

    return pl.pallas_call(
        body, name=name, grid=(B, P, nq),
        in_specs=[pl.BlockSpec((tq, PAIR_Q), lambda b, p, i: (b * nq + i, p)),
                  pl.BlockSpec((S, PAIR_KV), lambda b, p, i: (b, p))],
        out_specs=[pl.BlockSpec((tq, LANES), lambda b, p, i: (b * nq + i, p))] * 2,
        out_shape=[jax.ShapeDtypeStruct((T, P * LANES), BF16), jax.ShapeDtypeStruct((T, P * LANES), F32)],
        scratch_shapes=[pltpu.VMEM((2, tq, LANES), F32)] * 3,
        compiler_params=_cparams(("parallel", "parallel", "arbitrary")),
    )(qx, kvx)


def _attn_bwd(qx, kvx, o, lse, do, *, S, scale, ew, name, bias_grad=False):
    T = qx.shape[0]
    P = qx.shape[1] // PAIR_Q
    B = T // S
    tq = _tile(S, 256)
    nq = S // tq

    def body(q_ref, kv_ref, o_ref, lse_ref, do_ref, dq_ref, dkv_ref, *rest):
        kj = pl.program_id(2)
        if bias_grad:
            csum_ref, rsum_ref, dq_sc, delta_sc, dk_sc, dv_sc = rest
            csum_ref[...] = jnp.zeros_like(csum_ref)

            @pl.when(kj == 0)
            def _():
                rsum_ref[...] = jnp.zeros_like(rsum_ref)
        else:
            dq_sc, delta_sc, dk_sc, dv_sc = rest
        masks = _head_masks(ew)
        lane = lax.broadcasted_iota(jnp.int32, (tq, LANES), 1)
        lo = lane < HEAD_DIM
        vmask = [lo, jnp.logical_not(lo)]

        @pl.when(kj == 0)
        def _():
            dq_sc[...] = jnp.zeros_like(dq_sc)
            for c in range(nq):
                rows = pl.ds(c * tq, tq)
                x = do_ref[rows, :].astype(F32) * o_ref[rows, :].astype(F32)
                r0 = jnp.sum(jnp.where(lo, x, 0.0), axis=1, keepdims=True)
                r1 = jnp.sum(jnp.where(lo, 0.0, x), axis=1, keepdims=True)
                delta_sc[rows, :] = jnp.where(lo, r0, r1)

        k = kv_ref[:, 0:PAIR_Q]
        v = kv_ref[:, PAIR_Q:PAIR_KV]
        dk_sc[...] = jnp.zeros_like(dk_sc)
        dv_sc[...] = jnp.zeros_like(dv_sc)

        def step(qi, diag):
            rows = pl.ds(pl.multiple_of(qi * tq, tq), tq)
            q = q_ref[rows, :]
            dov = do_ref[rows, :]
            lse_v = lse_ref[rows, :]
            dl_v = delta_sc[rows, :]
            for h in range(2):
                qh = jnp.where(masks[h], q, jnp.zeros_like(q))
                s = _dot_nt(qh, k)
                if scale != 1.0:
                    s = s * scale
                if diag:
                    row = lax.broadcasted_iota(jnp.int32, s.shape, 0)
                    col = lax.broadcasted_iota(jnp.int32, s.shape, 1)
                    s = jnp.where(col <= row, s, NEG_BIG)
                c0 = h * HEAD_DIM
                p = jnp.exp(s - lse_v[:, c0:c0 + 1])
                doh = jnp.where(vmask[h], dov, jnp.zeros_like(dov))
                dp = _dot_nt(doh, v)
                ds = p * (dp - dl_v[:, c0:c0 + 1])
                if bias_grad:
                    csum_ref[h:h + 1, :] += jnp.sum(ds, axis=0, keepdims=True)
                    rsum_ref[rows, :] += jnp.where(vmask[h], jnp.sum(ds, axis=1, keepdims=True), 0.0)
                if scale != 1.0:
                    ds = ds * scale
                pb = p.astype(BF16)
                dsb = ds.astype(BF16)
                dv_sc[h] += _dot_tn(pb, dov)
                dk_sc[h] += _dot_tn(dsb, q)
                dq_sc[rows, :] += jnp.where(masks[h], jnp.dot(dsb, k, preferred_element_type=F32), 0.0)

        step(kj, True)

        def loop_body(qi, carry):
            step(qi, False)
            return carry

        lax.fori_loop(kj + 1, nq, loop_body, 0)
        dkv_ref[:, 0:PAIR_Q] = (jnp.where(masks[0], dk_sc[0], 0.0) + jnp.where(masks[1], dk_sc[1], 0.0)).astype(BF16)
        dkv_ref[:, PAIR_Q:PAIR_KV] = jnp.where(lo, dv_sc[0], dv_sc[1]).astype(BF16)

        @pl.when(kj == nq - 1)
        def _():
            dq_ref[...] = dq_sc[...].astype(BF16)

    out_specs = [pl.BlockSpec((S, PAIR_Q), lambda b, p, j: (b, p)),
                 pl.BlockSpec((tq, PAIR_KV), lambda b, p, j: (b * nq + j, p))]
    out_shape = [jax.ShapeDtypeStruct((T, P * PAIR_Q), BF16), jax.ShapeDtypeStruct((T, P * PAIR_KV), BF16)]
    if bias_grad:
        out_specs.append(pl.BlockSpec((None, 8, tq), lambda b, p, j: (p, 0, b * nq + j)))
        out_shape.append(jax.ShapeDtypeStruct((P, 8, T), F32))
        out_specs.append(pl.BlockSpec((S, LANES), lambda b, p, j: (b, p)))
        out_shape.append(jax.ShapeDtypeStruct((T, P * LANES), F32))
    return pl.pallas_call(
        body, name=name, grid=(B, P, nq),
        in_specs=[pl.BlockSpec((S, PAIR_Q), lambda b, p, j: (b, p)),
                  pl.BlockSpec((tq, PAIR_KV), lambda b, p, j: (b * nq + j, p)),
                  pl.BlockSpec((S, LANES), lambda b, p, j: (b, p)),
                  pl.BlockSpec((S, LANES), lambda b, p, j: (b, p)),
                  pl.BlockSpec((S, LANES), lambda b, p, j: (b, p))],
        out_specs=out_specs, out_shape=out_shape,
        scratch_shapes=[pltpu.VMEM((S, PAIR_Q), F32), pltpu.VMEM((S, LANES), F32),
                        pltpu.VMEM((2, tq, PAIR_Q), F32), pltpu.VMEM((2, tq, LANES), F32)],
        compiler_params=_cparams(("parallel", "parallel", "arbitrary")),
    )(qx, kvx, o, lse, do)


def _fox_consts(P):
    H = 2 * P
    eq = np.zeros((3 * LANES, P * LANES), np.float32)
    ek = np.zeros((3 * LANES, P * LANES), np.float32)
    ones_q = np.zeros((1, P * LANES), np.float32)
    ones_k = np.zeros((1, P * LANES), np.float32)
    for h in range(H):
        base = (h // 2) * LANES + FOX_EXTRA * (h % 2)
        for part in range(3):
            eq[part * LANES + h, base + part] = 1.0
            ones_q[0, base + 3 + part] = 1.0
            ones_k[0, base + part] = 1.0
            ek[part * LANES + h, base + 3 + part] = -1.0
    return eq, ek, ones_q, ones_k


def _split3(f):
    hi = f.astype(BF16)
    r = f - hi.astype(F32)
    mid = r.astype(BF16)
    lo = (r - mid.astype(F32)).astype(BF16)
    return hi, mid, lo


def _tri_sum(tri, x):
    hi, mid, lo = _split3(x)
    return (jnp.dot(tri, hi, preferred_element_type=F32) + jnp.dot(tri, mid, preferred_element_type=F32)
            + jnp.dot(tri, lo, preferred_element_type=F32))


def _log1p_pos(e):
    return jnp.where(e < 0.01, e * (1.0 - e * (0.5 - e * (1.0 / 3.0))), jnp.log(1.0 + e))


def _fox_prep(qkv, fl, b_row, *, S, D, name):
    T = qkv.shape[0]
    P = D // LANES
    B = T // S
    tt = _tile(S, 256)
    per = S // tt
    eq, ek, ones_q, ones_k = _fox_consts(P)
    q_scale = HEAD_DIM ** -0.5

    def body(q_ref, k_ref, v_ref, fl_ref, b_ref, eq_ref, ek_ref, oq_ref, ok_ref, qx_ref, kvx_ref, carry):
        i = pl.program_id(1)

        @pl.when(i == 0)
        def _():
            carry[...] = jnp.zeros_like(carry)

        z = fl_ref[...] + b_ref[...]
        logf = jnp.minimum(z, 0.0) - _log1p_pos(jnp.exp(-jnp.abs(z)))
        row = lax.broadcasted_iota(jnp.int32, (tt, tt), 0)
        col = lax.broadcasted_iota(jnp.int32, (tt, tt), 1)
        tri = (col <= row).astype(BF16)
        f = _tri_sum(tri, logf) + carry[...]
        carry[...] = f[tt - 1:tt, :]
        parts = jnp.concatenate(_split3(f), axis=1)
        xq = jnp.dot(parts, eq_ref[...], preferred_element_type=F32) + oq_ref[...]
        xk = jnp.dot(parts, ek_ref[...], preferred_element_type=F32) + ok_ref[...]
        for p in range(P):
            c = slice(p * LANES, (p + 1) * LANES)
            qx_ref[:, p * PAIR_Q:p * PAIR_Q + LANES] = (q_ref[:, c].astype(F32) * q_scale).astype(BF16)
            qx_ref[:, p * PAIR_Q + LANES:(p + 1) * PAIR_Q] = xq[:, c].astype(BF16)
            kvx_ref[:, p * PAIR_KV:p * PAIR_KV + LANES] = k_ref[:, c]
            kvx_ref[:, p * PAIR_KV + LANES:p * PAIR_KV + PAIR_Q] = xk[:, c].astype(BF16)
            kvx_ref[:, p * PAIR_KV + PAIR_Q:(p + 1) * PAIR_KV] = v_ref[:, c]

    tok = lambda b, i: (b * per + i, 0)
    const = lambda b, i: (0, 0)
    return pl.pallas_call(
        body, name=name, grid=(B, per),
        in_specs=[pl.BlockSpec((tt, D), lambda b, i: (b * per + i, 0)),
                  pl.BlockSpec((tt, D), lambda b, i: (b * per + i, 1)),
                  pl.BlockSpec((tt, D), lambda b, i: (b * per + i, 2)),
                  pl.BlockSpec((tt, LANES), tok), pl.BlockSpec((1, LANES), const),
                  pl.BlockSpec(eq.shape, const), pl.BlockSpec(ek.shape, const),
                  pl.BlockSpec(ones_q.shape, const), pl.BlockSpec(ones_k.shape, const)],
        out_specs=[pl.BlockSpec((tt, P * PAIR_Q), tok), pl.BlockSpec((tt, P * PAIR_KV), tok)],
        out_shape=[jax.ShapeDtypeStruct((T, P * PAIR_Q), BF16), jax.ShapeDtypeStruct((T, P * PAIR_KV), BF16)],
        scratch_shapes=[pltpu.VMEM((1, LANES), F32)],
        compiler_params=_cparams(("arbitrary", "arbitrary")),
    )(qkv, qkv, qkv, fl, b_row, jnp.asarray(eq, BF16), jnp.asarray(ek, BF16), jnp.asarray(ones_q), jnp.asarray(ones_k))


def _fox_unprep(dqx, dkvx, csum, rsum, fl, b_row, *, S, D, name):
    T = dqx.shape[0]
    P = D // LANES
    B = T // S
    tt = _tile(S, 256)
    per = S // tt
    q_scale = HEAD_DIM ** -0.5

    def body(dq_ref, dkv_ref, cs_ref, rs_ref, fl_ref, b_ref, dqkv_ref, dfl_ref, db_ref, carry):
        b = pl.program_id(0)
        i = pl.program_id(1)

        @pl.when(i == 0)
        def _():
            carry[...] = jnp.zeros_like(carry)

        @pl.when((i == 0) & (b == 0))
        def _():
            db_ref[...] = jnp.zeros_like(db_ref)

        df = rs_ref[...] - cs_ref[...]
        for p in range(P):
            rq = slice(p * LANES, (p + 1) * LANES)
            dqkv_ref[:, rq] = (dq_ref[:, p * PAIR_Q:p * PAIR_Q + LANES].astype(F32) * q_scale).astype(BF16)
            dqkv_ref[:, D + p * LANES:D + (p + 1) * LANES] = dkv_ref[:, p * PAIR_KV:p * PAIR_KV + LANES]
            dqkv_ref[:, 2 * D + p * LANES:2 * D + (p + 1) * LANES] = dkv_ref[:, p * PAIR_KV + PAIR_Q:(p + 1) * PAIR_KV]
        row = lax.broadcasted_iota(jnp.int32, (tt, tt), 0)
        col = lax.broadcasted_iota(jnp.int32, (tt, tt), 1)
        tri = (col >= row).astype(BF16)
        dlogf = _tri_sum(tri, df) + carry[...]
        carry[...] = dlogf[0:1, :]
        z = fl_ref[...] + b_ref[...]
        e = jnp.exp(-jnp.abs(z))
        sig_neg = jnp.where(z >= 0.0, e, 1.0) / (1.0 + e)
        dfl = dlogf * sig_neg
        dfl_ref[...] = dfl.astype(BF16)
        db_ref[...] += jnp.sum(dfl, axis=0, keepdims=True)

    rev = lambda b, i: (b * per + per - 1 - i, 0)
    const = lambda b, i: (0, 0)
    return pl.pallas_call(
        body, name=name, grid=(B, per),
        in_specs=[pl.BlockSpec((tt, P * PAIR_Q), rev), pl.BlockSpec((tt, P * PAIR_KV), rev),
                  pl.BlockSpec((tt, LANES), rev), pl.BlockSpec((tt, LANES), rev), pl.BlockSpec((tt, LANES), rev),
                  pl.BlockSpec((1, LANES), const)],
        out_specs=[pl.BlockSpec((tt, 3 * D), rev), pl.BlockSpec((tt, LANES), rev), pl.BlockSpec((1, LANES), const)],
        out_shape=[jax.ShapeDtypeStruct((T, 3 * D), BF16), jax.ShapeDtypeStruct((T, LANES), BF16),
                   jax.ShapeDtypeStruct((1, LANES), F32)],
        scratch_shapes=[pltpu.VMEM((1, LANES), F32)],
        compiler_params=_cparams(("arbitrary", "arbitrary")),
    )(dqx, dkvx, csum, rsum, fl, b_row)


def _rms(x):
    r = lax.rsqrt(jnp.mean(x * x, axis=-1, keepdims=True) + NORM_EPS)
    return x * r, r


def _mla_mid(lat, gq, gkv, cos_t, sin_s, *, name):
    T, W = lat.shape
    Rq = W - 2 * LANES
    tt = _tile(T, 512)

    def body(l_ref, gq_ref, gkv_ref, c_ref, s_ref, o_ref):
        nq, _ = _rms(l_ref[:, 0:Rq])
        nkv, _ = _rms(l_ref[:, Rq:Rq + LANES])
        o_ref[:, 0:Rq] = (nq * gq_ref[...]).astype(BF16)
        o_ref[:, Rq:Rq + LANES] = (nkv * gkv_ref[...]).astype(BF16)
        o_ref[:, Rq + LANES:W] = _rope128(l_ref[:, Rq + LANES:W], c_ref[...], s_ref[...]).astype(BF16)

    return pl.pallas_call(
        body, name=name, grid=(T // tt,),
        in_specs=[pl.BlockSpec((tt, W), lambda i: (i, 0)), pl.BlockSpec((1, Rq), lambda i: (0, 0)),
                  pl.BlockSpec((1, LANES), lambda i: (0, 0)), pl.BlockSpec((tt, LANES), lambda i: (i, 0)),
                  pl.BlockSpec((tt, LANES), lambda i: (i, 0))],
        out_specs=pl.BlockSpec((tt, W), lambda i: (i, 0)),
        out_shape=jax.ShapeDtypeStruct((T, W), BF16),
        compiler_params=_cparams(("parallel",)),
    )(lat, gq, gkv, cos_t, sin_s)


def _mla_mid_bwd(lat, dcq, dckr, gq, gkv, cos_t, sin_s, *, name):
    T, W = lat.shape
    Rq = W - 2 * LANES
    tt = _tile(T, 512)

    def norm_bwd(x, dy, g):
        n, r = _rms(x)
        dn = dy * g
        return r * (dn - n * jnp.mean(dn * n, axis=-1, keepdims=True)), jnp.sum(dy * n, axis=0, keepdims=True)

    def body(l_ref, dq_ref, dk_ref, gq_ref, gkv_ref, c_ref, s_ref, o_ref, dgq_ref, dgkv_ref):
        i = pl.program_id(0)

        @pl.when(i == 0)
        def _():
            dgq_ref[...] = jnp.zeros_like(dgq_ref)
            dgkv_ref[...] = jnp.zeros_like(dgkv_ref)

        dxq, dgq = norm_bwd(l_ref[:, 0:Rq], dq_ref[...], gq_ref[...])
        dxkv, dgkv = norm_bwd(l_ref[:, Rq:Rq + LANES], dk_ref[:, 0:LANES], gkv_ref[...])
        o_ref[:, 0:Rq] = dxq.astype(BF16)
        o_ref[:, Rq:Rq + LANES] = dxkv.astype(BF16)
        o_ref[:, Rq + LANES:W] = _rope128(dk_ref[:, LANES:2 * LANES], c_ref[...], -s_ref[...]).astype(BF16)
        dgq_ref[...] += dgq
        dgkv_ref[...] += dgkv

    return pl.pallas_call(
        body, name=name, grid=(T // tt,),
        in_specs=[pl.BlockSpec((tt, W), lambda i: (i, 0)), pl.BlockSpec((tt, Rq), lambda i: (i, 0)),
                  pl.BlockSpec((tt, 2 * LANES), lambda i: (i, 0)), pl.BlockSpec((1, Rq), lambda i: (0, 0)),
                  pl.BlockSpec((1, LANES), lambda i: (0, 0)), pl.BlockSpec((tt, LANES), lambda i: (i, 0)),
                  pl.BlockSpec((tt, LANES), lambda i: (i, 0))],
        out_specs=[pl.BlockSpec((tt, W), lambda i: (i, 0)), pl.BlockSpec((1, Rq), lambda i: (0, 0)),
                   pl.BlockSpec((1, LANES), lambda i: (0, 0))],
        out_shape=[jax.ShapeDtypeStruct((T, W), BF16), jax.ShapeDtypeStruct((1, Rq), F32),
                   jax.ShapeDtypeStruct((1, LANES), F32)],
        compiler_params=_cparams(("arbitrary",)),
    )(lat, dcq, dckr, gq, gkv, cos_t, sin_s)


def _uq_to_pairs(w):
    Rq = w.shape[0]
    P = w.shape[1] // (2 * (HEAD_DIM + ROPE_DIM))
    w4 = w.reshape(Rq, P, 2, HEAD_DIM + ROPE_DIM)
    nope = w4[..., :HEAD_DIM].reshape(Rq, P, 2 * HEAD_DIM)
    rope = w4[..., HEAD_DIM:].reshape(Rq, P, 2 * ROPE_DIM)
    pad = jnp.zeros((Rq, P, PAIR_Q - 2 * HEAD_DIM - 2 * ROPE_DIM), w.dtype)
    return jnp.concatenate([nope, rope, pad], axis=-1).reshape(Rq, P * PAIR_Q)


def _uq_from_pairs(g):
    Rq = g.shape[0]
    P = g.shape[1] // PAIR_Q
    g3 = g.reshape(Rq, P, PAIR_Q)
    nope = g3[..., :2 * HEAD_DIM].reshape(Rq, P, 2, HEAD_DIM)
    rope = g3[..., 2 * HEAD_DIM:2 * HEAD_DIM + 2 * ROPE_DIM].reshape(Rq, P, 2, ROPE_DIM)
    return jnp.concatenate([nope, rope], axis=-1).reshape(Rq, P * 2 * (HEAD_DIM + ROPE_DIM))


def _ukv_to_pairs(w):
    P = w.shape[1] // (4 * HEAD_DIM)
    w4 = w.reshape(KV_RANK, P, 2, 2 * HEAD_DIM)
    kn = w4[..., :HEAD_DIM].reshape(KV_RANK, P, 2 * HEAD_DIM)
    vv = w4[..., HEAD_DIM:].reshape(KV_RANK, P, 2 * HEAD_DIM)
    top = jnp.concatenate([kn, jnp.zeros((KV_RANK, P, LANES), w.dtype), vv], axis=-1)
    place = np.zeros((LANES, P, PAIR_KV), np.float32)
    for r in range(ROPE_DIM):
        place[r, :, LANES + r] = 1.0
        place[r, :, LANES + ROPE_DIM + r] = 1.0
    return jnp.concatenate([top, jnp.asarray(place, w.dtype)], axis=0).reshape(KV_RANK + LANES, P * PAIR_KV)


def _ukv_from_pairs(g):
    P = g.shape[1] // PAIR_KV
    g3 = g[:KV_RANK].reshape(KV_RANK, P, PAIR_KV)
    kn = g3[..., :2 * HEAD_DIM].reshape(KV_RANK, P, 2, HEAD_DIM)
    vv = g3[..., PAIR_Q:].reshape(KV_RANK, P, 2, HEAD_DIM)
    return jnp.concatenate([kn, vv], axis=-1).reshape(KV_RANK, P * 4 * HEAD_DIM)


def _mlp_fwd(h2, w1, w2, x1, gate, *, S, tag):
    p, u = _mm(h2, w1, "nn", name=f"mlp_up_{tag}", out_dtypes=(BF16, BF16),
               epilogue=lambda acc: (acc, jnp.square(jnp.maximum(acc, 0.0))))
    x2, z = _mm(u, w2, "nn", name=f"mlp_down_{tag}", out_dtypes=(F32, F32), extras=(x1,), rowvecs=(gate,), seq=S,
                epilogue=lambda acc, xr, g: (xr + g * acc, acc))
    return x2, (p, u, z)


def _local_step(x, target, pos_f, inv_freq_row, sign_row, mod, w, *, S):
    T, D = x.shape
    L = mod.shape[0]
    cos_t, sin_s = _rope_tables(pos_f, inv_freq_row, sign_row)
    saved = []
    for i in range(L):
        j = i // 2
        sh_m, sc_m, g_m, sh_f, sc_f, g_f = (mod[i, s] for s in range(6))
        h = _norm_mod(x, w["norm_mix_g"][i], sc_m, sh_m, S=S, name=f"norm_mix_{i}")
        if i % 2 == 0:
            qkv = _mm(h, w["fox_qkv"][j], "nn", name=f"fox_qkv_{i}", out_dtypes=(BF16,))
            fl = _mm(h, w["fox_f"][j], "nn", name=f"fox_f_{i}")
            qx, kvx = _fox_prep(qkv, fl, w["fox_b"][j], S=S, D=D, name=f"fox_prep_{i}")
            o, lse = _attn_fwd(qx, kvx, S=S, scale=1.0, ew=FOX_EXTRA, name=f"fox_attn_{i}")
            mix = (qx, kvx, o, lse, fl)
            w_out = w["fox_out"][j]
        else:
            lat = _mm(h, w["mla_down"][j], "nn", name=f"mla_down_{i}")
            Rq = lat.shape[1] - 2 * LANES
            cqr = _mla_mid(lat, w["mla_gq"][j], w["mla_gkv"][j], cos_t, sin_s, name=f"mla_mid_{i}")
            qx = _mm(cqr, w["mla_uq"][j], "nn", name=f"mla_uq_{i}", out_dtypes=(BF16,), a_sz=Rq, tk=Rq,
                     tables=(cos_t, sin_s), epilogue=lambda acc, c, s: (_rope_pairs(acc, c, s, 1.0),))
            kvx = _mm(cqr, w["mla_ukv"][j], "nn", name=f"mla_ukv_{i}", out_dtypes=(BF16,), a_off=Rq,
                      a_sz=2 * LANES, tk=2 * LANES, tn=PAIR_KV)
            o, lse = _attn_fwd(qx, kvx, S=S, scale=(HEAD_DIM + ROPE_DIM) ** -0.5, ew=ROPE_DIM, name=f"mla_attn_{i}")
            mix = (qx, kvx, o, lse, lat, cqr)
            w_out = w["mla_out"][j]
        x1, y = _mm(o, w_out, "nn", name=f"mix_out_{i}", out_dtypes=(F32, F32), extras=(x,), rowvecs=(g_m,), seq=S,
                    epilogue=lambda acc, xr, g: (xr + g * acc, acc))
        h2 = _norm_mod(x1, w["norm_mlp_g"][i], sc_f, sh_f, S=S, name=f"norm_mlp_{i}")
        x2, mlp = _mlp_fwd(h2, w["mlp_w1"][i], w["mlp_w2"][i], x1, g_f, S=S, tag=str(i))
        saved.append((x, h, mix, y, x1, h2, mlp))
        x = x2

    dx, dg_final, loss = _final_loss(x, target, w["final_norm_g"])

    grads = {k: [None] * len(v) for k, v in w.items() if k != "final_norm_g"}
    grads["final_norm_g"] = dg_final
    dmod = [None] * L
    for i in reversed(range(L)):
        j = i // 2
        x0, h, mix, y, x1, h2, (p, u, z) = saved[i]
        sh_m, sc_m, g_m, sh_f, sc_f, g_f = (mod[i, s] for s in range(6))
        dz, dg_f = _gate_bwd(dx, z, g_f, S=S, name=f"gate_mlp_bwd_{i}")
        grads["mlp_w2"][i] = _mm(u, dz, "tn", name=f"mlp_w2_grad_{i}")
        dp = _mm(dz, w["mlp_w2"][i], "nt", name=f"mlp_down_bwd_{i}", out_dtypes=(BF16,), extras=(p,),
                 epilogue=lambda acc, pv: (acc * (2.0 * jnp.maximum(pv.astype(F32), 0.0)),))
        grads["mlp_w1"][i] = _mm(h2, dp, "tn", name=f"mlp_w1_grad_{i}")
        dh2 = _mm(dp, w["mlp_w1"][i], "nt", name=f"mlp_up_bwd_{i}")
        dx1, dsh_f, dsc_f, dgn = _norm_mod_bwd(x1, dh2, dx, w["norm_mlp_g"][i], sc_f, S=S, name=f"norm_mlp_bwd_{i}")
        grads["norm_mlp_g"][i] = dgn
        dy, dg_m = _gate_bwd(dx1, y, g_m, S=S, name=f"gate_mix_bwd_{i}")
        if i % 2 == 0:
            qx, kvx, o, lse, fl = mix
            grads["fox_out"][j] = _mm(o, dy, "tn", name=f"fox_out_grad_{i}")
            do = _mm(dy, w["fox_out"][j], "nt", name=f"fox_out_bwd_{i}", out_dtypes=(BF16,))
            dqx, dkvx, csum, rsum = _attn_bwd(qx, kvx, o, lse, do, S=S, scale=1.0, ew=FOX_EXTRA,
                                              name=f"fox_attn_bwd_{i}", bias_grad=True)
            n_heads = D // HEAD_DIM
            csum = jnp.pad(csum[:, :2, :].reshape(n_heads, T).T, ((0, 0), (0, LANES - n_heads)))
            rsum = jnp.pad(rsum.reshape(T, n_heads, HEAD_DIM)[:, :, 0], ((0, 0), (0, LANES - n_heads)))
            dqkv, dfl, db = _fox_unprep(dqx, dkvx, csum, rsum, fl, w["fox_b"][j], S=S, D=D, name=f"fox_unprep_{i}")
            grads["fox_b"][j] = db
            grads["fox_qkv"][j] = _mm(h, dqkv, "tn", name=f"fox_qkv_grad_{i}")
            grads["fox_f"][j] = _mm(h, dfl, "tn", name=f"fox_f_grad_{i}")
            dh_f = _mm(dfl, w["fox_f"][j], "nt", name=f"fox_f_bwd_{i}")
            dh = _mm(dqkv, w["fox_qkv"][j], "nt", name=f"fox_qkv_bwd_{i}", extras=(dh_f,),
                     epilogue=lambda acc, e: (acc + e,))
        else:
            qx, kvx, o, lse, lat, cqr = mix
            Rq = lat.shape[1] - 2 * LANES
            grads["mla_out"][j] = _mm(o, dy, "tn", name=f"mla_out_grad_{i}")
            do = _mm(dy, w["mla_out"][j], "nt", name=f"mla_out_bwd_{i}", out_dtypes=(BF16,))
            dqx, dkvx = _attn_bwd(qx, kvx, o, lse, do, S=S, scale=(HEAD_DIM + ROPE_DIM) ** -0.5, ew=ROPE_DIM,
                                  name=f"mla_attn_bwd_{i}")
            dqpre = _unrope(dqx, cos_t, sin_s)
            grads["mla_uq"][j] = _mm(cqr, dqpre, "tn", name=f"mla_uq_grad_{i}", a_sz=Rq, tm=min(Rq, 256))
            grads["mla_ukv"][j] = _mm(cqr, dkvx, "tn", name=f"mla_ukv_grad_{i}", a_off=Rq, a_sz=2 * LANES,
                                      tm=2 * LANES, tn=PAIR_KV)
            dcq = _mm(dqpre, w["mla_uq"][j], "nt", name=f"mla_uq_bwd_{i}")
            dckr = _mm(dkvx, w["mla_ukv"][j], "nt", name=f"mla_ukv_bwd_{i}", tk=PAIR_KV * 2)
            dlat, dgq, dgkv = _mla_mid_bwd(lat, dcq, dckr, w["mla_gq"][j], w["mla_gkv"][j], cos_t, sin_s,
                                           name=f"mla_mid_bwd_{i}")
            grads["mla_gq"][j] = dgq
            grads["mla_gkv"][j] = dgkv
            grads["mla_down"][j] = _mm(h, dlat, "tn", name=f"mla_down_grad_{i}")
            dh = _mm(dlat, w["mla_down"][j], "nt", name=f"mla_down_bwd_{i}")
        dx, dsh_m, dsc_m, dgn = _norm_mod_bwd(x0, dh, dx1, w["norm_mix_g"][i], sc_m, S=S, name=f"norm_mix_bwd_{i}")
        grads["norm_mix_g"][i] = dgn
        dmod[i] = jnp.stack([dsh_m, dsc_m, dg_m, dsh_f, dsc_f, dg_f])
    return loss, dx, jnp.stack(dmod), grads


def _kernel_weights(full, small):
    D = full["fox_w_out"].shape[-1]
    w = {}
    w["fox_qkv"] = [m[:, :3 * D] for m in full["fox_w_in"]]
    w["fox_f"] = [jnp.pad(m[:, 3 * D:], ((0, 0), (0, LANES - (m.shape[1] - 3 * D)))) for m in full["fox_w_in"]]
    w["fox_b"] = [jnp.pad(b, (0, LANES - b.shape[0]))[None, :] for b in small["fox_b_f"]]
    w["fox_out"] = list(full["fox_w_out"])
    w["mla_down"] = [jnp.concatenate([dq, jnp.pad(dkv, ((0, 0), (0, 2 * LANES - dkv.shape[1])))], axis=1)
                     for dq, dkv in zip(full["mla_w_dq"], full["mla_w_dkv"])]
    w["mla_uq"] = [_uq_to_pairs(m) for m in full["mla_w_uq"]]
    w["mla_ukv"] = [_ukv_to_pairs(m) for m in full["mla_w_ukv"]]
    w["mla_out"] = list(full["mla_w_out"])
    w["mla_gq"] = [g[None, :] for g in small["mla_q_norm_g"]]
    w["mla_gkv"] = [g[None, :] for g in small["mla_kv_norm_g"]]
    w["mlp_w1"] = list(full["mlp_w1"])
    w["mlp_w2"] = list(full["mlp_w2"])
    w["norm_mix_g"] = [g[None, :] for g in small["norm_mix_g"]]
    w["norm_mlp_g"] = [g[None, :] for g in small["norm_mlp_g"]]
    w["final_norm_g"] = small["final_norm_g"][None, :]
    return w


def _natural_grads(g, n_fox_heads, rq):
    big = {
        "fox_w_in": jnp.stack([jnp.concatenate([a, b[:, :n_fox_heads]], axis=1) for a, b in zip(g["fox_qkv"], g["fox_f"])]),
        "fox_w_out": jnp.stack(g["fox_out"]),
        "mla_w_dq": jnp.stack([m[:, :rq] for m in g["mla_down"]]),
        "mla_w_uq": jnp.stack([_uq_from_pairs(m) for m in g["mla_uq"]]),
        "mla_w_dkv": jnp.stack([m[:, rq:rq + KV_RANK + ROPE_DIM] for m in g["mla_down"]]),
        "mla_w_ukv": jnp.stack([_ukv_from_pairs(m) for m in g["mla_ukv"]]),
        "mla_w_out": jnp.stack(g["mla_out"]),
        "mlp_w1": jnp.stack(g["mlp_w1"]),
        "mlp_w2": jnp.stack(g["mlp_w2"]),
    }
    small = {
        "norm_mix_g": jnp.concatenate(g["norm_mix_g"], axis=0),
        "norm_mlp_g": jnp.concatenate(g["norm_mlp_g"], axis=0),
        "final_norm_g": g["final_norm_g"][0],
        "fox_b_f": jnp.concatenate(g["fox_b"], axis=0)[:, :n_fox_heads],
        "mla_q_norm_g": jnp.concatenate(g["mla_gq"], axis=0),
        "mla_kv_norm_g": jnp.concatenate(g["mla_gkv"], axis=0),
    }
    return big, small


def _silu(c):
    return c * (1.0 / (1.0 + jnp.exp(-c)))


def _ada_fwd(c_all, ada_w, ada_b_cols):
    L, D, C = ada_w.shape
    Bg = c_all.shape[0]
    tc = _tile(C, 512)

    def body(c_ref, w_ref, b_ref, o_ref):
        ca = _silu(c_ref[...]).astype(BF16)
        o_ref[...] = jnp.dot(ca, w_ref[...].astype(BF16), preferred_element_type=F32) + b_ref[...]

    return pl.pallas_call(
        body, name="ada_fwd", grid=(L, C // tc),
        in_specs=[pl.BlockSpec((Bg, D), lambda l, j: (0, 0)), pl.BlockSpec((None, D, tc), lambda l, j: (l, 0, j)),
                  pl.BlockSpec((None, 1, tc), lambda l, j: (l, 0, j))],
        out_specs=pl.BlockSpec((None, Bg, tc), lambda l, j: (l, 0, j)),
        out_shape=jax.ShapeDtypeStruct((L, Bg, C), F32),
        compiler_params=_cparams(("parallel", "parallel")),
    )(c_all, ada_w, ada_b_cols)


def _ada_bwd(c_all, dmod_cols):
    L, Bg, C = dmod_cols.shape
    D = c_all.shape[1]
    tc = _tile(C, 512)

    def body(c_ref, d_ref, o_ref):
        ca = _silu(c_ref[...]).astype(BF16)
        o_ref[...] = _dot_tn(ca, d_ref[...].astype(BF16))

    return pl.pallas_call(
        body, name="ada_bwd", grid=(L, C // tc),
        in_specs=[pl.BlockSpec((Bg, D), lambda l, j: (0, 0)), pl.BlockSpec((None, Bg, tc), lambda l, j: (l, 0, j))],
        out_specs=pl.BlockSpec((None, D, tc), lambda l, j: (l, 0, j)),
        out_shape=jax.ShapeDtypeStruct((L, D, C), F32),
        compiler_params=_cparams(("parallel", "parallel")),
    )(c_all, dmod_cols)


def _adamw(w, g, m, v, *, name):
    shape = w.shape
    C = shape[-1]
    R = int(np.prod(shape[:-1])) if len(shape) > 1 else 1
    w2, g2, m2, v2 = (a.reshape(R, C) for a in (w, g, m, v))
    tr = R
    while tr * C > 256 * 1024 and tr % 16 == 0:
        tr //= 2

    def body(w_ref, g_ref, m_ref, v_ref, d_ref, nm_ref, nv_ref):
        gv = g_ref[...]
        mn = ADAM_B1 * m_ref[...] + (1.0 - ADAM_B1) * gv
        vn = ADAM_B2 * v_ref[...] + (1.0 - ADAM_B2) * jnp.square(gv)
        m_hat = mn / (1.0 - ADAM_B1 ** ADAM_STEP)
        v_hat = vn / (1.0 - ADAM_B2 ** ADAM_STEP)
        d_ref[...] = -ADAM_LR * (m_hat / (jnp.sqrt(v_hat) + ADAM_EPS) + ADAM_WD * w_ref[...])
        nm_ref[...] = mn
        nv_ref[...] = vn

    spec = pl.BlockSpec((tr, C), lambda i: (i, 0))
    out = pl.pallas_call(
        body, name=name, grid=(R // tr,), in_specs=[spec] * 4, out_specs=[spec] * 3,
        out_shape=[jax.ShapeDtypeStruct((R, C), F32)] * 3, compiler_params=_cparams(("parallel",)),
    )(w2, g2, m2, v2)
    return tuple(a.reshape(shape) for a in out)


def _sum_gathered(dm8, sm8):
    n_dev, Bl, R, D = dm8.shape
    Rs = sm8.shape[1]

    def body(dm_ref, sm_ref, ob_ref, os_ref):
        acc_b = jnp.zeros((R, D), F32)
        acc_s = jnp.zeros((Rs, D), F32)
        for d in range(n_dev):
            for b in range(Bl):
                acc_b = acc_b + dm_ref[d, b]
            acc_s = acc_s + sm_ref[d]
        ob_ref[...] = acc_b
        os_ref[...] = acc_s

    return pl.pallas_call(
        body, name="sum_gathered",
        out_shape=[jax.ShapeDtypeStruct((R, D), F32), jax.ShapeDtypeStruct((Rs, D), F32)],
        compiler_params=_cparams(None),
    )(dm8, sm8)


N_DEV = 8
N_CHIP = 4
ANY = pl.BlockSpec(memory_space=pl.ANY)


def _mesh_pos():
    return lax.axis_index("x"), lax.axis_index("y"), lax.axis_index("c")


def _all_gather8(block, *, name, in_vmem):
    R, W = block.shape

    def body(x_ref, out_ref, send_sems, recv_sems, local_sem):
        x, y, c = _mesh_pos()
        me, sibling = (x, y, c), (x, y, 1 - c)
        chips = [(1 - x, y), (x, 1 - y), (1 - x, 1 - y)]

        def slot(px, py, pc):
            return out_ref.at[4 * px + 2 * py + pc]

        def copy(k, blk, to, src=None):
            return pltpu.make_async_remote_copy(
                src_ref=slot(*blk) if src is None else src, dst_ref=slot(*blk),
                send_sem=send_sems.at[k], recv_sem=recv_sems.at[k], device_id=to, device_id_type=MESH_ID)

        mine = pltpu.make_async_copy(x_ref, slot(*me), local_sem)
        mine.start()
        first = [copy(0, me, sibling, src=x_ref)]
        first += [copy(1 + j, me, (*chip, c), src=x_ref) for j, chip in enumerate(chips)]
        for cp in first:
            cp.start()
        passed = [copy(4 + j, (*chip, c), sibling) for j, chip in enumerate(chips)]
        for j, chip in enumerate(chips):
            copy(1 + j, (*chip, c), me).wait_recv()
            passed[j].start()
        copy(0, sibling, me).wait_recv()
        for j, chip in enumerate(chips):
            copy(4 + j, (*chip, 1 - c), me).wait_recv()
        for cp in first + passed:
            cp.wait_send()
        mine.wait()

    space = pl.BlockSpec(memory_space=pltpu.VMEM) if in_vmem else ANY
    return pl.pallas_call(
        body, name=name, out_shape=jax.ShapeDtypeStruct((N_DEV, R, W), block.dtype),
        in_specs=[space], out_specs=space,
        scratch_shapes=[pltpu.SemaphoreType.DMA((7,)), pltpu.SemaphoreType.DMA((7,)), pltpu.SemaphoreType.DMA],
        compiler_params=pltpu.CompilerParams(vmem_limit_bytes=VMEM_LIMIT_V7X),
    )(block)


def _pair_exchange(g, *, name):
    def body(g_ref, out_ref, send_sem, recv_sem):
        x, y, c = _mesh_pos()
        cp = pltpu.make_async_remote_copy(src_ref=g_ref.at[1 - c], dst_ref=out_ref, send_sem=send_sem,
                                          recv_sem=recv_sem, device_id=(x, y, 1 - c), device_id_type=MESH_ID)
        cp.start()
        cp.wait()

    return pl.pallas_call(
        body, name=name, out_shape=jax.ShapeDtypeStruct(g.shape[1:], g.dtype), in_specs=[ANY], out_specs=ANY,
        scratch_shapes=[pltpu.SemaphoreType.DMA, pltpu.SemaphoreType.DMA],
    )(g)


def _pair_gather(half, *, name):
    def body(h_ref, out_ref, send_sem, recv_sem, local_sem):
        x, y, c = _mesh_pos()
        mine = pltpu.make_async_copy(h_ref, out_ref.at[c], local_sem)
        mine.start()
        cp = pltpu.make_async_remote_copy(src_ref=h_ref, dst_ref=out_ref.at[c], send_sem=send_sem,
                                          recv_sem=recv_sem, device_id=(x, y, 1 - c), device_id_type=MESH_ID)
        cp.start()
        cp.wait()
        mine.wait()

    return pl.pallas_call(
        body, name=name, out_shape=jax.ShapeDtypeStruct((2,) + half.shape, half.dtype), in_specs=[ANY], out_specs=ANY,
        scratch_shapes=[pltpu.SemaphoreType.DMA, pltpu.SemaphoreType.DMA, pltpu.SemaphoreType.DMA],
    )(half)


def _chip_exchange(pb, *, name):
    def body(p_ref, out_ref, send_sems, recv_sems, local_sem):
        x, y, c = _mesh_pos()
        k_me = 2 * x + y
        chips = [(1 - x, y), (x, 1 - y), (1 - x, 1 - y)]
        mine = pltpu.make_async_copy(p_ref.at[k_me], out_ref.at[k_me], local_sem)
        mine.start()
        copies = [pltpu.make_async_remote_copy(
            src_ref=p_ref.at[2 * cx + cy], dst_ref=out_ref.at[k_me], send_sem=send_sems.at[j],
            recv_sem=recv_sems.at[j], device_id=(cx, cy, c), device_id_type=MESH_ID)
            for j, (cx, cy) in enumerate(chips)]
        for cp in copies:
            cp.start()
        for cp in copies:
            cp.wait()
        mine.wait()

    return pl.pallas_call(
        body, name=name, out_shape=jax.ShapeDtypeStruct(pb.shape, pb.dtype), in_specs=[ANY], out_specs=ANY,
        scratch_shapes=[pltpu.SemaphoreType.DMA((3,)), pltpu.SemaphoreType.DMA((3,)), pltpu.SemaphoreType.DMA],
    )(pb)


def _pair_add(g, recv, c_idx, *, name):
    _, n, R, W = g.shape
    tr = _tile(R, 256)

    def body(c_ref, g_ref, r_ref, o_ref):
        o_ref[...] = (g_ref[...] + r_ref[...]).astype(BF16)

    grid_spec = pltpu.PrefetchScalarGridSpec(
        num_scalar_prefetch=1, grid=(n, R // tr),
        in_specs=[pl.BlockSpec((None, None, tr, W), lambda k, i, c_ref: (c_ref[0], k, i, 0)),
                  pl.BlockSpec((None, tr, W), lambda k, i, c_ref: (k, i, 0))],
        out_specs=pl.BlockSpec((None, tr, W), lambda k, i, c_ref: (k, i, 0)))
    return pl.pallas_call(
        body, name=name, grid_spec=grid_spec, out_shape=jax.ShapeDtypeStruct((n, R, W), BF16),
        compiler_params=_cparams(("parallel", "parallel")),
    )(c_idx, g, recv)


def _sum_pieces(pieces, *, name):
    n, R, W = pieces.shape
    tr = _tile(R, 256)

    def body(p_ref, o_ref):
        acc = p_ref[0].astype(F32)
        for k in range(1, n):
            acc = acc + p_ref[k].astype(F32)
        o_ref[...] = acc

    return pl.pallas_call(
        body, name=name, grid=(R // tr,), in_specs=[pl.BlockSpec((n, tr, W), lambda i: (0, i, 0))],
        out_specs=pl.BlockSpec((tr, W), lambda i: (i, 0)), out_shape=jax.ShapeDtypeStruct((R, W), F32),
        compiler_params=_cparams(("parallel",)),
    )(pieces)


def _half_rows(n_elems):
    rows = -(-n_elems // (2 * PACK_W))
    return -(-rows // 256) * 256


def _pack_shards(shards):
    flat = jnp.concatenate([shards[n].reshape(-1) for n, _ in BIG_WEIGHTS])
    rows = _half_rows(flat.shape[0])
    return jnp.pad(flat, (0, 2 * rows * PACK_W - flat.shape[0])).reshape(2, rows, PACK_W)


def _unpack_full(flat4, shard_shapes):
    out, off = {}, 0
    for n, axis in BIG_WEIGHTS:
        shp = shard_shapes[n]
        size = int(np.prod(shp))
        seg = flat4[:, off:off + size].reshape((N_CHIP,) + shp)
        off += size
        seg = jnp.moveaxis(seg, 0, axis)
        out[n] = seg.reshape(shp[:axis] + (N_CHIP * shp[axis],) + shp[axis + 1:])
    return out


def _pack_full(full, shard_shapes):
    segs = []
    for n, axis in BIG_WEIGHTS:
        shp = shard_shapes[n]
        a = full[n].reshape(shp[:axis] + (N_CHIP, shp[axis]) + shp[axis + 1:])
        segs.append(jnp.moveaxis(a, axis, 0).reshape(N_CHIP, -1))
    flat = jnp.concatenate(segs, axis=1)
    rows = _half_rows(flat.shape[1])
    flat = jnp.pad(flat, ((0, 0), (0, 2 * rows * PACK_W - flat.shape[1])))
    return jnp.moveaxis(flat.reshape(N_CHIP, 2, rows, PACK_W), 1, 0)


def _unpack_shards(flat, shard_shapes):
    out, off = {}, 0
    for n, _ in BIG_WEIGHTS:
        size = int(np.prod(shard_shapes[n]))
        out[n] = flat[off:off + size].reshape(shard_shapes[n])
        off += size
    return out


SMALL = ("norm_mix_g", "norm_mlp_g", "final_norm_g", "fox_b_f", "mla_q_norm_g", "mla_kv_norm_g")
WEIGHT_ORDER = ("ada_w", "ada_b", "norm_mix_g", "norm_mlp_g", "fox_w_in", "fox_b_f", "fox_w_out", "mla_w_dq",
                "mla_q_norm_g", "mla_w_uq", "mla_w_dkv", "mla_kv_norm_g", "mla_w_ukv", "mla_w_out", "mlp_w1",
                "mlp_w2", "final_norm_g")


def _small_rows(vals, D):
    rows = [vals["norm_mix_g"], vals["norm_mlp_g"], vals["final_norm_g"][None, :]]
    for n in ("fox_b_f", "mla_q_norm_g", "mla_kv_norm_g"):
        flat = vals[n].reshape(-1)
        assert flat.shape[0] <= D
        rows.append(jnp.pad(flat, (0, D - flat.shape[0]))[None, :])
    return jnp.concatenate(rows, axis=0)


def _small_unrows(rows, shapes):
    L = shapes["norm_mix_g"][0]
    out = {"norm_mix_g": rows[0:L], "norm_mlp_g": rows[L:2 * L], "final_norm_g": rows[2 * L]}
    for k, n in enumerate(("fox_b_f", "mla_q_norm_g", "mla_kv_norm_g")):
        size = int(np.prod(shapes[n]))
        out[n] = rows[2 * L + 1 + k, :size].reshape(shapes[n])
    return out


def kernel(x, c, positions, ada_w, ada_b, norm_mix_g, norm_mlp_g, fox_w_in, fox_b_f, fox_w_out, mla_w_dq, mla_q_norm_g, mla_w_uq, mla_w_dkv, mla_kv_norm_g, mla_w_ukv, mla_w_out, mlp_w1, mlp_w2, final_norm_g, loss_target, m_ada_w, m_ada_b, m_norm_mix_g, m_norm_mlp_g, m_fox_w_in, m_fox_b_f, m_fox_w_out, m_mla_w_dq, m_mla_q_norm_g, m_mla_w_uq, m_mla_w_dkv, m_mla_kv_norm_g, m_mla_w_ukv, m_mla_w_out, m_mlp_w1, m_mlp_w2, m_final_norm_g, v_ada_w, v_ada_b, v_norm_mix_g, v_norm_mlp_g, v_fox_w_in, v_fox_b_f, v_fox_w_out, v_mla_w_dq, v_mla_q_norm_g, v_mla_w_uq, v_mla_w_dkv, v_mla_kv_norm_g, v_mla_w_ukv, v_mla_w_out, v_mlp_w1, v_mlp_w2, v_final_norm_g):
    args = dict(locals())
    wts = {n: args[n] for n in WEIGHT_ORDER}
    mom = {n: args["m_" + n] for n in WEIGHT_ORDER}
    var = {n: args["v_" + n] for n in WEIGHT_ORDER}
    Bl, S, D = x.shape
    T = Bl * S
    L = ada_w.shape[0]
    C = ada_w.shape[2]
    mx, my, mc = _mesh_pos()
    chip = 2 * mx + my
    dev = 4 * mx + 2 * my + mc
    c_idx = jnp.reshape(mc, (1,)).astype(jnp.int32)

    shard_shapes = {n: wts[n].shape for n, _ in BIG_WEIGHTS}
    packed = _pack_shards({n: wts[n].astype(BF16) for n, _ in BIG_WEIGHTS})
    my_half = lax.dynamic_index_in_dim(packed, mc, axis=0, keepdims=False)
    gathered = _all_gather8(my_half, name="gather_weights", in_vmem=False)
    full = _unpack_full(gathered.reshape(N_CHIP, -1), shard_shapes)
    small = {n: wts[n] for n in SMALL}
    q_cols = mla_q_norm_g.shape[1]

    c_pad = jnp.pad(c, ((0, 8 - Bl), (0, 0)))
    c8 = _all_gather8(c_pad, name="gather_c", in_vmem=True)
    c_all = c8[:, :Bl].reshape(N_DEV * Bl, D)
    ada_b_cols = lax.dynamic_slice_in_dim(ada_b, chip * C, C, axis=1)[:, None, :]
    mod_cols = _ada_fwd(c_all, ada_w, ada_b_cols)
    mod8 = _all_gather8(mod_cols.reshape(L * N_DEV * Bl, C), name="gather_mod", in_vmem=True)
    mod4 = mod8.reshape(N_CHIP, 2, L, N_DEV * Bl, C)[:, 0]
    mod_me = lax.dynamic_slice_in_dim(mod4, dev * Bl, Bl, axis=2)
    mod = jnp.transpose(mod_me, (1, 2, 0, 3)).reshape(L, Bl, 6, D)
    mod = jnp.transpose(mod, (0, 2, 1, 3))[:, :, :, None, :]

    qg8 = _all_gather8(jnp.pad(mla_q_norm_g, ((0, 8 - mla_q_norm_g.shape[0]), (0, LANES - q_cols))),
                       name="gather_qnorm", in_vmem=True)
    qg4 = qg8.reshape(N_CHIP, 2, 8, LANES)[:, 0, :mla_q_norm_g.shape[0], :q_cols]
    small["mla_q_norm_g"] = jnp.transpose(qg4, (1, 0, 2)).reshape(mla_q_norm_g.shape[0], N_CHIP * q_cols)

    w = _kernel_weights(full, small)
    half = ROPE_DIM // 2
    inv_freq = ROPE_THETA ** (-jnp.arange(0, ROPE_DIM, 2, dtype=F32) / ROPE_DIM)
    lane = np.arange(LANES)
    inv_freq_row = jnp.tile(inv_freq, LANES // half)[None, :]
    sign_row = jnp.asarray(np.where(lane < 2 * ROPE_DIM, np.where(lane % ROPE_DIM < half, -1.0, 1.0), 0.0), F32)[None, :]
    pos_f = positions.astype(F32).reshape(T, 1)
    loss_row, grad_x, dmod, g = _local_step(x.reshape(T, D), loss_target.reshape(T, D), pos_f, inv_freq_row, sign_row,
                                            mod, w, S=S)
    g_big, g_small = _natural_grads(g, fox_b_f.shape[1], full["mla_w_dq"].shape[-1])

    Rs = -(-(2 * L + 5) // 8) * 8
    srows = jnp.concatenate([_small_rows(g_small, D), jnp.pad(loss_row, ((0, 0), (0, D - LANES)))], axis=0)
    srows = jnp.pad(srows, ((0, Rs - srows.shape[0]), (0, 0)))
    drows = jnp.transpose(dmod[:, :, :, 0, :], (2, 0, 1, 3)).reshape(Bl * L * 6, D)
    both8 = _all_gather8(jnp.concatenate([drows, srows], axis=0), name="gather_small", in_vmem=True)
    dm8 = both8[:, :Bl * L * 6].reshape(N_DEV, Bl, L * 6, D)
    sm8 = both8[:, Bl * L * 6:]
    adb_rows, small_sum = _sum_gathered(dm8, sm8)
    grad_ada_b = adb_rows.reshape(L, 6 * D)
    loss = small_sum[2 * L + 4, 0]
    small_shapes = {n: (wts[n].shape if n != "mla_q_norm_g" else (wts[n].shape[0], N_CHIP * q_cols)) for n in SMALL}
    gs = _small_unrows(small_sum, small_shapes)
    gs["mla_q_norm_g"] = lax.dynamic_slice_in_dim(gs["mla_q_norm_g"], chip * q_cols, q_cols, axis=1)

    dmod16 = jnp.transpose(dm8.reshape(N_DEV, Bl, L, 6 * D), (2, 0, 1, 3)).reshape(L, N_DEV * Bl, 6 * D)
    dmod_cols = lax.dynamic_slice_in_dim(dmod16, chip * C, C, axis=2)
    grad_ada_w = _ada_bwd(c_all, dmod_cols)

    gp = _pack_full(g_big, shard_shapes)
    from_sibling = _pair_exchange(gp, name="grad_pair_exchange")
    pair_sum = _pair_add(gp, from_sibling, c_idx, name="grad_pair_add")
    pieces = _chip_exchange(pair_sum, name="grad_chip_exchange")
    g_half = _sum_pieces(pieces, name="grad_sum_pieces")
    g_flat = _pair_gather(g_half, name="grad_pair_gather").reshape(-1)
    grads = _unpack_shards(g_flat, shard_shapes)
    grads.update(gs)
    grads["ada_w"] = grad_ada_w
    grads["ada_b"] = grad_ada_b

    delta, new_m, new_v = {}, {}, {}
    for n in ("ada_w",) + tuple(k for k, _ in BIG_WEIGHTS) + ("ada_b",):
        delta[n], new_m[n], new_v[n] = _adamw(wts[n], grads[n], mom[n], var[n], name=f"adamw_{n}")
    shard_small_shapes = {n: wts[n].shape for n in SMALL}
    packs = [jnp.pad(_small_rows({n: src[n] for n in SMALL}, D), ((0, Rs - 2 * L - 4), (0, 0)))
             for src in (wts, grads, mom, var)]
    for dst, rows in zip((delta, new_m, new_v), _adamw(*packs, name="adamw_small")):
        dst.update(_small_unrows(rows, shard_small_shapes))

    return (loss, grad_x.reshape(Bl, S, D), *[grads[n] for n in WEIGHT_ORDER], *[delta[n] for n in WEIGHT_ORDER],
            *[new_m[n] for n in WEIGHT_ORDER], *[new_v[n] for n in WEIGHT_ORDER])
```

```python
import functools

import numpy as np
import jax
import jax.numpy as jnp
from jax import lax
from jax.experimental import pallas as pl
from jax.experimental.pallas import tpu as pltpu

F32 = jnp.float32
BF16 = jnp.bfloat16
MESH_ID = pl.DeviceIdType.MESH

NORM_EPS = 1e-6
ROPE_THETA = 10000.0
HEAD_DIM = 64
ROPE_DIM = 32
KV_RANK = 128
FOX_EXTRA = 6
PAIR_Q = 256
PAIR_KV = 384
LANES = 128
ADAM_LR = 0.001
ADAM_B1 = 0.9
ADAM_B2 = 0.999
ADAM_EPS = 1e-08
ADAM_WD = 0.01
ADAM_STEP = 10
VMEM_LIMIT_V7X = 48 * 1024 * 1024
NEG_BIG = -1e30

BIG_WEIGHTS = (("fox_w_in", 2), ("fox_w_out", 1), ("mla_w_dq", 1), ("mla_w_uq", 2), ("mla_w_dkv", 1),
               ("mla_w_ukv", 2), ("mla_w_out", 1), ("mlp_w1", 2), ("mlp_w2", 1))


def _cparams(sem=None):
    return pltpu.CompilerParams(dimension_semantics=sem, vmem_limit_bytes=VMEM_LIMIT_V7X)


def _tile(n, want):
    if n <= want:
        return n
    for t in range(want - want % LANES, 0, -LANES):
        if n % t == 0:
            return t
    raise ValueError((n, want))


def _mm(a, b, mode, *, name, out_dtypes=(F32,), epilogue=None, extras=(), rowvecs=(), tables=(),
        seq=None, a_off=0, a_sz=None, b_layer=None, out_stack=None, out_split=0, tm=512, tn=512, tk=1024):
    b_rows, b_cols = b.shape[-2], b.shape[-1]
    n_split = b.shape[1] if b.ndim == 4 else 1
    if mode == "nn":
        M, K, N = a.shape[0], b_rows, b_cols * n_split
    elif mode == "nt":
        M, K, N = a.shape[0], b_cols * n_split, b_rows
    else:
        assert b.ndim == 2
        K, N = b.shape
        M = a.shape[1] - a_off if a_sz is None else a_sz
    if a_sz is not None and mode != "tn":
        assert a_sz == K
    tm = _tile(seq if rowvecs else M, tm)
    n_piece = N // max(out_split, n_split if mode == "nn" else 1, 1)
    tn = _tile(n_piece, tn)
    tk = _tile(K // (n_split if mode == "nt" else 1), 512 if mode == "tn" else tk)
    nk = K // tk
    ne, nr, nt_ = len(extras), len(rowvecs), len(tables)
    no = len(out_dtypes)

    if mode == "tn":
        assert a_off % tm == 0
        a_spec = pl.BlockSpec((tk, tm), lambda i, j, k: (k, i + a_off // tm))
        dims = (((0,), (0,)), ((), ()))
    else:
        assert a_off % tk == 0
        a_spec = pl.BlockSpec((tm, tk), lambda i, j, k: (i, k + a_off // tk))
        dims = (((1,), (0,)), ((), ())) if mode == "nn" else (((1,), (1,)), ((), ()))
    lead = () if b.ndim == 2 else (b_layer,)
    sq = (None,) * (b.ndim - 2)
    if mode == "nt":
        kb = b_cols // tk
        if b.ndim == 4:
            b_spec = pl.BlockSpec(sq + (tn, tk), lambda i, j, k: lead + (k // kb, j, k % kb))
        else:
            b_spec = pl.BlockSpec(sq + (tn, tk), lambda i, j, k: lead + (j, k))
    else:
        nb = b_cols // tn
        if b.ndim == 4:
            b_spec = pl.BlockSpec(sq + (tk, tn), lambda i, j, k: lead + (j // nb, k, j % nb))
        else:
            b_spec = pl.BlockSpec(sq + (tk, tn), lambda i, j, k: lead + (k, j))
    in_specs = [a_spec, b_spec]
    in_specs += [pl.BlockSpec((tm, tn), lambda i, j, k: (i, j)) for _ in extras]
    if rowvecs:
        assert seq % tm == 0
        per = seq // tm
        in_specs += [pl.BlockSpec((None, 1, tn), lambda i, j, k: (i // per, 0, j)) for _ in rowvecs]
    in_specs += [pl.BlockSpec((tm, LANES), lambda i, j, k: (i, 0)) for _ in tables]
    operands = [a, b, *extras, *rowvecs, *tables]
    aliases = {}
    if out_stack is None:
        out_specs = [pl.BlockSpec((tm, tn), lambda i, j, k: (i, j)) for _ in out_dtypes]
        out_shape = [jax.ShapeDtypeStruct((M, N), d) for d in out_dtypes]
    else:
        prev, layer, n_layers = out_stack
        assert no == 1
        if out_split:
            ob = n_piece // tn
            out_specs = [pl.BlockSpec((None, None, tm, tn), lambda i, j, k: (layer, j // ob, i, j % ob))]
            out_shape = [jax.ShapeDtypeStruct((n_layers, out_split, M, n_piece), out_dtypes[0])]
        else:
            out_specs = [pl.BlockSpec((None, tm, tn), lambda i, j, k: (layer, i, j))]
            out_shape = [jax.ShapeDtypeStruct((n_layers, M, N), out_dtypes[0])]
        if prev is not None:
            in_specs.append(pl.BlockSpec(memory_space=pl.ANY))
            aliases = {len(operands): 0}
            operands.append(prev)
    n_in = len(operands)

    def body(*refs):
        a_ref, b_ref = refs[0], refs[1]
        side = refs[2:2 + ne + nr + nt_]
        outs = refs[n_in:n_in + no]

        def finish(acc):
            res = (acc,) if epilogue is None else epilogue(acc, *[r[...] for r in side])
            for o_ref, r in zip(outs, res):
                o_ref[...] = r.astype(o_ref.dtype)

        part = lax.dot_general(a_ref[...].astype(BF16), b_ref[...].astype(BF16), dims,
                               preferred_element_type=F32)
        if nk == 1:
            finish(part)
        else:
            acc_ref = refs[-1]
            k = pl.program_id(2)

            @pl.when(k == 0)
            def _():
                acc_ref[...] = part

            @pl.when(k > 0)
            def _():
                acc_ref[...] += part

            @pl.when(k == nk - 1)
            def _():
                finish(acc_ref[...])

    res = pl.pallas_call(
        body, name=name, grid=(M // tm, N // tn, nk), in_specs=in_specs, out_specs=out_specs,
        out_shape=out_shape, scratch_shapes=[pltpu.VMEM((tm, tn), F32)] if nk > 1 else [],
        input_output_aliases=aliases,
        compiler_params=_cparams(("parallel", "parallel", "arbitrary")),
    )(*operands)
    return res[0] if no == 1 else tuple(res)


def _rope128(x, cos_t, sin_s):
    lane = lax.broadcasted_iota(jnp.int32, x.shape, 1)
    first = (lane % ROPE_DIM) < (ROPE_DIM // 2)
    swapped = jnp.where(first, pltpu.roll(x, LANES - ROPE_DIM // 2, 1), pltpu.roll(x, ROPE_DIM // 2, 1))
    return x * cos_t + swapped * sin_s


def _rope_pairs(acc, cos_t, sin_s, sign):
    parts = []
    for p in range(acc.shape[1] // PAIR_Q):
        parts.append(acc[:, p * PAIR_Q:p * PAIR_Q + LANES])
        parts.append(_rope128(acc[:, p * PAIR_Q + LANES:(p + 1) * PAIR_Q], cos_t, sign * sin_s))
    return jnp.concatenate(parts, axis=1)


def _rope_tables(pos_f, inv_freq_row, sign_row):
    T = pos_f.shape[0]
    tt = _tile(T, 512)

    def body(p_ref, f_ref, s_ref, cos_ref, sin_ref):
        ang = p_ref[...] * f_ref[...]
        cos_ref[...] = jnp.cos(ang)
        sin_ref[...] = jnp.sin(ang) * s_ref[...]

    return pl.pallas_call(
        body, name="rope_tables", grid=(T // tt,),
        in_specs=[pl.BlockSpec((tt, 1), lambda i: (i, 0)), pl.BlockSpec((1, LANES), lambda i: (0, 0)),
                  pl.BlockSpec((1, LANES), lambda i: (0, 0))],
        out_specs=[pl.BlockSpec((tt, LANES), lambda i: (i, 0))] * 2,
        out_shape=[jax.ShapeDtypeStruct((T, LANES), F32)] * 2,
        compiler_params=_cparams(("parallel",)),
    )(pos_f, inv_freq_row, sign_row)


def _unrope(dqx, cos_t, sin_s):
    T, W = dqx.shape
    tt = _tile(T, 512)

    def body(d_ref, c_ref, s_ref, o_ref):
        o_ref[...] = _rope_pairs(d_ref[...].astype(F32), c_ref[...], s_ref[...], -1.0).astype(BF16)

    return pl.pallas_call(
        body, name="mla_unrope", grid=(T // tt,),
        in_specs=[pl.BlockSpec((tt, W), lambda i: (i, 0)), pl.BlockSpec((tt, LANES), lambda i: (i, 0)),
                  pl.BlockSpec((tt, LANES), lambda i: (i, 0))],
        out_specs=pl.BlockSpec((tt, W), lambda i: (i, 0)),
        out_shape=jax.ShapeDtypeStruct((T, W), BF16),
        compiler_params=_cparams(("parallel",)),
    )(dqx, cos_t, sin_s)


def _row_specs(tt, D, per, n):
    return [pl.BlockSpec((None, 1, D), lambda i: (i // per, 0, 0)) for _ in range(n)]


def _norm_mod(x, gain, sc, sh, *, S, name):
    T, D = x.shape
    tt = _tile(S, 512)
    per = S // tt

    def body(x_ref, g_ref, sc_ref, sh_ref, h_ref):
        xv = x_ref[...]
        r = lax.rsqrt(jnp.mean(xv * xv, axis=-1, keepdims=True) + NORM_EPS)
        h_ref[...] = ((xv * r) * g_ref[...] * (1.0 + sc_ref[...]) + sh_ref[...]).astype(BF16)

    return pl.pallas_call(
        body, name=name, grid=(T // tt,),
        in_specs=[pl.BlockSpec((tt, D), lambda i: (i, 0)), pl.BlockSpec((1, D), lambda i: (0, 0))]
        + _row_specs(tt, D, per, 2),
        out_specs=pl.BlockSpec((tt, D), lambda i: (i, 0)),
        out_shape=jax.ShapeDtypeStruct((T, D), BF16),
        compiler_params=_cparams(("parallel",)),
    )(x, gain, sc, sh)


def _norm_mod_bwd(x, dh, dres, gain, sc, *, S, name):
    T, D = x.shape
    B = T // S
    tt = _tile(S, 512)
    per = S // tt

    def body(x_ref, dh_ref, dres_ref, g_ref, sc_ref, dx_ref, dsh_ref, dsc_ref, dg_ref):
        i = pl.program_id(0)
        xv = x_ref[...]
        dhv = dh_ref[...].astype(F32)
        r = lax.rsqrt(jnp.mean(xv * xv, axis=-1, keepdims=True) + NORM_EPS)
        n = xv * r
        g = g_ref[...]
        one_sc = 1.0 + sc_ref[...]
        dn = dhv * (g * one_sc)
        dx_ref[...] = dres_ref[...] + r * (dn - n * jnp.mean(dn * n, axis=-1, keepdims=True))
        dhn = dhv * n

        @pl.when(i % per == 0)
        def _():
            dsh_ref[...] = jnp.zeros_like(dsh_ref)
            dsc_ref[...] = jnp.zeros_like(dsc_ref)

        @pl.when(i == 0)
        def _():
            dg_ref[...] = jnp.zeros_like(dg_ref)

        dsh_ref[...] += jnp.sum(dhv, axis=0, keepdims=True)
        dsc_ref[...] += jnp.sum(dhn, axis=0, keepdims=True) * g
        dg_ref[...] += jnp.sum(dhn, axis=0, keepdims=True) * one_sc

    return pl.pallas_call(
        body, name=name, grid=(T // tt,),
        in_specs=[pl.BlockSpec((tt, D), lambda i: (i, 0))] * 3 + [pl.BlockSpec((1, D), lambda i: (0, 0))]
        + _row_specs(tt, D, per, 1),
        out_specs=[pl.BlockSpec((tt, D), lambda i: (i, 0))] + _row_specs(tt, D, per, 2)
        + [pl.BlockSpec((1, D), lambda i: (0, 0))],
        out_shape=[jax.ShapeDtypeStruct((T, D), F32), jax.ShapeDtypeStruct((B, 1, D), F32),
                   jax.ShapeDtypeStruct((B, 1, D), F32), jax.ShapeDtypeStruct((1, D), F32)],
        compiler_params=_cparams(("arbitrary",)),
    )(x, dh, dres, gain, sc)


def _gate_bwd(dx, y, g, *, S, name):
    T, D = dx.shape
    B = T // S
    tt = _tile(S, 512)
    per = S // tt

    def body(dx_ref, y_ref, g_ref, dy_ref, dg_ref):
        i = pl.program_id(0)
        dxv = dx_ref[...]
        dy_ref[...] = (dxv * g_ref[...]).astype(BF16)

        @pl.when(i % per == 0)
        def _():
            dg_ref[...] = jnp.zeros_like(dg_ref)

        dg_ref[...] += jnp.sum(dxv * y_ref[...], axis=0, keepdims=True)

    return pl.pallas_call(
        body, name=name, grid=(T // tt,),
        in_specs=[pl.BlockSpec((tt, D), lambda i: (i, 0))] * 2 + _row_specs(tt, D, per, 1),
        out_specs=[pl.BlockSpec((tt, D), lambda i: (i, 0))] + _row_specs(tt, D, per, 1),
        out_shape=[jax.ShapeDtypeStruct((T, D), BF16), jax.ShapeDtypeStruct((B, 1, D), F32)],
        compiler_params=_cparams(("arbitrary",)),
    )(dx, y, g)


def _final_loss(x, target, gain):
    T, D = x.shape
    tt = _tile(T, 512)

    def body(x_ref, t_ref, g_ref, dx_ref, dg_ref, loss_ref):
        i = pl.program_id(0)
        xv = x_ref[...]
        r = lax.rsqrt(jnp.mean(xv * xv, axis=-1, keepdims=True) + NORM_EPS)
        n = xv * r
        g = g_ref[...]
        err = n * g - t_ref[...]
        dy = err * (1.0 / D)
        dn = dy * g
        dx_ref[...] = r * (dn - n * jnp.mean(dn * n, axis=-1, keepdims=True))

        @pl.when(i == 0)
        def _():
            dg_ref[...] = jnp.zeros_like(dg_ref)
            loss_ref[...] = jnp.zeros_like(loss_ref)

        dg_ref[...] += jnp.sum(dy * n, axis=0, keepdims=True)
        loss_ref[...] += jnp.sum(jnp.sum(err * err, axis=-1, keepdims=True), axis=0, keepdims=True) * (0.5 / D)

    return pl.pallas_call(
        body, name="final_loss", grid=(T // tt,),
        in_specs=[pl.BlockSpec((tt, D), lambda i: (i, 0))] * 2 + [pl.BlockSpec((1, D), lambda i: (0, 0))],
        out_specs=[pl.BlockSpec((tt, D), lambda i: (i, 0)), pl.BlockSpec((1, D), lambda i: (0, 0)),
                   pl.BlockSpec((1, LANES), lambda i: (0, 0))],
        out_shape=[jax.ShapeDtypeStruct((T, D), F32), jax.ShapeDtypeStruct((1, D), F32),
                   jax.ShapeDtypeStruct((1, LANES), F32)],
        compiler_params=_cparams(("arbitrary",)),
    )(x, target, gain)


def _head_masks(ew):
    lane = lax.broadcasted_iota(jnp.int32, (1, PAIR_Q), 1)
    m0 = (lane < HEAD_DIM) | ((lane >= LANES) & (lane < LANES + ew))
    m1 = ((lane >= HEAD_DIM) & (lane < LANES)) | ((lane >= LANES + ew) & (lane < LANES + 2 * ew))
    return m0, m1


def _dot_nt(a, b):
    return lax.dot_general(a, b, (((1,), (1,)), ((), ())), preferred_element_type=F32)


def _dot_tn(a, b):
    return lax.dot_general(a, b, (((0,), (0,)), ((), ())), preferred_element_type=F32)


def _attn_fwd(qx, kvx, *, S, scale, ew, name):
    T = qx.shape[0]
    P = qx.shape[1] // PAIR_Q
    B = T // S
    tq = _tile(S, 256)
    nq = S // tq

    def body(q_ref, kv_ref, o_ref, lse_ref, m_sc, l_sc, acc_sc):
        qi = pl.program_id(2)
        q = q_ref[...]
        masks = _head_masks(ew)
        qh = [jnp.where(m, q, jnp.zeros_like(q)) for m in masks]
        m_sc[...] = jnp.full(m_sc.shape, NEG_BIG, F32)
        l_sc[...] = jnp.zeros_like(l_sc)
        acc_sc[...] = jnp.zeros_like(acc_sc)

        def step(kj, diag):
            rows = pl.ds(pl.multiple_of(kj * tq, tq), tq)
            k = kv_ref[rows, 0:PAIR_Q]
            v = kv_ref[rows, PAIR_Q:PAIR_KV]
            for h in range(2):
                s = _dot_nt(qh[h], k)
                if scale != 1.0:
                    s = s * scale
                if diag:
                    row = lax.broadcasted_iota(jnp.int32, s.shape, 0)
                    col = lax.broadcasted_iota(jnp.int32, s.shape, 1)
                    s = jnp.where(col <= row, s, NEG_BIG)
                m_prev = m_sc[h]
                m_new = jnp.maximum(m_prev, jnp.max(s, axis=1, keepdims=True))
                alpha = jnp.exp(m_prev - m_new)
                p = jnp.exp(s - m_new[:, 0:1])
                l_sc[h] = alpha * l_sc[h] + jnp.sum(p, axis=1, keepdims=True)
                acc_sc[h] = alpha * acc_sc[h] + jnp.dot(p.astype(BF16), v, preferred_element_type=F32)
                m_sc[h] = m_new

        def loop_body(kj, carry):
            step(kj, False)
            return carry

        lax.fori_loop(0, qi, loop_body, 0)
        step(qi, True)
        lane = lax.broadcasted_iota(jnp.int32, (tq, LANES), 1)
        lo = lane < HEAD_DIM
        o_ref[...] = jnp.where(lo, acc_sc[0] / l_sc[0], acc_sc[1] / l_sc[1]).astype(BF16)
        lse_ref[...] = jnp.where(lo, m_sc[0] + jnp.log(l_sc[0]), m_sc[1] + jnp.log(l_sc[1]))

    return pl.pallas_call(
        body, name=name, grid=(B, P, nq),
        in_specs=[pl.BlockSpec((tq, PAIR_Q), lambda b, p, i: (b * nq + i, p)),
                  pl.BlockSpec((S, PAIR_KV), lambda b, p, i: (b, p))],
        out_specs=[pl.BlockSpec((tq, LANES), lambda b, p, i: (b * nq + i, p))] * 2,
        out_shape=[jax.ShapeDtypeStruct((T, P * LANES), BF16), jax.ShapeDtypeStruct((T, P * LANES), F32)],
        scratch_shapes=[pltpu.VMEM((2, tq, LANES), F32)] * 3,
        compiler_params=_cparams(("parallel", "parallel", "arbitrary")),
    )(qx, kvx)


def _attn_bwd(qx, kvx, o, lse, do, *, S, scale, ew, name, bias_grad=False):
    T = qx.shape[0]
    P = qx.shape[1] // PAIR_Q
    B = T // S
    tq = _tile(S, 256)
    nq = S // tq

    def body(q_ref, kv_ref, o_ref, lse_ref, do_ref, dq_ref, dkv_ref, *rest):
        kj = pl.program_id(2)
        if bias_grad:
            csum_ref, rsum_ref, dq_sc, delta_sc, dk_sc, dv_sc = rest
            csum_ref[...] = jnp.zeros_like(csum_ref)

            @pl.when(kj == 0)
            def _():
                rsum_ref[...] = jnp.zeros_like(rsum_ref)
        else:
            dq_sc, delta_sc, dk_sc, dv_sc = rest
        masks = _head_masks(ew)
        lane = lax.broadcasted_iota(jnp.int32, (tq, LANES), 1)
        lo = lane < HEAD_DIM
        vmask = [lo, jnp.logical_not(lo)]

        @pl.when(kj == 0)
        def _():
            dq_sc[...] = jnp.zeros_like(dq_sc)
            for c in range(nq):
                rows = pl.ds(c * tq, tq)
                x = do_ref[rows, :].astype(F32) * o_ref[rows, :].astype(F32)
                r0 = jnp.sum(jnp.where(lo, x, 0.0), axis=1, keepdims=True)
                r1 = jnp.sum(jnp.where(lo, 0.0, x), axis=1, keepdims=True)
                delta_sc[rows, :] = jnp.where(lo, r0, r1)

        k = kv_ref[:, 0:PAIR_Q]
        v = kv_ref[:, PAIR_Q:PAIR_KV]
        dk_sc[...] = jnp.zeros_like(dk_sc)
        dv_sc[...] = jnp.zeros_like(dv_sc)

        def step(qi, diag):
            rows = pl.ds(pl.multiple_of(qi * tq, tq), tq)
            q = q_ref[rows, :]
            dov = do_ref[rows, :]
            lse_v = lse_ref[rows, :]
            dl_v = delta_sc[rows, :]
            for h in range(2):
                qh = jnp.where(masks[h], q, jnp.zeros_like(q))
                s = _dot_nt(qh, k)
                if scale != 1.0:
                    s = s * scale
                if diag:
                    row = lax.broadcasted_iota(jnp.int32, s.shape, 0)
                    col = lax.broadcasted_iota(jnp.int32, s.shape, 1)
                    s = jnp.where(col <= row, s, NEG_BIG)
                c0 = h * HEAD_DIM
                p = jnp.exp(s - lse_v[:, c0:c0 + 1])
                doh = jnp.where(vmask[h], dov, jnp.zeros_like(dov))
                dp = _dot_nt(doh, v)
                ds = p * (dp - dl_v[:, c0:c0 + 1])
                if bias_grad:
                    csum_ref[h:h + 1, :] += jnp.sum(ds, axis=0, keepdims=True)
                    rsum_ref[rows, :] += jnp.where(vmask[h], jnp.sum(ds, axis=1, keepdims=True), 0.0)
                if scale != 1.0:
                    ds = ds * scale
                pb = p.astype(BF16)
                dsb = ds.astype(BF16)
                dv_sc[h] += _dot_tn(pb, dov)
                dk_sc[h] += _dot_tn(dsb, q)
                dq_sc[rows, :] += jnp.where(masks[h], jnp.dot(dsb, k, preferred_element_type=F32), 0.0)

        step(kj, True)

        def loop_body(qi, carry):
            step(qi, False)
            return carry

        lax.fori_loop(kj + 1, nq, loop_body, 0)
        dkv_ref[:, 0:PAIR_Q] = (jnp.where(masks[0], dk_sc[0], 0.0) + jnp.where(masks[1], dk_sc[1], 0.0)).astype(BF16)
        dkv_ref[:, PAIR_Q:PAIR_KV] = jnp.where(lo, dv_sc[0], dv_sc[1]).astype(BF16)

        @pl.when(kj == nq - 1)
        def _():
            dq_ref[...] = dq_sc[...].astype(BF16)

    out_specs = [pl.BlockSpec((S, PAIR_Q), lambda b, p, j: (b, p)),
                 pl.BlockSpec((tq, PAIR_KV), lambda b, p, j: (b * nq + j, p))]
    out_shape = [jax.ShapeDtypeStruct((T, P * PAIR_Q), BF16), jax.ShapeDtypeStruct((T, P * PAIR_KV), BF16)]
    if bias_grad:
        out_specs.append(pl.BlockSpec((None, 8, tq), lambda b, p, j: (p, 0, b * nq + j)))
        out_shape.append(jax.ShapeDtypeStruct((P, 8, T), F32))
        out_specs.append(pl.BlockSpec((S, LANES), lambda b, p, j: (b, p)))
        out_shape.append(jax.ShapeDtypeStruct((T, P * LANES), F32))
    return pl.pallas_call(
        body, name=name, grid=(B, P, nq),
        in_specs=[pl.BlockSpec((S, PAIR_Q), lambda b, p, j: (b, p)),
                  pl.BlockSpec((tq, PAIR_KV), lambda b, p, j: (b * nq + j, p)),
                  pl.BlockSpec((S, LANES), lambda b, p, j: (b, p)),
                  pl.BlockSpec((S, LANES), lambda b, p, j: (b, p)),
                  pl.BlockSpec((S, LANES), lambda b, p, j: (b, p))],
        out_specs=out_specs, out_shape=out_shape,
        scratch_shapes=[pltpu.VMEM((S, PAIR_Q), F32), pltpu.VMEM((S, LANES), F32),
                        pltpu.VMEM((2, tq, PAIR_Q), F32), pltpu.VMEM((2, tq, LANES), F32)],
        compiler_params=_cparams(("parallel", "parallel", "arbitrary")),
    )(qx, kvx, o, lse, do)


def _fox_consts(P):
    H = 2 * P
    eq = np.zeros((3 * LANES, P * LANES), np.float32)
    ek = np.zeros((3 * LANES, P * LANES), np.float32)
    ones_q = np.zeros((1, P * LANES), np.float32)
    ones_k = np.zeros((1, P * LANES), np.float32)
    for h in range(H):
        base = (h // 2) * LANES + FOX_EXTRA * (h % 2)
        for part in range(3):
            eq[part * LANES + h, base + part] = 1.0
            ones_q[0, base + 3 + part] = 1.0
            ones_k[0, base + part] = 1.0
            ek[part * LANES + h, base + 3 + part] = -1.0
    return eq, ek, ones_q, ones_k


def _split3(f):
    hi = f.astype(BF16)
    r = f - hi.astype(F32)
    mid = r.astype(BF16)
    lo = (r - mid.astype(F32)).astype(BF16)
    return hi, mid, lo


def _tri_sum(tri, x):
    hi, mid, lo = _split3(x)
    return (jnp.dot(tri, hi, preferred_element_type=F32) + jnp.dot(tri, mid, preferred_element_type=F32)
            + jnp.dot(tri, lo, preferred_element_type=F32))


def _log1p_pos(e):
    return jnp.where(e < 0.01, e * (1.0 - e * (0.5 - e * (1.0 / 3.0))), jnp.log(1.0 + e))


def _fox_prep(qkv, fl, b_row, *, S, D, name):
    T = qkv.shape[0]
    P = D // LANES
    B = T // S
    tt = _tile(S, 256)
    per = S // tt
    eq, ek, ones_q, ones_k = _fox_consts(P)
    q_scale = HEAD_DIM ** -0.5

    def body(q_ref, k_ref, v_ref, fl_ref, b_ref, eq_ref, ek_ref, oq_ref, ok_ref, qx_ref, kvx_ref, carry):
        i = pl.program_id(1)

        @pl.when(i == 0)
        def _():
            carry[...] = jnp.zeros_like(carry)

        z = fl_ref[...] + b_ref[...]
        logf = jnp.minimum(z, 0.0) - _log1p_pos(jnp.exp(-jnp.abs(z)))
        row = lax.broadcasted_iota(jnp.int32, (tt, tt), 0)
        col = lax.broadcasted_iota(jnp.int32, (tt, tt), 1)
        tri = (col <= row).astype(BF16)
        f = _tri_sum(tri, logf) + carry[...]
        carry[...] = f[tt - 1:tt, :]
        parts = jnp.concatenate(_split3(f), axis=1)
        xq = jnp.dot(parts, eq_ref[...], preferred_element_type=F32) + oq_ref[...]
        xk = jnp.dot(parts, ek_ref[...], preferred_element_type=F32) + ok_ref[...]
        for p in range(P):
            c = slice(p * LANES, (p + 1) * LANES)
            qx_ref[:, p * PAIR_Q:p * PAIR_Q + LANES] = (q_ref[:, c].astype(F32) * q_scale).astype(BF16)
            qx_ref[:, p * PAIR_Q + LANES:(p + 1) * PAIR_Q] = xq[:, c].astype(BF16)
            kvx_ref[:, p * PAIR_KV:p * PAIR_KV + LANES] = k_ref[:, c]
            kvx_ref[:, p * PAIR_KV + LANES:p * PAIR_KV + PAIR_Q] = xk[:, c].astype(BF16)
            kvx_ref[:, p * PAIR_KV + PAIR_Q:(p + 1) * PAIR_KV] = v_ref[:, c]

    tok = lambda b, i: (b * per + i, 0)
    const = lambda b, i: (0, 0)
    return pl.pallas_call(
        body, name=name, grid=(B, per),
        in_specs=[pl.BlockSpec((tt, D), lambda b, i: (b * per + i, 0)),
                  pl.BlockSpec((tt, D), lambda b, i: (b * per + i, 1)),
                  pl.BlockSpec((tt, D), lambda b, i: (b * per + i, 2)),
                  pl.BlockSpec((tt, LANES), tok), pl.BlockSpec((1, LANES), const),
                  pl.BlockSpec(eq.shape, const), pl.BlockSpec(ek.shape, const),
                  pl.BlockSpec(ones_q.shape, const), pl.BlockSpec(ones_k.shape, const)],
        out_specs=[pl.BlockSpec((tt, P * PAIR_Q), tok), pl.BlockSpec((tt, P * PAIR_KV), tok)],
        out_shape=[jax.ShapeDtypeStruct((T, P * PAIR_Q), BF16), jax.ShapeDtypeStruct((T, P * PAIR_KV), BF16)],
        scratch_shapes=[pltpu.VMEM((1, LANES), F32)],
        compiler_params=_cparams(("arbitrary", "arbitrary")),
    )(qkv, qkv, qkv, fl, b_row, jnp.asarray(eq, BF16), jnp.asarray(ek, BF16), jnp.asarray(ones_q), jnp.asarray(ones_k))


def _fox_unprep(dqx, dkvx, csum, rsum, fl, b_row, *, S, D, name):
    T = dqx.shape[0]
    P = D // LANES
    B = T // S
    tt = _tile(S, 256)
    per = S // tt
    q_scale = HEAD_DIM ** -0.5

    def body(dq_ref, dkv_ref, cs_ref, rs_ref, fl_ref, b_ref, dqkv_ref, dfl_ref, db_ref, carry):
        b = pl.program_id(0)
        i = pl.program_id(1)

        @pl.when(i == 0)
        def _():
            carry[...] = jnp.zeros_like(carry)

        @pl.when((i == 0) & (b == 0))
        def _():
            db_ref[...] = jnp.zeros_like(db_ref)

        df = rs_ref[...] - cs_ref[...]
        for p in range(P):
            rq = slice(p * LANES, (p + 1) * LANES)
            dqkv_ref[:, rq] = (dq_ref[:, p * PAIR_Q:p * PAIR_Q + LANES].astype(F32) * q_scale).astype(BF16)
            dqkv_ref[:, D + p * LANES:D + (p + 1) * LANES] = dkv_ref[:, p * PAIR_KV:p * PAIR_KV + LANES]
            dqkv_ref[:, 2 * D + p * LANES:2 * D + (p + 1) * LANES] = dkv_ref[:, p * PAIR_KV + PAIR_Q:(p + 1) * PAIR_KV]
        row = lax.broadcasted_iota(jnp.int32, (tt, tt), 0)
        col = lax.broadcasted_iota(jnp.int32, (tt, tt), 1)
        tri = (col >= row).astype(BF16)
        dlogf = _tri_sum(tri, df) + carry[...]
        carry[...] = dlogf[0:1, :]
        z = fl_ref[...] + b_ref[...]
        e = jnp.exp(-jnp.abs(z))
        sig_neg = jnp.where(z >= 0.0, e, 1.0) / (1.0 + e)
        dfl = dlogf * sig_neg
        dfl_ref[...] = dfl.astype(BF16)
        db_ref[...] += jnp.sum(dfl, axis=0, keepdims=True)

    rev = lambda b, i: (b * per + per - 1 - i, 0)
    const = lambda b, i: (0, 0)
    return pl.pallas_call(
        body, name=name, grid=(B, per),
        in_specs=[pl.BlockSpec((tt, P * PAIR_Q), rev), pl.BlockSpec((tt, P * PAIR_KV), rev),
                  pl.BlockSpec((tt, LANES), rev), pl.BlockSpec((tt, LANES), rev), pl.BlockSpec((tt, LANES), rev),
                  pl.BlockSpec((1, LANES), const)],
        out_specs=[pl.BlockSpec((tt, 3 * D), rev), pl.BlockSpec((tt, LANES), rev), pl.BlockSpec((1, LANES), const)],
        out_shape=[jax.ShapeDtypeStruct((T, 3 * D), BF16), jax.ShapeDtypeStruct((T, LANES), BF16),
                   jax.ShapeDtypeStruct((1, LANES), F32)],
        scratch_shapes=[pltpu.VMEM((1, LANES), F32)],
        compiler_params=_cparams(("arbitrary", "arbitrary")),
    )(dqx, dkvx, csum, rsum, fl, b_row)


def _rms(x):
    r = lax.rsqrt(jnp.mean(x * x, axis=-1, keepdims=True) + NORM_EPS)
    return x * r, r


def _mla_mid(lat, gq, gkv, cos_t, sin_s, *, name):
    T, W = lat.shape
    Rq = W - 2 * LANES
    tt = _tile(T, 512)

    def body(l_ref, gq_ref, gkv_ref, c_ref, s_ref, o_ref):
        nq, _ = _rms(l_ref[:, 0:Rq])
        nkv, _ = _rms(l_ref[:, Rq:Rq + LANES])
        o_ref[:, 0:Rq] = (nq * gq_ref[...]).astype(BF16)
        o_ref[:, Rq:Rq + LANES] = (nkv * gkv_ref[...]).astype(BF16)
        o_ref[:, Rq + LANES:W] = _rope128(l_ref[:, Rq + LANES:W], c_ref[...], s_ref[...]).astype(BF16)

    return pl.pallas_call(
        body, name=name, grid=(T // tt,),
        in_specs=[pl.BlockSpec((tt, W), lambda i: (i, 0)), pl.BlockSpec((1, Rq), lambda i: (0, 0)),
                  pl.BlockSpec((1, LANES), lambda i: (0, 0)), pl.BlockSpec((tt, LANES), lambda i: (i, 0)),
                  pl.BlockSpec((tt, LANES), lambda i: (i, 0))],
        out_specs=pl.BlockSpec((tt, W), lambda i: (i, 0)),
        out_shape=jax.ShapeDtypeStruct((T, W), BF16),
        compiler_params=_cparams(("parallel",)),
    )(lat, gq, gkv, cos_t, sin_s)


def _mla_mid_bwd(lat, dcq, dckr, gq, gkv, cos_t, sin_s, *, name):
    T, W = lat.shape
    Rq = W - 2 * LANES
    tt = _tile(T, 512)

    def norm_bwd(x, dy, g):
        n, r = _rms(x)
        dn = dy * g
        return r * (dn - n * jnp.mean(dn * n, axis=-1, keepdims=True)), jnp.sum(dy * n, axis=0, keepdims=True)

    def body(l_ref, dq_ref, dk_ref, gq_ref, gkv_ref, c_ref, s_ref, o_ref, dgq_ref, dgkv_ref):
        i = pl.program_id(0)

        @pl.when(i == 0)
        def _():
            dgq_ref[...] = jnp.zeros_like(dgq_ref)
            dgkv_ref[...] = jnp.zeros_like(dgkv_ref)

        dxq, dgq = norm_bwd(l_ref[:, 0:Rq], dq_ref[...], gq_ref[...])
        dxkv, dgkv = norm_bwd(l_ref[:, Rq:Rq + LANES], dk_ref[:, 0:LANES], gkv_ref[...])
        o_ref[:, 0:Rq] = dxq.astype(BF16)
        o_ref[:, Rq:Rq + LANES] = dxkv.astype(BF16)
        o_ref[:, Rq + LANES:W] = _rope128(dk_ref[:, LANES:2 * LANES], c_ref[...], -s_ref[...]).astype(BF16)
        dgq_ref[...] += dgq
        dgkv_ref[...] += dgkv

    return pl.pallas_call(
        body, name=name, grid=(T // tt,),
        in_specs=[pl.BlockSpec((tt, W), lambda i: (i, 0)), pl.BlockSpec((tt, Rq), lambda i: (i, 0)),
                  pl.BlockSpec((tt, 2 * LANES), lambda i: (i, 0)), pl.BlockSpec((1, Rq), lambda i: (0, 0)),
                  pl.BlockSpec((1, LANES), lambda i: (0, 0)), pl.BlockSpec((tt, LANES), lambda i: (i, 0)),
                  pl.BlockSpec((tt, LANES), lambda i: (i, 0))],
        out_specs=[pl.BlockSpec((tt, W), lambda i: (i, 0)), pl.BlockSpec((1, Rq), lambda i: (0, 0)),
                   pl.BlockSpec((1, LANES), lambda i: (0, 0))],
        out_shape=[jax.ShapeDtypeStruct((T, W), BF16), jax.ShapeDtypeStruct((1, Rq), F32),
                   jax.ShapeDtypeStruct((1, LANES), F32)],
        compiler_params=_cparams(("arbitrary",)),
    )(lat, dcq, dckr, gq, gkv, cos_t, sin_s)


def _uq_to_pairs(w):
    Rq = w.shape[0]
    P = w.shape[1] // (2 * (HEAD_DIM + ROPE_DIM))
    w4 = w.reshape(Rq, P, 2, HEAD_DIM + ROPE_DIM)
    nope = w4[..., :HEAD_DIM].reshape(Rq, P, 2 * HEAD_DIM)
    rope = w4[..., HEAD_DIM:].reshape(Rq, P, 2 * ROPE_DIM)
    pad = jnp.zeros((Rq, P, PAIR_Q - 2 * HEAD_DIM - 2 * ROPE_DIM), w.dtype)
    return jnp.concatenate([nope, rope, pad], axis=-1).reshape(Rq, P * PAIR_Q)


def _uq_from_pairs(g):
    Rq = g.shape[0]
    P = g.shape[1] // PAIR_Q
    g3 = g.reshape(Rq, P, PAIR_Q)
    nope = g3[..., :2 * HEAD_DIM].reshape(Rq, P, 2, HEAD_DIM)
    rope = g3[..., 2 * HEAD_DIM:2 * HEAD_DIM + 2 * ROPE_DIM].reshape(Rq, P, 2, ROPE_DIM)
    return jnp.concatenate([nope, rope], axis=-1).reshape(Rq, P * 2 * (HEAD_DIM + ROPE_DIM))


def _ukv_to_pairs(w):
    P = w.shape[1] // (4 * HEAD_DIM)
    w4 = w.reshape(KV_RANK, P, 2, 2 * HEAD_DIM)
    kn = w4[..., :HEAD_DIM].reshape(KV_RANK, P, 2 * HEAD_DIM)
    vv = w4[..., HEAD_DIM:].reshape(KV_RANK, P, 2 * HEAD_DIM)
    top = jnp.concatenate([kn, jnp.zeros((KV_RANK, P, LANES), w.dtype), vv], axis=-1)
    place = np.zeros((LANES, P, PAIR_KV), np.float32)
    for r in range(ROPE_DIM):
        place[r, :, LANES + r] = 1.0
        place[r, :, LANES + ROPE_DIM + r] = 1.0
    return jnp.concatenate([top, jnp.asarray(place, w.dtype)], axis=0).reshape(KV_RANK + LANES, P * PAIR_KV)


def _ukv_from_pairs(g):
    P = g.shape[1] // PAIR_KV
    g3 = g[:KV_RANK].reshape(KV_RANK, P, PAIR_KV)
    kn = g3[..., :2 * HEAD_DIM].reshape(KV_RANK, P, 2, HEAD_DIM)
    vv = g3[..., PAIR_Q:].reshape(KV_RANK, P, 2, HEAD_DIM)
    return jnp.concatenate([kn, vv], axis=-1).reshape(KV_RANK, P * 4 * HEAD_DIM)


def _mlp_fwd(h2, w, i, x1, gate, *, S):
    p, u = _mm(h2, w["mlp_w1"], "nn", name=f"mlp_up_{i}", b_layer=i, out_dtypes=(BF16, BF16),
               epilogue=lambda acc: (acc, jnp.square(jnp.maximum(acc, 0.0))))
    x2, z = _mm(u, w["mlp_w2"], "nn", name=f"mlp_down_{i}", b_layer=i, out_dtypes=(F32, F32), extras=(x1,),
                rowvecs=(gate,), seq=S, epilogue=lambda acc, xr, g: (xr + g * acc, acc))
    return x2, (p, u, z)


STACKED_GRADS = ("fox_out", "mla_down", "mla_uq", "mla_ukv", "mla_out", "mlp_w1", "mlp_w2")


def _local_step(x, target, pos_f, inv_freq_row, sign_row, mod, w, *, S):
    T, D = x.shape
    L = mod.shape[0]
    L2 = w["fox_out"].shape[0]
    n_split = w["mlp_w1"].shape[1]
    cos_t, sin_s = _rope_tables(pos_f, inv_freq_row, sign_row)
    saved = []
    for i in range(L):
        j = i // 2
        sh_m, sc_m, g_m, sh_f, sc_f, g_f = (mod[i, s] for s in range(6))
        h = _norm_mod(x, w["norm_mix_g"][i], sc_m, sh_m, S=S, name=f"norm_mix_{i}")
        if i % 2 == 0:
            qkv = _mm(h, w["fox_qkv"], "nn", name=f"fox_qkv_{i}", b_layer=j, out_dtypes=(BF16,))
            fl = _mm(h, w["fox_f"], "nn", name=f"fox_f_{i}", b_layer=j)
            qx, kvx = _fox_prep(qkv, fl, w["fox_b"][j], S=S, D=D, name=f"fox_prep_{i}")
            o, lse = _attn_fwd(qx, kvx, S=S, scale=1.0, ew=FOX_EXTRA, name=f"fox_attn_{i}")
            mix = (qx, kvx, o, lse, fl)
            w_out = w["fox_out"]
        else:
            lat = _mm(h, w["mla_down"], "nn", name=f"mla_down_{i}", b_layer=j)
            Rq = lat.shape[1] - 2 * LANES
            cqr = _mla_mid(lat, w["mla_gq"][j], w["mla_gkv"][j], cos_t, sin_s, name=f"mla_mid_{i}")
            qx = _mm(cqr, w["mla_uq"], "nn", name=f"mla_uq_{i}", b_layer=j, out_dtypes=(BF16,), a_sz=Rq, tk=Rq,
                     tables=(cos_t, sin_s), epilogue=lambda acc, c, s: (_rope_pairs(acc, c, s, 1.0),))
            kvx = _mm(cqr, w["mla_ukv"], "nn", name=f"mla_ukv_{i}", b_layer=j, out_dtypes=(BF16,), a_off=Rq,
                      a_sz=2 * LANES, tk=2 * LANES, tn=PAIR_KV)
            o, lse = _attn_fwd(qx, kvx, S=S, scale=(HEAD_DIM + ROPE_DIM) ** -0.5, ew=ROPE_DIM, name=f"mla_attn_{i}")
            mix = (qx, kvx, o, lse, lat, cqr)
            w_out = w["mla_out"]
        x1, y = _mm(o, w_out, "nn", name=f"mix_out_{i}", b_layer=j, out_dtypes=(F32, F32), extras=(x,),
                    rowvecs=(g_m,), seq=S, epilogue=lambda acc, xr, g: (xr + g * acc, acc))
        h2 = _norm_mod(x1, w["norm_mlp_g"][i], sc_f, sh_f, S=S, name=f"norm_mlp_{i}")
        x2, mlp = _mlp_fwd(h2, w, i, x1, g_f, S=S)
        saved.append((x, h, mix, y, x1, h2, mlp))
        x = x2

    dx, dg_final, loss = _final_loss(x, target, w["final_norm_g"])

    grads = {k: [None] * len(w[k]) for k in ("norm_mix_g", "norm_mlp_g", "fox_b", "mla_gq", "mla_gkv")}
    grads.update({k: [None] * L2 for k in ("fox_qkv", "fox_f")})
    grads.update({k: None for k in STACKED_GRADS})
    grads["final_norm_g"] = dg_final

    def stacked(key, layer, n_layers, a, b, **kw):
        grads[key] = _mm(a, b, "tn", out_stack=(grads[key], layer, n_layers), **kw)

    dmod = [None] * L
    for i in reversed(range(L)):
        j = i // 2
        x0, h, mix, y, x1, h2, (p, u, z) = saved[i]
        sh_m, sc_m, g_m, sh_f, sc_f, g_f = (mod[i, s] for s in range(6))
        dz, dg_f = _gate_bwd(dx, z, g_f, S=S, name=f"gate_mlp_bwd_{i}")
        stacked("mlp_w2", i, L, u, dz, name=f"mlp_w2_grad_{i}")
        dp = _mm(dz, w["mlp_w2"], "nt", name=f"mlp_down_bwd_{i}", b_layer=i, out_dtypes=(BF16,), extras=(p,),
                 epilogue=lambda acc, pv: (acc * (2.0 * jnp.maximum(pv.astype(F32), 0.0)),))
        stacked("mlp_w1", i, L, h2, dp, name=f"mlp_w1_grad_{i}", out_split=n_split)
        dh2 = _mm(dp, w["mlp_w1"], "nt", name=f"mlp_up_bwd_{i}", b_layer=i)
        dx1, dsh_f, dsc_f, dgn = _norm_mod_bwd(x1, dh2, dx, w["norm_mlp_g"][i], sc_f, S=S, name=f"norm_mlp_bwd_{i}")
        grads["norm_mlp_g"][i] = dgn
        dy, dg_m = _gate_bwd(dx1, y, g_m, S=S, name=f"gate_mix_bwd_{i}")
        if i % 2 == 0:
            qx, kvx, o, lse, fl = mix
            stacked("fox_out", j, L2, o, dy, name=f"fox_out_grad_{i}")
            do = _mm(dy, w["fox_out"], "nt", name=f"fox_out_bwd_{i}", b_layer=j, out_dtypes=(BF16,))
            dqx, dkvx, csum, rsum = _attn_bwd(qx, kvx, o, lse, do, S=S, scale=1.0, ew=FOX_EXTRA,
                                              name=f"fox_attn_bwd_{i}", bias_grad=True)
            n_heads = D // HEAD_DIM
            csum = jnp.pad(csum[:, :2, :].reshape(n_heads, T).T, ((0, 0), (0, LANES - n_heads)))
            rsum = jnp.pad(rsum.reshape(T, n_heads, HEAD_DIM)[:, :, 0], ((0, 0), (0, LANES - n_heads)))
            dqkv, dfl, db = _fox_unprep(dqx, dkvx, csum, rsum, fl, w["fox_b"][j], S=S, D=D, name=f"fox_unprep_{i}")
            grads["fox_b"][j] = db
            grads["fox_qkv"][j] = _mm(h, dqkv, "tn", name=f"fox_qkv_grad_{i}")
            grads["fox_f"][j] = _mm(h, dfl, "tn", name=f"fox_f_grad_{i}")
            dh_f = _mm(dfl, w["fox_f"], "nt", name=f"fox_f_bwd_{i}", b_layer=j)
            dh = _mm(dqkv, w["fox_qkv"], "nt", name=f"fox_qkv_bwd_{i}", b_layer=j, extras=(dh_f,),
                     epilogue=lambda acc, e: (acc + e,))
        else:
            qx, kvx, o, lse, lat, cqr = mix
            Rq = lat.shape[1] - 2 * LANES
            stacked("mla_out", j, L2, o, dy, name=f"mla_out_grad_{i}")
            do = _mm(dy, w["mla_out"], "nt", name=f"mla_out_bwd_{i}", b_layer=j, out_dtypes=(BF16,))
            dqx, dkvx = _attn_bwd(qx, kvx, o, lse, do, S=S, scale=(HEAD_DIM + ROPE_DIM) ** -0.5, ew=ROPE_DIM,
                                  name=f"mla_attn_bwd_{i}")
            dqpre = _unrope(dqx, cos_t, sin_s)
            stacked("mla_uq", j, L2, cqr, dqpre, name=f"mla_uq_grad_{i}", a_sz=Rq, tm=min(Rq, 256),
                    out_split=n_split)
            stacked("mla_ukv", j, L2, cqr, dkvx, name=f"mla_ukv_grad_{i}", a_off=Rq, a_sz=2 * LANES,
                    tm=2 * LANES, tn=PAIR_KV, out_split=n_split)
            dcq = _mm(dqpre, w["mla_uq"], "nt", name=f"mla_uq_bwd_{i}", b_layer=j)
            dckr = _mm(dkvx, w["mla_ukv"], "nt", name=f"mla_ukv_bwd_{i}", b_layer=j, tk=PAIR_KV * 2)
            dlat, dgq, dgkv = _mla_mid_bwd(lat, dcq, dckr, w["mla_gq"][j], w["mla_gkv"][j], cos_t, sin_s,
                                           name=f"mla_mid_bwd_{i}")
            grads["mla_gq"][j] = dgq
            grads["mla_gkv"][j] = dgkv
            stacked("mla_down", j, L2, h, dlat, name=f"mla_down_grad_{i}")
            dh = _mm(dlat, w["mla_down"], "nt", name=f"mla_down_bwd_{i}", b_layer=j)
        dx, dsh_m, dsc_m, dgn = _norm_mod_bwd(x0, dh, dx1, w["norm_mix_g"][i], sc_m, S=S, name=f"norm_mix_bwd_{i}")
        grads["norm_mix_g"][i] = dgn
        dmod[i] = jnp.stack([dsh_m, dsc_m, dg_m, dsh_f, dsc_f, dg_f])
    return loss, dx, jnp.stack(dmod), grads


GATHERED = ("fox_in", "fox_out", "mla_down", "mla_uq", "mla_ukv", "mla_out", "mlp_w1", "mlp_w2")
ROW_SHARDED = ("fox_out", "mla_down", "mla_out", "mlp_w2")


def _shard_layouts(wts):
    dkv = wts["mla_w_dkv"]
    dkv = jnp.pad(dkv, ((0, 0), (0, 0), (0, 2 * LANES - dkv.shape[2])))
    return {
        "fox_in": wts["fox_w_in"].astype(BF16),
        "fox_out": wts["fox_w_out"].astype(BF16),
        "mla_down": jnp.concatenate([wts["mla_w_dq"], dkv], axis=2).astype(BF16),
        "mla_uq": jax.vmap(_uq_to_pairs)(wts["mla_w_uq"].astype(BF16)),
        "mla_ukv": jax.vmap(_ukv_to_pairs)(wts["mla_w_ukv"].astype(BF16)),
        "mla_out": wts["mla_w_out"].astype(BF16),
        "mlp_w1": wts["mlp_w1"].astype(BF16),
        "mlp_w2": wts["mlp_w2"].astype(BF16),
    }


def _small_layouts(small):
    return {
        "fox_b": [jnp.pad(b, (0, LANES - b.shape[0]))[None, :] for b in small["fox_b_f"]],
        "mla_gq": [g[None, :] for g in small["mla_q_norm_g"]],
        "mla_gkv": [g[None, :] for g in small["mla_kv_norm_g"]],
        "norm_mix_g": [g[None, :] for g in small["norm_mix_g"]],
        "norm_mlp_g": [g[None, :] for g in small["norm_mlp_g"]],
        "final_norm_g": small["final_norm_g"][None, :],
    }


def _full_layouts(gathered, D):
    fox = gathered["fox_in"]
    L2, ns, _, bs = fox.shape
    fox = jnp.transpose(fox, (0, 2, 1, 3)).reshape(L2, D, ns * bs)
    w = {"fox_qkv": fox[:, :, :3 * D],
         "fox_f": jnp.pad(fox[:, :, 3 * D:], ((0, 0), (0, 0), (0, LANES - (ns * bs - 3 * D))))}
    for n in ROW_SHARDED:
        g = gathered[n]
        w[n] = g.reshape(g.shape[0], g.shape[1] * g.shape[2], g.shape[3])
    for n in ("mla_uq", "mla_ukv", "mlp_w1"):
        w[n] = gathered[n]
    return w


def _grad_layouts(g, n_fox_heads, ns):
    fox = jnp.stack([jnp.concatenate([a, b[:, :n_fox_heads]], axis=1) for a, b in zip(g["fox_qkv"], g["fox_f"])])
    L2, D, cols = fox.shape
    out = {"fox_in": jnp.transpose(fox.reshape(L2, D, ns, cols // ns), (0, 2, 1, 3))}
    for n in ROW_SHARDED:
        a = g[n]
        out[n] = a.reshape(a.shape[0], ns, a.shape[1] // ns, a.shape[2])
    for n in ("mla_uq", "mla_ukv", "mlp_w1"):
        out[n] = g[n]
    return out


def _natural_shard_grads(s, rq):
    return {
        "fox_w_in": s["fox_in"], "fox_w_out": s["fox_out"], "mla_w_out": s["mla_out"],
        "mlp_w1": s["mlp_w1"], "mlp_w2": s["mlp_w2"],
        "mla_w_dq": s["mla_down"][:, :, :rq],
        "mla_w_dkv": s["mla_down"][:, :, rq:rq + KV_RANK + ROPE_DIM],
        "mla_w_uq": jax.vmap(_uq_from_pairs)(s["mla_uq"]),
        "mla_w_ukv": jax.vmap(_ukv_from_pairs)(s["mla_ukv"]),
    }


def _small_grads(g, n_fox_heads):
    return {
        "norm_mix_g": jnp.concatenate(g["norm_mix_g"], axis=0),
        "norm_mlp_g": jnp.concatenate(g["norm_mlp_g"], axis=0),
        "final_norm_g": g["final_norm_g"][0],
        "fox_b_f": jnp.concatenate(g["fox_b"], axis=0)[:, :n_fox_heads],
        "mla_q_norm_g": jnp.concatenate(g["mla_gq"], axis=0),
        "mla_kv_norm_g": jnp.concatenate(g["mla_gkv"], axis=0),
    }


def _silu(c):
    return c * (1.0 / (1.0 + jnp.exp(-c)))


def _ada_fwd(c_all, ada_w, ada_b_cols):
    L, D, C = ada_w.shape
    Bg = c_all.shape[0]
    tc = _tile(C, 512)

    def body(c_ref, w_ref, b_ref, o_ref):
        ca = _silu(c_ref[...]).astype(BF16)
        o_ref[...] = jnp.dot(ca, w_ref[...].astype(BF16), preferred_element_type=F32) + b_ref[...]

    return pl.pallas_call(
        body, name="ada_fwd", grid=(L, C // tc),
        in_specs=[pl.BlockSpec((Bg, D), lambda l, j: (0, 0)), pl.BlockSpec((None, D, tc), lambda l, j: (l, 0, j)),
                  pl.BlockSpec((None, 1, tc), lambda l, j: (l, 0, j))],
        out_specs=pl.BlockSpec((None, Bg, tc), lambda l, j: (l, 0, j)),
        out_shape=jax.ShapeDtypeStruct((L, Bg, C), F32),
        compiler_params=_cparams(("parallel", "parallel")),
    )(c_all, ada_w, ada_b_cols)


def _ada_bwd(c_all, dmod_cols):
    L, Bg, C = dmod_cols.shape
    D = c_all.shape[1]
    tc = _tile(C, 512)

    def body(c_ref, d_ref, o_ref):
        ca = _silu(c_ref[...]).astype(BF16)
        o_ref[...] = _dot_tn(ca, d_ref[...].astype(BF16))

    return pl.pallas_call(
        body, name="ada_bwd", grid=(L, C // tc),
        in_specs=[pl.BlockSpec((Bg, D), lambda l, j: (0, 0)), pl.BlockSpec((None, Bg, tc), lambda l, j: (l, 0, j))],
        out_specs=pl.BlockSpec((None, D, tc), lambda l, j: (l, 0, j)),
        out_shape=jax.ShapeDtypeStruct((L, D, C), F32),
        compiler_params=_cparams(("parallel", "parallel")),
    )(c_all, dmod_cols)


def _adamw(w, g, m, v, *, name):
    shape = w.shape
    C = shape[-1]
    R = int(np.prod(shape[:-1])) if len(shape) > 1 else 1
    w2, g2, m2, v2 = (a.reshape(R, C) for a in (w, g, m, v))
    tr = _row_tile(R, C)

    def body(w_ref, g_ref, m_ref, v_ref, d_ref, nm_ref, nv_ref):
        gv = g_ref[...]
        mn = ADAM_B1 * m_ref[...] + (1.0 - ADAM_B1) * gv
        vn = ADAM_B2 * v_ref[...] + (1.0 - ADAM_B2) * jnp.square(gv)
        m_hat = mn / (1.0 - ADAM_B1 ** ADAM_STEP)
        v_hat = vn / (1.0 - ADAM_B2 ** ADAM_STEP)
        d_ref[...] = -ADAM_LR * (m_hat / (jnp.sqrt(v_hat) + ADAM_EPS) + ADAM_WD * w_ref[...])
        nm_ref[...] = mn
        nv_ref[...] = vn

    spec = pl.BlockSpec((tr, C), lambda i: (i, 0))
    out = pl.pallas_call(
        body, name=name, grid=(R // tr,), in_specs=[spec] * 4, out_specs=[spec] * 3,
        out_shape=[jax.ShapeDtypeStruct((R, C), F32)] * 3, compiler_params=_cparams(("parallel",)),
    )(w2, g2, m2, v2)
    return tuple(a.reshape(shape) for a in out)


def _sum_gathered(dm8, sm8):
    n_dev, Bl, R, D = dm8.shape
    Rs = sm8.shape[1]

    def body(dm_ref, sm_ref, ob_ref, os_ref):
        acc_b = jnp.zeros((R, D), F32)
        acc_s = jnp.zeros((Rs, D), F32)
        for d in range(n_dev):
            for b in range(Bl):
                acc_b = acc_b + dm_ref[d, b]
            acc_s = acc_s + sm_ref[d]
        ob_ref[...] = acc_b
        os_ref[...] = acc_s

    return pl.pallas_call(
        body, name="sum_gathered",
        out_shape=[jax.ShapeDtypeStruct((R, D), F32), jax.ShapeDtypeStruct((Rs, D), F32)],
        compiler_params=_cparams(None),
    )(dm8, sm8)


N_DEV = 8
N_CHIP = 4
ANY = pl.BlockSpec(memory_space=pl.ANY)


def _mesh_pos():
    return lax.axis_index("x"), lax.axis_index("y"), lax.axis_index("c")


def _all_gather8(block, *, name, in_vmem):
    R, W = block.shape

    def body(x_ref, out_ref, send_sems, recv_sems, local_sem):
        x, y, c = _mesh_pos()
        me, sibling = (x, y, c), (x, y, 1 - c)
        chips = [(1 - x, y), (x, 1 - y), (1 - x, 1 - y)]

        def slot(px, py, pc):
            return out_ref.at[4 * px + 2 * py + pc]

        def copy(k, blk, to, src=None):
            return pltpu.make_async_remote_copy(
                src_ref=slot(*blk) if src is None else src, dst_ref=slot(*blk),
                send_sem=send_sems.at[k], recv_sem=recv_sems.at[k], device_id=to, device_id_type=MESH_ID)

        mine = pltpu.make_async_copy(x_ref, slot(*me), local_sem)
        mine.start()
        first = [copy(0, me, sibling, src=x_ref)]
        first += [copy(1 + j, me, (*chip, c), src=x_ref) for j, chip in enumerate(chips)]
        for cp in first:
            cp.start()
        passed = [copy(4 + j, (*chip, c), sibling) for j, chip in enumerate(chips)]
        for j, chip in enumerate(chips):
            copy(1 + j, (*chip, c), me).wait_recv()
            passed[j].start()
        copy(0, sibling, me).wait_recv()
        for j, chip in enumerate(chips):
            copy(4 + j, (*chip, 1 - c), me).wait_recv()
        for cp in first + passed:
            cp.wait_send()
        mine.wait()

    space = pl.BlockSpec(memory_space=pltpu.VMEM) if in_vmem else ANY
    return pl.pallas_call(
        body, name=name, out_shape=jax.ShapeDtypeStruct((N_DEV, R, W), block.dtype),
        in_specs=[space], out_specs=space,
        scratch_shapes=[pltpu.SemaphoreType.DMA((7,)), pltpu.SemaphoreType.DMA((7,)), pltpu.SemaphoreType.DMA],
        compiler_params=pltpu.CompilerParams(vmem_limit_bytes=VMEM_LIMIT_V7X),
    )(block)


def _comm_call(body, arrays, out_shapes, n_sems, *, name):
    return pl.pallas_call(
        body, name=name, out_shape=out_shapes, in_specs=[ANY] * len(arrays), out_specs=[ANY] * len(out_shapes),
        scratch_shapes=[pltpu.SemaphoreType.DMA((n_sems,)), pltpu.SemaphoreType.DMA((n_sems,)),
                        pltpu.SemaphoreType.DMA((len(arrays),))],
    )(*arrays)


def _gather_weights(shards, *, name):
    n = len(shards)

    def body(*refs):
        xs, outs = refs[:n], refs[n:2 * n]
        send_sems, recv_sems, local_sems = refs[2 * n:]
        x, y, c = _mesh_pos()
        me, sibling = (x, y, c), (x, y, 1 - c)
        chips = [(1 - x, y), (x, 1 - y), (1 - x, 1 - y)]
        waits = []
        for i in range(n):
            nl = shards[i].shape[0] // 2
            own = xs[i].at[pl.ds(c * nl, nl)]

            def slot(px, py, pc, i=i, nl=nl):
                return outs[i].at[pl.ds(pc * nl, nl), 2 * px + py]

            def copy(k, blk, to, src=None, i=i, slot=slot):
                return pltpu.make_async_remote_copy(
                    src_ref=slot(*blk) if src is None else src, dst_ref=slot(*blk),
                    send_sem=send_sems.at[7 * i + k], recv_sem=recv_sems.at[7 * i + k], device_id=to,
                    device_id_type=MESH_ID)

            mine = pltpu.make_async_copy(own, slot(*me), local_sems.at[i])
            mine.start()
            first = [copy(0, me, sibling, src=own)]
            first += [copy(1 + j, me, (*chip, c), src=own) for j, chip in enumerate(chips)]
            for cp in first:
                cp.start()
            waits.append((copy, mine, first))
        for copy, mine, first in waits:
            passed = [copy(4 + j, (*chip, c), sibling) for j, chip in enumerate(chips)]
            for j, chip in enumerate(chips):
                copy(1 + j, (*chip, c), me).wait_recv()
                passed[j].start()
            copy(0, sibling, me).wait_recv()
            for j, chip in enumerate(chips):
                copy(4 + j, (*chip, 1 - c), me).wait_recv()
            for cp in first + passed:
                cp.wait_send()
            mine.wait()

    out_shapes = [jax.ShapeDtypeStruct((s.shape[0], N_CHIP) + s.shape[1:], s.dtype) for s in shards]
    return _comm_call(body, shards, out_shapes, 7 * n, name=name)


def _pair_exchange(gs, *, name):
    n = len(gs)

    def body(*refs):
        xs, outs = refs[:n], refs[n:2 * n]
        send_sems, recv_sems, _ = refs[2 * n:]
        x, y, c = _mesh_pos()
        copies = []
        for i in range(n):
            nl = gs[i].shape[0] // 2
            cp = pltpu.make_async_remote_copy(
                src_ref=xs[i].at[pl.ds((1 - c) * nl, nl)], dst_ref=outs[i], send_sem=send_sems.at[i],
                recv_sem=recv_sems.at[i], device_id=(x, y, 1 - c), device_id_type=MESH_ID)
            cp.start()
            copies.append(cp)
        for cp in copies:
            cp.wait()

    out_shapes = [jax.ShapeDtypeStruct((g.shape[0] // 2,) + g.shape[1:], g.dtype) for g in gs]
    return _comm_call(body, gs, out_shapes, n, name=name)


def _chip_exchange(ps, *, name):
    n = len(ps)

    def body(*refs):
        xs, outs = refs[:n], refs[n:2 * n]
        send_sems, recv_sems, local_sems = refs[2 * n:]
        x, y, c = _mesh_pos()
        k_me = 2 * x + y
        chips = [(1 - x, y), (x, 1 - y), (1 - x, 1 - y)]
        copies = []
        for i in range(n):
            nl = ps[i].shape[0]
            mine = pltpu.make_async_copy(xs[i].at[pl.ds(0, nl), k_me], outs[i].at[k_me], local_sems.at[i])
            mine.start()
            copies.append(mine)
            for j, (cx, cy) in enumerate(chips):
                cp = pltpu.make_async_remote_copy(
                    src_ref=xs[i].at[pl.ds(0, nl), 2 * cx + cy], dst_ref=outs[i].at[k_me],
                    send_sem=send_sems.at[3 * i + j], recv_sem=recv_sems.at[3 * i + j],
                    device_id=(cx, cy, c), device_id_type=MESH_ID)
                cp.start()
                copies.append(cp)
        for cp in copies:
            cp.wait()

    out_shapes = [jax.ShapeDtypeStruct((p.shape[1], p.shape[0]) + p.shape[2:], p.dtype) for p in ps]
    return _comm_call(body, ps, out_shapes, 3 * n, name=name)


def _pair_gather(ss, *, name):
    n = len(ss)

    def body(*refs):
        xs, outs = refs[:n], refs[n:2 * n]
        send_sems, recv_sems, local_sems = refs[2 * n:]
        x, y, c = _mesh_pos()
        copies = []
        for i in range(n):
            nl = ss[i].shape[0]
            dst = outs[i].at[pl.ds(c * nl, nl)]
            mine = pltpu.make_async_copy(xs[i], dst, local_sems.at[i])
            mine.start()
            cp = pltpu.make_async_remote_copy(src_ref=xs[i], dst_ref=dst, send_sem=send_sems.at[i],
                                              recv_sem=recv_sems.at[i], device_id=(x, y, 1 - c),
                                              device_id_type=MESH_ID)
            cp.start()
            copies += [mine, cp]
        for cp in copies:
            cp.wait()

    out_shapes = [jax.ShapeDtypeStruct((2 * s.shape[0],) + s.shape[1:], s.dtype) for s in ss]
    return _comm_call(body, ss, out_shapes, n, name=name)


def _row_tile(rows, cols):
    tr = rows
    while tr * cols > 256 * 1024 and tr % 16 == 0:
        tr //= 2
    return tr


def _pair_add(g, recv, c_idx, *, name):
    shape = recv.shape
    W = shape[-1]
    R = int(np.prod(shape[:-1]))
    tr = _row_tile(R, W)

    def body(c_ref, g_ref, r_ref, o_ref):
        o_ref[...] = (g_ref[...] + r_ref[...]).astype(BF16)

    grid_spec = pltpu.PrefetchScalarGridSpec(
        num_scalar_prefetch=1, grid=(R // tr,),
        in_specs=[pl.BlockSpec((None, tr, W), lambda i, c_ref: (c_ref[0], i, 0)),
                  pl.BlockSpec((tr, W), lambda i, c_ref: (i, 0))],
        out_specs=pl.BlockSpec((tr, W), lambda i, c_ref: (i, 0)))
    out = pl.pallas_call(
        body, name=name, grid_spec=grid_spec, out_shape=jax.ShapeDtypeStruct((R, W), BF16),
        compiler_params=_cparams(("parallel",)),
    )(c_idx, g.reshape(2, R, W), recv.reshape(R, W))
    return out.reshape(shape)


def _sum_pieces(pieces, *, name):
    n = pieces.shape[0]
    shape = pieces.shape[1:]
    W = shape[-1]
    R = int(np.prod(shape[:-1]))
    tr = _row_tile(R, W)

    def body(p_ref, o_ref):
        acc = p_ref[0].astype(F32)
        for k in range(1, n):
            acc = acc + p_ref[k].astype(F32)
        o_ref[...] = acc

    out = pl.pallas_call(
        body, name=name, grid=(R // tr,), in_specs=[pl.BlockSpec((n, tr, W), lambda i: (0, i, 0))],
        out_specs=pl.BlockSpec((tr, W), lambda i: (i, 0)), out_shape=jax.ShapeDtypeStruct((R, W), F32),
        compiler_params=_cparams(("parallel",)),
    )(pieces.reshape(n, R, W))
    return out.reshape(shape)


SMALL = ("norm_mix_g", "norm_mlp_g", "final_norm_g", "fox_b_f", "mla_q_norm_g", "mla_kv_norm_g")
WEIGHT_ORDER = ("ada_w", "ada_b", "norm_mix_g", "norm_mlp_g", "fox_w_in", "fox_b_f", "fox_w_out", "mla_w_dq",
                "mla_q_norm_g", "mla_w_uq", "mla_w_dkv", "mla_kv_norm_g", "mla_w_ukv", "mla_w_out", "mlp_w1",
                "mlp_w2", "final_norm_g")


def _small_rows(vals, D):
    rows = [vals["norm_mix_g"], vals["norm_mlp_g"], vals["final_norm_g"][None, :]]
    for n in ("fox_b_f", "mla_q_norm_g", "mla_kv_norm_g"):
        flat = vals[n].reshape(-1)
        assert flat.shape[0] <= D
        rows.append(jnp.pad(flat, (0, D - flat.shape[0]))[None, :])
    return jnp.concatenate(rows, axis=0)


def _small_unrows(rows, shapes):
    L = shapes["norm_mix_g"][0]
    out = {"norm_mix_g": rows[0:L], "norm_mlp_g": rows[L:2 * L], "final_norm_g": rows[2 * L]}
    for k, n in enumerate(("fox_b_f", "mla_q_norm_g", "mla_kv_norm_g")):
        size = int(np.prod(shapes[n]))
        out[n] = rows[2 * L + 1 + k, :size].reshape(shapes[n])
    return out


def kernel(x, c, positions, ada_w, ada_b, norm_mix_g, norm_mlp_g, fox_w_in, fox_b_f, fox_w_out, mla_w_dq, mla_q_norm_g, mla_w_uq, mla_w_dkv, mla_kv_norm_g, mla_w_ukv, mla_w_out, mlp_w1, mlp_w2, final_norm_g, loss_target, m_ada_w, m_ada_b, m_norm_mix_g, m_norm_mlp_g, m_fox_w_in, m_fox_b_f, m_fox_w_out, m_mla_w_dq, m_mla_q_norm_g, m_mla_w_uq, m_mla_w_dkv, m_mla_kv_norm_g, m_mla_w_ukv, m_mla_w_out, m_mlp_w1, m_mlp_w2, m_final_norm_g, v_ada_w, v_ada_b, v_norm_mix_g, v_norm_mlp_g, v_fox_w_in, v_fox_b_f, v_fox_w_out, v_mla_w_dq, v_mla_q_norm_g, v_mla_w_uq, v_mla_w_dkv, v_mla_kv_norm_g, v_mla_w_ukv, v_mla_w_out, v_mlp_w1, v_mlp_w2, v_final_norm_g):
    args = dict(locals())
    wts = {n: args[n] for n in WEIGHT_ORDER}
    mom = {n: args["m_" + n] for n in WEIGHT_ORDER}
    var = {n: args["v_" + n] for n in WEIGHT_ORDER}
    Bl, S, D = x.shape
    T = Bl * S
    L = ada_w.shape[0]
    C = ada_w.shape[2]
    mx, my, mc = _mesh_pos()
    chip = 2 * mx + my
    dev = 4 * mx + 2 * my + mc
    c_idx = jnp.reshape(mc, (1,)).astype(jnp.int32)

    shards = _shard_layouts(wts)
    gathered = dict(zip(GATHERED, _gather_weights([shards[n] for n in GATHERED], name="gather_weights")))
    small = {n: wts[n] for n in SMALL}
    L2, q_cols = mla_q_norm_g.shape

    c_pad = jnp.concatenate([c, jnp.pad(mla_q_norm_g, ((0, 8 - Bl - L2), (0, D - q_cols)))], axis=0)
    c8 = _all_gather8(c_pad, name="gather_c", in_vmem=True)
    c_all = c8[:, :Bl].reshape(N_DEV * Bl, D)
    qg4 = c8.reshape(N_CHIP, 2, 8, D)[:, 0, Bl:Bl + L2, :q_cols]
    small["mla_q_norm_g"] = jnp.transpose(qg4, (1, 0, 2)).reshape(L2, N_CHIP * q_cols)
    ada_b_cols = lax.dynamic_slice_in_dim(ada_b, chip * C, C, axis=1)[:, None, :]
    mod_cols = _ada_fwd(c_all, ada_w, ada_b_cols)
    mod8 = _all_gather8(mod_cols.reshape(L * N_DEV * Bl, C), name="gather_mod", in_vmem=True)
    mod4 = mod8.reshape(N_CHIP, 2, L, N_DEV * Bl, C)[:, 0]
    mod_me = lax.dynamic_slice_in_dim(mod4, dev * Bl, Bl, axis=2)
    mod = jnp.transpose(mod_me, (1, 2, 0, 3)).reshape(L, Bl, 6, D)
    mod = jnp.transpose(mod, (0, 2, 1, 3))[:, :, :, None, :]

    w = _full_layouts(gathered, D)
    w.update(_small_layouts(small))
    half = ROPE_DIM // 2
    inv_freq = ROPE_THETA ** (-jnp.arange(0, ROPE_DIM, 2, dtype=F32) / ROPE_DIM)
    lane = np.arange(LANES)
    inv_freq_row = jnp.tile(inv_freq, LANES // half)[None, :]
    sign_row = jnp.asarray(np.where(lane < 2 * ROPE_DIM, np.where(lane % ROPE_DIM < half, -1.0, 1.0), 0.0), F32)[None, :]
    pos_f = positions.astype(F32).reshape(T, 1)
    loss_row, grad_x, dmod, g = _local_step(x.reshape(T, D), loss_target.reshape(T, D), pos_f, inv_freq_row, sign_row,
                                            mod, w, S=S)
    g_small = _small_grads(g, fox_b_f.shape[1])

    Rs = -(-(2 * L + 5) // 8) * 8
    srows = jnp.concatenate([_small_rows(g_small, D), jnp.pad(loss_row, ((0, 0), (0, D - LANES)))], axis=0)
    srows = jnp.pad(srows, ((0, Rs - srows.shape[0]), (0, 0)))
    drows = jnp.transpose(dmod[:, :, :, 0, :], (2, 0, 1, 3)).reshape(Bl * L * 6, D)
    both8 = _all_gather8(jnp.concatenate([drows, srows], axis=0), name="gather_small", in_vmem=True)
    dm8 = both8[:, :Bl * L * 6].reshape(N_DEV, Bl, L * 6, D)
    sm8 = both8[:, Bl * L * 6:]
    adb_rows, small_sum = _sum_gathered(dm8, sm8)
    grad_ada_b = adb_rows.reshape(L, 6 * D)
    loss = small_sum[2 * L + 4, 0]
    small_shapes = {n: (wts[n].shape if n != "mla_q_norm_g" else (wts[n].shape[0], N_CHIP * q_cols)) for n in SMALL}
    gs = _small_unrows(small_sum, small_shapes)
    gs["mla_q_norm_g"] = lax.dynamic_slice_in_dim(gs["mla_q_norm_g"], chip * q_cols, q_cols, axis=1)

    dmod16 = jnp.transpose(dm8.reshape(N_DEV, Bl, L, 6 * D), (2, 0, 1, 3)).reshape(L, N_DEV * Bl, 6 * D)
    dmod_cols = lax.dynamic_slice_in_dim(dmod16, chip * C, C, axis=2)
    grad_ada_w = _ada_bwd(c_all, dmod_cols)

    gl = _grad_layouts(g, fox_b_f.shape[1], N_CHIP)
    big = [gl[n] for n in GATHERED]
    from_sibling = _pair_exchange(big, name="grad_pair_exchange")
    pair_sums = [_pair_add(a, r, c_idx, name=f"grad_pair_add_{n}") for n, a, r in zip(GATHERED, big, from_sibling)]
    pieces = _chip_exchange(pair_sums, name="grad_chip_exchange")
    halves = [_sum_pieces(p, name=f"grad_sum_pieces_{n}") for n, p in zip(GATHERED, pieces)]
    reduced = dict(zip(GATHERED, _pair_gather(halves, name="grad_pair_gather")))
    grads = _natural_shard_grads(reduced, mla_w_dq.shape[-1])
    grads.update(gs)
    grads["ada_w"] = grad_ada_w
    grads["ada_b"] = grad_ada_b

    delta, new_m, new_v = {}, {}, {}
    for n in ("ada_w",) + tuple(k for k, _ in BIG_WEIGHTS) + ("ada_b",):
        delta[n], new_m[n], new_v[n] = _adamw(wts[n], grads[n], mom[n], var[n], name=f"adamw_{n}")
    shard_small_shapes = {n: wts[n].shape for n in SMALL}
    packs = [jnp.pad(_small_rows({n: src[n] for n in SMALL}, D), ((0, Rs - 2 * L - 4), (0, 0)))
             for src in (wts, grads, mom, var)]
    for dst, rows in zip((delta, new_m, new_v), _adamw(*packs, name="adamw_small")):
        dst.update(_small_unrows(rows, shard_small_shapes))

    return (loss, grad_x.reshape(Bl, S, D), *[grads[n] for n in WEIGHT_ORDER], *[delta[n] for n in WEIGHT_ORDER],
            *[new_m[n] for n in WEIGHT_ORDER], *[new_v[n] for n in WEIGHT_ORDER])
```

```python
import functools

import numpy as np
import jax
import jax.numpy as jnp
from jax import lax
from jax.experimental import pallas as pl
from jax.experimental.pallas import tpu as pltpu

F32 = jnp.float32
BF16 = jnp.bfloat16
MESH_ID = pl.DeviceIdType.MESH

NORM_EPS = 1e-6
ROPE_THETA = 10000.0
HEAD_DIM = 64
ROPE_DIM = 32
KV_RANK = 128
FOX_EXTRA = 6
PAIR_Q = 256
PAIR_KV = 384
LANES = 128
ADAM_LR = 0.001
ADAM_B1 = 0.9
ADAM_B2 = 0.999
ADAM_EPS = 1e-08
ADAM_WD = 0.01
ADAM_STEP = 10
VMEM_LIMIT_V7X = 48 * 1024 * 1024
NEG_BIG = -1e30
ATTN_UNROLL = 2

BIG_WEIGHTS = (("fox_w_in", 2), ("fox_w_out", 1), ("mla_w_dq", 1), ("mla_w_uq", 2), ("mla_w_dkv", 1),
               ("mla_w_ukv", 2), ("mla_w_out", 1), ("mlp_w1", 2), ("mlp_w2", 1))


def _cparams(sem=None):
    return pltpu.CompilerParams(dimension_semantics=sem, vmem_limit_bytes=VMEM_LIMIT_V7X)


def _tile(n, want):
    if n <= want:
        return n
    for t in range(want - want % LANES, 0, -LANES):
        if n % t == 0:
            return t
    raise ValueError((n, want))


def _mm(a, b, mode, *, name, out_dtypes=(F32,), epilogue=None, extras=(), rowvecs=(), tables=(),
        seq=None, a_off=0, a_sz=None, b_layer=None, out_stack=None, out_split=0, tm=512, tn=512, tk=1024):
    b_rows, b_cols = b.shape[-2], b.shape[-1]
    n_split = b.shape[1] if b.ndim == 4 else 1
    if mode == "nn":
        M, K, N = a.shape[0], b_rows, b_cols * n_split
    elif mode == "nt":
        M, K, N = a.shape[0], b_cols * n_split, b_rows
    else:
        assert b.ndim == 2
        K, N = b.shape
        M = a.shape[1] - a_off if a_sz is None else a_sz
    if a_sz is not None and mode != "tn":
        assert a_sz == K
    tm = _tile(seq if rowvecs else M, tm)
    n_piece = N // max(out_split, n_split if mode == "nn" else 1, 1)
    tn = _tile(n_piece, tn)
    tk = _tile(K // (n_split if mode == "nt" else 1), 512 if mode == "tn" else tk)
    nk = K // tk
    ne, nr, nt_ = len(extras), len(rowvecs), len(tables)
    no = len(out_dtypes)

    if mode == "tn":
        assert a_off % tm == 0
        a_spec = pl.BlockSpec((tk, tm), lambda i, j, k: (k, i + a_off // tm))
        dims = (((0,), (0,)), ((), ()))
    else:
        assert a_off % tk == 0
        a_spec = pl.BlockSpec((tm, tk), lambda i, j, k: (i, k + a_off // tk))
        dims = (((1,), (0,)), ((), ())) if mode == "nn" else (((1,), (1,)), ((), ()))
    lead = () if b.ndim == 2 else (b_layer,)
    sq = (None,) * (b.ndim - 2)
    if mode == "nt":
        kb = b_cols // tk
        if b.ndim == 4:
            b_spec = pl.BlockSpec(sq + (tn, tk), lambda i, j, k: lead + (k // kb, j, k % kb))
        else:
            b_spec = pl.BlockSpec(sq + (tn, tk), lambda i, j, k: lead + (j, k))
    else:
        nb = b_cols // tn
        if b.ndim == 4:
            b_spec = pl.BlockSpec(sq + (tk, tn), lambda i, j, k: lead + (j // nb, k, j % nb))
        else:
            b_spec = pl.BlockSpec(sq + (tk, tn), lambda i, j, k: lead + (k, j))
    in_specs = [a_spec, b_spec]
    in_specs += [pl.BlockSpec((tm, tn), lambda i, j, k: (i, j)) for _ in extras]
    if rowvecs:
        assert seq % tm == 0
        per = seq // tm
        in_specs += [pl.BlockSpec((None, 1, tn), lambda i, j, k: (i // per, 0, j)) for _ in rowvecs]
    in_specs += [pl.BlockSpec((tm, LANES), lambda i, j, k: (i, 0)) for _ in tables]
    operands = [a, b, *extras, *rowvecs, *tables]
    aliases = {}
    if out_stack is None:
        out_specs = [pl.BlockSpec((tm, tn), lambda i, j, k: (i, j)) for _ in out_dtypes]
        out_shape = [jax.ShapeDtypeStruct((M, N), d) for d in out_dtypes]
    else:
        prev, layer, n_layers = out_stack
        assert no == 1
        if out_split:
            ob = n_piece // tn
            out_specs = [pl.BlockSpec((None, None, tm, tn), lambda i, j, k: (layer, j // ob, i, j % ob))]
            out_shape = [jax.ShapeDtypeStruct((n_layers, out_split, M, n_piece), out_dtypes[0])]
        else:
            out_specs = [pl.BlockSpec((None, tm, tn), lambda i, j, k: (layer, i, j))]
            out_shape = [jax.ShapeDtypeStruct((n_layers, M, N), out_dtypes[0])]
        if prev is not None:
            in_specs.append(pl.BlockSpec(memory_space=pl.ANY))
            aliases = {len(operands): 0}
            operands.append(prev)
    n_in = len(operands)

    def body(*refs):
        a_ref, b_ref = refs[0], refs[1]
        side = refs[2:2 + ne + nr + nt_]
        outs = refs[n_in:n_in + no]

        def finish(acc):
            res = (acc,) if epilogue is None else epilogue(acc, *[r[...] for r in side])
            for o_ref, r in zip(outs, res):
                o_ref[...] = r.astype(o_ref.dtype)

        part = lax.dot_general(a_ref[...].astype(BF16), b_ref[...].astype(BF16), dims,
                               preferred_element_type=F32)
        if nk == 1:
            finish(part)
        else:
            acc_ref = refs[-1]
            k = pl.program_id(2)

            @pl.when(k == 0)
            def _():
                acc_ref[...] = part

            @pl.when(k > 0)
            def _():
                acc_ref[...] += part

            @pl.when(k == nk - 1)
            def _():
                finish(acc_ref[...])

    res = pl.pallas_call(
        body, name=name, grid=(M // tm, N // tn, nk), in_specs=in_specs, out_specs=out_specs,
        out_shape=out_shape, scratch_shapes=[pltpu.VMEM((tm, tn), F32)] if nk > 1 else [],
        input_output_aliases=aliases,
        compiler_params=_cparams(("parallel", "parallel", "arbitrary")),
    )(*operands)
    return res[0] if no == 1 else tuple(res)


def _rope128(x, cos_t, sin_s):
    lane = lax.broadcasted_iota(jnp.int32, x.shape, 1)
    first = (lane % ROPE_DIM) < (ROPE_DIM // 2)
    swapped = jnp.where(first, pltpu.roll(x, LANES - ROPE_DIM // 2, 1), pltpu.roll(x, ROPE_DIM // 2, 1))
    return x * cos_t + swapped * sin_s


def _rope_pairs(acc, cos_t, sin_s, sign):
    parts = []
    for p in range(acc.shape[1] // PAIR_Q):
        parts.append(acc[:, p * PAIR_Q:p * PAIR_Q + LANES])
        parts.append(_rope128(acc[:, p * PAIR_Q + LANES:(p + 1) * PAIR_Q], cos_t, sign * sin_s))
    return jnp.concatenate(parts, axis=1)


def _rope_tables(pos_f, inv_freq_row, sign_row):
    T = pos_f.shape[0]
    tt = _tile(T, 512)

    def body(p_ref, f_ref, s_ref, cos_ref, sin_ref):
        ang = p_ref[...] * f_ref[...]
        cos_ref[...] = jnp.cos(ang)
        sin_ref[...] = jnp.sin(ang) * s_ref[...]

    return pl.pallas_call(
        body, name="rope_tables", grid=(T // tt,),
        in_specs=[pl.BlockSpec((tt, 1), lambda i: (i, 0)), pl.BlockSpec((1, LANES), lambda i: (0, 0)),
                  pl.BlockSpec((1, LANES), lambda i: (0, 0))],
        out_specs=[pl.BlockSpec((tt, LANES), lambda i: (i, 0))] * 2,
        out_shape=[jax.ShapeDtypeStruct((T, LANES), F32)] * 2,
        compiler_params=_cparams(("parallel",)),
    )(pos_f, inv_freq_row, sign_row)


def _unrope(dqx, cos_t, sin_s):
    T, W = dqx.shape
    tt = _tile(T, 512)

    def body(d_ref, c_ref, s_ref, o_ref):
        o_ref[...] = _rope_pairs(d_ref[...].astype(F32), c_ref[...], s_ref[...], -1.0).astype(BF16)

    return pl.pallas_call(
        body, name="mla_unrope", grid=(T // tt,),
        in_specs=[pl.BlockSpec((tt, W), lambda i: (i, 0)), pl.BlockSpec((tt, LANES), lambda i: (i, 0)),
                  pl.BlockSpec((tt, LANES), lambda i: (i, 0))],
        out_specs=pl.BlockSpec((tt, W), lambda i: (i, 0)),
        out_shape=jax.ShapeDtypeStruct((T, W), BF16),
        compiler_params=_cparams(("parallel",)),
    )(dqx, cos_t, sin_s)


def _row_specs(tt, D, per, n):
    return [pl.BlockSpec((None, 1, D), lambda i: (i // per, 0, 0)) for _ in range(n)]


def _norm_mod(x, gain, sc, sh, *, S, name):
    T, D = x.shape
    tt = _tile(S, 512)
    per = S // tt

    def body(x_ref, g_ref, sc_ref, sh_ref, h_ref):
        xv = x_ref[...]
        r = lax.rsqrt(jnp.mean(xv * xv, axis=-1, keepdims=True) + NORM_EPS)
        h_ref[...] = ((xv * r) * g_ref[...] * (1.0 + sc_ref[...]) + sh_ref[...]).astype(BF16)

    return pl.pallas_call(
        body, name=name, grid=(T // tt,),
        in_specs=[pl.BlockSpec((tt, D), lambda i: (i, 0)), pl.BlockSpec((1, D), lambda i: (0, 0))]
        + _row_specs(tt, D, per, 2),
        out_specs=pl.BlockSpec((tt, D), lambda i: (i, 0)),
        out_shape=jax.ShapeDtypeStruct((T, D), BF16),
        compiler_params=_cparams(("parallel",)),
    )(x, gain, sc, sh)


def _norm_mod_bwd(x, dh, dres, gain, sc, *, S, name):
    T, D = x.shape
    B = T // S
    tt = _tile(S, 512)
    per = S // tt

    def body(x_ref, dh_ref, dres_ref, g_ref, sc_ref, dx_ref, dsh_ref, dsc_ref, dg_ref):
        i = pl.program_id(0)
        xv = x_ref[...]
        dhv = dh_ref[...].astype(F32)
        r = lax.rsqrt(jnp.mean(xv * xv, axis=-1, keepdims=True) + NORM_EPS)
        n = xv * r
        g = g_ref[...]
        one_sc = 1.0 + sc_ref[...]
        dn = dhv * (g * one_sc)
        dx_ref[...] = dres_ref[...] + r * (dn - n * jnp.mean(dn * n, axis=-1, keepdims=True))
        dhn = dhv * n

        @pl.when(i % per == 0)
        def _():
            dsh_ref[...] = jnp.zeros_like(dsh_ref)
            dsc_ref[...] = jnp.zeros_like(dsc_ref)

        @pl.when(i == 0)
        def _():
            dg_ref[...] = jnp.zeros_like(dg_ref)

        dsh_ref[...] += jnp.sum(dhv, axis=0, keepdims=True)
        dsc_ref[...] += jnp.sum(dhn, axis=0, keepdims=True) * g
        dg_ref[...] += jnp.sum(dhn, axis=0, keepdims=True) * one_sc

    return pl.pallas_call(
        body, name=name, grid=(T // tt,),
        in_specs=[pl.BlockSpec((tt, D), lambda i: (i, 0))] * 3 + [pl.BlockSpec((1, D), lambda i: (0, 0))]
        + _row_specs(tt, D, per, 1),
        out_specs=[pl.BlockSpec((tt, D), lambda i: (i, 0))] + _row_specs(tt, D, per, 2)
        + [pl.BlockSpec((1, D), lambda i: (0, 0))],
        out_shape=[jax.ShapeDtypeStruct((T, D), F32), jax.ShapeDtypeStruct((B, 1, D), F32),
                   jax.ShapeDtypeStruct((B, 1, D), F32), jax.ShapeDtypeStruct((1, D), F32)],
        compiler_params=_cparams(("arbitrary",)),
    )(x, dh, dres, gain, sc)


def _gate_bwd(dx, y, g, *, S, name):
    T, D = dx.shape
    B = T // S
    tt = _tile(S, 512)
    per = S // tt

    def body(dx_ref, y_ref, g_ref, dy_ref, dg_ref):
        i = pl.program_id(0)
        dxv = dx_ref[...]
        dy_ref[...] = (dxv * g_ref[...]).astype(BF16)

        @pl.when(i % per == 0)
        def _():
            dg_ref[...] = jnp.zeros_like(dg_ref)

        dg_ref[...] += jnp.sum(dxv * y_ref[...], axis=0, keepdims=True)

    return pl.pallas_call(
        body, name=name, grid=(T // tt,),
        in_specs=[pl.BlockSpec((tt, D), lambda i: (i, 0))] * 2 + _row_specs(tt, D, per, 1),
        out_specs=[pl.BlockSpec((tt, D), lambda i: (i, 0))] + _row_specs(tt, D, per, 1),
        out_shape=[jax.ShapeDtypeStruct((T, D), BF16), jax.ShapeDtypeStruct((B, 1, D), F32)],
        compiler_params=_cparams(("arbitrary",)),
    )(dx, y, g)


def _final_loss(x, target, gain):
    T, D = x.shape
    tt = _tile(T, 512)

    def body(x_ref, t_ref, g_ref, dx_ref, dg_ref, loss_ref):
        i = pl.program_id(0)
        xv = x_ref[...]
        r = lax.rsqrt(jnp.mean(xv * xv, axis=-1, keepdims=True) + NORM_EPS)
        n = xv * r
        g = g_ref[...]
        err = n * g - t_ref[...]
        dy = err * (1.0 / D)
        dn = dy * g
        dx_ref[...] = r * (dn - n * jnp.mean(dn * n, axis=-1, keepdims=True))

        @pl.when(i == 0)
        def _():
            dg_ref[...] = jnp.zeros_like(dg_ref)
            loss_ref[...] = jnp.zeros_like(loss_ref)

        dg_ref[...] += jnp.sum(dy * n, axis=0, keepdims=True)
        loss_ref[...] += jnp.sum(jnp.sum(err * err, axis=-1, keepdims=True), axis=0, keepdims=True) * (0.5 / D)

    return pl.pallas_call(
        body, name="final_loss", grid=(T // tt,),
        in_specs=[pl.BlockSpec((tt, D), lambda i: (i, 0))] * 2 + [pl.BlockSpec((1, D), lambda i: (0, 0))],
        out_specs=[pl.BlockSpec((tt, D), lambda i: (i, 0)), pl.BlockSpec((1, D), lambda i: (0, 0)),
                   pl.BlockSpec((1, LANES), lambda i: (0, 0))],
        out_shape=[jax.ShapeDtypeStruct((T, D), F32), jax.ShapeDtypeStruct((1, D), F32),
                   jax.ShapeDtypeStruct((1, LANES), F32)],
        compiler_params=_cparams(("arbitrary",)),
    )(x, target, gain)


def _head_masks(ew):
    lane = lax.broadcasted_iota(jnp.int32, (1, PAIR_Q), 1)
    m0 = (lane < HEAD_DIM) | ((lane >= LANES) & (lane < LANES + ew))
    m1 = ((lane >= HEAD_DIM) & (lane < LANES)) | ((lane >= LANES + ew) & (lane < LANES + 2 * ew))
    return m0, m1


def _dot_nt(a, b):
    return lax.dot_general(a, b, (((1,), (1,)), ((), ())), preferred_element_type=F32)


def _dot_tn(a, b):
    return lax.dot_general(a, b, (((0,), (0,)), ((), ())), preferred_element_type=F32)


def _lane_halves(x, op):
    acc = x[:, 0:LANES]
    for g in range(1, x.shape[1] // LANES):
        acc = op(acc, x[:, g * LANES:(g + 1) * LANES])
    return acc


def _head_rows(cols_lane_replicated):
    t = cols_lane_replicated.T
    sub = lax.broadcasted_iota(jnp.int32, (8, t.shape[1]), 0)
    return jnp.where(sub == 1, t[HEAD_DIM:HEAD_DIM + 8], t[0:8])


def _attn_fwd(qx, kvx, *, S, scale, ew, name):
    T = qx.shape[0]
    P = qx.shape[1] // PAIR_Q
    B = T // S
    tq = _tile(S, 256)
    nq = S // tq
    assert nq % ATTN_UNROLL == 0

    def body(q_ref, kv_ref, o_ref, lse_ref, m_sc, l_sc, acc_sc):
        qi = pl.program_id(2)
        q = q_ref[...]
        masks = _head_masks(ew)
        qh = [jnp.where(m, q, jnp.zeros_like(q)) for m in masks]

        def logits(h, k, kj):
            s = _dot_nt(qh[h], k)
            if scale != 1.0:
                s = s * scale
            row = lax.broadcasted_iota(jnp.int32, s.shape, 0)
            col = lax.broadcasted_iota(jnp.int32, s.shape, 1)
            return jnp.where(col - row <= (qi - kj) * tq, s, NEG_BIG)

        def sweep(step):
            def loop_body(t, carry):
                for u in range(ATTN_UNROLL):
                    step(t * ATTN_UNROLL + u)
                return carry

            lax.fori_loop(0, (qi + ATTN_UNROLL) // ATTN_UNROLL, loop_body, 0)

        def max_step(kj):
            k = kv_ref[pl.ds(pl.multiple_of(kj * tq, tq), tq), 0:PAIR_Q]
            for h in range(2):
                m_sc[h] = jnp.maximum(m_sc[h], _lane_halves(logits(h, k, kj), jnp.maximum))

        def sum_step(kj):
            rows = pl.ds(pl.multiple_of(kj * tq, tq), tq)
            k = kv_ref[rows, 0:PAIR_Q]
            v = kv_ref[rows, PAIR_Q:PAIR_KV]
            for h in range(2):
                s = logits(h, k, kj)
                m = m_sc[h]
                p = jnp.concatenate([jnp.exp(s[:, g * LANES:(g + 1) * LANES] - m) for g in range(tq // LANES)], axis=1)
                l_sc[h] += _lane_halves(p, jnp.add)
                acc_sc[h] += jnp.dot(p.astype(BF16), v, preferred_element_type=F32)

        m_sc[...] = jnp.full(m_sc.shape, NEG_BIG, F32)
        sweep(max_step)
        for h in range(2):
            m_sc[h] = jnp.broadcast_to(jnp.max(m_sc[h], axis=1, keepdims=True), (tq, LANES))
        l_sc[...] = jnp.zeros_like(l_sc)
        acc_sc[...] = jnp.zeros_like(acc_sc)
        sweep(sum_step)
        lane = lax.broadcasted_iota(jnp.int32, (tq, LANES), 1)
        lo = lane < HEAD_DIM
        l = [jnp.sum(l_sc[h], axis=1, keepdims=True) for h in range(2)]
        o_ref[...] = jnp.where(lo, acc_sc[0] / l[0], acc_sc[1] / l[1]).astype(BF16)
        lse_ref[...] = _head_rows(jnp.where(lo, m_sc[0] + jnp.log(l[0]), m_sc[1] + jnp.log(l[1])))

    return pl.pallas_call(
        body, name=name, grid=(B, P, nq),
        in_specs=[pl.BlockSpec((tq, PAIR_Q), lambda b, p, i: (b * nq + i, p)),
                  pl.BlockSpec((S, PAIR_KV), lambda b, p, i: (b, p))],
        out_specs=[pl.BlockSpec((tq, LANES), lambda b, p, i: (b * nq + i, p)),
                   pl.BlockSpec((None, None, 8, tq), lambda b, p, i: (b * nq + i, p, 0, 0))],
        out_shape=[jax.ShapeDtypeStruct((T, P * LANES), BF16), jax.ShapeDtypeStruct((T // tq, P, 8, tq), F32)],
        scratch_shapes=[pltpu.VMEM((2, tq, LANES), F32)] * 3,
        compiler_params=_cparams(("parallel", "parallel", "arbitrary")),
    )(qx, kvx)


def _attn_bwd(qx, kvx, o, lse, do, *, S, scale, ew, name, bias_grad=False):
    T = qx.shape[0]
    P = qx.shape[1] // PAIR_Q
    B = T // S
    tq = _tile(S, 256)
    nq = S // tq
    assert nq % ATTN_UNROLL == 0

    def body(q_ref, kv_ref, o_ref, lse_ref, do_ref, dq_ref, dkv_ref, *rest):
        kj = pl.program_id(2)
        if bias_grad:
            csum_ref, rsum_ref, dq_sc, delta_sc, dk_sc, dv_sc, cs_sc = rest
            cs_sc[...] = jnp.zeros_like(cs_sc)

            @pl.when(kj == 0)
            def _():
                rsum_ref[...] = jnp.zeros_like(rsum_ref)
        else:
            dq_sc, delta_sc, dk_sc, dv_sc = rest
        masks = _head_masks(ew)
        lane = lax.broadcasted_iota(jnp.int32, (tq, LANES), 1)
        lo = lane < HEAD_DIM
        vmask = [lo, jnp.logical_not(lo)]

        @pl.when(kj == 0)
        def _():
            dq_sc[...] = jnp.zeros_like(dq_sc)
            for c in range(nq):
                rows = pl.ds(c * tq, tq)
                x = do_ref[rows, :].astype(F32) * o_ref[rows, :].astype(F32)
                r0 = jnp.sum(jnp.where(lo, x, 0.0), axis=1, keepdims=True)
                r1 = jnp.sum(jnp.where(lo, 0.0, x), axis=1, keepdims=True)
                delta_sc[c] = _head_rows(jnp.where(lo, r0, r1))

        k = kv_ref[:, 0:PAIR_Q]
        v = kv_ref[:, PAIR_Q:PAIR_KV]
        kh = [jnp.where(m, k, jnp.zeros_like(k)) for m in masks]
        vh = [jnp.where(m, v, jnp.zeros_like(v)) for m in vmask]
        dk_sc[...] = jnp.zeros_like(dk_sc)
        dv_sc[...] = jnp.zeros_like(dv_sc)

        def step(qi):
            rows = pl.ds(pl.multiple_of(qi * tq, tq), tq)
            q = q_ref[rows, :]
            dov = do_ref[rows, :]
            lse8 = lse_ref[qi]
            dl8 = delta_sc[qi]
            for h in range(2):
                st = _dot_nt(kh[h], q)
                if scale != 1.0:
                    st = st * scale
                key = lax.broadcasted_iota(jnp.int32, st.shape, 0)
                qry = lax.broadcasted_iota(jnp.int32, st.shape, 1)
                st = jnp.where(key - qry <= (qi - kj) * tq, st, NEG_BIG)
                pt = jnp.exp(st - lse8[h:h + 1, :])
                dpt = _dot_nt(vh[h], dov)
                dst = pt * (dpt - dl8[h:h + 1, :])
                if bias_grad:
                    cs_sc[h] += _lane_halves(dst, jnp.add)
                    rsum_ref[qi, h:h + 1, :] += jnp.sum(dst, axis=0, keepdims=True)
                if scale != 1.0:
                    dst = dst * scale
                ptb = pt.astype(BF16)
                dstb = dst.astype(BF16)
                dv_sc[h] += jnp.dot(ptb, dov, preferred_element_type=F32)
                dk_sc[h] += jnp.dot(dstb, q, preferred_element_type=F32)
                dq_sc[rows, :] += _dot_tn(dstb, kh[h])

        def loop_body(t, carry):
            for u in range(ATTN_UNROLL):
                step(t * ATTN_UNROLL + u)
            return carry

        lax.fori_loop(kj // ATTN_UNROLL, nq // ATTN_UNROLL, loop_body, 0)
        dkv_ref[:, 0:PAIR_Q] = (jnp.where(masks[0], dk_sc[0], 0.0) + jnp.where(masks[1], dk_sc[1], 0.0)).astype(BF16)
        dkv_ref[:, PAIR_Q:PAIR_KV] = jnp.where(lo, dv_sc[0], dv_sc[1]).astype(BF16)
        if bias_grad:
            csum_ref[...] = jnp.where(lo, jnp.sum(cs_sc[0], axis=1, keepdims=True),
                                      jnp.sum(cs_sc[1], axis=1, keepdims=True))

        @pl.when(kj == nq - 1)
        def _():
            dq_ref[...] = dq_sc[...].astype(BF16)

    rows_spec = pl.BlockSpec((nq, None, 8, tq), lambda b, p, j: (b, p, 0, 0))
    out_specs = [pl.BlockSpec((S, PAIR_Q), lambda b, p, j: (b, p)),
                 pl.BlockSpec((tq, PAIR_KV), lambda b, p, j: (b * nq + j, p))]
    out_shape = [jax.ShapeDtypeStruct((T, P * PAIR_Q), BF16), jax.ShapeDtypeStruct((T, P * PAIR_KV), BF16)]
    scratch = [pltpu.VMEM((S, PAIR_Q), F32), pltpu.VMEM((nq, 8, tq), F32),
               pltpu.VMEM((2, tq, PAIR_Q), F32), pltpu.VMEM((2, tq, LANES), F32)]
    if bias_grad:
        out_specs += [pl.BlockSpec((tq, LANES), lambda b, p, j: (b * nq + j, p)), rows_spec]
        out_shape += [jax.ShapeDtypeStruct((T, P * LANES), F32), jax.ShapeDtypeStruct((T // tq, P, 8, tq), F32)]
        scratch.append(pltpu.VMEM((2, tq, LANES), F32))
    return pl.pallas_call(
        body, name=name, grid=(B, P, nq),
        in_specs=[pl.BlockSpec((S, PAIR_Q), lambda b, p, j: (b, p)),
                  pl.BlockSpec((tq, PAIR_KV), lambda b, p, j: (b * nq + j, p)),
                  pl.BlockSpec((S, LANES), lambda b, p, j: (b, p)), rows_spec,
                  pl.BlockSpec((S, LANES), lambda b, p, j: (b, p))],
        out_specs=out_specs, out_shape=out_shape, scratch_shapes=scratch,
        compiler_params=_cparams(("parallel", "parallel", "arbitrary")),
    )(qx, kvx, o, lse, do)


def _fox_consts(P):
    H = 2 * P
    eq = np.zeros((3 * LANES, P * LANES), np.float32)
    ek = np.zeros((3 * LANES, P * LANES), np.float32)
    ones_q = np.zeros((1, P * LANES), np.float32)
    ones_k = np.zeros((1, P * LANES), np.float32)
    for h in range(H):
        base = (h // 2) * LANES + FOX_EXTRA * (h % 2)
        for part in range(3):
            eq[part * LANES + h, base + part] = 1.0
            ones_q[0, base + 3 + part] = 1.0
            ones_k[0, base + part] = 1.0
            ek[part * LANES + h, base + 3 + part] = -1.0
    return eq, ek, ones_q, ones_k


def _split3(f):
    hi = f.astype(BF16)
    r = f - hi.astype(F32)
    mid = r.astype(BF16)
    lo = (r - mid.astype(F32)).astype(BF16)
    return hi, mid, lo


def _tri_sum(tri, x):
    hi, mid, lo = _split3(x)
    return (jnp.dot(tri, hi, preferred_element_type=F32) + jnp.dot(tri, mid, preferred_element_type=F32)
            + jnp.dot(tri, lo, preferred_element_type=F32))


def _log1p_pos(e):
    return jnp.where(e < 0.01, e * (1.0 - e * (0.5 - e * (1.0 / 3.0))), jnp.log(1.0 + e))


def _fox_prep(qkv, fl, b_row, *, S, D, name):
    T = qkv.shape[0]
    P = D // LANES
    B = T // S
    tt = _tile(S, 256)
    per = S // tt
    eq, ek, ones_q, ones_k = _fox_consts(P)
    q_scale = HEAD_DIM ** -0.5

    def body(q_ref, k_ref, v_ref, fl_ref, b_ref, eq_ref, ek_ref, oq_ref, ok_ref, qx_ref, kvx_ref, carry):
        i = pl.program_id(1)

        @pl.when(i == 0)
        def _():
            carry[...] = jnp.zeros_like(carry)

        z = fl_ref[...] + b_ref[...]
        logf = jnp.minimum(z, 0.0) - _log1p_pos(jnp.exp(-jnp.abs(z)))
        row = lax.broadcasted_iota(jnp.int32, (tt, tt), 0)
        col = lax.broadcasted_iota(jnp.int32, (tt, tt), 1)
        tri = (col <= row).astype(BF16)
        f = _tri_sum(tri, logf) + carry[...]
        carry[...] = f[tt - 1:tt, :]
        parts = jnp.concatenate(_split3(f), axis=1)
        xq = jnp.dot(parts, eq_ref[...], preferred_element_type=F32) + oq_ref[...]
        xk = jnp.dot(parts, ek_ref[...], preferred_element_type=F32) + ok_ref[...]
        for p in range(P):
            c = slice(p * LANES, (p + 1) * LANES)
            qx_ref[:, p * PAIR_Q:p * PAIR_Q + LANES] = (q_ref[:, c].astype(F32) * q_scale).astype(BF16)
            qx_ref[:, p * PAIR_Q + LANES:(p + 1) * PAIR_Q] = xq[:, c].astype(BF16)
            kvx_ref[:, p * PAIR_KV:p * PAIR_KV + LANES] = k_ref[:, c]
            kvx_ref[:, p * PAIR_KV + LANES:p * PAIR_KV + PAIR_Q] = xk[:, c].astype(BF16)
            kvx_ref[:, p * PAIR_KV + PAIR_Q:(p + 1) * PAIR_KV] = v_ref[:, c]

    tok = lambda b, i: (b * per + i, 0)
    const = lambda b, i: (0, 0)
    return pl.pallas_call(
        body, name=name, grid=(B, per),
        in_specs=[pl.BlockSpec((tt, D), lambda b, i: (b * per + i, 0)),
                  pl.BlockSpec((tt, D), lambda b, i: (b * per + i, 1)),
                  pl.BlockSpec((tt, D), lambda b, i: (b * per + i, 2)),
                  pl.BlockSpec((tt, LANES), tok), pl.BlockSpec((1, LANES), const),
                  pl.BlockSpec(eq.shape, const), pl.BlockSpec(ek.shape, const),
                  pl.BlockSpec(ones_q.shape, const), pl.BlockSpec(ones_k.shape, const)],
        out_specs=[pl.BlockSpec((tt, P * PAIR_Q), tok), pl.BlockSpec((tt, P * PAIR_KV), tok)],
        out_shape=[jax.ShapeDtypeStruct((T, P * PAIR_Q), BF16), jax.ShapeDtypeStruct((T, P * PAIR_KV), BF16)],
        scratch_shapes=[pltpu.VMEM((1, LANES), F32)],
        compiler_params=_cparams(("arbitrary", "arbitrary")),
    )(qkv, qkv, qkv, fl, b_row, jnp.asarray(eq, BF16), jnp.asarray(ek, BF16), jnp.asarray(ones_q), jnp.asarray(ones_k))


def _fox_unprep(dqx, dkvx, csum, rsum, fl, b_row, *, S, D, name):
    T = dqx.shape[0]
    P = D // LANES
    B = T // S
    tt = _tile(S, 256)
    per = S // tt
    q_scale = HEAD_DIM ** -0.5

    def body(dq_ref, dkv_ref, cs_ref, rs_ref, fl_ref, b_ref, dqkv_ref, dfl_ref, db_ref, carry):
        b = pl.program_id(0)
        i = pl.program_id(1)

        @pl.when(i == 0)
        def _():
            carry[...] = jnp.zeros_like(carry)

        @pl.when((i == 0) & (b == 0))
        def _():
            db_ref[...] = jnp.zeros_like(db_ref)

        df = rs_ref[...] - cs_ref[...]
        for p in range(P):
            rq = slice(p * LANES, (p + 1) * LANES)
            dqkv_ref[:, rq] = (dq_ref[:, p * PAIR_Q:p * PAIR_Q + LANES].astype(F32) * q_scale).astype(BF16)
            dqkv_ref[:, D + p * LANES:D + (p + 1) * LANES] = dkv_ref[:, p * PAIR_KV:p * PAIR_KV + LANES]
            dqkv_ref[:, 2 * D + p * LANES:2 * D + (p + 1) * LANES] = dkv_ref[:, p * PAIR_KV + PAIR_Q:(p + 1) * PAIR_KV]
        row = lax.broadcasted_iota(jnp.int32, (tt, tt), 0)
        col = lax.broadcasted_iota(jnp.int32, (tt, tt), 1)
        tri = (col >= row).astype(BF16)
        dlogf = _tri_sum(tri, df) + carry[...]
        carry[...] = dlogf[0:1, :]
        z = fl_ref[...] + b_ref[...]
        e = jnp.exp(-jnp.abs(z))
        sig_neg = jnp.where(z >= 0.0, e, 1.0) / (1.0 + e)
        dfl = dlogf * sig_neg
        dfl_ref[...] = dfl.astype(BF16)
        db_ref[...] += jnp.sum(dfl, axis=0, keepdims=True)

    rev = lambda b, i: (b * per + per - 1 - i, 0)
    const = lambda b, i: (0, 0)
    return pl.pallas_call(
        body, name=name, grid=(B, per),
        in_specs=[pl.BlockSpec((tt, P * PAIR_Q), rev), pl.BlockSpec((tt, P * PAIR_KV), rev),
                  pl.BlockSpec((tt, LANES), rev), pl.BlockSpec((tt, LANES), rev), pl.BlockSpec((tt, LANES), rev),
                  pl.BlockSpec((1, LANES), const)],
        out_specs=[pl.BlockSpec((tt, 3 * D), rev), pl.BlockSpec((tt, LANES), rev), pl.BlockSpec((1, LANES), const)],
        out_shape=[jax.ShapeDtypeStruct((T, 3 * D), BF16), jax.ShapeDtypeStruct((T, LANES), BF16),
                   jax.ShapeDtypeStruct((1, LANES), F32)],
        scratch_shapes=[pltpu.VMEM((1, LANES), F32)],
        compiler_params=_cparams(("arbitrary", "arbitrary")),
    )(dqx, dkvx, csum, rsum, fl, b_row)


def _rms(x):
    r = lax.rsqrt(jnp.mean(x * x, axis=-1, keepdims=True) + NORM_EPS)
    return x * r, r


def _mla_mid(lat, gq, gkv, cos_t, sin_s, *, name):
    T, W = lat.shape
    Rq = W - 2 * LANES
    tt = _tile(T, 512)

    def body(l_ref, gq_ref, gkv_ref, c_ref, s_ref, o_ref):
        nq, _ = _rms(l_ref[:, 0:Rq])
        nkv, _ = _rms(l_ref[:, Rq:Rq + LANES])
        o_ref[:, 0:Rq] = (nq * gq_ref[...]).astype(BF16)
        o_ref[:, Rq:Rq + LANES] = (nkv * gkv_ref[...]).astype(BF16)
        o_ref[:, Rq + LANES:W] = _rope128(l_ref[:, Rq + LANES:W], c_ref[...], s_ref[...]).astype(BF16)

    return pl.pallas_call(
        body, name=name, grid=(T // tt,),
        in_specs=[pl.BlockSpec((tt, W), lambda i: (i, 0)), pl.BlockSpec((1, Rq), lambda i: (0, 0)),
                  pl.BlockSpec((1, LANES), lambda i: (0, 0)), pl.BlockSpec((tt, LANES), lambda i: (i, 0)),
                  pl.BlockSpec((tt, LANES), lambda i: (i, 0))],
        out_specs=pl.BlockSpec((tt, W), lambda i: (i, 0)),
        out_shape=jax.ShapeDtypeStruct((T, W), BF16),
        compiler_params=_cparams(("parallel",)),
    )(lat, gq, gkv, cos_t, sin_s)


def _mla_mid_bwd(lat, dcq, dckr, gq, gkv, cos_t, sin_s, *, name):
    T, W = lat.shape
    Rq = W - 2 * LANES
    tt = _tile(T, 512)

    def norm_bwd(x, dy, g):
        n, r = _rms(x)
        dn = dy * g
        return r * (dn - n * jnp.mean(dn * n, axis=-1, keepdims=True)), jnp.sum(dy * n, axis=0, keepdims=True)

    def body(l_ref, dq_ref, dk_ref, gq_ref, gkv_ref, c_ref, s_ref, o_ref, dgq_ref, dgkv_ref):
        i = pl.program_id(0)

        @pl.when(i == 0)
        def _():
            dgq_ref[...] = jnp.zeros_like(dgq_ref)
            dgkv_ref[...] = jnp.zeros_like(dgkv_ref)

        dxq, dgq = norm_bwd(l_ref[:, 0:Rq], dq_ref[...], gq_ref[...])
        dxkv, dgkv = norm_bwd(l_ref[:, Rq:Rq + LANES], dk_ref[:, 0:LANES], gkv_ref[...])
        o_ref[:, 0:Rq] = dxq.astype(BF16)
        o_ref[:, Rq:Rq + LANES] = dxkv.astype(BF16)
        o_ref[:, Rq + LANES:W] = _rope128(dk_ref[:, LANES:2 * LANES], c_ref[...], -s_ref[...]).astype(BF16)
        dgq_ref[...] += dgq
        dgkv_ref[...] += dgkv

    return pl.pallas_call(
        body, name=name, grid=(T // tt,),
        in_specs=[pl.BlockSpec((tt, W), lambda i: (i, 0)), pl.BlockSpec((tt, Rq), lambda i: (i, 0)),
                  pl.BlockSpec((tt, 2 * LANES), lambda i: (i, 0)), pl.BlockSpec((1, Rq), lambda i: (0, 0)),
                  pl.BlockSpec((1, LANES), lambda i: (0, 0)), pl.BlockSpec((tt, LANES), lambda i: (i, 0)),
                  pl.BlockSpec((tt, LANES), lambda i: (i, 0))],
        out_specs=[pl.BlockSpec((tt, W), lambda i: (i, 0)), pl.BlockSpec((1, Rq), lambda i: (0, 0)),
                   pl.BlockSpec((1, LANES), lambda i: (0, 0))],
        out_shape=[jax.ShapeDtypeStruct((T, W), BF16), jax.ShapeDtypeStruct((1, Rq), F32),
                   jax.ShapeDtypeStruct((1, LANES), F32)],
        compiler_params=_cparams(("arbitrary",)),
    )(lat, dcq, dckr, gq, gkv, cos_t, sin_s)


def _uq_to_pairs(w):
    Rq = w.shape[0]
    P = w.shape[1] // (2 * (HEAD_DIM + ROPE_DIM))
    w4 = w.reshape(Rq, P, 2, HEAD_DIM + ROPE_DIM)
    nope = w4[..., :HEAD_DIM].reshape(Rq, P, 2 * HEAD_DIM)
    rope = w4[..., HEAD_DIM:].reshape(Rq, P, 2 * ROPE_DIM)
    pad = jnp.zeros((Rq, P, PAIR_Q - 2 * HEAD_DIM - 2 * ROPE_DIM), w.dtype)
    return jnp.concatenate([nope, rope, pad], axis=-1).reshape(Rq, P * PAIR_Q)


def _uq_from_pairs(g):
    Rq = g.shape[0]
    P = g.shape[1] // PAIR_Q
    g3 = g.reshape(Rq, P, PAIR_Q)
    nope = g3[..., :2 * HEAD_DIM].reshape(Rq, P, 2, HEAD_DIM)
    rope = g3[..., 2 * HEAD_DIM:2 * HEAD_DIM + 2 * ROPE_DIM].reshape(Rq, P, 2, ROPE_DIM)
    return jnp.concatenate([nope, rope], axis=-1).reshape(Rq, P * 2 * (HEAD_DIM + ROPE_DIM))


def _ukv_to_pairs(w):
    P = w.shape[1] // (4 * HEAD_DIM)
    w4 = w.reshape(KV_RANK, P, 2, 2 * HEAD_DIM)
    kn = w4[..., :HEAD_DIM].reshape(KV_RANK, P, 2 * HEAD_DIM)
    vv = w4[..., HEAD_DIM:].reshape(KV_RANK, P, 2 * HEAD_DIM)
    top = jnp.concatenate([kn, jnp.zeros((KV_RANK, P, LANES), w.dtype), vv], axis=-1)
    place = np.zeros((LANES, P, PAIR_KV), np.float32)
    for r in range(ROPE_DIM):
        place[r, :, LANES + r] = 1.0
        place[r, :, LANES + ROPE_DIM + r] = 1.0
    return jnp.concatenate([top, jnp.asarray(place, w.dtype)], axis=0).reshape(KV_RANK + LANES, P * PAIR_KV)


def _ukv_from_pairs(g):
    P = g.shape[1] // PAIR_KV
    g3 = g[:KV_RANK].reshape(KV_RANK, P, PAIR_KV)
    kn = g3[..., :2 * HEAD_DIM].reshape(KV_RANK, P, 2, HEAD_DIM)
    vv = g3[..., PAIR_Q:].reshape(KV_RANK, P, 2, HEAD_DIM)
    return jnp.concatenate([kn, vv], axis=-1).reshape(KV_RANK, P * 4 * HEAD_DIM)


def _mlp_fwd(h2, w, i, x1, gate, *, S):
    p, u = _mm(h2, w["mlp_w1"], "nn", name=f"mlp_up_{i}", b_layer=i, out_dtypes=(BF16, BF16),
               epilogue=lambda acc: (acc, jnp.square(jnp.maximum(acc, 0.0))))
    x2, z = _mm(u, w["mlp_w2"], "nn", name=f"mlp_down_{i}", b_layer=i, out_dtypes=(F32, F32), extras=(x1,),
                rowvecs=(gate,), seq=S, epilogue=lambda acc, xr, g: (xr + g * acc, acc))
    return x2, (p, u, z)


STACKED_GRADS = ("fox_out", "mla_down", "mla_uq", "mla_ukv", "mla_out", "mlp_w1", "mlp_w2")


def _local_step(x, target, pos_f, inv_freq_row, sign_row, mod, w, *, S):
    T, D = x.shape
    L = mod.shape[0]
    L2 = w["fox_out"].shape[0]
    n_split = w["mlp_w1"].shape[1]
    cos_t, sin_s = _rope_tables(pos_f, inv_freq_row, sign_row)
    saved = []
    for i in range(L):
        j = i // 2
        sh_m, sc_m, g_m, sh_f, sc_f, g_f = (mod[i, s] for s in range(6))
        h = _norm_mod(x, w["norm_mix_g"][i], sc_m, sh_m, S=S, name=f"norm_mix_{i}")
        if i % 2 == 0:
            qkv = _mm(h, w["fox_qkv"], "nn", name=f"fox_qkv_{i}", b_layer=j, out_dtypes=(BF16,))
            fl = _mm(h, w["fox_f"], "nn", name=f"fox_f_{i}", b_layer=j)
            qx, kvx = _fox_prep(qkv, fl, w["fox_b"][j], S=S, D=D, name=f"fox_prep_{i}")
            o, lse = _attn_fwd(qx, kvx, S=S, scale=1.0, ew=FOX_EXTRA, name=f"fox_attn_{i}")
            mix = (qx, kvx, o, lse, fl)
            w_out = w["fox_out"]
        else:
            lat = _mm(h, w["mla_down"], "nn", name=f"mla_down_{i}", b_layer=j)
            Rq = lat.shape[1] - 2 * LANES
            cqr = _mla_mid(lat, w["mla_gq"][j], w["mla_gkv"][j], cos_t, sin_s, name=f"mla_mid_{i}")
            qx = _mm(cqr, w["mla_uq"], "nn", name=f"mla_uq_{i}", b_layer=j, out_dtypes=(BF16,), a_sz=Rq, tk=Rq,
                     tables=(cos_t, sin_s), epilogue=lambda acc, c, s: (_rope_pairs(acc, c, s, 1.0),))
            kvx = _mm(cqr, w["mla_ukv"], "nn", name=f"mla_ukv_{i}", b_layer=j, out_dtypes=(BF16,), a_off=Rq,
                      a_sz=2 * LANES, tk=2 * LANES, tn=PAIR_KV)
            o, lse = _attn_fwd(qx, kvx, S=S, scale=(HEAD_DIM + ROPE_DIM) ** -0.5, ew=ROPE_DIM, name=f"mla_attn_{i}")
            mix = (qx, kvx, o, lse, lat, cqr)
            w_out = w["mla_out"]
        x1, y = _mm(o, w_out, "nn", name=f"mix_out_{i}", b_layer=j, out_dtypes=(F32, F32), extras=(x,),
                    rowvecs=(g_m,), seq=S, epilogue=lambda acc, xr, g: (xr + g * acc, acc))
        h2 = _norm_mod(x1, w["norm_mlp_g"][i], sc_f, sh_f, S=S, name=f"norm_mlp_{i}")
        x2, mlp = _mlp_fwd(h2, w, i, x1, g_f, S=S)
        saved.append((x, h, mix, y, x1, h2, mlp))
        x = x2

    dx, dg_final, loss = _final_loss(x, target, w["final_norm_g"])

    grads = {k: [None] * len(w[k]) for k in ("norm_mix_g", "norm_mlp_g", "fox_b", "mla_gq", "mla_gkv")}
    grads.update({k: [None] * L2 for k in ("fox_qkv", "fox_f")})
    grads.update({k: None for k in STACKED_GRADS})
    grads["final_norm_g"] = dg_final

    def stacked(key, layer, n_layers, a, b, **kw):
        grads[key] = _mm(a, b, "tn", out_stack=(grads[key], layer, n_layers), **kw)

    dmod = [None] * L
    for i in reversed(range(L)):
        j = i // 2
        x0, h, mix, y, x1, h2, (p, u, z) = saved[i]
        sh_m, sc_m, g_m, sh_f, sc_f, g_f = (mod[i, s] for s in range(6))
        dz, dg_f = _gate_bwd(dx, z, g_f, S=S, name=f"gate_mlp_bwd_{i}")
        stacked("mlp_w2", i, L, u, dz, name=f"mlp_w2_grad_{i}")
        dp = _mm(dz, w["mlp_w2"], "nt", name=f"mlp_down_bwd_{i}", b_layer=i, out_dtypes=(BF16,), extras=(p,),
                 epilogue=lambda acc, pv: (acc * (2.0 * jnp.maximum(pv.astype(F32), 0.0)),))
        stacked("mlp_w1", i, L, h2, dp, name=f"mlp_w1_grad_{i}", out_split=n_split)
        dh2 = _mm(dp, w["mlp_w1"], "nt", name=f"mlp_up_bwd_{i}", b_layer=i)
        dx1, dsh_f, dsc_f, dgn = _norm_mod_bwd(x1, dh2, dx, w["norm_mlp_g"][i], sc_f, S=S, name=f"norm_mlp_bwd_{i}")
        grads["norm_mlp_g"][i] = dgn
        dy, dg_m = _gate_bwd(dx1, y, g_m, S=S, name=f"gate_mix_bwd_{i}")
        if i % 2 == 0:
            qx, kvx, o, lse, fl = mix
            stacked("fox_out", j, L2, o, dy, name=f"fox_out_grad_{i}")
            do = _mm(dy, w["fox_out"], "nt", name=f"fox_out_bwd_{i}", b_layer=j, out_dtypes=(BF16,))
            dqx, dkvx, csum, rsum = _attn_bwd(qx, kvx, o, lse, do, S=S, scale=1.0, ew=FOX_EXTRA,
                                              name=f"fox_attn_bwd_{i}", bias_grad=True)
            n_heads = D // HEAD_DIM
            csum = jnp.pad(csum.reshape(T, n_heads, HEAD_DIM)[:, :, 0], ((0, 0), (0, LANES - n_heads)))
            rsum = jnp.transpose(rsum[:, :, :2, :], (0, 3, 1, 2)).reshape(T, n_heads)
            rsum = jnp.pad(rsum, ((0, 0), (0, LANES - n_heads)))
            dqkv, dfl, db = _fox_unprep(dqx, dkvx, csum, rsum, fl, w["fox_b"][j], S=S, D=D, name=f"fox_unprep_{i}")
            grads["fox_b"][j] = db
            grads["fox_qkv"][j] = _mm(h, dqkv, "tn", name=f"fox_qkv_grad_{i}")
            grads["fox_f"][j] = _mm(h, dfl, "tn", name=f"fox_f_grad_{i}")
            dh_f = _mm(dfl, w["fox_f"], "nt", name=f"fox_f_bwd_{i}", b_layer=j)
            dh = _mm(dqkv, w["fox_qkv"], "nt", name=f"fox_qkv_bwd_{i}", b_layer=j, extras=(dh_f,),
                     epilogue=lambda acc, e: (acc + e,))
        else:
            qx, kvx, o, lse, lat, cqr = mix
            Rq = lat.shape[1] - 2 * LANES
            stacked("mla_out", j, L2, o, dy, name=f"mla_out_grad_{i}")
            do = _mm(dy, w["mla_out"], "nt", name=f"mla_out_bwd_{i}", b_layer=j, out_dtypes=(BF16,))
            dqx, dkvx = _attn_bwd(qx, kvx, o, lse, do, S=S, scale=(HEAD_DIM + ROPE_DIM) ** -0.5, ew=ROPE_DIM,
                                  name=f"mla_attn_bwd_{i}")
            dqpre = _unrope(dqx, cos_t, sin_s)
            stacked("mla_uq", j, L2, cqr, dqpre, name=f"mla_uq_grad_{i}", a_sz=Rq, tm=min(Rq, 256),
                    out_split=n_split)
            stacked("mla_ukv", j, L2, cqr, dkvx, name=f"mla_ukv_grad_{i}", a_off=Rq, a_sz=2 * LANES,
                    tm=2 * LANES, tn=PAIR_KV, out_split=n_split)
            dcq = _mm(dqpre, w["mla_uq"], "nt", name=f"mla_uq_bwd_{i}", b_layer=j)
            dckr = _mm(dkvx, w["mla_ukv"], "nt", name=f"mla_ukv_bwd_{i}", b_layer=j, tk=PAIR_KV * 2)
            dlat, dgq, dgkv = _mla_mid_bwd(lat, dcq, dckr, w["mla_gq"][j], w["mla_gkv"][j], cos_t, sin_s,
                                           name=f"mla_mid_bwd_{i}")
            grads["mla_gq"][j] = dgq
            grads["mla_gkv"][j] = dgkv
            stacked("mla_down", j, L2, h, dlat, name=f"mla_down_grad_{i}")
            dh = _mm(dlat, w["mla_down"], "nt", name=f"mla_down_bwd_{i}", b_layer=j)
        dx, dsh_m, dsc_m, dgn = _norm_mod_bwd(x0, dh, dx1, w["norm_mix_g"][i], sc_m, S=S, name=f"norm_mix_bwd_{i}")
        grads["norm_mix_g"][i] = dgn
        dmod[i] = jnp.stack([dsh_m, dsc_m, dg_m, dsh_f, dsc_f, dg_f])
    return loss, dx, jnp.stack(dmod), grads


GATHERED = ("fox_in", "fox_out", "mla_down", "mla_uq", "mla_ukv", "mla_out", "mlp_w1", "mlp_w2")
ROW_SHARDED = ("fox_out", "mla_down", "mla_out", "mlp_w2")


def _shard_layouts(wts):
    dkv = wts["mla_w_dkv"]
    dkv = jnp.pad(dkv, ((0, 0), (0, 0), (0, 2 * LANES - dkv.shape[2])))
    return {
        "fox_in": wts["fox_w_in"].astype(BF16),
        "fox_out": wts["fox_w_out"].astype(BF16),
        "mla_down": jnp.concatenate([wts["mla_w_dq"], dkv], axis=2).astype(BF16),
        "mla_uq": jax.vmap(_uq_to_pairs)(wts["mla_w_uq"].astype(BF16)),
        "mla_ukv": jax.vmap(_ukv_to_pairs)(wts["mla_w_ukv"].astype(BF16)),
        "mla_out": wts["mla_w_out"].astype(BF16),
        "mlp_w1": wts["mlp_w1"].astype(BF16),
        "mlp_w2": wts["mlp_w2"].astype(BF16),
    }


def _small_layouts(small):
    return {
        "fox_b": [jnp.pad(b, (0, LANES - b.shape[0]))[None, :] for b in small["fox_b_f"]],
        "mla_gq": [g[None, :] for g in small["mla_q_norm_g"]],
        "mla_gkv": [g[None, :] for g in small["mla_kv_norm_g"]],
        "norm_mix_g": [g[None, :] for g in small["norm_mix_g"]],
        "norm_mlp_g": [g[None, :] for g in small["norm_mlp_g"]],
        "final_norm_g": small["final_norm_g"][None, :],
    }


def _full_layouts(gathered, D):
    fox = gathered["fox_in"]
    L2, ns, _, bs = fox.shape
    fox = jnp.transpose(fox, (0, 2, 1, 3)).reshape(L2, D, ns * bs)
    w = {"fox_qkv": fox[:, :, :3 * D],
         "fox_f": jnp.pad(fox[:, :, 3 * D:], ((0, 0), (0, 0), (0, LANES - (ns * bs - 3 * D))))}
    for n in ROW_SHARDED:
        g = gathered[n]
        w[n] = g.reshape(g.shape[0], g.shape[1] * g.shape[2], g.shape[3])
    for n in ("mla_uq", "mla_ukv", "mlp_w1"):
        w[n] = gathered[n]
    return w


def _grad_layouts(g, n_fox_heads, ns):
    fox = jnp.stack([jnp.concatenate([a, b[:, :n_fox_heads]], axis=1) for a, b in zip(g["fox_qkv"], g["fox_f"])])
    L2, D, cols = fox.shape
    out = {"fox_in": jnp.transpose(fox.reshape(L2, D, ns, cols // ns), (0, 2, 1, 3))}
    for n in ROW_SHARDED:
        a = g[n]
        out[n] = a.reshape(a.shape[0], ns, a.shape[1] // ns, a.shape[2])
    for n in ("mla_uq", "mla_ukv", "mlp_w1"):
        out[n] = g[n]
    return out


def _natural_shard_grads(s, rq):
    return {
        "fox_w_in": s["fox_in"], "fox_w_out": s["fox_out"], "mla_w_out": s["mla_out"],
        "mlp_w1": s["mlp_w1"], "mlp_w2": s["mlp_w2"],
        "mla_w_dq": s["mla_down"][:, :, :rq],
        "mla_w_dkv": s["mla_down"][:, :, rq:rq + KV_RANK + ROPE_DIM],
        "mla_w_uq": jax.vmap(_uq_from_pairs)(s["mla_uq"]),
        "mla_w_ukv": jax.vmap(_ukv_from_pairs)(s["mla_ukv"]),
    }


def _small_grads(g, n_fox_heads):
    return {
        "norm_mix_g": jnp.concatenate(g["norm_mix_g"], axis=0),
        "norm_mlp_g": jnp.concatenate(g["norm_mlp_g"], axis=0),
        "final_norm_g": g["final_norm_g"][0],
        "fox_b_f": jnp.concatenate(g["fox_b"], axis=0)[:, :n_fox_heads],
        "mla_q_norm_g": jnp.concatenate(g["mla_gq"], axis=0),
        "mla_kv_norm_g": jnp.concatenate(g["mla_gkv"], axis=0),
    }


def _silu(c):
    return c * (1.0 / (1.0 + jnp.exp(-c)))


def _ada_fwd(c_all, ada_w, ada_b_cols):
    L, D, C = ada_w.shape
    Bg = c_all.shape[0]
    tc = _tile(C, 512)

    def body(c_ref, w_ref, b_ref, o_ref):
        ca = _silu(c_ref[...]).astype(BF16)
        o_ref[...] = jnp.dot(ca, w_ref[...].astype(BF16), preferred_element_type=F32) + b_ref[...]

    return pl.pallas_call(
        body, name="ada_fwd", grid=(L, C // tc),
        in_specs=[pl.BlockSpec((Bg, D), lambda l, j: (0, 0)), pl.BlockSpec((None, D, tc), lambda l, j: (l, 0, j)),
                  pl.BlockSpec((None, 1, tc), lambda l, j: (l, 0, j))],
        out_specs=pl.BlockSpec((None, Bg, tc), lambda l, j: (l, 0, j)),
        out_shape=jax.ShapeDtypeStruct((L, Bg, C), F32),
        compiler_params=_cparams(("parallel", "parallel")),
    )(c_all, ada_w, ada_b_cols)


def _ada_bwd(c_all, dmod_cols):
    L, Bg, C = dmod_cols.shape
    D = c_all.shape[1]
    tc = _tile(C, 512)

    def body(c_ref, d_ref, o_ref):
        ca = _silu(c_ref[...]).astype(BF16)
        o_ref[...] = _dot_tn(ca, d_ref[...].astype(BF16))

    return pl.pallas_call(
        body, name="ada_bwd", grid=(L, C // tc),
        in_specs=[pl.BlockSpec((Bg, D), lambda l, j: (0, 0)), pl.BlockSpec((None, Bg, tc), lambda l, j: (l, 0, j))],
        out_specs=pl.BlockSpec((None, D, tc), lambda l, j: (l, 0, j)),
        out_shape=jax.ShapeDtypeStruct((L, D, C), F32),
        compiler_params=_cparams(("parallel", "parallel")),
    )(c_all, dmod_cols)


def _adamw_update(w, gv, m, v):
    mn = ADAM_B1 * m + (1.0 - ADAM_B1) * gv
    vn = ADAM_B2 * v + (1.0 - ADAM_B2) * jnp.square(gv)
    m_hat = mn / (1.0 - ADAM_B1 ** ADAM_STEP)
    v_hat = vn / (1.0 - ADAM_B2 ** ADAM_STEP)
    return -ADAM_LR * (m_hat / (jnp.sqrt(v_hat) + ADAM_EPS) + ADAM_WD * w), mn, vn


def _adamw(w, g, m, v, *, name):
    shape = w.shape
    C = shape[-1]
    R = int(np.prod(shape[:-1])) if len(shape) > 1 else 1
    w2, g2, m2, v2 = (a.reshape(R, C) for a in (w, g, m, v))
    tr = _row_tile(R, C)

    def body(w_ref, g_ref, m_ref, v_ref, d_ref, nm_ref, nv_ref):
        d_ref[...], nm_ref[...], nv_ref[...] = _adamw_update(w_ref[...], g_ref[...], m_ref[...], v_ref[...])

    spec = pl.BlockSpec((tr, C), lambda i: (i, 0))
    out = pl.pallas_call(
        body, name=name, grid=(R // tr,), in_specs=[spec] * 4, out_specs=[spec] * 3,
        out_shape=[jax.ShapeDtypeStruct((R, C), F32)] * 3, compiler_params=_cparams(("parallel",)),
    )(w2, g2, m2, v2)
    return tuple(a.reshape(shape) for a in out)


def _adamw_halves(w, g_own, g_peer, m, v, c_idx, *, name):
    shape = w.shape
    C = shape[-1]
    R = int(np.prod(shape[:-1])) // 2
    tr = _row_tile(R, C)

    def body(c_ref, w_ref, go_ref, gp_ref, m_ref, v_ref, g_ref, d_ref, nm_ref, nv_ref):
        gv = jnp.where(pl.program_id(0) == c_ref[0], go_ref[...], gp_ref[...])
        g_ref[...] = gv
        d_ref[...], nm_ref[...], nv_ref[...] = _adamw_update(w_ref[...], gv, m_ref[...], v_ref[...])

    full = pl.BlockSpec((None, tr, C), lambda hh, i, c_ref: (hh, i, 0))
    half = pl.BlockSpec((tr, C), lambda hh, i, c_ref: (i, 0))
    grid_spec = pltpu.PrefetchScalarGridSpec(
        num_scalar_prefetch=1, grid=(2, R // tr), in_specs=[full, half, half, full, full], out_specs=[full] * 4)
    out = pl.pallas_call(
        body, name=name, grid_spec=grid_spec, out_shape=[jax.ShapeDtypeStruct((2, R, C), F32)] * 4,
        compiler_params=_cparams(("parallel", "parallel")),
    )(c_idx, w.reshape(2, R, C), g_own.reshape(R, C), g_peer.reshape(R, C), m.reshape(2, R, C), v.reshape(2, R, C))
    return tuple(a.reshape(shape) for a in out)


def _sum_gathered(dm8, sm8):
    n_dev, Bl, R, D = dm8.shape
    Rs = sm8.shape[1]

    def body(dm_ref, sm_ref, ob_ref, os_ref):
        acc_b = jnp.zeros((R, D), F32)
        acc_s = jnp.zeros((Rs, D), F32)
        for d in range(n_dev):
            for b in range(Bl):
                acc_b = acc_b + dm_ref[d, b]
            acc_s = acc_s + sm_ref[d]
        ob_ref[...] = acc_b
        os_ref[...] = acc_s

    return pl.pallas_call(
        body, name="sum_gathered",
        out_shape=[jax.ShapeDtypeStruct((R, D), F32), jax.ShapeDtypeStruct((Rs, D), F32)],
        compiler_params=_cparams(None),
    )(dm8, sm8)


N_DEV = 8
N_CHIP = 4
ANY = pl.BlockSpec(memory_space=pl.ANY)


def _mesh_pos():
    return lax.axis_index("x"), lax.axis_index("y"), lax.axis_index("c")


def _all_gather8(block, *, name, in_vmem):
    R, W = block.shape

    def body(x_ref, out_ref, send_sems, recv_sems, local_sem):
        x, y, c = _mesh_pos()
        me, sibling = (x, y, c), (x, y, 1 - c)
        chips = [(1 - x, y), (x, 1 - y), (1 - x, 1 - y)]

        def slot(px, py, pc):
            return out_ref.at[4 * px + 2 * py + pc]

        def copy(k, blk, to, src=None):
            return pltpu.make_async_remote_copy(
                src_ref=slot(*blk) if src is None else src, dst_ref=slot(*blk),
                send_sem=send_sems.at[k], recv_sem=recv_sems.at[k], device_id=to, device_id_type=MESH_ID)

        mine = pltpu.make_async_copy(x_ref, slot(*me), local_sem)
        mine.start()
        first = [copy(0, me, sibling, src=x_ref)]
        first += [copy(1 + j, me, (*chip, c), src=x_ref) for j, chip in enumerate(chips)]
        for cp in first:
            cp.start()
        passed = [copy(4 + j, (*chip, c), sibling) for j, chip in enumerate(chips)]
        for j, chip in enumerate(chips):
            copy(1 + j, (*chip, c), me).wait_recv()
            passed[j].start()
        copy(0, sibling, me).wait_recv()
        for j, chip in enumerate(chips):
            copy(4 + j, (*chip, 1 - c), me).wait_recv()
        for cp in first + passed:
            cp.wait_send()
        mine.wait()

    space = pl.BlockSpec(memory_space=pltpu.VMEM) if in_vmem else ANY
    return pl.pallas_call(
        body, name=name, out_shape=jax.ShapeDtypeStruct((N_DEV, R, W), block.dtype),
        in_specs=[space], out_specs=space,
        scratch_shapes=[pltpu.SemaphoreType.DMA((7,)), pltpu.SemaphoreType.DMA((7,)), pltpu.SemaphoreType.DMA],
        compiler_params=pltpu.CompilerParams(vmem_limit_bytes=VMEM_LIMIT_V7X),
    )(block)


def _comm_call(body, arrays, out_shapes, n_sems, *, name):
    return pl.pallas_call(
        body, name=name, out_shape=out_shapes, in_specs=[ANY] * len(arrays), out_specs=[ANY] * len(out_shapes),
        scratch_shapes=[pltpu.SemaphoreType.DMA((n_sems,)), pltpu.SemaphoreType.DMA((n_sems,)),
                        pltpu.SemaphoreType.DMA((len(arrays),))],
    )(*arrays)


def _gather_weights(shards, *, name):
    n = len(shards)

    def body(*refs):
        xs, outs = refs[:n], refs[n:2 * n]
        send_sems, recv_sems, local_sems = refs[2 * n:]
        x, y, c = _mesh_pos()
        me, sibling = (x, y, c), (x, y, 1 - c)
        chips = [(1 - x, y), (x, 1 - y), (1 - x, 1 - y)]
        waits = []
        for i in range(n):
            nl = shards[i].shape[0] // 2
            own = xs[i].at[pl.ds(c * nl, nl)]

            def slot(px, py, pc, i=i, nl=nl):
                return outs[i].at[pl.ds(pc * nl, nl), 2 * px + py]

            def copy(k, blk, to, src=None, i=i, slot=slot):
                return pltpu.make_async_remote_copy(
                    src_ref=slot(*blk) if src is None else src, dst_ref=slot(*blk),
                    send_sem=send_sems.at[7 * i + k], recv_sem=recv_sems.at[7 * i + k], device_id=to,
                    device_id_type=MESH_ID)

            mine = pltpu.make_async_copy(own, slot(*me), local_sems.at[i])
            mine.start()
            first = [copy(0, me, sibling, src=own)]
            first += [copy(1 + j, me, (*chip, c), src=own) for j, chip in enumerate(chips)]
            for cp in first:
                cp.start()
            waits.append((copy, mine, first))
        for copy, mine, first in waits:
            passed = [copy(4 + j, (*chip, c), sibling) for j, chip in enumerate(chips)]
            for j, chip in enumerate(chips):
                copy(1 + j, (*chip, c), me).wait_recv()
                passed[j].start()
            copy(0, sibling, me).wait_recv()
            for j, chip in enumerate(chips):
                copy(4 + j, (*chip, 1 - c), me).wait_recv()
            for cp in first + passed:
                cp.wait_send()
            mine.wait()

    out_shapes = [jax.ShapeDtypeStruct((s.shape[0], N_CHIP) + s.shape[1:], s.dtype) for s in shards]
    return _comm_call(body, shards, out_shapes, 7 * n, name=name)


def _pair_exchange(gs, *, name):
    n = len(gs)

    def body(*refs):
        xs, outs = refs[:n], refs[n:2 * n]
        send_sems, recv_sems, _ = refs[2 * n:]
        x, y, c = _mesh_pos()
        copies = []
        for i in range(n):
            nl = gs[i].shape[0] // 2
            cp = pltpu.make_async_remote_copy(
                src_ref=xs[i].at[pl.ds((1 - c) * nl, nl)], dst_ref=outs[i], send_sem=send_sems.at[i],
                recv_sem=recv_sems.at[i], device_id=(x, y, 1 - c), device_id_type=MESH_ID)
            cp.start()
            copies.append(cp)
        for cp in copies:
            cp.wait()

    out_shapes = [jax.ShapeDtypeStruct((g.shape[0] // 2,) + g.shape[1:], g.dtype) for g in gs]
    return _comm_call(body, gs, out_shapes, n, name=name)


def _chip_exchange(ps, *, name):
    n = len(ps)

    def body(*refs):
        xs, outs = refs[:n], refs[n:2 * n]
        send_sems, recv_sems, local_sems = refs[2 * n:]
        x, y, c = _mesh_pos()
        k_me = 2 * x + y
        chips = [(1 - x, y), (x, 1 - y), (1 - x, 1 - y)]
        copies = []
        for i in range(n):
            nl = ps[i].shape[0]
            mine = pltpu.make_async_copy(xs[i].at[pl.ds(0, nl), k_me], outs[i].at[k_me], local_sems.at[i])
            mine.start()
            copies.append(mine)
            for j, (cx, cy) in enumerate(chips):
                cp = pltpu.make_async_remote_copy(
                    src_ref=xs[i].at[pl.ds(0, nl), 2 * cx + cy], dst_ref=outs[i].at[k_me],
                    send_sem=send_sems.at[3 * i + j], recv_sem=recv_sems.at[3 * i + j],
                    device_id=(cx, cy, c), device_id_type=MESH_ID)
                cp.start()
                copies.append(cp)
        for cp in copies:
            cp.wait()

    out_shapes = [jax.ShapeDtypeStruct((p.shape[1], p.shape[0]) + p.shape[2:], p.dtype) for p in ps]
    return _comm_call(body, ps, out_shapes, 3 * n, name=name)


def _pair_swap(ss, *, name):
    n = len(ss)

    def body(*refs):
        xs, outs = refs[:n], refs[n:2 * n]
        send_sems, recv_sems, _ = refs[2 * n:]
        x, y, c = _mesh_pos()
        copies = []
        for i in range(n):
            cp = pltpu.make_async_remote_copy(src_ref=xs[i], dst_ref=outs[i], send_sem=send_sems.at[i],
                                              recv_sem=recv_sems.at[i], device_id=(x, y, 1 - c),
                                              device_id_type=MESH_ID)
            cp.start()
            copies.append(cp)
        for cp in copies:
            cp.wait()

    out_shapes = [jax.ShapeDtypeStruct(s.shape, s.dtype) for s in ss]
    return _comm_call(body, ss, out_shapes, n, name=name)


def _row_tile(rows, cols):
    tr = rows
    while tr * cols > 256 * 1024 and tr % 16 == 0:
        tr //= 2
    return tr


def _pair_add(g, recv, c_idx, *, name):
    shape = recv.shape
    W = shape[-1]
    R = int(np.prod(shape[:-1]))
    tr = _row_tile(R, W)

    def body(c_ref, g_ref, r_ref, o_ref):
        o_ref[...] = (g_ref[...] + r_ref[...]).astype(BF16)

    grid_spec = pltpu.PrefetchScalarGridSpec(
        num_scalar_prefetch=1, grid=(R // tr,),
        in_specs=[pl.BlockSpec((None, tr, W), lambda i, c_ref: (c_ref[0], i, 0)),
                  pl.BlockSpec((tr, W), lambda i, c_ref: (i, 0))],
        out_specs=pl.BlockSpec((tr, W), lambda i, c_ref: (i, 0)))
    out = pl.pallas_call(
        body, name=name, grid_spec=grid_spec, out_shape=jax.ShapeDtypeStruct((R, W), BF16),
        compiler_params=_cparams(("parallel",)),
    )(c_idx, g.reshape(2, R, W), recv.reshape(R, W))
    return out.reshape(shape)


def _sum_pieces(pieces, *, name):
    n = pieces.shape[0]
    shape = pieces.shape[1:]
    W = shape[-1]
    R = int(np.prod(shape[:-1]))
    tr = _row_tile(R, W)

    def body(p_ref, o_ref):
        acc = p_ref[0].astype(F32)
        for k in range(1, n):
            acc = acc + p_ref[k].astype(F32)
        o_ref[...] = acc

    out = pl.pallas_call(
        body, name=name, grid=(R // tr,), in_specs=[pl.BlockSpec((n, tr, W), lambda i: (0, i, 0))],
        out_specs=pl.BlockSpec((tr, W), lambda i: (i, 0)), out_shape=jax.ShapeDtypeStruct((R, W), F32),
        compiler_params=_cparams(("parallel",)),
    )(pieces.reshape(n, R, W))
    return out.reshape(shape)


SMALL = ("norm_mix_g", "norm_mlp_g", "final_norm_g", "fox_b_f", "mla_q_norm_g", "mla_kv_norm_g")
WEIGHT_ORDER = ("ada_w", "ada_b", "norm_mix_g", "norm_mlp_g", "fox_w_in", "fox_b_f", "fox_w_out", "mla_w_dq",
                "mla_q_norm_g", "mla_w_uq", "mla_w_dkv", "mla_kv_norm_g", "mla_w_ukv", "mla_w_out", "mlp_w1",
                "mlp_w2", "final_norm_g")


def _small_rows(vals, D):
    rows = [vals["norm_mix_g"], vals["norm_mlp_g"], vals["final_norm_g"][None, :]]
    for n in ("fox_b_f", "mla_q_norm_g", "mla_kv_norm_g"):
        flat = vals[n].reshape(-1)
        assert flat.shape[0] <= D
        rows.append(jnp.pad(flat, (0, D - flat.shape[0]))[None, :])
    return jnp.concatenate(rows, axis=0)


def _small_unrows(rows, shapes):
    L = shapes["norm_mix_g"][0]
    out = {"norm_mix_g": rows[0:L], "norm_mlp_g": rows[L:2 * L], "final_norm_g": rows[2 * L]}
    for k, n in enumerate(("fox_b_f", "mla_q_norm_g", "mla_kv_norm_g")):
        size = int(np.prod(shapes[n]))
        out[n] = rows[2 * L + 1 + k, :size].reshape(shapes[n])
    return out


def kernel(x, c, positions, ada_w, ada_b, norm_mix_g, norm_mlp_g, fox_w_in, fox_b_f, fox_w_out, mla_w_dq, mla_q_norm_g, mla_w_uq, mla_w_dkv, mla_kv_norm_g, mla_w_ukv, mla_w_out, mlp_w1, mlp_w2, final_norm_g, loss_target, m_ada_w, m_ada_b, m_norm_mix_g, m_norm_mlp_g, m_fox_w_in, m_fox_b_f, m_fox_w_out, m_mla_w_dq, m_mla_q_norm_g, m_mla_w_uq, m_mla_w_dkv, m_mla_kv_norm_g, m_mla_w_ukv, m_mla_w_out, m_mlp_w1, m_mlp_w2, m_final_norm_g, v_ada_w, v_ada_b, v_norm_mix_g, v_norm_mlp_g, v_fox_w_in, v_fox_b_f, v_fox_w_out, v_mla_w_dq, v_mla_q_norm_g, v_mla_w_uq, v_mla_w_dkv, v_mla_kv_norm_g, v_mla_w_ukv, v_mla_w_out, v_mlp_w1, v_mlp_w2, v_final_norm_g):
    args = dict(locals())
    wts = {n: args[n] for n in WEIGHT_ORDER}
    mom = {n: args["m_" + n] for n in WEIGHT_ORDER}
    var = {n: args["v_" + n] for n in WEIGHT_ORDER}
    Bl, S, D = x.shape
    T = Bl * S
    L = ada_w.shape[0]
    C = ada_w.shape[2]
    mx, my, mc = _mesh_pos()
    chip = 2 * mx + my
    dev = 4 * mx + 2 * my + mc
    c_idx = jnp.reshape(mc, (1,)).astype(jnp.int32)

    shards = _shard_layouts(wts)
    gathered = dict(zip(GATHERED, _gather_weights([shards[n] for n in GATHERED], name="gather_weights")))
    small = {n: wts[n] for n in SMALL}
    L2, q_cols = mla_q_norm_g.shape

    c_pad = jnp.concatenate([c, jnp.pad(mla_q_norm_g, ((0, 8 - Bl - L2), (0, D - q_cols)))], axis=0)
    c8 = _all_gather8(c_pad, name="gather_c", in_vmem=True)
    c_all = c8[:, :Bl].reshape(N_DEV * Bl, D)
    qg4 = c8.reshape(N_CHIP, 2, 8, D)[:, 0, Bl:Bl + L2, :q_cols]
    small["mla_q_norm_g"] = jnp.transpose(qg4, (1, 0, 2)).reshape(L2, N_CHIP * q_cols)
    ada_b_cols = lax.dynamic_slice_in_dim(ada_b, chip * C, C, axis=1)[:, None, :]
    mod_cols = _ada_fwd(c_all, ada_w, ada_b_cols)
    mod8 = _all_gather8(mod_cols.reshape(L * N_DEV * Bl, C), name="gather_mod", in_vmem=True)
    mod4 = mod8.reshape(N_CHIP, 2, L, N_DEV * Bl, C)[:, 0]
    mod_me = lax.dynamic_slice_in_dim(mod4, dev * Bl, Bl, axis=2)
    mod = jnp.transpose(mod_me, (1, 2, 0, 3)).reshape(L, Bl, 6, D)
    mod = jnp.transpose(mod, (0, 2, 1, 3))[:, :, :, None, :]

    w = _full_layouts(gathered, D)
    w.update(_small_layouts(small))
    half = ROPE_DIM // 2
    inv_freq = ROPE_THETA ** (-jnp.arange(0, ROPE_DIM, 2, dtype=F32) / ROPE_DIM)
    lane = np.arange(LANES)
    inv_freq_row = jnp.tile(inv_freq, LANES // half)[None, :]
    sign_row = jnp.asarray(np.where(lane < 2 * ROPE_DIM, np.where(lane % ROPE_DIM < half, -1.0, 1.0), 0.0), F32)[None, :]
    pos_f = positions.astype(F32).reshape(T, 1)
    loss_row, grad_x, dmod, g = _local_step(x.reshape(T, D), loss_target.reshape(T, D), pos_f, inv_freq_row, sign_row,
                                            mod, w, S=S)
    g_small = _small_grads(g, fox_b_f.shape[1])

    Rs = -(-(2 * L + 5) // 8) * 8
    srows = jnp.concatenate([_small_rows(g_small, D), jnp.pad(loss_row, ((0, 0), (0, D - LANES)))], axis=0)
    srows = jnp.pad(srows, ((0, Rs - srows.shape[0]), (0, 0)))
    drows = jnp.transpose(dmod[:, :, :, 0, :], (2, 0, 1, 3)).reshape(Bl * L * 6, D)
    both8 = _all_gather8(jnp.concatenate([drows, srows], axis=0), name="gather_small", in_vmem=True)
    dm8 = both8[:, :Bl * L * 6].reshape(N_DEV, Bl, L * 6, D)
    sm8 = both8[:, Bl * L * 6:]
    adb_rows, small_sum = _sum_gathered(dm8, sm8)
    grad_ada_b = adb_rows.reshape(L, 6 * D)
    loss = small_sum[2 * L + 4, 0]
    small_shapes = {n: (wts[n].shape if n != "mla_q_norm_g" else (wts[n].shape[0], N_CHIP * q_cols)) for n in SMALL}
    gs = _small_unrows(small_sum, small_shapes)
    gs["mla_q_norm_g"] = lax.dynamic_slice_in_dim(gs["mla_q_norm_g"], chip * q_cols, q_cols, axis=1)

    dmod16 = jnp.transpose(dm8.reshape(N_DEV, Bl, L, 6 * D), (2, 0, 1, 3)).reshape(L, N_DEV * Bl, 6 * D)
    dmod_cols = lax.dynamic_slice_in_dim(dmod16, chip * C, C, axis=2)
    grad_ada_w = _ada_bwd(c_all, dmod_cols)

    gl = _grad_layouts(g, fox_b_f.shape[1], N_CHIP)
    big = [gl[n] for n in GATHERED]
    from_sibling = _pair_exchange(big, name="grad_pair_exchange")
    pair_sums = [_pair_add(a, r, c_idx, name=f"grad_pair_add_{n}") for n, a, r in zip(GATHERED, big, from_sibling)]
    pieces = _chip_exchange(pair_sums, name="grad_chip_exchange")
    halves = [_sum_pieces(p, name=f"grad_sum_pieces_{n}") for n, p in zip(GATHERED, pieces)]
    from_peer = _pair_swap(halves, name="grad_pair_swap")
    g_own = _natural_shard_grads(dict(zip(GATHERED, halves)), mla_w_dq.shape[-1])
    g_peer = _natural_shard_grads(dict(zip(GATHERED, from_peer)), mla_w_dq.shape[-1])
    grads = dict(gs)
    grads["ada_w"] = grad_ada_w
    grads["ada_b"] = grad_ada_b

    delta, new_m, new_v = {}, {}, {}
    for n, _ in BIG_WEIGHTS:
        grads[n], delta[n], new_m[n], new_v[n] = _adamw_halves(wts[n], g_own[n], g_peer[n], mom[n], var[n], c_idx,
                                                               name=f"adamw_{n}")
    for n in ("ada_w", "ada_b"):
        delta[n], new_m[n], new_v[n] = _adamw(wts[n], grads[n], mom[n], var[n], name=f"adamw_{n}")
    shard_small_shapes = {n: wts[n].shape for n in SMALL}
    packs = [jnp.pad(_small_rows({n: src[n] for n in SMALL}, D), ((0, Rs - 2 * L - 4), (0, 0)))
             for src in (wts, grads, mom, var)]
    for dst, rows in zip((delta, new_m, new_v), _adamw(*packs, name="adamw_small")):
        dst.update(_small_unrows(rows, shard_small_shapes))

    return (loss, grad_x.reshape(Bl, S, D), *[grads[n] for n in WEIGHT_ORDER], *[delta[n] for n in WEIGHT_ORDER],
            *[new_m[n] for n in WEIGHT_ORDER], *[new_v[n] for n in WEIGHT_ORDER])
```

```python
import functools

import numpy as np
import jax
import jax.numpy as jnp
from jax import lax
from jax.experimental import pallas as pl
from jax.experimental.pallas import tpu as pltpu

F32 = jnp.float32
BF16 = jnp.bfloat16
MESH_ID = pl.DeviceIdType.MESH

NORM_EPS = 1e-6
ROPE_THETA = 10000.0
HEAD_DIM = 64
ROPE_DIM = 32
KV_RANK = 128
FOX_EXTRA = 6
PAIR_Q = 256
PAIR_KV = 384
LANES = 128
ADAM_LR = 0.001
ADAM_B1 = 0.9
ADAM_B2 = 0.999
ADAM_EPS = 1e-08
ADAM_WD = 0.01
ADAM_STEP = 10
VMEM_LIMIT_V7X = 48 * 1024 * 1024
MM_VMEM_BUDGET = 36 * 1024 * 1024
NEG_BIG = -1e30
ATTN_UNROLL = 2

BIG_WEIGHTS = (("fox_w_in", 2), ("fox_w_out", 1), ("mla_w_dq", 1), ("mla_w_uq", 2), ("mla_w_dkv", 1),
               ("mla_w_ukv", 2), ("mla_w_out", 1), ("mlp_w1", 2), ("mlp_w2", 1))


def _cparams(sem=None):
    return pltpu.CompilerParams(dimension_semantics=sem, vmem_limit_bytes=VMEM_LIMIT_V7X)


def _tile(n, want):
    if n <= want:
        return n
    for t in range(want - want % LANES, 0, -LANES):
        if n % t == 0:
            return t
    raise ValueError((n, want))


def _mm(a, b, mode, *, name, out_dtypes=(F32,), epilogue=None, extras=(), rowvecs=(), tables=(),
        seq=None, a_off=0, a_sz=None, b_layer=None, out_stack=None, out_split=0, tm=1024, tn=1024, tk=2048):
    b_rows, b_cols = b.shape[-2], b.shape[-1]
    n_split = b.shape[1] if b.ndim == 4 else 1
    assert mode in ("nn", "nt")
    if mode == "nn":
        M, K, N = a.shape[0], b_rows, b_cols * n_split
    else:
        M, K, N = a.shape[0], b_cols * n_split, b_rows
    assert a_sz is None or a_sz == K
    tm = _tile(seq if rowvecs else M, tm)
    n_piece = N // max(out_split, n_split if mode == "nn" else 1, 1)
    tn = _tile(n_piece, tn)
    tk = _tile(K // (n_split if mode == "nt" else 1), tk)
    ne, nr, nt_ = len(extras), len(rowvecs), len(tables)
    no = len(out_dtypes)

    def vmem_estimate():
        blocks = tm * tk * a.dtype.itemsize + tk * tn * b.dtype.itemsize
        blocks += tm * tn * (sum(e.dtype.itemsize for e in extras) + sum(jnp.dtype(d).itemsize for d in out_dtypes))
        return 2 * blocks + 2 * tm * tn * 4

    while vmem_estimate() > MM_VMEM_BUDGET and max(tm, tn) > 256:
        if tn >= tm:
            tn //= 2
        else:
            tm //= 2
    nk = K // tk

    assert a_off % tk == 0
    a_spec = pl.BlockSpec((tm, tk), lambda i, j, k: (i, k + a_off // tk))
    dims = (((1,), (0,)), ((), ())) if mode == "nn" else (((1,), (1,)), ((), ()))
    lead = () if b.ndim == 2 else (b_layer,)
    sq = (None,) * (b.ndim - 2)
    if mode == "nt":
        kb = b_cols // tk
        if b.ndim == 4:
            b_spec = pl.BlockSpec(sq + (tn, tk), lambda i, j, k: lead + (k // kb, j, k % kb))
        else:
            b_spec = pl.BlockSpec(sq + (tn, tk), lambda i, j, k: lead + (j, k))
    else:
        nb = b_cols // tn
        if b.ndim == 4:
            b_spec = pl.BlockSpec(sq + (tk, tn), lambda i, j, k: lead + (j // nb, k, j % nb))
        else:
            b_spec = pl.BlockSpec(sq + (tk, tn), lambda i, j, k: lead + (k, j))
    in_specs = [a_spec, b_spec]
    in_specs += [pl.BlockSpec((tm, tn), lambda i, j, k: (i, j)) for _ in extras]
    if rowvecs:
        assert seq % tm == 0
        per = seq // tm
        in_specs += [pl.BlockSpec((None, 1, tn), lambda i, j, k: (i // per, 0, j)) for _ in rowvecs]
    in_specs += [pl.BlockSpec((tm, LANES), lambda i, j, k: (i, 0)) for _ in tables]
    operands = [a, b, *extras, *rowvecs, *tables]
    aliases = {}
    if out_stack is None:
        out_specs = [pl.BlockSpec((tm, tn), lambda i, j, k: (i, j)) for _ in out_dtypes]
        out_shape = [jax.ShapeDtypeStruct((M, N), d) for d in out_dtypes]
    else:
        prev, layer, n_layers = out_stack
        assert no == 1
        if out_split:
            ob = n_piece // tn
            out_specs = [pl.BlockSpec((None, None, tm, tn), lambda i, j, k: (layer, j // ob, i, j % ob))]
            out_shape = [jax.ShapeDtypeStruct((n_layers, out_split, M, n_piece), out_dtypes[0])]
        else:
            out_specs = [pl.BlockSpec((None, tm, tn), lambda i, j, k: (layer, i, j))]
            out_shape = [jax.ShapeDtypeStruct((n_layers, M, N), out_dtypes[0])]
        if prev is not None:
            in_specs.append(pl.BlockSpec(memory_space=pl.ANY))
            aliases = {len(operands): 0}
            operands.append(prev)
    n_in = len(operands)

    def body(*refs):
        a_ref, b_ref = refs[0], refs[1]
        side = refs[2:2 + ne + nr + nt_]
        outs = refs[n_in:n_in + no]

        def finish(acc):
            res = (acc,) if epilogue is None else epilogue(acc, *[r[...] for r in side])
            for o_ref, r in zip(outs, res):
                o_ref[...] = r.astype(o_ref.dtype)

        part = lax.dot_general(a_ref[...].astype(BF16), b_ref[...].astype(BF16), dims,
                               preferred_element_type=F32)
        if nk == 1:
            finish(part)
        else:
            acc_ref = refs[-1]
            k = pl.program_id(2)

            @pl.when(k == 0)
            def _():
                acc_ref[...] = part

            @pl.when(k > 0)
            def _():
                acc_ref[...] += part

            @pl.when(k == nk - 1)
            def _():
                finish(acc_ref[...])

    res = pl.pallas_call(
        body, name=name, grid=(M // tm, N // tn, nk), in_specs=in_specs, out_specs=out_specs,
        out_shape=out_shape, scratch_shapes=[pltpu.VMEM((tm, tn), F32)] if nk > 1 else [],
        input_output_aliases=aliases,
        compiler_params=_cparams(("parallel", "parallel", "arbitrary")),
    )(*operands)
    return res[0] if no == 1 else tuple(res)


def _rope128(x, cos_t, sin_s):
    lane = lax.broadcasted_iota(jnp.int32, x.shape, 1)
    first = (lane % ROPE_DIM) < (ROPE_DIM // 2)
    swapped = jnp.where(first, pltpu.roll(x, LANES - ROPE_DIM // 2, 1), pltpu.roll(x, ROPE_DIM // 2, 1))
    return x * cos_t + swapped * sin_s


def _rope_pairs(acc, cos_t, sin_s, sign):
    parts = []
    for p in range(acc.shape[1] // PAIR_Q):
        parts.append(acc[:, p * PAIR_Q:p * PAIR_Q + LANES])
        parts.append(_rope128(acc[:, p * PAIR_Q + LANES:(p + 1) * PAIR_Q], cos_t, sign * sin_s))
    return jnp.concatenate(parts, axis=1)


def _rope_tables(pos_f, inv_freq_row, sign_row):
    T = pos_f.shape[0]
    tt = _tile(T, 512)

    def body(p_ref, f_ref, s_ref, cos_ref, sin_ref):
        ang = p_ref[...] * f_ref[...]
        cos_ref[...] = jnp.cos(ang)
        sin_ref[...] = jnp.sin(ang) * s_ref[...]

    return pl.pallas_call(
        body, name="rope_tables", grid=(T // tt,),
        in_specs=[pl.BlockSpec((tt, 1), lambda i: (i, 0)), pl.BlockSpec((1, LANES), lambda i: (0, 0)),
                  pl.BlockSpec((1, LANES), lambda i: (0, 0))],
        out_specs=[pl.BlockSpec((tt, LANES), lambda i: (i, 0))] * 2,
        out_shape=[jax.ShapeDtypeStruct((T, LANES), F32)] * 2,
        compiler_params=_cparams(("parallel",)),
    )(pos_f, inv_freq_row, sign_row)


def _unrope(dqx, cos_t, sin_s):
    T, W = dqx.shape
    tt = _tile(T, 512)

    def body(d_ref, c_ref, s_ref, o_ref):
        o_ref[...] = _rope_pairs(d_ref[...].astype(F32), c_ref[...], s_ref[...], -1.0).astype(BF16)

    return pl.pallas_call(
        body, name="mla_unrope", grid=(T // tt,),
        in_specs=[pl.BlockSpec((tt, W), lambda i: (i, 0)), pl.BlockSpec((tt, LANES), lambda i: (i, 0)),
                  pl.BlockSpec((tt, LANES), lambda i: (i, 0))],
        out_specs=pl.BlockSpec((tt, W), lambda i: (i, 0)),
        out_shape=jax.ShapeDtypeStruct((T, W), BF16),
        compiler_params=_cparams(("parallel",)),
    )(dqx, cos_t, sin_s)


def _row_specs(tt, D, per, n):
    return [pl.BlockSpec((None, 1, D), lambda i: (i // per, 0, 0)) for _ in range(n)]


def _norm_mod(x, gain, sc, sh, *, S, name):
    T, D = x.shape
    tt = _tile(S, 512)
    per = S // tt

    def body(x_ref, g_ref, sc_ref, sh_ref, h_ref):
        xv = x_ref[...]
        r = lax.rsqrt(jnp.mean(xv * xv, axis=-1, keepdims=True) + NORM_EPS)
        h_ref[...] = ((xv * r) * g_ref[...] * (1.0 + sc_ref[...]) + sh_ref[...]).astype(BF16)

    return pl.pallas_call(
        body, name=name, grid=(T // tt,),
        in_specs=[pl.BlockSpec((tt, D), lambda i: (i, 0)), pl.BlockSpec((1, D), lambda i: (0, 0))]
        + _row_specs(tt, D, per, 2),
        out_specs=pl.BlockSpec((tt, D), lambda i: (i, 0)),
        out_shape=jax.ShapeDtypeStruct((T, D), BF16),
        compiler_params=_cparams(("parallel",)),
    )(x, gain, sc, sh)


def _norm_mod_bwd(x, dh, dres, gain, sc, *, S, name):
    T, D = x.shape
    B = T // S
    tt = _tile(S, 512)
    per = S // tt

    def body(x_ref, dh_ref, dres_ref, g_ref, sc_ref, dx_ref, dsh_ref, dsc_ref, dg_ref):
        i = pl.program_id(0)
        xv = x_ref[...]
        dhv = dh_ref[...].astype(F32)
        r = lax.rsqrt(jnp.mean(xv * xv, axis=-1, keepdims=True) + NORM_EPS)
        n = xv * r
        g = g_ref[...]
        one_sc = 1.0 + sc_ref[...]
        dn = dhv * (g * one_sc)
        dx_ref[...] = dres_ref[...] + r * (dn - n * jnp.mean(dn * n, axis=-1, keepdims=True))
        dhn = dhv * n

        @pl.when(i % per == 0)
        def _():
            dsh_ref[...] = jnp.zeros_like(dsh_ref)
            dsc_ref[...] = jnp.zeros_like(dsc_ref)

        @pl.when(i == 0)
        def _():
            dg_ref[...] = jnp.zeros_like(dg_ref)

        dsh_ref[...] += jnp.sum(dhv, axis=0, keepdims=True)
        dsc_ref[...] += jnp.sum(dhn, axis=0, keepdims=True) * g
        dg_ref[...] += jnp.sum(dhn, axis=0, keepdims=True) * one_sc

    return pl.pallas_call(
        body, name=name, grid=(T // tt,),
        in_specs=[pl.BlockSpec((tt, D), lambda i: (i, 0))] * 3 + [pl.BlockSpec((1, D), lambda i: (0, 0))]
        + _row_specs(tt, D, per, 1),
        out_specs=[pl.BlockSpec((tt, D), lambda i: (i, 0))] + _row_specs(tt, D, per, 2)
        + [pl.BlockSpec((1, D), lambda i: (0, 0))],
        out_shape=[jax.ShapeDtypeStruct((T, D), F32), jax.ShapeDtypeStruct((B, 1, D), F32),
                   jax.ShapeDtypeStruct((B, 1, D), F32), jax.ShapeDtypeStruct((1, D), F32)],
        compiler_params=_cparams(("arbitrary",)),
    )(x, dh, dres, gain, sc)


def _gate_bwd(dx, y, g, *, S, name):
    T, D = dx.shape
    B = T // S
    tt = _tile(S, 512)
    per = S // tt

    def body(dx_ref, y_ref, g_ref, dy_ref, dg_ref):
        i = pl.program_id(0)
        dxv = dx_ref[...]
        dy_ref[...] = (dxv * g_ref[...]).astype(BF16)

        @pl.when(i % per == 0)
        def _():
            dg_ref[...] = jnp.zeros_like(dg_ref)

        dg_ref[...] += jnp.sum(dxv * y_ref[...], axis=0, keepdims=True)

    return pl.pallas_call(
        body, name=name, grid=(T // tt,),
        in_specs=[pl.BlockSpec((tt, D), lambda i: (i, 0))] * 2 + _row_specs(tt, D, per, 1),
        out_specs=[pl.BlockSpec((tt, D), lambda i: (i, 0))] + _row_specs(tt, D, per, 1),
        out_shape=[jax.ShapeDtypeStruct((T, D), BF16), jax.ShapeDtypeStruct((B, 1, D), F32)],
        compiler_params=_cparams(("arbitrary",)),
    )(dx, y, g)


def _final_loss(x, target, gain):
    T, D = x.shape
    tt = _tile(T, 512)

    def body(x_ref, t_ref, g_ref, dx_ref, dg_ref, loss_ref):
        i = pl.program_id(0)
        xv = x_ref[...]
        r = lax.rsqrt(jnp.mean(xv * xv, axis=-1, keepdims=True) + NORM_EPS)
        n = xv * r
        g = g_ref[...]
        err = n * g - t_ref[...]
        dy = err * (1.0 / D)
        dn = dy * g
        dx_ref[...] = r * (dn - n * jnp.mean(dn * n, axis=-1, keepdims=True))

        @pl.when(i == 0)
        def _():
            dg_ref[...] = jnp.zeros_like(dg_ref)
            loss_ref[...] = jnp.zeros_like(loss_ref)

        dg_ref[...] += jnp.sum(dy * n, axis=0, keepdims=True)
        loss_ref[...] += jnp.sum(jnp.sum(err * err, axis=-1, keepdims=True), axis=0, keepdims=True) * (0.5 / D)

    return pl.pallas_call(
        body, name="final_loss", grid=(T // tt,),
        in_specs=[pl.BlockSpec((tt, D), lambda i: (i, 0))] * 2 + [pl.BlockSpec((1, D), lambda i: (0, 0))],
        out_specs=[pl.BlockSpec((tt, D), lambda i: (i, 0)), pl.BlockSpec((1, D), lambda i: (0, 0)),
                   pl.BlockSpec((1, LANES), lambda i: (0, 0))],
        out_shape=[jax.ShapeDtypeStruct((T, D), F32), jax.ShapeDtypeStruct((1, D), F32),
                   jax.ShapeDtypeStruct((1, LANES), F32)],
        compiler_params=_cparams(("arbitrary",)),
    )(x, target, gain)


def _head_masks(ew):
    lane = lax.broadcasted_iota(jnp.int32, (1, PAIR_Q), 1)
    m0 = (lane < HEAD_DIM) | ((lane >= LANES) & (lane < LANES + ew))
    m1 = ((lane >= HEAD_DIM) & (lane < LANES)) | ((lane >= LANES + ew) & (lane < LANES + 2 * ew))
    return m0, m1


def _dot_nt(a, b):
    return lax.dot_general(a, b, (((1,), (1,)), ((), ())), preferred_element_type=F32)


def _dot_tn(a, b):
    return lax.dot_general(a, b, (((0,), (0,)), ((), ())), preferred_element_type=F32)


def _lane_halves(x, op):
    acc = x[:, 0:LANES]
    for g in range(1, x.shape[1] // LANES):
        acc = op(acc, x[:, g * LANES:(g + 1) * LANES])
    return acc


def _head_rows(cols_lane_replicated):
    t = cols_lane_replicated.T
    sub = lax.broadcasted_iota(jnp.int32, (8, t.shape[1]), 0)
    return jnp.where(sub == 1, t[HEAD_DIM:HEAD_DIM + 8], t[0:8])


def _attn_fwd(qx, kvx, *, S, scale, ew, name):
    T = qx.shape[0]
    P = qx.shape[1] // PAIR_Q
    B = T // S
    tq = _tile(S, 256)
    nq = S // tq
    assert nq % ATTN_UNROLL == 0

    def body(q_ref, kv_ref, o_ref, lse_ref, m_sc, l_sc, acc_sc):
        qi = pl.program_id(2)
        q = q_ref[...]
        masks = _head_masks(ew)
        qh = [jnp.where(m, q, jnp.zeros_like(q)) for m in masks]

        def logits(h, k, kj):
            s = _dot_nt(qh[h], k)
            if scale != 1.0:
                s = s * scale
            row = lax.broadcasted_iota(jnp.int32, s.shape, 0)
            col = lax.broadcasted_iota(jnp.int32, s.shape, 1)
            return jnp.where(col - row <= (qi - kj) * tq, s, NEG_BIG)

        def sweep(step):
            def loop_body(t, carry):
                for u in range(ATTN_UNROLL):
                    step(t * ATTN_UNROLL + u)
                return carry

            lax.fori_loop(0, (qi + ATTN_UNROLL) // ATTN_UNROLL, loop_body, 0)

        def max_step(kj):
            k = kv_ref[pl.ds(pl.multiple_of(kj * tq, tq), tq), 0:PAIR_Q]
            for h in range(2):
                m_sc[h] = jnp.maximum(m_sc[h], _lane_halves(logits(h, k, kj), jnp.maximum))

        def sum_step(kj):
            rows = pl.ds(pl.multiple_of(kj * tq, tq), tq)
            k = kv_ref[rows, 0:PAIR_Q]
            v = kv_ref[rows, PAIR_Q:PAIR_KV]
            for h in range(2):
                s = logits(h, k, kj)
                m = m_sc[h]
                p = jnp.concatenate([jnp.exp(s[:, g * LANES:(g + 1) * LANES] - m) for g in range(tq // LANES)], axis=1)
                l_sc[h] += _lane_halves(p, jnp.add)
                acc_sc[h] += jnp.dot(p.astype(BF16), v, preferred_element_type=F32)

        m_sc[...] = jnp.full(m_sc.shape, NEG_BIG, F32)
        sweep(max_step)
        for h in range(2):
            m_sc[h] = jnp.broadcast_to(jnp.max(m_sc[h], axis=1, keepdims=True), (tq, LANES))
        l_sc[...] = jnp.zeros_like(l_sc)
        acc_sc[...] = jnp.zeros_like(acc_sc)
        sweep(sum_step)
        lane = lax.broadcasted_iota(jnp.int32, (tq, LANES), 1)
        lo = lane < HEAD_DIM
        l = [jnp.sum(l_sc[h], axis=1, keepdims=True) for h in range(2)]
        o_ref[...] = jnp.where(lo, acc_sc[0] / l[0], acc_sc[1] / l[1]).astype(BF16)
        lse_ref[...] = _head_rows(jnp.where(lo, m_sc[0] + jnp.log(l[0]), m_sc[1] + jnp.log(l[1])))

    return pl.pallas_call(
        body, name=name, grid=(B, P, nq),
        in_specs=[pl.BlockSpec((tq, PAIR_Q), lambda b, p, i: (b * nq + i, p)),
                  pl.BlockSpec((S, PAIR_KV), lambda b, p, i: (b, p))],
        out_specs=[pl.BlockSpec((tq, LANES), lambda b, p, i: (b * nq + i, p)),
                   pl.BlockSpec((None, None, 8, tq), lambda b, p, i: (b * nq + i, p, 0, 0))],
        out_shape=[jax.ShapeDtypeStruct((T, P * LANES), BF16), jax.ShapeDtypeStruct((T // tq, P, 8, tq), F32)],
        scratch_shapes=[pltpu.VMEM((2, tq, LANES), F32)] * 3,
        compiler_params=_cparams(("parallel", "parallel", "arbitrary")),
    )(qx, kvx)


def _attn_bwd(qx, kvx, o, lse, do, *, S, scale, ew, name, bias_grad=False):
    T = qx.shape[0]
    P = qx.shape[1] // PAIR_Q
    B = T // S
    tq = _tile(S, 256)
    nq = S // tq
    assert nq % ATTN_UNROLL == 0

    def body(q_ref, kv_ref, o_ref, lse_ref, do_ref, dq_ref, dkv_ref, *rest):
        kj = pl.program_id(2)
        if bias_grad:
            csum_ref, rsum_ref, dq_sc, delta_sc, dk_sc, dv_sc, cs_sc = rest
            cs_sc[...] = jnp.zeros_like(cs_sc)

            @pl.when(kj == 0)
            def _():
                rsum_ref[...] = jnp.zeros_like(rsum_ref)
        else:
            dq_sc, delta_sc, dk_sc, dv_sc = rest
        masks = _head_masks(ew)
        lane = lax.broadcasted_iota(jnp.int32, (tq, LANES), 1)
        lo = lane < HEAD_DIM
        vmask = [lo, jnp.logical_not(lo)]

        @pl.when(kj == 0)
        def _():
            dq_sc[...] = jnp.zeros_like(dq_sc)
            for c in range(nq):
                rows = pl.ds(c * tq, tq)
                x = do_ref[rows, :].astype(F32) * o_ref[rows, :].astype(F32)
                r0 = jnp.sum(jnp.where(lo, x, 0.0), axis=1, keepdims=True)
                r1 = jnp.sum(jnp.where(lo, 0.0, x), axis=1, keepdims=True)
                delta_sc[c] = _head_rows(jnp.where(lo, r0, r1))

        k = kv_ref[:, 0:PAIR_Q]
        v = kv_ref[:, PAIR_Q:PAIR_KV]
        kh = [jnp.where(m, k, jnp.zeros_like(k)) for m in masks]
        vh = [jnp.where(m, v, jnp.zeros_like(v)) for m in vmask]
        dk_sc[...] = jnp.zeros_like(dk_sc)
        dv_sc[...] = jnp.zeros_like(dv_sc)

        def step(qi):
            rows = pl.ds(pl.multiple_of(qi * tq, tq), tq)
            q = q_ref[rows, :]
            dov = do_ref[rows, :]
            lse8 = lse_ref[qi]
            dl8 = delta_sc[qi]
            for h in range(2):
                st = _dot_nt(kh[h], q)
                if scale != 1.0:
                    st = st * scale
                key = lax.broadcasted_iota(jnp.int32, st.shape, 0)
                qry = lax.broadcasted_iota(jnp.int32, st.shape, 1)
                st = jnp.where(key - qry <= (qi - kj) * tq, st, NEG_BIG)
                pt = jnp.exp(st - lse8[h:h + 1, :])
                dpt = _dot_nt(vh[h], dov)
                dst = pt * (dpt - dl8[h:h + 1, :])
                if bias_grad:
                    cs_sc[h] += _lane_halves(dst, jnp.add)
                    rsum_ref[qi, h:h + 1, :] += jnp.sum(dst, axis=0, keepdims=True)
                if scale != 1.0:
                    dst = dst * scale
                ptb = pt.astype(BF16)
                dstb = dst.astype(BF16)
                dv_sc[h] += jnp.dot(ptb, dov, preferred_element_type=F32)
                dk_sc[h] += jnp.dot(dstb, q, preferred_element_type=F32)
                dq_sc[rows, :] += _dot_tn(dstb, kh[h])

        def loop_body(t, carry):
            for u in range(ATTN_UNROLL):
                step(t * ATTN_UNROLL + u)
            return carry

        lax.fori_loop(kj // ATTN_UNROLL, nq // ATTN_UNROLL, loop_body, 0)
        dkv_ref[:, 0:PAIR_Q] = (jnp.where(masks[0], dk_sc[0], 0.0) + jnp.where(masks[1], dk_sc[1], 0.0)).astype(BF16)
        dkv_ref[:, PAIR_Q:PAIR_KV] = jnp.where(lo, dv_sc[0], dv_sc[1]).astype(BF16)
        if bias_grad:
            csum_ref[...] = jnp.where(lo, jnp.sum(cs_sc[0], axis=1, keepdims=True),
                                      jnp.sum(cs_sc[1], axis=1, keepdims=True))

        @pl.when(kj == nq - 1)
        def _():
            dq_ref[...] = dq_sc[...].astype(BF16)

    rows_spec = pl.BlockSpec((nq, None, 8, tq), lambda b, p, j: (b, p, 0, 0))
    out_specs = [pl.BlockSpec((S, PAIR_Q), lambda b, p, j: (b, p)),
                 pl.BlockSpec((tq, PAIR_KV), lambda b, p, j: (b * nq + j, p))]
    out_shape = [jax.ShapeDtypeStruct((T, P * PAIR_Q), BF16), jax.ShapeDtypeStruct((T, P * PAIR_KV), BF16)]
    scratch = [pltpu.VMEM((S, PAIR_Q), F32), pltpu.VMEM((nq, 8, tq), F32),
               pltpu.VMEM((2, tq, PAIR_Q), F32), pltpu.VMEM((2, tq, LANES), F32)]
    if bias_grad:
        out_specs += [pl.BlockSpec((tq, LANES), lambda b, p, j: (b * nq + j, p)), rows_spec]
        out_shape += [jax.ShapeDtypeStruct((T, P * LANES), F32), jax.ShapeDtypeStruct((T // tq, P, 8, tq), F32)]
        scratch.append(pltpu.VMEM((2, tq, LANES), F32))
    return pl.pallas_call(
        body, name=name, grid=(B, P, nq),
        in_specs=[pl.BlockSpec((S, PAIR_Q), lambda b, p, j: (b, p)),
                  pl.BlockSpec((tq, PAIR_KV), lambda b, p, j: (b * nq + j, p)),
                  pl.BlockSpec((S, LANES), lambda b, p, j: (b, p)), rows_spec,
                  pl.BlockSpec((S, LANES), lambda b, p, j: (b, p))],
        out_specs=out_specs, out_shape=out_shape, scratch_shapes=scratch,
        compiler_params=_cparams(("parallel", "parallel", "arbitrary")),
    )(qx, kvx, o, lse, do)


def _fox_consts(P):
    H = 2 * P
    eq = np.zeros((3 * LANES, P * LANES), np.float32)
    ek = np.zeros((3 * LANES, P * LANES), np.float32)
    ones_q = np.zeros((1, P * LANES), np.float32)
    ones_k = np.zeros((1, P * LANES), np.float32)
    for h in range(H):
        base = (h // 2) * LANES + FOX_EXTRA * (h % 2)
        for part in range(3):
            eq[part * LANES + h, base + part] = 1.0
            ones_q[0, base + 3 + part] = 1.0
            ones_k[0, base + part] = 1.0
            ek[part * LANES + h, base + 3 + part] = -1.0
    return eq, ek, ones_q, ones_k


def _split3(f):
    hi = f.astype(BF16)
    r = f - hi.astype(F32)
    mid = r.astype(BF16)
    lo = (r - mid.astype(F32)).astype(BF16)
    return hi, mid, lo


def _tri_sum(tri, x):
    hi, mid, lo = _split3(x)
    return (jnp.dot(tri, hi, preferred_element_type=F32) + jnp.dot(tri, mid, preferred_element_type=F32)
            + jnp.dot(tri, lo, preferred_element_type=F32))


def _log1p_pos(e):
    return jnp.where(e < 0.01, e * (1.0 - e * (0.5 - e * (1.0 / 3.0))), jnp.log(1.0 + e))


def _fox_prep(qkv, fl, b_row, *, S, D, name):
    T = qkv.shape[0]
    P = D // LANES
    B = T // S
    tt = _tile(S, 256)
    per = S // tt
    eq, ek, ones_q, ones_k = _fox_consts(P)
    q_scale = HEAD_DIM ** -0.5

    def body(q_ref, k_ref, v_ref, fl_ref, b_ref, eq_ref, ek_ref, oq_ref, ok_ref, qx_ref, kvx_ref, carry):
        i = pl.program_id(1)

        @pl.when(i == 0)
        def _():
            carry[...] = jnp.zeros_like(carry)

        z = fl_ref[...] + b_ref[...]
        logf = jnp.minimum(z, 0.0) - _log1p_pos(jnp.exp(-jnp.abs(z)))
        row = lax.broadcasted_iota(jnp.int32, (tt, tt), 0)
        col = lax.broadcasted_iota(jnp.int32, (tt, tt), 1)
        tri = (col <= row).astype(BF16)
        f = _tri_sum(tri, logf) + carry[...]
        carry[...] = f[tt - 1:tt, :]
        parts = jnp.concatenate(_split3(f), axis=1)
        xq = jnp.dot(parts, eq_ref[...], preferred_element_type=F32) + oq_ref[...]
        xk = jnp.dot(parts, ek_ref[...], preferred_element_type=F32) + ok_ref[...]
        for p in range(P):
            c = slice(p * LANES, (p + 1) * LANES)
            qx_ref[:, p * PAIR_Q:p * PAIR_Q + LANES] = (q_ref[:, c].astype(F32) * q_scale).astype(BF16)
            qx_ref[:, p * PAIR_Q + LANES:(p + 1) * PAIR_Q] = xq[:, c].astype(BF16)
            kvx_ref[:, p * PAIR_KV:p * PAIR_KV + LANES] = k_ref[:, c]
            kvx_ref[:, p * PAIR_KV + LANES:p * PAIR_KV + PAIR_Q] = xk[:, c].astype(BF16)
            kvx_ref[:, p * PAIR_KV + PAIR_Q:(p + 1) * PAIR_KV] = v_ref[:, c]

    tok = lambda b, i: (b * per + i, 0)
    const = lambda b, i: (0, 0)
    return pl.pallas_call(
        body, name=name, grid=(B, per),
        in_specs=[pl.BlockSpec((tt, D), lambda b, i: (b * per + i, 0)),
                  pl.BlockSpec((tt, D), lambda b, i: (b * per + i, 1)),
                  pl.BlockSpec((tt, D), lambda b, i: (b * per + i, 2)),
                  pl.BlockSpec((tt, LANES), tok), pl.BlockSpec((1, LANES), const),
                  pl.BlockSpec(eq.shape, const), pl.BlockSpec(ek.shape, const),
                  pl.BlockSpec(ones_q.shape, const), pl.BlockSpec(ones_k.shape, const)],
        out_specs=[pl.BlockSpec((tt, P * PAIR_Q), tok), pl.BlockSpec((tt, P * PAIR_KV), tok)],
        out_shape=[jax.ShapeDtypeStruct((T, P * PAIR_Q), BF16), jax.ShapeDtypeStruct((T, P * PAIR_KV), BF16)],
        scratch_shapes=[pltpu.VMEM((1, LANES), F32)],
        compiler_params=_cparams(("arbitrary", "arbitrary")),
    )(qkv, qkv, qkv, fl, b_row, jnp.asarray(eq, BF16), jnp.asarray(ek, BF16), jnp.asarray(ones_q), jnp.asarray(ones_k))


def _fox_unprep(dqx, dkvx, csum, rsum, fl, b_row, *, S, D, name):
    T = dqx.shape[0]
    P = D // LANES
    B = T // S
    tt = _tile(S, 256)
    per = S // tt
    q_scale = HEAD_DIM ** -0.5

    def body(dq_ref, dkv_ref, cs_ref, rs_ref, fl_ref, b_ref, dqkv_ref, dfl_ref, db_ref, carry):
        b = pl.program_id(0)
        i = pl.program_id(1)

        @pl.when(i == 0)
        def _():
            carry[...] = jnp.zeros_like(carry)

        @pl.when((i == 0) & (b == 0))
        def _():
            db_ref[...] = jnp.zeros_like(db_ref)

        df = rs_ref[...] - cs_ref[...]
        for p in range(P):
            rq = slice(p * LANES, (p + 1) * LANES)
            dqkv_ref[:, rq] = (dq_ref[:, p * PAIR_Q:p * PAIR_Q + LANES].astype(F32) * q_scale).astype(BF16)
            dqkv_ref[:, D + p * LANES:D + (p + 1) * LANES] = dkv_ref[:, p * PAIR_KV:p * PAIR_KV + LANES]
            dqkv_ref[:, 2 * D + p * LANES:2 * D + (p + 1) * LANES] = dkv_ref[:, p * PAIR_KV + PAIR_Q:(p + 1) * PAIR_KV]
        row = lax.broadcasted_iota(jnp.int32, (tt, tt), 0)
        col = lax.broadcasted_iota(jnp.int32, (tt, tt), 1)
        tri = (col >= row).astype(BF16)
        dlogf = _tri_sum(tri, df) + carry[...]
        carry[...] = dlogf[0:1, :]
        z = fl_ref[...] + b_ref[...]
        e = jnp.exp(-jnp.abs(z))
        sig_neg = jnp.where(z >= 0.0, e, 1.0) / (1.0 + e)
        dfl = dlogf * sig_neg
        dfl_ref[...] = dfl.astype(BF16)
        db_ref[...] += jnp.sum(dfl, axis=0, keepdims=True)

    rev = lambda b, i: (b * per + per - 1 - i, 0)
    const = lambda b, i: (0, 0)
    return pl.pallas_call(
        body, name=name, grid=(B, per),
        in_specs=[pl.BlockSpec((tt, P * PAIR_Q), rev), pl.BlockSpec((tt, P * PAIR_KV), rev),
                  pl.BlockSpec((tt, LANES), rev), pl.BlockSpec((tt, LANES), rev), pl.BlockSpec((tt, LANES), rev),
                  pl.BlockSpec((1, LANES), const)],
        out_specs=[pl.BlockSpec((tt, 3 * D), rev), pl.BlockSpec((tt, LANES), rev), pl.BlockSpec((1, LANES), const)],
        out_shape=[jax.ShapeDtypeStruct((T, 3 * D), BF16), jax.ShapeDtypeStruct((T, LANES), BF16),
                   jax.ShapeDtypeStruct((1, LANES), F32)],
        scratch_shapes=[pltpu.VMEM((1, LANES), F32)],
        compiler_params=_cparams(("arbitrary", "arbitrary")),
    )(dqx, dkvx, csum, rsum, fl, b_row)


def _rms(x):
    r = lax.rsqrt(jnp.mean(x * x, axis=-1, keepdims=True) + NORM_EPS)
    return x * r, r


def _mla_mid(lat, gq, gkv, cos_t, sin_s, *, name):
    T, W = lat.shape
    Rq = W - 2 * LANES
    tt = _tile(T, 512)

    def body(l_ref, gq_ref, gkv_ref, c_ref, s_ref, o_ref):
        nq, _ = _rms(l_ref[:, 0:Rq])
        nkv, _ = _rms(l_ref[:, Rq:Rq + LANES])
        o_ref[:, 0:Rq] = (nq * gq_ref[...]).astype(BF16)
        o_ref[:, Rq:Rq + LANES] = (nkv * gkv_ref[...]).astype(BF16)
        o_ref[:, Rq + LANES:W] = _rope128(l_ref[:, Rq + LANES:W], c_ref[...], s_ref[...]).astype(BF16)

    return pl.pallas_call(
        body, name=name, grid=(T // tt,),
        in_specs=[pl.BlockSpec((tt, W), lambda i: (i, 0)), pl.BlockSpec((1, Rq), lambda i: (0, 0)),
                  pl.BlockSpec((1, LANES), lambda i: (0, 0)), pl.BlockSpec((tt, LANES), lambda i: (i, 0)),
                  pl.BlockSpec((tt, LANES), lambda i: (i, 0))],
        out_specs=pl.BlockSpec((tt, W), lambda i: (i, 0)),
        out_shape=jax.ShapeDtypeStruct((T, W), BF16),
        compiler_params=_cparams(("parallel",)),
    )(lat, gq, gkv, cos_t, sin_s)


def _mla_mid_bwd(lat, dcq, dckr, gq, gkv, cos_t, sin_s, *, name):
    T, W = lat.shape
    Rq = W - 2 * LANES
    tt = _tile(T, 512)

    def norm_bwd(x, dy, g):
        n, r = _rms(x)
        dn = dy * g
        return r * (dn - n * jnp.mean(dn * n, axis=-1, keepdims=True)), jnp.sum(dy * n, axis=0, keepdims=True)

    def body(l_ref, dq_ref, dk_ref, gq_ref, gkv_ref, c_ref, s_ref, o_ref, dgq_ref, dgkv_ref):
        i = pl.program_id(0)

        @pl.when(i == 0)
        def _():
            dgq_ref[...] = jnp.zeros_like(dgq_ref)
            dgkv_ref[...] = jnp.zeros_like(dgkv_ref)

        dxq, dgq = norm_bwd(l_ref[:, 0:Rq], dq_ref[...], gq_ref[...])
        dxkv, dgkv = norm_bwd(l_ref[:, Rq:Rq + LANES], dk_ref[:, 0:LANES], gkv_ref[...])
        o_ref[:, 0:Rq] = dxq.astype(BF16)
        o_ref[:, Rq:Rq + LANES] = dxkv.astype(BF16)
        o_ref[:, Rq + LANES:W] = _rope128(dk_ref[:, LANES:2 * LANES], c_ref[...], -s_ref[...]).astype(BF16)
        dgq_ref[...] += dgq
        dgkv_ref[...] += dgkv

    return pl.pallas_call(
        body, name=name, grid=(T // tt,),
        in_specs=[pl.BlockSpec((tt, W), lambda i: (i, 0)), pl.BlockSpec((tt, Rq), lambda i: (i, 0)),
                  pl.BlockSpec((tt, 2 * LANES), lambda i: (i, 0)), pl.BlockSpec((1, Rq), lambda i: (0, 0)),
                  pl.BlockSpec((1, LANES), lambda i: (0, 0)), pl.BlockSpec((tt, LANES), lambda i: (i, 0)),
                  pl.BlockSpec((tt, LANES), lambda i: (i, 0))],
        out_specs=[pl.BlockSpec((tt, W), lambda i: (i, 0)), pl.BlockSpec((1, Rq), lambda i: (0, 0)),
                   pl.BlockSpec((1, LANES), lambda i: (0, 0))],
        out_shape=[jax.ShapeDtypeStruct((T, W), BF16), jax.ShapeDtypeStruct((1, Rq), F32),
                   jax.ShapeDtypeStruct((1, LANES), F32)],
        compiler_params=_cparams(("arbitrary",)),
    )(lat, dcq, dckr, gq, gkv, cos_t, sin_s)


def _uq_to_pairs(w):
    Rq = w.shape[0]
    P = w.shape[1] // (2 * (HEAD_DIM + ROPE_DIM))
    w4 = w.reshape(Rq, P, 2, HEAD_DIM + ROPE_DIM)
    nope = w4[..., :HEAD_DIM].reshape(Rq, P, 2 * HEAD_DIM)
    rope = w4[..., HEAD_DIM:].reshape(Rq, P, 2 * ROPE_DIM)
    pad = jnp.zeros((Rq, P, PAIR_Q - 2 * HEAD_DIM - 2 * ROPE_DIM), w.dtype)
    return jnp.concatenate([nope, rope, pad], axis=-1).reshape(Rq, P * PAIR_Q)


def _uq_from_pairs(g):
    Rq = g.shape[0]
    P = g.shape[1] // PAIR_Q
    g3 = g.reshape(Rq, P, PAIR_Q)
    nope = g3[..., :2 * HEAD_DIM].reshape(Rq, P, 2, HEAD_DIM)
    rope = g3[..., 2 * HEAD_DIM:2 * HEAD_DIM + 2 * ROPE_DIM].reshape(Rq, P, 2, ROPE_DIM)
    return jnp.concatenate([nope, rope], axis=-1).reshape(Rq, P * 2 * (HEAD_DIM + ROPE_DIM))


def _ukv_to_pairs(w):
    P = w.shape[1] // (4 * HEAD_DIM)
    w4 = w.reshape(KV_RANK, P, 2, 2 * HEAD_DIM)
    kn = w4[..., :HEAD_DIM].reshape(KV_RANK, P, 2 * HEAD_DIM)
    vv = w4[..., HEAD_DIM:].reshape(KV_RANK, P, 2 * HEAD_DIM)
    top = jnp.concatenate([kn, jnp.zeros((KV_RANK, P, LANES), w.dtype), vv], axis=-1)
    place = np.zeros((LANES, P, PAIR_KV), np.float32)
    for r in range(ROPE_DIM):
        place[r, :, LANES + r] = 1.0
        place[r, :, LANES + ROPE_DIM + r] = 1.0
    return jnp.concatenate([top, jnp.asarray(place, w.dtype)], axis=0).reshape(KV_RANK + LANES, P * PAIR_KV)


def _ukv_from_pairs(g):
    P = g.shape[1] // PAIR_KV
    g3 = g[:KV_RANK].reshape(KV_RANK, P, PAIR_KV)
    kn = g3[..., :2 * HEAD_DIM].reshape(KV_RANK, P, 2, HEAD_DIM)
    vv = g3[..., PAIR_Q:].reshape(KV_RANK, P, 2, HEAD_DIM)
    return jnp.concatenate([kn, vv], axis=-1).reshape(KV_RANK, P * 4 * HEAD_DIM)


def _mlp_fwd(h2, w, i, x1, gate, *, S):
    p, u = _mm(h2, w["mlp_w1"], "nn", name=f"mlp_up_{i}", b_layer=i, out_dtypes=(BF16, BF16),
               epilogue=lambda acc: (acc, jnp.square(jnp.maximum(acc, 0.0))))
    x2, z = _mm(u, w["mlp_w2"], "nn", name=f"mlp_down_{i}", b_layer=i, out_dtypes=(F32, F32), extras=(x1,),
                rowvecs=(gate,), seq=S, epilogue=lambda acc, xr, g: (xr + g * acc, acc))
    return x2, (p, u, z)


STACKED_GRADS = ("fox_out", "mla_down", "mla_uq", "mla_ukv", "mla_out", "mlp_w1", "mlp_w2")


def _local_step(x, target, pos_f, inv_freq_row, sign_row, mod, w, *, S):
    T, D = x.shape
    L = mod.shape[0]
    L2 = w["fox_out"].shape[0]
    n_split = w["mlp_w1"].shape[1]
    cos_t, sin_s = _rope_tables(pos_f, inv_freq_row, sign_row)
    saved = []
    for i in range(L):
        j = i // 2
        sh_m, sc_m, g_m, sh_f, sc_f, g_f = (mod[i, s] for s in range(6))
        h = _norm_mod(x, w["norm_mix_g"][i], sc_m, sh_m, S=S, name=f"norm_mix_{i}")
        if i % 2 == 0:
            qkv = _mm(h, w["fox_qkv"], "nn", name=f"fox_qkv_{i}", b_layer=j, out_dtypes=(BF16,))
            fl = _mm(h, w["fox_f"], "nn", name=f"fox_f_{i}", b_layer=j)
            qx, kvx = _fox_prep(qkv, fl, w["fox_b"][j], S=S, D=D, name=f"fox_prep_{i}")
            o, lse = _attn_fwd(qx, kvx, S=S, scale=1.0, ew=FOX_EXTRA, name=f"fox_attn_{i}")
            mix = (qx, kvx, o, lse, fl)
            w_out = w["fox_out"]
        else:
            lat = _mm(h, w["mla_down"], "nn", name=f"mla_down_{i}", b_layer=j)
            Rq = lat.shape[1] - 2 * LANES
            cqr = _mla_mid(lat, w["mla_gq"][j], w["mla_gkv"][j], cos_t, sin_s, name=f"mla_mid_{i}")
            qx = _mm(cqr, w["mla_uq"], "nn", name=f"mla_uq_{i}", b_layer=j, out_dtypes=(BF16,), a_sz=Rq, tk=Rq,
                     tables=(cos_t, sin_s), epilogue=lambda acc, c, s: (_rope_pairs(acc, c, s, 1.0),))
            kvx = _mm(cqr, w["mla_ukv"], "nn", name=f"mla_ukv_{i}", b_layer=j, out_dtypes=(BF16,), a_off=Rq,
                      a_sz=2 * LANES, tk=2 * LANES, tn=PAIR_KV)
            o, lse = _attn_fwd(qx, kvx, S=S, scale=(HEAD_DIM + ROPE_DIM) ** -0.5, ew=ROPE_DIM, name=f"mla_attn_{i}")
            mix = (qx, kvx, o, lse, lat, cqr)
            w_out = w["mla_out"]
        x1, y = _mm(o, w_out, "nn", name=f"mix_out_{i}", b_layer=j, out_dtypes=(F32, F32), extras=(x,),
                    rowvecs=(g_m,), seq=S, epilogue=lambda acc, xr, g: (xr + g * acc, acc))
        h2 = _norm_mod(x1, w["norm_mlp_g"][i], sc_f, sh_f, S=S, name=f"norm_mlp_{i}")
        x2, mlp = _mlp_fwd(h2, w, i, x1, g_f, S=S)
        saved.append((x, h, mix, y, x1, h2, mlp))
        x = x2

    dx, dg_final, loss = _final_loss(x, target, w["final_norm_g"])

    grads = {k: [None] * len(w[k]) for k in ("norm_mix_g", "norm_mlp_g", "fox_b", "mla_gq", "mla_gkv")}
    grads.update({k: [None] * L2 for k in ("fox_qkv", "fox_f")})
    grads.update({k: None for k in STACKED_GRADS})
    grads["final_norm_g"] = dg_final

    def stacked(key, layer, n_layers, a, b, **kw):
        grads[key] = _mm(a.T, b, "nn", out_stack=(grads[key], layer, n_layers), **kw)

    dmod = [None] * L
    for i in reversed(range(L)):
        j = i // 2
        x0, h, mix, y, x1, h2, (p, u, z) = saved[i]
        sh_m, sc_m, g_m, sh_f, sc_f, g_f = (mod[i, s] for s in range(6))
        dz, dg_f = _gate_bwd(dx, z, g_f, S=S, name=f"gate_mlp_bwd_{i}")
        stacked("mlp_w2", i, L, u, dz, name=f"mlp_w2_grad_{i}")
        dp = _mm(dz, w["mlp_w2"], "nt", name=f"mlp_down_bwd_{i}", b_layer=i, out_dtypes=(BF16,), extras=(p,),
                 epilogue=lambda acc, pv: (acc * (2.0 * jnp.maximum(pv.astype(F32), 0.0)),))
        stacked("mlp_w1", i, L, h2, dp, name=f"mlp_w1_grad_{i}", out_split=n_split)
        dh2 = _mm(dp, w["mlp_w1"], "nt", name=f"mlp_up_bwd_{i}", b_layer=i)
        dx1, dsh_f, dsc_f, dgn = _norm_mod_bwd(x1, dh2, dx, w["norm_mlp_g"][i], sc_f, S=S, name=f"norm_mlp_bwd_{i}")
        grads["norm_mlp_g"][i] = dgn
        dy, dg_m = _gate_bwd(dx1, y, g_m, S=S, name=f"gate_mix_bwd_{i}")
        if i % 2 == 0:
            qx, kvx, o, lse, fl = mix
            stacked("fox_out", j, L2, o, dy, name=f"fox_out_grad_{i}")
            do = _mm(dy, w["fox_out"], "nt", name=f"fox_out_bwd_{i}", b_layer=j, out_dtypes=(BF16,))
            dqx, dkvx, csum, rsum = _attn_bwd(qx, kvx, o, lse, do, S=S, scale=1.0, ew=FOX_EXTRA,
                                              name=f"fox_attn_bwd_{i}", bias_grad=True)
            n_heads = D // HEAD_DIM
            csum = jnp.pad(csum.reshape(T, n_heads, HEAD_DIM)[:, :, 0], ((0, 0), (0, LANES - n_heads)))
            rsum = jnp.transpose(rsum[:, :, :2, :], (0, 3, 1, 2)).reshape(T, n_heads)
            rsum = jnp.pad(rsum, ((0, 0), (0, LANES - n_heads)))
            dqkv, dfl, db = _fox_unprep(dqx, dkvx, csum, rsum, fl, w["fox_b"][j], S=S, D=D, name=f"fox_unprep_{i}")
            grads["fox_b"][j] = db
            h_t = h.T
            grads["fox_qkv"][j] = _mm(h_t, dqkv, "nn", name=f"fox_qkv_grad_{i}")
            grads["fox_f"][j] = _mm(h_t, dfl, "nn", name=f"fox_f_grad_{i}")
            dh_f = _mm(dfl, w["fox_f"], "nt", name=f"fox_f_bwd_{i}", b_layer=j)
            dh = _mm(dqkv, w["fox_qkv"], "nt", name=f"fox_qkv_bwd_{i}", b_layer=j, extras=(dh_f,),
                     epilogue=lambda acc, e: (acc + e,))
        else:
            qx, kvx, o, lse, lat, cqr = mix
            Rq = lat.shape[1] - 2 * LANES
            stacked("mla_out", j, L2, o, dy, name=f"mla_out_grad_{i}")
            do = _mm(dy, w["mla_out"], "nt", name=f"mla_out_bwd_{i}", b_layer=j, out_dtypes=(BF16,))
            dqx, dkvx = _attn_bwd(qx, kvx, o, lse, do, S=S, scale=(HEAD_DIM + ROPE_DIM) ** -0.5, ew=ROPE_DIM,
                                  name=f"mla_attn_bwd_{i}")
            dqpre = _unrope(dqx, cos_t, sin_s)
            stacked("mla_uq", j, L2, cqr[:, :Rq], dqpre, name=f"mla_uq_grad_{i}", out_split=n_split)
            stacked("mla_ukv", j, L2, cqr[:, Rq:], dkvx, name=f"mla_ukv_grad_{i}", tn=PAIR_KV, out_split=n_split)
            dcq = _mm(dqpre, w["mla_uq"], "nt", name=f"mla_uq_bwd_{i}", b_layer=j)
            dckr = _mm(dkvx, w["mla_ukv"], "nt", name=f"mla_ukv_bwd_{i}", b_layer=j, tk=PAIR_KV * 2)
            dlat, dgq, dgkv = _mla_mid_bwd(lat, dcq, dckr, w["mla_gq"][j], w["mla_gkv"][j], cos_t, sin_s,
                                           name=f"mla_mid_bwd_{i}")
            grads["mla_gq"][j] = dgq
            grads["mla_gkv"][j] = dgkv
            stacked("mla_down", j, L2, h, dlat, name=f"mla_down_grad_{i}")
            dh = _mm(dlat, w["mla_down"], "nt", name=f"mla_down_bwd_{i}", b_layer=j)
        dx, dsh_m, dsc_m, dgn = _norm_mod_bwd(x0, dh, dx1, w["norm_mix_g"][i], sc_m, S=S, name=f"norm_mix_bwd_{i}")
        grads["norm_mix_g"][i] = dgn
        dmod[i] = jnp.stack([dsh_m, dsc_m, dg_m, dsh_f, dsc_f, dg_f])
    return loss, dx, jnp.stack(dmod), grads


GATHERED = ("fox_in", "fox_out", "mla_down", "mla_uq", "mla_ukv", "mla_out", "mlp_w1", "mlp_w2")
ROW_SHARDED = ("fox_out", "mla_down", "mla_out", "mlp_w2")


def _shard_layouts(wts):
    dkv = wts["mla_w_dkv"]
    dkv = jnp.pad(dkv, ((0, 0), (0, 0), (0, 2 * LANES - dkv.shape[2])))
    return {
        "fox_in": wts["fox_w_in"].astype(BF16),
        "fox_out": wts["fox_w_out"].astype(BF16),
        "mla_down": jnp.concatenate([wts["mla_w_dq"], dkv], axis=2).astype(BF16),
        "mla_uq": jax.vmap(_uq_to_pairs)(wts["mla_w_uq"].astype(BF16)),
        "mla_ukv": jax.vmap(_ukv_to_pairs)(wts["mla_w_ukv"].astype(BF16)),
        "mla_out": wts["mla_w_out"].astype(BF16),
        "mlp_w1": wts["mlp_w1"].astype(BF16),
        "mlp_w2": wts["mlp_w2"].astype(BF16),
    }


def _small_layouts(small):
    return {
        "fox_b": [jnp.pad(b, (0, LANES - b.shape[0]))[None, :] for b in small["fox_b_f"]],
        "mla_gq": [g[None, :] for g in small["mla_q_norm_g"]],
        "mla_gkv": [g[None, :] for g in small["mla_kv_norm_g"]],
        "norm_mix_g": [g[None, :] for g in small["norm_mix_g"]],
        "norm_mlp_g": [g[None, :] for g in small["norm_mlp_g"]],
        "final_norm_g": small["final_norm_g"][None, :],
    }


def _full_layouts(gathered, D):
    fox = gathered["fox_in"]
    L2, ns, _, bs = fox.shape
    fox = jnp.transpose(fox, (0, 2, 1, 3)).reshape(L2, D, ns * bs)
    w = {"fox_qkv": fox[:, :, :3 * D],
         "fox_f": jnp.pad(fox[:, :, 3 * D:], ((0, 0), (0, 0), (0, LANES - (ns * bs - 3 * D))))}
    for n in ROW_SHARDED:
        g = gathered[n]
        w[n] = g.reshape(g.shape[0], g.shape[1] * g.shape[2], g.shape[3])
    for n in ("mla_uq", "mla_ukv", "mlp_w1"):
        w[n] = gathered[n]
    return w


def _grad_layouts(g, n_fox_heads, ns):
    fox = jnp.stack([jnp.concatenate([a, b[:, :n_fox_heads]], axis=1) for a, b in zip(g["fox_qkv"], g["fox_f"])])
    L2, D, cols = fox.shape
    out = {"fox_in": jnp.transpose(fox.reshape(L2, D, ns, cols // ns), (0, 2, 1, 3))}
    for n in ROW_SHARDED:
        a = g[n]
        out[n] = a.reshape(a.shape[0], ns, a.shape[1] // ns, a.shape[2])
    for n in ("mla_uq", "mla_ukv", "mlp_w1"):
        out[n] = g[n]
    return out


def _natural_shard_grads(s, rq):
    return {
        "fox_w_in": s["fox_in"], "fox_w_out": s["fox_out"], "mla_w_out": s["mla_out"],
        "mlp_w1": s["mlp_w1"], "mlp_w2": s["mlp_w2"],
        "mla_w_dq": s["mla_down"][:, :, :rq],
        "mla_w_dkv": s["mla_down"][:, :, rq:rq + KV_RANK + ROPE_DIM],
        "mla_w_uq": jax.vmap(_uq_from_pairs)(s["mla_uq"]),
        "mla_w_ukv": jax.vmap(_ukv_from_pairs)(s["mla_ukv"]),
    }


def _small_grads(g, n_fox_heads):
    return {
        "norm_mix_g": jnp.concatenate(g["norm_mix_g"], axis=0),
        "norm_mlp_g": jnp.concatenate(g["norm_mlp_g"], axis=0),
        "final_norm_g": g["final_norm_g"][0],
        "fox_b_f": jnp.concatenate(g["fox_b"], axis=0)[:, :n_fox_heads],
        "mla_q_norm_g": jnp.concatenate(g["mla_gq"], axis=0),
        "mla_kv_norm_g": jnp.concatenate(g["mla_gkv"], axis=0),
    }


def _silu(c):
    return c * (1.0 / (1.0 + jnp.exp(-c)))


def _ada_fwd(c_all, ada_w, ada_b_cols):
    L, D, C = ada_w.shape
    Bg = c_all.shape[0]
    tc = _tile(C, 512)

    def body(c_ref, w_ref, b_ref, o_ref):
        ca = _silu(c_ref[...]).astype(BF16)
        o_ref[...] = jnp.dot(ca, w_ref[...].astype(BF16), preferred_element_type=F32) + b_ref[...]

    return pl.pallas_call(
        body, name="ada_fwd", grid=(L, C // tc),
        in_specs=[pl.BlockSpec((Bg, D), lambda l, j: (0, 0)), pl.BlockSpec((None, D, tc), lambda l, j: (l, 0, j)),
                  pl.BlockSpec((None, 1, tc), lambda l, j: (l, 0, j))],
        out_specs=pl.BlockSpec((None, Bg, tc), lambda l, j: (l, 0, j)),
        out_shape=jax.ShapeDtypeStruct((L, Bg, C), F32),
        compiler_params=_cparams(("parallel", "parallel")),
    )(c_all, ada_w, ada_b_cols)


def _ada_bwd(c_all, dmod_cols):
    L, Bg, C = dmod_cols.shape
    D = c_all.shape[1]
    tc = _tile(C, 512)

    def body(c_ref, d_ref, o_ref):
        ca = _silu(c_ref[...]).astype(BF16)
        o_ref[...] = _dot_tn(ca, d_ref[...].astype(BF16))

    return pl.pallas_call(
        body, name="ada_bwd", grid=(L, C // tc),
        in_specs=[pl.BlockSpec((Bg, D), lambda l, j: (0, 0)), pl.BlockSpec((None, Bg, tc), lambda l, j: (l, 0, j))],
        out_specs=pl.BlockSpec((None, D, tc), lambda l, j: (l, 0, j)),
        out_shape=jax.ShapeDtypeStruct((L, D, C), F32),
        compiler_params=_cparams(("parallel", "parallel")),
    )(c_all, dmod_cols)


def _adamw_update(w, gv, m, v):
    mn = ADAM_B1 * m + (1.0 - ADAM_B1) * gv
    vn = ADAM_B2 * v + (1.0 - ADAM_B2) * jnp.square(gv)
    m_hat = mn / (1.0 - ADAM_B1 ** ADAM_STEP)
    v_hat = vn / (1.0 - ADAM_B2 ** ADAM_STEP)
    return -ADAM_LR * (m_hat / (jnp.sqrt(v_hat) + ADAM_EPS) + ADAM_WD * w), mn, vn


def _adamw(w, g, m, v, *, name):
    shape = w.shape
    C = shape[-1]
    R = int(np.prod(shape[:-1])) if len(shape) > 1 else 1
    w2, g2, m2, v2 = (a.reshape(R, C) for a in (w, g, m, v))
    tr = _row_tile(R, C)

    def body(w_ref, g_ref, m_ref, v_ref, d_ref, nm_ref, nv_ref):
        d_ref[...], nm_ref[...], nv_ref[...] = _adamw_update(w_ref[...], g_ref[...], m_ref[...], v_ref[...])

    spec = pl.BlockSpec((tr, C), lambda i: (i, 0))
    out = pl.pallas_call(
        body, name=name, grid=(R // tr,), in_specs=[spec] * 4, out_specs=[spec] * 3,
        out_shape=[jax.ShapeDtypeStruct((R, C), F32)] * 3, compiler_params=_cparams(("parallel",)),
    )(w2, g2, m2, v2)
    return tuple(a.reshape(shape) for a in out)


def _adamw_halves(w, g_own, g_peer, m, v, c_idx, *, name):
    shape = w.shape
    C = shape[-1]
    R = int(np.prod(shape[:-1])) // 2
    tr = _row_tile(R, C)

    def body(c_ref, w_ref, go_ref, gp_ref, m_ref, v_ref, g_ref, d_ref, nm_ref, nv_ref):
        gv = jnp.where(pl.program_id(0) == c_ref[0], go_ref[...], gp_ref[...])
        g_ref[...] = gv
        d_ref[...], nm_ref[...], nv_ref[...] = _adamw_update(w_ref[...], gv, m_ref[...], v_ref[...])

    full = pl.BlockSpec((None, tr, C), lambda hh, i, c_ref: (hh, i, 0))
    half = pl.BlockSpec((tr, C), lambda hh, i, c_ref: (i, 0))
    grid_spec = pltpu.PrefetchScalarGridSpec(
        num_scalar_prefetch=1, grid=(2, R // tr), in_specs=[full, half, half, full, full], out_specs=[full] * 4)
    out = pl.pallas_call(
        body, name=name, grid_spec=grid_spec, out_shape=[jax.ShapeDtypeStruct((2, R, C), F32)] * 4,
        compiler_params=_cparams(("parallel", "parallel")),
    )(c_idx, w.reshape(2, R, C), g_own.reshape(R, C), g_peer.reshape(R, C), m.reshape(2, R, C), v.reshape(2, R, C))
    return tuple(a.reshape(shape) for a in out)


def _sum_gathered(dm8, sm8):
    n_dev, Bl, R, D = dm8.shape
    Rs = sm8.shape[1]

    def body(dm_ref, sm_ref, ob_ref, os_ref):
        acc_b = jnp.zeros((R, D), F32)
        acc_s = jnp.zeros((Rs, D), F32)
        for d in range(n_dev):
            for b in range(Bl):
                acc_b = acc_b + dm_ref[d, b]
            acc_s = acc_s + sm_ref[d]
        ob_ref[...] = acc_b
        os_ref[...] = acc_s

    return pl.pallas_call(
        body, name="sum_gathered",
        out_shape=[jax.ShapeDtypeStruct((R, D), F32), jax.ShapeDtypeStruct((Rs, D), F32)],
        compiler_params=_cparams(None),
    )(dm8, sm8)


N_DEV = 8
N_CHIP = 4
ANY = pl.BlockSpec(memory_space=pl.ANY)


def _mesh_pos():
    return lax.axis_index("x"), lax.axis_index("y"), lax.axis_index("c")


def _all_gather8(block, *, name, in_vmem):
    R, W = block.shape

    def body(x_ref, out_ref, send_sems, recv_sems, local_sem):
        x, y, c = _mesh_pos()
        me, sibling = (x, y, c), (x, y, 1 - c)
        chips = [(1 - x, y), (x, 1 - y), (1 - x, 1 - y)]

        def slot(px, py, pc):
            return out_ref.at[4 * px + 2 * py + pc]

        def copy(k, blk, to, src=None):
            return pltpu.make_async_remote_copy(
                src_ref=slot(*blk) if src is None else src, dst_ref=slot(*blk),
                send_sem=send_sems.at[k], recv_sem=recv_sems.at[k], device_id=to, device_id_type=MESH_ID)

        mine = pltpu.make_async_copy(x_ref, slot(*me), local_sem)
        mine.start()
        first = [copy(0, me, sibling, src=x_ref)]
        first += [copy(1 + j, me, (*chip, c), src=x_ref) for j, chip in enumerate(chips)]
        for cp in first:
            cp.start()
        passed = [copy(4 + j, (*chip, c), sibling) for j, chip in enumerate(chips)]
        for j, chip in enumerate(chips):
            copy(1 + j, (*chip, c), me).wait_recv()
            passed[j].start()
        copy(0, sibling, me).wait_recv()
        for j, chip in enumerate(chips):
            copy(4 + j, (*chip, 1 - c), me).wait_recv()
        for cp in first + passed:
            cp.wait_send()
        mine.wait()

    space = pl.BlockSpec(memory_space=pltpu.VMEM) if in_vmem else ANY
    return pl.pallas_call(
        body, name=name, out_shape=jax.ShapeDtypeStruct((N_DEV, R, W), block.dtype),
        in_specs=[space], out_specs=space,
        scratch_shapes=[pltpu.SemaphoreType.DMA((7,)), pltpu.SemaphoreType.DMA((7,)), pltpu.SemaphoreType.DMA],
        compiler_params=pltpu.CompilerParams(vmem_limit_bytes=VMEM_LIMIT_V7X),
    )(block)


def _comm_call(body, arrays, out_shapes, n_sems, *, name):
    return pl.pallas_call(
        body, name=name, out_shape=out_shapes, in_specs=[ANY] * len(arrays), out_specs=[ANY] * len(out_shapes),
        scratch_shapes=[pltpu.SemaphoreType.DMA((n_sems,)), pltpu.SemaphoreType.DMA((n_sems,)),
                        pltpu.SemaphoreType.DMA((len(arrays),))],
    )(*arrays)


def _gather_weights(shards, *, name):
    n = len(shards)

    def body(*refs):
        xs, outs = refs[:n], refs[n:2 * n]
        send_sems, recv_sems, local_sems = refs[2 * n:]
        x, y, c = _mesh_pos()
        me, sibling = (x, y, c), (x, y, 1 - c)
        chips = [(1 - x, y), (x, 1 - y), (1 - x, 1 - y)]
        waits = []
        for i in range(n):
            nl = shards[i].shape[0] // 2
            own = xs[i].at[pl.ds(c * nl, nl)]

            def slot(px, py, pc, i=i, nl=nl):
                return outs[i].at[pl.ds(pc * nl, nl), 2 * px + py]

            def copy(k, blk, to, src=None, i=i, slot=slot):
                return pltpu.make_async_remote_copy(
                    src_ref=slot(*blk) if src is None else src, dst_ref=slot(*blk),
                    send_sem=send_sems.at[7 * i + k], recv_sem=recv_sems.at[7 * i + k], device_id=to,
                    device_id_type=MESH_ID)

            mine = pltpu.make_async_copy(own, slot(*me), local_sems.at[i])
            mine.start()
            first = [copy(0, me, sibling, src=own)]
            first += [copy(1 + j, me, (*chip, c), src=own) for j, chip in enumerate(chips)]
            for cp in first:
                cp.start()
            waits.append((copy, mine, first))
        for copy, mine, first in waits:
            passed = [copy(4 + j, (*chip, c), sibling) for j, chip in enumerate(chips)]
            for j, chip in enumerate(chips):
                copy(1 + j, (*chip, c), me).wait_recv()
                passed[j].start()
            copy(0, sibling, me).wait_recv()
            for j, chip in enumerate(chips):
                copy(4 + j, (*chip, 1 - c), me).wait_recv()
            for cp in first + passed:
                cp.wait_send()
            mine.wait()

    out_shapes = [jax.ShapeDtypeStruct((s.shape[0], N_CHIP) + s.shape[1:], s.dtype) for s in shards]
    return _comm_call(body, shards, out_shapes, 7 * n, name=name)


def _pair_exchange(gs, *, name):
    n = len(gs)

    def body(*refs):
        xs, outs = refs[:n], refs[n:2 * n]
        send_sems, recv_sems, _ = refs[2 * n:]
        x, y, c = _mesh_pos()
        copies = []
        for i in range(n):
            nl = gs[i].shape[0] // 2
            cp = pltpu.make_async_remote_copy(
                src_ref=xs[i].at[pl.ds((1 - c) * nl, nl)], dst_ref=outs[i], send_sem=send_sems.at[i],
                recv_sem=recv_sems.at[i], device_id=(x, y, 1 - c), device_id_type=MESH_ID)
            cp.start()
            copies.append(cp)
        for cp in copies:
            cp.wait()

    out_shapes = [jax.ShapeDtypeStruct((g.shape[0] // 2,) + g.shape[1:], g.dtype) for g in gs]
    return _comm_call(body, gs, out_shapes, n, name=name)


def _chip_exchange(ps, *, name):
    n = len(ps)

    def body(*refs):
        xs, outs = refs[:n], refs[n:2 * n]
        send_sems, recv_sems, local_sems = refs[2 * n:]
        x, y, c = _mesh_pos()
        k_me = 2 * x + y
        chips = [(1 - x, y), (x, 1 - y), (1 - x, 1 - y)]
        copies = []
        for i in range(n):
            nl = ps[i].shape[0]
            mine = pltpu.make_async_copy(xs[i].at[pl.ds(0, nl), k_me], outs[i].at[k_me], local_sems.at[i])
            mine.start()
            copies.append(mine)
            for j, (cx, cy) in enumerate(chips):
                cp = pltpu.make_async_remote_copy(
                    src_ref=xs[i].at[pl.ds(0, nl), 2 * cx + cy], dst_ref=outs[i].at[k_me],
                    send_sem=send_sems.at[3 * i + j], recv_sem=recv_sems.at[3 * i + j],
                    device_id=(cx, cy, c), device_id_type=MESH_ID)
                cp.start()
                copies.append(cp)
        for cp in copies:
            cp.wait()

    out_shapes = [jax.ShapeDtypeStruct((p.shape[1], p.shape[0]) + p.shape[2:], p.dtype) for p in ps]
    return _comm_call(body, ps, out_shapes, 3 * n, name=name)


def _pair_swap(ss, *, name):
    n = len(ss)

    def body(*refs):
        xs, outs = refs[:n], refs[n:2 * n]
        send_sems, recv_sems, _ = refs[2 * n:]
        x, y, c = _mesh_pos()
        copies = []
        for i in range(n):
            cp = pltpu.make_async_remote_copy(src_ref=xs[i], dst_ref=outs[i], send_sem=send_sems.at[i],
                                              recv_sem=recv_sems.at[i], device_id=(x, y, 1 - c),
                                              device_id_type=MESH_ID)
            cp.start()
            copies.append(cp)
        for cp in copies:
            cp.wait()

    out_shapes = [jax.ShapeDtypeStruct(s.shape, s.dtype) for s in ss]
    return _comm_call(body, ss, out_shapes, n, name=name)


def _row_tile(rows, cols):
    tr = rows
    while tr * cols > 256 * 1024 and tr % 16 == 0:
        tr //= 2
    return tr


def _pair_add(g, recv, c_idx, *, name):
    shape = recv.shape
    W = shape[-1]
    R = int(np.prod(shape[:-1]))
    tr = _row_tile(R, W)

    def body(c_ref, g_ref, r_ref, o_ref):
        o_ref[...] = (g_ref[...] + r_ref[...]).astype(BF16)

    grid_spec = pltpu.PrefetchScalarGridSpec(
        num_scalar_prefetch=1, grid=(R // tr,),
        in_specs=[pl.BlockSpec((None, tr, W), lambda i, c_ref: (c_ref[0], i, 0)),
                  pl.BlockSpec((tr, W), lambda i, c_ref: (i, 0))],
        out_specs=pl.BlockSpec((tr, W), lambda i, c_ref: (i, 0)))
    out = pl.pallas_call(
        body, name=name, grid_spec=grid_spec, out_shape=jax.ShapeDtypeStruct((R, W), BF16),
        compiler_params=_cparams(("parallel",)),
    )(c_idx, g.reshape(2, R, W), recv.reshape(R, W))
    return out.reshape(shape)


def _sum_pieces(pieces, *, name):
    n = pieces.shape[0]
    shape = pieces.shape[1:]
    W = shape[-1]
    R = int(np.prod(shape[:-1]))
    tr = _row_tile(R, W)

    def body(p_ref, o_ref):
        acc = p_ref[0].astype(F32)
        for k in range(1, n):
            acc = acc + p_ref[k].astype(F32)
        o_ref[...] = acc

    out = pl.pallas_call(
        body, name=name, grid=(R // tr,), in_specs=[pl.BlockSpec((n, tr, W), lambda i: (0, i, 0))],
        out_specs=pl.BlockSpec((tr, W), lambda i: (i, 0)), out_shape=jax.ShapeDtypeStruct((R, W), F32),
        compiler_params=_cparams(("parallel",)),
    )(pieces.reshape(n, R, W))
    return out.reshape(shape)


SMALL = ("norm_mix_g", "norm_mlp_g", "final_norm_g", "fox_b_f", "mla_q_norm_g", "mla_kv_norm_g")
WEIGHT_ORDER = ("ada_w", "ada_b", "norm_mix_g", "norm_mlp_g", "fox_w_in", "fox_b_f", "fox_w_out", "mla_w_dq",
                "mla_q_norm_g", "mla_w_uq", "mla_w_dkv", "mla_kv_norm_g", "mla_w_ukv", "mla_w_out", "mlp_w1",
                "mlp_w2", "final_norm_g")


def _small_rows(vals, D):
    rows = [vals["norm_mix_g"], vals["norm_mlp_g"], vals["final_norm_g"][None, :]]
    for n in ("fox_b_f", "mla_q_norm_g", "mla_kv_norm_g"):
        flat = vals[n].reshape(-1)
        assert flat.shape[0] <= D
        rows.append(jnp.pad(flat, (0, D - flat.shape[0]))[None, :])
    return jnp.concatenate(rows, axis=0)


def _small_unrows(rows, shapes):
    L = shapes["norm_mix_g"][0]
    out = {"norm_mix_g": rows[0:L], "norm_mlp_g": rows[L:2 * L], "final_norm_g": rows[2 * L]}
    for k, n in enumerate(("fox_b_f", "mla_q_norm_g", "mla_kv_norm_g")):
        size = int(np.prod(shapes[n]))
        out[n] = rows[2 * L + 1 + k, :size].reshape(shapes[n])
    return out


def kernel(x, c, positions, ada_w, ada_b, norm_mix_g, norm_mlp_g, fox_w_in, fox_b_f, fox_w_out, mla_w_dq, mla_q_norm_g, mla_w_uq, mla_w_dkv, mla_kv_norm_g, mla_w_ukv, mla_w_out, mlp_w1, mlp_w2, final_norm_g, loss_target, m_ada_w, m_ada_b, m_norm_mix_g, m_norm_mlp_g, m_fox_w_in, m_fox_b_f, m_fox_w_out, m_mla_w_dq, m_mla_q_norm_g, m_mla_w_uq, m_mla_w_dkv, m_mla_kv_norm_g, m_mla_w_ukv, m_mla_w_out, m_mlp_w1, m_mlp_w2, m_final_norm_g, v_ada_w, v_ada_b, v_norm_mix_g, v_norm_mlp_g, v_fox_w_in, v_fox_b_f, v_fox_w_out, v_mla_w_dq, v_mla_q_norm_g, v_mla_w_uq, v_mla_w_dkv, v_mla_kv_norm_g, v_mla_w_ukv, v_mla_w_out, v_mlp_w1, v_mlp_w2, v_final_norm_g):
    args = dict(locals())
    wts = {n: args[n] for n in WEIGHT_ORDER}
    mom = {n: args["m_" + n] for n in WEIGHT_ORDER}
    var = {n: args["v_" + n] for n in WEIGHT_ORDER}
    Bl, S, D = x.shape
    T = Bl * S
    L = ada_w.shape[0]
    C = ada_w.shape[2]
    mx, my, mc = _mesh_pos()
    chip = 2 * mx + my
    dev = 4 * mx + 2 * my + mc
    c_idx = jnp.reshape(mc, (1,)).astype(jnp.int32)

    shards = _shard_layouts(wts)
    gathered = dict(zip(GATHERED, _gather_weights([shards[n] for n in GATHERED], name="gather_weights")))
    small = {n: wts[n] for n in SMALL}
    L2, q_cols = mla_q_norm_g.shape

    c_pad = jnp.concatenate([c, jnp.pad(mla_q_norm_g, ((0, 8 - Bl - L2), (0, D - q_cols)))], axis=0)
    c8 = _all_gather8(c_pad, name="gather_c", in_vmem=True)
    c_all = c8[:, :Bl].reshape(N_DEV * Bl, D)
    qg4 = c8.reshape(N_CHIP, 2, 8, D)[:, 0, Bl:Bl + L2, :q_cols]
    small["mla_q_norm_g"] = jnp.transpose(qg4, (1, 0, 2)).reshape(L2, N_CHIP * q_cols)
    ada_b_cols = lax.dynamic_slice_in_dim(ada_b, chip * C, C, axis=1)[:, None, :]
    mod_cols = _ada_fwd(c_all, ada_w, ada_b_cols)
    mod8 = _all_gather8(mod_cols.reshape(L * N_DEV * Bl, C), name="gather_mod", in_vmem=True)
    mod4 = mod8.reshape(N_CHIP, 2, L, N_DEV * Bl, C)[:, 0]
    mod_me = lax.dynamic_slice_in_dim(mod4, dev * Bl, Bl, axis=2)
    mod = jnp.transpose(mod_me, (1, 2, 0, 3)).reshape(L, Bl, 6, D)
    mod = jnp.transpose(mod, (0, 2, 1, 3))[:, :, :, None, :]

    w = _full_layouts(gathered, D)
    w.update(_small_layouts(small))
    half = ROPE_DIM // 2
    inv_freq = ROPE_THETA ** (-jnp.arange(0, ROPE_DIM, 2, dtype=F32) / ROPE_DIM)
    lane = np.arange(LANES)
    inv_freq_row = jnp.tile(inv_freq, LANES // half)[None, :]
    sign_row = jnp.asarray(np.where(lane < 2 * ROPE_DIM, np.where(lane % ROPE_DIM < half, -1.0, 1.0), 0.0), F32)[None, :]
    pos_f = positions.astype(F32).reshape(T, 1)
    loss_row, grad_x, dmod, g = _local_step(x.reshape(T, D), loss_target.reshape(T, D), pos_f, inv_freq_row, sign_row,
                                            mod, w, S=S)
    g_small = _small_grads(g, fox_b_f.shape[1])

    Rs = -(-(2 * L + 5) // 8) * 8
    srows = jnp.concatenate([_small_rows(g_small, D), jnp.pad(loss_row, ((0, 0), (0, D - LANES)))], axis=0)
    srows = jnp.pad(srows, ((0, Rs - srows.shape[0]), (0, 0)))
    drows = jnp.transpose(dmod[:, :, :, 0, :], (2, 0, 1, 3)).reshape(Bl * L * 6, D)
    both8 = _all_gather8(jnp.concatenate([drows, srows], axis=0), name="gather_small", in_vmem=True)
    dm8 = both8[:, :Bl * L * 6].reshape(N_DEV, Bl, L * 6, D)
    sm8 = both8[:, Bl * L * 6:]
    adb_rows, small_sum = _sum_gathered(dm8, sm8)
    grad_ada_b = adb_rows.reshape(L, 6 * D)
    loss = small_sum[2 * L + 4, 0]
    small_shapes = {n: (wts[n].shape if n != "mla_q_norm_g" else (wts[n].shape[0], N_CHIP * q_cols)) for n in SMALL}
    gs = _small_unrows(small_sum, small_shapes)
    gs["mla_q_norm_g"] = lax.dynamic_slice_in_dim(gs["mla_q_norm_g"], chip * q_cols, q_cols, axis=1)

    dmod16 = jnp.transpose(dm8.reshape(N_DEV, Bl, L, 6 * D), (2, 0, 1, 3)).reshape(L, N_DEV * Bl, 6 * D)
    dmod_cols = lax.dynamic_slice_in_dim(dmod16, chip * C, C, axis=2)
    grad_ada_w = _ada_bwd(c_all, dmod_cols)

    gl = _grad_layouts(g, fox_b_f.shape[1], N_CHIP)
    big = [gl[n] for n in GATHERED]
    from_sibling = _pair_exchange(big, name="grad_pair_exchange")
    pair_sums = [_pair_add(a, r, c_idx, name=f"grad_pair_add_{n}") for n, a, r in zip(GATHERED, big, from_sibling)]
    pieces = _chip_exchange(pair_sums, name="grad_chip_exchange")
    halves = [_sum_pieces(p, name=f"grad_sum_pieces_{n}") for n, p in zip(GATHERED, pieces)]
    from_peer = _pair_swap(halves, name="grad_pair_swap")
    g_own = _natural_shard_grads(dict(zip(GATHERED, halves)), mla_w_dq.shape[-1])
    g_peer = _natural_shard_grads(dict(zip(GATHERED, from_peer)), mla_w_dq.shape[-1])
    grads = dict(gs)
    grads["ada_w"] = grad_ada_w
    grads["ada_b"] = grad_ada_b

    delta, new_m, new_v = {}, {}, {}
    for n, _ in BIG_WEIGHTS:
        grads[n], delta[n], new_m[n], new_v[n] = _adamw_halves(wts[n], g_own[n], g_peer[n], mom[n], var[n], c_idx,
                                                               name=f"adamw_{n}")
    for n in ("ada_w", "ada_b"):
        delta[n], new_m[n], new_v[n] = _adamw(wts[n], grads[n], mom[n], var[n], name=f"adamw_{n}")
    shard_small_shapes = {n: wts[n].shape for n in SMALL}
    packs = [jnp.pad(_small_rows({n: src[n] for n in SMALL}, D), ((0, Rs - 2 * L - 4), (0, 0)))
             for src in (wts, grads, mom, var)]
    for dst, rows in zip((delta, new_m, new_v), _adamw(*packs, name="adamw_small")):
        dst.update(_small_unrows(rows, shard_small_shapes))

    return (loss, grad_x.reshape(Bl, S, D), *[grads[n] for n in WEIGHT_ORDER], *[delta[n] for n in WEIGHT_ORDER],
            *[new_m[n] for n in WEIGHT_ORDER], *[new_v[n] for n in WEIGHT_ORDER])
```

```python
import functools

import numpy as np
import jax
import jax.numpy as jnp
from jax import lax
from jax.experimental import pallas as pl
from jax.experimental.pallas import tpu as pltpu

F32 = jnp.float32
BF16 = jnp.bfloat16
MESH_ID = pl.DeviceIdType.MESH

NORM_EPS = 1e-6
ROPE_THETA = 10000.0
HEAD_DIM = 64
ROPE_DIM = 32
KV_RANK = 128
FOX_EXTRA = 6
PAIR_Q = 256
PAIR_KV = 384
LANES = 128
ADAM_LR = 0.001
ADAM_B1 = 0.9
ADAM_B2 = 0.999
ADAM_EPS = 1e-08
ADAM_WD = 0.01
ADAM_STEP = 10
VMEM_LIMIT_V7X = 48 * 1024 * 1024
MM_VMEM_BUDGET = 36 * 1024 * 1024
NEG_BIG = -1e30
ATTN_UNROLL = 2

BIG_WEIGHTS = (("fox_w_in", 2), ("fox_w_out", 1), ("mla_w_dq", 1), ("mla_w_uq", 2), ("mla_w_dkv", 1),
               ("mla_w_ukv", 2), ("mla_w_out", 1), ("mlp_w1", 2), ("mlp_w2", 1))


def _cparams(sem=None):
    return pltpu.CompilerParams(dimension_semantics=sem, vmem_limit_bytes=VMEM_LIMIT_V7X)


def _tile(n, want):
    if n <= want:
        return n
    for t in range(want - want % LANES, 0, -LANES):
        if n % t == 0:
            return t
    raise ValueError((n, want))


def _mm(a, b, mode, *, name, out_dtypes=(F32,), epilogue=None, extras=(), rowvecs=(), tables=(),
        seq=None, a_off=0, a_sz=None, b_layer=None, out_stack=None, out_split=0, tm=1024, tn=1024, tk=2048):
    if isinstance(b, (list, tuple)):
        b, b_layer = b[b_layer]
    b_rows, b_cols = b.shape[-2], b.shape[-1]
    n_split = b.shape[1] if b.ndim == 4 else 1
    assert mode in ("nn", "nt")
    if mode == "nn":
        M, K, N = a.shape[0], b_rows, b_cols * n_split
    else:
        M, K, N = a.shape[0], b_cols * n_split, b_rows
    assert a_sz is None or a_sz == K
    tm = _tile(seq if rowvecs else M, tm)
    n_piece = N // max(out_split, n_split if mode == "nn" else 1, 1)
    tn = _tile(n_piece, tn)
    tk = _tile(K // (n_split if mode == "nt" else 1), tk)
    ne, nr, nt_ = len(extras), len(rowvecs), len(tables)
    no = len(out_dtypes)

    def vmem_estimate():
        blocks = tm * tk * a.dtype.itemsize + tk * tn * b.dtype.itemsize
        blocks += tm * tn * (sum(e.dtype.itemsize for e in extras) + sum(jnp.dtype(d).itemsize for d in out_dtypes))
        return 2 * blocks + 2 * tm * tn * 4

    while vmem_estimate() > MM_VMEM_BUDGET and max(tm, tn) > 256:
        if tn >= tm:
            tn //= 2
        else:
            tm //= 2
    nk = K // tk

    assert a_off % tk == 0
    a_spec = pl.BlockSpec((tm, tk), lambda i, j, k: (i, k + a_off // tk))
    dims = (((1,), (0,)), ((), ())) if mode == "nn" else (((1,), (1,)), ((), ()))
    lead = () if b.ndim == 2 else (b_layer,)
    sq = (None,) * (b.ndim - 2)
    if mode == "nt":
        kb = b_cols // tk
        if b.ndim == 4:
            b_spec = pl.BlockSpec(sq + (tn, tk), lambda i, j, k: lead + (k // kb, j, k % kb))
        else:
            b_spec = pl.BlockSpec(sq + (tn, tk), lambda i, j, k: lead + (j, k))
    else:
        nb = b_cols // tn
        if b.ndim == 4:
            b_spec = pl.BlockSpec(sq + (tk, tn), lambda i, j, k: lead + (j // nb, k, j % nb))
        else:
            b_spec = pl.BlockSpec(sq + (tk, tn), lambda i, j, k: lead + (k, j))
    in_specs = [a_spec, b_spec]
    in_specs += [pl.BlockSpec((tm, tn), lambda i, j, k: (i, j)) for _ in extras]
    if rowvecs:
        assert seq % tm == 0
        per = seq // tm
        in_specs += [pl.BlockSpec((None, 1, tn), lambda i, j, k: (i // per, 0, j)) for _ in rowvecs]
    in_specs += [pl.BlockSpec((tm, LANES), lambda i, j, k: (i, 0)) for _ in tables]
    operands = [a, b, *extras, *rowvecs, *tables]
    aliases = {}
    if out_stack is None:
        out_specs = [pl.BlockSpec((tm, tn), lambda i, j, k: (i, j)) for _ in out_dtypes]
        out_shape = [jax.ShapeDtypeStruct((M, N), d) for d in out_dtypes]
    else:
        prev, layer, n_layers = out_stack
        assert no == 1
        if out_split:
            ob = n_piece // tn
            out_specs = [pl.BlockSpec((None, None, tm, tn), lambda i, j, k: (layer, j // ob, i, j % ob))]
            out_shape = [jax.ShapeDtypeStruct((n_layers, out_split, M, n_piece), out_dtypes[0])]
        else:
            out_specs = [pl.BlockSpec((None, tm, tn), lambda i, j, k: (layer, i, j))]
            out_shape = [jax.ShapeDtypeStruct((n_layers, M, N), out_dtypes[0])]
        if prev is not None:
            in_specs.append(pl.BlockSpec(memory_space=pl.ANY))
            aliases = {len(operands): 0}
            operands.append(prev)
    n_in = len(operands)

    def body(*refs):
        a_ref, b_ref = refs[0], refs[1]
        side = refs[2:2 + ne + nr + nt_]
        outs = refs[n_in:n_in + no]

        def finish(acc):
            res = (acc,) if epilogue is None else epilogue(acc, *[r[...] for r in side])
            for o_ref, r in zip(outs, res):
                o_ref[...] = r.astype(o_ref.dtype)

        part = lax.dot_general(a_ref[...].astype(BF16), b_ref[...].astype(BF16), dims,
                               preferred_element_type=F32)
        if nk == 1:
            finish(part)
        else:
            acc_ref = refs[-1]
            k = pl.program_id(2)

            @pl.when(k == 0)
            def _():
                acc_ref[...] = part

            @pl.when(k > 0)
            def _():
                acc_ref[...] += part

            @pl.when(k == nk - 1)
            def _():
                finish(acc_ref[...])

    res = pl.pallas_call(
        body, name=name, grid=(M // tm, N // tn, nk), in_specs=in_specs, out_specs=out_specs,
        out_shape=out_shape, scratch_shapes=[pltpu.VMEM((tm, tn), F32)] if nk > 1 else [],
        input_output_aliases=aliases,
        compiler_params=_cparams(("parallel", "parallel", "arbitrary")),
    )(*operands)
    return res[0] if no == 1 else tuple(res)


def _rope128(x, cos_t, sin_s):
    lane = lax.broadcasted_iota(jnp.int32, x.shape, 1)
    first = (lane % ROPE_DIM) < (ROPE_DIM // 2)
    swapped = jnp.where(first, pltpu.roll(x, LANES - ROPE_DIM // 2, 1), pltpu.roll(x, ROPE_DIM // 2, 1))
    return x * cos_t + swapped * sin_s


def _rope_pairs(acc, cos_t, sin_s, sign):
    parts = []
    for p in range(acc.shape[1] // PAIR_Q):
        parts.append(acc[:, p * PAIR_Q:p * PAIR_Q + LANES])
        parts.append(_rope128(acc[:, p * PAIR_Q + LANES:(p + 1) * PAIR_Q], cos_t, sign * sin_s))
    return jnp.concatenate(parts, axis=1)


def _rope_tables(pos_f, inv_freq_row, sign_row):
    T = pos_f.shape[0]
    tt = _tile(T, 512)

    def body(p_ref, f_ref, s_ref, cos_ref, sin_ref):
        ang = p_ref[...] * f_ref[...]
        cos_ref[...] = jnp.cos(ang)
        sin_ref[...] = jnp.sin(ang) * s_ref[...]

    return pl.pallas_call(
        body, name="rope_tables", grid=(T // tt,),
        in_specs=[pl.BlockSpec((tt, 1), lambda i: (i, 0)), pl.BlockSpec((1, LANES), lambda i: (0, 0)),
                  pl.BlockSpec((1, LANES), lambda i: (0, 0))],
        out_specs=[pl.BlockSpec((tt, LANES), lambda i: (i, 0))] * 2,
        out_shape=[jax.ShapeDtypeStruct((T, LANES), F32)] * 2,
        compiler_params=_cparams(("parallel",)),
    )(pos_f, inv_freq_row, sign_row)


def _unrope(dqx, cos_t, sin_s):
    T, W = dqx.shape
    tt = _tile(T, 512)

    def body(d_ref, c_ref, s_ref, o_ref):
        o_ref[...] = _rope_pairs(d_ref[...].astype(F32), c_ref[...], s_ref[...], -1.0).astype(BF16)

    return pl.pallas_call(
        body, name="mla_unrope", grid=(T // tt,),
        in_specs=[pl.BlockSpec((tt, W), lambda i: (i, 0)), pl.BlockSpec((tt, LANES), lambda i: (i, 0)),
                  pl.BlockSpec((tt, LANES), lambda i: (i, 0))],
        out_specs=pl.BlockSpec((tt, W), lambda i: (i, 0)),
        out_shape=jax.ShapeDtypeStruct((T, W), BF16),
        compiler_params=_cparams(("parallel",)),
    )(dqx, cos_t, sin_s)


def _row_specs(tt, D, per, n):
    return [pl.BlockSpec((None, 1, D), lambda i: (i // per, 0, 0)) for _ in range(n)]


def _norm_mod(x, gain, sc, sh, *, S, name):
    T, D = x.shape
    tt = _tile(S, 512)
    per = S // tt

    def body(x_ref, g_ref, sc_ref, sh_ref, h_ref):
        xv = x_ref[...]
        r = lax.rsqrt(jnp.mean(xv * xv, axis=-1, keepdims=True) + NORM_EPS)
        h_ref[...] = ((xv * r) * g_ref[...] * (1.0 + sc_ref[...]) + sh_ref[...]).astype(BF16)

    return pl.pallas_call(
        body, name=name, grid=(T // tt,),
        in_specs=[pl.BlockSpec((tt, D), lambda i: (i, 0)), pl.BlockSpec((1, D), lambda i: (0, 0))]
        + _row_specs(tt, D, per, 2),
        out_specs=pl.BlockSpec((tt, D), lambda i: (i, 0)),
        out_shape=jax.ShapeDtypeStruct((T, D), BF16),
        compiler_params=_cparams(("parallel",)),
    )(x, gain, sc, sh)


def _norm_mod_bwd(x, dh, dres, gain, sc, *, S, name):
    T, D = x.shape
    B = T // S
    tt = _tile(S, 512)
    per = S // tt

    def body(x_ref, dh_ref, dres_ref, g_ref, sc_ref, dx_ref, dsh_ref, dsc_ref, dg_ref):
        i = pl.program_id(0)
        xv = x_ref[...]
        dhv = dh_ref[...].astype(F32)
        r = lax.rsqrt(jnp.mean(xv * xv, axis=-1, keepdims=True) + NORM_EPS)
        n = xv * r
        g = g_ref[...]
        one_sc = 1.0 + sc_ref[...]
        dn = dhv * (g * one_sc)
        dx_ref[...] = dres_ref[...] + r * (dn - n * jnp.mean(dn * n, axis=-1, keepdims=True))
        dhn = dhv * n

        @pl.when(i % per == 0)
        def _():
            dsh_ref[...] = jnp.zeros_like(dsh_ref)
            dsc_ref[...] = jnp.zeros_like(dsc_ref)

        @pl.when(i == 0)
        def _():
            dg_ref[...] = jnp.zeros_like(dg_ref)

        dsh_ref[...] += jnp.sum(dhv, axis=0, keepdims=True)
        dsc_ref[...] += jnp.sum(dhn, axis=0, keepdims=True) * g
        dg_ref[...] += jnp.sum(dhn, axis=0, keepdims=True) * one_sc

    return pl.pallas_call(
        body, name=name, grid=(T // tt,),
        in_specs=[pl.BlockSpec((tt, D), lambda i: (i, 0))] * 3 + [pl.BlockSpec((1, D), lambda i: (0, 0))]
        + _row_specs(tt, D, per, 1),
        out_specs=[pl.BlockSpec((tt, D), lambda i: (i, 0))] + _row_specs(tt, D, per, 2)
        + [pl.BlockSpec((1, D), lambda i: (0, 0))],
        out_shape=[jax.ShapeDtypeStruct((T, D), F32), jax.ShapeDtypeStruct((B, 1, D), F32),
                   jax.ShapeDtypeStruct((B, 1, D), F32), jax.ShapeDtypeStruct((1, D), F32)],
        compiler_params=_cparams(("arbitrary",)),
    )(x, dh, dres, gain, sc)


def _gate_bwd(dx, y, g, *, S, name):
    T, D = dx.shape
    B = T // S
    tt = _tile(S, 512)
    per = S // tt

    def body(dx_ref, y_ref, g_ref, dy_ref, dg_ref):
        i = pl.program_id(0)
        dxv = dx_ref[...]
        dy_ref[...] = (dxv * g_ref[...]).astype(BF16)

        @pl.when(i % per == 0)
        def _():
            dg_ref[...] = jnp.zeros_like(dg_ref)

        dg_ref[...] += jnp.sum(dxv * y_ref[...], axis=0, keepdims=True)

    return pl.pallas_call(
        body, name=name, grid=(T // tt,),
        in_specs=[pl.BlockSpec((tt, D), lambda i: (i, 0))] * 2 + _row_specs(tt, D, per, 1),
        out_specs=[pl.BlockSpec((tt, D), lambda i: (i, 0))] + _row_specs(tt, D, per, 1),
        out_shape=[jax.ShapeDtypeStruct((T, D), BF16), jax.ShapeDtypeStruct((B, 1, D), F32)],
        compiler_params=_cparams(("arbitrary",)),
    )(dx, y, g)


def _final_loss(x, target, gain):
    T, D = x.shape
    tt = _tile(T, 512)

    def body(x_ref, t_ref, g_ref, dx_ref, dg_ref, loss_ref):
        i = pl.program_id(0)
        xv = x_ref[...]
        r = lax.rsqrt(jnp.mean(xv * xv, axis=-1, keepdims=True) + NORM_EPS)
        n = xv * r
        g = g_ref[...]
        err = n * g - t_ref[...]
        dy = err * (1.0 / D)
        dn = dy * g
        dx_ref[...] = r * (dn - n * jnp.mean(dn * n, axis=-1, keepdims=True))

        @pl.when(i == 0)
        def _():
            dg_ref[...] = jnp.zeros_like(dg_ref)
            loss_ref[...] = jnp.zeros_like(loss_ref)

        dg_ref[...] += jnp.sum(dy * n, axis=0, keepdims=True)
        loss_ref[...] += jnp.sum(jnp.sum(err * err, axis=-1, keepdims=True), axis=0, keepdims=True) * (0.5 / D)

    return pl.pallas_call(
        body, name="final_loss", grid=(T // tt,),
        in_specs=[pl.BlockSpec((tt, D), lambda i: (i, 0))] * 2 + [pl.BlockSpec((1, D), lambda i: (0, 0))],
        out_specs=[pl.BlockSpec((tt, D), lambda i: (i, 0)), pl.BlockSpec((1, D), lambda i: (0, 0)),
                   pl.BlockSpec((1, LANES), lambda i: (0, 0))],
        out_shape=[jax.ShapeDtypeStruct((T, D), F32), jax.ShapeDtypeStruct((1, D), F32),
                   jax.ShapeDtypeStruct((1, LANES), F32)],
        compiler_params=_cparams(("arbitrary",)),
    )(x, target, gain)


def _head_masks(ew):
    lane = lax.broadcasted_iota(jnp.int32, (1, PAIR_Q), 1)
    m0 = (lane < HEAD_DIM) | ((lane >= LANES) & (lane < LANES + ew))
    m1 = ((lane >= HEAD_DIM) & (lane < LANES)) | ((lane >= LANES + ew) & (lane < LANES + 2 * ew))
    return m0, m1


def _dot_nt(a, b):
    return lax.dot_general(a, b, (((1,), (1,)), ((), ())), preferred_element_type=F32)


def _dot_tn(a, b):
    return lax.dot_general(a, b, (((0,), (0,)), ((), ())), preferred_element_type=F32)


def _lane_halves(x, op):
    acc = x[:, 0:LANES]
    for g in range(1, x.shape[1] // LANES):
        acc = op(acc, x[:, g * LANES:(g + 1) * LANES])
    return acc


def _head_rows(cols_lane_replicated):
    t = cols_lane_replicated.T
    sub = lax.broadcasted_iota(jnp.int32, (8, t.shape[1]), 0)
    return jnp.where(sub == 1, t[HEAD_DIM:HEAD_DIM + 8], t[0:8])


def _attn_fwd(qx, kvx, *, S, scale, ew, name):
    T = qx.shape[0]
    P = qx.shape[1] // PAIR_Q
    B = T // S
    tq = _tile(S, 256)
    nq = S // tq
    assert nq % ATTN_UNROLL == 0

    def body(q_ref, kv_ref, o_ref, lse_ref, m_sc, l_sc, acc_sc):
        qi = pl.program_id(2)
        q = q_ref[...]
        masks = _head_masks(ew)
        qh = [jnp.where(m, q, jnp.zeros_like(q)) for m in masks]

        def logits(h, k, kj):
            s = _dot_nt(qh[h], k)
            if scale != 1.0:
                s = s * scale
            row = lax.broadcasted_iota(jnp.int32, s.shape, 0)
            col = lax.broadcasted_iota(jnp.int32, s.shape, 1)
            return jnp.where(col - row <= (qi - kj) * tq, s, NEG_BIG)

        def sweep(step):
            def loop_body(t, carry):
                for u in range(ATTN_UNROLL):
                    step(t * ATTN_UNROLL + u)
                return carry

            lax.fori_loop(0, (qi + ATTN_UNROLL) // ATTN_UNROLL, loop_body, 0)

        def max_step(kj):
            k = kv_ref[pl.ds(pl.multiple_of(kj * tq, tq), tq), 0:PAIR_Q]
            for h in range(2):
                m_sc[h] = jnp.maximum(m_sc[h], _lane_halves(logits(h, k, kj), jnp.maximum))

        def sum_step(kj):
            rows = pl.ds(pl.multiple_of(kj * tq, tq), tq)
            k = kv_ref[rows, 0:PAIR_Q]
            v = kv_ref[rows, PAIR_Q:PAIR_KV]
            for h in range(2):
                s = logits(h, k, kj)
                m = m_sc[h]
                p = jnp.concatenate([jnp.exp(s[:, g * LANES:(g + 1) * LANES] - m) for g in range(tq // LANES)], axis=1)
                l_sc[h] += _lane_halves(p, jnp.add)
                acc_sc[h] += jnp.dot(p.astype(BF16), v, preferred_element_type=F32)

        m_sc[...] = jnp.full(m_sc.shape, NEG_BIG, F32)
        sweep(max_step)
        for h in range(2):
            m_sc[h] = jnp.broadcast_to(jnp.max(m_sc[h], axis=1, keepdims=True), (tq, LANES))
        l_sc[...] = jnp.zeros_like(l_sc)
        acc_sc[...] = jnp.zeros_like(acc_sc)
        sweep(sum_step)
        lane = lax.broadcasted_iota(jnp.int32, (tq, LANES), 1)
        lo = lane < HEAD_DIM
        l = [jnp.sum(l_sc[h], axis=1, keepdims=True) for h in range(2)]
        o_ref[...] = jnp.where(lo, acc_sc[0] / l[0], acc_sc[1] / l[1]).astype(BF16)
        lse_ref[...] = _head_rows(jnp.where(lo, m_sc[0] + jnp.log(l[0]), m_sc[1] + jnp.log(l[1])))

    return pl.pallas_call(
        body, name=name, grid=(B, P, nq),
        in_specs=[pl.BlockSpec((tq, PAIR_Q), lambda b, p, i: (b * nq + i, p)),
                  pl.BlockSpec((S, PAIR_KV), lambda b, p, i: (b, p))],
        out_specs=[pl.BlockSpec((tq, LANES), lambda b, p, i: (b * nq + i, p)),
                   pl.BlockSpec((None, None, 8, tq), lambda b, p, i: (b * nq + i, p, 0, 0))],
        out_shape=[jax.ShapeDtypeStruct((T, P * LANES), BF16), jax.ShapeDtypeStruct((T // tq, P, 8, tq), F32)],
        scratch_shapes=[pltpu.VMEM((2, tq, LANES), F32)] * 3,
        compiler_params=_cparams(("parallel", "parallel", "arbitrary")),
    )(qx, kvx)


def _attn_bwd(qx, kvx, o, lse, do, *, S, scale, ew, name, bias_grad=False):
    T = qx.shape[0]
    P = qx.shape[1] // PAIR_Q
    B = T // S
    tq = _tile(S, 256)
    nq = S // tq
    assert nq % ATTN_UNROLL == 0

    def body(q_ref, kv_ref, o_ref, lse_ref, do_ref, dq_ref, dkv_ref, *rest):
        kj = pl.program_id(2)
        if bias_grad:
            csum_ref, rsum_ref, dq_sc, delta_sc, dk_sc, dv_sc, cs_sc = rest
            cs_sc[...] = jnp.zeros_like(cs_sc)

            @pl.when(kj == 0)
            def _():
                rsum_ref[...] = jnp.zeros_like(rsum_ref)
        else:
            dq_sc, delta_sc, dk_sc, dv_sc = rest
        masks = _head_masks(ew)
        lane = lax.broadcasted_iota(jnp.int32, (tq, LANES), 1)
        lo = lane < HEAD_DIM
        vmask = [lo, jnp.logical_not(lo)]

        @pl.when(kj == 0)
        def _():
            dq_sc[...] = jnp.zeros_like(dq_sc)
            for c in range(nq):
                rows = pl.ds(c * tq, tq)
                x = do_ref[rows, :].astype(F32) * o_ref[rows, :].astype(F32)
                r0 = jnp.sum(jnp.where(lo, x, 0.0), axis=1, keepdims=True)
                r1 = jnp.sum(jnp.where(lo, 0.0, x), axis=1, keepdims=True)
                delta_sc[c] = _head_rows(jnp.where(lo, r0, r1))

        k = kv_ref[:, 0:PAIR_Q]
        v = kv_ref[:, PAIR_Q:PAIR_KV]
        kh = [jnp.where(m, k, jnp.zeros_like(k)) for m in masks]
        vh = [jnp.where(m, v, jnp.zeros_like(v)) for m in vmask]
        dk_sc[...] = jnp.zeros_like(dk_sc)
        dv_sc[...] = jnp.zeros_like(dv_sc)

        def step(qi):
            rows = pl.ds(pl.multiple_of(qi * tq, tq), tq)
            q = q_ref[rows, :]
            dov = do_ref[rows, :]
            lse8 = lse_ref[qi]
            dl8 = delta_sc[qi]
            for h in range(2):
                st = _dot_nt(kh[h], q)
                if scale != 1.0:
                    st = st * scale
                key = lax.broadcasted_iota(jnp.int32, st.shape, 0)
                qry = lax.broadcasted_iota(jnp.int32, st.shape, 1)
                st = jnp.where(key - qry <= (qi - kj) * tq, st, NEG_BIG)
                pt = jnp.exp(st - lse8[h:h + 1, :])
                dpt = _dot_nt(vh[h], dov)
                dst = pt * (dpt - dl8[h:h + 1, :])
                if bias_grad:
                    cs_sc[h] += _lane_halves(dst, jnp.add)
                    rsum_ref[qi, h:h + 1, :] += jnp.sum(dst, axis=0, keepdims=True)
                if scale != 1.0:
                    dst = dst * scale
                ptb = pt.astype(BF16)
                dstb = dst.astype(BF16)
                dv_sc[h] += jnp.dot(ptb, dov, preferred_element_type=F32)
                dk_sc[h] += jnp.dot(dstb, q, preferred_element_type=F32)
                dq_sc[rows, :] += _dot_tn(dstb, kh[h])

        def loop_body(t, carry):
            for u in range(ATTN_UNROLL):
                step(t * ATTN_UNROLL + u)
            return carry

        lax.fori_loop(kj // ATTN_UNROLL, nq // ATTN_UNROLL, loop_body, 0)
        dkv_ref[:, 0:PAIR_Q] = (jnp.where(masks[0], dk_sc[0], 0.0) + jnp.where(masks[1], dk_sc[1], 0.0)).astype(BF16)
        dkv_ref[:, PAIR_Q:PAIR_KV] = jnp.where(lo, dv_sc[0], dv_sc[1]).astype(BF16)
        if bias_grad:
            csum_ref[...] = jnp.where(lo, jnp.sum(cs_sc[0], axis=1, keepdims=True),
                                      jnp.sum(cs_sc[1], axis=1, keepdims=True))

        @pl.when(kj == nq - 1)
        def _():
            dq_ref[...] = dq_sc[...].astype(BF16)

    rows_spec = pl.BlockSpec((nq, None, 8, tq), lambda b, p, j: (b, p, 0, 0))
    out_specs = [pl.BlockSpec((S, PAIR_Q), lambda b, p, j: (b, p)),
                 pl.BlockSpec((tq, PAIR_KV), lambda b, p, j: (b * nq + j, p))]
    out_shape = [jax.ShapeDtypeStruct((T, P * PAIR_Q), BF16), jax.ShapeDtypeStruct((T, P * PAIR_KV), BF16)]
    scratch = [pltpu.VMEM((S, PAIR_Q), F32), pltpu.VMEM((nq, 8, tq), F32),
               pltpu.VMEM((2, tq, PAIR_Q), F32), pltpu.VMEM((2, tq, LANES), F32)]
    if bias_grad:
        out_specs += [pl.BlockSpec((tq, LANES), lambda b, p, j: (b * nq + j, p)), rows_spec]
        out_shape += [jax.ShapeDtypeStruct((T, P * LANES), F32), jax.ShapeDtypeStruct((T // tq, P, 8, tq), F32)]
        scratch.append(pltpu.VMEM((2, tq, LANES), F32))
    return pl.pallas_call(
        body, name=name, grid=(B, P, nq),
        in_specs=[pl.BlockSpec((S, PAIR_Q), lambda b, p, j: (b, p)),
                  pl.BlockSpec((tq, PAIR_KV), lambda b, p, j: (b * nq + j, p)),
                  pl.BlockSpec((S, LANES), lambda b, p, j: (b, p)), rows_spec,
                  pl.BlockSpec((S, LANES), lambda b, p, j: (b, p))],
        out_specs=out_specs, out_shape=out_shape, scratch_shapes=scratch,
        compiler_params=_cparams(("parallel", "parallel", "arbitrary")),
    )(qx, kvx, o, lse, do)


def _fox_consts(P):
    H = 2 * P
    eq = np.zeros((3 * LANES, P * LANES), np.float32)
    ek = np.zeros((3 * LANES, P * LANES), np.float32)
    ones_q = np.zeros((1, P * LANES), np.float32)
    ones_k = np.zeros((1, P * LANES), np.float32)
    for h in range(H):
        base = (h // 2) * LANES + FOX_EXTRA * (h % 2)
        for part in range(3):
            eq[part * LANES + h, base + part] = 1.0
            ones_q[0, base + 3 + part] = 1.0
            ones_k[0, base + part] = 1.0
            ek[part * LANES + h, base + 3 + part] = -1.0
    return eq, ek, ones_q, ones_k


def _split3(f):
    hi = f.astype(BF16)
    r = f - hi.astype(F32)
    mid = r.astype(BF16)
    lo = (r - mid.astype(F32)).astype(BF16)
    return hi, mid, lo


def _tri_sum(tri, x):
    hi, mid, lo = _split3(x)
    return (jnp.dot(tri, hi, preferred_element_type=F32) + jnp.dot(tri, mid, preferred_element_type=F32)
            + jnp.dot(tri, lo, preferred_element_type=F32))


def _log1p_pos(e):
    return jnp.where(e < 0.01, e * (1.0 - e * (0.5 - e * (1.0 / 3.0))), jnp.log(1.0 + e))


def _fox_prep(qkv, fl, b_row, *, S, D, name):
    T = qkv.shape[0]
    P = D // LANES
    B = T // S
    tt = _tile(S, 256)
    per = S // tt
    eq, ek, ones_q, ones_k = _fox_consts(P)
    q_scale = HEAD_DIM ** -0.5

    def body(q_ref, k_ref, v_ref, fl_ref, b_ref, eq_ref, ek_ref, oq_ref, ok_ref, qx_ref, kvx_ref, carry):
        i = pl.program_id(1)

        @pl.when(i == 0)
        def _():
            carry[...] = jnp.zeros_like(carry)

        z = fl_ref[...] + b_ref[...]
        logf = jnp.minimum(z, 0.0) - _log1p_pos(jnp.exp(-jnp.abs(z)))
        row = lax.broadcasted_iota(jnp.int32, (tt, tt), 0)
        col = lax.broadcasted_iota(jnp.int32, (tt, tt), 1)
        tri = (col <= row).astype(BF16)
        f = _tri_sum(tri, logf) + carry[...]
        carry[...] = f[tt - 1:tt, :]
        parts = jnp.concatenate(_split3(f), axis=1)
        xq = jnp.dot(parts, eq_ref[...], preferred_element_type=F32) + oq_ref[...]
        xk = jnp.dot(parts, ek_ref[...], preferred_element_type=F32) + ok_ref[...]
        for p in range(P):
            c = slice(p * LANES, (p + 1) * LANES)
            qx_ref[:, p * PAIR_Q:p * PAIR_Q + LANES] = (q_ref[:, c].astype(F32) * q_scale).astype(BF16)
            qx_ref[:, p * PAIR_Q + LANES:(p + 1) * PAIR_Q] = xq[:, c].astype(BF16)
            kvx_ref[:, p * PAIR_KV:p * PAIR_KV + LANES] = k_ref[:, c]
            kvx_ref[:, p * PAIR_KV + LANES:p * PAIR_KV + PAIR_Q] = xk[:, c].astype(BF16)
            kvx_ref[:, p * PAIR_KV + PAIR_Q:(p + 1) * PAIR_KV] = v_ref[:, c]

    tok = lambda b, i: (b * per + i, 0)
    const = lambda b, i: (0, 0)
    return pl.pallas_call(
        body, name=name, grid=(B, per),
        in_specs=[pl.BlockSpec((tt, D), lambda b, i: (b * per + i, 0)),
                  pl.BlockSpec((tt, D), lambda b, i: (b * per + i, 1)),
                  pl.BlockSpec((tt, D), lambda b, i: (b * per + i, 2)),
                  pl.BlockSpec((tt, LANES), tok), pl.BlockSpec((1, LANES), const),
                  pl.BlockSpec(eq.shape, const), pl.BlockSpec(ek.shape, const),
                  pl.BlockSpec(ones_q.shape, const), pl.BlockSpec(ones_k.shape, const)],
        out_specs=[pl.BlockSpec((tt, P * PAIR_Q), tok), pl.BlockSpec((tt, P * PAIR_KV), tok)],
        out_shape=[jax.ShapeDtypeStruct((T, P * PAIR_Q), BF16), jax.ShapeDtypeStruct((T, P * PAIR_KV), BF16)],
        scratch_shapes=[pltpu.VMEM((1, LANES), F32)],
        compiler_params=_cparams(("arbitrary", "arbitrary")),
    )(qkv, qkv, qkv, fl, b_row, jnp.asarray(eq, BF16), jnp.asarray(ek, BF16), jnp.asarray(ones_q), jnp.asarray(ones_k))


def _fox_unprep(dqx, dkvx, csum, rsum, fl, b_row, *, S, D, name):
    T = dqx.shape[0]
    P = D // LANES
    B = T // S
    tt = _tile(S, 256)
    per = S // tt
    q_scale = HEAD_DIM ** -0.5

    def body(dq_ref, dkv_ref, cs_ref, rs_ref, fl_ref, b_ref, dqkv_ref, dfl_ref, db_ref, carry):
        b = pl.program_id(0)
        i = pl.program_id(1)

        @pl.when(i == 0)
        def _():
            carry[...] = jnp.zeros_like(carry)

        @pl.when((i == 0) & (b == 0))
        def _():
            db_ref[...] = jnp.zeros_like(db_ref)

        df = rs_ref[...] - cs_ref[...]
        for p in range(P):
            rq = slice(p * LANES, (p + 1) * LANES)
            dqkv_ref[:, rq] = (dq_ref[:, p * PAIR_Q:p * PAIR_Q + LANES].astype(F32) * q_scale).astype(BF16)
            dqkv_ref[:, D + p * LANES:D + (p + 1) * LANES] = dkv_ref[:, p * PAIR_KV:p * PAIR_KV + LANES]
            dqkv_ref[:, 2 * D + p * LANES:2 * D + (p + 1) * LANES] = dkv_ref[:, p * PAIR_KV + PAIR_Q:(p + 1) * PAIR_KV]
        row = lax.broadcasted_iota(jnp.int32, (tt, tt), 0)
        col = lax.broadcasted_iota(jnp.int32, (tt, tt), 1)
        tri = (col >= row).astype(BF16)
        dlogf = _tri_sum(tri, df) + carry[...]
        carry[...] = dlogf[0:1, :]
        z = fl_ref[...] + b_ref[...]
        e = jnp.exp(-jnp.abs(z))
        sig_neg = jnp.where(z >= 0.0, e, 1.0) / (1.0 + e)
        dfl = dlogf * sig_neg
        dfl_ref[...] = dfl.astype(BF16)
        db_ref[...] += jnp.sum(dfl, axis=0, keepdims=True)

    rev = lambda b, i: (b * per + per - 1 - i, 0)
    const = lambda b, i: (0, 0)
    return pl.pallas_call(
        body, name=name, grid=(B, per),
        in_specs=[pl.BlockSpec((tt, P * PAIR_Q), rev), pl.BlockSpec((tt, P * PAIR_KV), rev),
                  pl.BlockSpec((tt, LANES), rev), pl.BlockSpec((tt, LANES), rev), pl.BlockSpec((tt, LANES), rev),
                  pl.BlockSpec((1, LANES), const)],
        out_specs=[pl.BlockSpec((tt, 3 * D), rev), pl.BlockSpec((tt, LANES), rev), pl.BlockSpec((1, LANES), const)],
        out_shape=[jax.ShapeDtypeStruct((T, 3 * D), BF16), jax.ShapeDtypeStruct((T, LANES), BF16),
                   jax.ShapeDtypeStruct((1, LANES), F32)],
        scratch_shapes=[pltpu.VMEM((1, LANES), F32)],
        compiler_params=_cparams(("arbitrary", "arbitrary")),
    )(dqx, dkvx, csum, rsum, fl, b_row)


def _rms(x):
    r = lax.rsqrt(jnp.mean(x * x, axis=-1, keepdims=True) + NORM_EPS)
    return x * r, r


def _mla_mid(lat, gq, gkv, cos_t, sin_s, *, name):
    T, W = lat.shape
    Rq = W - 2 * LANES
    tt = _tile(T, 512)

    def body(l_ref, gq_ref, gkv_ref, c_ref, s_ref, o_ref):
        nq, _ = _rms(l_ref[:, 0:Rq])
        nkv, _ = _rms(l_ref[:, Rq:Rq + LANES])
        o_ref[:, 0:Rq] = (nq * gq_ref[...]).astype(BF16)
        o_ref[:, Rq:Rq + LANES] = (nkv * gkv_ref[...]).astype(BF16)
        o_ref[:, Rq + LANES:W] = _rope128(l_ref[:, Rq + LANES:W], c_ref[...], s_ref[...]).astype(BF16)

    return pl.pallas_call(
        body, name=name, grid=(T // tt,),
        in_specs=[pl.BlockSpec((tt, W), lambda i: (i, 0)), pl.BlockSpec((1, Rq), lambda i: (0, 0)),
                  pl.BlockSpec((1, LANES), lambda i: (0, 0)), pl.BlockSpec((tt, LANES), lambda i: (i, 0)),
                  pl.BlockSpec((tt, LANES), lambda i: (i, 0))],
        out_specs=pl.BlockSpec((tt, W), lambda i: (i, 0)),
        out_shape=jax.ShapeDtypeStruct((T, W), BF16),
        compiler_params=_cparams(("parallel",)),
    )(lat, gq, gkv, cos_t, sin_s)


def _mla_mid_bwd(lat, dcq, dckr, gq, gkv, cos_t, sin_s, *, name):
    T, W = lat.shape
    Rq = W - 2 * LANES
    tt = _tile(T, 512)

    def norm_bwd(x, dy, g):
        n, r = _rms(x)
        dn = dy * g
        return r * (dn - n * jnp.mean(dn * n, axis=-1, keepdims=True)), jnp.sum(dy * n, axis=0, keepdims=True)

    def body(l_ref, dq_ref, dk_ref, gq_ref, gkv_ref, c_ref, s_ref, o_ref, dgq_ref, dgkv_ref):
        i = pl.program_id(0)

        @pl.when(i == 0)
        def _():
            dgq_ref[...] = jnp.zeros_like(dgq_ref)
            dgkv_ref[...] = jnp.zeros_like(dgkv_ref)

        dxq, dgq = norm_bwd(l_ref[:, 0:Rq], dq_ref[...], gq_ref[...])
        dxkv, dgkv = norm_bwd(l_ref[:, Rq:Rq + LANES], dk_ref[:, 0:LANES], gkv_ref[...])
        o_ref[:, 0:Rq] = dxq.astype(BF16)
        o_ref[:, Rq:Rq + LANES] = dxkv.astype(BF16)
        o_ref[:, Rq + LANES:W] = _rope128(dk_ref[:, LANES:2 * LANES], c_ref[...], -s_ref[...]).astype(BF16)
        dgq_ref[...] += dgq
        dgkv_ref[...] += dgkv

    return pl.pallas_call(
        body, name=name, grid=(T // tt,),
        in_specs=[pl.BlockSpec((tt, W), lambda i: (i, 0)), pl.BlockSpec((tt, Rq), lambda i: (i, 0)),
                  pl.BlockSpec((tt, 2 * LANES), lambda i: (i, 0)), pl.BlockSpec((1, Rq), lambda i: (0, 0)),
                  pl.BlockSpec((1, LANES), lambda i: (0, 0)), pl.BlockSpec((tt, LANES), lambda i: (i, 0)),
                  pl.BlockSpec((tt, LANES), lambda i: (i, 0))],
        out_specs=[pl.BlockSpec((tt, W), lambda i: (i, 0)), pl.BlockSpec((1, Rq), lambda i: (0, 0)),
                   pl.BlockSpec((1, LANES), lambda i: (0, 0))],
        out_shape=[jax.ShapeDtypeStruct((T, W), BF16), jax.ShapeDtypeStruct((1, Rq), F32),
                   jax.ShapeDtypeStruct((1, LANES), F32)],
        compiler_params=_cparams(("arbitrary",)),
    )(lat, dcq, dckr, gq, gkv, cos_t, sin_s)


def _uq_to_pairs(w):
    Rq = w.shape[0]
    P = w.shape[1] // (2 * (HEAD_DIM + ROPE_DIM))
    w4 = w.reshape(Rq, P, 2, HEAD_DIM + ROPE_DIM)
    nope = w4[..., :HEAD_DIM].reshape(Rq, P, 2 * HEAD_DIM)
    rope = w4[..., HEAD_DIM:].reshape(Rq, P, 2 * ROPE_DIM)
    pad = jnp.zeros((Rq, P, PAIR_Q - 2 * HEAD_DIM - 2 * ROPE_DIM), w.dtype)
    return jnp.concatenate([nope, rope, pad], axis=-1).reshape(Rq, P * PAIR_Q)


def _uq_from_pairs(g):
    Rq = g.shape[0]
    P = g.shape[1] // PAIR_Q
    g3 = g.reshape(Rq, P, PAIR_Q)
    nope = g3[..., :2 * HEAD_DIM].reshape(Rq, P, 2, HEAD_DIM)
    rope = g3[..., 2 * HEAD_DIM:2 * HEAD_DIM + 2 * ROPE_DIM].reshape(Rq, P, 2, ROPE_DIM)
    return jnp.concatenate([nope, rope], axis=-1).reshape(Rq, P * 2 * (HEAD_DIM + ROPE_DIM))


def _ukv_to_pairs(w):
    P = w.shape[1] // (4 * HEAD_DIM)
    w4 = w.reshape(KV_RANK, P, 2, 2 * HEAD_DIM)
    kn = w4[..., :HEAD_DIM].reshape(KV_RANK, P, 2 * HEAD_DIM)
    vv = w4[..., HEAD_DIM:].reshape(KV_RANK, P, 2 * HEAD_DIM)
    top = jnp.concatenate([kn, jnp.zeros((KV_RANK, P, LANES), w.dtype), vv], axis=-1)
    place = np.zeros((LANES, P, PAIR_KV), np.float32)
    for r in range(ROPE_DIM):
        place[r, :, LANES + r] = 1.0
        place[r, :, LANES + ROPE_DIM + r] = 1.0
    return jnp.concatenate([top, jnp.asarray(place, w.dtype)], axis=0).reshape(KV_RANK + LANES, P * PAIR_KV)


def _ukv_from_pairs(g):
    P = g.shape[1] // PAIR_KV
    g3 = g[:KV_RANK].reshape(KV_RANK, P, PAIR_KV)
    kn = g3[..., :2 * HEAD_DIM].reshape(KV_RANK, P, 2, HEAD_DIM)
    vv = g3[..., PAIR_Q:].reshape(KV_RANK, P, 2, HEAD_DIM)
    return jnp.concatenate([kn, vv], axis=-1).reshape(KV_RANK, P * 4 * HEAD_DIM)


def _mlp_fwd(h2, w, i, x1, gate, *, S):
    p, u = _mm(h2, w["mlp_w1"], "nn", name=f"mlp_up_{i}", b_layer=i, out_dtypes=(BF16, BF16),
               epilogue=lambda acc: (acc, jnp.square(jnp.maximum(acc, 0.0))))
    x2, z = _mm(u, w["mlp_w2"], "nn", name=f"mlp_down_{i}", b_layer=i, out_dtypes=(F32, F32), extras=(x1,),
                rowvecs=(gate,), seq=S, epilogue=lambda acc, xr, g: (xr + g * acc, acc))
    return x2, (p, u, z)


STACKED_GRADS = ("fox_out", "mla_down", "mla_uq", "mla_ukv", "mla_out", "mlp_w1", "mlp_w2")


def _local_step(x, target, pos_f, inv_freq_row, sign_row, mod, w, slots, *, S, after_layer0=None, before_layer0=None):
    T, D = x.shape
    L = mod.shape[0]
    L2 = len(w["fox_out"])
    n_split = w["mlp_w1"][0][0].shape[1]
    cos_t, sin_s = _rope_tables(pos_f, inv_freq_row, sign_row)
    saved = []
    for i in range(L):
        j = i // 2
        sh_m, sc_m, g_m, sh_f, sc_f, g_f = (mod[i, s] for s in range(6))
        h = _norm_mod(x, w["norm_mix_g"][i], sc_m, sh_m, S=S, name=f"norm_mix_{i}")
        if i % 2 == 0:
            qkv = _mm(h, w["fox_qkv"], "nn", name=f"fox_qkv_{i}", b_layer=j, out_dtypes=(BF16,))
            fl = _mm(h, w["fox_f"], "nn", name=f"fox_f_{i}", b_layer=j)
            qx, kvx = _fox_prep(qkv, fl, w["fox_b"][j], S=S, D=D, name=f"fox_prep_{i}")
            o, lse = _attn_fwd(qx, kvx, S=S, scale=1.0, ew=FOX_EXTRA, name=f"fox_attn_{i}")
            mix = (qx, kvx, o, lse, fl)
            w_out = w["fox_out"]
        else:
            lat = _mm(h, w["mla_down"], "nn", name=f"mla_down_{i}", b_layer=j)
            Rq = lat.shape[1] - 2 * LANES
            cqr = _mla_mid(lat, w["mla_gq"][j], w["mla_gkv"][j], cos_t, sin_s, name=f"mla_mid_{i}")
            qx = _mm(cqr, w["mla_uq"], "nn", name=f"mla_uq_{i}", b_layer=j, out_dtypes=(BF16,), a_sz=Rq, tk=Rq,
                     tables=(cos_t, sin_s), epilogue=lambda acc, c, s: (_rope_pairs(acc, c, s, 1.0),))
            kvx = _mm(cqr, w["mla_ukv"], "nn", name=f"mla_ukv_{i}", b_layer=j, out_dtypes=(BF16,), a_off=Rq,
                      a_sz=2 * LANES, tk=2 * LANES, tn=PAIR_KV)
            o, lse = _attn_fwd(qx, kvx, S=S, scale=(HEAD_DIM + ROPE_DIM) ** -0.5, ew=ROPE_DIM, name=f"mla_attn_{i}")
            mix = (qx, kvx, o, lse, lat, cqr)
            w_out = w["mla_out"]
        x1, y = _mm(o, w_out, "nn", name=f"mix_out_{i}", b_layer=j, out_dtypes=(F32, F32), extras=(x,),
                    rowvecs=(g_m,), seq=S, epilogue=lambda acc, xr, g: (xr + g * acc, acc))
        h2 = _norm_mod(x1, w["norm_mlp_g"][i], sc_f, sh_f, S=S, name=f"norm_mlp_{i}")
        x2, mlp = _mlp_fwd(h2, w, i, x1, g_f, S=S)
        saved.append((x, h, mix, y, x1, h2, mlp))
        x = x2
        if i == 0 and after_layer0 is not None:
            w = after_layer0(x, w)

    dx, dg_final, loss = _final_loss(x, target, w["final_norm_g"])

    grads = {k: [None] * len(w[k]) for k in ("norm_mix_g", "norm_mlp_g", "fox_b", "mla_gq", "mla_gkv")}
    grads.update({k: [None] * L2 for k in ("fox_qkv", "fox_f")})
    grads.update({k: {} for k in STACKED_GRADS})
    grads["final_norm_g"] = dg_final

    def stacked(key, layer, _, a, b, **kw):
        group, idx, count = slots[(key, layer)]
        grads[key][group] = _mm(a.T, b, "nn", out_stack=(grads[key].get(group), idx, count), **kw)

    dmod = [None] * L
    for i in reversed(range(L)):
        j = i // 2
        x0, h, mix, y, x1, h2, (p, u, z) = saved[i]
        sh_m, sc_m, g_m, sh_f, sc_f, g_f = (mod[i, s] for s in range(6))
        if i == 0 and before_layer0 is not None:
            g_f = g_f + before_layer0(grads)[0, 0]
        dz, dg_f = _gate_bwd(dx, z, g_f, S=S, name=f"gate_mlp_bwd_{i}")
        stacked("mlp_w2", i, L, u, dz, name=f"mlp_w2_grad_{i}")
        dp = _mm(dz, w["mlp_w2"], "nt", name=f"mlp_down_bwd_{i}", b_layer=i, out_dtypes=(BF16,), extras=(p,),
                 epilogue=lambda acc, pv: (acc * (2.0 * jnp.maximum(pv.astype(F32), 0.0)),))
        stacked("mlp_w1", i, L, h2, dp, name=f"mlp_w1_grad_{i}", out_split=n_split)
        dh2 = _mm(dp, w["mlp_w1"], "nt", name=f"mlp_up_bwd_{i}", b_layer=i)
        dx1, dsh_f, dsc_f, dgn = _norm_mod_bwd(x1, dh2, dx, w["norm_mlp_g"][i], sc_f, S=S, name=f"norm_mlp_bwd_{i}")
        grads["norm_mlp_g"][i] = dgn
        dy, dg_m = _gate_bwd(dx1, y, g_m, S=S, name=f"gate_mix_bwd_{i}")
        if i % 2 == 0:
            qx, kvx, o, lse, fl = mix
            stacked("fox_out", j, L2, o, dy, name=f"fox_out_grad_{i}")
            do = _mm(dy, w["fox_out"], "nt", name=f"fox_out_bwd_{i}", b_layer=j, out_dtypes=(BF16,))
            dqx, dkvx, csum, rsum = _attn_bwd(qx, kvx, o, lse, do, S=S, scale=1.0, ew=FOX_EXTRA,
                                              name=f"fox_attn_bwd_{i}", bias_grad=True)
            n_heads = D // HEAD_DIM
            csum = jnp.pad(csum.reshape(T, n_heads, HEAD_DIM)[:, :, 0], ((0, 0), (0, LANES - n_heads)))
            rsum = jnp.transpose(rsum[:, :, :2, :], (0, 3, 1, 2)).reshape(T, n_heads)
            rsum = jnp.pad(rsum, ((0, 0), (0, LANES - n_heads)))
            dqkv, dfl, db = _fox_unprep(dqx, dkvx, csum, rsum, fl, w["fox_b"][j], S=S, D=D, name=f"fox_unprep_{i}")
            grads["fox_b"][j] = db
            h_t = h.T
            grads["fox_qkv"][j] = _mm(h_t, dqkv, "nn", name=f"fox_qkv_grad_{i}")
            grads["fox_f"][j] = _mm(h_t, dfl, "nn", name=f"fox_f_grad_{i}")
            dh_f = _mm(dfl, w["fox_f"], "nt", name=f"fox_f_bwd_{i}", b_layer=j)
            dh = _mm(dqkv, w["fox_qkv"], "nt", name=f"fox_qkv_bwd_{i}", b_layer=j, extras=(dh_f,),
                     epilogue=lambda acc, e: (acc + e,))
        else:
            qx, kvx, o, lse, lat, cqr = mix
            Rq = lat.shape[1] - 2 * LANES
            stacked("mla_out", j, L2, o, dy, name=f"mla_out_grad_{i}")
            do = _mm(dy, w["mla_out"], "nt", name=f"mla_out_bwd_{i}", b_layer=j, out_dtypes=(BF16,))
            dqx, dkvx = _attn_bwd(qx, kvx, o, lse, do, S=S, scale=(HEAD_DIM + ROPE_DIM) ** -0.5, ew=ROPE_DIM,
                                  name=f"mla_attn_bwd_{i}")
            dqpre = _unrope(dqx, cos_t, sin_s)
            stacked("mla_uq", j, L2, cqr[:, :Rq], dqpre, name=f"mla_uq_grad_{i}", out_split=n_split)
            stacked("mla_ukv", j, L2, cqr[:, Rq:], dkvx, name=f"mla_ukv_grad_{i}", tn=PAIR_KV, out_split=n_split)
            dcq = _mm(dqpre, w["mla_uq"], "nt", name=f"mla_uq_bwd_{i}", b_layer=j)
            dckr = _mm(dkvx, w["mla_ukv"], "nt", name=f"mla_ukv_bwd_{i}", b_layer=j, tk=PAIR_KV * 2)
            dlat, dgq, dgkv = _mla_mid_bwd(lat, dcq, dckr, w["mla_gq"][j], w["mla_gkv"][j], cos_t, sin_s,
                                           name=f"mla_mid_bwd_{i}")
            grads["mla_gq"][j] = dgq
            grads["mla_gkv"][j] = dgkv
            stacked("mla_down", j, L2, h, dlat, name=f"mla_down_grad_{i}")
            dh = _mm(dlat, w["mla_down"], "nt", name=f"mla_down_bwd_{i}", b_layer=j)
        dx, dsh_m, dsc_m, dgn = _norm_mod_bwd(x0, dh, dx1, w["norm_mix_g"][i], sc_m, S=S, name=f"norm_mix_bwd_{i}")
        grads["norm_mix_g"][i] = dgn
        dmod[i] = jnp.stack([dsh_m, dsc_m, dg_m, dsh_f, dsc_f, dg_f])
    return loss, dx, jnp.stack(dmod), grads


GATHERED = ("fox_in", "fox_out", "mla_down", "mla_uq", "mla_ukv", "mla_out", "mlp_w1", "mlp_w2")
ROW_SHARDED = ("fox_out", "mla_down", "mla_out", "mlp_w2")


def _shard_layouts(wts):
    dkv = wts["mla_w_dkv"]
    dkv = jnp.pad(dkv, ((0, 0), (0, 0), (0, 2 * LANES - dkv.shape[2])))
    return {
        "fox_in": _pad_lanes(wts["fox_w_in"].astype(BF16)),
        "fox_out": wts["fox_w_out"].astype(BF16),
        "mla_down": jnp.concatenate([wts["mla_w_dq"], dkv], axis=2).astype(BF16),
        "mla_uq": jax.vmap(_uq_to_pairs)(wts["mla_w_uq"].astype(BF16)),
        "mla_ukv": jax.vmap(_ukv_to_pairs)(wts["mla_w_ukv"].astype(BF16)),
        "mla_out": wts["mla_w_out"].astype(BF16),
        "mlp_w1": wts["mlp_w1"].astype(BF16),
        "mlp_w2": wts["mlp_w2"].astype(BF16),
    }


def _small_layouts(small):
    return {
        "fox_b": [jnp.pad(b, (0, LANES - b.shape[0]))[None, :] for b in small["fox_b_f"]],
        "mla_gq": [g[None, :] for g in small["mla_q_norm_g"]],
        "mla_gkv": [g[None, :] for g in small["mla_kv_norm_g"]],
        "norm_mix_g": [g[None, :] for g in small["norm_mix_g"]],
        "norm_mlp_g": [g[None, :] for g in small["norm_mlp_g"]],
        "final_norm_g": small["final_norm_g"][None, :],
    }


def _comm_groups(L, L2):
    first = [("fox_in", 0, 1), ("fox_out", 0, 1), ("mlp_w1", 0, 1), ("mlp_w2", 0, 1)]
    rest = [("fox_in", 1, L2 - 1), ("fox_out", 1, L2 - 1), ("mla_down", 0, L2), ("mla_uq", 0, L2),
            ("mla_ukv", 0, L2), ("mla_out", 0, L2), ("mlp_w1", 1, L - 1), ("mlp_w2", 1, L - 1)]
    return {"first": first, "rest": [e for e in rest if e[2] > 0]}


def _layer_slots(groups):
    return {(n, s + l): (g, l, cnt) for g, entries in groups.items() for n, s, cnt in entries for l in range(cnt)}


def _pad_lanes(a):
    cols = a.shape[-1]
    return jnp.pad(a, [(0, 0)] * (a.ndim - 1) + [(0, -cols % LANES)])


def _weight_views(name, gathered, D, n_fox_heads):
    n, ns, rows, cols = gathered.shape
    if name == "fox_in":
        true_cols = (3 * D + n_fox_heads) // ns
        fox = jnp.concatenate([gathered[:, k, :, :true_cols] for k in range(ns)], axis=-1)
        return {"fox_qkv": fox[:, :, :3 * D], "fox_f": _pad_lanes(fox[:, :, 3 * D:])}
    if name in ROW_SHARDED:
        return {name: gathered.reshape(n, ns * rows, cols)}
    return {name: gathered}


def _grad_pieces(name, g, qkv_f, n_fox_heads, ns):
    if name == "fox_in":
        fox = jnp.stack([jnp.concatenate([a, b[:, :n_fox_heads]], axis=1) for a, b in qkv_f])
        cols = fox.shape[2] // ns
        return jnp.stack([_pad_lanes(fox[:, :, k * cols:(k + 1) * cols]) for k in range(ns)], axis=1)
    if name in ROW_SHARDED:
        return g.reshape(g.shape[0], ns, g.shape[1] // ns, g.shape[2])
    return g


def _small_grads(g, n_fox_heads):
    return {
        "norm_mix_g": jnp.concatenate(g["norm_mix_g"], axis=0),
        "norm_mlp_g": jnp.concatenate(g["norm_mlp_g"], axis=0),
        "final_norm_g": g["final_norm_g"][0],
        "fox_b_f": jnp.concatenate(g["fox_b"], axis=0)[:, :n_fox_heads],
        "mla_q_norm_g": jnp.concatenate(g["mla_gq"], axis=0),
        "mla_kv_norm_g": jnp.concatenate(g["mla_gkv"], axis=0),
    }


def _silu(c):
    return c * (1.0 / (1.0 + jnp.exp(-c)))


def _ada_fwd(c_all, ada_w, ada_b_cols):
    L, D, C = ada_w.shape
    Bg = c_all.shape[0]
    tc = _tile(C, 512)

    def body(c_ref, w_ref, b_ref, o_ref):
        ca = _silu(c_ref[...]).astype(BF16)
        o_ref[...] = jnp.dot(ca, w_ref[...].astype(BF16), preferred_element_type=F32) + b_ref[...]

    return pl.pallas_call(
        body, name="ada_fwd", grid=(L, C // tc),
        in_specs=[pl.BlockSpec((Bg, D), lambda l, j: (0, 0)), pl.BlockSpec((None, D, tc), lambda l, j: (l, 0, j)),
                  pl.BlockSpec((None, 1, tc), lambda l, j: (l, 0, j))],
        out_specs=pl.BlockSpec((None, Bg, tc), lambda l, j: (l, 0, j)),
        out_shape=jax.ShapeDtypeStruct((L, Bg, C), F32),
        compiler_params=_cparams(("parallel", "parallel")),
    )(c_all, ada_w, ada_b_cols)


def _ada_bwd(c_all, dmod_cols):
    L, Bg, C = dmod_cols.shape
    D = c_all.shape[1]
    tc = _tile(C, 512)

    def body(c_ref, d_ref, o_ref):
        ca = _silu(c_ref[...]).astype(BF16)
        o_ref[...] = _dot_tn(ca, d_ref[...].astype(BF16))

    return pl.pallas_call(
        body, name="ada_bwd", grid=(L, C // tc),
        in_specs=[pl.BlockSpec((Bg, D), lambda l, j: (0, 0)), pl.BlockSpec((None, Bg, tc), lambda l, j: (l, 0, j))],
        out_specs=pl.BlockSpec((None, D, tc), lambda l, j: (l, 0, j)),
        out_shape=jax.ShapeDtypeStruct((L, D, C), F32),
        compiler_params=_cparams(("parallel", "parallel")),
    )(c_all, dmod_cols)


def _adamw_update(w, gv, m, v):
    mn = ADAM_B1 * m + (1.0 - ADAM_B1) * gv
    vn = ADAM_B2 * v + (1.0 - ADAM_B2) * jnp.square(gv)
    m_hat = mn / (1.0 - ADAM_B1 ** ADAM_STEP)
    v_hat = vn / (1.0 - ADAM_B2 ** ADAM_STEP)
    return -ADAM_LR * (m_hat / (jnp.sqrt(v_hat) + ADAM_EPS) + ADAM_WD * w), mn, vn


def _adamw(w, g, m, v, *, name):
    shape = w.shape
    C = shape[-1]
    R = int(np.prod(shape[:-1])) if len(shape) > 1 else 1
    w2, g2, m2, v2 = (a.reshape(R, C) for a in (w, g, m, v))
    tr = _row_tile(R, C)

    def body(w_ref, g_ref, m_ref, v_ref, d_ref, nm_ref, nv_ref):
        d_ref[...], nm_ref[...], nv_ref[...] = _adamw_update(w_ref[...], g_ref[...], m_ref[...], v_ref[...])

    spec = pl.BlockSpec((tr, C), lambda i: (i, 0))
    out = pl.pallas_call(
        body, name=name, grid=(R // tr,), in_specs=[spec] * 4, out_specs=[spec] * 3,
        out_shape=[jax.ShapeDtypeStruct((R, C), F32)] * 3, compiler_params=_cparams(("parallel",)),
    )(w2, g2, m2, v2)
    return tuple(a.reshape(shape) for a in out)


def _adamw_halves(w, g_own, g_peer, m, v, c_idx, *, name):
    L, rows, C = w.shape
    R = rows // 2
    tr = _row_tile(R, C)

    def body(c_ref, w_ref, go_ref, gp_ref, m_ref, v_ref, g_ref, d_ref, nm_ref, nv_ref):
        gv = jnp.where(pl.program_id(1) == c_ref[0], go_ref[...], gp_ref[...])
        g_ref[...] = gv
        d_ref[...], nm_ref[...], nv_ref[...] = _adamw_update(w_ref[...], gv, m_ref[...], v_ref[...])

    full = pl.BlockSpec((None, None, tr, C), lambda l, hh, i, c_ref: (l, hh, i, 0))
    half = pl.BlockSpec((None, tr, C), lambda l, hh, i, c_ref: (l, i, 0))
    grid_spec = pltpu.PrefetchScalarGridSpec(
        num_scalar_prefetch=1, grid=(L, 2, R // tr), in_specs=[full, half, half, full, full], out_specs=[full] * 4)
    split = lambda a: a.reshape(L, 2, R, C)
    out = pl.pallas_call(
        body, name=name, grid_spec=grid_spec, out_shape=[jax.ShapeDtypeStruct((L, 2, R, C), F32)] * 4,
        compiler_params=_cparams(("parallel", "parallel", "parallel")),
    )(c_idx, split(w), g_own, g_peer, split(m), split(v))
    return tuple(a.reshape(w.shape) for a in out)


def _sum_gathered(dm8, sm8):
    n_dev, Bl, R, D = dm8.shape
    Rs = sm8.shape[1]

    def body(dm_ref, sm_ref, ob_ref, os_ref):
        acc_b = jnp.zeros((R, D), F32)
        acc_s = jnp.zeros((Rs, D), F32)
        for d in range(n_dev):
            for b in range(Bl):
                acc_b = acc_b + dm_ref[d, b]
            acc_s = acc_s + sm_ref[d]
        ob_ref[...] = acc_b
        os_ref[...] = acc_s

    return pl.pallas_call(
        body, name="sum_gathered",
        out_shape=[jax.ShapeDtypeStruct((R, D), F32), jax.ShapeDtypeStruct((Rs, D), F32)],
        compiler_params=_cparams(None),
    )(dm8, sm8)


N_DEV = 8
N_CHIP = 4
ANY = pl.BlockSpec(memory_space=pl.ANY)
HBM = pl.BlockSpec(memory_space=pltpu.HBM)
SEM = pl.BlockSpec(memory_space=pltpu.SEMAPHORE)
DATAFLOW = pltpu.SideEffectType.DATAFLOW_SIDE_EFFECTING


def _mesh_pos():
    return lax.axis_index("x"), lax.axis_index("y"), lax.axis_index("c")


def _all_gather8(block, *, name, in_vmem):
    R, W = block.shape

    def body(x_ref, out_ref, send_sems, recv_sems, local_sem):
        x, y, c = _mesh_pos()
        me, sibling = (x, y, c), (x, y, 1 - c)
        chips = [(1 - x, y), (x, 1 - y), (1 - x, 1 - y)]

        def slot(px, py, pc):
            return out_ref.at[4 * px + 2 * py + pc]

        def copy(k, blk, to, src=None):
            return pltpu.make_async_remote_copy(
                src_ref=slot(*blk) if src is None else src, dst_ref=slot(*blk),
                send_sem=send_sems.at[k], recv_sem=recv_sems.at[k], device_id=to, device_id_type=MESH_ID)

        mine = pltpu.make_async_copy(x_ref, slot(*me), local_sem)
        mine.start()
        first = [copy(0, me, sibling, src=x_ref)]
        first += [copy(1 + j, me, (*chip, c), src=x_ref) for j, chip in enumerate(chips)]
        for cp in first:
            cp.start()
        passed = [copy(4 + j, (*chip, c), sibling) for j, chip in enumerate(chips)]
        for j, chip in enumerate(chips):
            copy(1 + j, (*chip, c), me).wait_recv()
            passed[j].start()
        copy(0, sibling, me).wait_recv()
        for j, chip in enumerate(chips):
            copy(4 + j, (*chip, 1 - c), me).wait_recv()
        for cp in first + passed:
            cp.wait_send()
        mine.wait()

    space = pl.BlockSpec(memory_space=pltpu.VMEM) if in_vmem else ANY
    return pl.pallas_call(
        body, name=name, out_shape=jax.ShapeDtypeStruct((N_DEV, R, W), block.dtype),
        in_specs=[space], out_specs=space,
        scratch_shapes=[pltpu.SemaphoreType.DMA((7,)), pltpu.SemaphoreType.DMA((7,)), pltpu.SemaphoreType.DMA],
        compiler_params=pltpu.CompilerParams(vmem_limit_bytes=VMEM_LIMIT_V7X),
    )(block)


def _comm_call(body, arrays, out_shapes, n_sems, *, name):
    return pl.pallas_call(
        body, name=name, out_shape=out_shapes, in_specs=[ANY] * len(arrays), out_specs=[ANY] * len(out_shapes),
        scratch_shapes=[pltpu.SemaphoreType.DMA((n_sems,)), pltpu.SemaphoreType.DMA((n_sems,)),
                        pltpu.SemaphoreType.DMA((len(arrays),))],
    )(*arrays)


def _gather_weights(shards, *, name):
    n = len(shards)

    def body(*refs):
        xs, outs = refs[:n], refs[n:2 * n]
        send_sems, recv_sems, local_sems = refs[2 * n:]
        x, y, c = _mesh_pos()
        me, sibling = (x, y, c), (x, y, 1 - c)
        chips = [(1 - x, y), (x, 1 - y), (1 - x, 1 - y)]
        waits = []
        for i in range(n):
            nl = shards[i].shape[0]
            own = xs[i].at[pl.ds(0, nl), c]

            def slot(px, py, pc, i=i, nl=nl):
                return outs[i].at[pl.ds(0, nl), 2 * px + py, pc]

            def copy(k, blk, to, src=None, i=i, slot=slot):
                return pltpu.make_async_remote_copy(
                    src_ref=slot(*blk) if src is None else src, dst_ref=slot(*blk),
                    send_sem=send_sems.at[7 * i + k], recv_sem=recv_sems.at[7 * i + k], device_id=to,
                    device_id_type=MESH_ID)

            mine = pltpu.make_async_copy(own, slot(*me), local_sems.at[i])
            mine.start()
            first = [copy(0, me, sibling, src=own)]
            first += [copy(1 + j, me, (*chip, c), src=own) for j, chip in enumerate(chips)]
            for cp in first:
                cp.start()
            waits.append((copy, mine, first))
        for copy, mine, first in waits:
            passed = [copy(4 + j, (*chip, c), sibling) for j, chip in enumerate(chips)]
            for j, chip in enumerate(chips):
                copy(1 + j, (*chip, c), me).wait_recv()
                passed[j].start()
            copy(0, sibling, me).wait_recv()
            for j, chip in enumerate(chips):
                copy(4 + j, (*chip, 1 - c), me).wait_recv()
            for cp in first + passed:
                cp.wait_send()
            mine.wait()

    out_shapes = [jax.ShapeDtypeStruct((s.shape[0], N_CHIP) + s.shape[1:], s.dtype) for s in shards]
    return _comm_call(body, shards, out_shapes, 7 * n, name=name)


def _place_own(shard, chip_idx, c_idx, *, name):
    n, _, rows, cols = shard.shape
    tr = _row_tile(rows, cols)

    def body(k_ref, c_ref, x_ref, o_ref):
        o_ref[...] = x_ref[...]

    grid_spec = pltpu.PrefetchScalarGridSpec(
        num_scalar_prefetch=2, grid=(n, rows // tr),
        in_specs=[pl.BlockSpec((None, None, tr, cols), lambda l, i, k_ref, c_ref: (l, c_ref[0], i, 0))],
        out_specs=pl.BlockSpec((None, None, None, tr, cols), lambda l, i, k_ref, c_ref: (l, k_ref[0], c_ref[0], i, 0)))
    return pl.pallas_call(
        body, name=name, grid_spec=grid_spec,
        out_shape=jax.ShapeDtypeStruct((n, N_CHIP, 2, rows, cols), shard.dtype),
        compiler_params=_cparams(("parallel", "parallel")),
    )(chip_idx, c_idx, shard)


def _gather_copies(x_refs, land_refs, send_sems, recv_sems):
    x, y, c = _mesh_pos()
    k_me = 2 * x + y
    targets = [(x, y, 1 - c), (1 - x, y, c), (x, 1 - y, c), (1 - x, 1 - y, c)]
    copies = []
    for i, (x_ref, land_ref) in enumerate(zip(x_refs, land_refs)):
        nl = x_ref.shape[0]
        for j, to in enumerate(targets):
            copies.append(pltpu.make_async_remote_copy(
                src_ref=x_ref.at[pl.ds(0, nl), c], dst_ref=land_ref.at[pl.ds(0, nl), k_me, c],
                send_sem=send_sems.at[4 * i + j], recv_sem=recv_sems.at[4 * i + j], device_id=to,
                device_id_type=MESH_ID))
    return copies


def _split_start(copies_fn, srcs, lands, after, *, name, sems_per_array):
    n = len(srcs)

    def body(*refs):
        send_sems, recv_sems = refs[2 * n + 1], refs[2 * n + 2]
        for cp in copies_fn(refs[:n], refs[n:2 * n], send_sems, recv_sems):
            cp.start()
        refs[-1][...] = jnp.zeros_like(refs[-1])

    operands = [pltpu.with_memory_space_constraint(a, pltpu.HBM) for a in list(srcs) + list(lands)]
    n_sems = sems_per_array * n
    out_shape = ([pltpu.SemaphoreType.DMA((n_sems,)), pltpu.SemaphoreType.DMA((n_sems,))]
                 + [pltpu.HBM(a.shape, a.dtype) for a in operands] + [jax.ShapeDtypeStruct((8, LANES), F32)])
    res = pl.pallas_call(
        body, name=name, out_shape=out_shape, in_specs=[HBM] * (2 * n) + [ANY],
        out_specs=[SEM, SEM] + [HBM] * (2 * n) + [pl.BlockSpec(memory_space=pltpu.VMEM)],
        input_output_aliases={i: 2 + i for i in range(2 * n)},
        compiler_params=pltpu.CompilerParams(has_side_effects=DATAFLOW),
    )(*operands, after)
    return res[0], res[1], list(res[2:2 + n]), list(res[2 + n:2 + 2 * n]), res[-1]


def _split_wait(copies_fn, send_sems, recv_sems, srcs, lands, after, *, name):
    n = len(srcs)

    def body(*refs):
        for cp in copies_fn(refs[:n], refs[n:2 * n], refs[2 * n], refs[2 * n + 1]):
            cp.wait_send()
            cp.wait_recv()

    res = pl.pallas_call(
        body, name=name, out_shape=[pltpu.HBM(a.shape, a.dtype) for a in list(srcs) + list(lands)],
        in_specs=[HBM] * (2 * n) + [SEM, SEM, ANY], out_specs=[HBM] * (2 * n),
        input_output_aliases={i: i for i in range(2 * n)},
        compiler_params=pltpu.CompilerParams(has_side_effects=DATAFLOW),
    )(*srcs, *lands, send_sems, recv_sems, after)
    return list(res[:n]), list(res[n:])


def _gather_forward(lands, *, name):
    n = len(lands)

    def body(*refs):
        xs = refs[:n]
        send_sems, recv_sems, _ = refs[2 * n:]
        x, y, c = _mesh_pos()
        chips = [(1 - x, y), (x, 1 - y), (1 - x, 1 - y)]
        copies = []
        for i in range(n):
            nl = lands[i].shape[0]
            for j, (cx, cy) in enumerate(chips):
                here = xs[i].at[pl.ds(0, nl), 2 * cx + cy, c]
                cp = pltpu.make_async_remote_copy(
                    src_ref=here, dst_ref=here, send_sem=send_sems.at[3 * i + j], recv_sem=recv_sems.at[3 * i + j],
                    device_id=(x, y, 1 - c), device_id_type=MESH_ID)
                cp.start()
                copies.append(cp)
        for cp in copies:
            cp.wait()

    return pl.pallas_call(
        body, name=name, out_shape=[jax.ShapeDtypeStruct(a.shape, a.dtype) for a in lands],
        in_specs=[ANY] * n, out_specs=[ANY] * n, input_output_aliases={i: i for i in range(n)},
        scratch_shapes=[pltpu.SemaphoreType.DMA((3 * n,)), pltpu.SemaphoreType.DMA((3 * n,)),
                        pltpu.SemaphoreType.DMA((1,))],
    )(*lands)


def _pair_exchange(gs, *, name):
    n = len(gs)

    def body(*refs):
        xs, outs = refs[:n], refs[n:2 * n]
        send_sems, recv_sems, _ = refs[2 * n:]
        x, y, c = _mesh_pos()
        copies = []
        for i in range(n):
            nl, ns = gs[i].shape[:2]
            cp = pltpu.make_async_remote_copy(
                src_ref=xs[i].at[pl.ds(0, nl), pl.ds(0, ns), 1 - c], dst_ref=outs[i], send_sem=send_sems.at[i],
                recv_sem=recv_sems.at[i], device_id=(x, y, 1 - c), device_id_type=MESH_ID)
            cp.start()
            copies.append(cp)
        for cp in copies:
            cp.wait()

    out_shapes = [jax.ShapeDtypeStruct(g.shape[:2] + g.shape[3:], g.dtype) for g in gs]
    return _comm_call(body, gs, out_shapes, n, name=name)


def _chip_copies(p_refs, land_refs, send_sems, recv_sems):
    x, y, c = _mesh_pos()
    k_me = 2 * x + y
    chips = [(1 - x, y), (x, 1 - y), (1 - x, 1 - y)]
    copies = []
    for i, (p_ref, land_ref) in enumerate(zip(p_refs, land_refs)):
        nl = p_ref.shape[0]
        for j, (cx, cy) in enumerate(chips):
            copies.append(pltpu.make_async_remote_copy(
                src_ref=p_ref.at[pl.ds(0, nl), 2 * cx + cy], dst_ref=land_ref.at[k_me],
                send_sem=send_sems.at[3 * i + j], recv_sem=recv_sems.at[3 * i + j],
                device_id=(cx, cy, c), device_id_type=MESH_ID))
    return copies


def _chip_landing(ps):
    return [lax.empty((p.shape[1], p.shape[0]) + p.shape[2:], p.dtype) for p in ps]


def _chip_exchange(ps, *, name):
    n = len(ps)

    def body(*refs):
        send_sems, recv_sems, _ = refs[2 * n:]
        copies = _chip_copies(refs[:n], refs[n:2 * n], send_sems, recv_sems)
        for cp in copies:
            cp.start()
        for cp in copies:
            cp.wait()

    out_shapes = [jax.ShapeDtypeStruct((p.shape[1], p.shape[0]) + p.shape[2:], p.dtype) for p in ps]
    return _comm_call(body, ps, out_shapes, 3 * n, name=name)


def _pair_swap(ss, *, name):
    n = len(ss)

    def body(*refs):
        xs, outs = refs[:n], refs[n:2 * n]
        send_sems, recv_sems, _ = refs[2 * n:]
        x, y, c = _mesh_pos()
        copies = []
        for i in range(n):
            cp = pltpu.make_async_remote_copy(src_ref=xs[i], dst_ref=outs[i], send_sem=send_sems.at[i],
                                              recv_sem=recv_sems.at[i], device_id=(x, y, 1 - c),
                                              device_id_type=MESH_ID)
            cp.start()
            copies.append(cp)
        for cp in copies:
            cp.wait()

    out_shapes = [jax.ShapeDtypeStruct(s.shape, s.dtype) for s in ss]
    return _comm_call(body, ss, out_shapes, n, name=name)


def _row_tile(rows, cols):
    tr = rows
    while tr * cols > 256 * 1024 and tr % 16 == 0:
        tr //= 2
    return tr


def _pair_add(g, recv, c_idx, *, name):
    n, ns, _, rows, W = g.shape
    tr = _row_tile(rows, W)

    def body(c_ref, g_ref, r_ref, o_ref):
        o_ref[...] = (g_ref[...] + r_ref[...]).astype(BF16)

    piece = pl.BlockSpec((None, tr, W), lambda p, i, c_ref: (p, i, 0))
    grid_spec = pltpu.PrefetchScalarGridSpec(
        num_scalar_prefetch=1, grid=(n * ns, rows // tr),
        in_specs=[pl.BlockSpec((None, None, tr, W), lambda p, i, c_ref: (p, c_ref[0], i, 0)), piece],
        out_specs=piece)
    out = pl.pallas_call(
        body, name=name, grid_spec=grid_spec, out_shape=jax.ShapeDtypeStruct((n * ns, rows, W), BF16),
        compiler_params=_cparams(("parallel", "parallel")),
    )(c_idx, g.reshape(n * ns, 2, rows, W), recv.reshape(n * ns, rows, W))
    return out.reshape(n, ns, rows, W)


def _sum_pieces(land, own, chip_idx, *, name):
    n, nl, A, W = land.shape
    tr = _row_tile(A, W)

    def body(k_ref, l_ref, o_ref, out_ref):
        acc = jnp.zeros(out_ref.shape, F32)
        for k in range(n):
            acc = acc + jnp.where(k == k_ref[0], o_ref[...], l_ref[k]).astype(F32)
        out_ref[...] = acc

    grid_spec = pltpu.PrefetchScalarGridSpec(
        num_scalar_prefetch=1, grid=(nl, A // tr),
        in_specs=[pl.BlockSpec((n, None, tr, W), lambda l, i, k_ref: (0, l, i, 0)),
                  pl.BlockSpec((None, None, tr, W), lambda l, i, k_ref: (l, k_ref[0], i, 0))],
        out_specs=pl.BlockSpec((None, tr, W), lambda l, i, k_ref: (l, i, 0)))
    return pl.pallas_call(
        body, name=name, grid_spec=grid_spec, out_shape=jax.ShapeDtypeStruct((nl, A, W), F32),
        compiler_params=_cparams(("parallel", "parallel")),
    )(chip_idx, land, own)


SMALL = ("norm_mix_g", "norm_mlp_g", "final_norm_g", "fox_b_f", "mla_q_norm_g", "mla_kv_norm_g")
WEIGHT_ORDER = ("ada_w", "ada_b", "norm_mix_g", "norm_mlp_g", "fox_w_in", "fox_b_f", "fox_w_out", "mla_w_dq",
                "mla_q_norm_g", "mla_w_uq", "mla_w_dkv", "mla_kv_norm_g", "mla_w_ukv", "mla_w_out", "mlp_w1",
                "mlp_w2", "final_norm_g")


def _small_rows(vals, D):
    rows = [vals["norm_mix_g"], vals["norm_mlp_g"], vals["final_norm_g"][None, :]]
    for n in ("fox_b_f", "mla_q_norm_g", "mla_kv_norm_g"):
        flat = vals[n].reshape(-1)
        assert flat.shape[0] <= D
        rows.append(jnp.pad(flat, (0, D - flat.shape[0]))[None, :])
    return jnp.concatenate(rows, axis=0)


def _small_unrows(rows, shapes):
    L = shapes["norm_mix_g"][0]
    out = {"norm_mix_g": rows[0:L], "norm_mlp_g": rows[L:2 * L], "final_norm_g": rows[2 * L]}
    for k, n in enumerate(("fox_b_f", "mla_q_norm_g", "mla_kv_norm_g")):
        size = int(np.prod(shapes[n]))
        out[n] = rows[2 * L + 1 + k, :size].reshape(shapes[n])
    return out


def kernel(x, c, positions, ada_w, ada_b, norm_mix_g, norm_mlp_g, fox_w_in, fox_b_f, fox_w_out, mla_w_dq, mla_q_norm_g, mla_w_uq, mla_w_dkv, mla_kv_norm_g, mla_w_ukv, mla_w_out, mlp_w1, mlp_w2, final_norm_g, loss_target, m_ada_w, m_ada_b, m_norm_mix_g, m_norm_mlp_g, m_fox_w_in, m_fox_b_f, m_fox_w_out, m_mla_w_dq, m_mla_q_norm_g, m_mla_w_uq, m_mla_w_dkv, m_mla_kv_norm_g, m_mla_w_ukv, m_mla_w_out, m_mlp_w1, m_mlp_w2, m_final_norm_g, v_ada_w, v_ada_b, v_norm_mix_g, v_norm_mlp_g, v_fox_w_in, v_fox_b_f, v_fox_w_out, v_mla_w_dq, v_mla_q_norm_g, v_mla_w_uq, v_mla_w_dkv, v_mla_kv_norm_g, v_mla_w_ukv, v_mla_w_out, v_mlp_w1, v_mlp_w2, v_final_norm_g):
    args = dict(locals())
    wts = {n: args[n] for n in WEIGHT_ORDER}
    mom = {n: args["m_" + n] for n in WEIGHT_ORDER}
    var = {n: args["v_" + n] for n in WEIGHT_ORDER}
    Bl, S, D = x.shape
    T = Bl * S
    L = ada_w.shape[0]
    C = ada_w.shape[2]
    mx, my, mc = _mesh_pos()
    chip = 2 * mx + my
    dev = 4 * mx + 2 * my + mc
    c_idx = jnp.reshape(mc, (1,)).astype(jnp.int32)
    chip_idx = jnp.reshape(chip, (1,)).astype(jnp.int32)
    small = {n: wts[n] for n in SMALL}
    L2, q_cols = mla_q_norm_g.shape
    n_fox_heads = fox_b_f.shape[1]

    shards = _shard_layouts(wts)
    groups = _comm_groups(L, L2)
    slots = _layer_slots(groups)

    def row_halves(a):
        return a.reshape(a.shape[:-2] + (2, a.shape[-2] // 2, a.shape[-1]))

    def whole_rows(a):
        return a.reshape(a.shape[:2] + (a.shape[2] * a.shape[3], a.shape[4]))

    part = {g: [row_halves(shards[n][s:s + cnt]) for n, s, cnt in entries] for g, entries in groups.items()}
    first = _gather_weights(part["first"], name="gather_first")
    own_placed = [_place_own(a, chip_idx, c_idx, name=f"gather_place_{n}")
                  for a, (n, _, _) in zip(part["rest"], groups["rest"])]
    rest_sems = _split_start(_gather_copies, part["rest"], own_placed, first[0], name="gather_rest_start",
                             sems_per_array=4)

    def layer_weights(w, group, arrays):
        for (n, s, cnt), a in zip(groups[group], arrays):
            for key, view in _weight_views(n, whole_rows(a), D, n_fox_heads).items():
                for l in range(cnt):
                    w[key][s + l] = (view, l)

    w = {key: [None] * L2 for key in ("fox_qkv", "fox_f", "fox_out", "mla_down", "mla_uq", "mla_ukv", "mla_out")}
    w.update({key: [None] * L for key in ("mlp_w1", "mlp_w2")})
    layer_weights(w, "first", first)

    def after_layer0(x_now, w):
        _, landed = _split_wait(_gather_copies, *rest_sems[:4], x_now, name="gather_rest_wait")
        layer_weights(w, "rest", _gather_forward(landed, name="gather_rest_forward"))
        return w

    c_pad = jnp.concatenate([c, jnp.pad(mla_q_norm_g, ((0, 8 - Bl - L2), (0, D - q_cols)))], axis=0)
    c8 = _all_gather8(c_pad, name="gather_c", in_vmem=True)
    c_all = c8[:, :Bl].reshape(N_DEV * Bl, D)
    qg4 = c8.reshape(N_CHIP, 2, 8, D)[:, 0, Bl:Bl + L2, :q_cols]
    small["mla_q_norm_g"] = jnp.transpose(qg4, (1, 0, 2)).reshape(L2, N_CHIP * q_cols)
    ada_b_cols = lax.dynamic_slice_in_dim(ada_b, chip * C, C, axis=1)[:, None, :]
    mod_cols = _ada_fwd(c_all, ada_w, ada_b_cols)
    mod8 = _all_gather8(mod_cols.reshape(L * N_DEV * Bl, C), name="gather_mod", in_vmem=True)
    mod4 = mod8.reshape(N_CHIP, 2, L, N_DEV * Bl, C)[:, 0]
    mod_me = lax.dynamic_slice_in_dim(mod4, dev * Bl, Bl, axis=2)
    mod = jnp.transpose(mod_me, (1, 2, 0, 3)).reshape(L, Bl, 6, D)
    mod = jnp.transpose(mod, (0, 2, 1, 3))[:, :, :, None, :]

    w.update(_small_layouts(small))
    mod = mod + rest_sems[4][0, 0]
    pending = {}

    def grad_pieces(group, g_now):
        out = []
        for n, s, cnt in groups[group]:
            qkv_f = [(g_now["fox_qkv"][j], g_now["fox_f"][j]) for j in range(s, s + cnt)] if n == "fox_in" else None
            stacked_g = None if n == "fox_in" else g_now[n][group]
            out.append(row_halves(_grad_pieces(n, stacked_g, qkv_f, n_fox_heads, N_CHIP)))
        return out

    def pair_sums(group, g_now):
        big = grad_pieces(group, g_now)
        sibling = _pair_exchange(big, name=f"grad_pair_exchange_{group}")
        return [_pair_add(a, r, c_idx, name=f"grad_pair_add_{group}_{n}")
                for (n, _, _), a, r in zip(groups[group], big, sibling)]

    def before_layer0(g_now):
        ps = pair_sums("rest", g_now)
        pending["rest"] = _split_start(_chip_copies, ps, _chip_landing(ps), chip_idx, name="grad_exchange_rest_start",
                                       sems_per_array=3)
        return pending["rest"][4]

    half = ROPE_DIM // 2
    inv_freq = ROPE_THETA ** (-jnp.arange(0, ROPE_DIM, 2, dtype=F32) / ROPE_DIM)
    lane = np.arange(LANES)
    inv_freq_row = jnp.tile(inv_freq, LANES // half)[None, :]
    sign_row = jnp.asarray(np.where(lane < 2 * ROPE_DIM, np.where(lane % ROPE_DIM < half, -1.0, 1.0), 0.0), F32)[None, :]
    pos_f = positions.astype(F32).reshape(T, 1)
    loss_row, grad_x, dmod, g = _local_step(x.reshape(T, D), loss_target.reshape(T, D), pos_f, inv_freq_row, sign_row,
                                            mod, w, slots, S=S, after_layer0=after_layer0, before_layer0=before_layer0)
    g_small = _small_grads(g, n_fox_heads)
    ps_first = pair_sums("first", g)
    pending["first"] = _split_start(_chip_copies, ps_first, _chip_landing(ps_first), chip_idx,
                                    name="grad_exchange_first_start", sems_per_array=3)

    Rs = -(-(2 * L + 5) // 8) * 8
    srows = jnp.concatenate([_small_rows(g_small, D), jnp.pad(loss_row, ((0, 0), (0, D - LANES)))], axis=0)
    srows = jnp.pad(srows, ((0, Rs - srows.shape[0]), (0, 0)))
    drows = jnp.transpose(dmod[:, :, :, 0, :], (2, 0, 1, 3)).reshape(Bl * L * 6, D)
    both8 = _all_gather8(jnp.concatenate([drows, srows], axis=0), name="gather_small", in_vmem=True)
    dm8 = both8[:, :Bl * L * 6].reshape(N_DEV, Bl, L * 6, D)
    sm8 = both8[:, Bl * L * 6:]
    adb_rows, small_sum = _sum_gathered(dm8, sm8)
    grad_ada_b = adb_rows.reshape(L, 6 * D)
    loss = small_sum[2 * L + 4, 0]
    small_shapes = {n: (wts[n].shape if n != "mla_q_norm_g" else (wts[n].shape[0], N_CHIP * q_cols)) for n in SMALL}
    gs = _small_unrows(small_sum, small_shapes)
    gs["mla_q_norm_g"] = lax.dynamic_slice_in_dim(gs["mla_q_norm_g"], chip * q_cols, q_cols, axis=1)

    dmod16 = jnp.transpose(dm8.reshape(N_DEV, Bl, L, 6 * D), (2, 0, 1, 3)).reshape(L, N_DEV * Bl, 6 * D)
    dmod_cols = lax.dynamic_slice_in_dim(dmod16, chip * C, C, axis=2)
    grad_ada_w = _ada_bwd(c_all, dmod_cols)

    grads = dict(gs)
    grads["ada_w"] = grad_ada_w
    grads["ada_b"] = grad_ada_b
    delta, new_m, new_v = {}, {}, {}
    for n in ("ada_w", "ada_b"):
        delta[n], new_m[n], new_v[n] = _adamw(wts[n], grads[n], mom[n], var[n], name=f"adamw_{n}")
    shard_small_shapes = {n: wts[n].shape for n in SMALL}
    packs = [jnp.pad(_small_rows({n: src[n] for n in SMALL}, D), ((0, Rs - 2 * L - 4), (0, 0)))
             for src in (wts, grads, mom, var)]
    for dst, rows in zip((delta, new_m, new_v), _adamw(*packs, name="adamw_small")):
        dst.update(_small_unrows(rows, shard_small_shapes))

    halves = {}
    for group, after in (("rest", grad_x), ("first", delta["ada_w"])):
        send_sems, recv_sems, ps, lands, _ = pending[group]
        ps, lands = _split_wait(_chip_copies, send_sems, recv_sems, ps, lands, after, name=f"grad_exchange_{group}_wait")
        sums = [_sum_pieces(ld, p, chip_idx, name=f"grad_sum_{group}_{n}")
                for (n, _, _), ld, p in zip(groups[group], lands, ps)]
        swapped = _pair_swap(sums, name=f"grad_pair_swap_{group}")
        for (n, _, _), a, b in zip(groups[group], sums, swapped):
            halves[(n, group)] = (a, b)

    def all_layers(n, which):
        return jnp.concatenate([halves[(n, grp)][which] for grp in ("first", "rest") if (n, grp) in halves], axis=0)

    own = {n: all_layers(n, 0) for n in GATHERED}
    peer = {n: all_layers(n, 1) for n in GATHERED}
    for nat, n in (("fox_w_in", "fox_in"), ("fox_w_out", "fox_out"), ("mla_w_out", "mla_out"), ("mlp_w1", "mlp_w1"),
                   ("mlp_w2", "mlp_w2")):
        cols = wts[nat].shape[-1]
        res = _adamw_halves(_pad_lanes(wts[nat]), own[n], peer[n], _pad_lanes(mom[nat]), _pad_lanes(var[nat]), c_idx,
                            name=f"adamw_{nat}")
        grads[nat], delta[nat], new_m[nat], new_v[nat] = (a[..., :cols] for a in res)
    joined = {n: jnp.concatenate([jnp.where(mc == 0, own[n], peer[n]), jnp.where(mc == 0, peer[n], own[n])], axis=1)
              for n in ("mla_down", "mla_uq", "mla_ukv")}
    rq = mla_w_dq.shape[-1]
    grads["mla_w_dq"] = joined["mla_down"][:, :, :rq]
    grads["mla_w_dkv"] = joined["mla_down"][:, :, rq:rq + KV_RANK + ROPE_DIM]
    grads["mla_w_uq"] = jax.vmap(_uq_from_pairs)(joined["mla_uq"])
    grads["mla_w_ukv"] = jax.vmap(_ukv_from_pairs)(joined["mla_ukv"])
    for n in ("mla_w_dq", "mla_w_dkv", "mla_w_uq", "mla_w_ukv"):
        delta[n], new_m[n], new_v[n] = _adamw(wts[n], grads[n], mom[n], var[n], name=f"adamw_{n}")

    return (loss, grad_x.reshape(Bl, S, D), *[grads[n] for n in WEIGHT_ORDER], *[delta[n] for n in WEIGHT_ORDER],
            *[new_m[n] for n in WEIGHT_ORDER], *[new_v[n] for n in WEIGHT_ORDER])
```

```python
import functools

import numpy as np
import jax
import jax.numpy as jnp
from jax import lax
from jax.experimental import pallas as pl
from jax.experimental.pallas import tpu as pltpu

F32 = jnp.float32
BF16 = jnp.bfloat16
MESH_ID = pl.DeviceIdType.MESH

NORM_EPS = 1e-6
ROPE_THETA = 10000.0
HEAD_DIM = 64
ROPE_DIM = 32
KV_RANK = 128
FOX_EXTRA = 6
PAIR_Q = 256
PAIR_KV = 384
LANES = 128
ADAM_LR = 0.001
ADAM_B1 = 0.9
ADAM_B2 = 0.999
ADAM_EPS = 1e-08
ADAM_WD = 0.01
ADAM_STEP = 10
VMEM_LIMIT_V7X = 48 * 1024 * 1024
MM_VMEM_BUDGET = 36 * 1024 * 1024
NEG_BIG = -1e30
ATTN_UNROLL = 4

BIG_WEIGHTS = (("fox_w_in", 2), ("fox_w_out", 1), ("mla_w_dq", 1), ("mla_w_uq", 2), ("mla_w_dkv", 1),
               ("mla_w_ukv", 2), ("mla_w_out", 1), ("mlp_w1", 2), ("mlp_w2", 1))


def _cparams(sem=None):
    return pltpu.CompilerParams(dimension_semantics=sem, vmem_limit_bytes=VMEM_LIMIT_V7X)


def _tile(n, want):
    if n <= want:
        return n
    for t in range(want - want % LANES, 0, -LANES):
        if n % t == 0:
            return t
    raise ValueError((n, want))


def _mm(a, b, mode, *, name, out_dtypes=(F32,), epilogue=None, extras=(), rowvecs=(), tables=(),
        seq=None, a_off=0, a_sz=None, b_layer=None, out_stack=None, out_split=0, tm=1024, tn=1024, tk=2048):
    if isinstance(b, (list, tuple)):
        b, b_layer = b[b_layer]
    b_rows, b_cols = b.shape[-2], b.shape[-1]
    n_split = b.shape[1] if b.ndim == 4 else 1
    assert mode in ("nn", "nt")
    if mode == "nn":
        M, K, N = a.shape[0], b_rows, b_cols * n_split
    else:
        M, K, N = a.shape[0], b_cols * n_split, b_rows
    assert a_sz is None or a_sz == K
    tm = _tile(seq if rowvecs else M, tm)
    n_piece = N // max(out_split, n_split if mode == "nn" else 1, 1)
    tn = _tile(n_piece, tn)
    tk = _tile(K // (n_split if mode == "nt" else 1), tk)
    ne, nr, nt_ = len(extras), len(rowvecs), len(tables)
    no = len(out_dtypes)

    def vmem_estimate():
        blocks = tm * tk * a.dtype.itemsize + tk * tn * b.dtype.itemsize
        blocks += tm * tn * (sum(e.dtype.itemsize for e in extras) + sum(jnp.dtype(d).itemsize for d in out_dtypes))
        return 2 * blocks + 2 * tm * tn * 4

    while vmem_estimate() > MM_VMEM_BUDGET and max(tm, tn) > 256:
        if tn >= tm:
            tn //= 2
        else:
            tm //= 2
    nk = K // tk

    assert a_off % tk == 0
    a_spec = pl.BlockSpec((tm, tk), lambda i, j, k: (i, k + a_off // tk))
    dims = (((1,), (0,)), ((), ())) if mode == "nn" else (((1,), (1,)), ((), ()))
    lead = () if b.ndim == 2 else (b_layer,)
    sq = (None,) * (b.ndim - 2)
    if mode == "nt":
        kb = b_cols // tk
        if b.ndim == 4:
            b_spec = pl.BlockSpec(sq + (tn, tk), lambda i, j, k: lead + (k // kb, j, k % kb))
        else:
            b_spec = pl.BlockSpec(sq + (tn, tk), lambda i, j, k: lead + (j, k))
    else:
        nb = b_cols // tn
        if b.ndim == 4:
            b_spec = pl.BlockSpec(sq + (tk, tn), lambda i, j, k: lead + (j // nb, k, j % nb))
        else:
            b_spec = pl.BlockSpec(sq + (tk, tn), lambda i, j, k: lead + (k, j))
    in_specs = [a_spec, b_spec]
    in_specs += [pl.BlockSpec((tm, tn), lambda i, j, k: (i, j)) for _ in extras]
    if rowvecs:
        assert seq % tm == 0
        per = seq // tm
        in_specs += [pl.BlockSpec((None, 1, tn), lambda i, j, k: (i // per, 0, j)) for _ in rowvecs]
    in_specs += [pl.BlockSpec((tm, LANES), lambda i, j, k: (i, 0)) for _ in tables]
    operands = [a, b, *extras, *rowvecs, *tables]
    aliases = {}
    if out_stack is None:
        out_specs = [pl.BlockSpec((tm, tn), lambda i, j, k: (i, j)) for _ in out_dtypes]
        out_shape = [jax.ShapeDtypeStruct((M, N), d) for d in out_dtypes]
    else:
        prev, layer, n_layers = out_stack
        assert no == 1
        if out_split:
            ob = n_piece // tn
            out_specs = [pl.BlockSpec((None, None, tm, tn), lambda i, j, k: (layer, j // ob, i, j % ob))]
            out_shape = [jax.ShapeDtypeStruct((n_layers, out_split, M, n_piece), out_dtypes[0])]
        else:
            out_specs = [pl.BlockSpec((None, tm, tn), lambda i, j, k: (layer, i, j))]
            out_shape = [jax.ShapeDtypeStruct((n_layers, M, N), out_dtypes[0])]
        if prev is not None:
            in_specs.append(pl.BlockSpec(memory_space=pl.ANY))
            aliases = {len(operands): 0}
            operands.append(prev)
    n_in = len(operands)

    def body(*refs):
        a_ref, b_ref = refs[0], refs[1]
        side = refs[2:2 + ne + nr + nt_]
        outs = refs[n_in:n_in + no]

        def finish(acc):
            res = (acc,) if epilogue is None else epilogue(acc, *[r[...] for r in side])
            for o_ref, r in zip(outs, res):
                o_ref[...] = r.astype(o_ref.dtype)

        part = lax.dot_general(a_ref[...].astype(BF16), b_ref[...].astype(BF16), dims,
                               preferred_element_type=F32)
        if nk == 1:
            finish(part)
        else:
            acc_ref = refs[-1]
            k = pl.program_id(2)

            @pl.when(k == 0)
            def _():
                acc_ref[...] = part

            @pl.when(k > 0)
            def _():
                acc_ref[...] += part

            @pl.when(k == nk - 1)
            def _():
                finish(acc_ref[...])

    res = pl.pallas_call(
        body, name=name, grid=(M // tm, N // tn, nk), in_specs=in_specs, out_specs=out_specs,
        out_shape=out_shape, scratch_shapes=[pltpu.VMEM((tm, tn), F32)] if nk > 1 else [],
        input_output_aliases=aliases,
        compiler_params=_cparams(("parallel", "parallel", "arbitrary")),
    )(*operands)
    return res[0] if no == 1 else tuple(res)


def _rope128(x, cos_t, sin_s):
    lane = lax.broadcasted_iota(jnp.int32, x.shape, 1)
    first = (lane % ROPE_DIM) < (ROPE_DIM // 2)
    swapped = jnp.where(first, pltpu.roll(x, LANES - ROPE_DIM // 2, 1), pltpu.roll(x, ROPE_DIM // 2, 1))
    return x * cos_t + swapped * sin_s


def _rope_pairs(acc, cos_t, sin_s, sign):
    parts = []
    for p in range(acc.shape[1] // PAIR_Q):
        parts.append(acc[:, p * PAIR_Q:p * PAIR_Q + LANES])
        parts.append(_rope128(acc[:, p * PAIR_Q + LANES:(p + 1) * PAIR_Q], cos_t, sign * sin_s))
    return jnp.concatenate(parts, axis=1)


def _rope_tables(pos_f, inv_freq_row, sign_row):
    T = pos_f.shape[0]
    tt = _tile(T, 512)

    def body(p_ref, f_ref, s_ref, cos_ref, sin_ref):
        ang = p_ref[...] * f_ref[...]
        cos_ref[...] = jnp.cos(ang)
        sin_ref[...] = jnp.sin(ang) * s_ref[...]

    return pl.pallas_call(
        body, name="rope_tables", grid=(T // tt,),
        in_specs=[pl.BlockSpec((tt, 1), lambda i: (i, 0)), pl.BlockSpec((1, LANES), lambda i: (0, 0)),
                  pl.BlockSpec((1, LANES), lambda i: (0, 0))],
        out_specs=[pl.BlockSpec((tt, LANES), lambda i: (i, 0))] * 2,
        out_shape=[jax.ShapeDtypeStruct((T, LANES), F32)] * 2,
        compiler_params=_cparams(("parallel",)),
    )(pos_f, inv_freq_row, sign_row)


def _unrope(dqx, cos_t, sin_s):
    T, W = dqx.shape
    tt = _tile(T, 512)

    def body(d_ref, c_ref, s_ref, o_ref):
        o_ref[...] = _rope_pairs(d_ref[...].astype(F32), c_ref[...], s_ref[...], -1.0).astype(BF16)

    return pl.pallas_call(
        body, name="mla_unrope", grid=(T // tt,),
        in_specs=[pl.BlockSpec((tt, W), lambda i: (i, 0)), pl.BlockSpec((tt, LANES), lambda i: (i, 0)),
                  pl.BlockSpec((tt, LANES), lambda i: (i, 0))],
        out_specs=pl.BlockSpec((tt, W), lambda i: (i, 0)),
        out_shape=jax.ShapeDtypeStruct((T, W), BF16),
        compiler_params=_cparams(("parallel",)),
    )(dqx, cos_t, sin_s)


def _row_specs(tt, D, per, n):
    return [pl.BlockSpec((None, 1, D), lambda i: (i // per, 0, 0)) for _ in range(n)]


def _norm_mod(x, gain, sc, sh, *, S, name):
    T, D = x.shape
    tt = _tile(S, 512)
    per = S // tt

    def body(x_ref, g_ref, sc_ref, sh_ref, h_ref):
        xv = x_ref[...]
        r = lax.rsqrt(jnp.mean(xv * xv, axis=-1, keepdims=True) + NORM_EPS)
        h_ref[...] = ((xv * r) * g_ref[...] * (1.0 + sc_ref[...]) + sh_ref[...]).astype(BF16)

    return pl.pallas_call(
        body, name=name, grid=(T // tt,),
        in_specs=[pl.BlockSpec((tt, D), lambda i: (i, 0)), pl.BlockSpec((1, D), lambda i: (0, 0))]
        + _row_specs(tt, D, per, 2),
        out_specs=pl.BlockSpec((tt, D), lambda i: (i, 0)),
        out_shape=jax.ShapeDtypeStruct((T, D), BF16),
        compiler_params=_cparams(("parallel",)),
    )(x, gain, sc, sh)


def _norm_mod_bwd(x, dh, dres, gain, sc, *, S, name):
    T, D = x.shape
    B = T // S
    tt = _tile(S, 512)
    per = S // tt

    def body(x_ref, dh_ref, dres_ref, g_ref, sc_ref, dx_ref, dsh_ref, dsc_ref, dg_ref):
        i = pl.program_id(0)
        xv = x_ref[...]
        dhv = dh_ref[...].astype(F32)
        r = lax.rsqrt(jnp.mean(xv * xv, axis=-1, keepdims=True) + NORM_EPS)
        n = xv * r
        g = g_ref[...]
        one_sc = 1.0 + sc_ref[...]
        dn = dhv * (g * one_sc)
        dx_ref[...] = dres_ref[...] + r * (dn - n * jnp.mean(dn * n, axis=-1, keepdims=True))
        dhn = dhv * n

        @pl.when(i % per == 0)
        def _():
            dsh_ref[...] = jnp.zeros_like(dsh_ref)
            dsc_ref[...] = jnp.zeros_like(dsc_ref)

        @pl.when(i == 0)
        def _():
            dg_ref[...] = jnp.zeros_like(dg_ref)

        dsh_ref[...] += jnp.sum(dhv, axis=0, keepdims=True)
        dsc_ref[...] += jnp.sum(dhn, axis=0, keepdims=True) * g
        dg_ref[...] += jnp.sum(dhn, axis=0, keepdims=True) * one_sc

    return pl.pallas_call(
        body, name=name, grid=(T // tt,),
        in_specs=[pl.BlockSpec((tt, D), lambda i: (i, 0))] * 3 + [pl.BlockSpec((1, D), lambda i: (0, 0))]
        + _row_specs(tt, D, per, 1),
        out_specs=[pl.BlockSpec((tt, D), lambda i: (i, 0))] + _row_specs(tt, D, per, 2)
        + [pl.BlockSpec((1, D), lambda i: (0, 0))],
        out_shape=[jax.ShapeDtypeStruct((T, D), F32), jax.ShapeDtypeStruct((B, 1, D), F32),
                   jax.ShapeDtypeStruct((B, 1, D), F32), jax.ShapeDtypeStruct((1, D), F32)],
        compiler_params=_cparams(("arbitrary",)),
    )(x, dh, dres, gain, sc)


def _gate_bwd(dx, y, g, *, S, name):
    T, D = dx.shape
    B = T // S
    tt = _tile(S, 512)
    per = S // tt

    def body(dx_ref, y_ref, g_ref, dy_ref, dg_ref):
        i = pl.program_id(0)
        dxv = dx_ref[...]
        dy_ref[...] = (dxv * g_ref[...]).astype(BF16)

        @pl.when(i % per == 0)
        def _():
            dg_ref[...] = jnp.zeros_like(dg_ref)

        dg_ref[...] += jnp.sum(dxv * y_ref[...], axis=0, keepdims=True)

    return pl.pallas_call(
        body, name=name, grid=(T // tt,),
        in_specs=[pl.BlockSpec((tt, D), lambda i: (i, 0))] * 2 + _row_specs(tt, D, per, 1),
        out_specs=[pl.BlockSpec((tt, D), lambda i: (i, 0))] + _row_specs(tt, D, per, 1),
        out_shape=[jax.ShapeDtypeStruct((T, D), BF16), jax.ShapeDtypeStruct((B, 1, D), F32)],
        compiler_params=_cparams(("arbitrary",)),
    )(dx, y, g)


def _final_loss(x, target, gain):
    T, D = x.shape
    tt = _tile(T, 512)

    def body(x_ref, t_ref, g_ref, dx_ref, dg_ref, loss_ref):
        i = pl.program_id(0)
        xv = x_ref[...]
        r = lax.rsqrt(jnp.mean(xv * xv, axis=-1, keepdims=True) + NORM_EPS)
        n = xv * r
        g = g_ref[...]
        err = n * g - t_ref[...]
        dy = err * (1.0 / D)
        dn = dy * g
        dx_ref[...] = r * (dn - n * jnp.mean(dn * n, axis=-1, keepdims=True))

        @pl.when(i == 0)
        def _():
            dg_ref[...] = jnp.zeros_like(dg_ref)
            loss_ref[...] = jnp.zeros_like(loss_ref)

        dg_ref[...] += jnp.sum(dy * n, axis=0, keepdims=True)
        loss_ref[...] += jnp.sum(jnp.sum(err * err, axis=-1, keepdims=True), axis=0, keepdims=True) * (0.5 / D)

    return pl.pallas_call(
        body, name="final_loss", grid=(T // tt,),
        in_specs=[pl.BlockSpec((tt, D), lambda i: (i, 0))] * 2 + [pl.BlockSpec((1, D), lambda i: (0, 0))],
        out_specs=[pl.BlockSpec((tt, D), lambda i: (i, 0)), pl.BlockSpec((1, D), lambda i: (0, 0)),
                   pl.BlockSpec((1, LANES), lambda i: (0, 0))],
        out_shape=[jax.ShapeDtypeStruct((T, D), F32), jax.ShapeDtypeStruct((1, D), F32),
                   jax.ShapeDtypeStruct((1, LANES), F32)],
        compiler_params=_cparams(("arbitrary",)),
    )(x, target, gain)


def _head_masks(ew):
    lane = lax.broadcasted_iota(jnp.int32, (1, PAIR_Q), 1)
    m0 = (lane < HEAD_DIM) | ((lane >= LANES) & (lane < LANES + ew))
    m1 = ((lane >= HEAD_DIM) & (lane < LANES)) | ((lane >= LANES + ew) & (lane < LANES + 2 * ew))
    return m0, m1


def _dot_nt(a, b):
    return lax.dot_general(a, b, (((1,), (1,)), ((), ())), preferred_element_type=F32)


def _dot_tn(a, b):
    return lax.dot_general(a, b, (((0,), (0,)), ((), ())), preferred_element_type=F32)


def _lane_halves(x, op):
    acc = x[:, 0:LANES]
    for g in range(1, x.shape[1] // LANES):
        acc = op(acc, x[:, g * LANES:(g + 1) * LANES])
    return acc


def _head_rows(cols_lane_replicated):
    t = cols_lane_replicated.T
    sub = lax.broadcasted_iota(jnp.int32, (8, t.shape[1]), 0)
    return jnp.where(sub == 1, t[HEAD_DIM:HEAD_DIM + 8], t[0:8])


def _attn_trip(n_blocks):
    return ATTN_UNROLL if n_blocks % ATTN_UNROLL == 0 else 2


def _attn_fwd(qx, kvx, *, S, scale, ew, name):
    T = qx.shape[0]
    P = qx.shape[1] // PAIR_Q
    B = T // S
    tq = _tile(S, 256)
    nq = S // tq
    big = _attn_trip(nq)

    def body(q_ref, kv_ref, o_ref, lse_ref, m_sc, l_sc, acc_sc):
        qi = pl.program_id(2)
        q = q_ref[...]
        masks = _head_masks(ew)
        qh = [jnp.where(m, q, jnp.zeros_like(q)) for m in masks]

        def logits(h, k, kj):
            s = _dot_nt(qh[h], k)
            if scale != 1.0:
                s = s * scale
            row = lax.broadcasted_iota(jnp.int32, s.shape, 0)
            col = lax.broadcasted_iota(jnp.int32, s.shape, 1)
            return jnp.where(col - row <= (qi - kj) * tq, s, NEG_BIG)

        def sweep(step):
            def loop_body(t, carry):
                for u in range(big):
                    step(t * big + u)
                return carry

            if big == 2:
                lax.fori_loop(0, (qi + 2) // 2, loop_body, 0)
            else:
                trips = (qi + 2) // big
                lax.fori_loop(0, trips, loop_body, 0)

                @pl.when(qi % big <= 1)
                def _():
                    for u in range(2):
                        step(trips * big + u)

        def max_step(kj):
            k = kv_ref[pl.ds(pl.multiple_of(kj * tq, tq), tq), 0:PAIR_Q]
            for h in range(2):
                m_sc[h] = jnp.maximum(m_sc[h], _lane_halves(logits(h, k, kj), jnp.maximum))

        def sum_step(kj):
            rows = pl.ds(pl.multiple_of(kj * tq, tq), tq)
            k = kv_ref[rows, 0:PAIR_Q]
            v = kv_ref[rows, PAIR_Q:PAIR_KV]
            for h in range(2):
                s = logits(h, k, kj)
                m = m_sc[h]
                p = jnp.concatenate([jnp.exp(s[:, g * LANES:(g + 1) * LANES] - m) for g in range(tq // LANES)], axis=1)
                l_sc[h] += _lane_halves(p, jnp.add)
                acc_sc[h] += jnp.dot(p.astype(BF16), v, preferred_element_type=F32)

        m_sc[...] = jnp.full(m_sc.shape, NEG_BIG, F32)
        sweep(max_step)
        for h in range(2):
            m_sc[h] = jnp.broadcast_to(jnp.max(m_sc[h], axis=1, keepdims=True), (tq, LANES))
        l_sc[...] = jnp.zeros_like(l_sc)
        acc_sc[...] = jnp.zeros_like(acc_sc)
        sweep(sum_step)
        lane = lax.broadcasted_iota(jnp.int32, (tq, LANES), 1)
        lo = lane < HEAD_DIM
        l = [jnp.sum(l_sc[h], axis=1, keepdims=True) for h in range(2)]
        o_ref[...] = jnp.where(lo, acc_sc[0] / l[0], acc_sc[1] / l[1]).astype(BF16)
        lse_ref[...] = _head_rows(jnp.where(lo, m_sc[0] + jnp.log(l[0]), m_sc[1] + jnp.log(l[1])))

    return pl.pallas_call(
        body, name=name, grid=(B, P, nq),
        in_specs=[pl.BlockSpec((tq, PAIR_Q), lambda b, p, i: (b * nq + i, p)),
                  pl.BlockSpec((S, PAIR_KV), lambda b, p, i: (b, p))],
        out_specs=[pl.BlockSpec((tq, LANES), lambda b, p, i: (b * nq + i, p)),
                   pl.BlockSpec((None, None, 8, tq), lambda b, p, i: (b * nq + i, p, 0, 0))],
        out_shape=[jax.ShapeDtypeStruct((T, P * LANES), BF16), jax.ShapeDtypeStruct((T // tq, P, 8, tq), F32)],
        scratch_shapes=[pltpu.VMEM((2, tq, LANES), F32)] * 3,
        compiler_params=_cparams(("parallel", "parallel", "arbitrary")),
    )(qx, kvx)


def _attn_bwd(qx, kvx, o, lse, do, *, S, scale, ew, name, bias_grad=False):
    T = qx.shape[0]
    P = qx.shape[1] // PAIR_Q
    B = T // S
    tq = _tile(S, 256)
    nq = S // tq
    big = _attn_trip(nq)

    def body(q_ref, kv_ref, o_ref, lse_ref, do_ref, dq_ref, dkv_ref, *rest):
        kj = pl.program_id(2)
        if bias_grad:
            csum_ref, rsum_ref, dq_sc, delta_sc, dk_sc, dv_sc, cs_sc = rest
            cs_sc[...] = jnp.zeros_like(cs_sc)

            @pl.when(kj == 0)
            def _():
                rsum_ref[...] = jnp.zeros_like(rsum_ref)
        else:
            dq_sc, delta_sc, dk_sc, dv_sc = rest
        masks = _head_masks(ew)
        lane = lax.broadcasted_iota(jnp.int32, (tq, LANES), 1)
        lo = lane < HEAD_DIM
        vmask = [lo, jnp.logical_not(lo)]

        @pl.when(kj == 0)
        def _():
            dq_sc[...] = jnp.zeros_like(dq_sc)
            for c in range(nq):
                rows = pl.ds(c * tq, tq)
                x = do_ref[rows, :].astype(F32) * o_ref[rows, :].astype(F32)
                r0 = jnp.sum(jnp.where(lo, x, 0.0), axis=1, keepdims=True)
                r1 = jnp.sum(jnp.where(lo, 0.0, x), axis=1, keepdims=True)
                delta_sc[c] = _head_rows(jnp.where(lo, r0, r1))

        k = kv_ref[:, 0:PAIR_Q]
        v = kv_ref[:, PAIR_Q:PAIR_KV]
        kh = [jnp.where(m, k, jnp.zeros_like(k)) for m in masks]
        vh = [jnp.where(m, v, jnp.zeros_like(v)) for m in vmask]
        dk_sc[...] = jnp.zeros_like(dk_sc)
        dv_sc[...] = jnp.zeros_like(dv_sc)

        def step(qi):
            rows = pl.ds(pl.multiple_of(qi * tq, tq), tq)
            q = q_ref[rows, :]
            dov = do_ref[rows, :]
            lse8 = lse_ref[qi]
            dl8 = delta_sc[qi]
            for h in range(2):
                st = _dot_nt(kh[h], q)
                if scale != 1.0:
                    st = st * scale
                key = lax.broadcasted_iota(jnp.int32, st.shape, 0)
                qry = lax.broadcasted_iota(jnp.int32, st.shape, 1)
                st = jnp.where(key - qry <= (qi - kj) * tq, st, NEG_BIG)
                pt = jnp.exp(st - lse8[h:h + 1, :])
                dpt = _dot_nt(vh[h], dov)
                dst = pt * (dpt - dl8[h:h + 1, :])
                if bias_grad:
                    cs_sc[h] += _lane_halves(dst, jnp.add)
                    rsum_ref[qi, h:h + 1, :] += jnp.sum(dst, axis=0, keepdims=True)
                if scale != 1.0:
                    dst = dst * scale
                ptb = pt.astype(BF16)
                dstb = dst.astype(BF16)
                dv_sc[h] += jnp.dot(ptb, dov, preferred_element_type=F32)
                dk_sc[h] += jnp.dot(dstb, q, preferred_element_type=F32)
                dq_sc[rows, :] += _dot_tn(dstb, kh[h])

        def loop_body(t, carry):
            for u in range(big):
                step(t * big + u)
            return carry

        if big == 2:
            lax.fori_loop(kj // 2, nq // 2, loop_body, 0)
        else:
            half_empty = kj % big >= 2
            lax.fori_loop(kj // big + half_empty.astype(jnp.int32), nq // big, loop_body, 0)

            @pl.when(half_empty)
            def _():
                for u in range(2):
                    step((kj // big) * big + 2 + u)
        dkv_ref[:, 0:PAIR_Q] = (jnp.where(masks[0], dk_sc[0], 0.0) + jnp.where(masks[1], dk_sc[1], 0.0)).astype(BF16)
        dkv_ref[:, PAIR_Q:PAIR_KV] = jnp.where(lo, dv_sc[0], dv_sc[1]).astype(BF16)
        if bias_grad:
            csum_ref[...] = jnp.where(lo, jnp.sum(cs_sc[0], axis=1, keepdims=True),
                                      jnp.sum(cs_sc[1], axis=1, keepdims=True))

        @pl.when(kj == nq - 1)
        def _():
            dq_ref[...] = dq_sc[...].astype(BF16)

    rows_spec = pl.BlockSpec((nq, None, 8, tq), lambda b, p, j: (b, p, 0, 0))
    out_specs = [pl.BlockSpec((S, PAIR_Q), lambda b, p, j: (b, p)),
                 pl.BlockSpec((tq, PAIR_KV), lambda b, p, j: (b * nq + j, p))]
    out_shape = [jax.ShapeDtypeStruct((T, P * PAIR_Q), BF16), jax.ShapeDtypeStruct((T, P * PAIR_KV), BF16)]
    scratch = [pltpu.VMEM((S, PAIR_Q), F32), pltpu.VMEM((nq, 8, tq), F32),
               pltpu.VMEM((2, tq, PAIR_Q), F32), pltpu.VMEM((2, tq, LANES), F32)]
    if bias_grad:
        out_specs += [pl.BlockSpec((tq, LANES), lambda b, p, j: (b * nq + j, p)), rows_spec]
        out_shape += [jax.ShapeDtypeStruct((T, P * LANES), F32), jax.ShapeDtypeStruct((T // tq, P, 8, tq), F32)]
        scratch.append(pltpu.VMEM((2, tq, LANES), F32))
    return pl.pallas_call(
        body, name=name, grid=(B, P, nq),
        in_specs=[pl.BlockSpec((S, PAIR_Q), lambda b, p, j: (b, p)),
                  pl.BlockSpec((tq, PAIR_KV), lambda b, p, j: (b * nq + j, p)),
                  pl.BlockSpec((S, LANES), lambda b, p, j: (b, p)), rows_spec,
                  pl.BlockSpec((S, LANES), lambda b, p, j: (b, p))],
        out_specs=out_specs, out_shape=out_shape, scratch_shapes=scratch,
        compiler_params=_cparams(("parallel", "parallel", "arbitrary")),
    )(qx, kvx, o, lse, do)


def _fox_consts(P):
    H = 2 * P
    eq = np.zeros((3 * LANES, P * LANES), np.float32)
    ek = np.zeros((3 * LANES, P * LANES), np.float32)
    ones_q = np.zeros((1, P * LANES), np.float32)
    ones_k = np.zeros((1, P * LANES), np.float32)
    for h in range(H):
        base = (h // 2) * LANES + FOX_EXTRA * (h % 2)
        for part in range(3):
            eq[part * LANES + h, base + part] = 1.0
            ones_q[0, base + 3 + part] = 1.0
            ones_k[0, base + part] = 1.0
            ek[part * LANES + h, base + 3 + part] = -1.0
    return eq, ek, ones_q, ones_k


def _split3(f):
    hi = f.astype(BF16)
    r = f - hi.astype(F32)
    mid = r.astype(BF16)
    lo = (r - mid.astype(F32)).astype(BF16)
    return hi, mid, lo


def _tri_sum(tri, x):
    hi, mid, lo = _split3(x)
    return (jnp.dot(tri, hi, preferred_element_type=F32) + jnp.dot(tri, mid, preferred_element_type=F32)
            + jnp.dot(tri, lo, preferred_element_type=F32))


def _log1p_pos(e):
    return jnp.where(e < 0.01, e * (1.0 - e * (0.5 - e * (1.0 / 3.0))), jnp.log(1.0 + e))


def _fox_prep(qkv, fl, b_row, *, S, D, name):
    T = qkv.shape[0]
    P = D // LANES
    B = T // S
    tt = _tile(S, 256)
    per = S // tt
    eq, ek, ones_q, ones_k = _fox_consts(P)
    q_scale = HEAD_DIM ** -0.5

    def body(q_ref, k_ref, v_ref, fl_ref, b_ref, eq_ref, ek_ref, oq_ref, ok_ref, qx_ref, kvx_ref, carry):
        i = pl.program_id(1)

        @pl.when(i == 0)
        def _():
            carry[...] = jnp.zeros_like(carry)

        z = fl_ref[...] + b_ref[...]
        logf = jnp.minimum(z, 0.0) - _log1p_pos(jnp.exp(-jnp.abs(z)))
        row = lax.broadcasted_iota(jnp.int32, (tt, tt), 0)
        col = lax.broadcasted_iota(jnp.int32, (tt, tt), 1)
        tri = (col <= row).astype(BF16)
        f = _tri_sum(tri, logf) + carry[...]
        carry[...] = f[tt - 1:tt, :]
        parts = jnp.concatenate(_split3(f), axis=1)
        xq = jnp.dot(parts, eq_ref[...], preferred_element_type=F32) + oq_ref[...]
        xk = jnp.dot(parts, ek_ref[...], preferred_element_type=F32) + ok_ref[...]
        for p in range(P):
            c = slice(p * LANES, (p + 1) * LANES)
            qx_ref[:, p * PAIR_Q:p * PAIR_Q + LANES] = (q_ref[:, c].astype(F32) * q_scale).astype(BF16)
            qx_ref[:, p * PAIR_Q + LANES:(p + 1) * PAIR_Q] = xq[:, c].astype(BF16)
            kvx_ref[:, p * PAIR_KV:p * PAIR_KV + LANES] = k_ref[:, c]
            kvx_ref[:, p * PAIR_KV + LANES:p * PAIR_KV + PAIR_Q] = xk[:, c].astype(BF16)
            kvx_ref[:, p * PAIR_KV + PAIR_Q:(p + 1) * PAIR_KV] = v_ref[:, c]

    tok = lambda b, i: (b * per + i, 0)
    const = lambda b, i: (0, 0)
    return pl.pallas_call(
        body, name=name, grid=(B, per),
        in_specs=[pl.BlockSpec((tt, D), lambda b, i: (b * per + i, 0)),
                  pl.BlockSpec((tt, D), lambda b, i: (b * per + i, 1)),
                  pl.BlockSpec((tt, D), lambda b, i: (b * per + i, 2)),
                  pl.BlockSpec((tt, LANES), tok), pl.BlockSpec((1, LANES), const),
                  pl.BlockSpec(eq.shape, const), pl.BlockSpec(ek.shape, const),
                  pl.BlockSpec(ones_q.shape, const), pl.BlockSpec(ones_k.shape, const)],
        out_specs=[pl.BlockSpec((tt, P * PAIR_Q), tok), pl.BlockSpec((tt, P * PAIR_KV), tok)],
        out_shape=[jax.ShapeDtypeStruct((T, P * PAIR_Q), BF16), jax.ShapeDtypeStruct((T, P * PAIR_KV), BF16)],
        scratch_shapes=[pltpu.VMEM((1, LANES), F32)],
        compiler_params=_cparams(("arbitrary", "arbitrary")),
    )(qkv, qkv, qkv, fl, b_row, jnp.asarray(eq, BF16), jnp.asarray(ek, BF16), jnp.asarray(ones_q), jnp.asarray(ones_k))


def _fox_unprep(dqx, dkvx, csum, rsum, fl, b_row, *, S, D, name):
    T = dqx.shape[0]
    P = D // LANES
    B = T // S
    tt = _tile(S, 256)
    per = S // tt
    q_scale = HEAD_DIM ** -0.5

    def body(dq_ref, dkv_ref, cs_ref, rs_ref, fl_ref, b_ref, dqkv_ref, dfl_ref, db_ref, carry):
        b = pl.program_id(0)
        i = pl.program_id(1)

        @pl.when(i == 0)
        def _():
            carry[...] = jnp.zeros_like(carry)

        @pl.when((i == 0) & (b == 0))
        def _():
            db_ref[...] = jnp.zeros_like(db_ref)

        df = rs_ref[...] - cs_ref[...]
        for p in range(P):
            rq = slice(p * LANES, (p + 1) * LANES)
            dqkv_ref[:, rq] = (dq_ref[:, p * PAIR_Q:p * PAIR_Q + LANES].astype(F32) * q_scale).astype(BF16)
            dqkv_ref[:, D + p * LANES:D + (p + 1) * LANES] = dkv_ref[:, p * PAIR_KV:p * PAIR_KV + LANES]
            dqkv_ref[:, 2 * D + p * LANES:2 * D + (p + 1) * LANES] = dkv_ref[:, p * PAIR_KV + PAIR_Q:(p + 1) * PAIR_KV]
        row = lax.broadcasted_iota(jnp.int32, (tt, tt), 0)
        col = lax.broadcasted_iota(jnp.int32, (tt, tt), 1)
        tri = (col >= row).astype(BF16)
        dlogf = _tri_sum(tri, df) + carry[...]
        carry[...] = dlogf[0:1, :]
        z = fl_ref[...] + b_ref[...]
        e = jnp.exp(-jnp.abs(z))
        sig_neg = jnp.where(z >= 0.0, e, 1.0) / (1.0 + e)
        dfl = dlogf * sig_neg
        dfl_ref[...] = dfl.astype(BF16)
        db_ref[...] += jnp.sum(dfl, axis=0, keepdims=True)

    rev = lambda b, i: (b * per + per - 1 - i, 0)
    const = lambda b, i: (0, 0)
    return pl.pallas_call(
        body, name=name, grid=(B, per),
        in_specs=[pl.BlockSpec((tt, P * PAIR_Q), rev), pl.BlockSpec((tt, P * PAIR_KV), rev),
                  pl.BlockSpec((tt, LANES), rev), pl.BlockSpec((tt, LANES), rev), pl.BlockSpec((tt, LANES), rev),
                  pl.BlockSpec((1, LANES), const)],
        out_specs=[pl.BlockSpec((tt, 3 * D), rev), pl.BlockSpec((tt, LANES), rev), pl.BlockSpec((1, LANES), const)],
        out_shape=[jax.ShapeDtypeStruct((T, 3 * D), BF16), jax.ShapeDtypeStruct((T, LANES), BF16),
                   jax.ShapeDtypeStruct((1, LANES), F32)],
        scratch_shapes=[pltpu.VMEM((1, LANES), F32)],
        compiler_params=_cparams(("arbitrary", "arbitrary")),
    )(dqx, dkvx, csum, rsum, fl, b_row)


def _rms(x):
    r = lax.rsqrt(jnp.mean(x * x, axis=-1, keepdims=True) + NORM_EPS)
    return x * r, r


def _mla_mid(lat, gq, gkv, cos_t, sin_s, *, name):
    T, W = lat.shape
    Rq = W - 2 * LANES
    tt = _tile(T, 512)

    def body(l_ref, gq_ref, gkv_ref, c_ref, s_ref, o_ref):
        nq, _ = _rms(l_ref[:, 0:Rq])
        nkv, _ = _rms(l_ref[:, Rq:Rq + LANES])
        o_ref[:, 0:Rq] = (nq * gq_ref[...]).astype(BF16)
        o_ref[:, Rq:Rq + LANES] = (nkv * gkv_ref[...]).astype(BF16)
        o_ref[:, Rq + LANES:W] = _rope128(l_ref[:, Rq + LANES:W], c_ref[...], s_ref[...]).astype(BF16)

    return pl.pallas_call(
        body, name=name, grid=(T // tt,),
        in_specs=[pl.BlockSpec((tt, W), lambda i: (i, 0)), pl.BlockSpec((1, Rq), lambda i: (0, 0)),
                  pl.BlockSpec((1, LANES), lambda i: (0, 0)), pl.BlockSpec((tt, LANES), lambda i: (i, 0)),
                  pl.BlockSpec((tt, LANES), lambda i: (i, 0))],
        out_specs=pl.BlockSpec((tt, W), lambda i: (i, 0)),
        out_shape=jax.ShapeDtypeStruct((T, W), BF16),
        compiler_params=_cparams(("parallel",)),
    )(lat, gq, gkv, cos_t, sin_s)


def _mla_mid_bwd(lat, dcq, dckr, gq, gkv, cos_t, sin_s, *, name):
    T, W = lat.shape
    Rq = W - 2 * LANES
    tt = _tile(T, 512)

    def norm_bwd(x, dy, g):
        n, r = _rms(x)
        dn = dy * g
        return r * (dn - n * jnp.mean(dn * n, axis=-1, keepdims=True)), jnp.sum(dy * n, axis=0, keepdims=True)

    def body(l_ref, dq_ref, dk_ref, gq_ref, gkv_ref, c_ref, s_ref, o_ref, dgq_ref, dgkv_ref):
        i = pl.program_id(0)

        @pl.when(i == 0)
        def _():
            dgq_ref[...] = jnp.zeros_like(dgq_ref)
            dgkv_ref[...] = jnp.zeros_like(dgkv_ref)

        dxq, dgq = norm_bwd(l_ref[:, 0:Rq], dq_ref[...], gq_ref[...])
        dxkv, dgkv = norm_bwd(l_ref[:, Rq:Rq + LANES], dk_ref[:, 0:LANES], gkv_ref[...])
        o_ref[:, 0:Rq] = dxq.astype(BF16)
        o_ref[:, Rq:Rq + LANES] = dxkv.astype(BF16)
        o_ref[:, Rq + LANES:W] = _rope128(dk_ref[:, LANES:2 * LANES], c_ref[...], -s_ref[...]).astype(BF16)
        dgq_ref[...] += dgq
        dgkv_ref[...] += dgkv

    return pl.pallas_call(
        body, name=name, grid=(T // tt,),
        in_specs=[pl.BlockSpec((tt, W), lambda i: (i, 0)), pl.BlockSpec((tt, Rq), lambda i: (i, 0)),
                  pl.BlockSpec((tt, 2 * LANES), lambda i: (i, 0)), pl.BlockSpec((1, Rq), lambda i: (0, 0)),
                  pl.BlockSpec((1, LANES), lambda i: (0, 0)), pl.BlockSpec((tt, LANES), lambda i: (i, 0)),
                  pl.BlockSpec((tt, LANES), lambda i: (i, 0))],
        out_specs=[pl.BlockSpec((tt, W), lambda i: (i, 0)), pl.BlockSpec((1, Rq), lambda i: (0, 0)),
                   pl.BlockSpec((1, LANES), lambda i: (0, 0))],
        out_shape=[jax.ShapeDtypeStruct((T, W), BF16), jax.ShapeDtypeStruct((1, Rq), F32),
                   jax.ShapeDtypeStruct((1, LANES), F32)],
        compiler_params=_cparams(("arbitrary",)),
    )(lat, dcq, dckr, gq, gkv, cos_t, sin_s)


def _uq_to_pairs(w):
    Rq = w.shape[0]
    P = w.shape[1] // (2 * (HEAD_DIM + ROPE_DIM))
    w4 = w.reshape(Rq, P, 2, HEAD_DIM + ROPE_DIM)
    nope = w4[..., :HEAD_DIM].reshape(Rq, P, 2 * HEAD_DIM)
    rope = w4[..., HEAD_DIM:].reshape(Rq, P, 2 * ROPE_DIM)
    pad = jnp.zeros((Rq, P, PAIR_Q - 2 * HEAD_DIM - 2 * ROPE_DIM), w.dtype)
    return jnp.concatenate([nope, rope, pad], axis=-1).reshape(Rq, P * PAIR_Q)


def _uq_from_pairs(g):
    Rq = g.shape[0]
    P = g.shape[1] // PAIR_Q
    g3 = g.reshape(Rq, P, PAIR_Q)
    nope = g3[..., :2 * HEAD_DIM].reshape(Rq, P, 2, HEAD_DIM)
    rope = g3[..., 2 * HEAD_DIM:2 * HEAD_DIM + 2 * ROPE_DIM].reshape(Rq, P, 2, ROPE_DIM)
    return jnp.concatenate([nope, rope], axis=-1).reshape(Rq, P * 2 * (HEAD_DIM + ROPE_DIM))


def _ukv_to_pairs(w):
    P = w.shape[1] // (4 * HEAD_DIM)
    w4 = w.reshape(KV_RANK, P, 2, 2 * HEAD_DIM)
    kn = w4[..., :HEAD_DIM].reshape(KV_RANK, P, 2 * HEAD_DIM)
    vv = w4[..., HEAD_DIM:].reshape(KV_RANK, P, 2 * HEAD_DIM)
    top = jnp.concatenate([kn, jnp.zeros((KV_RANK, P, LANES), w.dtype), vv], axis=-1)
    place = np.zeros((LANES, P, PAIR_KV), np.float32)
    for r in range(ROPE_DIM):
        place[r, :, LANES + r] = 1.0
        place[r, :, LANES + ROPE_DIM + r] = 1.0
    return jnp.concatenate([top, jnp.asarray(place, w.dtype)], axis=0).reshape(KV_RANK + LANES, P * PAIR_KV)


def _ukv_from_pairs(g):
    P = g.shape[1] // PAIR_KV
    g3 = g[:KV_RANK].reshape(KV_RANK, P, PAIR_KV)
    kn = g3[..., :2 * HEAD_DIM].reshape(KV_RANK, P, 2, HEAD_DIM)
    vv = g3[..., PAIR_Q:].reshape(KV_RANK, P, 2, HEAD_DIM)
    return jnp.concatenate([kn, vv], axis=-1).reshape(KV_RANK, P * 4 * HEAD_DIM)


def _mlp_fwd(h2, w, i, x1, gate, *, S):
    p, u = _mm(h2, w["mlp_w1"], "nn", name=f"mlp_up_{i}", b_layer=i, out_dtypes=(BF16, BF16),
               epilogue=lambda acc: (acc, jnp.square(jnp.maximum(acc, 0.0))))
    x2, z = _mm(u, w["mlp_w2"], "nn", name=f"mlp_down_{i}", b_layer=i, out_dtypes=(F32, F32), extras=(x1,),
                rowvecs=(gate,), seq=S, epilogue=lambda acc, xr, g: (xr + g * acc, acc))
    return x2, (p, u, z)


STACKED_GRADS = ("fox_out", "mla_down", "mla_uq", "mla_ukv", "mla_out", "mlp_w1", "mlp_w2")


def _local_step(x, target, pos_f, inv_freq_row, sign_row, mod, w, slots, *, S, after_layer0=None, before_layer0=None):
    T, D = x.shape
    L = mod.shape[0]
    L2 = len(w["fox_out"])
    n_split = w["mlp_w1"][0][0].shape[1]
    cos_t, sin_s = _rope_tables(pos_f, inv_freq_row, sign_row)
    saved = []
    for i in range(L):
        j = i // 2
        sh_m, sc_m, g_m, sh_f, sc_f, g_f = (mod[i, s] for s in range(6))
        h = _norm_mod(x, w["norm_mix_g"][i], sc_m, sh_m, S=S, name=f"norm_mix_{i}")
        if i % 2 == 0:
            qkv = _mm(h, w["fox_qkv"], "nn", name=f"fox_qkv_{i}", b_layer=j, out_dtypes=(BF16,))
            fl = _mm(h, w["fox_f"], "nn", name=f"fox_f_{i}", b_layer=j)
            qx, kvx = _fox_prep(qkv, fl, w["fox_b"][j], S=S, D=D, name=f"fox_prep_{i}")
            o, lse = _attn_fwd(qx, kvx, S=S, scale=1.0, ew=FOX_EXTRA, name=f"fox_attn_{i}")
            mix = (qx, kvx, o, lse, fl)
            w_out = w["fox_out"]
        else:
            lat = _mm(h, w["mla_down"], "nn", name=f"mla_down_{i}", b_layer=j)
            Rq = lat.shape[1] - 2 * LANES
            cqr = _mla_mid(lat, w["mla_gq"][j], w["mla_gkv"][j], cos_t, sin_s, name=f"mla_mid_{i}")
            qx = _mm(cqr, w["mla_uq"], "nn", name=f"mla_uq_{i}", b_layer=j, out_dtypes=(BF16,), a_sz=Rq, tk=Rq,
                     tables=(cos_t, sin_s), epilogue=lambda acc, c, s: (_rope_pairs(acc, c, s, 1.0),))
            kvx = _mm(cqr, w["mla_ukv"], "nn", name=f"mla_ukv_{i}", b_layer=j, out_dtypes=(BF16,), a_off=Rq,
                      a_sz=2 * LANES, tk=2 * LANES, tn=PAIR_KV)
            o, lse = _attn_fwd(qx, kvx, S=S, scale=(HEAD_DIM + ROPE_DIM) ** -0.5, ew=ROPE_DIM, name=f"mla_attn_{i}")
            mix = (qx, kvx, o, lse, lat, cqr)
            w_out = w["mla_out"]
        x1, y = _mm(o, w_out, "nn", name=f"mix_out_{i}", b_layer=j, out_dtypes=(F32, F32), extras=(x,),
                    rowvecs=(g_m,), seq=S, epilogue=lambda acc, xr, g: (xr + g * acc, acc))
        h2 = _norm_mod(x1, w["norm_mlp_g"][i], sc_f, sh_f, S=S, name=f"norm_mlp_{i}")
        x2, mlp = _mlp_fwd(h2, w, i, x1, g_f, S=S)
        saved.append((x, h, mix, y, x1, h2, mlp))
        x = x2
        if i == 0 and after_layer0 is not None:
            w = after_layer0(x, w)

    dx, dg_final, loss = _final_loss(x, target, w["final_norm_g"])

    grads = {k: [None] * len(w[k]) for k in ("norm_mix_g", "norm_mlp_g", "fox_b", "mla_gq", "mla_gkv")}
    grads.update({k: [None] * L2 for k in ("fox_qkv", "fox_f")})
    grads.update({k: {} for k in STACKED_GRADS})
    grads["final_norm_g"] = dg_final

    def stacked(key, layer, _, a, b, **kw):
        group, idx, count = slots[(key, layer)]
        grads[key][group] = _mm(a.T, b, "nn", out_stack=(grads[key].get(group), idx, count), **kw)

    dmod = [None] * L
    for i in reversed(range(L)):
        j = i // 2
        x0, h, mix, y, x1, h2, (p, u, z) = saved[i]
        sh_m, sc_m, g_m, sh_f, sc_f, g_f = (mod[i, s] for s in range(6))
        if i == 0 and before_layer0 is not None:
            g_f = g_f + before_layer0(grads)[0, 0]
        dz, dg_f = _gate_bwd(dx, z, g_f, S=S, name=f"gate_mlp_bwd_{i}")
        stacked("mlp_w2", i, L, u, dz, name=f"mlp_w2_grad_{i}")
        dp = _mm(dz, w["mlp_w2"], "nt", name=f"mlp_down_bwd_{i}", b_layer=i, out_dtypes=(BF16,), extras=(p,),
                 epilogue=lambda acc, pv: (acc * (2.0 * jnp.maximum(pv.astype(F32), 0.0)),))
        stacked("mlp_w1", i, L, h2, dp, name=f"mlp_w1_grad_{i}", out_split=n_split)
        dh2 = _mm(dp, w["mlp_w1"], "nt", name=f"mlp_up_bwd_{i}", b_layer=i)
        dx1, dsh_f, dsc_f, dgn = _norm_mod_bwd(x1, dh2, dx, w["norm_mlp_g"][i], sc_f, S=S, name=f"norm_mlp_bwd_{i}")
        grads["norm_mlp_g"][i] = dgn
        dy, dg_m = _gate_bwd(dx1, y, g_m, S=S, name=f"gate_mix_bwd_{i}")
        if i % 2 == 0:
            qx, kvx, o, lse, fl = mix
            stacked("fox_out", j, L2, o, dy, name=f"fox_out_grad_{i}")
            do = _mm(dy, w["fox_out"], "nt", name=f"fox_out_bwd_{i}", b_layer=j, out_dtypes=(BF16,))
            dqx, dkvx, csum, rsum = _attn_bwd(qx, kvx, o, lse, do, S=S, scale=1.0, ew=FOX_EXTRA,
                                              name=f"fox_attn_bwd_{i}", bias_grad=True)
            n_heads = D // HEAD_DIM
            csum = jnp.pad(csum.reshape(T, n_heads, HEAD_DIM)[:, :, 0], ((0, 0), (0, LANES - n_heads)))
            rsum = jnp.transpose(rsum[:, :, :2, :], (0, 3, 1, 2)).reshape(T, n_heads)
            rsum = jnp.pad(rsum, ((0, 0), (0, LANES - n_heads)))
            dqkv, dfl, db = _fox_unprep(dqx, dkvx, csum, rsum, fl, w["fox_b"][j], S=S, D=D, name=f"fox_unprep_{i}")
            grads["fox_b"][j] = db
            h_t = h.T
            grads["fox_qkv"][j] = _mm(h_t, dqkv, "nn", name=f"fox_qkv_grad_{i}")
            grads["fox_f"][j] = _mm(h_t, dfl, "nn", name=f"fox_f_grad_{i}")
            dh_f = _mm(dfl, w["fox_f"], "nt", name=f"fox_f_bwd_{i}", b_layer=j)
            dh = _mm(dqkv, w["fox_qkv"], "nt", name=f"fox_qkv_bwd_{i}", b_layer=j, extras=(dh_f,),
                     epilogue=lambda acc, e: (acc + e,))
        else:
            qx, kvx, o, lse, lat, cqr = mix
            Rq = lat.shape[1] - 2 * LANES
            stacked("mla_out", j, L2, o, dy, name=f"mla_out_grad_{i}")
            do = _mm(dy, w["mla_out"], "nt", name=f"mla_out_bwd_{i}", b_layer=j, out_dtypes=(BF16,))
            dqx, dkvx = _attn_bwd(qx, kvx, o, lse, do, S=S, scale=(HEAD_DIM + ROPE_DIM) ** -0.5, ew=ROPE_DIM,
                                  name=f"mla_attn_bwd_{i}")
            dqpre = _unrope(dqx, cos_t, sin_s)
            stacked("mla_uq", j, L2, cqr[:, :Rq], dqpre, name=f"mla_uq_grad_{i}", out_split=n_split)
            stacked("mla_ukv", j, L2, cqr[:, Rq:], dkvx, name=f"mla_ukv_grad_{i}", tn=PAIR_KV, out_split=n_split)
            dcq = _mm(dqpre, w["mla_uq"], "nt", name=f"mla_uq_bwd_{i}", b_layer=j)
            dckr = _mm(dkvx, w["mla_ukv"], "nt", name=f"mla_ukv_bwd_{i}", b_layer=j, tk=PAIR_KV * 2)
            dlat, dgq, dgkv = _mla_mid_bwd(lat, dcq, dckr, w["mla_gq"][j], w["mla_gkv"][j], cos_t, sin_s,
                                           name=f"mla_mid_bwd_{i}")
            grads["mla_gq"][j] = dgq
            grads["mla_gkv"][j] = dgkv
            stacked("mla_down", j, L2, h, dlat, name=f"mla_down_grad_{i}")
            dh = _mm(dlat, w["mla_down"], "nt", name=f"mla_down_bwd_{i}", b_layer=j)
        dx, dsh_m, dsc_m, dgn = _norm_mod_bwd(x0, dh, dx1, w["norm_mix_g"][i], sc_m, S=S, name=f"norm_mix_bwd_{i}")
        grads["norm_mix_g"][i] = dgn
        dmod[i] = jnp.stack([dsh_m, dsc_m, dg_m, dsh_f, dsc_f, dg_f])
    return loss, dx, jnp.stack(dmod), grads


GATHERED = ("fox_in", "fox_out", "mla_down", "mla_uq", "mla_ukv", "mla_out", "mlp_w1", "mlp_w2")
ROW_SHARDED = ("fox_out", "mla_down", "mla_out", "mlp_w2")


def _shard_layouts(wts):
    dkv = wts["mla_w_dkv"]
    dkv = jnp.pad(dkv, ((0, 0), (0, 0), (0, 2 * LANES - dkv.shape[2])))
    return {
        "fox_in": _pad_lanes(wts["fox_w_in"].astype(BF16)),
        "fox_out": wts["fox_w_out"].astype(BF16),
        "mla_down": jnp.concatenate([wts["mla_w_dq"], dkv], axis=2).astype(BF16),
        "mla_uq": jax.vmap(_uq_to_pairs)(wts["mla_w_uq"].astype(BF16)),
        "mla_ukv": jax.vmap(_ukv_to_pairs)(wts["mla_w_ukv"].astype(BF16)),
        "mla_out": wts["mla_w_out"].astype(BF16),
        "mlp_w1": wts["mlp_w1"].astype(BF16),
        "mlp_w2": wts["mlp_w2"].astype(BF16),
    }


def _small_layouts(small):
    return {
        "fox_b": [jnp.pad(b, (0, LANES - b.shape[0]))[None, :] for b in small["fox_b_f"]],
        "mla_gq": [g[None, :] for g in small["mla_q_norm_g"]],
        "mla_gkv": [g[None, :] for g in small["mla_kv_norm_g"]],
        "norm_mix_g": [g[None, :] for g in small["norm_mix_g"]],
        "norm_mlp_g": [g[None, :] for g in small["norm_mlp_g"]],
        "final_norm_g": small["final_norm_g"][None, :],
    }


def _comm_groups(L, L2):
    first = [("fox_in", 0, 1), ("fox_out", 0, 1), ("mlp_w1", 0, 1), ("mlp_w2", 0, 1)]
    rest = [("fox_in", 1, L2 - 1), ("fox_out", 1, L2 - 1), ("mla_down", 0, L2), ("mla_uq", 0, L2),
            ("mla_ukv", 0, L2), ("mla_out", 0, L2), ("mlp_w1", 1, L - 1), ("mlp_w2", 1, L - 1)]
    return {"first": first, "rest": [e for e in rest if e[2] > 0]}


def _layer_slots(groups):
    return {(n, s + l): (g, l, cnt) for g, entries in groups.items() for n, s, cnt in entries for l in range(cnt)}


def _pad_lanes(a):
    cols = a.shape[-1]
    return jnp.pad(a, [(0, 0)] * (a.ndim - 1) + [(0, -cols % LANES)])


def _weight_views(name, gathered, D, n_fox_heads):
    n, ns, rows, cols = gathered.shape
    if name == "fox_in":
        true_cols = (3 * D + n_fox_heads) // ns
        fox = jnp.concatenate([gathered[:, k, :, :true_cols] for k in range(ns)], axis=-1)
        return {"fox_qkv": fox[:, :, :3 * D], "fox_f": _pad_lanes(fox[:, :, 3 * D:])}
    if name in ROW_SHARDED:
        return {name: gathered.reshape(n, ns * rows, cols)}
    return {name: gathered}


def _grad_pieces(name, g, qkv_f, n_fox_heads, ns):
    if name == "fox_in":
        fox = jnp.stack([jnp.concatenate([a, b[:, :n_fox_heads]], axis=1) for a, b in qkv_f])
        cols = fox.shape[2] // ns
        return jnp.stack([_pad_lanes(fox[:, :, k * cols:(k + 1) * cols]) for k in range(ns)], axis=1)
    if name in ROW_SHARDED:
        return g.reshape(g.shape[0], ns, g.shape[1] // ns, g.shape[2])
    return g


def _small_grads(g, n_fox_heads):
    return {
        "norm_mix_g": jnp.concatenate(g["norm_mix_g"], axis=0),
        "norm_mlp_g": jnp.concatenate(g["norm_mlp_g"], axis=0),
        "final_norm_g": g["final_norm_g"][0],
        "fox_b_f": jnp.concatenate(g["fox_b"], axis=0)[:, :n_fox_heads],
        "mla_q_norm_g": jnp.concatenate(g["mla_gq"], axis=0),
        "mla_kv_norm_g": jnp.concatenate(g["mla_gkv"], axis=0),
    }


def _silu(c):
    return c * (1.0 / (1.0 + jnp.exp(-c)))


def _ada_fwd(c_all, ada_w, ada_b_cols):
    L, D, C = ada_w.shape
    Bg = c_all.shape[0]
    tc = _tile(C, 512)

    def body(c_ref, w_ref, b_ref, o_ref):
        ca = _silu(c_ref[...]).astype(BF16)
        o_ref[...] = jnp.dot(ca, w_ref[...].astype(BF16), preferred_element_type=F32) + b_ref[...]

    return pl.pallas_call(
        body, name="ada_fwd", grid=(L, C // tc),
        in_specs=[pl.BlockSpec((Bg, D), lambda l, j: (0, 0)), pl.BlockSpec((None, D, tc), lambda l, j: (l, 0, j)),
                  pl.BlockSpec((None, 1, tc), lambda l, j: (l, 0, j))],
        out_specs=pl.BlockSpec((None, Bg, tc), lambda l, j: (l, 0, j)),
        out_shape=jax.ShapeDtypeStruct((L, Bg, C), F32),
        compiler_params=_cparams(("parallel", "parallel")),
    )(c_all, ada_w, ada_b_cols)


def _ada_bwd(c_all, dmod_cols):
    L, Bg, C = dmod_cols.shape
    D = c_all.shape[1]
    tc = _tile(C, 512)

    def body(c_ref, d_ref, o_ref):
        ca = _silu(c_ref[...]).astype(BF16)
        o_ref[...] = _dot_tn(ca, d_ref[...].astype(BF16))

    return pl.pallas_call(
        body, name="ada_bwd", grid=(L, C // tc),
        in_specs=[pl.BlockSpec((Bg, D), lambda l, j: (0, 0)), pl.BlockSpec((None, Bg, tc), lambda l, j: (l, 0, j))],
        out_specs=pl.BlockSpec((None, D, tc), lambda l, j: (l, 0, j)),
        out_shape=jax.ShapeDtypeStruct((L, D, C), F32),
        compiler_params=_cparams(("parallel", "parallel")),
    )(c_all, dmod_cols)


def _adamw_update(w, gv, m, v):
    mn = ADAM_B1 * m + (1.0 - ADAM_B1) * gv
    vn = ADAM_B2 * v + (1.0 - ADAM_B2) * jnp.square(gv)
    m_hat = mn / (1.0 - ADAM_B1 ** ADAM_STEP)
    v_hat = vn / (1.0 - ADAM_B2 ** ADAM_STEP)
    return -ADAM_LR * (m_hat / (jnp.sqrt(v_hat) + ADAM_EPS) + ADAM_WD * w), mn, vn


def _adamw(w, g, m, v, *, name):
    shape = w.shape
    C = shape[-1]
    R = int(np.prod(shape[:-1])) if len(shape) > 1 else 1
    w2, g2, m2, v2 = (a.reshape(R, C) for a in (w, g, m, v))
    tr = _row_tile(R, C)

    def body(w_ref, g_ref, m_ref, v_ref, d_ref, nm_ref, nv_ref):
        d_ref[...], nm_ref[...], nv_ref[...] = _adamw_update(w_ref[...], g_ref[...], m_ref[...], v_ref[...])

    spec = pl.BlockSpec((tr, C), lambda i: (i, 0))
    out = pl.pallas_call(
        body, name=name, grid=(R // tr,), in_specs=[spec] * 4, out_specs=[spec] * 3,
        out_shape=[jax.ShapeDtypeStruct((R, C), F32)] * 3, compiler_params=_cparams(("parallel",)),
    )(w2, g2, m2, v2)
    return tuple(a.reshape(shape) for a in out)


def _adamw_halves(w, g_own, g_peer, m, v, c_idx, *, name):
    L, rows, C = w.shape
    R = rows // 2
    tr = _row_tile(R, C)

    def body(c_ref, w_ref, go_ref, gp_ref, m_ref, v_ref, g_ref, d_ref, nm_ref, nv_ref):
        gv = jnp.where(pl.program_id(1) == c_ref[0], go_ref[...], gp_ref[...])
        g_ref[...] = gv
        d_ref[...], nm_ref[...], nv_ref[...] = _adamw_update(w_ref[...], gv, m_ref[...], v_ref[...])

    full = pl.BlockSpec((None, None, tr, C), lambda l, hh, i, c_ref: (l, hh, i, 0))
    half = pl.BlockSpec((None, tr, C), lambda l, hh, i, c_ref: (l, i, 0))
    grid_spec = pltpu.PrefetchScalarGridSpec(
        num_scalar_prefetch=1, grid=(L, 2, R // tr), in_specs=[full, half, half, full, full], out_specs=[full] * 4)
    split = lambda a: a.reshape(L, 2, R, C)
    out = pl.pallas_call(
        body, name=name, grid_spec=grid_spec, out_shape=[jax.ShapeDtypeStruct((L, 2, R, C), F32)] * 4,
        compiler_params=_cparams(("parallel", "parallel", "parallel")),
    )(c_idx, split(w), g_own, g_peer, split(m), split(v))
    return tuple(a.reshape(w.shape) for a in out)


def _sum_gathered(dm8, sm8):
    n_dev, Bl, R, D = dm8.shape
    Rs = sm8.shape[1]

    def body(dm_ref, sm_ref, ob_ref, os_ref):
        acc_b = jnp.zeros((R, D), F32)
        acc_s = jnp.zeros((Rs, D), F32)
        for d in range(n_dev):
            for b in range(Bl):
                acc_b = acc_b + dm_ref[d, b]
            acc_s = acc_s + sm_ref[d]
        ob_ref[...] = acc_b
        os_ref[...] = acc_s

    return pl.pallas_call(
        body, name="sum_gathered",
        out_shape=[jax.ShapeDtypeStruct((R, D), F32), jax.ShapeDtypeStruct((Rs, D), F32)],
        compiler_params=_cparams(None),
    )(dm8, sm8)


N_DEV = 8
N_CHIP = 4
ANY = pl.BlockSpec(memory_space=pl.ANY)
HBM = pl.BlockSpec(memory_space=pltpu.HBM)
SEM = pl.BlockSpec(memory_space=pltpu.SEMAPHORE)
DATAFLOW = pltpu.SideEffectType.DATAFLOW_SIDE_EFFECTING


def _mesh_pos():
    return lax.axis_index("x"), lax.axis_index("y"), lax.axis_index("c")


def _all_gather8(block, *, name, in_vmem):
    R, W = block.shape

    def body(x_ref, out_ref, send_sems, recv_sems, local_sem):
        x, y, c = _mesh_pos()
        me, sibling = (x, y, c), (x, y, 1 - c)
        chips = [(1 - x, y), (x, 1 - y), (1 - x, 1 - y)]

        def slot(px, py, pc):
            return out_ref.at[4 * px + 2 * py + pc]

        def copy(k, blk, to, src=None):
            return pltpu.make_async_remote_copy(
                src_ref=slot(*blk) if src is None else src, dst_ref=slot(*blk),
                send_sem=send_sems.at[k], recv_sem=recv_sems.at[k], device_id=to, device_id_type=MESH_ID)

        mine = pltpu.make_async_copy(x_ref, slot(*me), local_sem)
        mine.start()
        first = [copy(0, me, sibling, src=x_ref)]
        first += [copy(1 + j, me, (*chip, c), src=x_ref) for j, chip in enumerate(chips)]
        for cp in first:
            cp.start()
        passed = [copy(4 + j, (*chip, c), sibling) for j, chip in enumerate(chips)]
        for j, chip in enumerate(chips):
            copy(1 + j, (*chip, c), me).wait_recv()
            passed[j].start()
        copy(0, sibling, me).wait_recv()
        for j, chip in enumerate(chips):
            copy(4 + j, (*chip, 1 - c), me).wait_recv()
        for cp in first + passed:
            cp.wait_send()
        mine.wait()

    space = pl.BlockSpec(memory_space=pltpu.VMEM) if in_vmem else ANY
    return pl.pallas_call(
        body, name=name, out_shape=jax.ShapeDtypeStruct((N_DEV, R, W), block.dtype),
        in_specs=[space], out_specs=space,
        scratch_shapes=[pltpu.SemaphoreType.DMA((7,)), pltpu.SemaphoreType.DMA((7,)), pltpu.SemaphoreType.DMA],
        compiler_params=pltpu.CompilerParams(vmem_limit_bytes=VMEM_LIMIT_V7X),
    )(block)


def _comm_call(body, arrays, out_shapes, n_sems, *, name):
    return pl.pallas_call(
        body, name=name, out_shape=out_shapes, in_specs=[ANY] * len(arrays), out_specs=[ANY] * len(out_shapes),
        scratch_shapes=[pltpu.SemaphoreType.DMA((n_sems,)), pltpu.SemaphoreType.DMA((n_sems,)),
                        pltpu.SemaphoreType.DMA((len(arrays),))],
    )(*arrays)


def _gather_weights(shards, *, name):
    n = len(shards)

    def body(*refs):
        xs, outs = refs[:n], refs[n:2 * n]
        send_sems, recv_sems, local_sems = refs[2 * n:]
        x, y, c = _mesh_pos()
        me, sibling = (x, y, c), (x, y, 1 - c)
        chips = [(1 - x, y), (x, 1 - y), (1 - x, 1 - y)]
        waits = []
        for i in range(n):
            nl = shards[i].shape[0]
            own = xs[i].at[pl.ds(0, nl), c]

            def slot(px, py, pc, i=i, nl=nl):
                return outs[i].at[pl.ds(0, nl), 2 * px + py, pc]

            def copy(k, blk, to, src=None, i=i, slot=slot):
                return pltpu.make_async_remote_copy(
                    src_ref=slot(*blk) if src is None else src, dst_ref=slot(*blk),
                    send_sem=send_sems.at[7 * i + k], recv_sem=recv_sems.at[7 * i + k], device_id=to,
                    device_id_type=MESH_ID)

            mine = pltpu.make_async_copy(own, slot(*me), local_sems.at[i])
            mine.start()
            first = [copy(0, me, sibling, src=own)]
            first += [copy(1 + j, me, (*chip, c), src=own) for j, chip in enumerate(chips)]
            for cp in first:
                cp.start()
            waits.append((copy, mine, first))
        for copy, mine, first in waits:
            passed = [copy(4 + j, (*chip, c), sibling) for j, chip in enumerate(chips)]
            for j, chip in enumerate(chips):
                copy(1 + j, (*chip, c), me).wait_recv()
                passed[j].start()
            copy(0, sibling, me).wait_recv()
            for j, chip in enumerate(chips):
                copy(4 + j, (*chip, 1 - c), me).wait_recv()
            for cp in first + passed:
                cp.wait_send()
            mine.wait()

    out_shapes = [jax.ShapeDtypeStruct((s.shape[0], N_CHIP) + s.shape[1:], s.dtype) for s in shards]
    return _comm_call(body, shards, out_shapes, 7 * n, name=name)


def _place_own(shard, chip_idx, c_idx, *, name):
    n, _, rows, cols = shard.shape
    tr = _row_tile(rows, cols)

    def body(k_ref, c_ref, x_ref, o_ref):
        o_ref[...] = x_ref[...]

    grid_spec = pltpu.PrefetchScalarGridSpec(
        num_scalar_prefetch=2, grid=(n, rows // tr),
        in_specs=[pl.BlockSpec((None, None, tr, cols), lambda l, i, k_ref, c_ref: (l, c_ref[0], i, 0))],
        out_specs=pl.BlockSpec((None, None, None, tr, cols), lambda l, i, k_ref, c_ref: (l, k_ref[0], c_ref[0], i, 0)))
    return pl.pallas_call(
        body, name=name, grid_spec=grid_spec,
        out_shape=jax.ShapeDtypeStruct((n, N_CHIP, 2, rows, cols), shard.dtype),
        compiler_params=_cparams(("parallel", "parallel")),
    )(chip_idx, c_idx, shard)


def _gather_copies(x_refs, land_refs, send_sems, recv_sems):
    x, y, c = _mesh_pos()
    k_me = 2 * x + y
    targets = [(x, y, 1 - c), (1 - x, y, c), (x, 1 - y, c), (1 - x, 1 - y, c)]
    copies = []
    for i, (x_ref, land_ref) in enumerate(zip(x_refs, land_refs)):
        nl = x_ref.shape[0]
        for j, to in enumerate(targets):
            copies.append(pltpu.make_async_remote_copy(
                src_ref=x_ref.at[pl.ds(0, nl), c], dst_ref=land_ref.at[pl.ds(0, nl), k_me, c],
                send_sem=send_sems.at[4 * i + j], recv_sem=recv_sems.at[4 * i + j], device_id=to,
                device_id_type=MESH_ID))
    return copies


def _split_start(copies_fn, srcs, lands, after, *, name, sems_per_array):
    n = len(srcs)

    def body(*refs):
        send_sems, recv_sems = refs[2 * n + 1], refs[2 * n + 2]
        for cp in copies_fn(refs[:n], refs[n:2 * n], send_sems, recv_sems):
            cp.start()
        refs[-1][...] = jnp.zeros_like(refs[-1])

    operands = [pltpu.with_memory_space_constraint(a, pltpu.HBM) for a in list(srcs) + list(lands)]
    n_sems = sems_per_array * n
    out_shape = ([pltpu.SemaphoreType.DMA((n_sems,)), pltpu.SemaphoreType.DMA((n_sems,))]
                 + [pltpu.HBM(a.shape, a.dtype) for a in operands] + [jax.ShapeDtypeStruct((8, LANES), F32)])
    res = pl.pallas_call(
        body, name=name, out_shape=out_shape, in_specs=[HBM] * (2 * n) + [ANY],
        out_specs=[SEM, SEM] + [HBM] * (2 * n) + [pl.BlockSpec(memory_space=pltpu.VMEM)],
        input_output_aliases={i: 2 + i for i in range(2 * n)},
        compiler_params=pltpu.CompilerParams(has_side_effects=DATAFLOW),
    )(*operands, after)
    return res[0], res[1], list(res[2:2 + n]), list(res[2 + n:2 + 2 * n]), res[-1]


def _split_wait(copies_fn, send_sems, recv_sems, srcs, lands, after, *, name):
    n = len(srcs)

    def body(*refs):
        for cp in copies_fn(refs[:n], refs[n:2 * n], refs[2 * n], refs[2 * n + 1]):
            cp.wait_send()
            cp.wait_recv()

    res = pl.pallas_call(
        body, name=name, out_shape=[pltpu.HBM(a.shape, a.dtype) for a in list(srcs) + list(lands)],
        in_specs=[HBM] * (2 * n) + [SEM, SEM, ANY], out_specs=[HBM] * (2 * n),
        input_output_aliases={i: i for i in range(2 * n)},
        compiler_params=pltpu.CompilerParams(has_side_effects=DATAFLOW),
    )(*srcs, *lands, send_sems, recv_sems, after)
    return list(res[:n]), list(res[n:])


def _gather_forward(lands, *, name):
    n = len(lands)

    def body(*refs):
        xs = refs[:n]
        send_sems, recv_sems, _ = refs[2 * n:]
        x, y, c = _mesh_pos()
        chips = [(1 - x, y), (x, 1 - y), (1 - x, 1 - y)]
        copies = []
        for i in range(n):
            nl = lands[i].shape[0]
            for j, (cx, cy) in enumerate(chips):
                here = xs[i].at[pl.ds(0, nl), 2 * cx + cy, c]
                cp = pltpu.make_async_remote_copy(
                    src_ref=here, dst_ref=here, send_sem=send_sems.at[3 * i + j], recv_sem=recv_sems.at[3 * i + j],
                    device_id=(x, y, 1 - c), device_id_type=MESH_ID)
                cp.start()
                copies.append(cp)
        for cp in copies:
            cp.wait()

    return pl.pallas_call(
        body, name=name, out_shape=[jax.ShapeDtypeStruct(a.shape, a.dtype) for a in lands],
        in_specs=[ANY] * n, out_specs=[ANY] * n, input_output_aliases={i: i for i in range(n)},
        scratch_shapes=[pltpu.SemaphoreType.DMA((3 * n,)), pltpu.SemaphoreType.DMA((3 * n,)),
                        pltpu.SemaphoreType.DMA((1,))],
    )(*lands)


def _pair_exchange(gs, *, name):
    n = len(gs)

    def body(*refs):
        xs, outs = refs[:n], refs[n:2 * n]
        send_sems, recv_sems, _ = refs[2 * n:]
        x, y, c = _mesh_pos()
        copies = []
        for i in range(n):
            nl, ns = gs[i].shape[:2]
            cp = pltpu.make_async_remote_copy(
                src_ref=xs[i].at[pl.ds(0, nl), pl.ds(0, ns), 1 - c], dst_ref=outs[i], send_sem=send_sems.at[i],
                recv_sem=recv_sems.at[i], device_id=(x, y, 1 - c), device_id_type=MESH_ID)
            cp.start()
            copies.append(cp)
        for cp in copies:
            cp.wait()

    out_shapes = [jax.ShapeDtypeStruct(g.shape[:2] + g.shape[3:], g.dtype) for g in gs]
    return _comm_call(body, gs, out_shapes, n, name=name)


def _chip_copies(p_refs, land_refs, send_sems, recv_sems):
    x, y, c = _mesh_pos()
    k_me = 2 * x + y
    chips = [(1 - x, y), (x, 1 - y), (1 - x, 1 - y)]
    copies = []
    for i, (p_ref, land_ref) in enumerate(zip(p_refs, land_refs)):
        nl = p_ref.shape[0]
        for j, (cx, cy) in enumerate(chips):
            copies.append(pltpu.make_async_remote_copy(
                src_ref=p_ref.at[pl.ds(0, nl), 2 * cx + cy], dst_ref=land_ref.at[k_me],
                send_sem=send_sems.at[3 * i + j], recv_sem=recv_sems.at[3 * i + j],
                device_id=(cx, cy, c), device_id_type=MESH_ID))
    return copies


def _chip_landing(ps):
    return [lax.empty((p.shape[1], p.shape[0]) + p.shape[2:], p.dtype) for p in ps]


def _chip_exchange(ps, *, name):
    n = len(ps)

    def body(*refs):
        send_sems, recv_sems, _ = refs[2 * n:]
        copies = _chip_copies(refs[:n], refs[n:2 * n], send_sems, recv_sems)
        for cp in copies:
            cp.start()
        for cp in copies:
            cp.wait()

    out_shapes = [jax.ShapeDtypeStruct((p.shape[1], p.shape[0]) + p.shape[2:], p.dtype) for p in ps]
    return _comm_call(body, ps, out_shapes, 3 * n, name=name)


def _pair_swap(ss, *, name):
    n = len(ss)

    def body(*refs):
        xs, outs = refs[:n], refs[n:2 * n]
        send_sems, recv_sems, _ = refs[2 * n:]
        x, y, c = _mesh_pos()
        copies = []
        for i in range(n):
            cp = pltpu.make_async_remote_copy(src_ref=xs[i], dst_ref=outs[i], send_sem=send_sems.at[i],
                                              recv_sem=recv_sems.at[i], device_id=(x, y, 1 - c),
                                              device_id_type=MESH_ID)
            cp.start()
            copies.append(cp)
        for cp in copies:
            cp.wait()

    out_shapes = [jax.ShapeDtypeStruct(s.shape, s.dtype) for s in ss]
    return _comm_call(body, ss, out_shapes, n, name=name)


def _row_tile(rows, cols):
    tr = rows
    while tr * cols > 256 * 1024 and tr % 16 == 0:
        tr //= 2
    return tr


def _pair_add(g, recv, c_idx, *, name):
    n, ns, _, rows, W = g.shape
    tr = _row_tile(rows, W)

    def body(c_ref, g_ref, r_ref, o_ref):
        o_ref[...] = (g_ref[...] + r_ref[...]).astype(BF16)

    piece = pl.BlockSpec((None, tr, W), lambda p, i, c_ref: (p, i, 0))
    grid_spec = pltpu.PrefetchScalarGridSpec(
        num_scalar_prefetch=1, grid=(n * ns, rows // tr),
        in_specs=[pl.BlockSpec((None, None, tr, W), lambda p, i, c_ref: (p, c_ref[0], i, 0)), piece],
        out_specs=piece)
    out = pl.pallas_call(
        body, name=name, grid_spec=grid_spec, out_shape=jax.ShapeDtypeStruct((n * ns, rows, W), BF16),
        compiler_params=_cparams(("parallel", "parallel")),
    )(c_idx, g.reshape(n * ns, 2, rows, W), recv.reshape(n * ns, rows, W))
    return out.reshape(n, ns, rows, W)


def _sum_pieces(land, own, chip_idx, *, name):
    n, nl, A, W = land.shape
    tr = _row_tile(A, W)

    def body(k_ref, l_ref, o_ref, out_ref):
        acc = jnp.zeros(out_ref.shape, F32)
        for k in range(n):
            acc = acc + jnp.where(k == k_ref[0], o_ref[...], l_ref[k]).astype(F32)
        out_ref[...] = acc

    grid_spec = pltpu.PrefetchScalarGridSpec(
        num_scalar_prefetch=1, grid=(nl, A // tr),
        in_specs=[pl.BlockSpec((n, None, tr, W), lambda l, i, k_ref: (0, l, i, 0)),
                  pl.BlockSpec((None, None, tr, W), lambda l, i, k_ref: (l, k_ref[0], i, 0))],
        out_specs=pl.BlockSpec((None, tr, W), lambda l, i, k_ref: (l, i, 0)))
    return pl.pallas_call(
        body, name=name, grid_spec=grid_spec, out_shape=jax.ShapeDtypeStruct((nl, A, W), F32),
        compiler_params=_cparams(("parallel", "parallel")),
    )(chip_idx, land, own)


SMALL = ("norm_mix_g", "norm_mlp_g", "final_norm_g", "fox_b_f", "mla_q_norm_g", "mla_kv_norm_g")
WEIGHT_ORDER = ("ada_w", "ada_b", "norm_mix_g", "norm_mlp_g", "fox_w_in", "fox_b_f", "fox_w_out", "mla_w_dq",
                "mla_q_norm_g", "mla_w_uq", "mla_w_dkv", "mla_kv_norm_g", "mla_w_ukv", "mla_w_out", "mlp_w1",
                "mlp_w2", "final_norm_g")


def _small_rows(vals, D):
    rows = [vals["norm_mix_g"], vals["norm_mlp_g"], vals["final_norm_g"][None, :]]
    for n in ("fox_b_f", "mla_q_norm_g", "mla_kv_norm_g"):
        flat = vals[n].reshape(-1)
        assert flat.shape[0] <= D
        rows.append(jnp.pad(flat, (0, D - flat.shape[0]))[None, :])
    return jnp.concatenate(rows, axis=0)


def _small_unrows(rows, shapes):
    L = shapes["norm_mix_g"][0]
    out = {"norm_mix_g": rows[0:L], "norm_mlp_g": rows[L:2 * L], "final_norm_g": rows[2 * L]}
    for k, n in enumerate(("fox_b_f", "mla_q_norm_g", "mla_kv_norm_g")):
        size = int(np.prod(shapes[n]))
        out[n] = rows[2 * L + 1 + k, :size].reshape(shapes[n])
    return out


def kernel(x, c, positions, ada_w, ada_b, norm_mix_g, norm_mlp_g, fox_w_in, fox_b_f, fox_w_out, mla_w_dq, mla_q_norm_g, mla_w_uq, mla_w_dkv, mla_kv_norm_g, mla_w_ukv, mla_w_out, mlp_w1, mlp_w2, final_norm_g, loss_target, m_ada_w, m_ada_b, m_norm_mix_g, m_norm_mlp_g, m_fox_w_in, m_fox_b_f, m_fox_w_out, m_mla_w_dq, m_mla_q_norm_g, m_mla_w_uq, m_mla_w_dkv, m_mla_kv_norm_g, m_mla_w_ukv, m_mla_w_out, m_mlp_w1, m_mlp_w2, m_final_norm_g, v_ada_w, v_ada_b, v_norm_mix_g, v_norm_mlp_g, v_fox_w_in, v_fox_b_f, v_fox_w_out, v_mla_w_dq, v_mla_q_norm_g, v_mla_w_uq, v_mla_w_dkv, v_mla_kv_norm_g, v_mla_w_ukv, v_mla_w_out, v_mlp_w1, v_mlp_w2, v_final_norm_g):
    args = dict(locals())
    wts = {n: args[n] for n in WEIGHT_ORDER}
    mom = {n: args["m_" + n] for n in WEIGHT_ORDER}
    var = {n: args["v_" + n] for n in WEIGHT_ORDER}
    Bl, S, D = x.shape
    T = Bl * S
    L = ada_w.shape[0]
    C = ada_w.shape[2]
    mx, my, mc = _mesh_pos()
    chip = 2 * mx + my
    dev = 4 * mx + 2 * my + mc
    c_idx = jnp.reshape(mc, (1,)).astype(jnp.int32)
    chip_idx = jnp.reshape(chip, (1,)).astype(jnp.int32)
    small = {n: wts[n] for n in SMALL}
    L2, q_cols = mla_q_norm_g.shape
    n_fox_heads = fox_b_f.shape[1]

    shards = _shard_layouts(wts)
    groups = _comm_groups(L, L2)
    slots = _layer_slots(groups)

    def row_halves(a):
        return a.reshape(a.shape[:-2] + (2, a.shape[-2] // 2, a.shape[-1]))

    def whole_rows(a):
        return a.reshape(a.shape[:2] + (a.shape[2] * a.shape[3], a.shape[4]))

    part = {g: [row_halves(shards[n][s:s + cnt]) for n, s, cnt in entries] for g, entries in groups.items()}
    first = _gather_weights(part["first"], name="gather_first")
    own_placed = [_place_own(a, chip_idx, c_idx, name=f"gather_place_{n}")
                  for a, (n, _, _) in zip(part["rest"], groups["rest"])]
    rest_sems = _split_start(_gather_copies, part["rest"], own_placed, first[0], name="gather_rest_start",
                             sems_per_array=4)

    def layer_weights(w, group, arrays):
        for (n, s, cnt), a in zip(groups[group], arrays):
            for key, view in _weight_views(n, whole_rows(a), D, n_fox_heads).items():
                for l in range(cnt):
                    w[key][s + l] = (view, l)

    w = {key: [None] * L2 for key in ("fox_qkv", "fox_f", "fox_out", "mla_down", "mla_uq", "mla_ukv", "mla_out")}
    w.update({key: [None] * L for key in ("mlp_w1", "mlp_w2")})
    layer_weights(w, "first", first)

    def after_layer0(x_now, w):
        _, landed = _split_wait(_gather_copies, *rest_sems[:4], x_now, name="gather_rest_wait")
        layer_weights(w, "rest", _gather_forward(landed, name="gather_rest_forward"))
        return w

    c_pad = jnp.concatenate([c, jnp.pad(mla_q_norm_g, ((0, 8 - Bl - L2), (0, D - q_cols)))], axis=0)
    c8 = _all_gather8(c_pad, name="gather_c", in_vmem=True)
    c_all = c8[:, :Bl].reshape(N_DEV * Bl, D)
    qg4 = c8.reshape(N_CHIP, 2, 8, D)[:, 0, Bl:Bl + L2, :q_cols]
    small["mla_q_norm_g"] = jnp.transpose(qg4, (1, 0, 2)).reshape(L2, N_CHIP * q_cols)
    ada_b_cols = lax.dynamic_slice_in_dim(ada_b, chip * C, C, axis=1)[:, None, :]
    mod_cols = _ada_fwd(c_all, ada_w, ada_b_cols)
    mod8 = _all_gather8(mod_cols.reshape(L * N_DEV * Bl, C), name="gather_mod", in_vmem=True)
    mod4 = mod8.reshape(N_CHIP, 2, L, N_DEV * Bl, C)[:, 0]
    mod_me = lax.dynamic_slice_in_dim(mod4, dev * Bl, Bl, axis=2)
    mod = jnp.transpose(mod_me, (1, 2, 0, 3)).reshape(L, Bl, 6, D)
    mod = jnp.transpose(mod, (0, 2, 1, 3))[:, :, :, None, :]

    w.update(_small_layouts(small))
    mod = mod + rest_sems[4][0, 0]
    pending = {}

    def grad_pieces(group, g_now):
        out = []
        for n, s, cnt in groups[group]:
            qkv_f = [(g_now["fox_qkv"][j], g_now["fox_f"][j]) for j in range(s, s + cnt)] if n == "fox_in" else None
            stacked_g = None if n == "fox_in" else g_now[n][group]
            out.append(row_halves(_grad_pieces(n, stacked_g, qkv_f, n_fox_heads, N_CHIP)))
        return out

    def pair_sums(group, g_now):
        big = grad_pieces(group, g_now)
        sibling = _pair_exchange(big, name=f"grad_pair_exchange_{group}")
        return [_pair_add(a, r, c_idx, name=f"grad_pair_add_{group}_{n}")
                for (n, _, _), a, r in zip(groups[group], big, sibling)]

    def before_layer0(g_now):
        ps = pair_sums("rest", g_now)
        pending["rest"] = _split_start(_chip_copies, ps, _chip_landing(ps), chip_idx, name="grad_exchange_rest_start",
                                       sems_per_array=3)
        return pending["rest"][4]

    half = ROPE_DIM // 2
    inv_freq = ROPE_THETA ** (-jnp.arange(0, ROPE_DIM, 2, dtype=F32) / ROPE_DIM)
    lane = np.arange(LANES)
    inv_freq_row = jnp.tile(inv_freq, LANES // half)[None, :]
    sign_row = jnp.asarray(np.where(lane < 2 * ROPE_DIM, np.where(lane % ROPE_DIM < half, -1.0, 1.0), 0.0), F32)[None, :]
    pos_f = positions.astype(F32).reshape(T, 1)
    loss_row, grad_x, dmod, g = _local_step(x.reshape(T, D), loss_target.reshape(T, D), pos_f, inv_freq_row, sign_row,
                                            mod, w, slots, S=S, after_layer0=after_layer0, before_layer0=before_layer0)
    g_small = _small_grads(g, n_fox_heads)
    ps_first = pair_sums("first", g)
    pending["first"] = _split_start(_chip_copies, ps_first, _chip_landing(ps_first), chip_idx,
                                    name="grad_exchange_first_start", sems_per_array=3)

    Rs = -(-(2 * L + 5) // 8) * 8
    srows = jnp.concatenate([_small_rows(g_small, D), jnp.pad(loss_row, ((0, 0), (0, D - LANES)))], axis=0)
    srows = jnp.pad(srows, ((0, Rs - srows.shape[0]), (0, 0)))
    drows = jnp.transpose(dmod[:, :, :, 0, :], (2, 0, 1, 3)).reshape(Bl * L * 6, D)
    both8 = _all_gather8(jnp.concatenate([drows, srows], axis=0), name="gather_small", in_vmem=True)
    dm8 = both8[:, :Bl * L * 6].reshape(N_DEV, Bl, L * 6, D)
    sm8 = both8[:, Bl * L * 6:]
    adb_rows, small_sum = _sum_gathered(dm8, sm8)
    grad_ada_b = adb_rows.reshape(L, 6 * D)
    loss = small_sum[2 * L + 4, 0]
    small_shapes = {n: (wts[n].shape if n != "mla_q_norm_g" else (wts[n].shape[0], N_CHIP * q_cols)) for n in SMALL}
    gs = _small_unrows(small_sum, small_shapes)
    gs["mla_q_norm_g"] = lax.dynamic_slice_in_dim(gs["mla_q_norm_g"], chip * q_cols, q_cols, axis=1)

    dmod16 = jnp.transpose(dm8.reshape(N_DEV, Bl, L, 6 * D), (2, 0, 1, 3)).reshape(L, N_DEV * Bl, 6 * D)
    dmod_cols = lax.dynamic_slice_in_dim(dmod16, chip * C, C, axis=2)
    grad_ada_w = _ada_bwd(c_all, dmod_cols)

    grads = dict(gs)
    grads["ada_w"] = grad_ada_w
    grads["ada_b"] = grad_ada_b
    delta, new_m, new_v = {}, {}, {}
    for n in ("ada_w", "ada_b"):
        delta[n], new_m[n], new_v[n] = _adamw(wts[n], grads[n], mom[n], var[n], name=f"adamw_{n}")
    shard_small_shapes = {n: wts[n].shape for n in SMALL}
    packs = [jnp.pad(_small_rows({n: src[n] for n in SMALL}, D), ((0, Rs - 2 * L - 4), (0, 0)))
             for src in (wts, grads, mom, var)]
    for dst, rows in zip((delta, new_m, new_v), _adamw(*packs, name="adamw_small")):
        dst.update(_small_unrows(rows, shard_small_shapes))

    halves = {}
    for group, after in (("rest", grad_x), ("first", delta["ada_w"])):
        send_sems, recv_sems, ps, lands, _ = pending[group]
        ps, lands = _split_wait(_chip_copies, send_sems, recv_sems, ps, lands, after, name=f"grad_exchange_{group}_wait")
        sums = [_sum_pieces(ld, p, chip_idx, name=f"grad_sum_{group}_{n}")
                for (n, _, _), ld, p in zip(groups[group], lands, ps)]
        swapped = _pair_swap(sums, name=f"grad_pair_swap_{group}")
        for (n, _, _), a, b in zip(groups[group], sums, swapped):
            halves[(n, group)] = (a, b)

    def all_layers(n, which):
        return jnp.concatenate([halves[(n, grp)][which] for grp in ("first", "rest") if (n, grp) in halves], axis=0)

    own = {n: all_layers(n, 0) for n in GATHERED}
    peer = {n: all_layers(n, 1) for n in GATHERED}
    for nat, n in (("fox_w_in", "fox_in"), ("fox_w_out", "fox_out"), ("mla_w_out", "mla_out"), ("mlp_w1", "mlp_w1"),
                   ("mlp_w2", "mlp_w2")):
        cols = wts[nat].shape[-1]
        res = _adamw_halves(_pad_lanes(wts[nat]), own[n], peer[n], _pad_lanes(mom[nat]), _pad_lanes(var[nat]), c_idx,
                            name=f"adamw_{nat}")
        grads[nat], delta[nat], new_m[nat], new_v[nat] = (a[..., :cols] for a in res)
    joined = {n: jnp.concatenate([jnp.where(mc == 0, own[n], peer[n]), jnp.where(mc == 0, peer[n], own[n])], axis=1)
              for n in ("mla_down", "mla_uq", "mla_ukv")}
    rq = mla_w_dq.shape[-1]
    grads["mla_w_dq"] = joined["mla_down"][:, :, :rq]
    grads["mla_w_dkv"] = joined["mla_down"][:, :, rq:rq + KV_RANK + ROPE_DIM]
    grads["mla_w_uq"] = jax.vmap(_uq_from_pairs)(joined["mla_uq"])
    grads["mla_w_ukv"] = jax.vmap(_ukv_from_pairs)(joined["mla_ukv"])
    for n in ("mla_w_dq", "mla_w_dkv", "mla_w_uq", "mla_w_ukv"):
        delta[n], new_m[n], new_v[n] = _adamw(wts[n], grads[n], mom[n], var[n], name=f"adamw_{n}")

    return (loss, grad_x.reshape(Bl, S, D), *[grads[n] for n in WEIGHT_ORDER], *[delta[n] for n in WEIGHT_ORDER],
            *[new_m[n] for n in WEIGHT_ORDER], *[new_v[n] for n in WEIGHT_ORDER])
```

```python
import functools

import numpy as np
import jax
import jax.numpy as jnp
from jax import lax
from jax.experimental import pallas as pl
from jax.experimental.pallas import tpu as pltpu

F32 = jnp.float32
BF16 = jnp.bfloat16
MESH_ID = pl.DeviceIdType.MESH

NORM_EPS = 1e-6
ROPE_THETA = 10000.0
HEAD_DIM = 64
ROPE_DIM = 32
KV_RANK = 128
FOX_EXTRA = 6
PAIR_Q = 256
PAIR_KV = 384
LANES = 128
ADAM_LR = 0.001
ADAM_B1 = 0.9
ADAM_B2 = 0.999
ADAM_EPS = 1e-08
ADAM_WD = 0.01
ADAM_STEP = 10
VMEM_LIMIT_V7X = 48 * 1024 * 1024
MM_VMEM_BUDGET = 36 * 1024 * 1024
NEG_BIG = -1e30
ATTN_UNROLL = 4

BIG_WEIGHTS = (("fox_w_in", 2), ("fox_w_out", 1), ("mla_w_dq", 1), ("mla_w_uq", 2), ("mla_w_dkv", 1),
               ("mla_w_ukv", 2), ("mla_w_out", 1), ("mlp_w1", 2), ("mlp_w2", 1))


def _cparams(sem=None):
    return pltpu.CompilerParams(dimension_semantics=sem, vmem_limit_bytes=VMEM_LIMIT_V7X)


def _tile(n, want):
    if n <= want:
        return n
    for t in range(want - want % LANES, 0, -LANES):
        if n % t == 0:
            return t
    raise ValueError((n, want))


def _mm(a, b, mode, *, name, out_dtypes=(F32,), epilogue=None, extras=(), rowvecs=(), tables=(),
        seq=None, a_off=0, a_sz=None, b_layer=None, out_stack=None, out_split=0, tm=1024, tn=1024, tk=2048):
    if isinstance(b, (list, tuple)):
        b, b_layer = b[b_layer]
    b_rows, b_cols = b.shape[-2], b.shape[-1]
    n_split = b.shape[1] if b.ndim == 4 else 1
    assert mode in ("nn", "nt")
    if mode == "nn":
        M, K, N = a.shape[0], b_rows, b_cols * n_split
    else:
        M, K, N = a.shape[0], b_cols * n_split, b_rows
    assert a_sz is None or a_sz == K
    tm = _tile(seq if rowvecs else M, tm)
    n_piece = N // max(out_split, n_split if mode == "nn" else 1, 1)
    tn = _tile(n_piece, tn)
    tk = _tile(K // (n_split if mode == "nt" else 1), tk)
    ne, nr, nt_ = len(extras), len(rowvecs), len(tables)
    no = len(out_dtypes)

    def vmem_estimate():
        blocks = tm * tk * a.dtype.itemsize + tk * tn * b.dtype.itemsize
        blocks += tm * tn * (sum(e.dtype.itemsize for e in extras) + sum(jnp.dtype(d).itemsize for d in out_dtypes))
        return 2 * blocks + 2 * tm * tn * 4

    while vmem_estimate() > MM_VMEM_BUDGET and max(tm, tn) > 256:
        if tn >= tm:
            tn //= 2
        else:
            tm //= 2
    nk = K // tk

    assert a_off % tk == 0
    a_spec = pl.BlockSpec((tm, tk), lambda i, j, k: (i, k + a_off // tk))
    dims = (((1,), (0,)), ((), ())) if mode == "nn" else (((1,), (1,)), ((), ()))
    lead = () if b.ndim == 2 else (b_layer,)
    sq = (None,) * (b.ndim - 2)
    if mode == "nt":
        kb = b_cols // tk
        if b.ndim == 4:
            b_spec = pl.BlockSpec(sq + (tn, tk), lambda i, j, k: lead + (k // kb, j, k % kb))
        else:
            b_spec = pl.BlockSpec(sq + (tn, tk), lambda i, j, k: lead + (j, k))
    else:
        nb = b_cols // tn
        if b.ndim == 4:
            b_spec = pl.BlockSpec(sq + (tk, tn), lambda i, j, k: lead + (j // nb, k, j % nb))
        else:
            b_spec = pl.BlockSpec(sq + (tk, tn), lambda i, j, k: lead + (k, j))
    in_specs = [a_spec, b_spec]
    in_specs += [pl.BlockSpec((tm, tn), lambda i, j, k: (i, j)) for _ in extras]
    if rowvecs:
        assert seq % tm == 0
        per = seq // tm
        in_specs += [pl.BlockSpec((None, 1, tn), lambda i, j, k: (i // per, 0, j)) for _ in rowvecs]
    in_specs += [pl.BlockSpec((tm, LANES), lambda i, j, k: (i, 0)) for _ in tables]
    operands = [a, b, *extras, *rowvecs, *tables]
    aliases = {}
    if out_stack is None:
        out_specs = [pl.BlockSpec((tm, tn), lambda i, j, k: (i, j)) for _ in out_dtypes]
        out_shape = [jax.ShapeDtypeStruct((M, N), d) for d in out_dtypes]
    else:
        prev, layer, n_layers = out_stack
        assert no == 1
        if out_split:
            ob = n_piece // tn
            out_specs = [pl.BlockSpec((None, None, tm, tn), lambda i, j, k: (layer, j // ob, i, j % ob))]
            out_shape = [jax.ShapeDtypeStruct((n_layers, out_split, M, n_piece), out_dtypes[0])]
        else:
            out_specs = [pl.BlockSpec((None, tm, tn), lambda i, j, k: (layer, i, j))]
            out_shape = [jax.ShapeDtypeStruct((n_layers, M, N), out_dtypes[0])]
        if prev is not None:
            in_specs.append(pl.BlockSpec(memory_space=pl.ANY))
            aliases = {len(operands): 0}
            operands.append(prev)
    n_in = len(operands)

    def body(*refs):
        a_ref, b_ref = refs[0], refs[1]
        side = refs[2:2 + ne + nr + nt_]
        outs = refs[n_in:n_in + no]

        def finish(acc):
            res = (acc,) if epilogue is None else epilogue(acc, *[r[...] for r in side])
            for o_ref, r in zip(outs, res):
                o_ref[...] = r.astype(o_ref.dtype)

        part = lax.dot_general(a_ref[...].astype(BF16), b_ref[...].astype(BF16), dims,
                               preferred_element_type=F32)
        if nk == 1:
            finish(part)
        else:
            acc_ref = refs[-1]
            k = pl.program_id(2)

            @pl.when(k == 0)
            def _():
                acc_ref[...] = part

            @pl.when(k > 0)
            def _():
                acc_ref[...] += part

            @pl.when(k == nk - 1)
            def _():
                finish(acc_ref[...])

    res = pl.pallas_call(
        body, name=name, grid=(M // tm, N // tn, nk), in_specs=in_specs, out_specs=out_specs,
        out_shape=out_shape, scratch_shapes=[pltpu.VMEM((tm, tn), F32)] if nk > 1 else [],
        input_output_aliases=aliases,
        compiler_params=_cparams(("parallel", "parallel", "arbitrary")),
    )(*operands)
    return res[0] if no == 1 else tuple(res)


def _rope128(x, cos_t, sin_s):
    lane = lax.broadcasted_iota(jnp.int32, x.shape, 1)
    first = (lane % ROPE_DIM) < (ROPE_DIM // 2)
    swapped = jnp.where(first, pltpu.roll(x, LANES - ROPE_DIM // 2, 1), pltpu.roll(x, ROPE_DIM // 2, 1))
    return x * cos_t + swapped * sin_s


def _rope_pairs(acc, cos_t, sin_s, sign):
    parts = []
    for p in range(acc.shape[1] // PAIR_Q):
        parts.append(acc[:, p * PAIR_Q:p * PAIR_Q + LANES])
        parts.append(_rope128(acc[:, p * PAIR_Q + LANES:(p + 1) * PAIR_Q], cos_t, sign * sin_s))
    return jnp.concatenate(parts, axis=1)


def _rope_tables(pos_f, inv_freq_row, sign_row):
    T = pos_f.shape[0]
    tt = _tile(T, 512)

    def body(p_ref, f_ref, s_ref, cos_ref, sin_ref):
        ang = p_ref[...] * f_ref[...]
        cos_ref[...] = jnp.cos(ang)
        sin_ref[...] = jnp.sin(ang) * s_ref[...]

    return pl.pallas_call(
        body, name="rope_tables", grid=(T // tt,),
        in_specs=[pl.BlockSpec((tt, 1), lambda i: (i, 0)), pl.BlockSpec((1, LANES), lambda i: (0, 0)),
                  pl.BlockSpec((1, LANES), lambda i: (0, 0))],
        out_specs=[pl.BlockSpec((tt, LANES), lambda i: (i, 0))] * 2,
        out_shape=[jax.ShapeDtypeStruct((T, LANES), F32)] * 2,
        compiler_params=_cparams(("parallel",)),
    )(pos_f, inv_freq_row, sign_row)


def _unrope(dqx, cos_t, sin_s):
    T, W = dqx.shape
    tt = _tile(T, 512)

    def body(d_ref, c_ref, s_ref, o_ref):
        o_ref[...] = _rope_pairs(d_ref[...].astype(F32), c_ref[...], s_ref[...], -1.0).astype(BF16)

    return pl.pallas_call(
        body, name="mla_unrope", grid=(T // tt,),
        in_specs=[pl.BlockSpec((tt, W), lambda i: (i, 0)), pl.BlockSpec((tt, LANES), lambda i: (i, 0)),
                  pl.BlockSpec((tt, LANES), lambda i: (i, 0))],
        out_specs=pl.BlockSpec((tt, W), lambda i: (i, 0)),
        out_shape=jax.ShapeDtypeStruct((T, W), BF16),
        compiler_params=_cparams(("parallel",)),
    )(dqx, cos_t, sin_s)


def _row_specs(tt, D, per, n):
    return [pl.BlockSpec((None, 1, D), lambda i: (i // per, 0, 0)) for _ in range(n)]


def _norm_mod(x, gain, sc, sh, *, S, name):
    T, D = x.shape
    tt = _tile(S, 512)
    per = S // tt

    def body(x_ref, g_ref, sc_ref, sh_ref, h_ref):
        xv = x_ref[...]
        r = lax.rsqrt(jnp.mean(xv * xv, axis=-1, keepdims=True) + NORM_EPS)
        h_ref[...] = ((xv * r) * g_ref[...] * (1.0 + sc_ref[...]) + sh_ref[...]).astype(BF16)

    return pl.pallas_call(
        body, name=name, grid=(T // tt,),
        in_specs=[pl.BlockSpec((tt, D), lambda i: (i, 0)), pl.BlockSpec((1, D), lambda i: (0, 0))]
        + _row_specs(tt, D, per, 2),
        out_specs=pl.BlockSpec((tt, D), lambda i: (i, 0)),
        out_shape=jax.ShapeDtypeStruct((T, D), BF16),
        compiler_params=_cparams(("parallel",)),
    )(x, gain, sc, sh)


def _norm_mod_bwd(x, dh, dres, gain, sc, *, S, name):
    T, D = x.shape
    B = T // S
    tt = _tile(S, 512)
    per = S // tt

    def body(x_ref, dh_ref, dres_ref, g_ref, sc_ref, dx_ref, dsh_ref, dsc_ref, dg_ref):
        i = pl.program_id(0)
        xv = x_ref[...]
        dhv = dh_ref[...].astype(F32)
        r = lax.rsqrt(jnp.mean(xv * xv, axis=-1, keepdims=True) + NORM_EPS)
        n = xv * r
        g = g_ref[...]
        one_sc = 1.0 + sc_ref[...]
        dn = dhv * (g * one_sc)
        dx_ref[...] = dres_ref[...] + r * (dn - n * jnp.mean(dn * n, axis=-1, keepdims=True))
        dhn = dhv * n

        @pl.when(i % per == 0)
        def _():
            dsh_ref[...] = jnp.zeros_like(dsh_ref)
            dsc_ref[...] = jnp.zeros_like(dsc_ref)

        @pl.when(i == 0)
        def _():
            dg_ref[...] = jnp.zeros_like(dg_ref)

        dsh_ref[...] += jnp.sum(dhv, axis=0, keepdims=True)
        dsc_ref[...] += jnp.sum(dhn, axis=0, keepdims=True) * g
        dg_ref[...] += jnp.sum(dhn, axis=0, keepdims=True) * one_sc

    return pl.pallas_call(
        body, name=name, grid=(T // tt,),
        in_specs=[pl.BlockSpec((tt, D), lambda i: (i, 0))] * 3 + [pl.BlockSpec((1, D), lambda i: (0, 0))]
        + _row_specs(tt, D, per, 1),
        out_specs=[pl.BlockSpec((tt, D), lambda i: (i, 0))] + _row_specs(tt, D, per, 2)
        + [pl.BlockSpec((1, D), lambda i: (0, 0))],
        out_shape=[jax.ShapeDtypeStruct((T, D), F32), jax.ShapeDtypeStruct((B, 1, D), F32),
                   jax.ShapeDtypeStruct((B, 1, D), F32), jax.ShapeDtypeStruct((1, D), F32)],
        compiler_params=_cparams(("arbitrary",)),
    )(x, dh, dres, gain, sc)


def _gate_bwd(dx, y, g, *, S, name):
    T, D = dx.shape
    B = T // S
    tt = _tile(S, 512)
    per = S // tt

    def body(dx_ref, y_ref, g_ref, dy_ref, dg_ref):
        i = pl.program_id(0)
        dxv = dx_ref[...]
        dy_ref[...] = (dxv * g_ref[...]).astype(BF16)

        @pl.when(i % per == 0)
        def _():
            dg_ref[...] = jnp.zeros_like(dg_ref)

        dg_ref[...] += jnp.sum(dxv * y_ref[...], axis=0, keepdims=True)

    return pl.pallas_call(
        body, name=name, grid=(T // tt,),
        in_specs=[pl.BlockSpec((tt, D), lambda i: (i, 0))] * 2 + _row_specs(tt, D, per, 1),
        out_specs=[pl.BlockSpec((tt, D), lambda i: (i, 0))] + _row_specs(tt, D, per, 1),
        out_shape=[jax.ShapeDtypeStruct((T, D), BF16), jax.ShapeDtypeStruct((B, 1, D), F32)],
        compiler_params=_cparams(("arbitrary",)),
    )(dx, y, g)


def _final_loss(x, target, gain):
    T, D = x.shape
    tt = _tile(T, 512)

    def body(x_ref, t_ref, g_ref, dx_ref, dg_ref, loss_ref):
        i = pl.program_id(0)
        xv = x_ref[...]
        r = lax.rsqrt(jnp.mean(xv * xv, axis=-1, keepdims=True) + NORM_EPS)
        n = xv * r
        g = g_ref[...]
        err = n * g - t_ref[...]
        dy = err * (1.0 / D)
        dn = dy * g
        dx_ref[...] = r * (dn - n * jnp.mean(dn * n, axis=-1, keepdims=True))

        @pl.when(i == 0)
        def _():
            dg_ref[...] = jnp.zeros_like(dg_ref)
            loss_ref[...] = jnp.zeros_like(loss_ref)

        dg_ref[...] += jnp.sum(dy * n, axis=0, keepdims=True)
        loss_ref[...] += jnp.sum(jnp.sum(err * err, axis=-1, keepdims=True), axis=0, keepdims=True) * (0.5 / D)

    return pl.pallas_call(
        body, name="final_loss", grid=(T // tt,),
        in_specs=[pl.BlockSpec((tt, D), lambda i: (i, 0))] * 2 + [pl.BlockSpec((1, D), lambda i: (0, 0))],
        out_specs=[pl.BlockSpec((tt, D), lambda i: (i, 0)), pl.BlockSpec((1, D), lambda i: (0, 0)),
                   pl.BlockSpec((1, LANES), lambda i: (0, 0))],
        out_shape=[jax.ShapeDtypeStruct((T, D), F32), jax.ShapeDtypeStruct((1, D), F32),
                   jax.ShapeDtypeStruct((1, LANES), F32)],
        compiler_params=_cparams(("arbitrary",)),
    )(x, target, gain)


def _head_masks(ew):
    lane = lax.broadcasted_iota(jnp.int32, (1, PAIR_Q), 1)
    m0 = (lane < HEAD_DIM) | ((lane >= LANES) & (lane < LANES + ew))
    m1 = ((lane >= HEAD_DIM) & (lane < LANES)) | ((lane >= LANES + ew) & (lane < LANES + 2 * ew))
    return m0, m1


def _dot_nt(a, b):
    return lax.dot_general(a, b, (((1,), (1,)), ((), ())), preferred_element_type=F32)


def _dot_tn(a, b):
    return lax.dot_general(a, b, (((0,), (0,)), ((), ())), preferred_element_type=F32)


def _lane_halves(x, op):
    acc = x[:, 0:LANES]
    for g in range(1, x.shape[1] // LANES):
        acc = op(acc, x[:, g * LANES:(g + 1) * LANES])
    return acc


def _head_rows(cols_lane_replicated):
    t = cols_lane_replicated.T
    sub = lax.broadcasted_iota(jnp.int32, (8, t.shape[1]), 0)
    return jnp.where(sub == 1, t[HEAD_DIM:HEAD_DIM + 8], t[0:8])


def _attn_trip(n_blocks):
    return ATTN_UNROLL if n_blocks % ATTN_UNROLL == 0 else 2


def _attn_fwd(qx, kvx, *, S, scale, ew, name):
    T = qx.shape[0]
    P = qx.shape[1] // PAIR_Q
    B = T // S
    tq = _tile(S, 256)
    nq = S // tq
    big = _attn_trip(nq)

    def body(q_ref, kv_ref, o_ref, lse_ref, m_sc, l_sc, acc_sc):
        qi = pl.program_id(2)
        q = q_ref[...]
        masks = _head_masks(ew)
        qh = [jnp.where(m, q, jnp.zeros_like(q)) for m in masks]

        def logits(h, k, kj):
            s = _dot_nt(qh[h], k)
            if scale != 1.0:
                s = s * scale
            row = lax.broadcasted_iota(jnp.int32, s.shape, 0)
            col = lax.broadcasted_iota(jnp.int32, s.shape, 1)
            return jnp.where(col - row <= (qi - kj) * tq, s, NEG_BIG)

        def trip(first, count):
            rows = [pl.ds(pl.multiple_of((first + u) * tq, tq), tq) for u in range(count)]
            for h in range(2):
                ss = [logits(h, kv_ref[rows[u], 0:PAIR_Q], first + u) for u in range(count)]
                m_prev = m_sc[h]
                m_elem = m_prev
                for s in ss:
                    m_elem = jnp.maximum(m_elem, _lane_halves(s, jnp.maximum))
                m_new = jnp.broadcast_to(jnp.max(m_elem, axis=1, keepdims=True), (tq, LANES))
                alpha = jnp.exp(m_prev - m_new)
                l = alpha * l_sc[h]
                acc = alpha * acc_sc[h]
                for u, s in enumerate(ss):
                    p = jnp.concatenate([jnp.exp(s[:, g * LANES:(g + 1) * LANES] - m_new)
                                         for g in range(tq // LANES)], axis=1)
                    l = l + _lane_halves(p, jnp.add)
                    acc = acc + jnp.dot(p.astype(BF16), kv_ref[rows[u], PAIR_Q:PAIR_KV], preferred_element_type=F32)
                m_sc[h] = m_new
                l_sc[h] = l
                acc_sc[h] = acc

        m_sc[...] = jnp.full(m_sc.shape, NEG_BIG, F32)
        l_sc[...] = jnp.zeros_like(l_sc)
        acc_sc[...] = jnp.zeros_like(acc_sc)

        def loop_body(t, carry):
            trip(t * big, big)
            return carry

        if big == 2:
            lax.fori_loop(0, (qi + 2) // 2, loop_body, 0)
        else:
            trips = (qi + 2) // big
            lax.fori_loop(0, trips, loop_body, 0)

            @pl.when(qi % big <= 1)
            def _():
                trip(trips * big, 2)

        lane = lax.broadcasted_iota(jnp.int32, (tq, LANES), 1)
        lo = lane < HEAD_DIM
        l = [jnp.sum(l_sc[h], axis=1, keepdims=True) for h in range(2)]
        o_ref[...] = jnp.where(lo, acc_sc[0] / l[0], acc_sc[1] / l[1]).astype(BF16)
        lse_ref[...] = _head_rows(jnp.where(lo, m_sc[0] + jnp.log(l[0]), m_sc[1] + jnp.log(l[1])))

    return pl.pallas_call(
        body, name=name, grid=(B, P, nq),
        in_specs=[pl.BlockSpec((tq, PAIR_Q), lambda b, p, i: (b * nq + i, p)),
                  pl.BlockSpec((S, PAIR_KV), lambda b, p, i: (b, p))],
        out_specs=[pl.BlockSpec((tq, LANES), lambda b, p, i: (b * nq + i, p)),
                   pl.BlockSpec((None, None, 8, tq), lambda b, p, i: (b * nq + i, p, 0, 0))],
        out_shape=[jax.ShapeDtypeStruct((T, P * LANES), BF16), jax.ShapeDtypeStruct((T // tq, P, 8, tq), F32)],
        scratch_shapes=[pltpu.VMEM((2, tq, LANES), F32)] * 3,
        compiler_params=_cparams(("parallel", "parallel", "arbitrary")),
    )(qx, kvx)


def _attn_bwd(qx, kvx, o, lse, do, *, S, scale, ew, name, bias_grad=False):
    T = qx.shape[0]
    P = qx.shape[1] // PAIR_Q
    B = T // S
    tq = _tile(S, 256)
    nq = S // tq
    big = _attn_trip(nq)

    def body(q_ref, kv_ref, o_ref, lse_ref, do_ref, dq_ref, dkv_ref, *rest):
        kj = pl.program_id(2)
        if bias_grad:
            csum_ref, rsum_ref, dq_sc, delta_sc, dk_sc, dv_sc, cs_sc = rest
            cs_sc[...] = jnp.zeros_like(cs_sc)

            @pl.when(kj == 0)
            def _():
                rsum_ref[...] = jnp.zeros_like(rsum_ref)
        else:
            dq_sc, delta_sc, dk_sc, dv_sc = rest
        masks = _head_masks(ew)
        lane = lax.broadcasted_iota(jnp.int32, (tq, LANES), 1)
        lo = lane < HEAD_DIM
        vmask = [lo, jnp.logical_not(lo)]

        @pl.when(kj == 0)
        def _():
            dq_sc[...] = jnp.zeros_like(dq_sc)
            for c in range(nq):
                rows = pl.ds(c * tq, tq)
                x = do_ref[rows, :].astype(F32) * o_ref[rows, :].astype(F32)
                r0 = jnp.sum(jnp.where(lo, x, 0.0), axis=1, keepdims=True)
                r1 = jnp.sum(jnp.where(lo, 0.0, x), axis=1, keepdims=True)
                delta_sc[c] = _head_rows(jnp.where(lo, r0, r1))

        k = kv_ref[:, 0:PAIR_Q]
        v = kv_ref[:, PAIR_Q:PAIR_KV]
        kh = [jnp.where(m, k, jnp.zeros_like(k)) for m in masks]
        vh = [jnp.where(m, v, jnp.zeros_like(v)) for m in vmask]
        dk_sc[...] = jnp.zeros_like(dk_sc)
        dv_sc[...] = jnp.zeros_like(dv_sc)

        def step(qi):
            rows = pl.ds(pl.multiple_of(qi * tq, tq), tq)
            q = q_ref[rows, :]
            dov = do_ref[rows, :]
            lse8 = lse_ref[qi]
            dl8 = delta_sc[qi]
            for h in range(2):
                st = _dot_nt(kh[h], q)
                if scale != 1.0:
                    st = st * scale
                key = lax.broadcasted_iota(jnp.int32, st.shape, 0)
                qry = lax.broadcasted_iota(jnp.int32, st.shape, 1)
                st = jnp.where(key - qry <= (qi - kj) * tq, st, NEG_BIG)
                pt = jnp.exp(st - lse8[h:h + 1, :])
                dpt = _dot_nt(vh[h], dov)
                dst = pt * (dpt - dl8[h:h + 1, :])
                if bias_grad:
                    cs_sc[h] += _lane_halves(dst, jnp.add)
                    rsum_ref[qi, h:h + 1, :] += jnp.sum(dst, axis=0, keepdims=True)
                if scale != 1.0:
                    dst = dst * scale
                ptb = pt.astype(BF16)
                dstb = dst.astype(BF16)
                dv_sc[h] += jnp.dot(ptb, dov, preferred_element_type=F32)
                dk_sc[h] += jnp.dot(dstb, q, preferred_element_type=F32)
                dq_sc[rows, :] += _dot_tn(dstb, kh[h])

        def loop_body(t, carry):
            for u in range(big):
                step(t * big + u)
            return carry

        if big == 2:
            lax.fori_loop(kj // 2, nq // 2, loop_body, 0)
        else:
            half_empty = kj % big >= 2
            lax.fori_loop(kj // big + half_empty.astype(jnp.int32), nq // big, loop_body, 0)

            @pl.when(half_empty)
            def _():
                for u in range(2):
                    step((kj // big) * big + 2 + u)
        dkv_ref[:, 0:PAIR_Q] = (jnp.where(masks[0], dk_sc[0], 0.0) + jnp.where(masks[1], dk_sc[1], 0.0)).astype(BF16)
        dkv_ref[:, PAIR_Q:PAIR_KV] = jnp.where(lo, dv_sc[0], dv_sc[1]).astype(BF16)
        if bias_grad:
            csum_ref[...] = jnp.where(lo, jnp.sum(cs_sc[0], axis=1, keepdims=True),
                                      jnp.sum(cs_sc[1], axis=1, keepdims=True))

        @pl.when(kj == nq - 1)
        def _():
            dq_ref[...] = dq_sc[...].astype(BF16)

    rows_spec = pl.BlockSpec((nq, None, 8, tq), lambda b, p, j: (b, p, 0, 0))
    out_specs = [pl.BlockSpec((S, PAIR_Q), lambda b, p, j: (b, p)),
                 pl.BlockSpec((tq, PAIR_KV), lambda b, p, j: (b * nq + j, p))]
    out_shape = [jax.ShapeDtypeStruct((T, P * PAIR_Q), BF16), jax.ShapeDtypeStruct((T, P * PAIR_KV), BF16)]
    scratch = [pltpu.VMEM((S, PAIR_Q), F32), pltpu.VMEM((nq, 8, tq), F32),
               pltpu.VMEM((2, tq, PAIR_Q), F32), pltpu.VMEM((2, tq, LANES), F32)]
    if bias_grad:
        out_specs += [pl.BlockSpec((tq, LANES), lambda b, p, j: (b * nq + j, p)), rows_spec]
        out_shape += [jax.ShapeDtypeStruct((T, P * LANES), F32), jax.ShapeDtypeStruct((T // tq, P, 8, tq), F32)]
        scratch.append(pltpu.VMEM((2, tq, LANES), F32))
    return pl.pallas_call(
        body, name=name, grid=(B, P, nq),
        in_specs=[pl.BlockSpec((S, PAIR_Q), lambda b, p, j: (b, p)),
                  pl.BlockSpec((tq, PAIR_KV), lambda b, p, j: (b * nq + j, p)),
                  pl.BlockSpec((S, LANES), lambda b, p, j: (b, p)), rows_spec,
                  pl.BlockSpec((S, LANES), lambda b, p, j: (b, p))],
        out_specs=out_specs, out_shape=out_shape, scratch_shapes=scratch,
        compiler_params=_cparams(("parallel", "parallel", "arbitrary")),
    )(qx, kvx, o, lse, do)


def _fox_consts(P):
    H = 2 * P
    eq = np.zeros((3 * LANES, P * LANES), np.float32)
    ek = np.zeros((3 * LANES, P * LANES), np.float32)
    ones_q = np.zeros((1, P * LANES), np.float32)
    ones_k = np.zeros((1, P * LANES), np.float32)
    for h in range(H):
        base = (h // 2) * LANES + FOX_EXTRA * (h % 2)
        for part in range(3):
            eq[part * LANES + h, base + part] = 1.0
            ones_q[0, base + 3 + part] = 1.0
            ones_k[0, base + part] = 1.0
            ek[part * LANES + h, base + 3 + part] = -1.0
    return eq, ek, ones_q, ones_k


def _split3(f):
    hi = f.astype(BF16)
    r = f - hi.astype(F32)
    mid = r.astype(BF16)
    lo = (r - mid.astype(F32)).astype(BF16)
    return hi, mid, lo


def _tri_sum(tri, x):
    hi, mid, lo = _split3(x)
    return (jnp.dot(tri, hi, preferred_element_type=F32) + jnp.dot(tri, mid, preferred_element_type=F32)
            + jnp.dot(tri, lo, preferred_element_type=F32))


def _log1p_pos(e):
    return jnp.where(e < 0.01, e * (1.0 - e * (0.5 - e * (1.0 / 3.0))), jnp.log(1.0 + e))


def _fox_prep(qkv, fl, b_row, *, S, D, name):
    T = qkv.shape[0]
    P = D // LANES
    B = T // S
    tt = _tile(S, 256)
    per = S // tt
    eq, ek, ones_q, ones_k = _fox_consts(P)
    q_scale = HEAD_DIM ** -0.5

    def body(q_ref, k_ref, v_ref, fl_ref, b_ref, eq_ref, ek_ref, oq_ref, ok_ref, qx_ref, kvx_ref, carry):
        i = pl.program_id(1)

        @pl.when(i == 0)
        def _():
            carry[...] = jnp.zeros_like(carry)

        z = fl_ref[...] + b_ref[...]
        logf = jnp.minimum(z, 0.0) - _log1p_pos(jnp.exp(-jnp.abs(z)))
        row = lax.broadcasted_iota(jnp.int32, (tt, tt), 0)
        col = lax.broadcasted_iota(jnp.int32, (tt, tt), 1)
        tri = (col <= row).astype(BF16)
        f = _tri_sum(tri, logf) + carry[...]
        carry[...] = f[tt - 1:tt, :]
        parts = jnp.concatenate(_split3(f), axis=1)
        xq = jnp.dot(parts, eq_ref[...], preferred_element_type=F32) + oq_ref[...]
        xk = jnp.dot(parts, ek_ref[...], preferred_element_type=F32) + ok_ref[...]
        for p in range(P):
            c = slice(p * LANES, (p + 1) * LANES)
            qx_ref[:, p * PAIR_Q:p * PAIR_Q + LANES] = (q_ref[:, c].astype(F32) * q_scale).astype(BF16)
            qx_ref[:, p * PAIR_Q + LANES:(p + 1) * PAIR_Q] = xq[:, c].astype(BF16)
            kvx_ref[:, p * PAIR_KV:p * PAIR_KV + LANES] = k_ref[:, c]
            kvx_ref[:, p * PAIR_KV + LANES:p * PAIR_KV + PAIR_Q] = xk[:, c].astype(BF16)
            kvx_ref[:, p * PAIR_KV + PAIR_Q:(p + 1) * PAIR_KV] = v_ref[:, c]

    tok = lambda b, i: (b * per + i, 0)
    const = lambda b, i: (0, 0)
    return pl.pallas_call(
        body, name=name, grid=(B, per),
        in_specs=[pl.BlockSpec((tt, D), lambda b, i: (b * per + i, 0)),
                  pl.BlockSpec((tt, D), lambda b, i: (b * per + i, 1)),
                  pl.BlockSpec((tt, D), lambda b, i: (b * per + i, 2)),
                  pl.BlockSpec((tt, LANES), tok), pl.BlockSpec((1, LANES), const),
                  pl.BlockSpec(eq.shape, const), pl.BlockSpec(ek.shape, const),
                  pl.BlockSpec(ones_q.shape, const), pl.BlockSpec(ones_k.shape, const)],
        out_specs=[pl.BlockSpec((tt, P * PAIR_Q), tok), pl.BlockSpec((tt, P * PAIR_KV), tok)],
        out_shape=[jax.ShapeDtypeStruct((T, P * PAIR_Q), BF16), jax.ShapeDtypeStruct((T, P * PAIR_KV), BF16)],
        scratch_shapes=[pltpu.VMEM((1, LANES), F32)],
        compiler_params=_cparams(("arbitrary", "arbitrary")),
    )(qkv, qkv, qkv, fl, b_row, jnp.asarray(eq, BF16), jnp.asarray(ek, BF16), jnp.asarray(ones_q), jnp.asarray(ones_k))


def _fox_unprep(dqx, dkvx, csum, rsum, fl, b_row, *, S, D, name):
    T = dqx.shape[0]
    P = D // LANES
    B = T // S
    tt = _tile(S, 256)
    per = S // tt
    q_scale = HEAD_DIM ** -0.5

    def body(dq_ref, dkv_ref, cs_ref, rs_ref, fl_ref, b_ref, dqkv_ref, dfl_ref, db_ref, carry):
        b = pl.program_id(0)
        i = pl.program_id(1)

        @pl.when(i == 0)
        def _():
            carry[...] = jnp.zeros_like(carry)

        @pl.when((i == 0) & (b == 0))
        def _():
            db_ref[...] = jnp.zeros_like(db_ref)

        df = rs_ref[...] - cs_ref[...]
        for p in range(P):
            rq = slice(p * LANES, (p + 1) * LANES)
            dqkv_ref[:, rq] = (dq_ref[:, p * PAIR_Q:p * PAIR_Q + LANES].astype(F32) * q_scale).astype(BF16)
            dqkv_ref[:, D + p * LANES:D + (p + 1) * LANES] = dkv_ref[:, p * PAIR_KV:p * PAIR_KV + LANES]
            dqkv_ref[:, 2 * D + p * LANES:2 * D + (p + 1) * LANES] = dkv_ref[:, p * PAIR_KV + PAIR_Q:(p + 1) * PAIR_KV]
        row = lax.broadcasted_iota(jnp.int32, (tt, tt), 0)
        col = lax.broadcasted_iota(jnp.int32, (tt, tt), 1)
        tri = (col >= row).astype(BF16)
        dlogf = _tri_sum(tri, df) + carry[...]
        carry[...] = dlogf[0:1, :]
        z = fl_ref[...] + b_ref[...]
        e = jnp.exp(-jnp.abs(z))
        sig_neg = jnp.where(z >= 0.0, e, 1.0) / (1.0 + e)
        dfl = dlogf * sig_neg
        dfl_ref[...] = dfl.astype(BF16)
        db_ref[...] += jnp.sum(dfl, axis=0, keepdims=True)

    rev = lambda b, i: (b * per + per - 1 - i, 0)
    const = lambda b, i: (0, 0)
    return pl.pallas_call(
        body, name=name, grid=(B, per),
        in_specs=[pl.BlockSpec((tt, P * PAIR_Q), rev), pl.BlockSpec((tt, P * PAIR_KV), rev),
                  pl.BlockSpec((tt, LANES), rev), pl.BlockSpec((tt, LANES), rev), pl.BlockSpec((tt, LANES), rev),
                  pl.BlockSpec((1, LANES), const)],
        out_specs=[pl.BlockSpec((tt, 3 * D), rev), pl.BlockSpec((tt, LANES), rev), pl.BlockSpec((1, LANES), const)],
        out_shape=[jax.ShapeDtypeStruct((T, 3 * D), BF16), jax.ShapeDtypeStruct((T, LANES), BF16),
                   jax.ShapeDtypeStruct((1, LANES), F32)],
        scratch_shapes=[pltpu.VMEM((1, LANES), F32)],
        compiler_params=_cparams(("arbitrary", "arbitrary")),
    )(dqx, dkvx, csum, rsum, fl, b_row)


def _rms(x):
    r = lax.rsqrt(jnp.mean(x * x, axis=-1, keepdims=True) + NORM_EPS)
    return x * r, r


def _mla_mid(lat, gq, gkv, cos_t, sin_s, *, name):
    T, W = lat.shape
    Rq = W - 2 * LANES
    tt = _tile(T, 512)

    def body(l_ref, gq_ref, gkv_ref, c_ref, s_ref, o_ref):
        nq, _ = _rms(l_ref[:, 0:Rq])
        nkv, _ = _rms(l_ref[:, Rq:Rq + LANES])
        o_ref[:, 0:Rq] = (nq * gq_ref[...]).astype(BF16)
        o_ref[:, Rq:Rq + LANES] = (nkv * gkv_ref[...]).astype(BF16)
        o_ref[:, Rq + LANES:W] = _rope128(l_ref[:, Rq + LANES:W], c_ref[...], s_ref[...]).astype(BF16)

    return pl.pallas_call(
        body, name=name, grid=(T // tt,),
        in_specs=[pl.BlockSpec((tt, W), lambda i: (i, 0)), pl.BlockSpec((1, Rq), lambda i: (0, 0)),
                  pl.BlockSpec((1, LANES), lambda i: (0, 0)), pl.BlockSpec((tt, LANES), lambda i: (i, 0)),
                  pl.BlockSpec((tt, LANES), lambda i: (i, 0))],
        out_specs=pl.BlockSpec((tt, W), lambda i: (i, 0)),
        out_shape=jax.ShapeDtypeStruct((T, W), BF16),
        compiler_params=_cparams(("parallel",)),
    )(lat, gq, gkv, cos_t, sin_s)


def _mla_mid_bwd(lat, dcq, dckr, gq, gkv, cos_t, sin_s, *, name):
    T, W = lat.shape
    Rq = W - 2 * LANES
    tt = _tile(T, 512)

    def norm_bwd(x, dy, g):
        n, r = _rms(x)
        dn = dy * g
        return r * (dn - n * jnp.mean(dn * n, axis=-1, keepdims=True)), jnp.sum(dy * n, axis=0, keepdims=True)

    def body(l_ref, dq_ref, dk_ref, gq_ref, gkv_ref, c_ref, s_ref, o_ref, dgq_ref, dgkv_ref):
        i = pl.program_id(0)

        @pl.when(i == 0)
        def _():
            dgq_ref[...] = jnp.zeros_like(dgq_ref)
            dgkv_ref[...] = jnp.zeros_like(dgkv_ref)

        dxq, dgq = norm_bwd(l_ref[:, 0:Rq], dq_ref[...], gq_ref[...])
        dxkv, dgkv = norm_bwd(l_ref[:, Rq:Rq + LANES], dk_ref[:, 0:LANES], gkv_ref[...])
        o_ref[:, 0:Rq] = dxq.astype(BF16)
        o_ref[:, Rq:Rq + LANES] = dxkv.astype(BF16)
        o_ref[:, Rq + LANES:W] = _rope128(dk_ref[:, LANES:2 * LANES], c_ref[...], -s_ref[...]).astype(BF16)
        dgq_ref[...] += dgq
        dgkv_ref[...] += dgkv

    return pl.pallas_call(
        body, name=name, grid=(T // tt,),
        in_specs=[pl.BlockSpec((tt, W), lambda i: (i, 0)), pl.BlockSpec((tt, Rq), lambda i: (i, 0)),
                  pl.BlockSpec((tt, 2 * LANES), lambda i: (i, 0)), pl.BlockSpec((1, Rq), lambda i: (0, 0)),
                  pl.BlockSpec((1, LANES), lambda i: (0, 0)), pl.BlockSpec((tt, LANES), lambda i: (i, 0)),
                  pl.BlockSpec((tt, LANES), lambda i: (i, 0))],
        out_specs=[pl.BlockSpec((tt, W), lambda i: (i, 0)), pl.BlockSpec((1, Rq), lambda i: (0, 0)),
                   pl.BlockSpec((1, LANES), lambda i: (0, 0))],
        out_shape=[jax.ShapeDtypeStruct((T, W), BF16), jax.ShapeDtypeStruct((1, Rq), F32),
                   jax.ShapeDtypeStruct((1, LANES), F32)],
        compiler_params=_cparams(("arbitrary",)),
    )(lat, dcq, dckr, gq, gkv, cos_t, sin_s)


def _uq_to_pairs(w):
    Rq = w.shape[0]
    P = w.shape[1] // (2 * (HEAD_DIM + ROPE_DIM))
    w4 = w.reshape(Rq, P, 2, HEAD_DIM + ROPE_DIM)
    nope = w4[..., :HEAD_DIM].reshape(Rq, P, 2 * HEAD_DIM)
    rope = w4[..., HEAD_DIM:].reshape(Rq, P, 2 * ROPE_DIM)
    pad = jnp.zeros((Rq, P, PAIR_Q - 2 * HEAD_DIM - 2 * ROPE_DIM), w.dtype)
    return jnp.concatenate([nope, rope, pad], axis=-1).reshape(Rq, P * PAIR_Q)


def _uq_from_pairs(g):
    Rq = g.shape[0]
    P = g.shape[1] // PAIR_Q
    g3 = g.reshape(Rq, P, PAIR_Q)
    nope = g3[..., :2 * HEAD_DIM].reshape(Rq, P, 2, HEAD_DIM)
    rope = g3[..., 2 * HEAD_DIM:2 * HEAD_DIM + 2 * ROPE_DIM].reshape(Rq, P, 2, ROPE_DIM)
    return jnp.concatenate([nope, rope], axis=-1).reshape(Rq, P * 2 * (HEAD_DIM + ROPE_DIM))


def _ukv_to_pairs(w):
    P = w.shape[1] // (4 * HEAD_DIM)
    w4 = w.reshape(KV_RANK, P, 2, 2 * HEAD_DIM)
    kn = w4[..., :HEAD_DIM].reshape(KV_RANK, P, 2 * HEAD_DIM)
    vv = w4[..., HEAD_DIM:].reshape(KV_RANK, P, 2 * HEAD_DIM)
    top = jnp.concatenate([kn, jnp.zeros((KV_RANK, P, LANES), w.dtype), vv], axis=-1)
    place = np.zeros((LANES, P, PAIR_KV), np.float32)
    for r in range(ROPE_DIM):
        place[r, :, LANES + r] = 1.0
        place[r, :, LANES + ROPE_DIM + r] = 1.0
    return jnp.concatenate([top, jnp.asarray(place, w.dtype)], axis=0).reshape(KV_RANK + LANES, P * PAIR_KV)


def _ukv_from_pairs(g):
    P = g.shape[1] // PAIR_KV
    g3 = g[:KV_RANK].reshape(KV_RANK, P, PAIR_KV)
    kn = g3[..., :2 * HEAD_DIM].reshape(KV_RANK, P, 2, HEAD_DIM)
    vv = g3[..., PAIR_Q:].reshape(KV_RANK, P, 2, HEAD_DIM)
    return jnp.concatenate([kn, vv], axis=-1).reshape(KV_RANK, P * 4 * HEAD_DIM)


def _mlp_fwd(h2, w, i, x1, gate, *, S):
    p, u = _mm(h2, w["mlp_w1"], "nn", name=f"mlp_up_{i}", b_layer=i, out_dtypes=(BF16, BF16),
               epilogue=lambda acc: (acc, jnp.square(jnp.maximum(acc, 0.0))))
    x2, z = _mm(u, w["mlp_w2"], "nn", name=f"mlp_down_{i}", b_layer=i, out_dtypes=(F32, F32), extras=(x1,),
                rowvecs=(gate,), seq=S, epilogue=lambda acc, xr, g: (xr + g * acc, acc))
    return x2, (p, u, z)


STACKED_GRADS = ("fox_out", "mla_down", "mla_uq", "mla_ukv", "mla_out", "mlp_w1", "mlp_w2")


def _local_step(x, target, pos_f, inv_freq_row, sign_row, mod, w, slots, *, S, after_layer0=None, before_layer0=None):
    T, D = x.shape
    L = mod.shape[0]
    L2 = len(w["fox_out"])
    n_split = w["mlp_w1"][0][0].shape[1]
    cos_t, sin_s = _rope_tables(pos_f, inv_freq_row, sign_row)
    saved = []
    for i in range(L):
        j = i // 2
        sh_m, sc_m, g_m, sh_f, sc_f, g_f = (mod[i, s] for s in range(6))
        h = _norm_mod(x, w["norm_mix_g"][i], sc_m, sh_m, S=S, name=f"norm_mix_{i}")
        if i % 2 == 0:
            qkv = _mm(h, w["fox_qkv"], "nn", name=f"fox_qkv_{i}", b_layer=j, out_dtypes=(BF16,))
            fl = _mm(h, w["fox_f"], "nn", name=f"fox_f_{i}", b_layer=j)
            qx, kvx = _fox_prep(qkv, fl, w["fox_b"][j], S=S, D=D, name=f"fox_prep_{i}")
            o, lse = _attn_fwd(qx, kvx, S=S, scale=1.0, ew=FOX_EXTRA, name=f"fox_attn_{i}")
            mix = (qx, kvx, o, lse, fl)
            w_out = w["fox_out"]
        else:
            lat = _mm(h, w["mla_down"], "nn", name=f"mla_down_{i}", b_layer=j)
            Rq = lat.shape[1] - 2 * LANES
            cqr = _mla_mid(lat, w["mla_gq"][j], w["mla_gkv"][j], cos_t, sin_s, name=f"mla_mid_{i}")
            qx = _mm(cqr, w["mla_uq"], "nn", name=f"mla_uq_{i}", b_layer=j, out_dtypes=(BF16,), a_sz=Rq, tk=Rq,
                     tables=(cos_t, sin_s), epilogue=lambda acc, c, s: (_rope_pairs(acc, c, s, 1.0),))
            kvx = _mm(cqr, w["mla_ukv"], "nn", name=f"mla_ukv_{i}", b_layer=j, out_dtypes=(BF16,), a_off=Rq,
                      a_sz=2 * LANES, tk=2 * LANES, tn=PAIR_KV)
            o, lse = _attn_fwd(qx, kvx, S=S, scale=(HEAD_DIM + ROPE_DIM) ** -0.5, ew=ROPE_DIM, name=f"mla_attn_{i}")
            mix = (qx, kvx, o, lse, lat, cqr)
            w_out = w["mla_out"]
        x1, y = _mm(o, w_out, "nn", name=f"mix_out_{i}", b_layer=j, out_dtypes=(F32, F32), extras=(x,),
                    rowvecs=(g_m,), seq=S, epilogue=lambda acc, xr, g: (xr + g * acc, acc))
        h2 = _norm_mod(x1, w["norm_mlp_g"][i], sc_f, sh_f, S=S, name=f"norm_mlp_{i}")
        x2, mlp = _mlp_fwd(h2, w, i, x1, g_f, S=S)
        saved.append((x, h, mix, y, x1, h2, mlp))
        x = x2
        if i == 0 and after_layer0 is not None:
            w = after_layer0(x, w)

    dx, dg_final, loss = _final_loss(x, target, w["final_norm_g"])

    grads = {k: [None] * len(w[k]) for k in ("norm_mix_g", "norm_mlp_g", "fox_b", "mla_gq", "mla_gkv")}
    grads.update({k: [None] * L2 for k in ("fox_qkv", "fox_f")})
    grads.update({k: {} for k in STACKED_GRADS})
    grads["final_norm_g"] = dg_final

    def stacked(key, layer, _, a, b, **kw):
        group, idx, count = slots[(key, layer)]
        grads[key][group] = _mm(a.T, b, "nn", out_stack=(grads[key].get(group), idx, count), **kw)

    dmod = [None] * L
    for i in reversed(range(L)):
        j = i // 2
        x0, h, mix, y, x1, h2, (p, u, z) = saved[i]
        sh_m, sc_m, g_m, sh_f, sc_f, g_f = (mod[i, s] for s in range(6))
        if i == 0 and before_layer0 is not None:
            g_f = g_f + before_layer0(grads)[0, 0]
        dz, dg_f = _gate_bwd(dx, z, g_f, S=S, name=f"gate_mlp_bwd_{i}")
        stacked("mlp_w2", i, L, u, dz, name=f"mlp_w2_grad_{i}")
        dp = _mm(dz, w["mlp_w2"], "nt", name=f"mlp_down_bwd_{i}", b_layer=i, out_dtypes=(BF16,), extras=(p,),
                 epilogue=lambda acc, pv: (acc * (2.0 * jnp.maximum(pv.astype(F32), 0.0)),))
        stacked("mlp_w1", i, L, h2, dp, name=f"mlp_w1_grad_{i}", out_split=n_split)
        dh2 = _mm(dp, w["mlp_w1"], "nt", name=f"mlp_up_bwd_{i}", b_layer=i)
        dx1, dsh_f, dsc_f, dgn = _norm_mod_bwd(x1, dh2, dx, w["norm_mlp_g"][i], sc_f, S=S, name=f"norm_mlp_bwd_{i}")
        grads["norm_mlp_g"][i] = dgn
        dy, dg_m = _gate_bwd(dx1, y, g_m, S=S, name=f"gate_mix_bwd_{i}")
        if i % 2 == 0:
            qx, kvx, o, lse, fl = mix
            stacked("fox_out", j, L2, o, dy, name=f"fox_out_grad_{i}")
            do = _mm(dy, w["fox_out"], "nt", name=f"fox_out_bwd_{i}", b_layer=j, out_dtypes=(BF16,))
            dqx, dkvx, csum, rsum = _attn_bwd(qx, kvx, o, lse, do, S=S, scale=1.0, ew=FOX_EXTRA,
                                              name=f"fox_attn_bwd_{i}", bias_grad=True)
            n_heads = D // HEAD_DIM
            csum = jnp.pad(csum.reshape(T, n_heads, HEAD_DIM)[:, :, 0], ((0, 0), (0, LANES - n_heads)))
            rsum = jnp.transpose(rsum[:, :, :2, :], (0, 3, 1, 2)).reshape(T, n_heads)
            rsum = jnp.pad(rsum, ((0, 0), (0, LANES - n_heads)))
            dqkv, dfl, db = _fox_unprep(dqx, dkvx, csum, rsum, fl, w["fox_b"][j], S=S, D=D, name=f"fox_unprep_{i}")
            grads["fox_b"][j] = db
            h_t = h.T
            grads["fox_qkv"][j] = _mm(h_t, dqkv, "nn", name=f"fox_qkv_grad_{i}")
            grads["fox_f"][j] = _mm(h_t, dfl, "nn", name=f"fox_f_grad_{i}")
            dh_f = _mm(dfl, w["fox_f"], "nt", name=f"fox_f_bwd_{i}", b_layer=j)
            dh = _mm(dqkv, w["fox_qkv"], "nt", name=f"fox_qkv_bwd_{i}", b_layer=j, extras=(dh_f,),
                     epilogue=lambda acc, e: (acc + e,))
        else:
            qx, kvx, o, lse, lat, cqr = mix
            Rq = lat.shape[1] - 2 * LANES
            stacked("mla_out", j, L2, o, dy, name=f"mla_out_grad_{i}")
            do = _mm(dy, w["mla_out"], "nt", name=f"mla_out_bwd_{i}", b_layer=j, out_dtypes=(BF16,))
            dqx, dkvx = _attn_bwd(qx, kvx, o, lse, do, S=S, scale=(HEAD_DIM + ROPE_DIM) ** -0.5, ew=ROPE_DIM,
                                  name=f"mla_attn_bwd_{i}")
            dqpre = _unrope(dqx, cos_t, sin_s)
            stacked("mla_uq", j, L2, cqr[:, :Rq], dqpre, name=f"mla_uq_grad_{i}", out_split=n_split)
            stacked("mla_ukv", j, L2, cqr[:, Rq:], dkvx, name=f"mla_ukv_grad_{i}", tn=PAIR_KV, out_split=n_split)
            dcq = _mm(dqpre, w["mla_uq"], "nt", name=f"mla_uq_bwd_{i}", b_layer=j)
            dckr = _mm(dkvx, w["mla_ukv"], "nt", name=f"mla_ukv_bwd_{i}", b_layer=j, tk=PAIR_KV * 2)
            dlat, dgq, dgkv = _mla_mid_bwd(lat, dcq, dckr, w["mla_gq"][j], w["mla_gkv"][j], cos_t, sin_s,
                                           name=f"mla_mid_bwd_{i}")
            grads["mla_gq"][j] = dgq
            grads["mla_gkv"][j] = dgkv
            stacked("mla_down", j, L2, h, dlat, name=f"mla_down_grad_{i}")
            dh = _mm(dlat, w["mla_down"], "nt", name=f"mla_down_bwd_{i}", b_layer=j)
        dx, dsh_m, dsc_m, dgn = _norm_mod_bwd(x0, dh, dx1, w["norm_mix_g"][i], sc_m, S=S, name=f"norm_mix_bwd_{i}")
        grads["norm_mix_g"][i] = dgn
        dmod[i] = jnp.stack([dsh_m, dsc_m, dg_m, dsh_f, dsc_f, dg_f])
    return loss, dx, jnp.stack(dmod), grads


GATHERED = ("fox_in", "fox_out", "mla_down", "mla_uq", "mla_ukv", "mla_out", "mlp_w1", "mlp_w2")
ROW_SHARDED = ("fox_out", "mla_down", "mla_out", "mlp_w2")


def _shard_layouts(wts):
    dkv = wts["mla_w_dkv"]
    dkv = jnp.pad(dkv, ((0, 0), (0, 0), (0, 2 * LANES - dkv.shape[2])))
    return {
        "fox_in": _pad_lanes(wts["fox_w_in"].astype(BF16)),
        "fox_out": wts["fox_w_out"].astype(BF16),
        "mla_down": jnp.concatenate([wts["mla_w_dq"], dkv], axis=2).astype(BF16),
        "mla_uq": jax.vmap(_uq_to_pairs)(wts["mla_w_uq"].astype(BF16)),
        "mla_ukv": jax.vmap(_ukv_to_pairs)(wts["mla_w_ukv"].astype(BF16)),
        "mla_out": wts["mla_w_out"].astype(BF16),
        "mlp_w1": wts["mlp_w1"].astype(BF16),
        "mlp_w2": wts["mlp_w2"].astype(BF16),
    }


def _small_layouts(small):
    return {
        "fox_b": [jnp.pad(b, (0, LANES - b.shape[0]))[None, :] for b in small["fox_b_f"]],
        "mla_gq": [g[None, :] for g in small["mla_q_norm_g"]],
        "mla_gkv": [g[None, :] for g in small["mla_kv_norm_g"]],
        "norm_mix_g": [g[None, :] for g in small["norm_mix_g"]],
        "norm_mlp_g": [g[None, :] for g in small["norm_mlp_g"]],
        "final_norm_g": small["final_norm_g"][None, :],
    }


def _comm_groups(L, L2):
    first = [("fox_in", 0, 1), ("fox_out", 0, 1), ("mlp_w1", 0, 1), ("mlp_w2", 0, 1)]
    rest = [("fox_in", 1, L2 - 1), ("fox_out", 1, L2 - 1), ("mla_down", 0, L2), ("mla_uq", 0, L2),
            ("mla_ukv", 0, L2), ("mla_out", 0, L2), ("mlp_w1", 1, L - 1), ("mlp_w2", 1, L - 1)]
    return {"first": first, "rest": [e for e in rest if e[2] > 0]}


def _layer_slots(groups):
    return {(n, s + l): (g, l, cnt) for g, entries in groups.items() for n, s, cnt in entries for l in range(cnt)}


def _pad_lanes(a):
    cols = a.shape[-1]
    return jnp.pad(a, [(0, 0)] * (a.ndim - 1) + [(0, -cols % LANES)])


def _weight_views(name, gathered, D, n_fox_heads):
    n, ns, rows, cols = gathered.shape
    if name == "fox_in":
        true_cols = (3 * D + n_fox_heads) // ns
        fox = jnp.concatenate([gathered[:, k, :, :true_cols] for k in range(ns)], axis=-1)
        return {"fox_qkv": fox[:, :, :3 * D], "fox_f": _pad_lanes(fox[:, :, 3 * D:])}
    if name in ROW_SHARDED:
        return {name: gathered.reshape(n, ns * rows, cols)}
    return {name: gathered}


def _grad_pieces(name, g, qkv_f, n_fox_heads, ns):
    if name == "fox_in":
        fox = jnp.stack([jnp.concatenate([a, b[:, :n_fox_heads]], axis=1) for a, b in qkv_f])
        cols = fox.shape[2] // ns
        return jnp.stack([_pad_lanes(fox[:, :, k * cols:(k + 1) * cols]) for k in range(ns)], axis=1)
    if name in ROW_SHARDED:
        return g.reshape(g.shape[0], ns, g.shape[1] // ns, g.shape[2])
    return g


def _small_grads(g, n_fox_heads):
    return {
        "norm_mix_g": jnp.concatenate(g["norm_mix_g"], axis=0),
        "norm_mlp_g": jnp.concatenate(g["norm_mlp_g"], axis=0),
        "final_norm_g": g["final_norm_g"][0],
        "fox_b_f": jnp.concatenate(g["fox_b"], axis=0)[:, :n_fox_heads],
        "mla_q_norm_g": jnp.concatenate(g["mla_gq"], axis=0),
        "mla_kv_norm_g": jnp.concatenate(g["mla_gkv"], axis=0),
    }


def _silu(c):
    return c * (1.0 / (1.0 + jnp.exp(-c)))


def _ada_fwd(c_all, ada_w, ada_b_cols):
    L, D, C = ada_w.shape
    Bg = c_all.shape[0]
    tc = _tile(C, 512)

    def body(c_ref, w_ref, b_ref, o_ref):
        ca = _silu(c_ref[...]).astype(BF16)
        o_ref[...] = jnp.dot(ca, w_ref[...].astype(BF16), preferred_element_type=F32) + b_ref[...]

    return pl.pallas_call(
        body, name="ada_fwd", grid=(L, C // tc),
        in_specs=[pl.BlockSpec((Bg, D), lambda l, j: (0, 0)), pl.BlockSpec((None, D, tc), lambda l, j: (l, 0, j)),
                  pl.BlockSpec((None, 1, tc), lambda l, j: (l, 0, j))],
        out_specs=pl.BlockSpec((None, Bg, tc), lambda l, j: (l, 0, j)),
        out_shape=jax.ShapeDtypeStruct((L, Bg, C), F32),
        compiler_params=_cparams(("parallel", "parallel")),
    )(c_all, ada_w, ada_b_cols)


def _ada_bwd(c_all, dmod_cols):
    L, Bg, C = dmod_cols.shape
    D = c_all.shape[1]
    tc = _tile(C, 512)

    def body(c_ref, d_ref, o_ref):
        ca = _silu(c_ref[...]).astype(BF16)
        o_ref[...] = _dot_tn(ca, d_ref[...].astype(BF16))

    return pl.pallas_call(
        body, name="ada_bwd", grid=(L, C // tc),
        in_specs=[pl.BlockSpec((Bg, D), lambda l, j: (0, 0)), pl.BlockSpec((None, Bg, tc), lambda l, j: (l, 0, j))],
        out_specs=pl.BlockSpec((None, D, tc), lambda l, j: (l, 0, j)),
        out_shape=jax.ShapeDtypeStruct((L, D, C), F32),
        compiler_params=_cparams(("parallel", "parallel")),
    )(c_all, dmod_cols)


def _adamw_update(w, gv, m, v):
    mn = ADAM_B1 * m + (1.0 - ADAM_B1) * gv
    vn = ADAM_B2 * v + (1.0 - ADAM_B2) * jnp.square(gv)
    m_hat = mn / (1.0 - ADAM_B1 ** ADAM_STEP)
    v_hat = vn / (1.0 - ADAM_B2 ** ADAM_STEP)
    return -ADAM_LR * (m_hat / (jnp.sqrt(v_hat) + ADAM_EPS) + ADAM_WD * w), mn, vn


def _adamw(w, g, m, v, *, name):
    shape = w.shape
    C = shape[-1]
    R = int(np.prod(shape[:-1])) if len(shape) > 1 else 1
    w2, g2, m2, v2 = (a.reshape(R, C) for a in (w, g, m, v))
    tr = _row_tile(R, C)

    def body(w_ref, g_ref, m_ref, v_ref, d_ref, nm_ref, nv_ref):
        d_ref[...], nm_ref[...], nv_ref[...] = _adamw_update(w_ref[...], g_ref[...], m_ref[...], v_ref[...])

    spec = pl.BlockSpec((tr, C), lambda i: (i, 0))
    out = pl.pallas_call(
        body, name=name, grid=(R // tr,), in_specs=[spec] * 4, out_specs=[spec] * 3,
        out_shape=[jax.ShapeDtypeStruct((R, C), F32)] * 3, compiler_params=_cparams(("parallel",)),
    )(w2, g2, m2, v2)
    return tuple(a.reshape(shape) for a in out)


def _adamw_halves(w, g_own, g_peer, m, v, c_idx, *, name):
    L, rows, C = w.shape
    R = rows // 2
    tr = _row_tile(R, C)

    def body(c_ref, w_ref, go_ref, gp_ref, m_ref, v_ref, g_ref, d_ref, nm_ref, nv_ref):
        gv = jnp.where(pl.program_id(1) == c_ref[0], go_ref[...], gp_ref[...])
        g_ref[...] = gv
        d_ref[...], nm_ref[...], nv_ref[...] = _adamw_update(w_ref[...], gv, m_ref[...], v_ref[...])

    full = pl.BlockSpec((None, None, tr, C), lambda l, hh, i, c_ref: (l, hh, i, 0))
    half = pl.BlockSpec((None, tr, C), lambda l, hh, i, c_ref: (l, i, 0))
    grid_spec = pltpu.PrefetchScalarGridSpec(
        num_scalar_prefetch=1, grid=(L, 2, R // tr), in_specs=[full, half, half, full, full], out_specs=[full] * 4)
    split = lambda a: a.reshape(L, 2, R, C)
    out = pl.pallas_call(
        body, name=name, grid_spec=grid_spec, out_shape=[jax.ShapeDtypeStruct((L, 2, R, C), F32)] * 4,
        compiler_params=_cparams(("parallel", "parallel", "parallel")),
    )(c_idx, split(w), g_own, g_peer, split(m), split(v))
    return tuple(a.reshape(w.shape) for a in out)


def _sum_gathered(dm8, sm8):
    n_dev, Bl, R, D = dm8.shape
    Rs = sm8.shape[1]

    def body(dm_ref, sm_ref, ob_ref, os_ref):
        acc_b = jnp.zeros((R, D), F32)
        acc_s = jnp.zeros((Rs, D), F32)
        for d in range(n_dev):
            for b in range(Bl):
                acc_b = acc_b + dm_ref[d, b]
            acc_s = acc_s + sm_ref[d]
        ob_ref[...] = acc_b
        os_ref[...] = acc_s

    return pl.pallas_call(
        body, name="sum_gathered",
        out_shape=[jax.ShapeDtypeStruct((R, D), F32), jax.ShapeDtypeStruct((Rs, D), F32)],
        compiler_params=_cparams(None),
    )(dm8, sm8)


N_DEV = 8
N_CHIP = 4
ANY = pl.BlockSpec(memory_space=pl.ANY)
HBM = pl.BlockSpec(memory_space=pltpu.HBM)
SEM = pl.BlockSpec(memory_space=pltpu.SEMAPHORE)
DATAFLOW = pltpu.SideEffectType.DATAFLOW_SIDE_EFFECTING


def _mesh_pos():
    return lax.axis_index("x"), lax.axis_index("y"), lax.axis_index("c")


def _all_gather8(block, *, name, in_vmem):
    R, W = block.shape

    def body(x_ref, out_ref, send_sems, recv_sems, local_sem):
        x, y, c = _mesh_pos()
        me, sibling = (x, y, c), (x, y, 1 - c)
        chips = [(1 - x, y), (x, 1 - y), (1 - x, 1 - y)]

        def slot(px, py, pc):
            return out_ref.at[4 * px + 2 * py + pc]

        def copy(k, blk, to, src=None):
            return pltpu.make_async_remote_copy(
                src_ref=slot(*blk) if src is None else src, dst_ref=slot(*blk),
                send_sem=send_sems.at[k], recv_sem=recv_sems.at[k], device_id=to, device_id_type=MESH_ID)

        mine = pltpu.make_async_copy(x_ref, slot(*me), local_sem)
        mine.start()
        first = [copy(0, me, sibling, src=x_ref)]
        first += [copy(1 + j, me, (*chip, c), src=x_ref) for j, chip in enumerate(chips)]
        for cp in first:
            cp.start()
        passed = [copy(4 + j, (*chip, c), sibling) for j, chip in enumerate(chips)]
        for j, chip in enumerate(chips):
            copy(1 + j, (*chip, c), me).wait_recv()
            passed[j].start()
        copy(0, sibling, me).wait_recv()
        for j, chip in enumerate(chips):
            copy(4 + j, (*chip, 1 - c), me).wait_recv()
        for cp in first + passed:
            cp.wait_send()
        mine.wait()

    space = pl.BlockSpec(memory_space=pltpu.VMEM) if in_vmem else ANY
    return pl.pallas_call(
        body, name=name, out_shape=jax.ShapeDtypeStruct((N_DEV, R, W), block.dtype),
        in_specs=[space], out_specs=space,
        scratch_shapes=[pltpu.SemaphoreType.DMA((7,)), pltpu.SemaphoreType.DMA((7,)), pltpu.SemaphoreType.DMA],
        compiler_params=pltpu.CompilerParams(vmem_limit_bytes=VMEM_LIMIT_V7X),
    )(block)


def _comm_call(body, arrays, out_shapes, n_sems, *, name):
    return pl.pallas_call(
        body, name=name, out_shape=out_shapes, in_specs=[ANY] * len(arrays), out_specs=[ANY] * len(out_shapes),
        scratch_shapes=[pltpu.SemaphoreType.DMA((n_sems,)), pltpu.SemaphoreType.DMA((n_sems,)),
                        pltpu.SemaphoreType.DMA((len(arrays),))],
    )(*arrays)


def _gather_weights(shards, *, name):
    n = len(shards)

    def body(*refs):
        xs, outs = refs[:n], refs[n:2 * n]
        send_sems, recv_sems, local_sems = refs[2 * n:]
        x, y, c = _mesh_pos()
        me, sibling = (x, y, c), (x, y, 1 - c)
        chips = [(1 - x, y), (x, 1 - y), (1 - x, 1 - y)]
        waits = []
        for i in range(n):
            nl = shards[i].shape[0]
            own = xs[i].at[pl.ds(0, nl), c]

            def slot(px, py, pc, i=i, nl=nl):
                return outs[i].at[pl.ds(0, nl), 2 * px + py, pc]

            def copy(k, blk, to, src=None, i=i, slot=slot):
                return pltpu.make_async_remote_copy(
                    src_ref=slot(*blk) if src is None else src, dst_ref=slot(*blk),
                    send_sem=send_sems.at[7 * i + k], recv_sem=recv_sems.at[7 * i + k], device_id=to,
                    device_id_type=MESH_ID)

            mine = pltpu.make_async_copy(own, slot(*me), local_sems.at[i])
            mine.start()
            first = [copy(0, me, sibling, src=own)]
            first += [copy(1 + j, me, (*chip, c), src=own) for j, chip in enumerate(chips)]
            for cp in first:
                cp.start()
            waits.append((copy, mine, first))
        for copy, mine, first in waits:
            passed = [copy(4 + j, (*chip, c), sibling) for j, chip in enumerate(chips)]
            for j, chip in enumerate(chips):
                copy(1 + j, (*chip, c), me).wait_recv()
                passed[j].start()
            copy(0, sibling, me).wait_recv()
            for j, chip in enumerate(chips):
                copy(4 + j, (*chip, 1 - c), me).wait_recv()
            for cp in first + passed:
                cp.wait_send()
            mine.wait()

    out_shapes = [jax.ShapeDtypeStruct((s.shape[0], N_CHIP) + s.shape[1:], s.dtype) for s in shards]
    return _comm_call(body, shards, out_shapes, 7 * n, name=name)


def _place_own(shard, chip_idx, c_idx, *, name):
    n, _, rows, cols = shard.shape
    tr = _row_tile(rows, cols)

    def body(k_ref, c_ref, x_ref, o_ref):
        o_ref[...] = x_ref[...]

    grid_spec = pltpu.PrefetchScalarGridSpec(
        num_scalar_prefetch=2, grid=(n, rows // tr),
        in_specs=[pl.BlockSpec((None, None, tr, cols), lambda l, i, k_ref, c_ref: (l, c_ref[0], i, 0))],
        out_specs=pl.BlockSpec((None, None, None, tr, cols), lambda l, i, k_ref, c_ref: (l, k_ref[0], c_ref[0], i, 0)))
    return pl.pallas_call(
        body, name=name, grid_spec=grid_spec,
        out_shape=jax.ShapeDtypeStruct((n, N_CHIP, 2, rows, cols), shard.dtype),
        compiler_params=_cparams(("parallel", "parallel")),
    )(chip_idx, c_idx, shard)


def _gather_copies(x_refs, land_refs, send_sems, recv_sems):
    x, y, c = _mesh_pos()
    k_me = 2 * x + y
    targets = [(x, y, 1 - c), (1 - x, y, c), (x, 1 - y, c), (1 - x, 1 - y, c)]
    copies = []
    for i, (x_ref, land_ref) in enumerate(zip(x_refs, land_refs)):
        nl = x_ref.shape[0]
        for j, to in enumerate(targets):
            copies.append(pltpu.make_async_remote_copy(
                src_ref=x_ref.at[pl.ds(0, nl), c], dst_ref=land_ref.at[pl.ds(0, nl), k_me, c],
                send_sem=send_sems.at[4 * i + j], recv_sem=recv_sems.at[4 * i + j], device_id=to,
                device_id_type=MESH_ID))
    return copies


def _split_start(copies_fn, srcs, lands, after, *, name, sems_per_array):
    n = len(srcs)

    def body(*refs):
        send_sems, recv_sems = refs[2 * n + 1], refs[2 * n + 2]
        for cp in copies_fn(refs[:n], refs[n:2 * n], send_sems, recv_sems):
            cp.start()
        refs[-1][...] = jnp.zeros_like(refs[-1])

    operands = [pltpu.with_memory_space_constraint(a, pltpu.HBM) for a in list(srcs) + list(lands)]
    n_sems = sems_per_array * n
    out_shape = ([pltpu.SemaphoreType.DMA((n_sems,)), pltpu.SemaphoreType.DMA((n_sems,))]
                 + [pltpu.HBM(a.shape, a.dtype) for a in operands] + [jax.ShapeDtypeStruct((8, LANES), F32)])
    res = pl.pallas_call(
        body, name=name, out_shape=out_shape, in_specs=[HBM] * (2 * n) + [ANY],
        out_specs=[SEM, SEM] + [HBM] * (2 * n) + [pl.BlockSpec(memory_space=pltpu.VMEM)],
        input_output_aliases={i: 2 + i for i in range(2 * n)},
        compiler_params=pltpu.CompilerParams(has_side_effects=DATAFLOW),
    )(*operands, after)
    return res[0], res[1], list(res[2:2 + n]), list(res[2 + n:2 + 2 * n]), res[-1]


def _split_wait(copies_fn, send_sems, recv_sems, srcs, lands, after, *, name):
    n = len(srcs)

    def body(*refs):
        for cp in copies_fn(refs[:n], refs[n:2 * n], refs[2 * n], refs[2 * n + 1]):
            cp.wait_send()
            cp.wait_recv()

    res = pl.pallas_call(
        body, name=name, out_shape=[pltpu.HBM(a.shape, a.dtype) for a in list(srcs) + list(lands)],
        in_specs=[HBM] * (2 * n) + [SEM, SEM, ANY], out_specs=[HBM] * (2 * n),
        input_output_aliases={i: i for i in range(2 * n)},
        compiler_params=pltpu.CompilerParams(has_side_effects=DATAFLOW),
    )(*srcs, *lands, send_sems, recv_sems, after)
    return list(res[:n]), list(res[n:])


def _gather_forward(lands, *, name):
    n = len(lands)

    def body(*refs):
        xs = refs[:n]
        send_sems, recv_sems, _ = refs[2 * n:]
        x, y, c = _mesh_pos()
        chips = [(1 - x, y), (x, 1 - y), (1 - x, 1 - y)]
        copies = []
        for i in range(n):
            nl = lands[i].shape[0]
            for j, (cx, cy) in enumerate(chips):
                here = xs[i].at[pl.ds(0, nl), 2 * cx + cy, c]
                cp = pltpu.make_async_remote_copy(
                    src_ref=here, dst_ref=here, send_sem=send_sems.at[3 * i + j], recv_sem=recv_sems.at[3 * i + j],
                    device_id=(x, y, 1 - c), device_id_type=MESH_ID)
                cp.start()
                copies.append(cp)
        for cp in copies:
            cp.wait()

    return pl.pallas_call(
        body, name=name, out_shape=[jax.ShapeDtypeStruct(a.shape, a.dtype) for a in lands],
        in_specs=[ANY] * n, out_specs=[ANY] * n, input_output_aliases={i: i for i in range(n)},
        scratch_shapes=[pltpu.SemaphoreType.DMA((3 * n,)), pltpu.SemaphoreType.DMA((3 * n,)),
                        pltpu.SemaphoreType.DMA((1,))],
    )(*lands)


def _pair_exchange(gs, *, name):
    n = len(gs)

    def body(*refs):
        xs, outs = refs[:n], refs[n:2 * n]
        send_sems, recv_sems, _ = refs[2 * n:]
        x, y, c = _mesh_pos()
        copies = []
        for i in range(n):
            nl, ns = gs[i].shape[:2]
            cp = pltpu.make_async_remote_copy(
                src_ref=xs[i].at[pl.ds(0, nl), pl.ds(0, ns), 1 - c], dst_ref=outs[i], send_sem=send_sems.at[i],
                recv_sem=recv_sems.at[i], device_id=(x, y, 1 - c), device_id_type=MESH_ID)
            cp.start()
            copies.append(cp)
        for cp in copies:
            cp.wait()

    out_shapes = [jax.ShapeDtypeStruct(g.shape[:2] + g.shape[3:], g.dtype) for g in gs]
    return _comm_call(body, gs, out_shapes, n, name=name)


def _chip_copies(p_refs, land_refs, send_sems, recv_sems):
    x, y, c = _mesh_pos()
    k_me = 2 * x + y
    chips = [(1 - x, y), (x, 1 - y), (1 - x, 1 - y)]
    copies = []
    for i, (p_ref, land_ref) in enumerate(zip(p_refs, land_refs)):
        nl = p_ref.shape[0]
        for j, (cx, cy) in enumerate(chips):
            copies.append(pltpu.make_async_remote_copy(
                src_ref=p_ref.at[pl.ds(0, nl), 2 * cx + cy], dst_ref=land_ref.at[k_me],
                send_sem=send_sems.at[3 * i + j], recv_sem=recv_sems.at[3 * i + j],
                device_id=(cx, cy, c), device_id_type=MESH_ID))
    return copies


def _chip_landing(ps):
    return [lax.empty((p.shape[1], p.shape[0]) + p.shape[2:], p.dtype) for p in ps]


def _chip_exchange(ps, *, name):
    n = len(ps)

    def body(*refs):
        send_sems, recv_sems, _ = refs[2 * n:]
        copies = _chip_copies(refs[:n], refs[n:2 * n], send_sems, recv_sems)
        for cp in copies:
            cp.start()
        for cp in copies:
            cp.wait()

    out_shapes = [jax.ShapeDtypeStruct((p.shape[1], p.shape[0]) + p.shape[2:], p.dtype) for p in ps]
    return _comm_call(body, ps, out_shapes, 3 * n, name=name)


def _pair_swap(ss, *, name):
    n = len(ss)

    def body(*refs):
        xs, outs = refs[:n], refs[n:2 * n]
        send_sems, recv_sems, _ = refs[2 * n:]
        x, y, c = _mesh_pos()
        copies = []
        for i in range(n):
            cp = pltpu.make_async_remote_copy(src_ref=xs[i], dst_ref=outs[i], send_sem=send_sems.at[i],
                                              recv_sem=recv_sems.at[i], device_id=(x, y, 1 - c),
                                              device_id_type=MESH_ID)
            cp.start()
            copies.append(cp)
        for cp in copies:
            cp.wait()

    out_shapes = [jax.ShapeDtypeStruct(s.shape, s.dtype) for s in ss]
    return _comm_call(body, ss, out_shapes, n, name=name)


def _row_tile(rows, cols):
    tr = rows
    while tr * cols > 256 * 1024 and tr % 16 == 0:
        tr //= 2
    return tr


def _pair_add(g, recv, c_idx, *, name):
    n, ns, _, rows, W = g.shape
    tr = _row_tile(rows, W)

    def body(c_ref, g_ref, r_ref, o_ref):
        o_ref[...] = (g_ref[...] + r_ref[...]).astype(BF16)

    piece = pl.BlockSpec((None, tr, W), lambda p, i, c_ref: (p, i, 0))
    grid_spec = pltpu.PrefetchScalarGridSpec(
        num_scalar_prefetch=1, grid=(n * ns, rows // tr),
        in_specs=[pl.BlockSpec((None, None, tr, W), lambda p, i, c_ref: (p, c_ref[0], i, 0)), piece],
        out_specs=piece)
    out = pl.pallas_call(
        body, name=name, grid_spec=grid_spec, out_shape=jax.ShapeDtypeStruct((n * ns, rows, W), BF16),
        compiler_params=_cparams(("parallel", "parallel")),
    )(c_idx, g.reshape(n * ns, 2, rows, W), recv.reshape(n * ns, rows, W))
    return out.reshape(n, ns, rows, W)


def _sum_pieces(land, own, chip_idx, *, name):
    n, nl, A, W = land.shape
    tr = _row_tile(A, W)

    def body(k_ref, l_ref, o_ref, out_ref):
        acc = jnp.zeros(out_ref.shape, F32)
        for k in range(n):
            acc = acc + jnp.where(k == k_ref[0], o_ref[...], l_ref[k]).astype(F32)
        out_ref[...] = acc

    grid_spec = pltpu.PrefetchScalarGridSpec(
        num_scalar_prefetch=1, grid=(nl, A // tr),
        in_specs=[pl.BlockSpec((n, None, tr, W), lambda l, i, k_ref: (0, l, i, 0)),
                  pl.BlockSpec((None, None, tr, W), lambda l, i, k_ref: (l, k_ref[0], i, 0))],
        out_specs=pl.BlockSpec((None, tr, W), lambda l, i, k_ref: (l, i, 0)))
    return pl.pallas_call(
        body, name=name, grid_spec=grid_spec, out_shape=jax.ShapeDtypeStruct((nl, A, W), F32),
        compiler_params=_cparams(("parallel", "parallel")),
    )(chip_idx, land, own)


SMALL = ("norm_mix_g", "norm_mlp_g", "final_norm_g", "fox_b_f", "mla_q_norm_g", "mla_kv_norm_g")
WEIGHT_ORDER = ("ada_w", "ada_b", "norm_mix_g", "norm_mlp_g", "fox_w_in", "fox_b_f", "fox_w_out", "mla_w_dq",
                "mla_q_norm_g", "mla_w_uq", "mla_w_dkv", "mla_kv_norm_g", "mla_w_ukv", "mla_w_out", "mlp_w1",
                "mlp_w2", "final_norm_g")


def _small_rows(vals, D):
    rows = [vals["norm_mix_g"], vals["norm_mlp_g"], vals["final_norm_g"][None, :]]
    for n in ("fox_b_f", "mla_q_norm_g", "mla_kv_norm_g"):
        flat = vals[n].reshape(-1)
        assert flat.shape[0] <= D
        rows.append(jnp.pad(flat, (0, D - flat.shape[0]))[None, :])
    return jnp.concatenate(rows, axis=0)


def _small_unrows(rows, shapes):
    L = shapes["norm_mix_g"][0]
    out = {"norm_mix_g": rows[0:L], "norm_mlp_g": rows[L:2 * L], "final_norm_g": rows[2 * L]}
    for k, n in enumerate(("fox_b_f", "mla_q_norm_g", "mla_kv_norm_g")):
        size = int(np.prod(shapes[n]))
        out[n] = rows[2 * L + 1 + k, :size].reshape(shapes[n])
    return out


def kernel(x, c, positions, ada_w, ada_b, norm_mix_g, norm_mlp_g, fox_w_in, fox_b_f, fox_w_out, mla_w_dq, mla_q_norm_g, mla_w_uq, mla_w_dkv, mla_kv_norm_g, mla_w_ukv, mla_w_out, mlp_w1, mlp_w2, final_norm_g, loss_target, m_ada_w, m_ada_b, m_norm_mix_g, m_norm_mlp_g, m_fox_w_in, m_fox_b_f, m_fox_w_out, m_mla_w_dq, m_mla_q_norm_g, m_mla_w_uq, m_mla_w_dkv, m_mla_kv_norm_g, m_mla_w_ukv, m_mla_w_out, m_mlp_w1, m_mlp_w2, m_final_norm_g, v_ada_w, v_ada_b, v_norm_mix_g, v_norm_mlp_g, v_fox_w_in, v_fox_b_f, v_fox_w_out, v_mla_w_dq, v_mla_q_norm_g, v_mla_w_uq, v_mla_w_dkv, v_mla_kv_norm_g, v_mla_w_ukv, v_mla_w_out, v_mlp_w1, v_mlp_w2, v_final_norm_g):
    args = dict(locals())
    wts = {n: args[n] for n in WEIGHT_ORDER}
    mom = {n: args["m_" + n] for n in WEIGHT_ORDER}
    var = {n: args["v_" + n] for n in WEIGHT_ORDER}
    Bl, S, D = x.shape
    T = Bl * S
    L = ada_w.shape[0]
    C = ada_w.shape[2]
    mx, my, mc = _mesh_pos()
    chip = 2 * mx + my
    dev = 4 * mx + 2 * my + mc
    c_idx = jnp.reshape(mc, (1,)).astype(jnp.int32)
    chip_idx = jnp.reshape(chip, (1,)).astype(jnp.int32)
    small = {n: wts[n] for n in SMALL}
    L2, q_cols = mla_q_norm_g.shape
    n_fox_heads = fox_b_f.shape[1]

    shards = _shard_layouts(wts)
    groups = _comm_groups(L, L2)
    slots = _layer_slots(groups)

    def row_halves(a):
        return a.reshape(a.shape[:-2] + (2, a.shape[-2] // 2, a.shape[-1]))

    def whole_rows(a):
        return a.reshape(a.shape[:2] + (a.shape[2] * a.shape[3], a.shape[4]))

    part = {g: [row_halves(shards[n][s:s + cnt]) for n, s, cnt in entries] for g, entries in groups.items()}
    first = _gather_weights(part["first"], name="gather_first")
    own_placed = [_place_own(a, chip_idx, c_idx, name=f"gather_place_{n}")
                  for a, (n, _, _) in zip(part["rest"], groups["rest"])]
    rest_sems = _split_start(_gather_copies, part["rest"], own_placed, first[0], name="gather_rest_start",
                             sems_per_array=4)

    def layer_weights(w, group, arrays):
        for (n, s, cnt), a in zip(groups[group], arrays):
            for key, view in _weight_views(n, whole_rows(a), D, n_fox_heads).items():
                for l in range(cnt):
                    w[key][s + l] = (view, l)

    w = {key: [None] * L2 for key in ("fox_qkv", "fox_f", "fox_out", "mla_down", "mla_uq", "mla_ukv", "mla_out")}
    w.update({key: [None] * L for key in ("mlp_w1", "mlp_w2")})
    layer_weights(w, "first", first)

    def after_layer0(x_now, w):
        _, landed = _split_wait(_gather_copies, *rest_sems[:4], x_now, name="gather_rest_wait")
        layer_weights(w, "rest", _gather_forward(landed, name="gather_rest_forward"))
        return w

    c_pad = jnp.concatenate([c, jnp.pad(mla_q_norm_g, ((0, 8 - Bl - L2), (0, D - q_cols)))], axis=0)
    c8 = _all_gather8(c_pad, name="gather_c", in_vmem=True)
    c_all = c8[:, :Bl].reshape(N_DEV * Bl, D)
    qg4 = c8.reshape(N_CHIP, 2, 8, D)[:, 0, Bl:Bl + L2, :q_cols]
    small["mla_q_norm_g"] = jnp.transpose(qg4, (1, 0, 2)).reshape(L2, N_CHIP * q_cols)
    ada_b_cols = lax.dynamic_slice_in_dim(ada_b, chip * C, C, axis=1)[:, None, :]
    mod_cols = _ada_fwd(c_all, ada_w, ada_b_cols)
    mod8 = _all_gather8(mod_cols.reshape(L * N_DEV * Bl, C), name="gather_mod", in_vmem=True)
    mod4 = mod8.reshape(N_CHIP, 2, L, N_DEV * Bl, C)[:, 0]
    mod_me = lax.dynamic_slice_in_dim(mod4, dev * Bl, Bl, axis=2)
    mod = jnp.transpose(mod_me, (1, 2, 0, 3)).reshape(L, Bl, 6, D)
    mod = jnp.transpose(mod, (0, 2, 1, 3))[:, :, :, None, :]

    w.update(_small_layouts(small))
    mod = mod + rest_sems[4][0, 0]
    pending = {}

    def grad_pieces(group, g_now):
        out = []
        for n, s, cnt in groups[group]:
            qkv_f = [(g_now["fox_qkv"][j], g_now["fox_f"][j]) for j in range(s, s + cnt)] if n == "fox_in" else None
            stacked_g = None if n == "fox_in" else g_now[n][group]
            out.append(row_halves(_grad_pieces(n, stacked_g, qkv_f, n_fox_heads, N_CHIP)))
        return out

    def pair_sums(group, g_now):
        big = grad_pieces(group, g_now)
        sibling = _pair_exchange(big, name=f"grad_pair_exchange_{group}")
        return [_pair_add(a, r, c_idx, name=f"grad_pair_add_{group}_{n}")
                for (n, _, _), a, r in zip(groups[group], big, sibling)]

    def before_layer0(g_now):
        ps = pair_sums("rest", g_now)
        pending["rest"] = _split_start(_chip_copies, ps, _chip_landing(ps), chip_idx, name="grad_exchange_rest_start",
                                       sems_per_array=3)
        return pending["rest"][4]

    half = ROPE_DIM // 2
    inv_freq = ROPE_THETA ** (-jnp.arange(0, ROPE_DIM, 2, dtype=F32) / ROPE_DIM)
    lane = np.arange(LANES)
    inv_freq_row = jnp.tile(inv_freq, LANES // half)[None, :]
    sign_row = jnp.asarray(np.where(lane < 2 * ROPE_DIM, np.where(lane % ROPE_DIM < half, -1.0, 1.0), 0.0), F32)[None, :]
    pos_f = positions.astype(F32).reshape(T, 1)
    loss_row, grad_x, dmod, g = _local_step(x.reshape(T, D), loss_target.reshape(T, D), pos_f, inv_freq_row, sign_row,
                                            mod, w, slots, S=S, after_layer0=after_layer0, before_layer0=before_layer0)
    g_small = _small_grads(g, n_fox_heads)
    ps_first = pair_sums("first", g)
    pending["first"] = _split_start(_chip_copies, ps_first, _chip_landing(ps_first), chip_idx,
                                    name="grad_exchange_first_start", sems_per_array=3)

    Rs = -(-(2 * L + 5) // 8) * 8
    srows = jnp.concatenate([_small_rows(g_small, D), jnp.pad(loss_row, ((0, 0), (0, D - LANES)))], axis=0)
    srows = jnp.pad(srows, ((0, Rs - srows.shape[0]), (0, 0)))
    drows = jnp.transpose(dmod[:, :, :, 0, :], (2, 0, 1, 3)).reshape(Bl * L * 6, D)
    both8 = _all_gather8(jnp.concatenate([drows, srows], axis=0), name="gather_small", in_vmem=True)
    dm8 = both8[:, :Bl * L * 6].reshape(N_DEV, Bl, L * 6, D)
    sm8 = both8[:, Bl * L * 6:]
    adb_rows, small_sum = _sum_gathered(dm8, sm8)
    grad_ada_b = adb_rows.reshape(L, 6 * D)
    loss = small_sum[2 * L + 4, 0]
    small_shapes = {n: (wts[n].shape if n != "mla_q_norm_g" else (wts[n].shape[0], N_CHIP * q_cols)) for n in SMALL}
    gs = _small_unrows(small_sum, small_shapes)
    gs["mla_q_norm_g"] = lax.dynamic_slice_in_dim(gs["mla_q_norm_g"], chip * q_cols, q_cols, axis=1)

    dmod16 = jnp.transpose(dm8.reshape(N_DEV, Bl, L, 6 * D), (2, 0, 1, 3)).reshape(L, N_DEV * Bl, 6 * D)
    dmod_cols = lax.dynamic_slice_in_dim(dmod16, chip * C, C, axis=2)
    grad_ada_w = _ada_bwd(c_all, dmod_cols)

    grads = dict(gs)
    grads["ada_w"] = grad_ada_w
    grads["ada_b"] = grad_ada_b
    delta, new_m, new_v = {}, {}, {}
    for n in ("ada_w", "ada_b"):
        delta[n], new_m[n], new_v[n] = _adamw(wts[n], grads[n], mom[n], var[n], name=f"adamw_{n}")
    shard_small_shapes = {n: wts[n].shape for n in SMALL}
    packs = [jnp.pad(_small_rows({n: src[n] for n in SMALL}, D), ((0, Rs - 2 * L - 4), (0, 0)))
             for src in (wts, grads, mom, var)]
    for dst, rows in zip((delta, new_m, new_v), _adamw(*packs, name="adamw_small")):
        dst.update(_small_unrows(rows, shard_small_shapes))

    halves = {}
    for group, after in (("rest", grad_x), ("first", delta["ada_w"])):
        send_sems, recv_sems, ps, lands, _ = pending[group]
        ps, lands = _split_wait(_chip_copies, send_sems, recv_sems, ps, lands, after, name=f"grad_exchange_{group}_wait")
        sums = [_sum_pieces(ld, p, chip_idx, name=f"grad_sum_{group}_{n}")
                for (n, _, _), ld, p in zip(groups[group], lands, ps)]
        swapped = _pair_swap(sums, name=f"grad_pair_swap_{group}")
        for (n, _, _), a, b in zip(groups[group], sums, swapped):
            halves[(n, group)] = (a, b)

    def all_layers(n, which):
        return jnp.concatenate([halves[(n, grp)][which] for grp in ("first", "rest") if (n, grp) in halves], axis=0)

    own = {n: all_layers(n, 0) for n in GATHERED}
    peer = {n: all_layers(n, 1) for n in GATHERED}
    for nat, n in (("fox_w_in", "fox_in"), ("fox_w_out", "fox_out"), ("mla_w_out", "mla_out"), ("mlp_w1", "mlp_w1"),
                   ("mlp_w2", "mlp_w2")):
        cols = wts[nat].shape[-1]
        res = _adamw_halves(_pad_lanes(wts[nat]), own[n], peer[n], _pad_lanes(mom[nat]), _pad_lanes(var[nat]), c_idx,
                            name=f"adamw_{nat}")
        grads[nat], delta[nat], new_m[nat], new_v[nat] = (a[..., :cols] for a in res)
    joined = {n: jnp.concatenate([jnp.where(mc == 0, own[n], peer[n]), jnp.where(mc == 0, peer[n], own[n])], axis=1)
              for n in ("mla_down", "mla_uq", "mla_ukv")}
    rq = mla_w_dq.shape[-1]
    grads["mla_w_dq"] = joined["mla_down"][:, :, :rq]
    grads["mla_w_dkv"] = joined["mla_down"][:, :, rq:rq + KV_RANK + ROPE_DIM]
    grads["mla_w_uq"] = jax.vmap(_uq_from_pairs)(joined["mla_uq"])
    grads["mla_w_ukv"] = jax.vmap(_ukv_from_pairs)(joined["mla_ukv"])
    for n in ("mla_w_dq", "mla_w_dkv", "mla_w_uq", "mla_w_ukv"):
        delta[n], new_m[n], new_v[n] = _adamw(wts[n], grads[n], mom[n], var[n], name=f"adamw_{n}")

    return (loss, grad_x.reshape(Bl, S, D), *[grads[n] for n in WEIGHT_ORDER], *[delta[n] for n in WEIGHT_ORDER],
            *[new_m[n] for n in WEIGHT_ORDER], *[new_v[n] for n in WEIGHT_ORDER])
```

```python
import functools

import numpy as np
import jax
import jax.numpy as jnp
from jax import lax
from jax.experimental import pallas as pl
from jax.experimental.pallas import tpu as pltpu

F32 = jnp.float32
BF16 = jnp.bfloat16
MESH_ID = pl.DeviceIdType.MESH

NORM_EPS = 1e-6
ROPE_THETA = 10000.0
HEAD_DIM = 64
ROPE_DIM = 32
KV_RANK = 128
FOX_EXTRA = 6
PAIR_Q = 256
PAIR_KV = 384
LANES = 128
ADAM_LR = 0.001
ADAM_B1 = 0.9
ADAM_B2 = 0.999
ADAM_EPS = 1e-08
ADAM_WD = 0.01
ADAM_STEP = 10
VMEM_LIMIT_V7X = 48 * 1024 * 1024
MM_VMEM_BUDGET = 36 * 1024 * 1024
NEG_BIG = -1e30
ATTN_UNROLL = 4

BIG_WEIGHTS = (("fox_w_in", 2), ("fox_w_out", 1), ("mla_w_dq", 1), ("mla_w_uq", 2), ("mla_w_dkv", 1),
               ("mla_w_ukv", 2), ("mla_w_out", 1), ("mlp_w1", 2), ("mlp_w2", 1))


def _cparams(sem=None):
    return pltpu.CompilerParams(dimension_semantics=sem, vmem_limit_bytes=VMEM_LIMIT_V7X)


def _tile(n, want):
    if n <= want:
        return n
    for t in range(want - want % LANES, 0, -LANES):
        if n % t == 0:
            return t
    raise ValueError((n, want))


def _mm(a, b, mode, *, name, out_dtypes=(F32,), epilogue=None, extras=(), rowvecs=(), tables=(),
        seq=None, a_off=0, a_sz=None, b_layer=None, out_stack=None, out_split=0, out_t=(), tm=1024, tn=1024,
        tk=2048):
    if isinstance(b, (list, tuple)):
        b, b_layer = b[b_layer]
    b_rows, b_cols = b.shape[-2], b.shape[-1]
    n_split = b.shape[1] if b.ndim == 4 else 1
    assert mode in ("nn", "nt")
    if mode == "nn":
        M, K, N = a.shape[0], b_rows, b_cols * n_split
    else:
        M, K, N = a.shape[0], b_cols * n_split, b_rows
    assert a_sz is None or a_sz == K
    tm = _tile(seq if rowvecs else M, tm)
    n_piece = N // max(out_split, n_split if mode == "nn" else 1, 1)
    tn = _tile(n_piece, tn)
    tk = _tile(K // (n_split if mode == "nt" else 1), tk)
    ne, nr, nt_ = len(extras), len(rowvecs), len(tables)
    no = len(out_dtypes)

    def vmem_estimate():
        blocks = tm * tk * a.dtype.itemsize + tk * tn * b.dtype.itemsize
        blocks += tm * tn * (sum(e.dtype.itemsize for e in extras) + sum(jnp.dtype(d).itemsize for d in out_dtypes))
        return 2 * blocks + 2 * tm * tn * 4

    while vmem_estimate() > MM_VMEM_BUDGET and max(tm, tn) > 256:
        if tn >= tm:
            tn //= 2
        else:
            tm //= 2
    nk = K // tk

    assert a_off % tk == 0
    a_spec = pl.BlockSpec((tm, tk), lambda i, j, k: (i, k + a_off // tk))
    dims = (((1,), (0,)), ((), ())) if mode == "nn" else (((1,), (1,)), ((), ()))
    lead = () if b.ndim == 2 else (b_layer,)
    sq = (None,) * (b.ndim - 2)
    if mode == "nt":
        kb = b_cols // tk
        if b.ndim == 4:
            b_spec = pl.BlockSpec(sq + (tn, tk), lambda i, j, k: lead + (k // kb, j, k % kb))
        else:
            b_spec = pl.BlockSpec(sq + (tn, tk), lambda i, j, k: lead + (j, k))
    else:
        nb = b_cols // tn
        if b.ndim == 4:
            b_spec = pl.BlockSpec(sq + (tk, tn), lambda i, j, k: lead + (j // nb, k, j % nb))
        else:
            b_spec = pl.BlockSpec(sq + (tk, tn), lambda i, j, k: lead + (k, j))
    in_specs = [a_spec, b_spec]
    in_specs += [pl.BlockSpec((tm, tn), lambda i, j, k: (i, j)) for _ in extras]
    if rowvecs:
        assert seq % tm == 0
        per = seq // tm
        in_specs += [pl.BlockSpec((None, 1, tn), lambda i, j, k: (i // per, 0, j)) for _ in rowvecs]
    in_specs += [pl.BlockSpec((tm, LANES), lambda i, j, k: (i, 0)) for _ in tables]
    operands = [a, b, *extras, *rowvecs, *tables]
    aliases = {}
    transposed = tuple(out_t) + (False,) * (no - len(out_t))
    if out_stack is None:
        out_specs = [pl.BlockSpec((tn, tm), lambda i, j, k: (j, i)) if t else pl.BlockSpec((tm, tn), lambda i, j, k: (i, j))
                     for t in transposed]
        out_shape = [jax.ShapeDtypeStruct((N, M) if t else (M, N), d) for d, t in zip(out_dtypes, transposed)]
    else:
        prev, layer, n_layers = out_stack
        assert no == 1
        if out_split:
            ob = n_piece // tn
            out_specs = [pl.BlockSpec((None, None, tm, tn), lambda i, j, k: (layer, j // ob, i, j % ob))]
            out_shape = [jax.ShapeDtypeStruct((n_layers, out_split, M, n_piece), out_dtypes[0])]
        else:
            out_specs = [pl.BlockSpec((None, tm, tn), lambda i, j, k: (layer, i, j))]
            out_shape = [jax.ShapeDtypeStruct((n_layers, M, N), out_dtypes[0])]
        if prev is not None:
            in_specs.append(pl.BlockSpec(memory_space=pl.ANY))
            aliases = {len(operands): 0}
            operands.append(prev)
    n_in = len(operands)

    def body(*refs):
        a_ref, b_ref = refs[0], refs[1]
        side = refs[2:2 + ne + nr + nt_]
        outs = refs[n_in:n_in + no]

        def finish(acc):
            res = (acc,) if epilogue is None else epilogue(acc, *[r[...] for r in side])
            for o_ref, r, t in zip(outs, res, transposed):
                o_ref[...] = (r.T if t else r).astype(o_ref.dtype)

        part = lax.dot_general(a_ref[...].astype(BF16), b_ref[...].astype(BF16), dims,
                               preferred_element_type=F32)
        if nk == 1:
            finish(part)
        else:
            acc_ref = refs[-1]
            k = pl.program_id(2)

            @pl.when(k == 0)
            def _():
                acc_ref[...] = part

            @pl.when(k > 0)
            def _():
                acc_ref[...] += part

            @pl.when(k == nk - 1)
            def _():
                finish(acc_ref[...])

    res = pl.pallas_call(
        body, name=name, grid=(M // tm, N // tn, nk), in_specs=in_specs, out_specs=out_specs,
        out_shape=out_shape, scratch_shapes=[pltpu.VMEM((tm, tn), F32)] if nk > 1 else [],
        input_output_aliases=aliases,
        compiler_params=_cparams(("parallel", "parallel", "arbitrary")),
    )(*operands)
    return res[0] if no == 1 else tuple(res)


def _rope128(x, cos_t, sin_s):
    lane = lax.broadcasted_iota(jnp.int32, x.shape, 1)
    first = (lane % ROPE_DIM) < (ROPE_DIM // 2)
    swapped = jnp.where(first, pltpu.roll(x, LANES - ROPE_DIM // 2, 1), pltpu.roll(x, ROPE_DIM // 2, 1))
    return x * cos_t + swapped * sin_s


def _rope_pairs(acc, cos_t, sin_s, sign):
    parts = []
    for p in range(acc.shape[1] // PAIR_Q):
        parts.append(acc[:, p * PAIR_Q:p * PAIR_Q + LANES])
        parts.append(_rope128(acc[:, p * PAIR_Q + LANES:(p + 1) * PAIR_Q], cos_t, sign * sin_s))
    return jnp.concatenate(parts, axis=1)


def _rope_tables(pos_f, inv_freq_row, sign_row):
    T = pos_f.shape[0]
    tt = _tile(T, 512)

    def body(p_ref, f_ref, s_ref, cos_ref, sin_ref):
        ang = p_ref[...] * f_ref[...]
        cos_ref[...] = jnp.cos(ang)
        sin_ref[...] = jnp.sin(ang) * s_ref[...]

    return pl.pallas_call(
        body, name="rope_tables", grid=(T // tt,),
        in_specs=[pl.BlockSpec((tt, 1), lambda i: (i, 0)), pl.BlockSpec((1, LANES), lambda i: (0, 0)),
                  pl.BlockSpec((1, LANES), lambda i: (0, 0))],
        out_specs=[pl.BlockSpec((tt, LANES), lambda i: (i, 0))] * 2,
        out_shape=[jax.ShapeDtypeStruct((T, LANES), F32)] * 2,
        compiler_params=_cparams(("parallel",)),
    )(pos_f, inv_freq_row, sign_row)


def _unrope(dqx, cos_t, sin_s):
    T, W = dqx.shape
    tt = _tile(T, 512)

    def body(d_ref, c_ref, s_ref, o_ref):
        o_ref[...] = _rope_pairs(d_ref[...].astype(F32), c_ref[...], s_ref[...], -1.0).astype(BF16)

    return pl.pallas_call(
        body, name="mla_unrope", grid=(T // tt,),
        in_specs=[pl.BlockSpec((tt, W), lambda i: (i, 0)), pl.BlockSpec((tt, LANES), lambda i: (i, 0)),
                  pl.BlockSpec((tt, LANES), lambda i: (i, 0))],
        out_specs=pl.BlockSpec((tt, W), lambda i: (i, 0)),
        out_shape=jax.ShapeDtypeStruct((T, W), BF16),
        compiler_params=_cparams(("parallel",)),
    )(dqx, cos_t, sin_s)


def _row_specs(tt, D, per, n):
    return [pl.BlockSpec((None, 1, D), lambda i: (i // per, 0, 0)) for _ in range(n)]


def _norm_mod(x, gain, sc, sh, *, S, name):
    T, D = x.shape
    tt = _tile(S, 512)
    per = S // tt

    def body(x_ref, g_ref, sc_ref, sh_ref, h_ref, ht_ref):
        xv = x_ref[...]
        r = lax.rsqrt(jnp.mean(xv * xv, axis=-1, keepdims=True) + NORM_EPS)
        h = (xv * r) * g_ref[...] * (1.0 + sc_ref[...]) + sh_ref[...]
        h_ref[...] = h.astype(BF16)
        ht_ref[...] = h.T.astype(BF16)

    return pl.pallas_call(
        body, name=name, grid=(T // tt,),
        in_specs=[pl.BlockSpec((tt, D), lambda i: (i, 0)), pl.BlockSpec((1, D), lambda i: (0, 0))]
        + _row_specs(tt, D, per, 2),
        out_specs=[pl.BlockSpec((tt, D), lambda i: (i, 0)), pl.BlockSpec((D, tt), lambda i: (0, i))],
        out_shape=[jax.ShapeDtypeStruct((T, D), BF16), jax.ShapeDtypeStruct((D, T), BF16)],
        compiler_params=_cparams(("parallel",)),
    )(x, gain, sc, sh)


def _norm_mod_bwd(x, dh, dres, gain, sc, *, S, name):
    T, D = x.shape
    B = T // S
    tt = _tile(S, 512)
    per = S // tt

    def body(x_ref, dh_ref, dres_ref, g_ref, sc_ref, dx_ref, dsh_ref, dsc_ref, dg_ref):
        i = pl.program_id(0)
        xv = x_ref[...]
        dhv = dh_ref[...].astype(F32)
        r = lax.rsqrt(jnp.mean(xv * xv, axis=-1, keepdims=True) + NORM_EPS)
        n = xv * r
        g = g_ref[...]
        one_sc = 1.0 + sc_ref[...]
        dn = dhv * (g * one_sc)
        dx_ref[...] = dres_ref[...] + r * (dn - n * jnp.mean(dn * n, axis=-1, keepdims=True))
        dhn = dhv * n

        @pl.when(i % per == 0)
        def _():
            dsh_ref[...] = jnp.zeros_like(dsh_ref)
            dsc_ref[...] = jnp.zeros_like(dsc_ref)

        @pl.when(i == 0)
        def _():
            dg_ref[...] = jnp.zeros_like(dg_ref)

        dsh_ref[...] += jnp.sum(dhv, axis=0, keepdims=True)
        dsc_ref[...] += jnp.sum(dhn, axis=0, keepdims=True) * g
        dg_ref[...] += jnp.sum(dhn, axis=0, keepdims=True) * one_sc

    return pl.pallas_call(
        body, name=name, grid=(T // tt,),
        in_specs=[pl.BlockSpec((tt, D), lambda i: (i, 0))] * 3 + [pl.BlockSpec((1, D), lambda i: (0, 0))]
        + _row_specs(tt, D, per, 1),
        out_specs=[pl.BlockSpec((tt, D), lambda i: (i, 0))] + _row_specs(tt, D, per, 2)
        + [pl.BlockSpec((1, D), lambda i: (0, 0))],
        out_shape=[jax.ShapeDtypeStruct((T, D), F32), jax.ShapeDtypeStruct((B, 1, D), F32),
                   jax.ShapeDtypeStruct((B, 1, D), F32), jax.ShapeDtypeStruct((1, D), F32)],
        compiler_params=_cparams(("arbitrary",)),
    )(x, dh, dres, gain, sc)


def _gate_bwd(dx, y, g, *, S, name):
    T, D = dx.shape
    B = T // S
    tt = _tile(S, 512)
    per = S // tt

    def body(dx_ref, y_ref, g_ref, dy_ref, dg_ref):
        i = pl.program_id(0)
        dxv = dx_ref[...]
        dy_ref[...] = (dxv * g_ref[...]).astype(BF16)

        @pl.when(i % per == 0)
        def _():
            dg_ref[...] = jnp.zeros_like(dg_ref)

        dg_ref[...] += jnp.sum(dxv * y_ref[...], axis=0, keepdims=True)

    return pl.pallas_call(
        body, name=name, grid=(T // tt,),
        in_specs=[pl.BlockSpec((tt, D), lambda i: (i, 0))] * 2 + _row_specs(tt, D, per, 1),
        out_specs=[pl.BlockSpec((tt, D), lambda i: (i, 0))] + _row_specs(tt, D, per, 1),
        out_shape=[jax.ShapeDtypeStruct((T, D), BF16), jax.ShapeDtypeStruct((B, 1, D), F32)],
        compiler_params=_cparams(("arbitrary",)),
    )(dx, y, g)


def _final_loss(x, target, gain):
    T, D = x.shape
    tt = _tile(T, 512)

    def body(x_ref, t_ref, g_ref, dx_ref, dg_ref, loss_ref):
        i = pl.program_id(0)
        xv = x_ref[...]
        r = lax.rsqrt(jnp.mean(xv * xv, axis=-1, keepdims=True) + NORM_EPS)
        n = xv * r
        g = g_ref[...]
        err = n * g - t_ref[...]
        dy = err * (1.0 / D)
        dn = dy * g
        dx_ref[...] = r * (dn - n * jnp.mean(dn * n, axis=-1, keepdims=True))

        @pl.when(i == 0)
        def _():
            dg_ref[...] = jnp.zeros_like(dg_ref)
            loss_ref[...] = jnp.zeros_like(loss_ref)

        dg_ref[...] += jnp.sum(dy * n, axis=0, keepdims=True)
        loss_ref[...] += jnp.sum(jnp.sum(err * err, axis=-1, keepdims=True), axis=0, keepdims=True) * (0.5 / D)

    return pl.pallas_call(
        body, name="final_loss", grid=(T // tt,),
        in_specs=[pl.BlockSpec((tt, D), lambda i: (i, 0))] * 2 + [pl.BlockSpec((1, D), lambda i: (0, 0))],
        out_specs=[pl.BlockSpec((tt, D), lambda i: (i, 0)), pl.BlockSpec((1, D), lambda i: (0, 0)),
                   pl.BlockSpec((1, LANES), lambda i: (0, 0))],
        out_shape=[jax.ShapeDtypeStruct((T, D), F32), jax.ShapeDtypeStruct((1, D), F32),
                   jax.ShapeDtypeStruct((1, LANES), F32)],
        compiler_params=_cparams(("arbitrary",)),
    )(x, target, gain)


def _head_masks(ew):
    lane = lax.broadcasted_iota(jnp.int32, (1, PAIR_Q), 1)
    m0 = (lane < HEAD_DIM) | ((lane >= LANES) & (lane < LANES + ew))
    m1 = ((lane >= HEAD_DIM) & (lane < LANES)) | ((lane >= LANES + ew) & (lane < LANES + 2 * ew))
    return m0, m1


def _dot_nt(a, b):
    return lax.dot_general(a, b, (((1,), (1,)), ((), ())), preferred_element_type=F32)


def _dot_tn(a, b):
    return lax.dot_general(a, b, (((0,), (0,)), ((), ())), preferred_element_type=F32)


def _lane_halves(x, op):
    acc = x[:, 0:LANES]
    for g in range(1, x.shape[1] // LANES):
        acc = op(acc, x[:, g * LANES:(g + 1) * LANES])
    return acc


def _head_rows(cols_lane_replicated):
    t = cols_lane_replicated.T
    sub = lax.broadcasted_iota(jnp.int32, (8, t.shape[1]), 0)
    return jnp.where(sub == 1, t[HEAD_DIM:HEAD_DIM + 8], t[0:8])


def _attn_trip(n_blocks):
    return ATTN_UNROLL if n_blocks % ATTN_UNROLL == 0 else 2


def _attn_fwd(qx, kvx, *, S, scale, ew, name):
    T = qx.shape[0]
    P = qx.shape[1] // PAIR_Q
    B = T // S
    tq = _tile(S, 256)
    nq = S // tq
    big = _attn_trip(nq)

    def body(q_ref, kv_ref, o_ref, lse_ref, ot_ref, m_sc, l_sc, acc_sc):
        qi = pl.program_id(2)
        q = q_ref[...]
        masks = _head_masks(ew)
        qh = [jnp.where(m, q, jnp.zeros_like(q)) for m in masks]

        def logits(h, k, kj):
            s = _dot_nt(qh[h], k)
            if scale != 1.0:
                s = s * scale
            row = lax.broadcasted_iota(jnp.int32, s.shape, 0)
            col = lax.broadcasted_iota(jnp.int32, s.shape, 1)
            return jnp.where(col - row <= (qi - kj) * tq, s, NEG_BIG)

        def trip(first, count):
            rows = [pl.ds(pl.multiple_of((first + u) * tq, tq), tq) for u in range(count)]
            for h in range(2):
                ss = [logits(h, kv_ref[rows[u], 0:PAIR_Q], first + u) for u in range(count)]
                m_prev = m_sc[h]
                m_elem = m_prev
                for s in ss:
                    m_elem = jnp.maximum(m_elem, _lane_halves(s, jnp.maximum))
                m_new = jnp.broadcast_to(jnp.max(m_elem, axis=1, keepdims=True), (tq, LANES))
                alpha = jnp.exp(m_prev - m_new)
                l = alpha * l_sc[h]
                acc = alpha * acc_sc[h]
                for u, s in enumerate(ss):
                    p = jnp.concatenate([jnp.exp(s[:, g * LANES:(g + 1) * LANES] - m_new)
                                         for g in range(tq // LANES)], axis=1)
                    l = l + _lane_halves(p, jnp.add)
                    acc = acc + jnp.dot(p.astype(BF16), kv_ref[rows[u], PAIR_Q:PAIR_KV], preferred_element_type=F32)
                m_sc[h] = m_new
                l_sc[h] = l
                acc_sc[h] = acc

        m_sc[...] = jnp.full(m_sc.shape, NEG_BIG, F32)
        l_sc[...] = jnp.zeros_like(l_sc)
        acc_sc[...] = jnp.zeros_like(acc_sc)

        def loop_body(t, carry):
            trip(t * big, big)
            return carry

        if big == 2:
            lax.fori_loop(0, (qi + 2) // 2, loop_body, 0)
        else:
            trips = (qi + 2) // big
            lax.fori_loop(0, trips, loop_body, 0)

            @pl.when(qi % big <= 1)
            def _():
                trip(trips * big, 2)

        lane = lax.broadcasted_iota(jnp.int32, (tq, LANES), 1)
        lo = lane < HEAD_DIM
        l = [jnp.sum(l_sc[h], axis=1, keepdims=True) for h in range(2)]
        o = jnp.where(lo, acc_sc[0] / l[0], acc_sc[1] / l[1])
        o_ref[...] = o.astype(BF16)
        ot_ref[...] = o.T.astype(BF16)
        lse_ref[...] = _head_rows(jnp.where(lo, m_sc[0] + jnp.log(l[0]), m_sc[1] + jnp.log(l[1])))

    return pl.pallas_call(
        body, name=name, grid=(B, P, nq),
        in_specs=[pl.BlockSpec((tq, PAIR_Q), lambda b, p, i: (b * nq + i, p)),
                  pl.BlockSpec((S, PAIR_KV), lambda b, p, i: (b, p))],
        out_specs=[pl.BlockSpec((tq, LANES), lambda b, p, i: (b * nq + i, p)),
                   pl.BlockSpec((None, None, 8, tq), lambda b, p, i: (b * nq + i, p, 0, 0)),
                   pl.BlockSpec((LANES, tq), lambda b, p, i: (p, b * nq + i))],
        out_shape=[jax.ShapeDtypeStruct((T, P * LANES), BF16), jax.ShapeDtypeStruct((T // tq, P, 8, tq), F32),
                   jax.ShapeDtypeStruct((P * LANES, T), BF16)],
        scratch_shapes=[pltpu.VMEM((2, tq, LANES), F32)] * 3,
        compiler_params=_cparams(("parallel", "parallel", "arbitrary")),
    )(qx, kvx)


def _attn_bwd(qx, kvx, o, lse, do, *, S, scale, ew, name, bias_grad=False):
    T = qx.shape[0]
    P = qx.shape[1] // PAIR_Q
    B = T // S
    tq = _tile(S, 256)
    nq = S // tq
    big = _attn_trip(nq)

    def body(q_ref, kv_ref, o_ref, lse_ref, do_ref, dq_ref, dkv_ref, *rest):
        kj = pl.program_id(2)
        if bias_grad:
            csum_ref, rsum_ref, dq_sc, delta_sc, dk_sc, dv_sc, cs_sc = rest
            cs_sc[...] = jnp.zeros_like(cs_sc)

            @pl.when(kj == 0)
            def _():
                rsum_ref[...] = jnp.zeros_like(rsum_ref)
        else:
            dq_sc, delta_sc, dk_sc, dv_sc = rest
        masks = _head_masks(ew)
        lane = lax.broadcasted_iota(jnp.int32, (tq, LANES), 1)
        lo = lane < HEAD_DIM
        vmask = [lo, jnp.logical_not(lo)]

        @pl.when(kj == 0)
        def _():
            dq_sc[...] = jnp.zeros_like(dq_sc)
            for c in range(nq):
                rows = pl.ds(c * tq, tq)
                x = do_ref[rows, :].astype(F32) * o_ref[rows, :].astype(F32)
                r0 = jnp.sum(jnp.where(lo, x, 0.0), axis=1, keepdims=True)
                r1 = jnp.sum(jnp.where(lo, 0.0, x), axis=1, keepdims=True)
                delta_sc[c] = _head_rows(jnp.where(lo, r0, r1))

        k = kv_ref[:, 0:PAIR_Q]
        v = kv_ref[:, PAIR_Q:PAIR_KV]
        kh = [jnp.where(m, k, jnp.zeros_like(k)) for m in masks]
        vh = [jnp.where(m, v, jnp.zeros_like(v)) for m in vmask]
        dk_sc[...] = jnp.zeros_like(dk_sc)
        dv_sc[...] = jnp.zeros_like(dv_sc)

        def step(qi):
            rows = pl.ds(pl.multiple_of(qi * tq, tq), tq)
            q = q_ref[rows, :]
            dov = do_ref[rows, :]
            lse8 = lse_ref[qi]
            dl8 = delta_sc[qi]
            for h in range(2):
                st = _dot_nt(kh[h], q)
                if scale != 1.0:
                    st = st * scale
                key = lax.broadcasted_iota(jnp.int32, st.shape, 0)
                qry = lax.broadcasted_iota(jnp.int32, st.shape, 1)
                st = jnp.where(key - qry <= (qi - kj) * tq, st, NEG_BIG)
                pt = jnp.exp(st - lse8[h:h + 1, :])
                dpt = _dot_nt(vh[h], dov)
                dst = pt * (dpt - dl8[h:h + 1, :])
                if bias_grad:
                    cs_sc[h] += _lane_halves(dst, jnp.add)
                    rsum_ref[qi, h:h + 1, :] += jnp.sum(dst, axis=0, keepdims=True)
                if scale != 1.0:
                    dst = dst * scale
                ptb = pt.astype(BF16)
                dstb = dst.astype(BF16)
                dv_sc[h] += jnp.dot(ptb, dov, preferred_element_type=F32)
                dk_sc[h] += jnp.dot(dstb, q, preferred_element_type=F32)
                dq_sc[rows, :] += _dot_tn(dstb, kh[h])

        def loop_body(t, carry):
            for u in range(big):
                step(t * big + u)
            return carry

        if big == 2:
            lax.fori_loop(kj // 2, nq // 2, loop_body, 0)
        else:
            half_empty = kj % big >= 2
            lax.fori_loop(kj // big + half_empty.astype(jnp.int32), nq // big, loop_body, 0)

            @pl.when(half_empty)
            def _():
                for u in range(2):
                    step((kj // big) * big + 2 + u)
        dkv_ref[:, 0:PAIR_Q] = (jnp.where(masks[0], dk_sc[0], 0.0) + jnp.where(masks[1], dk_sc[1], 0.0)).astype(BF16)
        dkv_ref[:, PAIR_Q:PAIR_KV] = jnp.where(lo, dv_sc[0], dv_sc[1]).astype(BF16)
        if bias_grad:
            csum_ref[...] = jnp.where(lo, jnp.sum(cs_sc[0], axis=1, keepdims=True),
                                      jnp.sum(cs_sc[1], axis=1, keepdims=True))

        @pl.when(kj == nq - 1)
        def _():
            dq_ref[...] = dq_sc[...].astype(BF16)

    rows_spec = pl.BlockSpec((nq, None, 8, tq), lambda b, p, j: (b, p, 0, 0))
    out_specs = [pl.BlockSpec((S, PAIR_Q), lambda b, p, j: (b, p)),
                 pl.BlockSpec((tq, PAIR_KV), lambda b, p, j: (b * nq + j, p))]
    out_shape = [jax.ShapeDtypeStruct((T, P * PAIR_Q), BF16), jax.ShapeDtypeStruct((T, P * PAIR_KV), BF16)]
    scratch = [pltpu.VMEM((S, PAIR_Q), F32), pltpu.VMEM((nq, 8, tq), F32),
               pltpu.VMEM((2, tq, PAIR_Q), F32), pltpu.VMEM((2, tq, LANES), F32)]
    if bias_grad:
        out_specs += [pl.BlockSpec((tq, LANES), lambda b, p, j: (b * nq + j, p)), rows_spec]
        out_shape += [jax.ShapeDtypeStruct((T, P * LANES), F32), jax.ShapeDtypeStruct((T // tq, P, 8, tq), F32)]
        scratch.append(pltpu.VMEM((2, tq, LANES), F32))
    return pl.pallas_call(
        body, name=name, grid=(B, P, nq),
        in_specs=[pl.BlockSpec((S, PAIR_Q), lambda b, p, j: (b, p)),
                  pl.BlockSpec((tq, PAIR_KV), lambda b, p, j: (b * nq + j, p)),
                  pl.BlockSpec((S, LANES), lambda b, p, j: (b, p)), rows_spec,
                  pl.BlockSpec((S, LANES), lambda b, p, j: (b, p))],
        out_specs=out_specs, out_shape=out_shape, scratch_shapes=scratch,
        compiler_params=_cparams(("parallel", "parallel", "arbitrary")),
    )(qx, kvx, o, lse, do)


def _fox_consts(P):
    H = 2 * P
    eq = np.zeros((3 * LANES, P * LANES), np.float32)
    ek = np.zeros((3 * LANES, P * LANES), np.float32)
    ones_q = np.zeros((1, P * LANES), np.float32)
    ones_k = np.zeros((1, P * LANES), np.float32)
    for h in range(H):
        base = (h // 2) * LANES + FOX_EXTRA * (h % 2)
        for part in range(3):
            eq[part * LANES + h, base + part] = 1.0
            ones_q[0, base + 3 + part] = 1.0
            ones_k[0, base + part] = 1.0
            ek[part * LANES + h, base + 3 + part] = -1.0
    return eq, ek, ones_q, ones_k


def _split3(f):
    hi = f.astype(BF16)
    r = f - hi.astype(F32)
    mid = r.astype(BF16)
    lo = (r - mid.astype(F32)).astype(BF16)
    return hi, mid, lo


def _tri_sum(tri, x):
    hi, mid, lo = _split3(x)
    return (jnp.dot(tri, hi, preferred_element_type=F32) + jnp.dot(tri, mid, preferred_element_type=F32)
            + jnp.dot(tri, lo, preferred_element_type=F32))


def _log1p_pos(e):
    return jnp.where(e < 0.01, e * (1.0 - e * (0.5 - e * (1.0 / 3.0))), jnp.log(1.0 + e))


def _fox_prep(qkv, fl, b_row, *, S, D, name):
    T = qkv.shape[0]
    P = D // LANES
    B = T // S
    tt = _tile(S, 256)
    per = S // tt
    eq, ek, ones_q, ones_k = _fox_consts(P)
    q_scale = HEAD_DIM ** -0.5

    def body(q_ref, k_ref, v_ref, fl_ref, b_ref, eq_ref, ek_ref, oq_ref, ok_ref, qx_ref, kvx_ref, carry):
        i = pl.program_id(1)

        @pl.when(i == 0)
        def _():
            carry[...] = jnp.zeros_like(carry)

        z = fl_ref[...] + b_ref[...]
        logf = jnp.minimum(z, 0.0) - _log1p_pos(jnp.exp(-jnp.abs(z)))
        row = lax.broadcasted_iota(jnp.int32, (tt, tt), 0)
        col = lax.broadcasted_iota(jnp.int32, (tt, tt), 1)
        tri = (col <= row).astype(BF16)
        f = _tri_sum(tri, logf) + carry[...]
        carry[...] = f[tt - 1:tt, :]
        parts = jnp.concatenate(_split3(f), axis=1)
        xq = jnp.dot(parts, eq_ref[...], preferred_element_type=F32) + oq_ref[...]
        xk = jnp.dot(parts, ek_ref[...], preferred_element_type=F32) + ok_ref[...]
        for p in range(P):
            c = slice(p * LANES, (p + 1) * LANES)
            qx_ref[:, p * PAIR_Q:p * PAIR_Q + LANES] = (q_ref[:, c].astype(F32) * q_scale).astype(BF16)
            qx_ref[:, p * PAIR_Q + LANES:(p + 1) * PAIR_Q] = xq[:, c].astype(BF16)
            kvx_ref[:, p * PAIR_KV:p * PAIR_KV + LANES] = k_ref[:, c]
            kvx_ref[:, p * PAIR_KV + LANES:p * PAIR_KV + PAIR_Q] = xk[:, c].astype(BF16)
            kvx_ref[:, p * PAIR_KV + PAIR_Q:(p + 1) * PAIR_KV] = v_ref[:, c]

    tok = lambda b, i: (b * per + i, 0)
    const = lambda b, i: (0, 0)
    return pl.pallas_call(
        body, name=name, grid=(B, per),
        in_specs=[pl.BlockSpec((tt, D), lambda b, i: (b * per + i, 0)),
                  pl.BlockSpec((tt, D), lambda b, i: (b * per + i, 1)),
                  pl.BlockSpec((tt, D), lambda b, i: (b * per + i, 2)),
                  pl.BlockSpec((tt, LANES), tok), pl.BlockSpec((1, LANES), const),
                  pl.BlockSpec(eq.shape, const), pl.BlockSpec(ek.shape, const),
                  pl.BlockSpec(ones_q.shape, const), pl.BlockSpec(ones_k.shape, const)],
        out_specs=[pl.BlockSpec((tt, P * PAIR_Q), tok), pl.BlockSpec((tt, P * PAIR_KV), tok)],
        out_shape=[jax.ShapeDtypeStruct((T, P * PAIR_Q), BF16), jax.ShapeDtypeStruct((T, P * PAIR_KV), BF16)],
        scratch_shapes=[pltpu.VMEM((1, LANES), F32)],
        compiler_params=_cparams(("arbitrary", "arbitrary")),
    )(qkv, qkv, qkv, fl, b_row, jnp.asarray(eq, BF16), jnp.asarray(ek, BF16), jnp.asarray(ones_q), jnp.asarray(ones_k))


def _fox_unprep(dqx, dkvx, csum, rsum, fl, b_row, *, S, D, name):
    T = dqx.shape[0]
    P = D // LANES
    B = T // S
    tt = _tile(S, 256)
    per = S // tt
    q_scale = HEAD_DIM ** -0.5

    def body(dq_ref, dkv_ref, cs_ref, rs_ref, fl_ref, b_ref, dqkv_ref, dfl_ref, db_ref, carry):
        b = pl.program_id(0)
        i = pl.program_id(1)

        @pl.when(i == 0)
        def _():
            carry[...] = jnp.zeros_like(carry)

        @pl.when((i == 0) & (b == 0))
        def _():
            db_ref[...] = jnp.zeros_like(db_ref)

        df = rs_ref[...] - cs_ref[...]
        for p in range(P):
            rq = slice(p * LANES, (p + 1) * LANES)
            dqkv_ref[:, rq] = (dq_ref[:, p * PAIR_Q:p * PAIR_Q + LANES].astype(F32) * q_scale).astype(BF16)
            dqkv_ref[:, D + p * LANES:D + (p + 1) * LANES] = dkv_ref[:, p * PAIR_KV:p * PAIR_KV + LANES]
            dqkv_ref[:, 2 * D + p * LANES:2 * D + (p + 1) * LANES] = dkv_ref[:, p * PAIR_KV + PAIR_Q:(p + 1) * PAIR_KV]
        row = lax.broadcasted_iota(jnp.int32, (tt, tt), 0)
        col = lax.broadcasted_iota(jnp.int32, (tt, tt), 1)
        tri = (col >= row).astype(BF16)
        dlogf = _tri_sum(tri, df) + carry[...]
        carry[...] = dlogf[0:1, :]
        z = fl_ref[...] + b_ref[...]
        e = jnp.exp(-jnp.abs(z))
        sig_neg = jnp.where(z >= 0.0, e, 1.0) / (1.0 + e)
        dfl = dlogf * sig_neg
        dfl_ref[...] = dfl.astype(BF16)
        db_ref[...] += jnp.sum(dfl, axis=0, keepdims=True)

    rev = lambda b, i: (b * per + per - 1 - i, 0)
    const = lambda b, i: (0, 0)
    return pl.pallas_call(
        body, name=name, grid=(B, per),
        in_specs=[pl.BlockSpec((tt, P * PAIR_Q), rev), pl.BlockSpec((tt, P * PAIR_KV), rev),
                  pl.BlockSpec((tt, LANES), rev), pl.BlockSpec((tt, LANES), rev), pl.BlockSpec((tt, LANES), rev),
                  pl.BlockSpec((1, LANES), const)],
        out_specs=[pl.BlockSpec((tt, 3 * D), rev), pl.BlockSpec((tt, LANES), rev), pl.BlockSpec((1, LANES), const)],
        out_shape=[jax.ShapeDtypeStruct((T, 3 * D), BF16), jax.ShapeDtypeStruct((T, LANES), BF16),
                   jax.ShapeDtypeStruct((1, LANES), F32)],
        scratch_shapes=[pltpu.VMEM((1, LANES), F32)],
        compiler_params=_cparams(("arbitrary", "arbitrary")),
    )(dqx, dkvx, csum, rsum, fl, b_row)


def _rms(x):
    r = lax.rsqrt(jnp.mean(x * x, axis=-1, keepdims=True) + NORM_EPS)
    return x * r, r


def _mla_mid(lat, gq, gkv, cos_t, sin_s, *, name):
    T, W = lat.shape
    Rq = W - 2 * LANES
    tt = _tile(T, 512)

    def body(l_ref, gq_ref, gkv_ref, c_ref, s_ref, o_ref, ot_ref):
        nq, _ = _rms(l_ref[:, 0:Rq])
        nkv, _ = _rms(l_ref[:, Rq:Rq + LANES])
        parts = [nq * gq_ref[...], nkv * gkv_ref[...], _rope128(l_ref[:, Rq + LANES:W], c_ref[...], s_ref[...])]
        out = jnp.concatenate(parts, axis=1)
        o_ref[...] = out.astype(BF16)
        ot_ref[...] = out.T.astype(BF16)

    return pl.pallas_call(
        body, name=name, grid=(T // tt,),
        in_specs=[pl.BlockSpec((tt, W), lambda i: (i, 0)), pl.BlockSpec((1, Rq), lambda i: (0, 0)),
                  pl.BlockSpec((1, LANES), lambda i: (0, 0)), pl.BlockSpec((tt, LANES), lambda i: (i, 0)),
                  pl.BlockSpec((tt, LANES), lambda i: (i, 0))],
        out_specs=[pl.BlockSpec((tt, W), lambda i: (i, 0)), pl.BlockSpec((W, tt), lambda i: (0, i))],
        out_shape=[jax.ShapeDtypeStruct((T, W), BF16), jax.ShapeDtypeStruct((W, T), BF16)],
        compiler_params=_cparams(("parallel",)),
    )(lat, gq, gkv, cos_t, sin_s)


def _mla_mid_bwd(lat, dcq, dckr, gq, gkv, cos_t, sin_s, *, name):
    T, W = lat.shape
    Rq = W - 2 * LANES
    tt = _tile(T, 512)

    def norm_bwd(x, dy, g):
        n, r = _rms(x)
        dn = dy * g
        return r * (dn - n * jnp.mean(dn * n, axis=-1, keepdims=True)), jnp.sum(dy * n, axis=0, keepdims=True)

    def body(l_ref, dq_ref, dk_ref, gq_ref, gkv_ref, c_ref, s_ref, o_ref, dgq_ref, dgkv_ref):
        i = pl.program_id(0)

        @pl.when(i == 0)
        def _():
            dgq_ref[...] = jnp.zeros_like(dgq_ref)
            dgkv_ref[...] = jnp.zeros_like(dgkv_ref)

        dxq, dgq = norm_bwd(l_ref[:, 0:Rq], dq_ref[...], gq_ref[...])
        dxkv, dgkv = norm_bwd(l_ref[:, Rq:Rq + LANES], dk_ref[:, 0:LANES], gkv_ref[...])
        o_ref[:, 0:Rq] = dxq.astype(BF16)
        o_ref[:, Rq:Rq + LANES] = dxkv.astype(BF16)
        o_ref[:, Rq + LANES:W] = _rope128(dk_ref[:, LANES:2 * LANES], c_ref[...], -s_ref[...]).astype(BF16)
        dgq_ref[...] += dgq
        dgkv_ref[...] += dgkv

    return pl.pallas_call(
        body, name=name, grid=(T // tt,),
        in_specs=[pl.BlockSpec((tt, W), lambda i: (i, 0)), pl.BlockSpec((tt, Rq), lambda i: (i, 0)),
                  pl.BlockSpec((tt, 2 * LANES), lambda i: (i, 0)), pl.BlockSpec((1, Rq), lambda i: (0, 0)),
                  pl.BlockSpec((1, LANES), lambda i: (0, 0)), pl.BlockSpec((tt, LANES), lambda i: (i, 0)),
                  pl.BlockSpec((tt, LANES), lambda i: (i, 0))],
        out_specs=[pl.BlockSpec((tt, W), lambda i: (i, 0)), pl.BlockSpec((1, Rq), lambda i: (0, 0)),
                   pl.BlockSpec((1, LANES), lambda i: (0, 0))],
        out_shape=[jax.ShapeDtypeStruct((T, W), BF16), jax.ShapeDtypeStruct((1, Rq), F32),
                   jax.ShapeDtypeStruct((1, LANES), F32)],
        compiler_params=_cparams(("arbitrary",)),
    )(lat, dcq, dckr, gq, gkv, cos_t, sin_s)


def _uq_to_pairs(w):
    Rq = w.shape[0]
    P = w.shape[1] // (2 * (HEAD_DIM + ROPE_DIM))
    w4 = w.reshape(Rq, P, 2, HEAD_DIM + ROPE_DIM)
    nope = w4[..., :HEAD_DIM].reshape(Rq, P, 2 * HEAD_DIM)
    rope = w4[..., HEAD_DIM:].reshape(Rq, P, 2 * ROPE_DIM)
    pad = jnp.zeros((Rq, P, PAIR_Q - 2 * HEAD_DIM - 2 * ROPE_DIM), w.dtype)
    return jnp.concatenate([nope, rope, pad], axis=-1).reshape(Rq, P * PAIR_Q)


def _uq_from_pairs(g):
    Rq = g.shape[0]
    P = g.shape[1] // PAIR_Q
    g3 = g.reshape(Rq, P, PAIR_Q)
    nope = g3[..., :2 * HEAD_DIM].reshape(Rq, P, 2, HEAD_DIM)
    rope = g3[..., 2 * HEAD_DIM:2 * HEAD_DIM + 2 * ROPE_DIM].reshape(Rq, P, 2, ROPE_DIM)
    return jnp.concatenate([nope, rope], axis=-1).reshape(Rq, P * 2 * (HEAD_DIM + ROPE_DIM))


def _ukv_to_pairs(w):
    P = w.shape[1] // (4 * HEAD_DIM)
    w4 = w.reshape(KV_RANK, P, 2, 2 * HEAD_DIM)
    kn = w4[..., :HEAD_DIM].reshape(KV_RANK, P, 2 * HEAD_DIM)
    vv = w4[..., HEAD_DIM:].reshape(KV_RANK, P, 2 * HEAD_DIM)
    top = jnp.concatenate([kn, jnp.zeros((KV_RANK, P, LANES), w.dtype), vv], axis=-1)
    place = np.zeros((LANES, P, PAIR_KV), np.float32)
    for r in range(ROPE_DIM):
        place[r, :, LANES + r] = 1.0
        place[r, :, LANES + ROPE_DIM + r] = 1.0
    return jnp.concatenate([top, jnp.asarray(place, w.dtype)], axis=0).reshape(KV_RANK + LANES, P * PAIR_KV)


def _ukv_from_pairs(g):
    P = g.shape[1] // PAIR_KV
    g3 = g[:KV_RANK].reshape(KV_RANK, P, PAIR_KV)
    kn = g3[..., :2 * HEAD_DIM].reshape(KV_RANK, P, 2, HEAD_DIM)
    vv = g3[..., PAIR_Q:].reshape(KV_RANK, P, 2, HEAD_DIM)
    return jnp.concatenate([kn, vv], axis=-1).reshape(KV_RANK, P * 4 * HEAD_DIM)


def _mlp_fwd(h2, w, i, x1, gate, *, S):
    def act(acc):
        u = jnp.square(jnp.maximum(acc, 0.0))
        return acc, u, u

    p, u, u_t = _mm(h2, w["mlp_w1"], "nn", name=f"mlp_up_{i}", b_layer=i, out_dtypes=(BF16, BF16, BF16),
                    out_t=(False, False, True), epilogue=act)
    x2, z = _mm(u, w["mlp_w2"], "nn", name=f"mlp_down_{i}", b_layer=i, out_dtypes=(F32, F32), extras=(x1,),
                rowvecs=(gate,), seq=S, epilogue=lambda acc, xr, g: (xr + g * acc, acc))
    return x2, (p, u_t, z)


STACKED_GRADS = ("fox_out", "mla_down", "mla_uq", "mla_ukv", "mla_out", "mlp_w1", "mlp_w2")


def _local_step(x, target, pos_f, inv_freq_row, sign_row, mod, w, slots, *, S, after_layer0=None, before_layer0=None):
    T, D = x.shape
    L = mod.shape[0]
    L2 = len(w["fox_out"])
    n_split = w["mlp_w1"][0][0].shape[1]
    cos_t, sin_s = _rope_tables(pos_f, inv_freq_row, sign_row)
    saved = []
    for i in range(L):
        j = i // 2
        sh_m, sc_m, g_m, sh_f, sc_f, g_f = (mod[i, s] for s in range(6))
        h, h_t = _norm_mod(x, w["norm_mix_g"][i], sc_m, sh_m, S=S, name=f"norm_mix_{i}")
        if i % 2 == 0:
            qkv = _mm(h, w["fox_qkv"], "nn", name=f"fox_qkv_{i}", b_layer=j, out_dtypes=(BF16,))
            fl = _mm(h, w["fox_f"], "nn", name=f"fox_f_{i}", b_layer=j)
            qx, kvx = _fox_prep(qkv, fl, w["fox_b"][j], S=S, D=D, name=f"fox_prep_{i}")
            o, lse, o_t = _attn_fwd(qx, kvx, S=S, scale=1.0, ew=FOX_EXTRA, name=f"fox_attn_{i}")
            mix = (qx, kvx, o, lse, o_t, fl)
            w_out = w["fox_out"]
        else:
            lat = _mm(h, w["mla_down"], "nn", name=f"mla_down_{i}", b_layer=j)
            Rq = lat.shape[1] - 2 * LANES
            cqr, cqr_t = _mla_mid(lat, w["mla_gq"][j], w["mla_gkv"][j], cos_t, sin_s, name=f"mla_mid_{i}")
            qx = _mm(cqr, w["mla_uq"], "nn", name=f"mla_uq_{i}", b_layer=j, out_dtypes=(BF16,), a_sz=Rq, tk=Rq,
                     tables=(cos_t, sin_s), epilogue=lambda acc, c, s: (_rope_pairs(acc, c, s, 1.0),))
            kvx = _mm(cqr, w["mla_ukv"], "nn", name=f"mla_ukv_{i}", b_layer=j, out_dtypes=(BF16,), a_off=Rq,
                      a_sz=2 * LANES, tk=2 * LANES, tn=PAIR_KV)
            o, lse, o_t = _attn_fwd(qx, kvx, S=S, scale=(HEAD_DIM + ROPE_DIM) ** -0.5, ew=ROPE_DIM,
                                    name=f"mla_attn_{i}")
            mix = (qx, kvx, o, lse, o_t, lat, cqr_t)
            w_out = w["mla_out"]
        x1, y = _mm(o, w_out, "nn", name=f"mix_out_{i}", b_layer=j, out_dtypes=(F32, F32), extras=(x,),
                    rowvecs=(g_m,), seq=S, epilogue=lambda acc, xr, g: (xr + g * acc, acc))
        h2, h2_t = _norm_mod(x1, w["norm_mlp_g"][i], sc_f, sh_f, S=S, name=f"norm_mlp_{i}")
        x2, mlp = _mlp_fwd(h2, w, i, x1, g_f, S=S)
        saved.append((x, h_t, mix, y, x1, h2_t, mlp))
        x = x2
        if i == 0 and after_layer0 is not None:
            w = after_layer0(x, w)

    dx, dg_final, loss = _final_loss(x, target, w["final_norm_g"])

    grads = {k: [None] * len(w[k]) for k in ("norm_mix_g", "norm_mlp_g", "fox_b", "mla_gq", "mla_gkv")}
    grads.update({k: [None] * L2 for k in ("fox_qkv", "fox_f")})
    grads.update({k: {} for k in STACKED_GRADS})
    grads["final_norm_g"] = dg_final

    def stacked(key, layer, _, a_t, b, **kw):
        group, idx, count = slots[(key, layer)]
        grads[key][group] = _mm(a_t, b, "nn", out_stack=(grads[key].get(group), idx, count), **kw)

    dmod = [None] * L
    for i in reversed(range(L)):
        j = i // 2
        x0, h_t, mix, y, x1, h2_t, (p, u_t, z) = saved[i]
        sh_m, sc_m, g_m, sh_f, sc_f, g_f = (mod[i, s] for s in range(6))
        if i == 0 and before_layer0 is not None:
            g_f = g_f + before_layer0(grads)[0, 0]
        dz, dg_f = _gate_bwd(dx, z, g_f, S=S, name=f"gate_mlp_bwd_{i}")
        stacked("mlp_w2", i, L, u_t, dz, name=f"mlp_w2_grad_{i}")
        dp = _mm(dz, w["mlp_w2"], "nt", name=f"mlp_down_bwd_{i}", b_layer=i, out_dtypes=(BF16,), extras=(p,),
                 epilogue=lambda acc, pv: (acc * (2.0 * jnp.maximum(pv.astype(F32), 0.0)),))
        stacked("mlp_w1", i, L, h2_t, dp, name=f"mlp_w1_grad_{i}", out_split=n_split)
        dh2 = _mm(dp, w["mlp_w1"], "nt", name=f"mlp_up_bwd_{i}", b_layer=i)
        dx1, dsh_f, dsc_f, dgn = _norm_mod_bwd(x1, dh2, dx, w["norm_mlp_g"][i], sc_f, S=S, name=f"norm_mlp_bwd_{i}")
        grads["norm_mlp_g"][i] = dgn
        dy, dg_m = _gate_bwd(dx1, y, g_m, S=S, name=f"gate_mix_bwd_{i}")
        if i % 2 == 0:
            qx, kvx, o, lse, o_t, fl = mix
            stacked("fox_out", j, L2, o_t, dy, name=f"fox_out_grad_{i}")
            do = _mm(dy, w["fox_out"], "nt", name=f"fox_out_bwd_{i}", b_layer=j, out_dtypes=(BF16,))
            dqx, dkvx, csum, rsum = _attn_bwd(qx, kvx, o, lse, do, S=S, scale=1.0, ew=FOX_EXTRA,
                                              name=f"fox_attn_bwd_{i}", bias_grad=True)
            n_heads = D // HEAD_DIM
            csum = jnp.pad(csum.reshape(T, n_heads, HEAD_DIM)[:, :, 0], ((0, 0), (0, LANES - n_heads)))
            rsum = jnp.transpose(rsum[:, :, :2, :], (0, 3, 1, 2)).reshape(T, n_heads)
            rsum = jnp.pad(rsum, ((0, 0), (0, LANES - n_heads)))
            dqkv, dfl, db = _fox_unprep(dqx, dkvx, csum, rsum, fl, w["fox_b"][j], S=S, D=D, name=f"fox_unprep_{i}")
            grads["fox_b"][j] = db
            grads["fox_qkv"][j] = _mm(h_t, dqkv, "nn", name=f"fox_qkv_grad_{i}")
            grads["fox_f"][j] = _mm(h_t, dfl, "nn", name=f"fox_f_grad_{i}")
            dh_f = _mm(dfl, w["fox_f"], "nt", name=f"fox_f_bwd_{i}", b_layer=j)
            dh = _mm(dqkv, w["fox_qkv"], "nt", name=f"fox_qkv_bwd_{i}", b_layer=j, extras=(dh_f,),
                     epilogue=lambda acc, e: (acc + e,))
        else:
            qx, kvx, o, lse, o_t, lat, cqr_t = mix
            Rq = lat.shape[1] - 2 * LANES
            stacked("mla_out", j, L2, o_t, dy, name=f"mla_out_grad_{i}")
            do = _mm(dy, w["mla_out"], "nt", name=f"mla_out_bwd_{i}", b_layer=j, out_dtypes=(BF16,))
            dqx, dkvx = _attn_bwd(qx, kvx, o, lse, do, S=S, scale=(HEAD_DIM + ROPE_DIM) ** -0.5, ew=ROPE_DIM,
                                  name=f"mla_attn_bwd_{i}")
            dqpre = _unrope(dqx, cos_t, sin_s)
            stacked("mla_uq", j, L2, cqr_t[:Rq], dqpre, name=f"mla_uq_grad_{i}", out_split=n_split)
            stacked("mla_ukv", j, L2, cqr_t[Rq:], dkvx, name=f"mla_ukv_grad_{i}", tn=PAIR_KV, out_split=n_split)
            dcq = _mm(dqpre, w["mla_uq"], "nt", name=f"mla_uq_bwd_{i}", b_layer=j)
            dckr = _mm(dkvx, w["mla_ukv"], "nt", name=f"mla_ukv_bwd_{i}", b_layer=j, tk=PAIR_KV * 2)
            dlat, dgq, dgkv = _mla_mid_bwd(lat, dcq, dckr, w["mla_gq"][j], w["mla_gkv"][j], cos_t, sin_s,
                                           name=f"mla_mid_bwd_{i}")
            grads["mla_gq"][j] = dgq
            grads["mla_gkv"][j] = dgkv
            stacked("mla_down", j, L2, h_t, dlat, name=f"mla_down_grad_{i}")
            dh = _mm(dlat, w["mla_down"], "nt", name=f"mla_down_bwd_{i}", b_layer=j)
        dx, dsh_m, dsc_m, dgn = _norm_mod_bwd(x0, dh, dx1, w["norm_mix_g"][i], sc_m, S=S, name=f"norm_mix_bwd_{i}")
        grads["norm_mix_g"][i] = dgn
        dmod[i] = jnp.stack([dsh_m, dsc_m, dg_m, dsh_f, dsc_f, dg_f])
    return loss, dx, jnp.stack(dmod), grads


GATHERED = ("fox_in", "fox_out", "mla_down", "mla_uq", "mla_ukv", "mla_out", "mlp_w1", "mlp_w2")
ROW_SHARDED = ("fox_out", "mla_down", "mla_out", "mlp_w2")


def _shard_layouts(wts):
    dkv = wts["mla_w_dkv"]
    dkv = jnp.pad(dkv, ((0, 0), (0, 0), (0, 2 * LANES - dkv.shape[2])))
    return {
        "fox_in": _pad_lanes(wts["fox_w_in"].astype(BF16)),
        "fox_out": wts["fox_w_out"].astype(BF16),
        "mla_down": jnp.concatenate([wts["mla_w_dq"], dkv], axis=2).astype(BF16),
        "mla_uq": jax.vmap(_uq_to_pairs)(wts["mla_w_uq"].astype(BF16)),
        "mla_ukv": jax.vmap(_ukv_to_pairs)(wts["mla_w_ukv"].astype(BF16)),
        "mla_out": wts["mla_w_out"].astype(BF16),
        "mlp_w1": wts["mlp_w1"].astype(BF16),
        "mlp_w2": wts["mlp_w2"].astype(BF16),
    }


def _small_layouts(small):
    return {
        "fox_b": [jnp.pad(b, (0, LANES - b.shape[0]))[None, :] for b in small["fox_b_f"]],
        "mla_gq": [g[None, :] for g in small["mla_q_norm_g"]],
        "mla_gkv": [g[None, :] for g in small["mla_kv_norm_g"]],
        "norm_mix_g": [g[None, :] for g in small["norm_mix_g"]],
        "norm_mlp_g": [g[None, :] for g in small["norm_mlp_g"]],
        "final_norm_g": small["final_norm_g"][None, :],
    }


def _comm_groups(L, L2):
    first = [("fox_in", 0, 1), ("fox_out", 0, 1), ("mlp_w1", 0, 1), ("mlp_w2", 0, 1)]
    rest = [("fox_in", 1, L2 - 1), ("fox_out", 1, L2 - 1), ("mla_down", 0, L2), ("mla_uq", 0, L2),
            ("mla_ukv", 0, L2), ("mla_out", 0, L2), ("mlp_w1", 1, L - 1), ("mlp_w2", 1, L - 1)]
    return {"first": first, "rest": [e for e in rest if e[2] > 0]}


def _layer_slots(groups):
    return {(n, s + l): (g, l, cnt) for g, entries in groups.items() for n, s, cnt in entries for l in range(cnt)}


def _pad_lanes(a):
    cols = a.shape[-1]
    return jnp.pad(a, [(0, 0)] * (a.ndim - 1) + [(0, -cols % LANES)])


def _weight_views(name, gathered, D, n_fox_heads):
    n, ns, rows, cols = gathered.shape
    if name == "fox_in":
        true_cols = (3 * D + n_fox_heads) // ns
        fox = jnp.concatenate([gathered[:, k, :, :true_cols] for k in range(ns)], axis=-1)
        return {"fox_qkv": fox[:, :, :3 * D], "fox_f": _pad_lanes(fox[:, :, 3 * D:])}
    if name in ROW_SHARDED:
        return {name: gathered.reshape(n, ns * rows, cols)}
    return {name: gathered}


def _grad_pieces(name, g, qkv_f, n_fox_heads, ns):
    if name == "fox_in":
        fox = jnp.stack([jnp.concatenate([a, b[:, :n_fox_heads]], axis=1) for a, b in qkv_f])
        cols = fox.shape[2] // ns
        return jnp.stack([_pad_lanes(fox[:, :, k * cols:(k + 1) * cols]) for k in range(ns)], axis=1)
    if name in ROW_SHARDED:
        return g.reshape(g.shape[0], ns, g.shape[1] // ns, g.shape[2])
    return g


def _small_grads(g, n_fox_heads):
    return {
        "norm_mix_g": jnp.concatenate(g["norm_mix_g"], axis=0),
        "norm_mlp_g": jnp.concatenate(g["norm_mlp_g"], axis=0),
        "final_norm_g": g["final_norm_g"][0],
        "fox_b_f": jnp.concatenate(g["fox_b"], axis=0)[:, :n_fox_heads],
        "mla_q_norm_g": jnp.concatenate(g["mla_gq"], axis=0),
        "mla_kv_norm_g": jnp.concatenate(g["mla_gkv"], axis=0),
    }


def _silu(c):
    return c * (1.0 / (1.0 + jnp.exp(-c)))


def _ada_fwd(c_all, ada_w, ada_b_cols):
    L, D, C = ada_w.shape
    Bg = c_all.shape[0]
    tc = _tile(C, 512)

    def body(c_ref, w_ref, b_ref, o_ref):
        ca = _silu(c_ref[...]).astype(BF16)
        o_ref[...] = jnp.dot(ca, w_ref[...].astype(BF16), preferred_element_type=F32) + b_ref[...]

    return pl.pallas_call(
        body, name="ada_fwd", grid=(L, C // tc),
        in_specs=[pl.BlockSpec((Bg, D), lambda l, j: (0, 0)), pl.BlockSpec((None, D, tc), lambda l, j: (l, 0, j)),
                  pl.BlockSpec((None, 1, tc), lambda l, j: (l, 0, j))],
        out_specs=pl.BlockSpec((None, Bg, tc), lambda l, j: (l, 0, j)),
        out_shape=jax.ShapeDtypeStruct((L, Bg, C), F32),
        compiler_params=_cparams(("parallel", "parallel")),
    )(c_all, ada_w, ada_b_cols)


def _ada_bwd(c_all, dmod_cols):
    L, Bg, C = dmod_cols.shape
    D = c_all.shape[1]
    tc = _tile(C, 512)

    def body(c_ref, d_ref, o_ref):
        ca = _silu(c_ref[...]).astype(BF16)
        o_ref[...] = _dot_tn(ca, d_ref[...].astype(BF16))

    return pl.pallas_call(
        body, name="ada_bwd", grid=(L, C // tc),
        in_specs=[pl.BlockSpec((Bg, D), lambda l, j: (0, 0)), pl.BlockSpec((None, Bg, tc), lambda l, j: (l, 0, j))],
        out_specs=pl.BlockSpec((None, D, tc), lambda l, j: (l, 0, j)),
        out_shape=jax.ShapeDtypeStruct((L, D, C), F32),
        compiler_params=_cparams(("parallel", "parallel")),
    )(c_all, dmod_cols)


def _adamw_update(w, gv, m, v):
    mn = ADAM_B1 * m + (1.0 - ADAM_B1) * gv
    vn = ADAM_B2 * v + (1.0 - ADAM_B2) * jnp.square(gv)
    m_hat = mn / (1.0 - ADAM_B1 ** ADAM_STEP)
    v_hat = vn / (1.0 - ADAM_B2 ** ADAM_STEP)
    return -ADAM_LR * (m_hat / (jnp.sqrt(v_hat) + ADAM_EPS) + ADAM_WD * w), mn, vn


def _adamw(w, g, m, v, *, name):
    shape = w.shape
    C = shape[-1]
    R = int(np.prod(shape[:-1])) if len(shape) > 1 else 1
    w2, g2, m2, v2 = (a.reshape(R, C) for a in (w, g, m, v))
    tr = _row_tile(R, C)

    def body(w_ref, g_ref, m_ref, v_ref, d_ref, nm_ref, nv_ref):
        d_ref[...], nm_ref[...], nv_ref[...] = _adamw_update(w_ref[...], g_ref[...], m_ref[...], v_ref[...])

    spec = pl.BlockSpec((tr, C), lambda i: (i, 0))
    out = pl.pallas_call(
        body, name=name, grid=(R // tr,), in_specs=[spec] * 4, out_specs=[spec] * 3,
        out_shape=[jax.ShapeDtypeStruct((R, C), F32)] * 3, compiler_params=_cparams(("parallel",)),
    )(w2, g2, m2, v2)
    return tuple(a.reshape(shape) for a in out)


def _adamw_halves(w, g_own, g_peer, m, v, c_idx, *, name):
    L, rows, C = w.shape
    R = rows // 2
    tr = _row_tile(R, C)

    def body(c_ref, w_ref, go_ref, gp_ref, m_ref, v_ref, g_ref, d_ref, nm_ref, nv_ref):
        gv = jnp.where(pl.program_id(1) == c_ref[0], go_ref[...], gp_ref[...])
        g_ref[...] = gv
        d_ref[...], nm_ref[...], nv_ref[...] = _adamw_update(w_ref[...], gv, m_ref[...], v_ref[...])

    full = pl.BlockSpec((None, None, tr, C), lambda l, hh, i, c_ref: (l, hh, i, 0))
    half = pl.BlockSpec((None, tr, C), lambda l, hh, i, c_ref: (l, i, 0))
    grid_spec = pltpu.PrefetchScalarGridSpec(
        num_scalar_prefetch=1, grid=(L, 2, R // tr), in_specs=[full, half, half, full, full], out_specs=[full] * 4)
    split = lambda a: a.reshape(L, 2, R, C)
    out = pl.pallas_call(
        body, name=name, grid_spec=grid_spec, out_shape=[jax.ShapeDtypeStruct((L, 2, R, C), F32)] * 4,
        compiler_params=_cparams(("parallel", "parallel", "parallel")),
    )(c_idx, split(w), g_own, g_peer, split(m), split(v))
    return tuple(a.reshape(w.shape) for a in out)


def _sum_gathered(dm8, sm8):
    n_dev, Bl, R, D = dm8.shape
    Rs = sm8.shape[1]

    def body(dm_ref, sm_ref, ob_ref, os_ref):
        acc_b = jnp.zeros((R, D), F32)
        acc_s = jnp.zeros((Rs, D), F32)
        for d in range(n_dev):
            for b in range(Bl):
                acc_b = acc_b + dm_ref[d, b]
            acc_s = acc_s + sm_ref[d]
        ob_ref[...] = acc_b
        os_ref[...] = acc_s

    return pl.pallas_call(
        body, name="sum_gathered",
        out_shape=[jax.ShapeDtypeStruct((R, D), F32), jax.ShapeDtypeStruct((Rs, D), F32)],
        compiler_params=_cparams(None),
    )(dm8, sm8)


N_DEV = 8
N_CHIP = 4
ANY = pl.BlockSpec(memory_space=pl.ANY)
HBM = pl.BlockSpec(memory_space=pltpu.HBM)
SEM = pl.BlockSpec(memory_space=pltpu.SEMAPHORE)
DATAFLOW = pltpu.SideEffectType.DATAFLOW_SIDE_EFFECTING


def _mesh_pos():
    return lax.axis_index("x"), lax.axis_index("y"), lax.axis_index("c")


def _all_gather8(block, *, name, in_vmem):
    R, W = block.shape

    def body(x_ref, out_ref, send_sems, recv_sems, local_sem):
        x, y, c = _mesh_pos()
        me, sibling = (x, y, c), (x, y, 1 - c)
        chips = [(1 - x, y), (x, 1 - y), (1 - x, 1 - y)]

        def slot(px, py, pc):
            return out_ref.at[4 * px + 2 * py + pc]

        def copy(k, blk, to, src=None):
            return pltpu.make_async_remote_copy(
                src_ref=slot(*blk) if src is None else src, dst_ref=slot(*blk),
                send_sem=send_sems.at[k], recv_sem=recv_sems.at[k], device_id=to, device_id_type=MESH_ID)

        mine = pltpu.make_async_copy(x_ref, slot(*me), local_sem)
        mine.start()
        first = [copy(0, me, sibling, src=x_ref)]
        first += [copy(1 + j, me, (*chip, c), src=x_ref) for j, chip in enumerate(chips)]
        for cp in first:
            cp.start()
        passed = [copy(4 + j, (*chip, c), sibling) for j, chip in enumerate(chips)]
        for j, chip in enumerate(chips):
            copy(1 + j, (*chip, c), me).wait_recv()
            passed[j].start()
        copy(0, sibling, me).wait_recv()
        for j, chip in enumerate(chips):
            copy(4 + j, (*chip, 1 - c), me).wait_recv()
        for cp in first + passed:
            cp.wait_send()
        mine.wait()

    space = pl.BlockSpec(memory_space=pltpu.VMEM) if in_vmem else ANY
    return pl.pallas_call(
        body, name=name, out_shape=jax.ShapeDtypeStruct((N_DEV, R, W), block.dtype),
        in_specs=[space], out_specs=space,
        scratch_shapes=[pltpu.SemaphoreType.DMA((7,)), pltpu.SemaphoreType.DMA((7,)), pltpu.SemaphoreType.DMA],
        compiler_params=pltpu.CompilerParams(vmem_limit_bytes=VMEM_LIMIT_V7X),
    )(block)


def _comm_call(body, arrays, out_shapes, n_sems, *, name):
    return pl.pallas_call(
        body, name=name, out_shape=out_shapes, in_specs=[ANY] * len(arrays), out_specs=[ANY] * len(out_shapes),
        scratch_shapes=[pltpu.SemaphoreType.DMA((n_sems,)), pltpu.SemaphoreType.DMA((n_sems,)),
                        pltpu.SemaphoreType.DMA((len(arrays),))],
    )(*arrays)


def _gather_weights(shards, *, name):
    n = len(shards)

    def body(*refs):
        xs, outs = refs[:n], refs[n:2 * n]
        send_sems, recv_sems, local_sems = refs[2 * n:]
        x, y, c = _mesh_pos()
        me, sibling = (x, y, c), (x, y, 1 - c)
        chips = [(1 - x, y), (x, 1 - y), (1 - x, 1 - y)]
        waits = []
        for i in range(n):
            nl = shards[i].shape[0]
            own = xs[i].at[pl.ds(0, nl), c]

            def slot(px, py, pc, i=i, nl=nl):
                return outs[i].at[pl.ds(0, nl), 2 * px + py, pc]

            def copy(k, blk, to, src=None, i=i, slot=slot):
                return pltpu.make_async_remote_copy(
                    src_ref=slot(*blk) if src is None else src, dst_ref=slot(*blk),
                    send_sem=send_sems.at[7 * i + k], recv_sem=recv_sems.at[7 * i + k], device_id=to,
                    device_id_type=MESH_ID)

            mine = pltpu.make_async_copy(own, slot(*me), local_sems.at[i])
            mine.start()
            first = [copy(0, me, sibling, src=own)]
            first += [copy(1 + j, me, (*chip, c), src=own) for j, chip in enumerate(chips)]
            for cp in first:
                cp.start()
            waits.append((copy, mine, first))
        for copy, mine, first in waits:
            passed = [copy(4 + j, (*chip, c), sibling) for j, chip in enumerate(chips)]
            for j, chip in enumerate(chips):
                copy(1 + j, (*chip, c), me).wait_recv()
                passed[j].start()
            copy(0, sibling, me).wait_recv()
            for j, chip in enumerate(chips):
                copy(4 + j, (*chip, 1 - c), me).wait_recv()
            for cp in first + passed:
                cp.wait_send()
            mine.wait()

    out_shapes = [jax.ShapeDtypeStruct((s.shape[0], N_CHIP) + s.shape[1:], s.dtype) for s in shards]
    return _comm_call(body, shards, out_shapes, 7 * n, name=name)


def _place_own(shard, chip_idx, c_idx, *, name):
    n, _, rows, cols = shard.shape
    tr = _row_tile(rows, cols)

    def body(k_ref, c_ref, x_ref, o_ref):
        o_ref[...] = x_ref[...]

    grid_spec = pltpu.PrefetchScalarGridSpec(
        num_scalar_prefetch=2, grid=(n, rows // tr),
        in_specs=[pl.BlockSpec((None, None, tr, cols), lambda l, i, k_ref, c_ref: (l, c_ref[0], i, 0))],
        out_specs=pl.BlockSpec((None, None, None, tr, cols), lambda l, i, k_ref, c_ref: (l, k_ref[0], c_ref[0], i, 0)))
    return pl.pallas_call(
        body, name=name, grid_spec=grid_spec,
        out_shape=jax.ShapeDtypeStruct((n, N_CHIP, 2, rows, cols), shard.dtype),
        compiler_params=_cparams(("parallel", "parallel")),
    )(chip_idx, c_idx, shard)


def _gather_copies(x_refs, land_refs, send_sems, recv_sems):
    x, y, c = _mesh_pos()
    k_me = 2 * x + y
    targets = [(x, y, 1 - c), (1 - x, y, c), (x, 1 - y, c), (1 - x, 1 - y, c)]
    copies = []
    for i, (x_ref, land_ref) in enumerate(zip(x_refs, land_refs)):
        nl = x_ref.shape[0]
        for j, to in enumerate(targets):
            copies.append(pltpu.make_async_remote_copy(
                src_ref=x_ref.at[pl.ds(0, nl), c], dst_ref=land_ref.at[pl.ds(0, nl), k_me, c],
                send_sem=send_sems.at[4 * i + j], recv_sem=recv_sems.at[4 * i + j], device_id=to,
                device_id_type=MESH_ID))
    return copies


def _split_start(copies_fn, srcs, lands, after, *, name, sems_per_array):
    n = len(srcs)

    def body(*refs):
        send_sems, recv_sems = refs[2 * n + 1], refs[2 * n + 2]
        for cp in copies_fn(refs[:n], refs[n:2 * n], send_sems, recv_sems):
            cp.start()
        refs[-1][...] = jnp.zeros_like(refs[-1])

    operands = [pltpu.with_memory_space_constraint(a, pltpu.HBM) for a in list(srcs) + list(lands)]
    n_sems = sems_per_array * n
    out_shape = ([pltpu.SemaphoreType.DMA((n_sems,)), pltpu.SemaphoreType.DMA((n_sems,))]
                 + [pltpu.HBM(a.shape, a.dtype) for a in operands] + [jax.ShapeDtypeStruct((8, LANES), F32)])
    res = pl.pallas_call(
        body, name=name, out_shape=out_shape, in_specs=[HBM] * (2 * n) + [ANY],
        out_specs=[SEM, SEM] + [HBM] * (2 * n) + [pl.BlockSpec(memory_space=pltpu.VMEM)],
        input_output_aliases={i: 2 + i for i in range(2 * n)},
        compiler_params=pltpu.CompilerParams(has_side_effects=DATAFLOW),
    )(*operands, after)
    return res[0], res[1], list(res[2:2 + n]), list(res[2 + n:2 + 2 * n]), res[-1]


def _split_wait(copies_fn, send_sems, recv_sems, srcs, lands, after, *, name):
    n = len(srcs)

    def body(*refs):
        for cp in copies_fn(refs[:n], refs[n:2 * n], refs[2 * n], refs[2 * n + 1]):
            cp.wait_send()
            cp.wait_recv()

    res = pl.pallas_call(
        body, name=name, out_shape=[pltpu.HBM(a.shape, a.dtype) for a in list(srcs) + list(lands)],
        in_specs=[HBM] * (2 * n) + [SEM, SEM, ANY], out_specs=[HBM] * (2 * n),
        input_output_aliases={i: i for i in range(2 * n)},
        compiler_params=pltpu.CompilerParams(has_side_effects=DATAFLOW),
    )(*srcs, *lands, send_sems, recv_sems, after)
    return list(res[:n]), list(res[n:])


def _gather_forward(lands, *, name):
    n = len(lands)

    def body(*refs):
        xs = refs[:n]
        send_sems, recv_sems, _ = refs[2 * n:]
        x, y, c = _mesh_pos()
        chips = [(1 - x, y), (x, 1 - y), (1 - x, 1 - y)]
        copies = []
        for i in range(n):
            nl = lands[i].shape[0]
            for j, (cx, cy) in enumerate(chips):
                here = xs[i].at[pl.ds(0, nl), 2 * cx + cy, c]
                cp = pltpu.make_async_remote_copy(
                    src_ref=here, dst_ref=here, send_sem=send_sems.at[3 * i + j], recv_sem=recv_sems.at[3 * i + j],
                    device_id=(x, y, 1 - c), device_id_type=MESH_ID)
                cp.start()
                copies.append(cp)
        for cp in copies:
            cp.wait()

    return pl.pallas_call(
        body, name=name, out_shape=[jax.ShapeDtypeStruct(a.shape, a.dtype) for a in lands],
        in_specs=[ANY] * n, out_specs=[ANY] * n, input_output_aliases={i: i for i in range(n)},
        scratch_shapes=[pltpu.SemaphoreType.DMA((3 * n,)), pltpu.SemaphoreType.DMA((3 * n,)),
                        pltpu.SemaphoreType.DMA((1,))],
    )(*lands)


def _pair_exchange(gs, *, name):
    n = len(gs)

    def body(*refs):
        xs, outs = refs[:n], refs[n:2 * n]
        send_sems, recv_sems, _ = refs[2 * n:]
        x, y, c = _mesh_pos()
        copies = []
        for i in range(n):
            nl, ns = gs[i].shape[:2]
            cp = pltpu.make_async_remote_copy(
                src_ref=xs[i].at[pl.ds(0, nl), pl.ds(0, ns), 1 - c], dst_ref=outs[i], send_sem=send_sems.at[i],
                recv_sem=recv_sems.at[i], device_id=(x, y, 1 - c), device_id_type=MESH_ID)
            cp.start()
            copies.append(cp)
        for cp in copies:
            cp.wait()

    out_shapes = [jax.ShapeDtypeStruct(g.shape[:2] + g.shape[3:], g.dtype) for g in gs]
    return _comm_call(body, gs, out_shapes, n, name=name)


def _chip_copies(p_refs, land_refs, send_sems, recv_sems):
    x, y, c = _mesh_pos()
    k_me = 2 * x + y
    chips = [(1 - x, y), (x, 1 - y), (1 - x, 1 - y)]
    copies = []
    for i, (p_ref, land_ref) in enumerate(zip(p_refs, land_refs)):
        nl = p_ref.shape[0]
        for j, (cx, cy) in enumerate(chips):
            copies.append(pltpu.make_async_remote_copy(
                src_ref=p_ref.at[pl.ds(0, nl), 2 * cx + cy], dst_ref=land_ref.at[k_me],
                send_sem=send_sems.at[3 * i + j], recv_sem=recv_sems.at[3 * i + j],
                device_id=(cx, cy, c), device_id_type=MESH_ID))
    return copies


def _chip_landing(ps):
    return [lax.empty((p.shape[1], p.shape[0]) + p.shape[2:], p.dtype) for p in ps]


def _chip_exchange(ps, *, name):
    n = len(ps)

    def body(*refs):
        send_sems, recv_sems, _ = refs[2 * n:]
        copies = _chip_copies(refs[:n], refs[n:2 * n], send_sems, recv_sems)
        for cp in copies:
            cp.start()
        for cp in copies:
            cp.wait()

    out_shapes = [jax.ShapeDtypeStruct((p.shape[1], p.shape[0]) + p.shape[2:], p.dtype) for p in ps]
    return _comm_call(body, ps, out_shapes, 3 * n, name=name)


def _pair_swap(ss, *, name):
    n = len(ss)

    def body(*refs):
        xs, outs = refs[:n], refs[n:2 * n]
        send_sems, recv_sems, _ = refs[2 * n:]
        x, y, c = _mesh_pos()
        copies = []
        for i in range(n):
            cp = pltpu.make_async_remote_copy(src_ref=xs[i], dst_ref=outs[i], send_sem=send_sems.at[i],
                                              recv_sem=recv_sems.at[i], device_id=(x, y, 1 - c),
                                              device_id_type=MESH_ID)
            cp.start()
            copies.append(cp)
        for cp in copies:
            cp.wait()

    out_shapes = [jax.ShapeDtypeStruct(s.shape, s.dtype) for s in ss]
    return _comm_call(body, ss, out_shapes, n, name=name)


def _row_tile(rows, cols):
    tr = rows
    while tr * cols > 256 * 1024 and tr % 16 == 0:
        tr //= 2
    return tr


def _pair_add(g, recv, c_idx, *, name):
    n, ns, _, rows, W = g.shape
    tr = _row_tile(rows, W)

    def body(c_ref, g_ref, r_ref, o_ref):
        o_ref[...] = (g_ref[...] + r_ref[...]).astype(BF16)

    piece = pl.BlockSpec((None, tr, W), lambda p, i, c_ref: (p, i, 0))
    grid_spec = pltpu.PrefetchScalarGridSpec(
        num_scalar_prefetch=1, grid=(n * ns, rows // tr),
        in_specs=[pl.BlockSpec((None, None, tr, W), lambda p, i, c_ref: (p, c_ref[0], i, 0)), piece],
        out_specs=piece)
    out = pl.pallas_call(
        body, name=name, grid_spec=grid_spec, out_shape=jax.ShapeDtypeStruct((n * ns, rows, W), BF16),
        compiler_params=_cparams(("parallel", "parallel")),
    )(c_idx, g.reshape(n * ns, 2, rows, W), recv.reshape(n * ns, rows, W))
    return out.reshape(n, ns, rows, W)


def _sum_pieces(land, own, chip_idx, *, name):
    n, nl, A, W = land.shape
    tr = _row_tile(A, W)

    def body(k_ref, l_ref, o_ref, out_ref):
        acc = jnp.zeros(out_ref.shape, F32)
        for k in range(n):
            acc = acc + jnp.where(k == k_ref[0], o_ref[...], l_ref[k]).astype(F32)
        out_ref[...] = acc

    grid_spec = pltpu.PrefetchScalarGridSpec(
        num_scalar_prefetch=1, grid=(nl, A // tr),
        in_specs=[pl.BlockSpec((n, None, tr, W), lambda l, i, k_ref: (0, l, i, 0)),
                  pl.BlockSpec((None, None, tr, W), lambda l, i, k_ref: (l, k_ref[0], i, 0))],
        out_specs=pl.BlockSpec((None, tr, W), lambda l, i, k_ref: (l, i, 0)))
    return pl.pallas_call(
        body, name=name, grid_spec=grid_spec, out_shape=jax.ShapeDtypeStruct((nl, A, W), F32),
        compiler_params=_cparams(("parallel", "parallel")),
    )(chip_idx, land, own)


SMALL = ("norm_mix_g", "norm_mlp_g", "final_norm_g", "fox_b_f", "mla_q_norm_g", "mla_kv_norm_g")
WEIGHT_ORDER = ("ada_w", "ada_b", "norm_mix_g", "norm_mlp_g", "fox_w_in", "fox_b_f", "fox_w_out", "mla_w_dq",
                "mla_q_norm_g", "mla_w_uq", "mla_w_dkv", "mla_kv_norm_g", "mla_w_ukv", "mla_w_out", "mlp_w1",
                "mlp_w2", "final_norm_g")


def _small_rows(vals, D):
    rows = [vals["norm_mix_g"], vals["norm_mlp_g"], vals["final_norm_g"][None, :]]
    for n in ("fox_b_f", "mla_q_norm_g", "mla_kv_norm_g"):
        flat = vals[n].reshape(-1)
        assert flat.shape[0] <= D
        rows.append(jnp.pad(flat, (0, D - flat.shape[0]))[None, :])
    return jnp.concatenate(rows, axis=0)


def _small_unrows(rows, shapes):
    L = shapes["norm_mix_g"][0]
    out = {"norm_mix_g": rows[0:L], "norm_mlp_g": rows[L:2 * L], "final_norm_g": rows[2 * L]}
    for k, n in enumerate(("fox_b_f", "mla_q_norm_g", "mla_kv_norm_g")):
        size = int(np.prod(shapes[n]))
        out[n] = rows[2 * L + 1 + k, :size].reshape(shapes[n])
    return out


def kernel(x, c, positions, ada_w, ada_b, norm_mix_g, norm_mlp_g, fox_w_in, fox_b_f, fox_w_out, mla_w_dq, mla_q_norm_g, mla_w_uq, mla_w_dkv, mla_kv_norm_g, mla_w_ukv, mla_w_out, mlp_w1, mlp_w2, final_norm_g, loss_target, m_ada_w, m_ada_b, m_norm_mix_g, m_norm_mlp_g, m_fox_w_in, m_fox_b_f, m_fox_w_out, m_mla_w_dq, m_mla_q_norm_g, m_mla_w_uq, m_mla_w_dkv, m_mla_kv_norm_g, m_mla_w_ukv, m_mla_w_out, m_mlp_w1, m_mlp_w2, m_final_norm_g, v_ada_w, v_ada_b, v_norm_mix_g, v_norm_mlp_g, v_fox_w_in, v_fox_b_f, v_fox_w_out, v_mla_w_dq, v_mla_q_norm_g, v_mla_w_uq, v_mla_w_dkv, v_mla_kv_norm_g, v_mla_w_ukv, v_mla_w_out, v_mlp_w1, v_mlp_w2, v_final_norm_g):
    args = dict(locals())
    wts = {n: args[n] for n in WEIGHT_ORDER}
    mom = {n: args["m_" + n] for n in WEIGHT_ORDER}
    var = {n: args["v_" + n] for n in WEIGHT_ORDER}
    Bl, S, D = x.shape
    T = Bl * S
    L = ada_w.shape[0]
    C = ada_w.shape[2]
    mx, my, mc = _mesh_pos()
    chip = 2 * mx + my
    dev = 4 * mx + 2 * my + mc
    c_idx = jnp.reshape(mc, (1,)).astype(jnp.int32)
    chip_idx = jnp.reshape(chip, (1,)).astype(jnp.int32)
    small = {n: wts[n] for n in SMALL}
    L2, q_cols = mla_q_norm_g.shape
    n_fox_heads = fox_b_f.shape[1]

    shards = _shard_layouts(wts)
    groups = _comm_groups(L, L2)
    slots = _layer_slots(groups)

    def row_halves(a):
        return a.reshape(a.shape[:-2] + (2, a.shape[-2] // 2, a.shape[-1]))

    def whole_rows(a):
        return a.reshape(a.shape[:2] + (a.shape[2] * a.shape[3], a.shape[4]))

    part = {g: [row_halves(shards[n][s:s + cnt]) for n, s, cnt in entries] for g, entries in groups.items()}
    first = _gather_weights(part["first"], name="gather_first")
    own_placed = [_place_own(a, chip_idx, c_idx, name=f"gather_place_{n}")
                  for a, (n, _, _) in zip(part["rest"], groups["rest"])]
    rest_sems = _split_start(_gather_copies, part["rest"], own_placed, first[0], name="gather_rest_start",
                             sems_per_array=4)

    def layer_weights(w, group, arrays):
        for (n, s, cnt), a in zip(groups[group], arrays):
            for key, view in _weight_views(n, whole_rows(a), D, n_fox_heads).items():
                for l in range(cnt):
                    w[key][s + l] = (view, l)

    w = {key: [None] * L2 for key in ("fox_qkv", "fox_f", "fox_out", "mla_down", "mla_uq", "mla_ukv", "mla_out")}
    w.update({key: [None] * L for key in ("mlp_w1", "mlp_w2")})
    layer_weights(w, "first", first)

    def after_layer0(x_now, w):
        _, landed = _split_wait(_gather_copies, *rest_sems[:4], x_now, name="gather_rest_wait")
        layer_weights(w, "rest", _gather_forward(landed, name="gather_rest_forward"))
        return w

    c_pad = jnp.concatenate([c, jnp.pad(mla_q_norm_g, ((0, 8 - Bl - L2), (0, D - q_cols)))], axis=0)
    c8 = _all_gather8(c_pad, name="gather_c", in_vmem=True)
    c_all = c8[:, :Bl].reshape(N_DEV * Bl, D)
    qg4 = c8.reshape(N_CHIP, 2, 8, D)[:, 0, Bl:Bl + L2, :q_cols]
    small["mla_q_norm_g"] = jnp.transpose(qg4, (1, 0, 2)).reshape(L2, N_CHIP * q_cols)
    ada_b_cols = lax.dynamic_slice_in_dim(ada_b, chip * C, C, axis=1)[:, None, :]
    mod_cols = _ada_fwd(c_all, ada_w, ada_b_cols)
    mod8 = _all_gather8(mod_cols.reshape(L * N_DEV * Bl, C), name="gather_mod", in_vmem=True)
    mod4 = mod8.reshape(N_CHIP, 2, L, N_DEV * Bl, C)[:, 0]
    mod_me = lax.dynamic_slice_in_dim(mod4, dev * Bl, Bl, axis=2)
    mod = jnp.transpose(mod_me, (1, 2, 0, 3)).reshape(L, Bl, 6, D)
    mod = jnp.transpose(mod, (0, 2, 1, 3))[:, :, :, None, :]

    w.update(_small_layouts(small))
    mod = mod + rest_sems[4][0, 0]
    pending = {}

    def grad_pieces(group, g_now):
        out = []
        for n, s, cnt in groups[group]:
            qkv_f = [(g_now["fox_qkv"][j], g_now["fox_f"][j]) for j in range(s, s + cnt)] if n == "fox_in" else None
            stacked_g = None if n == "fox_in" else g_now[n][group]
            out.append(row_halves(_grad_pieces(n, stacked_g, qkv_f, n_fox_heads, N_CHIP)))
        return out

    def pair_sums(group, g_now):
        big = grad_pieces(group, g_now)
        sibling = _pair_exchange(big, name=f"grad_pair_exchange_{group}")
        return [_pair_add(a, r, c_idx, name=f"grad_pair_add_{group}_{n}")
                for (n, _, _), a, r in zip(groups[group], big, sibling)]

    def before_layer0(g_now):
        ps = pair_sums("rest", g_now)
        pending["rest"] = _split_start(_chip_copies, ps, _chip_landing(ps), chip_idx, name="grad_exchange_rest_start",
                                       sems_per_array=3)
        return pending["rest"][4]

    half = ROPE_DIM // 2
    inv_freq = ROPE_THETA ** (-jnp.arange(0, ROPE_DIM, 2, dtype=F32) / ROPE_DIM)
    lane = np.arange(LANES)
    inv_freq_row = jnp.tile(inv_freq, LANES // half)[None, :]
    sign_row = jnp.asarray(np.where(lane < 2 * ROPE_DIM, np.where(lane % ROPE_DIM < half, -1.0, 1.0), 0.0), F32)[None, :]
    pos_f = positions.astype(F32).reshape(T, 1)
    loss_row, grad_x, dmod, g = _local_step(x.reshape(T, D), loss_target.reshape(T, D), pos_f, inv_freq_row, sign_row,
                                            mod, w, slots, S=S, after_layer0=after_layer0, before_layer0=before_layer0)
    g_small = _small_grads(g, n_fox_heads)
    ps_first = pair_sums("first", g)
    pending["first"] = _split_start(_chip_copies, ps_first, _chip_landing(ps_first), chip_idx,
                                    name="grad_exchange_first_start", sems_per_array=3)

    Rs = -(-(2 * L + 5) // 8) * 8
    srows = jnp.concatenate([_small_rows(g_small, D), jnp.pad(loss_row, ((0, 0), (0, D - LANES)))], axis=0)
    srows = jnp.pad(srows, ((0, Rs - srows.shape[0]), (0, 0)))
    drows = jnp.transpose(dmod[:, :, :, 0, :], (2, 0, 1, 3)).reshape(Bl * L * 6, D)
    both8 = _all_gather8(jnp.concatenate([drows, srows], axis=0), name="gather_small", in_vmem=True)
    dm8 = both8[:, :Bl * L * 6].reshape(N_DEV, Bl, L * 6, D)
    sm8 = both8[:, Bl * L * 6:]
    adb_rows, small_sum = _sum_gathered(dm8, sm8)
    grad_ada_b = adb_rows.reshape(L, 6 * D)
    loss = small_sum[2 * L + 4, 0]
    small_shapes = {n: (wts[n].shape if n != "mla_q_norm_g" else (wts[n].shape[0], N_CHIP * q_cols)) for n in SMALL}
    gs = _small_unrows(small_sum, small_shapes)
    gs["mla_q_norm_g"] = lax.dynamic_slice_in_dim(gs["mla_q_norm_g"], chip * q_cols, q_cols, axis=1)

    dmod16 = jnp.transpose(dm8.reshape(N_DEV, Bl, L, 6 * D), (2, 0, 1, 3)).reshape(L, N_DEV * Bl, 6 * D)
    dmod_cols = lax.dynamic_slice_in_dim(dmod16, chip * C, C, axis=2)
    grad_ada_w = _ada_bwd(c_all, dmod_cols)

    grads = dict(gs)
    grads["ada_w"] = grad_ada_w
    grads["ada_b"] = grad_ada_b
    delta, new_m, new_v = {}, {}, {}
    for n in ("ada_w", "ada_b"):
        delta[n], new_m[n], new_v[n] = _adamw(wts[n], grads[n], mom[n], var[n], name=f"adamw_{n}")
    shard_small_shapes = {n: wts[n].shape for n in SMALL}
    packs = [jnp.pad(_small_rows({n: src[n] for n in SMALL}, D), ((0, Rs - 2 * L - 4), (0, 0)))
             for src in (wts, grads, mom, var)]
    for dst, rows in zip((delta, new_m, new_v), _adamw(*packs, name="adamw_small")):
        dst.update(_small_unrows(rows, shard_small_shapes))

    halves = {}
    for group, after in (("rest", grad_x), ("first", delta["ada_w"])):
        send_sems, recv_sems, ps, lands, _ = pending[group]
        ps, lands = _split_wait(_chip_copies, send_sems, recv_sems, ps, lands, after, name=f"grad_exchange_{group}_wait")
        sums = [_sum_pieces(ld, p, chip_idx, name=f"grad_sum_{group}_{n}")
                for (n, _, _), ld, p in zip(groups[group], lands, ps)]
        swapped = _pair_swap(sums, name=f"grad_pair_swap_{group}")
        for (n, _, _), a, b in zip(groups[group], sums, swapped):
            halves[(n, group)] = (a, b)

    def all_layers(n, which):
        return jnp.concatenate([halves[(n, grp)][which] for grp in ("first", "rest") if (n, grp) in halves], axis=0)

    own = {n: all_layers(n, 0) for n in GATHERED}
    peer = {n: all_layers(n, 1) for n in GATHERED}
    for nat, n in (("fox_w_in", "fox_in"), ("fox_w_out", "fox_out"), ("mla_w_out", "mla_out"), ("mlp_w1", "mlp_w1"),
                   ("mlp_w2", "mlp_w2")):
        cols = wts[nat].shape[-1]
        res = _adamw_halves(_pad_lanes(wts[nat]), own[n], peer[n], _pad_lanes(mom[nat]), _pad_lanes(var[nat]), c_idx,
                            name=f"adamw_{nat}")
        grads[nat], delta[nat], new_m[nat], new_v[nat] = (a[..., :cols] for a in res)
    joined = {n: jnp.concatenate([jnp.where(mc == 0, own[n], peer[n]), jnp.where(mc == 0, peer[n], own[n])], axis=1)
              for n in ("mla_down", "mla_uq", "mla_ukv")}
    rq = mla_w_dq.shape[-1]
    grads["mla_w_dq"] = joined["mla_down"][:, :, :rq]
    grads["mla_w_dkv"] = joined["mla_down"][:, :, rq:rq + KV_RANK + ROPE_DIM]
    grads["mla_w_uq"] = jax.vmap(_uq_from_pairs)(joined["mla_uq"])
    grads["mla_w_ukv"] = jax.vmap(_ukv_from_pairs)(joined["mla_ukv"])
    for n in ("mla_w_dq", "mla_w_dkv", "mla_w_uq", "mla_w_ukv"):
        delta[n], new_m[n], new_v[n] = _adamw(wts[n], grads[n], mom[n], var[n], name=f"adamw_{n}")

    return (loss, grad_x.reshape(Bl, S, D), *[grads[n] for n in WEIGHT_ORDER], *[delta[n] for n in WEIGHT_ORDER],
            *[new_m[n] for n in WEIGHT_ORDER], *[new_v[n] for n in WEIGHT_ORDER])
```

```python
import functools

import numpy as np
import jax
import jax.numpy as jnp
from jax import lax
from jax.experimental import pallas as pl
from jax.experimental.pallas import tpu as pltpu

F32 = jnp.float32
BF16 = jnp.bfloat16
MESH_ID = pl.DeviceIdType.MESH

NORM_EPS = 1e-6
ROPE_THETA = 10000.0
HEAD_DIM = 64
ROPE_DIM = 32
KV_RANK = 128
FOX_EXTRA = 6
PAIR_Q = 256
PAIR_KV = 384
LANES = 128
ADAM_LR = 0.001
ADAM_B1 = 0.9
ADAM_B2 = 0.999
ADAM_EPS = 1e-08
ADAM_WD = 0.01
ADAM_STEP = 10
VMEM_LIMIT_V7X = 48 * 1024 * 1024
MM_VMEM_BUDGET = 36 * 1024 * 1024
NEG_BIG = -1e30
ATTN_UNROLL = 4

BIG_WEIGHTS = (("fox_w_in", 2), ("fox_w_out", 1), ("mla_w_dq", 1), ("mla_w_uq", 2), ("mla_w_dkv", 1),
               ("mla_w_ukv", 2), ("mla_w_out", 1), ("mlp_w1", 2), ("mlp_w2", 1))


def _cparams(sem=None):
    return pltpu.CompilerParams(dimension_semantics=sem, vmem_limit_bytes=VMEM_LIMIT_V7X)


def _tile(n, want):
    if n <= want:
        return n
    for t in range(want - want % LANES, 0, -LANES):
        if n % t == 0:
            return t
    raise ValueError((n, want))


def _mm(a, b, mode, *, name, out_dtypes=(F32,), epilogue=None, extras=(), rowvecs=(), tables=(),
        seq=None, a_off=0, a_sz=None, b_layer=None, out_stack=None, out_split=0, out_t=(), tm=1024, tn=1024,
        tk=2048):
    if isinstance(b, (list, tuple)):
        b, b_layer = b[b_layer]
    b_rows, b_cols = b.shape[-2], b.shape[-1]
    n_split = b.shape[1] if b.ndim == 4 else 1
    assert mode in ("nn", "nt")
    if mode == "nn":
        M, K, N = a.shape[0], b_rows, b_cols * n_split
    else:
        M, K, N = a.shape[0], b_cols * n_split, b_rows
    assert a_sz is None or a_sz == K
    tm = _tile(seq if rowvecs else M, tm)
    n_piece = N // max(out_split, n_split if mode == "nn" else 1, 1)
    tn = _tile(n_piece, tn)
    tk = _tile(K // (n_split if mode == "nt" else 1), tk)
    ne, nr, nt_ = len(extras), len(rowvecs), len(tables)
    no = len(out_dtypes)

    def vmem_estimate():
        blocks = tm * tk * a.dtype.itemsize + tk * tn * b.dtype.itemsize
        blocks += tm * tn * (sum(e.dtype.itemsize for e in extras) + sum(jnp.dtype(d).itemsize for d in out_dtypes))
        return 2 * blocks + 2 * tm * tn * 4

    while vmem_estimate() > MM_VMEM_BUDGET and max(tm, tn) > 256:
        if tn >= tm:
            tn //= 2
        else:
            tm //= 2
    nk = K // tk

    assert a_off % tk == 0
    a_spec = pl.BlockSpec((tm, tk), lambda i, j, k: (i, k + a_off // tk))
    dims = (((1,), (0,)), ((), ())) if mode == "nn" else (((1,), (1,)), ((), ()))
    lead = () if b.ndim == 2 else (b_layer,)
    sq = (None,) * (b.ndim - 2)
    if mode == "nt":
        kb = b_cols // tk
        if b.ndim == 4:
            b_spec = pl.BlockSpec(sq + (tn, tk), lambda i, j, k: lead + (k // kb, j, k % kb))
        else:
            b_spec = pl.BlockSpec(sq + (tn, tk), lambda i, j, k: lead + (j, k))
    else:
        nb = b_cols // tn
        if b.ndim == 4:
            b_spec = pl.BlockSpec(sq + (tk, tn), lambda i, j, k: lead + (j // nb, k, j % nb))
        else:
            b_spec = pl.BlockSpec(sq + (tk, tn), lambda i, j, k: lead + (k, j))
    in_specs = [a_spec, b_spec]
    in_specs += [pl.BlockSpec((tm, tn), lambda i, j, k: (i, j)) for _ in extras]
    if rowvecs:
        assert seq % tm == 0
        per = seq // tm
        in_specs += [pl.BlockSpec((None, 1, tn), lambda i, j, k: (i // per, 0, j)) for _ in rowvecs]
    in_specs += [pl.BlockSpec((tm, LANES), lambda i, j, k: (i, 0)) for _ in tables]
    operands = [a, b, *extras, *rowvecs, *tables]
    aliases = {}
    transposed = tuple(out_t) + (False,) * (no - len(out_t))
    if out_stack is None:
        out_specs = [pl.BlockSpec((tn, tm), lambda i, j, k: (j, i)) if t else pl.BlockSpec((tm, tn), lambda i, j, k: (i, j))
                     for t in transposed]
        out_shape = [jax.ShapeDtypeStruct((N, M) if t else (M, N), d) for d, t in zip(out_dtypes, transposed)]
    else:
        prev, layer, n_layers = out_stack
        assert no == 1
        if out_split:
            ob = n_piece // tn
            out_specs = [pl.BlockSpec((None, None, tm, tn), lambda i, j, k: (layer, j // ob, i, j % ob))]
            out_shape = [jax.ShapeDtypeStruct((n_layers, out_split, M, n_piece), out_dtypes[0])]
        else:
            out_specs = [pl.BlockSpec((None, tm, tn), lambda i, j, k: (layer, i, j))]
            out_shape = [jax.ShapeDtypeStruct((n_layers, M, N), out_dtypes[0])]
        if prev is not None:
            in_specs.append(pl.BlockSpec(memory_space=pl.ANY))
            aliases = {len(operands): 0}
            operands.append(prev)
    n_in = len(operands)

    def body(*refs):
        a_ref, b_ref = refs[0], refs[1]
        side = refs[2:2 + ne + nr + nt_]
        outs = refs[n_in:n_in + no]

        def finish(acc):
            res = (acc,) if epilogue is None else epilogue(acc, *[r[...] for r in side])
            for o_ref, r, t in zip(outs, res, transposed):
                o_ref[...] = (r.T if t else r).astype(o_ref.dtype)

        part = lax.dot_general(a_ref[...].astype(BF16), b_ref[...].astype(BF16), dims,
                               preferred_element_type=F32)
        if nk == 1:
            finish(part)
        else:
            acc_ref = refs[-1]
            k = pl.program_id(2)

            @pl.when(k == 0)
            def _():
                acc_ref[...] = part

            @pl.when(k > 0)
            def _():
                acc_ref[...] += part

            @pl.when(k == nk - 1)
            def _():
                finish(acc_ref[...])

    res = pl.pallas_call(
        body, name=name, grid=(M // tm, N // tn, nk), in_specs=in_specs, out_specs=out_specs,
        out_shape=out_shape, scratch_shapes=[pltpu.VMEM((tm, tn), F32)] if nk > 1 else [],
        input_output_aliases=aliases,
        compiler_params=_cparams(("parallel", "parallel", "arbitrary")),
    )(*operands)
    return res[0] if no == 1 else tuple(res)


def _rope128(x, cos_t, sin_s):
    lane = lax.broadcasted_iota(jnp.int32, x.shape, 1)
    first = (lane % ROPE_DIM) < (ROPE_DIM // 2)
    swapped = jnp.where(first, pltpu.roll(x, LANES - ROPE_DIM // 2, 1), pltpu.roll(x, ROPE_DIM // 2, 1))
    return x * cos_t + swapped * sin_s


def _rope_pairs(acc, cos_t, sin_s, sign):
    parts = []
    for p in range(acc.shape[1] // PAIR_Q):
        parts.append(acc[:, p * PAIR_Q:p * PAIR_Q + LANES])
        parts.append(_rope128(acc[:, p * PAIR_Q + LANES:(p + 1) * PAIR_Q], cos_t, sign * sin_s))
    return jnp.concatenate(parts, axis=1)


def _rope_tables(pos_f, inv_freq_row, sign_row):
    T = pos_f.shape[0]
    tt = _tile(T, 512)

    def body(p_ref, f_ref, s_ref, cos_ref, sin_ref):
        ang = p_ref[...] * f_ref[...]
        cos_ref[...] = jnp.cos(ang)
        sin_ref[...] = jnp.sin(ang) * s_ref[...]

    return pl.pallas_call(
        body, name="rope_tables", grid=(T // tt,),
        in_specs=[pl.BlockSpec((tt, 1), lambda i: (i, 0)), pl.BlockSpec((1, LANES), lambda i: (0, 0)),
                  pl.BlockSpec((1, LANES), lambda i: (0, 0))],
        out_specs=[pl.BlockSpec((tt, LANES), lambda i: (i, 0))] * 2,
        out_shape=[jax.ShapeDtypeStruct((T, LANES), F32)] * 2,
        compiler_params=_cparams(("parallel",)),
    )(pos_f, inv_freq_row, sign_row)


def _unrope(dqx, cos_t, sin_s):
    T, W = dqx.shape
    tt = _tile(T, 512)

    def body(d_ref, c_ref, s_ref, o_ref):
        o_ref[...] = _rope_pairs(d_ref[...].astype(F32), c_ref[...], s_ref[...], -1.0).astype(BF16)

    return pl.pallas_call(
        body, name="mla_unrope", grid=(T // tt,),
        in_specs=[pl.BlockSpec((tt, W), lambda i: (i, 0)), pl.BlockSpec((tt, LANES), lambda i: (i, 0)),
                  pl.BlockSpec((tt, LANES), lambda i: (i, 0))],
        out_specs=pl.BlockSpec((tt, W), lambda i: (i, 0)),
        out_shape=jax.ShapeDtypeStruct((T, W), BF16),
        compiler_params=_cparams(("parallel",)),
    )(dqx, cos_t, sin_s)


def _row_specs(tt, D, per, n):
    return [pl.BlockSpec((None, 1, D), lambda i: (i // per, 0, 0)) for _ in range(n)]


def _norm_mod(x, gain, sc, sh, *, S, name):
    T, D = x.shape
    tt = _tile(S, 512)
    per = S // tt

    def body(x_ref, g_ref, sc_ref, sh_ref, h_ref, ht_ref):
        xv = x_ref[...]
        r = lax.rsqrt(jnp.mean(xv * xv, axis=-1, keepdims=True) + NORM_EPS)
        h = (xv * r) * g_ref[...] * (1.0 + sc_ref[...]) + sh_ref[...]
        h_ref[...] = h.astype(BF16)
        ht_ref[...] = h.T.astype(BF16)

    return pl.pallas_call(
        body, name=name, grid=(T // tt,),
        in_specs=[pl.BlockSpec((tt, D), lambda i: (i, 0)), pl.BlockSpec((1, D), lambda i: (0, 0))]
        + _row_specs(tt, D, per, 2),
        out_specs=[pl.BlockSpec((tt, D), lambda i: (i, 0)), pl.BlockSpec((D, tt), lambda i: (0, i))],
        out_shape=[jax.ShapeDtypeStruct((T, D), BF16), jax.ShapeDtypeStruct((D, T), BF16)],
        compiler_params=_cparams(("parallel",)),
    )(x, gain, sc, sh)


def _norm_mod_bwd(x, dh, dres, gain, sc, *, S, name):
    T, D = x.shape
    B = T // S
    tt = _tile(S, 512)
    per = S // tt

    def body(x_ref, dh_ref, dres_ref, g_ref, sc_ref, dx_ref, dsh_ref, dsc_ref, dg_ref):
        i = pl.program_id(0)
        xv = x_ref[...]
        dhv = dh_ref[...].astype(F32)
        r = lax.rsqrt(jnp.mean(xv * xv, axis=-1, keepdims=True) + NORM_EPS)
        n = xv * r
        g = g_ref[...]
        one_sc = 1.0 + sc_ref[...]
        dn = dhv * (g * one_sc)
        dx_ref[...] = dres_ref[...] + r * (dn - n * jnp.mean(dn * n, axis=-1, keepdims=True))
        dhn = dhv * n

        @pl.when(i % per == 0)
        def _():
            dsh_ref[...] = jnp.zeros_like(dsh_ref)
            dsc_ref[...] = jnp.zeros_like(dsc_ref)

        @pl.when(i == 0)
        def _():
            dg_ref[...] = jnp.zeros_like(dg_ref)

        dsh_ref[...] += jnp.sum(dhv, axis=0, keepdims=True)
        dsc_ref[...] += jnp.sum(dhn, axis=0, keepdims=True) * g
        dg_ref[...] += jnp.sum(dhn, axis=0, keepdims=True) * one_sc

    return pl.pallas_call(
        body, name=name, grid=(T // tt,),
        in_specs=[pl.BlockSpec((tt, D), lambda i: (i, 0))] * 3 + [pl.BlockSpec((1, D), lambda i: (0, 0))]
        + _row_specs(tt, D, per, 1),
        out_specs=[pl.BlockSpec((tt, D), lambda i: (i, 0))] + _row_specs(tt, D, per, 2)
        + [pl.BlockSpec((1, D), lambda i: (0, 0))],
        out_shape=[jax.ShapeDtypeStruct((T, D), F32), jax.ShapeDtypeStruct((B, 1, D), F32),
                   jax.ShapeDtypeStruct((B, 1, D), F32), jax.ShapeDtypeStruct((1, D), F32)],
        compiler_params=_cparams(("arbitrary",)),
    )(x, dh, dres, gain, sc)


def _gate_bwd(dx, y, g, *, S, name):
    T, D = dx.shape
    B = T // S
    tt = _tile(S, 512)
    per = S // tt

    def body(dx_ref, y_ref, g_ref, dy_ref, dg_ref):
        i = pl.program_id(0)
        dxv = dx_ref[...]
        dy_ref[...] = (dxv * g_ref[...]).astype(BF16)

        @pl.when(i % per == 0)
        def _():
            dg_ref[...] = jnp.zeros_like(dg_ref)

        dg_ref[...] += jnp.sum(dxv * y_ref[...], axis=0, keepdims=True)

    return pl.pallas_call(
        body, name=name, grid=(T // tt,),
        in_specs=[pl.BlockSpec((tt, D), lambda i: (i, 0))] * 2 + _row_specs(tt, D, per, 1),
        out_specs=[pl.BlockSpec((tt, D), lambda i: (i, 0))] + _row_specs(tt, D, per, 1),
        out_shape=[jax.ShapeDtypeStruct((T, D), BF16), jax.ShapeDtypeStruct((B, 1, D), F32)],
        compiler_params=_cparams(("arbitrary",)),
    )(dx, y, g)


def _final_loss(x, target, gain):
    T, D = x.shape
    tt = _tile(T, 512)

    def body(x_ref, t_ref, g_ref, dx_ref, dg_ref, loss_ref):
        i = pl.program_id(0)
        xv = x_ref[...]
        r = lax.rsqrt(jnp.mean(xv * xv, axis=-1, keepdims=True) + NORM_EPS)
        n = xv * r
        g = g_ref[...]
        err = n * g - t_ref[...]
        dy = err * (1.0 / D)
        dn = dy * g
        dx_ref[...] = r * (dn - n * jnp.mean(dn * n, axis=-1, keepdims=True))

        @pl.when(i == 0)
        def _():
            dg_ref[...] = jnp.zeros_like(dg_ref)
            loss_ref[...] = jnp.zeros_like(loss_ref)

        dg_ref[...] += jnp.sum(dy * n, axis=0, keepdims=True)
        loss_ref[...] += jnp.sum(jnp.sum(err * err, axis=-1, keepdims=True), axis=0, keepdims=True) * (0.5 / D)

    return pl.pallas_call(
        body, name="final_loss", grid=(T // tt,),
        in_specs=[pl.BlockSpec((tt, D), lambda i: (i, 0))] * 2 + [pl.BlockSpec((1, D), lambda i: (0, 0))],
        out_specs=[pl.BlockSpec((tt, D), lambda i: (i, 0)), pl.BlockSpec((1, D), lambda i: (0, 0)),
                   pl.BlockSpec((1, LANES), lambda i: (0, 0))],
        out_shape=[jax.ShapeDtypeStruct((T, D), F32), jax.ShapeDtypeStruct((1, D), F32),
                   jax.ShapeDtypeStruct((1, LANES), F32)],
        compiler_params=_cparams(("arbitrary",)),
    )(x, target, gain)


def _head_masks(ew):
    lane = lax.broadcasted_iota(jnp.int32, (1, PAIR_Q), 1)
    m0 = (lane < HEAD_DIM) | ((lane >= LANES) & (lane < LANES + ew))
    m1 = ((lane >= HEAD_DIM) & (lane < LANES)) | ((lane >= LANES + ew) & (lane < LANES + 2 * ew))
    return m0, m1


def _dot_nt(a, b):
    return lax.dot_general(a, b, (((1,), (1,)), ((), ())), preferred_element_type=F32)


def _dot_tn(a, b):
    return lax.dot_general(a, b, (((0,), (0,)), ((), ())), preferred_element_type=F32)


def _lane_halves(x, op):
    acc = x[:, 0:LANES]
    for g in range(1, x.shape[1] // LANES):
        acc = op(acc, x[:, g * LANES:(g + 1) * LANES])
    return acc


def _head_rows(cols_lane_replicated):
    t = cols_lane_replicated.T
    sub = lax.broadcasted_iota(jnp.int32, (8, t.shape[1]), 0)
    return jnp.where(sub == 1, t[HEAD_DIM:HEAD_DIM + 8], t[0:8])


def _attn_trip(n_blocks):
    return ATTN_UNROLL if n_blocks % ATTN_UNROLL == 0 else 2


def _attn_fwd(qx, kvx, *, S, scale, ew, name):
    T = qx.shape[0]
    P = qx.shape[1] // PAIR_Q
    B = T // S
    tq = _tile(S, 256)
    nq = S // tq
    big = _attn_trip(nq)

    def body(q_ref, kv_ref, o_ref, lse_ref, ot_ref, m_sc, l_sc, acc_sc):
        qi = pl.program_id(2)
        q = q_ref[...]
        masks = _head_masks(ew)
        qh = [jnp.where(m, q, jnp.zeros_like(q)) for m in masks]

        def logits(h, k, kj):
            s = _dot_nt(qh[h], k)
            if scale != 1.0:
                s = s * scale
            row = lax.broadcasted_iota(jnp.int32, s.shape, 0)
            col = lax.broadcasted_iota(jnp.int32, s.shape, 1)
            return jnp.where(col - row <= (qi - kj) * tq, s, NEG_BIG)

        def trip(first, count):
            rows = [pl.ds(pl.multiple_of((first + u) * tq, tq), tq) for u in range(count)]
            for h in range(2):
                ss = [logits(h, kv_ref[rows[u], 0:PAIR_Q], first + u) for u in range(count)]
                m_prev = m_sc[h]
                m_elem = m_prev
                for s in ss:
                    m_elem = jnp.maximum(m_elem, _lane_halves(s, jnp.maximum))
                m_new = jnp.broadcast_to(jnp.max(m_elem, axis=1, keepdims=True), (tq, LANES))
                alpha = jnp.exp(m_prev - m_new)
                l = alpha * l_sc[h]
                acc = alpha * acc_sc[h]
                for u, s in enumerate(ss):
                    p = jnp.concatenate([jnp.exp(s[:, g * LANES:(g + 1) * LANES] - m_new)
                                         for g in range(tq // LANES)], axis=1)
                    l = l + _lane_halves(p, jnp.add)
                    acc = acc + jnp.dot(p.astype(BF16), kv_ref[rows[u], PAIR_Q:PAIR_KV], preferred_element_type=F32)
                m_sc[h] = m_new
                l_sc[h] = l
                acc_sc[h] = acc

        m_sc[...] = jnp.full(m_sc.shape, NEG_BIG, F32)
        l_sc[...] = jnp.zeros_like(l_sc)
        acc_sc[...] = jnp.zeros_like(acc_sc)

        def loop_body(t, carry):
            trip(t * big, big)
            return carry

        if big == 2:
            lax.fori_loop(0, (qi + 2) // 2, loop_body, 0)
        else:
            trips = (qi + 2) // big
            lax.fori_loop(0, trips, loop_body, 0)

            @pl.when(qi % big <= 1)
            def _():
                trip(trips * big, 2)

        lane = lax.broadcasted_iota(jnp.int32, (tq, LANES), 1)
        lo = lane < HEAD_DIM
        l = [jnp.sum(l_sc[h], axis=1, keepdims=True) for h in range(2)]
        o = jnp.where(lo, acc_sc[0] / l[0], acc_sc[1] / l[1])
        o_ref[...] = o.astype(BF16)
        ot_ref[...] = o.T.astype(BF16)
        lse_ref[...] = _head_rows(jnp.where(lo, m_sc[0] + jnp.log(l[0]), m_sc[1] + jnp.log(l[1])))

    return pl.pallas_call(
        body, name=name, grid=(B, P, nq),
        in_specs=[pl.BlockSpec((tq, PAIR_Q), lambda b, p, i: (b * nq + i, p)),
                  pl.BlockSpec((S, PAIR_KV), lambda b, p, i: (b, p))],
        out_specs=[pl.BlockSpec((tq, LANES), lambda b, p, i: (b * nq + i, p)),
                   pl.BlockSpec((None, None, 8, tq), lambda b, p, i: (b * nq + i, p, 0, 0)),
                   pl.BlockSpec((LANES, tq), lambda b, p, i: (p, b * nq + i))],
        out_shape=[jax.ShapeDtypeStruct((T, P * LANES), BF16), jax.ShapeDtypeStruct((T // tq, P, 8, tq), F32),
                   jax.ShapeDtypeStruct((P * LANES, T), BF16)],
        scratch_shapes=[pltpu.VMEM((2, tq, LANES), F32)] * 3,
        compiler_params=_cparams(("parallel", "parallel", "arbitrary")),
    )(qx, kvx)


def _attn_bwd(qx, kvx, o, lse, do, *, S, scale, ew, name, bias_grad=False):
    T = qx.shape[0]
    P = qx.shape[1] // PAIR_Q
    B = T // S
    tq = _tile(S, 256)
    nq = S // tq
    big = _attn_trip(nq)

    def body(q_ref, kv_ref, o_ref, lse_ref, do_ref, dq_ref, dkv_ref, *rest):
        kj = pl.program_id(2)
        if bias_grad:
            csum_ref, rsum_ref, dq_sc, delta_sc, dk_sc, dv_sc, cs_sc = rest
            cs_sc[...] = jnp.zeros_like(cs_sc)

            @pl.when(kj == 0)
            def _():
                rsum_ref[...] = jnp.zeros_like(rsum_ref)
        else:
            dq_sc, delta_sc, dk_sc, dv_sc = rest
        masks = _head_masks(ew)
        lane = lax.broadcasted_iota(jnp.int32, (tq, LANES), 1)
        lo = lane < HEAD_DIM
        vmask = [lo, jnp.logical_not(lo)]

        @pl.when(kj == 0)
        def _():
            dq_sc[...] = jnp.zeros_like(dq_sc)
            for c in range(nq):
                rows = pl.ds(c * tq, tq)
                x = do_ref[rows, :].astype(F32) * o_ref[rows, :].astype(F32)
                r0 = jnp.sum(jnp.where(lo, x, 0.0), axis=1, keepdims=True)
                r1 = jnp.sum(jnp.where(lo, 0.0, x), axis=1, keepdims=True)
                delta_sc[c] = _head_rows(jnp.where(lo, r0, r1))

        k = kv_ref[:, 0:PAIR_Q]
        v = kv_ref[:, PAIR_Q:PAIR_KV]
        kh = [jnp.where(m, k, jnp.zeros_like(k)) for m in masks]
        vh = [jnp.where(m, v, jnp.zeros_like(v)) for m in vmask]
        dk_sc[...] = jnp.zeros_like(dk_sc)
        dv_sc[...] = jnp.zeros_like(dv_sc)

        def step(qi):
            rows = pl.ds(pl.multiple_of(qi * tq, tq), tq)
            q = q_ref[rows, :]
            dov = do_ref[rows, :]
            lse8 = lse_ref[qi]
            dl8 = delta_sc[qi]
            for h in range(2):
                st = _dot_nt(kh[h], q)
                if scale != 1.0:
                    st = st * scale
                key = lax.broadcasted_iota(jnp.int32, st.shape, 0)
                qry = lax.broadcasted_iota(jnp.int32, st.shape, 1)
                st = jnp.where(key - qry <= (qi - kj) * tq, st, NEG_BIG)
                pt = jnp.exp(st - lse8[h:h + 1, :])
                dpt = _dot_nt(vh[h], dov)
                dst = pt * (dpt - dl8[h:h + 1, :])
                if bias_grad:
                    cs_sc[h] += _lane_halves(dst, jnp.add)
                    rsum_ref[qi, h:h + 1, :] += jnp.sum(dst, axis=0, keepdims=True)
                if scale != 1.0:
                    dst = dst * scale
                ptb = pt.astype(BF16)
                dstb = dst.astype(BF16)
                dv_sc[h] += jnp.dot(ptb, dov, preferred_element_type=F32)
                dk_sc[h] += jnp.dot(dstb, q, preferred_element_type=F32)
                dq_sc[rows, :] += _dot_tn(dstb, kh[h])

        def loop_body(t, carry):
            for u in range(big):
                step(t * big + u)
            return carry

        if big == 2:
            lax.fori_loop(kj // 2, nq // 2, loop_body, 0)
        else:
            half_empty = kj % big >= 2
            lax.fori_loop(kj // big + half_empty.astype(jnp.int32), nq // big, loop_body, 0)

            @pl.when(half_empty)
            def _():
                for u in range(2):
                    step((kj // big) * big + 2 + u)
        dkv_ref[:, 0:PAIR_Q] = (jnp.where(masks[0], dk_sc[0], 0.0) + jnp.where(masks[1], dk_sc[1], 0.0)).astype(BF16)
        dkv_ref[:, PAIR_Q:PAIR_KV] = jnp.where(lo, dv_sc[0], dv_sc[1]).astype(BF16)
        if bias_grad:
            csum_ref[...] = jnp.where(lo, jnp.sum(cs_sc[0], axis=1, keepdims=True),
                                      jnp.sum(cs_sc[1], axis=1, keepdims=True))

        @pl.when(kj == nq - 1)
        def _():
            dq_ref[...] = dq_sc[...].astype(BF16)

    rows_spec = pl.BlockSpec((nq, None, 8, tq), lambda b, p, j: (b, p, 0, 0))
    out_specs = [pl.BlockSpec((S, PAIR_Q), lambda b, p, j: (b, p)),
                 pl.BlockSpec((tq, PAIR_KV), lambda b, p, j: (b * nq + j, p))]
    out_shape = [jax.ShapeDtypeStruct((T, P * PAIR_Q), BF16), jax.ShapeDtypeStruct((T, P * PAIR_KV), BF16)]
    scratch = [pltpu.VMEM((S, PAIR_Q), F32), pltpu.VMEM((nq, 8, tq), F32),
               pltpu.VMEM((2, tq, PAIR_Q), F32), pltpu.VMEM((2, tq, LANES), F32)]
    if bias_grad:
        out_specs += [pl.BlockSpec((tq, LANES), lambda b, p, j: (b * nq + j, p)), rows_spec]
        out_shape += [jax.ShapeDtypeStruct((T, P * LANES), F32), jax.ShapeDtypeStruct((T // tq, P, 8, tq), F32)]
        scratch.append(pltpu.VMEM((2, tq, LANES), F32))
    return pl.pallas_call(
        body, name=name, grid=(B, P, nq),
        in_specs=[pl.BlockSpec((S, PAIR_Q), lambda b, p, j: (b, p)),
                  pl.BlockSpec((tq, PAIR_KV), lambda b, p, j: (b * nq + j, p)),
                  pl.BlockSpec((S, LANES), lambda b, p, j: (b, p)), rows_spec,
                  pl.BlockSpec((S, LANES), lambda b, p, j: (b, p))],
        out_specs=out_specs, out_shape=out_shape, scratch_shapes=scratch,
        compiler_params=_cparams(("parallel", "parallel", "arbitrary")),
    )(qx, kvx, o, lse, do)


def _fox_consts(P):
    H = 2 * P
    eq = np.zeros((3 * LANES, P * LANES), np.float32)
    ek = np.zeros((3 * LANES, P * LANES), np.float32)
    ones_q = np.zeros((1, P * LANES), np.float32)
    ones_k = np.zeros((1, P * LANES), np.float32)
    for h in range(H):
        base = (h // 2) * LANES + FOX_EXTRA * (h % 2)
        for part in range(3):
            eq[part * LANES + h, base + part] = 1.0
            ones_q[0, base + 3 + part] = 1.0
            ones_k[0, base + part] = 1.0
            ek[part * LANES + h, base + 3 + part] = -1.0
    return eq, ek, ones_q, ones_k


def _split3(f):
    hi = f.astype(BF16)
    r = f - hi.astype(F32)
    mid = r.astype(BF16)
    lo = (r - mid.astype(F32)).astype(BF16)
    return hi, mid, lo


def _tri_sum(tri, x):
    hi, mid, lo = _split3(x)
    return (jnp.dot(tri, hi, preferred_element_type=F32) + jnp.dot(tri, mid, preferred_element_type=F32)
            + jnp.dot(tri, lo, preferred_element_type=F32))


def _log1p_pos(e):
    return jnp.where(e < 0.01, e * (1.0 - e * (0.5 - e * (1.0 / 3.0))), jnp.log(1.0 + e))


def _fox_prep(qkv, fl, b_row, *, S, D, name):
    T = qkv.shape[0]
    P = D // LANES
    B = T // S
    tt = _tile(S, 256)
    per = S // tt
    eq, ek, ones_q, ones_k = _fox_consts(P)
    q_scale = HEAD_DIM ** -0.5

    def body(q_ref, k_ref, v_ref, fl_ref, b_ref, eq_ref, ek_ref, oq_ref, ok_ref, qx_ref, kvx_ref, carry):
        i = pl.program_id(1)

        @pl.when(i == 0)
        def _():
            carry[...] = jnp.zeros_like(carry)

        z = fl_ref[...] + b_ref[...]
        logf = jnp.minimum(z, 0.0) - _log1p_pos(jnp.exp(-jnp.abs(z)))
        row = lax.broadcasted_iota(jnp.int32, (tt, tt), 0)
        col = lax.broadcasted_iota(jnp.int32, (tt, tt), 1)
        tri = (col <= row).astype(BF16)
        f = _tri_sum(tri, logf) + carry[...]
        carry[...] = f[tt - 1:tt, :]
        parts = jnp.concatenate(_split3(f), axis=1)
        xq = jnp.dot(parts, eq_ref[...], preferred_element_type=F32) + oq_ref[...]
        xk = jnp.dot(parts, ek_ref[...], preferred_element_type=F32) + ok_ref[...]
        for p in range(P):
            c = slice(p * LANES, (p + 1) * LANES)
            qx_ref[:, p * PAIR_Q:p * PAIR_Q + LANES] = (q_ref[:, c].astype(F32) * q_scale).astype(BF16)
            qx_ref[:, p * PAIR_Q + LANES:(p + 1) * PAIR_Q] = xq[:, c].astype(BF16)
            kvx_ref[:, p * PAIR_KV:p * PAIR_KV + LANES] = k_ref[:, c]
            kvx_ref[:, p * PAIR_KV + LANES:p * PAIR_KV + PAIR_Q] = xk[:, c].astype(BF16)
            kvx_ref[:, p * PAIR_KV + PAIR_Q:(p + 1) * PAIR_KV] = v_ref[:, c]

    tok = lambda b, i: (b * per + i, 0)
    const = lambda b, i: (0, 0)
    return pl.pallas_call(
        body, name=name, grid=(B, per),
        in_specs=[pl.BlockSpec((tt, D), lambda b, i: (b * per + i, 0)),
                  pl.BlockSpec((tt, D), lambda b, i: (b * per + i, 1)),
                  pl.BlockSpec((tt, D), lambda b, i: (b * per + i, 2)),
                  pl.BlockSpec((tt, LANES), tok), pl.BlockSpec((1, LANES), const),
                  pl.BlockSpec(eq.shape, const), pl.BlockSpec(ek.shape, const),
                  pl.BlockSpec(ones_q.shape, const), pl.BlockSpec(ones_k.shape, const)],
        out_specs=[pl.BlockSpec((tt, P * PAIR_Q), tok), pl.BlockSpec((tt, P * PAIR_KV), tok)],
        out_shape=[jax.ShapeDtypeStruct((T, P * PAIR_Q), BF16), jax.ShapeDtypeStruct((T, P * PAIR_KV), BF16)],
        scratch_shapes=[pltpu.VMEM((1, LANES), F32)],
        compiler_params=_cparams(("arbitrary", "arbitrary")),
    )(qkv, qkv, qkv, fl, b_row, jnp.asarray(eq, BF16), jnp.asarray(ek, BF16), jnp.asarray(ones_q), jnp.asarray(ones_k))


def _fox_unprep(dqx, dkvx, csum, rsum, fl, b_row, *, S, D, name):
    T = dqx.shape[0]
    P = D // LANES
    B = T // S
    tt = _tile(S, 256)
    per = S // tt
    q_scale = HEAD_DIM ** -0.5

    def body(dq_ref, dkv_ref, cs_ref, rs_ref, fl_ref, b_ref, dqkv_ref, dfl_ref, db_ref, carry):
        b = pl.program_id(0)
        i = pl.program_id(1)

        @pl.when(i == 0)
        def _():
            carry[...] = jnp.zeros_like(carry)

        @pl.when((i == 0) & (b == 0))
        def _():
            db_ref[...] = jnp.zeros_like(db_ref)

        df = rs_ref[...] - cs_ref[...]
        for p in range(P):
            rq = slice(p * LANES, (p + 1) * LANES)
            dqkv_ref[:, rq] = (dq_ref[:, p * PAIR_Q:p * PAIR_Q + LANES].astype(F32) * q_scale).astype(BF16)
            dqkv_ref[:, D + p * LANES:D + (p + 1) * LANES] = dkv_ref[:, p * PAIR_KV:p * PAIR_KV + LANES]
            dqkv_ref[:, 2 * D + p * LANES:2 * D + (p + 1) * LANES] = dkv_ref[:, p * PAIR_KV + PAIR_Q:(p + 1) * PAIR_KV]
        row = lax.broadcasted_iota(jnp.int32, (tt, tt), 0)
        col = lax.broadcasted_iota(jnp.int32, (tt, tt), 1)
        tri = (col >= row).astype(BF16)
        dlogf = _tri_sum(tri, df) + carry[...]
        carry[...] = dlogf[0:1, :]
        z = fl_ref[...] + b_ref[...]
        e = jnp.exp(-jnp.abs(z))
        sig_neg = jnp.where(z >= 0.0, e, 1.0) / (1.0 + e)
        dfl = dlogf * sig_neg
        dfl_ref[...] = dfl.astype(BF16)
        db_ref[...] += jnp.sum(dfl, axis=0, keepdims=True)

    rev = lambda b, i: (b * per + per - 1 - i, 0)
    const = lambda b, i: (0, 0)
    return pl.pallas_call(
        body, name=name, grid=(B, per),
        in_specs=[pl.BlockSpec((tt, P * PAIR_Q), rev), pl.BlockSpec((tt, P * PAIR_KV), rev),
                  pl.BlockSpec((tt, LANES), rev), pl.BlockSpec((tt, LANES), rev), pl.BlockSpec((tt, LANES), rev),
                  pl.BlockSpec((1, LANES), const)],
        out_specs=[pl.BlockSpec((tt, 3 * D), rev), pl.BlockSpec((tt, LANES), rev), pl.BlockSpec((1, LANES), const)],
        out_shape=[jax.ShapeDtypeStruct((T, 3 * D), BF16), jax.ShapeDtypeStruct((T, LANES), BF16),
                   jax.ShapeDtypeStruct((1, LANES), F32)],
        scratch_shapes=[pltpu.VMEM((1, LANES), F32)],
        compiler_params=_cparams(("arbitrary", "arbitrary")),
    )(dqx, dkvx, csum, rsum, fl, b_row)


def _rms(x):
    r = lax.rsqrt(jnp.mean(x * x, axis=-1, keepdims=True) + NORM_EPS)
    return x * r, r


def _mla_mid(lat, gq, gkv, cos_t, sin_s, *, name):
    T, W = lat.shape
    Rq = W - 2 * LANES
    tt = _tile(T, 512)

    def body(l_ref, gq_ref, gkv_ref, c_ref, s_ref, o_ref, ot_ref):
        nq, _ = _rms(l_ref[:, 0:Rq])
        nkv, _ = _rms(l_ref[:, Rq:Rq + LANES])
        parts = [nq * gq_ref[...], nkv * gkv_ref[...], _rope128(l_ref[:, Rq + LANES:W], c_ref[...], s_ref[...])]
        out = jnp.concatenate(parts, axis=1)
        o_ref[...] = out.astype(BF16)
        ot_ref[...] = out.T.astype(BF16)

    return pl.pallas_call(
        body, name=name, grid=(T // tt,),
        in_specs=[pl.BlockSpec((tt, W), lambda i: (i, 0)), pl.BlockSpec((1, Rq), lambda i: (0, 0)),
                  pl.BlockSpec((1, LANES), lambda i: (0, 0)), pl.BlockSpec((tt, LANES), lambda i: (i, 0)),
                  pl.BlockSpec((tt, LANES), lambda i: (i, 0))],
        out_specs=[pl.BlockSpec((tt, W), lambda i: (i, 0)), pl.BlockSpec((W, tt), lambda i: (0, i))],
        out_shape=[jax.ShapeDtypeStruct((T, W), BF16), jax.ShapeDtypeStruct((W, T), BF16)],
        compiler_params=_cparams(("parallel",)),
    )(lat, gq, gkv, cos_t, sin_s)


def _mla_mid_bwd(lat, dcq, dckr, gq, gkv, cos_t, sin_s, *, name):
    T, W = lat.shape
    Rq = W - 2 * LANES
    tt = _tile(T, 512)

    def norm_bwd(x, dy, g):
        n, r = _rms(x)
        dn = dy * g
        return r * (dn - n * jnp.mean(dn * n, axis=-1, keepdims=True)), jnp.sum(dy * n, axis=0, keepdims=True)

    def body(l_ref, dq_ref, dk_ref, gq_ref, gkv_ref, c_ref, s_ref, o_ref, dgq_ref, dgkv_ref):
        i = pl.program_id(0)

        @pl.when(i == 0)
        def _():
            dgq_ref[...] = jnp.zeros_like(dgq_ref)
            dgkv_ref[...] = jnp.zeros_like(dgkv_ref)

        dxq, dgq = norm_bwd(l_ref[:, 0:Rq], dq_ref[...], gq_ref[...])
        dxkv, dgkv = norm_bwd(l_ref[:, Rq:Rq + LANES], dk_ref[:, 0:LANES], gkv_ref[...])
        o_ref[:, 0:Rq] = dxq.astype(BF16)
        o_ref[:, Rq:Rq + LANES] = dxkv.astype(BF16)
        o_ref[:, Rq + LANES:W] = _rope128(dk_ref[:, LANES:2 * LANES], c_ref[...], -s_ref[...]).astype(BF16)
        dgq_ref[...] += dgq
        dgkv_ref[...] += dgkv

    return pl.pallas_call(
        body, name=name, grid=(T // tt,),
        in_specs=[pl.BlockSpec((tt, W), lambda i: (i, 0)), pl.BlockSpec((tt, Rq), lambda i: (i, 0)),
                  pl.BlockSpec((tt, 2 * LANES), lambda i: (i, 0)), pl.BlockSpec((1, Rq), lambda i: (0, 0)),
                  pl.BlockSpec((1, LANES), lambda i: (0, 0)), pl.BlockSpec((tt, LANES), lambda i: (i, 0)),
                  pl.BlockSpec((tt, LANES), lambda i: (i, 0))],
        out_specs=[pl.BlockSpec((tt, W), lambda i: (i, 0)), pl.BlockSpec((1, Rq), lambda i: (0, 0)),
                   pl.BlockSpec((1, LANES), lambda i: (0, 0))],
        out_shape=[jax.ShapeDtypeStruct((T, W), BF16), jax.ShapeDtypeStruct((1, Rq), F32),
                   jax.ShapeDtypeStruct((1, LANES), F32)],
        compiler_params=_cparams(("arbitrary",)),
    )(lat, dcq, dckr, gq, gkv, cos_t, sin_s)


def _uq_to_pairs(w):
    Rq = w.shape[0]
    P = w.shape[1] // (2 * (HEAD_DIM + ROPE_DIM))
    w4 = w.reshape(Rq, P, 2, HEAD_DIM + ROPE_DIM)
    nope = w4[..., :HEAD_DIM].reshape(Rq, P, 2 * HEAD_DIM)
    rope = w4[..., HEAD_DIM:].reshape(Rq, P, 2 * ROPE_DIM)
    pad = jnp.zeros((Rq, P, PAIR_Q - 2 * HEAD_DIM - 2 * ROPE_DIM), w.dtype)
    return jnp.concatenate([nope, rope, pad], axis=-1).reshape(Rq, P * PAIR_Q)


def _uq_from_pairs(g):
    Rq = g.shape[0]
    P = g.shape[1] // PAIR_Q
    g3 = g.reshape(Rq, P, PAIR_Q)
    nope = g3[..., :2 * HEAD_DIM].reshape(Rq, P, 2, HEAD_DIM)
    rope = g3[..., 2 * HEAD_DIM:2 * HEAD_DIM + 2 * ROPE_DIM].reshape(Rq, P, 2, ROPE_DIM)
    return jnp.concatenate([nope, rope], axis=-1).reshape(Rq, P * 2 * (HEAD_DIM + ROPE_DIM))


def _ukv_to_pairs(w):
    P = w.shape[1] // (4 * HEAD_DIM)
    w4 = w.reshape(KV_RANK, P, 2, 2 * HEAD_DIM)
    kn = w4[..., :HEAD_DIM].reshape(KV_RANK, P, 2 * HEAD_DIM)
    vv = w4[..., HEAD_DIM:].reshape(KV_RANK, P, 2 * HEAD_DIM)
    top = jnp.concatenate([kn, jnp.zeros((KV_RANK, P, LANES), w.dtype), vv], axis=-1)
    place = np.zeros((LANES, P, PAIR_KV), np.float32)
    for r in range(ROPE_DIM):
        place[r, :, LANES + r] = 1.0
        place[r, :, LANES + ROPE_DIM + r] = 1.0
    return jnp.concatenate([top, jnp.asarray(place, w.dtype)], axis=0).reshape(KV_RANK + LANES, P * PAIR_KV)


def _ukv_from_pairs(g):
    P = g.shape[1] // PAIR_KV
    g3 = g[:KV_RANK].reshape(KV_RANK, P, PAIR_KV)
    kn = g3[..., :2 * HEAD_DIM].reshape(KV_RANK, P, 2, HEAD_DIM)
    vv = g3[..., PAIR_Q:].reshape(KV_RANK, P, 2, HEAD_DIM)
    return jnp.concatenate([kn, vv], axis=-1).reshape(KV_RANK, P * 4 * HEAD_DIM)


def _mlp_fwd(h2, w, i, x1, gate, *, S):
    def act(acc):
        u = jnp.square(jnp.maximum(acc, 0.0))
        return acc, u, u

    p, u, u_t = _mm(h2, w["mlp_w1"], "nn", name=f"mlp_up_{i}", b_layer=i, out_dtypes=(BF16, BF16, BF16),
                    out_t=(False, False, True), epilogue=act)
    x2, z = _mm(u, w["mlp_w2"], "nn", name=f"mlp_down_{i}", b_layer=i, out_dtypes=(F32, F32), extras=(x1,),
                rowvecs=(gate,), seq=S, epilogue=lambda acc, xr, g: (xr + g * acc, acc))
    return x2, (p, u_t, z)


STACKED_GRADS = ("fox_out", "mla_down", "mla_uq", "mla_ukv", "mla_out", "mlp_w1", "mlp_w2")


def _local_step(x, target, pos_f, inv_freq_row, sign_row, mod, w, slots, *, S, hooks=None):
    hooks = hooks or {}
    T, D = x.shape
    L = mod.shape[0]
    L2 = len(w["fox_out"])
    cos_t, sin_s = _rope_tables(pos_f, inv_freq_row, sign_row)
    saved = []
    for i in range(L):
        j = i // 2
        sh_m, sc_m, g_m, sh_f, sc_f, g_f = (mod[i, s] for s in range(6))
        h, h_t = _norm_mod(x, w["norm_mix_g"][i], sc_m, sh_m, S=S, name=f"norm_mix_{i}")
        if i % 2 == 0:
            qkv = _mm(h, w["fox_qkv"], "nn", name=f"fox_qkv_{i}", b_layer=j, out_dtypes=(BF16,))
            fl = _mm(h, w["fox_f"], "nn", name=f"fox_f_{i}", b_layer=j)
            qx, kvx = _fox_prep(qkv, fl, w["fox_b"][j], S=S, D=D, name=f"fox_prep_{i}")
            o, lse, o_t = _attn_fwd(qx, kvx, S=S, scale=1.0, ew=FOX_EXTRA, name=f"fox_attn_{i}")
            mix = (qx, kvx, o, lse, o_t, fl)
            w_out = w["fox_out"]
        else:
            lat = _mm(h, w["mla_down"], "nn", name=f"mla_down_{i}", b_layer=j)
            Rq = lat.shape[1] - 2 * LANES
            cqr, cqr_t = _mla_mid(lat, w["mla_gq"][j], w["mla_gkv"][j], cos_t, sin_s, name=f"mla_mid_{i}")
            qx = _mm(cqr, w["mla_uq"], "nn", name=f"mla_uq_{i}", b_layer=j, out_dtypes=(BF16,), a_sz=Rq, tk=Rq,
                     tables=(cos_t, sin_s), epilogue=lambda acc, c, s: (_rope_pairs(acc, c, s, 1.0),))
            kvx = _mm(cqr, w["mla_ukv"], "nn", name=f"mla_ukv_{i}", b_layer=j, out_dtypes=(BF16,), a_off=Rq,
                      a_sz=2 * LANES, tk=2 * LANES, tn=PAIR_KV)
            o, lse, o_t = _attn_fwd(qx, kvx, S=S, scale=(HEAD_DIM + ROPE_DIM) ** -0.5, ew=ROPE_DIM,
                                    name=f"mla_attn_{i}")
            mix = (qx, kvx, o, lse, o_t, lat, cqr_t)
            w_out = w["mla_out"]
        x1, y = _mm(o, w_out, "nn", name=f"mix_out_{i}", b_layer=j, out_dtypes=(F32, F32), extras=(x,),
                    rowvecs=(g_m,), seq=S, epilogue=lambda acc, xr, g: (xr + g * acc, acc))
        h2, h2_t = _norm_mod(x1, w["norm_mlp_g"][i], sc_f, sh_f, S=S, name=f"norm_mlp_{i}")
        if i == 0 and "fwd_mlp0" in hooks:
            w = hooks["fwd_mlp0"](x1, w)
        x2, mlp = _mlp_fwd(h2, w, i, x1, g_f, S=S)
        saved.append((x, h_t, mix, y, x1, h2_t, mlp))
        x = x2
        if i == 0 and "fwd_layer1" in hooks:
            w = hooks["fwd_layer1"](x, w)

    dx, dg_final, loss = _final_loss(x, target, w["final_norm_g"])
    n_split = w["mlp_w1"][0][0].shape[1]

    grads = {k: [None] * len(w[k]) for k in ("norm_mix_g", "norm_mlp_g", "fox_b", "mla_gq", "mla_gkv")}
    grads.update({k: [None] * L2 for k in ("fox_qkv", "fox_f")})
    grads.update({k: {} for k in STACKED_GRADS})
    grads["final_norm_g"] = dg_final

    def stacked(key, layer, _, a_t, b, **kw):
        group, idx, count = slots[(key, layer)]
        grads[key][group] = _mm(a_t, b, "nn", out_stack=(grads[key].get(group), idx, count), **kw)

    dmod = [None] * L
    for i in reversed(range(L)):
        j = i // 2
        x0, h_t, mix, y, x1, h2_t, (p, u_t, z) = saved[i]
        sh_m, sc_m, g_m, sh_f, sc_f, g_f = (mod[i, s] for s in range(6))
        if i == 0 and "bwd_layer0" in hooks:
            g_f = g_f + hooks["bwd_layer0"](grads)[0, 0]
        dz, dg_f = _gate_bwd(dx, z, g_f, S=S, name=f"gate_mlp_bwd_{i}")
        stacked("mlp_w2", i, L, u_t, dz, name=f"mlp_w2_grad_{i}")
        dp = _mm(dz, w["mlp_w2"], "nt", name=f"mlp_down_bwd_{i}", b_layer=i, out_dtypes=(BF16,), extras=(p,),
                 epilogue=lambda acc, pv: (acc * (2.0 * jnp.maximum(pv.astype(F32), 0.0)),))
        stacked("mlp_w1", i, L, h2_t, dp, name=f"mlp_w1_grad_{i}", out_split=n_split)
        if i == 0 and "bwd_mix0" in hooks:
            g_m = g_m + hooks["bwd_mix0"](grads)[0, 0]
        dh2 = _mm(dp, w["mlp_w1"], "nt", name=f"mlp_up_bwd_{i}", b_layer=i)
        dx1, dsh_f, dsc_f, dgn = _norm_mod_bwd(x1, dh2, dx, w["norm_mlp_g"][i], sc_f, S=S, name=f"norm_mlp_bwd_{i}")
        grads["norm_mlp_g"][i] = dgn
        dy, dg_m = _gate_bwd(dx1, y, g_m, S=S, name=f"gate_mix_bwd_{i}")
        if i % 2 == 0:
            qx, kvx, o, lse, o_t, fl = mix
            stacked("fox_out", j, L2, o_t, dy, name=f"fox_out_grad_{i}")
            do = _mm(dy, w["fox_out"], "nt", name=f"fox_out_bwd_{i}", b_layer=j, out_dtypes=(BF16,))
            dqx, dkvx, csum, rsum = _attn_bwd(qx, kvx, o, lse, do, S=S, scale=1.0, ew=FOX_EXTRA,
                                              name=f"fox_attn_bwd_{i}", bias_grad=True)
            n_heads = D // HEAD_DIM
            csum = jnp.pad(csum.reshape(T, n_heads, HEAD_DIM)[:, :, 0], ((0, 0), (0, LANES - n_heads)))
            rsum = jnp.transpose(rsum[:, :, :2, :], (0, 3, 1, 2)).reshape(T, n_heads)
            rsum = jnp.pad(rsum, ((0, 0), (0, LANES - n_heads)))
            dqkv, dfl, db = _fox_unprep(dqx, dkvx, csum, rsum, fl, w["fox_b"][j], S=S, D=D, name=f"fox_unprep_{i}")
            grads["fox_b"][j] = db
            grads["fox_qkv"][j] = _mm(h_t, dqkv, "nn", name=f"fox_qkv_grad_{i}")
            grads["fox_f"][j] = _mm(h_t, dfl, "nn", name=f"fox_f_grad_{i}")
            dh_f = _mm(dfl, w["fox_f"], "nt", name=f"fox_f_bwd_{i}", b_layer=j)
            dh = _mm(dqkv, w["fox_qkv"], "nt", name=f"fox_qkv_bwd_{i}", b_layer=j, extras=(dh_f,),
                     epilogue=lambda acc, e: (acc + e,))
        else:
            qx, kvx, o, lse, o_t, lat, cqr_t = mix
            Rq = lat.shape[1] - 2 * LANES
            stacked("mla_out", j, L2, o_t, dy, name=f"mla_out_grad_{i}")
            do = _mm(dy, w["mla_out"], "nt", name=f"mla_out_bwd_{i}", b_layer=j, out_dtypes=(BF16,))
            dqx, dkvx = _attn_bwd(qx, kvx, o, lse, do, S=S, scale=(HEAD_DIM + ROPE_DIM) ** -0.5, ew=ROPE_DIM,
                                  name=f"mla_attn_bwd_{i}")
            dqpre = _unrope(dqx, cos_t, sin_s)
            stacked("mla_uq", j, L2, cqr_t[:Rq], dqpre, name=f"mla_uq_grad_{i}", out_split=n_split)
            stacked("mla_ukv", j, L2, cqr_t[Rq:], dkvx, name=f"mla_ukv_grad_{i}", tn=PAIR_KV, out_split=n_split)
            dcq = _mm(dqpre, w["mla_uq"], "nt", name=f"mla_uq_bwd_{i}", b_layer=j)
            dckr = _mm(dkvx, w["mla_ukv"], "nt", name=f"mla_ukv_bwd_{i}", b_layer=j, tk=PAIR_KV * 2)
            dlat, dgq, dgkv = _mla_mid_bwd(lat, dcq, dckr, w["mla_gq"][j], w["mla_gkv"][j], cos_t, sin_s,
                                           name=f"mla_mid_bwd_{i}")
            grads["mla_gq"][j] = dgq
            grads["mla_gkv"][j] = dgkv
            stacked("mla_down", j, L2, h_t, dlat, name=f"mla_down_grad_{i}")
            dh = _mm(dlat, w["mla_down"], "nt", name=f"mla_down_bwd_{i}", b_layer=j)
        dx, dsh_m, dsc_m, dgn = _norm_mod_bwd(x0, dh, dx1, w["norm_mix_g"][i], sc_m, S=S, name=f"norm_mix_bwd_{i}")
        grads["norm_mix_g"][i] = dgn
        dmod[i] = jnp.stack([dsh_m, dsc_m, dg_m, dsh_f, dsc_f, dg_f])
    return loss, dx, jnp.stack(dmod), grads


GATHERED = ("fox_in", "fox_out", "mla_down", "mla_uq", "mla_ukv", "mla_out", "mlp_w1", "mlp_w2")
ROW_SHARDED = ("fox_out", "mla_down", "mla_out", "mlp_w2")


def _shard_layouts(wts):
    dkv = wts["mla_w_dkv"]
    dkv = jnp.pad(dkv, ((0, 0), (0, 0), (0, 2 * LANES - dkv.shape[2])))
    return {
        "fox_in": _pad_lanes(wts["fox_w_in"].astype(BF16)),
        "fox_out": wts["fox_w_out"].astype(BF16),
        "mla_down": jnp.concatenate([wts["mla_w_dq"], dkv], axis=2).astype(BF16),
        "mla_uq": jax.vmap(_uq_to_pairs)(wts["mla_w_uq"].astype(BF16)),
        "mla_ukv": jax.vmap(_ukv_to_pairs)(wts["mla_w_ukv"].astype(BF16)),
        "mla_out": wts["mla_w_out"].astype(BF16),
        "mlp_w1": wts["mlp_w1"].astype(BF16),
        "mlp_w2": wts["mlp_w2"].astype(BF16),
    }


def _small_layouts(small):
    return {
        "fox_b": [jnp.pad(b, (0, LANES - b.shape[0]))[None, :] for b in small["fox_b_f"]],
        "mla_gq": [g[None, :] for g in small["mla_q_norm_g"]],
        "mla_gkv": [g[None, :] for g in small["mla_kv_norm_g"]],
        "norm_mix_g": [g[None, :] for g in small["norm_mix_g"]],
        "norm_mlp_g": [g[None, :] for g in small["norm_mlp_g"]],
        "final_norm_g": small["final_norm_g"][None, :],
    }


def _comm_groups(L, L2):
    rest = [("fox_in", 1, L2 - 1), ("fox_out", 1, L2 - 1), ("mla_down", 0, L2), ("mla_uq", 0, L2),
            ("mla_ukv", 0, L2), ("mla_out", 0, L2), ("mlp_w1", 1, L - 1), ("mlp_w2", 1, L - 1)]
    return {"mix0": [("fox_in", 0, 1), ("fox_out", 0, 1)], "mlp0": [("mlp_w1", 0, 1), ("mlp_w2", 0, 1)],
            "rest": [e for e in rest if e[2] > 0]}


def _layer_slots(groups):
    return {(n, s + l): (g, l, cnt) for g, entries in groups.items() for n, s, cnt in entries for l in range(cnt)}


def _pad_lanes(a):
    cols = a.shape[-1]
    return jnp.pad(a, [(0, 0)] * (a.ndim - 1) + [(0, -cols % LANES)])


def _weight_views(name, gathered, D, n_fox_heads):
    n, ns, rows, cols = gathered.shape
    if name == "fox_in":
        true_cols = (3 * D + n_fox_heads) // ns
        fox = jnp.concatenate([gathered[:, k, :, :true_cols] for k in range(ns)], axis=-1)
        return {"fox_qkv": fox[:, :, :3 * D], "fox_f": _pad_lanes(fox[:, :, 3 * D:])}
    if name in ROW_SHARDED:
        return {name: gathered.reshape(n, ns * rows, cols)}
    return {name: gathered}


def _grad_pieces(name, g, qkv_f, n_fox_heads, ns):
    if name == "fox_in":
        fox = jnp.stack([jnp.concatenate([a, b[:, :n_fox_heads]], axis=1) for a, b in qkv_f])
        cols = fox.shape[2] // ns
        return jnp.stack([_pad_lanes(fox[:, :, k * cols:(k + 1) * cols]) for k in range(ns)], axis=1)
    if name in ROW_SHARDED:
        return g.reshape(g.shape[0], ns, g.shape[1] // ns, g.shape[2])
    return g


def _small_grads(g, n_fox_heads):
    return {
        "norm_mix_g": jnp.concatenate(g["norm_mix_g"], axis=0),
        "norm_mlp_g": jnp.concatenate(g["norm_mlp_g"], axis=0),
        "final_norm_g": g["final_norm_g"][0],
        "fox_b_f": jnp.concatenate(g["fox_b"], axis=0)[:, :n_fox_heads],
        "mla_q_norm_g": jnp.concatenate(g["mla_gq"], axis=0),
        "mla_kv_norm_g": jnp.concatenate(g["mla_gkv"], axis=0),
    }


def _silu(c):
    return c * (1.0 / (1.0 + jnp.exp(-c)))


def _ada_fwd(c_all, ada_w, ada_b_cols):
    L, D, C = ada_w.shape
    Bg = c_all.shape[0]
    tc = _tile(C, 512)

    def body(c_ref, w_ref, b_ref, o_ref):
        ca = _silu(c_ref[...]).astype(BF16)
        o_ref[...] = jnp.dot(ca, w_ref[...].astype(BF16), preferred_element_type=F32) + b_ref[...]

    return pl.pallas_call(
        body, name="ada_fwd", grid=(L, C // tc),
        in_specs=[pl.BlockSpec((Bg, D), lambda l, j: (0, 0)), pl.BlockSpec((None, D, tc), lambda l, j: (l, 0, j)),
                  pl.BlockSpec((None, 1, tc), lambda l, j: (l, 0, j))],
        out_specs=pl.BlockSpec((None, Bg, tc), lambda l, j: (l, 0, j)),
        out_shape=jax.ShapeDtypeStruct((L, Bg, C), F32),
        compiler_params=_cparams(("parallel", "parallel")),
    )(c_all, ada_w, ada_b_cols)


def _ada_bwd(c_all, dmod_cols):
    L, Bg, C = dmod_cols.shape
    D = c_all.shape[1]
    tc = _tile(C, 512)

    def body(c_ref, d_ref, o_ref):
        ca = _silu(c_ref[...]).astype(BF16)
        o_ref[...] = _dot_tn(ca, d_ref[...].astype(BF16))

    return pl.pallas_call(
        body, name="ada_bwd", grid=(L, C // tc),
        in_specs=[pl.BlockSpec((Bg, D), lambda l, j: (0, 0)), pl.BlockSpec((None, Bg, tc), lambda l, j: (l, 0, j))],
        out_specs=pl.BlockSpec((None, D, tc), lambda l, j: (l, 0, j)),
        out_shape=jax.ShapeDtypeStruct((L, D, C), F32),
        compiler_params=_cparams(("parallel", "parallel")),
    )(c_all, dmod_cols)


def _adamw_update(w, gv, m, v):
    mn = ADAM_B1 * m + (1.0 - ADAM_B1) * gv
    vn = ADAM_B2 * v + (1.0 - ADAM_B2) * jnp.square(gv)
    m_hat = mn / (1.0 - ADAM_B1 ** ADAM_STEP)
    v_hat = vn / (1.0 - ADAM_B2 ** ADAM_STEP)
    return -ADAM_LR * (m_hat / (jnp.sqrt(v_hat) + ADAM_EPS) + ADAM_WD * w), mn, vn


def _adamw(w, g, m, v, *, name):
    shape = w.shape
    C = shape[-1]
    R = int(np.prod(shape[:-1])) if len(shape) > 1 else 1
    w2, g2, m2, v2 = (a.reshape(R, C) for a in (w, g, m, v))
    tr = _row_tile(R, C)

    def body(w_ref, g_ref, m_ref, v_ref, d_ref, nm_ref, nv_ref):
        d_ref[...], nm_ref[...], nv_ref[...] = _adamw_update(w_ref[...], g_ref[...], m_ref[...], v_ref[...])

    spec = pl.BlockSpec((tr, C), lambda i: (i, 0))
    out = pl.pallas_call(
        body, name=name, grid=(R // tr,), in_specs=[spec] * 4, out_specs=[spec] * 3,
        out_shape=[jax.ShapeDtypeStruct((R, C), F32)] * 3, compiler_params=_cparams(("parallel",)),
    )(w2, g2, m2, v2)
    return tuple(a.reshape(shape) for a in out)


def _adamw_halves(w, g_own, g_peer, m, v, c_idx, *, name):
    L, rows, C = w.shape
    R = rows // 2
    tr = _row_tile(R, C)

    def body(c_ref, w_ref, go_ref, gp_ref, m_ref, v_ref, g_ref, d_ref, nm_ref, nv_ref):
        gv = jnp.where(pl.program_id(1) == c_ref[0], go_ref[...], gp_ref[...])
        g_ref[...] = gv
        d_ref[...], nm_ref[...], nv_ref[...] = _adamw_update(w_ref[...], gv, m_ref[...], v_ref[...])

    full = pl.BlockSpec((None, None, tr, C), lambda l, hh, i, c_ref: (l, hh, i, 0))
    half = pl.BlockSpec((None, tr, C), lambda l, hh, i, c_ref: (l, i, 0))
    grid_spec = pltpu.PrefetchScalarGridSpec(
        num_scalar_prefetch=1, grid=(L, 2, R // tr), in_specs=[full, half, half, full, full], out_specs=[full] * 4)
    split = lambda a: a.reshape(L, 2, R, C)
    out = pl.pallas_call(
        body, name=name, grid_spec=grid_spec, out_shape=[jax.ShapeDtypeStruct((L, 2, R, C), F32)] * 4,
        compiler_params=_cparams(("parallel", "parallel", "parallel")),
    )(c_idx, split(w), g_own, g_peer, split(m), split(v))
    return tuple(a.reshape(w.shape) for a in out)


def _sum_gathered(dm8, sm8):
    n_dev, Bl, R, D = dm8.shape
    Rs = sm8.shape[1]

    def body(dm_ref, sm_ref, ob_ref, os_ref):
        acc_b = jnp.zeros((R, D), F32)
        acc_s = jnp.zeros((Rs, D), F32)
        for d in range(n_dev):
            for b in range(Bl):
                acc_b = acc_b + dm_ref[d, b]
            acc_s = acc_s + sm_ref[d]
        ob_ref[...] = acc_b
        os_ref[...] = acc_s

    return pl.pallas_call(
        body, name="sum_gathered",
        out_shape=[jax.ShapeDtypeStruct((R, D), F32), jax.ShapeDtypeStruct((Rs, D), F32)],
        compiler_params=_cparams(None),
    )(dm8, sm8)


N_DEV = 8
N_CHIP = 4
ANY = pl.BlockSpec(memory_space=pl.ANY)
HBM = pl.BlockSpec(memory_space=pltpu.HBM)
SEM = pl.BlockSpec(memory_space=pltpu.SEMAPHORE)
DATAFLOW = pltpu.SideEffectType.DATAFLOW_SIDE_EFFECTING


def _mesh_pos():
    return lax.axis_index("x"), lax.axis_index("y"), lax.axis_index("c")


def _all_gather8(block, *, name, in_vmem):
    R, W = block.shape

    def body(x_ref, out_ref, send_sems, recv_sems, local_sem):
        x, y, c = _mesh_pos()
        me, sibling = (x, y, c), (x, y, 1 - c)
        chips = [(1 - x, y), (x, 1 - y), (1 - x, 1 - y)]

        def slot(px, py, pc):
            return out_ref.at[4 * px + 2 * py + pc]

        def copy(k, blk, to, src=None):
            return pltpu.make_async_remote_copy(
                src_ref=slot(*blk) if src is None else src, dst_ref=slot(*blk),
                send_sem=send_sems.at[k], recv_sem=recv_sems.at[k], device_id=to, device_id_type=MESH_ID)

        mine = pltpu.make_async_copy(x_ref, slot(*me), local_sem)
        mine.start()
        first = [copy(0, me, sibling, src=x_ref)]
        first += [copy(1 + j, me, (*chip, c), src=x_ref) for j, chip in enumerate(chips)]
        for cp in first:
            cp.start()
        passed = [copy(4 + j, (*chip, c), sibling) for j, chip in enumerate(chips)]
        for j, chip in enumerate(chips):
            copy(1 + j, (*chip, c), me).wait_recv()
            passed[j].start()
        copy(0, sibling, me).wait_recv()
        for j, chip in enumerate(chips):
            copy(4 + j, (*chip, 1 - c), me).wait_recv()
        for cp in first + passed:
            cp.wait_send()
        mine.wait()

    space = pl.BlockSpec(memory_space=pltpu.VMEM) if in_vmem else ANY
    return pl.pallas_call(
        body, name=name, out_shape=jax.ShapeDtypeStruct((N_DEV, R, W), block.dtype),
        in_specs=[space], out_specs=space,
        scratch_shapes=[pltpu.SemaphoreType.DMA((7,)), pltpu.SemaphoreType.DMA((7,)), pltpu.SemaphoreType.DMA],
        compiler_params=pltpu.CompilerParams(vmem_limit_bytes=VMEM_LIMIT_V7X),
    )(block)


def _comm_call(body, arrays, out_shapes, n_sems, *, name):
    return pl.pallas_call(
        body, name=name, out_shape=out_shapes, in_specs=[ANY] * len(arrays), out_specs=[ANY] * len(out_shapes),
        scratch_shapes=[pltpu.SemaphoreType.DMA((n_sems,)), pltpu.SemaphoreType.DMA((n_sems,)),
                        pltpu.SemaphoreType.DMA((len(arrays),))],
    )(*arrays)


def _gather_weights(shards, *, name):
    n = len(shards)

    def body(*refs):
        xs, outs = refs[:n], refs[n:2 * n]
        send_sems, recv_sems, local_sems = refs[2 * n:]
        x, y, c = _mesh_pos()
        me, sibling = (x, y, c), (x, y, 1 - c)
        chips = [(1 - x, y), (x, 1 - y), (1 - x, 1 - y)]
        waits = []
        for i in range(n):
            nl = shards[i].shape[0]
            own = xs[i].at[pl.ds(0, nl), c]

            def slot(px, py, pc, i=i, nl=nl):
                return outs[i].at[pl.ds(0, nl), 2 * px + py, pc]

            def copy(k, blk, to, src=None, i=i, slot=slot):
                return pltpu.make_async_remote_copy(
                    src_ref=slot(*blk) if src is None else src, dst_ref=slot(*blk),
                    send_sem=send_sems.at[7 * i + k], recv_sem=recv_sems.at[7 * i + k], device_id=to,
                    device_id_type=MESH_ID)

            mine = pltpu.make_async_copy(own, slot(*me), local_sems.at[i])
            mine.start()
            first = [copy(0, me, sibling, src=own)]
            first += [copy(1 + j, me, (*chip, c), src=own) for j, chip in enumerate(chips)]
            for cp in first:
                cp.start()
            waits.append((copy, mine, first))
        for copy, mine, first in waits:
            passed = [copy(4 + j, (*chip, c), sibling) for j, chip in enumerate(chips)]
            for j, chip in enumerate(chips):
                copy(1 + j, (*chip, c), me).wait_recv()
                passed[j].start()
            copy(0, sibling, me).wait_recv()
            for j, chip in enumerate(chips):
                copy(4 + j, (*chip, 1 - c), me).wait_recv()
            for cp in first + passed:
                cp.wait_send()
            mine.wait()

    out_shapes = [jax.ShapeDtypeStruct((s.shape[0], N_CHIP) + s.shape[1:], s.dtype) for s in shards]
    return _comm_call(body, shards, out_shapes, 7 * n, name=name)


def _place_own(shard, chip_idx, c_idx, *, name):
    n, _, rows, cols = shard.shape
    tr = _row_tile(rows, cols)

    def body(k_ref, c_ref, x_ref, o_ref):
        o_ref[...] = x_ref[...]

    grid_spec = pltpu.PrefetchScalarGridSpec(
        num_scalar_prefetch=2, grid=(n, rows // tr),
        in_specs=[pl.BlockSpec((None, None, tr, cols), lambda l, i, k_ref, c_ref: (l, c_ref[0], i, 0))],
        out_specs=pl.BlockSpec((None, None, None, tr, cols), lambda l, i, k_ref, c_ref: (l, k_ref[0], c_ref[0], i, 0)))
    return pl.pallas_call(
        body, name=name, grid_spec=grid_spec,
        out_shape=jax.ShapeDtypeStruct((n, N_CHIP, 2, rows, cols), shard.dtype),
        compiler_params=_cparams(("parallel", "parallel")),
    )(chip_idx, c_idx, shard)


def _gather_copies(x_refs, land_refs, send_sems, recv_sems):
    x, y, c = _mesh_pos()
    k_me = 2 * x + y
    targets = [(x, y, 1 - c), (1 - x, y, c), (x, 1 - y, c), (1 - x, 1 - y, c)]
    copies = []
    for i, (x_ref, land_ref) in enumerate(zip(x_refs, land_refs)):
        nl = x_ref.shape[0]
        for j, to in enumerate(targets):
            copies.append(pltpu.make_async_remote_copy(
                src_ref=x_ref.at[pl.ds(0, nl), c], dst_ref=land_ref.at[pl.ds(0, nl), k_me, c],
                send_sem=send_sems.at[4 * i + j], recv_sem=recv_sems.at[4 * i + j], device_id=to,
                device_id_type=MESH_ID))
    return copies


def _split_start(copies_fn, srcs, lands, after, *, name, sems_per_array):
    n = len(srcs)

    def body(*refs):
        send_sems, recv_sems = refs[2 * n + 1], refs[2 * n + 2]
        for cp in copies_fn(refs[:n], refs[n:2 * n], send_sems, recv_sems):
            cp.start()
        refs[-1][...] = jnp.zeros_like(refs[-1])

    operands = [pltpu.with_memory_space_constraint(a, pltpu.HBM) for a in list(srcs) + list(lands)]
    n_sems = sems_per_array * n
    out_shape = ([pltpu.SemaphoreType.DMA((n_sems,)), pltpu.SemaphoreType.DMA((n_sems,))]
                 + [pltpu.HBM(a.shape, a.dtype) for a in operands] + [jax.ShapeDtypeStruct((8, LANES), F32)])
    res = pl.pallas_call(
        body, name=name, out_shape=out_shape, in_specs=[HBM] * (2 * n) + [ANY],
        out_specs=[SEM, SEM] + [HBM] * (2 * n) + [pl.BlockSpec(memory_space=pltpu.VMEM)],
        input_output_aliases={i: 2 + i for i in range(2 * n)},
        compiler_params=pltpu.CompilerParams(has_side_effects=DATAFLOW),
    )(*operands, after)
    return res[0], res[1], list(res[2:2 + n]), list(res[2 + n:2 + 2 * n]), res[-1]


def _split_wait(copies_fn, send_sems, recv_sems, srcs, lands, after, *, name):
    n = len(srcs)

    def body(*refs):
        for cp in copies_fn(refs[:n], refs[n:2 * n], refs[2 * n], refs[2 * n + 1]):
            cp.wait_send()
            cp.wait_recv()

    res = pl.pallas_call(
        body, name=name, out_shape=[pltpu.HBM(a.shape, a.dtype) for a in list(srcs) + list(lands)],
        in_specs=[HBM] * (2 * n) + [SEM, SEM, ANY], out_specs=[HBM] * (2 * n),
        input_output_aliases={i: i for i in range(2 * n)},
        compiler_params=pltpu.CompilerParams(has_side_effects=DATAFLOW),
    )(*srcs, *lands, send_sems, recv_sems, after)
    return list(res[:n]), list(res[n:])


def _gather_forward(lands, *, name):
    n = len(lands)

    def body(*refs):
        xs = refs[:n]
        send_sems, recv_sems, _ = refs[2 * n:]
        x, y, c = _mesh_pos()
        chips = [(1 - x, y), (x, 1 - y), (1 - x, 1 - y)]
        copies = []
        for i in range(n):
            nl = lands[i].shape[0]
            for j, (cx, cy) in enumerate(chips):
                here = xs[i].at[pl.ds(0, nl), 2 * cx + cy, c]
                cp = pltpu.make_async_remote_copy(
                    src_ref=here, dst_ref=here, send_sem=send_sems.at[3 * i + j], recv_sem=recv_sems.at[3 * i + j],
                    device_id=(x, y, 1 - c), device_id_type=MESH_ID)
                cp.start()
                copies.append(cp)
        for cp in copies:
            cp.wait()

    return pl.pallas_call(
        body, name=name, out_shape=[jax.ShapeDtypeStruct(a.shape, a.dtype) for a in lands],
        in_specs=[ANY] * n, out_specs=[ANY] * n, input_output_aliases={i: i for i in range(n)},
        scratch_shapes=[pltpu.SemaphoreType.DMA((3 * n,)), pltpu.SemaphoreType.DMA((3 * n,)),
                        pltpu.SemaphoreType.DMA((1,))],
    )(*lands)


def _pair_copies(g_refs, land_refs, send_sems, recv_sems):
    x, y, c = _mesh_pos()
    copies = []
    for i, (g_ref, land_ref) in enumerate(zip(g_refs, land_refs)):
        nl, ns = g_ref.shape[:2]
        copies.append(pltpu.make_async_remote_copy(
            src_ref=g_ref.at[pl.ds(0, nl), pl.ds(0, ns), 1 - c], dst_ref=land_ref, send_sem=send_sems.at[i],
            recv_sem=recv_sems.at[i], device_id=(x, y, 1 - c), device_id_type=MESH_ID))
    return copies


def _pair_exchange(gs, *, name):
    n = len(gs)

    def body(*refs):
        send_sems, recv_sems, _ = refs[2 * n:]
        copies = _pair_copies(refs[:n], refs[n:2 * n], send_sems, recv_sems)
        for cp in copies:
            cp.start()
        for cp in copies:
            cp.wait()

    out_shapes = [jax.ShapeDtypeStruct(g.shape[:2] + g.shape[3:], g.dtype) for g in gs]
    return _comm_call(body, gs, out_shapes, n, name=name)


def _chip_copies(p_refs, land_refs, send_sems, recv_sems):
    x, y, c = _mesh_pos()
    k_me = 2 * x + y
    chips = [(1 - x, y), (x, 1 - y), (1 - x, 1 - y)]
    copies = []
    for i, (p_ref, land_ref) in enumerate(zip(p_refs, land_refs)):
        nl = p_ref.shape[0]
        for j, (cx, cy) in enumerate(chips):
            copies.append(pltpu.make_async_remote_copy(
                src_ref=p_ref.at[pl.ds(0, nl), 2 * cx + cy], dst_ref=land_ref.at[k_me],
                send_sem=send_sems.at[3 * i + j], recv_sem=recv_sems.at[3 * i + j],
                device_id=(cx, cy, c), device_id_type=MESH_ID))
    return copies


def _chip_landing(ps):
    return [lax.empty((p.shape[1], p.shape[0]) + p.shape[2:], p.dtype) for p in ps]


def _chip_exchange(ps, *, name):
    n = len(ps)

    def body(*refs):
        send_sems, recv_sems, _ = refs[2 * n:]
        copies = _chip_copies(refs[:n], refs[n:2 * n], send_sems, recv_sems)
        for cp in copies:
            cp.start()
        for cp in copies:
            cp.wait()

    out_shapes = [jax.ShapeDtypeStruct((p.shape[1], p.shape[0]) + p.shape[2:], p.dtype) for p in ps]
    return _comm_call(body, ps, out_shapes, 3 * n, name=name)


def _pair_swap(ss, *, name):
    n = len(ss)

    def body(*refs):
        xs, outs = refs[:n], refs[n:2 * n]
        send_sems, recv_sems, _ = refs[2 * n:]
        x, y, c = _mesh_pos()
        copies = []
        for i in range(n):
            cp = pltpu.make_async_remote_copy(src_ref=xs[i], dst_ref=outs[i], send_sem=send_sems.at[i],
                                              recv_sem=recv_sems.at[i], device_id=(x, y, 1 - c),
                                              device_id_type=MESH_ID)
            cp.start()
            copies.append(cp)
        for cp in copies:
            cp.wait()

    out_shapes = [jax.ShapeDtypeStruct(s.shape, s.dtype) for s in ss]
    return _comm_call(body, ss, out_shapes, n, name=name)


def _row_tile(rows, cols):
    tr = rows
    while tr * cols > 256 * 1024 and tr % 16 == 0:
        tr //= 2
    return tr


def _pair_add(g, recv, c_idx, *, name):
    n, ns, _, rows, W = g.shape
    tr = _row_tile(rows, W)

    def body(c_ref, g_ref, r_ref, o_ref):
        o_ref[...] = (g_ref[...] + r_ref[...]).astype(BF16)

    piece = pl.BlockSpec((None, tr, W), lambda p, i, c_ref: (p, i, 0))
    grid_spec = pltpu.PrefetchScalarGridSpec(
        num_scalar_prefetch=1, grid=(n * ns, rows // tr),
        in_specs=[pl.BlockSpec((None, None, tr, W), lambda p, i, c_ref: (p, c_ref[0], i, 0)), piece],
        out_specs=piece)
    out = pl.pallas_call(
        body, name=name, grid_spec=grid_spec, out_shape=jax.ShapeDtypeStruct((n * ns, rows, W), BF16),
        compiler_params=_cparams(("parallel", "parallel")),
    )(c_idx, g.reshape(n * ns, 2, rows, W), recv.reshape(n * ns, rows, W))
    return out.reshape(n, ns, rows, W)


def _sum_pieces(land, own, chip_idx, *, name):
    n, nl, A, W = land.shape
    tr = _row_tile(A, W)

    def body(k_ref, l_ref, o_ref, out_ref):
        acc = jnp.zeros(out_ref.shape, F32)
        for k in range(n):
            acc = acc + jnp.where(k == k_ref[0], o_ref[...], l_ref[k]).astype(F32)
        out_ref[...] = acc

    grid_spec = pltpu.PrefetchScalarGridSpec(
        num_scalar_prefetch=1, grid=(nl, A // tr),
        in_specs=[pl.BlockSpec((n, None, tr, W), lambda l, i, k_ref: (0, l, i, 0)),
                  pl.BlockSpec((None, None, tr, W), lambda l, i, k_ref: (l, k_ref[0], i, 0))],
        out_specs=pl.BlockSpec((None, tr, W), lambda l, i, k_ref: (l, i, 0)))
    return pl.pallas_call(
        body, name=name, grid_spec=grid_spec, out_shape=jax.ShapeDtypeStruct((nl, A, W), F32),
        compiler_params=_cparams(("parallel", "parallel")),
    )(chip_idx, land, own)


SMALL = ("norm_mix_g", "norm_mlp_g", "final_norm_g", "fox_b_f", "mla_q_norm_g", "mla_kv_norm_g")
WEIGHT_ORDER = ("ada_w", "ada_b", "norm_mix_g", "norm_mlp_g", "fox_w_in", "fox_b_f", "fox_w_out", "mla_w_dq",
                "mla_q_norm_g", "mla_w_uq", "mla_w_dkv", "mla_kv_norm_g", "mla_w_ukv", "mla_w_out", "mlp_w1",
                "mlp_w2", "final_norm_g")


def _small_rows(vals, D):
    rows = [vals["norm_mix_g"], vals["norm_mlp_g"], vals["final_norm_g"][None, :]]
    for n in ("fox_b_f", "mla_q_norm_g", "mla_kv_norm_g"):
        flat = vals[n].reshape(-1)
        assert flat.shape[0] <= D
        rows.append(jnp.pad(flat, (0, D - flat.shape[0]))[None, :])
    return jnp.concatenate(rows, axis=0)


def _small_unrows(rows, shapes):
    L = shapes["norm_mix_g"][0]
    out = {"norm_mix_g": rows[0:L], "norm_mlp_g": rows[L:2 * L], "final_norm_g": rows[2 * L]}
    for k, n in enumerate(("fox_b_f", "mla_q_norm_g", "mla_kv_norm_g")):
        size = int(np.prod(shapes[n]))
        out[n] = rows[2 * L + 1 + k, :size].reshape(shapes[n])
    return out


def kernel(x, c, positions, ada_w, ada_b, norm_mix_g, norm_mlp_g, fox_w_in, fox_b_f, fox_w_out, mla_w_dq, mla_q_norm_g, mla_w_uq, mla_w_dkv, mla_kv_norm_g, mla_w_ukv, mla_w_out, mlp_w1, mlp_w2, final_norm_g, loss_target, m_ada_w, m_ada_b, m_norm_mix_g, m_norm_mlp_g, m_fox_w_in, m_fox_b_f, m_fox_w_out, m_mla_w_dq, m_mla_q_norm_g, m_mla_w_uq, m_mla_w_dkv, m_mla_kv_norm_g, m_mla_w_ukv, m_mla_w_out, m_mlp_w1, m_mlp_w2, m_final_norm_g, v_ada_w, v_ada_b, v_norm_mix_g, v_norm_mlp_g, v_fox_w_in, v_fox_b_f, v_fox_w_out, v_mla_w_dq, v_mla_q_norm_g, v_mla_w_uq, v_mla_w_dkv, v_mla_kv_norm_g, v_mla_w_ukv, v_mla_w_out, v_mlp_w1, v_mlp_w2, v_final_norm_g):
    args = dict(locals())
    wts = {n: args[n] for n in WEIGHT_ORDER}
    mom = {n: args["m_" + n] for n in WEIGHT_ORDER}
    var = {n: args["v_" + n] for n in WEIGHT_ORDER}
    Bl, S, D = x.shape
    T = Bl * S
    L = ada_w.shape[0]
    C = ada_w.shape[2]
    mx, my, mc = _mesh_pos()
    chip = 2 * mx + my
    dev = 4 * mx + 2 * my + mc
    c_idx = jnp.reshape(mc, (1,)).astype(jnp.int32)
    chip_idx = jnp.reshape(chip, (1,)).astype(jnp.int32)
    small = {n: wts[n] for n in SMALL}
    L2, q_cols = mla_q_norm_g.shape
    n_fox_heads = fox_b_f.shape[1]

    shards = _shard_layouts(wts)
    groups = _comm_groups(L, L2)
    slots = _layer_slots(groups)

    def row_halves(a):
        return a.reshape(a.shape[:-2] + (2, a.shape[-2] // 2, a.shape[-1]))

    def whole_rows(a):
        return a.reshape(a.shape[:2] + (a.shape[2] * a.shape[3], a.shape[4]))

    part = {g: [row_halves(shards[n][s:s + cnt]) for n, s, cnt in entries] for g, entries in groups.items()}
    mix0 = _gather_weights(part["mix0"], name="gather_mix0")
    gather_sems, after = {}, mix0[0]
    for group in ("mlp0", "rest"):
        placed = [_place_own(a, chip_idx, c_idx, name=f"gather_place_{group}_{n}")
                  for a, (n, _, _) in zip(part[group], groups[group])]
        gather_sems[group] = _split_start(_gather_copies, part[group], placed, after, name=f"gather_{group}_start",
                                          sems_per_array=4)
        after = gather_sems[group][4]

    def layer_weights(w, group, arrays):
        for (n, s, cnt), a in zip(groups[group], arrays):
            for key, view in _weight_views(n, whole_rows(a), D, n_fox_heads).items():
                for l in range(cnt):
                    w[key][s + l] = (view, l)

    w = {key: [None] * L2 for key in ("fox_qkv", "fox_f", "fox_out", "mla_down", "mla_uq", "mla_ukv", "mla_out")}
    w.update({key: [None] * L for key in ("mlp_w1", "mlp_w2")})
    layer_weights(w, "mix0", mix0)

    def gathered_now(group):
        def hook(x_now, w):
            _, landed = _split_wait(_gather_copies, *gather_sems[group][:4], x_now, name=f"gather_{group}_wait")
            layer_weights(w, group, _gather_forward(landed, name=f"gather_{group}_forward"))
            return w
        return hook

    c_pad = jnp.concatenate([c, jnp.pad(mla_q_norm_g, ((0, 8 - Bl - L2), (0, D - q_cols)))], axis=0)
    c8 = _all_gather8(c_pad, name="gather_c", in_vmem=True)
    c_all = c8[:, :Bl].reshape(N_DEV * Bl, D)
    qg4 = c8.reshape(N_CHIP, 2, 8, D)[:, 0, Bl:Bl + L2, :q_cols]
    small["mla_q_norm_g"] = jnp.transpose(qg4, (1, 0, 2)).reshape(L2, N_CHIP * q_cols)
    ada_b_cols = lax.dynamic_slice_in_dim(ada_b, chip * C, C, axis=1)[:, None, :]
    mod_cols = _ada_fwd(c_all, ada_w, ada_b_cols)
    mod8 = _all_gather8(mod_cols.reshape(L * N_DEV * Bl, C), name="gather_mod", in_vmem=True)
    mod4 = mod8.reshape(N_CHIP, 2, L, N_DEV * Bl, C)[:, 0]
    mod_me = lax.dynamic_slice_in_dim(mod4, dev * Bl, Bl, axis=2)
    mod = jnp.transpose(mod_me, (1, 2, 0, 3)).reshape(L, Bl, 6, D)
    mod = jnp.transpose(mod, (0, 2, 1, 3))[:, :, :, None, :]

    w.update(_small_layouts(small))
    mod = mod + after[0, 0]
    pending = {}

    def grad_pieces(group, g_now):
        out = []
        for n, s, cnt in groups[group]:
            qkv_f = [(g_now["fox_qkv"][j], g_now["fox_f"][j]) for j in range(s, s + cnt)] if n == "fox_in" else None
            stacked_g = None if n == "fox_in" else g_now[n][group]
            out.append(row_halves(_grad_pieces(n, stacked_g, qkv_f, n_fox_heads, N_CHIP)))
        return out

    def pair_added(group, big, sibling):
        return [_pair_add(a, r, c_idx, name=f"grad_pair_add_{group}_{n}")
                for (n, _, _), a, r in zip(groups[group], big, sibling)]

    def exchange_start(group, ps):
        pending[group] = _split_start(_chip_copies, ps, _chip_landing(ps), chip_idx,
                                      name=f"grad_exchange_{group}_start", sems_per_array=3)
        return pending[group][4]

    def bwd_layer0(g_now):
        big = grad_pieces("rest", g_now)
        landing = [lax.empty(a.shape[:2] + a.shape[3:], a.dtype) for a in big]
        pending["rest_pair"] = _split_start(_pair_copies, big, landing, chip_idx, name="grad_pair_rest_start",
                                            sems_per_array=1)
        return pending["rest_pair"][4]

    def bwd_mix0(g_now):
        send_sems, recv_sems, big, landed, _ = pending["rest_pair"]
        big, landed = _split_wait(_pair_copies, send_sems, recv_sems, big, landed, g_now["mlp_w1"]["mlp0"],
                                  name="grad_pair_rest_wait")
        exchange_start("rest", pair_added("rest", big, landed))
        big = grad_pieces("mlp0", g_now)
        return exchange_start("mlp0", pair_added("mlp0", big, _pair_exchange(big, name="grad_pair_exchange_mlp0")))

    half = ROPE_DIM // 2
    inv_freq = ROPE_THETA ** (-jnp.arange(0, ROPE_DIM, 2, dtype=F32) / ROPE_DIM)
    lane = np.arange(LANES)
    inv_freq_row = jnp.tile(inv_freq, LANES // half)[None, :]
    sign_row = jnp.asarray(np.where(lane < 2 * ROPE_DIM, np.where(lane % ROPE_DIM < half, -1.0, 1.0), 0.0), F32)[None, :]
    pos_f = positions.astype(F32).reshape(T, 1)
    loss_row, grad_x, dmod, g = _local_step(x.reshape(T, D), loss_target.reshape(T, D), pos_f, inv_freq_row, sign_row,
                                            mod, w, slots, S=S,
                                            hooks={"fwd_mlp0": gathered_now("mlp0"), "fwd_layer1": gathered_now("rest"),
                                                   "bwd_layer0": bwd_layer0, "bwd_mix0": bwd_mix0})
    g_small = _small_grads(g, n_fox_heads)
    big = grad_pieces("mix0", g)
    exchange_start("mix0", pair_added("mix0", big, _pair_exchange(big, name="grad_pair_exchange_mix0")))

    Rs = -(-(2 * L + 5) // 8) * 8
    srows = jnp.concatenate([_small_rows(g_small, D), jnp.pad(loss_row, ((0, 0), (0, D - LANES)))], axis=0)
    srows = jnp.pad(srows, ((0, Rs - srows.shape[0]), (0, 0)))
    drows = jnp.transpose(dmod[:, :, :, 0, :], (2, 0, 1, 3)).reshape(Bl * L * 6, D)
    both8 = _all_gather8(jnp.concatenate([drows, srows], axis=0), name="gather_small", in_vmem=True)
    dm8 = both8[:, :Bl * L * 6].reshape(N_DEV, Bl, L * 6, D)
    sm8 = both8[:, Bl * L * 6:]
    adb_rows, small_sum = _sum_gathered(dm8, sm8)
    grad_ada_b = adb_rows.reshape(L, 6 * D)
    loss = small_sum[2 * L + 4, 0]
    small_shapes = {n: (wts[n].shape if n != "mla_q_norm_g" else (wts[n].shape[0], N_CHIP * q_cols)) for n in SMALL}
    gs = _small_unrows(small_sum, small_shapes)
    gs["mla_q_norm_g"] = lax.dynamic_slice_in_dim(gs["mla_q_norm_g"], chip * q_cols, q_cols, axis=1)

    dmod16 = jnp.transpose(dm8.reshape(N_DEV, Bl, L, 6 * D), (2, 0, 1, 3)).reshape(L, N_DEV * Bl, 6 * D)
    dmod_cols = lax.dynamic_slice_in_dim(dmod16, chip * C, C, axis=2)
    grad_ada_w = _ada_bwd(c_all, dmod_cols)

    grads = dict(gs)
    grads["ada_w"] = grad_ada_w
    grads["ada_b"] = grad_ada_b
    delta, new_m, new_v = {}, {}, {}
    for n in ("ada_w", "ada_b"):
        delta[n], new_m[n], new_v[n] = _adamw(wts[n], grads[n], mom[n], var[n], name=f"adamw_{n}")
    shard_small_shapes = {n: wts[n].shape for n in SMALL}
    packs = [jnp.pad(_small_rows({n: src[n] for n in SMALL}, D), ((0, Rs - 2 * L - 4), (0, 0)))
             for src in (wts, grads, mom, var)]
    for dst, rows in zip((delta, new_m, new_v), _adamw(*packs, name="adamw_small")):
        dst.update(_small_unrows(rows, shard_small_shapes))

    halves = {}
    for group, after in (("rest", grad_x), ("mlp0", grad_x), ("mix0", delta["ada_w"])):
        send_sems, recv_sems, ps, lands, _ = pending[group]
        ps, lands = _split_wait(_chip_copies, send_sems, recv_sems, ps, lands, after, name=f"grad_exchange_{group}_wait")
        sums = [_sum_pieces(ld, p, chip_idx, name=f"grad_sum_{group}_{n}")
                for (n, _, _), ld, p in zip(groups[group], lands, ps)]
        swapped = _pair_swap(sums, name=f"grad_pair_swap_{group}")
        for (n, _, _), a, b in zip(groups[group], sums, swapped):
            halves[(n, group)] = (a, b)

    def all_layers(n, which):
        return jnp.concatenate([halves[(n, grp)][which] for grp in groups if (n, grp) in halves], axis=0)

    own = {n: all_layers(n, 0) for n in GATHERED}
    peer = {n: all_layers(n, 1) for n in GATHERED}
    for nat, n in (("fox_w_in", "fox_in"), ("fox_w_out", "fox_out"), ("mla_w_out", "mla_out"), ("mlp_w1", "mlp_w1"),
                   ("mlp_w2", "mlp_w2")):
        cols = wts[nat].shape[-1]
        res = _adamw_halves(_pad_lanes(wts[nat]), own[n], peer[n], _pad_lanes(mom[nat]), _pad_lanes(var[nat]), c_idx,
                            name=f"adamw_{nat}")
        grads[nat], delta[nat], new_m[nat], new_v[nat] = (a[..., :cols] for a in res)
    joined = {n: jnp.concatenate([jnp.where(mc == 0, own[n], peer[n]), jnp.where(mc == 0, peer[n], own[n])], axis=1)
              for n in ("mla_down", "mla_uq", "mla_ukv")}
    rq = mla_w_dq.shape[-1]
    grads["mla_w_dq"] = joined["mla_down"][:, :, :rq]
    grads["mla_w_dkv"] = joined["mla_down"][:, :, rq:rq + KV_RANK + ROPE_DIM]
    grads["mla_w_uq"] = jax.vmap(_uq_from_pairs)(joined["mla_uq"])
    grads["mla_w_ukv"] = jax.vmap(_ukv_from_pairs)(joined["mla_ukv"])
    for n in ("mla_w_dq", "mla_w_dkv", "mla_w_uq", "mla_w_ukv"):
        delta[n], new_m[n], new_v[n] = _adamw(wts[n], grads[n], mom[n], var[n], name=f"adamw_{n}")

    return (loss, grad_x.reshape(Bl, S, D), *[grads[n] for n in WEIGHT_ORDER], *[delta[n] for n in WEIGHT_ORDER],
            *[new_m[n] for n in WEIGHT_ORDER], *[new_v[n] for n in WEIGHT_ORDER])
```

```python
import functools

import numpy as np
import jax
import jax.numpy as jnp
from jax import lax
from jax.experimental import pallas as pl
from jax.experimental.pallas import tpu as pltpu

F32 = jnp.float32
BF16 = jnp.bfloat16
MESH_ID = pl.DeviceIdType.MESH

NORM_EPS = 1e-6
ROPE_THETA = 10000.0
HEAD_DIM = 64
ROPE_DIM = 32
KV_RANK = 128
FOX_EXTRA = 6
PAIR_Q = 256
PAIR_KV = 384
LANES = 128
ADAM_LR = 0.001
ADAM_B1 = 0.9
ADAM_B2 = 0.999
ADAM_EPS = 1e-08
ADAM_WD = 0.01
ADAM_STEP = 10
VMEM_LIMIT_V7X = 48 * 1024 * 1024
MM_VMEM_BUDGET = 36 * 1024 * 1024
NEG_BIG = -1e30
ATTN_UNROLL = 4

BIG_WEIGHTS = (("fox_w_in", 2), ("fox_w_out", 1), ("mla_w_dq", 1), ("mla_w_uq", 2), ("mla_w_dkv", 1),
               ("mla_w_ukv", 2), ("mla_w_out", 1), ("mlp_w1", 2), ("mlp_w2", 1))


def _cparams(sem=None):
    return pltpu.CompilerParams(dimension_semantics=sem, vmem_limit_bytes=VMEM_LIMIT_V7X)


def _tile(n, want):
    if n <= want:
        return n
    for t in range(want - want % LANES, 0, -LANES):
        if n % t == 0:
            return t
    raise ValueError((n, want))


def _mm(a, b, mode, *, name, out_dtypes=(F32,), epilogue=None, extras=(), rowvecs=(), tables=(),
        seq=None, a_off=0, a_sz=None, b_layer=None, out_stack=None, out_split=0, out_t=(), tm=1024, tn=1024,
        tk=2048):
    if isinstance(b, (list, tuple)):
        b, b_layer = b[b_layer]
    b_rows, b_cols = b.shape[-2], b.shape[-1]
    n_split = b.shape[1] if b.ndim == 4 else 1
    assert mode in ("nn", "nt")
    if mode == "nn":
        M, K, N = a.shape[0], b_rows, b_cols * n_split
    else:
        M, K, N = a.shape[0], b_cols * n_split, b_rows
    assert a_sz is None or a_sz == K
    tm = _tile(seq if rowvecs else M, tm)
    n_piece = N // max(out_split, n_split if mode == "nn" else 1, 1)
    tn = _tile(n_piece, tn)
    tk = _tile(K // (n_split if mode == "nt" else 1), tk)
    ne, nr, nt_ = len(extras), len(rowvecs), len(tables)
    no = len(out_dtypes)

    def vmem_estimate():
        blocks = tm * tk * a.dtype.itemsize + tk * tn * b.dtype.itemsize
        blocks += tm * tn * (sum(e.dtype.itemsize for e in extras) + sum(jnp.dtype(d).itemsize for d in out_dtypes))
        return 2 * blocks + 2 * tm * tn * 4

    while vmem_estimate() > MM_VMEM_BUDGET and max(tm, tn) > 256:
        if tn >= tm:
            tn //= 2
        else:
            tm //= 2
    nk = K // tk

    assert a_off % tk == 0
    a_spec = pl.BlockSpec((tm, tk), lambda i, j, k: (i, k + a_off // tk))
    dims = (((1,), (0,)), ((), ())) if mode == "nn" else (((1,), (1,)), ((), ()))
    lead = () if b.ndim == 2 else (b_layer,)
    sq = (None,) * (b.ndim - 2)
    if mode == "nt":
        kb = b_cols // tk
        if b.ndim == 4:
            b_spec = pl.BlockSpec(sq + (tn, tk), lambda i, j, k: lead + (k // kb, j, k % kb))
        else:
            b_spec = pl.BlockSpec(sq + (tn, tk), lambda i, j, k: lead + (j, k))
    else:
        nb = b_cols // tn
        if b.ndim == 4:
            b_spec = pl.BlockSpec(sq + (tk, tn), lambda i, j, k: lead + (j // nb, k, j % nb))
        else:
            b_spec = pl.BlockSpec(sq + (tk, tn), lambda i, j, k: lead + (k, j))
    in_specs = [a_spec, b_spec]
    in_specs += [pl.BlockSpec((tm, tn), lambda i, j, k: (i, j)) for _ in extras]
    if rowvecs:
        assert seq % tm == 0
        per = seq // tm
        in_specs += [pl.BlockSpec((None, 1, tn), lambda i, j, k: (i // per, 0, j)) for _ in rowvecs]
    in_specs += [pl.BlockSpec((tm, LANES), lambda i, j, k: (i, 0)) for _ in tables]
    operands = [a, b, *extras, *rowvecs, *tables]
    aliases = {}
    transposed = tuple(out_t) + (False,) * (no - len(out_t))
    if out_stack is None:
        out_specs = [pl.BlockSpec((tn, tm), lambda i, j, k: (j, i)) if t else pl.BlockSpec((tm, tn), lambda i, j, k: (i, j))
                     for t in transposed]
        out_shape = [jax.ShapeDtypeStruct((N, M) if t else (M, N), d) for d, t in zip(out_dtypes, transposed)]
    else:
        prev, layer, n_layers = out_stack
        assert no == 1
        if out_split:
            ob = n_piece // tn
            out_specs = [pl.BlockSpec((None, None, tm, tn), lambda i, j, k: (layer, j // ob, i, j % ob))]
            out_shape = [jax.ShapeDtypeStruct((n_layers, out_split, M, n_piece), out_dtypes[0])]
        else:
            out_specs = [pl.BlockSpec((None, tm, tn), lambda i, j, k: (layer, i, j))]
            out_shape = [jax.ShapeDtypeStruct((n_layers, M, N), out_dtypes[0])]
        if prev is not None:
            in_specs.append(pl.BlockSpec(memory_space=pl.ANY))
            aliases = {len(operands): 0}
            operands.append(prev)
    n_in = len(operands)

    def body(*refs):
        a_ref, b_ref = refs[0], refs[1]
        side = refs[2:2 + ne + nr + nt_]
        outs = refs[n_in:n_in + no]

        def finish(acc):
            res = (acc,) if epilogue is None else epilogue(acc, *[r[...] for r in side])
            for o_ref, r, t in zip(outs, res, transposed):
                o_ref[...] = (r.T if t else r).astype(o_ref.dtype)

        part = lax.dot_general(a_ref[...].astype(BF16), b_ref[...].astype(BF16), dims,
                               preferred_element_type=F32)
        if nk == 1:
            finish(part)
        else:
            acc_ref = refs[-1]
            k = pl.program_id(2)

            @pl.when(k == 0)
            def _():
                acc_ref[...] = part

            @pl.when(k > 0)
            def _():
                acc_ref[...] += part

            @pl.when(k == nk - 1)
            def _():
                finish(acc_ref[...])

    res = pl.pallas_call(
        body, name=name, grid=(M // tm, N // tn, nk), in_specs=in_specs, out_specs=out_specs,
        out_shape=out_shape, scratch_shapes=[pltpu.VMEM((tm, tn), F32)] if nk > 1 else [],
        input_output_aliases=aliases,
        compiler_params=_cparams(("parallel", "parallel", "arbitrary")),
    )(*operands)
    return res[0] if no == 1 else tuple(res)


def _rope128(x, cos_t, sin_s):
    lane = lax.broadcasted_iota(jnp.int32, x.shape, 1)
    first = (lane % ROPE_DIM) < (ROPE_DIM // 2)
    swapped = jnp.where(first, pltpu.roll(x, LANES - ROPE_DIM // 2, 1), pltpu.roll(x, ROPE_DIM // 2, 1))
    return x * cos_t + swapped * sin_s


def _rope_pairs(acc, cos_t, sin_s, sign):
    parts = []
    for p in range(acc.shape[1] // PAIR_Q):
        parts.append(acc[:, p * PAIR_Q:p * PAIR_Q + LANES])
        parts.append(_rope128(acc[:, p * PAIR_Q + LANES:(p + 1) * PAIR_Q], cos_t, sign * sin_s))
    return jnp.concatenate(parts, axis=1)


def _rope_tables(pos_f, inv_freq_row, sign_row):
    T = pos_f.shape[0]
    tt = _tile(T, 512)

    def body(p_ref, f_ref, s_ref, cos_ref, sin_ref):
        ang = p_ref[...] * f_ref[...]
        cos_ref[...] = jnp.cos(ang)
        sin_ref[...] = jnp.sin(ang) * s_ref[...]

    return pl.pallas_call(
        body, name="rope_tables", grid=(T // tt,),
        in_specs=[pl.BlockSpec((tt, 1), lambda i: (i, 0)), pl.BlockSpec((1, LANES), lambda i: (0, 0)),
                  pl.BlockSpec((1, LANES), lambda i: (0, 0))],
        out_specs=[pl.BlockSpec((tt, LANES), lambda i: (i, 0))] * 2,
        out_shape=[jax.ShapeDtypeStruct((T, LANES), F32)] * 2,
        compiler_params=_cparams(("parallel",)),
    )(pos_f, inv_freq_row, sign_row)


def _unrope(dqx, cos_t, sin_s):
    T, W = dqx.shape
    tt = _tile(T, 512)

    def body(d_ref, c_ref, s_ref, o_ref):
        o_ref[...] = _rope_pairs(d_ref[...].astype(F32), c_ref[...], s_ref[...], -1.0).astype(BF16)

    return pl.pallas_call(
        body, name="mla_unrope", grid=(T // tt,),
        in_specs=[pl.BlockSpec((tt, W), lambda i: (i, 0)), pl.BlockSpec((tt, LANES), lambda i: (i, 0)),
                  pl.BlockSpec((tt, LANES), lambda i: (i, 0))],
        out_specs=pl.BlockSpec((tt, W), lambda i: (i, 0)),
        out_shape=jax.ShapeDtypeStruct((T, W), BF16),
        compiler_params=_cparams(("parallel",)),
    )(dqx, cos_t, sin_s)


def _row_specs(tt, D, per, n):
    return [pl.BlockSpec((None, 1, D), lambda i: (i // per, 0, 0)) for _ in range(n)]


def _norm_mod(x, gain, sc, sh, *, S, name):
    T, D = x.shape
    tt = _tile(S, 512)
    per = S // tt

    def body(x_ref, g_ref, sc_ref, sh_ref, h_ref, ht_ref):
        xv = x_ref[...]
        r = lax.rsqrt(jnp.mean(xv * xv, axis=-1, keepdims=True) + NORM_EPS)
        h = (xv * r) * g_ref[...] * (1.0 + sc_ref[...]) + sh_ref[...]
        h_ref[...] = h.astype(BF16)
        ht_ref[...] = h.T.astype(BF16)

    return pl.pallas_call(
        body, name=name, grid=(T // tt,),
        in_specs=[pl.BlockSpec((tt, D), lambda i: (i, 0)), pl.BlockSpec((1, D), lambda i: (0, 0))]
        + _row_specs(tt, D, per, 2),
        out_specs=[pl.BlockSpec((tt, D), lambda i: (i, 0)), pl.BlockSpec((D, tt), lambda i: (0, i))],
        out_shape=[jax.ShapeDtypeStruct((T, D), BF16), jax.ShapeDtypeStruct((D, T), BF16)],
        compiler_params=_cparams(("parallel",)),
    )(x, gain, sc, sh)


def _norm_mod_bwd(x, dh, dres, gain, sc, *, S, name):
    T, D = x.shape
    B = T // S
    tt = _tile(S, 512)
    per = S // tt

    def body(x_ref, dh_ref, dres_ref, g_ref, sc_ref, dx_ref, dsh_ref, dsc_ref, dg_ref):
        i = pl.program_id(0)
        xv = x_ref[...]
        dhv = dh_ref[...].astype(F32)
        r = lax.rsqrt(jnp.mean(xv * xv, axis=-1, keepdims=True) + NORM_EPS)
        n = xv * r
        g = g_ref[...]
        one_sc = 1.0 + sc_ref[...]
        dn = dhv * (g * one_sc)
        dx_ref[...] = dres_ref[...] + r * (dn - n * jnp.mean(dn * n, axis=-1, keepdims=True))
        dhn = dhv * n

        @pl.when(i % per == 0)
        def _():
            dsh_ref[...] = jnp.zeros_like(dsh_ref)
            dsc_ref[...] = jnp.zeros_like(dsc_ref)

        @pl.when(i == 0)
        def _():
            dg_ref[...] = jnp.zeros_like(dg_ref)

        dsh_ref[...] += jnp.sum(dhv, axis=0, keepdims=True)
        dsc_ref[...] += jnp.sum(dhn, axis=0, keepdims=True) * g
        dg_ref[...] += jnp.sum(dhn, axis=0, keepdims=True) * one_sc

    return pl.pallas_call(
        body, name=name, grid=(T // tt,),
        in_specs=[pl.BlockSpec((tt, D), lambda i: (i, 0))] * 3 + [pl.BlockSpec((1, D), lambda i: (0, 0))]
        + _row_specs(tt, D, per, 1),
        out_specs=[pl.BlockSpec((tt, D), lambda i: (i, 0))] + _row_specs(tt, D, per, 2)
        + [pl.BlockSpec((1, D), lambda i: (0, 0))],
        out_shape=[jax.ShapeDtypeStruct((T, D), F32), jax.ShapeDtypeStruct((B, 1, D), F32),
                   jax.ShapeDtypeStruct((B, 1, D), F32), jax.ShapeDtypeStruct((1, D), F32)],
        compiler_params=_cparams(("arbitrary",)),
    )(x, dh, dres, gain, sc)


def _gate_bwd(dx, y, g, *, S, name):
    T, D = dx.shape
    B = T // S
    tt = _tile(S, 512)
    per = S // tt

    def body(dx_ref, y_ref, g_ref, dy_ref, dg_ref):
        i = pl.program_id(0)
        dxv = dx_ref[...]
        dy_ref[...] = (dxv * g_ref[...]).astype(BF16)

        @pl.when(i % per == 0)
        def _():
            dg_ref[...] = jnp.zeros_like(dg_ref)

        dg_ref[...] += jnp.sum(dxv * y_ref[...], axis=0, keepdims=True)

    return pl.pallas_call(
        body, name=name, grid=(T // tt,),
        in_specs=[pl.BlockSpec((tt, D), lambda i: (i, 0))] * 2 + _row_specs(tt, D, per, 1),
        out_specs=[pl.BlockSpec((tt, D), lambda i: (i, 0))] + _row_specs(tt, D, per, 1),
        out_shape=[jax.ShapeDtypeStruct((T, D), BF16), jax.ShapeDtypeStruct((B, 1, D), F32)],
        compiler_params=_cparams(("arbitrary",)),
    )(dx, y, g)


def _final_loss(x, target, gain):
    T, D = x.shape
    tt = _tile(T, 512)

    def body(x_ref, t_ref, g_ref, dx_ref, dg_ref, loss_ref):
        i = pl.program_id(0)
        xv = x_ref[...]
        r = lax.rsqrt(jnp.mean(xv * xv, axis=-1, keepdims=True) + NORM_EPS)
        n = xv * r
        g = g_ref[...]
        err = n * g - t_ref[...]
        dy = err * (1.0 / D)
        dn = dy * g
        dx_ref[...] = r * (dn - n * jnp.mean(dn * n, axis=-1, keepdims=True))

        @pl.when(i == 0)
        def _():
            dg_ref[...] = jnp.zeros_like(dg_ref)
            loss_ref[...] = jnp.zeros_like(loss_ref)

        dg_ref[...] += jnp.sum(dy * n, axis=0, keepdims=True)
        loss_ref[...] += jnp.sum(jnp.sum(err * err, axis=-1, keepdims=True), axis=0, keepdims=True) * (0.5 / D)

    return pl.pallas_call(
        body, name="final_loss", grid=(T // tt,),
        in_specs=[pl.BlockSpec((tt, D), lambda i: (i, 0))] * 2 + [pl.BlockSpec((1, D), lambda i: (0, 0))],
        out_specs=[pl.BlockSpec((tt, D), lambda i: (i, 0)), pl.BlockSpec((1, D), lambda i: (0, 0)),
                   pl.BlockSpec((1, LANES), lambda i: (0, 0))],
        out_shape=[jax.ShapeDtypeStruct((T, D), F32), jax.ShapeDtypeStruct((1, D), F32),
                   jax.ShapeDtypeStruct((1, LANES), F32)],
        compiler_params=_cparams(("arbitrary",)),
    )(x, target, gain)


def _head_masks(ew):
    lane = lax.broadcasted_iota(jnp.int32, (1, PAIR_Q), 1)
    m0 = (lane < HEAD_DIM) | ((lane >= LANES) & (lane < LANES + ew))
    m1 = ((lane >= HEAD_DIM) & (lane < LANES)) | ((lane >= LANES + ew) & (lane < LANES + 2 * ew))
    return m0, m1


def _dot_nt(a, b):
    return lax.dot_general(a, b, (((1,), (1,)), ((), ())), preferred_element_type=F32)


def _dot_tn(a, b):
    return lax.dot_general(a, b, (((0,), (0,)), ((), ())), preferred_element_type=F32)


def _lane_halves(x, op):
    acc = x[:, 0:LANES]
    for g in range(1, x.shape[1] // LANES):
        acc = op(acc, x[:, g * LANES:(g + 1) * LANES])
    return acc


def _head_rows(cols_lane_replicated):
    t = cols_lane_replicated.T
    sub = lax.broadcasted_iota(jnp.int32, (8, t.shape[1]), 0)
    return jnp.where(sub == 1, t[HEAD_DIM:HEAD_DIM + 8], t[0:8])


def _attn_trip(n_blocks):
    return ATTN_UNROLL if n_blocks % ATTN_UNROLL == 0 else 2


def _attn_fwd(qx, kvx, *, S, scale, ew, name):
    T = qx.shape[0]
    P = qx.shape[1] // PAIR_Q
    B = T // S
    tq = _tile(S, 256)
    nq = S // tq
    big = _attn_trip(nq)

    def body(q_ref, kv_ref, o_ref, lse_ref, ot_ref, m_sc, l_sc, acc_sc):
        qi = pl.program_id(2)
        q = q_ref[...]
        masks = _head_masks(ew)
        qh = [jnp.where(m, q, jnp.zeros_like(q)) for m in masks]

        def logits(h, k, kj):
            s = _dot_nt(qh[h], k)
            if scale != 1.0:
                s = s * scale
            row = lax.broadcasted_iota(jnp.int32, s.shape, 0)
            col = lax.broadcasted_iota(jnp.int32, s.shape, 1)
            return jnp.where(col - row <= (qi - kj) * tq, s, NEG_BIG)

        def trip(first, count):
            rows = [pl.ds(pl.multiple_of((first + u) * tq, tq), tq) for u in range(count)]
            for h in range(2):
                ss = [logits(h, kv_ref[rows[u], 0:PAIR_Q], first + u) for u in range(count)]
                m_prev = m_sc[h]
                m_elem = m_prev
                for s in ss:
                    m_elem = jnp.maximum(m_elem, _lane_halves(s, jnp.maximum))
                m_new = jnp.broadcast_to(jnp.max(m_elem, axis=1, keepdims=True), (tq, LANES))
                alpha = jnp.exp(m_prev - m_new)
                l = alpha * l_sc[h]
                acc = alpha * acc_sc[h]
                for u, s in enumerate(ss):
                    p = jnp.concatenate([jnp.exp(s[:, g * LANES:(g + 1) * LANES] - m_new)
                                         for g in range(tq // LANES)], axis=1)
                    l = l + _lane_halves(p, jnp.add)
                    acc = acc + jnp.dot(p.astype(BF16), kv_ref[rows[u], PAIR_Q:PAIR_KV], preferred_element_type=F32)
                m_sc[h] = m_new
                l_sc[h] = l
                acc_sc[h] = acc

        m_sc[...] = jnp.full(m_sc.shape, NEG_BIG, F32)
        l_sc[...] = jnp.zeros_like(l_sc)
        acc_sc[...] = jnp.zeros_like(acc_sc)

        def loop_body(t, carry):
            trip(t * big, big)
            return carry

        if big == 2:
            lax.fori_loop(0, (qi + 2) // 2, loop_body, 0)
        else:
            trips = (qi + 2) // big
            lax.fori_loop(0, trips, loop_body, 0)

            @pl.when(qi % big <= 1)
            def _():
                trip(trips * big, 2)

        lane = lax.broadcasted_iota(jnp.int32, (tq, LANES), 1)
        lo = lane < HEAD_DIM
        l = [jnp.sum(l_sc[h], axis=1, keepdims=True) for h in range(2)]
        o = jnp.where(lo, acc_sc[0] / l[0], acc_sc[1] / l[1])
        o_ref[...] = o.astype(BF16)
        ot_ref[...] = o.T.astype(BF16)
        lse_ref[...] = _head_rows(jnp.where(lo, m_sc[0] + jnp.log(l[0]), m_sc[1] + jnp.log(l[1])))

    return pl.pallas_call(
        body, name=name, grid=(B, P, nq),
        in_specs=[pl.BlockSpec((tq, PAIR_Q), lambda b, p, i: (b * nq + i, p)),
                  pl.BlockSpec((S, PAIR_KV), lambda b, p, i: (b, p))],
        out_specs=[pl.BlockSpec((tq, LANES), lambda b, p, i: (b * nq + i, p)),
                   pl.BlockSpec((None, None, 8, tq), lambda b, p, i: (b * nq + i, p, 0, 0)),
                   pl.BlockSpec((LANES, tq), lambda b, p, i: (p, b * nq + i))],
        out_shape=[jax.ShapeDtypeStruct((T, P * LANES), BF16), jax.ShapeDtypeStruct((T // tq, P, 8, tq), F32),
                   jax.ShapeDtypeStruct((P * LANES, T), BF16)],
        scratch_shapes=[pltpu.VMEM((2, tq, LANES), F32)] * 3,
        compiler_params=_cparams(("parallel", "parallel", "arbitrary")),
    )(qx, kvx)


def _attn_bwd(qx, kvx, o, lse, do, *, S, scale, ew, name, bias_grad=False):
    T = qx.shape[0]
    P = qx.shape[1] // PAIR_Q
    B = T // S
    tq = _tile(S, 256)
    nq = S // tq
    big = _attn_trip(nq)

    def body(q_ref, kv_ref, o_ref, lse_ref, do_ref, dq_ref, dkv_ref, *rest):
        kj = pl.program_id(2)
        if bias_grad:
            csum_ref, rsum_ref, dq_sc, delta_sc, dk_sc, dv_sc, cs_sc = rest
            cs_sc[...] = jnp.zeros_like(cs_sc)

            @pl.when(kj == 0)
            def _():
                rsum_ref[...] = jnp.zeros_like(rsum_ref)
        else:
            dq_sc, delta_sc, dk_sc, dv_sc = rest
        masks = _head_masks(ew)
        lane = lax.broadcasted_iota(jnp.int32, (tq, LANES), 1)
        lo = lane < HEAD_DIM
        vmask = [lo, jnp.logical_not(lo)]

        @pl.when(kj == 0)
        def _():
            dq_sc[...] = jnp.zeros_like(dq_sc)
            for c in range(nq):
                rows = pl.ds(c * tq, tq)
                x = do_ref[rows, :].astype(F32) * o_ref[rows, :].astype(F32)
                r0 = jnp.sum(jnp.where(lo, x, 0.0), axis=1, keepdims=True)
                r1 = jnp.sum(jnp.where(lo, 0.0, x), axis=1, keepdims=True)
                delta_sc[c] = _head_rows(jnp.where(lo, r0, r1))

        k = kv_ref[:, 0:PAIR_Q]
        v = kv_ref[:, PAIR_Q:PAIR_KV]
        kh = [jnp.where(m, k, jnp.zeros_like(k)) for m in masks]
        vh = [jnp.where(m, v, jnp.zeros_like(v)) for m in vmask]
        dk_sc[...] = jnp.zeros_like(dk_sc)
        dv_sc[...] = jnp.zeros_like(dv_sc)

        def step(qi):
            rows = pl.ds(pl.multiple_of(qi * tq, tq), tq)
            q = q_ref[rows, :]
            dov = do_ref[rows, :]
            lse8 = lse_ref[qi]
            dl8 = delta_sc[qi]
            for h in range(2):
                st = _dot_nt(kh[h], q)
                if scale != 1.0:
                    st = st * scale
                key = lax.broadcasted_iota(jnp.int32, st.shape, 0)
                qry = lax.broadcasted_iota(jnp.int32, st.shape, 1)
                st = jnp.where(key - qry <= (qi - kj) * tq, st, NEG_BIG)
                pt = jnp.exp(st - lse8[h:h + 1, :])
                dpt = _dot_nt(vh[h], dov)
                dst = pt * (dpt - dl8[h:h + 1, :])
                if bias_grad:
                    cs_sc[h] += _lane_halves(dst, jnp.add)
                    rsum_ref[qi, h:h + 1, :] += jnp.sum(dst, axis=0, keepdims=True)
                if scale != 1.0:
                    dst = dst * scale
                ptb = pt.astype(BF16)
                dstb = dst.astype(BF16)
                dv_sc[h] += jnp.dot(ptb, dov, preferred_element_type=F32)
                dk_sc[h] += jnp.dot(dstb, q, preferred_element_type=F32)
                dq_sc[rows, :] += _dot_tn(dstb, kh[h])

        def loop_body(t, carry):
            for u in range(big):
                step(t * big + u)
            return carry

        if big == 2:
            lax.fori_loop(kj // 2, nq // 2, loop_body, 0)
        else:
            half_empty = kj % big >= 2
            lax.fori_loop(kj // big + half_empty.astype(jnp.int32), nq // big, loop_body, 0)

            @pl.when(half_empty)
            def _():
                for u in range(2):
                    step((kj // big) * big + 2 + u)
        dkv_ref[:, 0:PAIR_Q] = (jnp.where(masks[0], dk_sc[0], 0.0) + jnp.where(masks[1], dk_sc[1], 0.0)).astype(BF16)
        dkv_ref[:, PAIR_Q:PAIR_KV] = jnp.where(lo, dv_sc[0], dv_sc[1]).astype(BF16)
        if bias_grad:
            csum_ref[...] = jnp.where(lo, jnp.sum(cs_sc[0], axis=1, keepdims=True),
                                      jnp.sum(cs_sc[1], axis=1, keepdims=True))

        @pl.when(kj == nq - 1)
        def _():
            dq_ref[...] = dq_sc[...].astype(BF16)

    rows_spec = pl.BlockSpec((nq, None, 8, tq), lambda b, p, j: (b, p, 0, 0))
    out_specs = [pl.BlockSpec((S, PAIR_Q), lambda b, p, j: (b, p)),
                 pl.BlockSpec((tq, PAIR_KV), lambda b, p, j: (b * nq + j, p))]
    out_shape = [jax.ShapeDtypeStruct((T, P * PAIR_Q), BF16), jax.ShapeDtypeStruct((T, P * PAIR_KV), BF16)]
    scratch = [pltpu.VMEM((S, PAIR_Q), F32), pltpu.VMEM((nq, 8, tq), F32),
               pltpu.VMEM((2, tq, PAIR_Q), F32), pltpu.VMEM((2, tq, LANES), F32)]
    if bias_grad:
        out_specs += [pl.BlockSpec((tq, LANES), lambda b, p, j: (b * nq + j, p)), rows_spec]
        out_shape += [jax.ShapeDtypeStruct((T, P * LANES), F32), jax.ShapeDtypeStruct((T // tq, P, 8, tq), F32)]
        scratch.append(pltpu.VMEM((2, tq, LANES), F32))
    return pl.pallas_call(
        body, name=name, grid=(B, P, nq),
        in_specs=[pl.BlockSpec((S, PAIR_Q), lambda b, p, j: (b, p)),
                  pl.BlockSpec((tq, PAIR_KV), lambda b, p, j: (b * nq + j, p)),
                  pl.BlockSpec((S, LANES), lambda b, p, j: (b, p)), rows_spec,
                  pl.BlockSpec((S, LANES), lambda b, p, j: (b, p))],
        out_specs=out_specs, out_shape=out_shape, scratch_shapes=scratch,
        compiler_params=_cparams(("parallel", "parallel", "arbitrary")),
    )(qx, kvx, o, lse, do)


def _fox_consts(P):
    H = 2 * P
    eq = np.zeros((3 * LANES, P * LANES), np.float32)
    ek = np.zeros((3 * LANES, P * LANES), np.float32)
    ones_q = np.zeros((1, P * LANES), np.float32)
    ones_k = np.zeros((1, P * LANES), np.float32)
    for h in range(H):
        base = (h // 2) * LANES + FOX_EXTRA * (h % 2)
        for part in range(3):
            eq[part * LANES + h, base + part] = 1.0
            ones_q[0, base + 3 + part] = 1.0
            ones_k[0, base + part] = 1.0
            ek[part * LANES + h, base + 3 + part] = -1.0
    return eq, ek, ones_q, ones_k


def _split3(f):
    hi = f.astype(BF16)
    r = f - hi.astype(F32)
    mid = r.astype(BF16)
    lo = (r - mid.astype(F32)).astype(BF16)
    return hi, mid, lo


def _tri_sum(tri, x):
    hi, mid, lo = _split3(x)
    return (jnp.dot(tri, hi, preferred_element_type=F32) + jnp.dot(tri, mid, preferred_element_type=F32)
            + jnp.dot(tri, lo, preferred_element_type=F32))


def _log1p_pos(e):
    return jnp.where(e < 0.01, e * (1.0 - e * (0.5 - e * (1.0 / 3.0))), jnp.log(1.0 + e))


def _fox_prep(qkv, fl, b_row, *, S, D, name):
    T = qkv.shape[0]
    P = D // LANES
    B = T // S
    tt = _tile(S, 256)
    per = S // tt
    eq, ek, ones_q, ones_k = _fox_consts(P)
    q_scale = HEAD_DIM ** -0.5

    def body(q_ref, k_ref, v_ref, fl_ref, b_ref, eq_ref, ek_ref, oq_ref, ok_ref, qx_ref, kvx_ref, carry):
        i = pl.program_id(1)

        @pl.when(i == 0)
        def _():
            carry[...] = jnp.zeros_like(carry)

        z = fl_ref[...] + b_ref[...]
        logf = jnp.minimum(z, 0.0) - _log1p_pos(jnp.exp(-jnp.abs(z)))
        row = lax.broadcasted_iota(jnp.int32, (tt, tt), 0)
        col = lax.broadcasted_iota(jnp.int32, (tt, tt), 1)
        tri = (col <= row).astype(BF16)
        f = _tri_sum(tri, logf) + carry[...]
        carry[...] = f[tt - 1:tt, :]
        parts = jnp.concatenate(_split3(f), axis=1)
        xq = jnp.dot(parts, eq_ref[...], preferred_element_type=F32) + oq_ref[...]
        xk = jnp.dot(parts, ek_ref[...], preferred_element_type=F32) + ok_ref[...]
        for p in range(P):
            c = slice(p * LANES, (p + 1) * LANES)
            qx_ref[:, p * PAIR_Q:p * PAIR_Q + LANES] = (q_ref[:, c].astype(F32) * q_scale).astype(BF16)
            qx_ref[:, p * PAIR_Q + LANES:(p + 1) * PAIR_Q] = xq[:, c].astype(BF16)
            kvx_ref[:, p * PAIR_KV:p * PAIR_KV + LANES] = k_ref[:, c]
            kvx_ref[:, p * PAIR_KV + LANES:p * PAIR_KV + PAIR_Q] = xk[:, c].astype(BF16)
            kvx_ref[:, p * PAIR_KV + PAIR_Q:(p + 1) * PAIR_KV] = v_ref[:, c]

    tok = lambda b, i: (b * per + i, 0)
    const = lambda b, i: (0, 0)
    return pl.pallas_call(
        body, name=name, grid=(B, per),
        in_specs=[pl.BlockSpec((tt, D), lambda b, i: (b * per + i, 0)),
                  pl.BlockSpec((tt, D), lambda b, i: (b * per + i, 1)),
                  pl.BlockSpec((tt, D), lambda b, i: (b * per + i, 2)),
                  pl.BlockSpec((tt, LANES), tok), pl.BlockSpec((1, LANES), const),
                  pl.BlockSpec(eq.shape, const), pl.BlockSpec(ek.shape, const),
                  pl.BlockSpec(ones_q.shape, const), pl.BlockSpec(ones_k.shape, const)],
        out_specs=[pl.BlockSpec((tt, P * PAIR_Q), tok), pl.BlockSpec((tt, P * PAIR_KV), tok)],
        out_shape=[jax.ShapeDtypeStruct((T, P * PAIR_Q), BF16), jax.ShapeDtypeStruct((T, P * PAIR_KV), BF16)],
        scratch_shapes=[pltpu.VMEM((1, LANES), F32)],
        compiler_params=_cparams(("arbitrary", "arbitrary")),
    )(qkv, qkv, qkv, fl, b_row, jnp.asarray(eq, BF16), jnp.asarray(ek, BF16), jnp.asarray(ones_q), jnp.asarray(ones_k))


def _fox_unprep(dqx, dkvx, csum, rsum, fl, b_row, *, S, D, name):
    T = dqx.shape[0]
    P = D // LANES
    B = T // S
    tt = _tile(S, 256)
    per = S // tt
    q_scale = HEAD_DIM ** -0.5

    def body(dq_ref, dkv_ref, cs_ref, rs_ref, fl_ref, b_ref, dqkv_ref, dfl_ref, db_ref, carry):
        b = pl.program_id(0)
        i = pl.program_id(1)

        @pl.when(i == 0)
        def _():
            carry[...] = jnp.zeros_like(carry)

        @pl.when((i == 0) & (b == 0))
        def _():
            db_ref[...] = jnp.zeros_like(db_ref)

        df = rs_ref[...] - cs_ref[...]
        for p in range(P):
            rq = slice(p * LANES, (p + 1) * LANES)
            dqkv_ref[:, rq] = (dq_ref[:, p * PAIR_Q:p * PAIR_Q + LANES].astype(F32) * q_scale).astype(BF16)
            dqkv_ref[:, D + p * LANES:D + (p + 1) * LANES] = dkv_ref[:, p * PAIR_KV:p * PAIR_KV + LANES]
            dqkv_ref[:, 2 * D + p * LANES:2 * D + (p + 1) * LANES] = dkv_ref[:, p * PAIR_KV + PAIR_Q:(p + 1) * PAIR_KV]
        row = lax.broadcasted_iota(jnp.int32, (tt, tt), 0)
        col = lax.broadcasted_iota(jnp.int32, (tt, tt), 1)
        tri = (col >= row).astype(BF16)
        dlogf = _tri_sum(tri, df) + carry[...]
        carry[...] = dlogf[0:1, :]
        z = fl_ref[...] + b_ref[...]
        e = jnp.exp(-jnp.abs(z))
        sig_neg = jnp.where(z >= 0.0, e, 1.0) / (1.0 + e)
        dfl = dlogf * sig_neg
        dfl_ref[...] = dfl.astype(BF16)
        db_ref[...] += jnp.sum(dfl, axis=0, keepdims=True)

    rev = lambda b, i: (b * per + per - 1 - i, 0)
    const = lambda b, i: (0, 0)
    return pl.pallas_call(
        body, name=name, grid=(B, per),
        in_specs=[pl.BlockSpec((tt, P * PAIR_Q), rev), pl.BlockSpec((tt, P * PAIR_KV), rev),
                  pl.BlockSpec((tt, LANES), rev), pl.BlockSpec((tt, LANES), rev), pl.BlockSpec((tt, LANES), rev),
                  pl.BlockSpec((1, LANES), const)],
        out_specs=[pl.BlockSpec((tt, 3 * D), rev), pl.BlockSpec((tt, LANES), rev), pl.BlockSpec((1, LANES), const)],
        out_shape=[jax.ShapeDtypeStruct((T, 3 * D), BF16), jax.ShapeDtypeStruct((T, LANES), BF16),
                   jax.ShapeDtypeStruct((1, LANES), F32)],
        scratch_shapes=[pltpu.VMEM((1, LANES), F32)],
        compiler_params=_cparams(("arbitrary", "arbitrary")),
    )(dqx, dkvx, csum, rsum, fl, b_row)


def _rms(x):
    r = lax.rsqrt(jnp.mean(x * x, axis=-1, keepdims=True) + NORM_EPS)
    return x * r, r


def _mla_mid(lat, gq, gkv, cos_t, sin_s, *, name):
    T, W = lat.shape
    Rq = W - 2 * LANES
    tt = _tile(T, 512)

    def body(l_ref, gq_ref, gkv_ref, c_ref, s_ref, o_ref, ot_ref):
        nq, _ = _rms(l_ref[:, 0:Rq])
        nkv, _ = _rms(l_ref[:, Rq:Rq + LANES])
        parts = [nq * gq_ref[...], nkv * gkv_ref[...], _rope128(l_ref[:, Rq + LANES:W], c_ref[...], s_ref[...])]
        out = jnp.concatenate(parts, axis=1)
        o_ref[...] = out.astype(BF16)
        ot_ref[...] = out.T.astype(BF16)

    return pl.pallas_call(
        body, name=name, grid=(T // tt,),
        in_specs=[pl.BlockSpec((tt, W), lambda i: (i, 0)), pl.BlockSpec((1, Rq), lambda i: (0, 0)),
                  pl.BlockSpec((1, LANES), lambda i: (0, 0)), pl.BlockSpec((tt, LANES), lambda i: (i, 0)),
                  pl.BlockSpec((tt, LANES), lambda i: (i, 0))],
        out_specs=[pl.BlockSpec((tt, W), lambda i: (i, 0)), pl.BlockSpec((W, tt), lambda i: (0, i))],
        out_shape=[jax.ShapeDtypeStruct((T, W), BF16), jax.ShapeDtypeStruct((W, T), BF16)],
        compiler_params=_cparams(("parallel",)),
    )(lat, gq, gkv, cos_t, sin_s)


def _mla_mid_bwd(lat, dcq, dckr, gq, gkv, cos_t, sin_s, *, name):
    T, W = lat.shape
    Rq = W - 2 * LANES
    tt = _tile(T, 512)

    def norm_bwd(x, dy, g):
        n, r = _rms(x)
        dn = dy * g
        return r * (dn - n * jnp.mean(dn * n, axis=-1, keepdims=True)), jnp.sum(dy * n, axis=0, keepdims=True)

    def body(l_ref, dq_ref, dk_ref, gq_ref, gkv_ref, c_ref, s_ref, o_ref, dgq_ref, dgkv_ref):
        i = pl.program_id(0)

        @pl.when(i == 0)
        def _():
            dgq_ref[...] = jnp.zeros_like(dgq_ref)
            dgkv_ref[...] = jnp.zeros_like(dgkv_ref)

        dxq, dgq = norm_bwd(l_ref[:, 0:Rq], dq_ref[...], gq_ref[...])
        dxkv, dgkv = norm_bwd(l_ref[:, Rq:Rq + LANES], dk_ref[:, 0:LANES], gkv_ref[...])
        o_ref[:, 0:Rq] = dxq.astype(BF16)
        o_ref[:, Rq:Rq + LANES] = dxkv.astype(BF16)
        o_ref[:, Rq + LANES:W] = _rope128(dk_ref[:, LANES:2 * LANES], c_ref[...], -s_ref[...]).astype(BF16)
        dgq_ref[...] += dgq
        dgkv_ref[...] += dgkv

    return pl.pallas_call(
        body, name=name, grid=(T // tt,),
        in_specs=[pl.BlockSpec((tt, W), lambda i: (i, 0)), pl.BlockSpec((tt, Rq), lambda i: (i, 0)),
                  pl.BlockSpec((tt, 2 * LANES), lambda i: (i, 0)), pl.BlockSpec((1, Rq), lambda i: (0, 0)),
                  pl.BlockSpec((1, LANES), lambda i: (0, 0)), pl.BlockSpec((tt, LANES), lambda i: (i, 0)),
                  pl.BlockSpec((tt, LANES), lambda i: (i, 0))],
        out_specs=[pl.BlockSpec((tt, W), lambda i: (i, 0)), pl.BlockSpec((1, Rq), lambda i: (0, 0)),
                   pl.BlockSpec((1, LANES), lambda i: (0, 0))],
        out_shape=[jax.ShapeDtypeStruct((T, W), BF16), jax.ShapeDtypeStruct((1, Rq), F32),
                   jax.ShapeDtypeStruct((1, LANES), F32)],
        compiler_params=_cparams(("arbitrary",)),
    )(lat, dcq, dckr, gq, gkv, cos_t, sin_s)


def _uq_to_pairs(w):
    Rq = w.shape[0]
    P = w.shape[1] // (2 * (HEAD_DIM + ROPE_DIM))
    w4 = w.reshape(Rq, P, 2, HEAD_DIM + ROPE_DIM)
    nope = w4[..., :HEAD_DIM].reshape(Rq, P, 2 * HEAD_DIM)
    rope = w4[..., HEAD_DIM:].reshape(Rq, P, 2 * ROPE_DIM)
    pad = jnp.zeros((Rq, P, PAIR_Q - 2 * HEAD_DIM - 2 * ROPE_DIM), w.dtype)
    return jnp.concatenate([nope, rope, pad], axis=-1).reshape(Rq, P * PAIR_Q)


def _uq_from_pairs(g):
    Rq = g.shape[0]
    P = g.shape[1] // PAIR_Q
    g3 = g.reshape(Rq, P, PAIR_Q)
    nope = g3[..., :2 * HEAD_DIM].reshape(Rq, P, 2, HEAD_DIM)
    rope = g3[..., 2 * HEAD_DIM:2 * HEAD_DIM + 2 * ROPE_DIM].reshape(Rq, P, 2, ROPE_DIM)
    return jnp.concatenate([nope, rope], axis=-1).reshape(Rq, P * 2 * (HEAD_DIM + ROPE_DIM))


def _ukv_to_pairs(w):
    P = w.shape[1] // (4 * HEAD_DIM)
    w4 = w.reshape(KV_RANK, P, 2, 2 * HEAD_DIM)
    kn = w4[..., :HEAD_DIM].reshape(KV_RANK, P, 2 * HEAD_DIM)
    vv = w4[..., HEAD_DIM:].reshape(KV_RANK, P, 2 * HEAD_DIM)
    top = jnp.concatenate([kn, jnp.zeros((KV_RANK, P, LANES), w.dtype), vv], axis=-1)
    place = np.zeros((LANES, P, PAIR_KV), np.float32)
    for r in range(ROPE_DIM):
        place[r, :, LANES + r] = 1.0
        place[r, :, LANES + ROPE_DIM + r] = 1.0
    return jnp.concatenate([top, jnp.asarray(place, w.dtype)], axis=0).reshape(KV_RANK + LANES, P * PAIR_KV)


def _ukv_from_pairs(g):
    P = g.shape[1] // PAIR_KV
    g3 = g[:KV_RANK].reshape(KV_RANK, P, PAIR_KV)
    kn = g3[..., :2 * HEAD_DIM].reshape(KV_RANK, P, 2, HEAD_DIM)
    vv = g3[..., PAIR_Q:].reshape(KV_RANK, P, 2, HEAD_DIM)
    return jnp.concatenate([kn, vv], axis=-1).reshape(KV_RANK, P * 4 * HEAD_DIM)


def _mlp_fwd(h2, w, i, x1, gate, *, S):
    def act(acc):
        u = jnp.square(jnp.maximum(acc, 0.0))
        return acc, u, u

    p, u, u_t = _mm(h2, w["mlp_w1"], "nn", name=f"mlp_up_{i}", b_layer=i, out_dtypes=(BF16, BF16, BF16),
                    out_t=(False, False, True), epilogue=act)
    x2, z = _mm(u, w["mlp_w2"], "nn", name=f"mlp_down_{i}", b_layer=i, out_dtypes=(F32, F32), extras=(x1,),
                rowvecs=(gate,), seq=S, epilogue=lambda acc, xr, g: (xr + g * acc, acc))
    return x2, (p, u_t, z)


STACKED_GRADS = ("fox_out", "mla_down", "mla_uq", "mla_ukv", "mla_out", "mlp_w1", "mlp_w2")


def _local_step(x, target, pos_f, inv_freq_row, sign_row, mod, w, slots, *, S, hooks=None):
    hooks = hooks or {}
    T, D = x.shape
    L = mod.shape[0]
    L2 = len(w["fox_out"])
    cos_t, sin_s = _rope_tables(pos_f, inv_freq_row, sign_row)
    saved = []
    for i in range(L):
        j = i // 2
        sh_m, sc_m, g_m, sh_f, sc_f, g_f = (mod[i, s] for s in range(6))
        h, h_t = _norm_mod(x, w["norm_mix_g"][i], sc_m, sh_m, S=S, name=f"norm_mix_{i}")
        if i % 2 == 0:
            qkv = _mm(h, w["fox_qkv"], "nn", name=f"fox_qkv_{i}", b_layer=j, out_dtypes=(BF16,))
            fl = _mm(h, w["fox_f"], "nn", name=f"fox_f_{i}", b_layer=j)
            qx, kvx = _fox_prep(qkv, fl, w["fox_b"][j], S=S, D=D, name=f"fox_prep_{i}")
            o, lse, o_t = _attn_fwd(qx, kvx, S=S, scale=1.0, ew=FOX_EXTRA, name=f"fox_attn_{i}")
            mix = (qx, kvx, o, lse, o_t, fl)
            w_out = w["fox_out"]
        else:
            lat = _mm(h, w["mla_down"], "nn", name=f"mla_down_{i}", b_layer=j)
            Rq = lat.shape[1] - 2 * LANES
            cqr, cqr_t = _mla_mid(lat, w["mla_gq"][j], w["mla_gkv"][j], cos_t, sin_s, name=f"mla_mid_{i}")
            qx = _mm(cqr, w["mla_uq"], "nn", name=f"mla_uq_{i}", b_layer=j, out_dtypes=(BF16,), a_sz=Rq, tk=Rq,
                     tables=(cos_t, sin_s), epilogue=lambda acc, c, s: (_rope_pairs(acc, c, s, 1.0),))
            kvx = _mm(cqr, w["mla_ukv"], "nn", name=f"mla_ukv_{i}", b_layer=j, out_dtypes=(BF16,), a_off=Rq,
                      a_sz=2 * LANES, tk=2 * LANES, tn=PAIR_KV)
            o, lse, o_t = _attn_fwd(qx, kvx, S=S, scale=(HEAD_DIM + ROPE_DIM) ** -0.5, ew=ROPE_DIM,
                                    name=f"mla_attn_{i}")
            mix = (qx, kvx, o, lse, o_t, lat, cqr_t)
            w_out = w["mla_out"]
        x1, y = _mm(o, w_out, "nn", name=f"mix_out_{i}", b_layer=j, out_dtypes=(F32, F32), extras=(x,),
                    rowvecs=(g_m,), seq=S, epilogue=lambda acc, xr, g: (xr + g * acc, acc))
        h2, h2_t = _norm_mod(x1, w["norm_mlp_g"][i], sc_f, sh_f, S=S, name=f"norm_mlp_{i}")
        if i == 0 and "fwd_mlp0" in hooks:
            w = hooks["fwd_mlp0"](x1, w)
        x2, mlp = _mlp_fwd(h2, w, i, x1, g_f, S=S)
        saved.append((x, h_t, mix, y, x1, h2_t, mlp))
        x = x2
        if i == 0 and "fwd_layer1" in hooks:
            w = hooks["fwd_layer1"](x, w)

    dx, dg_final, loss = _final_loss(x, target, w["final_norm_g"])
    n_split = w["mlp_w1"][0][0].shape[1]

    grads = {k: [None] * len(w[k]) for k in ("norm_mix_g", "norm_mlp_g", "fox_b", "mla_gq", "mla_gkv")}
    grads.update({k: [None] * L2 for k in ("fox_qkv", "fox_f")})
    grads.update({k: {} for k in STACKED_GRADS})
    grads["final_norm_g"] = dg_final

    def stacked(key, layer, _, a_t, b, **kw):
        group, idx, count = slots[(key, layer)]
        grads[key][group] = _mm(a_t, b, "nn", out_stack=(grads[key].get(group), idx, count), **kw)

    dmod = [None] * L
    for i in reversed(range(L)):
        j = i // 2
        x0, h_t, mix, y, x1, h2_t, (p, u_t, z) = saved[i]
        sh_m, sc_m, g_m, sh_f, sc_f, g_f = (mod[i, s] for s in range(6))
        if i == 0 and "bwd_layer0" in hooks:
            g_f = g_f + hooks["bwd_layer0"](grads)[0, 0]
        dz, dg_f = _gate_bwd(dx, z, g_f, S=S, name=f"gate_mlp_bwd_{i}")
        stacked("mlp_w2", i, L, u_t, dz, name=f"mlp_w2_grad_{i}")
        dp = _mm(dz, w["mlp_w2"], "nt", name=f"mlp_down_bwd_{i}", b_layer=i, out_dtypes=(BF16,), extras=(p,),
                 epilogue=lambda acc, pv: (acc * (2.0 * jnp.maximum(pv.astype(F32), 0.0)),))
        stacked("mlp_w1", i, L, h2_t, dp, name=f"mlp_w1_grad_{i}", out_split=n_split)
        if i == 0 and "bwd_mix0" in hooks:
            g_m = g_m + hooks["bwd_mix0"](grads)[0, 0]
        dh2 = _mm(dp, w["mlp_w1"], "nt", name=f"mlp_up_bwd_{i}", b_layer=i)
        dx1, dsh_f, dsc_f, dgn = _norm_mod_bwd(x1, dh2, dx, w["norm_mlp_g"][i], sc_f, S=S, name=f"norm_mlp_bwd_{i}")
        grads["norm_mlp_g"][i] = dgn
        dy, dg_m = _gate_bwd(dx1, y, g_m, S=S, name=f"gate_mix_bwd_{i}")
        if i % 2 == 0:
            qx, kvx, o, lse, o_t, fl = mix
            stacked("fox_out", j, L2, o_t, dy, name=f"fox_out_grad_{i}")
            do = _mm(dy, w["fox_out"], "nt", name=f"fox_out_bwd_{i}", b_layer=j, out_dtypes=(BF16,))
            dqx, dkvx, csum, rsum = _attn_bwd(qx, kvx, o, lse, do, S=S, scale=1.0, ew=FOX_EXTRA,
                                              name=f"fox_attn_bwd_{i}", bias_grad=True)
            n_heads = D // HEAD_DIM
            csum = jnp.pad(csum.reshape(T, n_heads, HEAD_DIM)[:, :, 0], ((0, 0), (0, LANES - n_heads)))
            rsum = jnp.transpose(rsum[:, :, :2, :], (0, 3, 1, 2)).reshape(T, n_heads)
            rsum = jnp.pad(rsum, ((0, 0), (0, LANES - n_heads)))
            dqkv, dfl, db = _fox_unprep(dqx, dkvx, csum, rsum, fl, w["fox_b"][j], S=S, D=D, name=f"fox_unprep_{i}")
            grads["fox_b"][j] = db
            grads["fox_qkv"][j] = _mm(h_t, dqkv, "nn", name=f"fox_qkv_grad_{i}")
            grads["fox_f"][j] = _mm(h_t, dfl, "nn", name=f"fox_f_grad_{i}")
            dh_f = _mm(dfl, w["fox_f"], "nt", name=f"fox_f_bwd_{i}", b_layer=j)
            dh = _mm(dqkv, w["fox_qkv"], "nt", name=f"fox_qkv_bwd_{i}", b_layer=j, extras=(dh_f,),
                     epilogue=lambda acc, e: (acc + e,))
        else:
            qx, kvx, o, lse, o_t, lat, cqr_t = mix
            Rq = lat.shape[1] - 2 * LANES
            stacked("mla_out", j, L2, o_t, dy, name=f"mla_out_grad_{i}")
            do = _mm(dy, w["mla_out"], "nt", name=f"mla_out_bwd_{i}", b_layer=j, out_dtypes=(BF16,))
            dqx, dkvx = _attn_bwd(qx, kvx, o, lse, do, S=S, scale=(HEAD_DIM + ROPE_DIM) ** -0.5, ew=ROPE_DIM,
                                  name=f"mla_attn_bwd_{i}")
            dqpre = _unrope(dqx, cos_t, sin_s)
            stacked("mla_uq", j, L2, cqr_t[:Rq], dqpre, name=f"mla_uq_grad_{i}", out_split=n_split)
            stacked("mla_ukv", j, L2, cqr_t[Rq:], dkvx, name=f"mla_ukv_grad_{i}", tn=PAIR_KV, out_split=n_split)
            dcq = _mm(dqpre, w["mla_uq"], "nt", name=f"mla_uq_bwd_{i}", b_layer=j)
            dckr = _mm(dkvx, w["mla_ukv"], "nt", name=f"mla_ukv_bwd_{i}", b_layer=j, tk=PAIR_KV * 2)
            dlat, dgq, dgkv = _mla_mid_bwd(lat, dcq, dckr, w["mla_gq"][j], w["mla_gkv"][j], cos_t, sin_s,
                                           name=f"mla_mid_bwd_{i}")
            grads["mla_gq"][j] = dgq
            grads["mla_gkv"][j] = dgkv
            stacked("mla_down", j, L2, h_t, dlat, name=f"mla_down_grad_{i}")
            dh = _mm(dlat, w["mla_down"], "nt", name=f"mla_down_bwd_{i}", b_layer=j)
        dx, dsh_m, dsc_m, dgn = _norm_mod_bwd(x0, dh, dx1, w["norm_mix_g"][i], sc_m, S=S, name=f"norm_mix_bwd_{i}")
        grads["norm_mix_g"][i] = dgn
        dmod[i] = jnp.stack([dsh_m, dsc_m, dg_m, dsh_f, dsc_f, dg_f])
    return loss, dx, jnp.stack(dmod), grads


GATHERED = ("fox_in", "fox_out", "mla_down", "mla_uq", "mla_ukv", "mla_out", "mlp_w1", "mlp_w2")
ROW_SHARDED = ("fox_out", "mla_down", "mla_out", "mlp_w2")


def _shard_layouts(wts):
    dkv = wts["mla_w_dkv"]
    dkv = jnp.pad(dkv, ((0, 0), (0, 0), (0, 2 * LANES - dkv.shape[2])))
    return {
        "fox_in": _pad_lanes(wts["fox_w_in"].astype(BF16)),
        "fox_out": wts["fox_w_out"].astype(BF16),
        "mla_down": jnp.concatenate([wts["mla_w_dq"], dkv], axis=2).astype(BF16),
        "mla_uq": jax.vmap(_uq_to_pairs)(wts["mla_w_uq"].astype(BF16)),
        "mla_ukv": jax.vmap(_ukv_to_pairs)(wts["mla_w_ukv"].astype(BF16)),
        "mla_out": wts["mla_w_out"].astype(BF16),
        "mlp_w1": wts["mlp_w1"].astype(BF16),
        "mlp_w2": wts["mlp_w2"].astype(BF16),
    }


def _small_layouts(small):
    return {
        "fox_b": [jnp.pad(b, (0, LANES - b.shape[0]))[None, :] for b in small["fox_b_f"]],
        "mla_gq": [g[None, :] for g in small["mla_q_norm_g"]],
        "mla_gkv": [g[None, :] for g in small["mla_kv_norm_g"]],
        "norm_mix_g": [g[None, :] for g in small["norm_mix_g"]],
        "norm_mlp_g": [g[None, :] for g in small["norm_mlp_g"]],
        "final_norm_g": small["final_norm_g"][None, :],
    }


def _comm_groups(L, L2):
    rest = [("fox_in", 1, L2 - 1), ("fox_out", 1, L2 - 1), ("mla_down", 0, L2), ("mla_uq", 0, L2),
            ("mla_ukv", 0, L2), ("mla_out", 0, L2), ("mlp_w1", 1, L - 1), ("mlp_w2", 1, L - 1)]
    return {"mix0": [("fox_in", 0, 1), ("fox_out", 0, 1)], "mlp0": [("mlp_w1", 0, 1), ("mlp_w2", 0, 1)],
            "rest": [e for e in rest if e[2] > 0]}


def _layer_slots(groups):
    return {(n, s + l): (g, l, cnt) for g, entries in groups.items() for n, s, cnt in entries for l in range(cnt)}


def _pad_lanes(a):
    cols = a.shape[-1]
    return jnp.pad(a, [(0, 0)] * (a.ndim - 1) + [(0, -cols % LANES)])


def _weight_views(name, gathered, D, n_fox_heads):
    n, ns, rows, cols = gathered.shape
    if name == "fox_in":
        true_cols = (3 * D + n_fox_heads) // ns
        fox = jnp.concatenate([gathered[:, k, :, :true_cols] for k in range(ns)], axis=-1)
        return {"fox_qkv": fox[:, :, :3 * D], "fox_f": _pad_lanes(fox[:, :, 3 * D:])}
    if name in ROW_SHARDED:
        return {name: gathered.reshape(n, ns * rows, cols)}
    return {name: gathered}


def _grad_pieces(name, g, qkv_f, n_fox_heads, ns):
    if name == "fox_in":
        fox = jnp.stack([jnp.concatenate([a, b[:, :n_fox_heads]], axis=1) for a, b in qkv_f])
        cols = fox.shape[2] // ns
        return jnp.stack([_pad_lanes(fox[:, :, k * cols:(k + 1) * cols]) for k in range(ns)], axis=1)
    if name in ROW_SHARDED:
        return g.reshape(g.shape[0], ns, g.shape[1] // ns, g.shape[2])
    return g


def _small_grads(g, n_fox_heads):
    return {
        "norm_mix_g": jnp.concatenate(g["norm_mix_g"], axis=0),
        "norm_mlp_g": jnp.concatenate(g["norm_mlp_g"], axis=0),
        "final_norm_g": g["final_norm_g"][0],
        "fox_b_f": jnp.concatenate(g["fox_b"], axis=0)[:, :n_fox_heads],
        "mla_q_norm_g": jnp.concatenate(g["mla_gq"], axis=0),
        "mla_kv_norm_g": jnp.concatenate(g["mla_gkv"], axis=0),
    }


def _silu(c):
    return c * (1.0 / (1.0 + jnp.exp(-c)))


def _ada_fwd(c_all, ada_w, ada_b_cols):
    L, D, C = ada_w.shape
    Bg = c_all.shape[0]
    tc = _tile(C, 512)

    def body(c_ref, w_ref, b_ref, o_ref):
        ca = _silu(c_ref[...]).astype(BF16)
        o_ref[...] = jnp.dot(ca, w_ref[...].astype(BF16), preferred_element_type=F32) + b_ref[...]

    return pl.pallas_call(
        body, name="ada_fwd", grid=(L, C // tc),
        in_specs=[pl.BlockSpec((Bg, D), lambda l, j: (0, 0)), pl.BlockSpec((None, D, tc), lambda l, j: (l, 0, j)),
                  pl.BlockSpec((None, 1, tc), lambda l, j: (l, 0, j))],
        out_specs=pl.BlockSpec((None, Bg, tc), lambda l, j: (l, 0, j)),
        out_shape=jax.ShapeDtypeStruct((L, Bg, C), F32),
        compiler_params=_cparams(("parallel", "parallel")),
    )(c_all, ada_w, ada_b_cols)


def _ada_bwd(c_all, dmod_cols):
    L, Bg, C = dmod_cols.shape
    D = c_all.shape[1]
    tc = _tile(C, 512)

    def body(c_ref, d_ref, o_ref):
        ca = _silu(c_ref[...]).astype(BF16)
        o_ref[...] = _dot_tn(ca, d_ref[...].astype(BF16))

    return pl.pallas_call(
        body, name="ada_bwd", grid=(L, C // tc),
        in_specs=[pl.BlockSpec((Bg, D), lambda l, j: (0, 0)), pl.BlockSpec((None, Bg, tc), lambda l, j: (l, 0, j))],
        out_specs=pl.BlockSpec((None, D, tc), lambda l, j: (l, 0, j)),
        out_shape=jax.ShapeDtypeStruct((L, D, C), F32),
        compiler_params=_cparams(("parallel", "parallel")),
    )(c_all, dmod_cols)


def _adamw_update(w, gv, m, v):
    mn = ADAM_B1 * m + (1.0 - ADAM_B1) * gv
    vn = ADAM_B2 * v + (1.0 - ADAM_B2) * jnp.square(gv)
    m_hat = mn / (1.0 - ADAM_B1 ** ADAM_STEP)
    v_hat = vn / (1.0 - ADAM_B2 ** ADAM_STEP)
    return -ADAM_LR * (m_hat / (jnp.sqrt(v_hat) + ADAM_EPS) + ADAM_WD * w), mn, vn


def _adamw(w, g, m, v, *, name):
    shape = w.shape
    C = shape[-1]
    R = int(np.prod(shape[:-1])) if len(shape) > 1 else 1
    w2, g2, m2, v2 = (a.reshape(R, C) for a in (w, g, m, v))
    tr = _row_tile(R, C)

    def body(w_ref, g_ref, m_ref, v_ref, d_ref, nm_ref, nv_ref):
        d_ref[...], nm_ref[...], nv_ref[...] = _adamw_update(w_ref[...], g_ref[...], m_ref[...], v_ref[...])

    spec = pl.BlockSpec((tr, C), lambda i: (i, 0))
    out = pl.pallas_call(
        body, name=name, grid=(R // tr,), in_specs=[spec] * 4, out_specs=[spec] * 3,
        out_shape=[jax.ShapeDtypeStruct((R, C), F32)] * 3, compiler_params=_cparams(("parallel",)),
    )(w2, g2, m2, v2)
    return tuple(a.reshape(shape) for a in out)


def _adamw_halves(w, g_own, g_peer, m, v, c_idx, *, name):
    L, rows, C = w.shape
    R = rows // 2
    tr = _row_tile(R, C)

    def body(c_ref, w_ref, go_ref, gp_ref, m_ref, v_ref, g_ref, d_ref, nm_ref, nv_ref):
        gv = jnp.where(pl.program_id(1) == c_ref[0], go_ref[...], gp_ref[...])
        g_ref[...] = gv
        d_ref[...], nm_ref[...], nv_ref[...] = _adamw_update(w_ref[...], gv, m_ref[...], v_ref[...])

    full = pl.BlockSpec((None, None, tr, C), lambda l, hh, i, c_ref: (l, hh, i, 0))
    half = pl.BlockSpec((None, tr, C), lambda l, hh, i, c_ref: (l, i, 0))
    grid_spec = pltpu.PrefetchScalarGridSpec(
        num_scalar_prefetch=1, grid=(L, 2, R // tr), in_specs=[full, half, half, full, full], out_specs=[full] * 4)
    split = lambda a: a.reshape(L, 2, R, C)
    out = pl.pallas_call(
        body, name=name, grid_spec=grid_spec, out_shape=[jax.ShapeDtypeStruct((L, 2, R, C), F32)] * 4,
        compiler_params=_cparams(("parallel", "parallel", "parallel")),
    )(c_idx, split(w), g_own, g_peer, split(m), split(v))
    return tuple(a.reshape(w.shape) for a in out)


def _sum_gathered(dm8, sm8):
    n_dev, Bl, R, D = dm8.shape
    Rs = sm8.shape[1]

    def body(dm_ref, sm_ref, ob_ref, os_ref):
        acc_b = jnp.zeros((R, D), F32)
        acc_s = jnp.zeros((Rs, D), F32)
        for d in range(n_dev):
            for b in range(Bl):
                acc_b = acc_b + dm_ref[d, b]
            acc_s = acc_s + sm_ref[d]
        ob_ref[...] = acc_b
        os_ref[...] = acc_s

    return pl.pallas_call(
        body, name="sum_gathered",
        out_shape=[jax.ShapeDtypeStruct((R, D), F32), jax.ShapeDtypeStruct((Rs, D), F32)],
        compiler_params=_cparams(None),
    )(dm8, sm8)


N_DEV = 8
N_CHIP = 4
ANY = pl.BlockSpec(memory_space=pl.ANY)
HBM = pl.BlockSpec(memory_space=pltpu.HBM)
SEM = pl.BlockSpec(memory_space=pltpu.SEMAPHORE)
DATAFLOW = pltpu.SideEffectType.DATAFLOW_SIDE_EFFECTING


def _mesh_pos():
    return lax.axis_index("x"), lax.axis_index("y"), lax.axis_index("c")


def _all_gather8(block, *, name, in_vmem):
    R, W = block.shape

    def body(x_ref, out_ref, send_sems, recv_sems, local_sem):
        x, y, c = _mesh_pos()
        me, sibling = (x, y, c), (x, y, 1 - c)
        chips = [(1 - x, y), (x, 1 - y), (1 - x, 1 - y)]

        def slot(px, py, pc):
            return out_ref.at[4 * px + 2 * py + pc]

        def copy(k, blk, to, src=None):
            return pltpu.make_async_remote_copy(
                src_ref=slot(*blk) if src is None else src, dst_ref=slot(*blk),
                send_sem=send_sems.at[k], recv_sem=recv_sems.at[k], device_id=to, device_id_type=MESH_ID)

        mine = pltpu.make_async_copy(x_ref, slot(*me), local_sem)
        mine.start()
        first = [copy(0, me, sibling, src=x_ref)]
        first += [copy(1 + j, me, (*chip, c), src=x_ref) for j, chip in enumerate(chips)]
        for cp in first:
            cp.start()
        passed = [copy(4 + j, (*chip, c), sibling) for j, chip in enumerate(chips)]
        for j, chip in enumerate(chips):
            copy(1 + j, (*chip, c), me).wait_recv()
            passed[j].start()
        copy(0, sibling, me).wait_recv()
        for j, chip in enumerate(chips):
            copy(4 + j, (*chip, 1 - c), me).wait_recv()
        for cp in first + passed:
            cp.wait_send()
        mine.wait()

    space = pl.BlockSpec(memory_space=pltpu.VMEM) if in_vmem else ANY
    return pl.pallas_call(
        body, name=name, out_shape=jax.ShapeDtypeStruct((N_DEV, R, W), block.dtype),
        in_specs=[space], out_specs=space,
        scratch_shapes=[pltpu.SemaphoreType.DMA((7,)), pltpu.SemaphoreType.DMA((7,)), pltpu.SemaphoreType.DMA],
        compiler_params=pltpu.CompilerParams(vmem_limit_bytes=VMEM_LIMIT_V7X),
    )(block)


def _comm_call(body, arrays, out_shapes, n_sems, *, name):
    return pl.pallas_call(
        body, name=name, out_shape=out_shapes, in_specs=[ANY] * len(arrays), out_specs=[ANY] * len(out_shapes),
        scratch_shapes=[pltpu.SemaphoreType.DMA((n_sems,)), pltpu.SemaphoreType.DMA((n_sems,)),
                        pltpu.SemaphoreType.DMA((len(arrays),))],
    )(*arrays)


def _gather_weights(shards, *, name):
    n = len(shards)

    def body(*refs):
        xs, outs = refs[:n], refs[n:2 * n]
        send_sems, recv_sems, local_sems = refs[2 * n:]
        x, y, c = _mesh_pos()
        me, sibling = (x, y, c), (x, y, 1 - c)
        chips = [(1 - x, y), (x, 1 - y), (1 - x, 1 - y)]
        waits = []
        for i in range(n):
            nl = shards[i].shape[0]
            own = xs[i].at[pl.ds(0, nl), c]

            def slot(px, py, pc, i=i, nl=nl):
                return outs[i].at[pl.ds(0, nl), 2 * px + py, pc]

            def copy(k, blk, to, src=None, i=i, slot=slot):
                return pltpu.make_async_remote_copy(
                    src_ref=slot(*blk) if src is None else src, dst_ref=slot(*blk),
                    send_sem=send_sems.at[7 * i + k], recv_sem=recv_sems.at[7 * i + k], device_id=to,
                    device_id_type=MESH_ID)

            mine = pltpu.make_async_copy(own, slot(*me), local_sems.at[i])
            mine.start()
            first = [copy(0, me, sibling, src=own)]
            first += [copy(1 + j, me, (*chip, c), src=own) for j, chip in enumerate(chips)]
            for cp in first:
                cp.start()
            waits.append((copy, mine, first))
        for copy, mine, first in waits:
            passed = [copy(4 + j, (*chip, c), sibling) for j, chip in enumerate(chips)]
            for j, chip in enumerate(chips):
                copy(1 + j, (*chip, c), me).wait_recv()
                passed[j].start()
            copy(0, sibling, me).wait_recv()
            for j, chip in enumerate(chips):
                copy(4 + j, (*chip, 1 - c), me).wait_recv()
            for cp in first + passed:
                cp.wait_send()
            mine.wait()

    out_shapes = [jax.ShapeDtypeStruct((s.shape[0], N_CHIP) + s.shape[1:], s.dtype) for s in shards]
    return _comm_call(body, shards, out_shapes, 7 * n, name=name)


def _place_own(shard, chip_idx, c_idx, *, name):
    n, _, rows, cols = shard.shape
    tr = _row_tile(rows, cols)

    def body(k_ref, c_ref, x_ref, o_ref):
        o_ref[...] = x_ref[...]

    grid_spec = pltpu.PrefetchScalarGridSpec(
        num_scalar_prefetch=2, grid=(n, rows // tr),
        in_specs=[pl.BlockSpec((None, None, tr, cols), lambda l, i, k_ref, c_ref: (l, c_ref[0], i, 0))],
        out_specs=pl.BlockSpec((None, None, None, tr, cols), lambda l, i, k_ref, c_ref: (l, k_ref[0], c_ref[0], i, 0)))
    return pl.pallas_call(
        body, name=name, grid_spec=grid_spec,
        out_shape=jax.ShapeDtypeStruct((n, N_CHIP, 2, rows, cols), shard.dtype),
        compiler_params=_cparams(("parallel", "parallel")),
    )(chip_idx, c_idx, shard)


def _gather_copies(x_refs, land_refs, send_sems, recv_sems):
    x, y, c = _mesh_pos()
    k_me = 2 * x + y
    targets = [(x, y, 1 - c), (1 - x, y, c), (x, 1 - y, c), (1 - x, 1 - y, c)]
    copies = []
    for i, (x_ref, land_ref) in enumerate(zip(x_refs, land_refs)):
        nl = x_ref.shape[0]
        for j, to in enumerate(targets):
            copies.append(pltpu.make_async_remote_copy(
                src_ref=x_ref.at[pl.ds(0, nl), c], dst_ref=land_ref.at[pl.ds(0, nl), k_me, c],
                send_sem=send_sems.at[4 * i + j], recv_sem=recv_sems.at[4 * i + j], device_id=to,
                device_id_type=MESH_ID))
    return copies


def _split_start(copies_fn, srcs, lands, after, *, name, sems_per_array):
    n = len(srcs)

    def body(*refs):
        send_sems, recv_sems = refs[2 * n + 1], refs[2 * n + 2]
        for cp in copies_fn(refs[:n], refs[n:2 * n], send_sems, recv_sems):
            cp.start()
        refs[-1][...] = jnp.zeros_like(refs[-1])

    operands = [pltpu.with_memory_space_constraint(a, pltpu.HBM) for a in list(srcs) + list(lands)]
    n_sems = sems_per_array * n
    out_shape = ([pltpu.SemaphoreType.DMA((n_sems,)), pltpu.SemaphoreType.DMA((n_sems,))]
                 + [pltpu.HBM(a.shape, a.dtype) for a in operands] + [jax.ShapeDtypeStruct((8, LANES), F32)])
    res = pl.pallas_call(
        body, name=name, out_shape=out_shape, in_specs=[HBM] * (2 * n) + [ANY],
        out_specs=[SEM, SEM] + [HBM] * (2 * n) + [pl.BlockSpec(memory_space=pltpu.VMEM)],
        input_output_aliases={i: 2 + i for i in range(2 * n)},
        compiler_params=pltpu.CompilerParams(has_side_effects=DATAFLOW),
    )(*operands, after)
    return res[0], res[1], list(res[2:2 + n]), list(res[2 + n:2 + 2 * n]), res[-1]


def _split_wait(copies_fn, send_sems, recv_sems, srcs, lands, after, *, name):
    n = len(srcs)

    def body(*refs):
        for cp in copies_fn(refs[:n], refs[n:2 * n], refs[2 * n], refs[2 * n + 1]):
            cp.wait_send()
            cp.wait_recv()

    res = pl.pallas_call(
        body, name=name, out_shape=[pltpu.HBM(a.shape, a.dtype) for a in list(srcs) + list(lands)],
        in_specs=[HBM] * (2 * n) + [SEM, SEM, ANY], out_specs=[HBM] * (2 * n),
        input_output_aliases={i: i for i in range(2 * n)},
        compiler_params=pltpu.CompilerParams(has_side_effects=DATAFLOW),
    )(*srcs, *lands, send_sems, recv_sems, after)
    return list(res[:n]), list(res[n:])


def _gather_forward(lands, *, name):
    n = len(lands)

    def body(*refs):
        xs = refs[:n]
        send_sems, recv_sems, _ = refs[2 * n:]
        x, y, c = _mesh_pos()
        chips = [(1 - x, y), (x, 1 - y), (1 - x, 1 - y)]
        copies = []
        for i in range(n):
            nl = lands[i].shape[0]
            for j, (cx, cy) in enumerate(chips):
                here = xs[i].at[pl.ds(0, nl), 2 * cx + cy, c]
                cp = pltpu.make_async_remote_copy(
                    src_ref=here, dst_ref=here, send_sem=send_sems.at[3 * i + j], recv_sem=recv_sems.at[3 * i + j],
                    device_id=(x, y, 1 - c), device_id_type=MESH_ID)
                cp.start()
                copies.append(cp)
        for cp in copies:
            cp.wait()

    return pl.pallas_call(
        body, name=name, out_shape=[jax.ShapeDtypeStruct(a.shape, a.dtype) for a in lands],
        in_specs=[ANY] * n, out_specs=[ANY] * n, input_output_aliases={i: i for i in range(n)},
        scratch_shapes=[pltpu.SemaphoreType.DMA((3 * n,)), pltpu.SemaphoreType.DMA((3 * n,)),
                        pltpu.SemaphoreType.DMA((1,))],
    )(*lands)


def _pair_copies(g_refs, land_refs, send_sems, recv_sems):
    x, y, c = _mesh_pos()
    copies = []
    for i, (g_ref, land_ref) in enumerate(zip(g_refs, land_refs)):
        nl, ns = g_ref.shape[:2]
        copies.append(pltpu.make_async_remote_copy(
            src_ref=g_ref.at[pl.ds(0, nl), pl.ds(0, ns), 1 - c], dst_ref=land_ref, send_sem=send_sems.at[i],
            recv_sem=recv_sems.at[i], device_id=(x, y, 1 - c), device_id_type=MESH_ID))
    return copies


def _pair_exchange(gs, *, name):
    n = len(gs)

    def body(*refs):
        send_sems, recv_sems, _ = refs[2 * n:]
        copies = _pair_copies(refs[:n], refs[n:2 * n], send_sems, recv_sems)
        for cp in copies:
            cp.start()
        for cp in copies:
            cp.wait()

    out_shapes = [jax.ShapeDtypeStruct(g.shape[:2] + g.shape[3:], g.dtype) for g in gs]
    return _comm_call(body, gs, out_shapes, n, name=name)


def _chip_copies(p_refs, land_refs, send_sems, recv_sems):
    x, y, c = _mesh_pos()
    k_me = 2 * x + y
    chips = [(1 - x, y), (x, 1 - y), (1 - x, 1 - y)]
    copies = []
    for i, (p_ref, land_ref) in enumerate(zip(p_refs, land_refs)):
        nl = p_ref.shape[0]
        for j, (cx, cy) in enumerate(chips):
            copies.append(pltpu.make_async_remote_copy(
                src_ref=p_ref.at[pl.ds(0, nl), 2 * cx + cy], dst_ref=land_ref.at[k_me],
                send_sem=send_sems.at[3 * i + j], recv_sem=recv_sems.at[3 * i + j],
                device_id=(cx, cy, c), device_id_type=MESH_ID))
    return copies


def _chip_landing(ps):
    return [lax.empty((p.shape[1], p.shape[0]) + p.shape[2:], p.dtype) for p in ps]


def _chip_exchange(ps, *, name):
    n = len(ps)

    def body(*refs):
        send_sems, recv_sems, _ = refs[2 * n:]
        copies = _chip_copies(refs[:n], refs[n:2 * n], send_sems, recv_sems)
        for cp in copies:
            cp.start()
        for cp in copies:
            cp.wait()

    out_shapes = [jax.ShapeDtypeStruct((p.shape[1], p.shape[0]) + p.shape[2:], p.dtype) for p in ps]
    return _comm_call(body, ps, out_shapes, 3 * n, name=name)


def _pair_swap(ss, *, name):
    n = len(ss)

    def body(*refs):
        xs, outs = refs[:n], refs[n:2 * n]
        send_sems, recv_sems, _ = refs[2 * n:]
        x, y, c = _mesh_pos()
        copies = []
        for i in range(n):
            cp = pltpu.make_async_remote_copy(src_ref=xs[i], dst_ref=outs[i], send_sem=send_sems.at[i],
                                              recv_sem=recv_sems.at[i], device_id=(x, y, 1 - c),
                                              device_id_type=MESH_ID)
            cp.start()
            copies.append(cp)
        for cp in copies:
            cp.wait()

    out_shapes = [jax.ShapeDtypeStruct(s.shape, s.dtype) for s in ss]
    return _comm_call(body, ss, out_shapes, n, name=name)


def _row_tile(rows, cols):
    tr = rows
    while tr * cols > 256 * 1024 and tr % 16 == 0:
        tr //= 2
    return tr


def _pair_add(g, recv, c_idx, *, name):
    n, ns, _, rows, W = g.shape
    tr = _row_tile(rows, W)

    def body(c_ref, g_ref, r_ref, o_ref):
        o_ref[...] = (g_ref[...] + r_ref[...]).astype(BF16)

    piece = pl.BlockSpec((None, tr, W), lambda p, i, c_ref: (p, i, 0))
    grid_spec = pltpu.PrefetchScalarGridSpec(
        num_scalar_prefetch=1, grid=(n * ns, rows // tr),
        in_specs=[pl.BlockSpec((None, None, tr, W), lambda p, i, c_ref: (p, c_ref[0], i, 0)), piece],
        out_specs=piece)
    out = pl.pallas_call(
        body, name=name, grid_spec=grid_spec, out_shape=jax.ShapeDtypeStruct((n * ns, rows, W), BF16),
        compiler_params=_cparams(("parallel", "parallel")),
    )(c_idx, g.reshape(n * ns, 2, rows, W), recv.reshape(n * ns, rows, W))
    return out.reshape(n, ns, rows, W)


def _sum_pieces(land, own, chip_idx, *, name):
    n, nl, A, W = land.shape
    tr = _row_tile(A, W)

    def body(k_ref, l_ref, o_ref, out_ref):
        acc = jnp.zeros(out_ref.shape, F32)
        for k in range(n):
            acc = acc + jnp.where(k == k_ref[0], o_ref[...], l_ref[k]).astype(F32)
        out_ref[...] = acc

    grid_spec = pltpu.PrefetchScalarGridSpec(
        num_scalar_prefetch=1, grid=(nl, A // tr),
        in_specs=[pl.BlockSpec((n, None, tr, W), lambda l, i, k_ref: (0, l, i, 0)),
                  pl.BlockSpec((None, None, tr, W), lambda l, i, k_ref: (l, k_ref[0], i, 0))],
        out_specs=pl.BlockSpec((None, tr, W), lambda l, i, k_ref: (l, i, 0)))
    return pl.pallas_call(
        body, name=name, grid_spec=grid_spec, out_shape=jax.ShapeDtypeStruct((nl, A, W), F32),
        compiler_params=_cparams(("parallel", "parallel")),
    )(chip_idx, land, own)


SMALL = ("norm_mix_g", "norm_mlp_g", "final_norm_g", "fox_b_f", "mla_q_norm_g", "mla_kv_norm_g")
WEIGHT_ORDER = ("ada_w", "ada_b", "norm_mix_g", "norm_mlp_g", "fox_w_in", "fox_b_f", "fox_w_out", "mla_w_dq",
                "mla_q_norm_g", "mla_w_uq", "mla_w_dkv", "mla_kv_norm_g", "mla_w_ukv", "mla_w_out", "mlp_w1",
                "mlp_w2", "final_norm_g")


def _small_rows(vals, D):
    rows = [vals["norm_mix_g"], vals["norm_mlp_g"], vals["final_norm_g"][None, :]]
    for n in ("fox_b_f", "mla_q_norm_g", "mla_kv_norm_g"):
        flat = vals[n].reshape(-1)
        assert flat.shape[0] <= D
        rows.append(jnp.pad(flat, (0, D - flat.shape[0]))[None, :])
    return jnp.concatenate(rows, axis=0)


def _small_unrows(rows, shapes):
    L = shapes["norm_mix_g"][0]
    out = {"norm_mix_g": rows[0:L], "norm_mlp_g": rows[L:2 * L], "final_norm_g": rows[2 * L]}
    for k, n in enumerate(("fox_b_f", "mla_q_norm_g", "mla_kv_norm_g")):
        size = int(np.prod(shapes[n]))
        out[n] = rows[2 * L + 1 + k, :size].reshape(shapes[n])
    return out


def kernel(x, c, positions, ada_w, ada_b, norm_mix_g, norm_mlp_g, fox_w_in, fox_b_f, fox_w_out, mla_w_dq, mla_q_norm_g, mla_w_uq, mla_w_dkv, mla_kv_norm_g, mla_w_ukv, mla_w_out, mlp_w1, mlp_w2, final_norm_g, loss_target, m_ada_w, m_ada_b, m_norm_mix_g, m_norm_mlp_g, m_fox_w_in, m_fox_b_f, m_fox_w_out, m_mla_w_dq, m_mla_q_norm_g, m_mla_w_uq, m_mla_w_dkv, m_mla_kv_norm_g, m_mla_w_ukv, m_mla_w_out, m_mlp_w1, m_mlp_w2, m_final_norm_g, v_ada_w, v_ada_b, v_norm_mix_g, v_norm_mlp_g, v_fox_w_in, v_fox_b_f, v_fox_w_out, v_mla_w_dq, v_mla_q_norm_g, v_mla_w_uq, v_mla_w_dkv, v_mla_kv_norm_g, v_mla_w_ukv, v_mla_w_out, v_mlp_w1, v_mlp_w2, v_final_norm_g):
    args = dict(locals())
    wts = {n: args[n] for n in WEIGHT_ORDER}
    mom = {n: args["m_" + n] for n in WEIGHT_ORDER}
    var = {n: args["v_" + n] for n in WEIGHT_ORDER}
    Bl, S, D = x.shape
    T = Bl * S
    L = ada_w.shape[0]
    C = ada_w.shape[2]
    mx, my, mc = _mesh_pos()
    chip = 2 * mx + my
    dev = 4 * mx + 2 * my + mc
    c_idx = jnp.reshape(mc, (1,)).astype(jnp.int32)
    chip_idx = jnp.reshape(chip, (1,)).astype(jnp.int32)
    small = {n: wts[n] for n in SMALL}
    L2, q_cols = mla_q_norm_g.shape
    n_fox_heads = fox_b_f.shape[1]

    shards = _shard_layouts(wts)
    groups = _comm_groups(L, L2)
    slots = _layer_slots(groups)

    def row_halves(a):
        return a.reshape(a.shape[:-2] + (2, a.shape[-2] // 2, a.shape[-1]))

    def whole_rows(a):
        return a.reshape(a.shape[:2] + (a.shape[2] * a.shape[3], a.shape[4]))

    part = {g: [row_halves(shards[n][s:s + cnt]) for n, s, cnt in entries] for g, entries in groups.items()}
    mix0 = _gather_weights(part["mix0"], name="gather_mix0")
    gather_sems, after = {}, mix0[0]
    for group in ("mlp0", "rest"):
        placed = [_place_own(a, chip_idx, c_idx, name=f"gather_place_{group}_{n}")
                  for a, (n, _, _) in zip(part[group], groups[group])]
        gather_sems[group] = _split_start(_gather_copies, part[group], placed, after, name=f"gather_{group}_start",
                                          sems_per_array=4)
        after = gather_sems[group][4]

    def layer_weights(w, group, arrays):
        for (n, s, cnt), a in zip(groups[group], arrays):
            for key, view in _weight_views(n, whole_rows(a), D, n_fox_heads).items():
                for l in range(cnt):
                    w[key][s + l] = (view, l)

    w = {key: [None] * L2 for key in ("fox_qkv", "fox_f", "fox_out", "mla_down", "mla_uq", "mla_ukv", "mla_out")}
    w.update({key: [None] * L for key in ("mlp_w1", "mlp_w2")})
    layer_weights(w, "mix0", mix0)

    def gathered_now(group):
        def hook(x_now, w):
            _, landed = _split_wait(_gather_copies, *gather_sems[group][:4], x_now, name=f"gather_{group}_wait")
            layer_weights(w, group, _gather_forward(landed, name=f"gather_{group}_forward"))
            return w
        return hook

    c_pad = jnp.concatenate([c, jnp.pad(mla_q_norm_g, ((0, 8 - Bl - L2), (0, D - q_cols)))], axis=0)
    c8 = _all_gather8(c_pad, name="gather_c", in_vmem=True)
    c_all = c8[:, :Bl].reshape(N_DEV * Bl, D)
    qg4 = c8.reshape(N_CHIP, 2, 8, D)[:, 0, Bl:Bl + L2, :q_cols]
    small["mla_q_norm_g"] = jnp.transpose(qg4, (1, 0, 2)).reshape(L2, N_CHIP * q_cols)
    ada_b_cols = lax.dynamic_slice_in_dim(ada_b, chip * C, C, axis=1)[:, None, :]
    mod_cols = _ada_fwd(c_all, ada_w, ada_b_cols)
    mod8 = _all_gather8(mod_cols.reshape(L * N_DEV * Bl, C), name="gather_mod", in_vmem=True)
    mod4 = mod8.reshape(N_CHIP, 2, L, N_DEV * Bl, C)[:, 0]
    mod_me = lax.dynamic_slice_in_dim(mod4, dev * Bl, Bl, axis=2)
    mod = jnp.transpose(mod_me, (1, 2, 0, 3)).reshape(L, Bl, 6, D)
    mod = jnp.transpose(mod, (0, 2, 1, 3))[:, :, :, None, :]

    w.update(_small_layouts(small))
    mod = mod + after[0, 0]
    pending = {}

    def grad_pieces(group, g_now):
        out = []
        for n, s, cnt in groups[group]:
            qkv_f = [(g_now["fox_qkv"][j], g_now["fox_f"][j]) for j in range(s, s + cnt)] if n == "fox_in" else None
            stacked_g = None if n == "fox_in" else g_now[n][group]
            out.append(row_halves(_grad_pieces(n, stacked_g, qkv_f, n_fox_heads, N_CHIP)))
        return out

    def pair_added(group, big, sibling):
        return [_pair_add(a, r, c_idx, name=f"grad_pair_add_{group}_{n}")
                for (n, _, _), a, r in zip(groups[group], big, sibling)]

    def exchange_start(group, ps, after=None):
        pending[group] = _split_start(_chip_copies, ps, _chip_landing(ps), chip_idx if after is None else after,
                                      name=f"grad_exchange_{group}_start", sems_per_array=3)
        return pending[group][4]

    def bwd_layer0(g_now):
        big = grad_pieces("rest", g_now)
        landing = [lax.empty(a.shape[:2] + a.shape[3:], a.dtype) for a in big]
        pending["rest_pair"] = _split_start(_pair_copies, big, landing, chip_idx, name="grad_pair_rest_start",
                                            sems_per_array=1)
        return pending["rest_pair"][4]

    def bwd_mix0(g_now):
        send_sems, recv_sems, big, landed, _ = pending["rest_pair"]
        big, landed = _split_wait(_pair_copies, send_sems, recv_sems, big, landed, g_now["mlp_w1"]["mlp0"],
                                  name="grad_pair_rest_wait")
        started = exchange_start("rest", pair_added("rest", big, landed))
        big = grad_pieces("mlp0", g_now)
        return exchange_start("mlp0", pair_added("mlp0", big, _pair_exchange(big, name="grad_pair_exchange_mlp0")),
                              after=started)

    half = ROPE_DIM // 2
    inv_freq = ROPE_THETA ** (-jnp.arange(0, ROPE_DIM, 2, dtype=F32) / ROPE_DIM)
    lane = np.arange(LANES)
    inv_freq_row = jnp.tile(inv_freq, LANES // half)[None, :]
    sign_row = jnp.asarray(np.where(lane < 2 * ROPE_DIM, np.where(lane % ROPE_DIM < half, -1.0, 1.0), 0.0), F32)[None, :]
    pos_f = positions.astype(F32).reshape(T, 1)
    loss_row, grad_x, dmod, g = _local_step(x.reshape(T, D), loss_target.reshape(T, D), pos_f, inv_freq_row, sign_row,
                                            mod, w, slots, S=S,
                                            hooks={"fwd_mlp0": gathered_now("mlp0"), "fwd_layer1": gathered_now("rest"),
                                                   "bwd_layer0": bwd_layer0, "bwd_mix0": bwd_mix0})
    g_small = _small_grads(g, n_fox_heads)
    big = grad_pieces("mix0", g)
    exchange_start("mix0", pair_added("mix0", big, _pair_exchange(big, name="grad_pair_exchange_mix0")))

    Rs = -(-(2 * L + 5) // 8) * 8
    srows = jnp.concatenate([_small_rows(g_small, D), jnp.pad(loss_row, ((0, 0), (0, D - LANES)))], axis=0)
    srows = jnp.pad(srows, ((0, Rs - srows.shape[0]), (0, 0)))
    drows = jnp.transpose(dmod[:, :, :, 0, :], (2, 0, 1, 3)).reshape(Bl * L * 6, D)
    both8 = _all_gather8(jnp.concatenate([drows, srows], axis=0), name="gather_small", in_vmem=True)
    dm8 = both8[:, :Bl * L * 6].reshape(N_DEV, Bl, L * 6, D)
    sm8 = both8[:, Bl * L * 6:]
    adb_rows, small_sum = _sum_gathered(dm8, sm8)
    grad_ada_b = adb_rows.reshape(L, 6 * D)
    loss = small_sum[2 * L + 4, 0]
    small_shapes = {n: (wts[n].shape if n != "mla_q_norm_g" else (wts[n].shape[0], N_CHIP * q_cols)) for n in SMALL}
    gs = _small_unrows(small_sum, small_shapes)
    gs["mla_q_norm_g"] = lax.dynamic_slice_in_dim(gs["mla_q_norm_g"], chip * q_cols, q_cols, axis=1)

    dmod16 = jnp.transpose(dm8.reshape(N_DEV, Bl, L, 6 * D), (2, 0, 1, 3)).reshape(L, N_DEV * Bl, 6 * D)
    dmod_cols = lax.dynamic_slice_in_dim(dmod16, chip * C, C, axis=2)
    grad_ada_w = _ada_bwd(c_all, dmod_cols)

    grads = dict(gs)
    grads["ada_w"] = grad_ada_w
    grads["ada_b"] = grad_ada_b
    delta, new_m, new_v = {}, {}, {}
    for n in ("ada_w", "ada_b"):
        delta[n], new_m[n], new_v[n] = _adamw(wts[n], grads[n], mom[n], var[n], name=f"adamw_{n}")
    shard_small_shapes = {n: wts[n].shape for n in SMALL}
    packs = [jnp.pad(_small_rows({n: src[n] for n in SMALL}, D), ((0, Rs - 2 * L - 4), (0, 0)))
             for src in (wts, grads, mom, var)]
    for dst, rows in zip((delta, new_m, new_v), _adamw(*packs, name="adamw_small")):
        dst.update(_small_unrows(rows, shard_small_shapes))

    halves = {}
    for group, after in (("rest", grad_x), ("mlp0", grad_x), ("mix0", delta["ada_w"])):
        send_sems, recv_sems, ps, lands, _ = pending[group]
        ps, lands = _split_wait(_chip_copies, send_sems, recv_sems, ps, lands, after, name=f"grad_exchange_{group}_wait")
        sums = [_sum_pieces(ld, p, chip_idx, name=f"grad_sum_{group}_{n}")
                for (n, _, _), ld, p in zip(groups[group], lands, ps)]
        swapped = _pair_swap(sums, name=f"grad_pair_swap_{group}")
        for (n, _, _), a, b in zip(groups[group], sums, swapped):
            halves[(n, group)] = (a, b)

    def all_layers(n, which):
        return jnp.concatenate([halves[(n, grp)][which] for grp in groups if (n, grp) in halves], axis=0)

    own = {n: all_layers(n, 0) for n in GATHERED}
    peer = {n: all_layers(n, 1) for n in GATHERED}
    for nat, n in (("fox_w_in", "fox_in"), ("fox_w_out", "fox_out"), ("mla_w_out", "mla_out"), ("mlp_w1", "mlp_w1"),
                   ("mlp_w2", "mlp_w2")):
        cols = wts[nat].shape[-1]
        res = _adamw_halves(_pad_lanes(wts[nat]), own[n], peer[n], _pad_lanes(mom[nat]), _pad_lanes(var[nat]), c_idx,
                            name=f"adamw_{nat}")
        grads[nat], delta[nat], new_m[nat], new_v[nat] = (a[..., :cols] for a in res)
    joined = {n: jnp.concatenate([jnp.where(mc == 0, own[n], peer[n]), jnp.where(mc == 0, peer[n], own[n])], axis=1)
              for n in ("mla_down", "mla_uq", "mla_ukv")}
    rq = mla_w_dq.shape[-1]
    grads["mla_w_dq"] = joined["mla_down"][:, :, :rq]
    grads["mla_w_dkv"] = joined["mla_down"][:, :, rq:rq + KV_RANK + ROPE_DIM]
    grads["mla_w_uq"] = jax.vmap(_uq_from_pairs)(joined["mla_uq"])
    grads["mla_w_ukv"] = jax.vmap(_ukv_from_pairs)(joined["mla_ukv"])
    for n in ("mla_w_dq", "mla_w_dkv", "mla_w_uq", "mla_w_ukv"):
        delta[n], new_m[n], new_v[n] = _adamw(wts[n], grads[n], mom[n], var[n], name=f"adamw_{n}")

    return (loss, grad_x.reshape(Bl, S, D), *[grads[n] for n in WEIGHT_ORDER], *[delta[n] for n in WEIGHT_ORDER],
            *[new_m[n] for n in WEIGHT_ORDER], *[new_v[n] for n in WEIGHT_ORDER])
```

```python
import functools

import numpy as np
import jax
import jax.numpy as jnp
from jax import lax
from jax.experimental import pallas as pl
from jax.experimental.pallas import tpu as pltpu

F32 = jnp.float32
BF16 = jnp.bfloat16
MESH_ID = pl.DeviceIdType.MESH

NORM_EPS = 1e-6
ROPE_THETA = 10000.0
HEAD_DIM = 64
ROPE_DIM = 32
KV_RANK = 128
FOX_EXTRA = 6
PAIR_Q = 256
PAIR_KV = 384
LANES = 128
ADAM_LR = 0.001
ADAM_B1 = 0.9
ADAM_B2 = 0.999
ADAM_EPS = 1e-08
ADAM_WD = 0.01
ADAM_STEP = 10
VMEM_LIMIT_V7X = 48 * 1024 * 1024
MM_VMEM_BUDGET = 36 * 1024 * 1024
NEG_BIG = -1e30
ATTN_UNROLL = 4
ATTN_BLOCK = 256
ATTN_Q_ROWS = 512

BIG_WEIGHTS = (("fox_w_in", 2), ("fox_w_out", 1), ("mla_w_dq", 1), ("mla_w_uq", 2), ("mla_w_dkv", 1),
               ("mla_w_ukv", 2), ("mla_w_out", 1), ("mlp_w1", 2), ("mlp_w2", 1))


def _cparams(sem=None):
    return pltpu.CompilerParams(dimension_semantics=sem, vmem_limit_bytes=VMEM_LIMIT_V7X)


def _tile(n, want):
    if n <= want:
        return n
    for t in range(want - want % LANES, 0, -LANES):
        if n % t == 0:
            return t
    raise ValueError((n, want))


def _mm(a, b, mode, *, name, out_dtypes=(F32,), epilogue=None, extras=(), rowvecs=(), tables=(),
        seq=None, a_off=0, a_sz=None, b_layer=None, out_stack=None, out_split=0, out_t=(), tm=1024, tn=1024,
        tk=2048):
    if isinstance(b, (list, tuple)):
        b, b_layer = b[b_layer]
    b_rows, b_cols = b.shape[-2], b.shape[-1]
    n_split = b.shape[1] if b.ndim == 4 else 1
    assert mode in ("nn", "nt")
    if mode == "nn":
        M, K, N = a.shape[0], b_rows, b_cols * n_split
    else:
        M, K, N = a.shape[0], b_cols * n_split, b_rows
    assert a_sz is None or a_sz == K
    tm = _tile(seq if rowvecs else M, tm)
    n_piece = N // max(out_split, n_split if mode == "nn" else 1, 1)
    tn = _tile(n_piece, tn)
    tk = _tile(K // (n_split if mode == "nt" else 1), tk)
    ne, nr, nt_ = len(extras), len(rowvecs), len(tables)
    no = len(out_dtypes)

    def vmem_estimate():
        blocks = tm * tk * a.dtype.itemsize + tk * tn * b.dtype.itemsize
        blocks += tm * tn * (sum(e.dtype.itemsize for e in extras) + sum(jnp.dtype(d).itemsize for d in out_dtypes))
        return 2 * blocks + 2 * tm * tn * 4

    while vmem_estimate() > MM_VMEM_BUDGET and max(tm, tn) > 256:
        if tn >= tm:
            tn //= 2
        else:
            tm //= 2
    nk = K // tk

    assert a_off % tk == 0
    a_spec = pl.BlockSpec((tm, tk), lambda i, j, k: (i, k + a_off // tk))
    dims = (((1,), (0,)), ((), ())) if mode == "nn" else (((1,), (1,)), ((), ()))
    lead = () if b.ndim == 2 else (b_layer,)
    sq = (None,) * (b.ndim - 2)
    if mode == "nt":
        kb = b_cols // tk
        if b.ndim == 4:
            b_spec = pl.BlockSpec(sq + (tn, tk), lambda i, j, k: lead + (k // kb, j, k % kb))
        else:
            b_spec = pl.BlockSpec(sq + (tn, tk), lambda i, j, k: lead + (j, k))
    else:
        nb = b_cols // tn
        if b.ndim == 4:
            b_spec = pl.BlockSpec(sq + (tk, tn), lambda i, j, k: lead + (j // nb, k, j % nb))
        else:
            b_spec = pl.BlockSpec(sq + (tk, tn), lambda i, j, k: lead + (k, j))
    in_specs = [a_spec, b_spec]
    in_specs += [pl.BlockSpec((tm, tn), lambda i, j, k: (i, j)) for _ in extras]
    if rowvecs:
        assert seq % tm == 0
        per = seq // tm
        in_specs += [pl.BlockSpec((None, 1, tn), lambda i, j, k: (i // per, 0, j)) for _ in rowvecs]
    in_specs += [pl.BlockSpec((tm, LANES), lambda i, j, k: (i, 0)) for _ in tables]
    operands = [a, b, *extras, *rowvecs, *tables]
    aliases = {}
    transposed = tuple(out_t) + (False,) * (no - len(out_t))
    if out_stack is None:
        out_specs = [pl.BlockSpec((tn, tm), lambda i, j, k: (j, i)) if t else pl.BlockSpec((tm, tn), lambda i, j, k: (i, j))
                     for t in transposed]
        out_shape = [jax.ShapeDtypeStruct((N, M) if t else (M, N), d) for d, t in zip(out_dtypes, transposed)]
    else:
        prev, layer, n_layers = out_stack
        assert no == 1
        if out_split:
            ob = n_piece // tn
            out_specs = [pl.BlockSpec((None, None, tm, tn), lambda i, j, k: (layer, j // ob, i, j % ob))]
            out_shape = [jax.ShapeDtypeStruct((n_layers, out_split, M, n_piece), out_dtypes[0])]
        else:
            out_specs = [pl.BlockSpec((None, tm, tn), lambda i, j, k: (layer, i, j))]
            out_shape = [jax.ShapeDtypeStruct((n_layers, M, N), out_dtypes[0])]
        if prev is not None:
            in_specs.append(pl.BlockSpec(memory_space=pl.ANY))
            aliases = {len(operands): 0}
            operands.append(prev)
    n_in = len(operands)

    def body(*refs):
        a_ref, b_ref = refs[0], refs[1]
        side = refs[2:2 + ne + nr + nt_]
        outs = refs[n_in:n_in + no]

        def finish(acc):
            res = (acc,) if epilogue is None else epilogue(acc, *[r[...] for r in side])
            for o_ref, r, t in zip(outs, res, transposed):
                o_ref[...] = (r.T if t else r).astype(o_ref.dtype)

        part = lax.dot_general(a_ref[...].astype(BF16), b_ref[...].astype(BF16), dims,
                               preferred_element_type=F32)
        if nk == 1:
            finish(part)
        else:
            acc_ref = refs[-1]
            k = pl.program_id(2)

            @pl.when(k == 0)
            def _():
                acc_ref[...] = part

            @pl.when(k > 0)
            def _():
                acc_ref[...] += part

            @pl.when(k == nk - 1)
            def _():
                finish(acc_ref[...])

    res = pl.pallas_call(
        body, name=name, grid=(M // tm, N // tn, nk), in_specs=in_specs, out_specs=out_specs,
        out_shape=out_shape, scratch_shapes=[pltpu.VMEM((tm, tn), F32)] if nk > 1 else [],
        input_output_aliases=aliases,
        compiler_params=_cparams(("parallel", "parallel", "arbitrary")),
    )(*operands)
    return res[0] if no == 1 else tuple(res)


def _rope128(x, cos_t, sin_s):
    lane = lax.broadcasted_iota(jnp.int32, x.shape, 1)
    first = (lane % ROPE_DIM) < (ROPE_DIM // 2)
    swapped = jnp.where(first, pltpu.roll(x, LANES - ROPE_DIM // 2, 1), pltpu.roll(x, ROPE_DIM // 2, 1))
    return x * cos_t + swapped * sin_s


def _rope_pairs(acc, cos_t, sin_s, sign):
    parts = []
    for p in range(acc.shape[1] // PAIR_Q):
        parts.append(acc[:, p * PAIR_Q:p * PAIR_Q + LANES])
        parts.append(_rope128(acc[:, p * PAIR_Q + LANES:(p + 1) * PAIR_Q], cos_t, sign * sin_s))
    return jnp.concatenate(parts, axis=1)


def _rope_tables(pos_f, inv_freq_row, sign_row):
    T = pos_f.shape[0]
    tt = _tile(T, 512)

    def body(p_ref, f_ref, s_ref, cos_ref, sin_ref):
        ang = p_ref[...] * f_ref[...]
        cos_ref[...] = jnp.cos(ang)
        sin_ref[...] = jnp.sin(ang) * s_ref[...]

    return pl.pallas_call(
        body, name="rope_tables", grid=(T // tt,),
        in_specs=[pl.BlockSpec((tt, 1), lambda i: (i, 0)), pl.BlockSpec((1, LANES), lambda i: (0, 0)),
                  pl.BlockSpec((1, LANES), lambda i: (0, 0))],
        out_specs=[pl.BlockSpec((tt, LANES), lambda i: (i, 0))] * 2,
        out_shape=[jax.ShapeDtypeStruct((T, LANES), F32)] * 2,
        compiler_params=_cparams(("parallel",)),
    )(pos_f, inv_freq_row, sign_row)


def _unrope(dqx, cos_t, sin_s):
    T, W = dqx.shape
    tt = _tile(T, 512)

    def body(d_ref, c_ref, s_ref, o_ref):
        o_ref[...] = _rope_pairs(d_ref[...].astype(F32), c_ref[...], s_ref[...], -1.0).astype(BF16)

    return pl.pallas_call(
        body, name="mla_unrope", grid=(T // tt,),
        in_specs=[pl.BlockSpec((tt, W), lambda i: (i, 0)), pl.BlockSpec((tt, LANES), lambda i: (i, 0)),
                  pl.BlockSpec((tt, LANES), lambda i: (i, 0))],
        out_specs=pl.BlockSpec((tt, W), lambda i: (i, 0)),
        out_shape=jax.ShapeDtypeStruct((T, W), BF16),
        compiler_params=_cparams(("parallel",)),
    )(dqx, cos_t, sin_s)


def _row_specs(tt, D, per, n):
    return [pl.BlockSpec((None, 1, D), lambda i: (i // per, 0, 0)) for _ in range(n)]


def _norm_mod(x, gain, sc, sh, *, S, name):
    T, D = x.shape
    tt = _tile(S, 512)
    per = S // tt

    def body(x_ref, g_ref, sc_ref, sh_ref, h_ref, ht_ref):
        xv = x_ref[...]
        r = lax.rsqrt(jnp.mean(xv * xv, axis=-1, keepdims=True) + NORM_EPS)
        h = (xv * r) * g_ref[...] * (1.0 + sc_ref[...]) + sh_ref[...]
        h_ref[...] = h.astype(BF16)
        ht_ref[...] = h.T.astype(BF16)

    return pl.pallas_call(
        body, name=name, grid=(T // tt,),
        in_specs=[pl.BlockSpec((tt, D), lambda i: (i, 0)), pl.BlockSpec((1, D), lambda i: (0, 0))]
        + _row_specs(tt, D, per, 2),
        out_specs=[pl.BlockSpec((tt, D), lambda i: (i, 0)), pl.BlockSpec((D, tt), lambda i: (0, i))],
        out_shape=[jax.ShapeDtypeStruct((T, D), BF16), jax.ShapeDtypeStruct((D, T), BF16)],
        compiler_params=_cparams(("parallel",)),
    )(x, gain, sc, sh)


def _norm_mod_bwd(x, dh, dres, gain, sc, *, S, name):
    T, D = x.shape
    B = T // S
    tt = _tile(S, 512)
    per = S // tt

    def body(x_ref, dh_ref, dres_ref, g_ref, sc_ref, dx_ref, dsh_ref, dsc_ref, dg_ref):
        i = pl.program_id(0)
        xv = x_ref[...]
        dhv = dh_ref[...].astype(F32)
        r = lax.rsqrt(jnp.mean(xv * xv, axis=-1, keepdims=True) + NORM_EPS)
        n = xv * r
        g = g_ref[...]
        one_sc = 1.0 + sc_ref[...]
        dn = dhv * (g * one_sc)
        dx_ref[...] = dres_ref[...] + r * (dn - n * jnp.mean(dn * n, axis=-1, keepdims=True))
        dhn = dhv * n

        @pl.when(i % per == 0)
        def _():
            dsh_ref[...] = jnp.zeros_like(dsh_ref)
            dsc_ref[...] = jnp.zeros_like(dsc_ref)

        @pl.when(i == 0)
        def _():
            dg_ref[...] = jnp.zeros_like(dg_ref)

        dsh_ref[...] += jnp.sum(dhv, axis=0, keepdims=True)
        dsc_ref[...] += jnp.sum(dhn, axis=0, keepdims=True) * g
        dg_ref[...] += jnp.sum(dhn, axis=0, keepdims=True) * one_sc

    return pl.pallas_call(
        body, name=name, grid=(T // tt,),
        in_specs=[pl.BlockSpec((tt, D), lambda i: (i, 0))] * 3 + [pl.BlockSpec((1, D), lambda i: (0, 0))]
        + _row_specs(tt, D, per, 1),
        out_specs=[pl.BlockSpec((tt, D), lambda i: (i, 0))] + _row_specs(tt, D, per, 2)
        + [pl.BlockSpec((1, D), lambda i: (0, 0))],
        out_shape=[jax.ShapeDtypeStruct((T, D), F32), jax.ShapeDtypeStruct((B, 1, D), F32),
                   jax.ShapeDtypeStruct((B, 1, D), F32), jax.ShapeDtypeStruct((1, D), F32)],
        compiler_params=_cparams(("arbitrary",)),
    )(x, dh, dres, gain, sc)


def _gate_bwd(dx, y, g, *, S, name):
    T, D = dx.shape
    B = T // S
    tt = _tile(S, 512)
    per = S // tt

    def body(dx_ref, y_ref, g_ref, dy_ref, dg_ref):
        i = pl.program_id(0)
        dxv = dx_ref[...]
        dy_ref[...] = (dxv * g_ref[...]).astype(BF16)

        @pl.when(i % per == 0)
        def _():
            dg_ref[...] = jnp.zeros_like(dg_ref)

        dg_ref[...] += jnp.sum(dxv * y_ref[...], axis=0, keepdims=True)

    return pl.pallas_call(
        body, name=name, grid=(T // tt,),
        in_specs=[pl.BlockSpec((tt, D), lambda i: (i, 0))] * 2 + _row_specs(tt, D, per, 1),
        out_specs=[pl.BlockSpec((tt, D), lambda i: (i, 0))] + _row_specs(tt, D, per, 1),
        out_shape=[jax.ShapeDtypeStruct((T, D), BF16), jax.ShapeDtypeStruct((B, 1, D), F32)],
        compiler_params=_cparams(("arbitrary",)),
    )(dx, y, g)


def _final_loss(x, target, gain):
    T, D = x.shape
    tt = _tile(T, 512)

    def body(x_ref, t_ref, g_ref, dx_ref, dg_ref, loss_ref):
        i = pl.program_id(0)
        xv = x_ref[...]
        r = lax.rsqrt(jnp.mean(xv * xv, axis=-1, keepdims=True) + NORM_EPS)
        n = xv * r
        g = g_ref[...]
        err = n * g - t_ref[...]
        dy = err * (1.0 / D)
        dn = dy * g
        dx_ref[...] = r * (dn - n * jnp.mean(dn * n, axis=-1, keepdims=True))

        @pl.when(i == 0)
        def _():
            dg_ref[...] = jnp.zeros_like(dg_ref)
            loss_ref[...] = jnp.zeros_like(loss_ref)

        dg_ref[...] += jnp.sum(dy * n, axis=0, keepdims=True)
        loss_ref[...] += jnp.sum(jnp.sum(err * err, axis=-1, keepdims=True), axis=0, keepdims=True) * (0.5 / D)

    return pl.pallas_call(
        body, name="final_loss", grid=(T // tt,),
        in_specs=[pl.BlockSpec((tt, D), lambda i: (i, 0))] * 2 + [pl.BlockSpec((1, D), lambda i: (0, 0))],
        out_specs=[pl.BlockSpec((tt, D), lambda i: (i, 0)), pl.BlockSpec((1, D), lambda i: (0, 0)),
                   pl.BlockSpec((1, LANES), lambda i: (0, 0))],
        out_shape=[jax.ShapeDtypeStruct((T, D), F32), jax.ShapeDtypeStruct((1, D), F32),
                   jax.ShapeDtypeStruct((1, LANES), F32)],
        compiler_params=_cparams(("arbitrary",)),
    )(x, target, gain)


def _head_masks(ew):
    lane = lax.broadcasted_iota(jnp.int32, (1, PAIR_Q), 1)
    m0 = (lane < HEAD_DIM) | ((lane >= LANES) & (lane < LANES + ew))
    m1 = ((lane >= HEAD_DIM) & (lane < LANES)) | ((lane >= LANES + ew) & (lane < LANES + 2 * ew))
    return m0, m1


def _dot_nt(a, b):
    return lax.dot_general(a, b, (((1,), (1,)), ((), ())), preferred_element_type=F32)


def _dot_tn(a, b):
    return lax.dot_general(a, b, (((0,), (0,)), ((), ())), preferred_element_type=F32)


def _lane_halves(x, op):
    acc = x[:, 0:LANES]
    for g in range(1, x.shape[1] // LANES):
        acc = op(acc, x[:, g * LANES:(g + 1) * LANES])
    return acc


def _head_rows(cols_lane_replicated):
    t = cols_lane_replicated.T
    sub = lax.broadcasted_iota(jnp.int32, (8, t.shape[1]), 0)
    return jnp.where(sub == 1, t[HEAD_DIM:HEAD_DIM + 8], t[0:8])


def _attn_fwd(qx, kvx, *, S, scale, ew, name):
    T = qx.shape[0]
    P = qx.shape[1] // PAIR_Q
    B = T // S
    tk = _tile(S, ATTN_BLOCK)
    tq = _tile(S, ATTN_Q_ROWS)
    nq = S // tq
    per = tq // tk

    def body(q_ref, kv_ref, o_ref, lse_ref, ot_ref, m_sc, l_sc, acc_sc):
        qi = pl.program_id(2)
        q = q_ref[...]
        masks = _head_masks(ew)
        qh = [jnp.where(m, q, jnp.zeros_like(q)) for m in masks]

        def logits(h, k, diagonal):
            s = _dot_nt(qh[h], k)
            if scale != 1.0:
                s = s * scale
            if diagonal is None:
                return s
            row = lax.broadcasted_iota(jnp.int32, s.shape, 0)
            col = lax.broadcasted_iota(jnp.int32, s.shape, 1)
            return jnp.where(col + diagonal * tk <= row, s, NEG_BIG)

        def trip(first, count, n_diagonal=0):
            rows = [pl.ds(pl.multiple_of((first + u) * tk, tk), tk) for u in range(count)]
            diag = [None] * (count - n_diagonal) + list(range(n_diagonal))
            for h in range(2):
                ss = [logits(h, kv_ref[rows[u], 0:PAIR_Q], diag[u]) for u in range(count)]
                m_prev = m_sc[h]
                m_elem = m_prev
                for s in ss:
                    m_elem = jnp.maximum(m_elem, _lane_halves(s, jnp.maximum))
                m_new = jnp.broadcast_to(jnp.max(m_elem, axis=1, keepdims=True), (tq, LANES))
                alpha = jnp.exp(m_prev - m_new)
                l = alpha * l_sc[h]
                acc = alpha * acc_sc[h]
                for u, s in enumerate(ss):
                    p = jnp.concatenate([jnp.exp(s[:, g * LANES:(g + 1) * LANES] - m_new)
                                         for g in range(tk // LANES)], axis=1)
                    l = l + _lane_halves(p, jnp.add)
                    acc = acc + jnp.dot(p.astype(BF16), kv_ref[rows[u], PAIR_Q:PAIR_KV], preferred_element_type=F32)
                m_sc[h] = m_new
                l_sc[h] = l
                acc_sc[h] = acc

        m_sc[...] = jnp.full(m_sc.shape, NEG_BIG, F32)
        l_sc[...] = jnp.zeros_like(l_sc)
        acc_sc[...] = jnp.zeros_like(acc_sc)

        def loop_body(t, carry):
            trip(t * ATTN_UNROLL, ATTN_UNROLL)
            return carry

        below = qi * per
        lax.fori_loop(0, below // ATTN_UNROLL, loop_body, 0)
        for left in range(0, ATTN_UNROLL, per):
            @pl.when(below % ATTN_UNROLL == left)
            def _(left=left):
                trip(below - left, left + per, n_diagonal=per)

        lane = lax.broadcasted_iota(jnp.int32, (tq, LANES), 1)
        lo = lane < HEAD_DIM
        l = [jnp.sum(l_sc[h], axis=1, keepdims=True) for h in range(2)]
        o = jnp.where(lo, acc_sc[0] / l[0], acc_sc[1] / l[1])
        o_ref[...] = o.astype(BF16)
        ot_ref[...] = o.T.astype(BF16)
        lse = jnp.where(lo, m_sc[0] + jnp.log(l[0]), m_sc[1] + jnp.log(l[1]))
        for r in range(per):
            lse_ref[r] = _head_rows(lse[r * tk:(r + 1) * tk])

    return pl.pallas_call(
        body, name=name, grid=(B, P, nq),
        in_specs=[pl.BlockSpec((tq, PAIR_Q), lambda b, p, i: (b * nq + i, p)),
                  pl.BlockSpec((S, PAIR_KV), lambda b, p, i: (b, p))],
        out_specs=[pl.BlockSpec((tq, LANES), lambda b, p, i: (b * nq + i, p)),
                   pl.BlockSpec((per, None, 8, tk), lambda b, p, i: (b * nq + i, p, 0, 0)),
                   pl.BlockSpec((LANES, tq), lambda b, p, i: (p, b * nq + i))],
        out_shape=[jax.ShapeDtypeStruct((T, P * LANES), BF16), jax.ShapeDtypeStruct((T // tk, P, 8, tk), F32),
                   jax.ShapeDtypeStruct((P * LANES, T), BF16)],
        scratch_shapes=[pltpu.VMEM((2, tq, LANES), F32)] * 3,
        compiler_params=_cparams(("parallel", "parallel", "arbitrary")),
    )(qx, kvx)


def _attn_bwd(qx, kvx, o, lse, do, *, S, scale, ew, name, bias_grad=False):
    T = qx.shape[0]
    P = qx.shape[1] // PAIR_Q
    B = T // S
    tq = _tile(S, 256)
    nq = S // tq

    def body(q_ref, kv_ref, o_ref, lse_ref, do_ref, dq_ref, dkv_ref, *rest):
        kj = pl.program_id(2)
        if bias_grad:
            csum_ref, rsum_ref, dq_sc, delta_sc, dk_sc, dv_sc, cs_sc = rest
            cs_sc[...] = jnp.zeros_like(cs_sc)

            @pl.when(kj == 0)
            def _():
                rsum_ref[...] = jnp.zeros_like(rsum_ref)
        else:
            dq_sc, delta_sc, dk_sc, dv_sc = rest
        masks = _head_masks(ew)
        lane = lax.broadcasted_iota(jnp.int32, (tq, LANES), 1)
        lo = lane < HEAD_DIM
        vmask = [lo, jnp.logical_not(lo)]

        @pl.when(kj == 0)
        def _():
            dq_sc[...] = jnp.zeros_like(dq_sc)
            for c in range(nq):
                rows = pl.ds(c * tq, tq)
                x = do_ref[rows, :].astype(F32) * o_ref[rows, :].astype(F32)
                r0 = jnp.sum(jnp.where(lo, x, 0.0), axis=1, keepdims=True)
                r1 = jnp.sum(jnp.where(lo, 0.0, x), axis=1, keepdims=True)
                delta_sc[c] = _head_rows(jnp.where(lo, r0, r1))

        k = kv_ref[:, 0:PAIR_Q]
        v = kv_ref[:, PAIR_Q:PAIR_KV]
        kh = [jnp.where(m, k, jnp.zeros_like(k)) for m in masks]
        vh = [jnp.where(m, v, jnp.zeros_like(v)) for m in vmask]
        dk_sc[...] = jnp.zeros_like(dk_sc)
        dv_sc[...] = jnp.zeros_like(dv_sc)

        def step(qi, diagonal):
            rows = pl.ds(pl.multiple_of(qi * tq, tq), tq)
            q = q_ref[rows, :]
            dov = do_ref[rows, :]
            lse8 = lse_ref[qi]
            dl8 = delta_sc[qi]
            for h in range(2):
                st = _dot_nt(kh[h], q)
                if scale != 1.0:
                    st = st * scale
                if diagonal:
                    key = lax.broadcasted_iota(jnp.int32, st.shape, 0)
                    qry = lax.broadcasted_iota(jnp.int32, st.shape, 1)
                    st = jnp.where(key <= qry, st, NEG_BIG)
                pt = jnp.exp(st - lse8[h:h + 1, :])
                dpt = _dot_nt(vh[h], dov)
                dst = pt * (dpt - dl8[h:h + 1, :])
                if bias_grad:
                    cs_sc[h] += _lane_halves(dst, jnp.add)
                    rsum_ref[qi, h:h + 1, :] += jnp.sum(dst, axis=0, keepdims=True)
                if scale != 1.0:
                    dst = dst * scale
                ptb = pt.astype(BF16)
                dstb = dst.astype(BF16)
                dv_sc[h] += jnp.dot(ptb, dov, preferred_element_type=F32)
                dk_sc[h] += jnp.dot(dstb, q, preferred_element_type=F32)
                dq_sc[rows, :] += _dot_tn(dstb, kh[h])

        above = nq - 1 - kj
        for left in range(ATTN_UNROLL):
            @pl.when(above % ATTN_UNROLL == left)
            def _(left=left):
                step(kj, True)
                for u in range(left):
                    step(kj + 1 + u, False)

        def loop_body(t, carry):
            for u in range(ATTN_UNROLL):
                step(kj + 1 + above % ATTN_UNROLL + t * ATTN_UNROLL + u, False)
            return carry

        lax.fori_loop(0, above // ATTN_UNROLL, loop_body, 0)
        dkv_ref[:, 0:PAIR_Q] = (jnp.where(masks[0], dk_sc[0], 0.0) + jnp.where(masks[1], dk_sc[1], 0.0)).astype(BF16)
        dkv_ref[:, PAIR_Q:PAIR_KV] = jnp.where(lo, dv_sc[0], dv_sc[1]).astype(BF16)
        if bias_grad:
            csum_ref[...] = jnp.where(lo, jnp.sum(cs_sc[0], axis=1, keepdims=True),
                                      jnp.sum(cs_sc[1], axis=1, keepdims=True))

        @pl.when(kj == nq - 1)
        def _():
            dq_ref[...] = dq_sc[...].astype(BF16)

    rows_spec = pl.BlockSpec((nq, None, 8, tq), lambda b, p, j: (b, p, 0, 0))
    out_specs = [pl.BlockSpec((S, PAIR_Q), lambda b, p, j: (b, p)),
                 pl.BlockSpec((tq, PAIR_KV), lambda b, p, j: (b * nq + j, p))]
    out_shape = [jax.ShapeDtypeStruct((T, P * PAIR_Q), BF16), jax.ShapeDtypeStruct((T, P * PAIR_KV), BF16)]
    scratch = [pltpu.VMEM((S, PAIR_Q), F32), pltpu.VMEM((nq, 8, tq), F32),
               pltpu.VMEM((2, tq, PAIR_Q), F32), pltpu.VMEM((2, tq, LANES), F32)]
    if bias_grad:
        out_specs += [pl.BlockSpec((tq, LANES), lambda b, p, j: (b * nq + j, p)), rows_spec]
        out_shape += [jax.ShapeDtypeStruct((T, P * LANES), F32), jax.ShapeDtypeStruct((T // tq, P, 8, tq), F32)]
        scratch.append(pltpu.VMEM((2, tq, LANES), F32))
    return pl.pallas_call(
        body, name=name, grid=(B, P, nq),
        in_specs=[pl.BlockSpec((S, PAIR_Q), lambda b, p, j: (b, p)),
                  pl.BlockSpec((tq, PAIR_KV), lambda b, p, j: (b * nq + j, p)),
                  pl.BlockSpec((S, LANES), lambda b, p, j: (b, p)), rows_spec,
                  pl.BlockSpec((S, LANES), lambda b, p, j: (b, p))],
        out_specs=out_specs, out_shape=out_shape, scratch_shapes=scratch,
        compiler_params=_cparams(("parallel", "parallel", "arbitrary")),
    )(qx, kvx, o, lse, do)


def _fox_consts(P):
    H = 2 * P
    eq = np.zeros((3 * LANES, P * LANES), np.float32)
    ek = np.zeros((3 * LANES, P * LANES), np.float32)
    ones_q = np.zeros((1, P * LANES), np.float32)
    ones_k = np.zeros((1, P * LANES), np.float32)
    for h in range(H):
        base = (h // 2) * LANES + FOX_EXTRA * (h % 2)
        for part in range(3):
            eq[part * LANES + h, base + part] = 1.0
            ones_q[0, base + 3 + part] = 1.0
            ones_k[0, base + part] = 1.0
            ek[part * LANES + h, base + 3 + part] = -1.0
    return eq, ek, ones_q, ones_k


def _split3(f):
    hi = f.astype(BF16)
    r = f - hi.astype(F32)
    mid = r.astype(BF16)
    lo = (r - mid.astype(F32)).astype(BF16)
    return hi, mid, lo


def _tri_sum(tri, x):
    hi, mid, lo = _split3(x)
    return (jnp.dot(tri, hi, preferred_element_type=F32) + jnp.dot(tri, mid, preferred_element_type=F32)
            + jnp.dot(tri, lo, preferred_element_type=F32))


def _log1p_pos(e):
    return jnp.where(e < 0.01, e * (1.0 - e * (0.5 - e * (1.0 / 3.0))), jnp.log(1.0 + e))


def _fox_prep(qkv, fl, b_row, *, S, D, name):
    T = qkv.shape[0]
    P = D // LANES
    B = T // S
    tt = _tile(S, 256)
    per = S // tt
    eq, ek, ones_q, ones_k = _fox_consts(P)
    q_scale = HEAD_DIM ** -0.5

    def body(q_ref, k_ref, v_ref, fl_ref, b_ref, eq_ref, ek_ref, oq_ref, ok_ref, qx_ref, kvx_ref, carry):
        i = pl.program_id(1)

        @pl.when(i == 0)
        def _():
            carry[...] = jnp.zeros_like(carry)

        z = fl_ref[...] + b_ref[...]
        logf = jnp.minimum(z, 0.0) - _log1p_pos(jnp.exp(-jnp.abs(z)))
        row = lax.broadcasted_iota(jnp.int32, (tt, tt), 0)
        col = lax.broadcasted_iota(jnp.int32, (tt, tt), 1)
        tri = (col <= row).astype(BF16)
        f = _tri_sum(tri, logf) + carry[...]
        carry[...] = f[tt - 1:tt, :]
        parts = jnp.concatenate(_split3(f), axis=1)
        xq = jnp.dot(parts, eq_ref[...], preferred_element_type=F32) + oq_ref[...]
        xk = jnp.dot(parts, ek_ref[...], preferred_element_type=F32) + ok_ref[...]
        for p in range(P):
            c = slice(p * LANES, (p + 1) * LANES)
            qx_ref[:, p * PAIR_Q:p * PAIR_Q + LANES] = (q_ref[:, c].astype(F32) * q_scale).astype(BF16)
            qx_ref[:, p * PAIR_Q + LANES:(p + 1) * PAIR_Q] = xq[:, c].astype(BF16)
            kvx_ref[:, p * PAIR_KV:p * PAIR_KV + LANES] = k_ref[:, c]
            kvx_ref[:, p * PAIR_KV + LANES:p * PAIR_KV + PAIR_Q] = xk[:, c].astype(BF16)
            kvx_ref[:, p * PAIR_KV + PAIR_Q:(p + 1) * PAIR_KV] = v_ref[:, c]

    tok = lambda b, i: (b * per + i, 0)
    const = lambda b, i: (0, 0)
    return pl.pallas_call(
        body, name=name, grid=(B, per),
        in_specs=[pl.BlockSpec((tt, D), lambda b, i: (b * per + i, 0)),
                  pl.BlockSpec((tt, D), lambda b, i: (b * per + i, 1)),
                  pl.BlockSpec((tt, D), lambda b, i: (b * per + i, 2)),
                  pl.BlockSpec((tt, LANES), tok), pl.BlockSpec((1, LANES), const),
                  pl.BlockSpec(eq.shape, const), pl.BlockSpec(ek.shape, const),
                  pl.BlockSpec(ones_q.shape, const), pl.BlockSpec(ones_k.shape, const)],
        out_specs=[pl.BlockSpec((tt, P * PAIR_Q), tok), pl.BlockSpec((tt, P * PAIR_KV), tok)],
        out_shape=[jax.ShapeDtypeStruct((T, P * PAIR_Q), BF16), jax.ShapeDtypeStruct((T, P * PAIR_KV), BF16)],
        scratch_shapes=[pltpu.VMEM((1, LANES), F32)],
        compiler_params=_cparams(("arbitrary", "arbitrary")),
    )(qkv, qkv, qkv, fl, b_row, jnp.asarray(eq, BF16), jnp.asarray(ek, BF16), jnp.asarray(ones_q), jnp.asarray(ones_k))


def _fox_unprep(dqx, dkvx, csum, rsum, fl, b_row, *, S, D, name):
    T = dqx.shape[0]
    P = D // LANES
    B = T // S
    tt = _tile(S, 256)
    per = S // tt
    q_scale = HEAD_DIM ** -0.5

    def body(dq_ref, dkv_ref, cs_ref, rs_ref, fl_ref, b_ref, dqkv_ref, dfl_ref, db_ref, carry):
        b = pl.program_id(0)
        i = pl.program_id(1)

        @pl.when(i == 0)
        def _():
            carry[...] = jnp.zeros_like(carry)

        @pl.when((i == 0) & (b == 0))
        def _():
            db_ref[...] = jnp.zeros_like(db_ref)

        df = rs_ref[...] - cs_ref[...]
        for p in range(P):
            rq = slice(p * LANES, (p + 1) * LANES)
            dqkv_ref[:, rq] = (dq_ref[:, p * PAIR_Q:p * PAIR_Q + LANES].astype(F32) * q_scale).astype(BF16)
            dqkv_ref[:, D + p * LANES:D + (p + 1) * LANES] = dkv_ref[:, p * PAIR_KV:p * PAIR_KV + LANES]
            dqkv_ref[:, 2 * D + p * LANES:2 * D + (p + 1) * LANES] = dkv_ref[:, p * PAIR_KV + PAIR_Q:(p + 1) * PAIR_KV]
        row = lax.broadcasted_iota(jnp.int32, (tt, tt), 0)
        col = lax.broadcasted_iota(jnp.int32, (tt, tt), 1)
        tri = (col >= row).astype(BF16)
        dlogf = _tri_sum(tri, df) + carry[...]
        carry[...] = dlogf[0:1, :]
        z = fl_ref[...] + b_ref[...]
        e = jnp.exp(-jnp.abs(z))
        sig_neg = jnp.where(z >= 0.0, e, 1.0) / (1.0 + e)
        dfl = dlogf * sig_neg
        dfl_ref[...] = dfl.astype(BF16)
        db_ref[...] += jnp.sum(dfl, axis=0, keepdims=True)

    rev = lambda b, i: (b * per + per - 1 - i, 0)
    const = lambda b, i: (0, 0)
    return pl.pallas_call(
        body, name=name, grid=(B, per),
        in_specs=[pl.BlockSpec((tt, P * PAIR_Q), rev), pl.BlockSpec((tt, P * PAIR_KV), rev),
                  pl.BlockSpec((tt, LANES), rev), pl.BlockSpec((tt, LANES), rev), pl.BlockSpec((tt, LANES), rev),
                  pl.BlockSpec((1, LANES), const)],
        out_specs=[pl.BlockSpec((tt, 3 * D), rev), pl.BlockSpec((tt, LANES), rev), pl.BlockSpec((1, LANES), const)],
        out_shape=[jax.ShapeDtypeStruct((T, 3 * D), BF16), jax.ShapeDtypeStruct((T, LANES), BF16),
                   jax.ShapeDtypeStruct((1, LANES), F32)],
        scratch_shapes=[pltpu.VMEM((1, LANES), F32)],
        compiler_params=_cparams(("arbitrary", "arbitrary")),
    )(dqx, dkvx, csum, rsum, fl, b_row)


def _rms(x):
    r = lax.rsqrt(jnp.mean(x * x, axis=-1, keepdims=True) + NORM_EPS)
    return x * r, r


def _mla_mid(lat, gq, gkv, cos_t, sin_s, *, name):
    T, W = lat.shape
    Rq = W - 2 * LANES
    tt = _tile(T, 512)

    def body(l_ref, gq_ref, gkv_ref, c_ref, s_ref, o_ref, ot_ref):
        nq, _ = _rms(l_ref[:, 0:Rq])
        nkv, _ = _rms(l_ref[:, Rq:Rq + LANES])
        parts = [nq * gq_ref[...], nkv * gkv_ref[...], _rope128(l_ref[:, Rq + LANES:W], c_ref[...], s_ref[...])]
        out = jnp.concatenate(parts, axis=1)
        o_ref[...] = out.astype(BF16)
        ot_ref[...] = out.T.astype(BF16)

    return pl.pallas_call(
        body, name=name, grid=(T // tt,),
        in_specs=[pl.BlockSpec((tt, W), lambda i: (i, 0)), pl.BlockSpec((1, Rq), lambda i: (0, 0)),
                  pl.BlockSpec((1, LANES), lambda i: (0, 0)), pl.BlockSpec((tt, LANES), lambda i: (i, 0)),
                  pl.BlockSpec((tt, LANES), lambda i: (i, 0))],
        out_specs=[pl.BlockSpec((tt, W), lambda i: (i, 0)), pl.BlockSpec((W, tt), lambda i: (0, i))],
        out_shape=[jax.ShapeDtypeStruct((T, W), BF16), jax.ShapeDtypeStruct((W, T), BF16)],
        compiler_params=_cparams(("parallel",)),
    )(lat, gq, gkv, cos_t, sin_s)


def _mla_mid_bwd(lat, dcq, dckr, gq, gkv, cos_t, sin_s, *, name):
    T, W = lat.shape
    Rq = W - 2 * LANES
    tt = _tile(T, 512)

    def norm_bwd(x, dy, g):
        n, r = _rms(x)
        dn = dy * g
        return r * (dn - n * jnp.mean(dn * n, axis=-1, keepdims=True)), jnp.sum(dy * n, axis=0, keepdims=True)

    def body(l_ref, dq_ref, dk_ref, gq_ref, gkv_ref, c_ref, s_ref, o_ref, dgq_ref, dgkv_ref):
        i = pl.program_id(0)

        @pl.when(i == 0)
        def _():
            dgq_ref[...] = jnp.zeros_like(dgq_ref)
            dgkv_ref[...] = jnp.zeros_like(dgkv_ref)

        dxq, dgq = norm_bwd(l_ref[:, 0:Rq], dq_ref[...], gq_ref[...])
        dxkv, dgkv = norm_bwd(l_ref[:, Rq:Rq + LANES], dk_ref[:, 0:LANES], gkv_ref[...])
        o_ref[:, 0:Rq] = dxq.astype(BF16)
        o_ref[:, Rq:Rq + LANES] = dxkv.astype(BF16)
        o_ref[:, Rq + LANES:W] = _rope128(dk_ref[:, LANES:2 * LANES], c_ref[...], -s_ref[...]).astype(BF16)
        dgq_ref[...] += dgq
        dgkv_ref[...] += dgkv

    return pl.pallas_call(
        body, name=name, grid=(T // tt,),
        in_specs=[pl.BlockSpec((tt, W), lambda i: (i, 0)), pl.BlockSpec((tt, Rq), lambda i: (i, 0)),
                  pl.BlockSpec((tt, 2 * LANES), lambda i: (i, 0)), pl.BlockSpec((1, Rq), lambda i: (0, 0)),
                  pl.BlockSpec((1, LANES), lambda i: (0, 0)), pl.BlockSpec((tt, LANES), lambda i: (i, 0)),
                  pl.BlockSpec((tt, LANES), lambda i: (i, 0))],
        out_specs=[pl.BlockSpec((tt, W), lambda i: (i, 0)), pl.BlockSpec((1, Rq), lambda i: (0, 0)),
                   pl.BlockSpec((1, LANES), lambda i: (0, 0))],
        out_shape=[jax.ShapeDtypeStruct((T, W), BF16), jax.ShapeDtypeStruct((1, Rq), F32),
                   jax.ShapeDtypeStruct((1, LANES), F32)],
        compiler_params=_cparams(("arbitrary",)),
    )(lat, dcq, dckr, gq, gkv, cos_t, sin_s)


def _uq_to_pairs(w):
    Rq = w.shape[0]
    P = w.shape[1] // (2 * (HEAD_DIM + ROPE_DIM))
    w4 = w.reshape(Rq, P, 2, HEAD_DIM + ROPE_DIM)
    nope = w4[..., :HEAD_DIM].reshape(Rq, P, 2 * HEAD_DIM)
    rope = w4[..., HEAD_DIM:].reshape(Rq, P, 2 * ROPE_DIM)
    pad = jnp.zeros((Rq, P, PAIR_Q - 2 * HEAD_DIM - 2 * ROPE_DIM), w.dtype)
    return jnp.concatenate([nope, rope, pad], axis=-1).reshape(Rq, P * PAIR_Q)


def _uq_from_pairs(g):
    Rq = g.shape[0]
    P = g.shape[1] // PAIR_Q
    g3 = g.reshape(Rq, P, PAIR_Q)
    nope = g3[..., :2 * HEAD_DIM].reshape(Rq, P, 2, HEAD_DIM)
    rope = g3[..., 2 * HEAD_DIM:2 * HEAD_DIM + 2 * ROPE_DIM].reshape(Rq, P, 2, ROPE_DIM)
    return jnp.concatenate([nope, rope], axis=-1).reshape(Rq, P * 2 * (HEAD_DIM + ROPE_DIM))


def _ukv_to_pairs(w):
    P = w.shape[1] // (4 * HEAD_DIM)
    w4 = w.reshape(KV_RANK, P, 2, 2 * HEAD_DIM)
    kn = w4[..., :HEAD_DIM].reshape(KV_RANK, P, 2 * HEAD_DIM)
    vv = w4[..., HEAD_DIM:].reshape(KV_RANK, P, 2 * HEAD_DIM)
    top = jnp.concatenate([kn, jnp.zeros((KV_RANK, P, LANES), w.dtype), vv], axis=-1)
    place = np.zeros((LANES, P, PAIR_KV), np.float32)
    for r in range(ROPE_DIM):
        place[r, :, LANES + r] = 1.0
        place[r, :, LANES + ROPE_DIM + r] = 1.0
    return jnp.concatenate([top, jnp.asarray(place, w.dtype)], axis=0).reshape(KV_RANK + LANES, P * PAIR_KV)


def _ukv_from_pairs(g):
    P = g.shape[1] // PAIR_KV
    g3 = g[:KV_RANK].reshape(KV_RANK, P, PAIR_KV)
    kn = g3[..., :2 * HEAD_DIM].reshape(KV_RANK, P, 2, HEAD_DIM)
    vv = g3[..., PAIR_Q:].reshape(KV_RANK, P, 2, HEAD_DIM)
    return jnp.concatenate([kn, vv], axis=-1).reshape(KV_RANK, P * 4 * HEAD_DIM)


def _mlp_fwd(h2, w, i, x1, gate, *, S):
    def act(acc):
        u = jnp.square(jnp.maximum(acc, 0.0))
        return acc, u, u

    p, u, u_t = _mm(h2, w["mlp_w1"], "nn", name=f"mlp_up_{i}", b_layer=i, out_dtypes=(BF16, BF16, BF16),
                    out_t=(False, False, True), epilogue=act)
    x2, z = _mm(u, w["mlp_w2"], "nn", name=f"mlp_down_{i}", b_layer=i, out_dtypes=(F32, F32), extras=(x1,),
                rowvecs=(gate,), seq=S, epilogue=lambda acc, xr, g: (xr + g * acc, acc))
    return x2, (p, u_t, z)


STACKED_GRADS = ("fox_out", "mla_down", "mla_uq", "mla_ukv", "mla_out", "mlp_w1", "mlp_w2")


def _local_step(x, target, pos_f, inv_freq_row, sign_row, mod, w, slots, *, S, hooks=None):
    hooks = hooks or {}
    T, D = x.shape
    L = mod.shape[0]
    L2 = len(w["fox_out"])
    cos_t, sin_s = _rope_tables(pos_f, inv_freq_row, sign_row)
    saved = []
    for i in range(L):
        j = i // 2
        sh_m, sc_m, g_m, sh_f, sc_f, g_f = (mod[i, s] for s in range(6))
        h, h_t = _norm_mod(x, w["norm_mix_g"][i], sc_m, sh_m, S=S, name=f"norm_mix_{i}")
        if i % 2 == 0:
            qkv = _mm(h, w["fox_qkv"], "nn", name=f"fox_qkv_{i}", b_layer=j, out_dtypes=(BF16,))
            fl = _mm(h, w["fox_f"], "nn", name=f"fox_f_{i}", b_layer=j)
            qx, kvx = _fox_prep(qkv, fl, w["fox_b"][j], S=S, D=D, name=f"fox_prep_{i}")
            o, lse, o_t = _attn_fwd(qx, kvx, S=S, scale=1.0, ew=FOX_EXTRA, name=f"fox_attn_{i}")
            mix = (qx, kvx, o, lse, o_t, fl)
            w_out = w["fox_out"]
        else:
            lat = _mm(h, w["mla_down"], "nn", name=f"mla_down_{i}", b_layer=j)
            Rq = lat.shape[1] - 2 * LANES
            cqr, cqr_t = _mla_mid(lat, w["mla_gq"][j], w["mla_gkv"][j], cos_t, sin_s, name=f"mla_mid_{i}")
            qx = _mm(cqr, w["mla_uq"], "nn", name=f"mla_uq_{i}", b_layer=j, out_dtypes=(BF16,), a_sz=Rq, tk=Rq,
                     tables=(cos_t, sin_s), epilogue=lambda acc, c, s: (_rope_pairs(acc, c, s, 1.0),))
            kvx = _mm(cqr, w["mla_ukv"], "nn", name=f"mla_ukv_{i}", b_layer=j, out_dtypes=(BF16,), a_off=Rq,
                      a_sz=2 * LANES, tk=2 * LANES, tn=PAIR_KV)
            o, lse, o_t = _attn_fwd(qx, kvx, S=S, scale=(HEAD_DIM + ROPE_DIM) ** -0.5, ew=ROPE_DIM,
                                    name=f"mla_attn_{i}")
            mix = (qx, kvx, o, lse, o_t, lat, cqr_t)
            w_out = w["mla_out"]
        x1, y = _mm(o, w_out, "nn", name=f"mix_out_{i}", b_layer=j, out_dtypes=(F32, F32), extras=(x,),
                    rowvecs=(g_m,), seq=S, epilogue=lambda acc, xr, g: (xr + g * acc, acc))
        h2, h2_t = _norm_mod(x1, w["norm_mlp_g"][i], sc_f, sh_f, S=S, name=f"norm_mlp_{i}")
        if i == 0 and "fwd_mlp0" in hooks:
            w = hooks["fwd_mlp0"](x1, w)
        x2, mlp = _mlp_fwd(h2, w, i, x1, g_f, S=S)
        saved.append((x, h_t, mix, y, x1, h2_t, mlp))
        x = x2
        if i == 0 and "fwd_layer1" in hooks:
            w = hooks["fwd_layer1"](x, w)

    dx, dg_final, loss = _final_loss(x, target, w["final_norm_g"])
    n_split = w["mlp_w1"][0][0].shape[1]

    grads = {k: [None] * len(w[k]) for k in ("norm_mix_g", "norm_mlp_g", "fox_b", "mla_gq", "mla_gkv")}
    grads.update({k: [None] * L2 for k in ("fox_qkv", "fox_f")})
    grads.update({k: {} for k in STACKED_GRADS})
    grads["final_norm_g"] = dg_final

    def stacked(key, layer, _, a_t, b, **kw):
        group, idx, count = slots[(key, layer)]
        grads[key][group] = _mm(a_t, b, "nn", out_stack=(grads[key].get(group), idx, count), **kw)

    dmod = [None] * L
    for i in reversed(range(L)):
        j = i // 2
        x0, h_t, mix, y, x1, h2_t, (p, u_t, z) = saved[i]
        sh_m, sc_m, g_m, sh_f, sc_f, g_f = (mod[i, s] for s in range(6))
        if i == 0 and "bwd_layer0" in hooks:
            g_f = g_f + hooks["bwd_layer0"](grads)[0, 0]
        dz, dg_f = _gate_bwd(dx, z, g_f, S=S, name=f"gate_mlp_bwd_{i}")
        stacked("mlp_w2", i, L, u_t, dz, name=f"mlp_w2_grad_{i}")
        dp = _mm(dz, w["mlp_w2"], "nt", name=f"mlp_down_bwd_{i}", b_layer=i, out_dtypes=(BF16,), extras=(p,),
                 epilogue=lambda acc, pv: (acc * (2.0 * jnp.maximum(pv.astype(F32), 0.0)),))
        stacked("mlp_w1", i, L, h2_t, dp, name=f"mlp_w1_grad_{i}", out_split=n_split)
        if i == 0 and "bwd_mix0" in hooks:
            g_m = g_m + hooks["bwd_mix0"](grads)[0, 0]
        dh2 = _mm(dp, w["mlp_w1"], "nt", name=f"mlp_up_bwd_{i}", b_layer=i)
        dx1, dsh_f, dsc_f, dgn = _norm_mod_bwd(x1, dh2, dx, w["norm_mlp_g"][i], sc_f, S=S, name=f"norm_mlp_bwd_{i}")
        grads["norm_mlp_g"][i] = dgn
        dy, dg_m = _gate_bwd(dx1, y, g_m, S=S, name=f"gate_mix_bwd_{i}")
        if i % 2 == 0:
            qx, kvx, o, lse, o_t, fl = mix
            stacked("fox_out", j, L2, o_t, dy, name=f"fox_out_grad_{i}")
            do = _mm(dy, w["fox_out"], "nt", name=f"fox_out_bwd_{i}", b_layer=j, out_dtypes=(BF16,))
            dqx, dkvx, csum, rsum = _attn_bwd(qx, kvx, o, lse, do, S=S, scale=1.0, ew=FOX_EXTRA,
                                              name=f"fox_attn_bwd_{i}", bias_grad=True)
            n_heads = D // HEAD_DIM
            csum = jnp.pad(csum.reshape(T, n_heads, HEAD_DIM)[:, :, 0], ((0, 0), (0, LANES - n_heads)))
            rsum = jnp.transpose(rsum[:, :, :2, :], (0, 3, 1, 2)).reshape(T, n_heads)
            rsum = jnp.pad(rsum, ((0, 0), (0, LANES - n_heads)))
            dqkv, dfl, db = _fox_unprep(dqx, dkvx, csum, rsum, fl, w["fox_b"][j], S=S, D=D, name=f"fox_unprep_{i}")
            grads["fox_b"][j] = db
            grads["fox_qkv"][j] = _mm(h_t, dqkv, "nn", name=f"fox_qkv_grad_{i}")
            grads["fox_f"][j] = _mm(h_t, dfl, "nn", name=f"fox_f_grad_{i}")
            dh_f = _mm(dfl, w["fox_f"], "nt", name=f"fox_f_bwd_{i}", b_layer=j)
            dh = _mm(dqkv, w["fox_qkv"], "nt", name=f"fox_qkv_bwd_{i}", b_layer=j, extras=(dh_f,),
                     epilogue=lambda acc, e: (acc + e,))
        else:
            qx, kvx, o, lse, o_t, lat, cqr_t = mix
            Rq = lat.shape[1] - 2 * LANES
            stacked("mla_out", j, L2, o_t, dy, name=f"mla_out_grad_{i}")
            do = _mm(dy, w["mla_out"], "nt", name=f"mla_out_bwd_{i}", b_layer=j, out_dtypes=(BF16,))
            dqx, dkvx = _attn_bwd(qx, kvx, o, lse, do, S=S, scale=(HEAD_DIM + ROPE_DIM) ** -0.5, ew=ROPE_DIM,
                                  name=f"mla_attn_bwd_{i}")
            dqpre = _unrope(dqx, cos_t, sin_s)
            stacked("mla_uq", j, L2, cqr_t[:Rq], dqpre, name=f"mla_uq_grad_{i}", out_split=n_split)
            stacked("mla_ukv", j, L2, cqr_t[Rq:], dkvx, name=f"mla_ukv_grad_{i}", tn=PAIR_KV, out_split=n_split)
            dcq = _mm(dqpre, w["mla_uq"], "nt", name=f"mla_uq_bwd_{i}", b_layer=j)
            dckr = _mm(dkvx, w["mla_ukv"], "nt", name=f"mla_ukv_bwd_{i}", b_layer=j, tk=PAIR_KV * 2)
            dlat, dgq, dgkv = _mla_mid_bwd(lat, dcq, dckr, w["mla_gq"][j], w["mla_gkv"][j], cos_t, sin_s,
                                           name=f"mla_mid_bwd_{i}")
            grads["mla_gq"][j] = dgq
            grads["mla_gkv"][j] = dgkv
            stacked("mla_down", j, L2, h_t, dlat, name=f"mla_down_grad_{i}")
            dh = _mm(dlat, w["mla_down"], "nt", name=f"mla_down_bwd_{i}", b_layer=j)
        dx, dsh_m, dsc_m, dgn = _norm_mod_bwd(x0, dh, dx1, w["norm_mix_g"][i], sc_m, S=S, name=f"norm_mix_bwd_{i}")
        grads["norm_mix_g"][i] = dgn
        dmod[i] = jnp.stack([dsh_m, dsc_m, dg_m, dsh_f, dsc_f, dg_f])
    return loss, dx, jnp.stack(dmod), grads


GATHERED = ("fox_in", "fox_out", "mla_down", "mla_uq", "mla_ukv", "mla_out", "mlp_w1", "mlp_w2")
ROW_SHARDED = ("fox_out", "mla_down", "mla_out", "mlp_w2")


def _shard_layouts(wts):
    dkv = wts["mla_w_dkv"]
    dkv = jnp.pad(dkv, ((0, 0), (0, 0), (0, 2 * LANES - dkv.shape[2])))
    return {
        "fox_in": _pad_lanes(wts["fox_w_in"].astype(BF16)),
        "fox_out": wts["fox_w_out"].astype(BF16),
        "mla_down": jnp.concatenate([wts["mla_w_dq"], dkv], axis=2).astype(BF16),
        "mla_uq": jax.vmap(_uq_to_pairs)(wts["mla_w_uq"].astype(BF16)),
        "mla_ukv": jax.vmap(_ukv_to_pairs)(wts["mla_w_ukv"].astype(BF16)),
        "mla_out": wts["mla_w_out"].astype(BF16),
        "mlp_w1": wts["mlp_w1"].astype(BF16),
        "mlp_w2": wts["mlp_w2"].astype(BF16),
    }


def _small_layouts(small):
    return {
        "fox_b": [jnp.pad(b, (0, LANES - b.shape[0]))[None, :] for b in small["fox_b_f"]],
        "mla_gq": [g[None, :] for g in small["mla_q_norm_g"]],
        "mla_gkv": [g[None, :] for g in small["mla_kv_norm_g"]],
        "norm_mix_g": [g[None, :] for g in small["norm_mix_g"]],
        "norm_mlp_g": [g[None, :] for g in small["norm_mlp_g"]],
        "final_norm_g": small["final_norm_g"][None, :],
    }


def _comm_groups(L, L2):
    rest = [("fox_in", 1, L2 - 1), ("fox_out", 1, L2 - 1), ("mla_down", 0, L2), ("mla_uq", 0, L2),
            ("mla_ukv", 0, L2), ("mla_out", 0, L2), ("mlp_w1", 1, L - 1), ("mlp_w2", 1, L - 1)]
    return {"mix0": [("fox_in", 0, 1), ("fox_out", 0, 1)], "mlp0": [("mlp_w1", 0, 1), ("mlp_w2", 0, 1)],
            "rest": [e for e in rest if e[2] > 0]}


def _layer_slots(groups):
    return {(n, s + l): (g, l, cnt) for g, entries in groups.items() for n, s, cnt in entries for l in range(cnt)}


def _pad_lanes(a):
    cols = a.shape[-1]
    return jnp.pad(a, [(0, 0)] * (a.ndim - 1) + [(0, -cols % LANES)])


def _weight_views(name, gathered, D, n_fox_heads):
    n, ns, rows, cols = gathered.shape
    if name == "fox_in":
        true_cols = (3 * D + n_fox_heads) // ns
        fox = jnp.concatenate([gathered[:, k, :, :true_cols] for k in range(ns)], axis=-1)
        return {"fox_qkv": fox[:, :, :3 * D], "fox_f": _pad_lanes(fox[:, :, 3 * D:])}
    if name in ROW_SHARDED:
        return {name: gathered.reshape(n, ns * rows, cols)}
    return {name: gathered}


def _grad_pieces(name, g, qkv_f, n_fox_heads, ns):
    if name == "fox_in":
        fox = jnp.stack([jnp.concatenate([a, b[:, :n_fox_heads]], axis=1) for a, b in qkv_f])
        cols = fox.shape[2] // ns
        return jnp.stack([_pad_lanes(fox[:, :, k * cols:(k + 1) * cols]) for k in range(ns)], axis=1)
    if name in ROW_SHARDED:
        return g.reshape(g.shape[0], ns, g.shape[1] // ns, g.shape[2])
    return g


def _small_grads(g, n_fox_heads):
    return {
        "norm_mix_g": jnp.concatenate(g["norm_mix_g"], axis=0),
        "norm_mlp_g": jnp.concatenate(g["norm_mlp_g"], axis=0),
        "final_norm_g": g["final_norm_g"][0],
        "fox_b_f": jnp.concatenate(g["fox_b"], axis=0)[:, :n_fox_heads],
        "mla_q_norm_g": jnp.concatenate(g["mla_gq"], axis=0),
        "mla_kv_norm_g": jnp.concatenate(g["mla_gkv"], axis=0),
    }


def _silu(c):
    return c * (1.0 / (1.0 + jnp.exp(-c)))


def _ada_fwd(c_all, ada_w, ada_b_cols):
    L, D, C = ada_w.shape
    Bg = c_all.shape[0]
    tc = _tile(C, 512)

    def body(c_ref, w_ref, b_ref, o_ref):
        ca = _silu(c_ref[...]).astype(BF16)
        o_ref[...] = jnp.dot(ca, w_ref[...].astype(BF16), preferred_element_type=F32) + b_ref[...]

    return pl.pallas_call(
        body, name="ada_fwd", grid=(L, C // tc),
        in_specs=[pl.BlockSpec((Bg, D), lambda l, j: (0, 0)), pl.BlockSpec((None, D, tc), lambda l, j: (l, 0, j)),
                  pl.BlockSpec((None, 1, tc), lambda l, j: (l, 0, j))],
        out_specs=pl.BlockSpec((None, Bg, tc), lambda l, j: (l, 0, j)),
        out_shape=jax.ShapeDtypeStruct((L, Bg, C), F32),
        compiler_params=_cparams(("parallel", "parallel")),
    )(c_all, ada_w, ada_b_cols)


def _ada_bwd(c_all, dmod_cols):
    L, Bg, C = dmod_cols.shape
    D = c_all.shape[1]
    tc = _tile(C, 512)

    def body(c_ref, d_ref, o_ref):
        ca = _silu(c_ref[...]).astype(BF16)
        o_ref[...] = _dot_tn(ca, d_ref[...].astype(BF16))

    return pl.pallas_call(
        body, name="ada_bwd", grid=(L, C // tc),
        in_specs=[pl.BlockSpec((Bg, D), lambda l, j: (0, 0)), pl.BlockSpec((None, Bg, tc), lambda l, j: (l, 0, j))],
        out_specs=pl.BlockSpec((None, D, tc), lambda l, j: (l, 0, j)),
        out_shape=jax.ShapeDtypeStruct((L, D, C), F32),
        compiler_params=_cparams(("parallel", "parallel")),
    )(c_all, dmod_cols)


def _adamw_update(w, gv, m, v):
    mn = ADAM_B1 * m + (1.0 - ADAM_B1) * gv
    vn = ADAM_B2 * v + (1.0 - ADAM_B2) * jnp.square(gv)
    m_hat = mn / (1.0 - ADAM_B1 ** ADAM_STEP)
    v_hat = vn / (1.0 - ADAM_B2 ** ADAM_STEP)
    return -ADAM_LR * (m_hat / (jnp.sqrt(v_hat) + ADAM_EPS) + ADAM_WD * w), mn, vn


def _adamw(w, g, m, v, *, name):
    shape = w.shape
    C = shape[-1]
    R = int(np.prod(shape[:-1])) if len(shape) > 1 else 1
    w2, g2, m2, v2 = (a.reshape(R, C) for a in (w, g, m, v))
    tr = _row_tile(R, C)

    def body(w_ref, g_ref, m_ref, v_ref, d_ref, nm_ref, nv_ref):
        d_ref[...], nm_ref[...], nv_ref[...] = _adamw_update(w_ref[...], g_ref[...], m_ref[...], v_ref[...])

    spec = pl.BlockSpec((tr, C), lambda i: (i, 0))
    out = pl.pallas_call(
        body, name=name, grid=(R // tr,), in_specs=[spec] * 4, out_specs=[spec] * 3,
        out_shape=[jax.ShapeDtypeStruct((R, C), F32)] * 3, compiler_params=_cparams(("parallel",)),
    )(w2, g2, m2, v2)
    return tuple(a.reshape(shape) for a in out)


def _adamw_halves(w, g_own, g_peer, m, v, c_idx, *, name):
    L, rows, C = w.shape
    R = rows // 2
    tr = _row_tile(R, C)

    def body(c_ref, w_ref, go_ref, gp_ref, m_ref, v_ref, g_ref, d_ref, nm_ref, nv_ref):
        gv = jnp.where(pl.program_id(1) == c_ref[0], go_ref[...], gp_ref[...])
        g_ref[...] = gv
        d_ref[...], nm_ref[...], nv_ref[...] = _adamw_update(w_ref[...], gv, m_ref[...], v_ref[...])

    full = pl.BlockSpec((None, None, tr, C), lambda l, hh, i, c_ref: (l, hh, i, 0))
    half = pl.BlockSpec((None, tr, C), lambda l, hh, i, c_ref: (l, i, 0))
    grid_spec = pltpu.PrefetchScalarGridSpec(
        num_scalar_prefetch=1, grid=(L, 2, R // tr), in_specs=[full, half, half, full, full], out_specs=[full] * 4)
    split = lambda a: a.reshape(L, 2, R, C)
    out = pl.pallas_call(
        body, name=name, grid_spec=grid_spec, out_shape=[jax.ShapeDtypeStruct((L, 2, R, C), F32)] * 4,
        compiler_params=_cparams(("parallel", "parallel", "parallel")),
    )(c_idx, split(w), g_own, g_peer, split(m), split(v))
    return tuple(a.reshape(w.shape) for a in out)


def _sum_gathered(dm8, sm8):
    n_dev, Bl, R, D = dm8.shape
    Rs = sm8.shape[1]

    def body(dm_ref, sm_ref, ob_ref, os_ref):
        acc_b = jnp.zeros((R, D), F32)
        acc_s = jnp.zeros((Rs, D), F32)
        for d in range(n_dev):
            for b in range(Bl):
                acc_b = acc_b + dm_ref[d, b]
            acc_s = acc_s + sm_ref[d]
        ob_ref[...] = acc_b
        os_ref[...] = acc_s

    return pl.pallas_call(
        body, name="sum_gathered",
        out_shape=[jax.ShapeDtypeStruct((R, D), F32), jax.ShapeDtypeStruct((Rs, D), F32)],
        compiler_params=_cparams(None),
    )(dm8, sm8)


N_DEV = 8
N_CHIP = 4
ANY = pl.BlockSpec(memory_space=pl.ANY)
HBM = pl.BlockSpec(memory_space=pltpu.HBM)
SEM = pl.BlockSpec(memory_space=pltpu.SEMAPHORE)
DATAFLOW = pltpu.SideEffectType.DATAFLOW_SIDE_EFFECTING


def _mesh_pos():
    return lax.axis_index("x"), lax.axis_index("y"), lax.axis_index("c")


def _all_gather8(block, *, name, in_vmem):
    R, W = block.shape

    def body(x_ref, out_ref, send_sems, recv_sems, local_sem):
        x, y, c = _mesh_pos()
        me, sibling = (x, y, c), (x, y, 1 - c)
        chips = [(1 - x, y), (x, 1 - y), (1 - x, 1 - y)]

        def slot(px, py, pc):
            return out_ref.at[4 * px + 2 * py + pc]

        def copy(k, blk, to, src=None):
            return pltpu.make_async_remote_copy(
                src_ref=slot(*blk) if src is None else src, dst_ref=slot(*blk),
                send_sem=send_sems.at[k], recv_sem=recv_sems.at[k], device_id=to, device_id_type=MESH_ID)

        mine = pltpu.make_async_copy(x_ref, slot(*me), local_sem)
        mine.start()
        first = [copy(0, me, sibling, src=x_ref)]
        first += [copy(1 + j, me, (*chip, c), src=x_ref) for j, chip in enumerate(chips)]
        for cp in first:
            cp.start()
        passed = [copy(4 + j, (*chip, c), sibling) for j, chip in enumerate(chips)]
        for j, chip in enumerate(chips):
            copy(1 + j, (*chip, c), me).wait_recv()
            passed[j].start()
        copy(0, sibling, me).wait_recv()
        for j, chip in enumerate(chips):
            copy(4 + j, (*chip, 1 - c), me).wait_recv()
        for cp in first + passed:
            cp.wait_send()
        mine.wait()

    space = pl.BlockSpec(memory_space=pltpu.VMEM) if in_vmem else ANY
    return pl.pallas_call(
        body, name=name, out_shape=jax.ShapeDtypeStruct((N_DEV, R, W), block.dtype),
        in_specs=[space], out_specs=space,
        scratch_shapes=[pltpu.SemaphoreType.DMA((7,)), pltpu.SemaphoreType.DMA((7,)), pltpu.SemaphoreType.DMA],
        compiler_params=pltpu.CompilerParams(vmem_limit_bytes=VMEM_LIMIT_V7X),
    )(block)


def _comm_call(body, arrays, out_shapes, n_sems, *, name):
    return pl.pallas_call(
        body, name=name, out_shape=out_shapes, in_specs=[ANY] * len(arrays), out_specs=[ANY] * len(out_shapes),
        scratch_shapes=[pltpu.SemaphoreType.DMA((n_sems,)), pltpu.SemaphoreType.DMA((n_sems,)),
                        pltpu.SemaphoreType.DMA((len(arrays),))],
    )(*arrays)


def _gather_weights(shards, *, name):
    n = len(shards)

    def body(*refs):
        xs, outs = refs[:n], refs[n:2 * n]
        send_sems, recv_sems, local_sems = refs[2 * n:]
        x, y, c = _mesh_pos()
        me, sibling = (x, y, c), (x, y, 1 - c)
        chips = [(1 - x, y), (x, 1 - y), (1 - x, 1 - y)]
        waits = []
        for i in range(n):
            nl = shards[i].shape[0]
            own = xs[i].at[pl.ds(0, nl), c]

            def slot(px, py, pc, i=i, nl=nl):
                return outs[i].at[pl.ds(0, nl), 2 * px + py, pc]

            def copy(k, blk, to, src=None, i=i, slot=slot):
                return pltpu.make_async_remote_copy(
                    src_ref=slot(*blk) if src is None else src, dst_ref=slot(*blk),
                    send_sem=send_sems.at[7 * i + k], recv_sem=recv_sems.at[7 * i + k], device_id=to,
                    device_id_type=MESH_ID)

            mine = pltpu.make_async_copy(own, slot(*me), local_sems.at[i])
            mine.start()
            first = [copy(0, me, sibling, src=own)]
            first += [copy(1 + j, me, (*chip, c), src=own) for j, chip in enumerate(chips)]
            for cp in first:
                cp.start()
            waits.append((copy, mine, first))
        for copy, mine, first in waits:
            passed = [copy(4 + j, (*chip, c), sibling) for j, chip in enumerate(chips)]
            for j, chip in enumerate(chips):
                copy(1 + j, (*chip, c), me).wait_recv()
                passed[j].start()
            copy(0, sibling, me).wait_recv()
            for j, chip in enumerate(chips):
                copy(4 + j, (*chip, 1 - c), me).wait_recv()
            for cp in first + passed:
                cp.wait_send()
            mine.wait()

    out_shapes = [jax.ShapeDtypeStruct((s.shape[0], N_CHIP) + s.shape[1:], s.dtype) for s in shards]
    return _comm_call(body, shards, out_shapes, 7 * n, name=name)


def _place_own(shard, chip_idx, c_idx, *, name):
    n, _, rows, cols = shard.shape
    tr = _row_tile(rows, cols)

    def body(k_ref, c_ref, x_ref, o_ref):
        o_ref[...] = x_ref[...]

    grid_spec = pltpu.PrefetchScalarGridSpec(
        num_scalar_prefetch=2, grid=(n, rows // tr),
        in_specs=[pl.BlockSpec((None, None, tr, cols), lambda l, i, k_ref, c_ref: (l, c_ref[0], i, 0))],
        out_specs=pl.BlockSpec((None, None, None, tr, cols), lambda l, i, k_ref, c_ref: (l, k_ref[0], c_ref[0], i, 0)))
    return pl.pallas_call(
        body, name=name, grid_spec=grid_spec,
        out_shape=jax.ShapeDtypeStruct((n, N_CHIP, 2, rows, cols), shard.dtype),
        compiler_params=_cparams(("parallel", "parallel")),
    )(chip_idx, c_idx, shard)


def _gather_copies(x_refs, land_refs, send_sems, recv_sems):
    x, y, c = _mesh_pos()
    k_me = 2 * x + y
    targets = [(x, y, 1 - c), (1 - x, y, c), (x, 1 - y, c), (1 - x, 1 - y, c)]
    copies = []
    for i, (x_ref, land_ref) in enumerate(zip(x_refs, land_refs)):
        nl = x_ref.shape[0]
        for j, to in enumerate(targets):
            copies.append(pltpu.make_async_remote_copy(
                src_ref=x_ref.at[pl.ds(0, nl), c], dst_ref=land_ref.at[pl.ds(0, nl), k_me, c],
                send_sem=send_sems.at[4 * i + j], recv_sem=recv_sems.at[4 * i + j], device_id=to,
                device_id_type=MESH_ID))
    return copies


def _split_start(copies_fn, srcs, lands, after, *, name, sems_per_array):
    n = len(srcs)

    def body(*refs):
        send_sems, recv_sems = refs[2 * n + 1], refs[2 * n + 2]
        for cp in copies_fn(refs[:n], refs[n:2 * n], send_sems, recv_sems):
            cp.start()
        refs[-1][...] = jnp.zeros_like(refs[-1])

    operands = [pltpu.with_memory_space_constraint(a, pltpu.HBM) for a in list(srcs) + list(lands)]
    n_sems = sems_per_array * n
    out_shape = ([pltpu.SemaphoreType.DMA((n_sems,)), pltpu.SemaphoreType.DMA((n_sems,))]
                 + [pltpu.HBM(a.shape, a.dtype) for a in operands] + [jax.ShapeDtypeStruct((8, LANES), F32)])
    res = pl.pallas_call(
        body, name=name, out_shape=out_shape, in_specs=[HBM] * (2 * n) + [ANY],
        out_specs=[SEM, SEM] + [HBM] * (2 * n) + [pl.BlockSpec(memory_space=pltpu.VMEM)],
        input_output_aliases={i: 2 + i for i in range(2 * n)},
        compiler_params=pltpu.CompilerParams(has_side_effects=DATAFLOW),
    )(*operands, after)
    return res[0], res[1], list(res[2:2 + n]), list(res[2 + n:2 + 2 * n]), res[-1]


def _split_wait(copies_fn, send_sems, recv_sems, srcs, lands, after, *, name):
    n = len(srcs)

    def body(*refs):
        for cp in copies_fn(refs[:n], refs[n:2 * n], refs[2 * n], refs[2 * n + 1]):
            cp.wait_send()
            cp.wait_recv()

    res = pl.pallas_call(
        body, name=name, out_shape=[pltpu.HBM(a.shape, a.dtype) for a in list(srcs) + list(lands)],
        in_specs=[HBM] * (2 * n) + [SEM, SEM, ANY], out_specs=[HBM] * (2 * n),
        input_output_aliases={i: i for i in range(2 * n)},
        compiler_params=pltpu.CompilerParams(has_side_effects=DATAFLOW),
    )(*srcs, *lands, send_sems, recv_sems, after)
    return list(res[:n]), list(res[n:])


def _gather_forward(lands, *, name):
    n = len(lands)

    def body(*refs):
        xs = refs[:n]
        send_sems, recv_sems, _ = refs[2 * n:]
        x, y, c = _mesh_pos()
        chips = [(1 - x, y), (x, 1 - y), (1 - x, 1 - y)]
        copies = []
        for i in range(n):
            nl = lands[i].shape[0]
            for j, (cx, cy) in enumerate(chips):
                here = xs[i].at[pl.ds(0, nl), 2 * cx + cy, c]
                cp = pltpu.make_async_remote_copy(
                    src_ref=here, dst_ref=here, send_sem=send_sems.at[3 * i + j], recv_sem=recv_sems.at[3 * i + j],
                    device_id=(x, y, 1 - c), device_id_type=MESH_ID)
                cp.start()
                copies.append(cp)
        for cp in copies:
            cp.wait()

    return pl.pallas_call(
        body, name=name, out_shape=[jax.ShapeDtypeStruct(a.shape, a.dtype) for a in lands],
        in_specs=[ANY] * n, out_specs=[ANY] * n, input_output_aliases={i: i for i in range(n)},
        scratch_shapes=[pltpu.SemaphoreType.DMA((3 * n,)), pltpu.SemaphoreType.DMA((3 * n,)),
                        pltpu.SemaphoreType.DMA((1,))],
    )(*lands)


def _pair_copies(g_refs, land_refs, send_sems, recv_sems):
    x, y, c = _mesh_pos()
    copies = []
    for i, (g_ref, land_ref) in enumerate(zip(g_refs, land_refs)):
        nl, ns = g_ref.shape[:2]
        copies.append(pltpu.make_async_remote_copy(
            src_ref=g_ref.at[pl.ds(0, nl), pl.ds(0, ns), 1 - c], dst_ref=land_ref, send_sem=send_sems.at[i],
            recv_sem=recv_sems.at[i], device_id=(x, y, 1 - c), device_id_type=MESH_ID))
    return copies


def _pair_exchange(gs, *, name):
    n = len(gs)

    def body(*refs):
        send_sems, recv_sems, _ = refs[2 * n:]
        copies = _pair_copies(refs[:n], refs[n:2 * n], send_sems, recv_sems)
        for cp in copies:
            cp.start()
        for cp in copies:
            cp.wait()

    out_shapes = [jax.ShapeDtypeStruct(g.shape[:2] + g.shape[3:], g.dtype) for g in gs]
    return _comm_call(body, gs, out_shapes, n, name=name)


def _chip_copies(p_refs, land_refs, send_sems, recv_sems):
    x, y, c = _mesh_pos()
    k_me = 2 * x + y
    chips = [(1 - x, y), (x, 1 - y), (1 - x, 1 - y)]
    copies = []
    for i, (p_ref, land_ref) in enumerate(zip(p_refs, land_refs)):
        nl = p_ref.shape[0]
        for j, (cx, cy) in enumerate(chips):
            copies.append(pltpu.make_async_remote_copy(
                src_ref=p_ref.at[pl.ds(0, nl), 2 * cx + cy], dst_ref=land_ref.at[k_me],
                send_sem=send_sems.at[3 * i + j], recv_sem=recv_sems.at[3 * i + j],
                device_id=(cx, cy, c), device_id_type=MESH_ID))
    return copies


def _chip_landing(ps):
    return [lax.empty((p.shape[1], p.shape[0]) + p.shape[2:], p.dtype) for p in ps]


def _chip_exchange(ps, *, name):
    n = len(ps)

    def body(*refs):
        send_sems, recv_sems, _ = refs[2 * n:]
        copies = _chip_copies(refs[:n], refs[n:2 * n], send_sems, recv_sems)
        for cp in copies:
            cp.start()
        for cp in copies:
            cp.wait()

    out_shapes = [jax.ShapeDtypeStruct((p.shape[1], p.shape[0]) + p.shape[2:], p.dtype) for p in ps]
    return _comm_call(body, ps, out_shapes, 3 * n, name=name)


def _pair_swap(ss, *, name):
    n = len(ss)

    def body(*refs):
        xs, outs = refs[:n], refs[n:2 * n]
        send_sems, recv_sems, _ = refs[2 * n:]
        x, y, c = _mesh_pos()
        copies = []
        for i in range(n):
            cp = pltpu.make_async_remote_copy(src_ref=xs[i], dst_ref=outs[i], send_sem=send_sems.at[i],
                                              recv_sem=recv_sems.at[i], device_id=(x, y, 1 - c),
                                              device_id_type=MESH_ID)
            cp.start()
            copies.append(cp)
        for cp in copies:
            cp.wait()

    out_shapes = [jax.ShapeDtypeStruct(s.shape, s.dtype) for s in ss]
    return _comm_call(body, ss, out_shapes, n, name=name)


def _row_tile(rows, cols):
    tr = rows
    while tr * cols > 256 * 1024 and tr % 16 == 0:
        tr //= 2
    return tr


def _pair_add(g, recv, c_idx, *, name):
    n, ns, _, rows, W = g.shape
    tr = _row_tile(rows, W)

    def body(c_ref, g_ref, r_ref, o_ref):
        o_ref[...] = (g_ref[...] + r_ref[...]).astype(BF16)

    piece = pl.BlockSpec((None, tr, W), lambda p, i, c_ref: (p, i, 0))
    grid_spec = pltpu.PrefetchScalarGridSpec(
        num_scalar_prefetch=1, grid=(n * ns, rows // tr),
        in_specs=[pl.BlockSpec((None, None, tr, W), lambda p, i, c_ref: (p, c_ref[0], i, 0)), piece],
        out_specs=piece)
    out = pl.pallas_call(
        body, name=name, grid_spec=grid_spec, out_shape=jax.ShapeDtypeStruct((n * ns, rows, W), BF16),
        compiler_params=_cparams(("parallel", "parallel")),
    )(c_idx, g.reshape(n * ns, 2, rows, W), recv.reshape(n * ns, rows, W))
    return out.reshape(n, ns, rows, W)


def _sum_pieces(land, own, chip_idx, *, name):
    n, nl, A, W = land.shape
    tr = _row_tile(A, W)

    def body(k_ref, l_ref, o_ref, out_ref):
        acc = jnp.zeros(out_ref.shape, F32)
        for k in range(n):
            acc = acc + jnp.where(k == k_ref[0], o_ref[...], l_ref[k]).astype(F32)
        out_ref[...] = acc

    grid_spec = pltpu.PrefetchScalarGridSpec(
        num_scalar_prefetch=1, grid=(nl, A // tr),
        in_specs=[pl.BlockSpec((n, None, tr, W), lambda l, i, k_ref: (0, l, i, 0)),
                  pl.BlockSpec((None, None, tr, W), lambda l, i, k_ref: (l, k_ref[0], i, 0))],
        out_specs=pl.BlockSpec((None, tr, W), lambda l, i, k_ref: (l, i, 0)))
    return pl.pallas_call(
        body, name=name, grid_spec=grid_spec, out_shape=jax.ShapeDtypeStruct((nl, A, W), F32),
        compiler_params=_cparams(("parallel", "parallel")),
    )(chip_idx, land, own)


SMALL = ("norm_mix_g", "norm_mlp_g", "final_norm_g", "fox_b_f", "mla_q_norm_g", "mla_kv_norm_g")
WEIGHT_ORDER = ("ada_w", "ada_b", "norm_mix_g", "norm_mlp_g", "fox_w_in", "fox_b_f", "fox_w_out", "mla_w_dq",
                "mla_q_norm_g", "mla_w_uq", "mla_w_dkv", "mla_kv_norm_g", "mla_w_ukv", "mla_w_out", "mlp_w1",
                "mlp_w2", "final_norm_g")


def _small_rows(vals, D):
    rows = [vals["norm_mix_g"], vals["norm_mlp_g"], vals["final_norm_g"][None, :]]
    for n in ("fox_b_f", "mla_q_norm_g", "mla_kv_norm_g"):
        flat = vals[n].reshape(-1)
        assert flat.shape[0] <= D
        rows.append(jnp.pad(flat, (0, D - flat.shape[0]))[None, :])
    return jnp.concatenate(rows, axis=0)


def _small_unrows(rows, shapes):
    L = shapes["norm_mix_g"][0]
    out = {"norm_mix_g": rows[0:L], "norm_mlp_g": rows[L:2 * L], "final_norm_g": rows[2 * L]}
    for k, n in enumerate(("fox_b_f", "mla_q_norm_g", "mla_kv_norm_g")):
        size = int(np.prod(shapes[n]))
        out[n] = rows[2 * L + 1 + k, :size].reshape(shapes[n])
    return out


def kernel(x, c, positions, ada_w, ada_b, norm_mix_g, norm_mlp_g, fox_w_in, fox_b_f, fox_w_out, mla_w_dq, mla_q_norm_g, mla_w_uq, mla_w_dkv, mla_kv_norm_g, mla_w_ukv, mla_w_out, mlp_w1, mlp_w2, final_norm_g, loss_target, m_ada_w, m_ada_b, m_norm_mix_g, m_norm_mlp_g, m_fox_w_in, m_fox_b_f, m_fox_w_out, m_mla_w_dq, m_mla_q_norm_g, m_mla_w_uq, m_mla_w_dkv, m_mla_kv_norm_g, m_mla_w_ukv, m_mla_w_out, m_mlp_w1, m_mlp_w2, m_final_norm_g, v_ada_w, v_ada_b, v_norm_mix_g, v_norm_mlp_g, v_fox_w_in, v_fox_b_f, v_fox_w_out, v_mla_w_dq, v_mla_q_norm_g, v_mla_w_uq, v_mla_w_dkv, v_mla_kv_norm_g, v_mla_w_ukv, v_mla_w_out, v_mlp_w1, v_mlp_w2, v_final_norm_g):
    args = dict(locals())
    wts = {n: args[n] for n in WEIGHT_ORDER}
    mom = {n: args["m_" + n] for n in WEIGHT_ORDER}
    var = {n: args["v_" + n] for n in WEIGHT_ORDER}
    Bl, S, D = x.shape
    T = Bl * S
    L = ada_w.shape[0]
    C = ada_w.shape[2]
    mx, my, mc = _mesh_pos()
    chip = 2 * mx + my
    dev = 4 * mx + 2 * my + mc
    c_idx = jnp.reshape(mc, (1,)).astype(jnp.int32)
    chip_idx = jnp.reshape(chip, (1,)).astype(jnp.int32)
    small = {n: wts[n] for n in SMALL}
    L2, q_cols = mla_q_norm_g.shape
    n_fox_heads = fox_b_f.shape[1]

    shards = _shard_layouts(wts)
    groups = _comm_groups(L, L2)
    slots = _layer_slots(groups)

    def row_halves(a):
        return a.reshape(a.shape[:-2] + (2, a.shape[-2] // 2, a.shape[-1]))

    def whole_rows(a):
        return a.reshape(a.shape[:2] + (a.shape[2] * a.shape[3], a.shape[4]))

    part = {g: [row_halves(shards[n][s:s + cnt]) for n, s, cnt in entries] for g, entries in groups.items()}
    mix0 = _gather_weights(part["mix0"], name="gather_mix0")
    gather_sems, after = {}, mix0[0]
    for group in ("mlp0", "rest"):
        placed = [_place_own(a, chip_idx, c_idx, name=f"gather_place_{group}_{n}")
                  for a, (n, _, _) in zip(part[group], groups[group])]
        gather_sems[group] = _split_start(_gather_copies, part[group], placed, after, name=f"gather_{group}_start",
                                          sems_per_array=4)
        after = gather_sems[group][4]

    def layer_weights(w, group, arrays):
        for (n, s, cnt), a in zip(groups[group], arrays):
            for key, view in _weight_views(n, whole_rows(a), D, n_fox_heads).items():
                for l in range(cnt):
                    w[key][s + l] = (view, l)

    w = {key: [None] * L2 for key in ("fox_qkv", "fox_f", "fox_out", "mla_down", "mla_uq", "mla_ukv", "mla_out")}
    w.update({key: [None] * L for key in ("mlp_w1", "mlp_w2")})
    layer_weights(w, "mix0", mix0)

    def gathered_now(group):
        def hook(x_now, w):
            _, landed = _split_wait(_gather_copies, *gather_sems[group][:4], x_now, name=f"gather_{group}_wait")
            layer_weights(w, group, _gather_forward(landed, name=f"gather_{group}_forward"))
            return w
        return hook

    c_pad = jnp.concatenate([c, jnp.pad(mla_q_norm_g, ((0, 8 - Bl - L2), (0, D - q_cols)))], axis=0)
    c8 = _all_gather8(c_pad, name="gather_c", in_vmem=True)
    c_all = c8[:, :Bl].reshape(N_DEV * Bl, D)
    qg4 = c8.reshape(N_CHIP, 2, 8, D)[:, 0, Bl:Bl + L2, :q_cols]
    small["mla_q_norm_g"] = jnp.transpose(qg4, (1, 0, 2)).reshape(L2, N_CHIP * q_cols)
    ada_b_cols = lax.dynamic_slice_in_dim(ada_b, chip * C, C, axis=1)[:, None, :]
    mod_cols = _ada_fwd(c_all, ada_w, ada_b_cols)
    mod8 = _all_gather8(mod_cols.reshape(L * N_DEV * Bl, C), name="gather_mod", in_vmem=True)
    mod4 = mod8.reshape(N_CHIP, 2, L, N_DEV * Bl, C)[:, 0]
    mod_me = lax.dynamic_slice_in_dim(mod4, dev * Bl, Bl, axis=2)
    mod = jnp.transpose(mod_me, (1, 2, 0, 3)).reshape(L, Bl, 6, D)
    mod = jnp.transpose(mod, (0, 2, 1, 3))[:, :, :, None, :]

    w.update(_small_layouts(small))
    mod = mod + after[0, 0]
    pending = {}

    def grad_pieces(group, g_now):
        out = []
        for n, s, cnt in groups[group]:
            qkv_f = [(g_now["fox_qkv"][j], g_now["fox_f"][j]) for j in range(s, s + cnt)] if n == "fox_in" else None
            stacked_g = None if n == "fox_in" else g_now[n][group]
            out.append(row_halves(_grad_pieces(n, stacked_g, qkv_f, n_fox_heads, N_CHIP)))
        return out

    def pair_added(group, big, sibling):
        return [_pair_add(a, r, c_idx, name=f"grad_pair_add_{group}_{n}")
                for (n, _, _), a, r in zip(groups[group], big, sibling)]

    def exchange_start(group, ps, after=None):
        pending[group] = _split_start(_chip_copies, ps, _chip_landing(ps), chip_idx if after is None else after,
                                      name=f"grad_exchange_{group}_start", sems_per_array=3)
        return pending[group][4]

    def bwd_layer0(g_now):
        big = grad_pieces("rest", g_now)
        landing = [lax.empty(a.shape[:2] + a.shape[3:], a.dtype) for a in big]
        pending["rest_pair"] = _split_start(_pair_copies, big, landing, chip_idx, name="grad_pair_rest_start",
                                            sems_per_array=1)
        return pending["rest_pair"][4]

    def bwd_mix0(g_now):
        send_sems, recv_sems, big, landed, _ = pending["rest_pair"]
        big, landed = _split_wait(_pair_copies, send_sems, recv_sems, big, landed, g_now["mlp_w1"]["mlp0"],
                                  name="grad_pair_rest_wait")
        started = exchange_start("rest", pair_added("rest", big, landed))
        big = grad_pieces("mlp0", g_now)
        return exchange_start("mlp0", pair_added("mlp0", big, _pair_exchange(big, name="grad_pair_exchange_mlp0")),
                              after=started)

    half = ROPE_DIM // 2
    inv_freq = ROPE_THETA ** (-jnp.arange(0, ROPE_DIM, 2, dtype=F32) / ROPE_DIM)
    lane = np.arange(LANES)
    inv_freq_row = jnp.tile(inv_freq, LANES // half)[None, :]
    sign_row = jnp.asarray(np.where(lane < 2 * ROPE_DIM, np.where(lane % ROPE_DIM < half, -1.0, 1.0), 0.0), F32)[None, :]
    pos_f = positions.astype(F32).reshape(T, 1)
    loss_row, grad_x, dmod, g = _local_step(x.reshape(T, D), loss_target.reshape(T, D), pos_f, inv_freq_row, sign_row,
                                            mod, w, slots, S=S,
                                            hooks={"fwd_mlp0": gathered_now("mlp0"), "fwd_layer1": gathered_now("rest"),
                                                   "bwd_layer0": bwd_layer0, "bwd_mix0": bwd_mix0})
    g_small = _small_grads(g, n_fox_heads)
    big = grad_pieces("mix0", g)
    exchange_start("mix0", pair_added("mix0", big, _pair_exchange(big, name="grad_pair_exchange_mix0")))

    Rs = -(-(2 * L + 5) // 8) * 8
    srows = jnp.concatenate([_small_rows(g_small, D), jnp.pad(loss_row, ((0, 0), (0, D - LANES)))], axis=0)
    srows = jnp.pad(srows, ((0, Rs - srows.shape[0]), (0, 0)))
    drows = jnp.transpose(dmod[:, :, :, 0, :], (2, 0, 1, 3)).reshape(Bl * L * 6, D)
    both8 = _all_gather8(jnp.concatenate([drows, srows], axis=0), name="gather_small", in_vmem=True)
    dm8 = both8[:, :Bl * L * 6].reshape(N_DEV, Bl, L * 6, D)
    sm8 = both8[:, Bl * L * 6:]
    adb_rows, small_sum = _sum_gathered(dm8, sm8)
    grad_ada_b = adb_rows.reshape(L, 6 * D)
    loss = small_sum[2 * L + 4, 0]
    small_shapes = {n: (wts[n].shape if n != "mla_q_norm_g" else (wts[n].shape[0], N_CHIP * q_cols)) for n in SMALL}
    gs = _small_unrows(small_sum, small_shapes)
    gs["mla_q_norm_g"] = lax.dynamic_slice_in_dim(gs["mla_q_norm_g"], chip * q_cols, q_cols, axis=1)

    dmod16 = jnp.transpose(dm8.reshape(N_DEV, Bl, L, 6 * D), (2, 0, 1, 3)).reshape(L, N_DEV * Bl, 6 * D)
    dmod_cols = lax.dynamic_slice_in_dim(dmod16, chip * C, C, axis=2)
    grad_ada_w = _ada_bwd(c_all, dmod_cols)

    grads = dict(gs)
    grads["ada_w"] = grad_ada_w
    grads["ada_b"] = grad_ada_b
    delta, new_m, new_v = {}, {}, {}
    for n in ("ada_w", "ada_b"):
        delta[n], new_m[n], new_v[n] = _adamw(wts[n], grads[n], mom[n], var[n], name=f"adamw_{n}")
    shard_small_shapes = {n: wts[n].shape for n in SMALL}
    packs = [jnp.pad(_small_rows({n: src[n] for n in SMALL}, D), ((0, Rs - 2 * L - 4), (0, 0)))
             for src in (wts, grads, mom, var)]
    for dst, rows in zip((delta, new_m, new_v), _adamw(*packs, name="adamw_small")):
        dst.update(_small_unrows(rows, shard_small_shapes))

    halves = {}
    for group, after in (("rest", grad_x), ("mlp0", grad_x), ("mix0", delta["ada_w"])):
        send_sems, recv_sems, ps, lands, _ = pending[group]
        ps, lands = _split_wait(_chip_copies, send_sems, recv_sems, ps, lands, after, name=f"grad_exchange_{group}_wait")
        sums = [_sum_pieces(ld, p, chip_idx, name=f"grad_sum_{group}_{n}")
                for (n, _, _), ld, p in zip(groups[group], lands, ps)]
        swapped = _pair_swap(sums, name=f"grad_pair_swap_{group}")
        for (n, _, _), a, b in zip(groups[group], sums, swapped):
            halves[(n, group)] = (a, b)

    def all_layers(n, which):
        return jnp.concatenate([halves[(n, grp)][which] for grp in groups if (n, grp) in halves], axis=0)

    own = {n: all_layers(n, 0) for n in GATHERED}
    peer = {n: all_layers(n, 1) for n in GATHERED}
    for nat, n in (("fox_w_in", "fox_in"), ("fox_w_out", "fox_out"), ("mla_w_out", "mla_out"), ("mlp_w1", "mlp_w1"),
                   ("mlp_w2", "mlp_w2")):
        cols = wts[nat].shape[-1]
        res = _adamw_halves(_pad_lanes(wts[nat]), own[n], peer[n], _pad_lanes(mom[nat]), _pad_lanes(var[nat]), c_idx,
                            name=f"adamw_{nat}")
        grads[nat], delta[nat], new_m[nat], new_v[nat] = (a[..., :cols] for a in res)
    joined = {n: jnp.concatenate([jnp.where(mc == 0, own[n], peer[n]), jnp.where(mc == 0, peer[n], own[n])], axis=1)
              for n in ("mla_down", "mla_uq", "mla_ukv")}
    rq = mla_w_dq.shape[-1]
    grads["mla_w_dq"] = joined["mla_down"][:, :, :rq]
    grads["mla_w_dkv"] = joined["mla_down"][:, :, rq:rq + KV_RANK + ROPE_DIM]
    grads["mla_w_uq"] = jax.vmap(_uq_from_pairs)(joined["mla_uq"])
    grads["mla_w_ukv"] = jax.vmap(_ukv_from_pairs)(joined["mla_ukv"])
    for n in ("mla_w_dq", "mla_w_dkv", "mla_w_uq", "mla_w_ukv"):
        delta[n], new_m[n], new_v[n] = _adamw(wts[n], grads[n], mom[n], var[n], name=f"adamw_{n}")

    return (loss, grad_x.reshape(Bl, S, D), *[grads[n] for n in WEIGHT_ORDER], *[delta[n] for n in WEIGHT_ORDER],
            *[new_m[n] for n in WEIGHT_ORDER], *[new_v[n] for n in WEIGHT_ORDER])
```

```python
import functools

import numpy as np
import jax
import jax.numpy as jnp
from jax import lax
from jax.experimental import pallas as pl
from jax.experimental.pallas import tpu as pltpu

F32 = jnp.float32
BF16 = jnp.bfloat16
MESH_ID = pl.DeviceIdType.MESH

NORM_EPS = 1e-6
ROPE_THETA = 10000.0
HEAD_DIM = 64
ROPE_DIM = 32
KV_RANK = 128
FOX_EXTRA = 6
PAIR_Q = 256
PAIR_KV = 384
LANES = 128
ADAM_LR = 0.001
ADAM_B1 = 0.9
ADAM_B2 = 0.999
ADAM_EPS = 1e-08
ADAM_WD = 0.01
ADAM_STEP = 10
VMEM_LIMIT_V7X = 48 * 1024 * 1024
MM_VMEM_BUDGET = 36 * 1024 * 1024
NEG_BIG = -1e30
ATTN_UNROLL = 4
ATTN_BLOCK = 256
ATTN_Q_ROWS = 512
ATTN_K_ROWS = 512

BIG_WEIGHTS = (("fox_w_in", 2), ("fox_w_out", 1), ("mla_w_dq", 1), ("mla_w_uq", 2), ("mla_w_dkv", 1),
               ("mla_w_ukv", 2), ("mla_w_out", 1), ("mlp_w1", 2), ("mlp_w2", 1))


def _cparams(sem=None):
    return pltpu.CompilerParams(dimension_semantics=sem, vmem_limit_bytes=VMEM_LIMIT_V7X)


def _tile(n, want):
    if n <= want:
        return n
    for t in range(want - want % LANES, 0, -LANES):
        if n % t == 0:
            return t
    raise ValueError((n, want))


def _mm(a, b, mode, *, name, out_dtypes=(F32,), epilogue=None, extras=(), rowvecs=(), tables=(),
        seq=None, a_off=0, a_sz=None, b_layer=None, out_stack=None, out_split=0, out_t=(), tm=1024, tn=1024,
        tk=2048):
    if isinstance(b, (list, tuple)):
        b, b_layer = b[b_layer]
    b_rows, b_cols = b.shape[-2], b.shape[-1]
    n_split = b.shape[1] if b.ndim == 4 else 1
    assert mode in ("nn", "nt")
    if mode == "nn":
        M, K, N = a.shape[0], b_rows, b_cols * n_split
    else:
        M, K, N = a.shape[0], b_cols * n_split, b_rows
    assert a_sz is None or a_sz == K
    tm = _tile(seq if rowvecs else M, tm)
    n_piece = N // max(out_split, n_split if mode == "nn" else 1, 1)
    tn = _tile(n_piece, tn)
    tk = _tile(K // (n_split if mode == "nt" else 1), tk)
    ne, nr, nt_ = len(extras), len(rowvecs), len(tables)
    no = len(out_dtypes)

    def vmem_estimate():
        blocks = tm * tk * a.dtype.itemsize + tk * tn * b.dtype.itemsize
        blocks += tm * tn * (sum(e.dtype.itemsize for e in extras) + sum(jnp.dtype(d).itemsize for d in out_dtypes))
        return 2 * blocks + 2 * tm * tn * 4

    while vmem_estimate() > MM_VMEM_BUDGET and max(tm, tn) > 256:
        if tn >= tm:
            tn //= 2
        else:
            tm //= 2
    nk = K // tk

    assert a_off % tk == 0
    a_spec = pl.BlockSpec((tm, tk), lambda i, j, k: (i, k + a_off // tk))
    dims = (((1,), (0,)), ((), ())) if mode == "nn" else (((1,), (1,)), ((), ()))
    lead = () if b.ndim == 2 else (b_layer,)
    sq = (None,) * (b.ndim - 2)
    if mode == "nt":
        kb = b_cols // tk
        if b.ndim == 4:
            b_spec = pl.BlockSpec(sq + (tn, tk), lambda i, j, k: lead + (k // kb, j, k % kb))
        else:
            b_spec = pl.BlockSpec(sq + (tn, tk), lambda i, j, k: lead + (j, k))
    else:
        nb = b_cols // tn
        if b.ndim == 4:
            b_spec = pl.BlockSpec(sq + (tk, tn), lambda i, j, k: lead + (j // nb, k, j % nb))
        else:
            b_spec = pl.BlockSpec(sq + (tk, tn), lambda i, j, k: lead + (k, j))
    in_specs = [a_spec, b_spec]
    in_specs += [pl.BlockSpec((tm, tn), lambda i, j, k: (i, j)) for _ in extras]
    if rowvecs:
        assert seq % tm == 0
        per = seq // tm
        in_specs += [pl.BlockSpec((None, 1, tn), lambda i, j, k: (i // per, 0, j)) for _ in rowvecs]
    in_specs += [pl.BlockSpec((tm, LANES), lambda i, j, k: (i, 0)) for _ in tables]
    operands = [a, b, *extras, *rowvecs, *tables]
    aliases = {}
    transposed = tuple(out_t) + (False,) * (no - len(out_t))
    if out_stack is None:
        out_specs = [pl.BlockSpec((tn, tm), lambda i, j, k: (j, i)) if t else pl.BlockSpec((tm, tn), lambda i, j, k: (i, j))
                     for t in transposed]
        out_shape = [jax.ShapeDtypeStruct((N, M) if t else (M, N), d) for d, t in zip(out_dtypes, transposed)]
    else:
        prev, layer, n_layers = out_stack
        assert no == 1
        if out_split:
            ob = n_piece // tn
            out_specs = [pl.BlockSpec((None, None, tm, tn), lambda i, j, k: (layer, j // ob, i, j % ob))]
            out_shape = [jax.ShapeDtypeStruct((n_layers, out_split, M, n_piece), out_dtypes[0])]
        else:
            out_specs = [pl.BlockSpec((None, tm, tn), lambda i, j, k: (layer, i, j))]
            out_shape = [jax.ShapeDtypeStruct((n_layers, M, N), out_dtypes[0])]
        if prev is not None:
            in_specs.append(pl.BlockSpec(memory_space=pl.ANY))
            aliases = {len(operands): 0}
            operands.append(prev)
    n_in = len(operands)

    def body(*refs):
        a_ref, b_ref = refs[0], refs[1]
        side = refs[2:2 + ne + nr + nt_]
        outs = refs[n_in:n_in + no]

        def finish(acc):
            res = (acc,) if epilogue is None else epilogue(acc, *[r[...] for r in side])
            for o_ref, r, t in zip(outs, res, transposed):
                o_ref[...] = (r.T if t else r).astype(o_ref.dtype)

        part = lax.dot_general(a_ref[...].astype(BF16), b_ref[...].astype(BF16), dims,
                               preferred_element_type=F32)
        if nk == 1:
            finish(part)
        else:
            acc_ref = refs[-1]
            k = pl.program_id(2)

            @pl.when(k == 0)
            def _():
                acc_ref[...] = part

            @pl.when(k > 0)
            def _():
                acc_ref[...] += part

            @pl.when(k == nk - 1)
            def _():
                finish(acc_ref[...])

    res = pl.pallas_call(
        body, name=name, grid=(M // tm, N // tn, nk), in_specs=in_specs, out_specs=out_specs,
        out_shape=out_shape, scratch_shapes=[pltpu.VMEM((tm, tn), F32)] if nk > 1 else [],
        input_output_aliases=aliases,
        compiler_params=_cparams(("parallel", "parallel", "arbitrary")),
    )(*operands)
    return res[0] if no == 1 else tuple(res)


def _rope128(x, cos_t, sin_s):
    lane = lax.broadcasted_iota(jnp.int32, x.shape, 1)
    first = (lane % ROPE_DIM) < (ROPE_DIM // 2)
    swapped = jnp.where(first, pltpu.roll(x, LANES - ROPE_DIM // 2, 1), pltpu.roll(x, ROPE_DIM // 2, 1))
    return x * cos_t + swapped * sin_s


def _rope_pairs(acc, cos_t, sin_s, sign):
    parts = []
    for p in range(acc.shape[1] // PAIR_Q):
        parts.append(acc[:, p * PAIR_Q:p * PAIR_Q + LANES])
        parts.append(_rope128(acc[:, p * PAIR_Q + LANES:(p + 1) * PAIR_Q], cos_t, sign * sin_s))
    return jnp.concatenate(parts, axis=1)


def _rope_tables(pos_f, inv_freq_row, sign_row):
    T = pos_f.shape[0]
    tt = _tile(T, 512)

    def body(p_ref, f_ref, s_ref, cos_ref, sin_ref):
        ang = p_ref[...] * f_ref[...]
        cos_ref[...] = jnp.cos(ang)
        sin_ref[...] = jnp.sin(ang) * s_ref[...]

    return pl.pallas_call(
        body, name="rope_tables", grid=(T // tt,),
        in_specs=[pl.BlockSpec((tt, 1), lambda i: (i, 0)), pl.BlockSpec((1, LANES), lambda i: (0, 0)),
                  pl.BlockSpec((1, LANES), lambda i: (0, 0))],
        out_specs=[pl.BlockSpec((tt, LANES), lambda i: (i, 0))] * 2,
        out_shape=[jax.ShapeDtypeStruct((T, LANES), F32)] * 2,
        compiler_params=_cparams(("parallel",)),
    )(pos_f, inv_freq_row, sign_row)


def _unrope(dqx, cos_t, sin_s):
    T, W = dqx.shape
    tt = _tile(T, 512)

    def body(d_ref, c_ref, s_ref, o_ref):
        o_ref[...] = _rope_pairs(d_ref[...].astype(F32), c_ref[...], s_ref[...], -1.0).astype(BF16)

    return pl.pallas_call(
        body, name="mla_unrope", grid=(T // tt,),
        in_specs=[pl.BlockSpec((tt, W), lambda i: (i, 0)), pl.BlockSpec((tt, LANES), lambda i: (i, 0)),
                  pl.BlockSpec((tt, LANES), lambda i: (i, 0))],
        out_specs=pl.BlockSpec((tt, W), lambda i: (i, 0)),
        out_shape=jax.ShapeDtypeStruct((T, W), BF16),
        compiler_params=_cparams(("parallel",)),
    )(dqx, cos_t, sin_s)


def _row_specs(tt, D, per, n):
    return [pl.BlockSpec((None, 1, D), lambda i: (i // per, 0, 0)) for _ in range(n)]


def _norm_mod(x, gain, sc, sh, *, S, name):
    T, D = x.shape
    tt = _tile(S, 512)
    per = S // tt

    def body(x_ref, g_ref, sc_ref, sh_ref, h_ref, ht_ref):
        xv = x_ref[...]
        r = lax.rsqrt(jnp.mean(xv * xv, axis=-1, keepdims=True) + NORM_EPS)
        h = (xv * r) * g_ref[...] * (1.0 + sc_ref[...]) + sh_ref[...]
        h_ref[...] = h.astype(BF16)
        ht_ref[...] = h.T.astype(BF16)

    return pl.pallas_call(
        body, name=name, grid=(T // tt,),
        in_specs=[pl.BlockSpec((tt, D), lambda i: (i, 0)), pl.BlockSpec((1, D), lambda i: (0, 0))]
        + _row_specs(tt, D, per, 2),
        out_specs=[pl.BlockSpec((tt, D), lambda i: (i, 0)), pl.BlockSpec((D, tt), lambda i: (0, i))],
        out_shape=[jax.ShapeDtypeStruct((T, D), BF16), jax.ShapeDtypeStruct((D, T), BF16)],
        compiler_params=_cparams(("parallel",)),
    )(x, gain, sc, sh)


def _norm_mod_bwd(x, dh, dres, gain, sc, *, S, name):
    T, D = x.shape
    B = T // S
    tt = _tile(S, 512)
    per = S // tt

    def body(x_ref, dh_ref, dres_ref, g_ref, sc_ref, dx_ref, dsh_ref, dsc_ref, dg_ref):
        i = pl.program_id(0)
        xv = x_ref[...]
        dhv = dh_ref[...].astype(F32)
        r = lax.rsqrt(jnp.mean(xv * xv, axis=-1, keepdims=True) + NORM_EPS)
        n = xv * r
        g = g_ref[...]
        one_sc = 1.0 + sc_ref[...]
        dn = dhv * (g * one_sc)
        dx_ref[...] = dres_ref[...] + r * (dn - n * jnp.mean(dn * n, axis=-1, keepdims=True))
        dhn = dhv * n

        @pl.when(i % per == 0)
        def _():
            dsh_ref[...] = jnp.zeros_like(dsh_ref)
            dsc_ref[...] = jnp.zeros_like(dsc_ref)

        @pl.when(i == 0)
        def _():
            dg_ref[...] = jnp.zeros_like(dg_ref)

        dsh_ref[...] += jnp.sum(dhv, axis=0, keepdims=True)
        dsc_ref[...] += jnp.sum(dhn, axis=0, keepdims=True) * g
        dg_ref[...] += jnp.sum(dhn, axis=0, keepdims=True) * one_sc

    return pl.pallas_call(
        body, name=name, grid=(T // tt,),
        in_specs=[pl.BlockSpec((tt, D), lambda i: (i, 0))] * 3 + [pl.BlockSpec((1, D), lambda i: (0, 0))]
        + _row_specs(tt, D, per, 1),
        out_specs=[pl.BlockSpec((tt, D), lambda i: (i, 0))] + _row_specs(tt, D, per, 2)
        + [pl.BlockSpec((1, D), lambda i: (0, 0))],
        out_shape=[jax.ShapeDtypeStruct((T, D), F32), jax.ShapeDtypeStruct((B, 1, D), F32),
                   jax.ShapeDtypeStruct((B, 1, D), F32), jax.ShapeDtypeStruct((1, D), F32)],
        compiler_params=_cparams(("arbitrary",)),
    )(x, dh, dres, gain, sc)


def _gate_bwd(dx, y, g, *, S, name):
    T, D = dx.shape
    B = T // S
    tt = _tile(S, 512)
    per = S // tt

    def body(dx_ref, y_ref, g_ref, dy_ref, dg_ref):
        i = pl.program_id(0)
        dxv = dx_ref[...]
        dy_ref[...] = (dxv * g_ref[...]).astype(BF16)

        @pl.when(i % per == 0)
        def _():
            dg_ref[...] = jnp.zeros_like(dg_ref)

        dg_ref[...] += jnp.sum(dxv * y_ref[...], axis=0, keepdims=True)

    return pl.pallas_call(
        body, name=name, grid=(T // tt,),
        in_specs=[pl.BlockSpec((tt, D), lambda i: (i, 0))] * 2 + _row_specs(tt, D, per, 1),
        out_specs=[pl.BlockSpec((tt, D), lambda i: (i, 0))] + _row_specs(tt, D, per, 1),
        out_shape=[jax.ShapeDtypeStruct((T, D), BF16), jax.ShapeDtypeStruct((B, 1, D), F32)],
        compiler_params=_cparams(("arbitrary",)),
    )(dx, y, g)


def _final_loss(x, target, gain):
    T, D = x.shape
    tt = _tile(T, 512)

    def body(x_ref, t_ref, g_ref, dx_ref, dg_ref, loss_ref):
        i = pl.program_id(0)
        xv = x_ref[...]
        r = lax.rsqrt(jnp.mean(xv * xv, axis=-1, keepdims=True) + NORM_EPS)
        n = xv * r
        g = g_ref[...]
        err = n * g - t_ref[...]
        dy = err * (1.0 / D)
        dn = dy * g
        dx_ref[...] = r * (dn - n * jnp.mean(dn * n, axis=-1, keepdims=True))

        @pl.when(i == 0)
        def _():
            dg_ref[...] = jnp.zeros_like(dg_ref)
            loss_ref[...] = jnp.zeros_like(loss_ref)

        dg_ref[...] += jnp.sum(dy * n, axis=0, keepdims=True)
        loss_ref[...] += jnp.sum(jnp.sum(err * err, axis=-1, keepdims=True), axis=0, keepdims=True) * (0.5 / D)

    return pl.pallas_call(
        body, name="final_loss", grid=(T // tt,),
        in_specs=[pl.BlockSpec((tt, D), lambda i: (i, 0))] * 2 + [pl.BlockSpec((1, D), lambda i: (0, 0))],
        out_specs=[pl.BlockSpec((tt, D), lambda i: (i, 0)), pl.BlockSpec((1, D), lambda i: (0, 0)),
                   pl.BlockSpec((1, LANES), lambda i: (0, 0))],
        out_shape=[jax.ShapeDtypeStruct((T, D), F32), jax.ShapeDtypeStruct((1, D), F32),
                   jax.ShapeDtypeStruct((1, LANES), F32)],
        compiler_params=_cparams(("arbitrary",)),
    )(x, target, gain)


def _head_masks(ew):
    lane = lax.broadcasted_iota(jnp.int32, (1, PAIR_Q), 1)
    m0 = (lane < HEAD_DIM) | ((lane >= LANES) & (lane < LANES + ew))
    m1 = ((lane >= HEAD_DIM) & (lane < LANES)) | ((lane >= LANES + ew) & (lane < LANES + 2 * ew))
    return m0, m1


def _dot_nt(a, b):
    return lax.dot_general(a, b, (((1,), (1,)), ((), ())), preferred_element_type=F32)


def _dot_tn(a, b):
    return lax.dot_general(a, b, (((0,), (0,)), ((), ())), preferred_element_type=F32)


def _lane_halves(x, op):
    acc = x[:, 0:LANES]
    for g in range(1, x.shape[1] // LANES):
        acc = op(acc, x[:, g * LANES:(g + 1) * LANES])
    return acc


def _head_rows(cols_lane_replicated):
    t = cols_lane_replicated.T
    sub = lax.broadcasted_iota(jnp.int32, (8, t.shape[1]), 0)
    return jnp.where(sub == 1, t[HEAD_DIM:HEAD_DIM + 8], t[0:8])


def _attn_fwd(qx, kvx, *, S, scale, ew, name):
    T = qx.shape[0]
    P = qx.shape[1] // PAIR_Q
    B = T // S
    tk = _tile(S, ATTN_BLOCK)
    tq = _tile(S, ATTN_Q_ROWS)
    nq = S // tq
    per = tq // tk

    def body(q_ref, kv_ref, o_ref, lse_ref, ot_ref, m_sc, l_sc, acc_sc):
        qi = pl.program_id(2)
        q = q_ref[...]
        masks = _head_masks(ew)
        qh = [jnp.where(m, q, jnp.zeros_like(q)) for m in masks]

        def logits(h, k, diagonal):
            s = _dot_nt(qh[h], k)
            if scale != 1.0:
                s = s * scale
            if diagonal is None:
                return s
            row = lax.broadcasted_iota(jnp.int32, s.shape, 0)
            col = lax.broadcasted_iota(jnp.int32, s.shape, 1)
            return jnp.where(col + diagonal * tk <= row, s, NEG_BIG)

        def trip(first, count, n_diagonal=0):
            rows = [pl.ds(pl.multiple_of((first + u) * tk, tk), tk) for u in range(count)]
            diag = [None] * (count - n_diagonal) + list(range(n_diagonal))
            for h in range(2):
                ss = [logits(h, kv_ref[rows[u], 0:PAIR_Q], diag[u]) for u in range(count)]
                m_prev = m_sc[h]
                m_elem = m_prev
                for s in ss:
                    m_elem = jnp.maximum(m_elem, _lane_halves(s, jnp.maximum))
                m_new = jnp.broadcast_to(jnp.max(m_elem, axis=1, keepdims=True), (tq, LANES))
                alpha = jnp.exp(m_prev - m_new)
                l = alpha * l_sc[h]
                acc = alpha * acc_sc[h]
                for u, s in enumerate(ss):
                    p = jnp.concatenate([jnp.exp(s[:, g * LANES:(g + 1) * LANES] - m_new)
                                         for g in range(tk // LANES)], axis=1)
                    l = l + _lane_halves(p, jnp.add)
                    acc = acc + jnp.dot(p.astype(BF16), kv_ref[rows[u], PAIR_Q:PAIR_KV], preferred_element_type=F32)
                m_sc[h] = m_new
                l_sc[h] = l
                acc_sc[h] = acc

        m_sc[...] = jnp.full(m_sc.shape, NEG_BIG, F32)
        l_sc[...] = jnp.zeros_like(l_sc)
        acc_sc[...] = jnp.zeros_like(acc_sc)

        def loop_body(t, carry):
            trip(t * ATTN_UNROLL, ATTN_UNROLL)
            return carry

        below = qi * per
        lax.fori_loop(0, below // ATTN_UNROLL, loop_body, 0)
        for left in range(0, ATTN_UNROLL, per):
            @pl.when(below % ATTN_UNROLL == left)
            def _(left=left):
                trip(below - left, left + per, n_diagonal=per)

        lane = lax.broadcasted_iota(jnp.int32, (tq, LANES), 1)
        lo = lane < HEAD_DIM
        l = [jnp.sum(l_sc[h], axis=1, keepdims=True) for h in range(2)]
        o = jnp.where(lo, acc_sc[0] / l[0], acc_sc[1] / l[1])
        o_ref[...] = o.astype(BF16)
        ot_ref[...] = o.T.astype(BF16)
        lse = jnp.where(lo, m_sc[0] + jnp.log(l[0]), m_sc[1] + jnp.log(l[1]))
        for r in range(per):
            lse_ref[r] = _head_rows(lse[r * tk:(r + 1) * tk])

    return pl.pallas_call(
        body, name=name, grid=(B, P, nq),
        in_specs=[pl.BlockSpec((tq, PAIR_Q), lambda b, p, i: (b * nq + i, p)),
                  pl.BlockSpec((S, PAIR_KV), lambda b, p, i: (b, p))],
        out_specs=[pl.BlockSpec((tq, LANES), lambda b, p, i: (b * nq + i, p)),
                   pl.BlockSpec((per, None, 8, tk), lambda b, p, i: (b * nq + i, p, 0, 0)),
                   pl.BlockSpec((LANES, tq), lambda b, p, i: (p, b * nq + i))],
        out_shape=[jax.ShapeDtypeStruct((T, P * LANES), BF16), jax.ShapeDtypeStruct((T // tk, P, 8, tk), F32),
                   jax.ShapeDtypeStruct((P * LANES, T), BF16)],
        scratch_shapes=[pltpu.VMEM((2, tq, LANES), F32)] * 3,
        compiler_params=_cparams(("parallel", "parallel", "arbitrary")),
    )(qx, kvx)


def _attn_bwd(qx, kvx, o, lse, do, *, S, scale, ew, name, bias_grad=False):
    T = qx.shape[0]
    P = qx.shape[1] // PAIR_Q
    B = T // S
    tq = _tile(S, ATTN_BLOCK)
    tk = _tile(S, ATTN_K_ROWS)
    nq = S // tq
    nk = S // tk
    per = tk // tq

    def body(q_ref, kv_ref, o_ref, lse_ref, do_ref, dq_ref, dkv_ref, *rest):
        kj = pl.program_id(2)
        if bias_grad:
            csum_ref, rsum_ref, dq_sc, delta_sc, dk_sc, dv_sc, cs_sc = rest
            cs_sc[...] = jnp.zeros_like(cs_sc)

            @pl.when(kj == 0)
            def _():
                rsum_ref[...] = jnp.zeros_like(rsum_ref)
        else:
            dq_sc, delta_sc, dk_sc, dv_sc = rest
        masks = _head_masks(ew)
        lo_q = lax.broadcasted_iota(jnp.int32, (tq, LANES), 1) < HEAD_DIM
        lo = lax.broadcasted_iota(jnp.int32, (tk, LANES), 1) < HEAD_DIM
        vmask = [lo, jnp.logical_not(lo)]

        @pl.when(kj == 0)
        def _():
            dq_sc[...] = jnp.zeros_like(dq_sc)
            for c in range(nq):
                rows = pl.ds(c * tq, tq)
                x = do_ref[rows, :].astype(F32) * o_ref[rows, :].astype(F32)
                r0 = jnp.sum(jnp.where(lo_q, x, 0.0), axis=1, keepdims=True)
                r1 = jnp.sum(jnp.where(lo_q, 0.0, x), axis=1, keepdims=True)
                delta_sc[c] = _head_rows(jnp.where(lo_q, r0, r1))

        k = kv_ref[:, 0:PAIR_Q]
        v = kv_ref[:, PAIR_Q:PAIR_KV]
        kh = [jnp.where(m, k, jnp.zeros_like(k)) for m in masks]
        vh = [jnp.where(m, v, jnp.zeros_like(v)) for m in vmask]
        dk_sc[...] = jnp.zeros_like(dk_sc)
        dv_sc[...] = jnp.zeros_like(dv_sc)

        def step(qi, diagonal):
            rows = pl.ds(pl.multiple_of(qi * tq, tq), tq)
            q = q_ref[rows, :]
            dov = do_ref[rows, :]
            lse8 = lse_ref[qi]
            dl8 = delta_sc[qi]
            for h in range(2):
                st = _dot_nt(kh[h], q)
                if scale != 1.0:
                    st = st * scale
                if diagonal is not None:
                    key = lax.broadcasted_iota(jnp.int32, st.shape, 0)
                    qry = lax.broadcasted_iota(jnp.int32, st.shape, 1)
                    st = jnp.where(key <= qry + diagonal * tq, st, NEG_BIG)
                pt = jnp.exp(st - lse8[h:h + 1, :])
                dpt = _dot_nt(vh[h], dov)
                dst = pt * (dpt - dl8[h:h + 1, :])
                if bias_grad:
                    cs_sc[h] += _lane_halves(dst, jnp.add)
                    rsum_ref[qi, h:h + 1, :] += jnp.sum(dst, axis=0, keepdims=True)
                if scale != 1.0:
                    dst = dst * scale
                ptb = pt.astype(BF16)
                dstb = dst.astype(BF16)
                dv_sc[h] += jnp.dot(ptb, dov, preferred_element_type=F32)
                dk_sc[h] += jnp.dot(dstb, q, preferred_element_type=F32)
                dq_sc[rows, :] += _dot_tn(dstb, kh[h])

        first = kj * per
        above = nq - per - first
        for left in range(0, ATTN_UNROLL, per):
            @pl.when(above % ATTN_UNROLL == left)
            def _(left=left):
                for d in range(per):
                    step(first + d, d)
                for u in range(left):
                    step(first + per + u, None)

        def loop_body(t, carry):
            for u in range(ATTN_UNROLL):
                step(first + per + above % ATTN_UNROLL + t * ATTN_UNROLL + u, None)
            return carry

        lax.fori_loop(0, above // ATTN_UNROLL, loop_body, 0)
        dkv_ref[:, 0:PAIR_Q] = (jnp.where(masks[0], dk_sc[0], 0.0) + jnp.where(masks[1], dk_sc[1], 0.0)).astype(BF16)
        dkv_ref[:, PAIR_Q:PAIR_KV] = jnp.where(lo, dv_sc[0], dv_sc[1]).astype(BF16)
        if bias_grad:
            csum_ref[...] = jnp.where(lo, jnp.sum(cs_sc[0], axis=1, keepdims=True),
                                      jnp.sum(cs_sc[1], axis=1, keepdims=True))

        @pl.when(kj == nk - 1)
        def _():
            dq_ref[...] = dq_sc[...].astype(BF16)

    rows_spec = pl.BlockSpec((nq, None, 8, tq), lambda b, p, j: (b, p, 0, 0))
    out_specs = [pl.BlockSpec((S, PAIR_Q), lambda b, p, j: (b, p)),
                 pl.BlockSpec((tk, PAIR_KV), lambda b, p, j: (b * nk + j, p))]
    out_shape = [jax.ShapeDtypeStruct((T, P * PAIR_Q), BF16), jax.ShapeDtypeStruct((T, P * PAIR_KV), BF16)]
    scratch = [pltpu.VMEM((S, PAIR_Q), F32), pltpu.VMEM((nq, 8, tq), F32),
               pltpu.VMEM((2, tk, PAIR_Q), F32), pltpu.VMEM((2, tk, LANES), F32)]
    if bias_grad:
        out_specs += [pl.BlockSpec((tk, LANES), lambda b, p, j: (b * nk + j, p)), rows_spec]
        out_shape += [jax.ShapeDtypeStruct((T, P * LANES), F32), jax.ShapeDtypeStruct((T // tq, P, 8, tq), F32)]
        scratch.append(pltpu.VMEM((2, tk, LANES), F32))
    return pl.pallas_call(
        body, name=name, grid=(B, P, nk),
        in_specs=[pl.BlockSpec((S, PAIR_Q), lambda b, p, j: (b, p)),
                  pl.BlockSpec((tk, PAIR_KV), lambda b, p, j: (b * nk + j, p)),
                  pl.BlockSpec((S, LANES), lambda b, p, j: (b, p)), rows_spec,
                  pl.BlockSpec((S, LANES), lambda b, p, j: (b, p))],
        out_specs=out_specs, out_shape=out_shape, scratch_shapes=scratch,
        compiler_params=_cparams(("parallel", "parallel", "arbitrary")),
    )(qx, kvx, o, lse, do)


def _fox_consts(P):
    H = 2 * P
    eq = np.zeros((3 * LANES, P * LANES), np.float32)
    ek = np.zeros((3 * LANES, P * LANES), np.float32)
    ones_q = np.zeros((1, P * LANES), np.float32)
    ones_k = np.zeros((1, P * LANES), np.float32)
    for h in range(H):
        base = (h // 2) * LANES + FOX_EXTRA * (h % 2)
        for part in range(3):
            eq[part * LANES + h, base + part] = 1.0
            ones_q[0, base + 3 + part] = 1.0
            ones_k[0, base + part] = 1.0
            ek[part * LANES + h, base + 3 + part] = -1.0
    return eq, ek, ones_q, ones_k


def _split3(f):
    hi = f.astype(BF16)
    r = f - hi.astype(F32)
    mid = r.astype(BF16)
    lo = (r - mid.astype(F32)).astype(BF16)
    return hi, mid, lo


def _tri_sum(tri, x):
    hi, mid, lo = _split3(x)
    return (jnp.dot(tri, hi, preferred_element_type=F32) + jnp.dot(tri, mid, preferred_element_type=F32)
            + jnp.dot(tri, lo, preferred_element_type=F32))


def _log1p_pos(e):
    return jnp.where(e < 0.01, e * (1.0 - e * (0.5 - e * (1.0 / 3.0))), jnp.log(1.0 + e))


def _fox_prep(qkv, fl, b_row, *, S, D, name):
    T = qkv.shape[0]
    P = D // LANES
    B = T // S
    tt = _tile(S, 256)
    per = S // tt
    eq, ek, ones_q, ones_k = _fox_consts(P)
    q_scale = HEAD_DIM ** -0.5

    def body(q_ref, k_ref, v_ref, fl_ref, b_ref, eq_ref, ek_ref, oq_ref, ok_ref, qx_ref, kvx_ref, carry):
        i = pl.program_id(1)

        @pl.when(i == 0)
        def _():
            carry[...] = jnp.zeros_like(carry)

        z = fl_ref[...] + b_ref[...]
        logf = jnp.minimum(z, 0.0) - _log1p_pos(jnp.exp(-jnp.abs(z)))
        row = lax.broadcasted_iota(jnp.int32, (tt, tt), 0)
        col = lax.broadcasted_iota(jnp.int32, (tt, tt), 1)
        tri = (col <= row).astype(BF16)
        f = _tri_sum(tri, logf) + carry[...]
        carry[...] = f[tt - 1:tt, :]
        parts = jnp.concatenate(_split3(f), axis=1)
        xq = jnp.dot(parts, eq_ref[...], preferred_element_type=F32) + oq_ref[...]
        xk = jnp.dot(parts, ek_ref[...], preferred_element_type=F32) + ok_ref[...]
        for p in range(P):
            c = slice(p * LANES, (p + 1) * LANES)
            qx_ref[:, p * PAIR_Q:p * PAIR_Q + LANES] = (q_ref[:, c].astype(F32) * q_scale).astype(BF16)
            qx_ref[:, p * PAIR_Q + LANES:(p + 1) * PAIR_Q] = xq[:, c].astype(BF16)
            kvx_ref[:, p * PAIR_KV:p * PAIR_KV + LANES] = k_ref[:, c]
            kvx_ref[:, p * PAIR_KV + LANES:p * PAIR_KV + PAIR_Q] = xk[:, c].astype(BF16)
            kvx_ref[:, p * PAIR_KV + PAIR_Q:(p + 1) * PAIR_KV] = v_ref[:, c]

    tok = lambda b, i: (b * per + i, 0)
    const = lambda b, i: (0, 0)
    return pl.pallas_call(
        body, name=name, grid=(B, per),
        in_specs=[pl.BlockSpec((tt, D), lambda b, i: (b * per + i, 0)),
                  pl.BlockSpec((tt, D), lambda b, i: (b * per + i, 1)),
                  pl.BlockSpec((tt, D), lambda b, i: (b * per + i, 2)),
                  pl.BlockSpec((tt, LANES), tok), pl.BlockSpec((1, LANES), const),
                  pl.BlockSpec(eq.shape, const), pl.BlockSpec(ek.shape, const),
                  pl.BlockSpec(ones_q.shape, const), pl.BlockSpec(ones_k.shape, const)],
        out_specs=[pl.BlockSpec((tt, P * PAIR_Q), tok), pl.BlockSpec((tt, P * PAIR_KV), tok)],
        out_shape=[jax.ShapeDtypeStruct((T, P * PAIR_Q), BF16), jax.ShapeDtypeStruct((T, P * PAIR_KV), BF16)],
        scratch_shapes=[pltpu.VMEM((1, LANES), F32)],
        compiler_params=_cparams(("arbitrary", "arbitrary")),
    )(qkv, qkv, qkv, fl, b_row, jnp.asarray(eq, BF16), jnp.asarray(ek, BF16), jnp.asarray(ones_q), jnp.asarray(ones_k))


def _fox_unprep(dqx, dkvx, csum, rsum, fl, b_row, *, S, D, name):
    T = dqx.shape[0]
    P = D // LANES
    B = T // S
    tt = _tile(S, 256)
    per = S // tt
    q_scale = HEAD_DIM ** -0.5

    def body(dq_ref, dkv_ref, cs_ref, rs_ref, fl_ref, b_ref, dqkv_ref, dfl_ref, db_ref, carry):
        b = pl.program_id(0)
        i = pl.program_id(1)

        @pl.when(i == 0)
        def _():
            carry[...] = jnp.zeros_like(carry)

        @pl.when((i == 0) & (b == 0))
        def _():
            db_ref[...] = jnp.zeros_like(db_ref)

        df = rs_ref[...] - cs_ref[...]
        for p in range(P):
            rq = slice(p * LANES, (p + 1) * LANES)
            dqkv_ref[:, rq] = (dq_ref[:, p * PAIR_Q:p * PAIR_Q + LANES].astype(F32) * q_scale).astype(BF16)
            dqkv_ref[:, D + p * LANES:D + (p + 1) * LANES] = dkv_ref[:, p * PAIR_KV:p * PAIR_KV + LANES]
            dqkv_ref[:, 2 * D + p * LANES:2 * D + (p + 1) * LANES] = dkv_ref[:, p * PAIR_KV + PAIR_Q:(p + 1) * PAIR_KV]
        row = lax.broadcasted_iota(jnp.int32, (tt, tt), 0)
        col = lax.broadcasted_iota(jnp.int32, (tt, tt), 1)
        tri = (col >= row).astype(BF16)
        dlogf = _tri_sum(tri, df) + carry[...]
        carry[...] = dlogf[0:1, :]
        z = fl_ref[...] + b_ref[...]
        e = jnp.exp(-jnp.abs(z))
        sig_neg = jnp.where(z >= 0.0, e, 1.0) / (1.0 + e)
        dfl = dlogf * sig_neg
        dfl_ref[...] = dfl.astype(BF16)
        db_ref[...] += jnp.sum(dfl, axis=0, keepdims=True)

    rev = lambda b, i: (b * per + per - 1 - i, 0)
    const = lambda b, i: (0, 0)
    return pl.pallas_call(
        body, name=name, grid=(B, per),
        in_specs=[pl.BlockSpec((tt, P * PAIR_Q), rev), pl.BlockSpec((tt, P * PAIR_KV), rev),
                  pl.BlockSpec((tt, LANES), rev), pl.BlockSpec((tt, LANES), rev), pl.BlockSpec((tt, LANES), rev),
                  pl.BlockSpec((1, LANES), const)],
        out_specs=[pl.BlockSpec((tt, 3 * D), rev), pl.BlockSpec((tt, LANES), rev), pl.BlockSpec((1, LANES), const)],
        out_shape=[jax.ShapeDtypeStruct((T, 3 * D), BF16), jax.ShapeDtypeStruct((T, LANES), BF16),
                   jax.ShapeDtypeStruct((1, LANES), F32)],
        scratch_shapes=[pltpu.VMEM((1, LANES), F32)],
        compiler_params=_cparams(("arbitrary", "arbitrary")),
    )(dqx, dkvx, csum, rsum, fl, b_row)


def _rms(x):
    r = lax.rsqrt(jnp.mean(x * x, axis=-1, keepdims=True) + NORM_EPS)
    return x * r, r


def _mla_mid(lat, gq, gkv, cos_t, sin_s, *, name):
    T, W = lat.shape
    Rq = W - 2 * LANES
    tt = _tile(T, 512)

    def body(l_ref, gq_ref, gkv_ref, c_ref, s_ref, o_ref, ot_ref):
        nq, _ = _rms(l_ref[:, 0:Rq])
        nkv, _ = _rms(l_ref[:, Rq:Rq + LANES])
        parts = [nq * gq_ref[...], nkv * gkv_ref[...], _rope128(l_ref[:, Rq + LANES:W], c_ref[...], s_ref[...])]
        out = jnp.concatenate(parts, axis=1)
        o_ref[...] = out.astype(BF16)
        ot_ref[...] = out.T.astype(BF16)

    return pl.pallas_call(
        body, name=name, grid=(T // tt,),
        in_specs=[pl.BlockSpec((tt, W), lambda i: (i, 0)), pl.BlockSpec((1, Rq), lambda i: (0, 0)),
                  pl.BlockSpec((1, LANES), lambda i: (0, 0)), pl.BlockSpec((tt, LANES), lambda i: (i, 0)),
                  pl.BlockSpec((tt, LANES), lambda i: (i, 0))],
        out_specs=[pl.BlockSpec((tt, W), lambda i: (i, 0)), pl.BlockSpec((W, tt), lambda i: (0, i))],
        out_shape=[jax.ShapeDtypeStruct((T, W), BF16), jax.ShapeDtypeStruct((W, T), BF16)],
        compiler_params=_cparams(("parallel",)),
    )(lat, gq, gkv, cos_t, sin_s)


def _mla_mid_bwd(lat, dcq, dckr, gq, gkv, cos_t, sin_s, *, name):
    T, W = lat.shape
    Rq = W - 2 * LANES
    tt = _tile(T, 512)

    def norm_bwd(x, dy, g):
        n, r = _rms(x)
        dn = dy * g
        return r * (dn - n * jnp.mean(dn * n, axis=-1, keepdims=True)), jnp.sum(dy * n, axis=0, keepdims=True)

    def body(l_ref, dq_ref, dk_ref, gq_ref, gkv_ref, c_ref, s_ref, o_ref, dgq_ref, dgkv_ref):
        i = pl.program_id(0)

        @pl.when(i == 0)
        def _():
            dgq_ref[...] = jnp.zeros_like(dgq_ref)
            dgkv_ref[...] = jnp.zeros_like(dgkv_ref)

        dxq, dgq = norm_bwd(l_ref[:, 0:Rq], dq_ref[...], gq_ref[...])
        dxkv, dgkv = norm_bwd(l_ref[:, Rq:Rq + LANES], dk_ref[:, 0:LANES], gkv_ref[...])
        o_ref[:, 0:Rq] = dxq.astype(BF16)
        o_ref[:, Rq:Rq + LANES] = dxkv.astype(BF16)
        o_ref[:, Rq + LANES:W] = _rope128(dk_ref[:, LANES:2 * LANES], c_ref[...], -s_ref[...]).astype(BF16)
        dgq_ref[...] += dgq
        dgkv_ref[...] += dgkv

    return pl.pallas_call(
        body, name=name, grid=(T // tt,),
        in_specs=[pl.BlockSpec((tt, W), lambda i: (i, 0)), pl.BlockSpec((tt, Rq), lambda i: (i, 0)),
                  pl.BlockSpec((tt, 2 * LANES), lambda i: (i, 0)), pl.BlockSpec((1, Rq), lambda i: (0, 0)),
                  pl.BlockSpec((1, LANES), lambda i: (0, 0)), pl.BlockSpec((tt, LANES), lambda i: (i, 0)),
                  pl.BlockSpec((tt, LANES), lambda i: (i, 0))],
        out_specs=[pl.BlockSpec((tt, W), lambda i: (i, 0)), pl.BlockSpec((1, Rq), lambda i: (0, 0)),
                   pl.BlockSpec((1, LANES), lambda i: (0, 0))],
        out_shape=[jax.ShapeDtypeStruct((T, W), BF16), jax.ShapeDtypeStruct((1, Rq), F32),
                   jax.ShapeDtypeStruct((1, LANES), F32)],
        compiler_params=_cparams(("arbitrary",)),
    )(lat, dcq, dckr, gq, gkv, cos_t, sin_s)


def _uq_to_pairs(w):
    Rq = w.shape[0]
    P = w.shape[1] // (2 * (HEAD_DIM + ROPE_DIM))
    w4 = w.reshape(Rq, P, 2, HEAD_DIM + ROPE_DIM)
    nope = w4[..., :HEAD_DIM].reshape(Rq, P, 2 * HEAD_DIM)
    rope = w4[..., HEAD_DIM:].reshape(Rq, P, 2 * ROPE_DIM)
    pad = jnp.zeros((Rq, P, PAIR_Q - 2 * HEAD_DIM - 2 * ROPE_DIM), w.dtype)
    return jnp.concatenate([nope, rope, pad], axis=-1).reshape(Rq, P * PAIR_Q)


def _uq_from_pairs(g):
    Rq = g.shape[0]
    P = g.shape[1] // PAIR_Q
    g3 = g.reshape(Rq, P, PAIR_Q)
    nope = g3[..., :2 * HEAD_DIM].reshape(Rq, P, 2, HEAD_DIM)
    rope = g3[..., 2 * HEAD_DIM:2 * HEAD_DIM + 2 * ROPE_DIM].reshape(Rq, P, 2, ROPE_DIM)
    return jnp.concatenate([nope, rope], axis=-1).reshape(Rq, P * 2 * (HEAD_DIM + ROPE_DIM))


def _ukv_to_pairs(w):
    P = w.shape[1] // (4 * HEAD_DIM)
    w4 = w.reshape(KV_RANK, P, 2, 2 * HEAD_DIM)
    kn = w4[..., :HEAD_DIM].reshape(KV_RANK, P, 2 * HEAD_DIM)
    vv = w4[..., HEAD_DIM:].reshape(KV_RANK, P, 2 * HEAD_DIM)
    top = jnp.concatenate([kn, jnp.zeros((KV_RANK, P, LANES), w.dtype), vv], axis=-1)
    place = np.zeros((LANES, P, PAIR_KV), np.float32)
    for r in range(ROPE_DIM):
        place[r, :, LANES + r] = 1.0
        place[r, :, LANES + ROPE_DIM + r] = 1.0
    return jnp.concatenate([top, jnp.asarray(place, w.dtype)], axis=0).reshape(KV_RANK + LANES, P * PAIR_KV)


def _ukv_from_pairs(g):
    P = g.shape[1] // PAIR_KV
    g3 = g[:KV_RANK].reshape(KV_RANK, P, PAIR_KV)
    kn = g3[..., :2 * HEAD_DIM].reshape(KV_RANK, P, 2, HEAD_DIM)
    vv = g3[..., PAIR_Q:].reshape(KV_RANK, P, 2, HEAD_DIM)
    return jnp.concatenate([kn, vv], axis=-1).reshape(KV_RANK, P * 4 * HEAD_DIM)


def _mlp_fwd(h2, w, i, x1, gate, *, S):
    def act(acc):
        u = jnp.square(jnp.maximum(acc, 0.0))
        return acc, u, u

    p, u, u_t = _mm(h2, w["mlp_w1"], "nn", name=f"mlp_up_{i}", b_layer=i, out_dtypes=(BF16, BF16, BF16),
                    out_t=(False, False, True), epilogue=act)
    x2, z = _mm(u, w["mlp_w2"], "nn", name=f"mlp_down_{i}", b_layer=i, out_dtypes=(F32, F32), extras=(x1,),
                rowvecs=(gate,), seq=S, epilogue=lambda acc, xr, g: (xr + g * acc, acc))
    return x2, (p, u_t, z)


STACKED_GRADS = ("fox_out", "mla_down", "mla_uq", "mla_ukv", "mla_out", "mlp_w1", "mlp_w2")


def _local_step(x, target, pos_f, inv_freq_row, sign_row, mod, w, slots, *, S, hooks=None):
    hooks = hooks or {}
    T, D = x.shape
    L = mod.shape[0]
    L2 = len(w["fox_out"])
    cos_t, sin_s = _rope_tables(pos_f, inv_freq_row, sign_row)
    saved = []
    for i in range(L):
        j = i // 2
        sh_m, sc_m, g_m, sh_f, sc_f, g_f = (mod[i, s] for s in range(6))
        h, h_t = _norm_mod(x, w["norm_mix_g"][i], sc_m, sh_m, S=S, name=f"norm_mix_{i}")
        if i % 2 == 0:
            qkv = _mm(h, w["fox_qkv"], "nn", name=f"fox_qkv_{i}", b_layer=j, out_dtypes=(BF16,))
            fl = _mm(h, w["fox_f"], "nn", name=f"fox_f_{i}", b_layer=j)
            qx, kvx = _fox_prep(qkv, fl, w["fox_b"][j], S=S, D=D, name=f"fox_prep_{i}")
            o, lse, o_t = _attn_fwd(qx, kvx, S=S, scale=1.0, ew=FOX_EXTRA, name=f"fox_attn_{i}")
            mix = (qx, kvx, o, lse, o_t, fl)
            w_out = w["fox_out"]
        else:
            lat = _mm(h, w["mla_down"], "nn", name=f"mla_down_{i}", b_layer=j)
            Rq = lat.shape[1] - 2 * LANES
            cqr, cqr_t = _mla_mid(lat, w["mla_gq"][j], w["mla_gkv"][j], cos_t, sin_s, name=f"mla_mid_{i}")
            qx = _mm(cqr, w["mla_uq"], "nn", name=f"mla_uq_{i}", b_layer=j, out_dtypes=(BF16,), a_sz=Rq, tk=Rq,
                     tables=(cos_t, sin_s), epilogue=lambda acc, c, s: (_rope_pairs(acc, c, s, 1.0),))
            kvx = _mm(cqr, w["mla_ukv"], "nn", name=f"mla_ukv_{i}", b_layer=j, out_dtypes=(BF16,), a_off=Rq,
                      a_sz=2 * LANES, tk=2 * LANES, tn=PAIR_KV)
            o, lse, o_t = _attn_fwd(qx, kvx, S=S, scale=(HEAD_DIM + ROPE_DIM) ** -0.5, ew=ROPE_DIM,
                                    name=f"mla_attn_{i}")
            mix = (qx, kvx, o, lse, o_t, lat, cqr_t)
            w_out = w["mla_out"]
        x1, y = _mm(o, w_out, "nn", name=f"mix_out_{i}", b_layer=j, out_dtypes=(F32, F32), extras=(x,),
                    rowvecs=(g_m,), seq=S, epilogue=lambda acc, xr, g: (xr + g * acc, acc))
        h2, h2_t = _norm_mod(x1, w["norm_mlp_g"][i], sc_f, sh_f, S=S, name=f"norm_mlp_{i}")
        if i == 0 and "fwd_mlp0" in hooks:
            w = hooks["fwd_mlp0"](x1, w)
        x2, mlp = _mlp_fwd(h2, w, i, x1, g_f, S=S)
        saved.append((x, h_t, mix, y, x1, h2_t, mlp))
        x = x2
        if i == 0 and "fwd_layer1" in hooks:
            w = hooks["fwd_layer1"](x, w)

    dx, dg_final, loss = _final_loss(x, target, w["final_norm_g"])
    n_split = w["mlp_w1"][0][0].shape[1]

    grads = {k: [None] * len(w[k]) for k in ("norm_mix_g", "norm_mlp_g", "fox_b", "mla_gq", "mla_gkv")}
    grads.update({k: [None] * L2 for k in ("fox_qkv", "fox_f")})
    grads.update({k: {} for k in STACKED_GRADS})
    grads["final_norm_g"] = dg_final

    def stacked(key, layer, _, a_t, b, **kw):
        group, idx, count = slots[(key, layer)]
        grads[key][group] = _mm(a_t, b, "nn", out_stack=(grads[key].get(group), idx, count), **kw)

    dmod = [None] * L
    for i in reversed(range(L)):
        j = i // 2
        x0, h_t, mix, y, x1, h2_t, (p, u_t, z) = saved[i]
        sh_m, sc_m, g_m, sh_f, sc_f, g_f = (mod[i, s] for s in range(6))
        if i == 0 and "bwd_layer0" in hooks:
            g_f = g_f + hooks["bwd_layer0"](grads)[0, 0]
        dz, dg_f = _gate_bwd(dx, z, g_f, S=S, name=f"gate_mlp_bwd_{i}")
        stacked("mlp_w2", i, L, u_t, dz, name=f"mlp_w2_grad_{i}")
        dp = _mm(dz, w["mlp_w2"], "nt", name=f"mlp_down_bwd_{i}", b_layer=i, out_dtypes=(BF16,), extras=(p,),
                 epilogue=lambda acc, pv: (acc * (2.0 * jnp.maximum(pv.astype(F32), 0.0)),))
        stacked("mlp_w1", i, L, h2_t, dp, name=f"mlp_w1_grad_{i}", out_split=n_split)
        if i == 0 and "bwd_mix0" in hooks:
            g_m = g_m + hooks["bwd_mix0"](grads)[0, 0]
        dh2 = _mm(dp, w["mlp_w1"], "nt", name=f"mlp_up_bwd_{i}", b_layer=i)
        dx1, dsh_f, dsc_f, dgn = _norm_mod_bwd(x1, dh2, dx, w["norm_mlp_g"][i], sc_f, S=S, name=f"norm_mlp_bwd_{i}")
        grads["norm_mlp_g"][i] = dgn
        dy, dg_m = _gate_bwd(dx1, y, g_m, S=S, name=f"gate_mix_bwd_{i}")
        if i % 2 == 0:
            qx, kvx, o, lse, o_t, fl = mix
            stacked("fox_out", j, L2, o_t, dy, name=f"fox_out_grad_{i}")
            do = _mm(dy, w["fox_out"], "nt", name=f"fox_out_bwd_{i}", b_layer=j, out_dtypes=(BF16,))
            dqx, dkvx, csum, rsum = _attn_bwd(qx, kvx, o, lse, do, S=S, scale=1.0, ew=FOX_EXTRA,
                                              name=f"fox_attn_bwd_{i}", bias_grad=True)
            n_heads = D // HEAD_DIM
            csum = jnp.pad(csum.reshape(T, n_heads, HEAD_DIM)[:, :, 0], ((0, 0), (0, LANES - n_heads)))
            rsum = jnp.transpose(rsum[:, :, :2, :], (0, 3, 1, 2)).reshape(T, n_heads)
            rsum = jnp.pad(rsum, ((0, 0), (0, LANES - n_heads)))
            dqkv, dfl, db = _fox_unprep(dqx, dkvx, csum, rsum, fl, w["fox_b"][j], S=S, D=D, name=f"fox_unprep_{i}")
            grads["fox_b"][j] = db
            grads["fox_qkv"][j] = _mm(h_t, dqkv, "nn", name=f"fox_qkv_grad_{i}")
            grads["fox_f"][j] = _mm(h_t, dfl, "nn", name=f"fox_f_grad_{i}")
            dh_f = _mm(dfl, w["fox_f"], "nt", name=f"fox_f_bwd_{i}", b_layer=j)
            dh = _mm(dqkv, w["fox_qkv"], "nt", name=f"fox_qkv_bwd_{i}", b_layer=j, extras=(dh_f,),
                     epilogue=lambda acc, e: (acc + e,))
        else:
            qx, kvx, o, lse, o_t, lat, cqr_t = mix
            Rq = lat.shape[1] - 2 * LANES
            stacked("mla_out", j, L2, o_t, dy, name=f"mla_out_grad_{i}")
            do = _mm(dy, w["mla_out"], "nt", name=f"mla_out_bwd_{i}", b_layer=j, out_dtypes=(BF16,))
            dqx, dkvx = _attn_bwd(qx, kvx, o, lse, do, S=S, scale=(HEAD_DIM + ROPE_DIM) ** -0.5, ew=ROPE_DIM,
                                  name=f"mla_attn_bwd_{i}")
            dqpre = _unrope(dqx, cos_t, sin_s)
            stacked("mla_uq", j, L2, cqr_t[:Rq], dqpre, name=f"mla_uq_grad_{i}", out_split=n_split)
            stacked("mla_ukv", j, L2, cqr_t[Rq:], dkvx, name=f"mla_ukv_grad_{i}", tn=PAIR_KV, out_split=n_split)
            dcq = _mm(dqpre, w["mla_uq"], "nt", name=f"mla_uq_bwd_{i}", b_layer=j)
            dckr = _mm(dkvx, w["mla_ukv"], "nt", name=f"mla_ukv_bwd_{i}", b_layer=j, tk=PAIR_KV * 2)
            dlat, dgq, dgkv = _mla_mid_bwd(lat, dcq, dckr, w["mla_gq"][j], w["mla_gkv"][j], cos_t, sin_s,
                                           name=f"mla_mid_bwd_{i}")
            grads["mla_gq"][j] = dgq
            grads["mla_gkv"][j] = dgkv
            stacked("mla_down", j, L2, h_t, dlat, name=f"mla_down_grad_{i}")
            dh = _mm(dlat, w["mla_down"], "nt", name=f"mla_down_bwd_{i}", b_layer=j)
        dx, dsh_m, dsc_m, dgn = _norm_mod_bwd(x0, dh, dx1, w["norm_mix_g"][i], sc_m, S=S, name=f"norm_mix_bwd_{i}")
        grads["norm_mix_g"][i] = dgn
        dmod[i] = jnp.stack([dsh_m, dsc_m, dg_m, dsh_f, dsc_f, dg_f])
    return loss, dx, jnp.stack(dmod), grads


GATHERED = ("fox_in", "fox_out", "mla_down", "mla_uq", "mla_ukv", "mla_out", "mlp_w1", "mlp_w2")
ROW_SHARDED = ("fox_out", "mla_down", "mla_out", "mlp_w2")


def _shard_layouts(wts):
    dkv = wts["mla_w_dkv"]
    dkv = jnp.pad(dkv, ((0, 0), (0, 0), (0, 2 * LANES - dkv.shape[2])))
    return {
        "fox_in": _pad_lanes(wts["fox_w_in"].astype(BF16)),
        "fox_out": wts["fox_w_out"].astype(BF16),
        "mla_down": jnp.concatenate([wts["mla_w_dq"], dkv], axis=2).astype(BF16),
        "mla_uq": jax.vmap(_uq_to_pairs)(wts["mla_w_uq"].astype(BF16)),
        "mla_ukv": jax.vmap(_ukv_to_pairs)(wts["mla_w_ukv"].astype(BF16)),
        "mla_out": wts["mla_w_out"].astype(BF16),
        "mlp_w1": wts["mlp_w1"].astype(BF16),
        "mlp_w2": wts["mlp_w2"].astype(BF16),
    }


def _small_layouts(small):
    return {
        "fox_b": [jnp.pad(b, (0, LANES - b.shape[0]))[None, :] for b in small["fox_b_f"]],
        "mla_gq": [g[None, :] for g in small["mla_q_norm_g"]],
        "mla_gkv": [g[None, :] for g in small["mla_kv_norm_g"]],
        "norm_mix_g": [g[None, :] for g in small["norm_mix_g"]],
        "norm_mlp_g": [g[None, :] for g in small["norm_mlp_g"]],
        "final_norm_g": small["final_norm_g"][None, :],
    }


def _comm_groups(L, L2):
    rest = [("fox_in", 1, L2 - 1), ("fox_out", 1, L2 - 1), ("mla_down", 0, L2), ("mla_uq", 0, L2),
            ("mla_ukv", 0, L2), ("mla_out", 0, L2), ("mlp_w1", 1, L - 1), ("mlp_w2", 1, L - 1)]
    return {"mix0": [("fox_in", 0, 1), ("fox_out", 0, 1)], "mlp0": [("mlp_w1", 0, 1), ("mlp_w2", 0, 1)],
            "rest": [e for e in rest if e[2] > 0]}


def _layer_slots(groups):
    return {(n, s + l): (g, l, cnt) for g, entries in groups.items() for n, s, cnt in entries for l in range(cnt)}


def _pad_lanes(a):
    cols = a.shape[-1]
    return jnp.pad(a, [(0, 0)] * (a.ndim - 1) + [(0, -cols % LANES)])


def _weight_views(name, gathered, D, n_fox_heads):
    n, ns, rows, cols = gathered.shape
    if name == "fox_in":
        true_cols = (3 * D + n_fox_heads) // ns
        fox = jnp.concatenate([gathered[:, k, :, :true_cols] for k in range(ns)], axis=-1)
        return {"fox_qkv": fox[:, :, :3 * D], "fox_f": _pad_lanes(fox[:, :, 3 * D:])}
    if name in ROW_SHARDED:
        return {name: gathered.reshape(n, ns * rows, cols)}
    return {name: gathered}


def _grad_pieces(name, g, qkv_f, n_fox_heads, ns):
    if name == "fox_in":
        fox = jnp.stack([jnp.concatenate([a, b[:, :n_fox_heads]], axis=1) for a, b in qkv_f])
        cols = fox.shape[2] // ns
        return jnp.stack([_pad_lanes(fox[:, :, k * cols:(k + 1) * cols]) for k in range(ns)], axis=1)
    if name in ROW_SHARDED:
        return g.reshape(g.shape[0], ns, g.shape[1] // ns, g.shape[2])
    return g


def _small_grads(g, n_fox_heads):
    return {
        "norm_mix_g": jnp.concatenate(g["norm_mix_g"], axis=0),
        "norm_mlp_g": jnp.concatenate(g["norm_mlp_g"], axis=0),
        "final_norm_g": g["final_norm_g"][0],
        "fox_b_f": jnp.concatenate(g["fox_b"], axis=0)[:, :n_fox_heads],
        "mla_q_norm_g": jnp.concatenate(g["mla_gq"], axis=0),
        "mla_kv_norm_g": jnp.concatenate(g["mla_gkv"], axis=0),
    }


def _silu(c):
    return c * (1.0 / (1.0 + jnp.exp(-c)))


def _ada_fwd(c_all, ada_w, ada_b_cols):
    L, D, C = ada_w.shape
    Bg = c_all.shape[0]
    tc = _tile(C, 512)

    def body(c_ref, w_ref, b_ref, o_ref):
        ca = _silu(c_ref[...]).astype(BF16)
        o_ref[...] = jnp.dot(ca, w_ref[...].astype(BF16), preferred_element_type=F32) + b_ref[...]

    return pl.pallas_call(
        body, name="ada_fwd", grid=(L, C // tc),
        in_specs=[pl.BlockSpec((Bg, D), lambda l, j: (0, 0)), pl.BlockSpec((None, D, tc), lambda l, j: (l, 0, j)),
                  pl.BlockSpec((None, 1, tc), lambda l, j: (l, 0, j))],
        out_specs=pl.BlockSpec((None, Bg, tc), lambda l, j: (l, 0, j)),
        out_shape=jax.ShapeDtypeStruct((L, Bg, C), F32),
        compiler_params=_cparams(("parallel", "parallel")),
    )(c_all, ada_w, ada_b_cols)


def _ada_bwd(c_all, dmod_cols):
    L, Bg, C = dmod_cols.shape
    D = c_all.shape[1]
    tc = _tile(C, 512)

    def body(c_ref, d_ref, o_ref):
        ca = _silu(c_ref[...]).astype(BF16)
        o_ref[...] = _dot_tn(ca, d_ref[...].astype(BF16))

    return pl.pallas_call(
        body, name="ada_bwd", grid=(L, C // tc),
        in_specs=[pl.BlockSpec((Bg, D), lambda l, j: (0, 0)), pl.BlockSpec((None, Bg, tc), lambda l, j: (l, 0, j))],
        out_specs=pl.BlockSpec((None, D, tc), lambda l, j: (l, 0, j)),
        out_shape=jax.ShapeDtypeStruct((L, D, C), F32),
        compiler_params=_cparams(("parallel", "parallel")),
    )(c_all, dmod_cols)


def _adamw_update(w, gv, m, v):
    mn = ADAM_B1 * m + (1.0 - ADAM_B1) * gv
    vn = ADAM_B2 * v + (1.0 - ADAM_B2) * jnp.square(gv)
    m_hat = mn / (1.0 - ADAM_B1 ** ADAM_STEP)
    v_hat = vn / (1.0 - ADAM_B2 ** ADAM_STEP)
    return -ADAM_LR * (m_hat / (jnp.sqrt(v_hat) + ADAM_EPS) + ADAM_WD * w), mn, vn


def _adamw(w, g, m, v, *, name):
    shape = w.shape
    C = shape[-1]
    R = int(np.prod(shape[:-1])) if len(shape) > 1 else 1
    w2, g2, m2, v2 = (a.reshape(R, C) for a in (w, g, m, v))
    tr = _row_tile(R, C)

    def body(w_ref, g_ref, m_ref, v_ref, d_ref, nm_ref, nv_ref):
        d_ref[...], nm_ref[...], nv_ref[...] = _adamw_update(w_ref[...], g_ref[...], m_ref[...], v_ref[...])

    spec = pl.BlockSpec((tr, C), lambda i: (i, 0))
    out = pl.pallas_call(
        body, name=name, grid=(R // tr,), in_specs=[spec] * 4, out_specs=[spec] * 3,
        out_shape=[jax.ShapeDtypeStruct((R, C), F32)] * 3, compiler_params=_cparams(("parallel",)),
    )(w2, g2, m2, v2)
    return tuple(a.reshape(shape) for a in out)


def _adamw_halves(w, g_own, g_peer, m, v, c_idx, *, name):
    L, rows, C = w.shape
    R = rows // 2
    tr = _row_tile(R, C)

    def body(c_ref, w_ref, go_ref, gp_ref, m_ref, v_ref, g_ref, d_ref, nm_ref, nv_ref):
        gv = jnp.where(pl.program_id(1) == c_ref[0], go_ref[...], gp_ref[...])
        g_ref[...] = gv
        d_ref[...], nm_ref[...], nv_ref[...] = _adamw_update(w_ref[...], gv, m_ref[...], v_ref[...])

    full = pl.BlockSpec((None, None, tr, C), lambda l, hh, i, c_ref: (l, hh, i, 0))
    half = pl.BlockSpec((None, tr, C), lambda l, hh, i, c_ref: (l, i, 0))
    grid_spec = pltpu.PrefetchScalarGridSpec(
        num_scalar_prefetch=1, grid=(L, 2, R // tr), in_specs=[full, half, half, full, full], out_specs=[full] * 4)
    split = lambda a: a.reshape(L, 2, R, C)
    out = pl.pallas_call(
        body, name=name, grid_spec=grid_spec, out_shape=[jax.ShapeDtypeStruct((L, 2, R, C), F32)] * 4,
        compiler_params=_cparams(("parallel", "parallel", "parallel")),
    )(c_idx, split(w), g_own, g_peer, split(m), split(v))
    return tuple(a.reshape(w.shape) for a in out)


def _sum_gathered(dm8, sm8):
    n_dev, Bl, R, D = dm8.shape
    Rs = sm8.shape[1]

    def body(dm_ref, sm_ref, ob_ref, os_ref):
        acc_b = jnp.zeros((R, D), F32)
        acc_s = jnp.zeros((Rs, D), F32)
        for d in range(n_dev):
            for b in range(Bl):
                acc_b = acc_b + dm_ref[d, b]
            acc_s = acc_s + sm_ref[d]
        ob_ref[...] = acc_b
        os_ref[...] = acc_s

    return pl.pallas_call(
        body, name="sum_gathered",
        out_shape=[jax.ShapeDtypeStruct((R, D), F32), jax.ShapeDtypeStruct((Rs, D), F32)],
        compiler_params=_cparams(None),
    )(dm8, sm8)


N_DEV = 8
N_CHIP = 4
ANY = pl.BlockSpec(memory_space=pl.ANY)
HBM = pl.BlockSpec(memory_space=pltpu.HBM)
SEM = pl.BlockSpec(memory_space=pltpu.SEMAPHORE)
DATAFLOW = pltpu.SideEffectType.DATAFLOW_SIDE_EFFECTING


def _mesh_pos():
    return lax.axis_index("x"), lax.axis_index("y"), lax.axis_index("c")


def _all_gather8(block, *, name, in_vmem):
    R, W = block.shape

    def body(x_ref, out_ref, send_sems, recv_sems, local_sem):
        x, y, c = _mesh_pos()
        me, sibling = (x, y, c), (x, y, 1 - c)
        chips = [(1 - x, y), (x, 1 - y), (1 - x, 1 - y)]

        def slot(px, py, pc):
            return out_ref.at[4 * px + 2 * py + pc]

        def copy(k, blk, to, src=None):
            return pltpu.make_async_remote_copy(
                src_ref=slot(*blk) if src is None else src, dst_ref=slot(*blk),
                send_sem=send_sems.at[k], recv_sem=recv_sems.at[k], device_id=to, device_id_type=MESH_ID)

        mine = pltpu.make_async_copy(x_ref, slot(*me), local_sem)
        mine.start()
        first = [copy(0, me, sibling, src=x_ref)]
        first += [copy(1 + j, me, (*chip, c), src=x_ref) for j, chip in enumerate(chips)]
        for cp in first:
            cp.start()
        passed = [copy(4 + j, (*chip, c), sibling) for j, chip in enumerate(chips)]
        for j, chip in enumerate(chips):
            copy(1 + j, (*chip, c), me).wait_recv()
            passed[j].start()
        copy(0, sibling, me).wait_recv()
        for j, chip in enumerate(chips):
            copy(4 + j, (*chip, 1 - c), me).wait_recv()
        for cp in first + passed:
            cp.wait_send()
        mine.wait()

    space = pl.BlockSpec(memory_space=pltpu.VMEM) if in_vmem else ANY
    return pl.pallas_call(
        body, name=name, out_shape=jax.ShapeDtypeStruct((N_DEV, R, W), block.dtype),
        in_specs=[space], out_specs=space,
        scratch_shapes=[pltpu.SemaphoreType.DMA((7,)), pltpu.SemaphoreType.DMA((7,)), pltpu.SemaphoreType.DMA],
        compiler_params=pltpu.CompilerParams(vmem_limit_bytes=VMEM_LIMIT_V7X),
    )(block)


def _comm_call(body, arrays, out_shapes, n_sems, *, name):
    return pl.pallas_call(
        body, name=name, out_shape=out_shapes, in_specs=[ANY] * len(arrays), out_specs=[ANY] * len(out_shapes),
        scratch_shapes=[pltpu.SemaphoreType.DMA((n_sems,)), pltpu.SemaphoreType.DMA((n_sems,)),
                        pltpu.SemaphoreType.DMA((len(arrays),))],
    )(*arrays)


def _gather_weights(shards, *, name):
    n = len(shards)

    def body(*refs):
        xs, outs = refs[:n], refs[n:2 * n]
        send_sems, recv_sems, local_sems = refs[2 * n:]
        x, y, c = _mesh_pos()
        me, sibling = (x, y, c), (x, y, 1 - c)
        chips = [(1 - x, y), (x, 1 - y), (1 - x, 1 - y)]
        waits = []
        for i in range(n):
            nl = shards[i].shape[0]
            own = xs[i].at[pl.ds(0, nl), c]

            def slot(px, py, pc, i=i, nl=nl):
                return outs[i].at[pl.ds(0, nl), 2 * px + py, pc]

            def copy(k, blk, to, src=None, i=i, slot=slot):
                return pltpu.make_async_remote_copy(
                    src_ref=slot(*blk) if src is None else src, dst_ref=slot(*blk),
                    send_sem=send_sems.at[7 * i + k], recv_sem=recv_sems.at[7 * i + k], device_id=to,
                    device_id_type=MESH_ID)

            mine = pltpu.make_async_copy(own, slot(*me), local_sems.at[i])
            mine.start()
            first = [copy(0, me, sibling, src=own)]
            first += [copy(1 + j, me, (*chip, c), src=own) for j, chip in enumerate(chips)]
            for cp in first:
                cp.start()
            waits.append((copy, mine, first))
        for copy, mine, first in waits:
            passed = [copy(4 + j, (*chip, c), sibling) for j, chip in enumerate(chips)]
            for j, chip in enumerate(chips):
                copy(1 + j, (*chip, c), me).wait_recv()
                passed[j].start()
            copy(0, sibling, me).wait_recv()
            for j, chip in enumerate(chips):
                copy(4 + j, (*chip, 1 - c), me).wait_recv()
            for cp in first + passed:
                cp.wait_send()
            mine.wait()

    out_shapes = [jax.ShapeDtypeStruct((s.shape[0], N_CHIP) + s.shape[1:], s.dtype) for s in shards]
    return _comm_call(body, shards, out_shapes, 7 * n, name=name)


def _place_own(shard, chip_idx, c_idx, *, name):
    n, _, rows, cols = shard.shape
    tr = _row_tile(rows, cols)

    def body(k_ref, c_ref, x_ref, o_ref):
        o_ref[...] = x_ref[...]

    grid_spec = pltpu.PrefetchScalarGridSpec(
        num_scalar_prefetch=2, grid=(n, rows // tr),
        in_specs=[pl.BlockSpec((None, None, tr, cols), lambda l, i, k_ref, c_ref: (l, c_ref[0], i, 0))],
        out_specs=pl.BlockSpec((None, None, None, tr, cols), lambda l, i, k_ref, c_ref: (l, k_ref[0], c_ref[0], i, 0)))
    return pl.pallas_call(
        body, name=name, grid_spec=grid_spec,
        out_shape=jax.ShapeDtypeStruct((n, N_CHIP, 2, rows, cols), shard.dtype),
        compiler_params=_cparams(("parallel", "parallel")),
    )(chip_idx, c_idx, shard)


def _gather_copies(x_refs, land_refs, send_sems, recv_sems):
    x, y, c = _mesh_pos()
    k_me = 2 * x + y
    targets = [(x, y, 1 - c), (1 - x, y, c), (x, 1 - y, c), (1 - x, 1 - y, c)]
    copies = []
    for i, (x_ref, land_ref) in enumerate(zip(x_refs, land_refs)):
        nl = x_ref.shape[0]
        for j, to in enumerate(targets):
            copies.append(pltpu.make_async_remote_copy(
                src_ref=x_ref.at[pl.ds(0, nl), c], dst_ref=land_ref.at[pl.ds(0, nl), k_me, c],
                send_sem=send_sems.at[4 * i + j], recv_sem=recv_sems.at[4 * i + j], device_id=to,
                device_id_type=MESH_ID))
    return copies


def _split_start(copies_fn, srcs, lands, after, *, name, sems_per_array):
    n = len(srcs)

    def body(*refs):
        send_sems, recv_sems = refs[2 * n + 1], refs[2 * n + 2]
        for cp in copies_fn(refs[:n], refs[n:2 * n], send_sems, recv_sems):
            cp.start()
        refs[-1][...] = jnp.zeros_like(refs[-1])

    operands = [pltpu.with_memory_space_constraint(a, pltpu.HBM) for a in list(srcs) + list(lands)]
    n_sems = sems_per_array * n
    out_shape = ([pltpu.SemaphoreType.DMA((n_sems,)), pltpu.SemaphoreType.DMA((n_sems,))]
                 + [pltpu.HBM(a.shape, a.dtype) for a in operands] + [jax.ShapeDtypeStruct((8, LANES), F32)])
    res = pl.pallas_call(
        body, name=name, out_shape=out_shape, in_specs=[HBM] * (2 * n) + [ANY],
        out_specs=[SEM, SEM] + [HBM] * (2 * n) + [pl.BlockSpec(memory_space=pltpu.VMEM)],
        input_output_aliases={i: 2 + i for i in range(2 * n)},
        compiler_params=pltpu.CompilerParams(has_side_effects=DATAFLOW),
    )(*operands, after)
    return res[0], res[1], list(res[2:2 + n]), list(res[2 + n:2 + 2 * n]), res[-1]


def _split_wait(copies_fn, send_sems, recv_sems, srcs, lands, after, *, name):
    n = len(srcs)

    def body(*refs):
        for cp in copies_fn(refs[:n], refs[n:2 * n], refs[2 * n], refs[2 * n + 1]):
            cp.wait_send()
            cp.wait_recv()

    res = pl.pallas_call(
        body, name=name, out_shape=[pltpu.HBM(a.shape, a.dtype) for a in list(srcs) + list(lands)],
        in_specs=[HBM] * (2 * n) + [SEM, SEM, ANY], out_specs=[HBM] * (2 * n),
        input_output_aliases={i: i for i in range(2 * n)},
        compiler_params=pltpu.CompilerParams(has_side_effects=DATAFLOW),
    )(*srcs, *lands, send_sems, recv_sems, after)
    return list(res[:n]), list(res[n:])


def _gather_forward(lands, *, name):
    n = len(lands)

    def body(*refs):
        xs = refs[:n]
        send_sems, recv_sems, _ = refs[2 * n:]
        x, y, c = _mesh_pos()
        chips = [(1 - x, y), (x, 1 - y), (1 - x, 1 - y)]
        copies = []
        for i in range(n):
            nl = lands[i].shape[0]
            for j, (cx, cy) in enumerate(chips):
                here = xs[i].at[pl.ds(0, nl), 2 * cx + cy, c]
                cp = pltpu.make_async_remote_copy(
                    src_ref=here, dst_ref=here, send_sem=send_sems.at[3 * i + j], recv_sem=recv_sems.at[3 * i + j],
                    device_id=(x, y, 1 - c), device_id_type=MESH_ID)
                cp.start()
                copies.append(cp)
        for cp in copies:
            cp.wait()

    return pl.pallas_call(
        body, name=name, out_shape=[jax.ShapeDtypeStruct(a.shape, a.dtype) for a in lands],
        in_specs=[ANY] * n, out_specs=[ANY] * n, input_output_aliases={i: i for i in range(n)},
        scratch_shapes=[pltpu.SemaphoreType.DMA((3 * n,)), pltpu.SemaphoreType.DMA((3 * n,)),
                        pltpu.SemaphoreType.DMA((1,))],
    )(*lands)


def _pair_copies(g_refs, land_refs, send_sems, recv_sems):
    x, y, c = _mesh_pos()
    copies = []
    for i, (g_ref, land_ref) in enumerate(zip(g_refs, land_refs)):
        nl, ns = g_ref.shape[:2]
        copies.append(pltpu.make_async_remote_copy(
            src_ref=g_ref.at[pl.ds(0, nl), pl.ds(0, ns), 1 - c], dst_ref=land_ref, send_sem=send_sems.at[i],
            recv_sem=recv_sems.at[i], device_id=(x, y, 1 - c), device_id_type=MESH_ID))
    return copies


def _pair_exchange(gs, *, name):
    n = len(gs)

    def body(*refs):
        send_sems, recv_sems, _ = refs[2 * n:]
        copies = _pair_copies(refs[:n], refs[n:2 * n], send_sems, recv_sems)
        for cp in copies:
            cp.start()
        for cp in copies:
            cp.wait()

    out_shapes = [jax.ShapeDtypeStruct(g.shape[:2] + g.shape[3:], g.dtype) for g in gs]
    return _comm_call(body, gs, out_shapes, n, name=name)


def _chip_copies(p_refs, land_refs, send_sems, recv_sems):
    x, y, c = _mesh_pos()
    k_me = 2 * x + y
    chips = [(1 - x, y), (x, 1 - y), (1 - x, 1 - y)]
    copies = []
    for i, (p_ref, land_ref) in enumerate(zip(p_refs, land_refs)):
        nl = p_ref.shape[0]
        for j, (cx, cy) in enumerate(chips):
            copies.append(pltpu.make_async_remote_copy(
                src_ref=p_ref.at[pl.ds(0, nl), 2 * cx + cy], dst_ref=land_ref.at[k_me],
                send_sem=send_sems.at[3 * i + j], recv_sem=recv_sems.at[3 * i + j],
                device_id=(cx, cy, c), device_id_type=MESH_ID))
    return copies


def _chip_landing(ps):
    return [lax.empty((p.shape[1], p.shape[0]) + p.shape[2:], p.dtype) for p in ps]


def _chip_exchange(ps, *, name):
    n = len(ps)

    def body(*refs):
        send_sems, recv_sems, _ = refs[2 * n:]
        copies = _chip_copies(refs[:n], refs[n:2 * n], send_sems, recv_sems)
        for cp in copies:
            cp.start()
        for cp in copies:
            cp.wait()

    out_shapes = [jax.ShapeDtypeStruct((p.shape[1], p.shape[0]) + p.shape[2:], p.dtype) for p in ps]
    return _comm_call(body, ps, out_shapes, 3 * n, name=name)


def _pair_swap(ss, *, name):
    n = len(ss)

    def body(*refs):
        xs, outs = refs[:n], refs[n:2 * n]
        send_sems, recv_sems, _ = refs[2 * n:]
        x, y, c = _mesh_pos()
        copies = []
        for i in range(n):
            cp = pltpu.make_async_remote_copy(src_ref=xs[i], dst_ref=outs[i], send_sem=send_sems.at[i],
                                              recv_sem=recv_sems.at[i], device_id=(x, y, 1 - c),
                                              device_id_type=MESH_ID)
            cp.start()
            copies.append(cp)
        for cp in copies:
            cp.wait()

    out_shapes = [jax.ShapeDtypeStruct(s.shape, s.dtype) for s in ss]
    return _comm_call(body, ss, out_shapes, n, name=name)


def _row_tile(rows, cols):
    tr = rows
    while tr * cols > 256 * 1024 and tr % 16 == 0:
        tr //= 2
    return tr


def _pair_add(g, recv, c_idx, *, name):
    n, ns, _, rows, W = g.shape
    tr = _row_tile(rows, W)

    def body(c_ref, g_ref, r_ref, o_ref):
        o_ref[...] = (g_ref[...] + r_ref[...]).astype(BF16)

    piece = pl.BlockSpec((None, tr, W), lambda p, i, c_ref: (p, i, 0))
    grid_spec = pltpu.PrefetchScalarGridSpec(
        num_scalar_prefetch=1, grid=(n * ns, rows // tr),
        in_specs=[pl.BlockSpec((None, None, tr, W), lambda p, i, c_ref: (p, c_ref[0], i, 0)), piece],
        out_specs=piece)
    out = pl.pallas_call(
        body, name=name, grid_spec=grid_spec, out_shape=jax.ShapeDtypeStruct((n * ns, rows, W), BF16),
        compiler_params=_cparams(("parallel", "parallel")),
    )(c_idx, g.reshape(n * ns, 2, rows, W), recv.reshape(n * ns, rows, W))
    return out.reshape(n, ns, rows, W)


def _sum_pieces(land, own, chip_idx, *, name):
    n, nl, A, W = land.shape
    tr = _row_tile(A, W)

    def body(k_ref, l_ref, o_ref, out_ref):
        acc = jnp.zeros(out_ref.shape, F32)
        for k in range(n):
            acc = acc + jnp.where(k == k_ref[0], o_ref[...], l_ref[k]).astype(F32)
        out_ref[...] = acc

    grid_spec = pltpu.PrefetchScalarGridSpec(
        num_scalar_prefetch=1, grid=(nl, A // tr),
        in_specs=[pl.BlockSpec((n, None, tr, W), lambda l, i, k_ref: (0, l, i, 0)),
                  pl.BlockSpec((None, None, tr, W), lambda l, i, k_ref: (l, k_ref[0], i, 0))],
        out_specs=pl.BlockSpec((None, tr, W), lambda l, i, k_ref: (l, i, 0)))
    return pl.pallas_call(
        body, name=name, grid_spec=grid_spec, out_shape=jax.ShapeDtypeStruct((nl, A, W), F32),
        compiler_params=_cparams(("parallel", "parallel")),
    )(chip_idx, land, own)


SMALL = ("norm_mix_g", "norm_mlp_g", "final_norm_g", "fox_b_f", "mla_q_norm_g", "mla_kv_norm_g")
WEIGHT_ORDER = ("ada_w", "ada_b", "norm_mix_g", "norm_mlp_g", "fox_w_in", "fox_b_f", "fox_w_out", "mla_w_dq",
                "mla_q_norm_g", "mla_w_uq", "mla_w_dkv", "mla_kv_norm_g", "mla_w_ukv", "mla_w_out", "mlp_w1",
                "mlp_w2", "final_norm_g")


def _small_rows(vals, D):
    rows = [vals["norm_mix_g"], vals["norm_mlp_g"], vals["final_norm_g"][None, :]]
    for n in ("fox_b_f", "mla_q_norm_g", "mla_kv_norm_g"):
        flat = vals[n].reshape(-1)
        assert flat.shape[0] <= D
        rows.append(jnp.pad(flat, (0, D - flat.shape[0]))[None, :])
    return jnp.concatenate(rows, axis=0)


def _small_unrows(rows, shapes):
    L = shapes["norm_mix_g"][0]
    out = {"norm_mix_g": rows[0:L], "norm_mlp_g": rows[L:2 * L], "final_norm_g": rows[2 * L]}
    for k, n in enumerate(("fox_b_f", "mla_q_norm_g", "mla_kv_norm_g")):
        size = int(np.prod(shapes[n]))
        out[n] = rows[2 * L + 1 + k, :size].reshape(shapes[n])
    return out


def kernel(x, c, positions, ada_w, ada_b, norm_mix_g, norm_mlp_g, fox_w_in, fox_b_f, fox_w_out, mla_w_dq, mla_q_norm_g, mla_w_uq, mla_w_dkv, mla_kv_norm_g, mla_w_ukv, mla_w_out, mlp_w1, mlp_w2, final_norm_g, loss_target, m_ada_w, m_ada_b, m_norm_mix_g, m_norm_mlp_g, m_fox_w_in, m_fox_b_f, m_fox_w_out, m_mla_w_dq, m_mla_q_norm_g, m_mla_w_uq, m_mla_w_dkv, m_mla_kv_norm_g, m_mla_w_ukv, m_mla_w_out, m_mlp_w1, m_mlp_w2, m_final_norm_g, v_ada_w, v_ada_b, v_norm_mix_g, v_norm_mlp_g, v_fox_w_in, v_fox_b_f, v_fox_w_out, v_mla_w_dq, v_mla_q_norm_g, v_mla_w_uq, v_mla_w_dkv, v_mla_kv_norm_g, v_mla_w_ukv, v_mla_w_out, v_mlp_w1, v_mlp_w2, v_final_norm_g):
    args = dict(locals())
    wts = {n: args[n] for n in WEIGHT_ORDER}
    mom = {n: args["m_" + n] for n in WEIGHT_ORDER}
    var = {n: args["v_" + n] for n in WEIGHT_ORDER}
    Bl, S, D = x.shape
    T = Bl * S
    L = ada_w.shape[0]
    C = ada_w.shape[2]
    mx, my, mc = _mesh_pos()
    chip = 2 * mx + my
    dev = 4 * mx + 2 * my + mc
    c_idx = jnp.reshape(mc, (1,)).astype(jnp.int32)
    chip_idx = jnp.reshape(chip, (1,)).astype(jnp.int32)
    small = {n: wts[n] for n in SMALL}
    L2, q_cols = mla_q_norm_g.shape
    n_fox_heads = fox_b_f.shape[1]

    shards = _shard_layouts(wts)
    groups = _comm_groups(L, L2)
    slots = _layer_slots(groups)

    def row_halves(a):
        return a.reshape(a.shape[:-2] + (2, a.shape[-2] // 2, a.shape[-1]))

    def whole_rows(a):
        return a.reshape(a.shape[:2] + (a.shape[2] * a.shape[3], a.shape[4]))

    part = {g: [row_halves(shards[n][s:s + cnt]) for n, s, cnt in entries] for g, entries in groups.items()}
    mix0 = _gather_weights(part["mix0"], name="gather_mix0")
    gather_sems, after = {}, mix0[0]
    for group in ("mlp0", "rest"):
        placed = [_place_own(a, chip_idx, c_idx, name=f"gather_place_{group}_{n}")
                  for a, (n, _, _) in zip(part[group], groups[group])]
        gather_sems[group] = _split_start(_gather_copies, part[group], placed, after, name=f"gather_{group}_start",
                                          sems_per_array=4)
        after = gather_sems[group][4]

    def layer_weights(w, group, arrays):
        for (n, s, cnt), a in zip(groups[group], arrays):
            for key, view in _weight_views(n, whole_rows(a), D, n_fox_heads).items():
                for l in range(cnt):
                    w[key][s + l] = (view, l)

    w = {key: [None] * L2 for key in ("fox_qkv", "fox_f", "fox_out", "mla_down", "mla_uq", "mla_ukv", "mla_out")}
    w.update({key: [None] * L for key in ("mlp_w1", "mlp_w2")})
    layer_weights(w, "mix0", mix0)

    def gathered_now(group):
        def hook(x_now, w):
            _, landed = _split_wait(_gather_copies, *gather_sems[group][:4], x_now, name=f"gather_{group}_wait")
            layer_weights(w, group, _gather_forward(landed, name=f"gather_{group}_forward"))
            return w
        return hook

    c_pad = jnp.concatenate([c, jnp.pad(mla_q_norm_g, ((0, 8 - Bl - L2), (0, D - q_cols)))], axis=0)
    c8 = _all_gather8(c_pad, name="gather_c", in_vmem=True)
    c_all = c8[:, :Bl].reshape(N_DEV * Bl, D)
    qg4 = c8.reshape(N_CHIP, 2, 8, D)[:, 0, Bl:Bl + L2, :q_cols]
    small["mla_q_norm_g"] = jnp.transpose(qg4, (1, 0, 2)).reshape(L2, N_CHIP * q_cols)
    ada_b_cols = lax.dynamic_slice_in_dim(ada_b, chip * C, C, axis=1)[:, None, :]
    mod_cols = _ada_fwd(c_all, ada_w, ada_b_cols)
    mod8 = _all_gather8(mod_cols.reshape(L * N_DEV * Bl, C), name="gather_mod", in_vmem=True)
    mod4 = mod8.reshape(N_CHIP, 2, L, N_DEV * Bl, C)[:, 0]
    mod_me = lax.dynamic_slice_in_dim(mod4, dev * Bl, Bl, axis=2)
    mod = jnp.transpose(mod_me, (1, 2, 0, 3)).reshape(L, Bl, 6, D)
    mod = jnp.transpose(mod, (0, 2, 1, 3))[:, :, :, None, :]

    w.update(_small_layouts(small))
    mod = mod + after[0, 0]
    pending = {}

    def grad_pieces(group, g_now):
        out = []
        for n, s, cnt in groups[group]:
            qkv_f = [(g_now["fox_qkv"][j], g_now["fox_f"][j]) for j in range(s, s + cnt)] if n == "fox_in" else None
            stacked_g = None if n == "fox_in" else g_now[n][group]
            out.append(row_halves(_grad_pieces(n, stacked_g, qkv_f, n_fox_heads, N_CHIP)))
        return out

    def pair_added(group, big, sibling):
        return [_pair_add(a, r, c_idx, name=f"grad_pair_add_{group}_{n}")
                for (n, _, _), a, r in zip(groups[group], big, sibling)]

    def exchange_start(group, ps, after=None):
        pending[group] = _split_start(_chip_copies, ps, _chip_landing(ps), chip_idx if after is None else after,
                                      name=f"grad_exchange_{group}_start", sems_per_array=3)
        return pending[group][4]

    def bwd_layer0(g_now):
        big = grad_pieces("rest", g_now)
        landing = [lax.empty(a.shape[:2] + a.shape[3:], a.dtype) for a in big]
        pending["rest_pair"] = _split_start(_pair_copies, big, landing, chip_idx, name="grad_pair_rest_start",
                                            sems_per_array=1)
        return pending["rest_pair"][4]

    def bwd_mix0(g_now):
        send_sems, recv_sems, big, landed, _ = pending["rest_pair"]
        big, landed = _split_wait(_pair_copies, send_sems, recv_sems, big, landed, g_now["mlp_w1"]["mlp0"],
                                  name="grad_pair_rest_wait")
        started = exchange_start("rest", pair_added("rest", big, landed))
        big = grad_pieces("mlp0", g_now)
        return exchange_start("mlp0", pair_added("mlp0", big, _pair_exchange(big, name="grad_pair_exchange_mlp0")),
                              after=started)

    half = ROPE_DIM // 2
    inv_freq = ROPE_THETA ** (-jnp.arange(0, ROPE_DIM, 2, dtype=F32) / ROPE_DIM)
    lane = np.arange(LANES)
    inv_freq_row = jnp.tile(inv_freq, LANES // half)[None, :]
    sign_row = jnp.asarray(np.where(lane < 2 * ROPE_DIM, np.where(lane % ROPE_DIM < half, -1.0, 1.0), 0.0), F32)[None, :]
    pos_f = positions.astype(F32).reshape(T, 1)
    loss_row, grad_x, dmod, g = _local_step(x.reshape(T, D), loss_target.reshape(T, D), pos_f, inv_freq_row, sign_row,
                                            mod, w, slots, S=S,
                                            hooks={"fwd_mlp0": gathered_now("mlp0"), "fwd_layer1": gathered_now("rest"),
                                                   "bwd_layer0": bwd_layer0, "bwd_mix0": bwd_mix0})
    g_small = _small_grads(g, n_fox_heads)
    big = grad_pieces("mix0", g)
    exchange_start("mix0", pair_added("mix0", big, _pair_exchange(big, name="grad_pair_exchange_mix0")))

    Rs = -(-(2 * L + 5) // 8) * 8
    srows = jnp.concatenate([_small_rows(g_small, D), jnp.pad(loss_row, ((0, 0), (0, D - LANES)))], axis=0)
    srows = jnp.pad(srows, ((0, Rs - srows.shape[0]), (0, 0)))
    drows = jnp.transpose(dmod[:, :, :, 0, :], (2, 0, 1, 3)).reshape(Bl * L * 6, D)
    both8 = _all_gather8(jnp.concatenate([drows, srows], axis=0), name="gather_small", in_vmem=True)
    dm8 = both8[:, :Bl * L * 6].reshape(N_DEV, Bl, L * 6, D)
    sm8 = both8[:, Bl * L * 6:]
    adb_rows, small_sum = _sum_gathered(dm8, sm8)
    grad_ada_b = adb_rows.reshape(L, 6 * D)
    loss = small_sum[2 * L + 4, 0]
    small_shapes = {n: (wts[n].shape if n != "mla_q_norm_g" else (wts[n].shape[0], N_CHIP * q_cols)) for n in SMALL}
    gs = _small_unrows(small_sum, small_shapes)
    gs["mla_q_norm_g"] = lax.dynamic_slice_in_dim(gs["mla_q_norm_g"], chip * q_cols, q_cols, axis=1)

    dmod16 = jnp.transpose(dm8.reshape(N_DEV, Bl, L, 6 * D), (2, 0, 1, 3)).reshape(L, N_DEV * Bl, 6 * D)
    dmod_cols = lax.dynamic_slice_in_dim(dmod16, chip * C, C, axis=2)
    grad_ada_w = _ada_bwd(c_all, dmod_cols)

    grads = dict(gs)
    grads["ada_w"] = grad_ada_w
    grads["ada_b"] = grad_ada_b
    delta, new_m, new_v = {}, {}, {}
    for n in ("ada_w", "ada_b"):
        delta[n], new_m[n], new_v[n] = _adamw(wts[n], grads[n], mom[n], var[n], name=f"adamw_{n}")
    shard_small_shapes = {n: wts[n].shape for n in SMALL}
    packs = [jnp.pad(_small_rows({n: src[n] for n in SMALL}, D), ((0, Rs - 2 * L - 4), (0, 0)))
             for src in (wts, grads, mom, var)]
    for dst, rows in zip((delta, new_m, new_v), _adamw(*packs, name="adamw_small")):
        dst.update(_small_unrows(rows, shard_small_shapes))

    halves = {}
    for group, after in (("rest", grad_x), ("mlp0", grad_x), ("mix0", delta["ada_w"])):
        send_sems, recv_sems, ps, lands, _ = pending[group]
        ps, lands = _split_wait(_chip_copies, send_sems, recv_sems, ps, lands, after, name=f"grad_exchange_{group}_wait")
        sums = [_sum_pieces(ld, p, chip_idx, name=f"grad_sum_{group}_{n}")
                for (n, _, _), ld, p in zip(groups[group], lands, ps)]
        swapped = _pair_swap(sums, name=f"grad_pair_swap_{group}")
        for (n, _, _), a, b in zip(groups[group], sums, swapped):
            halves[(n, group)] = (a, b)

    def all_layers(n, which):
        return jnp.concatenate([halves[(n, grp)][which] for grp in groups if (n, grp) in halves], axis=0)

    own = {n: all_layers(n, 0) for n in GATHERED}
    peer = {n: all_layers(n, 1) for n in GATHERED}
    for nat, n in (("fox_w_in", "fox_in"), ("fox_w_out", "fox_out"), ("mla_w_out", "mla_out"), ("mlp_w1", "mlp_w1"),
                   ("mlp_w2", "mlp_w2")):
        cols = wts[nat].shape[-1]
        res = _adamw_halves(_pad_lanes(wts[nat]), own[n], peer[n], _pad_lanes(mom[nat]), _pad_lanes(var[nat]), c_idx,
                            name=f"adamw_{nat}")
        grads[nat], delta[nat], new_m[nat], new_v[nat] = (a[..., :cols] for a in res)
    joined = {n: jnp.concatenate([jnp.where(mc == 0, own[n], peer[n]), jnp.where(mc == 0, peer[n], own[n])], axis=1)
              for n in ("mla_down", "mla_uq", "mla_ukv")}
    rq = mla_w_dq.shape[-1]
    grads["mla_w_dq"] = joined["mla_down"][:, :, :rq]
    grads["mla_w_dkv"] = joined["mla_down"][:, :, rq:rq + KV_RANK + ROPE_DIM]
    grads["mla_w_uq"] = jax.vmap(_uq_from_pairs)(joined["mla_uq"])
    grads["mla_w_ukv"] = jax.vmap(_ukv_from_pairs)(joined["mla_ukv"])
    for n in ("mla_w_dq", "mla_w_dkv", "mla_w_uq", "mla_w_ukv"):
        delta[n], new_m[n], new_v[n] = _adamw(wts[n], grads[n], mom[n], var[n], name=f"adamw_{n}")

    return (loss, grad_x.reshape(Bl, S, D), *[grads[n] for n in WEIGHT_ORDER], *[delta[n] for n in WEIGHT_ORDER],
            *[new_m[n] for n in WEIGHT_ORDER], *[new_v[n] for n in WEIGHT_ORDER])
```

```python
import functools

import numpy as np
import jax
import jax.numpy as jnp
from jax import lax
from jax.experimental import pallas as pl
from jax.experimental.pallas import tpu as pltpu

F32 = jnp.float32
BF16 = jnp.bfloat16
MESH_ID = pl.DeviceIdType.MESH

NORM_EPS = 1e-6
ROPE_THETA = 10000.0
HEAD_DIM = 64
ROPE_DIM = 32
KV_RANK = 128
FOX_EXTRA = 6
PAIR_Q = 256
PAIR_KV = 384
LANES = 128
ADAM_LR = 0.001
ADAM_B1 = 0.9
ADAM_B2 = 0.999
ADAM_EPS = 1e-08
ADAM_WD = 0.01
ADAM_STEP = 10
VMEM_LIMIT_V7X = 48 * 1024 * 1024
MM_VMEM_BUDGET = 36 * 1024 * 1024
NEG_BIG = -1e30
ATTN_UNROLL = 4
ATTN_BLOCK = 256
ATTN_Q_ROWS = 512
ATTN_K_ROWS = 512

BIG_WEIGHTS = (("fox_w_in", 2), ("fox_w_out", 1), ("mla_w_dq", 1), ("mla_w_uq", 2), ("mla_w_dkv", 1),
               ("mla_w_ukv", 2), ("mla_w_out", 1), ("mlp_w1", 2), ("mlp_w2", 1))


def _cparams(sem=None):
    return pltpu.CompilerParams(dimension_semantics=sem, vmem_limit_bytes=VMEM_LIMIT_V7X)


def _tile(n, want):
    if n <= want:
        return n
    for t in range(want - want % LANES, 0, -LANES):
        if n % t == 0:
            return t
    raise ValueError((n, want))


def _mm(a, b, mode, *, name, out_dtypes=(F32,), epilogue=None, extras=(), rowvecs=(), tables=(),
        seq=None, a_off=0, a_sz=None, b_layer=None, out_stack=None, out_split=0, out_t=(), tm=1024, tn=1024,
        tk=2048):
    if isinstance(b, (list, tuple)):
        b, b_layer = b[b_layer]
    b_rows, b_cols = b.shape[-2], b.shape[-1]
    n_split = b.shape[1] if b.ndim == 4 else 1
    assert mode in ("nn", "nt")
    if mode == "nn":
        M, K, N = a.shape[0], b_rows, b_cols * n_split
    else:
        M, K, N = a.shape[0], b_cols * n_split, b_rows
    assert a_sz is None or a_sz == K
    tm = _tile(seq if rowvecs else M, tm)
    n_piece = N // max(out_split, n_split if mode == "nn" else 1, 1)
    tn = _tile(n_piece, tn)
    tk = _tile(K // (n_split if mode == "nt" else 1), tk)
    ne, nr, nt_ = len(extras), len(rowvecs), len(tables)
    no = len(out_dtypes)

    def vmem_estimate():
        blocks = tm * tk * a.dtype.itemsize + tk * tn * b.dtype.itemsize
        blocks += tm * tn * (sum(e.dtype.itemsize for e in extras) + sum(jnp.dtype(d).itemsize for d in out_dtypes))
        return 2 * blocks + 2 * tm * tn * 4

    while vmem_estimate() > MM_VMEM_BUDGET and max(tm, tn) > 256:
        if tn >= tm:
            tn //= 2
        else:
            tm //= 2
    nk = K // tk

    assert a_off % tk == 0
    a_spec = pl.BlockSpec((tm, tk), lambda i, j, k: (i, k + a_off // tk))
    dims = (((1,), (0,)), ((), ())) if mode == "nn" else (((1,), (1,)), ((), ()))
    lead = () if b.ndim == 2 else (b_layer,)
    sq = (None,) * (b.ndim - 2)
    if mode == "nt":
        kb = b_cols // tk
        if b.ndim == 4:
            b_spec = pl.BlockSpec(sq + (tn, tk), lambda i, j, k: lead + (k // kb, j, k % kb))
        else:
            b_spec = pl.BlockSpec(sq + (tn, tk), lambda i, j, k: lead + (j, k))
    else:
        nb = b_cols // tn
        if b.ndim == 4:
            b_spec = pl.BlockSpec(sq + (tk, tn), lambda i, j, k: lead + (j // nb, k, j % nb))
        else:
            b_spec = pl.BlockSpec(sq + (tk, tn), lambda i, j, k: lead + (k, j))
    in_specs = [a_spec, b_spec]
    in_specs += [pl.BlockSpec((tm, tn), lambda i, j, k: (i, j)) for _ in extras]
    if rowvecs:
        assert seq % tm == 0
        per = seq // tm
        in_specs += [pl.BlockSpec((None, 1, tn), lambda i, j, k: (i // per, 0, j)) for _ in rowvecs]
    in_specs += [pl.BlockSpec((tm, LANES), lambda i, j, k: (i, 0)) for _ in tables]
    operands = [a, b, *extras, *rowvecs, *tables]
    aliases = {}
    transposed = tuple(out_t) + (False,) * (no - len(out_t))
    if out_stack is None:
        out_specs = [pl.BlockSpec((tn, tm), lambda i, j, k: (j, i)) if t else pl.BlockSpec((tm, tn), lambda i, j, k: (i, j))
                     for t in transposed]
        out_shape = [jax.ShapeDtypeStruct((N, M) if t else (M, N), d) for d, t in zip(out_dtypes, transposed)]
    else:
        prev, layer, n_layers = out_stack
        assert no == 1
        if out_split:
            ob = n_piece // tn
            out_specs = [pl.BlockSpec((None, None, tm, tn), lambda i, j, k: (layer, j // ob, i, j % ob))]
            out_shape = [jax.ShapeDtypeStruct((n_layers, out_split, M, n_piece), out_dtypes[0])]
        else:
            out_specs = [pl.BlockSpec((None, tm, tn), lambda i, j, k: (layer, i, j))]
            out_shape = [jax.ShapeDtypeStruct((n_layers, M, N), out_dtypes[0])]
        if prev is not None:
            in_specs.append(pl.BlockSpec(memory_space=pl.ANY))
            aliases = {len(operands): 0}
            operands.append(prev)
    n_in = len(operands)

    def body(*refs):
        a_ref, b_ref = refs[0], refs[1]
        side = refs[2:2 + ne + nr + nt_]
        outs = refs[n_in:n_in + no]

        def finish(acc):
            res = (acc,) if epilogue is None else epilogue(acc, *[r[...] for r in side])
            for o_ref, r, t in zip(outs, res, transposed):
                o_ref[...] = (r.T if t else r).astype(o_ref.dtype)

        part = lax.dot_general(a_ref[...].astype(BF16), b_ref[...].astype(BF16), dims,
                               preferred_element_type=F32)
        if nk == 1:
            finish(part)
        else:
            acc_ref = refs[-1]
            k = pl.program_id(2)

            @pl.when(k == 0)
            def _():
                acc_ref[...] = part

            @pl.when(k > 0)
            def _():
                acc_ref[...] += part

            @pl.when(k == nk - 1)
            def _():
                finish(acc_ref[...])

    res = pl.pallas_call(
        body, name=name, grid=(M // tm, N // tn, nk), in_specs=in_specs, out_specs=out_specs,
        out_shape=out_shape, scratch_shapes=[pltpu.VMEM((tm, tn), F32)] if nk > 1 else [],
        input_output_aliases=aliases,
        compiler_params=_cparams(("parallel", "parallel", "arbitrary")),
    )(*operands)
    return res[0] if no == 1 else tuple(res)


def _rope128(x, cos_t, sin_s):
    lane = lax.broadcasted_iota(jnp.int32, x.shape, 1)
    first = (lane % ROPE_DIM) < (ROPE_DIM // 2)
    swapped = jnp.where(first, pltpu.roll(x, LANES - ROPE_DIM // 2, 1), pltpu.roll(x, ROPE_DIM // 2, 1))
    return x * cos_t + swapped * sin_s


def _rope_pairs(acc, cos_t, sin_s, sign):
    parts = []
    for p in range(acc.shape[1] // PAIR_Q):
        parts.append(acc[:, p * PAIR_Q:p * PAIR_Q + LANES])
        parts.append(_rope128(acc[:, p * PAIR_Q + LANES:(p + 1) * PAIR_Q], cos_t, sign * sin_s))
    return jnp.concatenate(parts, axis=1)


def _rope_tables(pos_f, inv_freq_row, sign_row):
    T = pos_f.shape[0]
    tt = _tile(T, 512)

    def body(p_ref, f_ref, s_ref, cos_ref, sin_ref):
        ang = p_ref[...] * f_ref[...]
        cos_ref[...] = jnp.cos(ang)
        sin_ref[...] = jnp.sin(ang) * s_ref[...]

    return pl.pallas_call(
        body, name="rope_tables", grid=(T // tt,),
        in_specs=[pl.BlockSpec((tt, 1), lambda i: (i, 0)), pl.BlockSpec((1, LANES), lambda i: (0, 0)),
                  pl.BlockSpec((1, LANES), lambda i: (0, 0))],
        out_specs=[pl.BlockSpec((tt, LANES), lambda i: (i, 0))] * 2,
        out_shape=[jax.ShapeDtypeStruct((T, LANES), F32)] * 2,
        compiler_params=_cparams(("parallel",)),
    )(pos_f, inv_freq_row, sign_row)


def _unrope(dqx, cos_t, sin_s):
    T, W = dqx.shape
    tt = _tile(T, 512)

    def body(d_ref, c_ref, s_ref, o_ref):
        o_ref[...] = _rope_pairs(d_ref[...].astype(F32), c_ref[...], s_ref[...], -1.0).astype(BF16)

    return pl.pallas_call(
        body, name="mla_unrope", grid=(T // tt,),
        in_specs=[pl.BlockSpec((tt, W), lambda i: (i, 0)), pl.BlockSpec((tt, LANES), lambda i: (i, 0)),
                  pl.BlockSpec((tt, LANES), lambda i: (i, 0))],
        out_specs=pl.BlockSpec((tt, W), lambda i: (i, 0)),
        out_shape=jax.ShapeDtypeStruct((T, W), BF16),
        compiler_params=_cparams(("parallel",)),
    )(dqx, cos_t, sin_s)


def _row_specs(tt, D, per, n):
    return [pl.BlockSpec((None, 1, D), lambda i: (i // per, 0, 0)) for _ in range(n)]


def _norm_mod(x, gain, sc, sh, *, S, name):
    T, D = x.shape
    tt = _tile(S, 512)
    per = S // tt

    def body(x_ref, g_ref, sc_ref, sh_ref, h_ref, ht_ref):
        xv = x_ref[...]
        r = lax.rsqrt(jnp.mean(xv * xv, axis=-1, keepdims=True) + NORM_EPS)
        h = (xv * r) * g_ref[...] * (1.0 + sc_ref[...]) + sh_ref[...]
        h_ref[...] = h.astype(BF16)
        ht_ref[...] = h.T.astype(BF16)

    return pl.pallas_call(
        body, name=name, grid=(T // tt,),
        in_specs=[pl.BlockSpec((tt, D), lambda i: (i, 0)), pl.BlockSpec((1, D), lambda i: (0, 0))]
        + _row_specs(tt, D, per, 2),
        out_specs=[pl.BlockSpec((tt, D), lambda i: (i, 0)), pl.BlockSpec((D, tt), lambda i: (0, i))],
        out_shape=[jax.ShapeDtypeStruct((T, D), BF16), jax.ShapeDtypeStruct((D, T), BF16)],
        compiler_params=_cparams(("parallel",)),
    )(x, gain, sc, sh)


def _norm_mod_bwd(x, dh, dres, gain, sc, *, S, name):
    T, D = x.shape
    B = T // S
    tt = _tile(S, 512)
    per = S // tt

    def body(x_ref, dh_ref, dres_ref, g_ref, sc_ref, dx_ref, dsh_ref, dsc_ref, dg_ref):
        i = pl.program_id(0)
        xv = x_ref[...]
        dhv = dh_ref[...].astype(F32)
        r = lax.rsqrt(jnp.mean(xv * xv, axis=-1, keepdims=True) + NORM_EPS)
        n = xv * r
        g = g_ref[...]
        one_sc = 1.0 + sc_ref[...]
        dn = dhv * (g * one_sc)
        dx_ref[...] = dres_ref[...] + r * (dn - n * jnp.mean(dn * n, axis=-1, keepdims=True))
        dhn = dhv * n

        @pl.when(i % per == 0)
        def _():
            dsh_ref[...] = jnp.zeros_like(dsh_ref)
            dsc_ref[...] = jnp.zeros_like(dsc_ref)

        @pl.when(i == 0)
        def _():
            dg_ref[...] = jnp.zeros_like(dg_ref)

        dsh_ref[...] += jnp.sum(dhv, axis=0, keepdims=True)
        dsc_ref[...] += jnp.sum(dhn, axis=0, keepdims=True) * g
        dg_ref[...] += jnp.sum(dhn, axis=0, keepdims=True) * one_sc

    return pl.pallas_call(
        body, name=name, grid=(T // tt,),
        in_specs=[pl.BlockSpec((tt, D), lambda i: (i, 0))] * 3 + [pl.BlockSpec((1, D), lambda i: (0, 0))]
        + _row_specs(tt, D, per, 1),
        out_specs=[pl.BlockSpec((tt, D), lambda i: (i, 0))] + _row_specs(tt, D, per, 2)
        + [pl.BlockSpec((1, D), lambda i: (0, 0))],
        out_shape=[jax.ShapeDtypeStruct((T, D), F32), jax.ShapeDtypeStruct((B, 1, D), F32),
                   jax.ShapeDtypeStruct((B, 1, D), F32), jax.ShapeDtypeStruct((1, D), F32)],
        compiler_params=_cparams(("arbitrary",)),
    )(x, dh, dres, gain, sc)


def _gate_bwd(dx, y, g, *, S, name):
    T, D = dx.shape
    B = T // S
    tt = _tile(S, 512)
    per = S // tt

    def body(dx_ref, y_ref, g_ref, dy_ref, dg_ref):
        i = pl.program_id(0)
        dxv = dx_ref[...]
        dy_ref[...] = (dxv * g_ref[...]).astype(BF16)

        @pl.when(i % per == 0)
        def _():
            dg_ref[...] = jnp.zeros_like(dg_ref)

        dg_ref[...] += jnp.sum(dxv * y_ref[...], axis=0, keepdims=True)

    return pl.pallas_call(
        body, name=name, grid=(T // tt,),
        in_specs=[pl.BlockSpec((tt, D), lambda i: (i, 0))] * 2 + _row_specs(tt, D, per, 1),
        out_specs=[pl.BlockSpec((tt, D), lambda i: (i, 0))] + _row_specs(tt, D, per, 1),
        out_shape=[jax.ShapeDtypeStruct((T, D), BF16), jax.ShapeDtypeStruct((B, 1, D), F32)],
        compiler_params=_cparams(("arbitrary",)),
    )(dx, y, g)


def _final_loss(x, target, gain):
    T, D = x.shape
    tt = _tile(T, 512)

    def body(x_ref, t_ref, g_ref, dx_ref, dg_ref, loss_ref):
        i = pl.program_id(0)
        xv = x_ref[...]
        r = lax.rsqrt(jnp.mean(xv * xv, axis=-1, keepdims=True) + NORM_EPS)
        n = xv * r
        g = g_ref[...]
        err = n * g - t_ref[...]
        dy = err * (1.0 / D)
        dn = dy * g
        dx_ref[...] = r * (dn - n * jnp.mean(dn * n, axis=-1, keepdims=True))

        @pl.when(i == 0)
        def _():
            dg_ref[...] = jnp.zeros_like(dg_ref)
            loss_ref[...] = jnp.zeros_like(loss_ref)

        dg_ref[...] += jnp.sum(dy * n, axis=0, keepdims=True)
        loss_ref[...] += jnp.sum(jnp.sum(err * err, axis=-1, keepdims=True), axis=0, keepdims=True) * (0.5 / D)

    return pl.pallas_call(
        body, name="final_loss", grid=(T // tt,),
        in_specs=[pl.BlockSpec((tt, D), lambda i: (i, 0))] * 2 + [pl.BlockSpec((1, D), lambda i: (0, 0))],
        out_specs=[pl.BlockSpec((tt, D), lambda i: (i, 0)), pl.BlockSpec((1, D), lambda i: (0, 0)),
                   pl.BlockSpec((1, LANES), lambda i: (0, 0))],
        out_shape=[jax.ShapeDtypeStruct((T, D), F32), jax.ShapeDtypeStruct((1, D), F32),
                   jax.ShapeDtypeStruct((1, LANES), F32)],
        compiler_params=_cparams(("arbitrary",)),
    )(x, target, gain)


def _head_masks(ew):
    lane = lax.broadcasted_iota(jnp.int32, (1, PAIR_Q), 1)
    m0 = (lane < HEAD_DIM) | ((lane >= LANES) & (lane < LANES + ew))
    m1 = ((lane >= HEAD_DIM) & (lane < LANES)) | ((lane >= LANES + ew) & (lane < LANES + 2 * ew))
    return m0, m1


def _dot_nt(a, b):
    return lax.dot_general(a, b, (((1,), (1,)), ((), ())), preferred_element_type=F32)


def _dot_tn(a, b):
    return lax.dot_general(a, b, (((0,), (0,)), ((), ())), preferred_element_type=F32)


def _lane_halves(x, op):
    acc = x[:, 0:LANES]
    for g in range(1, x.shape[1] // LANES):
        acc = op(acc, x[:, g * LANES:(g + 1) * LANES])
    return acc


def _head_rows(cols_lane_replicated):
    t = cols_lane_replicated.T
    sub = lax.broadcasted_iota(jnp.int32, (8, t.shape[1]), 0)
    return jnp.where(sub == 1, t[HEAD_DIM:HEAD_DIM + 8], t[0:8])


def _attn_fwd(qx, kvx, *, S, scale, ew, name):
    T = qx.shape[0]
    P = qx.shape[1] // PAIR_Q
    B = T // S
    tk = _tile(S, ATTN_BLOCK)
    tq = _tile(S, ATTN_Q_ROWS)
    nq = S // tq
    per = tq // tk

    def body(q_ref, kv_ref, o_ref, lse_ref, ot_ref, m_sc, l_sc, acc_sc):
        qi = pl.program_id(2)
        q = q_ref[...]
        masks = _head_masks(ew)
        qh = [jnp.where(m, q, jnp.zeros_like(q)) for m in masks]

        def logits(h, k, diagonal):
            s = _dot_nt(qh[h], k)
            if scale != 1.0:
                s = s * scale
            if diagonal is None:
                return s
            row = lax.broadcasted_iota(jnp.int32, s.shape, 0)
            col = lax.broadcasted_iota(jnp.int32, s.shape, 1)
            return jnp.where(col + diagonal * tk <= row, s, NEG_BIG)

        def trip(first, count, n_diagonal=0):
            rows = [pl.ds(pl.multiple_of((first + u) * tk, tk), tk) for u in range(count)]
            diag = [None] * (count - n_diagonal) + list(range(n_diagonal))
            for h in range(2):
                ss = [logits(h, kv_ref[rows[u], 0:PAIR_Q], diag[u]) for u in range(count)]
                m_prev = m_sc[h]
                m_elem = m_prev
                for s in ss:
                    m_elem = jnp.maximum(m_elem, _lane_halves(s, jnp.maximum))
                m_new = jnp.broadcast_to(jnp.max(m_elem, axis=1, keepdims=True), (tq, LANES))
                alpha = jnp.exp(m_prev - m_new)
                l = alpha * l_sc[h]
                acc = alpha * acc_sc[h]
                for u, s in enumerate(ss):
                    p = jnp.concatenate([jnp.exp(s[:, g * LANES:(g + 1) * LANES] - m_new)
                                         for g in range(tk // LANES)], axis=1)
                    l = l + _lane_halves(p, jnp.add)
                    acc = acc + jnp.dot(p.astype(BF16), kv_ref[rows[u], PAIR_Q:PAIR_KV], preferred_element_type=F32)
                m_sc[h] = m_new
                l_sc[h] = l
                acc_sc[h] = acc

        m_sc[...] = jnp.full(m_sc.shape, NEG_BIG, F32)
        l_sc[...] = jnp.zeros_like(l_sc)
        acc_sc[...] = jnp.zeros_like(acc_sc)

        def loop_body(t, carry):
            trip(t * ATTN_UNROLL, ATTN_UNROLL)
            return carry

        below = qi * per
        lax.fori_loop(0, below // ATTN_UNROLL, loop_body, 0)
        for left in range(0, ATTN_UNROLL, per):
            @pl.when(below % ATTN_UNROLL == left)
            def _(left=left):
                trip(below - left, left + per, n_diagonal=per)

        lane = lax.broadcasted_iota(jnp.int32, (tq, LANES), 1)
        lo = lane < HEAD_DIM
        l = [jnp.sum(l_sc[h], axis=1, keepdims=True) for h in range(2)]
        o = jnp.where(lo, acc_sc[0] / l[0], acc_sc[1] / l[1])
        o_ref[...] = o.astype(BF16)
        ot_ref[...] = o.T.astype(BF16)
        lse = jnp.where(lo, m_sc[0] + jnp.log(l[0]), m_sc[1] + jnp.log(l[1]))
        for r in range(per):
            lse_ref[r] = _head_rows(lse[r * tk:(r + 1) * tk])

    return pl.pallas_call(
        body, name=name, grid=(B, P, nq),
        in_specs=[pl.BlockSpec((tq, PAIR_Q), lambda b, p, i: (b * nq + i, p)),
                  pl.BlockSpec((S, PAIR_KV), lambda b, p, i: (b, p))],
        out_specs=[pl.BlockSpec((tq, LANES), lambda b, p, i: (b * nq + i, p)),
                   pl.BlockSpec((per, None, 8, tk), lambda b, p, i: (b * nq + i, p, 0, 0)),
                   pl.BlockSpec((LANES, tq), lambda b, p, i: (p, b * nq + i))],
        out_shape=[jax.ShapeDtypeStruct((T, P * LANES), BF16), jax.ShapeDtypeStruct((T // tk, P, 8, tk), F32),
                   jax.ShapeDtypeStruct((P * LANES, T), BF16)],
        scratch_shapes=[pltpu.VMEM((2, tq, LANES), F32)] * 3,
        compiler_params=_cparams(("parallel", "parallel", "arbitrary")),
    )(qx, kvx)


def _attn_bwd(qx, kvx, o, lse, do, *, S, scale, ew, name, bias_grad=False):
    T = qx.shape[0]
    P = qx.shape[1] // PAIR_Q
    B = T // S
    tq = _tile(S, ATTN_BLOCK)
    tk = _tile(S, ATTN_K_ROWS)
    nq = S // tq
    nk = S // tk
    per = tk // tq

    def body(q_ref, kv_ref, o_ref, lse_ref, do_ref, dq_ref, dkv_ref, *rest):
        kj = pl.program_id(2)
        if bias_grad:
            csum_ref, rsum_ref, dq_sc, delta_sc, dk_sc, dv_sc, cs_sc = rest
            cs_sc[...] = jnp.zeros_like(cs_sc)

            @pl.when(kj == 0)
            def _():
                rsum_ref[...] = jnp.zeros_like(rsum_ref)
        else:
            dq_sc, delta_sc, dk_sc, dv_sc = rest
        masks = _head_masks(ew)
        lo_q = lax.broadcasted_iota(jnp.int32, (tq, LANES), 1) < HEAD_DIM
        lo = lax.broadcasted_iota(jnp.int32, (tk, LANES), 1) < HEAD_DIM
        vmask = [lo, jnp.logical_not(lo)]

        @pl.when(kj == 0)
        def _():
            dq_sc[...] = jnp.zeros_like(dq_sc)
            for c in range(nq):
                rows = pl.ds(c * tq, tq)
                x = do_ref[rows, :].astype(F32) * o_ref[rows, :].astype(F32)
                r0 = jnp.sum(jnp.where(lo_q, x, 0.0), axis=1, keepdims=True)
                r1 = jnp.sum(jnp.where(lo_q, 0.0, x), axis=1, keepdims=True)
                delta_sc[c] = _head_rows(jnp.where(lo_q, r0, r1))

        k = kv_ref[:, 0:PAIR_Q]
        v = kv_ref[:, PAIR_Q:PAIR_KV]
        kh = [jnp.where(m, k, jnp.zeros_like(k)) for m in masks]
        vh = [jnp.where(m, v, jnp.zeros_like(v)) for m in vmask]
        dk_sc[...] = jnp.zeros_like(dk_sc)
        dv_sc[...] = jnp.zeros_like(dv_sc)

        def step(qi, diagonal):
            rows = pl.ds(pl.multiple_of(qi * tq, tq), tq)
            q = q_ref[rows, :]
            dov = do_ref[rows, :]
            lse8 = lse_ref[qi]
            dl8 = delta_sc[qi]
            for h in range(2):
                st = _dot_nt(kh[h], q)
                if scale != 1.0:
                    st = st * scale
                if diagonal is not None:
                    key = lax.broadcasted_iota(jnp.int32, st.shape, 0)
                    qry = lax.broadcasted_iota(jnp.int32, st.shape, 1)
                    st = jnp.where(key <= qry + diagonal * tq, st, NEG_BIG)
                pt = jnp.exp(st - lse8[h:h + 1, :])
                dpt = _dot_nt(vh[h], dov)
                dst = pt * (dpt - dl8[h:h + 1, :])
                if bias_grad:
                    cs_sc[h] += _lane_halves(dst, jnp.add)
                    rsum_ref[qi, h:h + 1, :] += jnp.sum(dst, axis=0, keepdims=True)
                if scale != 1.0:
                    dst = dst * scale
                ptb = pt.astype(BF16)
                dstb = dst.astype(BF16)
                dv_sc[h] += jnp.dot(ptb, dov, preferred_element_type=F32)
                dk_sc[h] += jnp.dot(dstb, q, preferred_element_type=F32)
                dq_sc[rows, :] += _dot_tn(dstb, kh[h])

        first = kj * per
        above = nq - per - first
        for left in range(0, ATTN_UNROLL, per):
            @pl.when(above % ATTN_UNROLL == left)
            def _(left=left):
                for d in range(per):
                    step(first + d, d)
                for u in range(left):
                    step(first + per + u, None)

        def loop_body(t, carry):
            for u in range(ATTN_UNROLL):
                step(first + per + above % ATTN_UNROLL + t * ATTN_UNROLL + u, None)
            return carry

        lax.fori_loop(0, above // ATTN_UNROLL, loop_body, 0)
        dkv_ref[:, 0:PAIR_Q] = (jnp.where(masks[0], dk_sc[0], 0.0) + jnp.where(masks[1], dk_sc[1], 0.0)).astype(BF16)
        dkv_ref[:, PAIR_Q:PAIR_KV] = jnp.where(lo, dv_sc[0], dv_sc[1]).astype(BF16)
        if bias_grad:
            csum_ref[...] = jnp.where(lo, jnp.sum(cs_sc[0], axis=1, keepdims=True),
                                      jnp.sum(cs_sc[1], axis=1, keepdims=True))

        @pl.when(kj == nk - 1)
        def _():
            dq_ref[...] = dq_sc[...].astype(BF16)

    rows_spec = pl.BlockSpec((nq, None, 8, tq), lambda b, p, j: (b, p, 0, 0))
    out_specs = [pl.BlockSpec((S, PAIR_Q), lambda b, p, j: (b, p)),
                 pl.BlockSpec((tk, PAIR_KV), lambda b, p, j: (b * nk + j, p))]
    out_shape = [jax.ShapeDtypeStruct((T, P * PAIR_Q), BF16), jax.ShapeDtypeStruct((T, P * PAIR_KV), BF16)]
    scratch = [pltpu.VMEM((S, PAIR_Q), F32), pltpu.VMEM((nq, 8, tq), F32),
               pltpu.VMEM((2, tk, PAIR_Q), F32), pltpu.VMEM((2, tk, LANES), F32)]
    if bias_grad:
        out_specs += [pl.BlockSpec((tk, LANES), lambda b, p, j: (b * nk + j, p)), rows_spec]
        out_shape += [jax.ShapeDtypeStruct((T, P * LANES), F32), jax.ShapeDtypeStruct((T // tq, P, 8, tq), F32)]
        scratch.append(pltpu.VMEM((2, tk, LANES), F32))
    return pl.pallas_call(
        body, name=name, grid=(B, P, nk),
        in_specs=[pl.BlockSpec((S, PAIR_Q), lambda b, p, j: (b, p)),
                  pl.BlockSpec((tk, PAIR_KV), lambda b, p, j: (b * nk + j, p)),
                  pl.BlockSpec((S, LANES), lambda b, p, j: (b, p)), rows_spec,
                  pl.BlockSpec((S, LANES), lambda b, p, j: (b, p))],
        out_specs=out_specs, out_shape=out_shape, scratch_shapes=scratch,
        compiler_params=_cparams(("parallel", "parallel", "arbitrary")),
    )(qx, kvx, o, lse, do)


def _fox_consts(P):
    H = 2 * P
    eq = np.zeros((3 * LANES, P * LANES), np.float32)
    ek = np.zeros((3 * LANES, P * LANES), np.float32)
    ones_q = np.zeros((1, P * LANES), np.float32)
    ones_k = np.zeros((1, P * LANES), np.float32)
    for h in range(H):
        base = (h // 2) * LANES + FOX_EXTRA * (h % 2)
        for part in range(3):
            eq[part * LANES + h, base + part] = 1.0
            ones_q[0, base + 3 + part] = 1.0
            ones_k[0, base + part] = 1.0
            ek[part * LANES + h, base + 3 + part] = -1.0
    return eq, ek, ones_q, ones_k


def _split3(f):
    hi = f.astype(BF16)
    r = f - hi.astype(F32)
    mid = r.astype(BF16)
    lo = (r - mid.astype(F32)).astype(BF16)
    return hi, mid, lo


def _tri_sum(tri, x):
    hi, mid, lo = _split3(x)
    return (jnp.dot(tri, hi, preferred_element_type=F32) + jnp.dot(tri, mid, preferred_element_type=F32)
            + jnp.dot(tri, lo, preferred_element_type=F32))


def _log1p_pos(e):
    return jnp.where(e < 0.01, e * (1.0 - e * (0.5 - e * (1.0 / 3.0))), jnp.log(1.0 + e))


def _fox_prep(qkv, fl, b_row, *, S, D, name):
    T = qkv.shape[0]
    P = D // LANES
    B = T // S
    tt = _tile(S, 256)
    per = S // tt
    eq, ek, ones_q, ones_k = _fox_consts(P)
    q_scale = HEAD_DIM ** -0.5

    def body(q_ref, k_ref, v_ref, fl_ref, b_ref, eq_ref, ek_ref, oq_ref, ok_ref, qx_ref, kvx_ref, carry):
        i = pl.program_id(1)

        @pl.when(i == 0)
        def _():
            carry[...] = jnp.zeros_like(carry)

        z = fl_ref[...] + b_ref[...]
        logf = jnp.minimum(z, 0.0) - _log1p_pos(jnp.exp(-jnp.abs(z)))
        row = lax.broadcasted_iota(jnp.int32, (tt, tt), 0)
        col = lax.broadcasted_iota(jnp.int32, (tt, tt), 1)
        tri = (col <= row).astype(BF16)
        f = _tri_sum(tri, logf) + carry[...]
        carry[...] = f[tt - 1:tt, :]
        parts = jnp.concatenate(_split3(f), axis=1)
        xq = jnp.dot(parts, eq_ref[...], preferred_element_type=F32) + oq_ref[...]
        xk = jnp.dot(parts, ek_ref[...], preferred_element_type=F32) + ok_ref[...]
        for p in range(P):
            c = slice(p * LANES, (p + 1) * LANES)
            qx_ref[:, p * PAIR_Q:p * PAIR_Q + LANES] = (q_ref[:, c].astype(F32) * q_scale).astype(BF16)
            qx_ref[:, p * PAIR_Q + LANES:(p + 1) * PAIR_Q] = xq[:, c].astype(BF16)
            kvx_ref[:, p * PAIR_KV:p * PAIR_KV + LANES] = k_ref[:, c]
            kvx_ref[:, p * PAIR_KV + LANES:p * PAIR_KV + PAIR_Q] = xk[:, c].astype(BF16)
            kvx_ref[:, p * PAIR_KV + PAIR_Q:(p + 1) * PAIR_KV] = v_ref[:, c]

    tok = lambda b, i: (b * per + i, 0)
    const = lambda b, i: (0, 0)
    return pl.pallas_call(
        body, name=name, grid=(B, per),
        in_specs=[pl.BlockSpec((tt, D), lambda b, i: (b * per + i, 0)),
                  pl.BlockSpec((tt, D), lambda b, i: (b * per + i, 1)),
                  pl.BlockSpec((tt, D), lambda b, i: (b * per + i, 2)),
                  pl.BlockSpec((tt, LANES), tok), pl.BlockSpec((1, LANES), const),
                  pl.BlockSpec(eq.shape, const), pl.BlockSpec(ek.shape, const),
                  pl.BlockSpec(ones_q.shape, const), pl.BlockSpec(ones_k.shape, const)],
        out_specs=[pl.BlockSpec((tt, P * PAIR_Q), tok), pl.BlockSpec((tt, P * PAIR_KV), tok)],
        out_shape=[jax.ShapeDtypeStruct((T, P * PAIR_Q), BF16), jax.ShapeDtypeStruct((T, P * PAIR_KV), BF16)],
        scratch_shapes=[pltpu.VMEM((1, LANES), F32)],
        compiler_params=_cparams(("arbitrary", "arbitrary")),
    )(qkv, qkv, qkv, fl, b_row, jnp.asarray(eq, BF16), jnp.asarray(ek, BF16), jnp.asarray(ones_q), jnp.asarray(ones_k))


def _fox_unprep(dqx, dkvx, csum, rsum, fl, b_row, *, S, D, name):
    T = dqx.shape[0]
    P = D // LANES
    B = T // S
    tt = _tile(S, 256)
    per = S // tt
    q_scale = HEAD_DIM ** -0.5

    def body(dq_ref, dkv_ref, cs_ref, rs_ref, fl_ref, b_ref, dqkv_ref, dfl_ref, db_ref, carry):
        b = pl.program_id(0)
        i = pl.program_id(1)

        @pl.when(i == 0)
        def _():
            carry[...] = jnp.zeros_like(carry)

        @pl.when((i == 0) & (b == 0))
        def _():
            db_ref[...] = jnp.zeros_like(db_ref)

        df = rs_ref[...] - cs_ref[...]
        for p in range(P):
            rq = slice(p * LANES, (p + 1) * LANES)
            dqkv_ref[:, rq] = (dq_ref[:, p * PAIR_Q:p * PAIR_Q + LANES].astype(F32) * q_scale).astype(BF16)
            dqkv_ref[:, D + p * LANES:D + (p + 1) * LANES] = dkv_ref[:, p * PAIR_KV:p * PAIR_KV + LANES]
            dqkv_ref[:, 2 * D + p * LANES:2 * D + (p + 1) * LANES] = dkv_ref[:, p * PAIR_KV + PAIR_Q:(p + 1) * PAIR_KV]
        row = lax.broadcasted_iota(jnp.int32, (tt, tt), 0)
        col = lax.broadcasted_iota(jnp.int32, (tt, tt), 1)
        tri = (col >= row).astype(BF16)
        dlogf = _tri_sum(tri, df) + carry[...]
        carry[...] = dlogf[0:1, :]
        z = fl_ref[...] + b_ref[...]
        e = jnp.exp(-jnp.abs(z))
        sig_neg = jnp.where(z >= 0.0, e, 1.0) / (1.0 + e)
        dfl = dlogf * sig_neg
        dfl_ref[...] = dfl.astype(BF16)
        db_ref[...] += jnp.sum(dfl, axis=0, keepdims=True)

    rev = lambda b, i: (b * per + per - 1 - i, 0)
    const = lambda b, i: (0, 0)
    return pl.pallas_call(
        body, name=name, grid=(B, per),
        in_specs=[pl.BlockSpec((tt, P * PAIR_Q), rev), pl.BlockSpec((tt, P * PAIR_KV), rev),
                  pl.BlockSpec((tt, LANES), rev), pl.BlockSpec((tt, LANES), rev), pl.BlockSpec((tt, LANES), rev),
                  pl.BlockSpec((1, LANES), const)],
        out_specs=[pl.BlockSpec((tt, 3 * D), rev), pl.BlockSpec((tt, LANES), rev), pl.BlockSpec((1, LANES), const)],
        out_shape=[jax.ShapeDtypeStruct((T, 3 * D), BF16), jax.ShapeDtypeStruct((T, LANES), BF16),
                   jax.ShapeDtypeStruct((1, LANES), F32)],
        scratch_shapes=[pltpu.VMEM((1, LANES), F32)],
        compiler_params=_cparams(("arbitrary", "arbitrary")),
    )(dqx, dkvx, csum, rsum, fl, b_row)


def _rms(x):
    r = lax.rsqrt(jnp.mean(x * x, axis=-1, keepdims=True) + NORM_EPS)
    return x * r, r


def _mla_mid(lat, gq, gkv, cos_t, sin_s, *, name):
    T, W = lat.shape
    Rq = W - 2 * LANES
    tt = _tile(T, 512)

    def body(l_ref, gq_ref, gkv_ref, c_ref, s_ref, o_ref, ot_ref):
        nq, _ = _rms(l_ref[:, 0:Rq])
        nkv, _ = _rms(l_ref[:, Rq:Rq + LANES])
        parts = [nq * gq_ref[...], nkv * gkv_ref[...], _rope128(l_ref[:, Rq + LANES:W], c_ref[...], s_ref[...])]
        out = jnp.concatenate(parts, axis=1)
        o_ref[...] = out.astype(BF16)
        ot_ref[...] = out.T.astype(BF16)

    return pl.pallas_call(
        body, name=name, grid=(T // tt,),
        in_specs=[pl.BlockSpec((tt, W), lambda i: (i, 0)), pl.BlockSpec((1, Rq), lambda i: (0, 0)),
                  pl.BlockSpec((1, LANES), lambda i: (0, 0)), pl.BlockSpec((tt, LANES), lambda i: (i, 0)),
                  pl.BlockSpec((tt, LANES), lambda i: (i, 0))],
        out_specs=[pl.BlockSpec((tt, W), lambda i: (i, 0)), pl.BlockSpec((W, tt), lambda i: (0, i))],
        out_shape=[jax.ShapeDtypeStruct((T, W), BF16), jax.ShapeDtypeStruct((W, T), BF16)],
        compiler_params=_cparams(("parallel",)),
    )(lat, gq, gkv, cos_t, sin_s)


def _mla_mid_bwd(lat, dcq, dckr, gq, gkv, cos_t, sin_s, *, name):
    T, W = lat.shape
    Rq = W - 2 * LANES
    tt = _tile(T, 512)

    def norm_bwd(x, dy, g):
        n, r = _rms(x)
        dn = dy * g
        return r * (dn - n * jnp.mean(dn * n, axis=-1, keepdims=True)), jnp.sum(dy * n, axis=0, keepdims=True)

    def body(l_ref, dq_ref, dk_ref, gq_ref, gkv_ref, c_ref, s_ref, o_ref, dgq_ref, dgkv_ref):
        i = pl.program_id(0)

        @pl.when(i == 0)
        def _():
            dgq_ref[...] = jnp.zeros_like(dgq_ref)
            dgkv_ref[...] = jnp.zeros_like(dgkv_ref)

        dxq, dgq = norm_bwd(l_ref[:, 0:Rq], dq_ref[...], gq_ref[...])
        dxkv, dgkv = norm_bwd(l_ref[:, Rq:Rq + LANES], dk_ref[:, 0:LANES], gkv_ref[...])
        o_ref[:, 0:Rq] = dxq.astype(BF16)
        o_ref[:, Rq:Rq + LANES] = dxkv.astype(BF16)
        o_ref[:, Rq + LANES:W] = _rope128(dk_ref[:, LANES:2 * LANES], c_ref[...], -s_ref[...]).astype(BF16)
        dgq_ref[...] += dgq
        dgkv_ref[...] += dgkv

    return pl.pallas_call(
        body, name=name, grid=(T // tt,),
        in_specs=[pl.BlockSpec((tt, W), lambda i: (i, 0)), pl.BlockSpec((tt, Rq), lambda i: (i, 0)),
                  pl.BlockSpec((tt, 2 * LANES), lambda i: (i, 0)), pl.BlockSpec((1, Rq), lambda i: (0, 0)),
                  pl.BlockSpec((1, LANES), lambda i: (0, 0)), pl.BlockSpec((tt, LANES), lambda i: (i, 0)),
                  pl.BlockSpec((tt, LANES), lambda i: (i, 0))],
        out_specs=[pl.BlockSpec((tt, W), lambda i: (i, 0)), pl.BlockSpec((1, Rq), lambda i: (0, 0)),
                   pl.BlockSpec((1, LANES), lambda i: (0, 0))],
        out_shape=[jax.ShapeDtypeStruct((T, W), BF16), jax.ShapeDtypeStruct((1, Rq), F32),
                   jax.ShapeDtypeStruct((1, LANES), F32)],
        compiler_params=_cparams(("arbitrary",)),
    )(lat, dcq, dckr, gq, gkv, cos_t, sin_s)


def _uq_to_pairs(w):
    Rq = w.shape[0]
    P = w.shape[1] // (2 * (HEAD_DIM + ROPE_DIM))
    w4 = w.reshape(Rq, P, 2, HEAD_DIM + ROPE_DIM)
    nope = w4[..., :HEAD_DIM].reshape(Rq, P, 2 * HEAD_DIM)
    rope = w4[..., HEAD_DIM:].reshape(Rq, P, 2 * ROPE_DIM)
    pad = jnp.zeros((Rq, P, PAIR_Q - 2 * HEAD_DIM - 2 * ROPE_DIM), w.dtype)
    return jnp.concatenate([nope, rope, pad], axis=-1).reshape(Rq, P * PAIR_Q)


def _uq_from_pairs(g):
    Rq = g.shape[0]
    P = g.shape[1] // PAIR_Q
    g3 = g.reshape(Rq, P, PAIR_Q)
    nope = g3[..., :2 * HEAD_DIM].reshape(Rq, P, 2, HEAD_DIM)
    rope = g3[..., 2 * HEAD_DIM:2 * HEAD_DIM + 2 * ROPE_DIM].reshape(Rq, P, 2, ROPE_DIM)
    return jnp.concatenate([nope, rope], axis=-1).reshape(Rq, P * 2 * (HEAD_DIM + ROPE_DIM))


def _ukv_to_pairs(w):
    P = w.shape[1] // (4 * HEAD_DIM)
    w4 = w.reshape(KV_RANK, P, 2, 2 * HEAD_DIM)
    kn = w4[..., :HEAD_DIM].reshape(KV_RANK, P, 2 * HEAD_DIM)
    vv = w4[..., HEAD_DIM:].reshape(KV_RANK, P, 2 * HEAD_DIM)
    top = jnp.concatenate([kn, jnp.zeros((KV_RANK, P, LANES), w.dtype), vv], axis=-1)
    place = np.zeros((LANES, P, PAIR_KV), np.float32)
    for r in range(ROPE_DIM):
        place[r, :, LANES + r] = 1.0
        place[r, :, LANES + ROPE_DIM + r] = 1.0
    return jnp.concatenate([top, jnp.asarray(place, w.dtype)], axis=0).reshape(KV_RANK + LANES, P * PAIR_KV)


def _ukv_from_pairs(g):
    P = g.shape[1] // PAIR_KV
    g3 = g[:KV_RANK].reshape(KV_RANK, P, PAIR_KV)
    kn = g3[..., :2 * HEAD_DIM].reshape(KV_RANK, P, 2, HEAD_DIM)
    vv = g3[..., PAIR_Q:].reshape(KV_RANK, P, 2, HEAD_DIM)
    return jnp.concatenate([kn, vv], axis=-1).reshape(KV_RANK, P * 4 * HEAD_DIM)


def _mlp_fwd(h2, w, i, x1, gate, *, S):
    def act(acc):
        u = jnp.square(jnp.maximum(acc, 0.0))
        return u, u

    u, u_t = _mm(h2, w["mlp_w1"], "nn", name=f"mlp_up_{i}", b_layer=i, out_dtypes=(BF16, BF16),
                 out_t=(False, True), epilogue=act)
    x2, z = _mm(u, w["mlp_w2"], "nn", name=f"mlp_down_{i}", b_layer=i, out_dtypes=(F32, F32), extras=(x1,),
                rowvecs=(gate,), seq=S, epilogue=lambda acc, xr, g: (xr + g * acc, acc))
    return x2, (u, u_t, z)


STACKED_GRADS = ("fox_out", "mla_down", "mla_uq", "mla_ukv", "mla_out", "mlp_w1", "mlp_w2")


def _local_step(x, target, pos_f, inv_freq_row, sign_row, mod, w, slots, *, S, hooks=None):
    hooks = hooks or {}
    T, D = x.shape
    L = mod.shape[0]
    L2 = len(w["fox_out"])
    cos_t, sin_s = _rope_tables(pos_f, inv_freq_row, sign_row)
    saved = []
    for i in range(L):
        j = i // 2
        sh_m, sc_m, g_m, sh_f, sc_f, g_f = (mod[i, s] for s in range(6))
        h, h_t = _norm_mod(x, w["norm_mix_g"][i], sc_m, sh_m, S=S, name=f"norm_mix_{i}")
        if i % 2 == 0:
            qkv = _mm(h, w["fox_qkv"], "nn", name=f"fox_qkv_{i}", b_layer=j, out_dtypes=(BF16,))
            fl = _mm(h, w["fox_f"], "nn", name=f"fox_f_{i}", b_layer=j)
            qx, kvx = _fox_prep(qkv, fl, w["fox_b"][j], S=S, D=D, name=f"fox_prep_{i}")
            o, lse, o_t = _attn_fwd(qx, kvx, S=S, scale=1.0, ew=FOX_EXTRA, name=f"fox_attn_{i}")
            mix = (qx, kvx, o, lse, o_t, fl)
            w_out = w["fox_out"]
        else:
            lat = _mm(h, w["mla_down"], "nn", name=f"mla_down_{i}", b_layer=j)
            Rq = lat.shape[1] - 2 * LANES
            cqr, cqr_t = _mla_mid(lat, w["mla_gq"][j], w["mla_gkv"][j], cos_t, sin_s, name=f"mla_mid_{i}")
            qx = _mm(cqr, w["mla_uq"], "nn", name=f"mla_uq_{i}", b_layer=j, out_dtypes=(BF16,), a_sz=Rq, tk=Rq,
                     tables=(cos_t, sin_s), epilogue=lambda acc, c, s: (_rope_pairs(acc, c, s, 1.0),))
            kvx = _mm(cqr, w["mla_ukv"], "nn", name=f"mla_ukv_{i}", b_layer=j, out_dtypes=(BF16,), a_off=Rq,
                      a_sz=2 * LANES, tk=2 * LANES, tn=PAIR_KV)
            o, lse, o_t = _attn_fwd(qx, kvx, S=S, scale=(HEAD_DIM + ROPE_DIM) ** -0.5, ew=ROPE_DIM,
                                    name=f"mla_attn_{i}")
            mix = (qx, kvx, o, lse, o_t, lat, cqr_t)
            w_out = w["mla_out"]
        x1, y = _mm(o, w_out, "nn", name=f"mix_out_{i}", b_layer=j, out_dtypes=(F32, F32), extras=(x,),
                    rowvecs=(g_m,), seq=S, epilogue=lambda acc, xr, g: (xr + g * acc, acc))
        h2, h2_t = _norm_mod(x1, w["norm_mlp_g"][i], sc_f, sh_f, S=S, name=f"norm_mlp_{i}")
        if i == 0 and "fwd_mlp0" in hooks:
            w = hooks["fwd_mlp0"](x1, w)
        x2, mlp = _mlp_fwd(h2, w, i, x1, g_f, S=S)
        saved.append((x, h_t, mix, y, x1, h2_t, mlp))
        x = x2
        if i == 0 and "fwd_layer1" in hooks:
            w = hooks["fwd_layer1"](x, w)

    dx, dg_final, loss = _final_loss(x, target, w["final_norm_g"])
    n_split = w["mlp_w1"][0][0].shape[1]

    grads = {k: [None] * len(w[k]) for k in ("norm_mix_g", "norm_mlp_g", "fox_b", "mla_gq", "mla_gkv")}
    grads.update({k: [None] * L2 for k in ("fox_qkv", "fox_f")})
    grads.update({k: {} for k in STACKED_GRADS})
    grads["final_norm_g"] = dg_final

    def stacked(key, layer, _, a_t, b, **kw):
        group, idx, count = slots[(key, layer)]
        grads[key][group] = _mm(a_t, b, "nn", out_stack=(grads[key].get(group), idx, count), **kw)

    dmod = [None] * L
    for i in reversed(range(L)):
        j = i // 2
        x0, h_t, mix, y, x1, h2_t, (u, u_t, z) = saved[i]
        sh_m, sc_m, g_m, sh_f, sc_f, g_f = (mod[i, s] for s in range(6))
        if i == 0 and "bwd_layer0" in hooks:
            g_f = g_f + hooks["bwd_layer0"](grads)[0, 0]
        dz, dg_f = _gate_bwd(dx, z, g_f, S=S, name=f"gate_mlp_bwd_{i}")
        stacked("mlp_w2", i, L, u_t, dz, name=f"mlp_w2_grad_{i}")
        dp = _mm(dz, w["mlp_w2"], "nt", name=f"mlp_down_bwd_{i}", b_layer=i, out_dtypes=(BF16,), extras=(u,),
                 epilogue=lambda acc, uv: (acc * (2.0 * jnp.sqrt(uv.astype(F32))),))
        stacked("mlp_w1", i, L, h2_t, dp, name=f"mlp_w1_grad_{i}", out_split=n_split)
        if i == 0 and "bwd_mix0" in hooks:
            g_m = g_m + hooks["bwd_mix0"](grads)[0, 0]
        dh2 = _mm(dp, w["mlp_w1"], "nt", name=f"mlp_up_bwd_{i}", b_layer=i)
        dx1, dsh_f, dsc_f, dgn = _norm_mod_bwd(x1, dh2, dx, w["norm_mlp_g"][i], sc_f, S=S, name=f"norm_mlp_bwd_{i}")
        grads["norm_mlp_g"][i] = dgn
        dy, dg_m = _gate_bwd(dx1, y, g_m, S=S, name=f"gate_mix_bwd_{i}")
        if i % 2 == 0:
            qx, kvx, o, lse, o_t, fl = mix
            stacked("fox_out", j, L2, o_t, dy, name=f"fox_out_grad_{i}")
            do = _mm(dy, w["fox_out"], "nt", name=f"fox_out_bwd_{i}", b_layer=j, out_dtypes=(BF16,))
            dqx, dkvx, csum, rsum = _attn_bwd(qx, kvx, o, lse, do, S=S, scale=1.0, ew=FOX_EXTRA,
                                              name=f"fox_attn_bwd_{i}", bias_grad=True)
            n_heads = D // HEAD_DIM
            csum = jnp.pad(csum.reshape(T, n_heads, HEAD_DIM)[:, :, 0], ((0, 0), (0, LANES - n_heads)))
            rsum = jnp.transpose(rsum[:, :, :2, :], (0, 3, 1, 2)).reshape(T, n_heads)
            rsum = jnp.pad(rsum, ((0, 0), (0, LANES - n_heads)))
            dqkv, dfl, db = _fox_unprep(dqx, dkvx, csum, rsum, fl, w["fox_b"][j], S=S, D=D, name=f"fox_unprep_{i}")
            grads["fox_b"][j] = db
            grads["fox_qkv"][j] = _mm(h_t, dqkv, "nn", name=f"fox_qkv_grad_{i}")
            grads["fox_f"][j] = _mm(h_t, dfl, "nn", name=f"fox_f_grad_{i}")
            dh_f = _mm(dfl, w["fox_f"], "nt", name=f"fox_f_bwd_{i}", b_layer=j)
            dh = _mm(dqkv, w["fox_qkv"], "nt", name=f"fox_qkv_bwd_{i}", b_layer=j, extras=(dh_f,),
                     epilogue=lambda acc, e: (acc + e,))
        else:
            qx, kvx, o, lse, o_t, lat, cqr_t = mix
            Rq = lat.shape[1] - 2 * LANES
            stacked("mla_out", j, L2, o_t, dy, name=f"mla_out_grad_{i}")
            do = _mm(dy, w["mla_out"], "nt", name=f"mla_out_bwd_{i}", b_layer=j, out_dtypes=(BF16,))
            dqx, dkvx = _attn_bwd(qx, kvx, o, lse, do, S=S, scale=(HEAD_DIM + ROPE_DIM) ** -0.5, ew=ROPE_DIM,
                                  name=f"mla_attn_bwd_{i}")
            dqpre = _unrope(dqx, cos_t, sin_s)
            stacked("mla_uq", j, L2, cqr_t[:Rq], dqpre, name=f"mla_uq_grad_{i}", out_split=n_split)
            stacked("mla_ukv", j, L2, cqr_t[Rq:], dkvx, name=f"mla_ukv_grad_{i}", tn=PAIR_KV, out_split=n_split)
            dcq = _mm(dqpre, w["mla_uq"], "nt", name=f"mla_uq_bwd_{i}", b_layer=j)
            dckr = _mm(dkvx, w["mla_ukv"], "nt", name=f"mla_ukv_bwd_{i}", b_layer=j, tk=PAIR_KV * 2)
            dlat, dgq, dgkv = _mla_mid_bwd(lat, dcq, dckr, w["mla_gq"][j], w["mla_gkv"][j], cos_t, sin_s,
                                           name=f"mla_mid_bwd_{i}")
            grads["mla_gq"][j] = dgq
            grads["mla_gkv"][j] = dgkv
            stacked("mla_down", j, L2, h_t, dlat, name=f"mla_down_grad_{i}")
            dh = _mm(dlat, w["mla_down"], "nt", name=f"mla_down_bwd_{i}", b_layer=j)
        dx, dsh_m, dsc_m, dgn = _norm_mod_bwd(x0, dh, dx1, w["norm_mix_g"][i], sc_m, S=S, name=f"norm_mix_bwd_{i}")
        grads["norm_mix_g"][i] = dgn
        dmod[i] = jnp.stack([dsh_m, dsc_m, dg_m, dsh_f, dsc_f, dg_f])
    return loss, dx, jnp.stack(dmod), grads


GATHERED = ("fox_in", "fox_out", "mla_down", "mla_uq", "mla_ukv", "mla_out", "mlp_w1", "mlp_w2")
ROW_SHARDED = ("fox_out", "mla_down", "mla_out", "mlp_w2")


def _shard_layouts(wts):
    dkv = wts["mla_w_dkv"]
    dkv = jnp.pad(dkv, ((0, 0), (0, 0), (0, 2 * LANES - dkv.shape[2])))
    return {
        "fox_in": _pad_lanes(wts["fox_w_in"].astype(BF16)),
        "fox_out": wts["fox_w_out"].astype(BF16),
        "mla_down": jnp.concatenate([wts["mla_w_dq"], dkv], axis=2).astype(BF16),
        "mla_uq": jax.vmap(_uq_to_pairs)(wts["mla_w_uq"].astype(BF16)),
        "mla_ukv": jax.vmap(_ukv_to_pairs)(wts["mla_w_ukv"].astype(BF16)),
        "mla_out": wts["mla_w_out"].astype(BF16),
        "mlp_w1": wts["mlp_w1"].astype(BF16),
        "mlp_w2": wts["mlp_w2"].astype(BF16),
    }


def _small_layouts(small):
    return {
        "fox_b": [jnp.pad(b, (0, LANES - b.shape[0]))[None, :] for b in small["fox_b_f"]],
        "mla_gq": [g[None, :] for g in small["mla_q_norm_g"]],
        "mla_gkv": [g[None, :] for g in small["mla_kv_norm_g"]],
        "norm_mix_g": [g[None, :] for g in small["norm_mix_g"]],
        "norm_mlp_g": [g[None, :] for g in small["norm_mlp_g"]],
        "final_norm_g": small["final_norm_g"][None, :],
    }


def _comm_groups(L, L2):
    rest = [("fox_in", 1, L2 - 1), ("fox_out", 1, L2 - 1), ("mla_down", 0, L2), ("mla_uq", 0, L2),
            ("mla_ukv", 0, L2), ("mla_out", 0, L2), ("mlp_w1", 1, L - 1), ("mlp_w2", 1, L - 1)]
    return {"mix0": [("fox_in", 0, 1), ("fox_out", 0, 1)], "mlp0": [("mlp_w1", 0, 1), ("mlp_w2", 0, 1)],
            "rest": [e for e in rest if e[2] > 0]}


def _layer_slots(groups):
    return {(n, s + l): (g, l, cnt) for g, entries in groups.items() for n, s, cnt in entries for l in range(cnt)}


def _pad_lanes(a):
    cols = a.shape[-1]
    return jnp.pad(a, [(0, 0)] * (a.ndim - 1) + [(0, -cols % LANES)])


def _weight_views(name, gathered, D, n_fox_heads):
    n, ns, rows, cols = gathered.shape
    if name == "fox_in":
        true_cols = (3 * D + n_fox_heads) // ns
        fox = jnp.concatenate([gathered[:, k, :, :true_cols] for k in range(ns)], axis=-1)
        return {"fox_qkv": fox[:, :, :3 * D], "fox_f": _pad_lanes(fox[:, :, 3 * D:])}
    if name in ROW_SHARDED:
        return {name: gathered.reshape(n, ns * rows, cols)}
    return {name: gathered}


def _grad_pieces(name, g, qkv_f, n_fox_heads, ns):
    if name == "fox_in":
        fox = jnp.stack([jnp.concatenate([a, b[:, :n_fox_heads]], axis=1) for a, b in qkv_f])
        cols = fox.shape[2] // ns
        return jnp.stack([_pad_lanes(fox[:, :, k * cols:(k + 1) * cols]) for k in range(ns)], axis=1)
    if name in ROW_SHARDED:
        return g.reshape(g.shape[0], ns, g.shape[1] // ns, g.shape[2])
    return g


def _small_grads(g, n_fox_heads):
    return {
        "norm_mix_g": jnp.concatenate(g["norm_mix_g"], axis=0),
        "norm_mlp_g": jnp.concatenate(g["norm_mlp_g"], axis=0),
        "final_norm_g": g["final_norm_g"][0],
        "fox_b_f": jnp.concatenate(g["fox_b"], axis=0)[:, :n_fox_heads],
        "mla_q_norm_g": jnp.concatenate(g["mla_gq"], axis=0),
        "mla_kv_norm_g": jnp.concatenate(g["mla_gkv"], axis=0),
    }


def _silu(c):
    return c * (1.0 / (1.0 + jnp.exp(-c)))


def _ada_fwd(c_all, ada_w, ada_b_cols):
    L, D, C = ada_w.shape
    Bg = c_all.shape[0]
    tc = _tile(C, 512)

    def body(c_ref, w_ref, b_ref, o_ref):
        ca = _silu(c_ref[...]).astype(BF16)
        o_ref[...] = jnp.dot(ca, w_ref[...].astype(BF16), preferred_element_type=F32) + b_ref[...]

    return pl.pallas_call(
        body, name="ada_fwd", grid=(L, C // tc),
        in_specs=[pl.BlockSpec((Bg, D), lambda l, j: (0, 0)), pl.BlockSpec((None, D, tc), lambda l, j: (l, 0, j)),
                  pl.BlockSpec((None, 1, tc), lambda l, j: (l, 0, j))],
        out_specs=pl.BlockSpec((None, Bg, tc), lambda l, j: (l, 0, j)),
        out_shape=jax.ShapeDtypeStruct((L, Bg, C), F32),
        compiler_params=_cparams(("parallel", "parallel")),
    )(c_all, ada_w, ada_b_cols)


def _ada_bwd(c_all, dmod_cols):
    L, Bg, C = dmod_cols.shape
    D = c_all.shape[1]
    tc = _tile(C, 512)

    def body(c_ref, d_ref, o_ref):
        ca = _silu(c_ref[...]).astype(BF16)
        o_ref[...] = _dot_tn(ca, d_ref[...].astype(BF16))

    return pl.pallas_call(
        body, name="ada_bwd", grid=(L, C // tc),
        in_specs=[pl.BlockSpec((Bg, D), lambda l, j: (0, 0)), pl.BlockSpec((None, Bg, tc), lambda l, j: (l, 0, j))],
        out_specs=pl.BlockSpec((None, D, tc), lambda l, j: (l, 0, j)),
        out_shape=jax.ShapeDtypeStruct((L, D, C), F32),
        compiler_params=_cparams(("parallel", "parallel")),
    )(c_all, dmod_cols)


def _adamw_update(w, gv, m, v):
    mn = ADAM_B1 * m + (1.0 - ADAM_B1) * gv
    vn = ADAM_B2 * v + (1.0 - ADAM_B2) * jnp.square(gv)
    m_hat = mn / (1.0 - ADAM_B1 ** ADAM_STEP)
    v_hat = vn / (1.0 - ADAM_B2 ** ADAM_STEP)
    return -ADAM_LR * (m_hat / (jnp.sqrt(v_hat) + ADAM_EPS) + ADAM_WD * w), mn, vn


def _adamw(w, g, m, v, *, name):
    shape = w.shape
    C = shape[-1]
    R = int(np.prod(shape[:-1])) if len(shape) > 1 else 1
    w2, g2, m2, v2 = (a.reshape(R, C) for a in (w, g, m, v))
    tr = _row_tile(R, C)

    def body(w_ref, g_ref, m_ref, v_ref, d_ref, nm_ref, nv_ref):
        d_ref[...], nm_ref[...], nv_ref[...] = _adamw_update(w_ref[...], g_ref[...], m_ref[...], v_ref[...])

    spec = pl.BlockSpec((tr, C), lambda i: (i, 0))
    out = pl.pallas_call(
        body, name=name, grid=(R // tr,), in_specs=[spec] * 4, out_specs=[spec] * 3,
        out_shape=[jax.ShapeDtypeStruct((R, C), F32)] * 3, compiler_params=_cparams(("parallel",)),
    )(w2, g2, m2, v2)
    return tuple(a.reshape(shape) for a in out)


def _adamw_halves(w, g_own, g_peer, m, v, c_idx, *, name):
    L, rows, C = w.shape
    R = rows // 2
    tr = _row_tile(R, C)

    def body(c_ref, w_ref, go_ref, gp_ref, m_ref, v_ref, g_ref, d_ref, nm_ref, nv_ref):
        gv = jnp.where(pl.program_id(1) == c_ref[0], go_ref[...], gp_ref[...])
        g_ref[...] = gv
        d_ref[...], nm_ref[...], nv_ref[...] = _adamw_update(w_ref[...], gv, m_ref[...], v_ref[...])

    full = pl.BlockSpec((None, None, tr, C), lambda l, hh, i, c_ref: (l, hh, i, 0))
    half = pl.BlockSpec((None, tr, C), lambda l, hh, i, c_ref: (l, i, 0))
    grid_spec = pltpu.PrefetchScalarGridSpec(
        num_scalar_prefetch=1, grid=(L, 2, R // tr), in_specs=[full, half, half, full, full], out_specs=[full] * 4)
    split = lambda a: a.reshape(L, 2, R, C)
    out = pl.pallas_call(
        body, name=name, grid_spec=grid_spec, out_shape=[jax.ShapeDtypeStruct((L, 2, R, C), F32)] * 4,
        compiler_params=_cparams(("parallel", "parallel", "parallel")),
    )(c_idx, split(w), g_own, g_peer, split(m), split(v))
    return tuple(a.reshape(w.shape) for a in out)


def _sum_gathered(dm8, sm8):
    n_dev, Bl, R, D = dm8.shape
    Rs = sm8.shape[1]

    def body(dm_ref, sm_ref, ob_ref, os_ref):
        acc_b = jnp.zeros((R, D), F32)
        acc_s = jnp.zeros((Rs, D), F32)
        for d in range(n_dev):
            for b in range(Bl):
                acc_b = acc_b + dm_ref[d, b]
            acc_s = acc_s + sm_ref[d]
        ob_ref[...] = acc_b
        os_ref[...] = acc_s

    return pl.pallas_call(
        body, name="sum_gathered",
        out_shape=[jax.ShapeDtypeStruct((R, D), F32), jax.ShapeDtypeStruct((Rs, D), F32)],
        compiler_params=_cparams(None),
    )(dm8, sm8)


N_DEV = 8
N_CHIP = 4
ANY = pl.BlockSpec(memory_space=pl.ANY)
HBM = pl.BlockSpec(memory_space=pltpu.HBM)
SEM = pl.BlockSpec(memory_space=pltpu.SEMAPHORE)
DATAFLOW = pltpu.SideEffectType.DATAFLOW_SIDE_EFFECTING


def _mesh_pos():
    return lax.axis_index("x"), lax.axis_index("y"), lax.axis_index("c")


def _all_gather8(block, *, name, in_vmem):
    R, W = block.shape

    def body(x_ref, out_ref, send_sems, recv_sems, local_sem):
        x, y, c = _mesh_pos()
        me, sibling = (x, y, c), (x, y, 1 - c)
        chips = [(1 - x, y), (x, 1 - y), (1 - x, 1 - y)]

        def slot(px, py, pc):
            return out_ref.at[4 * px + 2 * py + pc]

        def copy(k, blk, to, src=None):
            return pltpu.make_async_remote_copy(
                src_ref=slot(*blk) if src is None else src, dst_ref=slot(*blk),
                send_sem=send_sems.at[k], recv_sem=recv_sems.at[k], device_id=to, device_id_type=MESH_ID)

        mine = pltpu.make_async_copy(x_ref, slot(*me), local_sem)
        mine.start()
        first = [copy(0, me, sibling, src=x_ref)]
        first += [copy(1 + j, me, (*chip, c), src=x_ref) for j, chip in enumerate(chips)]
        for cp in first:
            cp.start()
        passed = [copy(4 + j, (*chip, c), sibling) for j, chip in enumerate(chips)]
        for j, chip in enumerate(chips):
            copy(1 + j, (*chip, c), me).wait_recv()
            passed[j].start()
        copy(0, sibling, me).wait_recv()
        for j, chip in enumerate(chips):
            copy(4 + j, (*chip, 1 - c), me).wait_recv()
        for cp in first + passed:
            cp.wait_send()
        mine.wait()

    space = pl.BlockSpec(memory_space=pltpu.VMEM) if in_vmem else ANY
    return pl.pallas_call(
        body, name=name, out_shape=jax.ShapeDtypeStruct((N_DEV, R, W), block.dtype),
        in_specs=[space], out_specs=space,
        scratch_shapes=[pltpu.SemaphoreType.DMA((7,)), pltpu.SemaphoreType.DMA((7,)), pltpu.SemaphoreType.DMA],
        compiler_params=pltpu.CompilerParams(vmem_limit_bytes=VMEM_LIMIT_V7X),
    )(block)


def _comm_call(body, arrays, out_shapes, n_sems, *, name):
    return pl.pallas_call(
        body, name=name, out_shape=out_shapes, in_specs=[ANY] * len(arrays), out_specs=[ANY] * len(out_shapes),
        scratch_shapes=[pltpu.SemaphoreType.DMA((n_sems,)), pltpu.SemaphoreType.DMA((n_sems,)),
                        pltpu.SemaphoreType.DMA((len(arrays),))],
    )(*arrays)


def _gather_weights(shards, *, name):
    n = len(shards)

    def body(*refs):
        xs, outs = refs[:n], refs[n:2 * n]
        send_sems, recv_sems, local_sems = refs[2 * n:]
        x, y, c = _mesh_pos()
        me, sibling = (x, y, c), (x, y, 1 - c)
        chips = [(1 - x, y), (x, 1 - y), (1 - x, 1 - y)]
        waits = []
        for i in range(n):
            nl = shards[i].shape[0]
            own = xs[i].at[pl.ds(0, nl), c]

            def slot(px, py, pc, i=i, nl=nl):
                return outs[i].at[pl.ds(0, nl), 2 * px + py, pc]

            def copy(k, blk, to, src=None, i=i, slot=slot):
                return pltpu.make_async_remote_copy(
                    src_ref=slot(*blk) if src is None else src, dst_ref=slot(*blk),
                    send_sem=send_sems.at[7 * i + k], recv_sem=recv_sems.at[7 * i + k], device_id=to,
                    device_id_type=MESH_ID)

            mine = pltpu.make_async_copy(own, slot(*me), local_sems.at[i])
            mine.start()
            first = [copy(0, me, sibling, src=own)]
            first += [copy(1 + j, me, (*chip, c), src=own) for j, chip in enumerate(chips)]
            for cp in first:
                cp.start()
            waits.append((copy, mine, first))
        for copy, mine, first in waits:
            passed = [copy(4 + j, (*chip, c), sibling) for j, chip in enumerate(chips)]
            for j, chip in enumerate(chips):
                copy(1 + j, (*chip, c), me).wait_recv()
                passed[j].start()
            copy(0, sibling, me).wait_recv()
            for j, chip in enumerate(chips):
                copy(4 + j, (*chip, 1 - c), me).wait_recv()
            for cp in first + passed:
                cp.wait_send()
            mine.wait()

    out_shapes = [jax.ShapeDtypeStruct((s.shape[0], N_CHIP) + s.shape[1:], s.dtype) for s in shards]
    return _comm_call(body, shards, out_shapes, 7 * n, name=name)


def _place_own(shard, chip_idx, c_idx, *, name):
    n, _, rows, cols = shard.shape
    tr = _row_tile(rows, cols)

    def body(k_ref, c_ref, x_ref, o_ref):
        o_ref[...] = x_ref[...]

    grid_spec = pltpu.PrefetchScalarGridSpec(
        num_scalar_prefetch=2, grid=(n, rows // tr),
        in_specs=[pl.BlockSpec((None, None, tr, cols), lambda l, i, k_ref, c_ref: (l, c_ref[0], i, 0))],
        out_specs=pl.BlockSpec((None, None, None, tr, cols), lambda l, i, k_ref, c_ref: (l, k_ref[0], c_ref[0], i, 0)))
    return pl.pallas_call(
        body, name=name, grid_spec=grid_spec,
        out_shape=jax.ShapeDtypeStruct((n, N_CHIP, 2, rows, cols), shard.dtype),
        compiler_params=_cparams(("parallel", "parallel")),
    )(chip_idx, c_idx, shard)


def _gather_copies(x_refs, land_refs, send_sems, recv_sems):
    x, y, c = _mesh_pos()
    k_me = 2 * x + y
    targets = [(x, y, 1 - c), (1 - x, y, c), (x, 1 - y, c), (1 - x, 1 - y, c)]
    copies = []
    for i, (x_ref, land_ref) in enumerate(zip(x_refs, land_refs)):
        nl = x_ref.shape[0]
        for j, to in enumerate(targets):
            copies.append(pltpu.make_async_remote_copy(
                src_ref=x_ref.at[pl.ds(0, nl), c], dst_ref=land_ref.at[pl.ds(0, nl), k_me, c],
                send_sem=send_sems.at[4 * i + j], recv_sem=recv_sems.at[4 * i + j], device_id=to,
                device_id_type=MESH_ID))
    return copies


def _split_start(copies_fn, srcs, lands, after, *, name, sems_per_array):
    n = len(srcs)

    def body(*refs):
        send_sems, recv_sems = refs[2 * n + 1], refs[2 * n + 2]
        for cp in copies_fn(refs[:n], refs[n:2 * n], send_sems, recv_sems):
            cp.start()
        refs[-1][...] = jnp.zeros_like(refs[-1])

    operands = [pltpu.with_memory_space_constraint(a, pltpu.HBM) for a in list(srcs) + list(lands)]
    n_sems = sems_per_array * n
    out_shape = ([pltpu.SemaphoreType.DMA((n_sems,)), pltpu.SemaphoreType.DMA((n_sems,))]
                 + [pltpu.HBM(a.shape, a.dtype) for a in operands] + [jax.ShapeDtypeStruct((8, LANES), F32)])
    res = pl.pallas_call(
        body, name=name, out_shape=out_shape, in_specs=[HBM] * (2 * n) + [ANY],
        out_specs=[SEM, SEM] + [HBM] * (2 * n) + [pl.BlockSpec(memory_space=pltpu.VMEM)],
        input_output_aliases={i: 2 + i for i in range(2 * n)},
        compiler_params=pltpu.CompilerParams(has_side_effects=DATAFLOW),
    )(*operands, after)
    return res[0], res[1], list(res[2:2 + n]), list(res[2 + n:2 + 2 * n]), res[-1]


def _split_wait(copies_fn, send_sems, recv_sems, srcs, lands, after, *, name):
    n = len(srcs)

    def body(*refs):
        for cp in copies_fn(refs[:n], refs[n:2 * n], refs[2 * n], refs[2 * n + 1]):
            cp.wait_send()
            cp.wait_recv()

    res = pl.pallas_call(
        body, name=name, out_shape=[pltpu.HBM(a.shape, a.dtype) for a in list(srcs) + list(lands)],
        in_specs=[HBM] * (2 * n) + [SEM, SEM, ANY], out_specs=[HBM] * (2 * n),
        input_output_aliases={i: i for i in range(2 * n)},
        compiler_params=pltpu.CompilerParams(has_side_effects=DATAFLOW),
    )(*srcs, *lands, send_sems, recv_sems, after)
    return list(res[:n]), list(res[n:])


def _gather_forward(lands, *, name):
    n = len(lands)

    def body(*refs):
        xs = refs[:n]
        send_sems, recv_sems, _ = refs[2 * n:]
        x, y, c = _mesh_pos()
        chips = [(1 - x, y), (x, 1 - y), (1 - x, 1 - y)]
        copies = []
        for i in range(n):
            nl = lands[i].shape[0]
            for j, (cx, cy) in enumerate(chips):
                here = xs[i].at[pl.ds(0, nl), 2 * cx + cy, c]
                cp = pltpu.make_async_remote_copy(
                    src_ref=here, dst_ref=here, send_sem=send_sems.at[3 * i + j], recv_sem=recv_sems.at[3 * i + j],
                    device_id=(x, y, 1 - c), device_id_type=MESH_ID)
                cp.start()
                copies.append(cp)
        for cp in copies:
            cp.wait()

    return pl.pallas_call(
        body, name=name, out_shape=[jax.ShapeDtypeStruct(a.shape, a.dtype) for a in lands],
        in_specs=[ANY] * n, out_specs=[ANY] * n, input_output_aliases={i: i for i in range(n)},
        scratch_shapes=[pltpu.SemaphoreType.DMA((3 * n,)), pltpu.SemaphoreType.DMA((3 * n,)),
                        pltpu.SemaphoreType.DMA((1,))],
    )(*lands)


def _pair_copies(g_refs, land_refs, send_sems, recv_sems):
    x, y, c = _mesh_pos()
    copies = []
    for i, (g_ref, land_ref) in enumerate(zip(g_refs, land_refs)):
        nl, ns = g_ref.shape[:2]
        copies.append(pltpu.make_async_remote_copy(
            src_ref=g_ref.at[pl.ds(0, nl), pl.ds(0, ns), 1 - c], dst_ref=land_ref, send_sem=send_sems.at[i],
            recv_sem=recv_sems.at[i], device_id=(x, y, 1 - c), device_id_type=MESH_ID))
    return copies


def _pair_exchange(gs, *, name):
    n = len(gs)

    def body(*refs):
        send_sems, recv_sems, _ = refs[2 * n:]
        copies = _pair_copies(refs[:n], refs[n:2 * n], send_sems, recv_sems)
        for cp in copies:
            cp.start()
        for cp in copies:
            cp.wait()

    out_shapes = [jax.ShapeDtypeStruct(g.shape[:2] + g.shape[3:], g.dtype) for g in gs]
    return _comm_call(body, gs, out_shapes, n, name=name)


def _chip_copies(p_refs, land_refs, send_sems, recv_sems):
    x, y, c = _mesh_pos()
    k_me = 2 * x + y
    chips = [(1 - x, y), (x, 1 - y), (1 - x, 1 - y)]
    copies = []
    for i, (p_ref, land_ref) in enumerate(zip(p_refs, land_refs)):
        nl = p_ref.shape[0]
        for j, (cx, cy) in enumerate(chips):
            copies.append(pltpu.make_async_remote_copy(
                src_ref=p_ref.at[pl.ds(0, nl), 2 * cx + cy], dst_ref=land_ref.at[k_me],
                send_sem=send_sems.at[3 * i + j], recv_sem=recv_sems.at[3 * i + j],
                device_id=(cx, cy, c), device_id_type=MESH_ID))
    return copies


def _chip_landing(ps):
    return [lax.empty((p.shape[1], p.shape[0]) + p.shape[2:], p.dtype) for p in ps]


def _chip_exchange(ps, *, name):
    n = len(ps)

    def body(*refs):
        send_sems, recv_sems, _ = refs[2 * n:]
        copies = _chip_copies(refs[:n], refs[n:2 * n], send_sems, recv_sems)
        for cp in copies:
            cp.start()
        for cp in copies:
            cp.wait()

    out_shapes = [jax.ShapeDtypeStruct((p.shape[1], p.shape[0]) + p.shape[2:], p.dtype) for p in ps]
    return _comm_call(body, ps, out_shapes, 3 * n, name=name)


def _pair_swap(ss, *, name):
    n = len(ss)

    def body(*refs):
        xs, outs = refs[:n], refs[n:2 * n]
        send_sems, recv_sems, _ = refs[2 * n:]
        x, y, c = _mesh_pos()
        copies = []
        for i in range(n):
            cp = pltpu.make_async_remote_copy(src_ref=xs[i], dst_ref=outs[i], send_sem=send_sems.at[i],
                                              recv_sem=recv_sems.at[i], device_id=(x, y, 1 - c),
                                              device_id_type=MESH_ID)
            cp.start()
            copies.append(cp)
        for cp in copies:
            cp.wait()

    out_shapes = [jax.ShapeDtypeStruct(s.shape, s.dtype) for s in ss]
    return _comm_call(body, ss, out_shapes, n, name=name)


def _row_tile(rows, cols):
    tr = rows
    while tr * cols > 256 * 1024 and tr % 16 == 0:
        tr //= 2
    return tr


def _pair_add(g, recv, c_idx, *, name):
    n, ns, _, rows, W = g.shape
    tr = _row_tile(rows, W)

    def body(c_ref, g_ref, r_ref, o_ref):
        o_ref[...] = (g_ref[...] + r_ref[...]).astype(BF16)

    piece = pl.BlockSpec((None, tr, W), lambda p, i, c_ref: (p, i, 0))
    grid_spec = pltpu.PrefetchScalarGridSpec(
        num_scalar_prefetch=1, grid=(n * ns, rows // tr),
        in_specs=[pl.BlockSpec((None, None, tr, W), lambda p, i, c_ref: (p, c_ref[0], i, 0)), piece],
        out_specs=piece)
    out = pl.pallas_call(
        body, name=name, grid_spec=grid_spec, out_shape=jax.ShapeDtypeStruct((n * ns, rows, W), BF16),
        compiler_params=_cparams(("parallel", "parallel")),
    )(c_idx, g.reshape(n * ns, 2, rows, W), recv.reshape(n * ns, rows, W))
    return out.reshape(n, ns, rows, W)


def _sum_pieces(land, own, chip_idx, *, name):
    n, nl, A, W = land.shape
    tr = _row_tile(A, W)

    def body(k_ref, l_ref, o_ref, out_ref):
        acc = jnp.zeros(out_ref.shape, F32)
        for k in range(n):
            acc = acc + jnp.where(k == k_ref[0], o_ref[...], l_ref[k]).astype(F32)
        out_ref[...] = acc

    grid_spec = pltpu.PrefetchScalarGridSpec(
        num_scalar_prefetch=1, grid=(nl, A // tr),
        in_specs=[pl.BlockSpec((n, None, tr, W), lambda l, i, k_ref: (0, l, i, 0)),
                  pl.BlockSpec((None, None, tr, W), lambda l, i, k_ref: (l, k_ref[0], i, 0))],
        out_specs=pl.BlockSpec((None, tr, W), lambda l, i, k_ref: (l, i, 0)))
    return pl.pallas_call(
        body, name=name, grid_spec=grid_spec, out_shape=jax.ShapeDtypeStruct((nl, A, W), F32),
        compiler_params=_cparams(("parallel", "parallel")),
    )(chip_idx, land, own)


SMALL = ("norm_mix_g", "norm_mlp_g", "final_norm_g", "fox_b_f", "mla_q_norm_g", "mla_kv_norm_g")
WEIGHT_ORDER = ("ada_w", "ada_b", "norm_mix_g", "norm_mlp_g", "fox_w_in", "fox_b_f", "fox_w_out", "mla_w_dq",
                "mla_q_norm_g", "mla_w_uq", "mla_w_dkv", "mla_kv_norm_g", "mla_w_ukv", "mla_w_out", "mlp_w1",
                "mlp_w2", "final_norm_g")


def _small_rows(vals, D):
    rows = [vals["norm_mix_g"], vals["norm_mlp_g"], vals["final_norm_g"][None, :]]
    for n in ("fox_b_f", "mla_q_norm_g", "mla_kv_norm_g"):
        flat = vals[n].reshape(-1)
        assert flat.shape[0] <= D
        rows.append(jnp.pad(flat, (0, D - flat.shape[0]))[None, :])
    return jnp.concatenate(rows, axis=0)


def _small_unrows(rows, shapes):
    L = shapes["norm_mix_g"][0]
    out = {"norm_mix_g": rows[0:L], "norm_mlp_g": rows[L:2 * L], "final_norm_g": rows[2 * L]}
    for k, n in enumerate(("fox_b_f", "mla_q_norm_g", "mla_kv_norm_g")):
        size = int(np.prod(shapes[n]))
        out[n] = rows[2 * L + 1 + k, :size].reshape(shapes[n])
    return out


def kernel(x, c, positions, ada_w, ada_b, norm_mix_g, norm_mlp_g, fox_w_in, fox_b_f, fox_w_out, mla_w_dq, mla_q_norm_g, mla_w_uq, mla_w_dkv, mla_kv_norm_g, mla_w_ukv, mla_w_out, mlp_w1, mlp_w2, final_norm_g, loss_target, m_ada_w, m_ada_b, m_norm_mix_g, m_norm_mlp_g, m_fox_w_in, m_fox_b_f, m_fox_w_out, m_mla_w_dq, m_mla_q_norm_g, m_mla_w_uq, m_mla_w_dkv, m_mla_kv_norm_g, m_mla_w_ukv, m_mla_w_out, m_mlp_w1, m_mlp_w2, m_final_norm_g, v_ada_w, v_ada_b, v_norm_mix_g, v_norm_mlp_g, v_fox_w_in, v_fox_b_f, v_fox_w_out, v_mla_w_dq, v_mla_q_norm_g, v_mla_w_uq, v_mla_w_dkv, v_mla_kv_norm_g, v_mla_w_ukv, v_mla_w_out, v_mlp_w1, v_mlp_w2, v_final_norm_g):
    args = dict(locals())
    wts = {n: args[n] for n in WEIGHT_ORDER}
    mom = {n: args["m_" + n] for n in WEIGHT_ORDER}
    var = {n: args["v_" + n] for n in WEIGHT_ORDER}
    Bl, S, D = x.shape
    T = Bl * S
    L = ada_w.shape[0]
    C = ada_w.shape[2]
    mx, my, mc = _mesh_pos()
    chip = 2 * mx + my
    dev = 4 * mx + 2 * my + mc
    c_idx = jnp.reshape(mc, (1,)).astype(jnp.int32)
    chip_idx = jnp.reshape(chip, (1,)).astype(jnp.int32)
    small = {n: wts[n] for n in SMALL}
    L2, q_cols = mla_q_norm_g.shape
    n_fox_heads = fox_b_f.shape[1]

    shards = _shard_layouts(wts)
    groups = _comm_groups(L, L2)
    slots = _layer_slots(groups)

    def row_halves(a):
        return a.reshape(a.shape[:-2] + (2, a.shape[-2] // 2, a.shape[-1]))

    def whole_rows(a):
        return a.reshape(a.shape[:2] + (a.shape[2] * a.shape[3], a.shape[4]))

    part = {g: [row_halves(shards[n][s:s + cnt]) for n, s, cnt in entries] for g, entries in groups.items()}
    gather_sems, after = {}, chip_idx
    for group in groups:
        placed = [_place_own(a, chip_idx, c_idx, name=f"gather_place_{group}_{n}")
                  for a, (n, _, _) in zip(part[group], groups[group])]
        gather_sems[group] = _split_start(_gather_copies, part[group], placed, after, name=f"gather_{group}_start",
                                          sems_per_array=4)
        after = gather_sems[group][4]

    def layer_weights(w, group, arrays):
        for (n, s, cnt), a in zip(groups[group], arrays):
            for key, view in _weight_views(n, whole_rows(a), D, n_fox_heads).items():
                for l in range(cnt):
                    w[key][s + l] = (view, l)

    w = {key: [None] * L2 for key in ("fox_qkv", "fox_f", "fox_out", "mla_down", "mla_uq", "mla_ukv", "mla_out")}
    w.update({key: [None] * L for key in ("mlp_w1", "mlp_w2")})

    def gathered_now(group):
        def hook(x_now, w):
            _, landed = _split_wait(_gather_copies, *gather_sems[group][:4], x_now, name=f"gather_{group}_wait")
            layer_weights(w, group, _gather_forward(landed, name=f"gather_{group}_forward"))
            return w
        return hook

    c_pad = jnp.concatenate([c, jnp.pad(mla_q_norm_g, ((0, 8 - Bl - L2), (0, D - q_cols)))], axis=0) + after[0, 0]
    c8 = _all_gather8(c_pad, name="gather_c", in_vmem=True)
    c_all = c8[:, :Bl].reshape(N_DEV * Bl, D)
    qg4 = c8.reshape(N_CHIP, 2, 8, D)[:, 0, Bl:Bl + L2, :q_cols]
    small["mla_q_norm_g"] = jnp.transpose(qg4, (1, 0, 2)).reshape(L2, N_CHIP * q_cols)
    ada_b_cols = lax.dynamic_slice_in_dim(ada_b, chip * C, C, axis=1)[:, None, :]
    mod_cols = _ada_fwd(c_all, ada_w, ada_b_cols)
    mod8 = _all_gather8(mod_cols.reshape(L * N_DEV * Bl, C), name="gather_mod", in_vmem=True)
    mod4 = mod8.reshape(N_CHIP, 2, L, N_DEV * Bl, C)[:, 0]
    mod_me = lax.dynamic_slice_in_dim(mod4, dev * Bl, Bl, axis=2)
    mod = jnp.transpose(mod_me, (1, 2, 0, 3)).reshape(L, Bl, 6, D)
    mod = jnp.transpose(mod, (0, 2, 1, 3))[:, :, :, None, :]

    w.update(_small_layouts(small))
    w = gathered_now("mix0")(mod, w)
    pending = {}

    def grad_pieces(group, g_now):
        out = []
        for n, s, cnt in groups[group]:
            qkv_f = [(g_now["fox_qkv"][j], g_now["fox_f"][j]) for j in range(s, s + cnt)] if n == "fox_in" else None
            stacked_g = None if n == "fox_in" else g_now[n][group]
            out.append(row_halves(_grad_pieces(n, stacked_g, qkv_f, n_fox_heads, N_CHIP)))
        return out

    def pair_added(group, big, sibling):
        return [_pair_add(a, r, c_idx, name=f"grad_pair_add_{group}_{n}")
                for (n, _, _), a, r in zip(groups[group], big, sibling)]

    def exchange_start(group, ps, after=None):
        pending[group] = _split_start(_chip_copies, ps, _chip_landing(ps), chip_idx if after is None else after,
                                      name=f"grad_exchange_{group}_start", sems_per_array=3)
        return pending[group][4]

    def bwd_layer0(g_now):
        big = grad_pieces("rest", g_now)
        landing = [lax.empty(a.shape[:2] + a.shape[3:], a.dtype) for a in big]
        pending["rest_pair"] = _split_start(_pair_copies, big, landing, chip_idx, name="grad_pair_rest_start",
                                            sems_per_array=1)
        return pending["rest_pair"][4]

    def bwd_mix0(g_now):
        send_sems, recv_sems, big, landed, _ = pending["rest_pair"]
        big, landed = _split_wait(_pair_copies, send_sems, recv_sems, big, landed, g_now["mlp_w1"]["mlp0"],
                                  name="grad_pair_rest_wait")
        started = exchange_start("rest", pair_added("rest", big, landed))
        big = grad_pieces("mlp0", g_now)
        return exchange_start("mlp0", pair_added("mlp0", big, _pair_exchange(big, name="grad_pair_exchange_mlp0")),
                              after=started)

    half = ROPE_DIM // 2
    inv_freq = ROPE_THETA ** (-jnp.arange(0, ROPE_DIM, 2, dtype=F32) / ROPE_DIM)
    lane = np.arange(LANES)
    inv_freq_row = jnp.tile(inv_freq, LANES // half)[None, :]
    sign_row = jnp.asarray(np.where(lane < 2 * ROPE_DIM, np.where(lane % ROPE_DIM < half, -1.0, 1.0), 0.0), F32)[None, :]
    pos_f = positions.astype(F32).reshape(T, 1)
    loss_row, grad_x, dmod, g = _local_step(x.reshape(T, D), loss_target.reshape(T, D), pos_f, inv_freq_row, sign_row,
                                            mod, w, slots, S=S,
                                            hooks={"fwd_mlp0": gathered_now("mlp0"), "fwd_layer1": gathered_now("rest"),
                                                   "bwd_layer0": bwd_layer0, "bwd_mix0": bwd_mix0})
    g_small = _small_grads(g, n_fox_heads)
    big = grad_pieces("mix0", g)
    exchange_start("mix0", pair_added("mix0", big, _pair_exchange(big, name="grad_pair_exchange_mix0")))

    Rs = -(-(2 * L + 5) // 8) * 8
    srows = jnp.concatenate([_small_rows(g_small, D), jnp.pad(loss_row, ((0, 0), (0, D - LANES)))], axis=0)
    srows = jnp.pad(srows, ((0, Rs - srows.shape[0]), (0, 0)))
    drows = jnp.transpose(dmod[:, :, :, 0, :], (2, 0, 1, 3)).reshape(Bl * L * 6, D)
    both8 = _all_gather8(jnp.concatenate([drows, srows], axis=0), name="gather_small", in_vmem=True)
    dm8 = both8[:, :Bl * L * 6].reshape(N_DEV, Bl, L * 6, D)
    sm8 = both8[:, Bl * L * 6:]
    adb_rows, small_sum = _sum_gathered(dm8, sm8)
    grad_ada_b = adb_rows.reshape(L, 6 * D)
    loss = small_sum[2 * L + 4, 0]
    small_shapes = {n: (wts[n].shape if n != "mla_q_norm_g" else (wts[n].shape[0], N_CHIP * q_cols)) for n in SMALL}
    gs = _small_unrows(small_sum, small_shapes)
    gs["mla_q_norm_g"] = lax.dynamic_slice_in_dim(gs["mla_q_norm_g"], chip * q_cols, q_cols, axis=1)

    dmod16 = jnp.transpose(dm8.reshape(N_DEV, Bl, L, 6 * D), (2, 0, 1, 3)).reshape(L, N_DEV * Bl, 6 * D)
    dmod_cols = lax.dynamic_slice_in_dim(dmod16, chip * C, C, axis=2)
    grad_ada_w = _ada_bwd(c_all, dmod_cols)

    grads = dict(gs)
    grads["ada_w"] = grad_ada_w
    grads["ada_b"] = grad_ada_b
    delta, new_m, new_v = {}, {}, {}
    for n in ("ada_w", "ada_b"):
        delta[n], new_m[n], new_v[n] = _adamw(wts[n], grads[n], mom[n], var[n], name=f"adamw_{n}")
    shard_small_shapes = {n: wts[n].shape for n in SMALL}
    packs = [jnp.pad(_small_rows({n: src[n] for n in SMALL}, D), ((0, Rs - 2 * L - 4), (0, 0)))
             for src in (wts, grads, mom, var)]
    for dst, rows in zip((delta, new_m, new_v), _adamw(*packs, name="adamw_small")):
        dst.update(_small_unrows(rows, shard_small_shapes))

    halves = {}
    for group, after in (("rest", grad_x), ("mlp0", grad_x), ("mix0", delta["ada_w"])):
        send_sems, recv_sems, ps, lands, _ = pending[group]
        ps, lands = _split_wait(_chip_copies, send_sems, recv_sems, ps, lands, after, name=f"grad_exchange_{group}_wait")
        sums = [_sum_pieces(ld, p, chip_idx, name=f"grad_sum_{group}_{n}")
                for (n, _, _), ld, p in zip(groups[group], lands, ps)]
        swapped = _pair_swap(sums, name=f"grad_pair_swap_{group}")
        for (n, _, _), a, b in zip(groups[group], sums, swapped):
            halves[(n, group)] = (a, b)

    def all_layers(n, which):
        return jnp.concatenate([halves[(n, grp)][which] for grp in groups if (n, grp) in halves], axis=0)

    own = {n: all_layers(n, 0) for n in GATHERED}
    peer = {n: all_layers(n, 1) for n in GATHERED}
    for nat, n in (("fox_w_in", "fox_in"), ("fox_w_out", "fox_out"), ("mla_w_out", "mla_out"), ("mlp_w1", "mlp_w1"),
                   ("mlp_w2", "mlp_w2")):
        cols = wts[nat].shape[-1]
        res = _adamw_halves(_pad_lanes(wts[nat]), own[n], peer[n], _pad_lanes(mom[nat]), _pad_lanes(var[nat]), c_idx,
                            name=f"adamw_{nat}")
        grads[nat], delta[nat], new_m[nat], new_v[nat] = (a[..., :cols] for a in res)
    joined = {n: jnp.concatenate([jnp.where(mc == 0, own[n], peer[n]), jnp.where(mc == 0, peer[n], own[n])], axis=1)
              for n in ("mla_down", "mla_uq", "mla_ukv")}
    rq = mla_w_dq.shape[-1]
    grads["mla_w_dq"] = joined["mla_down"][:, :, :rq]
    grads["mla_w_dkv"] = joined["mla_down"][:, :, rq:rq + KV_RANK + ROPE_DIM]
    grads["mla_w_uq"] = jax.vmap(_uq_from_pairs)(joined["mla_uq"])
    grads["mla_w_ukv"] = jax.vmap(_ukv_from_pairs)(joined["mla_ukv"])
    for n in ("mla_w_dq", "mla_w_dkv", "mla_w_uq", "mla_w_ukv"):
        delta[n], new_m[n], new_v[n] = _adamw(wts[n], grads[n], mom[n], var[n], name=f"adamw_{n}")

    return (loss, grad_x.reshape(Bl, S, D), *[grads[n] for n in WEIGHT_ORDER], *[delta[n] for n in WEIGHT_ORDER],
            *[new_m[n] for n in WEIGHT_ORDER], *[new_v[n] for n in WEIGHT_ORDER])
```

```python
import numpy as np
import jax
import jax.numpy as jnp
from jax import lax
from jax.experimental import pallas as pl
from jax.experimental.pallas import tpu as pltpu

F32 = jnp.float32
BF16 = jnp.bfloat16
MESH_ID = pl.DeviceIdType.MESH

NORM_EPS = 1e-6
ROPE_THETA = 10000.0
HEAD_DIM = 64
ROPE_DIM = 32
KV_RANK = 128
FOX_EXTRA = 6
PAIR_Q = 256
PAIR_KV = 384
LANES = 128
ADAM_LR = 0.001
ADAM_B1 = 0.9
ADAM_B2 = 0.999
ADAM_EPS = 1e-08
ADAM_WD = 0.01
ADAM_STEP = 10
VMEM_LIMIT_V7X = 48 * 1024 * 1024
MM_VMEM_BUDGET = 36 * 1024 * 1024
NEG_BIG = -1e30
ATTN_UNROLL = 4
ATTN_BLOCK = 256
ATTN_Q_ROWS = 512
ATTN_K_ROWS = 512

BIG_WEIGHTS = (("fox_w_in", 2), ("fox_w_out", 1), ("mla_w_dq", 1), ("mla_w_uq", 2), ("mla_w_dkv", 1),
               ("mla_w_ukv", 2), ("mla_w_out", 1), ("mlp_w1", 2), ("mlp_w2", 1))


def _cparams(sem=None):
    return pltpu.CompilerParams(dimension_semantics=sem, vmem_limit_bytes=VMEM_LIMIT_V7X)


def _tile(n, want):
    if n <= want:
        return n
    for t in range(want - want % LANES, 0, -LANES):
        if n % t == 0:
            return t
    raise ValueError((n, want))


def _mm(a, b, mode, *, name, out_dtypes=(F32,), epilogue=None, extras=(), rowvecs=(), tables=(),
        seq=None, a_off=0, a_sz=None, b_layer=None, out_stack=None, out_split=0, out_t=(), tm=1024, tn=1024,
        tk=2048):
    if isinstance(b, (list, tuple)):
        b, b_layer = b[b_layer]
    b_rows, b_cols = b.shape[-2], b.shape[-1]
    n_split = b.shape[1] if b.ndim == 4 else 1
    assert mode in ("nn", "nt")
    if mode == "nn":
        M, K, N = a.shape[0], b_rows, b_cols * n_split
    else:
        M, K, N = a.shape[0], b_cols * n_split, b_rows
    assert a_sz is None or a_sz == K
    tm = _tile(seq if rowvecs else M, tm)
    n_piece = N // max(out_split, n_split if mode == "nn" else 1, 1)
    tn = _tile(n_piece, tn)
    tk = _tile(K // (n_split if mode == "nt" else 1), tk)
    ne, nr, nt_ = len(extras), len(rowvecs), len(tables)
    no = len(out_dtypes)

    def vmem_estimate():
        blocks = tm * tk * a.dtype.itemsize + tk * tn * b.dtype.itemsize
        blocks += tm * tn * (sum(e.dtype.itemsize for e in extras) + sum(jnp.dtype(d).itemsize for d in out_dtypes))
        return 2 * blocks + 2 * tm * tn * 4

    while vmem_estimate() > MM_VMEM_BUDGET and max(tm, tn) > 256:
        if tn >= tm:
            tn //= 2
        else:
            tm //= 2
    nk = K // tk

    assert a_off % tk == 0
    a_spec = pl.BlockSpec((tm, tk), lambda i, j, k: (i, k + a_off // tk))
    dims = (((1,), (0,)), ((), ())) if mode == "nn" else (((1,), (1,)), ((), ()))
    lead = () if b.ndim == 2 else (b_layer,)
    sq = (None,) * (b.ndim - 2)
    if mode == "nt":
        kb = b_cols // tk
        if b.ndim == 4:
            b_spec = pl.BlockSpec(sq + (tn, tk), lambda i, j, k: lead + (k // kb, j, k % kb))
        else:
            b_spec = pl.BlockSpec(sq + (tn, tk), lambda i, j, k: lead + (j, k))
    else:
        nb = b_cols // tn
        if b.ndim == 4:
            b_spec = pl.BlockSpec(sq + (tk, tn), lambda i, j, k: lead + (j // nb, k, j % nb))
        else:
            b_spec = pl.BlockSpec(sq + (tk, tn), lambda i, j, k: lead + (k, j))
    in_specs = [a_spec, b_spec]
    in_specs += [pl.BlockSpec((tm, tn), lambda i, j, k: (i, j)) for _ in extras]
    if rowvecs:
        assert seq % tm == 0
        per = seq // tm
        in_specs += [pl.BlockSpec((None, 1, tn), lambda i, j, k: (i // per, 0, j)) for _ in rowvecs]
    in_specs += [pl.BlockSpec((tm, LANES), lambda i, j, k: (i, 0)) for _ in tables]
    operands = [a, b, *extras, *rowvecs, *tables]
    aliases = {}
    transposed = tuple(out_t) + (False,) * (no - len(out_t))
    if out_stack is None:
        out_specs = [pl.BlockSpec((tn, tm), lambda i, j, k: (j, i)) if t else pl.BlockSpec((tm, tn), lambda i, j, k: (i, j))
                     for t in transposed]
        out_shape = [jax.ShapeDtypeStruct((N, M) if t else (M, N), d) for d, t in zip(out_dtypes, transposed)]
    else:
        prev, layer, n_layers = out_stack
        assert no == 1
        if out_split:
            ob = n_piece // tn
            out_specs = [pl.BlockSpec((None, None, tm, tn), lambda i, j, k: (layer, j // ob, i, j % ob))]
            out_shape = [jax.ShapeDtypeStruct((n_layers, out_split, M, n_piece), out_dtypes[0])]
        else:
            out_specs = [pl.BlockSpec((None, tm, tn), lambda i, j, k: (layer, i, j))]
            out_shape = [jax.ShapeDtypeStruct((n_layers, M, N), out_dtypes[0])]
        if prev is not None:
            in_specs.append(pl.BlockSpec(memory_space=pl.ANY))
            aliases = {len(operands): 0}
            operands.append(prev)
    n_in = len(operands)

    def body(*refs):
        a_ref, b_ref = refs[0], refs[1]
        side = refs[2:2 + ne + nr + nt_]
        outs = refs[n_in:n_in + no]

        def finish(acc):
            res = (acc,) if epilogue is None else epilogue(acc, *[r[...] for r in side])
            for o_ref, r, t in zip(outs, res, transposed):
                o_ref[...] = (r.T if t else r).astype(o_ref.dtype)

        part = lax.dot_general(a_ref[...].astype(BF16), b_ref[...].astype(BF16), dims,
                               preferred_element_type=F32)
        if nk == 1:
            finish(part)
        else:
            acc_ref = refs[-1]
            k = pl.program_id(2)

            @pl.when(k == 0)
            def _():
                acc_ref[...] = part

            @pl.when(k > 0)
            def _():
                acc_ref[...] += part

            @pl.when(k == nk - 1)
            def _():
                finish(acc_ref[...])

    res = pl.pallas_call(
        body, name=name, grid=(M // tm, N // tn, nk), in_specs=in_specs, out_specs=out_specs,
        out_shape=out_shape, scratch_shapes=[pltpu.VMEM((tm, tn), F32)] if nk > 1 else [],
        input_output_aliases=aliases,
        compiler_params=_cparams(("parallel", "parallel", "arbitrary")),
    )(*operands)
    return res[0] if no == 1 else tuple(res)


def _rope128(x, cos_t, sin_s):
    lane = lax.broadcasted_iota(jnp.int32, x.shape, 1)
    first = (lane % ROPE_DIM) < (ROPE_DIM // 2)
    swapped = jnp.where(first, pltpu.roll(x, LANES - ROPE_DIM // 2, 1), pltpu.roll(x, ROPE_DIM // 2, 1))
    return x * cos_t + swapped * sin_s


def _rope_pairs(acc, cos_t, sin_s, sign):
    parts = []
    for p in range(acc.shape[1] // PAIR_Q):
        parts.append(acc[:, p * PAIR_Q:p * PAIR_Q + LANES])
        parts.append(_rope128(acc[:, p * PAIR_Q + LANES:(p + 1) * PAIR_Q], cos_t, sign * sin_s))
    return jnp.concatenate(parts, axis=1)


def _rope_tables(pos_f, inv_freq_row, sign_row):
    T = pos_f.shape[0]
    tt = _tile(T, 512)

    def body(p_ref, f_ref, s_ref, cos_ref, sin_ref):
        ang = p_ref[...] * f_ref[...]
        cos_ref[...] = jnp.cos(ang)
        sin_ref[...] = jnp.sin(ang) * s_ref[...]

    return pl.pallas_call(
        body, name="rope_tables", grid=(T // tt,),
        in_specs=[pl.BlockSpec((tt, 1), lambda i: (i, 0)), pl.BlockSpec((1, LANES), lambda i: (0, 0)),
                  pl.BlockSpec((1, LANES), lambda i: (0, 0))],
        out_specs=[pl.BlockSpec((tt, LANES), lambda i: (i, 0))] * 2,
        out_shape=[jax.ShapeDtypeStruct((T, LANES), F32)] * 2,
        compiler_params=_cparams(("parallel",)),
    )(pos_f, inv_freq_row, sign_row)


def _unrope(dqx, cos_t, sin_s):
    T, W = dqx.shape
    tt = _tile(T, 512)

    def body(d_ref, c_ref, s_ref, o_ref):
        o_ref[...] = _rope_pairs(d_ref[...].astype(F32), c_ref[...], s_ref[...], -1.0).astype(BF16)

    return pl.pallas_call(
        body, name="mla_unrope", grid=(T // tt,),
        in_specs=[pl.BlockSpec((tt, W), lambda i: (i, 0)), pl.BlockSpec((tt, LANES), lambda i: (i, 0)),
                  pl.BlockSpec((tt, LANES), lambda i: (i, 0))],
        out_specs=pl.BlockSpec((tt, W), lambda i: (i, 0)),
        out_shape=jax.ShapeDtypeStruct((T, W), BF16),
        compiler_params=_cparams(("parallel",)),
    )(dqx, cos_t, sin_s)


def _row_specs(tt, D, per, n):
    return [pl.BlockSpec((None, 1, D), lambda i: (i // per, 0, 0)) for _ in range(n)]


def _norm_mod(x, gain, sc, sh, *, S, name):
    T, D = x.shape
    tt = _tile(S, 512)
    per = S // tt

    def body(x_ref, g_ref, sc_ref, sh_ref, h_ref, ht_ref):
        xv = x_ref[...]
        r = lax.rsqrt(jnp.mean(xv * xv, axis=-1, keepdims=True) + NORM_EPS)
        h = (xv * r) * g_ref[...] * (1.0 + sc_ref[...]) + sh_ref[...]
        h_ref[...] = h.astype(BF16)
        ht_ref[...] = h.T.astype(BF16)

    return pl.pallas_call(
        body, name=name, grid=(T // tt,),
        in_specs=[pl.BlockSpec((tt, D), lambda i: (i, 0)), pl.BlockSpec((1, D), lambda i: (0, 0))]
        + _row_specs(tt, D, per, 2),
        out_specs=[pl.BlockSpec((tt, D), lambda i: (i, 0)), pl.BlockSpec((D, tt), lambda i: (0, i))],
        out_shape=[jax.ShapeDtypeStruct((T, D), BF16), jax.ShapeDtypeStruct((D, T), BF16)],
        compiler_params=_cparams(("parallel",)),
    )(x, gain, sc, sh)


def _norm_mod_bwd(x, dh, dres, gain, sc, *, S, name):
    T, D = x.shape
    B = T // S
    tt = _tile(S, 512)
    per = S // tt

    def body(x_ref, dh_ref, dres_ref, g_ref, sc_ref, dx_ref, dsh_ref, dsc_ref, dg_ref):
        i = pl.program_id(0)
        xv = x_ref[...]
        dhv = dh_ref[...].astype(F32)
        r = lax.rsqrt(jnp.mean(xv * xv, axis=-1, keepdims=True) + NORM_EPS)
        n = xv * r
        g = g_ref[...]
        one_sc = 1.0 + sc_ref[...]
        dn = dhv * (g * one_sc)
        dx_ref[...] = dres_ref[...] + r * (dn - n * jnp.mean(dn * n, axis=-1, keepdims=True))
        dhn = dhv * n

        @pl.when(i % per == 0)
        def _():
            dsh_ref[...] = jnp.zeros_like(dsh_ref)
            dsc_ref[...] = jnp.zeros_like(dsc_ref)

        @pl.when(i == 0)
        def _():
            dg_ref[...] = jnp.zeros_like(dg_ref)

        dsh_ref[...] += jnp.sum(dhv, axis=0, keepdims=True)
        dsc_ref[...] += jnp.sum(dhn, axis=0, keepdims=True) * g
        dg_ref[...] += jnp.sum(dhn, axis=0, keepdims=True) * one_sc

    return pl.pallas_call(
        body, name=name, grid=(T // tt,),
        in_specs=[pl.BlockSpec((tt, D), lambda i: (i, 0))] * 3 + [pl.BlockSpec((1, D), lambda i: (0, 0))]
        + _row_specs(tt, D, per, 1),
        out_specs=[pl.BlockSpec((tt, D), lambda i: (i, 0))] + _row_specs(tt, D, per, 2)
        + [pl.BlockSpec((1, D), lambda i: (0, 0))],
        out_shape=[jax.ShapeDtypeStruct((T, D), F32), jax.ShapeDtypeStruct((B, 1, D), F32),
                   jax.ShapeDtypeStruct((B, 1, D), F32), jax.ShapeDtypeStruct((1, D), F32)],
        compiler_params=_cparams(("arbitrary",)),
    )(x, dh, dres, gain, sc)


def _gate_bwd(dx, y, g, *, S, name):
    T, D = dx.shape
    B = T // S
    tt = _tile(S, 512)
    per = S // tt

    def body(dx_ref, y_ref, g_ref, dy_ref, dg_ref):
        i = pl.program_id(0)
        dxv = dx_ref[...]
        dy_ref[...] = (dxv * g_ref[...]).astype(BF16)

        @pl.when(i % per == 0)
        def _():
            dg_ref[...] = jnp.zeros_like(dg_ref)

        dg_ref[...] += jnp.sum(dxv * y_ref[...], axis=0, keepdims=True)

    return pl.pallas_call(
        body, name=name, grid=(T // tt,),
        in_specs=[pl.BlockSpec((tt, D), lambda i: (i, 0))] * 2 + _row_specs(tt, D, per, 1),
        out_specs=[pl.BlockSpec((tt, D), lambda i: (i, 0))] + _row_specs(tt, D, per, 1),
        out_shape=[jax.ShapeDtypeStruct((T, D), BF16), jax.ShapeDtypeStruct((B, 1, D), F32)],
        compiler_params=_cparams(("arbitrary",)),
    )(dx, y, g)


def _final_loss(x, target, gain):
    T, D = x.shape
    tt = _tile(T, 512)

    def body(x_ref, t_ref, g_ref, dx_ref, dg_ref, loss_ref):
        i = pl.program_id(0)
        xv = x_ref[...]
        r = lax.rsqrt(jnp.mean(xv * xv, axis=-1, keepdims=True) + NORM_EPS)
        n = xv * r
        g = g_ref[...]
        err = n * g - t_ref[...]
        dy = err * (1.0 / D)
        dn = dy * g
        dx_ref[...] = r * (dn - n * jnp.mean(dn * n, axis=-1, keepdims=True))

        @pl.when(i == 0)
        def _():
            dg_ref[...] = jnp.zeros_like(dg_ref)
            loss_ref[...] = jnp.zeros_like(loss_ref)

        dg_ref[...] += jnp.sum(dy * n, axis=0, keepdims=True)
        loss_ref[...] += jnp.sum(jnp.sum(err * err, axis=-1, keepdims=True), axis=0, keepdims=True) * (0.5 / D)

    return pl.pallas_call(
        body, name="final_loss", grid=(T // tt,),
        in_specs=[pl.BlockSpec((tt, D), lambda i: (i, 0))] * 2 + [pl.BlockSpec((1, D), lambda i: (0, 0))],
        out_specs=[pl.BlockSpec((tt, D), lambda i: (i, 0)), pl.BlockSpec((1, D), lambda i: (0, 0)),
                   pl.BlockSpec((1, LANES), lambda i: (0, 0))],
        out_shape=[jax.ShapeDtypeStruct((T, D), F32), jax.ShapeDtypeStruct((1, D), F32),
                   jax.ShapeDtypeStruct((1, LANES), F32)],
        compiler_params=_cparams(("arbitrary",)),
    )(x, target, gain)


def _head_masks(ew):
    lane = lax.broadcasted_iota(jnp.int32, (1, PAIR_Q), 1)
    m0 = (lane < HEAD_DIM) | ((lane >= LANES) & (lane < LANES + ew))
    m1 = ((lane >= HEAD_DIM) & (lane < LANES)) | ((lane >= LANES + ew) & (lane < LANES + 2 * ew))
    return m0, m1


def _dot_nt(a, b):
    return lax.dot_general(a, b, (((1,), (1,)), ((), ())), preferred_element_type=F32)


def _dot_tn(a, b):
    return lax.dot_general(a, b, (((0,), (0,)), ((), ())), preferred_element_type=F32)


def _lane_halves(x, op):
    acc = x[:, 0:LANES]
    for g in range(1, x.shape[1] // LANES):
        acc = op(acc, x[:, g * LANES:(g + 1) * LANES])
    return acc


def _head_rows(cols_lane_replicated):
    t = cols_lane_replicated.T
    sub = lax.broadcasted_iota(jnp.int32, (8, t.shape[1]), 0)
    return jnp.where(sub == 1, t[HEAD_DIM:HEAD_DIM + 8], t[0:8])


def _attn_fwd(qx, kvx, *, S, scale, ew, name):
    T = qx.shape[0]
    P = qx.shape[1] // PAIR_Q
    B = T // S
    tk = _tile(S, ATTN_BLOCK)
    tq = _tile(S, ATTN_Q_ROWS)
    nq = S // tq
    per = tq // tk

    def body(q_ref, kv_ref, o_ref, lse_ref, ot_ref, m_sc, l_sc, acc_sc):
        qi = pl.program_id(2)
        q = q_ref[...]
        masks = _head_masks(ew)
        qh = [jnp.where(m, q, jnp.zeros_like(q)) for m in masks]

        def logits(h, k, diagonal):
            s = _dot_nt(qh[h], k)
            if scale != 1.0:
                s = s * scale
            if diagonal is None:
                return s
            row = lax.broadcasted_iota(jnp.int32, s.shape, 0)
            col = lax.broadcasted_iota(jnp.int32, s.shape, 1)
            return jnp.where(col + diagonal * tk <= row, s, NEG_BIG)

        def trip(first, count, n_diagonal=0):
            rows = [pl.ds(pl.multiple_of((first + u) * tk, tk), tk) for u in range(count)]
            diag = [None] * (count - n_diagonal) + list(range(n_diagonal))
            for h in range(2):
                ss = [logits(h, kv_ref[rows[u], 0:PAIR_Q], diag[u]) for u in range(count)]
                m_prev = m_sc[h]
                m_elem = m_prev
                for s in ss:
                    m_elem = jnp.maximum(m_elem, _lane_halves(s, jnp.maximum))
                m_new = jnp.broadcast_to(jnp.max(m_elem, axis=1, keepdims=True), (tq, LANES))
                alpha = jnp.exp(m_prev - m_new)
                l = alpha * l_sc[h]
                acc = alpha * acc_sc[h]
                for u, s in enumerate(ss):
                    p = jnp.concatenate([jnp.exp(s[:, g * LANES:(g + 1) * LANES] - m_new)
                                         for g in range(tk // LANES)], axis=1)
                    l = l + _lane_halves(p, jnp.add)
                    acc = acc + jnp.dot(p.astype(BF16), kv_ref[rows[u], PAIR_Q:PAIR_KV], preferred_element_type=F32)
                m_sc[h] = m_new
                l_sc[h] = l
                acc_sc[h] = acc

        m_sc[...] = jnp.full(m_sc.shape, NEG_BIG, F32)
        l_sc[...] = jnp.zeros_like(l_sc)
        acc_sc[...] = jnp.zeros_like(acc_sc)

        def loop_body(t, carry):
            trip(t * ATTN_UNROLL, ATTN_UNROLL)
            return carry

        below = qi * per
        lax.fori_loop(0, below // ATTN_UNROLL, loop_body, 0)
        for left in range(0, ATTN_UNROLL, per):
            @pl.when(below % ATTN_UNROLL == left)
            def _(left=left):
                trip(below - left, left + per, n_diagonal=per)

        lane = lax.broadcasted_iota(jnp.int32, (tq, LANES), 1)
        lo = lane < HEAD_DIM
        l = [jnp.sum(l_sc[h], axis=1, keepdims=True) for h in range(2)]
        o = jnp.where(lo, acc_sc[0] / l[0], acc_sc[1] / l[1])
        o_ref[...] = o.astype(BF16)
        ot_ref[...] = o.T.astype(BF16)
        lse = jnp.where(lo, m_sc[0] + jnp.log(l[0]), m_sc[1] + jnp.log(l[1]))
        for r in range(per):
            lse_ref[r] = _head_rows(lse[r * tk:(r + 1) * tk])

    return pl.pallas_call(
        body, name=name, grid=(B, P, nq),
        in_specs=[pl.BlockSpec((tq, PAIR_Q), lambda b, p, i: (b * nq + i, p)),
                  pl.BlockSpec((S, PAIR_KV), lambda b, p, i: (b, p))],
        out_specs=[pl.BlockSpec((tq, LANES), lambda b, p, i: (b * nq + i, p)),
                   pl.BlockSpec((per, None, 8, tk), lambda b, p, i: (b * nq + i, p, 0, 0)),
                   pl.BlockSpec((LANES, tq), lambda b, p, i: (p, b * nq + i))],
        out_shape=[jax.ShapeDtypeStruct((T, P * LANES), BF16), jax.ShapeDtypeStruct((T // tk, P, 8, tk), F32),
                   jax.ShapeDtypeStruct((P * LANES, T), BF16)],
        scratch_shapes=[pltpu.VMEM((2, tq, LANES), F32)] * 3,
        compiler_params=_cparams(("parallel", "parallel", "arbitrary")),
    )(qx, kvx)


def _attn_bwd(qx, kvx, o, lse, do, *, S, scale, ew, name, bias_grad=False):
    T = qx.shape[0]
    P = qx.shape[1] // PAIR_Q
    B = T // S
    tq = _tile(S, ATTN_BLOCK)
    tk = _tile(S, ATTN_K_ROWS)
    nq = S // tq
    nk = S // tk
    per = tk // tq

    def body(q_ref, kv_ref, o_ref, lse_ref, do_ref, dq_ref, dkv_ref, *rest):
        kj = pl.program_id(2)
        if bias_grad:
            csum_ref, rsum_ref, dq_sc, delta_sc, dk_sc, dv_sc, cs_sc = rest
            cs_sc[...] = jnp.zeros_like(cs_sc)

            @pl.when(kj == 0)
            def _():
                rsum_ref[...] = jnp.zeros_like(rsum_ref)
        else:
            dq_sc, delta_sc, dk_sc, dv_sc = rest
        masks = _head_masks(ew)
        lo_q = lax.broadcasted_iota(jnp.int32, (tq, LANES), 1) < HEAD_DIM
        lo = lax.broadcasted_iota(jnp.int32, (tk, LANES), 1) < HEAD_DIM
        vmask = [lo, jnp.logical_not(lo)]

        @pl.when(kj == 0)
        def _():
            dq_sc[...] = jnp.zeros_like(dq_sc)
            for c in range(nq):
                rows = pl.ds(c * tq, tq)
                x = do_ref[rows, :].astype(F32) * o_ref[rows, :].astype(F32)
                r0 = jnp.sum(jnp.where(lo_q, x, 0.0), axis=1, keepdims=True)
                r1 = jnp.sum(jnp.where(lo_q, 0.0, x), axis=1, keepdims=True)
                delta_sc[c] = _head_rows(jnp.where(lo_q, r0, r1))

        k = kv_ref[:, 0:PAIR_Q]
        v = kv_ref[:, PAIR_Q:PAIR_KV]
        kh = [jnp.where(m, k, jnp.zeros_like(k)) for m in masks]
        vh = [jnp.where(m, v, jnp.zeros_like(v)) for m in vmask]
        dk_sc[...] = jnp.zeros_like(dk_sc)
        dv_sc[...] = jnp.zeros_like(dv_sc)

        def step(qi, diagonal):
            rows = pl.ds(pl.multiple_of(qi * tq, tq), tq)
            q = q_ref[rows, :]
            dov = do_ref[rows, :]
            lse8 = lse_ref[qi]
            dl8 = delta_sc[qi]
            for h in range(2):
                st = _dot_nt(kh[h], q)
                if scale != 1.0:
                    st = st * scale
                if diagonal is not None:
                    key = lax.broadcasted_iota(jnp.int32, st.shape, 0)
                    qry = lax.broadcasted_iota(jnp.int32, st.shape, 1)
                    st = jnp.where(key <= qry + diagonal * tq, st, NEG_BIG)
                pt = jnp.exp(st - lse8[h:h + 1, :])
                dpt = _dot_nt(vh[h], dov)
                dst = pt * (dpt - dl8[h:h + 1, :])
                if bias_grad:
                    cs_sc[h] += _lane_halves(dst, jnp.add)
                    rsum_ref[qi, h:h + 1, :] += jnp.sum(dst, axis=0, keepdims=True)
                if scale != 1.0:
                    dst = dst * scale
                ptb = pt.astype(BF16)
                dstb = dst.astype(BF16)
                dv_sc[h] += jnp.dot(ptb, dov, preferred_element_type=F32)
                dk_sc[h] += jnp.dot(dstb, q, preferred_element_type=F32)
                dq_sc[rows, :] += _dot_tn(dstb, kh[h])

        first = kj * per
        above = nq - per - first
        for left in range(0, ATTN_UNROLL, per):
            @pl.when(above % ATTN_UNROLL == left)
            def _(left=left):
                for d in range(per):
                    step(first + d, d)
                for u in range(left):
                    step(first + per + u, None)

        def loop_body(t, carry):
            for u in range(ATTN_UNROLL):
                step(first + per + above % ATTN_UNROLL + t * ATTN_UNROLL + u, None)
            return carry

        lax.fori_loop(0, above // ATTN_UNROLL, loop_body, 0)
        dkv_ref[:, 0:PAIR_Q] = (jnp.where(masks[0], dk_sc[0], 0.0) + jnp.where(masks[1], dk_sc[1], 0.0)).astype(BF16)
        dkv_ref[:, PAIR_Q:PAIR_KV] = jnp.where(lo, dv_sc[0], dv_sc[1]).astype(BF16)
        if bias_grad:
            csum_ref[...] = jnp.where(lo, jnp.sum(cs_sc[0], axis=1, keepdims=True),
                                      jnp.sum(cs_sc[1], axis=1, keepdims=True))

        @pl.when(kj == nk - 1)
        def _():
            dq_ref[...] = dq_sc[...].astype(BF16)

    rows_spec = pl.BlockSpec((nq, None, 8, tq), lambda b, p, j: (b, p, 0, 0))
    out_specs = [pl.BlockSpec((S, PAIR_Q), lambda b, p, j: (b, p)),
                 pl.BlockSpec((tk, PAIR_KV), lambda b, p, j: (b * nk + j, p))]
    out_shape = [jax.ShapeDtypeStruct((T, P * PAIR_Q), BF16), jax.ShapeDtypeStruct((T, P * PAIR_KV), BF16)]
    scratch = [pltpu.VMEM((S, PAIR_Q), F32), pltpu.VMEM((nq, 8, tq), F32),
               pltpu.VMEM((2, tk, PAIR_Q), F32), pltpu.VMEM((2, tk, LANES), F32)]
    if bias_grad:
        out_specs += [pl.BlockSpec((tk, LANES), lambda b, p, j: (b * nk + j, p)), rows_spec]
        out_shape += [jax.ShapeDtypeStruct((T, P * LANES), F32), jax.ShapeDtypeStruct((T // tq, P, 8, tq), F32)]
        scratch.append(pltpu.VMEM((2, tk, LANES), F32))
    return pl.pallas_call(
        body, name=name, grid=(B, P, nk),
        in_specs=[pl.BlockSpec((S, PAIR_Q), lambda b, p, j: (b, p)),
                  pl.BlockSpec((tk, PAIR_KV), lambda b, p, j: (b * nk + j, p)),
                  pl.BlockSpec((S, LANES), lambda b, p, j: (b, p)), rows_spec,
                  pl.BlockSpec((S, LANES), lambda b, p, j: (b, p))],
        out_specs=out_specs, out_shape=out_shape, scratch_shapes=scratch,
        compiler_params=_cparams(("parallel", "parallel", "arbitrary")),
    )(qx, kvx, o, lse, do)


def _fox_consts(P):
    H = 2 * P
    eq = np.zeros((3 * LANES, P * LANES), np.float32)
    ek = np.zeros((3 * LANES, P * LANES), np.float32)
    ones_q = np.zeros((1, P * LANES), np.float32)
    ones_k = np.zeros((1, P * LANES), np.float32)
    for h in range(H):
        base = (h // 2) * LANES + FOX_EXTRA * (h % 2)
        for part in range(3):
            eq[part * LANES + h, base + part] = 1.0
            ones_q[0, base + 3 + part] = 1.0
            ones_k[0, base + part] = 1.0
            ek[part * LANES + h, base + 3 + part] = -1.0
    return eq, ek, ones_q, ones_k


def _split3(f):
    hi = f.astype(BF16)
    r = f - hi.astype(F32)
    mid = r.astype(BF16)
    lo = (r - mid.astype(F32)).astype(BF16)
    return hi, mid, lo


def _tri_sum(tri, x):
    hi, mid, lo = _split3(x)
    return (jnp.dot(tri, hi, preferred_element_type=F32) + jnp.dot(tri, mid, preferred_element_type=F32)
            + jnp.dot(tri, lo, preferred_element_type=F32))


def _log1p_pos(e):
    return jnp.where(e < 0.01, e * (1.0 - e * (0.5 - e * (1.0 / 3.0))), jnp.log(1.0 + e))


def _fox_prep(qkv, fl, b_row, *, S, D, name):
    T = qkv.shape[0]
    P = D // LANES
    B = T // S
    tt = _tile(S, 256)
    per = S // tt
    eq, ek, ones_q, ones_k = _fox_consts(P)
    q_scale = HEAD_DIM ** -0.5

    def body(q_ref, k_ref, v_ref, fl_ref, b_ref, eq_ref, ek_ref, oq_ref, ok_ref, qx_ref, kvx_ref, carry):
        i = pl.program_id(1)

        @pl.when(i == 0)
        def _():
            carry[...] = jnp.zeros_like(carry)

        z = fl_ref[...] + b_ref[...]
        logf = jnp.minimum(z, 0.0) - _log1p_pos(jnp.exp(-jnp.abs(z)))
        row = lax.broadcasted_iota(jnp.int32, (tt, tt), 0)
        col = lax.broadcasted_iota(jnp.int32, (tt, tt), 1)
        tri = (col <= row).astype(BF16)
        f = _tri_sum(tri, logf) + carry[...]
        carry[...] = f[tt - 1:tt, :]
        parts = jnp.concatenate(_split3(f), axis=1)
        xq = jnp.dot(parts, eq_ref[...], preferred_element_type=F32) + oq_ref[...]
        xk = jnp.dot(parts, ek_ref[...], preferred_element_type=F32) + ok_ref[...]
        for p in range(P):
            c = slice(p * LANES, (p + 1) * LANES)
            qx_ref[:, p * PAIR_Q:p * PAIR_Q + LANES] = (q_ref[:, c].astype(F32) * q_scale).astype(BF16)
            qx_ref[:, p * PAIR_Q + LANES:(p + 1) * PAIR_Q] = xq[:, c].astype(BF16)
            kvx_ref[:, p * PAIR_KV:p * PAIR_KV + LANES] = k_ref[:, c]
            kvx_ref[:, p * PAIR_KV + LANES:p * PAIR_KV + PAIR_Q] = xk[:, c].astype(BF16)
            kvx_ref[:, p * PAIR_KV + PAIR_Q:(p + 1) * PAIR_KV] = v_ref[:, c]

    tok = lambda b, i: (b * per + i, 0)
    const = lambda b, i: (0, 0)
    return pl.pallas_call(
        body, name=name, grid=(B, per),
        in_specs=[pl.BlockSpec((tt, D), lambda b, i: (b * per + i, 0)),
                  pl.BlockSpec((tt, D), lambda b, i: (b * per + i, 1)),
                  pl.BlockSpec((tt, D), lambda b, i: (b * per + i, 2)),
                  pl.BlockSpec((tt, LANES), tok), pl.BlockSpec((1, LANES), const),
                  pl.BlockSpec(eq.shape, const), pl.BlockSpec(ek.shape, const),
                  pl.BlockSpec(ones_q.shape, const), pl.BlockSpec(ones_k.shape, const)],
        out_specs=[pl.BlockSpec((tt, P * PAIR_Q), tok), pl.BlockSpec((tt, P * PAIR_KV), tok)],
        out_shape=[jax.ShapeDtypeStruct((T, P * PAIR_Q), BF16), jax.ShapeDtypeStruct((T, P * PAIR_KV), BF16)],
        scratch_shapes=[pltpu.VMEM((1, LANES), F32)],
        compiler_params=_cparams(("arbitrary", "arbitrary")),
    )(qkv, qkv, qkv, fl, b_row, jnp.asarray(eq, BF16), jnp.asarray(ek, BF16), jnp.asarray(ones_q), jnp.asarray(ones_k))


def _fox_unprep(dqx, dkvx, csum, rsum, fl, b_row, *, S, D, name):
    T = dqx.shape[0]
    P = D // LANES
    B = T // S
    tt = _tile(S, 256)
    per = S // tt
    q_scale = HEAD_DIM ** -0.5

    def body(dq_ref, dkv_ref, cs_ref, rs_ref, fl_ref, b_ref, dqkv_ref, dfl_ref, db_ref, carry):
        b = pl.program_id(0)
        i = pl.program_id(1)

        @pl.when(i == 0)
        def _():
            carry[...] = jnp.zeros_like(carry)

        @pl.when((i == 0) & (b == 0))
        def _():
            db_ref[...] = jnp.zeros_like(db_ref)

        df = rs_ref[...] - cs_ref[...]
        for p in range(P):
            rq = slice(p * LANES, (p + 1) * LANES)
            dqkv_ref[:, rq] = (dq_ref[:, p * PAIR_Q:p * PAIR_Q + LANES].astype(F32) * q_scale).astype(BF16)
            dqkv_ref[:, D + p * LANES:D + (p + 1) * LANES] = dkv_ref[:, p * PAIR_KV:p * PAIR_KV + LANES]
            dqkv_ref[:, 2 * D + p * LANES:2 * D + (p + 1) * LANES] = dkv_ref[:, p * PAIR_KV + PAIR_Q:(p + 1) * PAIR_KV]
        row = lax.broadcasted_iota(jnp.int32, (tt, tt), 0)
        col = lax.broadcasted_iota(jnp.int32, (tt, tt), 1)
        tri = (col >= row).astype(BF16)
        dlogf = _tri_sum(tri, df) + carry[...]
        carry[...] = dlogf[0:1, :]
        z = fl_ref[...] + b_ref[...]
        e = jnp.exp(-jnp.abs(z))
        sig_neg = jnp.where(z >= 0.0, e, 1.0) / (1.0 + e)
        dfl = dlogf * sig_neg
        dfl_ref[...] = dfl.astype(BF16)
        db_ref[...] += jnp.sum(dfl, axis=0, keepdims=True)

    rev = lambda b, i: (b * per + per - 1 - i, 0)
    const = lambda b, i: (0, 0)
    return pl.pallas_call(
        body, name=name, grid=(B, per),
        in_specs=[pl.BlockSpec((tt, P * PAIR_Q), rev), pl.BlockSpec((tt, P * PAIR_KV), rev),
                  pl.BlockSpec((tt, LANES), rev), pl.BlockSpec((tt, LANES), rev), pl.BlockSpec((tt, LANES), rev),
                  pl.BlockSpec((1, LANES), const)],
        out_specs=[pl.BlockSpec((tt, 3 * D), rev), pl.BlockSpec((tt, LANES), rev), pl.BlockSpec((1, LANES), const)],
        out_shape=[jax.ShapeDtypeStruct((T, 3 * D), BF16), jax.ShapeDtypeStruct((T, LANES), BF16),
                   jax.ShapeDtypeStruct((1, LANES), F32)],
        scratch_shapes=[pltpu.VMEM((1, LANES), F32)],
        compiler_params=_cparams(("arbitrary", "arbitrary")),
    )(dqx, dkvx, csum, rsum, fl, b_row)


def _rms(x):
    r = lax.rsqrt(jnp.mean(x * x, axis=-1, keepdims=True) + NORM_EPS)
    return x * r, r


def _mla_mid(lat, gq, gkv, cos_t, sin_s, *, name):
    T, W = lat.shape
    Rq = W - 2 * LANES
    tt = _tile(T, 512)

    def body(l_ref, gq_ref, gkv_ref, c_ref, s_ref, o_ref, ot_ref):
        nq, _ = _rms(l_ref[:, 0:Rq])
        nkv, _ = _rms(l_ref[:, Rq:Rq + LANES])
        parts = [nq * gq_ref[...], nkv * gkv_ref[...], _rope128(l_ref[:, Rq + LANES:W], c_ref[...], s_ref[...])]
        out = jnp.concatenate(parts, axis=1)
        o_ref[...] = out.astype(BF16)
        ot_ref[...] = out.T.astype(BF16)

    return pl.pallas_call(
        body, name=name, grid=(T // tt,),
        in_specs=[pl.BlockSpec((tt, W), lambda i: (i, 0)), pl.BlockSpec((1, Rq), lambda i: (0, 0)),
                  pl.BlockSpec((1, LANES), lambda i: (0, 0)), pl.BlockSpec((tt, LANES), lambda i: (i, 0)),
                  pl.BlockSpec((tt, LANES), lambda i: (i, 0))],
        out_specs=[pl.BlockSpec((tt, W), lambda i: (i, 0)), pl.BlockSpec((W, tt), lambda i: (0, i))],
        out_shape=[jax.ShapeDtypeStruct((T, W), BF16), jax.ShapeDtypeStruct((W, T), BF16)],
        compiler_params=_cparams(("parallel",)),
    )(lat, gq, gkv, cos_t, sin_s)


def _mla_mid_bwd(lat, dcq, dckr, gq, gkv, cos_t, sin_s, *, name):
    T, W = lat.shape
    Rq = W - 2 * LANES
    tt = _tile(T, 512)

    def norm_bwd(x, dy, g):
        n, r = _rms(x)
        dn = dy * g
        return r * (dn - n * jnp.mean(dn * n, axis=-1, keepdims=True)), jnp.sum(dy * n, axis=0, keepdims=True)

    def body(l_ref, dq_ref, dk_ref, gq_ref, gkv_ref, c_ref, s_ref, o_ref, dgq_ref, dgkv_ref):
        i = pl.program_id(0)

        @pl.when(i == 0)
        def _():
            dgq_ref[...] = jnp.zeros_like(dgq_ref)
            dgkv_ref[...] = jnp.zeros_like(dgkv_ref)

        dxq, dgq = norm_bwd(l_ref[:, 0:Rq], dq_ref[...], gq_ref[...])
        dxkv, dgkv = norm_bwd(l_ref[:, Rq:Rq + LANES], dk_ref[:, 0:LANES], gkv_ref[...])
        o_ref[:, 0:Rq] = dxq.astype(BF16)
        o_ref[:, Rq:Rq + LANES] = dxkv.astype(BF16)
        o_ref[:, Rq + LANES:W] = _rope128(dk_ref[:, LANES:2 * LANES], c_ref[...], -s_ref[...]).astype(BF16)
        dgq_ref[...] += dgq
        dgkv_ref[...] += dgkv

    return pl.pallas_call(
        body, name=name, grid=(T // tt,),
        in_specs=[pl.BlockSpec((tt, W), lambda i: (i, 0)), pl.BlockSpec((tt, Rq), lambda i: (i, 0)),
                  pl.BlockSpec((tt, 2 * LANES), lambda i: (i, 0)), pl.BlockSpec((1, Rq), lambda i: (0, 0)),
                  pl.BlockSpec((1, LANES), lambda i: (0, 0)), pl.BlockSpec((tt, LANES), lambda i: (i, 0)),
                  pl.BlockSpec((tt, LANES), lambda i: (i, 0))],
        out_specs=[pl.BlockSpec((tt, W), lambda i: (i, 0)), pl.BlockSpec((1, Rq), lambda i: (0, 0)),
                   pl.BlockSpec((1, LANES), lambda i: (0, 0))],
        out_shape=[jax.ShapeDtypeStruct((T, W), BF16), jax.ShapeDtypeStruct((1, Rq), F32),
                   jax.ShapeDtypeStruct((1, LANES), F32)],
        compiler_params=_cparams(("arbitrary",)),
    )(lat, dcq, dckr, gq, gkv, cos_t, sin_s)


def _uq_to_pairs(w):
    Rq = w.shape[0]
    P = w.shape[1] // (2 * (HEAD_DIM + ROPE_DIM))
    w4 = w.reshape(Rq, P, 2, HEAD_DIM + ROPE_DIM)
    nope = w4[..., :HEAD_DIM].reshape(Rq, P, 2 * HEAD_DIM)
    rope = w4[..., HEAD_DIM:].reshape(Rq, P, 2 * ROPE_DIM)
    pad = jnp.zeros((Rq, P, PAIR_Q - 2 * HEAD_DIM - 2 * ROPE_DIM), w.dtype)
    return jnp.concatenate([nope, rope, pad], axis=-1).reshape(Rq, P * PAIR_Q)


def _uq_from_pairs(g):
    Rq = g.shape[0]
    P = g.shape[1] // PAIR_Q
    g3 = g.reshape(Rq, P, PAIR_Q)
    nope = g3[..., :2 * HEAD_DIM].reshape(Rq, P, 2, HEAD_DIM)
    rope = g3[..., 2 * HEAD_DIM:2 * HEAD_DIM + 2 * ROPE_DIM].reshape(Rq, P, 2, ROPE_DIM)
    return jnp.concatenate([nope, rope], axis=-1).reshape(Rq, P * 2 * (HEAD_DIM + ROPE_DIM))


def _ukv_to_pairs(w):
    P = w.shape[1] // (4 * HEAD_DIM)
    w4 = w.reshape(KV_RANK, P, 2, 2 * HEAD_DIM)
    kn = w4[..., :HEAD_DIM].reshape(KV_RANK, P, 2 * HEAD_DIM)
    vv = w4[..., HEAD_DIM:].reshape(KV_RANK, P, 2 * HEAD_DIM)
    top = jnp.concatenate([kn, jnp.zeros((KV_RANK, P, LANES), w.dtype), vv], axis=-1)
    place = np.zeros((LANES, P, PAIR_KV), np.float32)
    for r in range(ROPE_DIM):
        place[r, :, LANES + r] = 1.0
        place[r, :, LANES + ROPE_DIM + r] = 1.0
    return jnp.concatenate([top, jnp.asarray(place, w.dtype)], axis=0).reshape(KV_RANK + LANES, P * PAIR_KV)


def _ukv_from_pairs(g):
    P = g.shape[1] // PAIR_KV
    g3 = g[:KV_RANK].reshape(KV_RANK, P, PAIR_KV)
    kn = g3[..., :2 * HEAD_DIM].reshape(KV_RANK, P, 2, HEAD_DIM)
    vv = g3[..., PAIR_Q:].reshape(KV_RANK, P, 2, HEAD_DIM)
    return jnp.concatenate([kn, vv], axis=-1).reshape(KV_RANK, P * 4 * HEAD_DIM)


def _mlp_fwd(h2, w, i, x1, gate, *, S):
    def act(acc):
        u = jnp.square(jnp.maximum(acc, 0.0))
        return acc, u, u

    p, u, u_t = _mm(h2, w["mlp_w1"], "nn", name=f"mlp_up_{i}", b_layer=i, out_dtypes=(BF16, BF16, BF16),
                    out_t=(False, False, True), epilogue=act)
    x2, z = _mm(u, w["mlp_w2"], "nn", name=f"mlp_down_{i}", b_layer=i, out_dtypes=(F32, BF16), extras=(x1,),
                rowvecs=(gate,), seq=S, epilogue=lambda acc, xr, g: (xr + g * acc, acc))
    return x2, (p, u_t, z)


STACKED_GRADS = ("fox_out", "mla_down", "mla_uq", "mla_ukv", "mla_out", "mlp_w1", "mlp_w2")


def _local_step(x, target, pos_f, inv_freq_row, sign_row, mod, w, slots, *, S, hooks=None):
    hooks = hooks or {}
    T, D = x.shape
    L = mod.shape[0]
    L2 = len(w["fox_out"])
    cos_t, sin_s = _rope_tables(pos_f, inv_freq_row, sign_row)
    saved = []
    for i in range(L):
        j = i // 2
        sh_m, sc_m, g_m, sh_f, sc_f, g_f = (mod[i, s] for s in range(6))
        h, h_t = _norm_mod(x, w["norm_mix_g"][i], sc_m, sh_m, S=S, name=f"norm_mix_{i}")
        if i % 2 == 0:
            qkv = _mm(h, w["fox_qkv"], "nn", name=f"fox_qkv_{i}", b_layer=j, out_dtypes=(BF16,))
            fl = _mm(h, w["fox_f"], "nn", name=f"fox_f_{i}", b_layer=j)
            qx, kvx = _fox_prep(qkv, fl, w["fox_b"][j], S=S, D=D, name=f"fox_prep_{i}")
            o, lse, o_t = _attn_fwd(qx, kvx, S=S, scale=1.0, ew=FOX_EXTRA, name=f"fox_attn_{i}")
            mix = (qx, kvx, o, lse, o_t, fl)
            w_out = w["fox_out"]
        else:
            lat = _mm(h, w["mla_down"], "nn", name=f"mla_down_{i}", b_layer=j)
            Rq = lat.shape[1] - 2 * LANES
            cqr, cqr_t = _mla_mid(lat, w["mla_gq"][j], w["mla_gkv"][j], cos_t, sin_s, name=f"mla_mid_{i}")
            qx = _mm(cqr, w["mla_uq"], "nn", name=f"mla_uq_{i}", b_layer=j, out_dtypes=(BF16,), a_sz=Rq, tk=Rq,
                     tables=(cos_t, sin_s), epilogue=lambda acc, c, s: (_rope_pairs(acc, c, s, 1.0),))
            kvx = _mm(cqr, w["mla_ukv"], "nn", name=f"mla_ukv_{i}", b_layer=j, out_dtypes=(BF16,), a_off=Rq,
                      a_sz=2 * LANES, tk=2 * LANES, tn=PAIR_KV)
            o, lse, o_t = _attn_fwd(qx, kvx, S=S, scale=(HEAD_DIM + ROPE_DIM) ** -0.5, ew=ROPE_DIM,
                                    name=f"mla_attn_{i}")
            mix = (qx, kvx, o, lse, o_t, lat, cqr_t)
            w_out = w["mla_out"]
        x1, y = _mm(o, w_out, "nn", name=f"mix_out_{i}", b_layer=j, out_dtypes=(F32, BF16), extras=(x,),
                    rowvecs=(g_m,), seq=S, epilogue=lambda acc, xr, g: (xr + g * acc, acc))
        h2, h2_t = _norm_mod(x1, w["norm_mlp_g"][i], sc_f, sh_f, S=S, name=f"norm_mlp_{i}")
        if i == 0 and "fwd_mlp0" in hooks:
            w = hooks["fwd_mlp0"](x1, w)
        x2, mlp = _mlp_fwd(h2, w, i, x1, g_f, S=S)
        saved.append((x, h_t, mix, y, x1, h2_t, mlp))
        x = x2
        if i == 0 and "fwd_layer1" in hooks:
            w = hooks["fwd_layer1"](x, w)

    dx, dg_final, loss = _final_loss(x, target, w["final_norm_g"])
    n_split = w["mlp_w1"][0][0].shape[1]

    grads = {k: [None] * len(w[k]) for k in ("norm_mix_g", "norm_mlp_g", "fox_b", "mla_gq", "mla_gkv")}
    grads.update({k: [None] * L2 for k in ("fox_qkv", "fox_f")})
    grads.update({k: {} for k in STACKED_GRADS})
    grads["final_norm_g"] = dg_final

    def stacked(key, layer, _, a_t, b, **kw):
        group, idx, count = slots[(key, layer)]
        grads[key][group] = _mm(a_t, b, "nn", out_stack=(grads[key].get(group), idx, count), **kw)

    dmod = [None] * L
    for i in reversed(range(L)):
        j = i // 2
        x0, h_t, mix, y, x1, h2_t, (p, u_t, z) = saved[i]
        sh_m, sc_m, g_m, sh_f, sc_f, g_f = (mod[i, s] for s in range(6))
        if i == 0 and "bwd_layer0" in hooks:
            g_f = g_f + hooks["bwd_layer0"](grads)[0, 0]
        dz, dg_f = _gate_bwd(dx, z, g_f, S=S, name=f"gate_mlp_bwd_{i}")
        stacked("mlp_w2", i, L, u_t, dz, name=f"mlp_w2_grad_{i}")
        dp = _mm(dz, w["mlp_w2"], "nt", name=f"mlp_down_bwd_{i}", b_layer=i, out_dtypes=(BF16,), extras=(p,),
                 epilogue=lambda acc, pv: (acc * (2.0 * jnp.maximum(pv.astype(F32), 0.0)),))
        stacked("mlp_w1", i, L, h2_t, dp, name=f"mlp_w1_grad_{i}", out_split=n_split)
        if i == 0 and "bwd_mix0" in hooks:
            g_m = g_m + hooks["bwd_mix0"](grads)[0, 0]
        dh2 = _mm(dp, w["mlp_w1"], "nt", name=f"mlp_up_bwd_{i}", b_layer=i)
        dx1, dsh_f, dsc_f, dgn = _norm_mod_bwd(x1, dh2, dx, w["norm_mlp_g"][i], sc_f, S=S, name=f"norm_mlp_bwd_{i}")
        grads["norm_mlp_g"][i] = dgn
        dy, dg_m = _gate_bwd(dx1, y, g_m, S=S, name=f"gate_mix_bwd_{i}")
        if i % 2 == 0:
            qx, kvx, o, lse, o_t, fl = mix
            stacked("fox_out", j, L2, o_t, dy, name=f"fox_out_grad_{i}")
            do = _mm(dy, w["fox_out"], "nt", name=f"fox_out_bwd_{i}", b_layer=j, out_dtypes=(BF16,))
            dqx, dkvx, csum, rsum = _attn_bwd(qx, kvx, o, lse, do, S=S, scale=1.0, ew=FOX_EXTRA,
                                              name=f"fox_attn_bwd_{i}", bias_grad=True)
            n_heads = D // HEAD_DIM
            csum = jnp.pad(csum.reshape(T, n_heads, HEAD_DIM)[:, :, 0], ((0, 0), (0, LANES - n_heads)))
            rsum = jnp.transpose(rsum[:, :, :2, :], (0, 3, 1, 2)).reshape(T, n_heads)
            rsum = jnp.pad(rsum, ((0, 0), (0, LANES - n_heads)))
            dqkv, dfl, db = _fox_unprep(dqx, dkvx, csum, rsum, fl, w["fox_b"][j], S=S, D=D, name=f"fox_unprep_{i}")
            grads["fox_b"][j] = db
            grads["fox_qkv"][j] = _mm(h_t, dqkv, "nn", name=f"fox_qkv_grad_{i}")
            grads["fox_f"][j] = _mm(h_t, dfl, "nn", name=f"fox_f_grad_{i}")
            dh_f = _mm(dfl, w["fox_f"], "nt", name=f"fox_f_bwd_{i}", b_layer=j)
            dh = _mm(dqkv, w["fox_qkv"], "nt", name=f"fox_qkv_bwd_{i}", b_layer=j, extras=(dh_f,),
                     epilogue=lambda acc, e: (acc + e,))
        else:
            qx, kvx, o, lse, o_t, lat, cqr_t = mix
            Rq = lat.shape[1] - 2 * LANES
            stacked("mla_out", j, L2, o_t, dy, name=f"mla_out_grad_{i}")
            do = _mm(dy, w["mla_out"], "nt", name=f"mla_out_bwd_{i}", b_layer=j, out_dtypes=(BF16,))
            dqx, dkvx = _attn_bwd(qx, kvx, o, lse, do, S=S, scale=(HEAD_DIM + ROPE_DIM) ** -0.5, ew=ROPE_DIM,
                                  name=f"mla_attn_bwd_{i}")
            dqpre = _unrope(dqx, cos_t, sin_s)
            stacked("mla_uq", j, L2, cqr_t[:Rq], dqpre, name=f"mla_uq_grad_{i}", out_split=n_split)
            stacked("mla_ukv", j, L2, cqr_t[Rq:], dkvx, name=f"mla_ukv_grad_{i}", tn=PAIR_KV, out_split=n_split)
            dcq = _mm(dqpre, w["mla_uq"], "nt", name=f"mla_uq_bwd_{i}", b_layer=j)
            dckr = _mm(dkvx, w["mla_ukv"], "nt", name=f"mla_ukv_bwd_{i}", b_layer=j, tk=PAIR_KV * 2)
            dlat, dgq, dgkv = _mla_mid_bwd(lat, dcq, dckr, w["mla_gq"][j], w["mla_gkv"][j], cos_t, sin_s,
                                           name=f"mla_mid_bwd_{i}")
            grads["mla_gq"][j] = dgq
            grads["mla_gkv"][j] = dgkv
            stacked("mla_down", j, L2, h_t, dlat, name=f"mla_down_grad_{i}")
            dh = _mm(dlat, w["mla_down"], "nt", name=f"mla_down_bwd_{i}", b_layer=j)
        dx, dsh_m, dsc_m, dgn = _norm_mod_bwd(x0, dh, dx1, w["norm_mix_g"][i], sc_m, S=S, name=f"norm_mix_bwd_{i}")
        grads["norm_mix_g"][i] = dgn
        dmod[i] = jnp.stack([dsh_m, dsc_m, dg_m, dsh_f, dsc_f, dg_f])
    return loss, dx, jnp.stack(dmod), grads


GATHERED = ("fox_in", "fox_out", "mla_down", "mla_uq", "mla_ukv", "mla_out", "mlp_w1", "mlp_w2")
ROW_SHARDED = ("fox_out", "mla_down", "mla_out", "mlp_w2")


def _shard_layouts(wts):
    dkv = wts["mla_w_dkv"]
    dkv = jnp.pad(dkv, ((0, 0), (0, 0), (0, 2 * LANES - dkv.shape[2])))
    return {
        "fox_in": _pad_lanes(wts["fox_w_in"].astype(BF16)),
        "fox_out": wts["fox_w_out"].astype(BF16),
        "mla_down": jnp.concatenate([wts["mla_w_dq"], dkv], axis=2).astype(BF16),
        "mla_uq": jax.vmap(_uq_to_pairs)(wts["mla_w_uq"].astype(BF16)),
        "mla_ukv": jax.vmap(_ukv_to_pairs)(wts["mla_w_ukv"].astype(BF16)),
        "mla_out": wts["mla_w_out"].astype(BF16),
        "mlp_w1": wts["mlp_w1"].astype(BF16),
        "mlp_w2": wts["mlp_w2"].astype(BF16),
    }


def _small_layouts(small):
    return {
        "fox_b": [jnp.pad(b, (0, LANES - b.shape[0]))[None, :] for b in small["fox_b_f"]],
        "mla_gq": [g[None, :] for g in small["mla_q_norm_g"]],
        "mla_gkv": [g[None, :] for g in small["mla_kv_norm_g"]],
        "norm_mix_g": [g[None, :] for g in small["norm_mix_g"]],
        "norm_mlp_g": [g[None, :] for g in small["norm_mlp_g"]],
        "final_norm_g": small["final_norm_g"][None, :],
    }


def _comm_groups(L, L2):
    rest = [("fox_in", 1, L2 - 1), ("fox_out", 1, L2 - 1), ("mla_down", 0, L2), ("mla_uq", 0, L2),
            ("mla_ukv", 0, L2), ("mla_out", 0, L2), ("mlp_w1", 1, L - 1), ("mlp_w2", 1, L - 1)]
    return {"mix0": [("fox_in", 0, 1), ("fox_out", 0, 1)], "mlp0": [("mlp_w1", 0, 1), ("mlp_w2", 0, 1)],
            "rest": [e for e in rest if e[2] > 0]}


def _layer_slots(groups):
    return {(n, s + l): (g, l, cnt) for g, entries in groups.items() for n, s, cnt in entries for l in range(cnt)}


def _pad_lanes(a):
    cols = a.shape[-1]
    return jnp.pad(a, [(0, 0)] * (a.ndim - 1) + [(0, -cols % LANES)])


def _weight_views(name, gathered, D, n_fox_heads):
    n, ns, rows, cols = gathered.shape
    if name == "fox_in":
        true_cols = (3 * D + n_fox_heads) // ns
        fox = jnp.concatenate([gathered[:, k, :, :true_cols] for k in range(ns)], axis=-1)
        return {"fox_qkv": fox[:, :, :3 * D], "fox_f": _pad_lanes(fox[:, :, 3 * D:])}
    if name in ROW_SHARDED:
        return {name: gathered.reshape(n, ns * rows, cols)}
    return {name: gathered}


def _grad_pieces(name, g, qkv_f, n_fox_heads, ns):
    if name == "fox_in":
        fox = jnp.stack([jnp.concatenate([a, b[:, :n_fox_heads]], axis=1) for a, b in qkv_f])
        cols = fox.shape[2] // ns
        return jnp.stack([_pad_lanes(fox[:, :, k * cols:(k + 1) * cols]) for k in range(ns)], axis=1)
    if name in ROW_SHARDED:
        return g.reshape(g.shape[0], ns, g.shape[1] // ns, g.shape[2])
    return g


def _small_grads(g, n_fox_heads):
    return {
        "norm_mix_g": jnp.concatenate(g["norm_mix_g"], axis=0),
        "norm_mlp_g": jnp.concatenate(g["norm_mlp_g"], axis=0),
        "final_norm_g": g["final_norm_g"][0],
        "fox_b_f": jnp.concatenate(g["fox_b"], axis=0)[:, :n_fox_heads],
        "mla_q_norm_g": jnp.concatenate(g["mla_gq"], axis=0),
        "mla_kv_norm_g": jnp.concatenate(g["mla_gkv"], axis=0),
    }


def _silu(c):
    return c * (1.0 / (1.0 + jnp.exp(-c)))


def _ada_fwd(c_all, ada_w, ada_b_cols):
    L, D, C = ada_w.shape
    Bg = c_all.shape[0]
    tc = _tile(C, 512)

    def body(c_ref, w_ref, b_ref, o_ref):
        ca = _silu(c_ref[...]).astype(BF16)
        o_ref[...] = jnp.dot(ca, w_ref[...].astype(BF16), preferred_element_type=F32) + b_ref[...]

    return pl.pallas_call(
        body, name="ada_fwd", grid=(L, C // tc),
        in_specs=[pl.BlockSpec((Bg, D), lambda l, j: (0, 0)), pl.BlockSpec((None, D, tc), lambda l, j: (l, 0, j)),
                  pl.BlockSpec((None, 1, tc), lambda l, j: (l, 0, j))],
        out_specs=pl.BlockSpec((None, Bg, tc), lambda l, j: (l, 0, j)),
        out_shape=jax.ShapeDtypeStruct((L, Bg, C), F32),
        compiler_params=_cparams(("parallel", "parallel")),
    )(c_all, ada_w, ada_b_cols)


def _ada_bwd(c_all, dmod_cols):
    L, Bg, C = dmod_cols.shape
    D = c_all.shape[1]
    tc = _tile(C, 512)

    def body(c_ref, d_ref, o_ref):
        ca = _silu(c_ref[...]).astype(BF16)
        o_ref[...] = _dot_tn(ca, d_ref[...].astype(BF16))

    return pl.pallas_call(
        body, name="ada_bwd", grid=(L, C // tc),
        in_specs=[pl.BlockSpec((Bg, D), lambda l, j: (0, 0)), pl.BlockSpec((None, Bg, tc), lambda l, j: (l, 0, j))],
        out_specs=pl.BlockSpec((None, D, tc), lambda l, j: (l, 0, j)),
        out_shape=jax.ShapeDtypeStruct((L, D, C), F32),
        compiler_params=_cparams(("parallel", "parallel")),
    )(c_all, dmod_cols)


def _adamw_update(w, gv, m, v):
    mn = ADAM_B1 * m + (1.0 - ADAM_B1) * gv
    vn = ADAM_B2 * v + (1.0 - ADAM_B2) * jnp.square(gv)
    m_hat = mn / (1.0 - ADAM_B1 ** ADAM_STEP)
    v_hat = vn / (1.0 - ADAM_B2 ** ADAM_STEP)
    return -ADAM_LR * (m_hat / (jnp.sqrt(v_hat) + ADAM_EPS) + ADAM_WD * w), mn, vn


def _adamw(w, g, m, v, *, name):
    shape = w.shape
    C = shape[-1]
    R = int(np.prod(shape[:-1])) if len(shape) > 1 else 1
    w2, g2, m2, v2 = (a.reshape(R, C) for a in (w, g, m, v))
    tr = _row_tile(R, C)

    def body(w_ref, g_ref, m_ref, v_ref, d_ref, nm_ref, nv_ref):
        d_ref[...], nm_ref[...], nv_ref[...] = _adamw_update(w_ref[...], g_ref[...], m_ref[...], v_ref[...])

    spec = pl.BlockSpec((tr, C), lambda i: (i, 0))
    out = pl.pallas_call(
        body, name=name, grid=(R // tr,), in_specs=[spec] * 4, out_specs=[spec] * 3,
        out_shape=[jax.ShapeDtypeStruct((R, C), F32)] * 3, compiler_params=_cparams(("parallel",)),
    )(w2, g2, m2, v2)
    return tuple(a.reshape(shape) for a in out)


def _adamw_halves(w, g_own, g_peer, m, v, c_idx, *, name):
    L, rows, C = w.shape
    R = rows // 2
    tr = _row_tile(R, C)

    def body(c_ref, w_ref, go_ref, gp_ref, m_ref, v_ref, g_ref, d_ref, nm_ref, nv_ref):
        gv = jnp.where(pl.program_id(1) == c_ref[0], go_ref[...], gp_ref[...])
        g_ref[...] = gv
        d_ref[...], nm_ref[...], nv_ref[...] = _adamw_update(w_ref[...], gv, m_ref[...], v_ref[...])

    full = pl.BlockSpec((None, None, tr, C), lambda l, hh, i, c_ref: (l, hh, i, 0))
    half = pl.BlockSpec((None, tr, C), lambda l, hh, i, c_ref: (l, i, 0))
    grid_spec = pltpu.PrefetchScalarGridSpec(
        num_scalar_prefetch=1, grid=(L, 2, R // tr), in_specs=[full, half, half, full, full], out_specs=[full] * 4)
    split = lambda a: a.reshape(L, 2, R, C)
    out = pl.pallas_call(
        body, name=name, grid_spec=grid_spec, out_shape=[jax.ShapeDtypeStruct((L, 2, R, C), F32)] * 4,
        compiler_params=_cparams(("parallel", "parallel", "parallel")),
    )(c_idx, split(w), g_own, g_peer, split(m), split(v))
    return tuple(a.reshape(w.shape) for a in out)


def _sum_gathered(dm8, sm8):
    n_dev, Bl, R, D = dm8.shape
    Rs = sm8.shape[1]

    def body(dm_ref, sm_ref, ob_ref, os_ref):
        acc_b = jnp.zeros((R, D), F32)
        acc_s = jnp.zeros((Rs, D), F32)
        for d in range(n_dev):
            for b in range(Bl):
                acc_b = acc_b + dm_ref[d, b]
            acc_s = acc_s + sm_ref[d]
        ob_ref[...] = acc_b
        os_ref[...] = acc_s

    return pl.pallas_call(
        body, name="sum_gathered",
        out_shape=[jax.ShapeDtypeStruct((R, D), F32), jax.ShapeDtypeStruct((Rs, D), F32)],
        compiler_params=_cparams(None),
    )(dm8, sm8)


N_DEV = 8
N_CHIP = 4
ANY = pl.BlockSpec(memory_space=pl.ANY)
HBM = pl.BlockSpec(memory_space=pltpu.HBM)
SEM = pl.BlockSpec(memory_space=pltpu.SEMAPHORE)
DATAFLOW = pltpu.SideEffectType.DATAFLOW_SIDE_EFFECTING


def _mesh_pos():
    return lax.axis_index("x"), lax.axis_index("y"), lax.axis_index("c")


def _all_gather8(block, *, name, in_vmem):
    R, W = block.shape

    def body(x_ref, out_ref, send_sems, recv_sems, local_sem):
        x, y, c = _mesh_pos()
        me, sibling = (x, y, c), (x, y, 1 - c)
        chips = [(1 - x, y), (x, 1 - y), (1 - x, 1 - y)]

        def slot(px, py, pc):
            return out_ref.at[4 * px + 2 * py + pc]

        def copy(k, blk, to, src=None):
            return pltpu.make_async_remote_copy(
                src_ref=slot(*blk) if src is None else src, dst_ref=slot(*blk),
                send_sem=send_sems.at[k], recv_sem=recv_sems.at[k], device_id=to, device_id_type=MESH_ID)

        mine = pltpu.make_async_copy(x_ref, slot(*me), local_sem)
        mine.start()
        first = [copy(0, me, sibling, src=x_ref)]
        first += [copy(1 + j, me, (*chip, c), src=x_ref) for j, chip in enumerate(chips)]
        for cp in first:
            cp.start()
        passed = [copy(4 + j, (*chip, c), sibling) for j, chip in enumerate(chips)]
        for j, chip in enumerate(chips):
            copy(1 + j, (*chip, c), me).wait_recv()
            passed[j].start()
        copy(0, sibling, me).wait_recv()
        for j, chip in enumerate(chips):
            copy(4 + j, (*chip, 1 - c), me).wait_recv()
        for cp in first + passed:
            cp.wait_send()
        mine.wait()

    space = pl.BlockSpec(memory_space=pltpu.VMEM) if in_vmem else ANY
    return pl.pallas_call(
        body, name=name, out_shape=jax.ShapeDtypeStruct((N_DEV, R, W), block.dtype),
        in_specs=[space], out_specs=space,
        scratch_shapes=[pltpu.SemaphoreType.DMA((7,)), pltpu.SemaphoreType.DMA((7,)), pltpu.SemaphoreType.DMA],
        compiler_params=pltpu.CompilerParams(vmem_limit_bytes=VMEM_LIMIT_V7X),
    )(block)


def _comm_call(body, arrays, out_shapes, n_sems, *, name):
    return pl.pallas_call(
        body, name=name, out_shape=out_shapes, in_specs=[ANY] * len(arrays), out_specs=[ANY] * len(out_shapes),
        scratch_shapes=[pltpu.SemaphoreType.DMA((n_sems,)), pltpu.SemaphoreType.DMA((n_sems,)),
                        pltpu.SemaphoreType.DMA((len(arrays),))],
    )(*arrays)


def _gather_weights(shards, *, name):
    n = len(shards)

    def body(*refs):
        xs, outs = refs[:n], refs[n:2 * n]
        send_sems, recv_sems, local_sems = refs[2 * n:]
        x, y, c = _mesh_pos()
        me, sibling = (x, y, c), (x, y, 1 - c)
        chips = [(1 - x, y), (x, 1 - y), (1 - x, 1 - y)]
        waits = []
        for i in range(n):
            nl = shards[i].shape[0]
            own = xs[i].at[pl.ds(0, nl), c]

            def slot(px, py, pc, i=i, nl=nl):
                return outs[i].at[pl.ds(0, nl), 2 * px + py, pc]

            def copy(k, blk, to, src=None, i=i, slot=slot):
                return pltpu.make_async_remote_copy(
                    src_ref=slot(*blk) if src is None else src, dst_ref=slot(*blk),
                    send_sem=send_sems.at[7 * i + k], recv_sem=recv_sems.at[7 * i + k], device_id=to,
                    device_id_type=MESH_ID)

            mine = pltpu.make_async_copy(own, slot(*me), local_sems.at[i])
            mine.start()
            first = [copy(0, me, sibling, src=own)]
            first += [copy(1 + j, me, (*chip, c), src=own) for j, chip in enumerate(chips)]
            for cp in first:
                cp.start()
            waits.append((copy, mine, first))
        for copy, mine, first in waits:
            passed = [copy(4 + j, (*chip, c), sibling) for j, chip in enumerate(chips)]
            for j, chip in enumerate(chips):
                copy(1 + j, (*chip, c), me).wait_recv()
                passed[j].start()
            copy(0, sibling, me).wait_recv()
            for j, chip in enumerate(chips):
                copy(4 + j, (*chip, 1 - c), me).wait_recv()
            for cp in first + passed:
                cp.wait_send()
            mine.wait()

    out_shapes = [jax.ShapeDtypeStruct((s.shape[0], N_CHIP) + s.shape[1:], s.dtype) for s in shards]
    return _comm_call(body, shards, out_shapes, 7 * n, name=name)


def _place_own(shard, chip_idx, c_idx, *, name):
    n, _, rows, cols = shard.shape
    tr = _row_tile(rows, cols)

    def body(k_ref, c_ref, x_ref, o_ref):
        o_ref[...] = x_ref[...]

    grid_spec = pltpu.PrefetchScalarGridSpec(
        num_scalar_prefetch=2, grid=(n, rows // tr),
        in_specs=[pl.BlockSpec((None, None, tr, cols), lambda l, i, k_ref, c_ref: (l, c_ref[0], i, 0))],
        out_specs=pl.BlockSpec((None, None, None, tr, cols), lambda l, i, k_ref, c_ref: (l, k_ref[0], c_ref[0], i, 0)))
    return pl.pallas_call(
        body, name=name, grid_spec=grid_spec,
        out_shape=jax.ShapeDtypeStruct((n, N_CHIP, 2, rows, cols), shard.dtype),
        compiler_params=_cparams(("parallel", "parallel")),
    )(chip_idx, c_idx, shard)


def _gather_copies(x_refs, land_refs, send_sems, recv_sems):
    x, y, c = _mesh_pos()
    k_me = 2 * x + y
    targets = [(x, y, 1 - c), (1 - x, y, c), (x, 1 - y, c), (1 - x, 1 - y, c)]
    copies = []
    for i, (x_ref, land_ref) in enumerate(zip(x_refs, land_refs)):
        nl = x_ref.shape[0]
        for j, to in enumerate(targets):
            copies.append(pltpu.make_async_remote_copy(
                src_ref=x_ref.at[pl.ds(0, nl), c], dst_ref=land_ref.at[pl.ds(0, nl), k_me, c],
                send_sem=send_sems.at[4 * i + j], recv_sem=recv_sems.at[4 * i + j], device_id=to,
                device_id_type=MESH_ID))
    return copies


def _split_start(copies_fn, srcs, lands, after, *, name, sems_per_array):
    n = len(srcs)

    def body(*refs):
        send_sems, recv_sems = refs[2 * n + 1], refs[2 * n + 2]
        for cp in copies_fn(refs[:n], refs[n:2 * n], send_sems, recv_sems):
            cp.start()
        refs[-1][...] = jnp.zeros_like(refs[-1])

    operands = [pltpu.with_memory_space_constraint(a, pltpu.HBM) for a in list(srcs) + list(lands)]
    n_sems = sems_per_array * n
    out_shape = ([pltpu.SemaphoreType.DMA((n_sems,)), pltpu.SemaphoreType.DMA((n_sems,))]
                 + [pltpu.HBM(a.shape, a.dtype) for a in operands] + [jax.ShapeDtypeStruct((8, LANES), F32)])
    res = pl.pallas_call(
        body, name=name, out_shape=out_shape, in_specs=[HBM] * (2 * n) + [ANY],
        out_specs=[SEM, SEM] + [HBM] * (2 * n) + [pl.BlockSpec(memory_space=pltpu.VMEM)],
        input_output_aliases={i: 2 + i for i in range(2 * n)},
        compiler_params=pltpu.CompilerParams(has_side_effects=DATAFLOW),
    )(*operands, after)
    return res[0], res[1], list(res[2:2 + n]), list(res[2 + n:2 + 2 * n]), res[-1]


def _split_wait(copies_fn, send_sems, recv_sems, srcs, lands, after, *, name):
    n = len(srcs)

    def body(*refs):
        for cp in copies_fn(refs[:n], refs[n:2 * n], refs[2 * n], refs[2 * n + 1]):
            cp.wait_send()
            cp.wait_recv()

    res = pl.pallas_call(
        body, name=name, out_shape=[pltpu.HBM(a.shape, a.dtype) for a in list(srcs) + list(lands)],
        in_specs=[HBM] * (2 * n) + [SEM, SEM, ANY], out_specs=[HBM] * (2 * n),
        input_output_aliases={i: i for i in range(2 * n)},
        compiler_params=pltpu.CompilerParams(has_side_effects=DATAFLOW),
    )(*srcs, *lands, send_sems, recv_sems, after)
    return list(res[:n]), list(res[n:])


def _gather_forward(lands, *, name):
    n = len(lands)

    def body(*refs):
        xs = refs[:n]
        send_sems, recv_sems, _ = refs[2 * n:]
        x, y, c = _mesh_pos()
        chips = [(1 - x, y), (x, 1 - y), (1 - x, 1 - y)]
        copies = []
        for i in range(n):
            nl = lands[i].shape[0]
            for j, (cx, cy) in enumerate(chips):
                here = xs[i].at[pl.ds(0, nl), 2 * cx + cy, c]
                cp = pltpu.make_async_remote_copy(
                    src_ref=here, dst_ref=here, send_sem=send_sems.at[3 * i + j], recv_sem=recv_sems.at[3 * i + j],
                    device_id=(x, y, 1 - c), device_id_type=MESH_ID)
                cp.start()
                copies.append(cp)
        for cp in copies:
            cp.wait()

    return pl.pallas_call(
        body, name=name, out_shape=[jax.ShapeDtypeStruct(a.shape, a.dtype) for a in lands],
        in_specs=[ANY] * n, out_specs=[ANY] * n, input_output_aliases={i: i for i in range(n)},
        scratch_shapes=[pltpu.SemaphoreType.DMA((3 * n,)), pltpu.SemaphoreType.DMA((3 * n,)),
                        pltpu.SemaphoreType.DMA((1,))],
    )(*lands)


def _pair_copies(g_refs, land_refs, send_sems, recv_sems):
    x, y, c = _mesh_pos()
    copies = []
    for i, (g_ref, land_ref) in enumerate(zip(g_refs, land_refs)):
        nl, ns = g_ref.shape[:2]
        copies.append(pltpu.make_async_remote_copy(
            src_ref=g_ref.at[pl.ds(0, nl), pl.ds(0, ns), 1 - c], dst_ref=land_ref, send_sem=send_sems.at[i],
            recv_sem=recv_sems.at[i], device_id=(x, y, 1 - c), device_id_type=MESH_ID))
    return copies


def _pair_exchange(gs, *, name):
    n = len(gs)

    def body(*refs):
        send_sems, recv_sems, _ = refs[2 * n:]
        copies = _pair_copies(refs[:n], refs[n:2 * n], send_sems, recv_sems)
        for cp in copies:
            cp.start()
        for cp in copies:
            cp.wait()

    out_shapes = [jax.ShapeDtypeStruct(g.shape[:2] + g.shape[3:], g.dtype) for g in gs]
    return _comm_call(body, gs, out_shapes, n, name=name)


def _chip_copies(p_refs, land_refs, send_sems, recv_sems):
    x, y, c = _mesh_pos()
    k_me = 2 * x + y
    chips = [(1 - x, y), (x, 1 - y), (1 - x, 1 - y)]
    copies = []
    for i, (p_ref, land_ref) in enumerate(zip(p_refs, land_refs)):
        nl = p_ref.shape[0]
        for j, (cx, cy) in enumerate(chips):
            copies.append(pltpu.make_async_remote_copy(
                src_ref=p_ref.at[pl.ds(0, nl), 2 * cx + cy], dst_ref=land_ref.at[k_me],
                send_sem=send_sems.at[3 * i + j], recv_sem=recv_sems.at[3 * i + j],
                device_id=(cx, cy, c), device_id_type=MESH_ID))
    return copies


def _chip_landing(ps):
    return [lax.empty((p.shape[1], p.shape[0]) + p.shape[2:], p.dtype) for p in ps]


def _pair_swap(ss, *, name):
    n = len(ss)

    def body(*refs):
        xs, outs = refs[:n], refs[n:2 * n]
        send_sems, recv_sems, _ = refs[2 * n:]
        x, y, c = _mesh_pos()
        copies = []
        for i in range(n):
            cp = pltpu.make_async_remote_copy(src_ref=xs[i], dst_ref=outs[i], send_sem=send_sems.at[i],
                                              recv_sem=recv_sems.at[i], device_id=(x, y, 1 - c),
                                              device_id_type=MESH_ID)
            cp.start()
            copies.append(cp)
        for cp in copies:
            cp.wait()

    out_shapes = [jax.ShapeDtypeStruct(s.shape, s.dtype) for s in ss]
    return _comm_call(body, ss, out_shapes, n, name=name)


def _row_tile(rows, cols):
    tr = rows
    while tr * cols > 256 * 1024 and tr % 16 == 0:
        tr //= 2
    return tr


def _pair_add(g, recv, c_idx, *, name):
    n, ns, _, rows, W = g.shape
    tr = _row_tile(rows, W)

    def body(c_ref, g_ref, r_ref, o_ref):
        o_ref[...] = (g_ref[...] + r_ref[...]).astype(BF16)

    piece = pl.BlockSpec((None, tr, W), lambda p, i, c_ref: (p, i, 0))
    grid_spec = pltpu.PrefetchScalarGridSpec(
        num_scalar_prefetch=1, grid=(n * ns, rows // tr),
        in_specs=[pl.BlockSpec((None, None, tr, W), lambda p, i, c_ref: (p, c_ref[0], i, 0)), piece],
        out_specs=piece)
    out = pl.pallas_call(
        body, name=name, grid_spec=grid_spec, out_shape=jax.ShapeDtypeStruct((n * ns, rows, W), BF16),
        compiler_params=_cparams(("parallel", "parallel")),
    )(c_idx, g.reshape(n * ns, 2, rows, W), recv.reshape(n * ns, rows, W))
    return out.reshape(n, ns, rows, W)


def _sum_pieces(land, own, chip_idx, *, name):
    n, nl, A, W = land.shape
    tr = _row_tile(A, W)

    def body(k_ref, l_ref, o_ref, out_ref):
        acc = jnp.zeros(out_ref.shape, F32)
        for k in range(n):
            acc = acc + jnp.where(k == k_ref[0], o_ref[...], l_ref[k]).astype(F32)
        out_ref[...] = acc

    grid_spec = pltpu.PrefetchScalarGridSpec(
        num_scalar_prefetch=1, grid=(nl, A // tr),
        in_specs=[pl.BlockSpec((n, None, tr, W), lambda l, i, k_ref: (0, l, i, 0)),
                  pl.BlockSpec((None, None, tr, W), lambda l, i, k_ref: (l, k_ref[0], i, 0))],
        out_specs=pl.BlockSpec((None, tr, W), lambda l, i, k_ref: (l, i, 0)))
    return pl.pallas_call(
        body, name=name, grid_spec=grid_spec, out_shape=jax.ShapeDtypeStruct((nl, A, W), F32),
        compiler_params=_cparams(("parallel", "parallel")),
    )(chip_idx, land, own)


SMALL = ("norm_mix_g", "norm_mlp_g", "final_norm_g", "fox_b_f", "mla_q_norm_g", "mla_kv_norm_g")
WEIGHT_ORDER = ("ada_w", "ada_b", "norm_mix_g", "norm_mlp_g", "fox_w_in", "fox_b_f", "fox_w_out", "mla_w_dq",
                "mla_q_norm_g", "mla_w_uq", "mla_w_dkv", "mla_kv_norm_g", "mla_w_ukv", "mla_w_out", "mlp_w1",
                "mlp_w2", "final_norm_g")


def _small_rows(vals, D):
    rows = [vals["norm_mix_g"], vals["norm_mlp_g"], vals["final_norm_g"][None, :]]
    for n in ("fox_b_f", "mla_q_norm_g", "mla_kv_norm_g"):
        flat = vals[n].reshape(-1)
        assert flat.shape[0] <= D
        rows.append(jnp.pad(flat, (0, D - flat.shape[0]))[None, :])
    return jnp.concatenate(rows, axis=0)


def _small_unrows(rows, shapes):
    L = shapes["norm_mix_g"][0]
    out = {"norm_mix_g": rows[0:L], "norm_mlp_g": rows[L:2 * L], "final_norm_g": rows[2 * L]}
    for k, n in enumerate(("fox_b_f", "mla_q_norm_g", "mla_kv_norm_g")):
        size = int(np.prod(shapes[n]))
        out[n] = rows[2 * L + 1 + k, :size].reshape(shapes[n])
    return out


def kernel(x, c, positions, ada_w, ada_b, norm_mix_g, norm_mlp_g, fox_w_in, fox_b_f, fox_w_out, mla_w_dq, mla_q_norm_g, mla_w_uq, mla_w_dkv, mla_kv_norm_g, mla_w_ukv, mla_w_out, mlp_w1, mlp_w2, final_norm_g, loss_target, m_ada_w, m_ada_b, m_norm_mix_g, m_norm_mlp_g, m_fox_w_in, m_fox_b_f, m_fox_w_out, m_mla_w_dq, m_mla_q_norm_g, m_mla_w_uq, m_mla_w_dkv, m_mla_kv_norm_g, m_mla_w_ukv, m_mla_w_out, m_mlp_w1, m_mlp_w2, m_final_norm_g, v_ada_w, v_ada_b, v_norm_mix_g, v_norm_mlp_g, v_fox_w_in, v_fox_b_f, v_fox_w_out, v_mla_w_dq, v_mla_q_norm_g, v_mla_w_uq, v_mla_w_dkv, v_mla_kv_norm_g, v_mla_w_ukv, v_mla_w_out, v_mlp_w1, v_mlp_w2, v_final_norm_g):
    args = dict(locals())
    wts = {n: args[n] for n in WEIGHT_ORDER}
    mom = {n: args["m_" + n] for n in WEIGHT_ORDER}
    var = {n: args["v_" + n] for n in WEIGHT_ORDER}
    Bl, S, D = x.shape
    T = Bl * S
    L = ada_w.shape[0]
    C = ada_w.shape[2]
    mx, my, mc = _mesh_pos()
    chip = 2 * mx + my
    dev = 4 * mx + 2 * my + mc
    c_idx = jnp.reshape(mc, (1,)).astype(jnp.int32)
    chip_idx = jnp.reshape(chip, (1,)).astype(jnp.int32)
    small = {n: wts[n] for n in SMALL}
    L2, q_cols = mla_q_norm_g.shape
    n_fox_heads = fox_b_f.shape[1]

    shards = _shard_layouts(wts)
    groups = _comm_groups(L, L2)
    slots = _layer_slots(groups)

    def row_halves(a):
        return a.reshape(a.shape[:-2] + (2, a.shape[-2] // 2, a.shape[-1]))

    def whole_rows(a):
        return a.reshape(a.shape[:2] + (a.shape[2] * a.shape[3], a.shape[4]))

    part = {g: [row_halves(shards[n][s:s + cnt]) for n, s, cnt in entries] for g, entries in groups.items()}
    mix0 = _gather_weights(part["mix0"], name="gather_mix0")
    gather_sems, after = {}, mix0[0]
    for group in ("mlp0", "rest"):
        placed = [_place_own(a, chip_idx, c_idx, name=f"gather_place_{group}_{n}")
                  for a, (n, _, _) in zip(part[group], groups[group])]
        gather_sems[group] = _split_start(_gather_copies, part[group], placed, after, name=f"gather_{group}_start",
                                          sems_per_array=4)
        after = gather_sems[group][4]

    def layer_weights(w, group, arrays):
        for (n, s, cnt), a in zip(groups[group], arrays):
            for key, view in _weight_views(n, whole_rows(a), D, n_fox_heads).items():
                for l in range(cnt):
                    w[key][s + l] = (view, l)

    w = {key: [None] * L2 for key in ("fox_qkv", "fox_f", "fox_out", "mla_down", "mla_uq", "mla_ukv", "mla_out")}
    w.update({key: [None] * L for key in ("mlp_w1", "mlp_w2")})
    layer_weights(w, "mix0", mix0)

    def gathered_now(group):
        def hook(x_now, w):
            _, landed = _split_wait(_gather_copies, *gather_sems[group][:4], x_now, name=f"gather_{group}_wait")
            layer_weights(w, group, _gather_forward(landed, name=f"gather_{group}_forward"))
            return w
        return hook

    c_pad = jnp.concatenate([c, jnp.pad(mla_q_norm_g, ((0, 8 - Bl - L2), (0, D - q_cols)))], axis=0)
    c8 = _all_gather8(c_pad, name="gather_c", in_vmem=True)
    c_all = c8[:, :Bl].reshape(N_DEV * Bl, D)
    qg4 = c8.reshape(N_CHIP, 2, 8, D)[:, 0, Bl:Bl + L2, :q_cols]
    small["mla_q_norm_g"] = jnp.transpose(qg4, (1, 0, 2)).reshape(L2, N_CHIP * q_cols)
    ada_b_cols = lax.dynamic_slice_in_dim(ada_b, chip * C, C, axis=1)[:, None, :]
    mod_cols = _ada_fwd(c_all, ada_w, ada_b_cols)
    mod8 = _all_gather8(mod_cols.reshape(L * N_DEV * Bl, C), name="gather_mod", in_vmem=True)
    mod4 = mod8.reshape(N_CHIP, 2, L, N_DEV * Bl, C)[:, 0]
    mod_me = lax.dynamic_slice_in_dim(mod4, dev * Bl, Bl, axis=2)
    mod = jnp.transpose(mod_me, (1, 2, 0, 3)).reshape(L, Bl, 6, D)
    mod = jnp.transpose(mod, (0, 2, 1, 3))[:, :, :, None, :]

    w.update(_small_layouts(small))
    mod = mod + after[0, 0]
    pending = {}

    def grad_pieces(group, g_now):
        out = []
        for n, s, cnt in groups[group]:
            qkv_f = [(g_now["fox_qkv"][j], g_now["fox_f"][j]) for j in range(s, s + cnt)] if n == "fox_in" else None
            stacked_g = None if n == "fox_in" else g_now[n][group]
            out.append(row_halves(_grad_pieces(n, stacked_g, qkv_f, n_fox_heads, N_CHIP)))
        return out

    def pair_added(group, big, sibling):
        return [_pair_add(a, r, c_idx, name=f"grad_pair_add_{group}_{n}")
                for (n, _, _), a, r in zip(groups[group], big, sibling)]

    def exchange_start(group, ps, after=None):
        pending[group] = _split_start(_chip_copies, ps, _chip_landing(ps), chip_idx if after is None else after,
                                      name=f"grad_exchange_{group}_start", sems_per_array=3)
        return pending[group][4]

    def bwd_layer0(g_now):
        big = grad_pieces("rest", g_now)
        landing = [lax.empty(a.shape[:2] + a.shape[3:], a.dtype) for a in big]
        pending["rest_pair"] = _split_start(_pair_copies, big, landing, chip_idx, name="grad_pair_rest_start",
                                            sems_per_array=1)
        return pending["rest_pair"][4]

    def bwd_mix0(g_now):
        send_sems, recv_sems, big, landed, _ = pending["rest_pair"]
        big, landed = _split_wait(_pair_copies, send_sems, recv_sems, big, landed, g_now["mlp_w1"]["mlp0"],
                                  name="grad_pair_rest_wait")
        started = exchange_start("rest", pair_added("rest", big, landed))
        big = grad_pieces("mlp0", g_now)
        return exchange_start("mlp0", pair_added("mlp0", big, _pair_exchange(big, name="grad_pair_exchange_mlp0")),
                              after=started)

    half = ROPE_DIM // 2
    inv_freq = ROPE_THETA ** (-jnp.arange(0, ROPE_DIM, 2, dtype=F32) / ROPE_DIM)
    lane = np.arange(LANES)
    inv_freq_row = jnp.tile(inv_freq, LANES // half)[None, :]
    sign_row = jnp.asarray(np.where(lane < 2 * ROPE_DIM, np.where(lane % ROPE_DIM < half, -1.0, 1.0), 0.0), F32)[None, :]
    pos_f = positions.astype(F32).reshape(T, 1)
    loss_row, grad_x, dmod, g = _local_step(x.reshape(T, D), loss_target.reshape(T, D), pos_f, inv_freq_row, sign_row,
                                            mod, w, slots, S=S,
                                            hooks={"fwd_mlp0": gathered_now("mlp0"), "fwd_layer1": gathered_now("rest"),
                                                   "bwd_layer0": bwd_layer0, "bwd_mix0": bwd_mix0})
    g_small = _small_grads(g, n_fox_heads)
    big = grad_pieces("mix0", g)
    exchange_start("mix0", pair_added("mix0", big, _pair_exchange(big, name="grad_pair_exchange_mix0")))

    Rs = -(-(2 * L + 5) // 8) * 8
    srows = jnp.concatenate([_small_rows(g_small, D), jnp.pad(loss_row, ((0, 0), (0, D - LANES)))], axis=0)
    srows = jnp.pad(srows, ((0, Rs - srows.shape[0]), (0, 0)))
    drows = jnp.transpose(dmod[:, :, :, 0, :], (2, 0, 1, 3)).reshape(Bl * L * 6, D)
    both8 = _all_gather8(jnp.concatenate([drows, srows], axis=0), name="gather_small", in_vmem=True)
    dm8 = both8[:, :Bl * L * 6].reshape(N_DEV, Bl, L * 6, D)
    sm8 = both8[:, Bl * L * 6:]
    adb_rows, small_sum = _sum_gathered(dm8, sm8)
    grad_ada_b = adb_rows.reshape(L, 6 * D)
    loss = small_sum[2 * L + 4, 0]
    small_shapes = {n: (wts[n].shape if n != "mla_q_norm_g" else (wts[n].shape[0], N_CHIP * q_cols)) for n in SMALL}
    gs = _small_unrows(small_sum, small_shapes)
    gs["mla_q_norm_g"] = lax.dynamic_slice_in_dim(gs["mla_q_norm_g"], chip * q_cols, q_cols, axis=1)

    dmod16 = jnp.transpose(dm8.reshape(N_DEV, Bl, L, 6 * D), (2, 0, 1, 3)).reshape(L, N_DEV * Bl, 6 * D)
    dmod_cols = lax.dynamic_slice_in_dim(dmod16, chip * C, C, axis=2)
    grad_ada_w = _ada_bwd(c_all, dmod_cols)

    grads = dict(gs)
    grads["ada_w"] = grad_ada_w
    grads["ada_b"] = grad_ada_b
    delta, new_m, new_v = {}, {}, {}
    for n in ("ada_w", "ada_b"):
        delta[n], new_m[n], new_v[n] = _adamw(wts[n], grads[n], mom[n], var[n], name=f"adamw_{n}")
    shard_small_shapes = {n: wts[n].shape for n in SMALL}
    packs = [jnp.pad(_small_rows({n: src[n] for n in SMALL}, D), ((0, Rs - 2 * L - 4), (0, 0)))
             for src in (wts, grads, mom, var)]
    for dst, rows in zip((delta, new_m, new_v), _adamw(*packs, name="adamw_small")):
        dst.update(_small_unrows(rows, shard_small_shapes))

    halves = {}
    for group, after in (("rest", grad_x), ("mlp0", grad_x), ("mix0", delta["ada_w"])):
        send_sems, recv_sems, ps, lands, _ = pending[group]
        ps, lands = _split_wait(_chip_copies, send_sems, recv_sems, ps, lands, after, name=f"grad_exchange_{group}_wait")
        sums = [_sum_pieces(ld, p, chip_idx, name=f"grad_sum_{group}_{n}")
                for (n, _, _), ld, p in zip(groups[group], lands, ps)]
        swapped = _pair_swap(sums, name=f"grad_pair_swap_{group}")
        for (n, _, _), a, b in zip(groups[group], sums, swapped):
            halves[(n, group)] = (a, b)

    def all_layers(n, which):
        return jnp.concatenate([halves[(n, grp)][which] for grp in groups if (n, grp) in halves], axis=0)

    own = {n: all_layers(n, 0) for n in GATHERED}
    peer = {n: all_layers(n, 1) for n in GATHERED}
    for nat, n in (("fox_w_in", "fox_in"), ("fox_w_out", "fox_out"), ("mla_w_out", "mla_out"), ("mlp_w1", "mlp_w1"),
                   ("mlp_w2", "mlp_w2")):
        cols = wts[nat].shape[-1]
        res = _adamw_halves(_pad_lanes(wts[nat]), own[n], peer[n], _pad_lanes(mom[nat]), _pad_lanes(var[nat]), c_idx,
                            name=f"adamw_{nat}")
        grads[nat], delta[nat], new_m[nat], new_v[nat] = (a[..., :cols] for a in res)
    joined = {n: jnp.concatenate([jnp.where(mc == 0, own[n], peer[n]), jnp.where(mc == 0, peer[n], own[n])], axis=1)
              for n in ("mla_down", "mla_uq", "mla_ukv")}
    rq = mla_w_dq.shape[-1]
    grads["mla_w_dq"] = joined["mla_down"][:, :, :rq]
    grads["mla_w_dkv"] = joined["mla_down"][:, :, rq:rq + KV_RANK + ROPE_DIM]
    grads["mla_w_uq"] = jax.vmap(_uq_from_pairs)(joined["mla_uq"])
    grads["mla_w_ukv"] = jax.vmap(_ukv_from_pairs)(joined["mla_ukv"])
    for n in ("mla_w_dq", "mla_w_dkv", "mla_w_uq", "mla_w_ukv"):
        delta[n], new_m[n], new_v[n] = _adamw(wts[n], grads[n], mom[n], var[n], name=f"adamw_{n}")

    return (loss, grad_x.reshape(Bl, S, D), *[grads[n] for n in WEIGHT_ORDER], *[delta[n] for n in WEIGHT_ORDER],
            *[new_m[n] for n in WEIGHT_ORDER], *[new_v[n] for n in WEIGHT_ORDER])
```

```python
import numpy as np
import jax
import jax.numpy as jnp
from jax import lax
from jax.experimental import pallas as pl
from jax.experimental.pallas import tpu as pltpu

F32 = jnp.float32
BF16 = jnp.bfloat16
MESH_ID = pl.DeviceIdType.MESH

NORM_EPS = 1e-6
ROPE_THETA = 10000.0
HEAD_DIM = 64
ROPE_DIM = 32
KV_RANK = 128
MLA_SCALE = (HEAD_DIM + ROPE_DIM) ** -0.5
FOX_EXTRA = 6
PAIR_Q = 256
PAIR_KV = 384
LANES = 128
ADAM_LR = 0.001
ADAM_B1 = 0.9
ADAM_B2 = 0.999
ADAM_EPS = 1e-08
ADAM_WD = 0.01
ADAM_STEP = 10
VMEM_LIMIT_V7X = 48 * 1024 * 1024
MM_VMEM_BUDGET = 36 * 1024 * 1024
NEG_BIG = -1e30
ATTN_UNROLL = 4
ATTN_BLOCK = 256
ATTN_Q_ROWS = 512
ATTN_K_ROWS = 512

BIG_WEIGHTS = (("fox_w_in", 2), ("fox_w_out", 1), ("mla_w_dq", 1), ("mla_w_uq", 2), ("mla_w_dkv", 1),
               ("mla_w_ukv", 2), ("mla_w_out", 1), ("mlp_w1", 2), ("mlp_w2", 1))


def _cparams(sem=None):
    return pltpu.CompilerParams(dimension_semantics=sem, vmem_limit_bytes=VMEM_LIMIT_V7X)


def _tile(n, want):
    if n <= want:
        return n
    for t in range(want - want % LANES, 0, -LANES):
        if n % t == 0:
            return t
    raise ValueError((n, want))


def _mm(a, b, mode, *, name, out_dtypes=(F32,), epilogue=None, extras=(), rowvecs=(), tables=(),
        seq=None, a_off=0, a_sz=None, b_layer=None, out_stack=None, out_split=0, out_t=(), full_rows=False,
        tm=1024, tn=1024, tk=2048):
    if isinstance(b, (list, tuple)):
        b, b_layer = b[b_layer]
    b_rows, b_cols = b.shape[-2], b.shape[-1]
    n_split = b.shape[1] if b.ndim == 4 else 1
    assert mode in ("nn", "nt")
    if mode == "nn":
        M, K, N = a.shape[0], b_rows, b_cols * n_split
    else:
        M, K, N = a.shape[0], b_cols * n_split, b_rows
    assert a_sz is None or a_sz == K
    tm = _tile(seq if rowvecs else M, tm)
    n_piece = N // max(out_split, n_split if mode == "nn" else 1, 1)
    tn = _tile(n_piece, tn)
    tk = _tile(K // (n_split if mode == "nt" else 1), tk)
    ne, nr, nt_ = len(extras), len(rowvecs), len(tables)
    no = len(out_dtypes)

    def vmem_estimate():
        blocks = tm * tk * a.dtype.itemsize + tk * tn * b.dtype.itemsize
        blocks += tm * tn * (sum(e.dtype.itemsize for e in extras) + sum(jnp.dtype(d).itemsize for d in out_dtypes))
        return 2 * blocks + 2 * tm * tn * 4

    if full_rows:
        assert tn == N
    while vmem_estimate() > MM_VMEM_BUDGET and max(tm, tn) > 256:
        if tn >= tm and not full_rows:
            tn //= 2
        else:
            tm //= 2
    nk = K // tk

    assert a_off % tk == 0
    a_spec = pl.BlockSpec((tm, tk), lambda i, j, k: (i, k + a_off // tk))
    dims = (((1,), (0,)), ((), ())) if mode == "nn" else (((1,), (1,)), ((), ()))
    lead = () if b.ndim == 2 else (b_layer,)
    sq = (None,) * (b.ndim - 2)
    if mode == "nt":
        kb = b_cols // tk
        if b.ndim == 4:
            b_spec = pl.BlockSpec(sq + (tn, tk), lambda i, j, k: lead + (k // kb, j, k % kb))
        else:
            b_spec = pl.BlockSpec(sq + (tn, tk), lambda i, j, k: lead + (j, k))
    else:
        nb = b_cols // tn
        if b.ndim == 4:
            b_spec = pl.BlockSpec(sq + (tk, tn), lambda i, j, k: lead + (j // nb, k, j % nb))
        else:
            b_spec = pl.BlockSpec(sq + (tk, tn), lambda i, j, k: lead + (k, j))
    in_specs = [a_spec, b_spec]
    in_specs += [pl.BlockSpec((tm, tn), lambda i, j, k: (i, j)) for _ in extras]
    if rowvecs:
        assert seq % tm == 0
        per = seq // tm
        in_specs += [pl.BlockSpec((None, 1, tn), lambda i, j, k: (i // per, 0, j)) for _ in rowvecs]
    in_specs += [pl.BlockSpec((tm, LANES), lambda i, j, k: (i, 0)) for _ in tables]
    operands = [a, b, *extras, *rowvecs, *tables]
    aliases = {}
    transposed = tuple(out_t) + (False,) * (no - len(out_t))
    if out_stack is None:
        out_specs = [pl.BlockSpec((tn, tm), lambda i, j, k: (j, i)) if t else pl.BlockSpec((tm, tn), lambda i, j, k: (i, j))
                     for t in transposed]
        out_shape = [jax.ShapeDtypeStruct((N, M) if t else (M, N), d) for d, t in zip(out_dtypes, transposed)]
    else:
        prev, layer, n_layers = out_stack
        assert no == 1
        if out_split:
            ob = n_piece // tn
            out_specs = [pl.BlockSpec((None, None, tm, tn), lambda i, j, k: (layer, j // ob, i, j % ob))]
            out_shape = [jax.ShapeDtypeStruct((n_layers, out_split, M, n_piece), out_dtypes[0])]
        else:
            out_specs = [pl.BlockSpec((None, tm, tn), lambda i, j, k: (layer, i, j))]
            out_shape = [jax.ShapeDtypeStruct((n_layers, M, N), out_dtypes[0])]
        if prev is not None:
            in_specs.append(pl.BlockSpec(memory_space=pl.ANY))
            aliases = {len(operands): 0}
            operands.append(prev)
    n_in = len(operands)

    def body(*refs):
        a_ref, b_ref = refs[0], refs[1]
        side = refs[2:2 + ne + nr + nt_]
        outs = refs[n_in:n_in + no]

        def finish(acc):
            res = (acc,) if epilogue is None else epilogue(acc, *[r[...] for r in side])
            for o_ref, r, t in zip(outs, res, transposed):
                o_ref[...] = (r.T if t else r).astype(o_ref.dtype)

        part = lax.dot_general(a_ref[...].astype(BF16), b_ref[...].astype(BF16), dims,
                               preferred_element_type=F32)
        if nk == 1:
            finish(part)
        else:
            acc_ref = refs[-1]
            k = pl.program_id(2)

            @pl.when(k == 0)
            def _():
                acc_ref[...] = part

            @pl.when(k > 0)
            def _():
                acc_ref[...] += part

            @pl.when(k == nk - 1)
            def _():
                finish(acc_ref[...])

    res = pl.pallas_call(
        body, name=name, grid=(M // tm, N // tn, nk), in_specs=in_specs, out_specs=out_specs,
        out_shape=out_shape, scratch_shapes=[pltpu.VMEM((tm, tn), F32)] if nk > 1 else [],
        input_output_aliases=aliases,
        compiler_params=_cparams(("parallel", "parallel", "arbitrary")),
    )(*operands)
    return res[0] if no == 1 else tuple(res)


def _rope128(x, cos_t, sin_s):
    lane = lax.broadcasted_iota(jnp.int32, x.shape, 1)
    first = (lane % ROPE_DIM) < (ROPE_DIM // 2)
    swapped = jnp.where(first, pltpu.roll(x, LANES - ROPE_DIM // 2, 1), pltpu.roll(x, ROPE_DIM // 2, 1))
    return x * cos_t + swapped * sin_s


def _rope_pairs(acc, cos_t, sin_s, sign):
    parts = []
    for p in range(acc.shape[1] // PAIR_Q):
        parts.append(acc[:, p * PAIR_Q:p * PAIR_Q + LANES])
        parts.append(_rope128(acc[:, p * PAIR_Q + LANES:(p + 1) * PAIR_Q], cos_t, sign * sin_s))
    return jnp.concatenate(parts, axis=1)


def _rope_tables(pos_f, inv_freq_row, sign_row):
    T = pos_f.shape[0]
    tt = _tile(T, 512)

    def body(p_ref, f_ref, s_ref, cos_ref, sin_ref):
        ang = p_ref[...] * f_ref[...]
        cos_ref[...] = jnp.cos(ang)
        sin_ref[...] = jnp.sin(ang) * s_ref[...]

    return pl.pallas_call(
        body, name="rope_tables", grid=(T // tt,),
        in_specs=[pl.BlockSpec((tt, 1), lambda i: (i, 0)), pl.BlockSpec((1, LANES), lambda i: (0, 0)),
                  pl.BlockSpec((1, LANES), lambda i: (0, 0))],
        out_specs=[pl.BlockSpec((tt, LANES), lambda i: (i, 0))] * 2,
        out_shape=[jax.ShapeDtypeStruct((T, LANES), F32)] * 2,
        compiler_params=_cparams(("parallel",)),
    )(pos_f, inv_freq_row, sign_row)


def _unrope(dqx, cos_t, sin_s):
    T, W = dqx.shape
    tt = _tile(T, 512)

    def body(d_ref, c_ref, s_ref, o_ref):
        o_ref[...] = _rope_pairs(d_ref[...].astype(F32) * MLA_SCALE, c_ref[...], s_ref[...], -1.0).astype(BF16)

    return pl.pallas_call(
        body, name="mla_unrope", grid=(T // tt,),
        in_specs=[pl.BlockSpec((tt, W), lambda i: (i, 0)), pl.BlockSpec((tt, LANES), lambda i: (i, 0)),
                  pl.BlockSpec((tt, LANES), lambda i: (i, 0))],
        out_specs=pl.BlockSpec((tt, W), lambda i: (i, 0)),
        out_shape=jax.ShapeDtypeStruct((T, W), BF16),
        compiler_params=_cparams(("parallel",)),
    )(dqx, cos_t, sin_s)


def _row_specs(tt, D, per, n):
    return [pl.BlockSpec((None, 1, D), lambda i: (i // per, 0, 0)) for _ in range(n)]


def _norm_mod(x, gain, sc, sh, *, S, name):
    T, D = x.shape
    tt = _tile(S, 512)
    per = S // tt

    def body(x_ref, g_ref, sc_ref, sh_ref, h_ref, ht_ref):
        xv = x_ref[...]
        r = lax.rsqrt(jnp.mean(xv * xv, axis=-1, keepdims=True) + NORM_EPS)
        h = (xv * r) * g_ref[...] * (1.0 + sc_ref[...]) + sh_ref[...]
        h_ref[...] = h.astype(BF16)
        ht_ref[...] = h.T.astype(BF16)

    return pl.pallas_call(
        body, name=name, grid=(T // tt,),
        in_specs=[pl.BlockSpec((tt, D), lambda i: (i, 0)), pl.BlockSpec((1, D), lambda i: (0, 0))]
        + _row_specs(tt, D, per, 2),
        out_specs=[pl.BlockSpec((tt, D), lambda i: (i, 0)), pl.BlockSpec((D, tt), lambda i: (0, i))],
        out_shape=[jax.ShapeDtypeStruct((T, D), BF16), jax.ShapeDtypeStruct((D, T), BF16)],
        compiler_params=_cparams(("parallel",)),
    )(x, gain, sc, sh)


def _norm_mod_bwd(x, dh, dres, gain, sc, *, S, name):
    T, D = x.shape
    B = T // S
    tt = _tile(S, 512)
    per = S // tt

    def body(x_ref, dh_ref, dres_ref, g_ref, sc_ref, dx_ref, dsh_ref, dsc_ref, dg_ref):
        i = pl.program_id(0)
        xv = x_ref[...]
        dhv = dh_ref[...].astype(F32)
        r = lax.rsqrt(jnp.mean(xv * xv, axis=-1, keepdims=True) + NORM_EPS)
        n = xv * r
        g = g_ref[...]
        one_sc = 1.0 + sc_ref[...]
        dn = dhv * (g * one_sc)
        dx_ref[...] = dres_ref[...] + r * (dn - n * jnp.mean(dn * n, axis=-1, keepdims=True))
        dhn = dhv * n

        @pl.when(i % per == 0)
        def _():
            dsh_ref[...] = jnp.zeros_like(dsh_ref)
            dsc_ref[...] = jnp.zeros_like(dsc_ref)

        @pl.when(i == 0)
        def _():
            dg_ref[...] = jnp.zeros_like(dg_ref)

        dsh_ref[...] += jnp.sum(dhv, axis=0, keepdims=True)
        dsc_ref[...] += jnp.sum(dhn, axis=0, keepdims=True) * g
        dg_ref[...] += jnp.sum(dhn, axis=0, keepdims=True) * one_sc

    return pl.pallas_call(
        body, name=name, grid=(T // tt,),
        in_specs=[pl.BlockSpec((tt, D), lambda i: (i, 0))] * 3 + [pl.BlockSpec((1, D), lambda i: (0, 0))]
        + _row_specs(tt, D, per, 1),
        out_specs=[pl.BlockSpec((tt, D), lambda i: (i, 0))] + _row_specs(tt, D, per, 2)
        + [pl.BlockSpec((1, D), lambda i: (0, 0))],
        out_shape=[jax.ShapeDtypeStruct((T, D), F32), jax.ShapeDtypeStruct((B, 1, D), F32),
                   jax.ShapeDtypeStruct((B, 1, D), F32), jax.ShapeDtypeStruct((1, D), F32)],
        compiler_params=_cparams(("arbitrary",)),
    )(x, dh, dres, gain, sc)


def _gate_bwd(dx, y, g, *, S, name):
    T, D = dx.shape
    B = T // S
    tt = _tile(S, 512)
    per = S // tt

    def body(dx_ref, y_ref, g_ref, dy_ref, dg_ref):
        i = pl.program_id(0)
        dxv = dx_ref[...]
        dy_ref[...] = (dxv * g_ref[...]).astype(BF16)

        @pl.when(i % per == 0)
        def _():
            dg_ref[...] = jnp.zeros_like(dg_ref)

        dg_ref[...] += jnp.sum(dxv * y_ref[...], axis=0, keepdims=True)

    return pl.pallas_call(
        body, name=name, grid=(T // tt,),
        in_specs=[pl.BlockSpec((tt, D), lambda i: (i, 0))] * 2 + _row_specs(tt, D, per, 1),
        out_specs=[pl.BlockSpec((tt, D), lambda i: (i, 0))] + _row_specs(tt, D, per, 1),
        out_shape=[jax.ShapeDtypeStruct((T, D), BF16), jax.ShapeDtypeStruct((B, 1, D), F32)],
        compiler_params=_cparams(("arbitrary",)),
    )(dx, y, g)


def _final_loss(x, target, gain):
    T, D = x.shape
    tt = _tile(T, 512)

    def body(x_ref, t_ref, g_ref, dx_ref, dg_ref, loss_ref):
        i = pl.program_id(0)
        xv = x_ref[...]
        r = lax.rsqrt(jnp.mean(xv * xv, axis=-1, keepdims=True) + NORM_EPS)
        n = xv * r
        g = g_ref[...]
        err = n * g - t_ref[...]
        dy = err * (1.0 / D)
        dn = dy * g
        dx_ref[...] = r * (dn - n * jnp.mean(dn * n, axis=-1, keepdims=True))

        @pl.when(i == 0)
        def _():
            dg_ref[...] = jnp.zeros_like(dg_ref)
            loss_ref[...] = jnp.zeros_like(loss_ref)

        dg_ref[...] += jnp.sum(dy * n, axis=0, keepdims=True)
        loss_ref[...] += jnp.sum(jnp.sum(err * err, axis=-1, keepdims=True), axis=0, keepdims=True) * (0.5 / D)

    return pl.pallas_call(
        body, name="final_loss", grid=(T // tt,),
        in_specs=[pl.BlockSpec((tt, D), lambda i: (i, 0))] * 2 + [pl.BlockSpec((1, D), lambda i: (0, 0))],
        out_specs=[pl.BlockSpec((tt, D), lambda i: (i, 0)), pl.BlockSpec((1, D), lambda i: (0, 0)),
                   pl.BlockSpec((1, LANES), lambda i: (0, 0))],
        out_shape=[jax.ShapeDtypeStruct((T, D), F32), jax.ShapeDtypeStruct((1, D), F32),
                   jax.ShapeDtypeStruct((1, LANES), F32)],
        compiler_params=_cparams(("arbitrary",)),
    )(x, target, gain)


def _head_masks(ew):
    lane = lax.broadcasted_iota(jnp.int32, (1, PAIR_Q), 1)
    m0 = (lane < HEAD_DIM) | ((lane >= LANES) & (lane < LANES + ew))
    m1 = ((lane >= HEAD_DIM) & (lane < LANES)) | ((lane >= LANES + ew) & (lane < LANES + 2 * ew))
    return m0, m1


def _dot_nt(a, b):
    return lax.dot_general(a, b, (((1,), (1,)), ((), ())), preferred_element_type=F32)


def _dot_tn(a, b):
    return lax.dot_general(a, b, (((0,), (0,)), ((), ())), preferred_element_type=F32)


def _lane_halves(x, op):
    acc = x[:, 0:LANES]
    for g in range(1, x.shape[1] // LANES):
        acc = op(acc, x[:, g * LANES:(g + 1) * LANES])
    return acc


def _head_rows(cols_lane_replicated):
    t = cols_lane_replicated.T
    sub = lax.broadcasted_iota(jnp.int32, (8, t.shape[1]), 0)
    return jnp.where(sub == 1, t[HEAD_DIM:HEAD_DIM + 8], t[0:8])


def _attn_fwd(qx, kvx, *, S, ew, name):
    T = qx.shape[0]
    P = qx.shape[1] // PAIR_Q
    B = T // S
    tk = _tile(S, ATTN_BLOCK)
    tq = _tile(S, ATTN_Q_ROWS)
    nq = S // tq
    per = tq // tk

    def body(q_ref, kv_ref, o_ref, lse_ref, ot_ref, m_sc, l_sc, acc_sc):
        qi = pl.program_id(2)
        q = q_ref[...]
        masks = _head_masks(ew)
        qh = [jnp.where(m, q, jnp.zeros_like(q)) for m in masks]

        def logits(h, k, diagonal):
            s = _dot_nt(qh[h], k)
            if diagonal is None:
                return s
            row = lax.broadcasted_iota(jnp.int32, s.shape, 0)
            col = lax.broadcasted_iota(jnp.int32, s.shape, 1)
            return jnp.where(col + diagonal * tk <= row, s, NEG_BIG)

        def trip(first, count, n_diagonal=0):
            rows = [pl.ds(pl.multiple_of((first + u) * tk, tk), tk) for u in range(count)]
            diag = [None] * (count - n_diagonal) + list(range(n_diagonal))
            for h in range(2):
                ss = [logits(h, kv_ref[rows[u], 0:PAIR_Q], diag[u]) for u in range(count)]
                m_prev = m_sc[h]
                m_elem = m_prev
                for s in ss:
                    m_elem = jnp.maximum(m_elem, _lane_halves(s, jnp.maximum))
                m_new = jnp.broadcast_to(jnp.max(m_elem, axis=1, keepdims=True), (tq, LANES))
                alpha = jnp.exp(m_prev - m_new)
                l = alpha * l_sc[h]
                acc = alpha * acc_sc[h]
                for u, s in enumerate(ss):
                    p = jnp.concatenate([jnp.exp(s[:, g * LANES:(g + 1) * LANES] - m_new)
                                         for g in range(tk // LANES)], axis=1)
                    l = l + _lane_halves(p, jnp.add)
                    acc = acc + jnp.dot(p.astype(BF16), kv_ref[rows[u], PAIR_Q:PAIR_KV], preferred_element_type=F32)
                m_sc[h] = m_new
                l_sc[h] = l
                acc_sc[h] = acc

        m_sc[...] = jnp.full(m_sc.shape, NEG_BIG, F32)
        l_sc[...] = jnp.zeros_like(l_sc)
        acc_sc[...] = jnp.zeros_like(acc_sc)

        def loop_body(t, carry):
            trip(t * ATTN_UNROLL, ATTN_UNROLL)
            return carry

        below = qi * per
        lax.fori_loop(0, below // ATTN_UNROLL, loop_body, 0)
        for left in range(0, ATTN_UNROLL, per):
            @pl.when(below % ATTN_UNROLL == left)
            def _(left=left):
                trip(below - left, left + per, n_diagonal=per)

        lane = lax.broadcasted_iota(jnp.int32, (tq, LANES), 1)
        lo = lane < HEAD_DIM
        l = [jnp.sum(l_sc[h], axis=1, keepdims=True) for h in range(2)]
        o = jnp.where(lo, acc_sc[0] / l[0], acc_sc[1] / l[1])
        o_ref[...] = o.astype(BF16)
        ot_ref[...] = o.T.astype(BF16)
        lse = jnp.where(lo, m_sc[0] + jnp.log(l[0]), m_sc[1] + jnp.log(l[1]))
        for r in range(per):
            lse_ref[r] = _head_rows(lse[r * tk:(r + 1) * tk])

    return pl.pallas_call(
        body, name=name, grid=(B, P, nq),
        in_specs=[pl.BlockSpec((tq, PAIR_Q), lambda b, p, i: (b * nq + i, p)),
                  pl.BlockSpec((S, PAIR_KV), lambda b, p, i: (b, p))],
        out_specs=[pl.BlockSpec((tq, LANES), lambda b, p, i: (b * nq + i, p)),
                   pl.BlockSpec((per, None, 8, tk), lambda b, p, i: (b * nq + i, p, 0, 0)),
                   pl.BlockSpec((LANES, tq), lambda b, p, i: (p, b * nq + i))],
        out_shape=[jax.ShapeDtypeStruct((T, P * LANES), BF16), jax.ShapeDtypeStruct((T // tk, P, 8, tk), F32),
                   jax.ShapeDtypeStruct((P * LANES, T), BF16)],
        scratch_shapes=[pltpu.VMEM((2, tq, LANES), F32)] * 3,
        compiler_params=_cparams(("parallel", "parallel", "arbitrary")),
    )(qx, kvx)


def _attn_bwd(qx, kvx, o, lse, do, *, S, ew, name, bias_grad=False):
    T = qx.shape[0]
    P = qx.shape[1] // PAIR_Q
    B = T // S
    tq = _tile(S, ATTN_BLOCK)
    tk = _tile(S, ATTN_K_ROWS)
    nq = S // tq
    nk = S // tk
    per = tk // tq

    def body(q_ref, kv_ref, o_ref, lse_ref, do_ref, dq_ref, dkv_ref, *rest):
        kj = pl.program_id(2)
        if bias_grad:
            csum_ref, rsum_ref, dq_sc, delta_sc, dk_sc, dv_sc, cs_sc = rest
            cs_sc[...] = jnp.zeros_like(cs_sc)

            @pl.when(kj == 0)
            def _():
                rsum_ref[...] = jnp.zeros_like(rsum_ref)
        else:
            dq_sc, delta_sc, dk_sc, dv_sc = rest
        masks = _head_masks(ew)
        lo_q = lax.broadcasted_iota(jnp.int32, (tq, LANES), 1) < HEAD_DIM
        lo = lax.broadcasted_iota(jnp.int32, (tk, LANES), 1) < HEAD_DIM
        vmask = [lo, jnp.logical_not(lo)]

        @pl.when(kj == 0)
        def _():
            dq_sc[...] = jnp.zeros_like(dq_sc)
            for c in range(nq):
                rows = pl.ds(c * tq, tq)
                x = do_ref[rows, :].astype(F32) * o_ref[rows, :].astype(F32)
                r0 = jnp.sum(jnp.where(lo_q, x, 0.0), axis=1, keepdims=True)
                r1 = jnp.sum(jnp.where(lo_q, 0.0, x), axis=1, keepdims=True)
                delta_sc[c] = _head_rows(jnp.where(lo_q, r0, r1))

        k = kv_ref[:, 0:PAIR_Q]
        v = kv_ref[:, PAIR_Q:PAIR_KV]
        kh = [jnp.where(m, k, jnp.zeros_like(k)) for m in masks]
        vh = [jnp.where(m, v, jnp.zeros_like(v)) for m in vmask]
        dk_sc[...] = jnp.zeros_like(dk_sc)
        dv_sc[...] = jnp.zeros_like(dv_sc)

        def step(qi, diagonal):
            rows = pl.ds(pl.multiple_of(qi * tq, tq), tq)
            q = q_ref[rows, :]
            dov = do_ref[rows, :]
            lse8 = lse_ref[qi]
            dl8 = delta_sc[qi]
            for h in range(2):
                st = _dot_nt(kh[h], q)
                if diagonal is not None:
                    key = lax.broadcasted_iota(jnp.int32, st.shape, 0)
                    qry = lax.broadcasted_iota(jnp.int32, st.shape, 1)
                    st = jnp.where(key <= qry + diagonal * tq, st, NEG_BIG)
                pt = jnp.exp(st - lse8[h:h + 1, :])
                dpt = _dot_nt(vh[h], dov)
                dst = pt * (dpt - dl8[h:h + 1, :])
                if bias_grad:
                    cs_sc[h] += _lane_halves(dst, jnp.add)
                    rsum_ref[qi, h:h + 1, :] += jnp.sum(dst, axis=0, keepdims=True)
                ptb = pt.astype(BF16)
                dstb = dst.astype(BF16)
                dv_sc[h] += jnp.dot(ptb, dov, preferred_element_type=F32)
                dk_sc[h] += jnp.dot(dstb, q, preferred_element_type=F32)
                dq_sc[rows, :] += _dot_tn(dstb, kh[h])

        first = kj * per
        above = nq - per - first
        for left in range(0, ATTN_UNROLL, per):
            @pl.when(above % ATTN_UNROLL == left)
            def _(left=left):
                for d in range(per):
                    step(first + d, d)
                for u in range(left):
                    step(first + per + u, None)

        def loop_body(t, carry):
            for u in range(ATTN_UNROLL):
                step(first + per + above % ATTN_UNROLL + t * ATTN_UNROLL + u, None)
            return carry

        lax.fori_loop(0, above // ATTN_UNROLL, loop_body, 0)
        dkv_ref[:, 0:PAIR_Q] = (jnp.where(masks[0], dk_sc[0], 0.0) + jnp.where(masks[1], dk_sc[1], 0.0)).astype(BF16)
        dkv_ref[:, PAIR_Q:PAIR_KV] = jnp.where(lo, dv_sc[0], dv_sc[1]).astype(BF16)
        if bias_grad:
            csum_ref[...] = jnp.where(lo, jnp.sum(cs_sc[0], axis=1, keepdims=True),
                                      jnp.sum(cs_sc[1], axis=1, keepdims=True))

        @pl.when(kj == nk - 1)
        def _():
            dq_ref[...] = dq_sc[...].astype(BF16)

    rows_spec = pl.BlockSpec((nq, None, 8, tq), lambda b, p, j: (b, p, 0, 0))
    out_specs = [pl.BlockSpec((S, PAIR_Q), lambda b, p, j: (b, p)),
                 pl.BlockSpec((tk, PAIR_KV), lambda b, p, j: (b * nk + j, p))]
    out_shape = [jax.ShapeDtypeStruct((T, P * PAIR_Q), BF16), jax.ShapeDtypeStruct((T, P * PAIR_KV), BF16)]
    scratch = [pltpu.VMEM((S, PAIR_Q), F32), pltpu.VMEM((nq, 8, tq), F32),
               pltpu.VMEM((2, tk, PAIR_Q), F32), pltpu.VMEM((2, tk, LANES), F32)]
    if bias_grad:
        out_specs += [pl.BlockSpec((tk, LANES), lambda b, p, j: (b * nk + j, p)), rows_spec]
        out_shape += [jax.ShapeDtypeStruct((T, P * LANES), F32), jax.ShapeDtypeStruct((T // tq, P, 8, tq), F32)]
        scratch.append(pltpu.VMEM((2, tk, LANES), F32))
    return pl.pallas_call(
        body, name=name, grid=(B, P, nk),
        in_specs=[pl.BlockSpec((S, PAIR_Q), lambda b, p, j: (b, p)),
                  pl.BlockSpec((tk, PAIR_KV), lambda b, p, j: (b * nk + j, p)),
                  pl.BlockSpec((S, LANES), lambda b, p, j: (b, p)), rows_spec,
                  pl.BlockSpec((S, LANES), lambda b, p, j: (b, p))],
        out_specs=out_specs, out_shape=out_shape, scratch_shapes=scratch,
        compiler_params=_cparams(("parallel", "parallel", "arbitrary")),
    )(qx, kvx, o, lse, do)


def _fox_consts(P):
    H = 2 * P
    eq = np.zeros((3 * LANES, P * LANES), np.float32)
    ek = np.zeros((3 * LANES, P * LANES), np.float32)
    ones_q = np.zeros((1, P * LANES), np.float32)
    ones_k = np.zeros((1, P * LANES), np.float32)
    for h in range(H):
        base = (h // 2) * LANES + FOX_EXTRA * (h % 2)
        for part in range(3):
            eq[part * LANES + h, base + part] = 1.0
            ones_q[0, base + 3 + part] = 1.0
            ones_k[0, base + part] = 1.0
            ek[part * LANES + h, base + 3 + part] = -1.0
    return eq, ek, ones_q, ones_k


def _split3(f):
    hi = f.astype(BF16)
    r = f - hi.astype(F32)
    mid = r.astype(BF16)
    lo = (r - mid.astype(F32)).astype(BF16)
    return hi, mid, lo


def _tri_sum(tri, x):
    hi, mid, lo = _split3(x)
    return (jnp.dot(tri, hi, preferred_element_type=F32) + jnp.dot(tri, mid, preferred_element_type=F32)
            + jnp.dot(tri, lo, preferred_element_type=F32))


def _log1p_pos(e):
    return jnp.where(e < 0.01, e * (1.0 - e * (0.5 - e * (1.0 / 3.0))), jnp.log(1.0 + e))


def _fox_prep(qkv, fl, b_row, *, S, D, name):
    T = qkv.shape[0]
    P = D // LANES
    B = T // S
    tt = _tile(S, 256)
    per = S // tt
    eq, ek, ones_q, ones_k = _fox_consts(P)
    q_scale = HEAD_DIM ** -0.5

    def body(q_ref, k_ref, v_ref, fl_ref, b_ref, eq_ref, ek_ref, oq_ref, ok_ref, qx_ref, kvx_ref, carry):
        i = pl.program_id(1)

        @pl.when(i == 0)
        def _():
            carry[...] = jnp.zeros_like(carry)

        z = fl_ref[...] + b_ref[...]
        logf = jnp.minimum(z, 0.0) - _log1p_pos(jnp.exp(-jnp.abs(z)))
        row = lax.broadcasted_iota(jnp.int32, (tt, tt), 0)
        col = lax.broadcasted_iota(jnp.int32, (tt, tt), 1)
        tri = (col <= row).astype(BF16)
        f = _tri_sum(tri, logf) + carry[...]
        carry[...] = f[tt - 1:tt, :]
        parts = jnp.concatenate(_split3(f), axis=1)
        xq = jnp.dot(parts, eq_ref[...], preferred_element_type=F32) + oq_ref[...]
        xk = jnp.dot(parts, ek_ref[...], preferred_element_type=F32) + ok_ref[...]
        for p in range(P):
            c = slice(p * LANES, (p + 1) * LANES)
            qx_ref[:, p * PAIR_Q:p * PAIR_Q + LANES] = (q_ref[:, c].astype(F32) * q_scale).astype(BF16)
            qx_ref[:, p * PAIR_Q + LANES:(p + 1) * PAIR_Q] = xq[:, c].astype(BF16)
            kvx_ref[:, p * PAIR_KV:p * PAIR_KV + LANES] = k_ref[:, c]
            kvx_ref[:, p * PAIR_KV + LANES:p * PAIR_KV + PAIR_Q] = xk[:, c].astype(BF16)
            kvx_ref[:, p * PAIR_KV + PAIR_Q:(p + 1) * PAIR_KV] = v_ref[:, c]

    tok = lambda b, i: (b * per + i, 0)
    const = lambda b, i: (0, 0)
    return pl.pallas_call(
        body, name=name, grid=(B, per),
        in_specs=[pl.BlockSpec((tt, D), lambda b, i: (b * per + i, 0)),
                  pl.BlockSpec((tt, D), lambda b, i: (b * per + i, 1)),
                  pl.BlockSpec((tt, D), lambda b, i: (b * per + i, 2)),
                  pl.BlockSpec((tt, LANES), tok), pl.BlockSpec((1, LANES), const),
                  pl.BlockSpec(eq.shape, const), pl.BlockSpec(ek.shape, const),
                  pl.BlockSpec(ones_q.shape, const), pl.BlockSpec(ones_k.shape, const)],
        out_specs=[pl.BlockSpec((tt, P * PAIR_Q), tok), pl.BlockSpec((tt, P * PAIR_KV), tok)],
        out_shape=[jax.ShapeDtypeStruct((T, P * PAIR_Q), BF16), jax.ShapeDtypeStruct((T, P * PAIR_KV), BF16)],
        scratch_shapes=[pltpu.VMEM((1, LANES), F32)],
        compiler_params=_cparams(("arbitrary", "arbitrary")),
    )(qkv, qkv, qkv, fl, b_row, jnp.asarray(eq, BF16), jnp.asarray(ek, BF16), jnp.asarray(ones_q), jnp.asarray(ones_k))


def _fox_unprep(dqx, dkvx, csum, rsum, fl, b_row, *, S, D, name):
    T = dqx.shape[0]
    P = D // LANES
    B = T // S
    tt = _tile(S, 256)
    per = S // tt
    q_scale = HEAD_DIM ** -0.5

    def body(dq_ref, dkv_ref, cs_ref, rs_ref, fl_ref, b_ref, dqkv_ref, dfl_ref, db_ref, carry):
        b = pl.program_id(0)
        i = pl.program_id(1)

        @pl.when(i == 0)
        def _():
            carry[...] = jnp.zeros_like(carry)

        @pl.when((i == 0) & (b == 0))
        def _():
            db_ref[...] = jnp.zeros_like(db_ref)

        df = rs_ref[...] - cs_ref[...]
        for p in range(P):
            rq = slice(p * LANES, (p + 1) * LANES)
            dqkv_ref[:, rq] = (dq_ref[:, p * PAIR_Q:p * PAIR_Q + LANES].astype(F32) * q_scale).astype(BF16)
            dqkv_ref[:, D + p * LANES:D + (p + 1) * LANES] = dkv_ref[:, p * PAIR_KV:p * PAIR_KV + LANES]
            dqkv_ref[:, 2 * D + p * LANES:2 * D + (p + 1) * LANES] = dkv_ref[:, p * PAIR_KV + PAIR_Q:(p + 1) * PAIR_KV]
        row = lax.broadcasted_iota(jnp.int32, (tt, tt), 0)
        col = lax.broadcasted_iota(jnp.int32, (tt, tt), 1)
        tri = (col >= row).astype(BF16)
        dlogf = _tri_sum(tri, df) + carry[...]
        carry[...] = dlogf[0:1, :]
        z = fl_ref[...] + b_ref[...]
        e = jnp.exp(-jnp.abs(z))
        sig_neg = jnp.where(z >= 0.0, e, 1.0) / (1.0 + e)
        dfl = dlogf * sig_neg
        dfl_ref[...] = dfl.astype(BF16)
        db_ref[...] += jnp.sum(dfl, axis=0, keepdims=True)

    rev = lambda b, i: (b * per + per - 1 - i, 0)
    const = lambda b, i: (0, 0)
    return pl.pallas_call(
        body, name=name, grid=(B, per),
        in_specs=[pl.BlockSpec((tt, P * PAIR_Q), rev), pl.BlockSpec((tt, P * PAIR_KV), rev),
                  pl.BlockSpec((tt, LANES), rev), pl.BlockSpec((tt, LANES), rev), pl.BlockSpec((tt, LANES), rev),
                  pl.BlockSpec((1, LANES), const)],
        out_specs=[pl.BlockSpec((tt, 3 * D), rev), pl.BlockSpec((tt, LANES), rev), pl.BlockSpec((1, LANES), const)],
        out_shape=[jax.ShapeDtypeStruct((T, 3 * D), BF16), jax.ShapeDtypeStruct((T, LANES), BF16),
                   jax.ShapeDtypeStruct((1, LANES), F32)],
        scratch_shapes=[pltpu.VMEM((1, LANES), F32)],
        compiler_params=_cparams(("arbitrary", "arbitrary")),
    )(dqx, dkvx, csum, rsum, fl, b_row)


def _rms(x):
    r = lax.rsqrt(jnp.mean(x * x, axis=-1, keepdims=True) + NORM_EPS)
    return x * r, r


def _mla_mid(lat, gq, gkv, cos_t, sin_s, *, name):
    T, W = lat.shape
    Rq = W - 2 * LANES
    tt = _tile(T, 512)

    def body(l_ref, gq_ref, gkv_ref, c_ref, s_ref, o_ref, ot_ref):
        nq, _ = _rms(l_ref[:, 0:Rq])
        nkv, _ = _rms(l_ref[:, Rq:Rq + LANES])
        parts = [nq * gq_ref[...], nkv * gkv_ref[...], _rope128(l_ref[:, Rq + LANES:W], c_ref[...], s_ref[...])]
        out = jnp.concatenate(parts, axis=1)
        o_ref[...] = out.astype(BF16)
        ot_ref[...] = out.T.astype(BF16)

    return pl.pallas_call(
        body, name=name, grid=(T // tt,),
        in_specs=[pl.BlockSpec((tt, W), lambda i: (i, 0)), pl.BlockSpec((1, Rq), lambda i: (0, 0)),
                  pl.BlockSpec((1, LANES), lambda i: (0, 0)), pl.BlockSpec((tt, LANES), lambda i: (i, 0)),
                  pl.BlockSpec((tt, LANES), lambda i: (i, 0))],
        out_specs=[pl.BlockSpec((tt, W), lambda i: (i, 0)), pl.BlockSpec((W, tt), lambda i: (0, i))],
        out_shape=[jax.ShapeDtypeStruct((T, W), BF16), jax.ShapeDtypeStruct((W, T), BF16)],
        compiler_params=_cparams(("parallel",)),
    )(lat, gq, gkv, cos_t, sin_s)


def _mla_mid_bwd(lat, dcq, dckr, gq, gkv, cos_t, sin_s, *, name):
    T, W = lat.shape
    Rq = W - 2 * LANES
    tt = _tile(T, 512)

    def norm_bwd(x, dy, g):
        n, r = _rms(x)
        dn = dy * g
        return r * (dn - n * jnp.mean(dn * n, axis=-1, keepdims=True)), jnp.sum(dy * n, axis=0, keepdims=True)

    def body(l_ref, dq_ref, dk_ref, gq_ref, gkv_ref, c_ref, s_ref, o_ref, dgq_ref, dgkv_ref):
        i = pl.program_id(0)

        @pl.when(i == 0)
        def _():
            dgq_ref[...] = jnp.zeros_like(dgq_ref)
            dgkv_ref[...] = jnp.zeros_like(dgkv_ref)

        dxq, dgq = norm_bwd(l_ref[:, 0:Rq], dq_ref[...], gq_ref[...])
        dxkv, dgkv = norm_bwd(l_ref[:, Rq:Rq + LANES], dk_ref[:, 0:LANES], gkv_ref[...])
        o_ref[:, 0:Rq] = dxq.astype(BF16)
        o_ref[:, Rq:Rq + LANES] = dxkv.astype(BF16)
        o_ref[:, Rq + LANES:W] = _rope128(dk_ref[:, LANES:2 * LANES], c_ref[...], -s_ref[...]).astype(BF16)
        dgq_ref[...] += dgq
        dgkv_ref[...] += dgkv

    return pl.pallas_call(
        body, name=name, grid=(T // tt,),
        in_specs=[pl.BlockSpec((tt, W), lambda i: (i, 0)), pl.BlockSpec((tt, Rq), lambda i: (i, 0)),
                  pl.BlockSpec((tt, 2 * LANES), lambda i: (i, 0)), pl.BlockSpec((1, Rq), lambda i: (0, 0)),
                  pl.BlockSpec((1, LANES), lambda i: (0, 0)), pl.BlockSpec((tt, LANES), lambda i: (i, 0)),
                  pl.BlockSpec((tt, LANES), lambda i: (i, 0))],
        out_specs=[pl.BlockSpec((tt, W), lambda i: (i, 0)), pl.BlockSpec((1, Rq), lambda i: (0, 0)),
                   pl.BlockSpec((1, LANES), lambda i: (0, 0))],
        out_shape=[jax.ShapeDtypeStruct((T, W), BF16), jax.ShapeDtypeStruct((1, Rq), F32),
                   jax.ShapeDtypeStruct((1, LANES), F32)],
        compiler_params=_cparams(("arbitrary",)),
    )(lat, dcq, dckr, gq, gkv, cos_t, sin_s)


def _uq_to_pairs(w):
    Rq = w.shape[0]
    P = w.shape[1] // (2 * (HEAD_DIM + ROPE_DIM))
    w4 = w.reshape(Rq, P, 2, HEAD_DIM + ROPE_DIM)
    nope = w4[..., :HEAD_DIM].reshape(Rq, P, 2 * HEAD_DIM)
    rope = w4[..., HEAD_DIM:].reshape(Rq, P, 2 * ROPE_DIM)
    pad = jnp.zeros((Rq, P, PAIR_Q - 2 * HEAD_DIM - 2 * ROPE_DIM), w.dtype)
    return jnp.concatenate([nope, rope, pad], axis=-1).reshape(Rq, P * PAIR_Q)


def _uq_from_pairs(g):
    Rq = g.shape[0]
    P = g.shape[1] // PAIR_Q
    g3 = g.reshape(Rq, P, PAIR_Q)
    nope = g3[..., :2 * HEAD_DIM].reshape(Rq, P, 2, HEAD_DIM)
    rope = g3[..., 2 * HEAD_DIM:2 * HEAD_DIM + 2 * ROPE_DIM].reshape(Rq, P, 2, ROPE_DIM)
    return jnp.concatenate([nope, rope], axis=-1).reshape(Rq, P * 2 * (HEAD_DIM + ROPE_DIM))


def _ukv_to_pairs(w):
    P = w.shape[1] // (4 * HEAD_DIM)
    w4 = w.reshape(KV_RANK, P, 2, 2 * HEAD_DIM)
    kn = w4[..., :HEAD_DIM].reshape(KV_RANK, P, 2 * HEAD_DIM)
    vv = w4[..., HEAD_DIM:].reshape(KV_RANK, P, 2 * HEAD_DIM)
    top = jnp.concatenate([kn, jnp.zeros((KV_RANK, P, LANES), w.dtype), vv], axis=-1)
    place = np.zeros((LANES, P, PAIR_KV), np.float32)
    for r in range(ROPE_DIM):
        place[r, :, LANES + r] = 1.0
        place[r, :, LANES + ROPE_DIM + r] = 1.0
    return jnp.concatenate([top, jnp.asarray(place, w.dtype)], axis=0).reshape(KV_RANK + LANES, P * PAIR_KV)


def _ukv_from_pairs(g):
    P = g.shape[1] // PAIR_KV
    g3 = g[:KV_RANK].reshape(KV_RANK, P, PAIR_KV)
    kn = g3[..., :2 * HEAD_DIM].reshape(KV_RANK, P, 2, HEAD_DIM)
    vv = g3[..., PAIR_Q:].reshape(KV_RANK, P, 2, HEAD_DIM)
    return jnp.concatenate([kn, vv], axis=-1).reshape(KV_RANK, P * 4 * HEAD_DIM)


def _residual_then_norm(acc, xr, g, gain, sc, sh):
    x_out = xr + g * acc
    r = lax.rsqrt(jnp.mean(x_out * x_out, axis=-1, keepdims=True) + NORM_EPS)
    h = (x_out * r) * gain * (1.0 + sc) + sh
    return x_out, acc, h, h


def _gated_out(a, w_stack, layer, x, gate, next_norm, *, S, name):
    if next_norm is None:
        return _mm(a, w_stack, "nn", name=name, b_layer=layer, out_dtypes=(F32, BF16), extras=(x,), rowvecs=(gate,),
                   seq=S, epilogue=lambda acc, xr, g: (xr + g * acc, acc)) + (None, None)
    return _mm(a, w_stack, "nn", name=name, b_layer=layer, out_dtypes=(F32, BF16, BF16, BF16),
               out_t=(False, False, False, True), extras=(x,), rowvecs=(gate,) + tuple(next_norm), seq=S,
               full_rows=True, epilogue=_residual_then_norm)


def _mlp_fwd(h2, w, i, x1, gate, next_norm, *, S):
    def act(acc):
        u = jnp.square(jnp.maximum(acc, 0.0))
        return acc, u, u

    p, u, u_t = _mm(h2, w["mlp_w1"], "nn", name=f"mlp_up_{i}", b_layer=i, out_dtypes=(BF16, BF16, BF16),
                    out_t=(False, False, True), epilogue=act)
    x2, z, h, h_t = _gated_out(u, w["mlp_w2"], i, x1, gate, next_norm, S=S, name=f"mlp_down_{i}")
    return x2, (p, u_t, z), h, h_t


STACKED_GRADS = ("fox_out", "mla_down", "mla_uq", "mla_ukv", "mla_out", "mlp_w1", "mlp_w2")


def _local_step(x, target, pos_f, inv_freq_row, sign_row, mod, w, slots, *, S, hooks=None):
    hooks = hooks or {}
    T, D = x.shape
    L = mod.shape[0]
    L2 = len(w["fox_out"])
    cos_t, sin_s = _rope_tables(pos_f, inv_freq_row, sign_row)
    saved = []
    B = mod.shape[2]

    def per_sequence(gain):
        return jnp.broadcast_to(gain[None], (B,) + gain.shape)

    h, h_t = _norm_mod(x, w["norm_mix_g"][0], mod[0, 1], mod[0, 0], S=S, name="norm_mix_0")
    for i in range(L):
        j = i // 2
        sh_m, sc_m, g_m, sh_f, sc_f, g_f = (mod[i, s] for s in range(6))
        if i % 2 == 0:
            qkv = _mm(h, w["fox_qkv"], "nn", name=f"fox_qkv_{i}", b_layer=j, out_dtypes=(BF16,))
            fl = _mm(h, w["fox_f"], "nn", name=f"fox_f_{i}", b_layer=j)
            qx, kvx = _fox_prep(qkv, fl, w["fox_b"][j], S=S, D=D, name=f"fox_prep_{i}")
            o, lse, o_t = _attn_fwd(qx, kvx, S=S, ew=FOX_EXTRA, name=f"fox_attn_{i}")
            mix = (qx, kvx, o, lse, o_t, fl)
            w_out = w["fox_out"]
        else:
            lat = _mm(h, w["mla_down"], "nn", name=f"mla_down_{i}", b_layer=j)
            Rq = lat.shape[1] - 2 * LANES
            cqr, cqr_t = _mla_mid(lat, w["mla_gq"][j], w["mla_gkv"][j], cos_t, sin_s, name=f"mla_mid_{i}")
            qx = _mm(cqr, w["mla_uq"], "nn", name=f"mla_uq_{i}", b_layer=j, out_dtypes=(BF16,), a_sz=Rq, tk=Rq,
                     tables=(cos_t, sin_s), epilogue=lambda acc, c, s: (_rope_pairs(acc * MLA_SCALE, c, s, 1.0),))
            kvx = _mm(cqr, w["mla_ukv"], "nn", name=f"mla_ukv_{i}", b_layer=j, out_dtypes=(BF16,), a_off=Rq,
                      a_sz=2 * LANES, tk=2 * LANES, tn=PAIR_KV)
            o, lse, o_t = _attn_fwd(qx, kvx, S=S, ew=ROPE_DIM, name=f"mla_attn_{i}")
            mix = (qx, kvx, o, lse, o_t, lat, cqr_t)
            w_out = w["mla_out"]
        x1, y, h2, h2_t = _gated_out(o, w_out, j, x, g_m, (per_sequence(w["norm_mlp_g"][i]), sc_f, sh_f), S=S,
                                     name=f"mix_out_{i}")
        if i == 0 and "fwd_mlp0" in hooks:
            w = hooks["fwd_mlp0"](x1, w)
        next_norm = (per_sequence(w["norm_mix_g"][i + 1]), mod[i + 1, 1], mod[i + 1, 0]) if i + 1 < L else None
        x2, mlp, h_next, h_next_t = _mlp_fwd(h2, w, i, x1, g_f, next_norm, S=S)
        saved.append((x, h_t, mix, y, x1, h2_t, mlp))
        x, h, h_t = x2, h_next, h_next_t
        if i == 0 and "fwd_layer1" in hooks:
            w = hooks["fwd_layer1"](x, w)

    dx, dg_final, loss = _final_loss(x, target, w["final_norm_g"])
    n_split = w["mlp_w1"][0][0].shape[1]

    grads = {k: [None] * len(w[k]) for k in ("norm_mix_g", "norm_mlp_g", "fox_b", "mla_gq", "mla_gkv")}
    grads.update({k: [None] * L2 for k in ("fox_qkv", "fox_f")})
    grads.update({k: {} for k in STACKED_GRADS})
    grads["final_norm_g"] = dg_final

    def stacked(key, layer, _, a_t, b, **kw):
        group, idx, count = slots[(key, layer)]
        grads[key][group] = _mm(a_t, b, "nn", out_stack=(grads[key].get(group), idx, count), **kw)

    dmod = [None] * L
    for i in reversed(range(L)):
        j = i // 2
        x0, h_t, mix, y, x1, h2_t, (p, u_t, z) = saved[i]
        sh_m, sc_m, g_m, sh_f, sc_f, g_f = (mod[i, s] for s in range(6))
        if i == 0 and "bwd_layer0" in hooks:
            g_f = g_f + hooks["bwd_layer0"](grads)[0, 0]
        dz, dg_f = _gate_bwd(dx, z, g_f, S=S, name=f"gate_mlp_bwd_{i}")
        stacked("mlp_w2", i, L, u_t, dz, name=f"mlp_w2_grad_{i}")
        dp = _mm(dz, w["mlp_w2"], "nt", name=f"mlp_down_bwd_{i}", b_layer=i, out_dtypes=(BF16,), extras=(p,),
                 epilogue=lambda acc, pv: (acc * (2.0 * jnp.maximum(pv.astype(F32), 0.0)),))
        stacked("mlp_w1", i, L, h2_t, dp, name=f"mlp_w1_grad_{i}", out_split=n_split)
        if i == 0 and "bwd_mix0" in hooks:
            g_m = g_m + hooks["bwd_mix0"](grads)[0, 0]
        dh2 = _mm(dp, w["mlp_w1"], "nt", name=f"mlp_up_bwd_{i}", b_layer=i)
        dx1, dsh_f, dsc_f, dgn = _norm_mod_bwd(x1, dh2, dx, w["norm_mlp_g"][i], sc_f, S=S, name=f"norm_mlp_bwd_{i}")
        grads["norm_mlp_g"][i] = dgn
        dy, dg_m = _gate_bwd(dx1, y, g_m, S=S, name=f"gate_mix_bwd_{i}")
        if i % 2 == 0:
            qx, kvx, o, lse, o_t, fl = mix
            stacked("fox_out", j, L2, o_t, dy, name=f"fox_out_grad_{i}")
            do = _mm(dy, w["fox_out"], "nt", name=f"fox_out_bwd_{i}", b_layer=j, out_dtypes=(BF16,))
            dqx, dkvx, csum, rsum = _attn_bwd(qx, kvx, o, lse, do, S=S, ew=FOX_EXTRA, name=f"fox_attn_bwd_{i}",
                                              bias_grad=True)
            n_heads = D // HEAD_DIM
            csum = jnp.pad(csum.reshape(T, n_heads, HEAD_DIM)[:, :, 0], ((0, 0), (0, LANES - n_heads)))
            rsum = jnp.transpose(rsum[:, :, :2, :], (0, 3, 1, 2)).reshape(T, n_heads)
            rsum = jnp.pad(rsum, ((0, 0), (0, LANES - n_heads)))
            dqkv, dfl, db = _fox_unprep(dqx, dkvx, csum, rsum, fl, w["fox_b"][j], S=S, D=D, name=f"fox_unprep_{i}")
            grads["fox_b"][j] = db
            grads["fox_qkv"][j] = _mm(h_t, dqkv, "nn", name=f"fox_qkv_grad_{i}")
            grads["fox_f"][j] = _mm(h_t, dfl, "nn", name=f"fox_f_grad_{i}")
            dh_f = _mm(dfl, w["fox_f"], "nt", name=f"fox_f_bwd_{i}", b_layer=j)
            dh = _mm(dqkv, w["fox_qkv"], "nt", name=f"fox_qkv_bwd_{i}", b_layer=j, extras=(dh_f,),
                     epilogue=lambda acc, e: (acc + e,))
        else:
            qx, kvx, o, lse, o_t, lat, cqr_t = mix
            Rq = lat.shape[1] - 2 * LANES
            stacked("mla_out", j, L2, o_t, dy, name=f"mla_out_grad_{i}")
            do = _mm(dy, w["mla_out"], "nt", name=f"mla_out_bwd_{i}", b_layer=j, out_dtypes=(BF16,))
            dqx, dkvx = _attn_bwd(qx, kvx, o, lse, do, S=S, ew=ROPE_DIM, name=f"mla_attn_bwd_{i}")
            dqpre = _unrope(dqx, cos_t, sin_s)
            stacked("mla_uq", j, L2, cqr_t[:Rq], dqpre, name=f"mla_uq_grad_{i}", out_split=n_split)
            stacked("mla_ukv", j, L2, cqr_t[Rq:], dkvx, name=f"mla_ukv_grad_{i}", tn=PAIR_KV, out_split=n_split)
            dcq = _mm(dqpre, w["mla_uq"], "nt", name=f"mla_uq_bwd_{i}", b_layer=j)
            dckr = _mm(dkvx, w["mla_ukv"], "nt", name=f"mla_ukv_bwd_{i}", b_layer=j, tk=PAIR_KV * 2)
            dlat, dgq, dgkv = _mla_mid_bwd(lat, dcq, dckr, w["mla_gq"][j], w["mla_gkv"][j], cos_t, sin_s,
                                           name=f"mla_mid_bwd_{i}")
            grads["mla_gq"][j] = dgq
            grads["mla_gkv"][j] = dgkv
            stacked("mla_down", j, L2, h_t, dlat, name=f"mla_down_grad_{i}")
            dh = _mm(dlat, w["mla_down"], "nt", name=f"mla_down_bwd_{i}", b_layer=j)
        dx, dsh_m, dsc_m, dgn = _norm_mod_bwd(x0, dh, dx1, w["norm_mix_g"][i], sc_m, S=S, name=f"norm_mix_bwd_{i}")
        grads["norm_mix_g"][i] = dgn
        dmod[i] = jnp.stack([dsh_m, dsc_m, dg_m, dsh_f, dsc_f, dg_f])
    return loss, dx, jnp.stack(dmod), grads


GATHERED = ("fox_in", "fox_out", "mla_down", "mla_uq", "mla_ukv", "mla_out", "mlp_w1", "mlp_w2")
ROW_SHARDED = ("fox_out", "mla_down", "mla_out", "mlp_w2")


def _shard_layouts(wts):
    dkv = wts["mla_w_dkv"]
    dkv = jnp.pad(dkv, ((0, 0), (0, 0), (0, 2 * LANES - dkv.shape[2])))
    return {
        "fox_in": _pad_lanes(wts["fox_w_in"].astype(BF16)),
        "fox_out": wts["fox_w_out"].astype(BF16),
        "mla_down": jnp.concatenate([wts["mla_w_dq"], dkv], axis=2).astype(BF16),
        "mla_uq": jax.vmap(_uq_to_pairs)(wts["mla_w_uq"].astype(BF16)),
        "mla_ukv": jax.vmap(_ukv_to_pairs)(wts["mla_w_ukv"].astype(BF16)),
        "mla_out": wts["mla_w_out"].astype(BF16),
        "mlp_w1": wts["mlp_w1"].astype(BF16),
        "mlp_w2": wts["mlp_w2"].astype(BF16),
    }


def _small_layouts(small):
    return {
        "fox_b": [jnp.pad(b, (0, LANES - b.shape[0]))[None, :] for b in small["fox_b_f"]],
        "mla_gq": [g[None, :] for g in small["mla_q_norm_g"]],
        "mla_gkv": [g[None, :] for g in small["mla_kv_norm_g"]],
        "norm_mix_g": [g[None, :] for g in small["norm_mix_g"]],
        "norm_mlp_g": [g[None, :] for g in small["norm_mlp_g"]],
        "final_norm_g": small["final_norm_g"][None, :],
    }


def _comm_groups(L, L2):
    rest = [("fox_in", 1, L2 - 1), ("fox_out", 1, L2 - 1), ("mla_down", 0, L2), ("mla_uq", 0, L2),
            ("mla_ukv", 0, L2), ("mla_out", 0, L2), ("mlp_w1", 1, L - 1), ("mlp_w2", 1, L - 1)]
    return {"mix0": [("fox_in", 0, 1), ("fox_out", 0, 1)], "mlp0": [("mlp_w1", 0, 1), ("mlp_w2", 0, 1)],
            "rest": [e for e in rest if e[2] > 0]}


def _layer_slots(groups):
    return {(n, s + l): (g, l, cnt) for g, entries in groups.items() for n, s, cnt in entries for l in range(cnt)}


def _pad_lanes(a):
    cols = a.shape[-1]
    return jnp.pad(a, [(0, 0)] * (a.ndim - 1) + [(0, -cols % LANES)])


def _weight_views(name, gathered, D, n_fox_heads):
    n, ns, rows, cols = gathered.shape
    if name == "fox_in":
        true_cols = (3 * D + n_fox_heads) // ns
        fox = jnp.concatenate([gathered[:, k, :, :true_cols] for k in range(ns)], axis=-1)
        return {"fox_qkv": fox[:, :, :3 * D], "fox_f": _pad_lanes(fox[:, :, 3 * D:])}
    if name in ROW_SHARDED:
        return {name: gathered.reshape(n, ns * rows, cols)}
    return {name: gathered}


def _grad_pieces(name, g, qkv_f, n_fox_heads, ns):
    if name == "fox_in":
        fox = jnp.stack([jnp.concatenate([a, b[:, :n_fox_heads]], axis=1) for a, b in qkv_f])
        cols = fox.shape[2] // ns
        return jnp.stack([_pad_lanes(fox[:, :, k * cols:(k + 1) * cols]) for k in range(ns)], axis=1)
    if name in ROW_SHARDED:
        return g.reshape(g.shape[0], ns, g.shape[1] // ns, g.shape[2])
    return g


def _small_grads(g, n_fox_heads):
    return {
        "norm_mix_g": jnp.concatenate(g["norm_mix_g"], axis=0),
        "norm_mlp_g": jnp.concatenate(g["norm_mlp_g"], axis=0),
        "final_norm_g": g["final_norm_g"][0],
        "fox_b_f": jnp.concatenate(g["fox_b"], axis=0)[:, :n_fox_heads],
        "mla_q_norm_g": jnp.concatenate(g["mla_gq"], axis=0),
        "mla_kv_norm_g": jnp.concatenate(g["mla_gkv"], axis=0),
    }


def _silu(c):
    return c * (1.0 / (1.0 + jnp.exp(-c)))


def _ada_fwd(c_all, ada_w, ada_b_cols):
    L, D, C = ada_w.shape
    Bg = c_all.shape[0]
    tc = _tile(C, 512)

    def body(c_ref, w_ref, b_ref, o_ref):
        ca = _silu(c_ref[...]).astype(BF16)
        o_ref[...] = jnp.dot(ca, w_ref[...].astype(BF16), preferred_element_type=F32) + b_ref[...]

    return pl.pallas_call(
        body, name="ada_fwd", grid=(L, C // tc),
        in_specs=[pl.BlockSpec((Bg, D), lambda l, j: (0, 0)), pl.BlockSpec((None, D, tc), lambda l, j: (l, 0, j)),
                  pl.BlockSpec((None, 1, tc), lambda l, j: (l, 0, j))],
        out_specs=pl.BlockSpec((None, Bg, tc), lambda l, j: (l, 0, j)),
        out_shape=jax.ShapeDtypeStruct((L, Bg, C), F32),
        compiler_params=_cparams(("parallel", "parallel")),
    )(c_all, ada_w, ada_b_cols)


def _ada_bwd(c_all, dmod_cols):
    L, Bg, C = dmod_cols.shape
    D = c_all.shape[1]
    tc = _tile(C, 512)

    def body(c_ref, d_ref, o_ref):
        ca = _silu(c_ref[...]).astype(BF16)
        o_ref[...] = _dot_tn(ca, d_ref[...].astype(BF16))

    return pl.pallas_call(
        body, name="ada_bwd", grid=(L, C // tc),
        in_specs=[pl.BlockSpec((Bg, D), lambda l, j: (0, 0)), pl.BlockSpec((None, Bg, tc), lambda l, j: (l, 0, j))],
        out_specs=pl.BlockSpec((None, D, tc), lambda l, j: (l, 0, j)),
        out_shape=jax.ShapeDtypeStruct((L, D, C), F32),
        compiler_params=_cparams(("parallel", "parallel")),
    )(c_all, dmod_cols)


def _adamw_update(w, gv, m, v):
    mn = ADAM_B1 * m + (1.0 - ADAM_B1) * gv
    vn = ADAM_B2 * v + (1.0 - ADAM_B2) * jnp.square(gv)
    m_hat = mn / (1.0 - ADAM_B1 ** ADAM_STEP)
    v_hat = vn / (1.0 - ADAM_B2 ** ADAM_STEP)
    return -ADAM_LR * (m_hat / (jnp.sqrt(v_hat) + ADAM_EPS) + ADAM_WD * w), mn, vn


def _adamw(w, g, m, v, *, name):
    shape = w.shape
    C = shape[-1]
    R = int(np.prod(shape[:-1])) if len(shape) > 1 else 1
    w2, g2, m2, v2 = (a.reshape(R, C) for a in (w, g, m, v))
    tr = _row_tile(R, C)

    def body(w_ref, g_ref, m_ref, v_ref, d_ref, nm_ref, nv_ref):
        d_ref[...], nm_ref[...], nv_ref[...] = _adamw_update(w_ref[...], g_ref[...], m_ref[...], v_ref[...])

    spec = pl.BlockSpec((tr, C), lambda i: (i, 0))
    out = pl.pallas_call(
        body, name=name, grid=(R // tr,), in_specs=[spec] * 4, out_specs=[spec] * 3,
        out_shape=[jax.ShapeDtypeStruct((R, C), F32)] * 3, compiler_params=_cparams(("parallel",)),
    )(w2, g2, m2, v2)
    return tuple(a.reshape(shape) for a in out)


def _adamw_halves(w, g_own, g_peer, m, v, c_idx, *, name):
    L, rows, C = w.shape
    R = rows // 2
    tr = _row_tile(R, C)

    def body(c_ref, w_ref, go_ref, gp_ref, m_ref, v_ref, g_ref, d_ref, nm_ref, nv_ref):
        gv = jnp.where(pl.program_id(1) == c_ref[0], go_ref[...], gp_ref[...])
        g_ref[...] = gv
        d_ref[...], nm_ref[...], nv_ref[...] = _adamw_update(w_ref[...], gv, m_ref[...], v_ref[...])

    full = pl.BlockSpec((None, None, tr, C), lambda l, hh, i, c_ref: (l, hh, i, 0))
    half = pl.BlockSpec((None, tr, C), lambda l, hh, i, c_ref: (l, i, 0))
    grid_spec = pltpu.PrefetchScalarGridSpec(
        num_scalar_prefetch=1, grid=(L, 2, R // tr), in_specs=[full, half, half, full, full], out_specs=[full] * 4)
    split = lambda a: a.reshape(L, 2, R, C)
    out = pl.pallas_call(
        body, name=name, grid_spec=grid_spec, out_shape=[jax.ShapeDtypeStruct((L, 2, R, C), F32)] * 4,
        compiler_params=_cparams(("parallel", "parallel", "parallel")),
    )(c_idx, split(w), g_own, g_peer, split(m), split(v))
    return tuple(a.reshape(w.shape) for a in out)


def _sum_gathered(dm8, sm8):
    n_dev, Bl, R, D = dm8.shape
    Rs = sm8.shape[1]

    def body(dm_ref, sm_ref, ob_ref, os_ref):
        acc_b = jnp.zeros((R, D), F32)
        acc_s = jnp.zeros((Rs, D), F32)
        for d in range(n_dev):
            for b in range(Bl):
                acc_b = acc_b + dm_ref[d, b]
            acc_s = acc_s + sm_ref[d]
        ob_ref[...] = acc_b
        os_ref[...] = acc_s

    return pl.pallas_call(
        body, name="sum_gathered",
        out_shape=[jax.ShapeDtypeStruct((R, D), F32), jax.ShapeDtypeStruct((Rs, D), F32)],
        compiler_params=_cparams(None),
    )(dm8, sm8)


N_DEV = 8
N_CHIP = 4
ANY = pl.BlockSpec(memory_space=pl.ANY)
HBM = pl.BlockSpec(memory_space=pltpu.HBM)
SEM = pl.BlockSpec(memory_space=pltpu.SEMAPHORE)
DATAFLOW = pltpu.SideEffectType.DATAFLOW_SIDE_EFFECTING


def _mesh_pos():
    return lax.axis_index("x"), lax.axis_index("y"), lax.axis_index("c")


def _all_gather8(block, *, name, in_vmem):
    R, W = block.shape

    def body(x_ref, out_ref, send_sems, recv_sems, local_sem):
        x, y, c = _mesh_pos()
        me, sibling = (x, y, c), (x, y, 1 - c)
        chips = [(1 - x, y), (x, 1 - y), (1 - x, 1 - y)]

        def slot(px, py, pc):
            return out_ref.at[4 * px + 2 * py + pc]

        def copy(k, blk, to, src=None):
            return pltpu.make_async_remote_copy(
                src_ref=slot(*blk) if src is None else src, dst_ref=slot(*blk),
                send_sem=send_sems.at[k], recv_sem=recv_sems.at[k], device_id=to, device_id_type=MESH_ID)

        mine = pltpu.make_async_copy(x_ref, slot(*me), local_sem)
        mine.start()
        first = [copy(0, me, sibling, src=x_ref)]
        first += [copy(1 + j, me, (*chip, c), src=x_ref) for j, chip in enumerate(chips)]
        for cp in first:
            cp.start()
        passed = [copy(4 + j, (*chip, c), sibling) for j, chip in enumerate(chips)]
        for j, chip in enumerate(chips):
            copy(1 + j, (*chip, c), me).wait_recv()
            passed[j].start()
        copy(0, sibling, me).wait_recv()
        for j, chip in enumerate(chips):
            copy(4 + j, (*chip, 1 - c), me).wait_recv()
        for cp in first + passed:
            cp.wait_send()
        mine.wait()

    space = pl.BlockSpec(memory_space=pltpu.VMEM) if in_vmem else ANY
    return pl.pallas_call(
        body, name=name, out_shape=jax.ShapeDtypeStruct((N_DEV, R, W), block.dtype),
        in_specs=[space], out_specs=space,
        scratch_shapes=[pltpu.SemaphoreType.DMA((7,)), pltpu.SemaphoreType.DMA((7,)), pltpu.SemaphoreType.DMA],
        compiler_params=pltpu.CompilerParams(vmem_limit_bytes=VMEM_LIMIT_V7X),
    )(block)


def _comm_call(body, arrays, out_shapes, n_sems, *, name):
    return pl.pallas_call(
        body, name=name, out_shape=out_shapes, in_specs=[ANY] * len(arrays), out_specs=[ANY] * len(out_shapes),
        scratch_shapes=[pltpu.SemaphoreType.DMA((n_sems,)), pltpu.SemaphoreType.DMA((n_sems,)),
                        pltpu.SemaphoreType.DMA((len(arrays),))],
    )(*arrays)


def _gather_weights(shards, *, name):
    n = len(shards)

    def body(*refs):
        xs, outs = refs[:n], refs[n:2 * n]
        send_sems, recv_sems, local_sems = refs[2 * n:]
        x, y, c = _mesh_pos()
        me, sibling = (x, y, c), (x, y, 1 - c)
        chips = [(1 - x, y), (x, 1 - y), (1 - x, 1 - y)]
        waits = []
        for i in range(n):
            nl = shards[i].shape[0]
            own = xs[i].at[pl.ds(0, nl), c]

            def slot(px, py, pc, i=i, nl=nl):
                return outs[i].at[pl.ds(0, nl), 2 * px + py, pc]

            def copy(k, blk, to, src=None, i=i, slot=slot):
                return pltpu.make_async_remote_copy(
                    src_ref=slot(*blk) if src is None else src, dst_ref=slot(*blk),
                    send_sem=send_sems.at[7 * i + k], recv_sem=recv_sems.at[7 * i + k], device_id=to,
                    device_id_type=MESH_ID)

            mine = pltpu.make_async_copy(own, slot(*me), local_sems.at[i])
            mine.start()
            first = [copy(0, me, sibling, src=own)]
            first += [copy(1 + j, me, (*chip, c), src=own) for j, chip in enumerate(chips)]
            for cp in first:
                cp.start()
            waits.append((copy, mine, first))
        for copy, mine, first in waits:
            passed = [copy(4 + j, (*chip, c), sibling) for j, chip in enumerate(chips)]
            for j, chip in enumerate(chips):
                copy(1 + j, (*chip, c), me).wait_recv()
                passed[j].start()
            copy(0, sibling, me).wait_recv()
            for j, chip in enumerate(chips):
                copy(4 + j, (*chip, 1 - c), me).wait_recv()
            for cp in first + passed:
                cp.wait_send()
            mine.wait()

    out_shapes = [jax.ShapeDtypeStruct((s.shape[0], N_CHIP) + s.shape[1:], s.dtype) for s in shards]
    return _comm_call(body, shards, out_shapes, 7 * n, name=name)


def _place_own(shard, chip_idx, c_idx, *, name):
    n, _, rows, cols = shard.shape
    tr = _row_tile(rows, cols)

    def body(k_ref, c_ref, x_ref, o_ref):
        o_ref[...] = x_ref[...]

    grid_spec = pltpu.PrefetchScalarGridSpec(
        num_scalar_prefetch=2, grid=(n, rows // tr),
        in_specs=[pl.BlockSpec((None, None, tr, cols), lambda l, i, k_ref, c_ref: (l, c_ref[0], i, 0))],
        out_specs=pl.BlockSpec((None, None, None, tr, cols), lambda l, i, k_ref, c_ref: (l, k_ref[0], c_ref[0], i, 0)))
    return pl.pallas_call(
        body, name=name, grid_spec=grid_spec,
        out_shape=jax.ShapeDtypeStruct((n, N_CHIP, 2, rows, cols), shard.dtype),
        compiler_params=_cparams(("parallel", "parallel")),
    )(chip_idx, c_idx, shard)


def _gather_copies(x_refs, land_refs, send_sems, recv_sems):
    x, y, c = _mesh_pos()
    k_me = 2 * x + y
    targets = [(x, y, 1 - c), (1 - x, y, c), (x, 1 - y, c), (1 - x, 1 - y, c)]
    copies = []
    for i, (x_ref, land_ref) in enumerate(zip(x_refs, land_refs)):
        nl = x_ref.shape[0]
        for j, to in enumerate(targets):
            copies.append(pltpu.make_async_remote_copy(
                src_ref=x_ref.at[pl.ds(0, nl), c], dst_ref=land_ref.at[pl.ds(0, nl), k_me, c],
                send_sem=send_sems.at[4 * i + j], recv_sem=recv_sems.at[4 * i + j], device_id=to,
                device_id_type=MESH_ID))
    return copies


def _split_start(copies_fn, srcs, lands, after, *, name, sems_per_array):
    n = len(srcs)

    def body(*refs):
        send_sems, recv_sems = refs[2 * n + 1], refs[2 * n + 2]
        for cp in copies_fn(refs[:n], refs[n:2 * n], send_sems, recv_sems):
            cp.start()
        refs[-1][...] = jnp.zeros_like(refs[-1])

    operands = [pltpu.with_memory_space_constraint(a, pltpu.HBM) for a in list(srcs) + list(lands)]
    n_sems = sems_per_array * n
    out_shape = ([pltpu.SemaphoreType.DMA((n_sems,)), pltpu.SemaphoreType.DMA((n_sems,))]
                 + [pltpu.HBM(a.shape, a.dtype) for a in operands] + [jax.ShapeDtypeStruct((8, LANES), F32)])
    res = pl.pallas_call(
        body, name=name, out_shape=out_shape, in_specs=[HBM] * (2 * n) + [ANY],
        out_specs=[SEM, SEM] + [HBM] * (2 * n) + [pl.BlockSpec(memory_space=pltpu.VMEM)],
        input_output_aliases={i: 2 + i for i in range(2 * n)},
        compiler_params=pltpu.CompilerParams(has_side_effects=DATAFLOW),
    )(*operands, after)
    return res[0], res[1], list(res[2:2 + n]), list(res[2 + n:2 + 2 * n]), res[-1]


def _split_wait(copies_fn, send_sems, recv_sems, srcs, lands, after, *, name):
    n = len(srcs)

    def body(*refs):
        for cp in copies_fn(refs[:n], refs[n:2 * n], refs[2 * n], refs[2 * n + 1]):
            cp.wait_send()
            cp.wait_recv()

    res = pl.pallas_call(
        body, name=name, out_shape=[pltpu.HBM(a.shape, a.dtype) for a in list(srcs) + list(lands)],
        in_specs=[HBM] * (2 * n) + [SEM, SEM, ANY], out_specs=[HBM] * (2 * n),
        input_output_aliases={i: i for i in range(2 * n)},
        compiler_params=pltpu.CompilerParams(has_side_effects=DATAFLOW),
    )(*srcs, *lands, send_sems, recv_sems, after)
    return list(res[:n]), list(res[n:])


def _gather_forward(lands, *, name):
    n = len(lands)

    def body(*refs):
        xs = refs[:n]
        send_sems, recv_sems, _ = refs[2 * n:]
        x, y, c = _mesh_pos()
        chips = [(1 - x, y), (x, 1 - y), (1 - x, 1 - y)]
        copies = []
        for i in range(n):
            nl = lands[i].shape[0]
            for j, (cx, cy) in enumerate(chips):
                here = xs[i].at[pl.ds(0, nl), 2 * cx + cy, c]
                cp = pltpu.make_async_remote_copy(
                    src_ref=here, dst_ref=here, send_sem=send_sems.at[3 * i + j], recv_sem=recv_sems.at[3 * i + j],
                    device_id=(x, y, 1 - c), device_id_type=MESH_ID)
                cp.start()
                copies.append(cp)
        for cp in copies:
            cp.wait()

    return pl.pallas_call(
        body, name=name, out_shape=[jax.ShapeDtypeStruct(a.shape, a.dtype) for a in lands],
        in_specs=[ANY] * n, out_specs=[ANY] * n, input_output_aliases={i: i for i in range(n)},
        scratch_shapes=[pltpu.SemaphoreType.DMA((3 * n,)), pltpu.SemaphoreType.DMA((3 * n,)),
                        pltpu.SemaphoreType.DMA((1,))],
    )(*lands)


def _pair_copies(g_refs, land_refs, send_sems, recv_sems):
    x, y, c = _mesh_pos()
    copies = []
    for i, (g_ref, land_ref) in enumerate(zip(g_refs, land_refs)):
        nl, ns = g_ref.shape[:2]
        copies.append(pltpu.make_async_remote_copy(
            src_ref=g_ref.at[pl.ds(0, nl), pl.ds(0, ns), 1 - c], dst_ref=land_ref, send_sem=send_sems.at[i],
            recv_sem=recv_sems.at[i], device_id=(x, y, 1 - c), device_id_type=MESH_ID))
    return copies


def _pair_exchange(gs, *, name):
    n = len(gs)

    def body(*refs):
        send_sems, recv_sems, _ = refs[2 * n:]
        copies = _pair_copies(refs[:n], refs[n:2 * n], send_sems, recv_sems)
        for cp in copies:
            cp.start()
        for cp in copies:
            cp.wait()

    out_shapes = [jax.ShapeDtypeStruct(g.shape[:2] + g.shape[3:], g.dtype) for g in gs]
    return _comm_call(body, gs, out_shapes, n, name=name)


def _chip_copies(p_refs, land_refs, send_sems, recv_sems):
    x, y, c = _mesh_pos()
    k_me = 2 * x + y
    chips = [(1 - x, y), (x, 1 - y), (1 - x, 1 - y)]
    copies = []
    for i, (p_ref, land_ref) in enumerate(zip(p_refs, land_refs)):
        nl = p_ref.shape[0]
        for j, (cx, cy) in enumerate(chips):
            copies.append(pltpu.make_async_remote_copy(
                src_ref=p_ref.at[pl.ds(0, nl), 2 * cx + cy], dst_ref=land_ref.at[k_me],
                send_sem=send_sems.at[3 * i + j], recv_sem=recv_sems.at[3 * i + j],
                device_id=(cx, cy, c), device_id_type=MESH_ID))
    return copies


def _chip_landing(ps):
    return [lax.empty((p.shape[1], p.shape[0]) + p.shape[2:], p.dtype) for p in ps]


def _pair_swap(ss, *, name):
    n = len(ss)

    def body(*refs):
        xs, outs = refs[:n], refs[n:2 * n]
        send_sems, recv_sems, _ = refs[2 * n:]
        x, y, c = _mesh_pos()
        copies = []
        for i in range(n):
            cp = pltpu.make_async_remote_copy(src_ref=xs[i], dst_ref=outs[i], send_sem=send_sems.at[i],
                                              recv_sem=recv_sems.at[i], device_id=(x, y, 1 - c),
                                              device_id_type=MESH_ID)
            cp.start()
            copies.append(cp)
        for cp in copies:
            cp.wait()

    out_shapes = [jax.ShapeDtypeStruct(s.shape, s.dtype) for s in ss]
    return _comm_call(body, ss, out_shapes, n, name=name)


def _row_tile(rows, cols):
    tr = rows
    while tr * cols > 256 * 1024 and tr % 16 == 0:
        tr //= 2
    return tr


def _pair_add(g, recv, c_idx, *, name):
    n, ns, _, rows, W = g.shape
    tr = _row_tile(rows, W)

    def body(c_ref, g_ref, r_ref, o_ref):
        o_ref[...] = (g_ref[...] + r_ref[...]).astype(BF16)

    piece = pl.BlockSpec((None, tr, W), lambda p, i, c_ref: (p, i, 0))
    grid_spec = pltpu.PrefetchScalarGridSpec(
        num_scalar_prefetch=1, grid=(n * ns, rows // tr),
        in_specs=[pl.BlockSpec((None, None, tr, W), lambda p, i, c_ref: (p, c_ref[0], i, 0)), piece],
        out_specs=piece)
    out = pl.pallas_call(
        body, name=name, grid_spec=grid_spec, out_shape=jax.ShapeDtypeStruct((n * ns, rows, W), BF16),
        compiler_params=_cparams(("parallel", "parallel")),
    )(c_idx, g.reshape(n * ns, 2, rows, W), recv.reshape(n * ns, rows, W))
    return out.reshape(n, ns, rows, W)


def _sum_pieces(land, own, chip_idx, *, name):
    n, nl, A, W = land.shape
    tr = _row_tile(A, W)

    def body(k_ref, l_ref, o_ref, out_ref):
        acc = jnp.zeros(out_ref.shape, F32)
        for k in range(n):
            acc = acc + jnp.where(k == k_ref[0], o_ref[...], l_ref[k]).astype(F32)
        out_ref[...] = acc

    grid_spec = pltpu.PrefetchScalarGridSpec(
        num_scalar_prefetch=1, grid=(nl, A // tr),
        in_specs=[pl.BlockSpec((n, None, tr, W), lambda l, i, k_ref: (0, l, i, 0)),
                  pl.BlockSpec((None, None, tr, W), lambda l, i, k_ref: (l, k_ref[0], i, 0))],
        out_specs=pl.BlockSpec((None, tr, W), lambda l, i, k_ref: (l, i, 0)))
    return pl.pallas_call(
        body, name=name, grid_spec=grid_spec, out_shape=jax.ShapeDtypeStruct((nl, A, W), F32),
        compiler_params=_cparams(("parallel", "parallel")),
    )(chip_idx, land, own)


SMALL = ("norm_mix_g", "norm_mlp_g", "final_norm_g", "fox_b_f", "mla_q_norm_g", "mla_kv_norm_g")
WEIGHT_ORDER = ("ada_w", "ada_b", "norm_mix_g", "norm_mlp_g", "fox_w_in", "fox_b_f", "fox_w_out", "mla_w_dq",
                "mla_q_norm_g", "mla_w_uq", "mla_w_dkv", "mla_kv_norm_g", "mla_w_ukv", "mla_w_out", "mlp_w1",
                "mlp_w2", "final_norm_g")


def _small_rows(vals, D):
    rows = [vals["norm_mix_g"], vals["norm_mlp_g"], vals["final_norm_g"][None, :]]
    for n in ("fox_b_f", "mla_q_norm_g", "mla_kv_norm_g"):
        flat = vals[n].reshape(-1)
        assert flat.shape[0] <= D
        rows.append(jnp.pad(flat, (0, D - flat.shape[0]))[None, :])
    return jnp.concatenate(rows, axis=0)


def _small_unrows(rows, shapes):
    L = shapes["norm_mix_g"][0]
    out = {"norm_mix_g": rows[0:L], "norm_mlp_g": rows[L:2 * L], "final_norm_g": rows[2 * L]}
    for k, n in enumerate(("fox_b_f", "mla_q_norm_g", "mla_kv_norm_g")):
        size = int(np.prod(shapes[n]))
        out[n] = rows[2 * L + 1 + k, :size].reshape(shapes[n])
    return out


def kernel(x, c, positions, ada_w, ada_b, norm_mix_g, norm_mlp_g, fox_w_in, fox_b_f, fox_w_out, mla_w_dq, mla_q_norm_g, mla_w_uq, mla_w_dkv, mla_kv_norm_g, mla_w_ukv, mla_w_out, mlp_w1, mlp_w2, final_norm_g, loss_target, m_ada_w, m_ada_b, m_norm_mix_g, m_norm_mlp_g, m_fox_w_in, m_fox_b_f, m_fox_w_out, m_mla_w_dq, m_mla_q_norm_g, m_mla_w_uq, m_mla_w_dkv, m_mla_kv_norm_g, m_mla_w_ukv, m_mla_w_out, m_mlp_w1, m_mlp_w2, m_final_norm_g, v_ada_w, v_ada_b, v_norm_mix_g, v_norm_mlp_g, v_fox_w_in, v_fox_b_f, v_fox_w_out, v_mla_w_dq, v_mla_q_norm_g, v_mla_w_uq, v_mla_w_dkv, v_mla_kv_norm_g, v_mla_w_ukv, v_mla_w_out, v_mlp_w1, v_mlp_w2, v_final_norm_g):
    args = dict(locals())
    wts = {n: args[n] for n in WEIGHT_ORDER}
    mom = {n: args["m_" + n] for n in WEIGHT_ORDER}
    var = {n: args["v_" + n] for n in WEIGHT_ORDER}
    Bl, S, D = x.shape
    T = Bl * S
    L = ada_w.shape[0]
    C = ada_w.shape[2]
    mx, my, mc = _mesh_pos()
    chip = 2 * mx + my
    dev = 4 * mx + 2 * my + mc
    c_idx = jnp.reshape(mc, (1,)).astype(jnp.int32)
    chip_idx = jnp.reshape(chip, (1,)).astype(jnp.int32)
    small = {n: wts[n] for n in SMALL}
    L2, q_cols = mla_q_norm_g.shape
    n_fox_heads = fox_b_f.shape[1]

    shards = _shard_layouts(wts)
    groups = _comm_groups(L, L2)
    slots = _layer_slots(groups)

    def row_halves(a):
        return a.reshape(a.shape[:-2] + (2, a.shape[-2] // 2, a.shape[-1]))

    def whole_rows(a):
        return a.reshape(a.shape[:2] + (a.shape[2] * a.shape[3], a.shape[4]))

    part = {g: [row_halves(shards[n][s:s + cnt]) for n, s, cnt in entries] for g, entries in groups.items()}
    mix0 = _gather_weights(part["mix0"], name="gather_mix0")
    gather_sems, after = {}, mix0[0]
    for group in ("mlp0", "rest"):
        placed = [_place_own(a, chip_idx, c_idx, name=f"gather_place_{group}_{n}")
                  for a, (n, _, _) in zip(part[group], groups[group])]
        gather_sems[group] = _split_start(_gather_copies, part[group], placed, after, name=f"gather_{group}_start",
                                          sems_per_array=4)
        after = gather_sems[group][4]

    def layer_weights(w, group, arrays):
        for (n, s, cnt), a in zip(groups[group], arrays):
            for key, view in _weight_views(n, whole_rows(a), D, n_fox_heads).items():
                for l in range(cnt):
                    w[key][s + l] = (view, l)

    w = {key: [None] * L2 for key in ("fox_qkv", "fox_f", "fox_out", "mla_down", "mla_uq", "mla_ukv", "mla_out")}
    w.update({key: [None] * L for key in ("mlp_w1", "mlp_w2")})
    layer_weights(w, "mix0", mix0)

    def gathered_now(group):
        def hook(x_now, w):
            _, landed = _split_wait(_gather_copies, *gather_sems[group][:4], x_now, name=f"gather_{group}_wait")
            layer_weights(w, group, _gather_forward(landed, name=f"gather_{group}_forward"))
            return w
        return hook

    c_pad = jnp.concatenate([c, jnp.pad(mla_q_norm_g, ((0, 8 - Bl - L2), (0, D - q_cols)))], axis=0)
    c8 = _all_gather8(c_pad, name="gather_c", in_vmem=True)
    c_all = c8[:, :Bl].reshape(N_DEV * Bl, D)
    qg4 = c8.reshape(N_CHIP, 2, 8, D)[:, 0, Bl:Bl + L2, :q_cols]
    small["mla_q_norm_g"] = jnp.transpose(qg4, (1, 0, 2)).reshape(L2, N_CHIP * q_cols)
    ada_b_cols = lax.dynamic_slice_in_dim(ada_b, chip * C, C, axis=1)[:, None, :]
    mod_cols = _ada_fwd(c_all, ada_w, ada_b_cols)
    mod8 = _all_gather8(mod_cols.reshape(L * N_DEV * Bl, C), name="gather_mod", in_vmem=True)
    mod4 = mod8.reshape(N_CHIP, 2, L, N_DEV * Bl, C)[:, 0]
    mod_me = lax.dynamic_slice_in_dim(mod4, dev * Bl, Bl, axis=2)
    mod = jnp.transpose(mod_me, (1, 2, 0, 3)).reshape(L, Bl, 6, D)
    mod = jnp.transpose(mod, (0, 2, 1, 3))[:, :, :, None, :]

    w.update(_small_layouts(small))
    mod = mod + after[0, 0]
    pending = {}

    def grad_pieces(group, g_now):
        out = []
        for n, s, cnt in groups[group]:
            qkv_f = [(g_now["fox_qkv"][j], g_now["fox_f"][j]) for j in range(s, s + cnt)] if n == "fox_in" else None
            stacked_g = None if n == "fox_in" else g_now[n][group]
            out.append(row_halves(_grad_pieces(n, stacked_g, qkv_f, n_fox_heads, N_CHIP)))
        return out

    def pair_added(group, big, sibling):
        return [_pair_add(a, r, c_idx, name=f"grad_pair_add_{group}_{n}")
                for (n, _, _), a, r in zip(groups[group], big, sibling)]

    def exchange_start(group, ps, after=None):
        pending[group] = _split_start(_chip_copies, ps, _chip_landing(ps), chip_idx if after is None else after,
                                      name=f"grad_exchange_{group}_start", sems_per_array=3)
        return pending[group][4]

    def bwd_layer0(g_now):
        big = grad_pieces("rest", g_now)
        landing = [lax.empty(a.shape[:2] + a.shape[3:], a.dtype) for a in big]
        pending["rest_pair"] = _split_start(_pair_copies, big, landing, chip_idx, name="grad_pair_rest_start",
                                            sems_per_array=1)
        return pending["rest_pair"][4]

    def bwd_mix0(g_now):
        send_sems, recv_sems, big, landed, _ = pending["rest_pair"]
        big, landed = _split_wait(_pair_copies, send_sems, recv_sems, big, landed, g_now["mlp_w1"]["mlp0"],
                                  name="grad_pair_rest_wait")
        started = exchange_start("rest", pair_added("rest", big, landed))
        big = grad_pieces("mlp0", g_now)
        return exchange_start("mlp0", pair_added("mlp0", big, _pair_exchange(big, name="grad_pair_exchange_mlp0")),
                              after=started)

    half = ROPE_DIM // 2
    inv_freq = ROPE_THETA ** (-jnp.arange(0, ROPE_DIM, 2, dtype=F32) / ROPE_DIM)
    lane = np.arange(LANES)
    inv_freq_row = jnp.tile(inv_freq, LANES // half)[None, :]
    sign_row = jnp.asarray(np.where(lane < 2 * ROPE_DIM, np.where(lane % ROPE_DIM < half, -1.0, 1.0), 0.0), F32)[None, :]
    pos_f = positions.astype(F32).reshape(T, 1)
    loss_row, grad_x, dmod, g = _local_step(x.reshape(T, D), loss_target.reshape(T, D), pos_f, inv_freq_row, sign_row,
                                            mod, w, slots, S=S,
                                            hooks={"fwd_mlp0": gathered_now("mlp0"), "fwd_layer1": gathered_now("rest"),
                                                   "bwd_layer0": bwd_layer0, "bwd_mix0": bwd_mix0})
    g_small = _small_grads(g, n_fox_heads)
    big = grad_pieces("mix0", g)
    exchange_start("mix0", pair_added("mix0", big, _pair_exchange(big, name="grad_pair_exchange_mix0")))

    Rs = -(-(2 * L + 5) // 8) * 8
    srows = jnp.concatenate([_small_rows(g_small, D), jnp.pad(loss_row, ((0, 0), (0, D - LANES)))], axis=0)
    srows = jnp.pad(srows, ((0, Rs - srows.shape[0]), (0, 0)))
    drows = jnp.transpose(dmod[:, :, :, 0, :], (2, 0, 1, 3)).reshape(Bl * L * 6, D)
    both8 = _all_gather8(jnp.concatenate([drows, srows], axis=0), name="gather_small", in_vmem=True)
    dm8 = both8[:, :Bl * L * 6].reshape(N_DEV, Bl, L * 6, D)
    sm8 = both8[:, Bl * L * 6:]
    adb_rows, small_sum = _sum_gathered(dm8, sm8)
    grad_ada_b = adb_rows.reshape(L, 6 * D)
    loss = small_sum[2 * L + 4, 0]
    small_shapes = {n: (wts[n].shape if n != "mla_q_norm_g" else (wts[n].shape[0], N_CHIP * q_cols)) for n in SMALL}
    gs = _small_unrows(small_sum, small_shapes)
    gs["mla_q_norm_g"] = lax.dynamic_slice_in_dim(gs["mla_q_norm_g"], chip * q_cols, q_cols, axis=1)

    dmod16 = jnp.transpose(dm8.reshape(N_DEV, Bl, L, 6 * D), (2, 0, 1, 3)).reshape(L, N_DEV * Bl, 6 * D)
    dmod_cols = lax.dynamic_slice_in_dim(dmod16, chip * C, C, axis=2)
    grad_ada_w = _ada_bwd(c_all, dmod_cols)

    grads = dict(gs)
    grads["ada_w"] = grad_ada_w
    grads["ada_b"] = grad_ada_b
    delta, new_m, new_v = {}, {}, {}
    for n in ("ada_w", "ada_b"):
        delta[n], new_m[n], new_v[n] = _adamw(wts[n], grads[n], mom[n], var[n], name=f"adamw_{n}")
    shard_small_shapes = {n: wts[n].shape for n in SMALL}
    packs = [jnp.pad(_small_rows({n: src[n] for n in SMALL}, D), ((0, Rs - 2 * L - 4), (0, 0)))
             for src in (wts, grads, mom, var)]
    for dst, rows in zip((delta, new_m, new_v), _adamw(*packs, name="adamw_small")):
        dst.update(_small_unrows(rows, shard_small_shapes))

    halves = {}
    for group, after in (("rest", grad_x), ("mlp0", grad_x), ("mix0", delta["ada_w"])):
        send_sems, recv_sems, ps, lands, _ = pending[group]
        ps, lands = _split_wait(_chip_copies, send_sems, recv_sems, ps, lands, after, name=f"grad_exchange_{group}_wait")
        sums = [_sum_pieces(ld, p, chip_idx, name=f"grad_sum_{group}_{n}")
                for (n, _, _), ld, p in zip(groups[group], lands, ps)]
        swapped = _pair_swap(sums, name=f"grad_pair_swap_{group}")
        for (n, _, _), a, b in zip(groups[group], sums, swapped):
            halves[(n, group)] = (a, b)

    def all_layers(n, which):
        return jnp.concatenate([halves[(n, grp)][which] for grp in groups if (n, grp) in halves], axis=0)

    own = {n: all_layers(n, 0) for n in GATHERED}
    peer = {n: all_layers(n, 1) for n in GATHERED}
    for nat, n in (("fox_w_in", "fox_in"), ("fox_w_out", "fox_out"), ("mla_w_out", "mla_out"), ("mlp_w1", "mlp_w1"),
                   ("mlp_w2", "mlp_w2")):
        cols = wts[nat].shape[-1]
        res = _adamw_halves(_pad_lanes(wts[nat]), own[n], peer[n], _pad_lanes(mom[nat]), _pad_lanes(var[nat]), c_idx,
                            name=f"adamw_{nat}")
        grads[nat], delta[nat], new_m[nat], new_v[nat] = (a[..., :cols] for a in res)
    joined = {n: jnp.concatenate([jnp.where(mc == 0, own[n], peer[n]), jnp.where(mc == 0, peer[n], own[n])], axis=1)
              for n in ("mla_down", "mla_uq", "mla_ukv")}
    rq = mla_w_dq.shape[-1]
    grads["mla_w_dq"] = joined["mla_down"][:, :, :rq]
    grads["mla_w_dkv"] = joined["mla_down"][:, :, rq:rq + KV_RANK + ROPE_DIM]
    grads["mla_w_uq"] = jax.vmap(_uq_from_pairs)(joined["mla_uq"])
    grads["mla_w_ukv"] = jax.vmap(_ukv_from_pairs)(joined["mla_ukv"])
    for n in ("mla_w_dq", "mla_w_dkv", "mla_w_uq", "mla_w_ukv"):
        delta[n], new_m[n], new_v[n] = _adamw(wts[n], grads[n], mom[n], var[n], name=f"adamw_{n}")

    return (loss, grad_x.reshape(Bl, S, D), *[grads[n] for n in WEIGHT_ORDER], *[delta[n] for n in WEIGHT_ORDER],
            *[new_m[n] for n in WEIGHT_ORDER], *[new_v[n] for n in WEIGHT_ORDER])
```

```python
import numpy as np
import jax
import jax.numpy as jnp
from jax import lax
from jax.experimental import pallas as pl
from jax.experimental.pallas import tpu as pltpu

F32 = jnp.float32
BF16 = jnp.bfloat16
MESH_ID = pl.DeviceIdType.MESH

NORM_EPS = 1e-6
ROPE_THETA = 10000.0
HEAD_DIM = 64
ROPE_DIM = 32
KV_RANK = 128
MLA_SCALE = (HEAD_DIM + ROPE_DIM) ** -0.5
FOX_EXTRA = 6
PAIR_Q = 256
PAIR_KV = 384
LANES = 128
ADAM_LR = 0.001
ADAM_B1 = 0.9
ADAM_B2 = 0.999
ADAM_EPS = 1e-08
ADAM_WD = 0.01
ADAM_STEP = 10
VMEM_LIMIT_V7X = 48 * 1024 * 1024
MM_VMEM_BUDGET = 36 * 1024 * 1024
NEG_BIG = -1e30
ATTN_UNROLL = 4
ATTN_BLOCK = 256
ATTN_Q_ROWS = 512
ATTN_K_ROWS = 512

BIG_WEIGHTS = (("fox_w_in", 2), ("fox_w_out", 1), ("mla_w_dq", 1), ("mla_w_uq", 2), ("mla_w_dkv", 1),
               ("mla_w_ukv", 2), ("mla_w_out", 1), ("mlp_w1", 2), ("mlp_w2", 1))


def _cparams(sem=None):
    return pltpu.CompilerParams(dimension_semantics=sem, vmem_limit_bytes=VMEM_LIMIT_V7X)


def _tile(n, want):
    if n <= want:
        return n
    for t in range(want - want % LANES, 0, -LANES):
        if n % t == 0:
            return t
    raise ValueError((n, want))


def _mm(a, b, mode, *, name, out_dtypes=(F32,), epilogue=None, extras=(), rowvecs=(), tables=(),
        seq=None, a_off=0, a_sz=None, b_layer=None, out_stack=None, out_split=0, out_t=(), full_rows=False,
        tm=1024, tn=1024, tk=2048):
    if isinstance(b, (list, tuple)):
        b, b_layer = b[b_layer]
    b_rows, b_cols = b.shape[-2], b.shape[-1]
    n_split = b.shape[1] if b.ndim == 4 else 1
    assert mode in ("nn", "nt")
    if mode == "nn":
        M, K, N = a.shape[0], b_rows, b_cols * n_split
    else:
        M, K, N = a.shape[0], b_cols * n_split, b_rows
    assert a_sz is None or a_sz == K
    tm = _tile(seq if rowvecs else M, tm)
    n_piece = N // max(out_split, n_split if mode == "nn" else 1, 1)
    tn = _tile(n_piece, tn)
    tk = _tile(K // (n_split if mode == "nt" else 1), tk)
    ne, nr, nt_ = len(extras), len(rowvecs), len(tables)
    no = len(out_dtypes)

    def vmem_estimate():
        blocks = tm * tk * a.dtype.itemsize + tk * tn * b.dtype.itemsize
        blocks += tm * tn * (sum(e.dtype.itemsize for e in extras) + sum(jnp.dtype(d).itemsize for d in out_dtypes))
        return 2 * blocks + 2 * tm * tn * 4

    if full_rows:
        assert tn == N
    while vmem_estimate() > MM_VMEM_BUDGET and max(tm, tn) > 256:
        if tn >= tm and not full_rows:
            tn //= 2
        else:
            tm //= 2
    nk = K // tk

    assert a_off % tk == 0
    a_spec = pl.BlockSpec((tm, tk), lambda i, j, k: (i, k + a_off // tk))
    dims = (((1,), (0,)), ((), ())) if mode == "nn" else (((1,), (1,)), ((), ()))
    lead = () if b.ndim == 2 else (b_layer,)
    sq = (None,) * (b.ndim - 2)
    if mode == "nt":
        kb = b_cols // tk
        if b.ndim == 4:
            b_spec = pl.BlockSpec(sq + (tn, tk), lambda i, j, k: lead + (k // kb, j, k % kb))
        else:
            b_spec = pl.BlockSpec(sq + (tn, tk), lambda i, j, k: lead + (j, k))
    else:
        nb = b_cols // tn
        if b.ndim == 4:
            b_spec = pl.BlockSpec(sq + (tk, tn), lambda i, j, k: lead + (j // nb, k, j % nb))
        else:
            b_spec = pl.BlockSpec(sq + (tk, tn), lambda i, j, k: lead + (k, j))
    in_specs = [a_spec, b_spec]
    in_specs += [pl.BlockSpec((tm, tn), lambda i, j, k: (i, j)) for _ in extras]
    if rowvecs:
        assert seq % tm == 0
        per = seq // tm
        in_specs += [pl.BlockSpec((None, 1, tn), lambda i, j, k: (i // per, 0, j)) for _ in rowvecs]
    in_specs += [pl.BlockSpec((tm, LANES), lambda i, j, k: (i, 0)) for _ in tables]
    operands = [a, b, *extras, *rowvecs, *tables]
    aliases = {}
    transposed = tuple(out_t) + (False,) * (no - len(out_t))
    if out_stack is None:
        out_specs = [pl.BlockSpec((tn, tm), lambda i, j, k: (j, i)) if t else pl.BlockSpec((tm, tn), lambda i, j, k: (i, j))
                     for t in transposed]
        out_shape = [jax.ShapeDtypeStruct((N, M) if t else (M, N), d) for d, t in zip(out_dtypes, transposed)]
    else:
        prev, layer, n_layers = out_stack
        assert no == 1
        if out_split:
            ob = n_piece // tn
            out_specs = [pl.BlockSpec((None, None, tm, tn), lambda i, j, k: (layer, j // ob, i, j % ob))]
            out_shape = [jax.ShapeDtypeStruct((n_layers, out_split, M, n_piece), out_dtypes[0])]
        else:
            out_specs = [pl.BlockSpec((None, tm, tn), lambda i, j, k: (layer, i, j))]
            out_shape = [jax.ShapeDtypeStruct((n_layers, M, N), out_dtypes[0])]
        if prev is not None:
            in_specs.append(pl.BlockSpec(memory_space=pl.ANY))
            aliases = {len(operands): 0}
            operands.append(prev)
    n_in = len(operands)

    def body(*refs):
        a_ref, b_ref = refs[0], refs[1]
        side = refs[2:2 + ne + nr + nt_]
        outs = refs[n_in:n_in + no]

        def finish(acc):
            res = (acc,) if epilogue is None else epilogue(acc, *[r[...] for r in side])
            for o_ref, r, t in zip(outs, res, transposed):
                o_ref[...] = (r.T if t else r).astype(o_ref.dtype)

        part = lax.dot_general(a_ref[...].astype(BF16), b_ref[...].astype(BF16), dims,
                               preferred_element_type=F32)
        if nk == 1:
            finish(part)
        else:
            acc_ref = refs[-1]
            k = pl.program_id(2)

            @pl.when(k == 0)
            def _():
                acc_ref[...] = part

            @pl.when(k > 0)
            def _():
                acc_ref[...] += part

            @pl.when(k == nk - 1)
            def _():
                finish(acc_ref[...])

    res = pl.pallas_call(
        body, name=name, grid=(M // tm, N // tn, nk), in_specs=in_specs, out_specs=out_specs,
        out_shape=out_shape, scratch_shapes=[pltpu.VMEM((tm, tn), F32)] if nk > 1 else [],
        input_output_aliases=aliases,
        compiler_params=_cparams(("parallel", "parallel", "arbitrary")),
    )(*operands)
    return res[0] if no == 1 else tuple(res)


def _rope128(x, cos_t, sin_s):
    lane = lax.broadcasted_iota(jnp.int32, x.shape, 1)
    first = (lane % ROPE_DIM) < (ROPE_DIM // 2)
    swapped = jnp.where(first, pltpu.roll(x, LANES - ROPE_DIM // 2, 1), pltpu.roll(x, ROPE_DIM // 2, 1))
    return x * cos_t + swapped * sin_s


def _rope_pairs(acc, cos_t, sin_s, sign):
    parts = []
    for p in range(acc.shape[1] // PAIR_Q):
        parts.append(acc[:, p * PAIR_Q:p * PAIR_Q + LANES])
        parts.append(_rope128(acc[:, p * PAIR_Q + LANES:(p + 1) * PAIR_Q], cos_t, sign * sin_s))
    return jnp.concatenate(parts, axis=1)


def _rope_tables(pos_f, inv_freq_row, sign_row):
    T = pos_f.shape[0]
    tt = _tile(T, 512)

    def body(p_ref, f_ref, s_ref, cos_ref, sin_ref):
        ang = p_ref[...] * f_ref[...]
        cos_ref[...] = jnp.cos(ang)
        sin_ref[...] = jnp.sin(ang) * s_ref[...]

    return pl.pallas_call(
        body, name="rope_tables", grid=(T // tt,),
        in_specs=[pl.BlockSpec((tt, 1), lambda i: (i, 0)), pl.BlockSpec((1, LANES), lambda i: (0, 0)),
                  pl.BlockSpec((1, LANES), lambda i: (0, 0))],
        out_specs=[pl.BlockSpec((tt, LANES), lambda i: (i, 0))] * 2,
        out_shape=[jax.ShapeDtypeStruct((T, LANES), F32)] * 2,
        compiler_params=_cparams(("parallel",)),
    )(pos_f, inv_freq_row, sign_row)


def _unrope(dqx, cos_t, sin_s):
    T, W = dqx.shape
    tt = _tile(T, 512)

    def body(d_ref, c_ref, s_ref, o_ref):
        o_ref[...] = _rope_pairs(d_ref[...].astype(F32) * MLA_SCALE, c_ref[...], s_ref[...], -1.0).astype(BF16)

    return pl.pallas_call(
        body, name="mla_unrope", grid=(T // tt,),
        in_specs=[pl.BlockSpec((tt, W), lambda i: (i, 0)), pl.BlockSpec((tt, LANES), lambda i: (i, 0)),
                  pl.BlockSpec((tt, LANES), lambda i: (i, 0))],
        out_specs=pl.BlockSpec((tt, W), lambda i: (i, 0)),
        out_shape=jax.ShapeDtypeStruct((T, W), BF16),
        compiler_params=_cparams(("parallel",)),
    )(dqx, cos_t, sin_s)


def _row_specs(tt, D, per, n):
    return [pl.BlockSpec((None, 1, D), lambda i: (i // per, 0, 0)) for _ in range(n)]


def _norm_mod(x, gain, sc, sh, *, S, name):
    T, D = x.shape
    tt = _tile(S, 512)
    per = S // tt

    def body(x_ref, g_ref, sc_ref, sh_ref, h_ref, ht_ref):
        xv = x_ref[...]
        r = lax.rsqrt(jnp.mean(xv * xv, axis=-1, keepdims=True) + NORM_EPS)
        h = (xv * r) * g_ref[...] * (1.0 + sc_ref[...]) + sh_ref[...]
        h_ref[...] = h.astype(BF16)
        ht_ref[...] = h.T.astype(BF16)

    return pl.pallas_call(
        body, name=name, grid=(T // tt,),
        in_specs=[pl.BlockSpec((tt, D), lambda i: (i, 0)), pl.BlockSpec((1, D), lambda i: (0, 0))]
        + _row_specs(tt, D, per, 2),
        out_specs=[pl.BlockSpec((tt, D), lambda i: (i, 0)), pl.BlockSpec((D, tt), lambda i: (0, i))],
        out_shape=[jax.ShapeDtypeStruct((T, D), BF16), jax.ShapeDtypeStruct((D, T), BF16)],
        compiler_params=_cparams(("parallel",)),
    )(x, gain, sc, sh)


def _norm_mod_bwd(x, dh, dres, gain, sc, *, S, name):
    T, D = x.shape
    B = T // S
    tt = _tile(S, 512)
    per = S // tt

    def body(x_ref, dh_ref, dres_ref, g_ref, sc_ref, dx_ref, dsh_ref, dsc_ref, dg_ref):
        i = pl.program_id(0)
        xv = x_ref[...]
        dhv = dh_ref[...].astype(F32)
        r = lax.rsqrt(jnp.mean(xv * xv, axis=-1, keepdims=True) + NORM_EPS)
        n = xv * r
        g = g_ref[...]
        one_sc = 1.0 + sc_ref[...]
        dn = dhv * (g * one_sc)
        dx_ref[...] = dres_ref[...] + r * (dn - n * jnp.mean(dn * n, axis=-1, keepdims=True))
        dhn = dhv * n

        @pl.when(i % per == 0)
        def _():
            dsh_ref[...] = jnp.zeros_like(dsh_ref)
            dsc_ref[...] = jnp.zeros_like(dsc_ref)

        @pl.when(i == 0)
        def _():
            dg_ref[...] = jnp.zeros_like(dg_ref)

        dsh_ref[...] += jnp.sum(dhv, axis=0, keepdims=True)
        dsc_ref[...] += jnp.sum(dhn, axis=0, keepdims=True) * g
        dg_ref[...] += jnp.sum(dhn, axis=0, keepdims=True) * one_sc

    return pl.pallas_call(
        body, name=name, grid=(T // tt,),
        in_specs=[pl.BlockSpec((tt, D), lambda i: (i, 0))] * 3 + [pl.BlockSpec((1, D), lambda i: (0, 0))]
        + _row_specs(tt, D, per, 1),
        out_specs=[pl.BlockSpec((tt, D), lambda i: (i, 0))] + _row_specs(tt, D, per, 2)
        + [pl.BlockSpec((1, D), lambda i: (0, 0))],
        out_shape=[jax.ShapeDtypeStruct((T, D), F32), jax.ShapeDtypeStruct((B, 1, D), F32),
                   jax.ShapeDtypeStruct((B, 1, D), F32), jax.ShapeDtypeStruct((1, D), F32)],
        compiler_params=_cparams(("arbitrary",)),
    )(x, dh, dres, gain, sc)


def _gate_bwd(dx, y, g, *, S, name):
    T, D = dx.shape
    B = T // S
    tt = _tile(S, 512)
    per = S // tt

    def body(dx_ref, y_ref, g_ref, dy_ref, dg_ref):
        i = pl.program_id(0)
        dxv = dx_ref[...]
        dy_ref[...] = (dxv * g_ref[...]).astype(BF16)

        @pl.when(i % per == 0)
        def _():
            dg_ref[...] = jnp.zeros_like(dg_ref)

        dg_ref[...] += jnp.sum(dxv * y_ref[...], axis=0, keepdims=True)

    return pl.pallas_call(
        body, name=name, grid=(T // tt,),
        in_specs=[pl.BlockSpec((tt, D), lambda i: (i, 0))] * 2 + _row_specs(tt, D, per, 1),
        out_specs=[pl.BlockSpec((tt, D), lambda i: (i, 0))] + _row_specs(tt, D, per, 1),
        out_shape=[jax.ShapeDtypeStruct((T, D), BF16), jax.ShapeDtypeStruct((B, 1, D), F32)],
        compiler_params=_cparams(("arbitrary",)),
    )(dx, y, g)


def _final_loss(x, target, gain):
    T, D = x.shape
    tt = _tile(T, 512)

    def body(x_ref, t_ref, g_ref, dx_ref, dg_ref, loss_ref):
        i = pl.program_id(0)
        xv = x_ref[...]
        r = lax.rsqrt(jnp.mean(xv * xv, axis=-1, keepdims=True) + NORM_EPS)
        n = xv * r
        g = g_ref[...]
        err = n * g - t_ref[...]
        dy = err * (1.0 / D)
        dn = dy * g
        dx_ref[...] = r * (dn - n * jnp.mean(dn * n, axis=-1, keepdims=True))

        @pl.when(i == 0)
        def _():
            dg_ref[...] = jnp.zeros_like(dg_ref)
            loss_ref[...] = jnp.zeros_like(loss_ref)

        dg_ref[...] += jnp.sum(dy * n, axis=0, keepdims=True)
        loss_ref[...] += jnp.sum(jnp.sum(err * err, axis=-1, keepdims=True), axis=0, keepdims=True) * (0.5 / D)

    return pl.pallas_call(
        body, name="final_loss", grid=(T // tt,),
        in_specs=[pl.BlockSpec((tt, D), lambda i: (i, 0))] * 2 + [pl.BlockSpec((1, D), lambda i: (0, 0))],
        out_specs=[pl.BlockSpec((tt, D), lambda i: (i, 0)), pl.BlockSpec((1, D), lambda i: (0, 0)),
                   pl.BlockSpec((1, LANES), lambda i: (0, 0))],
        out_shape=[jax.ShapeDtypeStruct((T, D), F32), jax.ShapeDtypeStruct((1, D), F32),
                   jax.ShapeDtypeStruct((1, LANES), F32)],
        compiler_params=_cparams(("arbitrary",)),
    )(x, target, gain)


def _head_masks(ew):
    lane = lax.broadcasted_iota(jnp.int32, (1, PAIR_Q), 1)
    m0 = (lane < HEAD_DIM) | ((lane >= LANES) & (lane < LANES + ew))
    m1 = ((lane >= HEAD_DIM) & (lane < LANES)) | ((lane >= LANES + ew) & (lane < LANES + 2 * ew))
    return m0, m1


def _dot_nt(a, b):
    return lax.dot_general(a, b, (((1,), (1,)), ((), ())), preferred_element_type=F32)


def _dot_tn(a, b):
    return lax.dot_general(a, b, (((0,), (0,)), ((), ())), preferred_element_type=F32)


def _lane_halves(x, op):
    acc = x[:, 0:LANES]
    for g in range(1, x.shape[1] // LANES):
        acc = op(acc, x[:, g * LANES:(g + 1) * LANES])
    return acc


def _head_rows(cols_lane_replicated):
    t = cols_lane_replicated.T
    sub = lax.broadcasted_iota(jnp.int32, (8, t.shape[1]), 0)
    return jnp.where(sub == 1, t[HEAD_DIM:HEAD_DIM + 8], t[0:8])


def _attn_fwd(qx, kvx, *, S, ew, name):
    T = qx.shape[0]
    P = qx.shape[1] // PAIR_Q
    B = T // S
    tk = _tile(S, ATTN_BLOCK)
    tq = _tile(S, ATTN_Q_ROWS)
    nq = S // tq
    per = tq // tk

    def body(q_ref, kv_ref, o_ref, lse_ref, ot_ref, m_sc, l_sc, acc_sc):
        qi = pl.program_id(2)
        q = q_ref[...]
        masks = _head_masks(ew)
        qh = [jnp.where(m, q, jnp.zeros_like(q)) for m in masks]

        def logits(h, k, diagonal):
            s = _dot_nt(qh[h], k)
            if diagonal is None:
                return s
            row = lax.broadcasted_iota(jnp.int32, s.shape, 0)
            col = lax.broadcasted_iota(jnp.int32, s.shape, 1)
            return jnp.where(col + diagonal * tk <= row, s, NEG_BIG)

        def trip(first, count, n_diagonal=0):
            rows = [pl.ds(pl.multiple_of((first + u) * tk, tk), tk) for u in range(count)]
            diag = [None] * (count - n_diagonal) + list(range(n_diagonal))
            for h in range(2):
                ss = [logits(h, kv_ref[rows[u], 0:PAIR_Q], diag[u]) for u in range(count)]
                m_prev = m_sc[h]
                m_elem = m_prev
                for s in ss:
                    m_elem = jnp.maximum(m_elem, _lane_halves(s, jnp.maximum))
                m_new = jnp.broadcast_to(jnp.max(m_elem, axis=1, keepdims=True), (tq, LANES))
                alpha = jnp.exp(m_prev - m_new)
                l = alpha * l_sc[h]
                acc = alpha * acc_sc[h]
                for u, s in enumerate(ss):
                    p = jnp.concatenate([jnp.exp(s[:, g * LANES:(g + 1) * LANES] - m_new)
                                         for g in range(tk // LANES)], axis=1)
                    l = l + _lane_halves(p, jnp.add)
                    acc = acc + jnp.dot(p.astype(BF16), kv_ref[rows[u], PAIR_Q:PAIR_KV], preferred_element_type=F32)
                m_sc[h] = m_new
                l_sc[h] = l
                acc_sc[h] = acc

        m_sc[...] = jnp.full(m_sc.shape, NEG_BIG, F32)
        l_sc[...] = jnp.zeros_like(l_sc)
        acc_sc[...] = jnp.zeros_like(acc_sc)

        def loop_body(t, carry):
            trip(t * ATTN_UNROLL, ATTN_UNROLL)
            return carry

        below = qi * per
        lax.fori_loop(0, below // ATTN_UNROLL, loop_body, 0)
        for left in range(0, ATTN_UNROLL, per):
            @pl.when(below % ATTN_UNROLL == left)
            def _(left=left):
                trip(below - left, left + per, n_diagonal=per)

        lane = lax.broadcasted_iota(jnp.int32, (tq, LANES), 1)
        lo = lane < HEAD_DIM
        l = [jnp.sum(l_sc[h], axis=1, keepdims=True) for h in range(2)]
        o = jnp.where(lo, acc_sc[0] / l[0], acc_sc[1] / l[1])
        o_ref[...] = o.astype(BF16)
        ot_ref[...] = o.T.astype(BF16)
        lse = jnp.where(lo, m_sc[0] + jnp.log(l[0]), m_sc[1] + jnp.log(l[1]))
        for r in range(per):
            lse_ref[r] = _head_rows(lse[r * tk:(r + 1) * tk])

    return pl.pallas_call(
        body, name=name, grid=(B, P, nq),
        in_specs=[pl.BlockSpec((tq, PAIR_Q), lambda b, p, i: (b * nq + i, p)),
                  pl.BlockSpec((S, PAIR_KV), lambda b, p, i: (b, p))],
        out_specs=[pl.BlockSpec((tq, LANES), lambda b, p, i: (b * nq + i, p)),
                   pl.BlockSpec((per, None, 8, tk), lambda b, p, i: (b * nq + i, p, 0, 0)),
                   pl.BlockSpec((LANES, tq), lambda b, p, i: (p, b * nq + i))],
        out_shape=[jax.ShapeDtypeStruct((T, P * LANES), BF16), jax.ShapeDtypeStruct((T // tk, P, 8, tk), F32),
                   jax.ShapeDtypeStruct((P * LANES, T), BF16)],
        scratch_shapes=[pltpu.VMEM((2, tq, LANES), F32)] * 3,
        compiler_params=_cparams(("parallel", "parallel", "arbitrary")),
    )(qx, kvx)


def _attn_bwd(qx, kvx, o, lse, do, *, S, ew, name, bias_grad=False):
    T = qx.shape[0]
    P = qx.shape[1] // PAIR_Q
    B = T // S
    tq = _tile(S, ATTN_BLOCK)
    tk = _tile(S, ATTN_K_ROWS)
    nq = S // tq
    nk = S // tk
    per = tk // tq

    def body(q_ref, kv_ref, o_ref, lse_ref, do_ref, dq_ref, dkv_ref, *rest):
        kj = pl.program_id(2)
        if bias_grad:
            csum_ref, rsum_ref, dq_sc, delta_sc, dk_sc, dv_sc, cs_sc = rest
            cs_sc[...] = jnp.zeros_like(cs_sc)

            @pl.when(kj == 0)
            def _():
                rsum_ref[...] = jnp.zeros_like(rsum_ref)
        else:
            dq_sc, delta_sc, dk_sc, dv_sc = rest
        masks = _head_masks(ew)
        lo_q = lax.broadcasted_iota(jnp.int32, (tq, LANES), 1) < HEAD_DIM
        lo = lax.broadcasted_iota(jnp.int32, (tk, LANES), 1) < HEAD_DIM
        vmask = [lo, jnp.logical_not(lo)]

        @pl.when(kj == 0)
        def _():
            dq_sc[...] = jnp.zeros_like(dq_sc)
            for c in range(nq):
                rows = pl.ds(c * tq, tq)
                x = do_ref[rows, :].astype(F32) * o_ref[rows, :].astype(F32)
                r0 = jnp.sum(jnp.where(lo_q, x, 0.0), axis=1, keepdims=True)
                r1 = jnp.sum(jnp.where(lo_q, 0.0, x), axis=1, keepdims=True)
                delta_sc[c] = _head_rows(jnp.where(lo_q, r0, r1))

        k = kv_ref[:, 0:PAIR_Q]
        v = kv_ref[:, PAIR_Q:PAIR_KV]
        kh = [jnp.where(m, k, jnp.zeros_like(k)) for m in masks]
        vh = [jnp.where(m, v, jnp.zeros_like(v)) for m in vmask]
        dk_sc[...] = jnp.zeros_like(dk_sc)
        dv_sc[...] = jnp.zeros_like(dv_sc)

        def step(qi, diagonal):
            rows = pl.ds(pl.multiple_of(qi * tq, tq), tq)
            q = q_ref[rows, :]
            dov = do_ref[rows, :]
            lse8 = lse_ref[qi]
            dl8 = delta_sc[qi]
            for h in range(2):
                st = _dot_nt(kh[h], q)
                if diagonal is not None:
                    key = lax.broadcasted_iota(jnp.int32, st.shape, 0)
                    qry = lax.broadcasted_iota(jnp.int32, st.shape, 1)
                    st = jnp.where(key <= qry + diagonal * tq, st, NEG_BIG)
                pt = jnp.exp(st - lse8[h:h + 1, :])
                dpt = _dot_nt(vh[h], dov)
                dst = pt * (dpt - dl8[h:h + 1, :])
                if bias_grad:
                    cs_sc[h] += _lane_halves(dst, jnp.add)
                    rsum_ref[qi, h:h + 1, :] += jnp.sum(dst, axis=0, keepdims=True)
                ptb = pt.astype(BF16)
                dstb = dst.astype(BF16)
                dv_sc[h] += jnp.dot(ptb, dov, preferred_element_type=F32)
                dk_sc[h] += jnp.dot(dstb, q, preferred_element_type=F32)
                dq_sc[rows, :] += _dot_tn(dstb, kh[h])

        first = kj * per
        above = nq - per - first
        for left in range(0, ATTN_UNROLL, per):
            @pl.when(above % ATTN_UNROLL == left)
            def _(left=left):
                for d in range(per):
                    step(first + d, d)
                for u in range(left):
                    step(first + per + u, None)

        def loop_body(t, carry):
            for u in range(ATTN_UNROLL):
                step(first + per + above % ATTN_UNROLL + t * ATTN_UNROLL + u, None)
            return carry

        lax.fori_loop(0, above // ATTN_UNROLL, loop_body, 0)
        dkv_ref[:, 0:PAIR_Q] = (jnp.where(masks[0], dk_sc[0], 0.0) + jnp.where(masks[1], dk_sc[1], 0.0)).astype(BF16)
        dkv_ref[:, PAIR_Q:PAIR_KV] = jnp.where(lo, dv_sc[0], dv_sc[1]).astype(BF16)
        if bias_grad:
            csum_ref[...] = jnp.where(lo, jnp.sum(cs_sc[0], axis=1, keepdims=True),
                                      jnp.sum(cs_sc[1], axis=1, keepdims=True))

        @pl.when(kj == nk - 1)
        def _():
            dq_ref[...] = dq_sc[...].astype(BF16)

    rows_spec = pl.BlockSpec((nq, None, 8, tq), lambda b, p, j: (b, p, 0, 0))
    out_specs = [pl.BlockSpec((S, PAIR_Q), lambda b, p, j: (b, p)),
                 pl.BlockSpec((tk, PAIR_KV), lambda b, p, j: (b * nk + j, p))]
    out_shape = [jax.ShapeDtypeStruct((T, P * PAIR_Q), BF16), jax.ShapeDtypeStruct((T, P * PAIR_KV), BF16)]
    scratch = [pltpu.VMEM((S, PAIR_Q), F32), pltpu.VMEM((nq, 8, tq), F32),
               pltpu.VMEM((2, tk, PAIR_Q), F32), pltpu.VMEM((2, tk, LANES), F32)]
    if bias_grad:
        out_specs += [pl.BlockSpec((tk, LANES), lambda b, p, j: (b * nk + j, p)), rows_spec]
        out_shape += [jax.ShapeDtypeStruct((T, P * LANES), F32), jax.ShapeDtypeStruct((T // tq, P, 8, tq), F32)]
        scratch.append(pltpu.VMEM((2, tk, LANES), F32))
    return pl.pallas_call(
        body, name=name, grid=(B, P, nk),
        in_specs=[pl.BlockSpec((S, PAIR_Q), lambda b, p, j: (b, p)),
                  pl.BlockSpec((tk, PAIR_KV), lambda b, p, j: (b * nk + j, p)),
                  pl.BlockSpec((S, LANES), lambda b, p, j: (b, p)), rows_spec,
                  pl.BlockSpec((S, LANES), lambda b, p, j: (b, p))],
        out_specs=out_specs, out_shape=out_shape, scratch_shapes=scratch,
        compiler_params=_cparams(("parallel", "parallel", "arbitrary")),
    )(qx, kvx, o, lse, do)


def _fox_consts(P):
    H = 2 * P
    eq = np.zeros((3 * LANES, P * LANES), np.float32)
    ek = np.zeros((3 * LANES, P * LANES), np.float32)
    ones_q = np.zeros((1, P * LANES), np.float32)
    ones_k = np.zeros((1, P * LANES), np.float32)
    for h in range(H):
        base = (h // 2) * LANES + FOX_EXTRA * (h % 2)
        for part in range(3):
            eq[part * LANES + h, base + part] = 1.0
            ones_q[0, base + 3 + part] = 1.0
            ones_k[0, base + part] = 1.0
            ek[part * LANES + h, base + 3 + part] = -1.0
    return eq, ek, ones_q, ones_k


def _split3(f):
    hi = f.astype(BF16)
    r = f - hi.astype(F32)
    mid = r.astype(BF16)
    lo = (r - mid.astype(F32)).astype(BF16)
    return hi, mid, lo


def _tri_sum(tri, x):
    hi, mid, lo = _split3(x)
    return (jnp.dot(tri, hi, preferred_element_type=F32) + jnp.dot(tri, mid, preferred_element_type=F32)
            + jnp.dot(tri, lo, preferred_element_type=F32))


def _log1p_pos(e):
    return jnp.where(e < 0.01, e * (1.0 - e * (0.5 - e * (1.0 / 3.0))), jnp.log(1.0 + e))


def _fox_prep(qkv, fl, b_row, *, S, D, name):
    T = qkv.shape[0]
    P = D // LANES
    B = T // S
    tt = _tile(S, 256)
    per = S // tt
    eq, ek, ones_q, ones_k = _fox_consts(P)
    q_scale = HEAD_DIM ** -0.5

    def body(q_ref, k_ref, v_ref, fl_ref, b_ref, eq_ref, ek_ref, oq_ref, ok_ref, qx_ref, kvx_ref, carry):
        i = pl.program_id(1)

        @pl.when(i == 0)
        def _():
            carry[...] = jnp.zeros_like(carry)

        z = fl_ref[...] + b_ref[...]
        logf = jnp.minimum(z, 0.0) - _log1p_pos(jnp.exp(-jnp.abs(z)))
        row = lax.broadcasted_iota(jnp.int32, (tt, tt), 0)
        col = lax.broadcasted_iota(jnp.int32, (tt, tt), 1)
        tri = (col <= row).astype(BF16)
        f = _tri_sum(tri, logf) + carry[...]
        carry[...] = f[tt - 1:tt, :]
        parts = jnp.concatenate(_split3(f), axis=1)
        xq = jnp.dot(parts, eq_ref[...], preferred_element_type=F32) + oq_ref[...]
        xk = jnp.dot(parts, ek_ref[...], preferred_element_type=F32) + ok_ref[...]
        for p in range(P):
            c = slice(p * LANES, (p + 1) * LANES)
            qx_ref[:, p * PAIR_Q:p * PAIR_Q + LANES] = (q_ref[:, c].astype(F32) * q_scale).astype(BF16)
            qx_ref[:, p * PAIR_Q + LANES:(p + 1) * PAIR_Q] = xq[:, c].astype(BF16)
            kvx_ref[:, p * PAIR_KV:p * PAIR_KV + LANES] = k_ref[:, c]
            kvx_ref[:, p * PAIR_KV + LANES:p * PAIR_KV + PAIR_Q] = xk[:, c].astype(BF16)
            kvx_ref[:, p * PAIR_KV + PAIR_Q:(p + 1) * PAIR_KV] = v_ref[:, c]

    tok = lambda b, i: (b * per + i, 0)
    const = lambda b, i: (0, 0)
    return pl.pallas_call(
        body, name=name, grid=(B, per),
        in_specs=[pl.BlockSpec((tt, D), lambda b, i: (b * per + i, 0)),
                  pl.BlockSpec((tt, D), lambda b, i: (b * per + i, 1)),
                  pl.BlockSpec((tt, D), lambda b, i: (b * per + i, 2)),
                  pl.BlockSpec((tt, LANES), tok), pl.BlockSpec((1, LANES), const),
                  pl.BlockSpec(eq.shape, const), pl.BlockSpec(ek.shape, const),
                  pl.BlockSpec(ones_q.shape, const), pl.BlockSpec(ones_k.shape, const)],
        out_specs=[pl.BlockSpec((tt, P * PAIR_Q), tok), pl.BlockSpec((tt, P * PAIR_KV), tok)],
        out_shape=[jax.ShapeDtypeStruct((T, P * PAIR_Q), BF16), jax.ShapeDtypeStruct((T, P * PAIR_KV), BF16)],
        scratch_shapes=[pltpu.VMEM((1, LANES), F32)],
        compiler_params=_cparams(("arbitrary", "arbitrary")),
    )(qkv, qkv, qkv, fl, b_row, jnp.asarray(eq, BF16), jnp.asarray(ek, BF16), jnp.asarray(ones_q), jnp.asarray(ones_k))


def _fox_unprep(dqx, dkvx, csum, rsum, fl, b_row, *, S, D, name):
    T = dqx.shape[0]
    P = D // LANES
    B = T // S
    tt = _tile(S, 256)
    per = S // tt
    q_scale = HEAD_DIM ** -0.5

    def body(dq_ref, dkv_ref, cs_ref, rs_ref, fl_ref, b_ref, dqkv_ref, db_ref, carry):
        b = pl.program_id(0)
        i = pl.program_id(1)

        @pl.when(i == 0)
        def _():
            carry[...] = jnp.zeros_like(carry)

        @pl.when((i == 0) & (b == 0))
        def _():
            db_ref[...] = jnp.zeros_like(db_ref)

        df = rs_ref[...] - cs_ref[...]
        for p in range(P):
            rq = slice(p * LANES, (p + 1) * LANES)
            dqkv_ref[:, rq] = (dq_ref[:, p * PAIR_Q:p * PAIR_Q + LANES].astype(F32) * q_scale).astype(BF16)
            dqkv_ref[:, D + p * LANES:D + (p + 1) * LANES] = dkv_ref[:, p * PAIR_KV:p * PAIR_KV + LANES]
            dqkv_ref[:, 2 * D + p * LANES:2 * D + (p + 1) * LANES] = dkv_ref[:, p * PAIR_KV + PAIR_Q:(p + 1) * PAIR_KV]
        row = lax.broadcasted_iota(jnp.int32, (tt, tt), 0)
        col = lax.broadcasted_iota(jnp.int32, (tt, tt), 1)
        tri = (col >= row).astype(BF16)
        dlogf = _tri_sum(tri, df) + carry[...]
        carry[...] = dlogf[0:1, :]
        z = fl_ref[...] + b_ref[...]
        e = jnp.exp(-jnp.abs(z))
        sig_neg = jnp.where(z >= 0.0, e, 1.0) / (1.0 + e)
        dfl = dlogf * sig_neg
        dqkv_ref[:, 3 * D:3 * D + LANES] = dfl.astype(BF16)
        db_ref[...] += jnp.sum(dfl, axis=0, keepdims=True)

    rev = lambda b, i: (b * per + per - 1 - i, 0)
    const = lambda b, i: (0, 0)
    return pl.pallas_call(
        body, name=name, grid=(B, per),
        in_specs=[pl.BlockSpec((tt, P * PAIR_Q), rev), pl.BlockSpec((tt, P * PAIR_KV), rev),
                  pl.BlockSpec((tt, LANES), rev), pl.BlockSpec((tt, LANES), rev), pl.BlockSpec((tt, LANES), rev),
                  pl.BlockSpec((1, LANES), const)],
        out_specs=[pl.BlockSpec((tt, 3 * D + LANES), rev), pl.BlockSpec((1, LANES), const)],
        out_shape=[jax.ShapeDtypeStruct((T, 3 * D + LANES), BF16), jax.ShapeDtypeStruct((1, LANES), F32)],
        scratch_shapes=[pltpu.VMEM((1, LANES), F32)],
        compiler_params=_cparams(("arbitrary", "arbitrary")),
    )(dqx, dkvx, csum, rsum, fl, b_row)


def _rms(x):
    r = lax.rsqrt(jnp.mean(x * x, axis=-1, keepdims=True) + NORM_EPS)
    return x * r, r


def _mla_mid(lat, gq, gkv, cos_t, sin_s, *, name):
    T, W = lat.shape
    Rq = W - 2 * LANES
    tt = _tile(T, 512)

    def body(l_ref, gq_ref, gkv_ref, c_ref, s_ref, o_ref, ot_ref):
        nq, _ = _rms(l_ref[:, 0:Rq])
        nkv, _ = _rms(l_ref[:, Rq:Rq + LANES])
        parts = [nq * gq_ref[...], nkv * gkv_ref[...], _rope128(l_ref[:, Rq + LANES:W], c_ref[...], s_ref[...])]
        out = jnp.concatenate(parts, axis=1)
        o_ref[...] = out.astype(BF16)
        ot_ref[...] = out.T.astype(BF16)

    return pl.pallas_call(
        body, name=name, grid=(T // tt,),
        in_specs=[pl.BlockSpec((tt, W), lambda i: (i, 0)), pl.BlockSpec((1, Rq), lambda i: (0, 0)),
                  pl.BlockSpec((1, LANES), lambda i: (0, 0)), pl.BlockSpec((tt, LANES), lambda i: (i, 0)),
                  pl.BlockSpec((tt, LANES), lambda i: (i, 0))],
        out_specs=[pl.BlockSpec((tt, W), lambda i: (i, 0)), pl.BlockSpec((W, tt), lambda i: (0, i))],
        out_shape=[jax.ShapeDtypeStruct((T, W), BF16), jax.ShapeDtypeStruct((W, T), BF16)],
        compiler_params=_cparams(("parallel",)),
    )(lat, gq, gkv, cos_t, sin_s)


def _mla_mid_bwd(lat, dcq, dckr, gq, gkv, cos_t, sin_s, *, name):
    T, W = lat.shape
    Rq = W - 2 * LANES
    tt = _tile(T, 512)

    def norm_bwd(x, dy, g):
        n, r = _rms(x)
        dn = dy * g
        return r * (dn - n * jnp.mean(dn * n, axis=-1, keepdims=True)), jnp.sum(dy * n, axis=0, keepdims=True)

    def body(l_ref, dq_ref, dk_ref, gq_ref, gkv_ref, c_ref, s_ref, o_ref, dgq_ref, dgkv_ref):
        i = pl.program_id(0)

        @pl.when(i == 0)
        def _():
            dgq_ref[...] = jnp.zeros_like(dgq_ref)
            dgkv_ref[...] = jnp.zeros_like(dgkv_ref)

        dxq, dgq = norm_bwd(l_ref[:, 0:Rq], dq_ref[...], gq_ref[...])
        dxkv, dgkv = norm_bwd(l_ref[:, Rq:Rq + LANES], dk_ref[:, 0:LANES], gkv_ref[...])
        o_ref[:, 0:Rq] = dxq.astype(BF16)
        o_ref[:, Rq:Rq + LANES] = dxkv.astype(BF16)
        o_ref[:, Rq + LANES:W] = _rope128(dk_ref[:, LANES:2 * LANES], c_ref[...], -s_ref[...]).astype(BF16)
        dgq_ref[...] += dgq
        dgkv_ref[...] += dgkv

    return pl.pallas_call(
        body, name=name, grid=(T // tt,),
        in_specs=[pl.BlockSpec((tt, W), lambda i: (i, 0)), pl.BlockSpec((tt, Rq), lambda i: (i, 0)),
                  pl.BlockSpec((tt, 2 * LANES), lambda i: (i, 0)), pl.BlockSpec((1, Rq), lambda i: (0, 0)),
                  pl.BlockSpec((1, LANES), lambda i: (0, 0)), pl.BlockSpec((tt, LANES), lambda i: (i, 0)),
                  pl.BlockSpec((tt, LANES), lambda i: (i, 0))],
        out_specs=[pl.BlockSpec((tt, W), lambda i: (i, 0)), pl.BlockSpec((1, Rq), lambda i: (0, 0)),
                   pl.BlockSpec((1, LANES), lambda i: (0, 0))],
        out_shape=[jax.ShapeDtypeStruct((T, W), BF16), jax.ShapeDtypeStruct((1, Rq), F32),
                   jax.ShapeDtypeStruct((1, LANES), F32)],
        compiler_params=_cparams(("arbitrary",)),
    )(lat, dcq, dckr, gq, gkv, cos_t, sin_s)


def _uq_to_pairs(w):
    Rq = w.shape[0]
    P = w.shape[1] // (2 * (HEAD_DIM + ROPE_DIM))
    w4 = w.reshape(Rq, P, 2, HEAD_DIM + ROPE_DIM)
    nope = w4[..., :HEAD_DIM].reshape(Rq, P, 2 * HEAD_DIM)
    rope = w4[..., HEAD_DIM:].reshape(Rq, P, 2 * ROPE_DIM)
    pad = jnp.zeros((Rq, P, PAIR_Q - 2 * HEAD_DIM - 2 * ROPE_DIM), w.dtype)
    return jnp.concatenate([nope, rope, pad], axis=-1).reshape(Rq, P * PAIR_Q)


def _uq_from_pairs(g):
    Rq = g.shape[0]
    P = g.shape[1] // PAIR_Q
    g3 = g.reshape(Rq, P, PAIR_Q)
    nope = g3[..., :2 * HEAD_DIM].reshape(Rq, P, 2, HEAD_DIM)
    rope = g3[..., 2 * HEAD_DIM:2 * HEAD_DIM + 2 * ROPE_DIM].reshape(Rq, P, 2, ROPE_DIM)
    return jnp.concatenate([nope, rope], axis=-1).reshape(Rq, P * 2 * (HEAD_DIM + ROPE_DIM))


def _ukv_to_pairs(w):
    P = w.shape[1] // (4 * HEAD_DIM)
    w4 = w.reshape(KV_RANK, P, 2, 2 * HEAD_DIM)
    kn = w4[..., :HEAD_DIM].reshape(KV_RANK, P, 2 * HEAD_DIM)
    vv = w4[..., HEAD_DIM:].reshape(KV_RANK, P, 2 * HEAD_DIM)
    top = jnp.concatenate([kn, jnp.zeros((KV_RANK, P, LANES), w.dtype), vv], axis=-1)
    place = np.zeros((LANES, P, PAIR_KV), np.float32)
    for r in range(ROPE_DIM):
        place[r, :, LANES + r] = 1.0
        place[r, :, LANES + ROPE_DIM + r] = 1.0
    return jnp.concatenate([top, jnp.asarray(place, w.dtype)], axis=0).reshape(KV_RANK + LANES, P * PAIR_KV)


def _ukv_from_pairs(g):
    P = g.shape[1] // PAIR_KV
    g3 = g[:KV_RANK].reshape(KV_RANK, P, PAIR_KV)
    kn = g3[..., :2 * HEAD_DIM].reshape(KV_RANK, P, 2, HEAD_DIM)
    vv = g3[..., PAIR_Q:].reshape(KV_RANK, P, 2, HEAD_DIM)
    return jnp.concatenate([kn, vv], axis=-1).reshape(KV_RANK, P * 4 * HEAD_DIM)


def _residual_then_norm(acc, xr, g, gain, sc, sh):
    x_out = xr + g * acc
    r = lax.rsqrt(jnp.mean(x_out * x_out, axis=-1, keepdims=True) + NORM_EPS)
    h = (x_out * r) * gain * (1.0 + sc) + sh
    return x_out, acc, h, h


def _gated_out(a, w_stack, layer, x, gate, next_norm, *, S, name):
    if next_norm is None:
        return _mm(a, w_stack, "nn", name=name, b_layer=layer, out_dtypes=(F32, BF16), extras=(x,), rowvecs=(gate,),
                   seq=S, epilogue=lambda acc, xr, g: (xr + g * acc, acc)) + (None, None)
    return _mm(a, w_stack, "nn", name=name, b_layer=layer, out_dtypes=(F32, BF16, BF16, BF16),
               out_t=(False, False, False, True), extras=(x,), rowvecs=(gate,) + tuple(next_norm), seq=S,
               full_rows=True, epilogue=_residual_then_norm)


def _mlp_fwd(h2, w, i, x1, gate, next_norm, *, S):
    def act(acc):
        u = jnp.square(jnp.maximum(acc, 0.0))
        return acc, u, u

    p, u, u_t = _mm(h2, w["mlp_w1"], "nn", name=f"mlp_up_{i}", b_layer=i, out_dtypes=(BF16, BF16, BF16),
                    out_t=(False, False, True), epilogue=act)
    x2, z, h, h_t = _gated_out(u, w["mlp_w2"], i, x1, gate, next_norm, S=S, name=f"mlp_down_{i}")
    return x2, (p, u_t, z), h, h_t


STACKED_GRADS = ("fox_out", "mla_down", "mla_uq", "mla_ukv", "mla_out", "mlp_w1", "mlp_w2")


def _local_step(x, target, pos_f, inv_freq_row, sign_row, mod, w, slots, *, S, hooks=None):
    hooks = hooks or {}
    T, D = x.shape
    L = mod.shape[0]
    L2 = len(w["fox_out"])
    cos_t, sin_s = _rope_tables(pos_f, inv_freq_row, sign_row)
    saved = []
    B = mod.shape[2]

    def per_sequence(gain):
        return jnp.broadcast_to(gain[None], (B,) + gain.shape)

    h, h_t = _norm_mod(x, w["norm_mix_g"][0], mod[0, 1], mod[0, 0], S=S, name="norm_mix_0")
    for i in range(L):
        j = i // 2
        sh_m, sc_m, g_m, sh_f, sc_f, g_f = (mod[i, s] for s in range(6))
        if i % 2 == 0:
            qkv = _mm(h, w["fox_qkv"], "nn", name=f"fox_qkv_{i}", b_layer=j, out_dtypes=(BF16,))
            fl = _mm(h, w["fox_f"], "nn", name=f"fox_f_{i}", b_layer=j)
            qx, kvx = _fox_prep(qkv, fl, w["fox_b"][j], S=S, D=D, name=f"fox_prep_{i}")
            o, lse, o_t = _attn_fwd(qx, kvx, S=S, ew=FOX_EXTRA, name=f"fox_attn_{i}")
            mix = (qx, kvx, o, lse, o_t, fl)
            w_out = w["fox_out"]
        else:
            lat = _mm(h, w["mla_down"], "nn", name=f"mla_down_{i}", b_layer=j)
            Rq = lat.shape[1] - 2 * LANES
            cqr, cqr_t = _mla_mid(lat, w["mla_gq"][j], w["mla_gkv"][j], cos_t, sin_s, name=f"mla_mid_{i}")
            qx = _mm(cqr, w["mla_uq"], "nn", name=f"mla_uq_{i}", b_layer=j, out_dtypes=(BF16,), a_sz=Rq, tk=Rq,
                     tables=(cos_t, sin_s), epilogue=lambda acc, c, s: (_rope_pairs(acc * MLA_SCALE, c, s, 1.0),))
            kvx = _mm(cqr, w["mla_ukv"], "nn", name=f"mla_ukv_{i}", b_layer=j, out_dtypes=(BF16,), a_off=Rq,
                      a_sz=2 * LANES, tk=2 * LANES, tn=PAIR_KV)
            o, lse, o_t = _attn_fwd(qx, kvx, S=S, ew=ROPE_DIM, name=f"mla_attn_{i}")
            mix = (qx, kvx, o, lse, o_t, lat, cqr_t)
            w_out = w["mla_out"]
        x1, y, h2, h2_t = _gated_out(o, w_out, j, x, g_m, (per_sequence(w["norm_mlp_g"][i]), sc_f, sh_f), S=S,
                                     name=f"mix_out_{i}")
        if i == 0 and "fwd_mlp0" in hooks:
            w = hooks["fwd_mlp0"](x1, w)
        next_norm = (per_sequence(w["norm_mix_g"][i + 1]), mod[i + 1, 1], mod[i + 1, 0]) if i + 1 < L else None
        x2, mlp, h_next, h_next_t = _mlp_fwd(h2, w, i, x1, g_f, next_norm, S=S)
        saved.append((x, h_t, mix, y, x1, h2_t, mlp))
        x, h, h_t = x2, h_next, h_next_t
        if i == 0 and "fwd_layer1" in hooks:
            w = hooks["fwd_layer1"](x, w)

    dx, dg_final, loss = _final_loss(x, target, w["final_norm_g"])
    n_split = w["mlp_w1"][0][0].shape[1]

    grads = {k: [None] * len(w[k]) for k in ("norm_mix_g", "norm_mlp_g", "fox_b", "mla_gq", "mla_gkv")}
    grads["fox_in"] = [None] * L2
    grads.update({k: {} for k in STACKED_GRADS})
    grads["final_norm_g"] = dg_final

    def stacked(key, layer, _, a_t, b, **kw):
        group, idx, count = slots[(key, layer)]
        grads[key][group] = _mm(a_t, b, "nn", out_stack=(grads[key].get(group), idx, count), **kw)

    dmod = [None] * L
    for i in reversed(range(L)):
        j = i // 2
        x0, h_t, mix, y, x1, h2_t, (p, u_t, z) = saved[i]
        sh_m, sc_m, g_m, sh_f, sc_f, g_f = (mod[i, s] for s in range(6))
        if i == 0 and "bwd_layer0" in hooks:
            g_f = g_f + hooks["bwd_layer0"](grads)[0, 0]
        dz, dg_f = _gate_bwd(dx, z, g_f, S=S, name=f"gate_mlp_bwd_{i}")
        stacked("mlp_w2", i, L, u_t, dz, name=f"mlp_w2_grad_{i}")
        dp = _mm(dz, w["mlp_w2"], "nt", name=f"mlp_down_bwd_{i}", b_layer=i, out_dtypes=(BF16,), extras=(p,),
                 epilogue=lambda acc, pv: (acc * (2.0 * jnp.maximum(pv.astype(F32), 0.0)),))
        stacked("mlp_w1", i, L, h2_t, dp, name=f"mlp_w1_grad_{i}", out_split=n_split)
        if i == 0 and "bwd_mix0" in hooks:
            g_m = g_m + hooks["bwd_mix0"](grads)[0, 0]
        dh2 = _mm(dp, w["mlp_w1"], "nt", name=f"mlp_up_bwd_{i}", b_layer=i, out_dtypes=(BF16,))
        dx1, dsh_f, dsc_f, dgn = _norm_mod_bwd(x1, dh2, dx, w["norm_mlp_g"][i], sc_f, S=S, name=f"norm_mlp_bwd_{i}")
        grads["norm_mlp_g"][i] = dgn
        dy, dg_m = _gate_bwd(dx1, y, g_m, S=S, name=f"gate_mix_bwd_{i}")
        if i % 2 == 0:
            qx, kvx, o, lse, o_t, fl = mix
            stacked("fox_out", j, L2, o_t, dy, name=f"fox_out_grad_{i}")
            do = _mm(dy, w["fox_out"], "nt", name=f"fox_out_bwd_{i}", b_layer=j, out_dtypes=(BF16,))
            dqx, dkvx, csum, rsum = _attn_bwd(qx, kvx, o, lse, do, S=S, ew=FOX_EXTRA, name=f"fox_attn_bwd_{i}",
                                              bias_grad=True)
            n_heads = D // HEAD_DIM
            csum = jnp.pad(csum.reshape(T, n_heads, HEAD_DIM)[:, :, 0], ((0, 0), (0, LANES - n_heads)))
            rsum = jnp.transpose(rsum[:, :, :2, :], (0, 3, 1, 2)).reshape(T, n_heads)
            rsum = jnp.pad(rsum, ((0, 0), (0, LANES - n_heads)))
            dproj, db = _fox_unprep(dqx, dkvx, csum, rsum, fl, w["fox_b"][j], S=S, D=D, name=f"fox_unprep_{i}")
            grads["fox_b"][j] = db
            grads["fox_in"][j] = _mm(h_t, dproj, "nn", name=f"fox_in_grad_{i}")
            dh = _mm(dproj, w["fox_in"], "nt", name=f"fox_in_bwd_{i}", b_layer=j, out_dtypes=(BF16,),
                     tk=dproj.shape[1])
        else:
            qx, kvx, o, lse, o_t, lat, cqr_t = mix
            Rq = lat.shape[1] - 2 * LANES
            stacked("mla_out", j, L2, o_t, dy, name=f"mla_out_grad_{i}")
            do = _mm(dy, w["mla_out"], "nt", name=f"mla_out_bwd_{i}", b_layer=j, out_dtypes=(BF16,))
            dqx, dkvx = _attn_bwd(qx, kvx, o, lse, do, S=S, ew=ROPE_DIM, name=f"mla_attn_bwd_{i}")
            dqpre = _unrope(dqx, cos_t, sin_s)
            stacked("mla_uq", j, L2, cqr_t[:Rq], dqpre, name=f"mla_uq_grad_{i}", out_split=n_split)
            stacked("mla_ukv", j, L2, cqr_t[Rq:], dkvx, name=f"mla_ukv_grad_{i}", tn=PAIR_KV, out_split=n_split)
            dcq = _mm(dqpre, w["mla_uq"], "nt", name=f"mla_uq_bwd_{i}", b_layer=j)
            dckr = _mm(dkvx, w["mla_ukv"], "nt", name=f"mla_ukv_bwd_{i}", b_layer=j, tk=PAIR_KV * 2)
            dlat, dgq, dgkv = _mla_mid_bwd(lat, dcq, dckr, w["mla_gq"][j], w["mla_gkv"][j], cos_t, sin_s,
                                           name=f"mla_mid_bwd_{i}")
            grads["mla_gq"][j] = dgq
            grads["mla_gkv"][j] = dgkv
            stacked("mla_down", j, L2, h_t, dlat, name=f"mla_down_grad_{i}")
            dh = _mm(dlat, w["mla_down"], "nt", name=f"mla_down_bwd_{i}", b_layer=j, out_dtypes=(BF16,))
        dx, dsh_m, dsc_m, dgn = _norm_mod_bwd(x0, dh, dx1, w["norm_mix_g"][i], sc_m, S=S, name=f"norm_mix_bwd_{i}")
        grads["norm_mix_g"][i] = dgn
        dmod[i] = jnp.stack([dsh_m, dsc_m, dg_m, dsh_f, dsc_f, dg_f])
    return loss, dx, jnp.stack(dmod), grads


GATHERED = ("fox_in", "fox_out", "mla_down", "mla_uq", "mla_ukv", "mla_out", "mlp_w1", "mlp_w2")
ROW_SHARDED = ("fox_out", "mla_down", "mla_out", "mlp_w2")


def _shard_layouts(wts):
    dkv = wts["mla_w_dkv"]
    dkv = jnp.pad(dkv, ((0, 0), (0, 0), (0, 2 * LANES - dkv.shape[2])))
    return {
        "fox_in": _pad_lanes(wts["fox_w_in"].astype(BF16)),
        "fox_out": wts["fox_w_out"].astype(BF16),
        "mla_down": jnp.concatenate([wts["mla_w_dq"], dkv], axis=2).astype(BF16),
        "mla_uq": jax.vmap(_uq_to_pairs)(wts["mla_w_uq"].astype(BF16)),
        "mla_ukv": jax.vmap(_ukv_to_pairs)(wts["mla_w_ukv"].astype(BF16)),
        "mla_out": wts["mla_w_out"].astype(BF16),
        "mlp_w1": wts["mlp_w1"].astype(BF16),
        "mlp_w2": wts["mlp_w2"].astype(BF16),
    }


def _small_layouts(small):
    return {
        "fox_b": [jnp.pad(b, (0, LANES - b.shape[0]))[None, :] for b in small["fox_b_f"]],
        "mla_gq": [g[None, :] for g in small["mla_q_norm_g"]],
        "mla_gkv": [g[None, :] for g in small["mla_kv_norm_g"]],
        "norm_mix_g": [g[None, :] for g in small["norm_mix_g"]],
        "norm_mlp_g": [g[None, :] for g in small["norm_mlp_g"]],
        "final_norm_g": small["final_norm_g"][None, :],
    }


def _comm_groups(L, L2):
    rest = [("fox_in", 1, L2 - 1), ("fox_out", 1, L2 - 1), ("mla_down", 0, L2), ("mla_uq", 0, L2),
            ("mla_ukv", 0, L2), ("mla_out", 0, L2), ("mlp_w1", 1, L - 1), ("mlp_w2", 1, L - 1)]
    return {"mix0": [("fox_in", 0, 1), ("fox_out", 0, 1)], "mlp0": [("mlp_w1", 0, 1), ("mlp_w2", 0, 1)],
            "rest": [e for e in rest if e[2] > 0]}


def _layer_slots(groups):
    return {(n, s + l): (g, l, cnt) for g, entries in groups.items() for n, s, cnt in entries for l in range(cnt)}


def _pad_lanes(a):
    cols = a.shape[-1]
    return jnp.pad(a, [(0, 0)] * (a.ndim - 1) + [(0, -cols % LANES)])


def _weight_views(name, gathered, D, n_fox_heads):
    n, ns, rows, cols = gathered.shape
    if name == "fox_in":
        true_cols = (3 * D + n_fox_heads) // ns
        fox = jnp.concatenate([gathered[:, k, :, :true_cols] for k in range(ns)], axis=-1)
        return {"fox_qkv": fox[:, :, :3 * D], "fox_f": _pad_lanes(fox[:, :, 3 * D:]), "fox_in": _pad_lanes(fox)}
    if name in ROW_SHARDED:
        return {name: gathered.reshape(n, ns * rows, cols)}
    return {name: gathered}


def _grad_pieces(name, g, qkv_f, n_fox_heads, ns):
    if name == "fox_in":
        D = qkv_f[0].shape[0]
        fox = jnp.stack([a[:, :3 * D + n_fox_heads] for a in qkv_f])
        cols = fox.shape[2] // ns
        return jnp.stack([_pad_lanes(fox[:, :, k * cols:(k + 1) * cols]) for k in range(ns)], axis=1)
    if name in ROW_SHARDED:
        return g.reshape(g.shape[0], ns, g.shape[1] // ns, g.shape[2])
    return g


def _small_grads(g, n_fox_heads):
    return {
        "norm_mix_g": jnp.concatenate(g["norm_mix_g"], axis=0),
        "norm_mlp_g": jnp.concatenate(g["norm_mlp_g"], axis=0),
        "final_norm_g": g["final_norm_g"][0],
        "fox_b_f": jnp.concatenate(g["fox_b"], axis=0)[:, :n_fox_heads],
        "mla_q_norm_g": jnp.concatenate(g["mla_gq"], axis=0),
        "mla_kv_norm_g": jnp.concatenate(g["mla_gkv"], axis=0),
    }


def _silu(c):
    return c * (1.0 / (1.0 + jnp.exp(-c)))


def _ada_fwd(c_all, ada_w, ada_b_cols):
    L, D, C = ada_w.shape
    Bg = c_all.shape[0]
    tc = _tile(C, 512)

    def body(c_ref, w_ref, b_ref, o_ref):
        ca = _silu(c_ref[...]).astype(BF16)
        o_ref[...] = jnp.dot(ca, w_ref[...].astype(BF16), preferred_element_type=F32) + b_ref[...]

    return pl.pallas_call(
        body, name="ada_fwd", grid=(L, C // tc),
        in_specs=[pl.BlockSpec((Bg, D), lambda l, j: (0, 0)), pl.BlockSpec((None, D, tc), lambda l, j: (l, 0, j)),
                  pl.BlockSpec((None, 1, tc), lambda l, j: (l, 0, j))],
        out_specs=pl.BlockSpec((None, Bg, tc), lambda l, j: (l, 0, j)),
        out_shape=jax.ShapeDtypeStruct((L, Bg, C), F32),
        compiler_params=_cparams(("parallel", "parallel")),
    )(c_all, ada_w, ada_b_cols)


def _ada_bwd(c_all, dmod_cols):
    L, Bg, C = dmod_cols.shape
    D = c_all.shape[1]
    tc = _tile(C, 512)

    def body(c_ref, d_ref, o_ref):
        ca = _silu(c_ref[...]).astype(BF16)
        o_ref[...] = _dot_tn(ca, d_ref[...].astype(BF16))

    return pl.pallas_call(
        body, name="ada_bwd", grid=(L, C // tc),
        in_specs=[pl.BlockSpec((Bg, D), lambda l, j: (0, 0)), pl.BlockSpec((None, Bg, tc), lambda l, j: (l, 0, j))],
        out_specs=pl.BlockSpec((None, D, tc), lambda l, j: (l, 0, j)),
        out_shape=jax.ShapeDtypeStruct((L, D, C), F32),
        compiler_params=_cparams(("parallel", "parallel")),
    )(c_all, dmod_cols)


def _adamw_update(w, gv, m, v):
    mn = ADAM_B1 * m + (1.0 - ADAM_B1) * gv
    vn = ADAM_B2 * v + (1.0 - ADAM_B2) * jnp.square(gv)
    m_hat = mn / (1.0 - ADAM_B1 ** ADAM_STEP)
    v_hat = vn / (1.0 - ADAM_B2 ** ADAM_STEP)
    return -ADAM_LR * (m_hat / (jnp.sqrt(v_hat) + ADAM_EPS) + ADAM_WD * w), mn, vn


def _adamw(w, g, m, v, *, name):
    shape = w.shape
    C = shape[-1]
    R = int(np.prod(shape[:-1])) if len(shape) > 1 else 1
    w2, g2, m2, v2 = (a.reshape(R, C) for a in (w, g, m, v))
    tr = _row_tile(R, C)

    def body(w_ref, g_ref, m_ref, v_ref, d_ref, nm_ref, nv_ref):
        d_ref[...], nm_ref[...], nv_ref[...] = _adamw_update(w_ref[...], g_ref[...], m_ref[...], v_ref[...])

    spec = pl.BlockSpec((tr, C), lambda i: (i, 0))
    out = pl.pallas_call(
        body, name=name, grid=(R // tr,), in_specs=[spec] * 4, out_specs=[spec] * 3,
        out_shape=[jax.ShapeDtypeStruct((R, C), F32)] * 3, compiler_params=_cparams(("parallel",)),
    )(w2, g2, m2, v2)
    return tuple(a.reshape(shape) for a in out)


def _adamw_halves(w, g_own, g_peer, m, v, c_idx, *, name):
    L, rows, C = w.shape
    R = rows // 2
    tr = _row_tile(R, C)

    def body(c_ref, w_ref, go_ref, gp_ref, m_ref, v_ref, g_ref, d_ref, nm_ref, nv_ref):
        gv = jnp.where(pl.program_id(1) == c_ref[0], go_ref[...], gp_ref[...])
        g_ref[...] = gv
        d_ref[...], nm_ref[...], nv_ref[...] = _adamw_update(w_ref[...], gv, m_ref[...], v_ref[...])

    full = pl.BlockSpec((None, None, tr, C), lambda l, hh, i, c_ref: (l, hh, i, 0))
    half = pl.BlockSpec((None, tr, C), lambda l, hh, i, c_ref: (l, i, 0))
    grid_spec = pltpu.PrefetchScalarGridSpec(
        num_scalar_prefetch=1, grid=(L, 2, R // tr), in_specs=[full, half, half, full, full], out_specs=[full] * 4)
    split = lambda a: a.reshape(L, 2, R, C)
    out = pl.pallas_call(
        body, name=name, grid_spec=grid_spec, out_shape=[jax.ShapeDtypeStruct((L, 2, R, C), F32)] * 4,
        compiler_params=_cparams(("parallel", "parallel", "parallel")),
    )(c_idx, split(w), g_own, g_peer, split(m), split(v))
    return tuple(a.reshape(w.shape) for a in out)


def _sum_gathered(dm8, sm8):
    n_dev, Bl, R, D = dm8.shape
    Rs = sm8.shape[1]

    def body(dm_ref, sm_ref, ob_ref, os_ref):
        acc_b = jnp.zeros((R, D), F32)
        acc_s = jnp.zeros((Rs, D), F32)
        for d in range(n_dev):
            for b in range(Bl):
                acc_b = acc_b + dm_ref[d, b]
            acc_s = acc_s + sm_ref[d]
        ob_ref[...] = acc_b
        os_ref[...] = acc_s

    return pl.pallas_call(
        body, name="sum_gathered",
        out_shape=[jax.ShapeDtypeStruct((R, D), F32), jax.ShapeDtypeStruct((Rs, D), F32)],
        compiler_params=_cparams(None),
    )(dm8, sm8)


N_DEV = 8
N_CHIP = 4
ANY = pl.BlockSpec(memory_space=pl.ANY)
HBM = pl.BlockSpec(memory_space=pltpu.HBM)
SEM = pl.BlockSpec(memory_space=pltpu.SEMAPHORE)
DATAFLOW = pltpu.SideEffectType.DATAFLOW_SIDE_EFFECTING


def _mesh_pos():
    return lax.axis_index("x"), lax.axis_index("y"), lax.axis_index("c")


def _all_gather8(block, *, name, in_vmem):
    R, W = block.shape

    def body(x_ref, out_ref, send_sems, recv_sems, local_sem):
        x, y, c = _mesh_pos()
        me, sibling = (x, y, c), (x, y, 1 - c)
        chips = [(1 - x, y), (x, 1 - y), (1 - x, 1 - y)]

        def slot(px, py, pc):
            return out_ref.at[4 * px + 2 * py + pc]

        def copy(k, blk, to, src=None):
            return pltpu.make_async_remote_copy(
                src_ref=slot(*blk) if src is None else src, dst_ref=slot(*blk),
                send_sem=send_sems.at[k], recv_sem=recv_sems.at[k], device_id=to, device_id_type=MESH_ID)

        mine = pltpu.make_async_copy(x_ref, slot(*me), local_sem)
        mine.start()
        first = [copy(0, me, sibling, src=x_ref)]
        first += [copy(1 + j, me, (*chip, c), src=x_ref) for j, chip in enumerate(chips)]
        for cp in first:
            cp.start()
        passed = [copy(4 + j, (*chip, c), sibling) for j, chip in enumerate(chips)]
        for j, chip in enumerate(chips):
            copy(1 + j, (*chip, c), me).wait_recv()
            passed[j].start()
        copy(0, sibling, me).wait_recv()
        for j, chip in enumerate(chips):
            copy(4 + j, (*chip, 1 - c), me).wait_recv()
        for cp in first + passed:
            cp.wait_send()
        mine.wait()

    space = pl.BlockSpec(memory_space=pltpu.VMEM) if in_vmem else ANY
    return pl.pallas_call(
        body, name=name, out_shape=jax.ShapeDtypeStruct((N_DEV, R, W), block.dtype),
        in_specs=[space], out_specs=space,
        scratch_shapes=[pltpu.SemaphoreType.DMA((7,)), pltpu.SemaphoreType.DMA((7,)), pltpu.SemaphoreType.DMA],
        compiler_params=pltpu.CompilerParams(vmem_limit_bytes=VMEM_LIMIT_V7X),
    )(block)


def _comm_call(body, arrays, out_shapes, n_sems, *, name):
    return pl.pallas_call(
        body, name=name, out_shape=out_shapes, in_specs=[ANY] * len(arrays), out_specs=[ANY] * len(out_shapes),
        scratch_shapes=[pltpu.SemaphoreType.DMA((n_sems,)), pltpu.SemaphoreType.DMA((n_sems,)),
                        pltpu.SemaphoreType.DMA((len(arrays),))],
    )(*arrays)


def _gather_weights(shards, *, name):
    n = len(shards)

    def body(*refs):
        xs, outs = refs[:n], refs[n:2 * n]
        send_sems, recv_sems, local_sems = refs[2 * n:]
        x, y, c = _mesh_pos()
        me, sibling = (x, y, c), (x, y, 1 - c)
        chips = [(1 - x, y), (x, 1 - y), (1 - x, 1 - y)]
        waits = []
        for i in range(n):
            nl = shards[i].shape[0]
            own = xs[i].at[pl.ds(0, nl), c]

            def slot(px, py, pc, i=i, nl=nl):
                return outs[i].at[pl.ds(0, nl), 2 * px + py, pc]

            def copy(k, blk, to, src=None, i=i, slot=slot):
                return pltpu.make_async_remote_copy(
                    src_ref=slot(*blk) if src is None else src, dst_ref=slot(*blk),
                    send_sem=send_sems.at[7 * i + k], recv_sem=recv_sems.at[7 * i + k], device_id=to,
                    device_id_type=MESH_ID)

            mine = pltpu.make_async_copy(own, slot(*me), local_sems.at[i])
            mine.start()
            first = [copy(0, me, sibling, src=own)]
            first += [copy(1 + j, me, (*chip, c), src=own) for j, chip in enumerate(chips)]
            for cp in first:
                cp.start()
            waits.append((copy, mine, first))
        for copy, mine, first in waits:
            passed = [copy(4 + j, (*chip, c), sibling) for j, chip in enumerate(chips)]
            for j, chip in enumerate(chips):
                copy(1 + j, (*chip, c), me).wait_recv()
                passed[j].start()
            copy(0, sibling, me).wait_recv()
            for j, chip in enumerate(chips):
                copy(4 + j, (*chip, 1 - c), me).wait_recv()
            for cp in first + passed:
                cp.wait_send()
            mine.wait()

    out_shapes = [jax.ShapeDtypeStruct((s.shape[0], N_CHIP) + s.shape[1:], s.dtype) for s in shards]
    return _comm_call(body, shards, out_shapes, 7 * n, name=name)


def _place_own(shard, chip_idx, c_idx, *, name):
    n, _, rows, cols = shard.shape
    tr = _row_tile(rows, cols)

    def body(k_ref, c_ref, x_ref, o_ref):
        o_ref[...] = x_ref[...]

    grid_spec = pltpu.PrefetchScalarGridSpec(
        num_scalar_prefetch=2, grid=(n, rows // tr),
        in_specs=[pl.BlockSpec((None, None, tr, cols), lambda l, i, k_ref, c_ref: (l, c_ref[0], i, 0))],
        out_specs=pl.BlockSpec((None, None, None, tr, cols), lambda l, i, k_ref, c_ref: (l, k_ref[0], c_ref[0], i, 0)))
    return pl.pallas_call(
        body, name=name, grid_spec=grid_spec,
        out_shape=jax.ShapeDtypeStruct((n, N_CHIP, 2, rows, cols), shard.dtype),
        compiler_params=_cparams(("parallel", "parallel")),
    )(chip_idx, c_idx, shard)


def _gather_copies(x_refs, land_refs, send_sems, recv_sems):
    x, y, c = _mesh_pos()
    k_me = 2 * x + y
    targets = [(x, y, 1 - c), (1 - x, y, c), (x, 1 - y, c), (1 - x, 1 - y, c)]
    copies = []
    for i, (x_ref, land_ref) in enumerate(zip(x_refs, land_refs)):
        nl = x_ref.shape[0]
        for j, to in enumerate(targets):
            copies.append(pltpu.make_async_remote_copy(
                src_ref=x_ref.at[pl.ds(0, nl), c], dst_ref=land_ref.at[pl.ds(0, nl), k_me, c],
                send_sem=send_sems.at[4 * i + j], recv_sem=recv_sems.at[4 * i + j], device_id=to,
                device_id_type=MESH_ID))
    return copies


def _split_start(copies_fn, srcs, lands, after, *, name, sems_per_array):
    n = len(srcs)

    def body(*refs):
        send_sems, recv_sems = refs[2 * n + 1], refs[2 * n + 2]
        for cp in copies_fn(refs[:n], refs[n:2 * n], send_sems, recv_sems):
            cp.start()
        refs[-1][...] = jnp.zeros_like(refs[-1])

    operands = [pltpu.with_memory_space_constraint(a, pltpu.HBM) for a in list(srcs) + list(lands)]
    n_sems = sems_per_array * n
    out_shape = ([pltpu.SemaphoreType.DMA((n_sems,)), pltpu.SemaphoreType.DMA((n_sems,))]
                 + [pltpu.HBM(a.shape, a.dtype) for a in operands] + [jax.ShapeDtypeStruct((8, LANES), F32)])
    res = pl.pallas_call(
        body, name=name, out_shape=out_shape, in_specs=[HBM] * (2 * n) + [ANY],
        out_specs=[SEM, SEM] + [HBM] * (2 * n) + [pl.BlockSpec(memory_space=pltpu.VMEM)],
        input_output_aliases={i: 2 + i for i in range(2 * n)},
        compiler_params=pltpu.CompilerParams(has_side_effects=DATAFLOW),
    )(*operands, after)
    return res[0], res[1], list(res[2:2 + n]), list(res[2 + n:2 + 2 * n]), res[-1]


def _split_wait(copies_fn, send_sems, recv_sems, srcs, lands, after, *, name):
    n = len(srcs)

    def body(*refs):
        for cp in copies_fn(refs[:n], refs[n:2 * n], refs[2 * n], refs[2 * n + 1]):
            cp.wait_send()
            cp.wait_recv()

    res = pl.pallas_call(
        body, name=name, out_shape=[pltpu.HBM(a.shape, a.dtype) for a in list(srcs) + list(lands)],
        in_specs=[HBM] * (2 * n) + [SEM, SEM, ANY], out_specs=[HBM] * (2 * n),
        input_output_aliases={i: i for i in range(2 * n)},
        compiler_params=pltpu.CompilerParams(has_side_effects=DATAFLOW),
    )(*srcs, *lands, send_sems, recv_sems, after)
    return list(res[:n]), list(res[n:])


def _gather_forward(lands, *, name):
    n = len(lands)

    def body(*refs):
        xs = refs[:n]
        send_sems, recv_sems, _ = refs[2 * n:]
        x, y, c = _mesh_pos()
        chips = [(1 - x, y), (x, 1 - y), (1 - x, 1 - y)]
        copies = []
        for i in range(n):
            nl = lands[i].shape[0]
            for j, (cx, cy) in enumerate(chips):
                here = xs[i].at[pl.ds(0, nl), 2 * cx + cy, c]
                cp = pltpu.make_async_remote_copy(
                    src_ref=here, dst_ref=here, send_sem=send_sems.at[3 * i + j], recv_sem=recv_sems.at[3 * i + j],
                    device_id=(x, y, 1 - c), device_id_type=MESH_ID)
                cp.start()
                copies.append(cp)
        for cp in copies:
            cp.wait()

    return pl.pallas_call(
        body, name=name, out_shape=[jax.ShapeDtypeStruct(a.shape, a.dtype) for a in lands],
        in_specs=[ANY] * n, out_specs=[ANY] * n, input_output_aliases={i: i for i in range(n)},
        scratch_shapes=[pltpu.SemaphoreType.DMA((3 * n,)), pltpu.SemaphoreType.DMA((3 * n,)),
                        pltpu.SemaphoreType.DMA((1,))],
    )(*lands)


def _pair_copies(g_refs, land_refs, send_sems, recv_sems):
    x, y, c = _mesh_pos()
    copies = []
    for i, (g_ref, land_ref) in enumerate(zip(g_refs, land_refs)):
        nl, ns = g_ref.shape[:2]
        copies.append(pltpu.make_async_remote_copy(
            src_ref=g_ref.at[pl.ds(0, nl), pl.ds(0, ns), 1 - c], dst_ref=land_ref, send_sem=send_sems.at[i],
            recv_sem=recv_sems.at[i], device_id=(x, y, 1 - c), device_id_type=MESH_ID))
    return copies


def _pair_exchange(gs, *, name):
    n = len(gs)

    def body(*refs):
        send_sems, recv_sems, _ = refs[2 * n:]
        copies = _pair_copies(refs[:n], refs[n:2 * n], send_sems, recv_sems)
        for cp in copies:
            cp.start()
        for cp in copies:
            cp.wait()

    out_shapes = [jax.ShapeDtypeStruct(g.shape[:2] + g.shape[3:], g.dtype) for g in gs]
    return _comm_call(body, gs, out_shapes, n, name=name)


def _chip_copies(p_refs, land_refs, send_sems, recv_sems):
    x, y, c = _mesh_pos()
    k_me = 2 * x + y
    chips = [(1 - x, y), (x, 1 - y), (1 - x, 1 - y)]
    copies = []
    for i, (p_ref, land_ref) in enumerate(zip(p_refs, land_refs)):
        nl = p_ref.shape[0]
        for j, (cx, cy) in enumerate(chips):
            copies.append(pltpu.make_async_remote_copy(
                src_ref=p_ref.at[pl.ds(0, nl), 2 * cx + cy], dst_ref=land_ref.at[k_me],
                send_sem=send_sems.at[3 * i + j], recv_sem=recv_sems.at[3 * i + j],
                device_id=(cx, cy, c), device_id_type=MESH_ID))
    return copies


def _chip_landing(ps):
    return [lax.empty((p.shape[1], p.shape[0]) + p.shape[2:], p.dtype) for p in ps]


def _pair_swap(ss, *, name):
    n = len(ss)

    def body(*refs):
        xs, outs = refs[:n], refs[n:2 * n]
        send_sems, recv_sems, _ = refs[2 * n:]
        x, y, c = _mesh_pos()
        copies = []
        for i in range(n):
            cp = pltpu.make_async_remote_copy(src_ref=xs[i], dst_ref=outs[i], send_sem=send_sems.at[i],
                                              recv_sem=recv_sems.at[i], device_id=(x, y, 1 - c),
                                              device_id_type=MESH_ID)
            cp.start()
            copies.append(cp)
        for cp in copies:
            cp.wait()

    out_shapes = [jax.ShapeDtypeStruct(s.shape, s.dtype) for s in ss]
    return _comm_call(body, ss, out_shapes, n, name=name)


def _row_tile(rows, cols):
    tr = rows
    while tr * cols > 256 * 1024 and tr % 16 == 0:
        tr //= 2
    return tr


def _pair_add(g, recv, c_idx, *, name):
    n, ns, _, rows, W = g.shape
    tr = _row_tile(rows, W)

    def body(c_ref, g_ref, r_ref, o_ref):
        o_ref[...] = (g_ref[...] + r_ref[...]).astype(BF16)

    piece = pl.BlockSpec((None, tr, W), lambda p, i, c_ref: (p, i, 0))
    grid_spec = pltpu.PrefetchScalarGridSpec(
        num_scalar_prefetch=1, grid=(n * ns, rows // tr),
        in_specs=[pl.BlockSpec((None, None, tr, W), lambda p, i, c_ref: (p, c_ref[0], i, 0)), piece],
        out_specs=piece)
    out = pl.pallas_call(
        body, name=name, grid_spec=grid_spec, out_shape=jax.ShapeDtypeStruct((n * ns, rows, W), BF16),
        compiler_params=_cparams(("parallel", "parallel")),
    )(c_idx, g.reshape(n * ns, 2, rows, W), recv.reshape(n * ns, rows, W))
    return out.reshape(n, ns, rows, W)


def _sum_pieces(land, own, chip_idx, *, name):
    n, nl, A, W = land.shape
    tr = _row_tile(A, W)

    def body(k_ref, l_ref, o_ref, out_ref):
        acc = jnp.zeros(out_ref.shape, F32)
        for k in range(n):
            acc = acc + jnp.where(k == k_ref[0], o_ref[...], l_ref[k]).astype(F32)
        out_ref[...] = acc

    grid_spec = pltpu.PrefetchScalarGridSpec(
        num_scalar_prefetch=1, grid=(nl, A // tr),
        in_specs=[pl.BlockSpec((n, None, tr, W), lambda l, i, k_ref: (0, l, i, 0)),
                  pl.BlockSpec((None, None, tr, W), lambda l, i, k_ref: (l, k_ref[0], i, 0))],
        out_specs=pl.BlockSpec((None, tr, W), lambda l, i, k_ref: (l, i, 0)))
    return pl.pallas_call(
        body, name=name, grid_spec=grid_spec, out_shape=jax.ShapeDtypeStruct((nl, A, W), F32),
        compiler_params=_cparams(("parallel", "parallel")),
    )(chip_idx, land, own)


SMALL = ("norm_mix_g", "norm_mlp_g", "final_norm_g", "fox_b_f", "mla_q_norm_g", "mla_kv_norm_g")
WEIGHT_ORDER = ("ada_w", "ada_b", "norm_mix_g", "norm_mlp_g", "fox_w_in", "fox_b_f", "fox_w_out", "mla_w_dq",
                "mla_q_norm_g", "mla_w_uq", "mla_w_dkv", "mla_kv_norm_g", "mla_w_ukv", "mla_w_out", "mlp_w1",
                "mlp_w2", "final_norm_g")


def _small_rows(vals, D):
    rows = [vals["norm_mix_g"], vals["norm_mlp_g"], vals["final_norm_g"][None, :]]
    for n in ("fox_b_f", "mla_q_norm_g", "mla_kv_norm_g"):
        flat = vals[n].reshape(-1)
        assert flat.shape[0] <= D
        rows.append(jnp.pad(flat, (0, D - flat.shape[0]))[None, :])
    return jnp.concatenate(rows, axis=0)


def _small_unrows(rows, shapes):
    L = shapes["norm_mix_g"][0]
    out = {"norm_mix_g": rows[0:L], "norm_mlp_g": rows[L:2 * L], "final_norm_g": rows[2 * L]}
    for k, n in enumerate(("fox_b_f", "mla_q_norm_g", "mla_kv_norm_g")):
        size = int(np.prod(shapes[n]))
        out[n] = rows[2 * L + 1 + k, :size].reshape(shapes[n])
    return out


def kernel(x, c, positions, ada_w, ada_b, norm_mix_g, norm_mlp_g, fox_w_in, fox_b_f, fox_w_out, mla_w_dq, mla_q_norm_g, mla_w_uq, mla_w_dkv, mla_kv_norm_g, mla_w_ukv, mla_w_out, mlp_w1, mlp_w2, final_norm_g, loss_target, m_ada_w, m_ada_b, m_norm_mix_g, m_norm_mlp_g, m_fox_w_in, m_fox_b_f, m_fox_w_out, m_mla_w_dq, m_mla_q_norm_g, m_mla_w_uq, m_mla_w_dkv, m_mla_kv_norm_g, m_mla_w_ukv, m_mla_w_out, m_mlp_w1, m_mlp_w2, m_final_norm_g, v_ada_w, v_ada_b, v_norm_mix_g, v_norm_mlp_g, v_fox_w_in, v_fox_b_f, v_fox_w_out, v_mla_w_dq, v_mla_q_norm_g, v_mla_w_uq, v_mla_w_dkv, v_mla_kv_norm_g, v_mla_w_ukv, v_mla_w_out, v_mlp_w1, v_mlp_w2, v_final_norm_g):
    args = dict(locals())
    wts = {n: args[n] for n in WEIGHT_ORDER}
    mom = {n: args["m_" + n] for n in WEIGHT_ORDER}
    var = {n: args["v_" + n] for n in WEIGHT_ORDER}
    Bl, S, D = x.shape
    T = Bl * S
    L = ada_w.shape[0]
    C = ada_w.shape[2]
    mx, my, mc = _mesh_pos()
    chip = 2 * mx + my
    dev = 4 * mx + 2 * my + mc
    c_idx = jnp.reshape(mc, (1,)).astype(jnp.int32)
    chip_idx = jnp.reshape(chip, (1,)).astype(jnp.int32)
    small = {n: wts[n] for n in SMALL}
    L2, q_cols = mla_q_norm_g.shape
    n_fox_heads = fox_b_f.shape[1]

    shards = _shard_layouts(wts)
    groups = _comm_groups(L, L2)
    slots = _layer_slots(groups)

    def row_halves(a):
        return a.reshape(a.shape[:-2] + (2, a.shape[-2] // 2, a.shape[-1]))

    def whole_rows(a):
        return a.reshape(a.shape[:2] + (a.shape[2] * a.shape[3], a.shape[4]))

    part = {g: [row_halves(shards[n][s:s + cnt]) for n, s, cnt in entries] for g, entries in groups.items()}
    mix0 = _gather_weights(part["mix0"], name="gather_mix0")
    gather_sems, after = {}, mix0[0]
    for group in ("mlp0", "rest"):
        placed = [_place_own(a, chip_idx, c_idx, name=f"gather_place_{group}_{n}")
                  for a, (n, _, _) in zip(part[group], groups[group])]
        gather_sems[group] = _split_start(_gather_copies, part[group], placed, after, name=f"gather_{group}_start",
                                          sems_per_array=4)
        after = gather_sems[group][4]

    def layer_weights(w, group, arrays):
        for (n, s, cnt), a in zip(groups[group], arrays):
            for key, view in _weight_views(n, whole_rows(a), D, n_fox_heads).items():
                for l in range(cnt):
                    w[key][s + l] = (view, l)

    w = {key: [None] * L2
         for key in ("fox_qkv", "fox_f", "fox_in", "fox_out", "mla_down", "mla_uq", "mla_ukv", "mla_out")}
    w.update({key: [None] * L for key in ("mlp_w1", "mlp_w2")})
    layer_weights(w, "mix0", mix0)

    def gathered_now(group):
        def hook(x_now, w):
            _, landed = _split_wait(_gather_copies, *gather_sems[group][:4], x_now, name=f"gather_{group}_wait")
            layer_weights(w, group, _gather_forward(landed, name=f"gather_{group}_forward"))
            return w
        return hook

    c_pad = jnp.concatenate([c, jnp.pad(mla_q_norm_g, ((0, 8 - Bl - L2), (0, D - q_cols)))], axis=0)
    c8 = _all_gather8(c_pad, name="gather_c", in_vmem=True)
    c_all = c8[:, :Bl].reshape(N_DEV * Bl, D)
    qg4 = c8.reshape(N_CHIP, 2, 8, D)[:, 0, Bl:Bl + L2, :q_cols]
    small["mla_q_norm_g"] = jnp.transpose(qg4, (1, 0, 2)).reshape(L2, N_CHIP * q_cols)
    ada_b_cols = lax.dynamic_slice_in_dim(ada_b, chip * C, C, axis=1)[:, None, :]
    mod_cols = _ada_fwd(c_all, ada_w, ada_b_cols)
    mod8 = _all_gather8(mod_cols.reshape(L * N_DEV * Bl, C), name="gather_mod", in_vmem=True)
    mod4 = mod8.reshape(N_CHIP, 2, L, N_DEV * Bl, C)[:, 0]
    mod_me = lax.dynamic_slice_in_dim(mod4, dev * Bl, Bl, axis=2)
    mod = jnp.transpose(mod_me, (1, 2, 0, 3)).reshape(L, Bl, 6, D)
    mod = jnp.transpose(mod, (0, 2, 1, 3))[:, :, :, None, :]

    w.update(_small_layouts(small))
    mod = mod + after[0, 0]
    pending = {}

    def grad_pieces(group, g_now):
        out = []
        for n, s, cnt in groups[group]:
            qkv_f = [g_now["fox_in"][j] for j in range(s, s + cnt)] if n == "fox_in" else None
            stacked_g = None if n == "fox_in" else g_now[n][group]
            out.append(row_halves(_grad_pieces(n, stacked_g, qkv_f, n_fox_heads, N_CHIP)))
        return out

    def pair_added(group, big, sibling):
        return [_pair_add(a, r, c_idx, name=f"grad_pair_add_{group}_{n}")
                for (n, _, _), a, r in zip(groups[group], big, sibling)]

    def exchange_start(group, ps, after=None):
        pending[group] = _split_start(_chip_copies, ps, _chip_landing(ps), chip_idx if after is None else after,
                                      name=f"grad_exchange_{group}_start", sems_per_array=3)
        return pending[group][4]

    def bwd_layer0(g_now):
        big = grad_pieces("rest", g_now)
        landing = [lax.empty(a.shape[:2] + a.shape[3:], a.dtype) for a in big]
        pending["rest_pair"] = _split_start(_pair_copies, big, landing, chip_idx, name="grad_pair_rest_start",
                                            sems_per_array=1)
        return pending["rest_pair"][4]

    def bwd_mix0(g_now):
        send_sems, recv_sems, big, landed, _ = pending["rest_pair"]
        big, landed = _split_wait(_pair_copies, send_sems, recv_sems, big, landed, g_now["mlp_w1"]["mlp0"],
                                  name="grad_pair_rest_wait")
        started = exchange_start("rest", pair_added("rest", big, landed))
        big = grad_pieces("mlp0", g_now)
        return exchange_start("mlp0", pair_added("mlp0", big, _pair_exchange(big, name="grad_pair_exchange_mlp0")),
                              after=started)

    half = ROPE_DIM // 2
    inv_freq = ROPE_THETA ** (-jnp.arange(0, ROPE_DIM, 2, dtype=F32) / ROPE_DIM)
    lane = np.arange(LANES)
    inv_freq_row = jnp.tile(inv_freq, LANES // half)[None, :]
    sign_row = jnp.asarray(np.where(lane < 2 * ROPE_DIM, np.where(lane % ROPE_DIM < half, -1.0, 1.0), 0.0), F32)[None, :]
    pos_f = positions.astype(F32).reshape(T, 1)
    loss_row, grad_x, dmod, g = _local_step(x.reshape(T, D), loss_target.reshape(T, D), pos_f, inv_freq_row, sign_row,
                                            mod, w, slots, S=S,
                                            hooks={"fwd_mlp0": gathered_now("mlp0"), "fwd_layer1": gathered_now("rest"),
                                                   "bwd_layer0": bwd_layer0, "bwd_mix0": bwd_mix0})
    g_small = _small_grads(g, n_fox_heads)
    big = grad_pieces("mix0", g)
    exchange_start("mix0", pair_added("mix0", big, _pair_exchange(big, name="grad_pair_exchange_mix0")))

    Rs = -(-(2 * L + 5) // 8) * 8
    srows = jnp.concatenate([_small_rows(g_small, D), jnp.pad(loss_row, ((0, 0), (0, D - LANES)))], axis=0)
    srows = jnp.pad(srows, ((0, Rs - srows.shape[0]), (0, 0)))
    drows = jnp.transpose(dmod[:, :, :, 0, :], (2, 0, 1, 3)).reshape(Bl * L * 6, D)
    both8 = _all_gather8(jnp.concatenate([drows, srows], axis=0), name="gather_small", in_vmem=True)
    dm8 = both8[:, :Bl * L * 6].reshape(N_DEV, Bl, L * 6, D)
    sm8 = both8[:, Bl * L * 6:]
    adb_rows, small_sum = _sum_gathered(dm8, sm8)
    grad_ada_b = adb_rows.reshape(L, 6 * D)
    loss = small_sum[2 * L + 4, 0]
    small_shapes = {n: (wts[n].shape if n != "mla_q_norm_g" else (wts[n].shape[0], N_CHIP * q_cols)) for n in SMALL}
    gs = _small_unrows(small_sum, small_shapes)
    gs["mla_q_norm_g"] = lax.dynamic_slice_in_dim(gs["mla_q_norm_g"], chip * q_cols, q_cols, axis=1)

    dmod16 = jnp.transpose(dm8.reshape(N_DEV, Bl, L, 6 * D), (2, 0, 1, 3)).reshape(L, N_DEV * Bl, 6 * D)
    dmod_cols = lax.dynamic_slice_in_dim(dmod16, chip * C, C, axis=2)
    grad_ada_w = _ada_bwd(c_all, dmod_cols)

    grads = dict(gs)
    grads["ada_w"] = grad_ada_w
    grads["ada_b"] = grad_ada_b
    delta, new_m, new_v = {}, {}, {}
    for n in ("ada_w", "ada_b"):
        delta[n], new_m[n], new_v[n] = _adamw(wts[n], grads[n], mom[n], var[n], name=f"adamw_{n}")
    shard_small_shapes = {n: wts[n].shape for n in SMALL}
    packs = [jnp.pad(_small_rows({n: src[n] for n in SMALL}, D), ((0, Rs - 2 * L - 4), (0, 0)))
             for src in (wts, grads, mom, var)]
    for dst, rows in zip((delta, new_m, new_v), _adamw(*packs, name="adamw_small")):
        dst.update(_small_unrows(rows, shard_small_shapes))

    halves = {}
    for group, after in (("rest", grad_x), ("mlp0", grad_x), ("mix0", delta["ada_w"])):
        send_sems, recv_sems, ps, lands, _ = pending[group]
        ps, lands = _split_wait(_chip_copies, send_sems, recv_sems, ps, lands, after, name=f"grad_exchange_{group}_wait")
        sums = [_sum_pieces(ld, p, chip_idx, name=f"grad_sum_{group}_{n}")
                for (n, _, _), ld, p in zip(groups[group], lands, ps)]
        swapped = _pair_swap(sums, name=f"grad_pair_swap_{group}")
        for (n, _, _), a, b in zip(groups[group], sums, swapped):
            halves[(n, group)] = (a, b)

    def all_layers(n, which):
        return jnp.concatenate([halves[(n, grp)][which] for grp in groups if (n, grp) in halves], axis=0)

    own = {n: all_layers(n, 0) for n in GATHERED}
    peer = {n: all_layers(n, 1) for n in GATHERED}
    for nat, n in (("fox_w_in", "fox_in"), ("fox_w_out", "fox_out"), ("mla_w_out", "mla_out"), ("mlp_w1", "mlp_w1"),
                   ("mlp_w2", "mlp_w2")):
        cols = wts[nat].shape[-1]
        res = _adamw_halves(_pad_lanes(wts[nat]), own[n], peer[n], _pad_lanes(mom[nat]), _pad_lanes(var[nat]), c_idx,
                            name=f"adamw_{nat}")
        grads[nat], delta[nat], new_m[nat], new_v[nat] = (a[..., :cols] for a in res)
    joined = {n: jnp.concatenate([jnp.where(mc == 0, own[n], peer[n]), jnp.where(mc == 0, peer[n], own[n])], axis=1)
              for n in ("mla_down", "mla_uq", "mla_ukv")}
    rq = mla_w_dq.shape[-1]
    grads["mla_w_dq"] = joined["mla_down"][:, :, :rq]
    grads["mla_w_dkv"] = joined["mla_down"][:, :, rq:rq + KV_RANK + ROPE_DIM]
    grads["mla_w_uq"] = jax.vmap(_uq_from_pairs)(joined["mla_uq"])
    grads["mla_w_ukv"] = jax.vmap(_ukv_from_pairs)(joined["mla_ukv"])
    for n in ("mla_w_dq", "mla_w_dkv", "mla_w_uq", "mla_w_ukv"):
        delta[n], new_m[n], new_v[n] = _adamw(wts[n], grads[n], mom[n], var[n], name=f"adamw_{n}")

    return (loss, grad_x.reshape(Bl, S, D), *[grads[n] for n in WEIGHT_ORDER], *[delta[n] for n in WEIGHT_ORDER],
            *[new_m[n] for n in WEIGHT_ORDER], *[new_v[n] for n in WEIGHT_ORDER])
```

```python
import numpy as np
import jax
import jax.numpy as jnp
from jax import lax
from jax.experimental import pallas as pl
from jax.experimental.pallas import tpu as pltpu

F32 = jnp.float32
BF16 = jnp.bfloat16
MESH_ID = pl.DeviceIdType.MESH

NORM_EPS = 1e-6
ROPE_THETA = 10000.0
HEAD_DIM = 64
ROPE_DIM = 32
KV_RANK = 128
MLA_SCALE = (HEAD_DIM + ROPE_DIM) ** -0.5
FOX_EXTRA = 6
PAIR_Q = 256
PAIR_KV = 384
LANES = 128
ADAM_LR = 0.001
ADAM_B1 = 0.9
ADAM_B2 = 0.999
ADAM_EPS = 1e-08
ADAM_WD = 0.01
ADAM_STEP = 10
VMEM_LIMIT_V7X = 56 * 1024 * 1024
MM_VMEM_BUDGET = 44 * 1024 * 1024
NEG_BIG = -1e30
ATTN_UNROLL = 4
ATTN_BLOCK = 256
ATTN_Q_ROWS = 512
ATTN_K_ROWS = 512

BIG_WEIGHTS = (("fox_w_in", 2), ("fox_w_out", 1), ("mla_w_dq", 1), ("mla_w_uq", 2), ("mla_w_dkv", 1),
               ("mla_w_ukv", 2), ("mla_w_out", 1), ("mlp_w1", 2), ("mlp_w2", 1))


def _cparams(sem=None):
    return pltpu.CompilerParams(dimension_semantics=sem, vmem_limit_bytes=VMEM_LIMIT_V7X)


def _tile(n, want):
    if n <= want:
        return n
    for t in range(want - want % LANES, 0, -LANES):
        if n % t == 0:
            return t
    raise ValueError((n, want))


def _mm(a, b, mode, *, name, out_dtypes=(F32,), epilogue=None, extras=(), rowvecs=(), tables=(),
        seq=None, a_off=0, a_sz=None, b_layer=None, out_stack=None, out_split=0, out_t=(), full_rows=False,
        tm=1024, tn=1024, tk=2048):
    if isinstance(b, (list, tuple)):
        b, b_layer = b[b_layer]
    b_rows, b_cols = b.shape[-2], b.shape[-1]
    n_split = b.shape[1] if b.ndim == 4 else 1
    assert mode in ("nn", "nt")
    if mode == "nn":
        M, K, N = a.shape[0], b_rows, b_cols * n_split
    else:
        M, K, N = a.shape[0], b_cols * n_split, b_rows
    assert a_sz is None or a_sz == K
    tm = _tile(seq if rowvecs else M, tm)
    n_piece = N // max(out_split, n_split if mode == "nn" else 1, 1)
    tn = _tile(n_piece, tn)
    tk = _tile(K // (n_split if mode == "nt" else 1), tk)
    ne, nr, nt_ = len(extras), len(rowvecs), len(tables)
    no = len(out_dtypes)

    def vmem_estimate():
        blocks = tm * tk * a.dtype.itemsize + tk * tn * b.dtype.itemsize
        blocks += tm * tn * (sum(e.dtype.itemsize for e in extras) + sum(jnp.dtype(d).itemsize for d in out_dtypes))
        return 2 * blocks + 2 * tm * tn * 4

    if full_rows:
        assert tn == N
    while vmem_estimate() > MM_VMEM_BUDGET and max(tm, tn) > 256:
        if tn >= tm and not full_rows:
            tn //= 2
        else:
            tm //= 2
    nk = K // tk

    assert a_off % tk == 0
    a_spec = pl.BlockSpec((tm, tk), lambda i, j, k: (i, k + a_off // tk))
    dims = (((1,), (0,)), ((), ())) if mode == "nn" else (((1,), (1,)), ((), ()))
    lead = () if b.ndim == 2 else (b_layer,)
    sq = (None,) * (b.ndim - 2)
    if mode == "nt":
        kb = b_cols // tk
        if b.ndim == 4:
            b_spec = pl.BlockSpec(sq + (tn, tk), lambda i, j, k: lead + (k // kb, j, k % kb))
        else:
            b_spec = pl.BlockSpec(sq + (tn, tk), lambda i, j, k: lead + (j, k))
    else:
        nb = b_cols // tn
        if b.ndim == 4:
            b_spec = pl.BlockSpec(sq + (tk, tn), lambda i, j, k: lead + (j // nb, k, j % nb))
        else:
            b_spec = pl.BlockSpec(sq + (tk, tn), lambda i, j, k: lead + (k, j))
    in_specs = [a_spec, b_spec]
    in_specs += [pl.BlockSpec((tm, tn), lambda i, j, k: (i, j)) for _ in extras]
    if rowvecs:
        assert seq % tm == 0
        per = seq // tm
        in_specs += [pl.BlockSpec((None, 1, tn), lambda i, j, k: (i // per, 0, j)) for _ in rowvecs]
    in_specs += [pl.BlockSpec((tm, LANES), lambda i, j, k: (i, 0)) for _ in tables]
    operands = [a, b, *extras, *rowvecs, *tables]
    aliases = {}
    transposed = tuple(out_t) + (False,) * (no - len(out_t))
    if out_stack is None:
        out_specs = [pl.BlockSpec((tn, tm), lambda i, j, k: (j, i)) if t else pl.BlockSpec((tm, tn), lambda i, j, k: (i, j))
                     for t in transposed]
        out_shape = [jax.ShapeDtypeStruct((N, M) if t else (M, N), d) for d, t in zip(out_dtypes, transposed)]
    else:
        prev, layer, n_layers = out_stack
        assert no == 1
        if out_split:
            ob = n_piece // tn
            out_specs = [pl.BlockSpec((None, None, tm, tn), lambda i, j, k: (layer, j // ob, i, j % ob))]
            out_shape = [jax.ShapeDtypeStruct((n_layers, out_split, M, n_piece), out_dtypes[0])]
        else:
            out_specs = [pl.BlockSpec((None, tm, tn), lambda i, j, k: (layer, i, j))]
            out_shape = [jax.ShapeDtypeStruct((n_layers, M, N), out_dtypes[0])]
        if prev is not None:
            in_specs.append(pl.BlockSpec(memory_space=pl.ANY))
            aliases = {len(operands): 0}
            operands.append(prev)
    n_in = len(operands)

    def body(*refs):
        a_ref, b_ref = refs[0], refs[1]
        side = refs[2:2 + ne + nr + nt_]
        outs = refs[n_in:n_in + no]

        def finish(acc):
            res = (acc,) if epilogue is None else epilogue(acc, *[r[...] for r in side])
            for o_ref, r, t in zip(outs, res, transposed):
                o_ref[...] = (r.T if t else r).astype(o_ref.dtype)

        part = lax.dot_general(a_ref[...].astype(BF16), b_ref[...].astype(BF16), dims,
                               preferred_element_type=F32)
        if nk == 1:
            finish(part)
        else:
            acc_ref = refs[-1]
            k = pl.program_id(2)

            @pl.when(k == 0)
            def _():
                acc_ref[...] = part

            @pl.when(k > 0)
            def _():
                acc_ref[...] += part

            @pl.when(k == nk - 1)
            def _():
                finish(acc_ref[...])

    res = pl.pallas_call(
        body, name=name, grid=(M // tm, N // tn, nk), in_specs=in_specs, out_specs=out_specs,
        out_shape=out_shape, scratch_shapes=[pltpu.VMEM((tm, tn), F32)] if nk > 1 else [],
        input_output_aliases=aliases,
        compiler_params=_cparams(("parallel", "parallel", "arbitrary")),
    )(*operands)
    return res[0] if no == 1 else tuple(res)


def _rope128(x, cos_t, sin_s):
    lane = lax.broadcasted_iota(jnp.int32, x.shape, 1)
    first = (lane % ROPE_DIM) < (ROPE_DIM // 2)
    swapped = jnp.where(first, pltpu.roll(x, LANES - ROPE_DIM // 2, 1), pltpu.roll(x, ROPE_DIM // 2, 1))
    return x * cos_t + swapped * sin_s


def _rope_pairs(acc, cos_t, sin_s, sign):
    parts = []
    for p in range(acc.shape[1] // PAIR_Q):
        parts.append(acc[:, p * PAIR_Q:p * PAIR_Q + LANES])
        parts.append(_rope128(acc[:, p * PAIR_Q + LANES:(p + 1) * PAIR_Q], cos_t, sign * sin_s))
    return jnp.concatenate(parts, axis=1)


def _rope_tables(pos_f, inv_freq_row, sign_row):
    T = pos_f.shape[0]
    tt = _tile(T, 512)

    def body(p_ref, f_ref, s_ref, cos_ref, sin_ref):
        ang = p_ref[...] * f_ref[...]
        cos_ref[...] = jnp.cos(ang)
        sin_ref[...] = jnp.sin(ang) * s_ref[...]

    return pl.pallas_call(
        body, name="rope_tables", grid=(T // tt,),
        in_specs=[pl.BlockSpec((tt, 1), lambda i: (i, 0)), pl.BlockSpec((1, LANES), lambda i: (0, 0)),
                  pl.BlockSpec((1, LANES), lambda i: (0, 0))],
        out_specs=[pl.BlockSpec((tt, LANES), lambda i: (i, 0))] * 2,
        out_shape=[jax.ShapeDtypeStruct((T, LANES), F32)] * 2,
        compiler_params=_cparams(("parallel",)),
    )(pos_f, inv_freq_row, sign_row)


def _unrope(dqx, cos_t, sin_s):
    T, W = dqx.shape
    tt = _tile(T, 512)

    def body(d_ref, c_ref, s_ref, o_ref):
        o_ref[...] = _rope_pairs(d_ref[...].astype(F32) * MLA_SCALE, c_ref[...], s_ref[...], -1.0).astype(BF16)

    return pl.pallas_call(
        body, name="mla_unrope", grid=(T // tt,),
        in_specs=[pl.BlockSpec((tt, W), lambda i: (i, 0)), pl.BlockSpec((tt, LANES), lambda i: (i, 0)),
                  pl.BlockSpec((tt, LANES), lambda i: (i, 0))],
        out_specs=pl.BlockSpec((tt, W), lambda i: (i, 0)),
        out_shape=jax.ShapeDtypeStruct((T, W), BF16),
        compiler_params=_cparams(("parallel",)),
    )(dqx, cos_t, sin_s)


def _row_specs(tt, D, per, n):
    return [pl.BlockSpec((None, 1, D), lambda i: (i // per, 0, 0)) for _ in range(n)]


def _norm_mod(x, gain, sc, sh, *, S, name):
    T, D = x.shape
    tt = _tile(S, 512)
    per = S // tt

    def body(x_ref, g_ref, sc_ref, sh_ref, h_ref, ht_ref):
        xv = x_ref[...]
        r = lax.rsqrt(jnp.mean(xv * xv, axis=-1, keepdims=True) + NORM_EPS)
        h = (xv * r) * g_ref[...] * (1.0 + sc_ref[...]) + sh_ref[...]
        h_ref[...] = h.astype(BF16)
        ht_ref[...] = h.T.astype(BF16)

    return pl.pallas_call(
        body, name=name, grid=(T // tt,),
        in_specs=[pl.BlockSpec((tt, D), lambda i: (i, 0)), pl.BlockSpec((1, D), lambda i: (0, 0))]
        + _row_specs(tt, D, per, 2),
        out_specs=[pl.BlockSpec((tt, D), lambda i: (i, 0)), pl.BlockSpec((D, tt), lambda i: (0, i))],
        out_shape=[jax.ShapeDtypeStruct((T, D), BF16), jax.ShapeDtypeStruct((D, T), BF16)],
        compiler_params=_cparams(("parallel",)),
    )(x, gain, sc, sh)


def _norm_mod_bwd(x, dh, dres, gain, sc, *, S, name):
    T, D = x.shape
    B = T // S
    tt = _tile(S, 512)
    per = S // tt

    def body(x_ref, dh_ref, dres_ref, g_ref, sc_ref, dx_ref, dsh_ref, dsc_ref, dg_ref):
        i = pl.program_id(0)
        xv = x_ref[...]
        dhv = dh_ref[...].astype(F32)
        r = lax.rsqrt(jnp.mean(xv * xv, axis=-1, keepdims=True) + NORM_EPS)
        n = xv * r
        g = g_ref[...]
        one_sc = 1.0 + sc_ref[...]
        dn = dhv * (g * one_sc)
        dx_ref[...] = dres_ref[...] + r * (dn - n * jnp.mean(dn * n, axis=-1, keepdims=True))
        dhn = dhv * n

        @pl.when(i % per == 0)
        def _():
            dsh_ref[...] = jnp.zeros_like(dsh_ref)
            dsc_ref[...] = jnp.zeros_like(dsc_ref)

        @pl.when(i == 0)
        def _():
            dg_ref[...] = jnp.zeros_like(dg_ref)

        dsh_ref[...] += jnp.sum(dhv, axis=0, keepdims=True)
        dsc_ref[...] += jnp.sum(dhn, axis=0, keepdims=True) * g
        dg_ref[...] += jnp.sum(dhn, axis=0, keepdims=True) * one_sc

    return pl.pallas_call(
        body, name=name, grid=(T // tt,),
        in_specs=[pl.BlockSpec((tt, D), lambda i: (i, 0))] * 3 + [pl.BlockSpec((1, D), lambda i: (0, 0))]
        + _row_specs(tt, D, per, 1),
        out_specs=[pl.BlockSpec((tt, D), lambda i: (i, 0))] + _row_specs(tt, D, per, 2)
        + [pl.BlockSpec((1, D), lambda i: (0, 0))],
        out_shape=[jax.ShapeDtypeStruct((T, D), F32), jax.ShapeDtypeStruct((B, 1, D), F32),
                   jax.ShapeDtypeStruct((B, 1, D), F32), jax.ShapeDtypeStruct((1, D), F32)],
        compiler_params=_cparams(("arbitrary",)),
    )(x, dh, dres, gain, sc)


def _gate_bwd(dx, y, g, *, S, name):
    T, D = dx.shape
    B = T // S
    tt = _tile(S, 512)
    per = S // tt

    def body(dx_ref, y_ref, g_ref, dy_ref, dg_ref):
        i = pl.program_id(0)
        dxv = dx_ref[...]
        dy_ref[...] = (dxv * g_ref[...]).astype(BF16)

        @pl.when(i % per == 0)
        def _():
            dg_ref[...] = jnp.zeros_like(dg_ref)

        dg_ref[...] += jnp.sum(dxv * y_ref[...], axis=0, keepdims=True)

    return pl.pallas_call(
        body, name=name, grid=(T // tt,),
        in_specs=[pl.BlockSpec((tt, D), lambda i: (i, 0))] * 2 + _row_specs(tt, D, per, 1),
        out_specs=[pl.BlockSpec((tt, D), lambda i: (i, 0))] + _row_specs(tt, D, per, 1),
        out_shape=[jax.ShapeDtypeStruct((T, D), BF16), jax.ShapeDtypeStruct((B, 1, D), F32)],
        compiler_params=_cparams(("arbitrary",)),
    )(dx, y, g)


def _final_loss(x, target, gain):
    T, D = x.shape
    tt = _tile(T, 512)

    def body(x_ref, t_ref, g_ref, dx_ref, dg_ref, loss_ref):
        i = pl.program_id(0)
        xv = x_ref[...]
        r = lax.rsqrt(jnp.mean(xv * xv, axis=-1, keepdims=True) + NORM_EPS)
        n = xv * r
        g = g_ref[...]
        err = n * g - t_ref[...]
        dy = err * (1.0 / D)
        dn = dy * g
        dx_ref[...] = r * (dn - n * jnp.mean(dn * n, axis=-1, keepdims=True))

        @pl.when(i == 0)
        def _():
            dg_ref[...] = jnp.zeros_like(dg_ref)
            loss_ref[...] = jnp.zeros_like(loss_ref)

        dg_ref[...] += jnp.sum(dy * n, axis=0, keepdims=True)
        loss_ref[...] += jnp.sum(jnp.sum(err * err, axis=-1, keepdims=True), axis=0, keepdims=True) * (0.5 / D)

    return pl.pallas_call(
        body, name="final_loss", grid=(T // tt,),
        in_specs=[pl.BlockSpec((tt, D), lambda i: (i, 0))] * 2 + [pl.BlockSpec((1, D), lambda i: (0, 0))],
        out_specs=[pl.BlockSpec((tt, D), lambda i: (i, 0)), pl.BlockSpec((1, D), lambda i: (0, 0)),
                   pl.BlockSpec((1, LANES), lambda i: (0, 0))],
        out_shape=[jax.ShapeDtypeStruct((T, D), F32), jax.ShapeDtypeStruct((1, D), F32),
                   jax.ShapeDtypeStruct((1, LANES), F32)],
        compiler_params=_cparams(("arbitrary",)),
    )(x, target, gain)


def _head_masks(ew):
    lane = lax.broadcasted_iota(jnp.int32, (1, PAIR_Q), 1)
    m0 = (lane < HEAD_DIM) | ((lane >= LANES) & (lane < LANES + ew))
    m1 = ((lane >= HEAD_DIM) & (lane < LANES)) | ((lane >= LANES + ew) & (lane < LANES + 2 * ew))
    return m0, m1


def _dot_nt(a, b):
    return lax.dot_general(a, b, (((1,), (1,)), ((), ())), preferred_element_type=F32)


def _dot_tn(a, b):
    return lax.dot_general(a, b, (((0,), (0,)), ((), ())), preferred_element_type=F32)


def _lane_halves(x, op):
    acc = x[:, 0:LANES]
    for g in range(1, x.shape[1] // LANES):
        acc = op(acc, x[:, g * LANES:(g + 1) * LANES])
    return acc


def _head_rows(cols_lane_replicated):
    t = cols_lane_replicated.T
    sub = lax.broadcasted_iota(jnp.int32, (8, t.shape[1]), 0)
    return jnp.where(sub == 1, t[HEAD_DIM:HEAD_DIM + 8], t[0:8])


def _attn_fwd(qx, kvx, *, S, ew, name):
    T = qx.shape[0]
    P = qx.shape[1] // PAIR_Q
    B = T // S
    tk = _tile(S, ATTN_BLOCK)
    tq = _tile(S, ATTN_Q_ROWS)
    nq = S // tq
    per = tq // tk

    def body(q_ref, kv_ref, o_ref, lse_ref, ot_ref, m_sc, l_sc, acc_sc):
        qi = pl.program_id(2)
        q = q_ref[...]
        masks = _head_masks(ew)
        qh = [jnp.where(m, q, jnp.zeros_like(q)) for m in masks]

        def logits(h, k, diagonal):
            s = _dot_nt(qh[h], k)
            if diagonal is None:
                return s
            row = lax.broadcasted_iota(jnp.int32, s.shape, 0)
            col = lax.broadcasted_iota(jnp.int32, s.shape, 1)
            return jnp.where(col + diagonal * tk <= row, s, NEG_BIG)

        def trip(first, count, n_diagonal=0):
            rows = [pl.ds(pl.multiple_of((first + u) * tk, tk), tk) for u in range(count)]
            diag = [None] * (count - n_diagonal) + list(range(n_diagonal))
            for h in range(2):
                ss = [logits(h, kv_ref[rows[u], 0:PAIR_Q], diag[u]) for u in range(count)]
                m_prev = m_sc[h]
                m_elem = m_prev
                for s in ss:
                    m_elem = jnp.maximum(m_elem, _lane_halves(s, jnp.maximum))
                m_new = jnp.broadcast_to(jnp.max(m_elem, axis=1, keepdims=True), (tq, LANES))
                alpha = jnp.exp(m_prev - m_new)
                l = alpha * l_sc[h]
                acc = alpha * acc_sc[h]
                for u, s in enumerate(ss):
                    p = jnp.concatenate([jnp.exp(s[:, g * LANES:(g + 1) * LANES] - m_new)
                                         for g in range(tk // LANES)], axis=1)
                    l = l + _lane_halves(p, jnp.add)
                    acc = acc + jnp.dot(p.astype(BF16), kv_ref[rows[u], PAIR_Q:PAIR_KV], preferred_element_type=F32)
                m_sc[h] = m_new
                l_sc[h] = l
                acc_sc[h] = acc

        m_sc[...] = jnp.full(m_sc.shape, NEG_BIG, F32)
        l_sc[...] = jnp.zeros_like(l_sc)
        acc_sc[...] = jnp.zeros_like(acc_sc)

        def loop_body(t, carry):
            trip(t * ATTN_UNROLL, ATTN_UNROLL)
            return carry

        below = qi * per
        lax.fori_loop(0, below // ATTN_UNROLL, loop_body, 0)
        for left in range(0, ATTN_UNROLL, per):
            @pl.when(below % ATTN_UNROLL == left)
            def _(left=left):
                trip(below - left, left + per, n_diagonal=per)

        lane = lax.broadcasted_iota(jnp.int32, (tq, LANES), 1)
        lo = lane < HEAD_DIM
        l = [jnp.sum(l_sc[h], axis=1, keepdims=True) for h in range(2)]
        o = jnp.where(lo, acc_sc[0] / l[0], acc_sc[1] / l[1])
        o_ref[...] = o.astype(BF16)
        ot_ref[...] = o.T.astype(BF16)
        lse = jnp.where(lo, m_sc[0] + jnp.log(l[0]), m_sc[1] + jnp.log(l[1]))
        for r in range(per):
            lse_ref[r] = _head_rows(lse[r * tk:(r + 1) * tk])

    return pl.pallas_call(
        body, name=name, grid=(B, P, nq),
        in_specs=[pl.BlockSpec((tq, PAIR_Q), lambda b, p, i: (b * nq + i, p)),
                  pl.BlockSpec((S, PAIR_KV), lambda b, p, i: (b, p))],
        out_specs=[pl.BlockSpec((tq, LANES), lambda b, p, i: (b * nq + i, p)),
                   pl.BlockSpec((per, None, 8, tk), lambda b, p, i: (b * nq + i, p, 0, 0)),
                   pl.BlockSpec((LANES, tq), lambda b, p, i: (p, b * nq + i))],
        out_shape=[jax.ShapeDtypeStruct((T, P * LANES), BF16), jax.ShapeDtypeStruct((T // tk, P, 8, tk), F32),
                   jax.ShapeDtypeStruct((P * LANES, T), BF16)],
        scratch_shapes=[pltpu.VMEM((2, tq, LANES), F32)] * 3,
        compiler_params=_cparams(("parallel", "parallel", "arbitrary")),
    )(qx, kvx)


def _attn_bwd(qx, kvx, o, lse, do, *, S, ew, name, bias_grad=False):
    T = qx.shape[0]
    P = qx.shape[1] // PAIR_Q
    B = T // S
    tq = _tile(S, ATTN_BLOCK)
    tk = _tile(S, ATTN_K_ROWS)
    nq = S // tq
    nk = S // tk
    per = tk // tq

    def body(q_ref, kv_ref, o_ref, lse_ref, do_ref, dq_ref, dkv_ref, *rest):
        kj = pl.program_id(2)
        if bias_grad:
            csum_ref, rsum_ref, dq_sc, delta_sc, dk_sc, dv_sc, cs_sc = rest
            cs_sc[...] = jnp.zeros_like(cs_sc)

            @pl.when(kj == 0)
            def _():
                rsum_ref[...] = jnp.zeros_like(rsum_ref)
        else:
            dq_sc, delta_sc, dk_sc, dv_sc = rest
        masks = _head_masks(ew)
        lo_q = lax.broadcasted_iota(jnp.int32, (tq, LANES), 1) < HEAD_DIM
        lo = lax.broadcasted_iota(jnp.int32, (tk, LANES), 1) < HEAD_DIM
        vmask = [lo, jnp.logical_not(lo)]

        @pl.when(kj == 0)
        def _():
            dq_sc[...] = jnp.zeros_like(dq_sc)
            for c in range(nq):
                rows = pl.ds(c * tq, tq)
                x = do_ref[rows, :].astype(F32) * o_ref[rows, :].astype(F32)
                r0 = jnp.sum(jnp.where(lo_q, x, 0.0), axis=1, keepdims=True)
                r1 = jnp.sum(jnp.where(lo_q, 0.0, x), axis=1, keepdims=True)
                delta_sc[c] = _head_rows(jnp.where(lo_q, r0, r1))

        k = kv_ref[:, 0:PAIR_Q]
        v = kv_ref[:, PAIR_Q:PAIR_KV]
        kh = [jnp.where(m, k, jnp.zeros_like(k)) for m in masks]
        vh = [jnp.where(m, v, jnp.zeros_like(v)) for m in vmask]
        dk_sc[...] = jnp.zeros_like(dk_sc)
        dv_sc[...] = jnp.zeros_like(dv_sc)

        def step(qi, diagonal):
            rows = pl.ds(pl.multiple_of(qi * tq, tq), tq)
            q = q_ref[rows, :]
            dov = do_ref[rows, :]
            lse8 = lse_ref[qi]
            dl8 = delta_sc[qi]
            for h in range(2):
                st = _dot_nt(kh[h], q)
                if diagonal is not None:
                    key = lax.broadcasted_iota(jnp.int32, st.shape, 0)
                    qry = lax.broadcasted_iota(jnp.int32, st.shape, 1)
                    st = jnp.where(key <= qry + diagonal * tq, st, NEG_BIG)
                pt = jnp.exp(st - lse8[h:h + 1, :])
                dpt = _dot_nt(vh[h], dov)
                dst = pt * (dpt - dl8[h:h + 1, :])
                if bias_grad:
                    cs_sc[h] += _lane_halves(dst, jnp.add)
                    rsum_ref[qi, h:h + 1, :] += jnp.sum(dst, axis=0, keepdims=True)
                ptb = pt.astype(BF16)
                dstb = dst.astype(BF16)
                dv_sc[h] += jnp.dot(ptb, dov, preferred_element_type=F32)
                dk_sc[h] += jnp.dot(dstb, q, preferred_element_type=F32)
                dq_sc[rows, :] += _dot_tn(dstb, kh[h])

        first = kj * per
        above = nq - per - first
        for left in range(0, ATTN_UNROLL, per):
            @pl.when(above % ATTN_UNROLL == left)
            def _(left=left):
                for d in range(per):
                    step(first + d, d)
                for u in range(left):
                    step(first + per + u, None)

        def loop_body(t, carry):
            for u in range(ATTN_UNROLL):
                step(first + per + above % ATTN_UNROLL + t * ATTN_UNROLL + u, None)
            return carry

        lax.fori_loop(0, above // ATTN_UNROLL, loop_body, 0)
        dkv_ref[:, 0:PAIR_Q] = (jnp.where(masks[0], dk_sc[0], 0.0) + jnp.where(masks[1], dk_sc[1], 0.0)).astype(BF16)
        dkv_ref[:, PAIR_Q:PAIR_KV] = jnp.where(lo, dv_sc[0], dv_sc[1]).astype(BF16)
        if bias_grad:
            csum_ref[...] = jnp.where(lo, jnp.sum(cs_sc[0], axis=1, keepdims=True),
                                      jnp.sum(cs_sc[1], axis=1, keepdims=True))

        @pl.when(kj == nk - 1)
        def _():
            dq_ref[...] = dq_sc[...].astype(BF16)

    rows_spec = pl.BlockSpec((nq, None, 8, tq), lambda b, p, j: (b, p, 0, 0))
    out_specs = [pl.BlockSpec((S, PAIR_Q), lambda b, p, j: (b, p)),
                 pl.BlockSpec((tk, PAIR_KV), lambda b, p, j: (b * nk + j, p))]
    out_shape = [jax.ShapeDtypeStruct((T, P * PAIR_Q), BF16), jax.ShapeDtypeStruct((T, P * PAIR_KV), BF16)]
    scratch = [pltpu.VMEM((S, PAIR_Q), F32), pltpu.VMEM((nq, 8, tq), F32),
               pltpu.VMEM((2, tk, PAIR_Q), F32), pltpu.VMEM((2, tk, LANES), F32)]
    if bias_grad:
        out_specs += [pl.BlockSpec((tk, LANES), lambda b, p, j: (b * nk + j, p)), rows_spec]
        out_shape += [jax.ShapeDtypeStruct((T, P * LANES), F32), jax.ShapeDtypeStruct((T // tq, P, 8, tq), F32)]
        scratch.append(pltpu.VMEM((2, tk, LANES), F32))
    return pl.pallas_call(
        body, name=name, grid=(B, P, nk),
        in_specs=[pl.BlockSpec((S, PAIR_Q), lambda b, p, j: (b, p)),
                  pl.BlockSpec((tk, PAIR_KV), lambda b, p, j: (b * nk + j, p)),
                  pl.BlockSpec((S, LANES), lambda b, p, j: (b, p)), rows_spec,
                  pl.BlockSpec((S, LANES), lambda b, p, j: (b, p))],
        out_specs=out_specs, out_shape=out_shape, scratch_shapes=scratch,
        compiler_params=_cparams(("parallel", "parallel", "arbitrary")),
    )(qx, kvx, o, lse, do)


def _fox_consts(P):
    H = 2 * P
    eq = np.zeros((3 * LANES, P * LANES), np.float32)
    ek = np.zeros((3 * LANES, P * LANES), np.float32)
    ones_q = np.zeros((1, P * LANES), np.float32)
    ones_k = np.zeros((1, P * LANES), np.float32)
    for h in range(H):
        base = (h // 2) * LANES + FOX_EXTRA * (h % 2)
        for part in range(3):
            eq[part * LANES + h, base + part] = 1.0
            ones_q[0, base + 3 + part] = 1.0
            ones_k[0, base + part] = 1.0
            ek[part * LANES + h, base + 3 + part] = -1.0
    return eq, ek, ones_q, ones_k


def _split3(f):
    hi = f.astype(BF16)
    r = f - hi.astype(F32)
    mid = r.astype(BF16)
    lo = (r - mid.astype(F32)).astype(BF16)
    return hi, mid, lo


def _tri_sum(tri, x):
    hi, mid, lo = _split3(x)
    return (jnp.dot(tri, hi, preferred_element_type=F32) + jnp.dot(tri, mid, preferred_element_type=F32)
            + jnp.dot(tri, lo, preferred_element_type=F32))


def _log1p_pos(e):
    return jnp.where(e < 0.01, e * (1.0 - e * (0.5 - e * (1.0 / 3.0))), jnp.log(1.0 + e))


def _fox_prep(qkv, fl, b_row, *, S, D, name):
    T = qkv.shape[0]
    P = D // LANES
    B = T // S
    tt = _tile(S, 256)
    per = S // tt
    eq, ek, ones_q, ones_k = _fox_consts(P)
    q_scale = HEAD_DIM ** -0.5

    def body(q_ref, k_ref, v_ref, fl_ref, b_ref, eq_ref, ek_ref, oq_ref, ok_ref, qx_ref, kvx_ref, carry):
        i = pl.program_id(1)

        @pl.when(i == 0)
        def _():
            carry[...] = jnp.zeros_like(carry)

        z = fl_ref[...] + b_ref[...]
        logf = jnp.minimum(z, 0.0) - _log1p_pos(jnp.exp(-jnp.abs(z)))
        row = lax.broadcasted_iota(jnp.int32, (tt, tt), 0)
        col = lax.broadcasted_iota(jnp.int32, (tt, tt), 1)
        tri = (col <= row).astype(BF16)
        f = _tri_sum(tri, logf) + carry[...]
        carry[...] = f[tt - 1:tt, :]
        parts = jnp.concatenate(_split3(f), axis=1)
        xq = jnp.dot(parts, eq_ref[...], preferred_element_type=F32) + oq_ref[...]
        xk = jnp.dot(parts, ek_ref[...], preferred_element_type=F32) + ok_ref[...]
        for p in range(P):
            c = slice(p * LANES, (p + 1) * LANES)
            qx_ref[:, p * PAIR_Q:p * PAIR_Q + LANES] = (q_ref[:, c].astype(F32) * q_scale).astype(BF16)
            qx_ref[:, p * PAIR_Q + LANES:(p + 1) * PAIR_Q] = xq[:, c].astype(BF16)
            kvx_ref[:, p * PAIR_KV:p * PAIR_KV + LANES] = k_ref[:, c]
            kvx_ref[:, p * PAIR_KV + LANES:p * PAIR_KV + PAIR_Q] = xk[:, c].astype(BF16)
            kvx_ref[:, p * PAIR_KV + PAIR_Q:(p + 1) * PAIR_KV] = v_ref[:, c]

    tok = lambda b, i: (b * per + i, 0)
    const = lambda b, i: (0, 0)
    return pl.pallas_call(
        body, name=name, grid=(B, per),
        in_specs=[pl.BlockSpec((tt, D), lambda b, i: (b * per + i, 0)),
                  pl.BlockSpec((tt, D), lambda b, i: (b * per + i, 1)),
                  pl.BlockSpec((tt, D), lambda b, i: (b * per + i, 2)),
                  pl.BlockSpec((tt, LANES), tok), pl.BlockSpec((1, LANES), const),
                  pl.BlockSpec(eq.shape, const), pl.BlockSpec(ek.shape, const),
                  pl.BlockSpec(ones_q.shape, const), pl.BlockSpec(ones_k.shape, const)],
        out_specs=[pl.BlockSpec((tt, P * PAIR_Q), tok), pl.BlockSpec((tt, P * PAIR_KV), tok)],
        out_shape=[jax.ShapeDtypeStruct((T, P * PAIR_Q), BF16), jax.ShapeDtypeStruct((T, P * PAIR_KV), BF16)],
        scratch_shapes=[pltpu.VMEM((1, LANES), F32)],
        compiler_params=_cparams(("arbitrary", "arbitrary")),
    )(qkv, qkv, qkv, fl, b_row, jnp.asarray(eq, BF16), jnp.asarray(ek, BF16), jnp.asarray(ones_q), jnp.asarray(ones_k))


def _fox_unprep(dqx, dkvx, csum, rsum, fl, b_row, *, S, D, name):
    T = dqx.shape[0]
    P = D // LANES
    B = T // S
    tt = _tile(S, 256)
    per = S // tt
    q_scale = HEAD_DIM ** -0.5

    def body(dq_ref, dkv_ref, cs_ref, rs_ref, fl_ref, b_ref, dqkv_ref, db_ref, carry):
        b = pl.program_id(0)
        i = pl.program_id(1)

        @pl.when(i == 0)
        def _():
            carry[...] = jnp.zeros_like(carry)

        @pl.when((i == 0) & (b == 0))
        def _():
            db_ref[...] = jnp.zeros_like(db_ref)

        df = rs_ref[...] - cs_ref[...]
        for p in range(P):
            rq = slice(p * LANES, (p + 1) * LANES)
            dqkv_ref[:, rq] = (dq_ref[:, p * PAIR_Q:p * PAIR_Q + LANES].astype(F32) * q_scale).astype(BF16)
            dqkv_ref[:, D + p * LANES:D + (p + 1) * LANES] = dkv_ref[:, p * PAIR_KV:p * PAIR_KV + LANES]
            dqkv_ref[:, 2 * D + p * LANES:2 * D + (p + 1) * LANES] = dkv_ref[:, p * PAIR_KV + PAIR_Q:(p + 1) * PAIR_KV]
        row = lax.broadcasted_iota(jnp.int32, (tt, tt), 0)
        col = lax.broadcasted_iota(jnp.int32, (tt, tt), 1)
        tri = (col >= row).astype(BF16)
        dlogf = _tri_sum(tri, df) + carry[...]
        carry[...] = dlogf[0:1, :]
        z = fl_ref[...] + b_ref[...]
        e = jnp.exp(-jnp.abs(z))
        sig_neg = jnp.where(z >= 0.0, e, 1.0) / (1.0 + e)
        dfl = dlogf * sig_neg
        dqkv_ref[:, 3 * D:3 * D + LANES] = dfl.astype(BF16)
        db_ref[...] += jnp.sum(dfl, axis=0, keepdims=True)

    rev = lambda b, i: (b * per + per - 1 - i, 0)
    const = lambda b, i: (0, 0)
    return pl.pallas_call(
        body, name=name, grid=(B, per),
        in_specs=[pl.BlockSpec((tt, P * PAIR_Q), rev), pl.BlockSpec((tt, P * PAIR_KV), rev),
                  pl.BlockSpec((tt, LANES), rev), pl.BlockSpec((tt, LANES), rev), pl.BlockSpec((tt, LANES), rev),
                  pl.BlockSpec((1, LANES), const)],
        out_specs=[pl.BlockSpec((tt, 3 * D + LANES), rev), pl.BlockSpec((1, LANES), const)],
        out_shape=[jax.ShapeDtypeStruct((T, 3 * D + LANES), BF16), jax.ShapeDtypeStruct((1, LANES), F32)],
        scratch_shapes=[pltpu.VMEM((1, LANES), F32)],
        compiler_params=_cparams(("arbitrary", "arbitrary")),
    )(dqx, dkvx, csum, rsum, fl, b_row)


def _rms(x):
    r = lax.rsqrt(jnp.mean(x * x, axis=-1, keepdims=True) + NORM_EPS)
    return x * r, r


def _mla_mid(lat, gq, gkv, cos_t, sin_s, *, name):
    T, W = lat.shape
    Rq = W - 2 * LANES
    tt = _tile(T, 512)

    def body(l_ref, gq_ref, gkv_ref, c_ref, s_ref, o_ref, ot_ref):
        nq, _ = _rms(l_ref[:, 0:Rq])
        nkv, _ = _rms(l_ref[:, Rq:Rq + LANES])
        parts = [nq * gq_ref[...], nkv * gkv_ref[...], _rope128(l_ref[:, Rq + LANES:W], c_ref[...], s_ref[...])]
        out = jnp.concatenate(parts, axis=1)
        o_ref[...] = out.astype(BF16)
        ot_ref[...] = out.T.astype(BF16)

    return pl.pallas_call(
        body, name=name, grid=(T // tt,),
        in_specs=[pl.BlockSpec((tt, W), lambda i: (i, 0)), pl.BlockSpec((1, Rq), lambda i: (0, 0)),
                  pl.BlockSpec((1, LANES), lambda i: (0, 0)), pl.BlockSpec((tt, LANES), lambda i: (i, 0)),
                  pl.BlockSpec((tt, LANES), lambda i: (i, 0))],
        out_specs=[pl.BlockSpec((tt, W), lambda i: (i, 0)), pl.BlockSpec((W, tt), lambda i: (0, i))],
        out_shape=[jax.ShapeDtypeStruct((T, W), BF16), jax.ShapeDtypeStruct((W, T), BF16)],
        compiler_params=_cparams(("parallel",)),
    )(lat, gq, gkv, cos_t, sin_s)


def _mla_mid_bwd(lat, dcq, dckr, gq, gkv, cos_t, sin_s, *, name):
    T, W = lat.shape
    Rq = W - 2 * LANES
    tt = _tile(T, 512)

    def norm_bwd(x, dy, g):
        n, r = _rms(x)
        dn = dy * g
        return r * (dn - n * jnp.mean(dn * n, axis=-1, keepdims=True)), jnp.sum(dy * n, axis=0, keepdims=True)

    def body(l_ref, dq_ref, dk_ref, gq_ref, gkv_ref, c_ref, s_ref, o_ref, dgq_ref, dgkv_ref):
        i = pl.program_id(0)

        @pl.when(i == 0)
        def _():
            dgq_ref[...] = jnp.zeros_like(dgq_ref)
            dgkv_ref[...] = jnp.zeros_like(dgkv_ref)

        dxq, dgq = norm_bwd(l_ref[:, 0:Rq], dq_ref[...], gq_ref[...])
        dxkv, dgkv = norm_bwd(l_ref[:, Rq:Rq + LANES], dk_ref[:, 0:LANES], gkv_ref[...])
        o_ref[:, 0:Rq] = dxq.astype(BF16)
        o_ref[:, Rq:Rq + LANES] = dxkv.astype(BF16)
        o_ref[:, Rq + LANES:W] = _rope128(dk_ref[:, LANES:2 * LANES], c_ref[...], -s_ref[...]).astype(BF16)
        dgq_ref[...] += dgq
        dgkv_ref[...] += dgkv

    return pl.pallas_call(
        body, name=name, grid=(T // tt,),
        in_specs=[pl.BlockSpec((tt, W), lambda i: (i, 0)), pl.BlockSpec((tt, Rq), lambda i: (i, 0)),
                  pl.BlockSpec((tt, 2 * LANES), lambda i: (i, 0)), pl.BlockSpec((1, Rq), lambda i: (0, 0)),
                  pl.BlockSpec((1, LANES), lambda i: (0, 0)), pl.BlockSpec((tt, LANES), lambda i: (i, 0)),
                  pl.BlockSpec((tt, LANES), lambda i: (i, 0))],
        out_specs=[pl.BlockSpec((tt, W), lambda i: (i, 0)), pl.BlockSpec((1, Rq), lambda i: (0, 0)),
                   pl.BlockSpec((1, LANES), lambda i: (0, 0))],
        out_shape=[jax.ShapeDtypeStruct((T, W), BF16), jax.ShapeDtypeStruct((1, Rq), F32),
                   jax.ShapeDtypeStruct((1, LANES), F32)],
        compiler_params=_cparams(("arbitrary",)),
    )(lat, dcq, dckr, gq, gkv, cos_t, sin_s)


def _uq_to_pairs(w):
    Rq = w.shape[0]
    P = w.shape[1] // (2 * (HEAD_DIM + ROPE_DIM))
    w4 = w.reshape(Rq, P, 2, HEAD_DIM + ROPE_DIM)
    nope = w4[..., :HEAD_DIM].reshape(Rq, P, 2 * HEAD_DIM)
    rope = w4[..., HEAD_DIM:].reshape(Rq, P, 2 * ROPE_DIM)
    pad = jnp.zeros((Rq, P, PAIR_Q - 2 * HEAD_DIM - 2 * ROPE_DIM), w.dtype)
    return jnp.concatenate([nope, rope, pad], axis=-1).reshape(Rq, P * PAIR_Q)


def _uq_from_pairs(g):
    Rq = g.shape[0]
    P = g.shape[1] // PAIR_Q
    g3 = g.reshape(Rq, P, PAIR_Q)
    nope = g3[..., :2 * HEAD_DIM].reshape(Rq, P, 2, HEAD_DIM)
    rope = g3[..., 2 * HEAD_DIM:2 * HEAD_DIM + 2 * ROPE_DIM].reshape(Rq, P, 2, ROPE_DIM)
    return jnp.concatenate([nope, rope], axis=-1).reshape(Rq, P * 2 * (HEAD_DIM + ROPE_DIM))


def _ukv_to_pairs(w):
    P = w.shape[1] // (4 * HEAD_DIM)
    w4 = w.reshape(KV_RANK, P, 2, 2 * HEAD_DIM)
    kn = w4[..., :HEAD_DIM].reshape(KV_RANK, P, 2 * HEAD_DIM)
    vv = w4[..., HEAD_DIM:].reshape(KV_RANK, P, 2 * HEAD_DIM)
    top = jnp.concatenate([kn, jnp.zeros((KV_RANK, P, LANES), w.dtype), vv], axis=-1)
    place = np.zeros((LANES, P, PAIR_KV), np.float32)
    for r in range(ROPE_DIM):
        place[r, :, LANES + r] = 1.0
        place[r, :, LANES + ROPE_DIM + r] = 1.0
    return jnp.concatenate([top, jnp.asarray(place, w.dtype)], axis=0).reshape(KV_RANK + LANES, P * PAIR_KV)


def _ukv_from_pairs(g):
    P = g.shape[1] // PAIR_KV
    g3 = g[:KV_RANK].reshape(KV_RANK, P, PAIR_KV)
    kn = g3[..., :2 * HEAD_DIM].reshape(KV_RANK, P, 2, HEAD_DIM)
    vv = g3[..., PAIR_Q:].reshape(KV_RANK, P, 2, HEAD_DIM)
    return jnp.concatenate([kn, vv], axis=-1).reshape(KV_RANK, P * 4 * HEAD_DIM)


def _residual_then_norm(acc, xr, g, gain, sc, sh):
    x_out = xr + g * acc
    r = lax.rsqrt(jnp.mean(x_out * x_out, axis=-1, keepdims=True) + NORM_EPS)
    h = (x_out * r) * gain * (1.0 + sc) + sh
    return x_out, acc, h, h


def _gated_out(a, w_stack, layer, x, gate, next_norm, *, S, name):
    if next_norm is None:
        return _mm(a, w_stack, "nn", name=name, b_layer=layer, out_dtypes=(F32, BF16), extras=(x,), rowvecs=(gate,),
                   seq=S, epilogue=lambda acc, xr, g: (xr + g * acc, acc)) + (None, None)
    return _mm(a, w_stack, "nn", name=name, b_layer=layer, out_dtypes=(F32, BF16, BF16, BF16),
               out_t=(False, False, False, True), extras=(x,), rowvecs=(gate,) + tuple(next_norm), seq=S,
               full_rows=True, tk=a.shape[1], epilogue=_residual_then_norm)


def _mlp_fwd(h2, w, i, x1, gate, next_norm, *, S):
    def act(acc):
        u = jnp.square(jnp.maximum(acc, 0.0))
        return acc, u, u

    p, u, u_t = _mm(h2, w["mlp_w1"], "nn", name=f"mlp_up_{i}", b_layer=i, out_dtypes=(BF16, BF16, BF16),
                    out_t=(False, False, True), epilogue=act)
    x2, z, h, h_t = _gated_out(u, w["mlp_w2"], i, x1, gate, next_norm, S=S, name=f"mlp_down_{i}")
    return x2, (p, u_t, z), h, h_t


STACKED_GRADS = ("fox_out", "mla_down", "mla_uq", "mla_ukv", "mla_out", "mlp_w1", "mlp_w2")


def _local_step(x, target, pos_f, inv_freq_row, sign_row, mod, w, slots, *, S, hooks=None):
    hooks = hooks or {}
    T, D = x.shape
    L = mod.shape[0]
    L2 = len(w["fox_out"])
    cos_t, sin_s = _rope_tables(pos_f, inv_freq_row, sign_row)
    saved = []
    B = mod.shape[2]

    def per_sequence(gain):
        return jnp.broadcast_to(gain[None], (B,) + gain.shape)

    h, h_t = _norm_mod(x, w["norm_mix_g"][0], mod[0, 1], mod[0, 0], S=S, name="norm_mix_0")
    for i in range(L):
        j = i // 2
        sh_m, sc_m, g_m, sh_f, sc_f, g_f = (mod[i, s] for s in range(6))
        if i % 2 == 0:
            qkv = _mm(h, w["fox_qkv"], "nn", name=f"fox_qkv_{i}", b_layer=j, out_dtypes=(BF16,))
            fl = _mm(h, w["fox_f"], "nn", name=f"fox_f_{i}", b_layer=j)
            qx, kvx = _fox_prep(qkv, fl, w["fox_b"][j], S=S, D=D, name=f"fox_prep_{i}")
            o, lse, o_t = _attn_fwd(qx, kvx, S=S, ew=FOX_EXTRA, name=f"fox_attn_{i}")
            mix = (qx, kvx, o, lse, o_t, fl)
            w_out = w["fox_out"]
        else:
            lat = _mm(h, w["mla_down"], "nn", name=f"mla_down_{i}", b_layer=j)
            Rq = lat.shape[1] - 2 * LANES
            cqr, cqr_t = _mla_mid(lat, w["mla_gq"][j], w["mla_gkv"][j], cos_t, sin_s, name=f"mla_mid_{i}")
            qx = _mm(cqr, w["mla_uq"], "nn", name=f"mla_uq_{i}", b_layer=j, out_dtypes=(BF16,), a_sz=Rq, tk=Rq,
                     tables=(cos_t, sin_s), epilogue=lambda acc, c, s: (_rope_pairs(acc * MLA_SCALE, c, s, 1.0),))
            kvx = _mm(cqr, w["mla_ukv"], "nn", name=f"mla_ukv_{i}", b_layer=j, out_dtypes=(BF16,), a_off=Rq,
                      a_sz=2 * LANES, tk=2 * LANES, tn=PAIR_KV)
            o, lse, o_t = _attn_fwd(qx, kvx, S=S, ew=ROPE_DIM, name=f"mla_attn_{i}")
            mix = (qx, kvx, o, lse, o_t, lat, cqr_t)
            w_out = w["mla_out"]
        x1, y, h2, h2_t = _gated_out(o, w_out, j, x, g_m, (per_sequence(w["norm_mlp_g"][i]), sc_f, sh_f), S=S,
                                     name=f"mix_out_{i}")
        if i == 0 and "fwd_mlp0" in hooks:
            w = hooks["fwd_mlp0"](x1, w)
        next_norm = (per_sequence(w["norm_mix_g"][i + 1]), mod[i + 1, 1], mod[i + 1, 0]) if i + 1 < L else None
        x2, mlp, h_next, h_next_t = _mlp_fwd(h2, w, i, x1, g_f, next_norm, S=S)
        saved.append((x, h_t, mix, y, x1, h2_t, mlp))
        x, h, h_t = x2, h_next, h_next_t
        if i == 0 and "fwd_layer1" in hooks:
            w = hooks["fwd_layer1"](x, w)

    dx, dg_final, loss = _final_loss(x, target, w["final_norm_g"])
    n_split = w["mlp_w1"][0][0].shape[1]

    grads = {k: [None] * len(w[k]) for k in ("norm_mix_g", "norm_mlp_g", "fox_b", "mla_gq", "mla_gkv")}
    grads["fox_in"] = [None] * L2
    grads.update({k: {} for k in STACKED_GRADS})
    grads["final_norm_g"] = dg_final

    def stacked(key, layer, _, a_t, b, **kw):
        group, idx, count = slots[(key, layer)]
        grads[key][group] = _mm(a_t, b, "nn", out_stack=(grads[key].get(group), idx, count), **kw)

    dmod = [None] * L
    for i in reversed(range(L)):
        j = i // 2
        x0, h_t, mix, y, x1, h2_t, (p, u_t, z) = saved[i]
        sh_m, sc_m, g_m, sh_f, sc_f, g_f = (mod[i, s] for s in range(6))
        if i == 0 and "bwd_layer0" in hooks:
            g_f = g_f + hooks["bwd_layer0"](grads)[0, 0]
        dz, dg_f = _gate_bwd(dx, z, g_f, S=S, name=f"gate_mlp_bwd_{i}")
        stacked("mlp_w2", i, L, u_t, dz, name=f"mlp_w2_grad_{i}")
        dp = _mm(dz, w["mlp_w2"], "nt", name=f"mlp_down_bwd_{i}", b_layer=i, out_dtypes=(BF16,), extras=(p,),
                 epilogue=lambda acc, pv: (acc * (2.0 * jnp.maximum(pv.astype(F32), 0.0)),))
        stacked("mlp_w1", i, L, h2_t, dp, name=f"mlp_w1_grad_{i}", out_split=n_split)
        if i == 0 and "bwd_mix0" in hooks:
            g_m = g_m + hooks["bwd_mix0"](grads)[0, 0]
        dh2 = _mm(dp, w["mlp_w1"], "nt", name=f"mlp_up_bwd_{i}", b_layer=i, out_dtypes=(BF16,))
        dx1, dsh_f, dsc_f, dgn = _norm_mod_bwd(x1, dh2, dx, w["norm_mlp_g"][i], sc_f, S=S, name=f"norm_mlp_bwd_{i}")
        grads["norm_mlp_g"][i] = dgn
        dy, dg_m = _gate_bwd(dx1, y, g_m, S=S, name=f"gate_mix_bwd_{i}")
        if i % 2 == 0:
            qx, kvx, o, lse, o_t, fl = mix
            stacked("fox_out", j, L2, o_t, dy, name=f"fox_out_grad_{i}")
            do = _mm(dy, w["fox_out"], "nt", name=f"fox_out_bwd_{i}", b_layer=j, out_dtypes=(BF16,))
            dqx, dkvx, csum, rsum = _attn_bwd(qx, kvx, o, lse, do, S=S, ew=FOX_EXTRA, name=f"fox_attn_bwd_{i}",
                                              bias_grad=True)
            n_heads = D // HEAD_DIM
            csum = jnp.pad(csum.reshape(T, n_heads, HEAD_DIM)[:, :, 0], ((0, 0), (0, LANES - n_heads)))
            rsum = jnp.transpose(rsum[:, :, :2, :], (0, 3, 1, 2)).reshape(T, n_heads)
            rsum = jnp.pad(rsum, ((0, 0), (0, LANES - n_heads)))
            dproj, db = _fox_unprep(dqx, dkvx, csum, rsum, fl, w["fox_b"][j], S=S, D=D, name=f"fox_unprep_{i}")
            grads["fox_b"][j] = db
            grads["fox_in"][j] = _mm(h_t, dproj, "nn", name=f"fox_in_grad_{i}")
            dh = _mm(dproj, w["fox_in"], "nt", name=f"fox_in_bwd_{i}", b_layer=j, out_dtypes=(BF16,),
                     tk=dproj.shape[1])
        else:
            qx, kvx, o, lse, o_t, lat, cqr_t = mix
            Rq = lat.shape[1] - 2 * LANES
            stacked("mla_out", j, L2, o_t, dy, name=f"mla_out_grad_{i}")
            do = _mm(dy, w["mla_out"], "nt", name=f"mla_out_bwd_{i}", b_layer=j, out_dtypes=(BF16,))
            dqx, dkvx = _attn_bwd(qx, kvx, o, lse, do, S=S, ew=ROPE_DIM, name=f"mla_attn_bwd_{i}")
            dqpre = _unrope(dqx, cos_t, sin_s)
            stacked("mla_uq", j, L2, cqr_t[:Rq], dqpre, name=f"mla_uq_grad_{i}", out_split=n_split)
            stacked("mla_ukv", j, L2, cqr_t[Rq:], dkvx, name=f"mla_ukv_grad_{i}", tn=PAIR_KV, out_split=n_split)
            dcq = _mm(dqpre, w["mla_uq"], "nt", name=f"mla_uq_bwd_{i}", b_layer=j)
            dckr = _mm(dkvx, w["mla_ukv"], "nt", name=f"mla_ukv_bwd_{i}", b_layer=j, tk=PAIR_KV * 2)
            dlat, dgq, dgkv = _mla_mid_bwd(lat, dcq, dckr, w["mla_gq"][j], w["mla_gkv"][j], cos_t, sin_s,
                                           name=f"mla_mid_bwd_{i}")
            grads["mla_gq"][j] = dgq
            grads["mla_gkv"][j] = dgkv
            stacked("mla_down", j, L2, h_t, dlat, name=f"mla_down_grad_{i}")
            dh = _mm(dlat, w["mla_down"], "nt", name=f"mla_down_bwd_{i}", b_layer=j, out_dtypes=(BF16,))
        dx, dsh_m, dsc_m, dgn = _norm_mod_bwd(x0, dh, dx1, w["norm_mix_g"][i], sc_m, S=S, name=f"norm_mix_bwd_{i}")
        grads["norm_mix_g"][i] = dgn
        dmod[i] = jnp.stack([dsh_m, dsc_m, dg_m, dsh_f, dsc_f, dg_f])
    return loss, dx, jnp.stack(dmod), grads


GATHERED = ("fox_in", "fox_out", "mla_down", "mla_uq", "mla_ukv", "mla_out", "mlp_w1", "mlp_w2")
ROW_SHARDED = ("fox_out", "mla_down", "mla_out", "mlp_w2")


def _shard_layouts(wts):
    dkv = wts["mla_w_dkv"]
    dkv = jnp.pad(dkv, ((0, 0), (0, 0), (0, 2 * LANES - dkv.shape[2])))
    return {
        "fox_in": _pad_lanes(wts["fox_w_in"].astype(BF16)),
        "fox_out": wts["fox_w_out"].astype(BF16),
        "mla_down": jnp.concatenate([wts["mla_w_dq"], dkv], axis=2).astype(BF16),
        "mla_uq": jax.vmap(_uq_to_pairs)(wts["mla_w_uq"].astype(BF16)),
        "mla_ukv": jax.vmap(_ukv_to_pairs)(wts["mla_w_ukv"].astype(BF16)),
        "mla_out": wts["mla_w_out"].astype(BF16),
        "mlp_w1": wts["mlp_w1"].astype(BF16),
        "mlp_w2": wts["mlp_w2"].astype(BF16),
    }


def _small_layouts(small):
    return {
        "fox_b": [jnp.pad(b, (0, LANES - b.shape[0]))[None, :] for b in small["fox_b_f"]],
        "mla_gq": [g[None, :] for g in small["mla_q_norm_g"]],
        "mla_gkv": [g[None, :] for g in small["mla_kv_norm_g"]],
        "norm_mix_g": [g[None, :] for g in small["norm_mix_g"]],
        "norm_mlp_g": [g[None, :] for g in small["norm_mlp_g"]],
        "final_norm_g": small["final_norm_g"][None, :],
    }


def _comm_groups(L, L2):
    rest = [("fox_in", 1, L2 - 1), ("fox_out", 1, L2 - 1), ("mla_down", 0, L2), ("mla_uq", 0, L2),
            ("mla_ukv", 0, L2), ("mla_out", 0, L2), ("mlp_w1", 1, L - 1), ("mlp_w2", 1, L - 1)]
    return {"mix0": [("fox_in", 0, 1), ("fox_out", 0, 1)], "mlp0": [("mlp_w1", 0, 1), ("mlp_w2", 0, 1)],
            "rest": [e for e in rest if e[2] > 0]}


def _layer_slots(groups):
    return {(n, s + l): (g, l, cnt) for g, entries in groups.items() for n, s, cnt in entries for l in range(cnt)}


def _pad_lanes(a):
    cols = a.shape[-1]
    return jnp.pad(a, [(0, 0)] * (a.ndim - 1) + [(0, -cols % LANES)])


def _weight_views(name, gathered, D, n_fox_heads):
    n, ns, rows, cols = gathered.shape
    if name == "fox_in":
        true_cols = (3 * D + n_fox_heads) // ns
        fox = jnp.concatenate([gathered[:, k, :, :true_cols] for k in range(ns)], axis=-1)
        return {"fox_qkv": fox[:, :, :3 * D], "fox_f": _pad_lanes(fox[:, :, 3 * D:]), "fox_in": _pad_lanes(fox)}
    if name in ROW_SHARDED:
        return {name: gathered.reshape(n, ns * rows, cols)}
    return {name: gathered}


def _grad_pieces(name, g, qkv_f, n_fox_heads, ns):
    if name == "fox_in":
        D = qkv_f[0].shape[0]
        fox = jnp.stack([a[:, :3 * D + n_fox_heads] for a in qkv_f])
        cols = fox.shape[2] // ns
        return jnp.stack([_pad_lanes(fox[:, :, k * cols:(k + 1) * cols]) for k in range(ns)], axis=1)
    if name in ROW_SHARDED:
        return g.reshape(g.shape[0], ns, g.shape[1] // ns, g.shape[2])
    return g


def _small_grads(g, n_fox_heads):
    return {
        "norm_mix_g": jnp.concatenate(g["norm_mix_g"], axis=0),
        "norm_mlp_g": jnp.concatenate(g["norm_mlp_g"], axis=0),
        "final_norm_g": g["final_norm_g"][0],
        "fox_b_f": jnp.concatenate(g["fox_b"], axis=0)[:, :n_fox_heads],
        "mla_q_norm_g": jnp.concatenate(g["mla_gq"], axis=0),
        "mla_kv_norm_g": jnp.concatenate(g["mla_gkv"], axis=0),
    }


def _silu(c):
    return c * (1.0 / (1.0 + jnp.exp(-c)))


def _ada_fwd(c_all, ada_w, ada_b_cols):
    L, D, C = ada_w.shape
    Bg = c_all.shape[0]
    tc = _tile(C, 512)

    def body(c_ref, w_ref, b_ref, o_ref):
        ca = _silu(c_ref[...]).astype(BF16)
        o_ref[...] = jnp.dot(ca, w_ref[...].astype(BF16), preferred_element_type=F32) + b_ref[...]

    return pl.pallas_call(
        body, name="ada_fwd", grid=(L, C // tc),
        in_specs=[pl.BlockSpec((Bg, D), lambda l, j: (0, 0)), pl.BlockSpec((None, D, tc), lambda l, j: (l, 0, j)),
                  pl.BlockSpec((None, 1, tc), lambda l, j: (l, 0, j))],
        out_specs=pl.BlockSpec((None, Bg, tc), lambda l, j: (l, 0, j)),
        out_shape=jax.ShapeDtypeStruct((L, Bg, C), F32),
        compiler_params=_cparams(("parallel", "parallel")),
    )(c_all, ada_w, ada_b_cols)


def _ada_bwd(c_all, dmod_cols):
    L, Bg, C = dmod_cols.shape
    D = c_all.shape[1]
    tc = _tile(C, 512)

    def body(c_ref, d_ref, o_ref):
        ca = _silu(c_ref[...]).astype(BF16)
        o_ref[...] = _dot_tn(ca, d_ref[...].astype(BF16))

    return pl.pallas_call(
        body, name="ada_bwd", grid=(L, C // tc),
        in_specs=[pl.BlockSpec((Bg, D), lambda l, j: (0, 0)), pl.BlockSpec((None, Bg, tc), lambda l, j: (l, 0, j))],
        out_specs=pl.BlockSpec((None, D, tc), lambda l, j: (l, 0, j)),
        out_shape=jax.ShapeDtypeStruct((L, D, C), F32),
        compiler_params=_cparams(("parallel", "parallel")),
    )(c_all, dmod_cols)


def _adamw_update(w, gv, m, v):
    mn = ADAM_B1 * m + (1.0 - ADAM_B1) * gv
    vn = ADAM_B2 * v + (1.0 - ADAM_B2) * jnp.square(gv)
    m_hat = mn / (1.0 - ADAM_B1 ** ADAM_STEP)
    v_hat = vn / (1.0 - ADAM_B2 ** ADAM_STEP)
    return -ADAM_LR * (m_hat / (jnp.sqrt(v_hat) + ADAM_EPS) + ADAM_WD * w), mn, vn


def _adamw(w, g, m, v, *, name):
    shape = w.shape
    C = shape[-1]
    R = int(np.prod(shape[:-1])) if len(shape) > 1 else 1
    w2, g2, m2, v2 = (a.reshape(R, C) for a in (w, g, m, v))
    tr = _row_tile(R, C)

    def body(w_ref, g_ref, m_ref, v_ref, d_ref, nm_ref, nv_ref):
        d_ref[...], nm_ref[...], nv_ref[...] = _adamw_update(w_ref[...], g_ref[...], m_ref[...], v_ref[...])

    spec = pl.BlockSpec((tr, C), lambda i: (i, 0))
    out = pl.pallas_call(
        body, name=name, grid=(R // tr,), in_specs=[spec] * 4, out_specs=[spec] * 3,
        out_shape=[jax.ShapeDtypeStruct((R, C), F32)] * 3, compiler_params=_cparams(("parallel",)),
    )(w2, g2, m2, v2)
    return tuple(a.reshape(shape) for a in out)


def _adamw_halves(w, g_own, g_peer, m, v, c_idx, *, name):
    L, rows, C = w.shape
    R = rows // 2
    tr = _row_tile(R, C)

    def body(c_ref, w_ref, go_ref, gp_ref, m_ref, v_ref, g_ref, d_ref, nm_ref, nv_ref):
        gv = jnp.where(pl.program_id(1) == c_ref[0], go_ref[...], gp_ref[...])
        g_ref[...] = gv
        d_ref[...], nm_ref[...], nv_ref[...] = _adamw_update(w_ref[...], gv, m_ref[...], v_ref[...])

    full = pl.BlockSpec((None, None, tr, C), lambda l, hh, i, c_ref: (l, hh, i, 0))
    half = pl.BlockSpec((None, tr, C), lambda l, hh, i, c_ref: (l, i, 0))
    grid_spec = pltpu.PrefetchScalarGridSpec(
        num_scalar_prefetch=1, grid=(L, 2, R // tr), in_specs=[full, half, half, full, full], out_specs=[full] * 4)
    split = lambda a: a.reshape(L, 2, R, C)
    out = pl.pallas_call(
        body, name=name, grid_spec=grid_spec, out_shape=[jax.ShapeDtypeStruct((L, 2, R, C), F32)] * 4,
        compiler_params=_cparams(("parallel", "parallel", "parallel")),
    )(c_idx, split(w), g_own, g_peer, split(m), split(v))
    return tuple(a.reshape(w.shape) for a in out)


def _sum_gathered(dm8, sm8):
    n_dev, Bl, R, D = dm8.shape
    Rs = sm8.shape[1]

    def body(dm_ref, sm_ref, ob_ref, os_ref):
        acc_b = jnp.zeros((R, D), F32)
        acc_s = jnp.zeros((Rs, D), F32)
        for d in range(n_dev):
            for b in range(Bl):
                acc_b = acc_b + dm_ref[d, b]
            acc_s = acc_s + sm_ref[d]
        ob_ref[...] = acc_b
        os_ref[...] = acc_s

    return pl.pallas_call(
        body, name="sum_gathered",
        out_shape=[jax.ShapeDtypeStruct((R, D), F32), jax.ShapeDtypeStruct((Rs, D), F32)],
        compiler_params=_cparams(None),
    )(dm8, sm8)


N_DEV = 8
N_CHIP = 4
ANY = pl.BlockSpec(memory_space=pl.ANY)
HBM = pl.BlockSpec(memory_space=pltpu.HBM)
SEM = pl.BlockSpec(memory_space=pltpu.SEMAPHORE)
DATAFLOW = pltpu.SideEffectType.DATAFLOW_SIDE_EFFECTING


def _mesh_pos():
    return lax.axis_index("x"), lax.axis_index("y"), lax.axis_index("c")


def _all_gather8(block, *, name, in_vmem):
    R, W = block.shape

    def body(x_ref, out_ref, send_sems, recv_sems, local_sem):
        x, y, c = _mesh_pos()
        me, sibling = (x, y, c), (x, y, 1 - c)
        chips = [(1 - x, y), (x, 1 - y), (1 - x, 1 - y)]

        def slot(px, py, pc):
            return out_ref.at[4 * px + 2 * py + pc]

        def copy(k, blk, to, src=None):
            return pltpu.make_async_remote_copy(
                src_ref=slot(*blk) if src is None else src, dst_ref=slot(*blk),
                send_sem=send_sems.at[k], recv_sem=recv_sems.at[k], device_id=to, device_id_type=MESH_ID)

        mine = pltpu.make_async_copy(x_ref, slot(*me), local_sem)
        mine.start()
        first = [copy(0, me, sibling, src=x_ref)]
        first += [copy(1 + j, me, (*chip, c), src=x_ref) for j, chip in enumerate(chips)]
        for cp in first:
            cp.start()
        passed = [copy(4 + j, (*chip, c), sibling) for j, chip in enumerate(chips)]
        for j, chip in enumerate(chips):
            copy(1 + j, (*chip, c), me).wait_recv()
            passed[j].start()
        copy(0, sibling, me).wait_recv()
        for j, chip in enumerate(chips):
            copy(4 + j, (*chip, 1 - c), me).wait_recv()
        for cp in first + passed:
            cp.wait_send()
        mine.wait()

    space = pl.BlockSpec(memory_space=pltpu.VMEM) if in_vmem else ANY
    return pl.pallas_call(
        body, name=name, out_shape=jax.ShapeDtypeStruct((N_DEV, R, W), block.dtype),
        in_specs=[space], out_specs=space,
        scratch_shapes=[pltpu.SemaphoreType.DMA((7,)), pltpu.SemaphoreType.DMA((7,)), pltpu.SemaphoreType.DMA],
        compiler_params=pltpu.CompilerParams(vmem_limit_bytes=VMEM_LIMIT_V7X),
    )(block)


def _comm_call(body, arrays, out_shapes, n_sems, *, name):
    return pl.pallas_call(
        body, name=name, out_shape=out_shapes, in_specs=[ANY] * len(arrays), out_specs=[ANY] * len(out_shapes),
        scratch_shapes=[pltpu.SemaphoreType.DMA((n_sems,)), pltpu.SemaphoreType.DMA((n_sems,)),
                        pltpu.SemaphoreType.DMA((len(arrays),))],
    )(*arrays)


def _gather_weights(shards, *, name):
    n = len(shards)

    def body(*refs):
        xs, outs = refs[:n], refs[n:2 * n]
        send_sems, recv_sems, local_sems = refs[2 * n:]
        x, y, c = _mesh_pos()
        me, sibling = (x, y, c), (x, y, 1 - c)
        chips = [(1 - x, y), (x, 1 - y), (1 - x, 1 - y)]
        waits = []
        for i in range(n):
            nl = shards[i].shape[0]
            own = xs[i].at[pl.ds(0, nl), c]

            def slot(px, py, pc, i=i, nl=nl):
                return outs[i].at[pl.ds(0, nl), 2 * px + py, pc]

            def copy(k, blk, to, src=None, i=i, slot=slot):
                return pltpu.make_async_remote_copy(
                    src_ref=slot(*blk) if src is None else src, dst_ref=slot(*blk),
                    send_sem=send_sems.at[7 * i + k], recv_sem=recv_sems.at[7 * i + k], device_id=to,
                    device_id_type=MESH_ID)

            mine = pltpu.make_async_copy(own, slot(*me), local_sems.at[i])
            mine.start()
            first = [copy(0, me, sibling, src=own)]
            first += [copy(1 + j, me, (*chip, c), src=own) for j, chip in enumerate(chips)]
            for cp in first:
                cp.start()
            waits.append((copy, mine, first))
        for copy, mine, first in waits:
            passed = [copy(4 + j, (*chip, c), sibling) for j, chip in enumerate(chips)]
            for j, chip in enumerate(chips):
                copy(1 + j, (*chip, c), me).wait_recv()
                passed[j].start()
            copy(0, sibling, me).wait_recv()
            for j, chip in enumerate(chips):
                copy(4 + j, (*chip, 1 - c), me).wait_recv()
            for cp in first + passed:
                cp.wait_send()
            mine.wait()

    out_shapes = [jax.ShapeDtypeStruct((s.shape[0], N_CHIP) + s.shape[1:], s.dtype) for s in shards]
    return _comm_call(body, shards, out_shapes, 7 * n, name=name)


def _place_own(shard, chip_idx, c_idx, *, name):
    n, _, rows, cols = shard.shape
    tr = _row_tile(rows, cols)

    def body(k_ref, c_ref, x_ref, o_ref):
        o_ref[...] = x_ref[...]

    grid_spec = pltpu.PrefetchScalarGridSpec(
        num_scalar_prefetch=2, grid=(n, rows // tr),
        in_specs=[pl.BlockSpec((None, None, tr, cols), lambda l, i, k_ref, c_ref: (l, c_ref[0], i, 0))],
        out_specs=pl.BlockSpec((None, None, None, tr, cols), lambda l, i, k_ref, c_ref: (l, k_ref[0], c_ref[0], i, 0)))
    return pl.pallas_call(
        body, name=name, grid_spec=grid_spec,
        out_shape=jax.ShapeDtypeStruct((n, N_CHIP, 2, rows, cols), shard.dtype),
        compiler_params=_cparams(("parallel", "parallel")),
    )(chip_idx, c_idx, shard)


def _gather_copies(x_refs, land_refs, send_sems, recv_sems):
    x, y, c = _mesh_pos()
    k_me = 2 * x + y
    targets = [(x, y, 1 - c), (1 - x, y, c), (x, 1 - y, c), (1 - x, 1 - y, c)]
    copies = []
    for i, (x_ref, land_ref) in enumerate(zip(x_refs, land_refs)):
        nl = x_ref.shape[0]
        for j, to in enumerate(targets):
            copies.append(pltpu.make_async_remote_copy(
                src_ref=x_ref.at[pl.ds(0, nl), c], dst_ref=land_ref.at[pl.ds(0, nl), k_me, c],
                send_sem=send_sems.at[4 * i + j], recv_sem=recv_sems.at[4 * i + j], device_id=to,
                device_id_type=MESH_ID))
    return copies


def _split_start(copies_fn, srcs, lands, after, *, name, sems_per_array):
    n = len(srcs)

    def body(*refs):
        send_sems, recv_sems = refs[2 * n + 1], refs[2 * n + 2]
        for cp in copies_fn(refs[:n], refs[n:2 * n], send_sems, recv_sems):
            cp.start()
        refs[-1][...] = jnp.zeros_like(refs[-1])

    operands = [pltpu.with_memory_space_constraint(a, pltpu.HBM) for a in list(srcs) + list(lands)]
    n_sems = sems_per_array * n
    out_shape = ([pltpu.SemaphoreType.DMA((n_sems,)), pltpu.SemaphoreType.DMA((n_sems,))]
                 + [pltpu.HBM(a.shape, a.dtype) for a in operands] + [jax.ShapeDtypeStruct((8, LANES), F32)])
    res = pl.pallas_call(
        body, name=name, out_shape=out_shape, in_specs=[HBM] * (2 * n) + [ANY],
        out_specs=[SEM, SEM] + [HBM] * (2 * n) + [pl.BlockSpec(memory_space=pltpu.VMEM)],
        input_output_aliases={i: 2 + i for i in range(2 * n)},
        compiler_params=pltpu.CompilerParams(has_side_effects=DATAFLOW),
    )(*operands, after)
    return res[0], res[1], list(res[2:2 + n]), list(res[2 + n:2 + 2 * n]), res[-1]


def _split_wait(copies_fn, send_sems, recv_sems, srcs, lands, after, *, name):
    n = len(srcs)

    def body(*refs):
        for cp in copies_fn(refs[:n], refs[n:2 * n], refs[2 * n], refs[2 * n + 1]):
            cp.wait_send()
            cp.wait_recv()

    res = pl.pallas_call(
        body, name=name, out_shape=[pltpu.HBM(a.shape, a.dtype) for a in list(srcs) + list(lands)],
        in_specs=[HBM] * (2 * n) + [SEM, SEM, ANY], out_specs=[HBM] * (2 * n),
        input_output_aliases={i: i for i in range(2 * n)},
        compiler_params=pltpu.CompilerParams(has_side_effects=DATAFLOW),
    )(*srcs, *lands, send_sems, recv_sems, after)
    return list(res[:n]), list(res[n:])


def _gather_forward(lands, *, name):
    n = len(lands)

    def body(*refs):
        xs = refs[:n]
        send_sems, recv_sems, _ = refs[2 * n:]
        x, y, c = _mesh_pos()
        chips = [(1 - x, y), (x, 1 - y), (1 - x, 1 - y)]
        copies = []
        for i in range(n):
            nl = lands[i].shape[0]
            for j, (cx, cy) in enumerate(chips):
                here = xs[i].at[pl.ds(0, nl), 2 * cx + cy, c]
                cp = pltpu.make_async_remote_copy(
                    src_ref=here, dst_ref=here, send_sem=send_sems.at[3 * i + j], recv_sem=recv_sems.at[3 * i + j],
                    device_id=(x, y, 1 - c), device_id_type=MESH_ID)
                cp.start()
                copies.append(cp)
        for cp in copies:
            cp.wait()

    return pl.pallas_call(
        body, name=name, out_shape=[jax.ShapeDtypeStruct(a.shape, a.dtype) for a in lands],
        in_specs=[ANY] * n, out_specs=[ANY] * n, input_output_aliases={i: i for i in range(n)},
        scratch_shapes=[pltpu.SemaphoreType.DMA((3 * n,)), pltpu.SemaphoreType.DMA((3 * n,)),
                        pltpu.SemaphoreType.DMA((1,))],
    )(*lands)


def _pair_copies(g_refs, land_refs, send_sems, recv_sems):
    x, y, c = _mesh_pos()
    copies = []
    for i, (g_ref, land_ref) in enumerate(zip(g_refs, land_refs)):
        nl, ns = g_ref.shape[:2]
        copies.append(pltpu.make_async_remote_copy(
            src_ref=g_ref.at[pl.ds(0, nl), pl.ds(0, ns), 1 - c], dst_ref=land_ref, send_sem=send_sems.at[i],
            recv_sem=recv_sems.at[i], device_id=(x, y, 1 - c), device_id_type=MESH_ID))
    return copies


def _pair_exchange(gs, *, name):
    n = len(gs)

    def body(*refs):
        send_sems, recv_sems, _ = refs[2 * n:]
        copies = _pair_copies(refs[:n], refs[n:2 * n], send_sems, recv_sems)
        for cp in copies:
            cp.start()
        for cp in copies:
            cp.wait()

    out_shapes = [jax.ShapeDtypeStruct(g.shape[:2] + g.shape[3:], g.dtype) for g in gs]
    return _comm_call(body, gs, out_shapes, n, name=name)


def _chip_copies(p_refs, land_refs, send_sems, recv_sems):
    x, y, c = _mesh_pos()
    k_me = 2 * x + y
    chips = [(1 - x, y), (x, 1 - y), (1 - x, 1 - y)]
    copies = []
    for i, (p_ref, land_ref) in enumerate(zip(p_refs, land_refs)):
        nl = p_ref.shape[0]
        for j, (cx, cy) in enumerate(chips):
            copies.append(pltpu.make_async_remote_copy(
                src_ref=p_ref.at[pl.ds(0, nl), 2 * cx + cy], dst_ref=land_ref.at[k_me],
                send_sem=send_sems.at[3 * i + j], recv_sem=recv_sems.at[3 * i + j],
                device_id=(cx, cy, c), device_id_type=MESH_ID))
    return copies


def _chip_landing(ps):
    return [lax.empty((p.shape[1], p.shape[0]) + p.shape[2:], p.dtype) for p in ps]


def _pair_swap(ss, *, name):
    n = len(ss)

    def body(*refs):
        xs, outs = refs[:n], refs[n:2 * n]
        send_sems, recv_sems, _ = refs[2 * n:]
        x, y, c = _mesh_pos()
        copies = []
        for i in range(n):
            cp = pltpu.make_async_remote_copy(src_ref=xs[i], dst_ref=outs[i], send_sem=send_sems.at[i],
                                              recv_sem=recv_sems.at[i], device_id=(x, y, 1 - c),
                                              device_id_type=MESH_ID)
            cp.start()
            copies.append(cp)
        for cp in copies:
            cp.wait()

    out_shapes = [jax.ShapeDtypeStruct(s.shape, s.dtype) for s in ss]
    return _comm_call(body, ss, out_shapes, n, name=name)


def _row_tile(rows, cols):
    tr = rows
    while tr * cols > 256 * 1024 and tr % 16 == 0:
        tr //= 2
    return tr


def _pair_add(g, recv, c_idx, *, name):
    n, ns, _, rows, W = g.shape
    tr = _row_tile(rows, W)

    def body(c_ref, g_ref, r_ref, o_ref):
        o_ref[...] = (g_ref[...] + r_ref[...]).astype(BF16)

    piece = pl.BlockSpec((None, tr, W), lambda p, i, c_ref: (p, i, 0))
    grid_spec = pltpu.PrefetchScalarGridSpec(
        num_scalar_prefetch=1, grid=(n * ns, rows // tr),
        in_specs=[pl.BlockSpec((None, None, tr, W), lambda p, i, c_ref: (p, c_ref[0], i, 0)), piece],
        out_specs=piece)
    out = pl.pallas_call(
        body, name=name, grid_spec=grid_spec, out_shape=jax.ShapeDtypeStruct((n * ns, rows, W), BF16),
        compiler_params=_cparams(("parallel", "parallel")),
    )(c_idx, g.reshape(n * ns, 2, rows, W), recv.reshape(n * ns, rows, W))
    return out.reshape(n, ns, rows, W)


def _sum_pieces(land, own, chip_idx, *, name):
    n, nl, A, W = land.shape
    tr = _row_tile(A, W)

    def body(k_ref, l_ref, o_ref, out_ref):
        acc = jnp.zeros(out_ref.shape, F32)
        for k in range(n):
            acc = acc + jnp.where(k == k_ref[0], o_ref[...], l_ref[k]).astype(F32)
        out_ref[...] = acc

    grid_spec = pltpu.PrefetchScalarGridSpec(
        num_scalar_prefetch=1, grid=(nl, A // tr),
        in_specs=[pl.BlockSpec((n, None, tr, W), lambda l, i, k_ref: (0, l, i, 0)),
                  pl.BlockSpec((None, None, tr, W), lambda l, i, k_ref: (l, k_ref[0], i, 0))],
        out_specs=pl.BlockSpec((None, tr, W), lambda l, i, k_ref: (l, i, 0)))
    return pl.pallas_call(
        body, name=name, grid_spec=grid_spec, out_shape=jax.ShapeDtypeStruct((nl, A, W), F32),
        compiler_params=_cparams(("parallel", "parallel")),
    )(chip_idx, land, own)


SMALL = ("norm_mix_g", "norm_mlp_g", "final_norm_g", "fox_b_f", "mla_q_norm_g", "mla_kv_norm_g")
WEIGHT_ORDER = ("ada_w", "ada_b", "norm_mix_g", "norm_mlp_g", "fox_w_in", "fox_b_f", "fox_w_out", "mla_w_dq",
                "mla_q_norm_g", "mla_w_uq", "mla_w_dkv", "mla_kv_norm_g", "mla_w_ukv", "mla_w_out", "mlp_w1",
                "mlp_w2", "final_norm_g")


def _small_rows(vals, D):
    rows = [vals["norm_mix_g"], vals["norm_mlp_g"], vals["final_norm_g"][None, :]]
    for n in ("fox_b_f", "mla_q_norm_g", "mla_kv_norm_g"):
        flat = vals[n].reshape(-1)
        assert flat.shape[0] <= D
        rows.append(jnp.pad(flat, (0, D - flat.shape[0]))[None, :])
    return jnp.concatenate(rows, axis=0)


def _small_unrows(rows, shapes):
    L = shapes["norm_mix_g"][0]
    out = {"norm_mix_g": rows[0:L], "norm_mlp_g": rows[L:2 * L], "final_norm_g": rows[2 * L]}
    for k, n in enumerate(("fox_b_f", "mla_q_norm_g", "mla_kv_norm_g")):
        size = int(np.prod(shapes[n]))
        out[n] = rows[2 * L + 1 + k, :size].reshape(shapes[n])
    return out


def kernel(x, c, positions, ada_w, ada_b, norm_mix_g, norm_mlp_g, fox_w_in, fox_b_f, fox_w_out, mla_w_dq, mla_q_norm_g, mla_w_uq, mla_w_dkv, mla_kv_norm_g, mla_w_ukv, mla_w_out, mlp_w1, mlp_w2, final_norm_g, loss_target, m_ada_w, m_ada_b, m_norm_mix_g, m_norm_mlp_g, m_fox_w_in, m_fox_b_f, m_fox_w_out, m_mla_w_dq, m_mla_q_norm_g, m_mla_w_uq, m_mla_w_dkv, m_mla_kv_norm_g, m_mla_w_ukv, m_mla_w_out, m_mlp_w1, m_mlp_w2, m_final_norm_g, v_ada_w, v_ada_b, v_norm_mix_g, v_norm_mlp_g, v_fox_w_in, v_fox_b_f, v_fox_w_out, v_mla_w_dq, v_mla_q_norm_g, v_mla_w_uq, v_mla_w_dkv, v_mla_kv_norm_g, v_mla_w_ukv, v_mla_w_out, v_mlp_w1, v_mlp_w2, v_final_norm_g):
    args = dict(locals())
    wts = {n: args[n] for n in WEIGHT_ORDER}
    mom = {n: args["m_" + n] for n in WEIGHT_ORDER}
    var = {n: args["v_" + n] for n in WEIGHT_ORDER}
    Bl, S, D = x.shape
    T = Bl * S
    L = ada_w.shape[0]
    C = ada_w.shape[2]
    mx, my, mc = _mesh_pos()
    chip = 2 * mx + my
    dev = 4 * mx + 2 * my + mc
    c_idx = jnp.reshape(mc, (1,)).astype(jnp.int32)
    chip_idx = jnp.reshape(chip, (1,)).astype(jnp.int32)
    small = {n: wts[n] for n in SMALL}
    L2, q_cols = mla_q_norm_g.shape
    n_fox_heads = fox_b_f.shape[1]

    shards = _shard_layouts(wts)
    groups = _comm_groups(L, L2)
    slots = _layer_slots(groups)

    def row_halves(a):
        return a.reshape(a.shape[:-2] + (2, a.shape[-2] // 2, a.shape[-1]))

    def whole_rows(a):
        return a.reshape(a.shape[:2] + (a.shape[2] * a.shape[3], a.shape[4]))

    part = {g: [row_halves(shards[n][s:s + cnt]) for n, s, cnt in entries] for g, entries in groups.items()}
    mix0 = _gather_weights(part["mix0"], name="gather_mix0")
    gather_sems, after = {}, mix0[0]
    for group in ("mlp0", "rest"):
        placed = [_place_own(a, chip_idx, c_idx, name=f"gather_place_{group}_{n}")
                  for a, (n, _, _) in zip(part[group], groups[group])]
        gather_sems[group] = _split_start(_gather_copies, part[group], placed, after, name=f"gather_{group}_start",
                                          sems_per_array=4)
        after = gather_sems[group][4]

    def layer_weights(w, group, arrays):
        for (n, s, cnt), a in zip(groups[group], arrays):
            for key, view in _weight_views(n, whole_rows(a), D, n_fox_heads).items():
                for l in range(cnt):
                    w[key][s + l] = (view, l)

    w = {key: [None] * L2
         for key in ("fox_qkv", "fox_f", "fox_in", "fox_out", "mla_down", "mla_uq", "mla_ukv", "mla_out")}
    w.update({key: [None] * L for key in ("mlp_w1", "mlp_w2")})
    layer_weights(w, "mix0", mix0)

    def gathered_now(group):
        def hook(x_now, w):
            _, landed = _split_wait(_gather_copies, *gather_sems[group][:4], x_now, name=f"gather_{group}_wait")
            layer_weights(w, group, _gather_forward(landed, name=f"gather_{group}_forward"))
            return w
        return hook

    c_pad = jnp.concatenate([c, jnp.pad(mla_q_norm_g, ((0, 8 - Bl - L2), (0, D - q_cols)))], axis=0)
    c8 = _all_gather8(c_pad, name="gather_c", in_vmem=True)
    c_all = c8[:, :Bl].reshape(N_DEV * Bl, D)
    qg4 = c8.reshape(N_CHIP, 2, 8, D)[:, 0, Bl:Bl + L2, :q_cols]
    small["mla_q_norm_g"] = jnp.transpose(qg4, (1, 0, 2)).reshape(L2, N_CHIP * q_cols)
    ada_b_cols = lax.dynamic_slice_in_dim(ada_b, chip * C, C, axis=1)[:, None, :]
    mod_cols = _ada_fwd(c_all, ada_w, ada_b_cols)
    mod8 = _all_gather8(mod_cols.reshape(L * N_DEV * Bl, C), name="gather_mod", in_vmem=True)
    mod4 = mod8.reshape(N_CHIP, 2, L, N_DEV * Bl, C)[:, 0]
    mod_me = lax.dynamic_slice_in_dim(mod4, dev * Bl, Bl, axis=2)
    mod = jnp.transpose(mod_me, (1, 2, 0, 3)).reshape(L, Bl, 6, D)
    mod = jnp.transpose(mod, (0, 2, 1, 3))[:, :, :, None, :]

    w.update(_small_layouts(small))
    mod = mod + after[0, 0]
    pending = {}

    def grad_pieces(group, g_now):
        out = []
        for n, s, cnt in groups[group]:
            qkv_f = [g_now["fox_in"][j] for j in range(s, s + cnt)] if n == "fox_in" else None
            stacked_g = None if n == "fox_in" else g_now[n][group]
            out.append(row_halves(_grad_pieces(n, stacked_g, qkv_f, n_fox_heads, N_CHIP)))
        return out

    def pair_added(group, big, sibling):
        return [_pair_add(a, r, c_idx, name=f"grad_pair_add_{group}_{n}")
                for (n, _, _), a, r in zip(groups[group], big, sibling)]

    def exchange_start(group, ps, after=None):
        pending[group] = _split_start(_chip_copies, ps, _chip_landing(ps), chip_idx if after is None else after,
                                      name=f"grad_exchange_{group}_start", sems_per_array=3)
        return pending[group][4]

    def bwd_layer0(g_now):
        big = grad_pieces("rest", g_now)
        landing = [lax.empty(a.shape[:2] + a.shape[3:], a.dtype) for a in big]
        pending["rest_pair"] = _split_start(_pair_copies, big, landing, chip_idx, name="grad_pair_rest_start",
                                            sems_per_array=1)
        return pending["rest_pair"][4]

    def bwd_mix0(g_now):
        send_sems, recv_sems, big, landed, _ = pending["rest_pair"]
        big, landed = _split_wait(_pair_copies, send_sems, recv_sems, big, landed, g_now["mlp_w1"]["mlp0"],
                                  name="grad_pair_rest_wait")
        started = exchange_start("rest", pair_added("rest", big, landed))
        big = grad_pieces("mlp0", g_now)
        return exchange_start("mlp0", pair_added("mlp0", big, _pair_exchange(big, name="grad_pair_exchange_mlp0")),
                              after=started)

    half = ROPE_DIM // 2
    inv_freq = ROPE_THETA ** (-jnp.arange(0, ROPE_DIM, 2, dtype=F32) / ROPE_DIM)
    lane = np.arange(LANES)
    inv_freq_row = jnp.tile(inv_freq, LANES // half)[None, :]
    sign_row = jnp.asarray(np.where(lane < 2 * ROPE_DIM, np.where(lane % ROPE_DIM < half, -1.0, 1.0), 0.0), F32)[None, :]
    pos_f = positions.astype(F32).reshape(T, 1)
    loss_row, grad_x, dmod, g = _local_step(x.reshape(T, D), loss_target.reshape(T, D), pos_f, inv_freq_row, sign_row,
                                            mod, w, slots, S=S,
                                            hooks={"fwd_mlp0": gathered_now("mlp0"), "fwd_layer1": gathered_now("rest"),
                                                   "bwd_layer0": bwd_layer0, "bwd_mix0": bwd_mix0})
    g_small = _small_grads(g, n_fox_heads)
    big = grad_pieces("mix0", g)
    exchange_start("mix0", pair_added("mix0", big, _pair_exchange(big, name="grad_pair_exchange_mix0")))

    Rs = -(-(2 * L + 5) // 8) * 8
    srows = jnp.concatenate([_small_rows(g_small, D), jnp.pad(loss_row, ((0, 0), (0, D - LANES)))], axis=0)
    srows = jnp.pad(srows, ((0, Rs - srows.shape[0]), (0, 0)))
    drows = jnp.transpose(dmod[:, :, :, 0, :], (2, 0, 1, 3)).reshape(Bl * L * 6, D)
    both8 = _all_gather8(jnp.concatenate([drows, srows], axis=0), name="gather_small", in_vmem=True)
    dm8 = both8[:, :Bl * L * 6].reshape(N_DEV, Bl, L * 6, D)
    sm8 = both8[:, Bl * L * 6:]
    adb_rows, small_sum = _sum_gathered(dm8, sm8)
    grad_ada_b = adb_rows.reshape(L, 6 * D)
    loss = small_sum[2 * L + 4, 0]
    small_shapes = {n: (wts[n].shape if n != "mla_q_norm_g" else (wts[n].shape[0], N_CHIP * q_cols)) for n in SMALL}
    gs = _small_unrows(small_sum, small_shapes)
    gs["mla_q_norm_g"] = lax.dynamic_slice_in_dim(gs["mla_q_norm_g"], chip * q_cols, q_cols, axis=1)

    dmod16 = jnp.transpose(dm8.reshape(N_DEV, Bl, L, 6 * D), (2, 0, 1, 3)).reshape(L, N_DEV * Bl, 6 * D)
    dmod_cols = lax.dynamic_slice_in_dim(dmod16, chip * C, C, axis=2)
    grad_ada_w = _ada_bwd(c_all, dmod_cols)

    grads = dict(gs)
    grads["ada_w"] = grad_ada_w
    grads["ada_b"] = grad_ada_b
    delta, new_m, new_v = {}, {}, {}
    for n in ("ada_w", "ada_b"):
        delta[n], new_m[n], new_v[n] = _adamw(wts[n], grads[n], mom[n], var[n], name=f"adamw_{n}")
    shard_small_shapes = {n: wts[n].shape for n in SMALL}
    packs = [jnp.pad(_small_rows({n: src[n] for n in SMALL}, D), ((0, Rs - 2 * L - 4), (0, 0)))
             for src in (wts, grads, mom, var)]
    for dst, rows in zip((delta, new_m, new_v), _adamw(*packs, name="adamw_small")):
        dst.update(_small_unrows(rows, shard_small_shapes))

    halves = {}
    for group, after in (("rest", grad_x), ("mlp0", grad_x), ("mix0", delta["ada_w"])):
        send_sems, recv_sems, ps, lands, _ = pending[group]
        ps, lands = _split_wait(_chip_copies, send_sems, recv_sems, ps, lands, after, name=f"grad_exchange_{group}_wait")
        sums = [_sum_pieces(ld, p, chip_idx, name=f"grad_sum_{group}_{n}")
                for (n, _, _), ld, p in zip(groups[group], lands, ps)]
        swapped = _pair_swap(sums, name=f"grad_pair_swap_{group}")
        for (n, _, _), a, b in zip(groups[group], sums, swapped):
            halves[(n, group)] = (a, b)

    def all_layers(n, which):
        return jnp.concatenate([halves[(n, grp)][which] for grp in groups if (n, grp) in halves], axis=0)

    own = {n: all_layers(n, 0) for n in GATHERED}
    peer = {n: all_layers(n, 1) for n in GATHERED}
    for nat, n in (("fox_w_in", "fox_in"), ("fox_w_out", "fox_out"), ("mla_w_out", "mla_out"), ("mlp_w1", "mlp_w1"),
                   ("mlp_w2", "mlp_w2")):
        cols = wts[nat].shape[-1]
        res = _adamw_halves(_pad_lanes(wts[nat]), own[n], peer[n], _pad_lanes(mom[nat]), _pad_lanes(var[nat]), c_idx,
                            name=f"adamw_{nat}")
        grads[nat], delta[nat], new_m[nat], new_v[nat] = (a[..., :cols] for a in res)
    joined = {n: jnp.concatenate([jnp.where(mc == 0, own[n], peer[n]), jnp.where(mc == 0, peer[n], own[n])], axis=1)
              for n in ("mla_down", "mla_uq", "mla_ukv")}
    rq = mla_w_dq.shape[-1]
    grads["mla_w_dq"] = joined["mla_down"][:, :, :rq]
    grads["mla_w_dkv"] = joined["mla_down"][:, :, rq:rq + KV_RANK + ROPE_DIM]
    grads["mla_w_uq"] = jax.vmap(_uq_from_pairs)(joined["mla_uq"])
    grads["mla_w_ukv"] = jax.vmap(_ukv_from_pairs)(joined["mla_ukv"])
    for n in ("mla_w_dq", "mla_w_dkv", "mla_w_uq", "mla_w_ukv"):
        delta[n], new_m[n], new_v[n] = _adamw(wts[n], grads[n], mom[n], var[n], name=f"adamw_{n}")

    return (loss, grad_x.reshape(Bl, S, D), *[grads[n] for n in WEIGHT_ORDER], *[delta[n] for n in WEIGHT_ORDER],
            *[new_m[n] for n in WEIGHT_ORDER], *[new_v[n] for n in WEIGHT_ORDER])
```

```python
import numpy as np
import jax
import jax.numpy as jnp
from jax import lax
from jax.experimental import pallas as pl
from jax.experimental.pallas import tpu as pltpu

F32 = jnp.float32
BF16 = jnp.bfloat16
MESH_ID = pl.DeviceIdType.MESH

NORM_EPS = 1e-6
ROPE_THETA = 10000.0
HEAD_DIM = 64
ROPE_DIM = 32
KV_RANK = 128
MLA_SCALE = (HEAD_DIM + ROPE_DIM) ** -0.5
FOX_EXTRA = 6
PAIR_Q = 256
PAIR_KV = 384
LANES = 128
ADAM_LR = 0.001
ADAM_B1 = 0.9
ADAM_B2 = 0.999
ADAM_EPS = 1e-08
ADAM_WD = 0.01
ADAM_STEP = 10
VMEM_LIMIT_V7X = 48 * 1024 * 1024
MM_VMEM_BUDGET = 36 * 1024 * 1024
MM_VMEM_HEADROOM = VMEM_LIMIT_V7X - MM_VMEM_BUDGET
NEG_BIG = -1e30
ATTN_UNROLL = 4
ATTN_BLOCK = 256
ATTN_Q_ROWS = 512
ATTN_K_ROWS = 512

BIG_WEIGHTS = (("fox_w_in", 2), ("fox_w_out", 1), ("mla_w_dq", 1), ("mla_w_uq", 2), ("mla_w_dkv", 1),
               ("mla_w_ukv", 2), ("mla_w_out", 1), ("mlp_w1", 2), ("mlp_w2", 1))


def _cparams(sem=None, vmem_limit=VMEM_LIMIT_V7X):
    return pltpu.CompilerParams(dimension_semantics=sem, vmem_limit_bytes=vmem_limit)


def _tile(n, want):
    if n <= want:
        return n
    for t in range(want - want % LANES, 0, -LANES):
        if n % t == 0:
            return t
    raise ValueError((n, want))


def _mm(a, b, mode, *, name, out_dtypes=(F32,), epilogue=None, extras=(), rowvecs=(), tables=(),
        seq=None, a_off=0, a_sz=None, b_layer=None, out_stack=None, out_split=0, out_t=(), full_rows=False,
        vmem_budget=MM_VMEM_BUDGET, tm=1024, tn=1024, tk=2048):
    if isinstance(b, (list, tuple)):
        b, b_layer = b[b_layer]
    b_rows, b_cols = b.shape[-2], b.shape[-1]
    n_split = b.shape[1] if b.ndim == 4 else 1
    assert mode in ("nn", "nt")
    if mode == "nn":
        M, K, N = a.shape[0], b_rows, b_cols * n_split
    else:
        M, K, N = a.shape[0], b_cols * n_split, b_rows
    assert a_sz is None or a_sz == K
    tm = _tile(seq if rowvecs else M, tm)
    n_piece = N // max(out_split, n_split if mode == "nn" else 1, 1)
    tn = _tile(n_piece, tn)
    tk = _tile(K // (n_split if mode == "nt" else 1), tk)
    ne, nr, nt_ = len(extras), len(rowvecs), len(tables)
    no = len(out_dtypes)

    def vmem_estimate():
        blocks = tm * tk * a.dtype.itemsize + tk * tn * b.dtype.itemsize
        blocks += tm * tn * (sum(e.dtype.itemsize for e in extras) + sum(jnp.dtype(d).itemsize for d in out_dtypes))
        return 2 * blocks + 2 * tm * tn * 4

    if full_rows:
        assert tn == N
    while vmem_estimate() > vmem_budget and max(tm, tn) > 256:
        if tn >= tm and not full_rows:
            tn //= 2
        else:
            tm //= 2
    nk = K // tk

    assert a_off % tk == 0
    a_spec = pl.BlockSpec((tm, tk), lambda i, j, k: (i, k + a_off // tk))
    dims = (((1,), (0,)), ((), ())) if mode == "nn" else (((1,), (1,)), ((), ()))
    lead = () if b.ndim == 2 else (b_layer,)
    sq = (None,) * (b.ndim - 2)
    if mode == "nt":
        kb = b_cols // tk
        if b.ndim == 4:
            b_spec = pl.BlockSpec(sq + (tn, tk), lambda i, j, k: lead + (k // kb, j, k % kb))
        else:
            b_spec = pl.BlockSpec(sq + (tn, tk), lambda i, j, k: lead + (j, k))
    else:
        nb = b_cols // tn
        if b.ndim == 4:
            b_spec = pl.BlockSpec(sq + (tk, tn), lambda i, j, k: lead + (j // nb, k, j % nb))
        else:
            b_spec = pl.BlockSpec(sq + (tk, tn), lambda i, j, k: lead + (k, j))
    in_specs = [a_spec, b_spec]
    in_specs += [pl.BlockSpec((tm, tn), lambda i, j, k: (i, j)) for _ in extras]
    if rowvecs:
        assert seq % tm == 0
        per = seq // tm
        in_specs += [pl.BlockSpec((None, 1, tn), lambda i, j, k: (i // per, 0, j)) for _ in rowvecs]
    in_specs += [pl.BlockSpec((tm, LANES), lambda i, j, k: (i, 0)) for _ in tables]
    operands = [a, b, *extras, *rowvecs, *tables]
    aliases = {}
    transposed = tuple(out_t) + (False,) * (no - len(out_t))
    if out_stack is None:
        out_specs = [pl.BlockSpec((tn, tm), lambda i, j, k: (j, i)) if t else pl.BlockSpec((tm, tn), lambda i, j, k: (i, j))
                     for t in transposed]
        out_shape = [jax.ShapeDtypeStruct((N, M) if t else (M, N), d) for d, t in zip(out_dtypes, transposed)]
    else:
        prev, layer, n_layers = out_stack
        assert no == 1
        if out_split:
            ob = n_piece // tn
            out_specs = [pl.BlockSpec((None, None, tm, tn), lambda i, j, k: (layer, j // ob, i, j % ob))]
            out_shape = [jax.ShapeDtypeStruct((n_layers, out_split, M, n_piece), out_dtypes[0])]
        else:
            out_specs = [pl.BlockSpec((None, tm, tn), lambda i, j, k: (layer, i, j))]
            out_shape = [jax.ShapeDtypeStruct((n_layers, M, N), out_dtypes[0])]
        if prev is not None:
            in_specs.append(pl.BlockSpec(memory_space=pl.ANY))
            aliases = {len(operands): 0}
            operands.append(prev)
    n_in = len(operands)

    def body(*refs):
        a_ref, b_ref = refs[0], refs[1]
        side = refs[2:2 + ne + nr + nt_]
        outs = refs[n_in:n_in + no]

        def finish(acc):
            res = (acc,) if epilogue is None else epilogue(acc, *[r[...] for r in side])
            for o_ref, r, t in zip(outs, res, transposed):
                o_ref[...] = (r.T if t else r).astype(o_ref.dtype)

        part = lax.dot_general(a_ref[...].astype(BF16), b_ref[...].astype(BF16), dims,
                               preferred_element_type=F32)
        if nk == 1:
            finish(part)
        else:
            acc_ref = refs[-1]
            k = pl.program_id(2)

            @pl.when(k == 0)
            def _():
                acc_ref[...] = part

            @pl.when(k > 0)
            def _():
                acc_ref[...] += part

            @pl.when(k == nk - 1)
            def _():
                finish(acc_ref[...])

    res = pl.pallas_call(
        body, name=name, grid=(M // tm, N // tn, nk), in_specs=in_specs, out_specs=out_specs,
        out_shape=out_shape, scratch_shapes=[pltpu.VMEM((tm, tn), F32)] if nk > 1 else [],
        input_output_aliases=aliases,
        compiler_params=_cparams(("parallel", "parallel", "arbitrary"), vmem_limit=vmem_budget + MM_VMEM_HEADROOM),
    )(*operands)
    return res[0] if no == 1 else tuple(res)


def _rope128(x, cos_t, sin_s):
    lane = lax.broadcasted_iota(jnp.int32, x.shape, 1)
    first = (lane % ROPE_DIM) < (ROPE_DIM // 2)
    swapped = jnp.where(first, pltpu.roll(x, LANES - ROPE_DIM // 2, 1), pltpu.roll(x, ROPE_DIM // 2, 1))
    return x * cos_t + swapped * sin_s


def _rope_pairs(acc, cos_t, sin_s, sign):
    parts = []
    for p in range(acc.shape[1] // PAIR_Q):
        parts.append(acc[:, p * PAIR_Q:p * PAIR_Q + LANES])
        parts.append(_rope128(acc[:, p * PAIR_Q + LANES:(p + 1) * PAIR_Q], cos_t, sign * sin_s))
    return jnp.concatenate(parts, axis=1)


def _rope_tables(pos_f, inv_freq_row, sign_row):
    T = pos_f.shape[0]
    tt = _tile(T, 512)

    def body(p_ref, f_ref, s_ref, cos_ref, sin_ref):
        ang = p_ref[...] * f_ref[...]
        cos_ref[...] = jnp.cos(ang)
        sin_ref[...] = jnp.sin(ang) * s_ref[...]

    return pl.pallas_call(
        body, name="rope_tables", grid=(T // tt,),
        in_specs=[pl.BlockSpec((tt, 1), lambda i: (i, 0)), pl.BlockSpec((1, LANES), lambda i: (0, 0)),
                  pl.BlockSpec((1, LANES), lambda i: (0, 0))],
        out_specs=[pl.BlockSpec((tt, LANES), lambda i: (i, 0))] * 2,
        out_shape=[jax.ShapeDtypeStruct((T, LANES), F32)] * 2,
        compiler_params=_cparams(("parallel",)),
    )(pos_f, inv_freq_row, sign_row)


def _unrope(dqx, cos_t, sin_s):
    T, W = dqx.shape
    tt = _tile(T, 512)

    def body(d_ref, c_ref, s_ref, o_ref):
        o_ref[...] = _rope_pairs(d_ref[...].astype(F32) * MLA_SCALE, c_ref[...], s_ref[...], -1.0).astype(BF16)

    return pl.pallas_call(
        body, name="mla_unrope", grid=(T // tt,),
        in_specs=[pl.BlockSpec((tt, W), lambda i: (i, 0)), pl.BlockSpec((tt, LANES), lambda i: (i, 0)),
                  pl.BlockSpec((tt, LANES), lambda i: (i, 0))],
        out_specs=pl.BlockSpec((tt, W), lambda i: (i, 0)),
        out_shape=jax.ShapeDtypeStruct((T, W), BF16),
        compiler_params=_cparams(("parallel",)),
    )(dqx, cos_t, sin_s)


def _row_specs(tt, D, per, n):
    return [pl.BlockSpec((None, 1, D), lambda i: (i // per, 0, 0)) for _ in range(n)]


def _norm_mod(x, gain, sc, sh, *, S, name):
    T, D = x.shape
    tt = _tile(S, 512)
    per = S // tt

    def body(x_ref, g_ref, sc_ref, sh_ref, h_ref, ht_ref):
        xv = x_ref[...]
        r = lax.rsqrt(jnp.mean(xv * xv, axis=-1, keepdims=True) + NORM_EPS)
        h = (xv * r) * g_ref[...] * (1.0 + sc_ref[...]) + sh_ref[...]
        h_ref[...] = h.astype(BF16)
        ht_ref[...] = h.T.astype(BF16)

    return pl.pallas_call(
        body, name=name, grid=(T // tt,),
        in_specs=[pl.BlockSpec((tt, D), lambda i: (i, 0)), pl.BlockSpec((1, D), lambda i: (0, 0))]
        + _row_specs(tt, D, per, 2),
        out_specs=[pl.BlockSpec((tt, D), lambda i: (i, 0)), pl.BlockSpec((D, tt), lambda i: (0, i))],
        out_shape=[jax.ShapeDtypeStruct((T, D), BF16), jax.ShapeDtypeStruct((D, T), BF16)],
        compiler_params=_cparams(("parallel",)),
    )(x, gain, sc, sh)


def _norm_mod_bwd(x, dh, dres, gain, sc, *, S, name):
    T, D = x.shape
    B = T // S
    tt = _tile(S, 512)
    per = S // tt

    def body(x_ref, dh_ref, dres_ref, g_ref, sc_ref, dx_ref, dsh_ref, dsc_ref, dg_ref):
        i = pl.program_id(0)
        xv = x_ref[...]
        dhv = dh_ref[...].astype(F32)
        r = lax.rsqrt(jnp.mean(xv * xv, axis=-1, keepdims=True) + NORM_EPS)
        n = xv * r
        g = g_ref[...]
        one_sc = 1.0 + sc_ref[...]
        dn = dhv * (g * one_sc)
        dx_ref[...] = dres_ref[...] + r * (dn - n * jnp.mean(dn * n, axis=-1, keepdims=True))
        dhn = dhv * n

        @pl.when(i % per == 0)
        def _():
            dsh_ref[...] = jnp.zeros_like(dsh_ref)
            dsc_ref[...] = jnp.zeros_like(dsc_ref)

        @pl.when(i == 0)
        def _():
            dg_ref[...] = jnp.zeros_like(dg_ref)

        dsh_ref[...] += jnp.sum(dhv, axis=0, keepdims=True)
        dsc_ref[...] += jnp.sum(dhn, axis=0, keepdims=True) * g
        dg_ref[...] += jnp.sum(dhn, axis=0, keepdims=True) * one_sc

    return pl.pallas_call(
        body, name=name, grid=(T // tt,),
        in_specs=[pl.BlockSpec((tt, D), lambda i: (i, 0))] * 3 + [pl.BlockSpec((1, D), lambda i: (0, 0))]
        + _row_specs(tt, D, per, 1),
        out_specs=[pl.BlockSpec((tt, D), lambda i: (i, 0))] + _row_specs(tt, D, per, 2)
        + [pl.BlockSpec((1, D), lambda i: (0, 0))],
        out_shape=[jax.ShapeDtypeStruct((T, D), F32), jax.ShapeDtypeStruct((B, 1, D), F32),
                   jax.ShapeDtypeStruct((B, 1, D), F32), jax.ShapeDtypeStruct((1, D), F32)],
        compiler_params=_cparams(("arbitrary",)),
    )(x, dh, dres, gain, sc)


def _gate_bwd(dx, y, g, *, S, name):
    T, D = dx.shape
    B = T // S
    tt = _tile(S, 512)
    per = S // tt

    def body(dx_ref, y_ref, g_ref, dy_ref, dg_ref):
        i = pl.program_id(0)
        dxv = dx_ref[...]
        dy_ref[...] = (dxv * g_ref[...]).astype(BF16)

        @pl.when(i % per == 0)
        def _():
            dg_ref[...] = jnp.zeros_like(dg_ref)

        dg_ref[...] += jnp.sum(dxv * y_ref[...], axis=0, keepdims=True)

    return pl.pallas_call(
        body, name=name, grid=(T // tt,),
        in_specs=[pl.BlockSpec((tt, D), lambda i: (i, 0))] * 2 + _row_specs(tt, D, per, 1),
        out_specs=[pl.BlockSpec((tt, D), lambda i: (i, 0))] + _row_specs(tt, D, per, 1),
        out_shape=[jax.ShapeDtypeStruct((T, D), BF16), jax.ShapeDtypeStruct((B, 1, D), F32)],
        compiler_params=_cparams(("arbitrary",)),
    )(dx, y, g)


def _final_loss(x, target, gain):
    T, D = x.shape
    tt = _tile(T, 512)

    def body(x_ref, t_ref, g_ref, dx_ref, dg_ref, loss_ref):
        i = pl.program_id(0)
        xv = x_ref[...]
        r = lax.rsqrt(jnp.mean(xv * xv, axis=-1, keepdims=True) + NORM_EPS)
        n = xv * r
        g = g_ref[...]
        err = n * g - t_ref[...]
        dy = err * (1.0 / D)
        dn = dy * g
        dx_ref[...] = r * (dn - n * jnp.mean(dn * n, axis=-1, keepdims=True))

        @pl.when(i == 0)
        def _():
            dg_ref[...] = jnp.zeros_like(dg_ref)
            loss_ref[...] = jnp.zeros_like(loss_ref)

        dg_ref[...] += jnp.sum(dy * n, axis=0, keepdims=True)
        loss_ref[...] += jnp.sum(jnp.sum(err * err, axis=-1, keepdims=True), axis=0, keepdims=True) * (0.5 / D)

    return pl.pallas_call(
        body, name="final_loss", grid=(T // tt,),
        in_specs=[pl.BlockSpec((tt, D), lambda i: (i, 0))] * 2 + [pl.BlockSpec((1, D), lambda i: (0, 0))],
        out_specs=[pl.BlockSpec((tt, D), lambda i: (i, 0)), pl.BlockSpec((1, D), lambda i: (0, 0)),
                   pl.BlockSpec((1, LANES), lambda i: (0, 0))],
        out_shape=[jax.ShapeDtypeStruct((T, D), F32), jax.ShapeDtypeStruct((1, D), F32),
                   jax.ShapeDtypeStruct((1, LANES), F32)],
        compiler_params=_cparams(("arbitrary",)),
    )(x, target, gain)


def _head_masks(ew):
    lane = lax.broadcasted_iota(jnp.int32, (1, PAIR_Q), 1)
    m0 = (lane < HEAD_DIM) | ((lane >= LANES) & (lane < LANES + ew))
    m1 = ((lane >= HEAD_DIM) & (lane < LANES)) | ((lane >= LANES + ew) & (lane < LANES + 2 * ew))
    return m0, m1


def _dot_nt(a, b):
    return lax.dot_general(a, b, (((1,), (1,)), ((), ())), preferred_element_type=F32)


def _dot_tn(a, b):
    return lax.dot_general(a, b, (((0,), (0,)), ((), ())), preferred_element_type=F32)


def _lane_halves(x, op):
    acc = x[:, 0:LANES]
    for g in range(1, x.shape[1] // LANES):
        acc = op(acc, x[:, g * LANES:(g + 1) * LANES])
    return acc


def _head_rows(cols_lane_replicated):
    t = cols_lane_replicated.T
    sub = lax.broadcasted_iota(jnp.int32, (8, t.shape[1]), 0)
    return jnp.where(sub == 1, t[HEAD_DIM:HEAD_DIM + 8], t[0:8])


def _attn_fwd(qx, kvx, *, S, ew, name):
    T = qx.shape[0]
    P = qx.shape[1] // PAIR_Q
    B = T // S
    tk = _tile(S, ATTN_BLOCK)
    tq = _tile(S, ATTN_Q_ROWS)
    nq = S // tq
    per = tq // tk

    def body(q_ref, kv_ref, o_ref, lse_ref, ot_ref, m_sc, l_sc, acc_sc):
        qi = pl.program_id(2)
        q = q_ref[...]
        masks = _head_masks(ew)
        qh = [jnp.where(m, q, jnp.zeros_like(q)) for m in masks]

        def logits(h, k, diagonal):
            s = _dot_nt(qh[h], k)
            if diagonal is None:
                return s
            row = lax.broadcasted_iota(jnp.int32, s.shape, 0)
            col = lax.broadcasted_iota(jnp.int32, s.shape, 1)
            return jnp.where(col + diagonal * tk <= row, s, NEG_BIG)

        def trip(first, count, n_diagonal=0):
            rows = [pl.ds(pl.multiple_of((first + u) * tk, tk), tk) for u in range(count)]
            diag = [None] * (count - n_diagonal) + list(range(n_diagonal))
            for h in range(2):
                ss = [logits(h, kv_ref[rows[u], 0:PAIR_Q], diag[u]) for u in range(count)]
                m_prev = m_sc[h]
                m_elem = m_prev
                for s in ss:
                    m_elem = jnp.maximum(m_elem, _lane_halves(s, jnp.maximum))
                m_new = jnp.broadcast_to(jnp.max(m_elem, axis=1, keepdims=True), (tq, LANES))
                alpha = jnp.exp(m_prev - m_new)
                l = alpha * l_sc[h]
                acc = alpha * acc_sc[h]
                for u, s in enumerate(ss):
                    p = jnp.concatenate([jnp.exp(s[:, g * LANES:(g + 1) * LANES] - m_new)
                                         for g in range(tk // LANES)], axis=1)
                    l = l + _lane_halves(p, jnp.add)
                    acc = acc + jnp.dot(p.astype(BF16), kv_ref[rows[u], PAIR_Q:PAIR_KV], preferred_element_type=F32)
                m_sc[h] = m_new
                l_sc[h] = l
                acc_sc[h] = acc

        m_sc[...] = jnp.full(m_sc.shape, NEG_BIG, F32)
        l_sc[...] = jnp.zeros_like(l_sc)
        acc_sc[...] = jnp.zeros_like(acc_sc)

        def loop_body(t, carry):
            trip(t * ATTN_UNROLL, ATTN_UNROLL)
            return carry

        below = qi * per
        lax.fori_loop(0, below // ATTN_UNROLL, loop_body, 0)
        for left in range(0, ATTN_UNROLL, per):
            @pl.when(below % ATTN_UNROLL == left)
            def _(left=left):
                trip(below - left, left + per, n_diagonal=per)

        lane = lax.broadcasted_iota(jnp.int32, (tq, LANES), 1)
        lo = lane < HEAD_DIM
        l = [jnp.sum(l_sc[h], axis=1, keepdims=True) for h in range(2)]
        o = jnp.where(lo, acc_sc[0] / l[0], acc_sc[1] / l[1])
        o_ref[...] = o.astype(BF16)
        ot_ref[...] = o.T.astype(BF16)
        lse = jnp.where(lo, m_sc[0] + jnp.log(l[0]), m_sc[1] + jnp.log(l[1]))
        for r in range(per):
            lse_ref[r] = _head_rows(lse[r * tk:(r + 1) * tk])

    return pl.pallas_call(
        body, name=name, grid=(B, P, nq),
        in_specs=[pl.BlockSpec((tq, PAIR_Q), lambda b, p, i: (b * nq + i, p)),
                  pl.BlockSpec((S, PAIR_KV), lambda b, p, i: (b, p))],
        out_specs=[pl.BlockSpec((tq, LANES), lambda b, p, i: (b * nq + i, p)),
                   pl.BlockSpec((per, None, 8, tk), lambda b, p, i: (b * nq + i, p, 0, 0)),
                   pl.BlockSpec((LANES, tq), lambda b, p, i: (p, b * nq + i))],
        out_shape=[jax.ShapeDtypeStruct((T, P * LANES), BF16), jax.ShapeDtypeStruct((T // tk, P, 8, tk), F32),
                   jax.ShapeDtypeStruct((P * LANES, T), BF16)],
        scratch_shapes=[pltpu.VMEM((2, tq, LANES), F32)] * 3,
        compiler_params=_cparams(("parallel", "parallel", "arbitrary")),
    )(qx, kvx)


def _attn_bwd(qx, kvx, o, lse, do, *, S, ew, name, bias_grad=False):
    T = qx.shape[0]
    P = qx.shape[1] // PAIR_Q
    B = T // S
    tq = _tile(S, ATTN_BLOCK)
    tk = _tile(S, ATTN_K_ROWS)
    nq = S // tq
    nk = S // tk
    per = tk // tq

    def body(q_ref, kv_ref, o_ref, lse_ref, do_ref, dq_ref, dkv_ref, *rest):
        kj = pl.program_id(2)
        if bias_grad:
            csum_ref, rsum_ref, dq_sc, delta_sc, dk_sc, dv_sc, cs_sc = rest
            cs_sc[...] = jnp.zeros_like(cs_sc)

            @pl.when(kj == 0)
            def _():
                rsum_ref[...] = jnp.zeros_like(rsum_ref)
        else:
            dq_sc, delta_sc, dk_sc, dv_sc = rest
        masks = _head_masks(ew)
        lo_q = lax.broadcasted_iota(jnp.int32, (tq, LANES), 1) < HEAD_DIM
        lo = lax.broadcasted_iota(jnp.int32, (tk, LANES), 1) < HEAD_DIM
        vmask = [lo, jnp.logical_not(lo)]

        @pl.when(kj == 0)
        def _():
            dq_sc[...] = jnp.zeros_like(dq_sc)
            for c in range(nq):
                rows = pl.ds(c * tq, tq)
                x = do_ref[rows, :].astype(F32) * o_ref[rows, :].astype(F32)
                r0 = jnp.sum(jnp.where(lo_q, x, 0.0), axis=1, keepdims=True)
                r1 = jnp.sum(jnp.where(lo_q, 0.0, x), axis=1, keepdims=True)
                delta_sc[c] = _head_rows(jnp.where(lo_q, r0, r1))

        k = kv_ref[:, 0:PAIR_Q]
        v = kv_ref[:, PAIR_Q:PAIR_KV]
        kh = [jnp.where(m, k, jnp.zeros_like(k)) for m in masks]
        vh = [jnp.where(m, v, jnp.zeros_like(v)) for m in vmask]
        dk_sc[...] = jnp.zeros_like(dk_sc)
        dv_sc[...] = jnp.zeros_like(dv_sc)

        def step(qi, diagonal):
            rows = pl.ds(pl.multiple_of(qi * tq, tq), tq)
            q = q_ref[rows, :]
            dov = do_ref[rows, :]
            lse8 = lse_ref[qi]
            dl8 = delta_sc[qi]
            for h in range(2):
                st = _dot_nt(kh[h], q)
                if diagonal is not None:
                    key = lax.broadcasted_iota(jnp.int32, st.shape, 0)
                    qry = lax.broadcasted_iota(jnp.int32, st.shape, 1)
                    st = jnp.where(key <= qry + diagonal * tq, st, NEG_BIG)
                pt = jnp.exp(st - lse8[h:h + 1, :])
                dpt = _dot_nt(vh[h], dov)
                dst = pt * (dpt - dl8[h:h + 1, :])
                if bias_grad:
                    cs_sc[h] += _lane_halves(dst, jnp.add)
                    rsum_ref[qi, h:h + 1, :] += jnp.sum(dst, axis=0, keepdims=True)
                ptb = pt.astype(BF16)
                dstb = dst.astype(BF16)
                dv_sc[h] += jnp.dot(ptb, dov, preferred_element_type=F32)
                dk_sc[h] += jnp.dot(dstb, q, preferred_element_type=F32)
                dq_sc[rows, :] += _dot_tn(dstb, kh[h])

        first = kj * per
        above = nq - per - first
        for left in range(0, ATTN_UNROLL, per):
            @pl.when(above % ATTN_UNROLL == left)
            def _(left=left):
                for d in range(per):
                    step(first + d, d)
                for u in range(left):
                    step(first + per + u, None)

        def loop_body(t, carry):
            for u in range(ATTN_UNROLL):
                step(first + per + above % ATTN_UNROLL + t * ATTN_UNROLL + u, None)
            return carry

        lax.fori_loop(0, above // ATTN_UNROLL, loop_body, 0)
        dkv_ref[:, 0:PAIR_Q] = (jnp.where(masks[0], dk_sc[0], 0.0) + jnp.where(masks[1], dk_sc[1], 0.0)).astype(BF16)
        dkv_ref[:, PAIR_Q:PAIR_KV] = jnp.where(lo, dv_sc[0], dv_sc[1]).astype(BF16)
        if bias_grad:
            csum_ref[...] = jnp.where(lo, jnp.sum(cs_sc[0], axis=1, keepdims=True),
                                      jnp.sum(cs_sc[1], axis=1, keepdims=True))

        @pl.when(kj == nk - 1)
        def _():
            dq_ref[...] = dq_sc[...].astype(BF16)

    rows_spec = pl.BlockSpec((nq, None, 8, tq), lambda b, p, j: (b, p, 0, 0))
    out_specs = [pl.BlockSpec((S, PAIR_Q), lambda b, p, j: (b, p)),
                 pl.BlockSpec((tk, PAIR_KV), lambda b, p, j: (b * nk + j, p))]
    out_shape = [jax.ShapeDtypeStruct((T, P * PAIR_Q), BF16), jax.ShapeDtypeStruct((T, P * PAIR_KV), BF16)]
    scratch = [pltpu.VMEM((S, PAIR_Q), F32), pltpu.VMEM((nq, 8, tq), F32),
               pltpu.VMEM((2, tk, PAIR_Q), F32), pltpu.VMEM((2, tk, LANES), F32)]
    if bias_grad:
        out_specs += [pl.BlockSpec((tk, LANES), lambda b, p, j: (b * nk + j, p)), rows_spec]
        out_shape += [jax.ShapeDtypeStruct((T, P * LANES), F32), jax.ShapeDtypeStruct((T // tq, P, 8, tq), F32)]
        scratch.append(pltpu.VMEM((2, tk, LANES), F32))
    return pl.pallas_call(
        body, name=name, grid=(B, P, nk),
        in_specs=[pl.BlockSpec((S, PAIR_Q), lambda b, p, j: (b, p)),
                  pl.BlockSpec((tk, PAIR_KV), lambda b, p, j: (b * nk + j, p)),
                  pl.BlockSpec((S, LANES), lambda b, p, j: (b, p)), rows_spec,
                  pl.BlockSpec((S, LANES), lambda b, p, j: (b, p))],
        out_specs=out_specs, out_shape=out_shape, scratch_shapes=scratch,
        compiler_params=_cparams(("parallel", "parallel", "arbitrary")),
    )(qx, kvx, o, lse, do)


def _fox_consts(P):
    H = 2 * P
    eq = np.zeros((3 * LANES, P * LANES), np.float32)
    ek = np.zeros((3 * LANES, P * LANES), np.float32)
    ones_q = np.zeros((1, P * LANES), np.float32)
    ones_k = np.zeros((1, P * LANES), np.float32)
    for h in range(H):
        base = (h // 2) * LANES + FOX_EXTRA * (h % 2)
        for part in range(3):
            eq[part * LANES + h, base + part] = 1.0
            ones_q[0, base + 3 + part] = 1.0
            ones_k[0, base + part] = 1.0
            ek[part * LANES + h, base + 3 + part] = -1.0
    return eq, ek, ones_q, ones_k


def _split3(f):
    hi = f.astype(BF16)
    r = f - hi.astype(F32)
    mid = r.astype(BF16)
    lo = (r - mid.astype(F32)).astype(BF16)
    return hi, mid, lo


def _tri_sum(tri, x):
    hi, mid, lo = _split3(x)
    return (jnp.dot(tri, hi, preferred_element_type=F32) + jnp.dot(tri, mid, preferred_element_type=F32)
            + jnp.dot(tri, lo, preferred_element_type=F32))


def _log1p_pos(e):
    return jnp.where(e < 0.01, e * (1.0 - e * (0.5 - e * (1.0 / 3.0))), jnp.log(1.0 + e))


def _fox_prep(qkv, fl, b_row, *, S, D, name):
    T = qkv.shape[0]
    P = D // LANES
    B = T // S
    tt = _tile(S, 256)
    per = S // tt
    eq, ek, ones_q, ones_k = _fox_consts(P)
    q_scale = HEAD_DIM ** -0.5

    def body(q_ref, k_ref, v_ref, fl_ref, b_ref, eq_ref, ek_ref, oq_ref, ok_ref, qx_ref, kvx_ref, carry):
        i = pl.program_id(1)

        @pl.when(i == 0)
        def _():
            carry[...] = jnp.zeros_like(carry)

        z = fl_ref[...] + b_ref[...]
        logf = jnp.minimum(z, 0.0) - _log1p_pos(jnp.exp(-jnp.abs(z)))
        row = lax.broadcasted_iota(jnp.int32, (tt, tt), 0)
        col = lax.broadcasted_iota(jnp.int32, (tt, tt), 1)
        tri = (col <= row).astype(BF16)
        f = _tri_sum(tri, logf) + carry[...]
        carry[...] = f[tt - 1:tt, :]
        parts = jnp.concatenate(_split3(f), axis=1)
        xq = jnp.dot(parts, eq_ref[...], preferred_element_type=F32) + oq_ref[...]
        xk = jnp.dot(parts, ek_ref[...], preferred_element_type=F32) + ok_ref[...]
        for p in range(P):
            c = slice(p * LANES, (p + 1) * LANES)
            qx_ref[:, p * PAIR_Q:p * PAIR_Q + LANES] = (q_ref[:, c].astype(F32) * q_scale).astype(BF16)
            qx_ref[:, p * PAIR_Q + LANES:(p + 1) * PAIR_Q] = xq[:, c].astype(BF16)
            kvx_ref[:, p * PAIR_KV:p * PAIR_KV + LANES] = k_ref[:, c]
            kvx_ref[:, p * PAIR_KV + LANES:p * PAIR_KV + PAIR_Q] = xk[:, c].astype(BF16)
            kvx_ref[:, p * PAIR_KV + PAIR_Q:(p + 1) * PAIR_KV] = v_ref[:, c]

    tok = lambda b, i: (b * per + i, 0)
    const = lambda b, i: (0, 0)
    return pl.pallas_call(
        body, name=name, grid=(B, per),
        in_specs=[pl.BlockSpec((tt, D), lambda b, i: (b * per + i, 0)),
                  pl.BlockSpec((tt, D), lambda b, i: (b * per + i, 1)),
                  pl.BlockSpec((tt, D), lambda b, i: (b * per + i, 2)),
                  pl.BlockSpec((tt, LANES), tok), pl.BlockSpec((1, LANES), const),
                  pl.BlockSpec(eq.shape, const), pl.BlockSpec(ek.shape, const),
                  pl.BlockSpec(ones_q.shape, const), pl.BlockSpec(ones_k.shape, const)],
        out_specs=[pl.BlockSpec((tt, P * PAIR_Q), tok), pl.BlockSpec((tt, P * PAIR_KV), tok)],
        out_shape=[jax.ShapeDtypeStruct((T, P * PAIR_Q), BF16), jax.ShapeDtypeStruct((T, P * PAIR_KV), BF16)],
        scratch_shapes=[pltpu.VMEM((1, LANES), F32)],
        compiler_params=_cparams(("arbitrary", "arbitrary")),
    )(qkv, qkv, qkv, fl, b_row, jnp.asarray(eq, BF16), jnp.asarray(ek, BF16), jnp.asarray(ones_q), jnp.asarray(ones_k))


def _fox_unprep(dqx, dkvx, csum, rsum, fl, b_row, *, S, D, name):
    T = dqx.shape[0]
    P = D // LANES
    B = T // S
    tt = _tile(S, 256)
    per = S // tt
    q_scale = HEAD_DIM ** -0.5

    def body(dq_ref, dkv_ref, cs_ref, rs_ref, fl_ref, b_ref, dqkv_ref, db_ref, carry):
        b = pl.program_id(0)
        i = pl.program_id(1)

        @pl.when(i == 0)
        def _():
            carry[...] = jnp.zeros_like(carry)

        @pl.when((i == 0) & (b == 0))
        def _():
            db_ref[...] = jnp.zeros_like(db_ref)

        df = rs_ref[...] - cs_ref[...]
        for p in range(P):
            rq = slice(p * LANES, (p + 1) * LANES)
            dqkv_ref[:, rq] = (dq_ref[:, p * PAIR_Q:p * PAIR_Q + LANES].astype(F32) * q_scale).astype(BF16)
            dqkv_ref[:, D + p * LANES:D + (p + 1) * LANES] = dkv_ref[:, p * PAIR_KV:p * PAIR_KV + LANES]
            dqkv_ref[:, 2 * D + p * LANES:2 * D + (p + 1) * LANES] = dkv_ref[:, p * PAIR_KV + PAIR_Q:(p + 1) * PAIR_KV]
        row = lax.broadcasted_iota(jnp.int32, (tt, tt), 0)
        col = lax.broadcasted_iota(jnp.int32, (tt, tt), 1)
        tri = (col >= row).astype(BF16)
        dlogf = _tri_sum(tri, df) + carry[...]
        carry[...] = dlogf[0:1, :]
        z = fl_ref[...] + b_ref[...]
        e = jnp.exp(-jnp.abs(z))
        sig_neg = jnp.where(z >= 0.0, e, 1.0) / (1.0 + e)
        dfl = dlogf * sig_neg
        dqkv_ref[:, 3 * D:3 * D + LANES] = dfl.astype(BF16)
        db_ref[...] += jnp.sum(dfl, axis=0, keepdims=True)

    rev = lambda b, i: (b * per + per - 1 - i, 0)
    const = lambda b, i: (0, 0)
    return pl.pallas_call(
        body, name=name, grid=(B, per),
        in_specs=[pl.BlockSpec((tt, P * PAIR_Q), rev), pl.BlockSpec((tt, P * PAIR_KV), rev),
                  pl.BlockSpec((tt, LANES), rev), pl.BlockSpec((tt, LANES), rev), pl.BlockSpec((tt, LANES), rev),
                  pl.BlockSpec((1, LANES), const)],
        out_specs=[pl.BlockSpec((tt, 3 * D + LANES), rev), pl.BlockSpec((1, LANES), const)],
        out_shape=[jax.ShapeDtypeStruct((T, 3 * D + LANES), BF16), jax.ShapeDtypeStruct((1, LANES), F32)],
        scratch_shapes=[pltpu.VMEM((1, LANES), F32)],
        compiler_params=_cparams(("arbitrary", "arbitrary")),
    )(dqx, dkvx, csum, rsum, fl, b_row)


def _rms(x):
    r = lax.rsqrt(jnp.mean(x * x, axis=-1, keepdims=True) + NORM_EPS)
    return x * r, r


def _mla_mid(lat, gq, gkv, cos_t, sin_s, *, name):
    T, W = lat.shape
    Rq = W - 2 * LANES
    tt = _tile(T, 512)

    def body(l_ref, gq_ref, gkv_ref, c_ref, s_ref, o_ref, ot_ref):
        nq, _ = _rms(l_ref[:, 0:Rq])
        nkv, _ = _rms(l_ref[:, Rq:Rq + LANES])
        parts = [nq * gq_ref[...], nkv * gkv_ref[...], _rope128(l_ref[:, Rq + LANES:W], c_ref[...], s_ref[...])]
        out = jnp.concatenate(parts, axis=1)
        o_ref[...] = out.astype(BF16)
        ot_ref[...] = out.T.astype(BF16)

    return pl.pallas_call(
        body, name=name, grid=(T // tt,),
        in_specs=[pl.BlockSpec((tt, W), lambda i: (i, 0)), pl.BlockSpec((1, Rq), lambda i: (0, 0)),
                  pl.BlockSpec((1, LANES), lambda i: (0, 0)), pl.BlockSpec((tt, LANES), lambda i: (i, 0)),
                  pl.BlockSpec((tt, LANES), lambda i: (i, 0))],
        out_specs=[pl.BlockSpec((tt, W), lambda i: (i, 0)), pl.BlockSpec((W, tt), lambda i: (0, i))],
        out_shape=[jax.ShapeDtypeStruct((T, W), BF16), jax.ShapeDtypeStruct((W, T), BF16)],
        compiler_params=_cparams(("parallel",)),
    )(lat, gq, gkv, cos_t, sin_s)


def _mla_mid_bwd(lat, dcq, dckr, gq, gkv, cos_t, sin_s, *, name):
    T, W = lat.shape
    Rq = W - 2 * LANES
    tt = _tile(T, 512)

    def norm_bwd(x, dy, g):
        n, r = _rms(x)
        dn = dy * g
        return r * (dn - n * jnp.mean(dn * n, axis=-1, keepdims=True)), jnp.sum(dy * n, axis=0, keepdims=True)

    def body(l_ref, dq_ref, dk_ref, gq_ref, gkv_ref, c_ref, s_ref, o_ref, dgq_ref, dgkv_ref):
        i = pl.program_id(0)

        @pl.when(i == 0)
        def _():
            dgq_ref[...] = jnp.zeros_like(dgq_ref)
            dgkv_ref[...] = jnp.zeros_like(dgkv_ref)

        dxq, dgq = norm_bwd(l_ref[:, 0:Rq], dq_ref[...], gq_ref[...])
        dxkv, dgkv = norm_bwd(l_ref[:, Rq:Rq + LANES], dk_ref[:, 0:LANES], gkv_ref[...])
        o_ref[:, 0:Rq] = dxq.astype(BF16)
        o_ref[:, Rq:Rq + LANES] = dxkv.astype(BF16)
        o_ref[:, Rq + LANES:W] = _rope128(dk_ref[:, LANES:2 * LANES], c_ref[...], -s_ref[...]).astype(BF16)
        dgq_ref[...] += dgq
        dgkv_ref[...] += dgkv

    return pl.pallas_call(
        body, name=name, grid=(T // tt,),
        in_specs=[pl.BlockSpec((tt, W), lambda i: (i, 0)), pl.BlockSpec((tt, Rq), lambda i: (i, 0)),
                  pl.BlockSpec((tt, 2 * LANES), lambda i: (i, 0)), pl.BlockSpec((1, Rq), lambda i: (0, 0)),
                  pl.BlockSpec((1, LANES), lambda i: (0, 0)), pl.BlockSpec((tt, LANES), lambda i: (i, 0)),
                  pl.BlockSpec((tt, LANES), lambda i: (i, 0))],
        out_specs=[pl.BlockSpec((tt, W), lambda i: (i, 0)), pl.BlockSpec((1, Rq), lambda i: (0, 0)),
                   pl.BlockSpec((1, LANES), lambda i: (0, 0))],
        out_shape=[jax.ShapeDtypeStruct((T, W), BF16), jax.ShapeDtypeStruct((1, Rq), F32),
                   jax.ShapeDtypeStruct((1, LANES), F32)],
        compiler_params=_cparams(("arbitrary",)),
    )(lat, dcq, dckr, gq, gkv, cos_t, sin_s)


def _uq_to_pairs(w):
    Rq = w.shape[0]
    P = w.shape[1] // (2 * (HEAD_DIM + ROPE_DIM))
    w4 = w.reshape(Rq, P, 2, HEAD_DIM + ROPE_DIM)
    nope = w4[..., :HEAD_DIM].reshape(Rq, P, 2 * HEAD_DIM)
    rope = w4[..., HEAD_DIM:].reshape(Rq, P, 2 * ROPE_DIM)
    pad = jnp.zeros((Rq, P, PAIR_Q - 2 * HEAD_DIM - 2 * ROPE_DIM), w.dtype)
    return jnp.concatenate([nope, rope, pad], axis=-1).reshape(Rq, P * PAIR_Q)


def _uq_from_pairs(g):
    Rq = g.shape[0]
    P = g.shape[1] // PAIR_Q
    g3 = g.reshape(Rq, P, PAIR_Q)
    nope = g3[..., :2 * HEAD_DIM].reshape(Rq, P, 2, HEAD_DIM)
    rope = g3[..., 2 * HEAD_DIM:2 * HEAD_DIM + 2 * ROPE_DIM].reshape(Rq, P, 2, ROPE_DIM)
    return jnp.concatenate([nope, rope], axis=-1).reshape(Rq, P * 2 * (HEAD_DIM + ROPE_DIM))


def _ukv_to_pairs(w):
    P = w.shape[1] // (4 * HEAD_DIM)
    w4 = w.reshape(KV_RANK, P, 2, 2 * HEAD_DIM)
    kn = w4[..., :HEAD_DIM].reshape(KV_RANK, P, 2 * HEAD_DIM)
    vv = w4[..., HEAD_DIM:].reshape(KV_RANK, P, 2 * HEAD_DIM)
    top = jnp.concatenate([kn, jnp.zeros((KV_RANK, P, LANES), w.dtype), vv], axis=-1)
    place = np.zeros((LANES, P, PAIR_KV), np.float32)
    for r in range(ROPE_DIM):
        place[r, :, LANES + r] = 1.0
        place[r, :, LANES + ROPE_DIM + r] = 1.0
    return jnp.concatenate([top, jnp.asarray(place, w.dtype)], axis=0).reshape(KV_RANK + LANES, P * PAIR_KV)


def _ukv_from_pairs(g):
    P = g.shape[1] // PAIR_KV
    g3 = g[:KV_RANK].reshape(KV_RANK, P, PAIR_KV)
    kn = g3[..., :2 * HEAD_DIM].reshape(KV_RANK, P, 2, HEAD_DIM)
    vv = g3[..., PAIR_Q:].reshape(KV_RANK, P, 2, HEAD_DIM)
    return jnp.concatenate([kn, vv], axis=-1).reshape(KV_RANK, P * 4 * HEAD_DIM)


def _residual_then_norm(acc, xr, g, gain, sc, sh):
    x_out = xr + g * acc
    r = lax.rsqrt(jnp.mean(x_out * x_out, axis=-1, keepdims=True) + NORM_EPS)
    h = (x_out * r) * gain * (1.0 + sc) + sh
    return x_out, acc, h, h


def _gated_out(a, w_stack, layer, x, gate, next_norm, *, S, name):
    if next_norm is None:
        return _mm(a, w_stack, "nn", name=name, b_layer=layer, out_dtypes=(F32, BF16), extras=(x,), rowvecs=(gate,),
                   seq=S, epilogue=lambda acc, xr, g: (xr + g * acc, acc)) + (None, None)
    long_k = a.shape[1] > 2048
    return _mm(a, w_stack, "nn", name=name, b_layer=layer, out_dtypes=(F32, BF16, BF16, BF16),
               out_t=(False, False, False, True), extras=(x,), rowvecs=(gate,) + tuple(next_norm), seq=S,
               full_rows=True, tk=a.shape[1], epilogue=_residual_then_norm,
               vmem_budget=MM_VMEM_BUDGET + (8 * 1024 * 1024 if long_k else 0))


def _mlp_fwd(h2, w, i, x1, gate, next_norm, *, S):
    def act(acc):
        u = jnp.square(jnp.maximum(acc, 0.0))
        return acc, u, u

    p, u, u_t = _mm(h2, w["mlp_w1"], "nn", name=f"mlp_up_{i}", b_layer=i, out_dtypes=(BF16, BF16, BF16),
                    out_t=(False, False, True), epilogue=act)
    x2, z, h, h_t = _gated_out(u, w["mlp_w2"], i, x1, gate, next_norm, S=S, name=f"mlp_down_{i}")
    return x2, (p, u_t, z), h, h_t


STACKED_GRADS = ("fox_out", "mla_down", "mla_uq", "mla_ukv", "mla_out", "mlp_w1", "mlp_w2")


def _local_step(x, target, pos_f, inv_freq_row, sign_row, mod, w, slots, *, S, hooks=None):
    hooks = hooks or {}
    T, D = x.shape
    L = mod.shape[0]
    L2 = len(w["fox_out"])
    cos_t, sin_s = _rope_tables(pos_f, inv_freq_row, sign_row)
    saved = []
    B = mod.shape[2]

    def per_sequence(gain):
        return jnp.broadcast_to(gain[None], (B,) + gain.shape)

    h, h_t = _norm_mod(x, w["norm_mix_g"][0], mod[0, 1], mod[0, 0], S=S, name="norm_mix_0")
    for i in range(L):
        j = i // 2
        sh_m, sc_m, g_m, sh_f, sc_f, g_f = (mod[i, s] for s in range(6))
        if i % 2 == 0:
            qkv = _mm(h, w["fox_qkv"], "nn", name=f"fox_qkv_{i}", b_layer=j, out_dtypes=(BF16,))
            fl = _mm(h, w["fox_f"], "nn", name=f"fox_f_{i}", b_layer=j)
            qx, kvx = _fox_prep(qkv, fl, w["fox_b"][j], S=S, D=D, name=f"fox_prep_{i}")
            o, lse, o_t = _attn_fwd(qx, kvx, S=S, ew=FOX_EXTRA, name=f"fox_attn_{i}")
            mix = (qx, kvx, o, lse, o_t, fl)
            w_out = w["fox_out"]
        else:
            lat = _mm(h, w["mla_down"], "nn", name=f"mla_down_{i}", b_layer=j)
            Rq = lat.shape[1] - 2 * LANES
            cqr, cqr_t = _mla_mid(lat, w["mla_gq"][j], w["mla_gkv"][j], cos_t, sin_s, name=f"mla_mid_{i}")
            qx = _mm(cqr, w["mla_uq"], "nn", name=f"mla_uq_{i}", b_layer=j, out_dtypes=(BF16,), a_sz=Rq, tk=Rq,
                     tables=(cos_t, sin_s), epilogue=lambda acc, c, s: (_rope_pairs(acc * MLA_SCALE, c, s, 1.0),))
            kvx = _mm(cqr, w["mla_ukv"], "nn", name=f"mla_ukv_{i}", b_layer=j, out_dtypes=(BF16,), a_off=Rq,
                      a_sz=2 * LANES, tk=2 * LANES, tn=PAIR_KV)
            o, lse, o_t = _attn_fwd(qx, kvx, S=S, ew=ROPE_DIM, name=f"mla_attn_{i}")
            mix = (qx, kvx, o, lse, o_t, lat, cqr_t)
            w_out = w["mla_out"]
        x1, y, h2, h2_t = _gated_out(o, w_out, j, x, g_m, (per_sequence(w["norm_mlp_g"][i]), sc_f, sh_f), S=S,
                                     name=f"mix_out_{i}")
        if i == 0 and "fwd_mlp0" in hooks:
            w = hooks["fwd_mlp0"](x1, w)
        next_norm = (per_sequence(w["norm_mix_g"][i + 1]), mod[i + 1, 1], mod[i + 1, 0]) if i + 1 < L else None
        x2, mlp, h_next, h_next_t = _mlp_fwd(h2, w, i, x1, g_f, next_norm, S=S)
        saved.append((x, h_t, mix, y, x1, h2_t, mlp))
        x, h, h_t = x2, h_next, h_next_t
        if i == 0 and "fwd_layer1" in hooks:
            w = hooks["fwd_layer1"](x, w)

    dx, dg_final, loss = _final_loss(x, target, w["final_norm_g"])
    n_split = w["mlp_w1"][0][0].shape[1]

    grads = {k: [None] * len(w[k]) for k in ("norm_mix_g", "norm_mlp_g", "fox_b", "mla_gq", "mla_gkv")}
    grads["fox_in"] = [None] * L2
    grads.update({k: {} for k in STACKED_GRADS})
    grads["final_norm_g"] = dg_final

    def stacked(key, layer, _, a_t, b, **kw):
        group, idx, count = slots[(key, layer)]
        grads[key][group] = _mm(a_t, b, "nn", out_stack=(grads[key].get(group), idx, count), **kw)

    dmod = [None] * L
    for i in reversed(range(L)):
        j = i // 2
        x0, h_t, mix, y, x1, h2_t, (p, u_t, z) = saved[i]
        sh_m, sc_m, g_m, sh_f, sc_f, g_f = (mod[i, s] for s in range(6))
        if i == 0 and "bwd_layer0" in hooks:
            g_f = g_f + hooks["bwd_layer0"](grads)[0, 0]
        dz, dg_f = _gate_bwd(dx, z, g_f, S=S, name=f"gate_mlp_bwd_{i}")
        stacked("mlp_w2", i, L, u_t, dz, name=f"mlp_w2_grad_{i}")
        dp = _mm(dz, w["mlp_w2"], "nt", name=f"mlp_down_bwd_{i}", b_layer=i, out_dtypes=(BF16,), extras=(p,),
                 epilogue=lambda acc, pv: (acc * (2.0 * jnp.maximum(pv.astype(F32), 0.0)),))
        stacked("mlp_w1", i, L, h2_t, dp, name=f"mlp_w1_grad_{i}", out_split=n_split)
        if i == 0 and "bwd_mix0" in hooks:
            g_m = g_m + hooks["bwd_mix0"](grads)[0, 0]
        dh2 = _mm(dp, w["mlp_w1"], "nt", name=f"mlp_up_bwd_{i}", b_layer=i, out_dtypes=(BF16,))
        dx1, dsh_f, dsc_f, dgn = _norm_mod_bwd(x1, dh2, dx, w["norm_mlp_g"][i], sc_f, S=S, name=f"norm_mlp_bwd_{i}")
        grads["norm_mlp_g"][i] = dgn
        dy, dg_m = _gate_bwd(dx1, y, g_m, S=S, name=f"gate_mix_bwd_{i}")
        if i % 2 == 0:
            qx, kvx, o, lse, o_t, fl = mix
            stacked("fox_out", j, L2, o_t, dy, name=f"fox_out_grad_{i}")
            do = _mm(dy, w["fox_out"], "nt", name=f"fox_out_bwd_{i}", b_layer=j, out_dtypes=(BF16,))
            dqx, dkvx, csum, rsum = _attn_bwd(qx, kvx, o, lse, do, S=S, ew=FOX_EXTRA, name=f"fox_attn_bwd_{i}",
                                              bias_grad=True)
            n_heads = D // HEAD_DIM
            csum = jnp.pad(csum.reshape(T, n_heads, HEAD_DIM)[:, :, 0], ((0, 0), (0, LANES - n_heads)))
            rsum = jnp.transpose(rsum[:, :, :2, :], (0, 3, 1, 2)).reshape(T, n_heads)
            rsum = jnp.pad(rsum, ((0, 0), (0, LANES - n_heads)))
            dproj, db = _fox_unprep(dqx, dkvx, csum, rsum, fl, w["fox_b"][j], S=S, D=D, name=f"fox_unprep_{i}")
            grads["fox_b"][j] = db
            grads["fox_in"][j] = _mm(h_t, dproj, "nn", name=f"fox_in_grad_{i}")
            dh = _mm(dproj, w["fox_in"], "nt", name=f"fox_in_bwd_{i}", b_layer=j, out_dtypes=(BF16,),
                     tk=dproj.shape[1])
        else:
            qx, kvx, o, lse, o_t, lat, cqr_t = mix
            Rq = lat.shape[1] - 2 * LANES
            stacked("mla_out", j, L2, o_t, dy, name=f"mla_out_grad_{i}")
            do = _mm(dy, w["mla_out"], "nt", name=f"mla_out_bwd_{i}", b_layer=j, out_dtypes=(BF16,))
            dqx, dkvx = _attn_bwd(qx, kvx, o, lse, do, S=S, ew=ROPE_DIM, name=f"mla_attn_bwd_{i}")
            dqpre = _unrope(dqx, cos_t, sin_s)
            stacked("mla_uq", j, L2, cqr_t[:Rq], dqpre, name=f"mla_uq_grad_{i}", out_split=n_split)
            stacked("mla_ukv", j, L2, cqr_t[Rq:], dkvx, name=f"mla_ukv_grad_{i}", tn=PAIR_KV, out_split=n_split)
            dcq = _mm(dqpre, w["mla_uq"], "nt", name=f"mla_uq_bwd_{i}", b_layer=j)
            dckr = _mm(dkvx, w["mla_ukv"], "nt", name=f"mla_ukv_bwd_{i}", b_layer=j, tk=PAIR_KV * 2)
            dlat, dgq, dgkv = _mla_mid_bwd(lat, dcq, dckr, w["mla_gq"][j], w["mla_gkv"][j], cos_t, sin_s,
                                           name=f"mla_mid_bwd_{i}")
            grads["mla_gq"][j] = dgq
            grads["mla_gkv"][j] = dgkv
            stacked("mla_down", j, L2, h_t, dlat, name=f"mla_down_grad_{i}")
            dh = _mm(dlat, w["mla_down"], "nt", name=f"mla_down_bwd_{i}", b_layer=j, out_dtypes=(BF16,))
        dx, dsh_m, dsc_m, dgn = _norm_mod_bwd(x0, dh, dx1, w["norm_mix_g"][i], sc_m, S=S, name=f"norm_mix_bwd_{i}")
        grads["norm_mix_g"][i] = dgn
        dmod[i] = jnp.stack([dsh_m, dsc_m, dg_m, dsh_f, dsc_f, dg_f])
    return loss, dx, jnp.stack(dmod), grads


GATHERED = ("fox_in", "fox_out", "mla_down", "mla_uq", "mla_ukv", "mla_out", "mlp_w1", "mlp_w2")
ROW_SHARDED = ("fox_out", "mla_down", "mla_out", "mlp_w2")


def _shard_layouts(wts):
    dkv = wts["mla_w_dkv"]
    dkv = jnp.pad(dkv, ((0, 0), (0, 0), (0, 2 * LANES - dkv.shape[2])))
    return {
        "fox_in": _pad_lanes(wts["fox_w_in"].astype(BF16)),
        "fox_out": wts["fox_w_out"].astype(BF16),
        "mla_down": jnp.concatenate([wts["mla_w_dq"], dkv], axis=2).astype(BF16),
        "mla_uq": jax.vmap(_uq_to_pairs)(wts["mla_w_uq"].astype(BF16)),
        "mla_ukv": jax.vmap(_ukv_to_pairs)(wts["mla_w_ukv"].astype(BF16)),
        "mla_out": wts["mla_w_out"].astype(BF16),
        "mlp_w1": wts["mlp_w1"].astype(BF16),
        "mlp_w2": wts["mlp_w2"].astype(BF16),
    }


def _small_layouts(small):
    return {
        "fox_b": [jnp.pad(b, (0, LANES - b.shape[0]))[None, :] for b in small["fox_b_f"]],
        "mla_gq": [g[None, :] for g in small["mla_q_norm_g"]],
        "mla_gkv": [g[None, :] for g in small["mla_kv_norm_g"]],
        "norm_mix_g": [g[None, :] for g in small["norm_mix_g"]],
        "norm_mlp_g": [g[None, :] for g in small["norm_mlp_g"]],
        "final_norm_g": small["final_norm_g"][None, :],
    }


def _comm_groups(L, L2):
    rest = [("fox_in", 1, L2 - 1), ("fox_out", 1, L2 - 1), ("mla_down", 0, L2), ("mla_uq", 0, L2),
            ("mla_ukv", 0, L2), ("mla_out", 0, L2), ("mlp_w1", 1, L - 1), ("mlp_w2", 1, L - 1)]
    return {"mix0": [("fox_in", 0, 1), ("fox_out", 0, 1)], "mlp0": [("mlp_w1", 0, 1), ("mlp_w2", 0, 1)],
            "rest": [e for e in rest if e[2] > 0]}


def _layer_slots(groups):
    return {(n, s + l): (g, l, cnt) for g, entries in groups.items() for n, s, cnt in entries for l in range(cnt)}


def _pad_lanes(a):
    cols = a.shape[-1]
    return jnp.pad(a, [(0, 0)] * (a.ndim - 1) + [(0, -cols % LANES)])


def _weight_views(name, gathered, D, n_fox_heads):
    n, ns, rows, cols = gathered.shape
    if name == "fox_in":
        true_cols = (3 * D + n_fox_heads) // ns
        fox = jnp.concatenate([gathered[:, k, :, :true_cols] for k in range(ns)], axis=-1)
        return {"fox_qkv": fox[:, :, :3 * D], "fox_f": _pad_lanes(fox[:, :, 3 * D:]), "fox_in": _pad_lanes(fox)}
    if name in ROW_SHARDED:
        return {name: gathered.reshape(n, ns * rows, cols)}
    return {name: gathered}


def _grad_pieces(name, g, qkv_f, n_fox_heads, ns):
    if name == "fox_in":
        D = qkv_f[0].shape[0]
        fox = jnp.stack([a[:, :3 * D + n_fox_heads] for a in qkv_f])
        cols = fox.shape[2] // ns
        return jnp.stack([_pad_lanes(fox[:, :, k * cols:(k + 1) * cols]) for k in range(ns)], axis=1)
    if name in ROW_SHARDED:
        return g.reshape(g.shape[0], ns, g.shape[1] // ns, g.shape[2])
    return g


def _small_grads(g, n_fox_heads):
    return {
        "norm_mix_g": jnp.concatenate(g["norm_mix_g"], axis=0),
        "norm_mlp_g": jnp.concatenate(g["norm_mlp_g"], axis=0),
        "final_norm_g": g["final_norm_g"][0],
        "fox_b_f": jnp.concatenate(g["fox_b"], axis=0)[:, :n_fox_heads],
        "mla_q_norm_g": jnp.concatenate(g["mla_gq"], axis=0),
        "mla_kv_norm_g": jnp.concatenate(g["mla_gkv"], axis=0),
    }


def _silu(c):
    return c * (1.0 / (1.0 + jnp.exp(-c)))


def _ada_fwd(c_all, ada_w, ada_b_cols):
    L, D, C = ada_w.shape
    Bg = c_all.shape[0]
    tc = _tile(C, 512)

    def body(c_ref, w_ref, b_ref, o_ref):
        ca = _silu(c_ref[...]).astype(BF16)
        o_ref[...] = jnp.dot(ca, w_ref[...].astype(BF16), preferred_element_type=F32) + b_ref[...]

    return pl.pallas_call(
        body, name="ada_fwd", grid=(L, C // tc),
        in_specs=[pl.BlockSpec((Bg, D), lambda l, j: (0, 0)), pl.BlockSpec((None, D, tc), lambda l, j: (l, 0, j)),
                  pl.BlockSpec((None, 1, tc), lambda l, j: (l, 0, j))],
        out_specs=pl.BlockSpec((None, Bg, tc), lambda l, j: (l, 0, j)),
        out_shape=jax.ShapeDtypeStruct((L, Bg, C), F32),
        compiler_params=_cparams(("parallel", "parallel")),
    )(c_all, ada_w, ada_b_cols)


def _ada_bwd(c_all, dmod_cols):
    L, Bg, C = dmod_cols.shape
    D = c_all.shape[1]
    tc = _tile(C, 512)

    def body(c_ref, d_ref, o_ref):
        ca = _silu(c_ref[...]).astype(BF16)
        o_ref[...] = _dot_tn(ca, d_ref[...].astype(BF16))

    return pl.pallas_call(
        body, name="ada_bwd", grid=(L, C // tc),
        in_specs=[pl.BlockSpec((Bg, D), lambda l, j: (0, 0)), pl.BlockSpec((None, Bg, tc), lambda l, j: (l, 0, j))],
        out_specs=pl.BlockSpec((None, D, tc), lambda l, j: (l, 0, j)),
        out_shape=jax.ShapeDtypeStruct((L, D, C), F32),
        compiler_params=_cparams(("parallel", "parallel")),
    )(c_all, dmod_cols)


def _adamw_update(w, gv, m, v):
    mn = ADAM_B1 * m + (1.0 - ADAM_B1) * gv
    vn = ADAM_B2 * v + (1.0 - ADAM_B2) * jnp.square(gv)
    m_hat = mn / (1.0 - ADAM_B1 ** ADAM_STEP)
    v_hat = vn / (1.0 - ADAM_B2 ** ADAM_STEP)
    return -ADAM_LR * (m_hat / (jnp.sqrt(v_hat) + ADAM_EPS) + ADAM_WD * w), mn, vn


def _adamw(w, g, m, v, *, name):
    shape = w.shape
    C = shape[-1]
    R = int(np.prod(shape[:-1])) if len(shape) > 1 else 1
    w2, g2, m2, v2 = (a.reshape(R, C) for a in (w, g, m, v))
    tr = _row_tile(R, C)

    def body(w_ref, g_ref, m_ref, v_ref, d_ref, nm_ref, nv_ref):
        d_ref[...], nm_ref[...], nv_ref[...] = _adamw_update(w_ref[...], g_ref[...], m_ref[...], v_ref[...])

    spec = pl.BlockSpec((tr, C), lambda i: (i, 0))
    out = pl.pallas_call(
        body, name=name, grid=(R // tr,), in_specs=[spec] * 4, out_specs=[spec] * 3,
        out_shape=[jax.ShapeDtypeStruct((R, C), F32)] * 3, compiler_params=_cparams(("parallel",)),
    )(w2, g2, m2, v2)
    return tuple(a.reshape(shape) for a in out)


def _adamw_halves(w, g_own, g_peer, m, v, c_idx, *, name):
    L, rows, C = w.shape
    R = rows // 2
    tr = _row_tile(R, C)

    def body(c_ref, w_ref, go_ref, gp_ref, m_ref, v_ref, g_ref, d_ref, nm_ref, nv_ref):
        gv = jnp.where(pl.program_id(1) == c_ref[0], go_ref[...], gp_ref[...])
        g_ref[...] = gv
        d_ref[...], nm_ref[...], nv_ref[...] = _adamw_update(w_ref[...], gv, m_ref[...], v_ref[...])

    full = pl.BlockSpec((None, None, tr, C), lambda l, hh, i, c_ref: (l, hh, i, 0))
    half = pl.BlockSpec((None, tr, C), lambda l, hh, i, c_ref: (l, i, 0))
    grid_spec = pltpu.PrefetchScalarGridSpec(
        num_scalar_prefetch=1, grid=(L, 2, R // tr), in_specs=[full, half, half, full, full], out_specs=[full] * 4)
    split = lambda a: a.reshape(L, 2, R, C)
    out = pl.pallas_call(
        body, name=name, grid_spec=grid_spec, out_shape=[jax.ShapeDtypeStruct((L, 2, R, C), F32)] * 4,
        compiler_params=_cparams(("parallel", "parallel", "parallel")),
    )(c_idx, split(w), g_own, g_peer, split(m), split(v))
    return tuple(a.reshape(w.shape) for a in out)


def _sum_gathered(dm8, sm8):
    n_dev, Bl, R, D = dm8.shape
    Rs = sm8.shape[1]

    def body(dm_ref, sm_ref, ob_ref, os_ref):
        acc_b = jnp.zeros((R, D), F32)
        acc_s = jnp.zeros((Rs, D), F32)
        for d in range(n_dev):
            for b in range(Bl):
                acc_b = acc_b + dm_ref[d, b]
            acc_s = acc_s + sm_ref[d]
        ob_ref[...] = acc_b
        os_ref[...] = acc_s

    return pl.pallas_call(
        body, name="sum_gathered",
        out_shape=[jax.ShapeDtypeStruct((R, D), F32), jax.ShapeDtypeStruct((Rs, D), F32)],
        compiler_params=_cparams(None),
    )(dm8, sm8)


N_DEV = 8
N_CHIP = 4
ANY = pl.BlockSpec(memory_space=pl.ANY)
HBM = pl.BlockSpec(memory_space=pltpu.HBM)
SEM = pl.BlockSpec(memory_space=pltpu.SEMAPHORE)
DATAFLOW = pltpu.SideEffectType.DATAFLOW_SIDE_EFFECTING


def _mesh_pos():
    return lax.axis_index("x"), lax.axis_index("y"), lax.axis_index("c")


def _all_gather8(block, *, name, in_vmem):
    R, W = block.shape

    def body(x_ref, out_ref, send_sems, recv_sems, local_sem):
        x, y, c = _mesh_pos()
        me, sibling = (x, y, c), (x, y, 1 - c)
        chips = [(1 - x, y), (x, 1 - y), (1 - x, 1 - y)]

        def slot(px, py, pc):
            return out_ref.at[4 * px + 2 * py + pc]

        def copy(k, blk, to, src=None):
            return pltpu.make_async_remote_copy(
                src_ref=slot(*blk) if src is None else src, dst_ref=slot(*blk),
                send_sem=send_sems.at[k], recv_sem=recv_sems.at[k], device_id=to, device_id_type=MESH_ID)

        mine = pltpu.make_async_copy(x_ref, slot(*me), local_sem)
        mine.start()
        first = [copy(0, me, sibling, src=x_ref)]
        first += [copy(1 + j, me, (*chip, c), src=x_ref) for j, chip in enumerate(chips)]
        for cp in first:
            cp.start()
        passed = [copy(4 + j, (*chip, c), sibling) for j, chip in enumerate(chips)]
        for j, chip in enumerate(chips):
            copy(1 + j, (*chip, c), me).wait_recv()
            passed[j].start()
        copy(0, sibling, me).wait_recv()
        for j, chip in enumerate(chips):
            copy(4 + j, (*chip, 1 - c), me).wait_recv()
        for cp in first + passed:
            cp.wait_send()
        mine.wait()

    space = pl.BlockSpec(memory_space=pltpu.VMEM) if in_vmem else ANY
    return pl.pallas_call(
        body, name=name, out_shape=jax.ShapeDtypeStruct((N_DEV, R, W), block.dtype),
        in_specs=[space], out_specs=space,
        scratch_shapes=[pltpu.SemaphoreType.DMA((7,)), pltpu.SemaphoreType.DMA((7,)), pltpu.SemaphoreType.DMA],
        compiler_params=pltpu.CompilerParams(vmem_limit_bytes=VMEM_LIMIT_V7X),
    )(block)


def _comm_call(body, arrays, out_shapes, n_sems, *, name):
    return pl.pallas_call(
        body, name=name, out_shape=out_shapes, in_specs=[ANY] * len(arrays), out_specs=[ANY] * len(out_shapes),
        scratch_shapes=[pltpu.SemaphoreType.DMA((n_sems,)), pltpu.SemaphoreType.DMA((n_sems,)),
                        pltpu.SemaphoreType.DMA((len(arrays),))],
    )(*arrays)


def _gather_weights(shards, *, name):
    n = len(shards)

    def body(*refs):
        xs, outs = refs[:n], refs[n:2 * n]
        send_sems, recv_sems, local_sems = refs[2 * n:]
        x, y, c = _mesh_pos()
        me, sibling = (x, y, c), (x, y, 1 - c)
        chips = [(1 - x, y), (x, 1 - y), (1 - x, 1 - y)]
        waits = []
        for i in range(n):
            nl = shards[i].shape[0]
            own = xs[i].at[pl.ds(0, nl), c]

            def slot(px, py, pc, i=i, nl=nl):
                return outs[i].at[pl.ds(0, nl), 2 * px + py, pc]

            def copy(k, blk, to, src=None, i=i, slot=slot):
                return pltpu.make_async_remote_copy(
                    src_ref=slot(*blk) if src is None else src, dst_ref=slot(*blk),
                    send_sem=send_sems.at[7 * i + k], recv_sem=recv_sems.at[7 * i + k], device_id=to,
                    device_id_type=MESH_ID)

            mine = pltpu.make_async_copy(own, slot(*me), local_sems.at[i])
            mine.start()
            first = [copy(0, me, sibling, src=own)]
            first += [copy(1 + j, me, (*chip, c), src=own) for j, chip in enumerate(chips)]
            for cp in first:
                cp.start()
            waits.append((copy, mine, first))
        for copy, mine, first in waits:
            passed = [copy(4 + j, (*chip, c), sibling) for j, chip in enumerate(chips)]
            for j, chip in enumerate(chips):
                copy(1 + j, (*chip, c), me).wait_recv()
                passed[j].start()
            copy(0, sibling, me).wait_recv()
            for j, chip in enumerate(chips):
                copy(4 + j, (*chip, 1 - c), me).wait_recv()
            for cp in first + passed:
                cp.wait_send()
            mine.wait()

    out_shapes = [jax.ShapeDtypeStruct((s.shape[0], N_CHIP) + s.shape[1:], s.dtype) for s in shards]
    return _comm_call(body, shards, out_shapes, 7 * n, name=name)


def _place_own(shard, chip_idx, c_idx, *, name):
    n, _, rows, cols = shard.shape
    tr = _row_tile(rows, cols)

    def body(k_ref, c_ref, x_ref, o_ref):
        o_ref[...] = x_ref[...]

    grid_spec = pltpu.PrefetchScalarGridSpec(
        num_scalar_prefetch=2, grid=(n, rows // tr),
        in_specs=[pl.BlockSpec((None, None, tr, cols), lambda l, i, k_ref, c_ref: (l, c_ref[0], i, 0))],
        out_specs=pl.BlockSpec((None, None, None, tr, cols), lambda l, i, k_ref, c_ref: (l, k_ref[0], c_ref[0], i, 0)))
    return pl.pallas_call(
        body, name=name, grid_spec=grid_spec,
        out_shape=jax.ShapeDtypeStruct((n, N_CHIP, 2, rows, cols), shard.dtype),
        compiler_params=_cparams(("parallel", "parallel")),
    )(chip_idx, c_idx, shard)


def _gather_copies(x_refs, land_refs, send_sems, recv_sems):
    x, y, c = _mesh_pos()
    k_me = 2 * x + y
    targets = [(x, y, 1 - c), (1 - x, y, c), (x, 1 - y, c), (1 - x, 1 - y, c)]
    copies = []
    for i, (x_ref, land_ref) in enumerate(zip(x_refs, land_refs)):
        nl = x_ref.shape[0]
        for j, to in enumerate(targets):
            copies.append(pltpu.make_async_remote_copy(
                src_ref=x_ref.at[pl.ds(0, nl), c], dst_ref=land_ref.at[pl.ds(0, nl), k_me, c],
                send_sem=send_sems.at[4 * i + j], recv_sem=recv_sems.at[4 * i + j], device_id=to,
                device_id_type=MESH_ID))
    return copies


def _split_start(copies_fn, srcs, lands, after, *, name, sems_per_array):
    n = len(srcs)

    def body(*refs):
        send_sems, recv_sems = refs[2 * n + 1], refs[2 * n + 2]
        for cp in copies_fn(refs[:n], refs[n:2 * n], send_sems, recv_sems):
            cp.start()
        refs[-1][...] = jnp.zeros_like(refs[-1])

    operands = [pltpu.with_memory_space_constraint(a, pltpu.HBM) for a in list(srcs) + list(lands)]
    n_sems = sems_per_array * n
    out_shape = ([pltpu.SemaphoreType.DMA((n_sems,)), pltpu.SemaphoreType.DMA((n_sems,))]
                 + [pltpu.HBM(a.shape, a.dtype) for a in operands] + [jax.ShapeDtypeStruct((8, LANES), F32)])
    res = pl.pallas_call(
        body, name=name, out_shape=out_shape, in_specs=[HBM] * (2 * n) + [ANY],
        out_specs=[SEM, SEM] + [HBM] * (2 * n) + [pl.BlockSpec(memory_space=pltpu.VMEM)],
        input_output_aliases={i: 2 + i for i in range(2 * n)},
        compiler_params=pltpu.CompilerParams(has_side_effects=DATAFLOW),
    )(*operands, after)
    return res[0], res[1], list(res[2:2 + n]), list(res[2 + n:2 + 2 * n]), res[-1]


def _split_wait(copies_fn, send_sems, recv_sems, srcs, lands, after, *, name):
    n = len(srcs)

    def body(*refs):
        for cp in copies_fn(refs[:n], refs[n:2 * n], refs[2 * n], refs[2 * n + 1]):
            cp.wait_send()
            cp.wait_recv()

    res = pl.pallas_call(
        body, name=name, out_shape=[pltpu.HBM(a.shape, a.dtype) for a in list(srcs) + list(lands)],
        in_specs=[HBM] * (2 * n) + [SEM, SEM, ANY], out_specs=[HBM] * (2 * n),
        input_output_aliases={i: i for i in range(2 * n)},
        compiler_params=pltpu.CompilerParams(has_side_effects=DATAFLOW),
    )(*srcs, *lands, send_sems, recv_sems, after)
    return list(res[:n]), list(res[n:])


def _gather_forward(lands, *, name):
    n = len(lands)

    def body(*refs):
        xs = refs[:n]
        send_sems, recv_sems, _ = refs[2 * n:]
        x, y, c = _mesh_pos()
        chips = [(1 - x, y), (x, 1 - y), (1 - x, 1 - y)]
        copies = []
        for i in range(n):
            nl = lands[i].shape[0]
            for j, (cx, cy) in enumerate(chips):
                here = xs[i].at[pl.ds(0, nl), 2 * cx + cy, c]
                cp = pltpu.make_async_remote_copy(
                    src_ref=here, dst_ref=here, send_sem=send_sems.at[3 * i + j], recv_sem=recv_sems.at[3 * i + j],
                    device_id=(x, y, 1 - c), device_id_type=MESH_ID)
                cp.start()
                copies.append(cp)
        for cp in copies:
            cp.wait()

    return pl.pallas_call(
        body, name=name, out_shape=[jax.ShapeDtypeStruct(a.shape, a.dtype) for a in lands],
        in_specs=[ANY] * n, out_specs=[ANY] * n, input_output_aliases={i: i for i in range(n)},
        scratch_shapes=[pltpu.SemaphoreType.DMA((3 * n,)), pltpu.SemaphoreType.DMA((3 * n,)),
                        pltpu.SemaphoreType.DMA((1,))],
    )(*lands)


def _pair_copies(g_refs, land_refs, send_sems, recv_sems):
    x, y, c = _mesh_pos()
    copies = []
    for i, (g_ref, land_ref) in enumerate(zip(g_refs, land_refs)):
        nl, ns = g_ref.shape[:2]
        copies.append(pltpu.make_async_remote_copy(
            src_ref=g_ref.at[pl.ds(0, nl), pl.ds(0, ns), 1 - c], dst_ref=land_ref, send_sem=send_sems.at[i],
            recv_sem=recv_sems.at[i], device_id=(x, y, 1 - c), device_id_type=MESH_ID))
    return copies


def _pair_exchange(gs, *, name):
    n = len(gs)

    def body(*refs):
        send_sems, recv_sems, _ = refs[2 * n:]
        copies = _pair_copies(refs[:n], refs[n:2 * n], send_sems, recv_sems)
        for cp in copies:
            cp.start()
        for cp in copies:
            cp.wait()

    out_shapes = [jax.ShapeDtypeStruct(g.shape[:2] + g.shape[3:], g.dtype) for g in gs]
    return _comm_call(body, gs, out_shapes, n, name=name)


def _chip_copies(p_refs, land_refs, send_sems, recv_sems):
    x, y, c = _mesh_pos()
    k_me = 2 * x + y
    chips = [(1 - x, y), (x, 1 - y), (1 - x, 1 - y)]
    copies = []
    for i, (p_ref, land_ref) in enumerate(zip(p_refs, land_refs)):
        nl = p_ref.shape[0]
        for j, (cx, cy) in enumerate(chips):
            copies.append(pltpu.make_async_remote_copy(
                src_ref=p_ref.at[pl.ds(0, nl), 2 * cx + cy], dst_ref=land_ref.at[k_me],
                send_sem=send_sems.at[3 * i + j], recv_sem=recv_sems.at[3 * i + j],
                device_id=(cx, cy, c), device_id_type=MESH_ID))
    return copies


def _chip_landing(ps):
    return [lax.empty((p.shape[1], p.shape[0]) + p.shape[2:], p.dtype) for p in ps]


def _pair_swap(ss, *, name):
    n = len(ss)

    def body(*refs):
        xs, outs = refs[:n], refs[n:2 * n]
        send_sems, recv_sems, _ = refs[2 * n:]
        x, y, c = _mesh_pos()
        copies = []
        for i in range(n):
            cp = pltpu.make_async_remote_copy(src_ref=xs[i], dst_ref=outs[i], send_sem=send_sems.at[i],
                                              recv_sem=recv_sems.at[i], device_id=(x, y, 1 - c),
                                              device_id_type=MESH_ID)
            cp.start()
            copies.append(cp)
        for cp in copies:
            cp.wait()

    out_shapes = [jax.ShapeDtypeStruct(s.shape, s.dtype) for s in ss]
    return _comm_call(body, ss, out_shapes, n, name=name)


def _row_tile(rows, cols):
    tr = rows
    while tr * cols > 256 * 1024 and tr % 16 == 0:
        tr //= 2
    return tr


def _pair_add(g, recv, c_idx, *, name):
    n, ns, _, rows, W = g.shape
    tr = _row_tile(rows, W)

    def body(c_ref, g_ref, r_ref, o_ref):
        o_ref[...] = (g_ref[...] + r_ref[...]).astype(BF16)

    piece = pl.BlockSpec((None, tr, W), lambda p, i, c_ref: (p, i, 0))
    grid_spec = pltpu.PrefetchScalarGridSpec(
        num_scalar_prefetch=1, grid=(n * ns, rows // tr),
        in_specs=[pl.BlockSpec((None, None, tr, W), lambda p, i, c_ref: (p, c_ref[0], i, 0)), piece],
        out_specs=piece)
    out = pl.pallas_call(
        body, name=name, grid_spec=grid_spec, out_shape=jax.ShapeDtypeStruct((n * ns, rows, W), BF16),
        compiler_params=_cparams(("parallel", "parallel")),
    )(c_idx, g.reshape(n * ns, 2, rows, W), recv.reshape(n * ns, rows, W))
    return out.reshape(n, ns, rows, W)


def _sum_pieces(land, own, chip_idx, *, name):
    n, nl, A, W = land.shape
    tr = _row_tile(A, W)

    def body(k_ref, l_ref, o_ref, out_ref):
        acc = jnp.zeros(out_ref.shape, F32)
        for k in range(n):
            acc = acc + jnp.where(k == k_ref[0], o_ref[...], l_ref[k]).astype(F32)
        out_ref[...] = acc

    grid_spec = pltpu.PrefetchScalarGridSpec(
        num_scalar_prefetch=1, grid=(nl, A // tr),
        in_specs=[pl.BlockSpec((n, None, tr, W), lambda l, i, k_ref: (0, l, i, 0)),
                  pl.BlockSpec((None, None, tr, W), lambda l, i, k_ref: (l, k_ref[0], i, 0))],
        out_specs=pl.BlockSpec((None, tr, W), lambda l, i, k_ref: (l, i, 0)))
    return pl.pallas_call(
        body, name=name, grid_spec=grid_spec, out_shape=jax.ShapeDtypeStruct((nl, A, W), F32),
        compiler_params=_cparams(("parallel", "parallel")),
    )(chip_idx, land, own)


SMALL = ("norm_mix_g", "norm_mlp_g", "final_norm_g", "fox_b_f", "mla_q_norm_g", "mla_kv_norm_g")
WEIGHT_ORDER = ("ada_w", "ada_b", "norm_mix_g", "norm_mlp_g", "fox_w_in", "fox_b_f", "fox_w_out", "mla_w_dq",
                "mla_q_norm_g", "mla_w_uq", "mla_w_dkv", "mla_kv_norm_g", "mla_w_ukv", "mla_w_out", "mlp_w1",
                "mlp_w2", "final_norm_g")


def _small_rows(vals, D):
    rows = [vals["norm_mix_g"], vals["norm_mlp_g"], vals["final_norm_g"][None, :]]
    for n in ("fox_b_f", "mla_q_norm_g", "mla_kv_norm_g"):
        flat = vals[n].reshape(-1)
        assert flat.shape[0] <= D
        rows.append(jnp.pad(flat, (0, D - flat.shape[0]))[None, :])
    return jnp.concatenate(rows, axis=0)


def _small_unrows(rows, shapes):
    L = shapes["norm_mix_g"][0]
    out = {"norm_mix_g": rows[0:L], "norm_mlp_g": rows[L:2 * L], "final_norm_g": rows[2 * L]}
    for k, n in enumerate(("fox_b_f", "mla_q_norm_g", "mla_kv_norm_g")):
        size = int(np.prod(shapes[n]))
        out[n] = rows[2 * L + 1 + k, :size].reshape(shapes[n])
    return out


def kernel(x, c, positions, ada_w, ada_b, norm_mix_g, norm_mlp_g, fox_w_in, fox_b_f, fox_w_out, mla_w_dq, mla_q_norm_g, mla_w_uq, mla_w_dkv, mla_kv_norm_g, mla_w_ukv, mla_w_out, mlp_w1, mlp_w2, final_norm_g, loss_target, m_ada_w, m_ada_b, m_norm_mix_g, m_norm_mlp_g, m_fox_w_in, m_fox_b_f, m_fox_w_out, m_mla_w_dq, m_mla_q_norm_g, m_mla_w_uq, m_mla_w_dkv, m_mla_kv_norm_g, m_mla_w_ukv, m_mla_w_out, m_mlp_w1, m_mlp_w2, m_final_norm_g, v_ada_w, v_ada_b, v_norm_mix_g, v_norm_mlp_g, v_fox_w_in, v_fox_b_f, v_fox_w_out, v_mla_w_dq, v_mla_q_norm_g, v_mla_w_uq, v_mla_w_dkv, v_mla_kv_norm_g, v_mla_w_ukv, v_mla_w_out, v_mlp_w1, v_mlp_w2, v_final_norm_g):
    args = dict(locals())
    wts = {n: args[n] for n in WEIGHT_ORDER}
    mom = {n: args["m_" + n] for n in WEIGHT_ORDER}
    var = {n: args["v_" + n] for n in WEIGHT_ORDER}
    Bl, S, D = x.shape
    T = Bl * S
    L = ada_w.shape[0]
    C = ada_w.shape[2]
    mx, my, mc = _mesh_pos()
    chip = 2 * mx + my
    dev = 4 * mx + 2 * my + mc
    c_idx = jnp.reshape(mc, (1,)).astype(jnp.int32)
    chip_idx = jnp.reshape(chip, (1,)).astype(jnp.int32)
    small = {n: wts[n] for n in SMALL}
    L2, q_cols = mla_q_norm_g.shape
    n_fox_heads = fox_b_f.shape[1]

    shards = _shard_layouts(wts)
    groups = _comm_groups(L, L2)
    slots = _layer_slots(groups)

    def row_halves(a):
        return a.reshape(a.shape[:-2] + (2, a.shape[-2] // 2, a.shape[-1]))

    def whole_rows(a):
        return a.reshape(a.shape[:2] + (a.shape[2] * a.shape[3], a.shape[4]))

    part = {g: [row_halves(shards[n][s:s + cnt]) for n, s, cnt in entries] for g, entries in groups.items()}
    mix0 = _gather_weights(part["mix0"], name="gather_mix0")
    gather_sems, after = {}, mix0[0]
    for group in ("mlp0", "rest"):
        placed = [_place_own(a, chip_idx, c_idx, name=f"gather_place_{group}_{n}")
                  for a, (n, _, _) in zip(part[group], groups[group])]
        gather_sems[group] = _split_start(_gather_copies, part[group], placed, after, name=f"gather_{group}_start",
                                          sems_per_array=4)
        after = gather_sems[group][4]

    def layer_weights(w, group, arrays):
        for (n, s, cnt), a in zip(groups[group], arrays):
            for key, view in _weight_views(n, whole_rows(a), D, n_fox_heads).items():
                for l in range(cnt):
                    w[key][s + l] = (view, l)

    w = {key: [None] * L2
         for key in ("fox_qkv", "fox_f", "fox_in", "fox_out", "mla_down", "mla_uq", "mla_ukv", "mla_out")}
    w.update({key: [None] * L for key in ("mlp_w1", "mlp_w2")})
    layer_weights(w, "mix0", mix0)

    def gathered_now(group):
        def hook(x_now, w):
            _, landed = _split_wait(_gather_copies, *gather_sems[group][:4], x_now, name=f"gather_{group}_wait")
            layer_weights(w, group, _gather_forward(landed, name=f"gather_{group}_forward"))
            return w
        return hook

    c_pad = jnp.concatenate([c, jnp.pad(mla_q_norm_g, ((0, 8 - Bl - L2), (0, D - q_cols)))], axis=0)
    c8 = _all_gather8(c_pad, name="gather_c", in_vmem=True)
    c_all = c8[:, :Bl].reshape(N_DEV * Bl, D)
    qg4 = c8.reshape(N_CHIP, 2, 8, D)[:, 0, Bl:Bl + L2, :q_cols]
    small["mla_q_norm_g"] = jnp.transpose(qg4, (1, 0, 2)).reshape(L2, N_CHIP * q_cols)
    ada_b_cols = lax.dynamic_slice_in_dim(ada_b, chip * C, C, axis=1)[:, None, :]
    mod_cols = _ada_fwd(c_all, ada_w, ada_b_cols)
    mod8 = _all_gather8(mod_cols.reshape(L * N_DEV * Bl, C), name="gather_mod", in_vmem=True)
    mod4 = mod8.reshape(N_CHIP, 2, L, N_DEV * Bl, C)[:, 0]
    mod_me = lax.dynamic_slice_in_dim(mod4, dev * Bl, Bl, axis=2)
    mod = jnp.transpose(mod_me, (1, 2, 0, 3)).reshape(L, Bl, 6, D)
    mod = jnp.transpose(mod, (0, 2, 1, 3))[:, :, :, None, :]

    w.update(_small_layouts(small))
    mod = mod + after[0, 0]
    pending = {}

    def grad_pieces(group, g_now):
        out = []
        for n, s, cnt in groups[group]:
            qkv_f = [g_now["fox_in"][j] for j in range(s, s + cnt)] if n == "fox_in" else None
            stacked_g = None if n == "fox_in" else g_now[n][group]
            out.append(row_halves(_grad_pieces(n, stacked_g, qkv_f, n_fox_heads, N_CHIP)))
        return out

    def pair_added(group, big, sibling):
        return [_pair_add(a, r, c_idx, name=f"grad_pair_add_{group}_{n}")
                for (n, _, _), a, r in zip(groups[group], big, sibling)]

    def exchange_start(group, ps, after=None):
        pending[group] = _split_start(_chip_copies, ps, _chip_landing(ps), chip_idx if after is None else after,
                                      name=f"grad_exchange_{group}_start", sems_per_array=3)
        return pending[group][4]

    def bwd_layer0(g_now):
        big = grad_pieces("rest", g_now)
        landing = [lax.empty(a.shape[:2] + a.shape[3:], a.dtype) for a in big]
        pending["rest_pair"] = _split_start(_pair_copies, big, landing, chip_idx, name="grad_pair_rest_start",
                                            sems_per_array=1)
        return pending["rest_pair"][4]

    def bwd_mix0(g_now):
        send_sems, recv_sems, big, landed, _ = pending["rest_pair"]
        big, landed = _split_wait(_pair_copies, send_sems, recv_sems, big, landed, g_now["mlp_w1"]["mlp0"],
                                  name="grad_pair_rest_wait")
        started = exchange_start("rest", pair_added("rest", big, landed))
        big = grad_pieces("mlp0", g_now)
        return exchange_start("mlp0", pair_added("mlp0", big, _pair_exchange(big, name="grad_pair_exchange_mlp0")),
                              after=started)

    half = ROPE_DIM // 2
    inv_freq = ROPE_THETA ** (-jnp.arange(0, ROPE_DIM, 2, dtype=F32) / ROPE_DIM)
    lane = np.arange(LANES)
    inv_freq_row = jnp.tile(inv_freq, LANES // half)[None, :]
    sign_row = jnp.asarray(np.where(lane < 2 * ROPE_DIM, np.where(lane % ROPE_DIM < half, -1.0, 1.0), 0.0), F32)[None, :]
    pos_f = positions.astype(F32).reshape(T, 1)
    loss_row, grad_x, dmod, g = _local_step(x.reshape(T, D), loss_target.reshape(T, D), pos_f, inv_freq_row, sign_row,
                                            mod, w, slots, S=S,
                                            hooks={"fwd_mlp0": gathered_now("mlp0"), "fwd_layer1": gathered_now("rest"),
                                                   "bwd_layer0": bwd_layer0, "bwd_mix0": bwd_mix0})
    g_small = _small_grads(g, n_fox_heads)
    big = grad_pieces("mix0", g)
    exchange_start("mix0", pair_added("mix0", big, _pair_exchange(big, name="grad_pair_exchange_mix0")))

    Rs = -(-(2 * L + 5) // 8) * 8
    srows = jnp.concatenate([_small_rows(g_small, D), jnp.pad(loss_row, ((0, 0), (0, D - LANES)))], axis=0)
    srows = jnp.pad(srows, ((0, Rs - srows.shape[0]), (0, 0)))
    drows = jnp.transpose(dmod[:, :, :, 0, :], (2, 0, 1, 3)).reshape(Bl * L * 6, D)
    both8 = _all_gather8(jnp.concatenate([drows, srows], axis=0), name="gather_small", in_vmem=True)
    dm8 = both8[:, :Bl * L * 6].reshape(N_DEV, Bl, L * 6, D)
    sm8 = both8[:, Bl * L * 6:]
    adb_rows, small_sum = _sum_gathered(dm8, sm8)
    grad_ada_b = adb_rows.reshape(L, 6 * D)
    loss = small_sum[2 * L + 4, 0]
    small_shapes = {n: (wts[n].shape if n != "mla_q_norm_g" else (wts[n].shape[0], N_CHIP * q_cols)) for n in SMALL}
    gs = _small_unrows(small_sum, small_shapes)
    gs["mla_q_norm_g"] = lax.dynamic_slice_in_dim(gs["mla_q_norm_g"], chip * q_cols, q_cols, axis=1)

    dmod16 = jnp.transpose(dm8.reshape(N_DEV, Bl, L, 6 * D), (2, 0, 1, 3)).reshape(L, N_DEV * Bl, 6 * D)
    dmod_cols = lax.dynamic_slice_in_dim(dmod16, chip * C, C, axis=2)
    grad_ada_w = _ada_bwd(c_all, dmod_cols)

    grads = dict(gs)
    grads["ada_w"] = grad_ada_w
    grads["ada_b"] = grad_ada_b
    delta, new_m, new_v = {}, {}, {}
    for n in ("ada_w", "ada_b"):
        delta[n], new_m[n], new_v[n] = _adamw(wts[n], grads[n], mom[n], var[n], name=f"adamw_{n}")
    shard_small_shapes = {n: wts[n].shape for n in SMALL}
    packs = [jnp.pad(_small_rows({n: src[n] for n in SMALL}, D), ((0, Rs - 2 * L - 4), (0, 0)))
             for src in (wts, grads, mom, var)]
    for dst, rows in zip((delta, new_m, new_v), _adamw(*packs, name="adamw_small")):
        dst.update(_small_unrows(rows, shard_small_shapes))

    halves = {}
    for group, after in (("rest", grad_x), ("mlp0", grad_x), ("mix0", delta["ada_w"])):
        send_sems, recv_sems, ps, lands, _ = pending[group]
        ps, lands = _split_wait(_chip_copies, send_sems, recv_sems, ps, lands, after, name=f"grad_exchange_{group}_wait")
        sums = [_sum_pieces(ld, p, chip_idx, name=f"grad_sum_{group}_{n}")
                for (n, _, _), ld, p in zip(groups[group], lands, ps)]
        swapped = _pair_swap(sums, name=f"grad_pair_swap_{group}")
        for (n, _, _), a, b in zip(groups[group], sums, swapped):
            halves[(n, group)] = (a, b)

    def all_layers(n, which):
        return jnp.concatenate([halves[(n, grp)][which] for grp in groups if (n, grp) in halves], axis=0)

    own = {n: all_layers(n, 0) for n in GATHERED}
    peer = {n: all_layers(n, 1) for n in GATHERED}
    for nat, n in (("fox_w_in", "fox_in"), ("fox_w_out", "fox_out"), ("mla_w_out", "mla_out"), ("mlp_w1", "mlp_w1"),
                   ("mlp_w2", "mlp_w2")):
        cols = wts[nat].shape[-1]
        res = _adamw_halves(_pad_lanes(wts[nat]), own[n], peer[n], _pad_lanes(mom[nat]), _pad_lanes(var[nat]), c_idx,
                            name=f"adamw_{nat}")
        grads[nat], delta[nat], new_m[nat], new_v[nat] = (a[..., :cols] for a in res)
    joined = {n: jnp.concatenate([jnp.where(mc == 0, own[n], peer[n]), jnp.where(mc == 0, peer[n], own[n])], axis=1)
              for n in ("mla_down", "mla_uq", "mla_ukv")}
    rq = mla_w_dq.shape[-1]
    grads["mla_w_dq"] = joined["mla_down"][:, :, :rq]
    grads["mla_w_dkv"] = joined["mla_down"][:, :, rq:rq + KV_RANK + ROPE_DIM]
    grads["mla_w_uq"] = jax.vmap(_uq_from_pairs)(joined["mla_uq"])
    grads["mla_w_ukv"] = jax.vmap(_ukv_from_pairs)(joined["mla_ukv"])
    for n in ("mla_w_dq", "mla_w_dkv", "mla_w_uq", "mla_w_ukv"):
        delta[n], new_m[n], new_v[n] = _adamw(wts[n], grads[n], mom[n], var[n], name=f"adamw_{n}")

    return (loss, grad_x.reshape(Bl, S, D), *[grads[n] for n in WEIGHT_ORDER], *[delta[n] for n in WEIGHT_ORDER],
            *[new_m[n] for n in WEIGHT_ORDER], *[new_v[n] for n in WEIGHT_ORDER])
```

```python
import numpy as np
import jax
import jax.numpy as jnp
from jax import lax
from jax.experimental import pallas as pl
from jax.experimental.pallas import tpu as pltpu

F32 = jnp.float32
BF16 = jnp.bfloat16
MESH_ID = pl.DeviceIdType.MESH

NORM_EPS = 1e-6
ROPE_THETA = 10000.0
HEAD_DIM = 64
ROPE_DIM = 32
KV_RANK = 128
MLA_SCALE = (HEAD_DIM + ROPE_DIM) ** -0.5
FOX_EXTRA = 6
PAIR_Q = 256
PAIR_KV = 384
LANES = 128
ADAM_LR = 0.001
ADAM_B1 = 0.9
ADAM_B2 = 0.999
ADAM_EPS = 1e-08
ADAM_WD = 0.01
ADAM_STEP = 10
VMEM_LIMIT_V7X = 32 * 1024 * 1024
MM_VMEM_BUDGET = 36 * 1024 * 1024
MM_VMEM_HEADROOM = 12 * 1024 * 1024
NEG_BIG = -1e30
ATTN_UNROLL = 4
ATTN_BLOCK = 256
ATTN_Q_ROWS = 512
ATTN_K_ROWS = 512

BIG_WEIGHTS = (("fox_w_in", 2), ("fox_w_out", 1), ("mla_w_dq", 1), ("mla_w_uq", 2), ("mla_w_dkv", 1),
               ("mla_w_ukv", 2), ("mla_w_out", 1), ("mlp_w1", 2), ("mlp_w2", 1))


def _cparams(sem=None, vmem_limit=VMEM_LIMIT_V7X):
    return pltpu.CompilerParams(dimension_semantics=sem, vmem_limit_bytes=vmem_limit)


def _tile(n, want):
    if n <= want:
        return n
    for t in range(want - want % LANES, 0, -LANES):
        if n % t == 0:
            return t
    raise ValueError((n, want))


def _mm(a, b, mode, *, name, out_dtypes=(F32,), epilogue=None, extras=(), rowvecs=(), tables=(),
        seq=None, a_off=0, a_sz=None, b_layer=None, out_stack=None, out_split=0, out_t=(), full_rows=False,
        vmem_budget=MM_VMEM_BUDGET, tm=1024, tn=1024, tk=2048):
    if isinstance(b, (list, tuple)):
        b, b_layer = b[b_layer]
    b_rows, b_cols = b.shape[-2], b.shape[-1]
    n_split = b.shape[1] if b.ndim == 4 else 1
    assert mode in ("nn", "nt")
    if mode == "nn":
        M, K, N = a.shape[0], b_rows, b_cols * n_split
    else:
        M, K, N = a.shape[0], b_cols * n_split, b_rows
    assert a_sz is None or a_sz == K
    tm = _tile(seq if rowvecs else M, tm)
    n_piece = N // max(out_split, n_split if mode == "nn" else 1, 1)
    tn = _tile(n_piece, tn)
    tk = _tile(K // (n_split if mode == "nt" else 1), tk)
    ne, nr, nt_ = len(extras), len(rowvecs), len(tables)
    no = len(out_dtypes)

    def vmem_estimate():
        blocks = tm * tk * a.dtype.itemsize + tk * tn * b.dtype.itemsize
        blocks += tm * tn * (sum(e.dtype.itemsize for e in extras) + sum(jnp.dtype(d).itemsize for d in out_dtypes))
        return 2 * blocks + 2 * tm * tn * 4

    if full_rows:
        assert tn == N
    while vmem_estimate() > vmem_budget and max(tm, tn) > 256:
        if tn >= tm and not full_rows:
            tn //= 2
        else:
            tm //= 2
    nk = K // tk

    assert a_off % tk == 0
    a_spec = pl.BlockSpec((tm, tk), lambda i, j, k: (i, k + a_off // tk))
    dims = (((1,), (0,)), ((), ())) if mode == "nn" else (((1,), (1,)), ((), ()))
    lead = () if b.ndim == 2 else (b_layer,)
    sq = (None,) * (b.ndim - 2)
    if mode == "nt":
        kb = b_cols // tk
        if b.ndim == 4:
            b_spec = pl.BlockSpec(sq + (tn, tk), lambda i, j, k: lead + (k // kb, j, k % kb))
        else:
            b_spec = pl.BlockSpec(sq + (tn, tk), lambda i, j, k: lead + (j, k))
    else:
        nb = b_cols // tn
        if b.ndim == 4:
            b_spec = pl.BlockSpec(sq + (tk, tn), lambda i, j, k: lead + (j // nb, k, j % nb))
        else:
            b_spec = pl.BlockSpec(sq + (tk, tn), lambda i, j, k: lead + (k, j))
    in_specs = [a_spec, b_spec]
    in_specs += [pl.BlockSpec((tm, tn), lambda i, j, k: (i, j)) for _ in extras]
    if rowvecs:
        assert seq % tm == 0
        per = seq // tm
        in_specs += [pl.BlockSpec((None, 1, tn), lambda i, j, k: (i // per, 0, j)) for _ in rowvecs]
    in_specs += [pl.BlockSpec((tm, LANES), lambda i, j, k: (i, 0)) for _ in tables]
    operands = [a, b, *extras, *rowvecs, *tables]
    aliases = {}
    transposed = tuple(out_t) + (False,) * (no - len(out_t))
    if out_stack is None:
        out_specs = [pl.BlockSpec((tn, tm), lambda i, j, k: (j, i)) if t else pl.BlockSpec((tm, tn), lambda i, j, k: (i, j))
                     for t in transposed]
        out_shape = [jax.ShapeDtypeStruct((N, M) if t else (M, N), d) for d, t in zip(out_dtypes, transposed)]
    else:
        prev, layer, n_layers = out_stack
        assert no == 1
        if out_split:
            ob = n_piece // tn
            out_specs = [pl.BlockSpec((None, None, tm, tn), lambda i, j, k: (layer, j // ob, i, j % ob))]
            out_shape = [jax.ShapeDtypeStruct((n_layers, out_split, M, n_piece), out_dtypes[0])]
        else:
            out_specs = [pl.BlockSpec((None, tm, tn), lambda i, j, k: (layer, i, j))]
            out_shape = [jax.ShapeDtypeStruct((n_layers, M, N), out_dtypes[0])]
        if prev is not None:
            in_specs.append(pl.BlockSpec(memory_space=pl.ANY))
            aliases = {len(operands): 0}
            operands.append(prev)
    n_in = len(operands)

    def body(*refs):
        a_ref, b_ref = refs[0], refs[1]
        side = refs[2:2 + ne + nr + nt_]
        outs = refs[n_in:n_in + no]

        def finish(acc):
            res = (acc,) if epilogue is None else epilogue(acc, *[r[...] for r in side])
            for o_ref, r, t in zip(outs, res, transposed):
                o_ref[...] = (r.T if t else r).astype(o_ref.dtype)

        part = lax.dot_general(a_ref[...].astype(BF16), b_ref[...].astype(BF16), dims,
                               preferred_element_type=F32)
        if nk == 1:
            finish(part)
        else:
            acc_ref = refs[-1]
            k = pl.program_id(2)

            @pl.when(k == 0)
            def _():
                acc_ref[...] = part

            @pl.when(k > 0)
            def _():
                acc_ref[...] += part

            @pl.when(k == nk - 1)
            def _():
                finish(acc_ref[...])

    res = pl.pallas_call(
        body, name=name, grid=(M // tm, N // tn, nk), in_specs=in_specs, out_specs=out_specs,
        out_shape=out_shape, scratch_shapes=[pltpu.VMEM((tm, tn), F32)] if nk > 1 else [],
        input_output_aliases=aliases,
        compiler_params=_cparams(("parallel", "parallel", "arbitrary"), vmem_limit=vmem_budget + MM_VMEM_HEADROOM),
    )(*operands)
    return res[0] if no == 1 else tuple(res)


def _rope128(x, cos_t, sin_s):
    lane = lax.broadcasted_iota(jnp.int32, x.shape, 1)
    first = (lane % ROPE_DIM) < (ROPE_DIM // 2)
    swapped = jnp.where(first, pltpu.roll(x, LANES - ROPE_DIM // 2, 1), pltpu.roll(x, ROPE_DIM // 2, 1))
    return x * cos_t + swapped * sin_s


def _rope_pairs(acc, cos_t, sin_s, sign):
    parts = []
    for p in range(acc.shape[1] // PAIR_Q):
        parts.append(acc[:, p * PAIR_Q:p * PAIR_Q + LANES])
        parts.append(_rope128(acc[:, p * PAIR_Q + LANES:(p + 1) * PAIR_Q], cos_t, sign * sin_s))
    return jnp.concatenate(parts, axis=1)


def _rope_tables(pos_f, inv_freq_row, sign_row):
    T = pos_f.shape[0]
    tt = _tile(T, 512)

    def body(p_ref, f_ref, s_ref, cos_ref, sin_ref):
        ang = p_ref[...] * f_ref[...]
        cos_ref[...] = jnp.cos(ang)
        sin_ref[...] = jnp.sin(ang) * s_ref[...]

    return pl.pallas_call(
        body, name="rope_tables", grid=(T // tt,),
        in_specs=[pl.BlockSpec((tt, 1), lambda i: (i, 0)), pl.BlockSpec((1, LANES), lambda i: (0, 0)),
                  pl.BlockSpec((1, LANES), lambda i: (0, 0))],
        out_specs=[pl.BlockSpec((tt, LANES), lambda i: (i, 0))] * 2,
        out_shape=[jax.ShapeDtypeStruct((T, LANES), F32)] * 2,
        compiler_params=_cparams(("parallel",)),
    )(pos_f, inv_freq_row, sign_row)


def _unrope(dqx, cos_t, sin_s):
    T, W = dqx.shape
    tt = _tile(T, 512)

    def body(d_ref, c_ref, s_ref, o_ref):
        o_ref[...] = _rope_pairs(d_ref[...].astype(F32) * MLA_SCALE, c_ref[...], s_ref[...], -1.0).astype(BF16)

    return pl.pallas_call(
        body, name="mla_unrope", grid=(T // tt,),
        in_specs=[pl.BlockSpec((tt, W), lambda i: (i, 0)), pl.BlockSpec((tt, LANES), lambda i: (i, 0)),
                  pl.BlockSpec((tt, LANES), lambda i: (i, 0))],
        out_specs=pl.BlockSpec((tt, W), lambda i: (i, 0)),
        out_shape=jax.ShapeDtypeStruct((T, W), BF16),
        compiler_params=_cparams(("parallel",)),
    )(dqx, cos_t, sin_s)


def _row_specs(tt, D, per, n):
    return [pl.BlockSpec((None, 1, D), lambda i: (i // per, 0, 0)) for _ in range(n)]


def _norm_mod(x, gain, sc, sh, *, S, name):
    T, D = x.shape
    tt = _tile(S, 512)
    per = S // tt

    def body(x_ref, g_ref, sc_ref, sh_ref, h_ref, ht_ref):
        xv = x_ref[...]
        r = lax.rsqrt(jnp.mean(xv * xv, axis=-1, keepdims=True) + NORM_EPS)
        h = (xv * r) * g_ref[...] * (1.0 + sc_ref[...]) + sh_ref[...]
        h_ref[...] = h.astype(BF16)
        ht_ref[...] = h.T.astype(BF16)

    return pl.pallas_call(
        body, name=name, grid=(T // tt,),
        in_specs=[pl.BlockSpec((tt, D), lambda i: (i, 0)), pl.BlockSpec((1, D), lambda i: (0, 0))]
        + _row_specs(tt, D, per, 2),
        out_specs=[pl.BlockSpec((tt, D), lambda i: (i, 0)), pl.BlockSpec((D, tt), lambda i: (0, i))],
        out_shape=[jax.ShapeDtypeStruct((T, D), BF16), jax.ShapeDtypeStruct((D, T), BF16)],
        compiler_params=_cparams(("parallel",)),
    )(x, gain, sc, sh)


def _norm_mod_bwd(x, dh, dres, gain, sc, *, S, name):
    T, D = x.shape
    B = T // S
    tt = _tile(S, 512)
    per = S // tt

    def body(x_ref, dh_ref, dres_ref, g_ref, sc_ref, dx_ref, dsh_ref, dsc_ref, dg_ref):
        i = pl.program_id(0)
        xv = x_ref[...]
        dhv = dh_ref[...].astype(F32)
        r = lax.rsqrt(jnp.mean(xv * xv, axis=-1, keepdims=True) + NORM_EPS)
        n = xv * r
        g = g_ref[...]
        one_sc = 1.0 + sc_ref[...]
        dn = dhv * (g * one_sc)
        dx_ref[...] = dres_ref[...] + r * (dn - n * jnp.mean(dn * n, axis=-1, keepdims=True))
        dhn = dhv * n

        @pl.when(i % per == 0)
        def _():
            dsh_ref[...] = jnp.zeros_like(dsh_ref)
            dsc_ref[...] = jnp.zeros_like(dsc_ref)

        @pl.when(i == 0)
        def _():
            dg_ref[...] = jnp.zeros_like(dg_ref)

        dsh_ref[...] += jnp.sum(dhv, axis=0, keepdims=True)
        dsc_ref[...] += jnp.sum(dhn, axis=0, keepdims=True) * g
        dg_ref[...] += jnp.sum(dhn, axis=0, keepdims=True) * one_sc

    return pl.pallas_call(
        body, name=name, grid=(T // tt,),
        in_specs=[pl.BlockSpec((tt, D), lambda i: (i, 0))] * 3 + [pl.BlockSpec((1, D), lambda i: (0, 0))]
        + _row_specs(tt, D, per, 1),
        out_specs=[pl.BlockSpec((tt, D), lambda i: (i, 0))] + _row_specs(tt, D, per, 2)
        + [pl.BlockSpec((1, D), lambda i: (0, 0))],
        out_shape=[jax.ShapeDtypeStruct((T, D), F32), jax.ShapeDtypeStruct((B, 1, D), F32),
                   jax.ShapeDtypeStruct((B, 1, D), F32), jax.ShapeDtypeStruct((1, D), F32)],
        compiler_params=_cparams(("arbitrary",)),
    )(x, dh, dres, gain, sc)


def _gate_bwd(dx, y, g, *, S, name):
    T, D = dx.shape
    B = T // S
    tt = _tile(S, 512)
    per = S // tt

    def body(dx_ref, y_ref, g_ref, dy_ref, dg_ref):
        i = pl.program_id(0)
        dxv = dx_ref[...]
        dy_ref[...] = (dxv * g_ref[...]).astype(BF16)

        @pl.when(i % per == 0)
        def _():
            dg_ref[...] = jnp.zeros_like(dg_ref)

        dg_ref[...] += jnp.sum(dxv * y_ref[...], axis=0, keepdims=True)

    return pl.pallas_call(
        body, name=name, grid=(T // tt,),
        in_specs=[pl.BlockSpec((tt, D), lambda i: (i, 0))] * 2 + _row_specs(tt, D, per, 1),
        out_specs=[pl.BlockSpec((tt, D), lambda i: (i, 0))] + _row_specs(tt, D, per, 1),
        out_shape=[jax.ShapeDtypeStruct((T, D), BF16), jax.ShapeDtypeStruct((B, 1, D), F32)],
        compiler_params=_cparams(("arbitrary",)),
    )(dx, y, g)


def _final_loss(x, target, gain):
    T, D = x.shape
    tt = _tile(T, 512)

    def body(x_ref, t_ref, g_ref, dx_ref, dg_ref, loss_ref):
        i = pl.program_id(0)
        xv = x_ref[...]
        r = lax.rsqrt(jnp.mean(xv * xv, axis=-1, keepdims=True) + NORM_EPS)
        n = xv * r
        g = g_ref[...]
        err = n * g - t_ref[...]
        dy = err * (1.0 / D)
        dn = dy * g
        dx_ref[...] = r * (dn - n * jnp.mean(dn * n, axis=-1, keepdims=True))

        @pl.when(i == 0)
        def _():
            dg_ref[...] = jnp.zeros_like(dg_ref)
            loss_ref[...] = jnp.zeros_like(loss_ref)

        dg_ref[...] += jnp.sum(dy * n, axis=0, keepdims=True)
        loss_ref[...] += jnp.sum(jnp.sum(err * err, axis=-1, keepdims=True), axis=0, keepdims=True) * (0.5 / D)

    return pl.pallas_call(
        body, name="final_loss", grid=(T // tt,),
        in_specs=[pl.BlockSpec((tt, D), lambda i: (i, 0))] * 2 + [pl.BlockSpec((1, D), lambda i: (0, 0))],
        out_specs=[pl.BlockSpec((tt, D), lambda i: (i, 0)), pl.BlockSpec((1, D), lambda i: (0, 0)),
                   pl.BlockSpec((1, LANES), lambda i: (0, 0))],
        out_shape=[jax.ShapeDtypeStruct((T, D), F32), jax.ShapeDtypeStruct((1, D), F32),
                   jax.ShapeDtypeStruct((1, LANES), F32)],
        compiler_params=_cparams(("arbitrary",)),
    )(x, target, gain)


def _head_masks(ew):
    lane = lax.broadcasted_iota(jnp.int32, (1, PAIR_Q), 1)
    m0 = (lane < HEAD_DIM) | ((lane >= LANES) & (lane < LANES + ew))
    m1 = ((lane >= HEAD_DIM) & (lane < LANES)) | ((lane >= LANES + ew) & (lane < LANES + 2 * ew))
    return m0, m1


def _dot_nt(a, b):
    return lax.dot_general(a, b, (((1,), (1,)), ((), ())), preferred_element_type=F32)


def _dot_tn(a, b):
    return lax.dot_general(a, b, (((0,), (0,)), ((), ())), preferred_element_type=F32)


def _lane_halves(x, op):
    acc = x[:, 0:LANES]
    for g in range(1, x.shape[1] // LANES):
        acc = op(acc, x[:, g * LANES:(g + 1) * LANES])
    return acc


def _head_rows(cols_lane_replicated):
    t = cols_lane_replicated.T
    sub = lax.broadcasted_iota(jnp.int32, (8, t.shape[1]), 0)
    return jnp.where(sub == 1, t[HEAD_DIM:HEAD_DIM + 8], t[0:8])


def _attn_fwd(qx, kvx, *, S, ew, name):
    T = qx.shape[0]
    P = qx.shape[1] // PAIR_Q
    B = T // S
    tk = _tile(S, ATTN_BLOCK)
    tq = _tile(S, ATTN_Q_ROWS)
    nq = S // tq
    per = tq // tk

    def body(q_ref, kv_ref, o_ref, lse_ref, ot_ref, m_sc, l_sc, acc_sc):
        qi = pl.program_id(2)
        q = q_ref[...]
        masks = _head_masks(ew)
        qh = [jnp.where(m, q, jnp.zeros_like(q)) for m in masks]

        def logits(h, k, diagonal):
            s = _dot_nt(qh[h], k)
            if diagonal is None:
                return s
            row = lax.broadcasted_iota(jnp.int32, s.shape, 0)
            col = lax.broadcasted_iota(jnp.int32, s.shape, 1)
            return jnp.where(col + diagonal * tk <= row, s, NEG_BIG)

        def trip(first, count, n_diagonal=0):
            rows = [pl.ds(pl.multiple_of((first + u) * tk, tk), tk) for u in range(count)]
            diag = [None] * (count - n_diagonal) + list(range(n_diagonal))
            for h in range(2):
                ss = [logits(h, kv_ref[rows[u], 0:PAIR_Q], diag[u]) for u in range(count)]
                m_prev = m_sc[h]
                m_elem = m_prev
                for s in ss:
                    m_elem = jnp.maximum(m_elem, _lane_halves(s, jnp.maximum))
                m_new = jnp.broadcast_to(jnp.max(m_elem, axis=1, keepdims=True), (tq, LANES))
                alpha = jnp.exp(m_prev - m_new)
                l = alpha * l_sc[h]
                acc = alpha * acc_sc[h]
                for u, s in enumerate(ss):
                    p = jnp.concatenate([jnp.exp(s[:, g * LANES:(g + 1) * LANES] - m_new)
                                         for g in range(tk // LANES)], axis=1)
                    l = l + _lane_halves(p, jnp.add)
                    acc = acc + jnp.dot(p.astype(BF16), kv_ref[rows[u], PAIR_Q:PAIR_KV], preferred_element_type=F32)
                m_sc[h] = m_new
                l_sc[h] = l
                acc_sc[h] = acc

        m_sc[...] = jnp.full(m_sc.shape, NEG_BIG, F32)
        l_sc[...] = jnp.zeros_like(l_sc)
        acc_sc[...] = jnp.zeros_like(acc_sc)

        def loop_body(t, carry):
            trip(t * ATTN_UNROLL, ATTN_UNROLL)
            return carry

        below = qi * per
        lax.fori_loop(0, below // ATTN_UNROLL, loop_body, 0)
        for left in range(0, ATTN_UNROLL, per):
            @pl.when(below % ATTN_UNROLL == left)
            def _(left=left):
                trip(below - left, left + per, n_diagonal=per)

        lane = lax.broadcasted_iota(jnp.int32, (tq, LANES), 1)
        lo = lane < HEAD_DIM
        l = [jnp.sum(l_sc[h], axis=1, keepdims=True) for h in range(2)]
        o = jnp.where(lo, acc_sc[0] / l[0], acc_sc[1] / l[1])
        o_ref[...] = o.astype(BF16)
        ot_ref[...] = o.T.astype(BF16)
        lse = jnp.where(lo, m_sc[0] + jnp.log(l[0]), m_sc[1] + jnp.log(l[1]))
        for r in range(per):
            lse_ref[r] = _head_rows(lse[r * tk:(r + 1) * tk])

    return pl.pallas_call(
        body, name=name, grid=(B, P, nq),
        in_specs=[pl.BlockSpec((tq, PAIR_Q), lambda b, p, i: (b * nq + i, p)),
                  pl.BlockSpec((S, PAIR_KV), lambda b, p, i: (b, p))],
        out_specs=[pl.BlockSpec((tq, LANES), lambda b, p, i: (b * nq + i, p)),
                   pl.BlockSpec((per, None, 8, tk), lambda b, p, i: (b * nq + i, p, 0, 0)),
                   pl.BlockSpec((LANES, tq), lambda b, p, i: (p, b * nq + i))],
        out_shape=[jax.ShapeDtypeStruct((T, P * LANES), BF16), jax.ShapeDtypeStruct((T // tk, P, 8, tk), F32),
                   jax.ShapeDtypeStruct((P * LANES, T), BF16)],
        scratch_shapes=[pltpu.VMEM((2, tq, LANES), F32)] * 3,
        compiler_params=_cparams(("parallel", "parallel", "arbitrary")),
    )(qx, kvx)


def _attn_bwd(qx, kvx, o, lse, do, *, S, ew, name, bias_grad=False):
    T = qx.shape[0]
    P = qx.shape[1] // PAIR_Q
    B = T // S
    tq = _tile(S, ATTN_BLOCK)
    tk = _tile(S, ATTN_K_ROWS)
    nq = S // tq
    nk = S // tk
    per = tk // tq

    def body(q_ref, kv_ref, o_ref, lse_ref, do_ref, dq_ref, dkv_ref, *rest):
        kj = pl.program_id(2)
        if bias_grad:
            csum_ref, rsum_ref, dq_sc, delta_sc, dk_sc, dv_sc, cs_sc = rest
            cs_sc[...] = jnp.zeros_like(cs_sc)

            @pl.when(kj == 0)
            def _():
                rsum_ref[...] = jnp.zeros_like(rsum_ref)
        else:
            dq_sc, delta_sc, dk_sc, dv_sc = rest
        masks = _head_masks(ew)
        lo_q = lax.broadcasted_iota(jnp.int32, (tq, LANES), 1) < HEAD_DIM
        lo = lax.broadcasted_iota(jnp.int32, (tk, LANES), 1) < HEAD_DIM
        vmask = [lo, jnp.logical_not(lo)]

        @pl.when(kj == 0)
        def _():
            dq_sc[...] = jnp.zeros_like(dq_sc)
            for c in range(nq):
                rows = pl.ds(c * tq, tq)
                x = do_ref[rows, :].astype(F32) * o_ref[rows, :].astype(F32)
                r0 = jnp.sum(jnp.where(lo_q, x, 0.0), axis=1, keepdims=True)
                r1 = jnp.sum(jnp.where(lo_q, 0.0, x), axis=1, keepdims=True)
                delta_sc[c] = _head_rows(jnp.where(lo_q, r0, r1))

        k = kv_ref[:, 0:PAIR_Q]
        v = kv_ref[:, PAIR_Q:PAIR_KV]
        kh = [jnp.where(m, k, jnp.zeros_like(k)) for m in masks]
        vh = [jnp.where(m, v, jnp.zeros_like(v)) for m in vmask]
        dk_sc[...] = jnp.zeros_like(dk_sc)
        dv_sc[...] = jnp.zeros_like(dv_sc)

        def step(qi, diagonal):
            rows = pl.ds(pl.multiple_of(qi * tq, tq), tq)
            q = q_ref[rows, :]
            dov = do_ref[rows, :]
            lse8 = lse_ref[qi]
            dl8 = delta_sc[qi]
            for h in range(2):
                st = _dot_nt(kh[h], q)
                if diagonal is not None:
                    key = lax.broadcasted_iota(jnp.int32, st.shape, 0)
                    qry = lax.broadcasted_iota(jnp.int32, st.shape, 1)
                    st = jnp.where(key <= qry + diagonal * tq, st, NEG_BIG)
                pt = jnp.exp(st - lse8[h:h + 1, :])
                dpt = _dot_nt(vh[h], dov)
                dst = pt * (dpt - dl8[h:h + 1, :])
                if bias_grad:
                    cs_sc[h] += _lane_halves(dst, jnp.add)
                    rsum_ref[qi, h:h + 1, :] += jnp.sum(dst, axis=0, keepdims=True)
                ptb = pt.astype(BF16)
                dstb = dst.astype(BF16)
                dv_sc[h] += jnp.dot(ptb, dov, preferred_element_type=F32)
                dk_sc[h] += jnp.dot(dstb, q, preferred_element_type=F32)
                dq_sc[rows, :] += _dot_tn(dstb, kh[h])

        first = kj * per
        above = nq - per - first
        for left in range(0, ATTN_UNROLL, per):
            @pl.when(above % ATTN_UNROLL == left)
            def _(left=left):
                for d in range(per):
                    step(first + d, d)
                for u in range(left):
                    step(first + per + u, None)

        def loop_body(t, carry):
            for u in range(ATTN_UNROLL):
                step(first + per + above % ATTN_UNROLL + t * ATTN_UNROLL + u, None)
            return carry

        lax.fori_loop(0, above // ATTN_UNROLL, loop_body, 0)
        dkv_ref[:, 0:PAIR_Q] = (jnp.where(masks[0], dk_sc[0], 0.0) + jnp.where(masks[1], dk_sc[1], 0.0)).astype(BF16)
        dkv_ref[:, PAIR_Q:PAIR_KV] = jnp.where(lo, dv_sc[0], dv_sc[1]).astype(BF16)
        if bias_grad:
            csum_ref[...] = jnp.where(lo, jnp.sum(cs_sc[0], axis=1, keepdims=True),
                                      jnp.sum(cs_sc[1], axis=1, keepdims=True))

        @pl.when(kj == nk - 1)
        def _():
            dq_ref[...] = dq_sc[...].astype(BF16)

    rows_spec = pl.BlockSpec((nq, None, 8, tq), lambda b, p, j: (b, p, 0, 0))
    out_specs = [pl.BlockSpec((S, PAIR_Q), lambda b, p, j: (b, p)),
                 pl.BlockSpec((tk, PAIR_KV), lambda b, p, j: (b * nk + j, p))]
    out_shape = [jax.ShapeDtypeStruct((T, P * PAIR_Q), BF16), jax.ShapeDtypeStruct((T, P * PAIR_KV), BF16)]
    scratch = [pltpu.VMEM((S, PAIR_Q), F32), pltpu.VMEM((nq, 8, tq), F32),
               pltpu.VMEM((2, tk, PAIR_Q), F32), pltpu.VMEM((2, tk, LANES), F32)]
    if bias_grad:
        out_specs += [pl.BlockSpec((tk, LANES), lambda b, p, j: (b * nk + j, p)), rows_spec]
        out_shape += [jax.ShapeDtypeStruct((T, P * LANES), F32), jax.ShapeDtypeStruct((T // tq, P, 8, tq), F32)]
        scratch.append(pltpu.VMEM((2, tk, LANES), F32))
    return pl.pallas_call(
        body, name=name, grid=(B, P, nk),
        in_specs=[pl.BlockSpec((S, PAIR_Q), lambda b, p, j: (b, p)),
                  pl.BlockSpec((tk, PAIR_KV), lambda b, p, j: (b * nk + j, p)),
                  pl.BlockSpec((S, LANES), lambda b, p, j: (b, p)), rows_spec,
                  pl.BlockSpec((S, LANES), lambda b, p, j: (b, p))],
        out_specs=out_specs, out_shape=out_shape, scratch_shapes=scratch,
        compiler_params=_cparams(("parallel", "parallel", "arbitrary")),
    )(qx, kvx, o, lse, do)


def _fox_consts(P):
    H = 2 * P
    eq = np.zeros((3 * LANES, P * LANES), np.float32)
    ek = np.zeros((3 * LANES, P * LANES), np.float32)
    ones_q = np.zeros((1, P * LANES), np.float32)
    ones_k = np.zeros((1, P * LANES), np.float32)
    for h in range(H):
        base = (h // 2) * LANES + FOX_EXTRA * (h % 2)
        for part in range(3):
            eq[part * LANES + h, base + part] = 1.0
            ones_q[0, base + 3 + part] = 1.0
            ones_k[0, base + part] = 1.0
            ek[part * LANES + h, base + 3 + part] = -1.0
    return eq, ek, ones_q, ones_k


def _split3(f):
    hi = f.astype(BF16)
    r = f - hi.astype(F32)
    mid = r.astype(BF16)
    lo = (r - mid.astype(F32)).astype(BF16)
    return hi, mid, lo


def _tri_sum(tri, x):
    hi, mid, lo = _split3(x)
    return (jnp.dot(tri, hi, preferred_element_type=F32) + jnp.dot(tri, mid, preferred_element_type=F32)
            + jnp.dot(tri, lo, preferred_element_type=F32))


def _log1p_pos(e):
    return jnp.where(e < 0.01, e * (1.0 - e * (0.5 - e * (1.0 / 3.0))), jnp.log(1.0 + e))


def _fox_prep(qkv, fl, b_row, *, S, D, name):
    T = qkv.shape[0]
    P = D // LANES
    B = T // S
    tt = _tile(S, 256)
    per = S // tt
    eq, ek, ones_q, ones_k = _fox_consts(P)
    q_scale = HEAD_DIM ** -0.5

    def body(q_ref, k_ref, v_ref, fl_ref, b_ref, eq_ref, ek_ref, oq_ref, ok_ref, qx_ref, kvx_ref, carry):
        i = pl.program_id(1)

        @pl.when(i == 0)
        def _():
            carry[...] = jnp.zeros_like(carry)

        z = fl_ref[...] + b_ref[...]
        logf = jnp.minimum(z, 0.0) - _log1p_pos(jnp.exp(-jnp.abs(z)))
        row = lax.broadcasted_iota(jnp.int32, (tt, tt), 0)
        col = lax.broadcasted_iota(jnp.int32, (tt, tt), 1)
        tri = (col <= row).astype(BF16)
        f = _tri_sum(tri, logf) + carry[...]
        carry[...] = f[tt - 1:tt, :]
        parts = jnp.concatenate(_split3(f), axis=1)
        xq = jnp.dot(parts, eq_ref[...], preferred_element_type=F32) + oq_ref[...]
        xk = jnp.dot(parts, ek_ref[...], preferred_element_type=F32) + ok_ref[...]
        for p in range(P):
            c = slice(p * LANES, (p + 1) * LANES)
            qx_ref[:, p * PAIR_Q:p * PAIR_Q + LANES] = (q_ref[:, c].astype(F32) * q_scale).astype(BF16)
            qx_ref[:, p * PAIR_Q + LANES:(p + 1) * PAIR_Q] = xq[:, c].astype(BF16)
            kvx_ref[:, p * PAIR_KV:p * PAIR_KV + LANES] = k_ref[:, c]
            kvx_ref[:, p * PAIR_KV + LANES:p * PAIR_KV + PAIR_Q] = xk[:, c].astype(BF16)
            kvx_ref[:, p * PAIR_KV + PAIR_Q:(p + 1) * PAIR_KV] = v_ref[:, c]

    tok = lambda b, i: (b * per + i, 0)
    const = lambda b, i: (0, 0)
    return pl.pallas_call(
        body, name=name, grid=(B, per),
        in_specs=[pl.BlockSpec((tt, D), lambda b, i: (b * per + i, 0)),
                  pl.BlockSpec((tt, D), lambda b, i: (b * per + i, 1)),
                  pl.BlockSpec((tt, D), lambda b, i: (b * per + i, 2)),
                  pl.BlockSpec((tt, LANES), tok), pl.BlockSpec((1, LANES), const),
                  pl.BlockSpec(eq.shape, const), pl.BlockSpec(ek.shape, const),
                  pl.BlockSpec(ones_q.shape, const), pl.BlockSpec(ones_k.shape, const)],
        out_specs=[pl.BlockSpec((tt, P * PAIR_Q), tok), pl.BlockSpec((tt, P * PAIR_KV), tok)],
        out_shape=[jax.ShapeDtypeStruct((T, P * PAIR_Q), BF16), jax.ShapeDtypeStruct((T, P * PAIR_KV), BF16)],
        scratch_shapes=[pltpu.VMEM((1, LANES), F32)],
        compiler_params=_cparams(("arbitrary", "arbitrary")),
    )(qkv, qkv, qkv, fl, b_row, jnp.asarray(eq, BF16), jnp.asarray(ek, BF16), jnp.asarray(ones_q), jnp.asarray(ones_k))


def _fox_unprep(dqx, dkvx, csum, rsum, fl, b_row, *, S, D, name):
    T = dqx.shape[0]
    P = D // LANES
    B = T // S
    tt = _tile(S, 256)
    per = S // tt
    q_scale = HEAD_DIM ** -0.5

    def body(dq_ref, dkv_ref, cs_ref, rs_ref, fl_ref, b_ref, dqkv_ref, db_ref, carry):
        b = pl.program_id(0)
        i = pl.program_id(1)

        @pl.when(i == 0)
        def _():
            carry[...] = jnp.zeros_like(carry)

        @pl.when((i == 0) & (b == 0))
        def _():
            db_ref[...] = jnp.zeros_like(db_ref)

        df = rs_ref[...] - cs_ref[...]
        for p in range(P):
            rq = slice(p * LANES, (p + 1) * LANES)
            dqkv_ref[:, rq] = (dq_ref[:, p * PAIR_Q:p * PAIR_Q + LANES].astype(F32) * q_scale).astype(BF16)
            dqkv_ref[:, D + p * LANES:D + (p + 1) * LANES] = dkv_ref[:, p * PAIR_KV:p * PAIR_KV + LANES]
            dqkv_ref[:, 2 * D + p * LANES:2 * D + (p + 1) * LANES] = dkv_ref[:, p * PAIR_KV + PAIR_Q:(p + 1) * PAIR_KV]
        row = lax.broadcasted_iota(jnp.int32, (tt, tt), 0)
        col = lax.broadcasted_iota(jnp.int32, (tt, tt), 1)
        tri = (col >= row).astype(BF16)
        dlogf = _tri_sum(tri, df) + carry[...]
        carry[...] = dlogf[0:1, :]
        z = fl_ref[...] + b_ref[...]
        e = jnp.exp(-jnp.abs(z))
        sig_neg = jnp.where(z >= 0.0, e, 1.0) / (1.0 + e)
        dfl = dlogf * sig_neg
        dqkv_ref[:, 3 * D:3 * D + LANES] = dfl.astype(BF16)
        db_ref[...] += jnp.sum(dfl, axis=0, keepdims=True)

    rev = lambda b, i: (b * per + per - 1 - i, 0)
    const = lambda b, i: (0, 0)
    return pl.pallas_call(
        body, name=name, grid=(B, per),
        in_specs=[pl.BlockSpec((tt, P * PAIR_Q), rev), pl.BlockSpec((tt, P * PAIR_KV), rev),
                  pl.BlockSpec((tt, LANES), rev), pl.BlockSpec((tt, LANES), rev), pl.BlockSpec((tt, LANES), rev),
                  pl.BlockSpec((1, LANES), const)],
        out_specs=[pl.BlockSpec((tt, 3 * D + LANES), rev), pl.BlockSpec((1, LANES), const)],
        out_shape=[jax.ShapeDtypeStruct((T, 3 * D + LANES), BF16), jax.ShapeDtypeStruct((1, LANES), F32)],
        scratch_shapes=[pltpu.VMEM((1, LANES), F32)],
        compiler_params=_cparams(("arbitrary", "arbitrary")),
    )(dqx, dkvx, csum, rsum, fl, b_row)


def _rms(x):
    r = lax.rsqrt(jnp.mean(x * x, axis=-1, keepdims=True) + NORM_EPS)
    return x * r, r


def _mla_mid(lat, gq, gkv, cos_t, sin_s, *, name):
    T, W = lat.shape
    Rq = W - 2 * LANES
    tt = _tile(T, 512)

    def body(l_ref, gq_ref, gkv_ref, c_ref, s_ref, o_ref, ot_ref):
        nq, _ = _rms(l_ref[:, 0:Rq])
        nkv, _ = _rms(l_ref[:, Rq:Rq + LANES])
        parts = [nq * gq_ref[...], nkv * gkv_ref[...], _rope128(l_ref[:, Rq + LANES:W], c_ref[...], s_ref[...])]
        out = jnp.concatenate(parts, axis=1)
        o_ref[...] = out.astype(BF16)
        ot_ref[...] = out.T.astype(BF16)

    return pl.pallas_call(
        body, name=name, grid=(T // tt,),
        in_specs=[pl.BlockSpec((tt, W), lambda i: (i, 0)), pl.BlockSpec((1, Rq), lambda i: (0, 0)),
                  pl.BlockSpec((1, LANES), lambda i: (0, 0)), pl.BlockSpec((tt, LANES), lambda i: (i, 0)),
                  pl.BlockSpec((tt, LANES), lambda i: (i, 0))],
        out_specs=[pl.BlockSpec((tt, W), lambda i: (i, 0)), pl.BlockSpec((W, tt), lambda i: (0, i))],
        out_shape=[jax.ShapeDtypeStruct((T, W), BF16), jax.ShapeDtypeStruct((W, T), BF16)],
        compiler_params=_cparams(("parallel",)),
    )(lat, gq, gkv, cos_t, sin_s)


def _mla_mid_bwd(lat, dcq, dckr, gq, gkv, cos_t, sin_s, *, name):
    T, W = lat.shape
    Rq = W - 2 * LANES
    tt = _tile(T, 512)

    def norm_bwd(x, dy, g):
        n, r = _rms(x)
        dn = dy * g
        return r * (dn - n * jnp.mean(dn * n, axis=-1, keepdims=True)), jnp.sum(dy * n, axis=0, keepdims=True)

    def body(l_ref, dq_ref, dk_ref, gq_ref, gkv_ref, c_ref, s_ref, o_ref, dgq_ref, dgkv_ref):
        i = pl.program_id(0)

        @pl.when(i == 0)
        def _():
            dgq_ref[...] = jnp.zeros_like(dgq_ref)
            dgkv_ref[...] = jnp.zeros_like(dgkv_ref)

        dxq, dgq = norm_bwd(l_ref[:, 0:Rq], dq_ref[...], gq_ref[...])
        dxkv, dgkv = norm_bwd(l_ref[:, Rq:Rq + LANES], dk_ref[:, 0:LANES], gkv_ref[...])
        o_ref[:, 0:Rq] = dxq.astype(BF16)
        o_ref[:, Rq:Rq + LANES] = dxkv.astype(BF16)
        o_ref[:, Rq + LANES:W] = _rope128(dk_ref[:, LANES:2 * LANES], c_ref[...], -s_ref[...]).astype(BF16)
        dgq_ref[...] += dgq
        dgkv_ref[...] += dgkv

    return pl.pallas_call(
        body, name=name, grid=(T // tt,),
        in_specs=[pl.BlockSpec((tt, W), lambda i: (i, 0)), pl.BlockSpec((tt, Rq), lambda i: (i, 0)),
                  pl.BlockSpec((tt, 2 * LANES), lambda i: (i, 0)), pl.BlockSpec((1, Rq), lambda i: (0, 0)),
                  pl.BlockSpec((1, LANES), lambda i: (0, 0)), pl.BlockSpec((tt, LANES), lambda i: (i, 0)),
                  pl.BlockSpec((tt, LANES), lambda i: (i, 0))],
        out_specs=[pl.BlockSpec((tt, W), lambda i: (i, 0)), pl.BlockSpec((1, Rq), lambda i: (0, 0)),
                   pl.BlockSpec((1, LANES), lambda i: (0, 0))],
        out_shape=[jax.ShapeDtypeStruct((T, W), BF16), jax.ShapeDtypeStruct((1, Rq), F32),
                   jax.ShapeDtypeStruct((1, LANES), F32)],
        compiler_params=_cparams(("arbitrary",)),
    )(lat, dcq, dckr, gq, gkv, cos_t, sin_s)


def _uq_to_pairs(w):
    Rq = w.shape[0]
    P = w.shape[1] // (2 * (HEAD_DIM + ROPE_DIM))
    w4 = w.reshape(Rq, P, 2, HEAD_DIM + ROPE_DIM)
    nope = w4[..., :HEAD_DIM].reshape(Rq, P, 2 * HEAD_DIM)
    rope = w4[..., HEAD_DIM:].reshape(Rq, P, 2 * ROPE_DIM)
    pad = jnp.zeros((Rq, P, PAIR_Q - 2 * HEAD_DIM - 2 * ROPE_DIM), w.dtype)
    return jnp.concatenate([nope, rope, pad], axis=-1).reshape(Rq, P * PAIR_Q)


def _uq_from_pairs(g):
    Rq = g.shape[0]
    P = g.shape[1] // PAIR_Q
    g3 = g.reshape(Rq, P, PAIR_Q)
    nope = g3[..., :2 * HEAD_DIM].reshape(Rq, P, 2, HEAD_DIM)
    rope = g3[..., 2 * HEAD_DIM:2 * HEAD_DIM + 2 * ROPE_DIM].reshape(Rq, P, 2, ROPE_DIM)
    return jnp.concatenate([nope, rope], axis=-1).reshape(Rq, P * 2 * (HEAD_DIM + ROPE_DIM))


def _ukv_to_pairs(w):
    P = w.shape[1] // (4 * HEAD_DIM)
    w4 = w.reshape(KV_RANK, P, 2, 2 * HEAD_DIM)
    kn = w4[..., :HEAD_DIM].reshape(KV_RANK, P, 2 * HEAD_DIM)
    vv = w4[..., HEAD_DIM:].reshape(KV_RANK, P, 2 * HEAD_DIM)
    top = jnp.concatenate([kn, jnp.zeros((KV_RANK, P, LANES), w.dtype), vv], axis=-1)
    place = np.zeros((LANES, P, PAIR_KV), np.float32)
    for r in range(ROPE_DIM):
        place[r, :, LANES + r] = 1.0
        place[r, :, LANES + ROPE_DIM + r] = 1.0
    return jnp.concatenate([top, jnp.asarray(place, w.dtype)], axis=0).reshape(KV_RANK + LANES, P * PAIR_KV)


def _ukv_from_pairs(g):
    P = g.shape[1] // PAIR_KV
    g3 = g[:KV_RANK].reshape(KV_RANK, P, PAIR_KV)
    kn = g3[..., :2 * HEAD_DIM].reshape(KV_RANK, P, 2, HEAD_DIM)
    vv = g3[..., PAIR_Q:].reshape(KV_RANK, P, 2, HEAD_DIM)
    return jnp.concatenate([kn, vv], axis=-1).reshape(KV_RANK, P * 4 * HEAD_DIM)


def _residual_then_norm(acc, xr, g, gain, sc, sh):
    x_out = xr + g * acc
    r = lax.rsqrt(jnp.mean(x_out * x_out, axis=-1, keepdims=True) + NORM_EPS)
    h = (x_out * r) * gain * (1.0 + sc) + sh
    return x_out, acc, h, h


def _gated_out(a, w_stack, layer, x, gate, next_norm, *, S, name):
    if next_norm is None:
        return _mm(a, w_stack, "nn", name=name, b_layer=layer, out_dtypes=(F32, BF16), extras=(x,), rowvecs=(gate,),
                   seq=S, epilogue=lambda acc, xr, g: (xr + g * acc, acc)) + (None, None)
    long_k = a.shape[1] > 2048
    return _mm(a, w_stack, "nn", name=name, b_layer=layer, out_dtypes=(F32, BF16, BF16, BF16),
               out_t=(False, False, False, True), extras=(x,), rowvecs=(gate,) + tuple(next_norm), seq=S,
               full_rows=True, tk=a.shape[1], epilogue=_residual_then_norm,
               vmem_budget=MM_VMEM_BUDGET + (8 * 1024 * 1024 if long_k else 0))


def _mlp_fwd(h2, w, i, x1, gate, next_norm, *, S):
    def act(acc):
        u = jnp.square(jnp.maximum(acc, 0.0))
        return acc, u, u

    p, u, u_t = _mm(h2, w["mlp_w1"], "nn", name=f"mlp_up_{i}", b_layer=i, out_dtypes=(BF16, BF16, BF16),
                    out_t=(False, False, True), epilogue=act)
    x2, z, h, h_t = _gated_out(u, w["mlp_w2"], i, x1, gate, next_norm, S=S, name=f"mlp_down_{i}")
    return x2, (p, u_t, z), h, h_t


STACKED_GRADS = ("fox_out", "mla_down", "mla_uq", "mla_ukv", "mla_out", "mlp_w1", "mlp_w2")


def _local_step(x, target, pos_f, inv_freq_row, sign_row, mod, w, slots, *, S, hooks=None):
    hooks = hooks or {}
    T, D = x.shape
    L = mod.shape[0]
    L2 = len(w["fox_out"])
    cos_t, sin_s = _rope_tables(pos_f, inv_freq_row, sign_row)
    saved = []
    B = mod.shape[2]

    def per_sequence(gain):
        return jnp.broadcast_to(gain[None], (B,) + gain.shape)

    h, h_t = _norm_mod(x, w["norm_mix_g"][0], mod[0, 1], mod[0, 0], S=S, name="norm_mix_0")
    for i in range(L):
        j = i // 2
        sh_m, sc_m, g_m, sh_f, sc_f, g_f = (mod[i, s] for s in range(6))
        if i % 2 == 0:
            qkv = _mm(h, w["fox_qkv"], "nn", name=f"fox_qkv_{i}", b_layer=j, out_dtypes=(BF16,))
            fl = _mm(h, w["fox_f"], "nn", name=f"fox_f_{i}", b_layer=j)
            qx, kvx = _fox_prep(qkv, fl, w["fox_b"][j], S=S, D=D, name=f"fox_prep_{i}")
            o, lse, o_t = _attn_fwd(qx, kvx, S=S, ew=FOX_EXTRA, name=f"fox_attn_{i}")
            mix = (qx, kvx, o, lse, o_t, fl)
            w_out = w["fox_out"]
        else:
            lat = _mm(h, w["mla_down"], "nn", name=f"mla_down_{i}", b_layer=j)
            Rq = lat.shape[1] - 2 * LANES
            cqr, cqr_t = _mla_mid(lat, w["mla_gq"][j], w["mla_gkv"][j], cos_t, sin_s, name=f"mla_mid_{i}")
            qx = _mm(cqr, w["mla_uq"], "nn", name=f"mla_uq_{i}", b_layer=j, out_dtypes=(BF16,), a_sz=Rq, tk=Rq,
                     tables=(cos_t, sin_s), epilogue=lambda acc, c, s: (_rope_pairs(acc * MLA_SCALE, c, s, 1.0),))
            kvx = _mm(cqr, w["mla_ukv"], "nn", name=f"mla_ukv_{i}", b_layer=j, out_dtypes=(BF16,), a_off=Rq,
                      a_sz=2 * LANES, tk=2 * LANES, tn=PAIR_KV)
            o, lse, o_t = _attn_fwd(qx, kvx, S=S, ew=ROPE_DIM, name=f"mla_attn_{i}")
            mix = (qx, kvx, o, lse, o_t, lat, cqr_t)
            w_out = w["mla_out"]
        x1, y, h2, h2_t = _gated_out(o, w_out, j, x, g_m, (per_sequence(w["norm_mlp_g"][i]), sc_f, sh_f), S=S,
                                     name=f"mix_out_{i}")
        if i == 0 and "fwd_mlp0" in hooks:
            w = hooks["fwd_mlp0"](x1, w)
        next_norm = (per_sequence(w["norm_mix_g"][i + 1]), mod[i + 1, 1], mod[i + 1, 0]) if i + 1 < L else None
        x2, mlp, h_next, h_next_t = _mlp_fwd(h2, w, i, x1, g_f, next_norm, S=S)
        saved.append((x, h_t, mix, y, x1, h2_t, mlp))
        x, h, h_t = x2, h_next, h_next_t
        if i == 0 and "fwd_layer1" in hooks:
            w = hooks["fwd_layer1"](x, w)

    dx, dg_final, loss = _final_loss(x, target, w["final_norm_g"])
    n_split = w["mlp_w1"][0][0].shape[1]

    grads = {k: [None] * len(w[k]) for k in ("norm_mix_g", "norm_mlp_g", "fox_b", "mla_gq", "mla_gkv")}
    grads["fox_in"] = [None] * L2
    grads.update({k: {} for k in STACKED_GRADS})
    grads["final_norm_g"] = dg_final

    def stacked(key, layer, _, a_t, b, **kw):
        group, idx, count = slots[(key, layer)]
        grads[key][group] = _mm(a_t, b, "nn", out_stack=(grads[key].get(group), idx, count), **kw)

    dmod = [None] * L
    for i in reversed(range(L)):
        j = i // 2
        x0, h_t, mix, y, x1, h2_t, (p, u_t, z) = saved[i]
        sh_m, sc_m, g_m, sh_f, sc_f, g_f = (mod[i, s] for s in range(6))
        if i == 0 and "bwd_layer0" in hooks:
            g_f = g_f + hooks["bwd_layer0"](grads)[0, 0]
        dz, dg_f = _gate_bwd(dx, z, g_f, S=S, name=f"gate_mlp_bwd_{i}")
        stacked("mlp_w2", i, L, u_t, dz, name=f"mlp_w2_grad_{i}")
        dp = _mm(dz, w["mlp_w2"], "nt", name=f"mlp_down_bwd_{i}", b_layer=i, out_dtypes=(BF16,), extras=(p,),
                 epilogue=lambda acc, pv: (acc * (2.0 * jnp.maximum(pv.astype(F32), 0.0)),))
        stacked("mlp_w1", i, L, h2_t, dp, name=f"mlp_w1_grad_{i}", out_split=n_split)
        if i == 0 and "bwd_mix0" in hooks:
            g_m = g_m + hooks["bwd_mix0"](grads)[0, 0]
        dh2 = _mm(dp, w["mlp_w1"], "nt", name=f"mlp_up_bwd_{i}", b_layer=i, out_dtypes=(BF16,))
        dx1, dsh_f, dsc_f, dgn = _norm_mod_bwd(x1, dh2, dx, w["norm_mlp_g"][i], sc_f, S=S, name=f"norm_mlp_bwd_{i}")
        grads["norm_mlp_g"][i] = dgn
        dy, dg_m = _gate_bwd(dx1, y, g_m, S=S, name=f"gate_mix_bwd_{i}")
        if i % 2 == 0:
            qx, kvx, o, lse, o_t, fl = mix
            stacked("fox_out", j, L2, o_t, dy, name=f"fox_out_grad_{i}")
            do = _mm(dy, w["fox_out"], "nt", name=f"fox_out_bwd_{i}", b_layer=j, out_dtypes=(BF16,))
            dqx, dkvx, csum, rsum = _attn_bwd(qx, kvx, o, lse, do, S=S, ew=FOX_EXTRA, name=f"fox_attn_bwd_{i}",
                                              bias_grad=True)
            n_heads = D // HEAD_DIM
            csum = jnp.pad(csum.reshape(T, n_heads, HEAD_DIM)[:, :, 0], ((0, 0), (0, LANES - n_heads)))
            rsum = jnp.transpose(rsum[:, :, :2, :], (0, 3, 1, 2)).reshape(T, n_heads)
            rsum = jnp.pad(rsum, ((0, 0), (0, LANES - n_heads)))
            dproj, db = _fox_unprep(dqx, dkvx, csum, rsum, fl, w["fox_b"][j], S=S, D=D, name=f"fox_unprep_{i}")
            grads["fox_b"][j] = db
            grads["fox_in"][j] = _mm(h_t, dproj, "nn", name=f"fox_in_grad_{i}")
            dh = _mm(dproj, w["fox_in"], "nt", name=f"fox_in_bwd_{i}", b_layer=j, out_dtypes=(BF16,),
                     tk=dproj.shape[1])
        else:
            qx, kvx, o, lse, o_t, lat, cqr_t = mix
            Rq = lat.shape[1] - 2 * LANES
            stacked("mla_out", j, L2, o_t, dy, name=f"mla_out_grad_{i}")
            do = _mm(dy, w["mla_out"], "nt", name=f"mla_out_bwd_{i}", b_layer=j, out_dtypes=(BF16,))
            dqx, dkvx = _attn_bwd(qx, kvx, o, lse, do, S=S, ew=ROPE_DIM, name=f"mla_attn_bwd_{i}")
            dqpre = _unrope(dqx, cos_t, sin_s)
            stacked("mla_uq", j, L2, cqr_t[:Rq], dqpre, name=f"mla_uq_grad_{i}", out_split=n_split)
            stacked("mla_ukv", j, L2, cqr_t[Rq:], dkvx, name=f"mla_ukv_grad_{i}", tn=PAIR_KV, out_split=n_split)
            dcq = _mm(dqpre, w["mla_uq"], "nt", name=f"mla_uq_bwd_{i}", b_layer=j)
            dckr = _mm(dkvx, w["mla_ukv"], "nt", name=f"mla_ukv_bwd_{i}", b_layer=j, tk=PAIR_KV * 2)
            dlat, dgq, dgkv = _mla_mid_bwd(lat, dcq, dckr, w["mla_gq"][j], w["mla_gkv"][j], cos_t, sin_s,
                                           name=f"mla_mid_bwd_{i}")
            grads["mla_gq"][j] = dgq
            grads["mla_gkv"][j] = dgkv
            stacked("mla_down", j, L2, h_t, dlat, name=f"mla_down_grad_{i}")
            dh = _mm(dlat, w["mla_down"], "nt", name=f"mla_down_bwd_{i}", b_layer=j, out_dtypes=(BF16,))
        dx, dsh_m, dsc_m, dgn = _norm_mod_bwd(x0, dh, dx1, w["norm_mix_g"][i], sc_m, S=S, name=f"norm_mix_bwd_{i}")
        grads["norm_mix_g"][i] = dgn
        dmod[i] = jnp.stack([dsh_m, dsc_m, dg_m, dsh_f, dsc_f, dg_f])
    return loss, dx, jnp.stack(dmod), grads


GATHERED = ("fox_in", "fox_out", "mla_down", "mla_uq", "mla_ukv", "mla_out", "mlp_w1", "mlp_w2")
ROW_SHARDED = ("fox_out", "mla_down", "mla_out", "mlp_w2")


def _shard_layouts(wts):
    dkv = wts["mla_w_dkv"]
    dkv = jnp.pad(dkv, ((0, 0), (0, 0), (0, 2 * LANES - dkv.shape[2])))
    return {
        "fox_in": _pad_lanes(wts["fox_w_in"].astype(BF16)),
        "fox_out": wts["fox_w_out"].astype(BF16),
        "mla_down": jnp.concatenate([wts["mla_w_dq"], dkv], axis=2).astype(BF16),
        "mla_uq": jax.vmap(_uq_to_pairs)(wts["mla_w_uq"].astype(BF16)),
        "mla_ukv": jax.vmap(_ukv_to_pairs)(wts["mla_w_ukv"].astype(BF16)),
        "mla_out": wts["mla_w_out"].astype(BF16),
        "mlp_w1": wts["mlp_w1"].astype(BF16),
        "mlp_w2": wts["mlp_w2"].astype(BF16),
    }


def _small_layouts(small):
    return {
        "fox_b": [jnp.pad(b, (0, LANES - b.shape[0]))[None, :] for b in small["fox_b_f"]],
        "mla_gq": [g[None, :] for g in small["mla_q_norm_g"]],
        "mla_gkv": [g[None, :] for g in small["mla_kv_norm_g"]],
        "norm_mix_g": [g[None, :] for g in small["norm_mix_g"]],
        "norm_mlp_g": [g[None, :] for g in small["norm_mlp_g"]],
        "final_norm_g": small["final_norm_g"][None, :],
    }


def _comm_groups(L, L2):
    rest = [("fox_in", 1, L2 - 1), ("fox_out", 1, L2 - 1), ("mla_down", 0, L2), ("mla_uq", 0, L2),
            ("mla_ukv", 0, L2), ("mla_out", 0, L2), ("mlp_w1", 1, L - 1), ("mlp_w2", 1, L - 1)]
    return {"mix0": [("fox_in", 0, 1), ("fox_out", 0, 1)], "mlp0": [("mlp_w1", 0, 1), ("mlp_w2", 0, 1)],
            "rest": [e for e in rest if e[2] > 0]}


def _layer_slots(groups):
    return {(n, s + l): (g, l, cnt) for g, entries in groups.items() for n, s, cnt in entries for l in range(cnt)}


def _pad_lanes(a):
    cols = a.shape[-1]
    return jnp.pad(a, [(0, 0)] * (a.ndim - 1) + [(0, -cols % LANES)])


def _weight_views(name, gathered, D, n_fox_heads):
    n, ns, rows, cols = gathered.shape
    if name == "fox_in":
        true_cols = (3 * D + n_fox_heads) // ns
        fox = jnp.concatenate([gathered[:, k, :, :true_cols] for k in range(ns)], axis=-1)
        return {"fox_qkv": fox[:, :, :3 * D], "fox_f": _pad_lanes(fox[:, :, 3 * D:]), "fox_in": _pad_lanes(fox)}
    if name in ROW_SHARDED:
        return {name: gathered.reshape(n, ns * rows, cols)}
    return {name: gathered}


def _grad_pieces(name, g, qkv_f, n_fox_heads, ns):
    if name == "fox_in":
        D = qkv_f[0].shape[0]
        fox = jnp.stack([a[:, :3 * D + n_fox_heads] for a in qkv_f])
        cols = fox.shape[2] // ns
        return jnp.stack([_pad_lanes(fox[:, :, k * cols:(k + 1) * cols]) for k in range(ns)], axis=1)
    if name in ROW_SHARDED:
        return g.reshape(g.shape[0], ns, g.shape[1] // ns, g.shape[2])
    return g


def _small_grads(g, n_fox_heads):
    return {
        "norm_mix_g": jnp.concatenate(g["norm_mix_g"], axis=0),
        "norm_mlp_g": jnp.concatenate(g["norm_mlp_g"], axis=0),
        "final_norm_g": g["final_norm_g"][0],
        "fox_b_f": jnp.concatenate(g["fox_b"], axis=0)[:, :n_fox_heads],
        "mla_q_norm_g": jnp.concatenate(g["mla_gq"], axis=0),
        "mla_kv_norm_g": jnp.concatenate(g["mla_gkv"], axis=0),
    }


def _silu(c):
    return c * (1.0 / (1.0 + jnp.exp(-c)))


def _ada_fwd(c_all, ada_w, ada_b_cols):
    L, D, C = ada_w.shape
    Bg = c_all.shape[0]
    tc = _tile(C, 512)

    def body(c_ref, w_ref, b_ref, o_ref):
        ca = _silu(c_ref[...]).astype(BF16)
        o_ref[...] = jnp.dot(ca, w_ref[...].astype(BF16), preferred_element_type=F32) + b_ref[...]

    return pl.pallas_call(
        body, name="ada_fwd", grid=(L, C // tc),
        in_specs=[pl.BlockSpec((Bg, D), lambda l, j: (0, 0)), pl.BlockSpec((None, D, tc), lambda l, j: (l, 0, j)),
                  pl.BlockSpec((None, 1, tc), lambda l, j: (l, 0, j))],
        out_specs=pl.BlockSpec((None, Bg, tc), lambda l, j: (l, 0, j)),
        out_shape=jax.ShapeDtypeStruct((L, Bg, C), F32),
        compiler_params=_cparams(("parallel", "parallel")),
    )(c_all, ada_w, ada_b_cols)


def _ada_bwd(c_all, dmod_cols):
    L, Bg, C = dmod_cols.shape
    D = c_all.shape[1]
    tc = _tile(C, 512)

    def body(c_ref, d_ref, o_ref):
        ca = _silu(c_ref[...]).astype(BF16)
        o_ref[...] = _dot_tn(ca, d_ref[...].astype(BF16))

    return pl.pallas_call(
        body, name="ada_bwd", grid=(L, C // tc),
        in_specs=[pl.BlockSpec((Bg, D), lambda l, j: (0, 0)), pl.BlockSpec((None, Bg, tc), lambda l, j: (l, 0, j))],
        out_specs=pl.BlockSpec((None, D, tc), lambda l, j: (l, 0, j)),
        out_shape=jax.ShapeDtypeStruct((L, D, C), F32),
        compiler_params=_cparams(("parallel", "parallel")),
    )(c_all, dmod_cols)


def _adamw_update(w, gv, m, v):
    mn = ADAM_B1 * m + (1.0 - ADAM_B1) * gv
    vn = ADAM_B2 * v + (1.0 - ADAM_B2) * jnp.square(gv)
    m_hat = mn / (1.0 - ADAM_B1 ** ADAM_STEP)
    v_hat = vn / (1.0 - ADAM_B2 ** ADAM_STEP)
    return -ADAM_LR * (m_hat / (jnp.sqrt(v_hat) + ADAM_EPS) + ADAM_WD * w), mn, vn


def _adamw(w, g, m, v, *, name):
    shape = w.shape
    C = shape[-1]
    R = int(np.prod(shape[:-1])) if len(shape) > 1 else 1
    w2, g2, m2, v2 = (a.reshape(R, C) for a in (w, g, m, v))
    tr = _row_tile(R, C)

    def body(w_ref, g_ref, m_ref, v_ref, d_ref, nm_ref, nv_ref):
        d_ref[...], nm_ref[...], nv_ref[...] = _adamw_update(w_ref[...], g_ref[...], m_ref[...], v_ref[...])

    spec = pl.BlockSpec((tr, C), lambda i: (i, 0))
    out = pl.pallas_call(
        body, name=name, grid=(R // tr,), in_specs=[spec] * 4, out_specs=[spec] * 3,
        out_shape=[jax.ShapeDtypeStruct((R, C), F32)] * 3, compiler_params=_cparams(("parallel",)),
    )(w2, g2, m2, v2)
    return tuple(a.reshape(shape) for a in out)


def _adamw_halves(w, g_own, g_peer, m, v, c_idx, *, name):
    L, rows, C = w.shape
    R = rows // 2
    tr = _row_tile(R, C)

    def body(c_ref, w_ref, go_ref, gp_ref, m_ref, v_ref, g_ref, d_ref, nm_ref, nv_ref):
        gv = jnp.where(pl.program_id(1) == c_ref[0], go_ref[...], gp_ref[...])
        g_ref[...] = gv
        d_ref[...], nm_ref[...], nv_ref[...] = _adamw_update(w_ref[...], gv, m_ref[...], v_ref[...])

    full = pl.BlockSpec((None, None, tr, C), lambda l, hh, i, c_ref: (l, hh, i, 0))
    half = pl.BlockSpec((None, tr, C), lambda l, hh, i, c_ref: (l, i, 0))
    grid_spec = pltpu.PrefetchScalarGridSpec(
        num_scalar_prefetch=1, grid=(L, 2, R // tr), in_specs=[full, half, half, full, full], out_specs=[full] * 4)
    split = lambda a: a.reshape(L, 2, R, C)
    out = pl.pallas_call(
        body, name=name, grid_spec=grid_spec, out_shape=[jax.ShapeDtypeStruct((L, 2, R, C), F32)] * 4,
        compiler_params=_cparams(("parallel", "parallel", "parallel")),
    )(c_idx, split(w), g_own, g_peer, split(m), split(v))
    return tuple(a.reshape(w.shape) for a in out)


def _sum_gathered(dm8, sm8):
    n_dev, Bl, R, D = dm8.shape
    Rs = sm8.shape[1]

    def body(dm_ref, sm_ref, ob_ref, os_ref):
        acc_b = jnp.zeros((R, D), F32)
        acc_s = jnp.zeros((Rs, D), F32)
        for d in range(n_dev):
            for b in range(Bl):
                acc_b = acc_b + dm_ref[d, b]
            acc_s = acc_s + sm_ref[d]
        ob_ref[...] = acc_b
        os_ref[...] = acc_s

    return pl.pallas_call(
        body, name="sum_gathered",
        out_shape=[jax.ShapeDtypeStruct((R, D), F32), jax.ShapeDtypeStruct((Rs, D), F32)],
        compiler_params=_cparams(None),
    )(dm8, sm8)


N_DEV = 8
N_CHIP = 4
ANY = pl.BlockSpec(memory_space=pl.ANY)
HBM = pl.BlockSpec(memory_space=pltpu.HBM)
SEM = pl.BlockSpec(memory_space=pltpu.SEMAPHORE)
DATAFLOW = pltpu.SideEffectType.DATAFLOW_SIDE_EFFECTING


def _mesh_pos():
    return lax.axis_index("x"), lax.axis_index("y"), lax.axis_index("c")


def _all_gather8(block, *, name, in_vmem):
    R, W = block.shape

    def body(x_ref, out_ref, send_sems, recv_sems, local_sem):
        x, y, c = _mesh_pos()
        me, sibling = (x, y, c), (x, y, 1 - c)
        chips = [(1 - x, y), (x, 1 - y), (1 - x, 1 - y)]

        def slot(px, py, pc):
            return out_ref.at[4 * px + 2 * py + pc]

        def copy(k, blk, to, src=None):
            return pltpu.make_async_remote_copy(
                src_ref=slot(*blk) if src is None else src, dst_ref=slot(*blk),
                send_sem=send_sems.at[k], recv_sem=recv_sems.at[k], device_id=to, device_id_type=MESH_ID)

        mine = pltpu.make_async_copy(x_ref, slot(*me), local_sem)
        mine.start()
        first = [copy(0, me, sibling, src=x_ref)]
        first += [copy(1 + j, me, (*chip, c), src=x_ref) for j, chip in enumerate(chips)]
        for cp in first:
            cp.start()
        passed = [copy(4 + j, (*chip, c), sibling) for j, chip in enumerate(chips)]
        for j, chip in enumerate(chips):
            copy(1 + j, (*chip, c), me).wait_recv()
            passed[j].start()
        copy(0, sibling, me).wait_recv()
        for j, chip in enumerate(chips):
            copy(4 + j, (*chip, 1 - c), me).wait_recv()
        for cp in first + passed:
            cp.wait_send()
        mine.wait()

    space = pl.BlockSpec(memory_space=pltpu.VMEM) if in_vmem else ANY
    return pl.pallas_call(
        body, name=name, out_shape=jax.ShapeDtypeStruct((N_DEV, R, W), block.dtype),
        in_specs=[space], out_specs=space,
        scratch_shapes=[pltpu.SemaphoreType.DMA((7,)), pltpu.SemaphoreType.DMA((7,)), pltpu.SemaphoreType.DMA],
        compiler_params=pltpu.CompilerParams(vmem_limit_bytes=VMEM_LIMIT_V7X),
    )(block)


def _comm_call(body, arrays, out_shapes, n_sems, *, name):
    return pl.pallas_call(
        body, name=name, out_shape=out_shapes, in_specs=[ANY] * len(arrays), out_specs=[ANY] * len(out_shapes),
        scratch_shapes=[pltpu.SemaphoreType.DMA((n_sems,)), pltpu.SemaphoreType.DMA((n_sems,)),
                        pltpu.SemaphoreType.DMA((len(arrays),))],
    )(*arrays)


def _gather_weights(shards, *, name):
    n = len(shards)

    def body(*refs):
        xs, outs = refs[:n], refs[n:2 * n]
        send_sems, recv_sems, local_sems = refs[2 * n:]
        x, y, c = _mesh_pos()
        me, sibling = (x, y, c), (x, y, 1 - c)
        chips = [(1 - x, y), (x, 1 - y), (1 - x, 1 - y)]
        waits = []
        for i in range(n):
            nl = shards[i].shape[0]
            own = xs[i].at[pl.ds(0, nl), c]

            def slot(px, py, pc, i=i, nl=nl):
                return outs[i].at[pl.ds(0, nl), 2 * px + py, pc]

            def copy(k, blk, to, src=None, i=i, slot=slot):
                return pltpu.make_async_remote_copy(
                    src_ref=slot(*blk) if src is None else src, dst_ref=slot(*blk),
                    send_sem=send_sems.at[7 * i + k], recv_sem=recv_sems.at[7 * i + k], device_id=to,
                    device_id_type=MESH_ID)

            mine = pltpu.make_async_copy(own, slot(*me), local_sems.at[i])
            mine.start()
            first = [copy(0, me, sibling, src=own)]
            first += [copy(1 + j, me, (*chip, c), src=own) for j, chip in enumerate(chips)]
            for cp in first:
                cp.start()
            waits.append((copy, mine, first))
        for copy, mine, first in waits:
            passed = [copy(4 + j, (*chip, c), sibling) for j, chip in enumerate(chips)]
            for j, chip in enumerate(chips):
                copy(1 + j, (*chip, c), me).wait_recv()
                passed[j].start()
            copy(0, sibling, me).wait_recv()
            for j, chip in enumerate(chips):
                copy(4 + j, (*chip, 1 - c), me).wait_recv()
            for cp in first + passed:
                cp.wait_send()
            mine.wait()

    out_shapes = [jax.ShapeDtypeStruct((s.shape[0], N_CHIP) + s.shape[1:], s.dtype) for s in shards]
    return _comm_call(body, shards, out_shapes, 7 * n, name=name)


def _place_own(shard, chip_idx, c_idx, *, name):
    n, _, rows, cols = shard.shape
    tr = _row_tile(rows, cols)

    def body(k_ref, c_ref, x_ref, o_ref):
        o_ref[...] = x_ref[...]

    grid_spec = pltpu.PrefetchScalarGridSpec(
        num_scalar_prefetch=2, grid=(n, rows // tr),
        in_specs=[pl.BlockSpec((None, None, tr, cols), lambda l, i, k_ref, c_ref: (l, c_ref[0], i, 0))],
        out_specs=pl.BlockSpec((None, None, None, tr, cols), lambda l, i, k_ref, c_ref: (l, k_ref[0], c_ref[0], i, 0)))
    return pl.pallas_call(
        body, name=name, grid_spec=grid_spec,
        out_shape=jax.ShapeDtypeStruct((n, N_CHIP, 2, rows, cols), shard.dtype),
        compiler_params=_cparams(("parallel", "parallel")),
    )(chip_idx, c_idx, shard)


def _gather_copies(x_refs, land_refs, send_sems, recv_sems):
    x, y, c = _mesh_pos()
    k_me = 2 * x + y
    targets = [(x, y, 1 - c), (1 - x, y, c), (x, 1 - y, c), (1 - x, 1 - y, c)]
    copies = []
    for i, (x_ref, land_ref) in enumerate(zip(x_refs, land_refs)):
        nl = x_ref.shape[0]
        for j, to in enumerate(targets):
            copies.append(pltpu.make_async_remote_copy(
                src_ref=x_ref.at[pl.ds(0, nl), c], dst_ref=land_ref.at[pl.ds(0, nl), k_me, c],
                send_sem=send_sems.at[4 * i + j], recv_sem=recv_sems.at[4 * i + j], device_id=to,
                device_id_type=MESH_ID))
    return copies


def _split_start(copies_fn, srcs, lands, after, *, name, sems_per_array):
    n = len(srcs)

    def body(*refs):
        send_sems, recv_sems = refs[2 * n + 1], refs[2 * n + 2]
        for cp in copies_fn(refs[:n], refs[n:2 * n], send_sems, recv_sems):
            cp.start()
        refs[-1][...] = jnp.zeros_like(refs[-1])

    operands = [pltpu.with_memory_space_constraint(a, pltpu.HBM) for a in list(srcs) + list(lands)]
    n_sems = sems_per_array * n
    out_shape = ([pltpu.SemaphoreType.DMA((n_sems,)), pltpu.SemaphoreType.DMA((n_sems,))]
                 + [pltpu.HBM(a.shape, a.dtype) for a in operands] + [jax.ShapeDtypeStruct((8, LANES), F32)])
    res = pl.pallas_call(
        body, name=name, out_shape=out_shape, in_specs=[HBM] * (2 * n) + [ANY],
        out_specs=[SEM, SEM] + [HBM] * (2 * n) + [pl.BlockSpec(memory_space=pltpu.VMEM)],
        input_output_aliases={i: 2 + i for i in range(2 * n)},
        compiler_params=pltpu.CompilerParams(has_side_effects=DATAFLOW),
    )(*operands, after)
    return res[0], res[1], list(res[2:2 + n]), list(res[2 + n:2 + 2 * n]), res[-1]


def _split_wait(copies_fn, send_sems, recv_sems, srcs, lands, after, *, name):
    n = len(srcs)

    def body(*refs):
        for cp in copies_fn(refs[:n], refs[n:2 * n], refs[2 * n], refs[2 * n + 1]):
            cp.wait_send()
            cp.wait_recv()

    res = pl.pallas_call(
        body, name=name, out_shape=[pltpu.HBM(a.shape, a.dtype) for a in list(srcs) + list(lands)],
        in_specs=[HBM] * (2 * n) + [SEM, SEM, ANY], out_specs=[HBM] * (2 * n),
        input_output_aliases={i: i for i in range(2 * n)},
        compiler_params=pltpu.CompilerParams(has_side_effects=DATAFLOW),
    )(*srcs, *lands, send_sems, recv_sems, after)
    return list(res[:n]), list(res[n:])


def _gather_forward(lands, *, name):
    n = len(lands)

    def body(*refs):
        xs = refs[:n]
        send_sems, recv_sems, _ = refs[2 * n:]
        x, y, c = _mesh_pos()
        chips = [(1 - x, y), (x, 1 - y), (1 - x, 1 - y)]
        copies = []
        for i in range(n):
            nl = lands[i].shape[0]
            for j, (cx, cy) in enumerate(chips):
                here = xs[i].at[pl.ds(0, nl), 2 * cx + cy, c]
                cp = pltpu.make_async_remote_copy(
                    src_ref=here, dst_ref=here, send_sem=send_sems.at[3 * i + j], recv_sem=recv_sems.at[3 * i + j],
                    device_id=(x, y, 1 - c), device_id_type=MESH_ID)
                cp.start()
                copies.append(cp)
        for cp in copies:
            cp.wait()

    return pl.pallas_call(
        body, name=name, out_shape=[jax.ShapeDtypeStruct(a.shape, a.dtype) for a in lands],
        in_specs=[ANY] * n, out_specs=[ANY] * n, input_output_aliases={i: i for i in range(n)},
        scratch_shapes=[pltpu.SemaphoreType.DMA((3 * n,)), pltpu.SemaphoreType.DMA((3 * n,)),
                        pltpu.SemaphoreType.DMA((1,))],
    )(*lands)


def _pair_copies(g_refs, land_refs, send_sems, recv_sems):
    x, y, c = _mesh_pos()
    copies = []
    for i, (g_ref, land_ref) in enumerate(zip(g_refs, land_refs)):
        nl, ns = g_ref.shape[:2]
        copies.append(pltpu.make_async_remote_copy(
            src_ref=g_ref.at[pl.ds(0, nl), pl.ds(0, ns), 1 - c], dst_ref=land_ref, send_sem=send_sems.at[i],
            recv_sem=recv_sems.at[i], device_id=(x, y, 1 - c), device_id_type=MESH_ID))
    return copies


def _pair_exchange(gs, *, name):
    n = len(gs)

    def body(*refs):
        send_sems, recv_sems, _ = refs[2 * n:]
        copies = _pair_copies(refs[:n], refs[n:2 * n], send_sems, recv_sems)
        for cp in copies:
            cp.start()
        for cp in copies:
            cp.wait()

    out_shapes = [jax.ShapeDtypeStruct(g.shape[:2] + g.shape[3:], g.dtype) for g in gs]
    return _comm_call(body, gs, out_shapes, n, name=name)


def _chip_copies(p_refs, land_refs, send_sems, recv_sems):
    x, y, c = _mesh_pos()
    k_me = 2 * x + y
    chips = [(1 - x, y), (x, 1 - y), (1 - x, 1 - y)]
    copies = []
    for i, (p_ref, land_ref) in enumerate(zip(p_refs, land_refs)):
        nl = p_ref.shape[0]
        for j, (cx, cy) in enumerate(chips):
            copies.append(pltpu.make_async_remote_copy(
                src_ref=p_ref.at[pl.ds(0, nl), 2 * cx + cy], dst_ref=land_ref.at[k_me],
                send_sem=send_sems.at[3 * i + j], recv_sem=recv_sems.at[3 * i + j],
                device_id=(cx, cy, c), device_id_type=MESH_ID))
    return copies


def _chip_landing(ps):
    return [lax.empty((p.shape[1], p.shape[0]) + p.shape[2:], p.dtype) for p in ps]


def _pair_swap(ss, *, name):
    n = len(ss)

    def body(*refs):
        xs, outs = refs[:n], refs[n:2 * n]
        send_sems, recv_sems, _ = refs[2 * n:]
        x, y, c = _mesh_pos()
        copies = []
        for i in range(n):
            cp = pltpu.make_async_remote_copy(src_ref=xs[i], dst_ref=outs[i], send_sem=send_sems.at[i],
                                              recv_sem=recv_sems.at[i], device_id=(x, y, 1 - c),
                                              device_id_type=MESH_ID)
            cp.start()
            copies.append(cp)
        for cp in copies:
            cp.wait()

    out_shapes = [jax.ShapeDtypeStruct(s.shape, s.dtype) for s in ss]
    return _comm_call(body, ss, out_shapes, n, name=name)


def _row_tile(rows, cols):
    tr = rows
    while tr * cols > 256 * 1024 and tr % 16 == 0:
        tr //= 2
    return tr


def _pair_add(g, recv, c_idx, *, name):
    n, ns, _, rows, W = g.shape
    tr = _row_tile(rows, W)

    def body(c_ref, g_ref, r_ref, o_ref):
        o_ref[...] = (g_ref[...] + r_ref[...]).astype(BF16)

    piece = pl.BlockSpec((None, tr, W), lambda p, i, c_ref: (p, i, 0))
    grid_spec = pltpu.PrefetchScalarGridSpec(
        num_scalar_prefetch=1, grid=(n * ns, rows // tr),
        in_specs=[pl.BlockSpec((None, None, tr, W), lambda p, i, c_ref: (p, c_ref[0], i, 0)), piece],
        out_specs=piece)
    out = pl.pallas_call(
        body, name=name, grid_spec=grid_spec, out_shape=jax.ShapeDtypeStruct((n * ns, rows, W), BF16),
        compiler_params=_cparams(("parallel", "parallel")),
    )(c_idx, g.reshape(n * ns, 2, rows, W), recv.reshape(n * ns, rows, W))
    return out.reshape(n, ns, rows, W)


def _sum_pieces(land, own, chip_idx, *, name):
    n, nl, A, W = land.shape
    tr = _row_tile(A, W)

    def body(k_ref, l_ref, o_ref, out_ref):
        acc = jnp.zeros(out_ref.shape, F32)
        for k in range(n):
            acc = acc + jnp.where(k == k_ref[0], o_ref[...], l_ref[k]).astype(F32)
        out_ref[...] = acc

    grid_spec = pltpu.PrefetchScalarGridSpec(
        num_scalar_prefetch=1, grid=(nl, A // tr),
        in_specs=[pl.BlockSpec((n, None, tr, W), lambda l, i, k_ref: (0, l, i, 0)),
                  pl.BlockSpec((None, None, tr, W), lambda l, i, k_ref: (l, k_ref[0], i, 0))],
        out_specs=pl.BlockSpec((None, tr, W), lambda l, i, k_ref: (l, i, 0)))
    return pl.pallas_call(
        body, name=name, grid_spec=grid_spec, out_shape=jax.ShapeDtypeStruct((nl, A, W), F32),
        compiler_params=_cparams(("parallel", "parallel")),
    )(chip_idx, land, own)


SMALL = ("norm_mix_g", "norm_mlp_g", "final_norm_g", "fox_b_f", "mla_q_norm_g", "mla_kv_norm_g")
WEIGHT_ORDER = ("ada_w", "ada_b", "norm_mix_g", "norm_mlp_g", "fox_w_in", "fox_b_f", "fox_w_out", "mla_w_dq",
                "mla_q_norm_g", "mla_w_uq", "mla_w_dkv", "mla_kv_norm_g", "mla_w_ukv", "mla_w_out", "mlp_w1",
                "mlp_w2", "final_norm_g")


def _small_rows(vals, D):
    rows = [vals["norm_mix_g"], vals["norm_mlp_g"], vals["final_norm_g"][None, :]]
    for n in ("fox_b_f", "mla_q_norm_g", "mla_kv_norm_g"):
        flat = vals[n].reshape(-1)
        assert flat.shape[0] <= D
        rows.append(jnp.pad(flat, (0, D - flat.shape[0]))[None, :])
    return jnp.concatenate(rows, axis=0)


def _small_unrows(rows, shapes):
    L = shapes["norm_mix_g"][0]
    out = {"norm_mix_g": rows[0:L], "norm_mlp_g": rows[L:2 * L], "final_norm_g": rows[2 * L]}
    for k, n in enumerate(("fox_b_f", "mla_q_norm_g", "mla_kv_norm_g")):
        size = int(np.prod(shapes[n]))
        out[n] = rows[2 * L + 1 + k, :size].reshape(shapes[n])
    return out


def kernel(x, c, positions, ada_w, ada_b, norm_mix_g, norm_mlp_g, fox_w_in, fox_b_f, fox_w_out, mla_w_dq, mla_q_norm_g, mla_w_uq, mla_w_dkv, mla_kv_norm_g, mla_w_ukv, mla_w_out, mlp_w1, mlp_w2, final_norm_g, loss_target, m_ada_w, m_ada_b, m_norm_mix_g, m_norm_mlp_g, m_fox_w_in, m_fox_b_f, m_fox_w_out, m_mla_w_dq, m_mla_q_norm_g, m_mla_w_uq, m_mla_w_dkv, m_mla_kv_norm_g, m_mla_w_ukv, m_mla_w_out, m_mlp_w1, m_mlp_w2, m_final_norm_g, v_ada_w, v_ada_b, v_norm_mix_g, v_norm_mlp_g, v_fox_w_in, v_fox_b_f, v_fox_w_out, v_mla_w_dq, v_mla_q_norm_g, v_mla_w_uq, v_mla_w_dkv, v_mla_kv_norm_g, v_mla_w_ukv, v_mla_w_out, v_mlp_w1, v_mlp_w2, v_final_norm_g):
    args = dict(locals())
    wts = {n: args[n] for n in WEIGHT_ORDER}
    mom = {n: args["m_" + n] for n in WEIGHT_ORDER}
    var = {n: args["v_" + n] for n in WEIGHT_ORDER}
    Bl, S, D = x.shape
    T = Bl * S
    L = ada_w.shape[0]
    C = ada_w.shape[2]
    mx, my, mc = _mesh_pos()
    chip = 2 * mx + my
    dev = 4 * mx + 2 * my + mc
    c_idx = jnp.reshape(mc, (1,)).astype(jnp.int32)
    chip_idx = jnp.reshape(chip, (1,)).astype(jnp.int32)
    small = {n: wts[n] for n in SMALL}
    L2, q_cols = mla_q_norm_g.shape
    n_fox_heads = fox_b_f.shape[1]

    shards = _shard_layouts(wts)
    groups = _comm_groups(L, L2)
    slots = _layer_slots(groups)

    def row_halves(a):
        return a.reshape(a.shape[:-2] + (2, a.shape[-2] // 2, a.shape[-1]))

    def whole_rows(a):
        return a.reshape(a.shape[:2] + (a.shape[2] * a.shape[3], a.shape[4]))

    part = {g: [row_halves(shards[n][s:s + cnt]) for n, s, cnt in entries] for g, entries in groups.items()}
    mix0 = _gather_weights(part["mix0"], name="gather_mix0")
    gather_sems, after = {}, mix0[0]
    for group in ("mlp0", "rest"):
        placed = [_place_own(a, chip_idx, c_idx, name=f"gather_place_{group}_{n}")
                  for a, (n, _, _) in zip(part[group], groups[group])]
        gather_sems[group] = _split_start(_gather_copies, part[group], placed, after, name=f"gather_{group}_start",
                                          sems_per_array=4)
        after = gather_sems[group][4]

    def layer_weights(w, group, arrays):
        for (n, s, cnt), a in zip(groups[group], arrays):
            for key, view in _weight_views(n, whole_rows(a), D, n_fox_heads).items():
                for l in range(cnt):
                    w[key][s + l] = (view, l)

    w = {key: [None] * L2
         for key in ("fox_qkv", "fox_f", "fox_in", "fox_out", "mla_down", "mla_uq", "mla_ukv", "mla_out")}
    w.update({key: [None] * L for key in ("mlp_w1", "mlp_w2")})
    layer_weights(w, "mix0", mix0)

    def gathered_now(group):
        def hook(x_now, w):
            _, landed = _split_wait(_gather_copies, *gather_sems[group][:4], x_now, name=f"gather_{group}_wait")
            layer_weights(w, group, _gather_forward(landed, name=f"gather_{group}_forward"))
            return w
        return hook

    c_pad = jnp.concatenate([c, jnp.pad(mla_q_norm_g, ((0, 8 - Bl - L2), (0, D - q_cols)))], axis=0)
    c8 = _all_gather8(c_pad, name="gather_c", in_vmem=True)
    c_all = c8[:, :Bl].reshape(N_DEV * Bl, D)
    qg4 = c8.reshape(N_CHIP, 2, 8, D)[:, 0, Bl:Bl + L2, :q_cols]
    small["mla_q_norm_g"] = jnp.transpose(qg4, (1, 0, 2)).reshape(L2, N_CHIP * q_cols)
    ada_b_cols = lax.dynamic_slice_in_dim(ada_b, chip * C, C, axis=1)[:, None, :]
    mod_cols = _ada_fwd(c_all, ada_w, ada_b_cols)
    mod8 = _all_gather8(mod_cols.reshape(L * N_DEV * Bl, C), name="gather_mod", in_vmem=True)
    mod4 = mod8.reshape(N_CHIP, 2, L, N_DEV * Bl, C)[:, 0]
    mod_me = lax.dynamic_slice_in_dim(mod4, dev * Bl, Bl, axis=2)
    mod = jnp.transpose(mod_me, (1, 2, 0, 3)).reshape(L, Bl, 6, D)
    mod = jnp.transpose(mod, (0, 2, 1, 3))[:, :, :, None, :]

    w.update(_small_layouts(small))
    mod = mod + after[0, 0]
    pending = {}

    def grad_pieces(group, g_now):
        out = []
        for n, s, cnt in groups[group]:
            qkv_f = [g_now["fox_in"][j] for j in range(s, s + cnt)] if n == "fox_in" else None
            stacked_g = None if n == "fox_in" else g_now[n][group]
            out.append(row_halves(_grad_pieces(n, stacked_g, qkv_f, n_fox_heads, N_CHIP)))
        return out

    def pair_added(group, big, sibling):
        return [_pair_add(a, r, c_idx, name=f"grad_pair_add_{group}_{n}")
                for (n, _, _), a, r in zip(groups[group], big, sibling)]

    def exchange_start(group, ps, after=None):
        pending[group] = _split_start(_chip_copies, ps, _chip_landing(ps), chip_idx if after is None else after,
                                      name=f"grad_exchange_{group}_start", sems_per_array=3)
        return pending[group][4]

    def bwd_layer0(g_now):
        big = grad_pieces("rest", g_now)
        landing = [lax.empty(a.shape[:2] + a.shape[3:], a.dtype) for a in big]
        pending["rest_pair"] = _split_start(_pair_copies, big, landing, chip_idx, name="grad_pair_rest_start",
                                            sems_per_array=1)
        return pending["rest_pair"][4]

    def bwd_mix0(g_now):
        send_sems, recv_sems, big, landed, _ = pending["rest_pair"]
        big, landed = _split_wait(_pair_copies, send_sems, recv_sems, big, landed, g_now["mlp_w1"]["mlp0"],
                                  name="grad_pair_rest_wait")
        started = exchange_start("rest", pair_added("rest", big, landed))
        big = grad_pieces("mlp0", g_now)
        return exchange_start("mlp0", pair_added("mlp0", big, _pair_exchange(big, name="grad_pair_exchange_mlp0")),
                              after=started)

    half = ROPE_DIM // 2
    inv_freq = ROPE_THETA ** (-jnp.arange(0, ROPE_DIM, 2, dtype=F32) / ROPE_DIM)
    lane = np.arange(LANES)
    inv_freq_row = jnp.tile(inv_freq, LANES // half)[None, :]
    sign_row = jnp.asarray(np.where(lane < 2 * ROPE_DIM, np.where(lane % ROPE_DIM < half, -1.0, 1.0), 0.0), F32)[None, :]
    pos_f = positions.astype(F32).reshape(T, 1)
    loss_row, grad_x, dmod, g = _local_step(x.reshape(T, D), loss_target.reshape(T, D), pos_f, inv_freq_row, sign_row,
                                            mod, w, slots, S=S,
                                            hooks={"fwd_mlp0": gathered_now("mlp0"), "fwd_layer1": gathered_now("rest"),
                                                   "bwd_layer0": bwd_layer0, "bwd_mix0": bwd_mix0})
    g_small = _small_grads(g, n_fox_heads)
    big = grad_pieces("mix0", g)
    exchange_start("mix0", pair_added("mix0", big, _pair_exchange(big, name="grad_pair_exchange_mix0")))

    Rs = -(-(2 * L + 5) // 8) * 8
    srows = jnp.concatenate([_small_rows(g_small, D), jnp.pad(loss_row, ((0, 0), (0, D - LANES)))], axis=0)
    srows = jnp.pad(srows, ((0, Rs - srows.shape[0]), (0, 0)))
    drows = jnp.transpose(dmod[:, :, :, 0, :], (2, 0, 1, 3)).reshape(Bl * L * 6, D)
    both8 = _all_gather8(jnp.concatenate([drows, srows], axis=0), name="gather_small", in_vmem=True)
    dm8 = both8[:, :Bl * L * 6].reshape(N_DEV, Bl, L * 6, D)
    sm8 = both8[:, Bl * L * 6:]
    adb_rows, small_sum = _sum_gathered(dm8, sm8)
    grad_ada_b = adb_rows.reshape(L, 6 * D)
    loss = small_sum[2 * L + 4, 0]
    small_shapes = {n: (wts[n].shape if n != "mla_q_norm_g" else (wts[n].shape[0], N_CHIP * q_cols)) for n in SMALL}
    gs = _small_unrows(small_sum, small_shapes)
    gs["mla_q_norm_g"] = lax.dynamic_slice_in_dim(gs["mla_q_norm_g"], chip * q_cols, q_cols, axis=1)

    dmod16 = jnp.transpose(dm8.reshape(N_DEV, Bl, L, 6 * D), (2, 0, 1, 3)).reshape(L, N_DEV * Bl, 6 * D)
    dmod_cols = lax.dynamic_slice_in_dim(dmod16, chip * C, C, axis=2)
    grad_ada_w = _ada_bwd(c_all, dmod_cols)

    grads = dict(gs)
    grads["ada_w"] = grad_ada_w
    grads["ada_b"] = grad_ada_b
    delta, new_m, new_v = {}, {}, {}
    for n in ("ada_w", "ada_b"):
        delta[n], new_m[n], new_v[n] = _adamw(wts[n], grads[n], mom[n], var[n], name=f"adamw_{n}")
    shard_small_shapes = {n: wts[n].shape for n in SMALL}
    packs = [jnp.pad(_small_rows({n: src[n] for n in SMALL}, D), ((0, Rs - 2 * L - 4), (0, 0)))
             for src in (wts, grads, mom, var)]
    for dst, rows in zip((delta, new_m, new_v), _adamw(*packs, name="adamw_small")):
        dst.update(_small_unrows(rows, shard_small_shapes))

    halves = {}
    for group, after in (("rest", grad_x), ("mlp0", grad_x), ("mix0", delta["ada_w"])):
        send_sems, recv_sems, ps, lands, _ = pending[group]
        ps, lands = _split_wait(_chip_copies, send_sems, recv_sems, ps, lands, after, name=f"grad_exchange_{group}_wait")
        sums = [_sum_pieces(ld, p, chip_idx, name=f"grad_sum_{group}_{n}")
                for (n, _, _), ld, p in zip(groups[group], lands, ps)]
        swapped = _pair_swap(sums, name=f"grad_pair_swap_{group}")
        for (n, _, _), a, b in zip(groups[group], sums, swapped):
            halves[(n, group)] = (a, b)

    def all_layers(n, which):
        return jnp.concatenate([halves[(n, grp)][which] for grp in groups if (n, grp) in halves], axis=0)

    own = {n: all_layers(n, 0) for n in GATHERED}
    peer = {n: all_layers(n, 1) for n in GATHERED}
    for nat, n in (("fox_w_in", "fox_in"), ("fox_w_out", "fox_out"), ("mla_w_out", "mla_out"), ("mlp_w1", "mlp_w1"),
                   ("mlp_w2", "mlp_w2")):
        cols = wts[nat].shape[-1]
        res = _adamw_halves(_pad_lanes(wts[nat]), own[n], peer[n], _pad_lanes(mom[nat]), _pad_lanes(var[nat]), c_idx,
                            name=f"adamw_{nat}")
        grads[nat], delta[nat], new_m[nat], new_v[nat] = (a[..., :cols] for a in res)
    joined = {n: jnp.concatenate([jnp.where(mc == 0, own[n], peer[n]), jnp.where(mc == 0, peer[n], own[n])], axis=1)
              for n in ("mla_down", "mla_uq", "mla_ukv")}
    rq = mla_w_dq.shape[-1]
    grads["mla_w_dq"] = joined["mla_down"][:, :, :rq]
    grads["mla_w_dkv"] = joined["mla_down"][:, :, rq:rq + KV_RANK + ROPE_DIM]
    grads["mla_w_uq"] = jax.vmap(_uq_from_pairs)(joined["mla_uq"])
    grads["mla_w_ukv"] = jax.vmap(_ukv_from_pairs)(joined["mla_ukv"])
    for n in ("mla_w_dq", "mla_w_dkv", "mla_w_uq", "mla_w_ukv"):
        delta[n], new_m[n], new_v[n] = _adamw(wts[n], grads[n], mom[n], var[n], name=f"adamw_{n}")

    return (loss, grad_x.reshape(Bl, S, D), *[grads[n] for n in WEIGHT_ORDER], *[delta[n] for n in WEIGHT_ORDER],
            *[new_m[n] for n in WEIGHT_ORDER], *[new_v[n] for n in WEIGHT_ORDER])
```

```python
import numpy as np
import jax
import jax.numpy as jnp
from jax import lax
from jax.experimental import pallas as pl
from jax.experimental.pallas import tpu as pltpu

F32 = jnp.float32
BF16 = jnp.bfloat16
MESH_ID = pl.DeviceIdType.MESH

NORM_EPS = 1e-6
ROPE_THETA = 10000.0
HEAD_DIM = 64
ROPE_DIM = 32
KV_RANK = 128
MLA_SCALE = (HEAD_DIM + ROPE_DIM) ** -0.5
FOX_EXTRA = 6
PAIR_Q = 256
PAIR_KV = 384
LANES = 128
ADAM_LR = 0.001
ADAM_B1 = 0.9
ADAM_B2 = 0.999
ADAM_EPS = 1e-08
ADAM_WD = 0.01
ADAM_STEP = 10
VMEM_LIMIT_V7X = 48 * 1024 * 1024
MM_VMEM_BUDGET = 36 * 1024 * 1024
MM_VMEM_HEADROOM = VMEM_LIMIT_V7X - MM_VMEM_BUDGET
NEG_BIG = -1e30
ATTN_UNROLL = 4
ATTN_BLOCK = 256
ATTN_Q_ROWS = 512
ATTN_K_ROWS = 512


def _cparams(sem=None, vmem_limit=VMEM_LIMIT_V7X):
    return pltpu.CompilerParams(dimension_semantics=sem, vmem_limit_bytes=vmem_limit)


def _tile(n, want):
    if n <= want:
        return n
    for t in range(want - want % LANES, 0, -LANES):
        if n % t == 0:
            return t
    raise ValueError((n, want))


def _mm(a, b, mode, *, name, out_dtypes=(F32,), epilogue=None, extras=(), rowvecs=(), tables=(),
        seq=None, a_off=0, a_sz=None, b_layer=None, out_stack=None, out_split=0, out_t=(), full_rows=False,
        vmem_budget=MM_VMEM_BUDGET, tm=1024, tn=1024, tk=2048):
    if isinstance(b, (list, tuple)):
        b, b_layer = b[b_layer]
    b_rows, b_cols = b.shape[-2], b.shape[-1]
    n_split = b.shape[1] if b.ndim == 4 else 1
    assert mode in ("nn", "nt")
    if mode == "nn":
        M, K, N = a.shape[0], b_rows, b_cols * n_split
    else:
        M, K, N = a.shape[0], b_cols * n_split, b_rows
    assert a_sz is None or a_sz == K
    tm = _tile(seq if rowvecs else M, tm)
    n_piece = N // max(out_split, n_split if mode == "nn" else 1, 1)
    tn = _tile(n_piece, tn)
    tk = _tile(K // (n_split if mode == "nt" else 1), tk)
    ne, nr, nt_ = len(extras), len(rowvecs), len(tables)
    no = len(out_dtypes)

    def vmem_estimate():
        blocks = tm * tk * a.dtype.itemsize + tk * tn * b.dtype.itemsize
        blocks += tm * tn * (sum(e.dtype.itemsize for e in extras) + sum(jnp.dtype(d).itemsize for d in out_dtypes))
        return 2 * blocks + 2 * tm * tn * 4

    if full_rows:
        assert tn == N
    while vmem_estimate() > vmem_budget and max(tm, tn) > 256:
        if tn >= tm and not full_rows:
            tn //= 2
        else:
            tm //= 2
    nk = K // tk

    assert a_off % tk == 0
    a_spec = pl.BlockSpec((tm, tk), lambda i, j, k: (i, k + a_off // tk))
    dims = (((1,), (0,)), ((), ())) if mode == "nn" else (((1,), (1,)), ((), ()))
    lead = () if b.ndim == 2 else (b_layer,)
    sq = (None,) * (b.ndim - 2)
    if mode == "nt":
        kb = b_cols // tk
        if b.ndim == 4:
            b_spec = pl.BlockSpec(sq + (tn, tk), lambda i, j, k: lead + (k // kb, j, k % kb))
        else:
            b_spec = pl.BlockSpec(sq + (tn, tk), lambda i, j, k: lead + (j, k))
    else:
        nb = b_cols // tn
        if b.ndim == 4:
            b_spec = pl.BlockSpec(sq + (tk, tn), lambda i, j, k: lead + (j // nb, k, j % nb))
        else:
            b_spec = pl.BlockSpec(sq + (tk, tn), lambda i, j, k: lead + (k, j))
    in_specs = [a_spec, b_spec]
    in_specs += [pl.BlockSpec((tm, tn), lambda i, j, k: (i, j)) for _ in extras]
    if rowvecs:
        assert seq % tm == 0
        per = seq // tm
        in_specs += [pl.BlockSpec((None, 1, tn), lambda i, j, k: (i // per, 0, j)) for _ in rowvecs]
    in_specs += [pl.BlockSpec((tm, LANES), lambda i, j, k: (i, 0)) for _ in tables]
    operands = [a, b, *extras, *rowvecs, *tables]
    aliases = {}
    transposed = tuple(out_t) + (False,) * (no - len(out_t))
    if out_stack is None:
        out_specs = [pl.BlockSpec((tn, tm), lambda i, j, k: (j, i)) if t else pl.BlockSpec((tm, tn), lambda i, j, k: (i, j))
                     for t in transposed]
        out_shape = [jax.ShapeDtypeStruct((N, M) if t else (M, N), d) for d, t in zip(out_dtypes, transposed)]
    else:
        prev, layer, n_layers = out_stack
        assert no == 1
        if out_split:
            ob = n_piece // tn
            out_specs = [pl.BlockSpec((None, None, tm, tn), lambda i, j, k: (layer, j // ob, i, j % ob))]
            out_shape = [jax.ShapeDtypeStruct((n_layers, out_split, M, n_piece), out_dtypes[0])]
        else:
            out_specs = [pl.BlockSpec((None, tm, tn), lambda i, j, k: (layer, i, j))]
            out_shape = [jax.ShapeDtypeStruct((n_layers, M, N), out_dtypes[0])]
        if prev is not None:
            in_specs.append(pl.BlockSpec(memory_space=pl.ANY))
            aliases = {len(operands): 0}
            operands.append(prev)
    n_in = len(operands)

    def body(*refs):
        a_ref, b_ref = refs[0], refs[1]
        side = refs[2:2 + ne + nr + nt_]
        outs = refs[n_in:n_in + no]

        def finish(acc):
            res = (acc,) if epilogue is None else epilogue(acc, *[r[...] for r in side])
            for o_ref, r, t in zip(outs, res, transposed):
                o_ref[...] = (r.T if t else r).astype(o_ref.dtype)

        part = lax.dot_general(a_ref[...].astype(BF16), b_ref[...].astype(BF16), dims,
                               preferred_element_type=F32)
        if nk == 1:
            finish(part)
        else:
            acc_ref = refs[-1]
            k = pl.program_id(2)

            @pl.when(k == 0)
            def _():
                acc_ref[...] = part

            @pl.when(k > 0)
            def _():
                acc_ref[...] += part

            @pl.when(k == nk - 1)
            def _():
                finish(acc_ref[...])

    res = pl.pallas_call(
        body, name=name, grid=(M // tm, N // tn, nk), in_specs=in_specs, out_specs=out_specs,
        out_shape=out_shape, scratch_shapes=[pltpu.VMEM((tm, tn), F32)] if nk > 1 else [],
        input_output_aliases=aliases,
        compiler_params=_cparams(("parallel", "parallel", "arbitrary"), vmem_limit=vmem_budget + MM_VMEM_HEADROOM),
    )(*operands)
    return res[0] if no == 1 else tuple(res)


def _rope128(x, cos_t, sin_s):
    lane = lax.broadcasted_iota(jnp.int32, x.shape, 1)
    first = (lane % ROPE_DIM) < (ROPE_DIM // 2)
    swapped = jnp.where(first, pltpu.roll(x, LANES - ROPE_DIM // 2, 1), pltpu.roll(x, ROPE_DIM // 2, 1))
    return x * cos_t + swapped * sin_s


def _rope_pairs(acc, cos_t, sin_s, sign):
    parts = []
    for p in range(acc.shape[1] // PAIR_Q):
        parts.append(acc[:, p * PAIR_Q:p * PAIR_Q + LANES])
        parts.append(_rope128(acc[:, p * PAIR_Q + LANES:(p + 1) * PAIR_Q], cos_t, sign * sin_s))
    return jnp.concatenate(parts, axis=1)


def _rope_tables(pos_f, inv_freq_row, sign_row):
    T = pos_f.shape[0]
    tt = _tile(T, 512)

    def body(p_ref, f_ref, s_ref, cos_ref, sin_ref):
        ang = p_ref[...] * f_ref[...]
        cos_ref[...] = jnp.cos(ang)
        sin_ref[...] = jnp.sin(ang) * s_ref[...]

    return pl.pallas_call(
        body, name="rope_tables", grid=(T // tt,),
        in_specs=[pl.BlockSpec((tt, 1), lambda i: (i, 0)), pl.BlockSpec((1, LANES), lambda i: (0, 0)),
                  pl.BlockSpec((1, LANES), lambda i: (0, 0))],
        out_specs=[pl.BlockSpec((tt, LANES), lambda i: (i, 0))] * 2,
        out_shape=[jax.ShapeDtypeStruct((T, LANES), F32)] * 2,
        compiler_params=_cparams(("parallel",)),
    )(pos_f, inv_freq_row, sign_row)


def _unrope(dqx, cos_t, sin_s):
    T, W = dqx.shape
    tt = _tile(T, 512)

    def body(d_ref, c_ref, s_ref, o_ref):
        o_ref[...] = _rope_pairs(d_ref[...].astype(F32) * MLA_SCALE, c_ref[...], s_ref[...], -1.0).astype(BF16)

    return pl.pallas_call(
        body, name="mla_unrope", grid=(T // tt,),
        in_specs=[pl.BlockSpec((tt, W), lambda i: (i, 0)), pl.BlockSpec((tt, LANES), lambda i: (i, 0)),
                  pl.BlockSpec((tt, LANES), lambda i: (i, 0))],
        out_specs=pl.BlockSpec((tt, W), lambda i: (i, 0)),
        out_shape=jax.ShapeDtypeStruct((T, W), BF16),
        compiler_params=_cparams(("parallel",)),
    )(dqx, cos_t, sin_s)


def _row_specs(tt, D, per, n):
    return [pl.BlockSpec((None, 1, D), lambda i: (i // per, 0, 0)) for _ in range(n)]


def _norm_mod(x, gain, sc, sh, *, S, name):
    T, D = x.shape
    tt = _tile(S, 512)
    per = S // tt

    def body(x_ref, g_ref, sc_ref, sh_ref, h_ref, ht_ref):
        xv = x_ref[...]
        r = lax.rsqrt(jnp.mean(xv * xv, axis=-1, keepdims=True) + NORM_EPS)
        h = (xv * r) * g_ref[...] * (1.0 + sc_ref[...]) + sh_ref[...]
        h_ref[...] = h.astype(BF16)
        ht_ref[...] = h.T.astype(BF16)

    return pl.pallas_call(
        body, name=name, grid=(T // tt,),
        in_specs=[pl.BlockSpec((tt, D), lambda i: (i, 0)), pl.BlockSpec((1, D), lambda i: (0, 0))]
        + _row_specs(tt, D, per, 2),
        out_specs=[pl.BlockSpec((tt, D), lambda i: (i, 0)), pl.BlockSpec((D, tt), lambda i: (0, i))],
        out_shape=[jax.ShapeDtypeStruct((T, D), BF16), jax.ShapeDtypeStruct((D, T), BF16)],
        compiler_params=_cparams(("parallel",)),
    )(x, gain, sc, sh)


def _norm_mod_bwd(x, dh, dres, gain, sc, gate=None, *, S, name):
    T, D = x.shape
    B = T // S
    tt = _tile(S, 512)
    per = S // tt
    n_gate = 0 if gate is None else 2

    def body(*refs):
        x_ref, dh_ref, dres_ref, g_ref, sc_ref = refs[:5]
        dx_ref, dsh_ref, dsc_ref, dg_ref = refs[5 + n_gate:9 + n_gate]
        i = pl.program_id(0)
        xv = x_ref[...]
        dhv = dh_ref[...].astype(F32)
        r = lax.rsqrt(jnp.mean(xv * xv, axis=-1, keepdims=True) + NORM_EPS)
        n = xv * r
        g = g_ref[...]
        one_sc = 1.0 + sc_ref[...]
        dn = dhv * (g * one_sc)
        dxv = dres_ref[...] + r * (dn - n * jnp.mean(dn * n, axis=-1, keepdims=True))
        dx_ref[...] = dxv
        dhn = dhv * n

        @pl.when(i % per == 0)
        def _():
            dsh_ref[...] = jnp.zeros_like(dsh_ref)
            dsc_ref[...] = jnp.zeros_like(dsc_ref)

        @pl.when(i == 0)
        def _():
            dg_ref[...] = jnp.zeros_like(dg_ref)

        dsh_ref[...] += jnp.sum(dhv, axis=0, keepdims=True)
        dsc_ref[...] += jnp.sum(dhn, axis=0, keepdims=True) * g
        dg_ref[...] += jnp.sum(dhn, axis=0, keepdims=True) * one_sc
        if gate is not None:
            y_ref, gate_ref = refs[5:7]
            dy_ref, dgate_ref = refs[9 + n_gate:]
            dy_ref[...] = (dxv * gate_ref[...]).astype(BF16)

            @pl.when(i % per == 0)
            def _():
                dgate_ref[...] = jnp.zeros_like(dgate_ref)

            dgate_ref[...] += jnp.sum(dxv * y_ref[...], axis=0, keepdims=True)

    tile = pl.BlockSpec((tt, D), lambda i: (i, 0))
    in_specs = [tile] * 3 + [pl.BlockSpec((1, D), lambda i: (0, 0))] + _row_specs(tt, D, per, 1)
    out_specs = [tile] + _row_specs(tt, D, per, 2) + [pl.BlockSpec((1, D), lambda i: (0, 0))]
    out_shape = [jax.ShapeDtypeStruct((T, D), F32), jax.ShapeDtypeStruct((B, 1, D), F32),
                 jax.ShapeDtypeStruct((B, 1, D), F32), jax.ShapeDtypeStruct((1, D), F32)]
    operands = [x, dh, dres, gain, sc]
    if gate is not None:
        in_specs += [tile] + _row_specs(tt, D, per, 1)
        out_specs += [tile] + _row_specs(tt, D, per, 1)
        out_shape += [jax.ShapeDtypeStruct((T, D), BF16), jax.ShapeDtypeStruct((B, 1, D), F32)]
        operands += list(gate)
    return pl.pallas_call(
        body, name=name, grid=(T // tt,), in_specs=in_specs, out_specs=out_specs, out_shape=out_shape,
        compiler_params=_cparams(("arbitrary",)),
    )(*operands)


def _gate_bwd(dx, y, g, *, S, name):
    T, D = dx.shape
    B = T // S
    tt = _tile(S, 512)
    per = S // tt

    def body(dx_ref, y_ref, g_ref, dy_ref, dg_ref):
        i = pl.program_id(0)
        dxv = dx_ref[...]
        dy_ref[...] = (dxv * g_ref[...]).astype(BF16)

        @pl.when(i % per == 0)
        def _():
            dg_ref[...] = jnp.zeros_like(dg_ref)

        dg_ref[...] += jnp.sum(dxv * y_ref[...], axis=0, keepdims=True)

    return pl.pallas_call(
        body, name=name, grid=(T // tt,),
        in_specs=[pl.BlockSpec((tt, D), lambda i: (i, 0))] * 2 + _row_specs(tt, D, per, 1),
        out_specs=[pl.BlockSpec((tt, D), lambda i: (i, 0))] + _row_specs(tt, D, per, 1),
        out_shape=[jax.ShapeDtypeStruct((T, D), BF16), jax.ShapeDtypeStruct((B, 1, D), F32)],
        compiler_params=_cparams(("arbitrary",)),
    )(dx, y, g)


def _final_loss(x, target, gain):
    T, D = x.shape
    tt = _tile(T, 512)

    def body(x_ref, t_ref, g_ref, dx_ref, dg_ref, loss_ref):
        i = pl.program_id(0)
        xv = x_ref[...]
        r = lax.rsqrt(jnp.mean(xv * xv, axis=-1, keepdims=True) + NORM_EPS)
        n = xv * r
        g = g_ref[...]
        err = n * g - t_ref[...]
        dy = err * (1.0 / D)
        dn = dy * g
        dx_ref[...] = r * (dn - n * jnp.mean(dn * n, axis=-1, keepdims=True))

        @pl.when(i == 0)
        def _():
            dg_ref[...] = jnp.zeros_like(dg_ref)
            loss_ref[...] = jnp.zeros_like(loss_ref)

        dg_ref[...] += jnp.sum(dy * n, axis=0, keepdims=True)
        loss_ref[...] += jnp.sum(jnp.sum(err * err, axis=-1, keepdims=True), axis=0, keepdims=True) * (0.5 / D)

    return pl.pallas_call(
        body, name="final_loss", grid=(T // tt,),
        in_specs=[pl.BlockSpec((tt, D), lambda i: (i, 0))] * 2 + [pl.BlockSpec((1, D), lambda i: (0, 0))],
        out_specs=[pl.BlockSpec((tt, D), lambda i: (i, 0)), pl.BlockSpec((1, D), lambda i: (0, 0)),
                   pl.BlockSpec((1, LANES), lambda i: (0, 0))],
        out_shape=[jax.ShapeDtypeStruct((T, D), F32), jax.ShapeDtypeStruct((1, D), F32),
                   jax.ShapeDtypeStruct((1, LANES), F32)],
        compiler_params=_cparams(("arbitrary",)),
    )(x, target, gain)


def _head_masks(ew):
    lane = lax.broadcasted_iota(jnp.int32, (1, PAIR_Q), 1)
    m0 = (lane < HEAD_DIM) | ((lane >= LANES) & (lane < LANES + ew))
    m1 = ((lane >= HEAD_DIM) & (lane < LANES)) | ((lane >= LANES + ew) & (lane < LANES + 2 * ew))
    return m0, m1


def _dot_nt(a, b):
    return lax.dot_general(a, b, (((1,), (1,)), ((), ())), preferred_element_type=F32)


def _dot_tn(a, b):
    return lax.dot_general(a, b, (((0,), (0,)), ((), ())), preferred_element_type=F32)


def _lane_halves(x, op):
    acc = x[:, 0:LANES]
    for g in range(1, x.shape[1] // LANES):
        acc = op(acc, x[:, g * LANES:(g + 1) * LANES])
    return acc


def _head_rows(cols_lane_replicated):
    t = cols_lane_replicated.T
    sub = lax.broadcasted_iota(jnp.int32, (8, t.shape[1]), 0)
    return jnp.where(sub == 1, t[HEAD_DIM:HEAD_DIM + 8], t[0:8])


def _attn_fwd(qx, kvx, *, S, ew, name):
    T = qx.shape[0]
    P = qx.shape[1] // PAIR_Q
    B = T // S
    tk = _tile(S, ATTN_BLOCK)
    tq = _tile(S, ATTN_Q_ROWS)
    nq = S // tq
    per = tq // tk

    def body(q_ref, kv_ref, o_ref, lse_ref, ot_ref, m_sc, l_sc, acc_sc):
        qi = pl.program_id(2)
        q = q_ref[...]
        masks = _head_masks(ew)
        qh = [jnp.where(m, q, jnp.zeros_like(q)) for m in masks]

        def logits(h, k, diagonal):
            s = _dot_nt(qh[h], k)
            if diagonal is None:
                return s
            row = lax.broadcasted_iota(jnp.int32, s.shape, 0)
            col = lax.broadcasted_iota(jnp.int32, s.shape, 1)
            return jnp.where(col + diagonal * tk <= row, s, NEG_BIG)

        def trip(first, count, n_diagonal=0):
            rows = [pl.ds(pl.multiple_of((first + u) * tk, tk), tk) for u in range(count)]
            diag = [None] * (count - n_diagonal) + list(range(n_diagonal))
            for h in range(2):
                ss = [logits(h, kv_ref[rows[u], 0:PAIR_Q], diag[u]) for u in range(count)]
                m_prev = m_sc[h]
                m_elem = m_prev
                for s in ss:
                    m_elem = jnp.maximum(m_elem, _lane_halves(s, jnp.maximum))
                m_new = jnp.broadcast_to(jnp.max(m_elem, axis=1, keepdims=True), (tq, LANES))
                alpha = jnp.exp(m_prev - m_new)
                l = alpha * l_sc[h]
                acc = alpha * acc_sc[h]
                for u, s in enumerate(ss):
                    p = jnp.concatenate([jnp.exp(s[:, g * LANES:(g + 1) * LANES] - m_new)
                                         for g in range(tk // LANES)], axis=1)
                    l = l + _lane_halves(p, jnp.add)
                    acc = acc + jnp.dot(p.astype(BF16), kv_ref[rows[u], PAIR_Q:PAIR_KV], preferred_element_type=F32)
                m_sc[h] = m_new
                l_sc[h] = l
                acc_sc[h] = acc

        m_sc[...] = jnp.full(m_sc.shape, NEG_BIG, F32)
        l_sc[...] = jnp.zeros_like(l_sc)
        acc_sc[...] = jnp.zeros_like(acc_sc)

        def loop_body(t, carry):
            trip(t * ATTN_UNROLL, ATTN_UNROLL)
            return carry

        below = qi * per
        lax.fori_loop(0, below // ATTN_UNROLL, loop_body, 0)
        for left in range(0, ATTN_UNROLL, per):
            @pl.when(below % ATTN_UNROLL == left)
            def _(left=left):
                trip(below - left, left + per, n_diagonal=per)

        lane = lax.broadcasted_iota(jnp.int32, (tq, LANES), 1)
        lo = lane < HEAD_DIM
        l = [jnp.sum(l_sc[h], axis=1, keepdims=True) for h in range(2)]
        o = jnp.where(lo, acc_sc[0] / l[0], acc_sc[1] / l[1])
        o_ref[...] = o.astype(BF16)
        ot_ref[...] = o.T.astype(BF16)
        lse = jnp.where(lo, m_sc[0] + jnp.log(l[0]), m_sc[1] + jnp.log(l[1]))
        for r in range(per):
            lse_ref[r] = _head_rows(lse[r * tk:(r + 1) * tk])

    return pl.pallas_call(
        body, name=name, grid=(B, P, nq),
        in_specs=[pl.BlockSpec((tq, PAIR_Q), lambda b, p, i: (b * nq + i, p)),
                  pl.BlockSpec((S, PAIR_KV), lambda b, p, i: (b, p))],
        out_specs=[pl.BlockSpec((tq, LANES), lambda b, p, i: (b * nq + i, p)),
                   pl.BlockSpec((per, None, 8, tk), lambda b, p, i: (b * nq + i, p, 0, 0)),
                   pl.BlockSpec((LANES, tq), lambda b, p, i: (p, b * nq + i))],
        out_shape=[jax.ShapeDtypeStruct((T, P * LANES), BF16), jax.ShapeDtypeStruct((T // tk, P, 8, tk), F32),
                   jax.ShapeDtypeStruct((P * LANES, T), BF16)],
        scratch_shapes=[pltpu.VMEM((2, tq, LANES), F32)] * 3,
        compiler_params=_cparams(("parallel", "parallel", "arbitrary")),
    )(qx, kvx)


def _attn_bwd(qx, kvx, o, lse, do, *, S, ew, name, bias_grad=False):
    T = qx.shape[0]
    P = qx.shape[1] // PAIR_Q
    B = T // S
    tq = _tile(S, ATTN_BLOCK)
    tk = _tile(S, ATTN_K_ROWS)
    nq = S // tq
    nk = S // tk
    per = tk // tq

    def body(q_ref, kv_ref, o_ref, lse_ref, do_ref, dq_ref, dkv_ref, *rest):
        kj = pl.program_id(2)
        if bias_grad:
            csum_ref, rsum_ref, dq_sc, delta_sc, dk_sc, dv_sc, cs_sc = rest
            cs_sc[...] = jnp.zeros_like(cs_sc)

            @pl.when(kj == 0)
            def _():
                rsum_ref[...] = jnp.zeros_like(rsum_ref)
        else:
            dq_sc, delta_sc, dk_sc, dv_sc = rest
        masks = _head_masks(ew)
        lo_q = lax.broadcasted_iota(jnp.int32, (tq, LANES), 1) < HEAD_DIM
        lo = lax.broadcasted_iota(jnp.int32, (tk, LANES), 1) < HEAD_DIM
        vmask = [lo, jnp.logical_not(lo)]

        @pl.when(kj == 0)
        def _():
            dq_sc[...] = jnp.zeros_like(dq_sc)
            for c in range(nq):
                rows = pl.ds(c * tq, tq)
                x = do_ref[rows, :].astype(F32) * o_ref[rows, :].astype(F32)
                r0 = jnp.sum(jnp.where(lo_q, x, 0.0), axis=1, keepdims=True)
                r1 = jnp.sum(jnp.where(lo_q, 0.0, x), axis=1, keepdims=True)
                delta_sc[c] = _head_rows(jnp.where(lo_q, r0, r1))

        k = kv_ref[:, 0:PAIR_Q]
        v = kv_ref[:, PAIR_Q:PAIR_KV]
        kh = [jnp.where(m, k, jnp.zeros_like(k)) for m in masks]
        vh = [jnp.where(m, v, jnp.zeros_like(v)) for m in vmask]
        dk_sc[...] = jnp.zeros_like(dk_sc)
        dv_sc[...] = jnp.zeros_like(dv_sc)

        def step(qi, diagonal):
            rows = pl.ds(pl.multiple_of(qi * tq, tq), tq)
            q = q_ref[rows, :]
            dov = do_ref[rows, :]
            lse8 = lse_ref[qi]
            dl8 = delta_sc[qi]
            for h in range(2):
                st = _dot_nt(kh[h], q)
                if diagonal is not None:
                    key = lax.broadcasted_iota(jnp.int32, st.shape, 0)
                    qry = lax.broadcasted_iota(jnp.int32, st.shape, 1)
                    st = jnp.where(key <= qry + diagonal * tq, st, NEG_BIG)
                pt = jnp.exp(st - lse8[h:h + 1, :])
                dpt = _dot_nt(vh[h], dov)
                dst = pt * (dpt - dl8[h:h + 1, :])
                if bias_grad:
                    cs_sc[h] += _lane_halves(dst, jnp.add)
                    rsum_ref[qi, h:h + 1, :] += jnp.sum(dst, axis=0, keepdims=True)
                ptb = pt.astype(BF16)
                dstb = dst.astype(BF16)
                dv_sc[h] += jnp.dot(ptb, dov, preferred_element_type=F32)
                dk_sc[h] += jnp.dot(dstb, q, preferred_element_type=F32)
                dq_sc[rows, :] += _dot_tn(dstb, kh[h])

        first = kj * per
        above = nq - per - first
        for left in range(0, ATTN_UNROLL, per):
            @pl.when(above % ATTN_UNROLL == left)
            def _(left=left):
                for d in range(per):
                    step(first + d, d)
                for u in range(left):
                    step(first + per + u, None)

        def loop_body(t, carry):
            for u in range(ATTN_UNROLL):
                step(first + per + above % ATTN_UNROLL + t * ATTN_UNROLL + u, None)
            return carry

        lax.fori_loop(0, above // ATTN_UNROLL, loop_body, 0)
        dkv_ref[:, 0:PAIR_Q] = (jnp.where(masks[0], dk_sc[0], 0.0) + jnp.where(masks[1], dk_sc[1], 0.0)).astype(BF16)
        dkv_ref[:, PAIR_Q:PAIR_KV] = jnp.where(lo, dv_sc[0], dv_sc[1]).astype(BF16)
        if bias_grad:
            csum_ref[...] = jnp.where(lo, jnp.sum(cs_sc[0], axis=1, keepdims=True),
                                      jnp.sum(cs_sc[1], axis=1, keepdims=True))

        @pl.when(kj == nk - 1)
        def _():
            dq_ref[...] = dq_sc[...].astype(BF16)

    rows_spec = pl.BlockSpec((nq, None, 8, tq), lambda b, p, j: (b, p, 0, 0))
    out_specs = [pl.BlockSpec((S, PAIR_Q), lambda b, p, j: (b, p)),
                 pl.BlockSpec((tk, PAIR_KV), lambda b, p, j: (b * nk + j, p))]
    out_shape = [jax.ShapeDtypeStruct((T, P * PAIR_Q), BF16), jax.ShapeDtypeStruct((T, P * PAIR_KV), BF16)]
    scratch = [pltpu.VMEM((S, PAIR_Q), F32), pltpu.VMEM((nq, 8, tq), F32),
               pltpu.VMEM((2, tk, PAIR_Q), F32), pltpu.VMEM((2, tk, LANES), F32)]
    if bias_grad:
        out_specs += [pl.BlockSpec((tk, LANES), lambda b, p, j: (b * nk + j, p)), rows_spec]
        out_shape += [jax.ShapeDtypeStruct((T, P * LANES), F32), jax.ShapeDtypeStruct((T // tq, P, 8, tq), F32)]
        scratch.append(pltpu.VMEM((2, tk, LANES), F32))
    return pl.pallas_call(
        body, name=name, grid=(B, P, nk),
        in_specs=[pl.BlockSpec((S, PAIR_Q), lambda b, p, j: (b, p)),
                  pl.BlockSpec((tk, PAIR_KV), lambda b, p, j: (b * nk + j, p)),
                  pl.BlockSpec((S, LANES), lambda b, p, j: (b, p)), rows_spec,
                  pl.BlockSpec((S, LANES), lambda b, p, j: (b, p))],
        out_specs=out_specs, out_shape=out_shape, scratch_shapes=scratch,
        compiler_params=_cparams(("parallel", "parallel", "arbitrary")),
    )(qx, kvx, o, lse, do)


def _fox_consts(P):
    H = 2 * P
    eq = np.zeros((3 * LANES, P * LANES), np.float32)
    ek = np.zeros((3 * LANES, P * LANES), np.float32)
    ones_q = np.zeros((1, P * LANES), np.float32)
    ones_k = np.zeros((1, P * LANES), np.float32)
    for h in range(H):
        base = (h // 2) * LANES + FOX_EXTRA * (h % 2)
        for part in range(3):
            eq[part * LANES + h, base + part] = 1.0
            ones_q[0, base + 3 + part] = 1.0
            ones_k[0, base + part] = 1.0
            ek[part * LANES + h, base + 3 + part] = -1.0
    return eq, ek, ones_q, ones_k


def _split3(f):
    hi = f.astype(BF16)
    r = f - hi.astype(F32)
    mid = r.astype(BF16)
    lo = (r - mid.astype(F32)).astype(BF16)
    return hi, mid, lo


def _tri_sum(tri, x):
    hi, mid, lo = _split3(x)
    return (jnp.dot(tri, hi, preferred_element_type=F32) + jnp.dot(tri, mid, preferred_element_type=F32)
            + jnp.dot(tri, lo, preferred_element_type=F32))


def _log1p_pos(e):
    return jnp.where(e < 0.01, e * (1.0 - e * (0.5 - e * (1.0 / 3.0))), jnp.log(1.0 + e))


def _fox_prep(qkv, fl, b_row, *, S, D, name):
    T = qkv.shape[0]
    P = D // LANES
    B = T // S
    tt = _tile(S, 256)
    per = S // tt
    eq, ek, ones_q, ones_k = _fox_consts(P)
    q_scale = HEAD_DIM ** -0.5

    def body(q_ref, k_ref, v_ref, fl_ref, b_ref, eq_ref, ek_ref, oq_ref, ok_ref, qx_ref, kvx_ref, carry):
        i = pl.program_id(1)

        @pl.when(i == 0)
        def _():
            carry[...] = jnp.zeros_like(carry)

        z = fl_ref[...] + b_ref[...]
        logf = jnp.minimum(z, 0.0) - _log1p_pos(jnp.exp(-jnp.abs(z)))
        row = lax.broadcasted_iota(jnp.int32, (tt, tt), 0)
        col = lax.broadcasted_iota(jnp.int32, (tt, tt), 1)
        tri = (col <= row).astype(BF16)
        f = _tri_sum(tri, logf) + carry[...]
        carry[...] = f[tt - 1:tt, :]
        parts = jnp.concatenate(_split3(f), axis=1)
        xq = jnp.dot(parts, eq_ref[...], preferred_element_type=F32) + oq_ref[...]
        xk = jnp.dot(parts, ek_ref[...], preferred_element_type=F32) + ok_ref[...]
        for p in range(P):
            c = slice(p * LANES, (p + 1) * LANES)
            qx_ref[:, p * PAIR_Q:p * PAIR_Q + LANES] = (q_ref[:, c].astype(F32) * q_scale).astype(BF16)
            qx_ref[:, p * PAIR_Q + LANES:(p + 1) * PAIR_Q] = xq[:, c].astype(BF16)
            kvx_ref[:, p * PAIR_KV:p * PAIR_KV + LANES] = k_ref[:, c]
            kvx_ref[:, p * PAIR_KV + LANES:p * PAIR_KV + PAIR_Q] = xk[:, c].astype(BF16)
            kvx_ref[:, p * PAIR_KV + PAIR_Q:(p + 1) * PAIR_KV] = v_ref[:, c]

    tok = lambda b, i: (b * per + i, 0)
    const = lambda b, i: (0, 0)
    return pl.pallas_call(
        body, name=name, grid=(B, per),
        in_specs=[pl.BlockSpec((tt, D), lambda b, i: (b * per + i, 0)),
                  pl.BlockSpec((tt, D), lambda b, i: (b * per + i, 1)),
                  pl.BlockSpec((tt, D), lambda b, i: (b * per + i, 2)),
                  pl.BlockSpec((tt, LANES), tok), pl.BlockSpec((1, LANES), const),
                  pl.BlockSpec(eq.shape, const), pl.BlockSpec(ek.shape, const),
                  pl.BlockSpec(ones_q.shape, const), pl.BlockSpec(ones_k.shape, const)],
        out_specs=[pl.BlockSpec((tt, P * PAIR_Q), tok), pl.BlockSpec((tt, P * PAIR_KV), tok)],
        out_shape=[jax.ShapeDtypeStruct((T, P * PAIR_Q), BF16), jax.ShapeDtypeStruct((T, P * PAIR_KV), BF16)],
        scratch_shapes=[pltpu.VMEM((1, LANES), F32)],
        compiler_params=_cparams(("arbitrary", "arbitrary")),
    )(qkv, qkv, qkv, fl, b_row, jnp.asarray(eq, BF16), jnp.asarray(ek, BF16), jnp.asarray(ones_q), jnp.asarray(ones_k))


def _fox_unprep(dqx, dkvx, csum, rsum, fl, b_row, *, S, D, name):
    T = dqx.shape[0]
    P = D // LANES
    B = T // S
    tt = _tile(S, 256)
    per = S // tt
    q_scale = HEAD_DIM ** -0.5

    def body(dq_ref, dkv_ref, cs_ref, rs_ref, fl_ref, b_ref, dqkv_ref, db_ref, carry):
        b = pl.program_id(0)
        i = pl.program_id(1)

        @pl.when(i == 0)
        def _():
            carry[...] = jnp.zeros_like(carry)

        @pl.when((i == 0) & (b == 0))
        def _():
            db_ref[...] = jnp.zeros_like(db_ref)

        df = rs_ref[...] - cs_ref[...]
        for p in range(P):
            rq = slice(p * LANES, (p + 1) * LANES)
            dqkv_ref[:, rq] = (dq_ref[:, p * PAIR_Q:p * PAIR_Q + LANES].astype(F32) * q_scale).astype(BF16)
            dqkv_ref[:, D + p * LANES:D + (p + 1) * LANES] = dkv_ref[:, p * PAIR_KV:p * PAIR_KV + LANES]
            dqkv_ref[:, 2 * D + p * LANES:2 * D + (p + 1) * LANES] = dkv_ref[:, p * PAIR_KV + PAIR_Q:(p + 1) * PAIR_KV]
        row = lax.broadcasted_iota(jnp.int32, (tt, tt), 0)
        col = lax.broadcasted_iota(jnp.int32, (tt, tt), 1)
        tri = (col >= row).astype(BF16)
        dlogf = _tri_sum(tri, df) + carry[...]
        carry[...] = dlogf[0:1, :]
        z = fl_ref[...] + b_ref[...]
        e = jnp.exp(-jnp.abs(z))
        sig_neg = jnp.where(z >= 0.0, e, 1.0) / (1.0 + e)
        dfl = dlogf * sig_neg
        dqkv_ref[:, 3 * D:3 * D + LANES] = dfl.astype(BF16)
        db_ref[...] += jnp.sum(dfl, axis=0, keepdims=True)

    rev = lambda b, i: (b * per + per - 1 - i, 0)
    const = lambda b, i: (0, 0)
    return pl.pallas_call(
        body, name=name, grid=(B, per),
        in_specs=[pl.BlockSpec((tt, P * PAIR_Q), rev), pl.BlockSpec((tt, P * PAIR_KV), rev),
                  pl.BlockSpec((tt, LANES), rev), pl.BlockSpec((tt, LANES), rev), pl.BlockSpec((tt, LANES), rev),
                  pl.BlockSpec((1, LANES), const)],
        out_specs=[pl.BlockSpec((tt, 3 * D + LANES), rev), pl.BlockSpec((1, LANES), const)],
        out_shape=[jax.ShapeDtypeStruct((T, 3 * D + LANES), BF16), jax.ShapeDtypeStruct((1, LANES), F32)],
        scratch_shapes=[pltpu.VMEM((1, LANES), F32)],
        compiler_params=_cparams(("arbitrary", "arbitrary")),
    )(dqx, dkvx, csum, rsum, fl, b_row)


def _rms(x):
    r = lax.rsqrt(jnp.mean(x * x, axis=-1, keepdims=True) + NORM_EPS)
    return x * r, r


def _mla_mid(lat, gq, gkv, cos_t, sin_s, *, name):
    T, W = lat.shape
    Rq = W - 2 * LANES
    tt = _tile(T, 512)

    def body(l_ref, gq_ref, gkv_ref, c_ref, s_ref, o_ref, ot_ref):
        nq, _ = _rms(l_ref[:, 0:Rq])
        nkv, _ = _rms(l_ref[:, Rq:Rq + LANES])
        parts = [nq * gq_ref[...], nkv * gkv_ref[...], _rope128(l_ref[:, Rq + LANES:W], c_ref[...], s_ref[...])]
        out = jnp.concatenate(parts, axis=1)
        o_ref[...] = out.astype(BF16)
        ot_ref[...] = out.T.astype(BF16)

    return pl.pallas_call(
        body, name=name, grid=(T // tt,),
        in_specs=[pl.BlockSpec((tt, W), lambda i: (i, 0)), pl.BlockSpec((1, Rq), lambda i: (0, 0)),
                  pl.BlockSpec((1, LANES), lambda i: (0, 0)), pl.BlockSpec((tt, LANES), lambda i: (i, 0)),
                  pl.BlockSpec((tt, LANES), lambda i: (i, 0))],
        out_specs=[pl.BlockSpec((tt, W), lambda i: (i, 0)), pl.BlockSpec((W, tt), lambda i: (0, i))],
        out_shape=[jax.ShapeDtypeStruct((T, W), BF16), jax.ShapeDtypeStruct((W, T), BF16)],
        compiler_params=_cparams(("parallel",)),
    )(lat, gq, gkv, cos_t, sin_s)


def _mla_mid_bwd(lat, dcq, dckr, gq, gkv, cos_t, sin_s, *, name):
    T, W = lat.shape
    Rq = W - 2 * LANES
    tt = _tile(T, 512)

    def norm_bwd(x, dy, g):
        n, r = _rms(x)
        dn = dy * g
        return r * (dn - n * jnp.mean(dn * n, axis=-1, keepdims=True)), jnp.sum(dy * n, axis=0, keepdims=True)

    def body(l_ref, dq_ref, dk_ref, gq_ref, gkv_ref, c_ref, s_ref, o_ref, dgq_ref, dgkv_ref):
        i = pl.program_id(0)

        @pl.when(i == 0)
        def _():
            dgq_ref[...] = jnp.zeros_like(dgq_ref)
            dgkv_ref[...] = jnp.zeros_like(dgkv_ref)

        dxq, dgq = norm_bwd(l_ref[:, 0:Rq], dq_ref[...], gq_ref[...])
        dxkv, dgkv = norm_bwd(l_ref[:, Rq:Rq + LANES], dk_ref[:, 0:LANES], gkv_ref[...])
        o_ref[:, 0:Rq] = dxq.astype(BF16)
        o_ref[:, Rq:Rq + LANES] = dxkv.astype(BF16)
        o_ref[:, Rq + LANES:W] = _rope128(dk_ref[:, LANES:2 * LANES], c_ref[...], -s_ref[...]).astype(BF16)
        dgq_ref[...] += dgq
        dgkv_ref[...] += dgkv

    return pl.pallas_call(
        body, name=name, grid=(T // tt,),
        in_specs=[pl.BlockSpec((tt, W), lambda i: (i, 0)), pl.BlockSpec((tt, Rq), lambda i: (i, 0)),
                  pl.BlockSpec((tt, 2 * LANES), lambda i: (i, 0)), pl.BlockSpec((1, Rq), lambda i: (0, 0)),
                  pl.BlockSpec((1, LANES), lambda i: (0, 0)), pl.BlockSpec((tt, LANES), lambda i: (i, 0)),
                  pl.BlockSpec((tt, LANES), lambda i: (i, 0))],
        out_specs=[pl.BlockSpec((tt, W), lambda i: (i, 0)), pl.BlockSpec((1, Rq), lambda i: (0, 0)),
                   pl.BlockSpec((1, LANES), lambda i: (0, 0))],
        out_shape=[jax.ShapeDtypeStruct((T, W), BF16), jax.ShapeDtypeStruct((1, Rq), F32),
                   jax.ShapeDtypeStruct((1, LANES), F32)],
        compiler_params=_cparams(("arbitrary",)),
    )(lat, dcq, dckr, gq, gkv, cos_t, sin_s)


def _uq_to_pairs(w):
    Rq = w.shape[0]
    P = w.shape[1] // (2 * (HEAD_DIM + ROPE_DIM))
    w4 = w.reshape(Rq, P, 2, HEAD_DIM + ROPE_DIM)
    nope = w4[..., :HEAD_DIM].reshape(Rq, P, 2 * HEAD_DIM)
    rope = w4[..., HEAD_DIM:].reshape(Rq, P, 2 * ROPE_DIM)
    pad = jnp.zeros((Rq, P, PAIR_Q - 2 * HEAD_DIM - 2 * ROPE_DIM), w.dtype)
    return jnp.concatenate([nope, rope, pad], axis=-1).reshape(Rq, P * PAIR_Q)


def _uq_from_pairs(g):
    Rq = g.shape[0]
    P = g.shape[1] // PAIR_Q
    g3 = g.reshape(Rq, P, PAIR_Q)
    nope = g3[..., :2 * HEAD_DIM].reshape(Rq, P, 2, HEAD_DIM)
    rope = g3[..., 2 * HEAD_DIM:2 * HEAD_DIM + 2 * ROPE_DIM].reshape(Rq, P, 2, ROPE_DIM)
    return jnp.concatenate([nope, rope], axis=-1).reshape(Rq, P * 2 * (HEAD_DIM + ROPE_DIM))


def _ukv_to_pairs(w):
    P = w.shape[1] // (4 * HEAD_DIM)
    w4 = w.reshape(KV_RANK, P, 2, 2 * HEAD_DIM)
    kn = w4[..., :HEAD_DIM].reshape(KV_RANK, P, 2 * HEAD_DIM)
    vv = w4[..., HEAD_DIM:].reshape(KV_RANK, P, 2 * HEAD_DIM)
    top = jnp.concatenate([kn, jnp.zeros((KV_RANK, P, LANES), w.dtype), vv], axis=-1)
    place = np.zeros((LANES, P, PAIR_KV), np.float32)
    for r in range(ROPE_DIM):
        place[r, :, LANES + r] = 1.0
        place[r, :, LANES + ROPE_DIM + r] = 1.0
    return jnp.concatenate([top, jnp.asarray(place, w.dtype)], axis=0).reshape(KV_RANK + LANES, P * PAIR_KV)


def _ukv_from_pairs(g):
    P = g.shape[1] // PAIR_KV
    g3 = g[:KV_RANK].reshape(KV_RANK, P, PAIR_KV)
    kn = g3[..., :2 * HEAD_DIM].reshape(KV_RANK, P, 2, HEAD_DIM)
    vv = g3[..., PAIR_Q:].reshape(KV_RANK, P, 2, HEAD_DIM)
    return jnp.concatenate([kn, vv], axis=-1).reshape(KV_RANK, P * 4 * HEAD_DIM)


def _residual_then_norm(acc, xr, g, gain, sc, sh):
    x_out = xr + g * acc
    r = lax.rsqrt(jnp.mean(x_out * x_out, axis=-1, keepdims=True) + NORM_EPS)
    h = (x_out * r) * gain * (1.0 + sc) + sh
    return x_out, acc, h, h


def _gated_out(a, w_stack, layer, x, gate, next_norm, *, S, name):
    if next_norm is None:
        return _mm(a, w_stack, "nn", name=name, b_layer=layer, out_dtypes=(F32, BF16), extras=(x,), rowvecs=(gate,),
                   seq=S, epilogue=lambda acc, xr, g: (xr + g * acc, acc)) + (None, None)
    long_k = a.shape[1] > 2048
    return _mm(a, w_stack, "nn", name=name, b_layer=layer, out_dtypes=(F32, BF16, BF16, BF16),
               out_t=(False, False, False, True), extras=(x,), rowvecs=(gate,) + tuple(next_norm), seq=S,
               full_rows=True, tk=a.shape[1], epilogue=_residual_then_norm,
               vmem_budget=MM_VMEM_BUDGET + (8 * 1024 * 1024 if long_k else 0))


def _mlp_fwd(h2, w, i, x1, gate, next_norm, *, S):
    def act(acc):
        u = jnp.square(jnp.maximum(acc, 0.0))
        return acc, u, u

    p, u, u_t = _mm(h2, w["mlp_w1"], "nn", name=f"mlp_up_{i}", b_layer=i, out_dtypes=(BF16, BF16, BF16),
                    out_t=(False, False, True), epilogue=act)
    x2, z, h, h_t = _gated_out(u, w["mlp_w2"], i, x1, gate, next_norm, S=S, name=f"mlp_down_{i}")
    return x2, (p, u_t, z), h, h_t


STACKED_GRADS = ("fox_out", "mla_down", "mla_uq", "mla_ukv", "mla_out", "mlp_w1", "mlp_w2")


def _local_step(x, target, pos_f, inv_freq_row, sign_row, mod, w, slots, *, S, hooks=None):
    hooks = hooks or {}
    T, D = x.shape
    L = mod.shape[0]
    L2 = len(w["fox_out"])
    cos_t, sin_s = _rope_tables(pos_f, inv_freq_row, sign_row)
    saved = []
    B = mod.shape[2]

    def per_sequence(gain):
        return jnp.broadcast_to(gain[None], (B,) + gain.shape)

    h, h_t = _norm_mod(x, w["norm_mix_g"][0], mod[0, 1], mod[0, 0], S=S, name="norm_mix_0")
    for i in range(L):
        j = i // 2
        sh_m, sc_m, g_m, sh_f, sc_f, g_f = (mod[i, s] for s in range(6))
        if i % 2 == 0:
            qkv = _mm(h, w["fox_qkv"], "nn", name=f"fox_qkv_{i}", b_layer=j, out_dtypes=(BF16,))
            fl = _mm(h, w["fox_f"], "nn", name=f"fox_f_{i}", b_layer=j)
            qx, kvx = _fox_prep(qkv, fl, w["fox_b"][j], S=S, D=D, name=f"fox_prep_{i}")
            o, lse, o_t = _attn_fwd(qx, kvx, S=S, ew=FOX_EXTRA, name=f"fox_attn_{i}")
            mix = (qx, kvx, o, lse, o_t, fl)
            w_out = w["fox_out"]
        else:
            lat = _mm(h, w["mla_down"], "nn", name=f"mla_down_{i}", b_layer=j)
            Rq = lat.shape[1] - 2 * LANES
            cqr, cqr_t = _mla_mid(lat, w["mla_gq"][j], w["mla_gkv"][j], cos_t, sin_s, name=f"mla_mid_{i}")
            qx = _mm(cqr, w["mla_uq"], "nn", name=f"mla_uq_{i}", b_layer=j, out_dtypes=(BF16,), a_sz=Rq, tk=Rq,
                     tables=(cos_t, sin_s), epilogue=lambda acc, c, s: (_rope_pairs(acc * MLA_SCALE, c, s, 1.0),))
            kvx = _mm(cqr, w["mla_ukv"], "nn", name=f"mla_ukv_{i}", b_layer=j, out_dtypes=(BF16,), a_off=Rq,
                      a_sz=2 * LANES, tk=2 * LANES, tn=PAIR_KV)
            o, lse, o_t = _attn_fwd(qx, kvx, S=S, ew=ROPE_DIM, name=f"mla_attn_{i}")
            mix = (qx, kvx, o, lse, o_t, lat, cqr_t)
            w_out = w["mla_out"]
        x1, y, h2, h2_t = _gated_out(o, w_out, j, x, g_m, (per_sequence(w["norm_mlp_g"][i]), sc_f, sh_f), S=S,
                                     name=f"mix_out_{i}")
        if i == 0 and "fwd_mlp0" in hooks:
            w = hooks["fwd_mlp0"](x1, w)
        next_norm = (per_sequence(w["norm_mix_g"][i + 1]), mod[i + 1, 1], mod[i + 1, 0]) if i + 1 < L else None
        x2, mlp, h_next, h_next_t = _mlp_fwd(h2, w, i, x1, g_f, next_norm, S=S)
        saved.append((x, h_t, mix, y, x1, h2_t, mlp))
        x, h, h_t = x2, h_next, h_next_t
        if i == 0 and "fwd_layer1" in hooks:
            w = hooks["fwd_layer1"](x, w)

    dx, dg_final, loss = _final_loss(x, target, w["final_norm_g"])
    n_split = w["mlp_w1"][0][0].shape[1]

    grads = {k: [None] * len(w[k]) for k in ("norm_mix_g", "norm_mlp_g", "fox_b", "mla_gq", "mla_gkv")}
    grads["fox_in"] = [None] * L2
    grads.update({k: {} for k in STACKED_GRADS})
    grads["final_norm_g"] = dg_final

    def stacked(key, layer, _, a_t, b, **kw):
        group, idx, count = slots[(key, layer)]
        grads[key][group] = _mm(a_t, b, "nn", out_stack=(grads[key].get(group), idx, count), **kw)

    dmod = [None] * L
    for i in reversed(range(L)):
        j = i // 2
        x0, h_t, mix, y, x1, h2_t, (p, u_t, z) = saved[i]
        sh_m, sc_m, g_m, sh_f, sc_f, g_f = (mod[i, s] for s in range(6))
        if i == L - 1:
            dz, dg_f = _gate_bwd(dx, z, g_f, S=S, name=f"gate_mlp_bwd_{i}")
        else:
            dz, dg_f = dz_below, dg_f_below
        stacked("mlp_w2", i, L, u_t, dz, name=f"mlp_w2_grad_{i}")
        dp = _mm(dz, w["mlp_w2"], "nt", name=f"mlp_down_bwd_{i}", b_layer=i, out_dtypes=(BF16,), extras=(p,),
                 epilogue=lambda acc, pv: (acc * (2.0 * jnp.maximum(pv.astype(F32), 0.0)),))
        stacked("mlp_w1", i, L, h2_t, dp, name=f"mlp_w1_grad_{i}", out_split=n_split)
        if i == 0 and "bwd_mix0" in hooks:
            g_m = g_m + hooks["bwd_mix0"](grads)[0, 0]
        dh2 = _mm(dp, w["mlp_w1"], "nt", name=f"mlp_up_bwd_{i}", b_layer=i, out_dtypes=(BF16,))
        dx1, dsh_f, dsc_f, dgn, dy, dg_m = _norm_mod_bwd(x1, dh2, dx, w["norm_mlp_g"][i], sc_f, gate=(y, g_m), S=S,
                                                         name=f"norm_mlp_bwd_{i}")
        grads["norm_mlp_g"][i] = dgn
        if i % 2 == 0:
            qx, kvx, o, lse, o_t, fl = mix
            stacked("fox_out", j, L2, o_t, dy, name=f"fox_out_grad_{i}")
            do = _mm(dy, w["fox_out"], "nt", name=f"fox_out_bwd_{i}", b_layer=j, out_dtypes=(BF16,))
            dqx, dkvx, csum, rsum = _attn_bwd(qx, kvx, o, lse, do, S=S, ew=FOX_EXTRA, name=f"fox_attn_bwd_{i}",
                                              bias_grad=True)
            n_heads = D // HEAD_DIM
            csum = jnp.pad(csum.reshape(T, n_heads, HEAD_DIM)[:, :, 0], ((0, 0), (0, LANES - n_heads)))
            rsum = jnp.transpose(rsum[:, :, :2, :], (0, 3, 1, 2)).reshape(T, n_heads)
            rsum = jnp.pad(rsum, ((0, 0), (0, LANES - n_heads)))
            dproj, db = _fox_unprep(dqx, dkvx, csum, rsum, fl, w["fox_b"][j], S=S, D=D, name=f"fox_unprep_{i}")
            grads["fox_b"][j] = db
            grads["fox_in"][j] = _mm(h_t, dproj, "nn", name=f"fox_in_grad_{i}")
            dh = _mm(dproj, w["fox_in"], "nt", name=f"fox_in_bwd_{i}", b_layer=j, out_dtypes=(BF16,),
                     tk=dproj.shape[1])
        else:
            qx, kvx, o, lse, o_t, lat, cqr_t = mix
            Rq = lat.shape[1] - 2 * LANES
            stacked("mla_out", j, L2, o_t, dy, name=f"mla_out_grad_{i}")
            do = _mm(dy, w["mla_out"], "nt", name=f"mla_out_bwd_{i}", b_layer=j, out_dtypes=(BF16,))
            dqx, dkvx = _attn_bwd(qx, kvx, o, lse, do, S=S, ew=ROPE_DIM, name=f"mla_attn_bwd_{i}")
            dqpre = _unrope(dqx, cos_t, sin_s)
            stacked("mla_uq", j, L2, cqr_t[:Rq], dqpre, name=f"mla_uq_grad_{i}", out_split=n_split)
            stacked("mla_ukv", j, L2, cqr_t[Rq:], dkvx, name=f"mla_ukv_grad_{i}", tn=PAIR_KV, out_split=n_split)
            dcq = _mm(dqpre, w["mla_uq"], "nt", name=f"mla_uq_bwd_{i}", b_layer=j)
            dckr = _mm(dkvx, w["mla_ukv"], "nt", name=f"mla_ukv_bwd_{i}", b_layer=j, tk=PAIR_KV * 2)
            dlat, dgq, dgkv = _mla_mid_bwd(lat, dcq, dckr, w["mla_gq"][j], w["mla_gkv"][j], cos_t, sin_s,
                                           name=f"mla_mid_bwd_{i}")
            grads["mla_gq"][j] = dgq
            grads["mla_gkv"][j] = dgkv
            stacked("mla_down", j, L2, h_t, dlat, name=f"mla_down_grad_{i}")
            dh = _mm(dlat, w["mla_down"], "nt", name=f"mla_down_bwd_{i}", b_layer=j, out_dtypes=(BF16,))
        if i == 0:
            dx, dsh_m, dsc_m, dgn = _norm_mod_bwd(x0, dh, dx1, w["norm_mix_g"][i], sc_m, S=S, name=f"norm_mix_bwd_{i}")
        else:
            gate_below = mod[i - 1, 5]
            if i == 1 and "bwd_layer0" in hooks:
                gate_below = gate_below + hooks["bwd_layer0"](grads)[0, 0]
            dx, dsh_m, dsc_m, dgn, dz_below, dg_f_below = _norm_mod_bwd(
                x0, dh, dx1, w["norm_mix_g"][i], sc_m, gate=(saved[i - 1][6][2], gate_below), S=S,
                name=f"norm_mix_bwd_{i}")
        grads["norm_mix_g"][i] = dgn
        dmod[i] = jnp.stack([dsh_m, dsc_m, dg_m, dsh_f, dsc_f, dg_f])
    return loss, dx, jnp.stack(dmod), grads


GATHERED = ("fox_in", "fox_out", "mla_down", "mla_uq", "mla_ukv", "mla_out", "mlp_w1", "mlp_w2")
ROW_SHARDED = ("fox_out", "mla_down", "mla_out", "mlp_w2")


def _shard_layouts(wts):
    dkv = wts["mla_w_dkv"]
    dkv = jnp.pad(dkv, ((0, 0), (0, 0), (0, 2 * LANES - dkv.shape[2])))
    return {
        "fox_in": _pad_lanes(wts["fox_w_in"].astype(BF16)),
        "fox_out": wts["fox_w_out"].astype(BF16),
        "mla_down": jnp.concatenate([wts["mla_w_dq"], dkv], axis=2).astype(BF16),
        "mla_uq": jax.vmap(_uq_to_pairs)(wts["mla_w_uq"].astype(BF16)),
        "mla_ukv": jax.vmap(_ukv_to_pairs)(wts["mla_w_ukv"].astype(BF16)),
        "mla_out": wts["mla_w_out"].astype(BF16),
        "mlp_w1": wts["mlp_w1"].astype(BF16),
        "mlp_w2": wts["mlp_w2"].astype(BF16),
    }


def _small_layouts(small):
    return {
        "fox_b": [jnp.pad(b, (0, LANES - b.shape[0]))[None, :] for b in small["fox_b_f"]],
        "mla_gq": [g[None, :] for g in small["mla_q_norm_g"]],
        "mla_gkv": [g[None, :] for g in small["mla_kv_norm_g"]],
        "norm_mix_g": [g[None, :] for g in small["norm_mix_g"]],
        "norm_mlp_g": [g[None, :] for g in small["norm_mlp_g"]],
        "final_norm_g": small["final_norm_g"][None, :],
    }


def _comm_groups(L, L2):
    rest = [("fox_in", 1, L2 - 1), ("fox_out", 1, L2 - 1), ("mla_down", 0, L2), ("mla_uq", 0, L2),
            ("mla_ukv", 0, L2), ("mla_out", 0, L2), ("mlp_w1", 1, L - 1), ("mlp_w2", 1, L - 1)]
    return {"mix0": [("fox_in", 0, 1), ("fox_out", 0, 1)], "mlp0": [("mlp_w1", 0, 1), ("mlp_w2", 0, 1)],
            "rest": [e for e in rest if e[2] > 0]}


def _layer_slots(groups):
    return {(n, s + l): (g, l, cnt) for g, entries in groups.items() for n, s, cnt in entries for l in range(cnt)}


def _pad_lanes(a):
    cols = a.shape[-1]
    return jnp.pad(a, [(0, 0)] * (a.ndim - 1) + [(0, -cols % LANES)])


def _weight_views(name, gathered, D, n_fox_heads):
    n, ns, rows, cols = gathered.shape
    if name == "fox_in":
        true_cols = (3 * D + n_fox_heads) // ns
        fox = jnp.concatenate([gathered[:, k, :, :true_cols] for k in range(ns)], axis=-1)
        return {"fox_qkv": fox[:, :, :3 * D], "fox_f": _pad_lanes(fox[:, :, 3 * D:]), "fox_in": _pad_lanes(fox)}
    if name in ROW_SHARDED:
        return {name: gathered.reshape(n, ns * rows, cols)}
    return {name: gathered}


def _grad_pieces(name, g, qkv_f, n_fox_heads, ns):
    if name == "fox_in":
        D = qkv_f[0].shape[0]
        fox = jnp.stack([a[:, :3 * D + n_fox_heads] for a in qkv_f])
        cols = fox.shape[2] // ns
        return jnp.stack([_pad_lanes(fox[:, :, k * cols:(k + 1) * cols]) for k in range(ns)], axis=1)
    if name in ROW_SHARDED:
        return g.reshape(g.shape[0], ns, g.shape[1] // ns, g.shape[2])
    return g


def _small_grads(g, n_fox_heads):
    return {
        "norm_mix_g": jnp.concatenate(g["norm_mix_g"], axis=0),
        "norm_mlp_g": jnp.concatenate(g["norm_mlp_g"], axis=0),
        "final_norm_g": g["final_norm_g"][0],
        "fox_b_f": jnp.concatenate(g["fox_b"], axis=0)[:, :n_fox_heads],
        "mla_q_norm_g": jnp.concatenate(g["mla_gq"], axis=0),
        "mla_kv_norm_g": jnp.concatenate(g["mla_gkv"], axis=0),
    }


def _silu(c):
    return c * (1.0 / (1.0 + jnp.exp(-c)))


def _ada_fwd(c_all, ada_w, ada_b_cols):
    L, D, C = ada_w.shape
    Bg = c_all.shape[0]
    tc = _tile(C, 512)

    def body(c_ref, w_ref, b_ref, o_ref):
        ca = _silu(c_ref[...]).astype(BF16)
        o_ref[...] = jnp.dot(ca, w_ref[...].astype(BF16), preferred_element_type=F32) + b_ref[...]

    return pl.pallas_call(
        body, name="ada_fwd", grid=(L, C // tc),
        in_specs=[pl.BlockSpec((Bg, D), lambda l, j: (0, 0)), pl.BlockSpec((None, D, tc), lambda l, j: (l, 0, j)),
                  pl.BlockSpec((None, 1, tc), lambda l, j: (l, 0, j))],
        out_specs=pl.BlockSpec((None, Bg, tc), lambda l, j: (l, 0, j)),
        out_shape=jax.ShapeDtypeStruct((L, Bg, C), F32),
        compiler_params=_cparams(("parallel", "parallel")),
    )(c_all, ada_w, ada_b_cols)


def _ada_bwd(c_all, dmod_cols):
    L, Bg, C = dmod_cols.shape
    D = c_all.shape[1]
    tc = _tile(C, 512)

    def body(c_ref, d_ref, o_ref):
        ca = _silu(c_ref[...]).astype(BF16)
        o_ref[...] = _dot_tn(ca, d_ref[...].astype(BF16))

    return pl.pallas_call(
        body, name="ada_bwd", grid=(L, C // tc),
        in_specs=[pl.BlockSpec((Bg, D), lambda l, j: (0, 0)), pl.BlockSpec((None, Bg, tc), lambda l, j: (l, 0, j))],
        out_specs=pl.BlockSpec((None, D, tc), lambda l, j: (l, 0, j)),
        out_shape=jax.ShapeDtypeStruct((L, D, C), F32),
        compiler_params=_cparams(("parallel", "parallel")),
    )(c_all, dmod_cols)


def _adamw_update(w, gv, m, v):
    mn = ADAM_B1 * m + (1.0 - ADAM_B1) * gv
    vn = ADAM_B2 * v + (1.0 - ADAM_B2) * jnp.square(gv)
    m_hat = mn / (1.0 - ADAM_B1 ** ADAM_STEP)
    v_hat = vn / (1.0 - ADAM_B2 ** ADAM_STEP)
    return -ADAM_LR * (m_hat / (jnp.sqrt(v_hat) + ADAM_EPS) + ADAM_WD * w), mn, vn


def _adamw(w, g, m, v, *, name):
    shape = w.shape
    C = shape[-1]
    R = int(np.prod(shape[:-1])) if len(shape) > 1 else 1
    w2, g2, m2, v2 = (a.reshape(R, C) for a in (w, g, m, v))
    tr = _row_tile(R, C)

    def body(w_ref, g_ref, m_ref, v_ref, d_ref, nm_ref, nv_ref):
        d_ref[...], nm_ref[...], nv_ref[...] = _adamw_update(w_ref[...], g_ref[...], m_ref[...], v_ref[...])

    spec = pl.BlockSpec((tr, C), lambda i: (i, 0))
    out = pl.pallas_call(
        body, name=name, grid=(R // tr,), in_specs=[spec] * 4, out_specs=[spec] * 3,
        out_shape=[jax.ShapeDtypeStruct((R, C), F32)] * 3, compiler_params=_cparams(("parallel",)),
    )(w2, g2, m2, v2)
    return tuple(a.reshape(shape) for a in out)


def _adamw_halves(w, g_own, g_peer, m, v, c_idx, *, name):
    L, rows, C = w.shape
    R = rows // 2
    tr = _row_tile(R, C)

    def body(c_ref, w_ref, go_ref, gp_ref, m_ref, v_ref, g_ref, d_ref, nm_ref, nv_ref):
        gv = jnp.where(pl.program_id(1) == c_ref[0], go_ref[...], gp_ref[...])
        g_ref[...] = gv
        d_ref[...], nm_ref[...], nv_ref[...] = _adamw_update(w_ref[...], gv, m_ref[...], v_ref[...])

    full = pl.BlockSpec((None, None, tr, C), lambda l, hh, i, c_ref: (l, hh, i, 0))
    half = pl.BlockSpec((None, tr, C), lambda l, hh, i, c_ref: (l, i, 0))
    grid_spec = pltpu.PrefetchScalarGridSpec(
        num_scalar_prefetch=1, grid=(L, 2, R // tr), in_specs=[full, half, half, full, full], out_specs=[full] * 4)
    split = lambda a: a.reshape(L, 2, R, C)
    out = pl.pallas_call(
        body, name=name, grid_spec=grid_spec, out_shape=[jax.ShapeDtypeStruct((L, 2, R, C), F32)] * 4,
        compiler_params=_cparams(("parallel", "parallel", "parallel")),
    )(c_idx, split(w), g_own, g_peer, split(m), split(v))
    return tuple(a.reshape(w.shape) for a in out)


def _sum_gathered(dm8, sm8):
    n_dev, Bl, R, D = dm8.shape
    Rs = sm8.shape[1]

    def body(dm_ref, sm_ref, ob_ref, os_ref):
        acc_b = jnp.zeros((R, D), F32)
        acc_s = jnp.zeros((Rs, D), F32)
        for d in range(n_dev):
            for b in range(Bl):
                acc_b = acc_b + dm_ref[d, b]
            acc_s = acc_s + sm_ref[d]
        ob_ref[...] = acc_b
        os_ref[...] = acc_s

    return pl.pallas_call(
        body, name="sum_gathered",
        out_shape=[jax.ShapeDtypeStruct((R, D), F32), jax.ShapeDtypeStruct((Rs, D), F32)],
        compiler_params=_cparams(None),
    )(dm8, sm8)


N_DEV = 8
N_CHIP = 4
ANY = pl.BlockSpec(memory_space=pl.ANY)
HBM = pl.BlockSpec(memory_space=pltpu.HBM)
SEM = pl.BlockSpec(memory_space=pltpu.SEMAPHORE)
DATAFLOW = pltpu.SideEffectType.DATAFLOW_SIDE_EFFECTING


def _mesh_pos():
    return lax.axis_index("x"), lax.axis_index("y"), lax.axis_index("c")


def _all_gather8(block, *, name, in_vmem):
    R, W = block.shape

    def body(x_ref, out_ref, send_sems, recv_sems, local_sem):
        x, y, c = _mesh_pos()
        me, sibling = (x, y, c), (x, y, 1 - c)
        chips = [(1 - x, y), (x, 1 - y), (1 - x, 1 - y)]

        def slot(px, py, pc):
            return out_ref.at[4 * px + 2 * py + pc]

        def copy(k, blk, to, src=None):
            return pltpu.make_async_remote_copy(
                src_ref=slot(*blk) if src is None else src, dst_ref=slot(*blk),
                send_sem=send_sems.at[k], recv_sem=recv_sems.at[k], device_id=to, device_id_type=MESH_ID)

        mine = pltpu.make_async_copy(x_ref, slot(*me), local_sem)
        mine.start()
        first = [copy(0, me, sibling, src=x_ref)]
        first += [copy(1 + j, me, (*chip, c), src=x_ref) for j, chip in enumerate(chips)]
        for cp in first:
            cp.start()
        passed = [copy(4 + j, (*chip, c), sibling) for j, chip in enumerate(chips)]
        for j, chip in enumerate(chips):
            copy(1 + j, (*chip, c), me).wait_recv()
            passed[j].start()
        copy(0, sibling, me).wait_recv()
        for j, chip in enumerate(chips):
            copy(4 + j, (*chip, 1 - c), me).wait_recv()
        for cp in first + passed:
            cp.wait_send()
        mine.wait()

    space = pl.BlockSpec(memory_space=pltpu.VMEM) if in_vmem else ANY
    return pl.pallas_call(
        body, name=name, out_shape=jax.ShapeDtypeStruct((N_DEV, R, W), block.dtype),
        in_specs=[space], out_specs=space,
        scratch_shapes=[pltpu.SemaphoreType.DMA((7,)), pltpu.SemaphoreType.DMA((7,)), pltpu.SemaphoreType.DMA],
        compiler_params=pltpu.CompilerParams(vmem_limit_bytes=VMEM_LIMIT_V7X),
    )(block)


def _comm_call(body, arrays, out_shapes, n_sems, *, name):
    return pl.pallas_call(
        body, name=name, out_shape=out_shapes, in_specs=[ANY] * len(arrays), out_specs=[ANY] * len(out_shapes),
        scratch_shapes=[pltpu.SemaphoreType.DMA((n_sems,)), pltpu.SemaphoreType.DMA((n_sems,)),
                        pltpu.SemaphoreType.DMA((len(arrays),))],
    )(*arrays)


def _gather_weights(shards, *, name):
    n = len(shards)

    def body(*refs):
        xs, outs = refs[:n], refs[n:2 * n]
        send_sems, recv_sems, local_sems = refs[2 * n:]
        x, y, c = _mesh_pos()
        me, sibling = (x, y, c), (x, y, 1 - c)
        chips = [(1 - x, y), (x, 1 - y), (1 - x, 1 - y)]
        waits = []
        for i in range(n):
            nl = shards[i].shape[0]
            own = xs[i].at[pl.ds(0, nl), c]

            def slot(px, py, pc, i=i, nl=nl):
                return outs[i].at[pl.ds(0, nl), 2 * px + py, pc]

            def copy(k, blk, to, src=None, i=i, slot=slot):
                return pltpu.make_async_remote_copy(
                    src_ref=slot(*blk) if src is None else src, dst_ref=slot(*blk),
                    send_sem=send_sems.at[7 * i + k], recv_sem=recv_sems.at[7 * i + k], device_id=to,
                    device_id_type=MESH_ID)

            mine = pltpu.make_async_copy(own, slot(*me), local_sems.at[i])
            mine.start()
            first = [copy(0, me, sibling, src=own)]
            first += [copy(1 + j, me, (*chip, c), src=own) for j, chip in enumerate(chips)]
            for cp in first:
                cp.start()
            waits.append((copy, mine, first))
        for copy, mine, first in waits:
            passed = [copy(4 + j, (*chip, c), sibling) for j, chip in enumerate(chips)]
            for j, chip in enumerate(chips):
                copy(1 + j, (*chip, c), me).wait_recv()
                passed[j].start()
            copy(0, sibling, me).wait_recv()
            for j, chip in enumerate(chips):
                copy(4 + j, (*chip, 1 - c), me).wait_recv()
            for cp in first + passed:
                cp.wait_send()
            mine.wait()

    out_shapes = [jax.ShapeDtypeStruct((s.shape[0], N_CHIP) + s.shape[1:], s.dtype) for s in shards]
    return _comm_call(body, shards, out_shapes, 7 * n, name=name)


def _place_own(shard, chip_idx, c_idx, *, name):
    n, _, rows, cols = shard.shape
    tr = _row_tile(rows, cols)

    def body(k_ref, c_ref, x_ref, o_ref):
        o_ref[...] = x_ref[...]

    grid_spec = pltpu.PrefetchScalarGridSpec(
        num_scalar_prefetch=2, grid=(n, rows // tr),
        in_specs=[pl.BlockSpec((None, None, tr, cols), lambda l, i, k_ref, c_ref: (l, c_ref[0], i, 0))],
        out_specs=pl.BlockSpec((None, None, None, tr, cols), lambda l, i, k_ref, c_ref: (l, k_ref[0], c_ref[0], i, 0)))
    return pl.pallas_call(
        body, name=name, grid_spec=grid_spec,
        out_shape=jax.ShapeDtypeStruct((n, N_CHIP, 2, rows, cols), shard.dtype),
        compiler_params=_cparams(("parallel", "parallel")),
    )(chip_idx, c_idx, shard)


def _gather_copies(x_refs, land_refs, send_sems, recv_sems):
    x, y, c = _mesh_pos()
    k_me = 2 * x + y
    targets = [(x, y, 1 - c), (1 - x, y, c), (x, 1 - y, c), (1 - x, 1 - y, c)]
    copies = []
    for i, (x_ref, land_ref) in enumerate(zip(x_refs, land_refs)):
        nl = x_ref.shape[0]
        for j, to in enumerate(targets):
            copies.append(pltpu.make_async_remote_copy(
                src_ref=x_ref.at[pl.ds(0, nl), c], dst_ref=land_ref.at[pl.ds(0, nl), k_me, c],
                send_sem=send_sems.at[4 * i + j], recv_sem=recv_sems.at[4 * i + j], device_id=to,
                device_id_type=MESH_ID))
    return copies


def _split_start(copies_fn, srcs, lands, after, *, name, sems_per_array):
    n = len(srcs)

    def body(*refs):
        send_sems, recv_sems = refs[2 * n + 1], refs[2 * n + 2]
        for cp in copies_fn(refs[:n], refs[n:2 * n], send_sems, recv_sems):
            cp.start()
        refs[-1][...] = jnp.zeros_like(refs[-1])

    operands = [pltpu.with_memory_space_constraint(a, pltpu.HBM) for a in list(srcs) + list(lands)]
    n_sems = sems_per_array * n
    out_shape = ([pltpu.SemaphoreType.DMA((n_sems,)), pltpu.SemaphoreType.DMA((n_sems,))]
                 + [pltpu.HBM(a.shape, a.dtype) for a in operands] + [jax.ShapeDtypeStruct((8, LANES), F32)])
    res = pl.pallas_call(
        body, name=name, out_shape=out_shape, in_specs=[HBM] * (2 * n) + [ANY],
        out_specs=[SEM, SEM] + [HBM] * (2 * n) + [pl.BlockSpec(memory_space=pltpu.VMEM)],
        input_output_aliases={i: 2 + i for i in range(2 * n)},
        compiler_params=pltpu.CompilerParams(has_side_effects=DATAFLOW),
    )(*operands, after)
    return res[0], res[1], list(res[2:2 + n]), list(res[2 + n:2 + 2 * n]), res[-1]


def _split_wait(copies_fn, send_sems, recv_sems, srcs, lands, after, *, name):
    n = len(srcs)

    def body(*refs):
        for cp in copies_fn(refs[:n], refs[n:2 * n], refs[2 * n], refs[2 * n + 1]):
            cp.wait_send()
            cp.wait_recv()

    res = pl.pallas_call(
        body, name=name, out_shape=[pltpu.HBM(a.shape, a.dtype) for a in list(srcs) + list(lands)],
        in_specs=[HBM] * (2 * n) + [SEM, SEM, ANY], out_specs=[HBM] * (2 * n),
        input_output_aliases={i: i for i in range(2 * n)},
        compiler_params=pltpu.CompilerParams(has_side_effects=DATAFLOW),
    )(*srcs, *lands, send_sems, recv_sems, after)
    return list(res[:n]), list(res[n:])


def _gather_forward(lands, *, name):
    n = len(lands)

    def body(*refs):
        xs = refs[:n]
        send_sems, recv_sems, _ = refs[2 * n:]
        x, y, c = _mesh_pos()
        chips = [(1 - x, y), (x, 1 - y), (1 - x, 1 - y)]
        copies = []
        for i in range(n):
            nl = lands[i].shape[0]
            for j, (cx, cy) in enumerate(chips):
                here = xs[i].at[pl.ds(0, nl), 2 * cx + cy, c]
                cp = pltpu.make_async_remote_copy(
                    src_ref=here, dst_ref=here, send_sem=send_sems.at[3 * i + j], recv_sem=recv_sems.at[3 * i + j],
                    device_id=(x, y, 1 - c), device_id_type=MESH_ID)
                cp.start()
                copies.append(cp)
        for cp in copies:
            cp.wait()

    return pl.pallas_call(
        body, name=name, out_shape=[jax.ShapeDtypeStruct(a.shape, a.dtype) for a in lands],
        in_specs=[ANY] * n, out_specs=[ANY] * n, input_output_aliases={i: i for i in range(n)},
        scratch_shapes=[pltpu.SemaphoreType.DMA((3 * n,)), pltpu.SemaphoreType.DMA((3 * n,)),
                        pltpu.SemaphoreType.DMA((1,))],
    )(*lands)


def _pair_copies(g_refs, land_refs, send_sems, recv_sems):
    x, y, c = _mesh_pos()
    copies = []
    for i, (g_ref, land_ref) in enumerate(zip(g_refs, land_refs)):
        nl, ns = g_ref.shape[:2]
        copies.append(pltpu.make_async_remote_copy(
            src_ref=g_ref.at[pl.ds(0, nl), pl.ds(0, ns), 1 - c], dst_ref=land_ref, send_sem=send_sems.at[i],
            recv_sem=recv_sems.at[i], device_id=(x, y, 1 - c), device_id_type=MESH_ID))
    return copies


def _pair_exchange(gs, *, name):
    n = len(gs)

    def body(*refs):
        send_sems, recv_sems, _ = refs[2 * n:]
        copies = _pair_copies(refs[:n], refs[n:2 * n], send_sems, recv_sems)
        for cp in copies:
            cp.start()
        for cp in copies:
            cp.wait()

    out_shapes = [jax.ShapeDtypeStruct(g.shape[:2] + g.shape[3:], g.dtype) for g in gs]
    return _comm_call(body, gs, out_shapes, n, name=name)


def _chip_copies(p_refs, land_refs, send_sems, recv_sems):
    x, y, c = _mesh_pos()
    k_me = 2 * x + y
    chips = [(1 - x, y), (x, 1 - y), (1 - x, 1 - y)]
    copies = []
    for i, (p_ref, land_ref) in enumerate(zip(p_refs, land_refs)):
        nl = p_ref.shape[0]
        for j, (cx, cy) in enumerate(chips):
            copies.append(pltpu.make_async_remote_copy(
                src_ref=p_ref.at[pl.ds(0, nl), 2 * cx + cy], dst_ref=land_ref.at[k_me],
                send_sem=send_sems.at[3 * i + j], recv_sem=recv_sems.at[3 * i + j],
                device_id=(cx, cy, c), device_id_type=MESH_ID))
    return copies


def _chip_landing(ps):
    return [lax.empty((p.shape[1], p.shape[0]) + p.shape[2:], p.dtype) for p in ps]


def _pair_swap(ss, *, name):
    n = len(ss)

    def body(*refs):
        xs, outs = refs[:n], refs[n:2 * n]
        send_sems, recv_sems, _ = refs[2 * n:]
        x, y, c = _mesh_pos()
        copies = []
        for i in range(n):
            cp = pltpu.make_async_remote_copy(src_ref=xs[i], dst_ref=outs[i], send_sem=send_sems.at[i],
                                              recv_sem=recv_sems.at[i], device_id=(x, y, 1 - c),
                                              device_id_type=MESH_ID)
            cp.start()
            copies.append(cp)
        for cp in copies:
            cp.wait()

    out_shapes = [jax.ShapeDtypeStruct(s.shape, s.dtype) for s in ss]
    return _comm_call(body, ss, out_shapes, n, name=name)


def _row_tile(rows, cols):
    tr = rows
    while tr * cols > 256 * 1024 and tr % 16 == 0:
        tr //= 2
    return tr


def _pair_add(g, recv, c_idx, *, name):
    n, ns, _, rows, W = g.shape
    tr = _row_tile(rows, W)

    def body(c_ref, g_ref, r_ref, o_ref):
        o_ref[...] = (g_ref[...] + r_ref[...]).astype(BF16)

    piece = pl.BlockSpec((None, tr, W), lambda p, i, c_ref: (p, i, 0))
    grid_spec = pltpu.PrefetchScalarGridSpec(
        num_scalar_prefetch=1, grid=(n * ns, rows // tr),
        in_specs=[pl.BlockSpec((None, None, tr, W), lambda p, i, c_ref: (p, c_ref[0], i, 0)), piece],
        out_specs=piece)
    out = pl.pallas_call(
        body, name=name, grid_spec=grid_spec, out_shape=jax.ShapeDtypeStruct((n * ns, rows, W), BF16),
        compiler_params=_cparams(("parallel", "parallel")),
    )(c_idx, g.reshape(n * ns, 2, rows, W), recv.reshape(n * ns, rows, W))
    return out.reshape(n, ns, rows, W)


def _sum_pieces(land, own, chip_idx, *, name):
    n, nl, A, W = land.shape
    tr = _row_tile(A, W)

    def body(k_ref, l_ref, o_ref, out_ref):
        acc = jnp.zeros(out_ref.shape, F32)
        for k in range(n):
            acc = acc + jnp.where(k == k_ref[0], o_ref[...], l_ref[k]).astype(F32)
        out_ref[...] = acc

    grid_spec = pltpu.PrefetchScalarGridSpec(
        num_scalar_prefetch=1, grid=(nl, A // tr),
        in_specs=[pl.BlockSpec((n, None, tr, W), lambda l, i, k_ref: (0, l, i, 0)),
                  pl.BlockSpec((None, None, tr, W), lambda l, i, k_ref: (l, k_ref[0], i, 0))],
        out_specs=pl.BlockSpec((None, tr, W), lambda l, i, k_ref: (l, i, 0)))
    return pl.pallas_call(
        body, name=name, grid_spec=grid_spec, out_shape=jax.ShapeDtypeStruct((nl, A, W), F32),
        compiler_params=_cparams(("parallel", "parallel")),
    )(chip_idx, land, own)


SMALL = ("norm_mix_g", "norm_mlp_g", "final_norm_g", "fox_b_f", "mla_q_norm_g", "mla_kv_norm_g")
WEIGHT_ORDER = ("ada_w", "ada_b", "norm_mix_g", "norm_mlp_g", "fox_w_in", "fox_b_f", "fox_w_out", "mla_w_dq",
                "mla_q_norm_g", "mla_w_uq", "mla_w_dkv", "mla_kv_norm_g", "mla_w_ukv", "mla_w_out", "mlp_w1",
                "mlp_w2", "final_norm_g")


def _small_rows(vals, D):
    rows = [vals["norm_mix_g"], vals["norm_mlp_g"], vals["final_norm_g"][None, :]]
    for n in ("fox_b_f", "mla_q_norm_g", "mla_kv_norm_g"):
        flat = vals[n].reshape(-1)
        assert flat.shape[0] <= D
        rows.append(jnp.pad(flat, (0, D - flat.shape[0]))[None, :])
    return jnp.concatenate(rows, axis=0)


def _small_unrows(rows, shapes):
    L = shapes["norm_mix_g"][0]
    out = {"norm_mix_g": rows[0:L], "norm_mlp_g": rows[L:2 * L], "final_norm_g": rows[2 * L]}
    for k, n in enumerate(("fox_b_f", "mla_q_norm_g", "mla_kv_norm_g")):
        size = int(np.prod(shapes[n]))
        out[n] = rows[2 * L + 1 + k, :size].reshape(shapes[n])
    return out


def kernel(x, c, positions, ada_w, ada_b, norm_mix_g, norm_mlp_g, fox_w_in, fox_b_f, fox_w_out, mla_w_dq, mla_q_norm_g, mla_w_uq, mla_w_dkv, mla_kv_norm_g, mla_w_ukv, mla_w_out, mlp_w1, mlp_w2, final_norm_g, loss_target, m_ada_w, m_ada_b, m_norm_mix_g, m_norm_mlp_g, m_fox_w_in, m_fox_b_f, m_fox_w_out, m_mla_w_dq, m_mla_q_norm_g, m_mla_w_uq, m_mla_w_dkv, m_mla_kv_norm_g, m_mla_w_ukv, m_mla_w_out, m_mlp_w1, m_mlp_w2, m_final_norm_g, v_ada_w, v_ada_b, v_norm_mix_g, v_norm_mlp_g, v_fox_w_in, v_fox_b_f, v_fox_w_out, v_mla_w_dq, v_mla_q_norm_g, v_mla_w_uq, v_mla_w_dkv, v_mla_kv_norm_g, v_mla_w_ukv, v_mla_w_out, v_mlp_w1, v_mlp_w2, v_final_norm_g):
    args = dict(locals())
    wts = {n: args[n] for n in WEIGHT_ORDER}
    mom = {n: args["m_" + n] for n in WEIGHT_ORDER}
    var = {n: args["v_" + n] for n in WEIGHT_ORDER}
    Bl, S, D = x.shape
    T = Bl * S
    L = ada_w.shape[0]
    C = ada_w.shape[2]
    mx, my, mc = _mesh_pos()
    chip = 2 * mx + my
    dev = 4 * mx + 2 * my + mc
    c_idx = jnp.reshape(mc, (1,)).astype(jnp.int32)
    chip_idx = jnp.reshape(chip, (1,)).astype(jnp.int32)
    small = {n: wts[n] for n in SMALL}
    L2, q_cols = mla_q_norm_g.shape
    n_fox_heads = fox_b_f.shape[1]

    shards = _shard_layouts(wts)
    groups = _comm_groups(L, L2)
    slots = _layer_slots(groups)

    def row_halves(a):
        return a.reshape(a.shape[:-2] + (2, a.shape[-2] // 2, a.shape[-1]))

    def whole_rows(a):
        return a.reshape(a.shape[:2] + (a.shape[2] * a.shape[3], a.shape[4]))

    part = {g: [row_halves(shards[n][s:s + cnt]) for n, s, cnt in entries] for g, entries in groups.items()}
    mix0 = _gather_weights(part["mix0"], name="gather_mix0")
    gather_sems, after = {}, mix0[0]
    for group in ("mlp0", "rest"):
        placed = [_place_own(a, chip_idx, c_idx, name=f"gather_place_{group}_{n}")
                  for a, (n, _, _) in zip(part[group], groups[group])]
        gather_sems[group] = _split_start(_gather_copies, part[group], placed, after, name=f"gather_{group}_start",
                                          sems_per_array=4)
        after = gather_sems[group][4]

    def layer_weights(w, group, arrays):
        for (n, s, cnt), a in zip(groups[group], arrays):
            for key, view in _weight_views(n, whole_rows(a), D, n_fox_heads).items():
                for l in range(cnt):
                    w[key][s + l] = (view, l)

    w = {key: [None] * L2
         for key in ("fox_qkv", "fox_f", "fox_in", "fox_out", "mla_down", "mla_uq", "mla_ukv", "mla_out")}
    w.update({key: [None] * L for key in ("mlp_w1", "mlp_w2")})
    layer_weights(w, "mix0", mix0)

    def gathered_now(group):
        def hook(x_now, w):
            _, landed = _split_wait(_gather_copies, *gather_sems[group][:4], x_now, name=f"gather_{group}_wait")
            layer_weights(w, group, _gather_forward(landed, name=f"gather_{group}_forward"))
            return w
        return hook

    c_pad = jnp.concatenate([c, jnp.pad(mla_q_norm_g, ((0, 8 - Bl - L2), (0, D - q_cols)))], axis=0)
    c8 = _all_gather8(c_pad, name="gather_c", in_vmem=True)
    c_all = c8[:, :Bl].reshape(N_DEV * Bl, D)
    qg4 = c8.reshape(N_CHIP, 2, 8, D)[:, 0, Bl:Bl + L2, :q_cols]
    small["mla_q_norm_g"] = jnp.transpose(qg4, (1, 0, 2)).reshape(L2, N_CHIP * q_cols)
    ada_b_cols = lax.dynamic_slice_in_dim(ada_b, chip * C, C, axis=1)[:, None, :]
    mod_cols = _ada_fwd(c_all, ada_w, ada_b_cols)
    mod8 = _all_gather8(mod_cols.reshape(L * N_DEV * Bl, C), name="gather_mod", in_vmem=True)
    mod4 = mod8.reshape(N_CHIP, 2, L, N_DEV * Bl, C)[:, 0]
    mod_me = lax.dynamic_slice_in_dim(mod4, dev * Bl, Bl, axis=2)
    mod = jnp.transpose(mod_me, (1, 2, 0, 3)).reshape(L, Bl, 6, D)
    mod = jnp.transpose(mod, (0, 2, 1, 3))[:, :, :, None, :]

    w.update(_small_layouts(small))
    mod = mod + after[0, 0]
    pending = {}

    def grad_pieces(group, g_now):
        out = []
        for n, s, cnt in groups[group]:
            qkv_f = [g_now["fox_in"][j] for j in range(s, s + cnt)] if n == "fox_in" else None
            stacked_g = None if n == "fox_in" else g_now[n][group]
            out.append(row_halves(_grad_pieces(n, stacked_g, qkv_f, n_fox_heads, N_CHIP)))
        return out

    def pair_added(group, big, sibling):
        return [_pair_add(a, r, c_idx, name=f"grad_pair_add_{group}_{n}")
                for (n, _, _), a, r in zip(groups[group], big, sibling)]

    def exchange_start(group, ps, after=None):
        pending[group] = _split_start(_chip_copies, ps, _chip_landing(ps), chip_idx if after is None else after,
                                      name=f"grad_exchange_{group}_start", sems_per_array=3)
        return pending[group][4]

    def bwd_layer0(g_now):
        big = grad_pieces("rest", g_now)
        landing = [lax.empty(a.shape[:2] + a.shape[3:], a.dtype) for a in big]
        pending["rest_pair"] = _split_start(_pair_copies, big, landing, chip_idx, name="grad_pair_rest_start",
                                            sems_per_array=1)
        return pending["rest_pair"][4]

    def bwd_mix0(g_now):
        send_sems, recv_sems, big, landed, _ = pending["rest_pair"]
        big, landed = _split_wait(_pair_copies, send_sems, recv_sems, big, landed, g_now["mlp_w1"]["mlp0"],
                                  name="grad_pair_rest_wait")
        started = exchange_start("rest", pair_added("rest", big, landed))
        big = grad_pieces("mlp0", g_now)
        return exchange_start("mlp0", pair_added("mlp0", big, _pair_exchange(big, name="grad_pair_exchange_mlp0")),
                              after=started)

    half = ROPE_DIM // 2
    inv_freq = ROPE_THETA ** (-jnp.arange(0, ROPE_DIM, 2, dtype=F32) / ROPE_DIM)
    lane = np.arange(LANES)
    inv_freq_row = jnp.tile(inv_freq, LANES // half)[None, :]
    sign_row = jnp.asarray(np.where(lane < 2 * ROPE_DIM, np.where(lane % ROPE_DIM < half, -1.0, 1.0), 0.0), F32)[None, :]
    pos_f = positions.astype(F32).reshape(T, 1)
    loss_row, grad_x, dmod, g = _local_step(x.reshape(T, D), loss_target.reshape(T, D), pos_f, inv_freq_row, sign_row,
                                            mod, w, slots, S=S,
                                            hooks={"fwd_mlp0": gathered_now("mlp0"), "fwd_layer1": gathered_now("rest"),
                                                   "bwd_layer0": bwd_layer0, "bwd_mix0": bwd_mix0})
    g_small = _small_grads(g, n_fox_heads)
    big = grad_pieces("mix0", g)
    exchange_start("mix0", pair_added("mix0", big, _pair_exchange(big, name="grad_pair_exchange_mix0")))

    Rs = -(-(2 * L + 5) // 8) * 8
    srows = jnp.concatenate([_small_rows(g_small, D), jnp.pad(loss_row, ((0, 0), (0, D - LANES)))], axis=0)
    srows = jnp.pad(srows, ((0, Rs - srows.shape[0]), (0, 0)))
    drows = jnp.transpose(dmod[:, :, :, 0, :], (2, 0, 1, 3)).reshape(Bl * L * 6, D)
    both8 = _all_gather8(jnp.concatenate([drows, srows], axis=0), name="gather_small", in_vmem=True)
    dm8 = both8[:, :Bl * L * 6].reshape(N_DEV, Bl, L * 6, D)
    sm8 = both8[:, Bl * L * 6:]
    adb_rows, small_sum = _sum_gathered(dm8, sm8)
    grad_ada_b = adb_rows.reshape(L, 6 * D)
    loss = small_sum[2 * L + 4, 0]
    small_shapes = {n: (wts[n].shape if n != "mla_q_norm_g" else (wts[n].shape[0], N_CHIP * q_cols)) for n in SMALL}
    gs = _small_unrows(small_sum, small_shapes)
    gs["mla_q_norm_g"] = lax.dynamic_slice_in_dim(gs["mla_q_norm_g"], chip * q_cols, q_cols, axis=1)

    dmod16 = jnp.transpose(dm8.reshape(N_DEV, Bl, L, 6 * D), (2, 0, 1, 3)).reshape(L, N_DEV * Bl, 6 * D)
    dmod_cols = lax.dynamic_slice_in_dim(dmod16, chip * C, C, axis=2)
    grad_ada_w = _ada_bwd(c_all, dmod_cols)

    grads = dict(gs)
    grads["ada_w"] = grad_ada_w
    grads["ada_b"] = grad_ada_b
    delta, new_m, new_v = {}, {}, {}
    for n in ("ada_w", "ada_b"):
        delta[n], new_m[n], new_v[n] = _adamw(wts[n], grads[n], mom[n], var[n], name=f"adamw_{n}")
    shard_small_shapes = {n: wts[n].shape for n in SMALL}
    packs = [jnp.pad(_small_rows({n: src[n] for n in SMALL}, D), ((0, Rs - 2 * L - 4), (0, 0)))
             for src in (wts, grads, mom, var)]
    for dst, rows in zip((delta, new_m, new_v), _adamw(*packs, name="adamw_small")):
        dst.update(_small_unrows(rows, shard_small_shapes))

    halves = {}
    for group, after in (("rest", grad_x), ("mlp0", grad_x), ("mix0", delta["ada_w"])):
        send_sems, recv_sems, ps, lands, _ = pending[group]
        ps, lands = _split_wait(_chip_copies, send_sems, recv_sems, ps, lands, after, name=f"grad_exchange_{group}_wait")
        sums = [_sum_pieces(ld, p, chip_idx, name=f"grad_sum_{group}_{n}")
                for (n, _, _), ld, p in zip(groups[group], lands, ps)]
        swapped = _pair_swap(sums, name=f"grad_pair_swap_{group}")
        for (n, _, _), a, b in zip(groups[group], sums, swapped):
            halves[(n, group)] = (a, b)

    def all_layers(n, which):
        return jnp.concatenate([halves[(n, grp)][which] for grp in groups if (n, grp) in halves], axis=0)

    own = {n: all_layers(n, 0) for n in GATHERED}
    peer = {n: all_layers(n, 1) for n in GATHERED}
    for nat, n in (("fox_w_in", "fox_in"), ("fox_w_out", "fox_out"), ("mla_w_out", "mla_out"), ("mlp_w1", "mlp_w1"),
                   ("mlp_w2", "mlp_w2")):
        cols = wts[nat].shape[-1]
        res = _adamw_halves(_pad_lanes(wts[nat]), own[n], peer[n], _pad_lanes(mom[nat]), _pad_lanes(var[nat]), c_idx,
                            name=f"adamw_{nat}")
        grads[nat], delta[nat], new_m[nat], new_v[nat] = (a[..., :cols] for a in res)
    joined = {n: jnp.concatenate([jnp.where(mc == 0, own[n], peer[n]), jnp.where(mc == 0, peer[n], own[n])], axis=1)
              for n in ("mla_down", "mla_uq", "mla_ukv")}
    rq = mla_w_dq.shape[-1]
    grads["mla_w_dq"] = joined["mla_down"][:, :, :rq]
    grads["mla_w_dkv"] = joined["mla_down"][:, :, rq:rq + KV_RANK + ROPE_DIM]
    grads["mla_w_uq"] = jax.vmap(_uq_from_pairs)(joined["mla_uq"])
    grads["mla_w_ukv"] = jax.vmap(_ukv_from_pairs)(joined["mla_ukv"])
    for n in ("mla_w_dq", "mla_w_dkv", "mla_w_uq", "mla_w_ukv"):
        delta[n], new_m[n], new_v[n] = _adamw(wts[n], grads[n], mom[n], var[n], name=f"adamw_{n}")

    return (loss, grad_x.reshape(Bl, S, D), *[grads[n] for n in WEIGHT_ORDER], *[delta[n] for n in WEIGHT_ORDER],
            *[new_m[n] for n in WEIGHT_ORDER], *[new_v[n] for n in WEIGHT_ORDER])
```

```python
import numpy as np
import jax
import jax.numpy as jnp
from jax import lax
from jax.experimental import pallas as pl
from jax.experimental.pallas import tpu as pltpu

F32 = jnp.float32
BF16 = jnp.bfloat16
MESH_ID = pl.DeviceIdType.MESH

NORM_EPS = 1e-6
ROPE_THETA = 10000.0
HEAD_DIM = 64
ROPE_DIM = 32
KV_RANK = 128
MLA_SCALE = (HEAD_DIM + ROPE_DIM) ** -0.5
FOX_EXTRA = 6
PAIR_Q = 256
PAIR_KV = 384
LANES = 128
ADAM_LR = 0.001
ADAM_B1 = 0.9
ADAM_B2 = 0.999
ADAM_EPS = 1e-08
ADAM_WD = 0.01
ADAM_STEP = 10
VMEM_LIMIT_V7X = 48 * 1024 * 1024
MM_VMEM_BUDGET = 36 * 1024 * 1024
MM_VMEM_HEADROOM = VMEM_LIMIT_V7X - MM_VMEM_BUDGET
NEG_BIG = -1e30
ATTN_UNROLL = 4
ATTN_BLOCK = 256
ATTN_Q_ROWS = 512
ATTN_K_ROWS = 512


def _cparams(sem=None, vmem_limit=VMEM_LIMIT_V7X):
    return pltpu.CompilerParams(dimension_semantics=sem, vmem_limit_bytes=vmem_limit)


def _tile(n, want):
    if n <= want:
        return n
    for t in range(want - want % LANES, 0, -LANES):
        if n % t == 0:
            return t
    raise ValueError((n, want))


def _mm(a, b, mode, *, name, out_dtypes=(F32,), epilogue=None, extras=(), rowvecs=(), tables=(),
        seq=None, a_off=0, a_sz=None, b_layer=None, out_stack=None, out_split=0, out_t=(), full_rows=False,
        vmem_budget=MM_VMEM_BUDGET, tm=1024, tn=1024, tk=2048):
    if isinstance(b, (list, tuple)):
        b, b_layer = b[b_layer]
    b_rows, b_cols = b.shape[-2], b.shape[-1]
    n_split = b.shape[1] if b.ndim == 4 else 1
    assert mode in ("nn", "nt")
    if mode == "nn":
        M, K, N = a.shape[0], b_rows, b_cols * n_split
    else:
        M, K, N = a.shape[0], b_cols * n_split, b_rows
    assert a_sz is None or a_sz == K
    tm = _tile(seq if rowvecs else M, tm)
    n_piece = N // max(out_split, n_split if mode == "nn" else 1, 1)
    tn = _tile(n_piece, tn)
    tk = _tile(K // (n_split if mode == "nt" else 1), tk)
    ne, nr, nt_ = len(extras), len(rowvecs), len(tables)
    no = len(out_dtypes)

    def vmem_estimate():
        blocks = tm * tk * a.dtype.itemsize + tk * tn * b.dtype.itemsize
        blocks += tm * tn * (sum(e.dtype.itemsize for e in extras) + sum(jnp.dtype(d).itemsize for d in out_dtypes))
        return 2 * blocks + 2 * tm * tn * 4

    if full_rows:
        assert tn == N
    while vmem_estimate() > vmem_budget and max(tm, tn) > 256:
        if tn >= tm and not full_rows:
            tn //= 2
        else:
            tm //= 2
    nk = K // tk

    assert a_off % tk == 0
    a_spec = pl.BlockSpec((tm, tk), lambda i, j, k: (i, k + a_off // tk))
    dims = (((1,), (0,)), ((), ())) if mode == "nn" else (((1,), (1,)), ((), ()))
    lead = () if b.ndim == 2 else (b_layer,)
    sq = (None,) * (b.ndim - 2)
    if mode == "nt":
        kb = b_cols // tk
        if b.ndim == 4:
            b_spec = pl.BlockSpec(sq + (tn, tk), lambda i, j, k: lead + (k // kb, j, k % kb))
        else:
            b_spec = pl.BlockSpec(sq + (tn, tk), lambda i, j, k: lead + (j, k))
    else:
        nb = b_cols // tn
        if b.ndim == 4:
            b_spec = pl.BlockSpec(sq + (tk, tn), lambda i, j, k: lead + (j // nb, k, j % nb))
        else:
            b_spec = pl.BlockSpec(sq + (tk, tn), lambda i, j, k: lead + (k, j))
    in_specs = [a_spec, b_spec]
    in_specs += [pl.BlockSpec((tm, tn), lambda i, j, k: (i, j)) for _ in extras]
    if rowvecs:
        assert seq % tm == 0
        per = seq // tm
        in_specs += [pl.BlockSpec((None, 1, tn), lambda i, j, k: (i // per, 0, j)) for _ in rowvecs]
    in_specs += [pl.BlockSpec((tm, LANES), lambda i, j, k: (i, 0)) for _ in tables]
    operands = [a, b, *extras, *rowvecs, *tables]
    aliases = {}
    transposed = tuple(out_t) + (False,) * (no - len(out_t))
    if out_stack is None:
        out_specs = [pl.BlockSpec((tn, tm), lambda i, j, k: (j, i)) if t else pl.BlockSpec((tm, tn), lambda i, j, k: (i, j))
                     for t in transposed]
        out_shape = [jax.ShapeDtypeStruct((N, M) if t else (M, N), d) for d, t in zip(out_dtypes, transposed)]
    else:
        prev, layer, n_layers = out_stack
        assert no == 1
        if out_split:
            ob = n_piece // tn
            out_specs = [pl.BlockSpec((None, None, tm, tn), lambda i, j, k: (layer, j // ob, i, j % ob))]
            out_shape = [jax.ShapeDtypeStruct((n_layers, out_split, M, n_piece), out_dtypes[0])]
        else:
            out_specs = [pl.BlockSpec((None, tm, tn), lambda i, j, k: (layer, i, j))]
            out_shape = [jax.ShapeDtypeStruct((n_layers, M, N), out_dtypes[0])]
        if prev is not None:
            in_specs.append(pl.BlockSpec(memory_space=pl.ANY))
            aliases = {len(operands): 0}
            operands.append(prev)
    n_in = len(operands)

    def body(*refs):
        a_ref, b_ref = refs[0], refs[1]
        side = refs[2:2 + ne + nr + nt_]
        outs = refs[n_in:n_in + no]

        def finish(acc):
            res = (acc,) if epilogue is None else epilogue(acc, *[r[...] for r in side])
            for o_ref, r, t in zip(outs, res, transposed):
                o_ref[...] = (r.T if t else r).astype(o_ref.dtype)

        part = lax.dot_general(a_ref[...].astype(BF16), b_ref[...].astype(BF16), dims,
                               preferred_element_type=F32)
        if nk == 1:
            finish(part)
        else:
            acc_ref = refs[-1]
            k = pl.program_id(2)

            @pl.when(k == 0)
            def _():
                acc_ref[...] = part

            @pl.when(k > 0)
            def _():
                acc_ref[...] += part

            @pl.when(k == nk - 1)
            def _():
                finish(acc_ref[...])

    res = pl.pallas_call(
        body, name=name, grid=(M // tm, N // tn, nk), in_specs=in_specs, out_specs=out_specs,
        out_shape=out_shape, scratch_shapes=[pltpu.VMEM((tm, tn), F32)] if nk > 1 else [],
        input_output_aliases=aliases,
        compiler_params=_cparams(("parallel", "parallel", "arbitrary"), vmem_limit=vmem_budget + MM_VMEM_HEADROOM),
    )(*operands)
    return res[0] if no == 1 else tuple(res)


def _rope128(x, cos_t, sin_s):
    lane = lax.broadcasted_iota(jnp.int32, x.shape, 1)
    first = (lane % ROPE_DIM) < (ROPE_DIM // 2)
    swapped = jnp.where(first, pltpu.roll(x, LANES - ROPE_DIM // 2, 1), pltpu.roll(x, ROPE_DIM // 2, 1))
    return x * cos_t + swapped * sin_s


def _rope_pairs(acc, cos_t, sin_s, sign):
    parts = []
    for p in range(acc.shape[1] // PAIR_Q):
        parts.append(acc[:, p * PAIR_Q:p * PAIR_Q + LANES])
        parts.append(_rope128(acc[:, p * PAIR_Q + LANES:(p + 1) * PAIR_Q], cos_t, sign * sin_s))
    return jnp.concatenate(parts, axis=1)


def _rope_tables(pos_f, inv_freq_row, sign_row):
    T = pos_f.shape[0]
    tt = _tile(T, 512)

    def body(p_ref, f_ref, s_ref, cos_ref, sin_ref):
        ang = p_ref[...] * f_ref[...]
        cos_ref[...] = jnp.cos(ang)
        sin_ref[...] = jnp.sin(ang) * s_ref[...]

    return pl.pallas_call(
        body, name="rope_tables", grid=(T // tt,),
        in_specs=[pl.BlockSpec((tt, 1), lambda i: (i, 0)), pl.BlockSpec((1, LANES), lambda i: (0, 0)),
                  pl.BlockSpec((1, LANES), lambda i: (0, 0))],
        out_specs=[pl.BlockSpec((tt, LANES), lambda i: (i, 0))] * 2,
        out_shape=[jax.ShapeDtypeStruct((T, LANES), F32)] * 2,
        compiler_params=_cparams(("parallel",)),
    )(pos_f, inv_freq_row, sign_row)


def _unrope(dqx, cos_t, sin_s):
    T, W = dqx.shape
    tt = _tile(T, 512)

    def body(d_ref, c_ref, s_ref, o_ref):
        o_ref[...] = _rope_pairs(d_ref[...].astype(F32) * MLA_SCALE, c_ref[...], s_ref[...], -1.0).astype(BF16)

    return pl.pallas_call(
        body, name="mla_unrope", grid=(T // tt,),
        in_specs=[pl.BlockSpec((tt, W), lambda i: (i, 0)), pl.BlockSpec((tt, LANES), lambda i: (i, 0)),
                  pl.BlockSpec((tt, LANES), lambda i: (i, 0))],
        out_specs=pl.BlockSpec((tt, W), lambda i: (i, 0)),
        out_shape=jax.ShapeDtypeStruct((T, W), BF16),
        compiler_params=_cparams(("parallel",)),
    )(dqx, cos_t, sin_s)


def _row_specs(tt, D, per, n):
    return [pl.BlockSpec((None, 1, D), lambda i: (i // per, 0, 0)) for _ in range(n)]


def _norm_mod(x, gain, sc, sh, *, S, name):
    T, D = x.shape
    tt = _tile(S, 512)
    per = S // tt

    def body(x_ref, g_ref, sc_ref, sh_ref, h_ref, ht_ref):
        xv = x_ref[...]
        r = lax.rsqrt(jnp.mean(xv * xv, axis=-1, keepdims=True) + NORM_EPS)
        h = (xv * r) * g_ref[...] * (1.0 + sc_ref[...]) + sh_ref[...]
        h_ref[...] = h.astype(BF16)
        ht_ref[...] = h.T.astype(BF16)

    return pl.pallas_call(
        body, name=name, grid=(T // tt,),
        in_specs=[pl.BlockSpec((tt, D), lambda i: (i, 0)), pl.BlockSpec((1, D), lambda i: (0, 0))]
        + _row_specs(tt, D, per, 2),
        out_specs=[pl.BlockSpec((tt, D), lambda i: (i, 0)), pl.BlockSpec((D, tt), lambda i: (0, i))],
        out_shape=[jax.ShapeDtypeStruct((T, D), BF16), jax.ShapeDtypeStruct((D, T), BF16)],
        compiler_params=_cparams(("parallel",)),
    )(x, gain, sc, sh)


def _norm_mod_bwd(x, dh, dres, gain, sc, gate=None, *, S, name):
    T, D = x.shape
    B = T // S
    tt = _tile(S, 512)
    per = S // tt
    n_gate = 0 if gate is None else 2

    def body(*refs):
        x_ref, dh_ref, dres_ref, g_ref, sc_ref = refs[:5]
        dx_ref, dsh_ref, dsc_ref, dg_ref = refs[5 + n_gate:9 + n_gate]
        i = pl.program_id(0)
        xv = x_ref[...]
        dhv = dh_ref[...].astype(F32)
        r = lax.rsqrt(jnp.mean(xv * xv, axis=-1, keepdims=True) + NORM_EPS)
        n = xv * r
        g = g_ref[...]
        one_sc = 1.0 + sc_ref[...]
        dn = dhv * (g * one_sc)
        dxv = dres_ref[...] + r * (dn - n * jnp.mean(dn * n, axis=-1, keepdims=True))
        dx_ref[...] = dxv
        dhn = dhv * n

        @pl.when(i % per == 0)
        def _():
            dsh_ref[...] = jnp.zeros_like(dsh_ref)
            dsc_ref[...] = jnp.zeros_like(dsc_ref)

        @pl.when(i == 0)
        def _():
            dg_ref[...] = jnp.zeros_like(dg_ref)

        dsh_ref[...] += jnp.sum(dhv, axis=0, keepdims=True)
        dsc_ref[...] += jnp.sum(dhn, axis=0, keepdims=True) * g
        dg_ref[...] += jnp.sum(dhn, axis=0, keepdims=True) * one_sc
        if gate is not None:
            y_ref, gate_ref = refs[5:7]
            dy_ref, dgate_ref = refs[9 + n_gate:]
            dy_ref[...] = (dxv * gate_ref[...]).astype(BF16)

            @pl.when(i % per == 0)
            def _():
                dgate_ref[...] = jnp.zeros_like(dgate_ref)

            dgate_ref[...] += jnp.sum(dxv * y_ref[...], axis=0, keepdims=True)

    tile = pl.BlockSpec((tt, D), lambda i: (i, 0))
    in_specs = [tile] * 3 + [pl.BlockSpec((1, D), lambda i: (0, 0))] + _row_specs(tt, D, per, 1)
    out_specs = [tile] + _row_specs(tt, D, per, 2) + [pl.BlockSpec((1, D), lambda i: (0, 0))]
    out_shape = [jax.ShapeDtypeStruct((T, D), F32), jax.ShapeDtypeStruct((B, 1, D), F32),
                 jax.ShapeDtypeStruct((B, 1, D), F32), jax.ShapeDtypeStruct((1, D), F32)]
    operands = [x, dh, dres, gain, sc]
    if gate is not None:
        in_specs += [tile] + _row_specs(tt, D, per, 1)
        out_specs += [tile] + _row_specs(tt, D, per, 1)
        out_shape += [jax.ShapeDtypeStruct((T, D), BF16), jax.ShapeDtypeStruct((B, 1, D), F32)]
        operands += list(gate)
    return pl.pallas_call(
        body, name=name, grid=(T // tt,), in_specs=in_specs, out_specs=out_specs, out_shape=out_shape,
        compiler_params=_cparams(("arbitrary",)),
    )(*operands)


def _gate_bwd(dx, y, g, *, S, name):
    T, D = dx.shape
    B = T // S
    tt = _tile(S, 512)
    per = S // tt

    def body(dx_ref, y_ref, g_ref, dy_ref, dg_ref):
        i = pl.program_id(0)
        dxv = dx_ref[...]
        dy_ref[...] = (dxv * g_ref[...]).astype(BF16)

        @pl.when(i % per == 0)
        def _():
            dg_ref[...] = jnp.zeros_like(dg_ref)

        dg_ref[...] += jnp.sum(dxv * y_ref[...], axis=0, keepdims=True)

    return pl.pallas_call(
        body, name=name, grid=(T // tt,),
        in_specs=[pl.BlockSpec((tt, D), lambda i: (i, 0))] * 2 + _row_specs(tt, D, per, 1),
        out_specs=[pl.BlockSpec((tt, D), lambda i: (i, 0))] + _row_specs(tt, D, per, 1),
        out_shape=[jax.ShapeDtypeStruct((T, D), BF16), jax.ShapeDtypeStruct((B, 1, D), F32)],
        compiler_params=_cparams(("arbitrary",)),
    )(dx, y, g)


def _final_loss(x, target, gain):
    T, D = x.shape
    tt = _tile(T, 512)

    def body(x_ref, t_ref, g_ref, dx_ref, dg_ref, loss_ref):
        i = pl.program_id(0)
        xv = x_ref[...]
        r = lax.rsqrt(jnp.mean(xv * xv, axis=-1, keepdims=True) + NORM_EPS)
        n = xv * r
        g = g_ref[...]
        err = n * g - t_ref[...]
        dy = err * (1.0 / D)
        dn = dy * g
        dx_ref[...] = r * (dn - n * jnp.mean(dn * n, axis=-1, keepdims=True))

        @pl.when(i == 0)
        def _():
            dg_ref[...] = jnp.zeros_like(dg_ref)
            loss_ref[...] = jnp.zeros_like(loss_ref)

        dg_ref[...] += jnp.sum(dy * n, axis=0, keepdims=True)
        loss_ref[...] += jnp.sum(jnp.sum(err * err, axis=-1, keepdims=True), axis=0, keepdims=True) * (0.5 / D)

    return pl.pallas_call(
        body, name="final_loss", grid=(T // tt,),
        in_specs=[pl.BlockSpec((tt, D), lambda i: (i, 0))] * 2 + [pl.BlockSpec((1, D), lambda i: (0, 0))],
        out_specs=[pl.BlockSpec((tt, D), lambda i: (i, 0)), pl.BlockSpec((1, D), lambda i: (0, 0)),
                   pl.BlockSpec((1, LANES), lambda i: (0, 0))],
        out_shape=[jax.ShapeDtypeStruct((T, D), F32), jax.ShapeDtypeStruct((1, D), F32),
                   jax.ShapeDtypeStruct((1, LANES), F32)],
        compiler_params=_cparams(("arbitrary",)),
    )(x, target, gain)


def _head_masks(ew):
    lane = lax.broadcasted_iota(jnp.int32, (1, PAIR_Q), 1)
    m0 = (lane < HEAD_DIM) | ((lane >= LANES) & (lane < LANES + ew))
    m1 = ((lane >= HEAD_DIM) & (lane < LANES)) | ((lane >= LANES + ew) & (lane < LANES + 2 * ew))
    return m0, m1


def _dot_nt(a, b):
    return lax.dot_general(a, b, (((1,), (1,)), ((), ())), preferred_element_type=F32)


def _dot_tn(a, b):
    return lax.dot_general(a, b, (((0,), (0,)), ((), ())), preferred_element_type=F32)


def _lane_halves(x, op):
    acc = x[:, 0:LANES]
    for g in range(1, x.shape[1] // LANES):
        acc = op(acc, x[:, g * LANES:(g + 1) * LANES])
    return acc


def _head_rows(cols_lane_replicated):
    t = cols_lane_replicated.T
    sub = lax.broadcasted_iota(jnp.int32, (8, t.shape[1]), 0)
    return jnp.where(sub == 1, t[HEAD_DIM:HEAD_DIM + 8], t[0:8])


def _attn_fwd(qx, kvx, *, S, ew, name):
    T = qx.shape[0]
    P = qx.shape[1] // PAIR_Q
    B = T // S
    tk = _tile(S, ATTN_BLOCK)
    tq = _tile(S, ATTN_Q_ROWS)
    nq = S // tq
    per = tq // tk

    def body(q_ref, kv_ref, o_ref, lse_ref, ot_ref, m_sc, l_sc, acc_sc):
        qi = pl.program_id(2)
        q = q_ref[...]
        masks = _head_masks(ew)
        qh = [jnp.where(m, q, jnp.zeros_like(q)) for m in masks]

        def logits(h, k, diagonal):
            s = _dot_nt(qh[h], k)
            if diagonal is None:
                return s
            row = lax.broadcasted_iota(jnp.int32, s.shape, 0)
            col = lax.broadcasted_iota(jnp.int32, s.shape, 1)
            return jnp.where(col + diagonal * tk <= row, s, NEG_BIG)

        def trip(first, count, n_diagonal=0):
            rows = [pl.ds(pl.multiple_of((first + u) * tk, tk), tk) for u in range(count)]
            diag = [None] * (count - n_diagonal) + list(range(n_diagonal))
            for h in range(2):
                ss = [logits(h, kv_ref[rows[u], 0:PAIR_Q], diag[u]) for u in range(count)]
                m_prev = m_sc[h]
                m_elem = m_prev
                for s in ss:
                    m_elem = jnp.maximum(m_elem, _lane_halves(s, jnp.maximum))
                m_new = jnp.broadcast_to(jnp.max(m_elem, axis=1, keepdims=True), (tq, LANES))
                alpha = jnp.exp(m_prev - m_new)
                l = alpha * l_sc[h]
                acc = alpha * acc_sc[h]
                for u, s in enumerate(ss):
                    p = jnp.concatenate([jnp.exp(s[:, g * LANES:(g + 1) * LANES] - m_new)
                                         for g in range(tk // LANES)], axis=1)
                    l = l + _lane_halves(p, jnp.add)
                    acc = acc + jnp.dot(p.astype(BF16), kv_ref[rows[u], PAIR_Q:PAIR_KV], preferred_element_type=F32)
                m_sc[h] = m_new
                l_sc[h] = l
                acc_sc[h] = acc

        m_sc[...] = jnp.full(m_sc.shape, NEG_BIG, F32)
        l_sc[...] = jnp.zeros_like(l_sc)
        acc_sc[...] = jnp.zeros_like(acc_sc)

        def loop_body(t, carry):
            trip(t * ATTN_UNROLL, ATTN_UNROLL)
            return carry

        below = qi * per
        lax.fori_loop(0, below // ATTN_UNROLL, loop_body, 0)
        for left in range(0, ATTN_UNROLL, per):
            @pl.when(below % ATTN_UNROLL == left)
            def _(left=left):
                trip(below - left, left + per, n_diagonal=per)

        lane = lax.broadcasted_iota(jnp.int32, (tq, LANES), 1)
        lo = lane < HEAD_DIM
        l = [jnp.sum(l_sc[h], axis=1, keepdims=True) for h in range(2)]
        o = jnp.where(lo, acc_sc[0] / l[0], acc_sc[1] / l[1])
        o_ref[...] = o.astype(BF16)
        ot_ref[...] = o.T.astype(BF16)
        lse = jnp.where(lo, m_sc[0] + jnp.log(l[0]), m_sc[1] + jnp.log(l[1]))
        for r in range(per):
            lse_ref[r] = _head_rows(lse[r * tk:(r + 1) * tk])

    return pl.pallas_call(
        body, name=name, grid=(B, P, nq),
        in_specs=[pl.BlockSpec((tq, PAIR_Q), lambda b, p, i: (b * nq + i, p)),
                  pl.BlockSpec((S, PAIR_KV), lambda b, p, i: (b, p))],
        out_specs=[pl.BlockSpec((tq, LANES), lambda b, p, i: (b * nq + i, p)),
                   pl.BlockSpec((per, None, 8, tk), lambda b, p, i: (b * nq + i, p, 0, 0)),
                   pl.BlockSpec((LANES, tq), lambda b, p, i: (p, b * nq + i))],
        out_shape=[jax.ShapeDtypeStruct((T, P * LANES), BF16), jax.ShapeDtypeStruct((T // tk, P, 8, tk), F32),
                   jax.ShapeDtypeStruct((P * LANES, T), BF16)],
        scratch_shapes=[pltpu.VMEM((2, tq, LANES), F32)] * 3,
        compiler_params=_cparams(("parallel", "parallel", "arbitrary")),
    )(qx, kvx)


def _attn_bwd(qx, kvx, o, lse, do, *, S, ew, name, bias_grad=False):
    T = qx.shape[0]
    P = qx.shape[1] // PAIR_Q
    B = T // S
    tq = _tile(S, ATTN_BLOCK)
    tk = _tile(S, ATTN_K_ROWS)
    nq = S // tq
    nk = S // tk
    per = tk // tq

    def body(q_ref, kv_ref, o_ref, lse_ref, do_ref, dq_ref, dkv_ref, *rest):
        kj = pl.program_id(2)
        if bias_grad:
            csum_ref, rsum_ref, dq_sc, delta_sc, dk_sc, dv_sc, cs_sc = rest
            cs_sc[...] = jnp.zeros_like(cs_sc)

            @pl.when(kj == 0)
            def _():
                rsum_ref[...] = jnp.zeros_like(rsum_ref)
        else:
            dq_sc, delta_sc, dk_sc, dv_sc = rest
        masks = _head_masks(ew)
        lo_q = lax.broadcasted_iota(jnp.int32, (tq, LANES), 1) < HEAD_DIM
        lo = lax.broadcasted_iota(jnp.int32, (tk, LANES), 1) < HEAD_DIM
        vmask = [lo, jnp.logical_not(lo)]

        @pl.when(kj == 0)
        def _():
            dq_sc[...] = jnp.zeros_like(dq_sc)
            for c in range(nq):
                rows = pl.ds(c * tq, tq)
                x = do_ref[rows, :].astype(F32) * o_ref[rows, :].astype(F32)
                r0 = jnp.sum(jnp.where(lo_q, x, 0.0), axis=1, keepdims=True)
                r1 = jnp.sum(jnp.where(lo_q, 0.0, x), axis=1, keepdims=True)
                delta_sc[c] = _head_rows(jnp.where(lo_q, r0, r1))

        k = kv_ref[:, 0:PAIR_Q]
        v = kv_ref[:, PAIR_Q:PAIR_KV]
        kh = [jnp.where(m, k, jnp.zeros_like(k)) for m in masks]
        vh = [jnp.where(m, v, jnp.zeros_like(v)) for m in vmask]
        dk_sc[...] = jnp.zeros_like(dk_sc)
        dv_sc[...] = jnp.zeros_like(dv_sc)

        def step(qi, diagonal):
            rows = pl.ds(pl.multiple_of(qi * tq, tq), tq)
            q = q_ref[rows, :]
            dov = do_ref[rows, :]
            lse8 = lse_ref[qi]
            dl8 = delta_sc[qi]
            for h in range(2):
                st = _dot_nt(kh[h], q)
                if diagonal is not None:
                    key = lax.broadcasted_iota(jnp.int32, st.shape, 0)
                    qry = lax.broadcasted_iota(jnp.int32, st.shape, 1)
                    st = jnp.where(key <= qry + diagonal * tq, st, NEG_BIG)
                pt = jnp.exp(st - lse8[h:h + 1, :])
                dpt = _dot_nt(vh[h], dov)
                dst = pt * (dpt - dl8[h:h + 1, :])
                if bias_grad:
                    cs_sc[h] += _lane_halves(dst, jnp.add)
                    rsum_ref[qi, h:h + 1, :] += jnp.sum(dst, axis=0, keepdims=True)
                ptb = pt.astype(BF16)
                dstb = dst.astype(BF16)
                dv_sc[h] += jnp.dot(ptb, dov, preferred_element_type=F32)
                dk_sc[h] += jnp.dot(dstb, q, preferred_element_type=F32)
                dq_sc[rows, :] += _dot_tn(dstb, kh[h])

        first = kj * per
        above = nq - per - first
        for left in range(0, ATTN_UNROLL, per):
            @pl.when(above % ATTN_UNROLL == left)
            def _(left=left):
                for d in range(per):
                    step(first + d, d)
                for u in range(left):
                    step(first + per + u, None)

        def loop_body(t, carry):
            for u in range(ATTN_UNROLL):
                step(first + per + above % ATTN_UNROLL + t * ATTN_UNROLL + u, None)
            return carry

        lax.fori_loop(0, above // ATTN_UNROLL, loop_body, 0)
        dkv_ref[:, 0:PAIR_Q] = (jnp.where(masks[0], dk_sc[0], 0.0) + jnp.where(masks[1], dk_sc[1], 0.0)).astype(BF16)
        dkv_ref[:, PAIR_Q:PAIR_KV] = jnp.where(lo, dv_sc[0], dv_sc[1]).astype(BF16)
        if bias_grad:
            csum_ref[...] = jnp.where(lo, jnp.sum(cs_sc[0], axis=1, keepdims=True),
                                      jnp.sum(cs_sc[1], axis=1, keepdims=True))

        @pl.when(kj == nk - 1)
        def _():
            dq_ref[...] = dq_sc[...].astype(BF16)

    rows_spec = pl.BlockSpec((nq, None, 8, tq), lambda b, p, j: (b, p, 0, 0))
    out_specs = [pl.BlockSpec((S, PAIR_Q), lambda b, p, j: (b, p)),
                 pl.BlockSpec((tk, PAIR_KV), lambda b, p, j: (b * nk + j, p))]
    out_shape = [jax.ShapeDtypeStruct((T, P * PAIR_Q), BF16), jax.ShapeDtypeStruct((T, P * PAIR_KV), BF16)]
    scratch = [pltpu.VMEM((S, PAIR_Q), F32), pltpu.VMEM((nq, 8, tq), F32),
               pltpu.VMEM((2, tk, PAIR_Q), F32), pltpu.VMEM((2, tk, LANES), F32)]
    if bias_grad:
        out_specs += [pl.BlockSpec((tk, LANES), lambda b, p, j: (b * nk + j, p)), rows_spec]
        out_shape += [jax.ShapeDtypeStruct((T, P * LANES), F32), jax.ShapeDtypeStruct((T // tq, P, 8, tq), F32)]
        scratch.append(pltpu.VMEM((2, tk, LANES), F32))
    return pl.pallas_call(
        body, name=name, grid=(B, P, nk),
        in_specs=[pl.BlockSpec((S, PAIR_Q), lambda b, p, j: (b, p)),
                  pl.BlockSpec((tk, PAIR_KV), lambda b, p, j: (b * nk + j, p)),
                  pl.BlockSpec((S, LANES), lambda b, p, j: (b, p)), rows_spec,
                  pl.BlockSpec((S, LANES), lambda b, p, j: (b, p))],
        out_specs=out_specs, out_shape=out_shape, scratch_shapes=scratch,
        compiler_params=_cparams(("parallel", "parallel", "arbitrary")),
    )(qx, kvx, o, lse, do)


def _fox_consts(P):
    H = 2 * P
    eq = np.zeros((3 * LANES, P * LANES), np.float32)
    ek = np.zeros((3 * LANES, P * LANES), np.float32)
    ones_q = np.zeros((1, P * LANES), np.float32)
    ones_k = np.zeros((1, P * LANES), np.float32)
    for h in range(H):
        base = (h // 2) * LANES + FOX_EXTRA * (h % 2)
        for part in range(3):
            eq[part * LANES + h, base + part] = 1.0
            ones_q[0, base + 3 + part] = 1.0
            ones_k[0, base + part] = 1.0
            ek[part * LANES + h, base + 3 + part] = -1.0
    return eq, ek, ones_q, ones_k


def _split3(f):
    hi = f.astype(BF16)
    r = f - hi.astype(F32)
    mid = r.astype(BF16)
    lo = (r - mid.astype(F32)).astype(BF16)
    return hi, mid, lo


def _tri_sum(tri, x):
    hi, mid, lo = _split3(x)
    return (jnp.dot(tri, hi, preferred_element_type=F32) + jnp.dot(tri, mid, preferred_element_type=F32)
            + jnp.dot(tri, lo, preferred_element_type=F32))


def _log1p_pos(e):
    return jnp.where(e < 0.01, e * (1.0 - e * (0.5 - e * (1.0 / 3.0))), jnp.log(1.0 + e))


def _fox_prep(qkv, fl, b_row, *, S, D, name):
    T = qkv.shape[0]
    P = D // LANES
    B = T // S
    tt = _tile(S, 256)
    per = S // tt
    eq, ek, ones_q, ones_k = _fox_consts(P)
    q_scale = HEAD_DIM ** -0.5

    def body(q_ref, k_ref, v_ref, fl_ref, b_ref, eq_ref, ek_ref, oq_ref, ok_ref, qx_ref, kvx_ref, carry):
        i = pl.program_id(1)

        @pl.when(i == 0)
        def _():
            carry[...] = jnp.zeros_like(carry)

        z = fl_ref[...] + b_ref[...]
        logf = jnp.minimum(z, 0.0) - _log1p_pos(jnp.exp(-jnp.abs(z)))
        row = lax.broadcasted_iota(jnp.int32, (tt, tt), 0)
        col = lax.broadcasted_iota(jnp.int32, (tt, tt), 1)
        tri = (col <= row).astype(BF16)
        f = _tri_sum(tri, logf) + carry[...]
        carry[...] = f[tt - 1:tt, :]
        parts = jnp.concatenate(_split3(f), axis=1)
        xq = jnp.dot(parts, eq_ref[...], preferred_element_type=F32) + oq_ref[...]
        xk = jnp.dot(parts, ek_ref[...], preferred_element_type=F32) + ok_ref[...]
        for p in range(P):
            c = slice(p * LANES, (p + 1) * LANES)
            qx_ref[:, p * PAIR_Q:p * PAIR_Q + LANES] = (q_ref[:, c].astype(F32) * q_scale).astype(BF16)
            qx_ref[:, p * PAIR_Q + LANES:(p + 1) * PAIR_Q] = xq[:, c].astype(BF16)
            kvx_ref[:, p * PAIR_KV:p * PAIR_KV + LANES] = k_ref[:, c]
            kvx_ref[:, p * PAIR_KV + LANES:p * PAIR_KV + PAIR_Q] = xk[:, c].astype(BF16)
            kvx_ref[:, p * PAIR_KV + PAIR_Q:(p + 1) * PAIR_KV] = v_ref[:, c]

    tok = lambda b, i: (b * per + i, 0)
    const = lambda b, i: (0, 0)
    return pl.pallas_call(
        body, name=name, grid=(B, per),
        in_specs=[pl.BlockSpec((tt, D), lambda b, i: (b * per + i, 0)),
                  pl.BlockSpec((tt, D), lambda b, i: (b * per + i, 1)),
                  pl.BlockSpec((tt, D), lambda b, i: (b * per + i, 2)),
                  pl.BlockSpec((tt, LANES), tok), pl.BlockSpec((1, LANES), const),
                  pl.BlockSpec(eq.shape, const), pl.BlockSpec(ek.shape, const),
                  pl.BlockSpec(ones_q.shape, const), pl.BlockSpec(ones_k.shape, const)],
        out_specs=[pl.BlockSpec((tt, P * PAIR_Q), tok), pl.BlockSpec((tt, P * PAIR_KV), tok)],
        out_shape=[jax.ShapeDtypeStruct((T, P * PAIR_Q), BF16), jax.ShapeDtypeStruct((T, P * PAIR_KV), BF16)],
        scratch_shapes=[pltpu.VMEM((1, LANES), F32)],
        compiler_params=_cparams(("arbitrary", "arbitrary")),
    )(qkv, qkv, qkv, fl, b_row, jnp.asarray(eq, BF16), jnp.asarray(ek, BF16), jnp.asarray(ones_q), jnp.asarray(ones_k))


def _fox_unprep(dqx, dkvx, csum, rsum, fl, b_row, *, S, D, name):
    T = dqx.shape[0]
    P = D // LANES
    B = T // S
    tt = _tile(S, 256)
    per = S // tt
    q_scale = HEAD_DIM ** -0.5

    def body(dq_ref, dkv_ref, cs_ref, rs_ref, fl_ref, b_ref, dqkv_ref, db_ref, carry):
        b = pl.program_id(0)
        i = pl.program_id(1)

        @pl.when(i == 0)
        def _():
            carry[...] = jnp.zeros_like(carry)

        @pl.when((i == 0) & (b == 0))
        def _():
            db_ref[...] = jnp.zeros_like(db_ref)

        df = rs_ref[...] - cs_ref[...]
        for p in range(P):
            rq = slice(p * LANES, (p + 1) * LANES)
            dqkv_ref[:, rq] = (dq_ref[:, p * PAIR_Q:p * PAIR_Q + LANES].astype(F32) * q_scale).astype(BF16)
            dqkv_ref[:, D + p * LANES:D + (p + 1) * LANES] = dkv_ref[:, p * PAIR_KV:p * PAIR_KV + LANES]
            dqkv_ref[:, 2 * D + p * LANES:2 * D + (p + 1) * LANES] = dkv_ref[:, p * PAIR_KV + PAIR_Q:(p + 1) * PAIR_KV]
        row = lax.broadcasted_iota(jnp.int32, (tt, tt), 0)
        col = lax.broadcasted_iota(jnp.int32, (tt, tt), 1)
        tri = (col >= row).astype(BF16)
        dlogf = _tri_sum(tri, df) + carry[...]
        carry[...] = dlogf[0:1, :]
        z = fl_ref[...] + b_ref[...]
        e = jnp.exp(-jnp.abs(z))
        sig_neg = jnp.where(z >= 0.0, e, 1.0) / (1.0 + e)
        dfl = dlogf * sig_neg
        dqkv_ref[:, 3 * D:3 * D + LANES] = dfl.astype(BF16)
        db_ref[...] += jnp.sum(dfl, axis=0, keepdims=True)

    rev = lambda b, i: (b * per + per - 1 - i, 0)
    const = lambda b, i: (0, 0)
    return pl.pallas_call(
        body, name=name, grid=(B, per),
        in_specs=[pl.BlockSpec((tt, P * PAIR_Q), rev), pl.BlockSpec((tt, P * PAIR_KV), rev),
                  pl.BlockSpec((tt, LANES), rev), pl.BlockSpec((tt, LANES), rev), pl.BlockSpec((tt, LANES), rev),
                  pl.BlockSpec((1, LANES), const)],
        out_specs=[pl.BlockSpec((tt, 3 * D + LANES), rev), pl.BlockSpec((1, LANES), const)],
        out_shape=[jax.ShapeDtypeStruct((T, 3 * D + LANES), BF16), jax.ShapeDtypeStruct((1, LANES), F32)],
        scratch_shapes=[pltpu.VMEM((1, LANES), F32)],
        compiler_params=_cparams(("arbitrary", "arbitrary")),
    )(dqx, dkvx, csum, rsum, fl, b_row)


def _rms(x):
    r = lax.rsqrt(jnp.mean(x * x, axis=-1, keepdims=True) + NORM_EPS)
    return x * r, r


def _mla_mid(lat, gq, gkv, cos_t, sin_s, *, name):
    T, W = lat.shape
    Rq = W - 2 * LANES
    tt = _tile(T, 512)

    def body(l_ref, gq_ref, gkv_ref, c_ref, s_ref, o_ref, ot_ref):
        nq, _ = _rms(l_ref[:, 0:Rq])
        nkv, _ = _rms(l_ref[:, Rq:Rq + LANES])
        parts = [nq * gq_ref[...], nkv * gkv_ref[...], _rope128(l_ref[:, Rq + LANES:W], c_ref[...], s_ref[...])]
        out = jnp.concatenate(parts, axis=1)
        o_ref[...] = out.astype(BF16)
        ot_ref[...] = out.T.astype(BF16)

    return pl.pallas_call(
        body, name=name, grid=(T // tt,),
        in_specs=[pl.BlockSpec((tt, W), lambda i: (i, 0)), pl.BlockSpec((1, Rq), lambda i: (0, 0)),
                  pl.BlockSpec((1, LANES), lambda i: (0, 0)), pl.BlockSpec((tt, LANES), lambda i: (i, 0)),
                  pl.BlockSpec((tt, LANES), lambda i: (i, 0))],
        out_specs=[pl.BlockSpec((tt, W), lambda i: (i, 0)), pl.BlockSpec((W, tt), lambda i: (0, i))],
        out_shape=[jax.ShapeDtypeStruct((T, W), BF16), jax.ShapeDtypeStruct((W, T), BF16)],
        compiler_params=_cparams(("parallel",)),
    )(lat, gq, gkv, cos_t, sin_s)


def _mla_mid_bwd(lat, dcq, dckr, gq, gkv, cos_t, sin_s, *, name):
    T, W = lat.shape
    Rq = W - 2 * LANES
    tt = _tile(T, 512)

    def norm_bwd(x, dy, g):
        n, r = _rms(x)
        dn = dy * g
        return r * (dn - n * jnp.mean(dn * n, axis=-1, keepdims=True)), jnp.sum(dy * n, axis=0, keepdims=True)

    def body(l_ref, dq_ref, dk_ref, gq_ref, gkv_ref, c_ref, s_ref, o_ref, dgq_ref, dgkv_ref):
        i = pl.program_id(0)

        @pl.when(i == 0)
        def _():
            dgq_ref[...] = jnp.zeros_like(dgq_ref)
            dgkv_ref[...] = jnp.zeros_like(dgkv_ref)

        dxq, dgq = norm_bwd(l_ref[:, 0:Rq], dq_ref[...], gq_ref[...])
        dxkv, dgkv = norm_bwd(l_ref[:, Rq:Rq + LANES], dk_ref[:, 0:LANES], gkv_ref[...])
        o_ref[:, 0:Rq] = dxq.astype(BF16)
        o_ref[:, Rq:Rq + LANES] = dxkv.astype(BF16)
        o_ref[:, Rq + LANES:W] = _rope128(dk_ref[:, LANES:2 * LANES], c_ref[...], -s_ref[...]).astype(BF16)
        dgq_ref[...] += dgq
        dgkv_ref[...] += dgkv

    return pl.pallas_call(
        body, name=name, grid=(T // tt,),
        in_specs=[pl.BlockSpec((tt, W), lambda i: (i, 0)), pl.BlockSpec((tt, Rq), lambda i: (i, 0)),
                  pl.BlockSpec((tt, 2 * LANES), lambda i: (i, 0)), pl.BlockSpec((1, Rq), lambda i: (0, 0)),
                  pl.BlockSpec((1, LANES), lambda i: (0, 0)), pl.BlockSpec((tt, LANES), lambda i: (i, 0)),
                  pl.BlockSpec((tt, LANES), lambda i: (i, 0))],
        out_specs=[pl.BlockSpec((tt, W), lambda i: (i, 0)), pl.BlockSpec((1, Rq), lambda i: (0, 0)),
                   pl.BlockSpec((1, LANES), lambda i: (0, 0))],
        out_shape=[jax.ShapeDtypeStruct((T, W), BF16), jax.ShapeDtypeStruct((1, Rq), F32),
                   jax.ShapeDtypeStruct((1, LANES), F32)],
        compiler_params=_cparams(("arbitrary",)),
    )(lat, dcq, dckr, gq, gkv, cos_t, sin_s)


def _uq_to_pairs(w):
    Rq = w.shape[0]
    P = w.shape[1] // (2 * (HEAD_DIM + ROPE_DIM))
    w4 = w.reshape(Rq, P, 2, HEAD_DIM + ROPE_DIM)
    nope = w4[..., :HEAD_DIM].reshape(Rq, P, 2 * HEAD_DIM)
    rope = w4[..., HEAD_DIM:].reshape(Rq, P, 2 * ROPE_DIM)
    pad = jnp.zeros((Rq, P, PAIR_Q - 2 * HEAD_DIM - 2 * ROPE_DIM), w.dtype)
    return jnp.concatenate([nope, rope, pad], axis=-1).reshape(Rq, P * PAIR_Q)


def _uq_from_pairs(g):
    Rq = g.shape[0]
    P = g.shape[1] // PAIR_Q
    g3 = g.reshape(Rq, P, PAIR_Q)
    nope = g3[..., :2 * HEAD_DIM].reshape(Rq, P, 2, HEAD_DIM)
    rope = g3[..., 2 * HEAD_DIM:2 * HEAD_DIM + 2 * ROPE_DIM].reshape(Rq, P, 2, ROPE_DIM)
    return jnp.concatenate([nope, rope], axis=-1).reshape(Rq, P * 2 * (HEAD_DIM + ROPE_DIM))


def _ukv_to_pairs(w):
    P = w.shape[1] // (4 * HEAD_DIM)
    w4 = w.reshape(KV_RANK, P, 2, 2 * HEAD_DIM)
    kn = w4[..., :HEAD_DIM].reshape(KV_RANK, P, 2 * HEAD_DIM)
    vv = w4[..., HEAD_DIM:].reshape(KV_RANK, P, 2 * HEAD_DIM)
    top = jnp.concatenate([kn, jnp.zeros((KV_RANK, P, LANES), w.dtype), vv], axis=-1)
    place = np.zeros((LANES, P, PAIR_KV), np.float32)
    for r in range(ROPE_DIM):
        place[r, :, LANES + r] = 1.0
        place[r, :, LANES + ROPE_DIM + r] = 1.0
    return jnp.concatenate([top, jnp.asarray(place, w.dtype)], axis=0).reshape(KV_RANK + LANES, P * PAIR_KV)


def _ukv_from_pairs(g):
    P = g.shape[1] // PAIR_KV
    g3 = g[:KV_RANK].reshape(KV_RANK, P, PAIR_KV)
    kn = g3[..., :2 * HEAD_DIM].reshape(KV_RANK, P, 2, HEAD_DIM)
    vv = g3[..., PAIR_Q:].reshape(KV_RANK, P, 2, HEAD_DIM)
    return jnp.concatenate([kn, vv], axis=-1).reshape(KV_RANK, P * 4 * HEAD_DIM)


def _residual_then_norm(acc, xr, g, gain, sc, sh):
    x_out = xr + g * acc
    r = lax.rsqrt(jnp.mean(x_out * x_out, axis=-1, keepdims=True) + NORM_EPS)
    h = (x_out * r) * gain * (1.0 + sc) + sh
    return x_out, acc, h, h


def _gated_out(a, w_stack, layer, x, gate, next_norm, *, S, name):
    if next_norm is None:
        return _mm(a, w_stack, "nn", name=name, b_layer=layer, out_dtypes=(F32, BF16), extras=(x,), rowvecs=(gate,),
                   seq=S, epilogue=lambda acc, xr, g: (xr + g * acc, acc)) + (None, None)
    long_k = a.shape[1] > 2048
    return _mm(a, w_stack, "nn", name=name, b_layer=layer, out_dtypes=(F32, BF16, BF16, BF16),
               out_t=(False, False, False, True), extras=(x,), rowvecs=(gate,) + tuple(next_norm), seq=S,
               full_rows=True, tk=a.shape[1], epilogue=_residual_then_norm,
               vmem_budget=MM_VMEM_BUDGET + (8 * 1024 * 1024 if long_k else 0))


def _mlp_fwd(h2, w, i, x1, gate, next_norm, *, S):
    def act(acc):
        u = jnp.square(jnp.maximum(acc, 0.0))
        return acc, u, u

    p, u, u_t = _mm(h2, w["mlp_w1"], "nn", name=f"mlp_up_{i}", b_layer=i, out_dtypes=(BF16, BF16, BF16),
                    out_t=(False, False, True), epilogue=act)
    x2, z, h, h_t = _gated_out(u, w["mlp_w2"], i, x1, gate, next_norm, S=S, name=f"mlp_down_{i}")
    return x2, (p, u_t, z), h, h_t


STACKED_GRADS = ("fox_out", "mla_down", "mla_uq", "mla_ukv", "mla_out", "mlp_w1", "mlp_w2")


def _local_step(x, target, pos_f, inv_freq_row, sign_row, mod, w, slots, *, S, hooks=None):
    hooks = hooks or {}
    T, D = x.shape
    L = mod.shape[0]
    L2 = len(w["fox_out"])
    cos_t, sin_s = _rope_tables(pos_f, inv_freq_row, sign_row)
    saved = []
    B = mod.shape[2]

    def per_sequence(gain):
        return jnp.broadcast_to(gain[None], (B,) + gain.shape)

    h, h_t = _norm_mod(x, w["norm_mix_g"][0], mod[0, 1], mod[0, 0], S=S, name="norm_mix_0")
    for i in range(L):
        j = i // 2
        sh_m, sc_m, g_m, sh_f, sc_f, g_f = (mod[i, s] for s in range(6))
        if i % 2 == 0:
            qkv = _mm(h, w["fox_qkv"], "nn", name=f"fox_qkv_{i}", b_layer=j, out_dtypes=(BF16,))
            fl = _mm(h, w["fox_f"], "nn", name=f"fox_f_{i}", b_layer=j)
            qx, kvx = _fox_prep(qkv, fl, w["fox_b"][j], S=S, D=D, name=f"fox_prep_{i}")
            o, lse, o_t = _attn_fwd(qx, kvx, S=S, ew=FOX_EXTRA, name=f"fox_attn_{i}")
            mix = (qx, kvx, o, lse, o_t, fl)
            w_out = w["fox_out"]
        else:
            lat = _mm(h, w["mla_down"], "nn", name=f"mla_down_{i}", b_layer=j)
            Rq = lat.shape[1] - 2 * LANES
            cqr, cqr_t = _mla_mid(lat, w["mla_gq"][j], w["mla_gkv"][j], cos_t, sin_s, name=f"mla_mid_{i}")
            qx = _mm(cqr, w["mla_uq"], "nn", name=f"mla_uq_{i}", b_layer=j, out_dtypes=(BF16,), a_sz=Rq, tk=Rq,
                     tables=(cos_t, sin_s), epilogue=lambda acc, c, s: (_rope_pairs(acc * MLA_SCALE, c, s, 1.0),))
            kvx = _mm(cqr, w["mla_ukv"], "nn", name=f"mla_ukv_{i}", b_layer=j, out_dtypes=(BF16,), a_off=Rq,
                      a_sz=2 * LANES, tk=2 * LANES, tn=PAIR_KV)
            o, lse, o_t = _attn_fwd(qx, kvx, S=S, ew=ROPE_DIM, name=f"mla_attn_{i}")
            mix = (qx, kvx, o, lse, o_t, lat, cqr_t)
            w_out = w["mla_out"]
        x1, y, h2, h2_t = _gated_out(o, w_out, j, x, g_m, (per_sequence(w["norm_mlp_g"][i]), sc_f, sh_f), S=S,
                                     name=f"mix_out_{i}")
        if i == 0 and "fwd_mlp0" in hooks:
            w = hooks["fwd_mlp0"](x1, w)
        next_norm = (per_sequence(w["norm_mix_g"][i + 1]), mod[i + 1, 1], mod[i + 1, 0]) if i + 1 < L else None
        x2, mlp, h_next, h_next_t = _mlp_fwd(h2, w, i, x1, g_f, next_norm, S=S)
        saved.append((x, h_t, mix, y, x1, h2_t, mlp))
        x, h, h_t = x2, h_next, h_next_t
        if i == 0 and "fwd_layer1" in hooks:
            w = hooks["fwd_layer1"](x, w)

    dx, dg_final, loss = _final_loss(x, target, w["final_norm_g"])
    n_split = w["mlp_w1"][0][0].shape[1]

    grads = {k: [None] * len(w[k]) for k in ("norm_mix_g", "norm_mlp_g", "fox_b", "mla_gq", "mla_gkv")}
    grads["fox_in"] = [None] * L2
    grads.update({k: {} for k in STACKED_GRADS})
    grads["final_norm_g"] = dg_final

    def stacked(key, layer, _, a_t, b, **kw):
        group, idx, count = slots[(key, layer)]
        grads[key][group] = _mm(a_t, b, "nn", out_stack=(grads[key].get(group), idx, count), **kw)

    dmod = [None] * L
    for i in reversed(range(L)):
        j = i // 2
        x0, h_t, mix, y, x1, h2_t, (p, u_t, z) = saved[i]
        sh_m, sc_m, g_m, sh_f, sc_f, g_f = (mod[i, s] for s in range(6))
        if i == L - 1:
            dz, dg_f = _gate_bwd(dx, z, g_f, S=S, name=f"gate_mlp_bwd_{i}")
        else:
            dz, dg_f = dz_below, dg_f_below
        stacked("mlp_w2", i, L, u_t, dz, name=f"mlp_w2_grad_{i}")
        dp = _mm(dz, w["mlp_w2"], "nt", name=f"mlp_down_bwd_{i}", b_layer=i, out_dtypes=(BF16,), extras=(p,),
                 epilogue=lambda acc, pv: (acc * (2.0 * jnp.maximum(pv.astype(F32), 0.0)),))
        stacked("mlp_w1", i, L, h2_t, dp, name=f"mlp_w1_grad_{i}", out_split=n_split)
        if i == 0 and "bwd_mix0" in hooks:
            g_m = g_m + hooks["bwd_mix0"](grads)[0, 0]
        dh2 = _mm(dp, w["mlp_w1"], "nt", name=f"mlp_up_bwd_{i}", b_layer=i, out_dtypes=(BF16,))
        dx1, dsh_f, dsc_f, dgn, dy, dg_m = _norm_mod_bwd(x1, dh2, dx, w["norm_mlp_g"][i], sc_f, gate=(y, g_m), S=S,
                                                         name=f"norm_mlp_bwd_{i}")
        grads["norm_mlp_g"][i] = dgn
        if i % 2 == 0:
            qx, kvx, o, lse, o_t, fl = mix
            stacked("fox_out", j, L2, o_t, dy, name=f"fox_out_grad_{i}")
            do = _mm(dy, w["fox_out"], "nt", name=f"fox_out_bwd_{i}", b_layer=j, out_dtypes=(BF16,))
            dqx, dkvx, csum, rsum = _attn_bwd(qx, kvx, o, lse, do, S=S, ew=FOX_EXTRA, name=f"fox_attn_bwd_{i}",
                                              bias_grad=True)
            n_heads = D // HEAD_DIM
            csum = jnp.pad(csum.reshape(T, n_heads, HEAD_DIM)[:, :, 0], ((0, 0), (0, LANES - n_heads)))
            rsum = jnp.transpose(rsum[:, :, :2, :], (0, 3, 1, 2)).reshape(T, n_heads)
            rsum = jnp.pad(rsum, ((0, 0), (0, LANES - n_heads)))
            dproj, db = _fox_unprep(dqx, dkvx, csum, rsum, fl, w["fox_b"][j], S=S, D=D, name=f"fox_unprep_{i}")
            grads["fox_b"][j] = db
            grads["fox_in"][j] = _mm(h_t, dproj, "nn", name=f"fox_in_grad_{i}")
            dh = _mm(dproj, w["fox_in"], "nt", name=f"fox_in_bwd_{i}", b_layer=j, out_dtypes=(BF16,),
                     tk=dproj.shape[1])
        else:
            qx, kvx, o, lse, o_t, lat, cqr_t = mix
            Rq = lat.shape[1] - 2 * LANES
            stacked("mla_out", j, L2, o_t, dy, name=f"mla_out_grad_{i}")
            do = _mm(dy, w["mla_out"], "nt", name=f"mla_out_bwd_{i}", b_layer=j, out_dtypes=(BF16,))
            dqx, dkvx = _attn_bwd(qx, kvx, o, lse, do, S=S, ew=ROPE_DIM, name=f"mla_attn_bwd_{i}")
            dqpre = _unrope(dqx, cos_t, sin_s)
            stacked("mla_uq", j, L2, cqr_t[:Rq], dqpre, name=f"mla_uq_grad_{i}", out_split=n_split)
            stacked("mla_ukv", j, L2, cqr_t[Rq:], dkvx, name=f"mla_ukv_grad_{i}", tn=PAIR_KV, out_split=n_split)
            dcq = _mm(dqpre, w["mla_uq"], "nt", name=f"mla_uq_bwd_{i}", b_layer=j)
            dckr = _mm(dkvx, w["mla_ukv"], "nt", name=f"mla_ukv_bwd_{i}", b_layer=j, tk=PAIR_KV * 2)
            dlat, dgq, dgkv = _mla_mid_bwd(lat, dcq, dckr, w["mla_gq"][j], w["mla_gkv"][j], cos_t, sin_s,
                                           name=f"mla_mid_bwd_{i}")
            grads["mla_gq"][j] = dgq
            grads["mla_gkv"][j] = dgkv
            stacked("mla_down", j, L2, h_t, dlat, name=f"mla_down_grad_{i}")
            dh = _mm(dlat, w["mla_down"], "nt", name=f"mla_down_bwd_{i}", b_layer=j, out_dtypes=(BF16,))
        if i == 0:
            dx, dsh_m, dsc_m, dgn = _norm_mod_bwd(x0, dh, dx1, w["norm_mix_g"][i], sc_m, S=S, name=f"norm_mix_bwd_{i}")
        else:
            gate_below = mod[i - 1, 5]
            if i == 1 and "bwd_layer0" in hooks:
                gate_below = gate_below + hooks["bwd_layer0"](grads)[0, 0]
            dx, dsh_m, dsc_m, dgn, dz_below, dg_f_below = _norm_mod_bwd(
                x0, dh, dx1, w["norm_mix_g"][i], sc_m, gate=(saved[i - 1][6][2], gate_below), S=S,
                name=f"norm_mix_bwd_{i}")
        grads["norm_mix_g"][i] = dgn
        dmod[i] = jnp.stack([dsh_m, dsc_m, dg_m, dsh_f, dsc_f, dg_f])
    return loss, dx, jnp.stack(dmod), grads


GATHERED = ("fox_in", "fox_out", "mla_down", "mla_uq", "mla_ukv", "mla_out", "mlp_w1", "mlp_w2")
ROW_SHARDED = ("fox_out", "mla_down", "mla_out", "mlp_w2")


def _shard_layouts(wts):
    dkv = wts["mla_w_dkv"]
    dkv = jnp.pad(dkv, ((0, 0), (0, 0), (0, 2 * LANES - dkv.shape[2])))
    return {
        "fox_in": _pad_lanes(wts["fox_w_in"].astype(BF16)),
        "fox_out": wts["fox_w_out"].astype(BF16),
        "mla_down": jnp.concatenate([wts["mla_w_dq"], dkv], axis=2).astype(BF16),
        "mla_uq": jax.vmap(_uq_to_pairs)(wts["mla_w_uq"].astype(BF16)),
        "mla_ukv": jax.vmap(_ukv_to_pairs)(wts["mla_w_ukv"].astype(BF16)),
        "mla_out": wts["mla_w_out"].astype(BF16),
        "mlp_w1": wts["mlp_w1"].astype(BF16),
        "mlp_w2": wts["mlp_w2"].astype(BF16),
    }


def _small_layouts(small):
    return {
        "fox_b": [jnp.pad(b, (0, LANES - b.shape[0]))[None, :] for b in small["fox_b_f"]],
        "mla_gq": [g[None, :] for g in small["mla_q_norm_g"]],
        "mla_gkv": [g[None, :] for g in small["mla_kv_norm_g"]],
        "norm_mix_g": [g[None, :] for g in small["norm_mix_g"]],
        "norm_mlp_g": [g[None, :] for g in small["norm_mlp_g"]],
        "final_norm_g": small["final_norm_g"][None, :],
    }


def _comm_groups(L, L2):
    rest = [("fox_in", 1, L2 - 1), ("fox_out", 1, L2 - 1), ("mla_down", 0, L2), ("mla_uq", 0, L2),
            ("mla_ukv", 0, L2), ("mla_out", 0, L2), ("mlp_w1", 1, L - 1), ("mlp_w2", 1, L - 1)]
    return {"mix0": [("fox_in", 0, 1), ("fox_out", 0, 1)], "mlp0": [("mlp_w1", 0, 1), ("mlp_w2", 0, 1)],
            "rest": [e for e in rest if e[2] > 0]}


def _layer_slots(groups):
    return {(n, s + l): (g, l, cnt) for g, entries in groups.items() for n, s, cnt in entries for l in range(cnt)}


def _pad_lanes(a):
    cols = a.shape[-1]
    return jnp.pad(a, [(0, 0)] * (a.ndim - 1) + [(0, -cols % LANES)])


def _weight_views(name, gathered, D, n_fox_heads):
    n, ns, rows, cols = gathered.shape
    if name == "fox_in":
        true_cols = (3 * D + n_fox_heads) // ns
        fox = jnp.concatenate([gathered[:, k, :, :true_cols] for k in range(ns)], axis=-1)
        return {"fox_qkv": fox[:, :, :3 * D], "fox_f": _pad_lanes(fox[:, :, 3 * D:]), "fox_in": _pad_lanes(fox)}
    if name in ROW_SHARDED:
        return {name: gathered.reshape(n, ns * rows, cols)}
    return {name: gathered}


def _grad_pieces(name, g, qkv_f, n_fox_heads, ns):
    if name == "fox_in":
        D = qkv_f[0].shape[0]
        fox = jnp.stack([a[:, :3 * D + n_fox_heads] for a in qkv_f])
        cols = fox.shape[2] // ns
        return jnp.stack([_pad_lanes(fox[:, :, k * cols:(k + 1) * cols]) for k in range(ns)], axis=1)
    if name in ROW_SHARDED:
        return g.reshape(g.shape[0], ns, g.shape[1] // ns, g.shape[2])
    return g


def _small_grads(g, n_fox_heads):
    return {
        "norm_mix_g": jnp.concatenate(g["norm_mix_g"], axis=0),
        "norm_mlp_g": jnp.concatenate(g["norm_mlp_g"], axis=0),
        "final_norm_g": g["final_norm_g"][0],
        "fox_b_f": jnp.concatenate(g["fox_b"], axis=0)[:, :n_fox_heads],
        "mla_q_norm_g": jnp.concatenate(g["mla_gq"], axis=0),
        "mla_kv_norm_g": jnp.concatenate(g["mla_gkv"], axis=0),
    }


def _silu(c):
    return c * (1.0 / (1.0 + jnp.exp(-c)))


def _ada_fwd(c_all, ada_w, ada_b_cols):
    L, D, C = ada_w.shape
    Bg = c_all.shape[0]
    tc = _tile(C, 512)

    def body(c_ref, w_ref, b_ref, o_ref):
        ca = _silu(c_ref[...]).astype(BF16)
        o_ref[...] = jnp.dot(ca, w_ref[...].astype(BF16), preferred_element_type=F32) + b_ref[...]

    return pl.pallas_call(
        body, name="ada_fwd", grid=(L, C // tc),
        in_specs=[pl.BlockSpec((Bg, D), lambda l, j: (0, 0)), pl.BlockSpec((None, D, tc), lambda l, j: (l, 0, j)),
                  pl.BlockSpec((None, 1, tc), lambda l, j: (l, 0, j))],
        out_specs=pl.BlockSpec((None, Bg, tc), lambda l, j: (l, 0, j)),
        out_shape=jax.ShapeDtypeStruct((L, Bg, C), F32),
        compiler_params=_cparams(("parallel", "parallel")),
    )(c_all, ada_w, ada_b_cols)


def _ada_bwd(c_all, dmod_cols):
    L, Bg, C = dmod_cols.shape
    D = c_all.shape[1]
    tc = _tile(C, 512)

    def body(c_ref, d_ref, o_ref):
        ca = _silu(c_ref[...]).astype(BF16)
        o_ref[...] = _dot_tn(ca, d_ref[...].astype(BF16))

    return pl.pallas_call(
        body, name="ada_bwd", grid=(L, C // tc),
        in_specs=[pl.BlockSpec((Bg, D), lambda l, j: (0, 0)), pl.BlockSpec((None, Bg, tc), lambda l, j: (l, 0, j))],
        out_specs=pl.BlockSpec((None, D, tc), lambda l, j: (l, 0, j)),
        out_shape=jax.ShapeDtypeStruct((L, D, C), F32),
        compiler_params=_cparams(("parallel", "parallel")),
    )(c_all, dmod_cols)


def _adamw_update(w, gv, m, v):
    mn = ADAM_B1 * m + (1.0 - ADAM_B1) * gv
    vn = ADAM_B2 * v + (1.0 - ADAM_B2) * jnp.square(gv)
    m_hat = mn / (1.0 - ADAM_B1 ** ADAM_STEP)
    v_hat = vn / (1.0 - ADAM_B2 ** ADAM_STEP)
    return -ADAM_LR * (m_hat / (jnp.sqrt(v_hat) + ADAM_EPS) + ADAM_WD * w), mn, vn


def _adamw(w, g, m, v, *, name):
    shape = w.shape
    C = shape[-1]
    R = int(np.prod(shape[:-1])) if len(shape) > 1 else 1
    w2, g2, m2, v2 = (a.reshape(R, C) for a in (w, g, m, v))
    tr = _row_tile(R, C)

    def body(w_ref, g_ref, m_ref, v_ref, d_ref, nm_ref, nv_ref):
        d_ref[...], nm_ref[...], nv_ref[...] = _adamw_update(w_ref[...], g_ref[...], m_ref[...], v_ref[...])

    spec = pl.BlockSpec((tr, C), lambda i: (i, 0))
    out = pl.pallas_call(
        body, name=name, grid=(R // tr,), in_specs=[spec] * 4, out_specs=[spec] * 3,
        out_shape=[jax.ShapeDtypeStruct((R, C), F32)] * 3, compiler_params=_cparams(("parallel",)),
    )(w2, g2, m2, v2)
    return tuple(a.reshape(shape) for a in out)


def _adamw_halves(w, g_own, g_peer, m, v, c_idx, *, name):
    L, rows, C = w.shape
    R = rows // 2
    tr = _row_tile(R, C)

    def body(c_ref, w_ref, go_ref, gp_ref, m_ref, v_ref, g_ref, d_ref, nm_ref, nv_ref):
        gv = jnp.where(pl.program_id(1) == c_ref[0], go_ref[...], gp_ref[...])
        g_ref[...] = gv
        d_ref[...], nm_ref[...], nv_ref[...] = _adamw_update(w_ref[...], gv, m_ref[...], v_ref[...])

    full = pl.BlockSpec((None, None, tr, C), lambda l, hh, i, c_ref: (l, hh, i, 0))
    half = pl.BlockSpec((None, tr, C), lambda l, hh, i, c_ref: (l, i, 0))
    grid_spec = pltpu.PrefetchScalarGridSpec(
        num_scalar_prefetch=1, grid=(L, 2, R // tr), in_specs=[full, half, half, full, full], out_specs=[full] * 4)
    split = lambda a: a.reshape(L, 2, R, C)
    params = pltpu.CompilerParams(dimension_semantics=("parallel", "parallel", "parallel"),
                                  vmem_limit_bytes=VMEM_LIMIT_V7X,
                                  allow_input_fusion=[False, True, False, False, True, True])
    out = pl.pallas_call(
        body, name=name, grid_spec=grid_spec, out_shape=[jax.ShapeDtypeStruct((L, 2, R, C), F32)] * 4,
        compiler_params=params,
    )(c_idx, split(w), g_own, g_peer, split(m), split(v))
    return tuple(a.reshape(w.shape) for a in out)


def _sum_gathered(dm8, sm8):
    n_dev, Bl, R, D = dm8.shape
    Rs = sm8.shape[1]

    def body(dm_ref, sm_ref, ob_ref, os_ref):
        acc_b = jnp.zeros((R, D), F32)
        acc_s = jnp.zeros((Rs, D), F32)
        for d in range(n_dev):
            for b in range(Bl):
                acc_b = acc_b + dm_ref[d, b]
            acc_s = acc_s + sm_ref[d]
        ob_ref[...] = acc_b
        os_ref[...] = acc_s

    return pl.pallas_call(
        body, name="sum_gathered",
        out_shape=[jax.ShapeDtypeStruct((R, D), F32), jax.ShapeDtypeStruct((Rs, D), F32)],
        compiler_params=_cparams(None),
    )(dm8, sm8)


N_DEV = 8
N_CHIP = 4
ANY = pl.BlockSpec(memory_space=pl.ANY)
HBM = pl.BlockSpec(memory_space=pltpu.HBM)
SEM = pl.BlockSpec(memory_space=pltpu.SEMAPHORE)
DATAFLOW = pltpu.SideEffectType.DATAFLOW_SIDE_EFFECTING


def _mesh_pos():
    return lax.axis_index("x"), lax.axis_index("y"), lax.axis_index("c")


def _all_gather8(block, *, name, in_vmem):
    R, W = block.shape

    def body(x_ref, out_ref, send_sems, recv_sems, local_sem):
        x, y, c = _mesh_pos()
        me, sibling = (x, y, c), (x, y, 1 - c)
        chips = [(1 - x, y), (x, 1 - y), (1 - x, 1 - y)]

        def slot(px, py, pc):
            return out_ref.at[4 * px + 2 * py + pc]

        def copy(k, blk, to, src=None):
            return pltpu.make_async_remote_copy(
                src_ref=slot(*blk) if src is None else src, dst_ref=slot(*blk),
                send_sem=send_sems.at[k], recv_sem=recv_sems.at[k], device_id=to, device_id_type=MESH_ID)

        mine = pltpu.make_async_copy(x_ref, slot(*me), local_sem)
        mine.start()
        first = [copy(0, me, sibling, src=x_ref)]
        first += [copy(1 + j, me, (*chip, c), src=x_ref) for j, chip in enumerate(chips)]
        for cp in first:
            cp.start()
        passed = [copy(4 + j, (*chip, c), sibling) for j, chip in enumerate(chips)]
        for j, chip in enumerate(chips):
            copy(1 + j, (*chip, c), me).wait_recv()
            passed[j].start()
        copy(0, sibling, me).wait_recv()
        for j, chip in enumerate(chips):
            copy(4 + j, (*chip, 1 - c), me).wait_recv()
        for cp in first + passed:
            cp.wait_send()
        mine.wait()

    space = pl.BlockSpec(memory_space=pltpu.VMEM) if in_vmem else ANY
    return pl.pallas_call(
        body, name=name, out_shape=jax.ShapeDtypeStruct((N_DEV, R, W), block.dtype),
        in_specs=[space], out_specs=space,
        scratch_shapes=[pltpu.SemaphoreType.DMA((7,)), pltpu.SemaphoreType.DMA((7,)), pltpu.SemaphoreType.DMA],
        compiler_params=pltpu.CompilerParams(vmem_limit_bytes=VMEM_LIMIT_V7X),
    )(block)


def _comm_call(body, arrays, out_shapes, n_sems, *, name):
    return pl.pallas_call(
        body, name=name, out_shape=out_shapes, in_specs=[ANY] * len(arrays), out_specs=[ANY] * len(out_shapes),
        scratch_shapes=[pltpu.SemaphoreType.DMA((n_sems,)), pltpu.SemaphoreType.DMA((n_sems,)),
                        pltpu.SemaphoreType.DMA((len(arrays),))],
    )(*arrays)


def _gather_weights(shards, *, name):
    n = len(shards)

    def body(*refs):
        xs, outs = refs[:n], refs[n:2 * n]
        send_sems, recv_sems, local_sems = refs[2 * n:]
        x, y, c = _mesh_pos()
        me, sibling = (x, y, c), (x, y, 1 - c)
        chips = [(1 - x, y), (x, 1 - y), (1 - x, 1 - y)]
        waits = []
        for i in range(n):
            nl = shards[i].shape[0]
            own = xs[i].at[pl.ds(0, nl), c]

            def slot(px, py, pc, i=i, nl=nl):
                return outs[i].at[pl.ds(0, nl), 2 * px + py, pc]

            def copy(k, blk, to, src=None, i=i, slot=slot):
                return pltpu.make_async_remote_copy(
                    src_ref=slot(*blk) if src is None else src, dst_ref=slot(*blk),
                    send_sem=send_sems.at[7 * i + k], recv_sem=recv_sems.at[7 * i + k], device_id=to,
                    device_id_type=MESH_ID)

            mine = pltpu.make_async_copy(own, slot(*me), local_sems.at[i])
            mine.start()
            first = [copy(0, me, sibling, src=own)]
            first += [copy(1 + j, me, (*chip, c), src=own) for j, chip in enumerate(chips)]
            for cp in first:
                cp.start()
            waits.append((copy, mine, first))
        for copy, mine, first in waits:
            passed = [copy(4 + j, (*chip, c), sibling) for j, chip in enumerate(chips)]
            for j, chip in enumerate(chips):
                copy(1 + j, (*chip, c), me).wait_recv()
                passed[j].start()
            copy(0, sibling, me).wait_recv()
            for j, chip in enumerate(chips):
                copy(4 + j, (*chip, 1 - c), me).wait_recv()
            for cp in first + passed:
                cp.wait_send()
            mine.wait()

    out_shapes = [jax.ShapeDtypeStruct((s.shape[0], N_CHIP) + s.shape[1:], s.dtype) for s in shards]
    return _comm_call(body, shards, out_shapes, 7 * n, name=name)


def _place_own(shard, chip_idx, c_idx, *, name):
    n, _, rows, cols = shard.shape
    tr = _row_tile(rows, cols)

    def body(k_ref, c_ref, x_ref, o_ref):
        o_ref[...] = x_ref[...]

    grid_spec = pltpu.PrefetchScalarGridSpec(
        num_scalar_prefetch=2, grid=(n, rows // tr),
        in_specs=[pl.BlockSpec((None, None, tr, cols), lambda l, i, k_ref, c_ref: (l, c_ref[0], i, 0))],
        out_specs=pl.BlockSpec((None, None, None, tr, cols), lambda l, i, k_ref, c_ref: (l, k_ref[0], c_ref[0], i, 0)))
    return pl.pallas_call(
        body, name=name, grid_spec=grid_spec,
        out_shape=jax.ShapeDtypeStruct((n, N_CHIP, 2, rows, cols), shard.dtype),
        compiler_params=_cparams(("parallel", "parallel")),
    )(chip_idx, c_idx, shard)


def _gather_copies(x_refs, land_refs, send_sems, recv_sems):
    x, y, c = _mesh_pos()
    k_me = 2 * x + y
    targets = [(x, y, 1 - c), (1 - x, y, c), (x, 1 - y, c), (1 - x, 1 - y, c)]
    copies = []
    for i, (x_ref, land_ref) in enumerate(zip(x_refs, land_refs)):
        nl = x_ref.shape[0]
        for j, to in enumerate(targets):
            copies.append(pltpu.make_async_remote_copy(
                src_ref=x_ref.at[pl.ds(0, nl), c], dst_ref=land_ref.at[pl.ds(0, nl), k_me, c],
                send_sem=send_sems.at[4 * i + j], recv_sem=recv_sems.at[4 * i + j], device_id=to,
                device_id_type=MESH_ID))
    return copies


def _split_start(copies_fn, srcs, lands, after, *, name, sems_per_array):
    n = len(srcs)

    def body(*refs):
        send_sems, recv_sems = refs[2 * n + 1], refs[2 * n + 2]
        for cp in copies_fn(refs[:n], refs[n:2 * n], send_sems, recv_sems):
            cp.start()
        refs[-1][...] = jnp.zeros_like(refs[-1])

    operands = [pltpu.with_memory_space_constraint(a, pltpu.HBM) for a in list(srcs) + list(lands)]
    n_sems = sems_per_array * n
    out_shape = ([pltpu.SemaphoreType.DMA((n_sems,)), pltpu.SemaphoreType.DMA((n_sems,))]
                 + [pltpu.HBM(a.shape, a.dtype) for a in operands] + [jax.ShapeDtypeStruct((8, LANES), F32)])
    res = pl.pallas_call(
        body, name=name, out_shape=out_shape, in_specs=[HBM] * (2 * n) + [ANY],
        out_specs=[SEM, SEM] + [HBM] * (2 * n) + [pl.BlockSpec(memory_space=pltpu.VMEM)],
        input_output_aliases={i: 2 + i for i in range(2 * n)},
        compiler_params=pltpu.CompilerParams(has_side_effects=DATAFLOW),
    )(*operands, after)
    return res[0], res[1], list(res[2:2 + n]), list(res[2 + n:2 + 2 * n]), res[-1]


def _split_wait(copies_fn, send_sems, recv_sems, srcs, lands, after, *, name):
    n = len(srcs)

    def body(*refs):
        for cp in copies_fn(refs[:n], refs[n:2 * n], refs[2 * n], refs[2 * n + 1]):
            cp.wait_send()
            cp.wait_recv()

    res = pl.pallas_call(
        body, name=name, out_shape=[pltpu.HBM(a.shape, a.dtype) for a in list(srcs) + list(lands)],
        in_specs=[HBM] * (2 * n) + [SEM, SEM, ANY], out_specs=[HBM] * (2 * n),
        input_output_aliases={i: i for i in range(2 * n)},
        compiler_params=pltpu.CompilerParams(has_side_effects=DATAFLOW),
    )(*srcs, *lands, send_sems, recv_sems, after)
    return list(res[:n]), list(res[n:])


def _gather_forward(lands, *, name):
    n = len(lands)

    def body(*refs):
        xs = refs[:n]
        send_sems, recv_sems, _ = refs[2 * n:]
        x, y, c = _mesh_pos()
        chips = [(1 - x, y), (x, 1 - y), (1 - x, 1 - y)]
        copies = []
        for i in range(n):
            nl = lands[i].shape[0]
            for j, (cx, cy) in enumerate(chips):
                here = xs[i].at[pl.ds(0, nl), 2 * cx + cy, c]
                cp = pltpu.make_async_remote_copy(
                    src_ref=here, dst_ref=here, send_sem=send_sems.at[3 * i + j], recv_sem=recv_sems.at[3 * i + j],
                    device_id=(x, y, 1 - c), device_id_type=MESH_ID)
                cp.start()
                copies.append(cp)
        for cp in copies:
            cp.wait()

    return pl.pallas_call(
        body, name=name, out_shape=[jax.ShapeDtypeStruct(a.shape, a.dtype) for a in lands],
        in_specs=[ANY] * n, out_specs=[ANY] * n, input_output_aliases={i: i for i in range(n)},
        scratch_shapes=[pltpu.SemaphoreType.DMA((3 * n,)), pltpu.SemaphoreType.DMA((3 * n,)),
                        pltpu.SemaphoreType.DMA((1,))],
    )(*lands)


def _pair_copies(g_refs, land_refs, send_sems, recv_sems):
    x, y, c = _mesh_pos()
    copies = []
    for i, (g_ref, land_ref) in enumerate(zip(g_refs, land_refs)):
        nl, ns = g_ref.shape[:2]
        copies.append(pltpu.make_async_remote_copy(
            src_ref=g_ref.at[pl.ds(0, nl), pl.ds(0, ns), 1 - c], dst_ref=land_ref, send_sem=send_sems.at[i],
            recv_sem=recv_sems.at[i], device_id=(x, y, 1 - c), device_id_type=MESH_ID))
    return copies


def _pair_exchange(gs, *, name):
    n = len(gs)

    def body(*refs):
        send_sems, recv_sems, _ = refs[2 * n:]
        copies = _pair_copies(refs[:n], refs[n:2 * n], send_sems, recv_sems)
        for cp in copies:
            cp.start()
        for cp in copies:
            cp.wait()

    out_shapes = [jax.ShapeDtypeStruct(g.shape[:2] + g.shape[3:], g.dtype) for g in gs]
    return _comm_call(body, gs, out_shapes, n, name=name)


def _chip_copies(p_refs, land_refs, send_sems, recv_sems):
    x, y, c = _mesh_pos()
    k_me = 2 * x + y
    chips = [(1 - x, y), (x, 1 - y), (1 - x, 1 - y)]
    copies = []
    for i, (p_ref, land_ref) in enumerate(zip(p_refs, land_refs)):
        nl = p_ref.shape[0]
        for j, (cx, cy) in enumerate(chips):
            copies.append(pltpu.make_async_remote_copy(
                src_ref=p_ref.at[pl.ds(0, nl), 2 * cx + cy], dst_ref=land_ref.at[k_me],
                send_sem=send_sems.at[3 * i + j], recv_sem=recv_sems.at[3 * i + j],
                device_id=(cx, cy, c), device_id_type=MESH_ID))
    return copies


def _chip_landing(ps):
    return [lax.empty((p.shape[1], p.shape[0]) + p.shape[2:], p.dtype) for p in ps]


def _pair_swap(ss, *, name):
    n = len(ss)

    def body(*refs):
        xs, outs = refs[:n], refs[n:2 * n]
        send_sems, recv_sems, _ = refs[2 * n:]
        x, y, c = _mesh_pos()
        copies = []
        for i in range(n):
            cp = pltpu.make_async_remote_copy(src_ref=xs[i], dst_ref=outs[i], send_sem=send_sems.at[i],
                                              recv_sem=recv_sems.at[i], device_id=(x, y, 1 - c),
                                              device_id_type=MESH_ID)
            cp.start()
            copies.append(cp)
        for cp in copies:
            cp.wait()

    out_shapes = [jax.ShapeDtypeStruct(s.shape, s.dtype) for s in ss]
    return _comm_call(body, ss, out_shapes, n, name=name)


def _row_tile(rows, cols):
    tr = rows
    while tr * cols > 256 * 1024 and tr % 16 == 0:
        tr //= 2
    return tr


def _pair_add(g, recv, c_idx, *, name):
    n, ns, _, rows, W = g.shape
    tr = _row_tile(rows, W)

    def body(c_ref, g_ref, r_ref, o_ref):
        o_ref[...] = (g_ref[...] + r_ref[...]).astype(BF16)

    piece = pl.BlockSpec((None, tr, W), lambda p, i, c_ref: (p, i, 0))
    grid_spec = pltpu.PrefetchScalarGridSpec(
        num_scalar_prefetch=1, grid=(n * ns, rows // tr),
        in_specs=[pl.BlockSpec((None, None, tr, W), lambda p, i, c_ref: (p, c_ref[0], i, 0)), piece],
        out_specs=piece)
    out = pl.pallas_call(
        body, name=name, grid_spec=grid_spec, out_shape=jax.ShapeDtypeStruct((n * ns, rows, W), BF16),
        compiler_params=_cparams(("parallel", "parallel")),
    )(c_idx, g.reshape(n * ns, 2, rows, W), recv.reshape(n * ns, rows, W))
    return out.reshape(n, ns, rows, W)


def _sum_pieces(land, own, chip_idx, *, name):
    n, nl, A, W = land.shape
    tr = _row_tile(A, W)

    def body(k_ref, l_ref, o_ref, out_ref):
        acc = jnp.zeros(out_ref.shape, F32)
        for k in range(n):
            acc = acc + jnp.where(k == k_ref[0], o_ref[...], l_ref[k]).astype(F32)
        out_ref[...] = acc

    grid_spec = pltpu.PrefetchScalarGridSpec(
        num_scalar_prefetch=1, grid=(nl, A // tr),
        in_specs=[pl.BlockSpec((n, None, tr, W), lambda l, i, k_ref: (0, l, i, 0)),
                  pl.BlockSpec((None, None, tr, W), lambda l, i, k_ref: (l, k_ref[0], i, 0))],
        out_specs=pl.BlockSpec((None, tr, W), lambda l, i, k_ref: (l, i, 0)))
    return pl.pallas_call(
        body, name=name, grid_spec=grid_spec, out_shape=jax.ShapeDtypeStruct((nl, A, W), F32),
        compiler_params=_cparams(("parallel", "parallel")),
    )(chip_idx, land, own)


SMALL = ("norm_mix_g", "norm_mlp_g", "final_norm_g", "fox_b_f", "mla_q_norm_g", "mla_kv_norm_g")
WEIGHT_ORDER = ("ada_w", "ada_b", "norm_mix_g", "norm_mlp_g", "fox_w_in", "fox_b_f", "fox_w_out", "mla_w_dq",
                "mla_q_norm_g", "mla_w_uq", "mla_w_dkv", "mla_kv_norm_g", "mla_w_ukv", "mla_w_out", "mlp_w1",
                "mlp_w2", "final_norm_g")


def _small_rows(vals, D):
    rows = [vals["norm_mix_g"], vals["norm_mlp_g"], vals["final_norm_g"][None, :]]
    for n in ("fox_b_f", "mla_q_norm_g", "mla_kv_norm_g"):
        flat = vals[n].reshape(-1)
        assert flat.shape[0] <= D
        rows.append(jnp.pad(flat, (0, D - flat.shape[0]))[None, :])
    return jnp.concatenate(rows, axis=0)


def _small_unrows(rows, shapes):
    L = shapes["norm_mix_g"][0]
    out = {"norm_mix_g": rows[0:L], "norm_mlp_g": rows[L:2 * L], "final_norm_g": rows[2 * L]}
    for k, n in enumerate(("fox_b_f", "mla_q_norm_g", "mla_kv_norm_g")):
        size = int(np.prod(shapes[n]))
        out[n] = rows[2 * L + 1 + k, :size].reshape(shapes[n])
    return out


def kernel(x, c, positions, ada_w, ada_b, norm_mix_g, norm_mlp_g, fox_w_in, fox_b_f, fox_w_out, mla_w_dq, mla_q_norm_g, mla_w_uq, mla_w_dkv, mla_kv_norm_g, mla_w_ukv, mla_w_out, mlp_w1, mlp_w2, final_norm_g, loss_target, m_ada_w, m_ada_b, m_norm_mix_g, m_norm_mlp_g, m_fox_w_in, m_fox_b_f, m_fox_w_out, m_mla_w_dq, m_mla_q_norm_g, m_mla_w_uq, m_mla_w_dkv, m_mla_kv_norm_g, m_mla_w_ukv, m_mla_w_out, m_mlp_w1, m_mlp_w2, m_final_norm_g, v_ada_w, v_ada_b, v_norm_mix_g, v_norm_mlp_g, v_fox_w_in, v_fox_b_f, v_fox_w_out, v_mla_w_dq, v_mla_q_norm_g, v_mla_w_uq, v_mla_w_dkv, v_mla_kv_norm_g, v_mla_w_ukv, v_mla_w_out, v_mlp_w1, v_mlp_w2, v_final_norm_g):
    args = dict(locals())
    wts = {n: args[n] for n in WEIGHT_ORDER}
    mom = {n: args["m_" + n] for n in WEIGHT_ORDER}
    var = {n: args["v_" + n] for n in WEIGHT_ORDER}
    Bl, S, D = x.shape
    T = Bl * S
    L = ada_w.shape[0]
    C = ada_w.shape[2]
    mx, my, mc = _mesh_pos()
    chip = 2 * mx + my
    dev = 4 * mx + 2 * my + mc
    c_idx = jnp.reshape(mc, (1,)).astype(jnp.int32)
    chip_idx = jnp.reshape(chip, (1,)).astype(jnp.int32)
    small = {n: wts[n] for n in SMALL}
    L2, q_cols = mla_q_norm_g.shape
    n_fox_heads = fox_b_f.shape[1]

    shards = _shard_layouts(wts)
    groups = _comm_groups(L, L2)
    slots = _layer_slots(groups)

    def row_halves(a):
        return a.reshape(a.shape[:-2] + (2, a.shape[-2] // 2, a.shape[-1]))

    def whole_rows(a):
        return a.reshape(a.shape[:2] + (a.shape[2] * a.shape[3], a.shape[4]))

    part = {g: [row_halves(shards[n][s:s + cnt]) for n, s, cnt in entries] for g, entries in groups.items()}
    mix0 = _gather_weights(part["mix0"], name="gather_mix0")
    gather_sems, after = {}, mix0[0]
    for group in ("mlp0", "rest"):
        placed = [_place_own(a, chip_idx, c_idx, name=f"gather_place_{group}_{n}")
                  for a, (n, _, _) in zip(part[group], groups[group])]
        gather_sems[group] = _split_start(_gather_copies, part[group], placed, after, name=f"gather_{group}_start",
                                          sems_per_array=4)
        after = gather_sems[group][4]

    def layer_weights(w, group, arrays):
        for (n, s, cnt), a in zip(groups[group], arrays):
            for key, view in _weight_views(n, whole_rows(a), D, n_fox_heads).items():
                for l in range(cnt):
                    w[key][s + l] = (view, l)

    w = {key: [None] * L2
         for key in ("fox_qkv", "fox_f", "fox_in", "fox_out", "mla_down", "mla_uq", "mla_ukv", "mla_out")}
    w.update({key: [None] * L for key in ("mlp_w1", "mlp_w2")})
    layer_weights(w, "mix0", mix0)

    def gathered_now(group):
        def hook(x_now, w):
            _, landed = _split_wait(_gather_copies, *gather_sems[group][:4], x_now, name=f"gather_{group}_wait")
            layer_weights(w, group, _gather_forward(landed, name=f"gather_{group}_forward"))
            return w
        return hook

    c_pad = jnp.concatenate([c, jnp.pad(mla_q_norm_g, ((0, 8 - Bl - L2), (0, D - q_cols)))], axis=0)
    c8 = _all_gather8(c_pad, name="gather_c", in_vmem=True)
    c_all = c8[:, :Bl].reshape(N_DEV * Bl, D)
    qg4 = c8.reshape(N_CHIP, 2, 8, D)[:, 0, Bl:Bl + L2, :q_cols]
    small["mla_q_norm_g"] = jnp.transpose(qg4, (1, 0, 2)).reshape(L2, N_CHIP * q_cols)
    ada_b_cols = lax.dynamic_slice_in_dim(ada_b, chip * C, C, axis=1)[:, None, :]
    mod_cols = _ada_fwd(c_all, ada_w, ada_b_cols)
    mod8 = _all_gather8(mod_cols.reshape(L * N_DEV * Bl, C), name="gather_mod", in_vmem=True)
    mod4 = mod8.reshape(N_CHIP, 2, L, N_DEV * Bl, C)[:, 0]
    mod_me = lax.dynamic_slice_in_dim(mod4, dev * Bl, Bl, axis=2)
    mod = jnp.transpose(mod_me, (1, 2, 0, 3)).reshape(L, Bl, 6, D)
    mod = jnp.transpose(mod, (0, 2, 1, 3))[:, :, :, None, :]

    w.update(_small_layouts(small))
    mod = mod + after[0, 0]
    pending = {}

    def grad_pieces(group, g_now):
        out = []
        for n, s, cnt in groups[group]:
            qkv_f = [g_now["fox_in"][j] for j in range(s, s + cnt)] if n == "fox_in" else None
            stacked_g = None if n == "fox_in" else g_now[n][group]
            out.append(row_halves(_grad_pieces(n, stacked_g, qkv_f, n_fox_heads, N_CHIP)))
        return out

    def pair_added(group, big, sibling):
        return [_pair_add(a, r, c_idx, name=f"grad_pair_add_{group}_{n}")
                for (n, _, _), a, r in zip(groups[group], big, sibling)]

    def exchange_start(group, ps, after=None):
        pending[group] = _split_start(_chip_copies, ps, _chip_landing(ps), chip_idx if after is None else after,
                                      name=f"grad_exchange_{group}_start", sems_per_array=3)
        return pending[group][4]

    def bwd_layer0(g_now):
        big = grad_pieces("rest", g_now)
        landing = [lax.empty(a.shape[:2] + a.shape[3:], a.dtype) for a in big]
        pending["rest_pair"] = _split_start(_pair_copies, big, landing, chip_idx, name="grad_pair_rest_start",
                                            sems_per_array=1)
        return pending["rest_pair"][4]

    def bwd_mix0(g_now):
        send_sems, recv_sems, big, landed, _ = pending["rest_pair"]
        big, landed = _split_wait(_pair_copies, send_sems, recv_sems, big, landed, g_now["mlp_w1"]["mlp0"],
                                  name="grad_pair_rest_wait")
        started = exchange_start("rest", pair_added("rest", big, landed))
        big = grad_pieces("mlp0", g_now)
        return exchange_start("mlp0", pair_added("mlp0", big, _pair_exchange(big, name="grad_pair_exchange_mlp0")),
                              after=started)

    half = ROPE_DIM // 2
    inv_freq = ROPE_THETA ** (-jnp.arange(0, ROPE_DIM, 2, dtype=F32) / ROPE_DIM)
    lane = np.arange(LANES)
    inv_freq_row = jnp.tile(inv_freq, LANES // half)[None, :]
    sign_row = jnp.asarray(np.where(lane < 2 * ROPE_DIM, np.where(lane % ROPE_DIM < half, -1.0, 1.0), 0.0), F32)[None, :]
    pos_f = positions.astype(F32).reshape(T, 1)
    loss_row, grad_x, dmod, g = _local_step(x.reshape(T, D), loss_target.reshape(T, D), pos_f, inv_freq_row, sign_row,
                                            mod, w, slots, S=S,
                                            hooks={"fwd_mlp0": gathered_now("mlp0"), "fwd_layer1": gathered_now("rest"),
                                                   "bwd_layer0": bwd_layer0, "bwd_mix0": bwd_mix0})
    g_small = _small_grads(g, n_fox_heads)
    big = grad_pieces("mix0", g)
    exchange_start("mix0", pair_added("mix0", big, _pair_exchange(big, name="grad_pair_exchange_mix0")))

    Rs = -(-(2 * L + 5) // 8) * 8
    srows = jnp.concatenate([_small_rows(g_small, D), jnp.pad(loss_row, ((0, 0), (0, D - LANES)))], axis=0)
    srows = jnp.pad(srows, ((0, Rs - srows.shape[0]), (0, 0)))
    drows = jnp.transpose(dmod[:, :, :, 0, :], (2, 0, 1, 3)).reshape(Bl * L * 6, D)
    both8 = _all_gather8(jnp.concatenate([drows, srows], axis=0), name="gather_small", in_vmem=True)
    dm8 = both8[:, :Bl * L * 6].reshape(N_DEV, Bl, L * 6, D)
    sm8 = both8[:, Bl * L * 6:]
    adb_rows, small_sum = _sum_gathered(dm8, sm8)
    grad_ada_b = adb_rows.reshape(L, 6 * D)
    loss = small_sum[2 * L + 4, 0]
    small_shapes = {n: (wts[n].shape if n != "mla_q_norm_g" else (wts[n].shape[0], N_CHIP * q_cols)) for n in SMALL}
    gs = _small_unrows(small_sum, small_shapes)
    gs["mla_q_norm_g"] = lax.dynamic_slice_in_dim(gs["mla_q_norm_g"], chip * q_cols, q_cols, axis=1)

    dmod16 = jnp.transpose(dm8.reshape(N_DEV, Bl, L, 6 * D), (2, 0, 1, 3)).reshape(L, N_DEV * Bl, 6 * D)
    dmod_cols = lax.dynamic_slice_in_dim(dmod16, chip * C, C, axis=2)
    grad_ada_w = _ada_bwd(c_all, dmod_cols)

    grads = dict(gs)
    grads["ada_w"] = grad_ada_w
    grads["ada_b"] = grad_ada_b
    delta, new_m, new_v = {}, {}, {}
    for n in ("ada_w", "ada_b"):
        delta[n], new_m[n], new_v[n] = _adamw(wts[n], grads[n], mom[n], var[n], name=f"adamw_{n}")
    shard_small_shapes = {n: wts[n].shape for n in SMALL}
    packs = [jnp.pad(_small_rows({n: src[n] for n in SMALL}, D), ((0, Rs - 2 * L - 4), (0, 0)))
             for src in (wts, grads, mom, var)]
    for dst, rows in zip((delta, new_m, new_v), _adamw(*packs, name="adamw_small")):
        dst.update(_small_unrows(rows, shard_small_shapes))

    halves = {}
    for group, after in (("rest", grad_x), ("mlp0", grad_x), ("mix0", delta["ada_w"])):
        send_sems, recv_sems, ps, lands, _ = pending[group]
        ps, lands = _split_wait(_chip_copies, send_sems, recv_sems, ps, lands, after, name=f"grad_exchange_{group}_wait")
        sums = [_sum_pieces(ld, p, chip_idx, name=f"grad_sum_{group}_{n}")
                for (n, _, _), ld, p in zip(groups[group], lands, ps)]
        swapped = _pair_swap(sums, name=f"grad_pair_swap_{group}")
        for (n, _, _), a, b in zip(groups[group], sums, swapped):
            halves[(n, group)] = (a, b)

    def all_layers(n, which):
        return jnp.concatenate([halves[(n, grp)][which] for grp in groups if (n, grp) in halves], axis=0)

    own = {n: all_layers(n, 0) for n in GATHERED}
    peer = {n: all_layers(n, 1) for n in GATHERED}
    for nat, n in (("fox_w_in", "fox_in"), ("fox_w_out", "fox_out"), ("mla_w_out", "mla_out"), ("mlp_w1", "mlp_w1"),
                   ("mlp_w2", "mlp_w2")):
        cols = wts[nat].shape[-1]
        res = _adamw_halves(_pad_lanes(wts[nat]), own[n], peer[n], _pad_lanes(mom[nat]), _pad_lanes(var[nat]), c_idx,
                            name=f"adamw_{nat}")
        grads[nat], delta[nat], new_m[nat], new_v[nat] = (a[..., :cols] for a in res)
    joined = {n: jnp.concatenate([jnp.where(mc == 0, own[n], peer[n]), jnp.where(mc == 0, peer[n], own[n])], axis=1)
              for n in ("mla_down", "mla_uq", "mla_ukv")}
    rq = mla_w_dq.shape[-1]
    grads["mla_w_dq"] = joined["mla_down"][:, :, :rq]
    grads["mla_w_dkv"] = joined["mla_down"][:, :, rq:rq + KV_RANK + ROPE_DIM]
    grads["mla_w_uq"] = jax.vmap(_uq_from_pairs)(joined["mla_uq"])
    grads["mla_w_ukv"] = jax.vmap(_ukv_from_pairs)(joined["mla_ukv"])
    for n in ("mla_w_dq", "mla_w_dkv", "mla_w_uq", "mla_w_ukv"):
        delta[n], new_m[n], new_v[n] = _adamw(wts[n], grads[n], mom[n], var[n], name=f"adamw_{n}")

    return (loss, grad_x.reshape(Bl, S, D), *[grads[n] for n in WEIGHT_ORDER], *[delta[n] for n in WEIGHT_ORDER],
            *[new_m[n] for n in WEIGHT_ORDER], *[new_v[n] for n in WEIGHT_ORDER])
```

```python
import numpy as np
import jax
import jax.numpy as jnp
from jax import lax
from jax.experimental import pallas as pl
from jax.experimental.pallas import tpu as pltpu

F32 = jnp.float32
BF16 = jnp.bfloat16
MESH_ID = pl.DeviceIdType.MESH

NORM_EPS = 1e-6
ROPE_THETA = 10000.0
HEAD_DIM = 64
ROPE_DIM = 32
KV_RANK = 128
MLA_SCALE = (HEAD_DIM + ROPE_DIM) ** -0.5
FOX_EXTRA = 6
PAIR_Q = 256
PAIR_KV = 384
LANES = 128
ADAM_LR = 0.001
ADAM_B1 = 0.9
ADAM_B2 = 0.999
ADAM_EPS = 1e-08
ADAM_WD = 0.01
ADAM_STEP = 10
VMEM_LIMIT_V7X = 48 * 1024 * 1024
MM_VMEM_BUDGET = 36 * 1024 * 1024
MM_VMEM_HEADROOM = VMEM_LIMIT_V7X - MM_VMEM_BUDGET
NEG_BIG = -1e30
ATTN_UNROLL = 4
ATTN_BLOCK = 256
ATTN_Q_ROWS = 512
ATTN_K_ROWS = 512


def _cparams(sem=None, vmem_limit=VMEM_LIMIT_V7X):
    return pltpu.CompilerParams(dimension_semantics=sem, vmem_limit_bytes=vmem_limit)


def _tile(n, want):
    if n <= want:
        return n
    for t in range(want - want % LANES, 0, -LANES):
        if n % t == 0:
            return t
    raise ValueError((n, want))


def _mm(a, b, mode, *, name, out_dtypes=(F32,), epilogue=None, extras=(), rowvecs=(), tables=(),
        seq=None, a_off=0, a_sz=None, b_layer=None, out_stack=None, out_split=0, out_t=(), full_rows=False,
        vmem_budget=MM_VMEM_BUDGET, tm=1024, tn=1024, tk=2048):
    if isinstance(b, (list, tuple)):
        b, b_layer = b[b_layer]
    b_rows, b_cols = b.shape[-2], b.shape[-1]
    n_split = b.shape[1] if b.ndim == 4 else 1
    assert mode in ("nn", "nt")
    if mode == "nn":
        M, K, N = a.shape[0], b_rows, b_cols * n_split
    else:
        M, K, N = a.shape[0], b_cols * n_split, b_rows
    assert a_sz is None or a_sz == K
    tm = _tile(seq if rowvecs else M, tm)
    n_piece = N // max(out_split, n_split if mode == "nn" else 1, 1)
    tn = _tile(n_piece, tn)
    tk = _tile(K // (n_split if mode == "nt" else 1), tk)
    ne, nr, nt_ = len(extras), len(rowvecs), len(tables)
    no = len(out_dtypes)

    def vmem_estimate():
        blocks = tm * tk * a.dtype.itemsize + tk * tn * b.dtype.itemsize
        blocks += tm * tn * (sum(e.dtype.itemsize for e in extras) + sum(jnp.dtype(d).itemsize for d in out_dtypes))
        return 2 * blocks + 2 * tm * tn * 4

    if full_rows:
        assert tn == N
    while vmem_estimate() > vmem_budget and max(tm, tn) > 256:
        if tn >= tm and not full_rows:
            tn //= 2
        else:
            tm //= 2
    nk = K // tk

    assert a_off % tk == 0
    a_spec = pl.BlockSpec((tm, tk), lambda i, j, k: (i, k + a_off // tk))
    dims = (((1,), (0,)), ((), ())) if mode == "nn" else (((1,), (1,)), ((), ()))
    lead = () if b.ndim == 2 else (b_layer,)
    sq = (None,) * (b.ndim - 2)
    if mode == "nt":
        kb = b_cols // tk
        if b.ndim == 4:
            b_spec = pl.BlockSpec(sq + (tn, tk), lambda i, j, k: lead + (k // kb, j, k % kb))
        else:
            b_spec = pl.BlockSpec(sq + (tn, tk), lambda i, j, k: lead + (j, k))
    else:
        nb = b_cols // tn
        if b.ndim == 4:
            b_spec = pl.BlockSpec(sq + (tk, tn), lambda i, j, k: lead + (j // nb, k, j % nb))
        else:
            b_spec = pl.BlockSpec(sq + (tk, tn), lambda i, j, k: lead + (k, j))
    in_specs = [a_spec, b_spec]
    in_specs += [pl.BlockSpec((tm, tn), lambda i, j, k: (i, j)) for _ in extras]
    if rowvecs:
        assert seq % tm == 0
        per = seq // tm
        in_specs += [pl.BlockSpec((None, 1, tn), lambda i, j, k: (i // per, 0, j)) for _ in rowvecs]
    in_specs += [pl.BlockSpec((tm, LANES), lambda i, j, k: (i, 0)) for _ in tables]
    operands = [a, b, *extras, *rowvecs, *tables]
    aliases = {}
    transposed = tuple(out_t) + (False,) * (no - len(out_t))
    if out_stack is None:
        out_specs = [pl.BlockSpec((tn, tm), lambda i, j, k: (j, i)) if t else pl.BlockSpec((tm, tn), lambda i, j, k: (i, j))
                     for t in transposed]
        out_shape = [jax.ShapeDtypeStruct((N, M) if t else (M, N), d) for d, t in zip(out_dtypes, transposed)]
    else:
        prev, layer, n_layers = out_stack
        assert no == 1
        if out_split:
            ob = n_piece // tn
            out_specs = [pl.BlockSpec((None, None, tm, tn), lambda i, j, k: (layer, j // ob, i, j % ob))]
            out_shape = [jax.ShapeDtypeStruct((n_layers, out_split, M, n_piece), out_dtypes[0])]
        else:
            out_specs = [pl.BlockSpec((None, tm, tn), lambda i, j, k: (layer, i, j))]
            out_shape = [jax.ShapeDtypeStruct((n_layers, M, N), out_dtypes[0])]
        if prev is not None:
            in_specs.append(pl.BlockSpec(memory_space=pl.ANY))
            aliases = {len(operands): 0}
            operands.append(prev)
    n_in = len(operands)

    def body(*refs):
        a_ref, b_ref = refs[0], refs[1]
        side = refs[2:2 + ne + nr + nt_]
        outs = refs[n_in:n_in + no]

        def finish(acc):
            res = (acc,) if epilogue is None else epilogue(acc, *[r[...] for r in side])
            for o_ref, r, t in zip(outs, res, transposed):
                o_ref[...] = (r.T if t else r).astype(o_ref.dtype)

        part = lax.dot_general(a_ref[...].astype(BF16), b_ref[...].astype(BF16), dims,
                               preferred_element_type=F32)
        if nk == 1:
            finish(part)
        else:
            acc_ref = refs[-1]
            k = pl.program_id(2)

            @pl.when(k == 0)
            def _():
                acc_ref[...] = part

            @pl.when(k > 0)
            def _():
                acc_ref[...] += part

            @pl.when(k == nk - 1)
            def _():
                finish(acc_ref[...])

    res = pl.pallas_call(
        body, name=name, grid=(M // tm, N // tn, nk), in_specs=in_specs, out_specs=out_specs,
        out_shape=out_shape, scratch_shapes=[pltpu.VMEM((tm, tn), F32)] if nk > 1 else [],
        input_output_aliases=aliases,
        compiler_params=_cparams(("parallel", "parallel", "arbitrary"), vmem_limit=vmem_budget + MM_VMEM_HEADROOM),
    )(*operands)
    return res[0] if no == 1 else tuple(res)


def _rope128(x, cos_t, sin_s):
    lane = lax.broadcasted_iota(jnp.int32, x.shape, 1)
    first = (lane % ROPE_DIM) < (ROPE_DIM // 2)
    swapped = jnp.where(first, pltpu.roll(x, LANES - ROPE_DIM // 2, 1), pltpu.roll(x, ROPE_DIM // 2, 1))
    return x * cos_t + swapped * sin_s


def _rope_pairs(acc, cos_t, sin_s, sign):
    parts = []
    for p in range(acc.shape[1] // PAIR_Q):
        parts.append(acc[:, p * PAIR_Q:p * PAIR_Q + LANES])
        parts.append(_rope128(acc[:, p * PAIR_Q + LANES:(p + 1) * PAIR_Q], cos_t, sign * sin_s))
    return jnp.concatenate(parts, axis=1)


def _rope_tables(pos_f, inv_freq_row, sign_row):
    T = pos_f.shape[0]
    tt = _tile(T, 512)

    def body(p_ref, f_ref, s_ref, cos_ref, sin_ref):
        ang = p_ref[...] * f_ref[...]
        cos_ref[...] = jnp.cos(ang)
        sin_ref[...] = jnp.sin(ang) * s_ref[...]

    return pl.pallas_call(
        body, name="rope_tables", grid=(T // tt,),
        in_specs=[pl.BlockSpec((tt, 1), lambda i: (i, 0)), pl.BlockSpec((1, LANES), lambda i: (0, 0)),
                  pl.BlockSpec((1, LANES), lambda i: (0, 0))],
        out_specs=[pl.BlockSpec((tt, LANES), lambda i: (i, 0))] * 2,
        out_shape=[jax.ShapeDtypeStruct((T, LANES), F32)] * 2,
        compiler_params=_cparams(("parallel",)),
    )(pos_f, inv_freq_row, sign_row)


def _unrope(dqx, cos_t, sin_s):
    T, W = dqx.shape
    tt = _tile(T, 512)

    def body(d_ref, c_ref, s_ref, o_ref):
        o_ref[...] = _rope_pairs(d_ref[...].astype(F32) * MLA_SCALE, c_ref[...], s_ref[...], -1.0).astype(BF16)

    return pl.pallas_call(
        body, name="mla_unrope", grid=(T // tt,),
        in_specs=[pl.BlockSpec((tt, W), lambda i: (i, 0)), pl.BlockSpec((tt, LANES), lambda i: (i, 0)),
                  pl.BlockSpec((tt, LANES), lambda i: (i, 0))],
        out_specs=pl.BlockSpec((tt, W), lambda i: (i, 0)),
        out_shape=jax.ShapeDtypeStruct((T, W), BF16),
        compiler_params=_cparams(("parallel",)),
    )(dqx, cos_t, sin_s)


def _row_specs(tt, D, per, n):
    return [pl.BlockSpec((None, 1, D), lambda i: (i // per, 0, 0)) for _ in range(n)]


def _norm_mod(x, gain, sc, sh, *, S, name):
    T, D = x.shape
    tt = _tile(S, 512)
    per = S // tt

    def body(x_ref, g_ref, sc_ref, sh_ref, h_ref, ht_ref):
        xv = x_ref[...]
        r = lax.rsqrt(jnp.mean(xv * xv, axis=-1, keepdims=True) + NORM_EPS)
        h = (xv * r) * g_ref[...] * (1.0 + sc_ref[...]) + sh_ref[...]
        h_ref[...] = h.astype(BF16)
        ht_ref[...] = h.T.astype(BF16)

    return pl.pallas_call(
        body, name=name, grid=(T // tt,),
        in_specs=[pl.BlockSpec((tt, D), lambda i: (i, 0)), pl.BlockSpec((1, D), lambda i: (0, 0))]
        + _row_specs(tt, D, per, 2),
        out_specs=[pl.BlockSpec((tt, D), lambda i: (i, 0)), pl.BlockSpec((D, tt), lambda i: (0, i))],
        out_shape=[jax.ShapeDtypeStruct((T, D), BF16), jax.ShapeDtypeStruct((D, T), BF16)],
        compiler_params=_cparams(("parallel",)),
    )(x, gain, sc, sh)


def _norm_mod_bwd(x, dh, dres, gain, sc, gate=None, *, S, name):
    T, D = x.shape
    B = T // S
    tt = _tile(S, 512)
    per = S // tt
    n_gate = 0 if gate is None else 2

    def body(*refs):
        x_ref, dh_ref, dres_ref, g_ref, sc_ref = refs[:5]
        dx_ref, dsh_ref, dsc_ref, dg_ref = refs[5 + n_gate:9 + n_gate]
        i = pl.program_id(0)
        xv = x_ref[...]
        dhv = dh_ref[...].astype(F32)
        r = lax.rsqrt(jnp.mean(xv * xv, axis=-1, keepdims=True) + NORM_EPS)
        n = xv * r
        g = g_ref[...]
        one_sc = 1.0 + sc_ref[...]
        dn = dhv * (g * one_sc)
        dxv = dres_ref[...] + r * (dn - n * jnp.mean(dn * n, axis=-1, keepdims=True))
        dx_ref[...] = dxv
        dhn = dhv * n

        @pl.when(i % per == 0)
        def _():
            dsh_ref[...] = jnp.zeros_like(dsh_ref)
            dsc_ref[...] = jnp.zeros_like(dsc_ref)

        @pl.when(i == 0)
        def _():
            dg_ref[...] = jnp.zeros_like(dg_ref)

        dsh_ref[...] += jnp.sum(dhv, axis=0, keepdims=True)
        dsc_ref[...] += jnp.sum(dhn, axis=0, keepdims=True) * g
        dg_ref[...] += jnp.sum(dhn, axis=0, keepdims=True) * one_sc
        if gate is not None:
            y_ref, gate_ref = refs[5:7]
            dy_ref, dgate_ref = refs[9 + n_gate:]
            dy_ref[...] = (dxv * gate_ref[...]).astype(BF16)

            @pl.when(i % per == 0)
            def _():
                dgate_ref[...] = jnp.zeros_like(dgate_ref)

            dgate_ref[...] += jnp.sum(dxv * y_ref[...], axis=0, keepdims=True)

    tile = pl.BlockSpec((tt, D), lambda i: (i, 0))
    in_specs = [tile] * 3 + [pl.BlockSpec((1, D), lambda i: (0, 0))] + _row_specs(tt, D, per, 1)
    out_specs = [tile] + _row_specs(tt, D, per, 2) + [pl.BlockSpec((1, D), lambda i: (0, 0))]
    out_shape = [jax.ShapeDtypeStruct((T, D), F32), jax.ShapeDtypeStruct((B, 1, D), F32),
                 jax.ShapeDtypeStruct((B, 1, D), F32), jax.ShapeDtypeStruct((1, D), F32)]
    operands = [x, dh, dres, gain, sc]
    if gate is not None:
        in_specs += [tile] + _row_specs(tt, D, per, 1)
        out_specs += [tile] + _row_specs(tt, D, per, 1)
        out_shape += [jax.ShapeDtypeStruct((T, D), BF16), jax.ShapeDtypeStruct((B, 1, D), F32)]
        operands += list(gate)
    return pl.pallas_call(
        body, name=name, grid=(T // tt,), in_specs=in_specs, out_specs=out_specs, out_shape=out_shape,
        compiler_params=_cparams(("arbitrary",)),
    )(*operands)


def _final_loss(x, target, gain, z, gate, *, S):
    T, D = x.shape
    B = T // S
    tt = _tile(S, 512)
    per = S // tt

    def body(x_ref, t_ref, g_ref, z_ref, gt_ref, dx_ref, dg_ref, loss_ref, dz_ref, dgt_ref):
        i = pl.program_id(0)
        xv = x_ref[...]
        r = lax.rsqrt(jnp.mean(xv * xv, axis=-1, keepdims=True) + NORM_EPS)
        n = xv * r
        g = g_ref[...]
        err = n * g - t_ref[...]
        dy = err * (1.0 / D)
        dn = dy * g
        dxv = r * (dn - n * jnp.mean(dn * n, axis=-1, keepdims=True))
        dx_ref[...] = dxv
        dz_ref[...] = (dxv * gt_ref[...]).astype(BF16)

        @pl.when(i == 0)
        def _():
            dg_ref[...] = jnp.zeros_like(dg_ref)
            loss_ref[...] = jnp.zeros_like(loss_ref)

        @pl.when(i % per == 0)
        def _():
            dgt_ref[...] = jnp.zeros_like(dgt_ref)

        dg_ref[...] += jnp.sum(dy * n, axis=0, keepdims=True)
        dgt_ref[...] += jnp.sum(dxv * z_ref[...], axis=0, keepdims=True)
        loss_ref[...] += jnp.sum(jnp.sum(err * err, axis=-1, keepdims=True), axis=0, keepdims=True) * (0.5 / D)

    rows = pl.BlockSpec((tt, D), lambda i: (i, 0))
    return pl.pallas_call(
        body, name="final_loss", grid=(T // tt,),
        in_specs=[rows, rows, pl.BlockSpec((1, D), lambda i: (0, 0)), rows] + _row_specs(tt, D, per, 1),
        out_specs=[rows, pl.BlockSpec((1, D), lambda i: (0, 0)), pl.BlockSpec((1, LANES), lambda i: (0, 0)),
                   rows] + _row_specs(tt, D, per, 1),
        out_shape=[jax.ShapeDtypeStruct((T, D), F32), jax.ShapeDtypeStruct((1, D), F32),
                   jax.ShapeDtypeStruct((1, LANES), F32), jax.ShapeDtypeStruct((T, D), BF16),
                   jax.ShapeDtypeStruct((B, 1, D), F32)],
        compiler_params=_cparams(("arbitrary",)),
    )(x, target, gain, z, gate)


def _head_masks(ew):
    lane = lax.broadcasted_iota(jnp.int32, (1, PAIR_Q), 1)
    m0 = (lane < HEAD_DIM) | ((lane >= LANES) & (lane < LANES + ew))
    m1 = ((lane >= HEAD_DIM) & (lane < LANES)) | ((lane >= LANES + ew) & (lane < LANES + 2 * ew))
    return m0, m1


def _dot_nt(a, b):
    return lax.dot_general(a, b, (((1,), (1,)), ((), ())), preferred_element_type=F32)


def _dot_tn(a, b):
    return lax.dot_general(a, b, (((0,), (0,)), ((), ())), preferred_element_type=F32)


def _lane_halves(x, op):
    acc = x[:, 0:LANES]
    for g in range(1, x.shape[1] // LANES):
        acc = op(acc, x[:, g * LANES:(g + 1) * LANES])
    return acc


def _head_rows(cols_lane_replicated):
    t = cols_lane_replicated.T
    sub = lax.broadcasted_iota(jnp.int32, (8, t.shape[1]), 0)
    return jnp.where(sub == 1, t[HEAD_DIM:HEAD_DIM + 8], t[0:8])


def _attn_fwd(qx, kvx, *, S, ew, name):
    T = qx.shape[0]
    P = qx.shape[1] // PAIR_Q
    B = T // S
    tk = _tile(S, ATTN_BLOCK)
    tq = _tile(S, ATTN_Q_ROWS)
    nq = S // tq
    per = tq // tk

    def body(q_ref, kv_ref, o_ref, lse_ref, ot_ref, m_sc, l_sc, acc_sc):
        qi = pl.program_id(2)
        q = q_ref[...]
        masks = _head_masks(ew)
        qh = [jnp.where(m, q, jnp.zeros_like(q)) for m in masks]

        def logits(h, k, diagonal):
            s = _dot_nt(qh[h], k)
            if diagonal is None:
                return s
            row = lax.broadcasted_iota(jnp.int32, s.shape, 0)
            col = lax.broadcasted_iota(jnp.int32, s.shape, 1)
            return jnp.where(col + diagonal * tk <= row, s, NEG_BIG)

        def trip(first, count, n_diagonal=0):
            rows = [pl.ds(pl.multiple_of((first + u) * tk, tk), tk) for u in range(count)]
            diag = [None] * (count - n_diagonal) + list(range(n_diagonal))
            for h in range(2):
                ss = [logits(h, kv_ref[rows[u], 0:PAIR_Q], diag[u]) for u in range(count)]
                m_prev = m_sc[h]
                m_elem = m_prev
                for s in ss:
                    m_elem = jnp.maximum(m_elem, _lane_halves(s, jnp.maximum))
                m_new = jnp.broadcast_to(jnp.max(m_elem, axis=1, keepdims=True), (tq, LANES))
                alpha = jnp.exp(m_prev - m_new)
                l = alpha * l_sc[h]
                acc = alpha * acc_sc[h]
                for u, s in enumerate(ss):
                    p = jnp.concatenate([jnp.exp(s[:, g * LANES:(g + 1) * LANES] - m_new)
                                         for g in range(tk // LANES)], axis=1)
                    l = l + _lane_halves(p, jnp.add)
                    acc = acc + jnp.dot(p.astype(BF16), kv_ref[rows[u], PAIR_Q:PAIR_KV], preferred_element_type=F32)
                m_sc[h] = m_new
                l_sc[h] = l
                acc_sc[h] = acc

        m_sc[...] = jnp.full(m_sc.shape, NEG_BIG, F32)
        l_sc[...] = jnp.zeros_like(l_sc)
        acc_sc[...] = jnp.zeros_like(acc_sc)

        def loop_body(t, carry):
            trip(t * ATTN_UNROLL, ATTN_UNROLL)
            return carry

        below = qi * per
        lax.fori_loop(0, below // ATTN_UNROLL, loop_body, 0)
        for left in range(0, ATTN_UNROLL, per):
            @pl.when(below % ATTN_UNROLL == left)
            def _(left=left):
                trip(below - left, left + per, n_diagonal=per)

        lane = lax.broadcasted_iota(jnp.int32, (tq, LANES), 1)
        lo = lane < HEAD_DIM
        l = [jnp.sum(l_sc[h], axis=1, keepdims=True) for h in range(2)]
        o = jnp.where(lo, acc_sc[0] / l[0], acc_sc[1] / l[1])
        o_ref[...] = o.astype(BF16)
        ot_ref[...] = o.T.astype(BF16)
        lse = jnp.where(lo, m_sc[0] + jnp.log(l[0]), m_sc[1] + jnp.log(l[1]))
        for r in range(per):
            lse_ref[r] = _head_rows(lse[r * tk:(r + 1) * tk])

    return pl.pallas_call(
        body, name=name, grid=(B, P, nq),
        in_specs=[pl.BlockSpec((tq, PAIR_Q), lambda b, p, i: (b * nq + i, p)),
                  pl.BlockSpec((S, PAIR_KV), lambda b, p, i: (b, p))],
        out_specs=[pl.BlockSpec((tq, LANES), lambda b, p, i: (b * nq + i, p)),
                   pl.BlockSpec((per, None, 8, tk), lambda b, p, i: (b * nq + i, p, 0, 0)),
                   pl.BlockSpec((LANES, tq), lambda b, p, i: (p, b * nq + i))],
        out_shape=[jax.ShapeDtypeStruct((T, P * LANES), BF16), jax.ShapeDtypeStruct((T // tk, P, 8, tk), F32),
                   jax.ShapeDtypeStruct((P * LANES, T), BF16)],
        scratch_shapes=[pltpu.VMEM((2, tq, LANES), F32)] * 3,
        compiler_params=_cparams(("parallel", "parallel", "arbitrary")),
    )(qx, kvx)


def _attn_bwd(qx, kvx, o, lse, do, *, S, ew, name, bias_grad=False):
    T = qx.shape[0]
    P = qx.shape[1] // PAIR_Q
    B = T // S
    tq = _tile(S, ATTN_BLOCK)
    tk = _tile(S, ATTN_K_ROWS)
    nq = S // tq
    nk = S // tk
    per = tk // tq

    def body(q_ref, kv_ref, o_ref, lse_ref, do_ref, dq_ref, dkv_ref, *rest):
        kj = pl.program_id(2)
        if bias_grad:
            csum_ref, rsum_ref, dq_sc, delta_sc, dk_sc, dv_sc, cs_sc = rest
            cs_sc[...] = jnp.zeros_like(cs_sc)

            @pl.when(kj == 0)
            def _():
                rsum_ref[...] = jnp.zeros_like(rsum_ref)
        else:
            dq_sc, delta_sc, dk_sc, dv_sc = rest
        masks = _head_masks(ew)
        lo_q = lax.broadcasted_iota(jnp.int32, (tq, LANES), 1) < HEAD_DIM
        lo = lax.broadcasted_iota(jnp.int32, (tk, LANES), 1) < HEAD_DIM
        vmask = [lo, jnp.logical_not(lo)]

        @pl.when(kj == 0)
        def _():
            dq_sc[...] = jnp.zeros_like(dq_sc)
            for c in range(nq):
                rows = pl.ds(c * tq, tq)
                x = do_ref[rows, :].astype(F32) * o_ref[rows, :].astype(F32)
                r0 = jnp.sum(jnp.where(lo_q, x, 0.0), axis=1, keepdims=True)
                r1 = jnp.sum(jnp.where(lo_q, 0.0, x), axis=1, keepdims=True)
                delta_sc[c] = _head_rows(jnp.where(lo_q, r0, r1))

        k = kv_ref[:, 0:PAIR_Q]
        v = kv_ref[:, PAIR_Q:PAIR_KV]
        kh = [jnp.where(m, k, jnp.zeros_like(k)) for m in masks]
        vh = [jnp.where(m, v, jnp.zeros_like(v)) for m in vmask]
        dk_sc[...] = jnp.zeros_like(dk_sc)
        dv_sc[...] = jnp.zeros_like(dv_sc)

        def step(qi, diagonal):
            rows = pl.ds(pl.multiple_of(qi * tq, tq), tq)
            q = q_ref[rows, :]
            dov = do_ref[rows, :]
            lse8 = lse_ref[qi]
            dl8 = delta_sc[qi]
            for h in range(2):
                st = _dot_nt(kh[h], q)
                if diagonal is not None:
                    key = lax.broadcasted_iota(jnp.int32, st.shape, 0)
                    qry = lax.broadcasted_iota(jnp.int32, st.shape, 1)
                    st = jnp.where(key <= qry + diagonal * tq, st, NEG_BIG)
                pt = jnp.exp(st - lse8[h:h + 1, :])
                dpt = _dot_nt(vh[h], dov)
                dst = pt * (dpt - dl8[h:h + 1, :])
                if bias_grad:
                    cs_sc[h] += _lane_halves(dst, jnp.add)
                    rsum_ref[qi, h:h + 1, :] += jnp.sum(dst, axis=0, keepdims=True)
                ptb = pt.astype(BF16)
                dstb = dst.astype(BF16)
                dv_sc[h] += jnp.dot(ptb, dov, preferred_element_type=F32)
                dk_sc[h] += jnp.dot(dstb, q, preferred_element_type=F32)
                dq_sc[rows, :] += _dot_tn(dstb, kh[h])

        first = kj * per
        above = nq - per - first
        for left in range(0, ATTN_UNROLL, per):
            @pl.when(above % ATTN_UNROLL == left)
            def _(left=left):
                for d in range(per):
                    step(first + d, d)
                for u in range(left):
                    step(first + per + u, None)

        def loop_body(t, carry):
            for u in range(ATTN_UNROLL):
                step(first + per + above % ATTN_UNROLL + t * ATTN_UNROLL + u, None)
            return carry

        lax.fori_loop(0, above // ATTN_UNROLL, loop_body, 0)
        dkv_ref[:, 0:PAIR_Q] = (jnp.where(masks[0], dk_sc[0], 0.0) + jnp.where(masks[1], dk_sc[1], 0.0)).astype(BF16)
        dkv_ref[:, PAIR_Q:PAIR_KV] = jnp.where(lo, dv_sc[0], dv_sc[1]).astype(BF16)
        if bias_grad:
            csum_ref[...] = jnp.where(lo, jnp.sum(cs_sc[0], axis=1, keepdims=True),
                                      jnp.sum(cs_sc[1], axis=1, keepdims=True))

        @pl.when(kj == nk - 1)
        def _():
            dq_ref[...] = dq_sc[...].astype(BF16)

    rows_spec = pl.BlockSpec((nq, None, 8, tq), lambda b, p, j: (b, p, 0, 0))
    out_specs = [pl.BlockSpec((S, PAIR_Q), lambda b, p, j: (b, p)),
                 pl.BlockSpec((tk, PAIR_KV), lambda b, p, j: (b * nk + j, p))]
    out_shape = [jax.ShapeDtypeStruct((T, P * PAIR_Q), BF16), jax.ShapeDtypeStruct((T, P * PAIR_KV), BF16)]
    scratch = [pltpu.VMEM((S, PAIR_Q), F32), pltpu.VMEM((nq, 8, tq), F32),
               pltpu.VMEM((2, tk, PAIR_Q), F32), pltpu.VMEM((2, tk, LANES), F32)]
    if bias_grad:
        out_specs += [pl.BlockSpec((tk, LANES), lambda b, p, j: (b * nk + j, p)), rows_spec]
        out_shape += [jax.ShapeDtypeStruct((T, P * LANES), F32), jax.ShapeDtypeStruct((T // tq, P, 8, tq), F32)]
        scratch.append(pltpu.VMEM((2, tk, LANES), F32))
    return pl.pallas_call(
        body, name=name, grid=(B, P, nk),
        in_specs=[pl.BlockSpec((S, PAIR_Q), lambda b, p, j: (b, p)),
                  pl.BlockSpec((tk, PAIR_KV), lambda b, p, j: (b * nk + j, p)),
                  pl.BlockSpec((S, LANES), lambda b, p, j: (b, p)), rows_spec,
                  pl.BlockSpec((S, LANES), lambda b, p, j: (b, p))],
        out_specs=out_specs, out_shape=out_shape, scratch_shapes=scratch,
        compiler_params=_cparams(("parallel", "parallel", "arbitrary")),
    )(qx, kvx, o, lse, do)


def _fox_consts(P):
    H = 2 * P
    eq = np.zeros((3 * LANES, P * LANES), np.float32)
    ek = np.zeros((3 * LANES, P * LANES), np.float32)
    ones_q = np.zeros((1, P * LANES), np.float32)
    ones_k = np.zeros((1, P * LANES), np.float32)
    for h in range(H):
        base = (h // 2) * LANES + FOX_EXTRA * (h % 2)
        for part in range(3):
            eq[part * LANES + h, base + part] = 1.0
            ones_q[0, base + 3 + part] = 1.0
            ones_k[0, base + part] = 1.0
            ek[part * LANES + h, base + 3 + part] = -1.0
    return eq, ek, ones_q, ones_k


def _split3(f):
    hi = f.astype(BF16)
    r = f - hi.astype(F32)
    mid = r.astype(BF16)
    lo = (r - mid.astype(F32)).astype(BF16)
    return hi, mid, lo


def _tri_sum(tri, x):
    hi, mid, lo = _split3(x)
    return (jnp.dot(tri, hi, preferred_element_type=F32) + jnp.dot(tri, mid, preferred_element_type=F32)
            + jnp.dot(tri, lo, preferred_element_type=F32))


def _log1p_pos(e):
    return jnp.where(e < 0.01, e * (1.0 - e * (0.5 - e * (1.0 / 3.0))), jnp.log(1.0 + e))


def _fox_prep(qkv, fl, b_row, *, S, D, name):
    T = qkv.shape[0]
    P = D // LANES
    B = T // S
    tt = _tile(S, 256)
    per = S // tt
    eq, ek, ones_q, ones_k = _fox_consts(P)
    q_scale = HEAD_DIM ** -0.5

    def body(q_ref, k_ref, v_ref, fl_ref, b_ref, eq_ref, ek_ref, oq_ref, ok_ref, qx_ref, kvx_ref, carry):
        i = pl.program_id(1)

        @pl.when(i == 0)
        def _():
            carry[...] = jnp.zeros_like(carry)

        z = fl_ref[...] + b_ref[...]
        logf = jnp.minimum(z, 0.0) - _log1p_pos(jnp.exp(-jnp.abs(z)))
        row = lax.broadcasted_iota(jnp.int32, (tt, tt), 0)
        col = lax.broadcasted_iota(jnp.int32, (tt, tt), 1)
        tri = (col <= row).astype(BF16)
        f = _tri_sum(tri, logf) + carry[...]
        carry[...] = f[tt - 1:tt, :]
        parts = jnp.concatenate(_split3(f), axis=1)
        xq = jnp.dot(parts, eq_ref[...], preferred_element_type=F32) + oq_ref[...]
        xk = jnp.dot(parts, ek_ref[...], preferred_element_type=F32) + ok_ref[...]
        for p in range(P):
            c = slice(p * LANES, (p + 1) * LANES)
            qx_ref[:, p * PAIR_Q:p * PAIR_Q + LANES] = (q_ref[:, c].astype(F32) * q_scale).astype(BF16)
            qx_ref[:, p * PAIR_Q + LANES:(p + 1) * PAIR_Q] = xq[:, c].astype(BF16)
            kvx_ref[:, p * PAIR_KV:p * PAIR_KV + LANES] = k_ref[:, c]
            kvx_ref[:, p * PAIR_KV + LANES:p * PAIR_KV + PAIR_Q] = xk[:, c].astype(BF16)
            kvx_ref[:, p * PAIR_KV + PAIR_Q:(p + 1) * PAIR_KV] = v_ref[:, c]

    tok = lambda b, i: (b * per + i, 0)
    const = lambda b, i: (0, 0)
    return pl.pallas_call(
        body, name=name, grid=(B, per),
        in_specs=[pl.BlockSpec((tt, D), lambda b, i: (b * per + i, 0)),
                  pl.BlockSpec((tt, D), lambda b, i: (b * per + i, 1)),
                  pl.BlockSpec((tt, D), lambda b, i: (b * per + i, 2)),
                  pl.BlockSpec((tt, LANES), tok), pl.BlockSpec((1, LANES), const),
                  pl.BlockSpec(eq.shape, const), pl.BlockSpec(ek.shape, const),
                  pl.BlockSpec(ones_q.shape, const), pl.BlockSpec(ones_k.shape, const)],
        out_specs=[pl.BlockSpec((tt, P * PAIR_Q), tok), pl.BlockSpec((tt, P * PAIR_KV), tok)],
        out_shape=[jax.ShapeDtypeStruct((T, P * PAIR_Q), BF16), jax.ShapeDtypeStruct((T, P * PAIR_KV), BF16)],
        scratch_shapes=[pltpu.VMEM((1, LANES), F32)],
        compiler_params=_cparams(("arbitrary", "arbitrary")),
    )(qkv, qkv, qkv, fl, b_row, jnp.asarray(eq, BF16), jnp.asarray(ek, BF16), jnp.asarray(ones_q), jnp.asarray(ones_k))


def _fox_unprep(dqx, dkvx, csum, rsum, fl, b_row, *, S, D, name):
    T = dqx.shape[0]
    P = D // LANES
    B = T // S
    tt = _tile(S, 256)
    per = S // tt
    q_scale = HEAD_DIM ** -0.5

    def body(dq_ref, dkv_ref, cs_ref, rs_ref, fl_ref, b_ref, dqkv_ref, db_ref, carry):
        b = pl.program_id(0)
        i = pl.program_id(1)

        @pl.when(i == 0)
        def _():
            carry[...] = jnp.zeros_like(carry)

        @pl.when((i == 0) & (b == 0))
        def _():
            db_ref[...] = jnp.zeros_like(db_ref)

        df = rs_ref[...] - cs_ref[...]
        for p in range(P):
            rq = slice(p * LANES, (p + 1) * LANES)
            dqkv_ref[:, rq] = (dq_ref[:, p * PAIR_Q:p * PAIR_Q + LANES].astype(F32) * q_scale).astype(BF16)
            dqkv_ref[:, D + p * LANES:D + (p + 1) * LANES] = dkv_ref[:, p * PAIR_KV:p * PAIR_KV + LANES]
            dqkv_ref[:, 2 * D + p * LANES:2 * D + (p + 1) * LANES] = dkv_ref[:, p * PAIR_KV + PAIR_Q:(p + 1) * PAIR_KV]
        row = lax.broadcasted_iota(jnp.int32, (tt, tt), 0)
        col = lax.broadcasted_iota(jnp.int32, (tt, tt), 1)
        tri = (col >= row).astype(BF16)
        dlogf = _tri_sum(tri, df) + carry[...]
        carry[...] = dlogf[0:1, :]
        z = fl_ref[...] + b_ref[...]
        e = jnp.exp(-jnp.abs(z))
        sig_neg = jnp.where(z >= 0.0, e, 1.0) / (1.0 + e)
        dfl = dlogf * sig_neg
        dqkv_ref[:, 3 * D:3 * D + LANES] = dfl.astype(BF16)
        db_ref[...] += jnp.sum(dfl, axis=0, keepdims=True)

    rev = lambda b, i: (b * per + per - 1 - i, 0)
    const = lambda b, i: (0, 0)
    return pl.pallas_call(
        body, name=name, grid=(B, per),
        in_specs=[pl.BlockSpec((tt, P * PAIR_Q), rev), pl.BlockSpec((tt, P * PAIR_KV), rev),
                  pl.BlockSpec((tt, LANES), rev), pl.BlockSpec((tt, LANES), rev), pl.BlockSpec((tt, LANES), rev),
                  pl.BlockSpec((1, LANES), const)],
        out_specs=[pl.BlockSpec((tt, 3 * D + LANES), rev), pl.BlockSpec((1, LANES), const)],
        out_shape=[jax.ShapeDtypeStruct((T, 3 * D + LANES), BF16), jax.ShapeDtypeStruct((1, LANES), F32)],
        scratch_shapes=[pltpu.VMEM((1, LANES), F32)],
        compiler_params=_cparams(("arbitrary", "arbitrary")),
    )(dqx, dkvx, csum, rsum, fl, b_row)


def _rms(x):
    r = lax.rsqrt(jnp.mean(x * x, axis=-1, keepdims=True) + NORM_EPS)
    return x * r, r


def _mla_mid(lat, gq, gkv, cos_t, sin_s, *, name):
    T, W = lat.shape
    Rq = W - 2 * LANES
    tt = _tile(T, 512)

    def body(l_ref, gq_ref, gkv_ref, c_ref, s_ref, o_ref, ot_ref):
        nq, _ = _rms(l_ref[:, 0:Rq])
        nkv, _ = _rms(l_ref[:, Rq:Rq + LANES])
        parts = [nq * gq_ref[...], nkv * gkv_ref[...], _rope128(l_ref[:, Rq + LANES:W], c_ref[...], s_ref[...])]
        out = jnp.concatenate(parts, axis=1)
        o_ref[...] = out.astype(BF16)
        ot_ref[...] = out.T.astype(BF16)

    return pl.pallas_call(
        body, name=name, grid=(T // tt,),
        in_specs=[pl.BlockSpec((tt, W), lambda i: (i, 0)), pl.BlockSpec((1, Rq), lambda i: (0, 0)),
                  pl.BlockSpec((1, LANES), lambda i: (0, 0)), pl.BlockSpec((tt, LANES), lambda i: (i, 0)),
                  pl.BlockSpec((tt, LANES), lambda i: (i, 0))],
        out_specs=[pl.BlockSpec((tt, W), lambda i: (i, 0)), pl.BlockSpec((W, tt), lambda i: (0, i))],
        out_shape=[jax.ShapeDtypeStruct((T, W), BF16), jax.ShapeDtypeStruct((W, T), BF16)],
        compiler_params=_cparams(("parallel",)),
    )(lat, gq, gkv, cos_t, sin_s)


def _mla_mid_bwd(lat, dcq, dckr, gq, gkv, cos_t, sin_s, *, name):
    T, W = lat.shape
    Rq = W - 2 * LANES
    tt = _tile(T, 512)

    def norm_bwd(x, dy, g):
        n, r = _rms(x)
        dn = dy * g
        return r * (dn - n * jnp.mean(dn * n, axis=-1, keepdims=True)), jnp.sum(dy * n, axis=0, keepdims=True)

    def body(l_ref, dq_ref, dk_ref, gq_ref, gkv_ref, c_ref, s_ref, o_ref, dgq_ref, dgkv_ref):
        i = pl.program_id(0)

        @pl.when(i == 0)
        def _():
            dgq_ref[...] = jnp.zeros_like(dgq_ref)
            dgkv_ref[...] = jnp.zeros_like(dgkv_ref)

        dxq, dgq = norm_bwd(l_ref[:, 0:Rq], dq_ref[...], gq_ref[...])
        dxkv, dgkv = norm_bwd(l_ref[:, Rq:Rq + LANES], dk_ref[:, 0:LANES], gkv_ref[...])
        o_ref[:, 0:Rq] = dxq.astype(BF16)
        o_ref[:, Rq:Rq + LANES] = dxkv.astype(BF16)
        o_ref[:, Rq + LANES:W] = _rope128(dk_ref[:, LANES:2 * LANES], c_ref[...], -s_ref[...]).astype(BF16)
        dgq_ref[...] += dgq
        dgkv_ref[...] += dgkv

    return pl.pallas_call(
        body, name=name, grid=(T // tt,),
        in_specs=[pl.BlockSpec((tt, W), lambda i: (i, 0)), pl.BlockSpec((tt, Rq), lambda i: (i, 0)),
                  pl.BlockSpec((tt, 2 * LANES), lambda i: (i, 0)), pl.BlockSpec((1, Rq), lambda i: (0, 0)),
                  pl.BlockSpec((1, LANES), lambda i: (0, 0)), pl.BlockSpec((tt, LANES), lambda i: (i, 0)),
                  pl.BlockSpec((tt, LANES), lambda i: (i, 0))],
        out_specs=[pl.BlockSpec((tt, W), lambda i: (i, 0)), pl.BlockSpec((1, Rq), lambda i: (0, 0)),
                   pl.BlockSpec((1, LANES), lambda i: (0, 0))],
        out_shape=[jax.ShapeDtypeStruct((T, W), BF16), jax.ShapeDtypeStruct((1, Rq), F32),
                   jax.ShapeDtypeStruct((1, LANES), F32)],
        compiler_params=_cparams(("arbitrary",)),
    )(lat, dcq, dckr, gq, gkv, cos_t, sin_s)


def _uq_to_pairs(w):
    Rq = w.shape[0]
    P = w.shape[1] // (2 * (HEAD_DIM + ROPE_DIM))
    w4 = w.reshape(Rq, P, 2, HEAD_DIM + ROPE_DIM)
    nope = w4[..., :HEAD_DIM].reshape(Rq, P, 2 * HEAD_DIM)
    rope = w4[..., HEAD_DIM:].reshape(Rq, P, 2 * ROPE_DIM)
    pad = jnp.zeros((Rq, P, PAIR_Q - 2 * HEAD_DIM - 2 * ROPE_DIM), w.dtype)
    return jnp.concatenate([nope, rope, pad], axis=-1).reshape(Rq, P * PAIR_Q)


def _uq_from_pairs(g):
    Rq = g.shape[0]
    P = g.shape[1] // PAIR_Q
    g3 = g.reshape(Rq, P, PAIR_Q)
    nope = g3[..., :2 * HEAD_DIM].reshape(Rq, P, 2, HEAD_DIM)
    rope = g3[..., 2 * HEAD_DIM:2 * HEAD_DIM + 2 * ROPE_DIM].reshape(Rq, P, 2, ROPE_DIM)
    return jnp.concatenate([nope, rope], axis=-1).reshape(Rq, P * 2 * (HEAD_DIM + ROPE_DIM))


def _ukv_to_pairs(w):
    P = w.shape[1] // (4 * HEAD_DIM)
    w4 = w.reshape(KV_RANK, P, 2, 2 * HEAD_DIM)
    kn = w4[..., :HEAD_DIM].reshape(KV_RANK, P, 2 * HEAD_DIM)
    vv = w4[..., HEAD_DIM:].reshape(KV_RANK, P, 2 * HEAD_DIM)
    top = jnp.concatenate([kn, jnp.zeros((KV_RANK, P, LANES), w.dtype), vv], axis=-1)
    place = np.zeros((LANES, P, PAIR_KV), np.float32)
    for r in range(ROPE_DIM):
        place[r, :, LANES + r] = 1.0
        place[r, :, LANES + ROPE_DIM + r] = 1.0
    return jnp.concatenate([top, jnp.asarray(place, w.dtype)], axis=0).reshape(KV_RANK + LANES, P * PAIR_KV)


def _ukv_from_pairs(g):
    P = g.shape[1] // PAIR_KV
    g3 = g[:KV_RANK].reshape(KV_RANK, P, PAIR_KV)
    kn = g3[..., :2 * HEAD_DIM].reshape(KV_RANK, P, 2, HEAD_DIM)
    vv = g3[..., PAIR_Q:].reshape(KV_RANK, P, 2, HEAD_DIM)
    return jnp.concatenate([kn, vv], axis=-1).reshape(KV_RANK, P * 4 * HEAD_DIM)


def _residual_then_norm(acc, xr, g, gain, sc, sh):
    x_out = xr + g * acc
    r = lax.rsqrt(jnp.mean(x_out * x_out, axis=-1, keepdims=True) + NORM_EPS)
    h = (x_out * r) * gain * (1.0 + sc) + sh
    return x_out, acc, h, h


def _gated_out(a, w_stack, layer, x, gate, next_norm, *, S, name):
    if next_norm is None:
        return _mm(a, w_stack, "nn", name=name, b_layer=layer, out_dtypes=(F32, BF16), extras=(x,), rowvecs=(gate,),
                   seq=S, epilogue=lambda acc, xr, g: (xr + g * acc, acc)) + (None, None)
    long_k = a.shape[1] > 2048
    return _mm(a, w_stack, "nn", name=name, b_layer=layer, out_dtypes=(F32, BF16, BF16, BF16),
               out_t=(False, False, False, True), extras=(x,), rowvecs=(gate,) + tuple(next_norm), seq=S,
               full_rows=True, tk=a.shape[1], epilogue=_residual_then_norm,
               vmem_budget=MM_VMEM_BUDGET + (8 * 1024 * 1024 if long_k else 0))


def _mlp_fwd(h2, w, i, x1, gate, next_norm, *, S):
    def act(acc):
        u = jnp.square(jnp.maximum(acc, 0.0))
        return acc, u, u

    p, u, u_t = _mm(h2, w["mlp_w1"], "nn", name=f"mlp_up_{i}", b_layer=i, out_dtypes=(BF16, BF16, BF16),
                    out_t=(False, False, True), epilogue=act)
    x2, z, h, h_t = _gated_out(u, w["mlp_w2"], i, x1, gate, next_norm, S=S, name=f"mlp_down_{i}")
    return x2, (p, u_t, z), h, h_t


STACKED_GRADS = ("fox_out", "mla_down", "mla_uq", "mla_ukv", "mla_out", "mlp_w1", "mlp_w2")


def _local_step(x, target, pos_f, inv_freq_row, sign_row, mod, w, slots, *, S, hooks=None):
    hooks = hooks or {}
    T, D = x.shape
    L = mod.shape[0]
    L2 = len(w["fox_out"])
    cos_t, sin_s = _rope_tables(pos_f, inv_freq_row, sign_row)
    saved = []
    B = mod.shape[2]

    def per_sequence(gain):
        return jnp.broadcast_to(gain[None], (B,) + gain.shape)

    h, h_t = _norm_mod(x, w["norm_mix_g"][0], mod[0, 1], mod[0, 0], S=S, name="norm_mix_0")
    for i in range(L):
        j = i // 2
        sh_m, sc_m, g_m, sh_f, sc_f, g_f = (mod[i, s] for s in range(6))
        if i % 2 == 0:
            qkv = _mm(h, w["fox_qkv"], "nn", name=f"fox_qkv_{i}", b_layer=j, out_dtypes=(BF16,))
            fl = _mm(h, w["fox_f"], "nn", name=f"fox_f_{i}", b_layer=j)
            qx, kvx = _fox_prep(qkv, fl, w["fox_b"][j], S=S, D=D, name=f"fox_prep_{i}")
            o, lse, o_t = _attn_fwd(qx, kvx, S=S, ew=FOX_EXTRA, name=f"fox_attn_{i}")
            mix = (qx, kvx, o, lse, o_t, fl)
            w_out = w["fox_out"]
        else:
            lat = _mm(h, w["mla_down"], "nn", name=f"mla_down_{i}", b_layer=j)
            Rq = lat.shape[1] - 2 * LANES
            cqr, cqr_t = _mla_mid(lat, w["mla_gq"][j], w["mla_gkv"][j], cos_t, sin_s, name=f"mla_mid_{i}")
            qx = _mm(cqr, w["mla_uq"], "nn", name=f"mla_uq_{i}", b_layer=j, out_dtypes=(BF16,), a_sz=Rq, tk=Rq,
                     tables=(cos_t, sin_s), epilogue=lambda acc, c, s: (_rope_pairs(acc * MLA_SCALE, c, s, 1.0),))
            kvx = _mm(cqr, w["mla_ukv"], "nn", name=f"mla_ukv_{i}", b_layer=j, out_dtypes=(BF16,), a_off=Rq,
                      a_sz=2 * LANES, tk=2 * LANES, tn=PAIR_KV)
            o, lse, o_t = _attn_fwd(qx, kvx, S=S, ew=ROPE_DIM, name=f"mla_attn_{i}")
            mix = (qx, kvx, o, lse, o_t, lat, cqr_t)
            w_out = w["mla_out"]
        x1, y, h2, h2_t = _gated_out(o, w_out, j, x, g_m, (per_sequence(w["norm_mlp_g"][i]), sc_f, sh_f), S=S,
                                     name=f"mix_out_{i}")
        if i == 0 and "fwd_mlp0" in hooks:
            w = hooks["fwd_mlp0"](x1, w)
        next_norm = (per_sequence(w["norm_mix_g"][i + 1]), mod[i + 1, 1], mod[i + 1, 0]) if i + 1 < L else None
        x2, mlp, h_next, h_next_t = _mlp_fwd(h2, w, i, x1, g_f, next_norm, S=S)
        saved.append((x, h_t, mix, y, x1, h2_t, mlp))
        x, h, h_t = x2, h_next, h_next_t
        if i == 0 and "fwd_layer1" in hooks:
            w = hooks["fwd_layer1"](x, w)

    dx, dg_final, loss, dz_below, dg_f_below = _final_loss(x, target, w["final_norm_g"], saved[L - 1][6][2],
                                                           mod[L - 1, 5], S=S)
    n_split = w["mlp_w1"][0][0].shape[1]

    grads = {k: [None] * len(w[k]) for k in ("norm_mix_g", "norm_mlp_g", "fox_b", "mla_gq", "mla_gkv")}
    grads["fox_in"] = [None] * L2
    grads.update({k: {} for k in STACKED_GRADS})
    grads["final_norm_g"] = dg_final

    def stacked(key, layer, _, a_t, b, **kw):
        group, idx, count = slots[(key, layer)]
        grads[key][group] = _mm(a_t, b, "nn", out_stack=(grads[key].get(group), idx, count), **kw)

    dmod = [None] * L
    for i in reversed(range(L)):
        j = i // 2
        x0, h_t, mix, y, x1, h2_t, (p, u_t, z) = saved[i]
        sh_m, sc_m, g_m, sh_f, sc_f, g_f = (mod[i, s] for s in range(6))
        dz, dg_f = dz_below, dg_f_below
        stacked("mlp_w2", i, L, u_t, dz, name=f"mlp_w2_grad_{i}")
        dp = _mm(dz, w["mlp_w2"], "nt", name=f"mlp_down_bwd_{i}", b_layer=i, out_dtypes=(BF16,), extras=(p,),
                 epilogue=lambda acc, pv: (acc * (2.0 * jnp.maximum(pv.astype(F32), 0.0)),))
        stacked("mlp_w1", i, L, h2_t, dp, name=f"mlp_w1_grad_{i}", out_split=n_split)
        if i == 0 and "bwd_mix0" in hooks:
            g_m = g_m + hooks["bwd_mix0"](grads)[0, 0]
        dh2 = _mm(dp, w["mlp_w1"], "nt", name=f"mlp_up_bwd_{i}", b_layer=i, out_dtypes=(BF16,))
        dx1, dsh_f, dsc_f, dgn, dy, dg_m = _norm_mod_bwd(x1, dh2, dx, w["norm_mlp_g"][i], sc_f, gate=(y, g_m), S=S,
                                                         name=f"norm_mlp_bwd_{i}")
        grads["norm_mlp_g"][i] = dgn
        if i % 2 == 0:
            qx, kvx, o, lse, o_t, fl = mix
            stacked("fox_out", j, L2, o_t, dy, name=f"fox_out_grad_{i}")
            do = _mm(dy, w["fox_out"], "nt", name=f"fox_out_bwd_{i}", b_layer=j, out_dtypes=(BF16,))
            dqx, dkvx, csum, rsum = _attn_bwd(qx, kvx, o, lse, do, S=S, ew=FOX_EXTRA, name=f"fox_attn_bwd_{i}",
                                              bias_grad=True)
            n_heads = D // HEAD_DIM
            csum = jnp.pad(csum.reshape(T, n_heads, HEAD_DIM)[:, :, 0], ((0, 0), (0, LANES - n_heads)))
            rsum = jnp.transpose(rsum[:, :, :2, :], (0, 3, 1, 2)).reshape(T, n_heads)
            rsum = jnp.pad(rsum, ((0, 0), (0, LANES - n_heads)))
            dproj, db = _fox_unprep(dqx, dkvx, csum, rsum, fl, w["fox_b"][j], S=S, D=D, name=f"fox_unprep_{i}")
            grads["fox_b"][j] = db
            grads["fox_in"][j] = _mm(h_t, dproj, "nn", name=f"fox_in_grad_{i}")
            dh = _mm(dproj, w["fox_in"], "nt", name=f"fox_in_bwd_{i}", b_layer=j, out_dtypes=(BF16,),
                     tk=dproj.shape[1])
        else:
            qx, kvx, o, lse, o_t, lat, cqr_t = mix
            Rq = lat.shape[1] - 2 * LANES
            stacked("mla_out", j, L2, o_t, dy, name=f"mla_out_grad_{i}")
            do = _mm(dy, w["mla_out"], "nt", name=f"mla_out_bwd_{i}", b_layer=j, out_dtypes=(BF16,))
            dqx, dkvx = _attn_bwd(qx, kvx, o, lse, do, S=S, ew=ROPE_DIM, name=f"mla_attn_bwd_{i}")
            dqpre = _unrope(dqx, cos_t, sin_s)
            stacked("mla_uq", j, L2, cqr_t[:Rq], dqpre, name=f"mla_uq_grad_{i}", out_split=n_split)
            stacked("mla_ukv", j, L2, cqr_t[Rq:], dkvx, name=f"mla_ukv_grad_{i}", tn=PAIR_KV, out_split=n_split)
            dcq = _mm(dqpre, w["mla_uq"], "nt", name=f"mla_uq_bwd_{i}", b_layer=j)
            dckr = _mm(dkvx, w["mla_ukv"], "nt", name=f"mla_ukv_bwd_{i}", b_layer=j, tk=PAIR_KV * 2)
            dlat, dgq, dgkv = _mla_mid_bwd(lat, dcq, dckr, w["mla_gq"][j], w["mla_gkv"][j], cos_t, sin_s,
                                           name=f"mla_mid_bwd_{i}")
            grads["mla_gq"][j] = dgq
            grads["mla_gkv"][j] = dgkv
            stacked("mla_down", j, L2, h_t, dlat, name=f"mla_down_grad_{i}")
            dh = _mm(dlat, w["mla_down"], "nt", name=f"mla_down_bwd_{i}", b_layer=j, out_dtypes=(BF16,))
        if i == 0:
            dx, dsh_m, dsc_m, dgn = _norm_mod_bwd(x0, dh, dx1, w["norm_mix_g"][i], sc_m, S=S, name=f"norm_mix_bwd_{i}")
        else:
            gate_below = mod[i - 1, 5]
            if i == 1 and "bwd_layer0" in hooks:
                gate_below = gate_below + hooks["bwd_layer0"](grads)[0, 0]
            dx, dsh_m, dsc_m, dgn, dz_below, dg_f_below = _norm_mod_bwd(
                x0, dh, dx1, w["norm_mix_g"][i], sc_m, gate=(saved[i - 1][6][2], gate_below), S=S,
                name=f"norm_mix_bwd_{i}")
        grads["norm_mix_g"][i] = dgn
        dmod[i] = jnp.stack([dsh_m, dsc_m, dg_m, dsh_f, dsc_f, dg_f])
    return loss, dx, jnp.stack(dmod), grads


GATHERED = ("fox_in", "fox_out", "mla_down", "mla_uq", "mla_ukv", "mla_out", "mlp_w1", "mlp_w2")
ROW_SHARDED = ("fox_out", "mla_down", "mla_out", "mlp_w2")


def _shard_layouts(wts):
    dkv = wts["mla_w_dkv"]
    dkv = jnp.pad(dkv, ((0, 0), (0, 0), (0, 2 * LANES - dkv.shape[2])))
    return {
        "fox_in": _pad_lanes(wts["fox_w_in"].astype(BF16)),
        "fox_out": wts["fox_w_out"].astype(BF16),
        "mla_down": jnp.concatenate([wts["mla_w_dq"], dkv], axis=2).astype(BF16),
        "mla_uq": jax.vmap(_uq_to_pairs)(wts["mla_w_uq"].astype(BF16)),
        "mla_ukv": jax.vmap(_ukv_to_pairs)(wts["mla_w_ukv"].astype(BF16)),
        "mla_out": wts["mla_w_out"].astype(BF16),
        "mlp_w1": wts["mlp_w1"].astype(BF16),
        "mlp_w2": wts["mlp_w2"].astype(BF16),
    }


def _small_layouts(small):
    return {
        "fox_b": [jnp.pad(b, (0, LANES - b.shape[0]))[None, :] for b in small["fox_b_f"]],
        "mla_gq": [g[None, :] for g in small["mla_q_norm_g"]],
        "mla_gkv": [g[None, :] for g in small["mla_kv_norm_g"]],
        "norm_mix_g": [g[None, :] for g in small["norm_mix_g"]],
        "norm_mlp_g": [g[None, :] for g in small["norm_mlp_g"]],
        "final_norm_g": small["final_norm_g"][None, :],
    }


def _comm_groups(L, L2):
    rest = [("fox_in", 1, L2 - 1), ("fox_out", 1, L2 - 1), ("mla_down", 0, L2), ("mla_uq", 0, L2),
            ("mla_ukv", 0, L2), ("mla_out", 0, L2), ("mlp_w1", 1, L - 1), ("mlp_w2", 1, L - 1)]
    return {"mix0": [("fox_in", 0, 1), ("fox_out", 0, 1)], "mlp0": [("mlp_w1", 0, 1), ("mlp_w2", 0, 1)],
            "rest": [e for e in rest if e[2] > 0]}


def _layer_slots(groups):
    return {(n, s + l): (g, l, cnt) for g, entries in groups.items() for n, s, cnt in entries for l in range(cnt)}


def _pad_lanes(a):
    cols = a.shape[-1]
    return jnp.pad(a, [(0, 0)] * (a.ndim - 1) + [(0, -cols % LANES)])


def _weight_views(name, gathered, D, n_fox_heads):
    n, ns, rows, cols = gathered.shape
    if name == "fox_in":
        true_cols = (3 * D + n_fox_heads) // ns
        fox = jnp.concatenate([gathered[:, k, :, :true_cols] for k in range(ns)], axis=-1)
        return {"fox_qkv": fox[:, :, :3 * D], "fox_f": _pad_lanes(fox[:, :, 3 * D:]), "fox_in": _pad_lanes(fox)}
    if name in ROW_SHARDED:
        return {name: gathered.reshape(n, ns * rows, cols)}
    return {name: gathered}


def _grad_pieces(name, g, qkv_f, n_fox_heads, ns):
    if name == "fox_in":
        D = qkv_f[0].shape[0]
        fox = jnp.stack([a[:, :3 * D + n_fox_heads] for a in qkv_f])
        cols = fox.shape[2] // ns
        return jnp.stack([_pad_lanes(fox[:, :, k * cols:(k + 1) * cols]) for k in range(ns)], axis=1)
    if name in ROW_SHARDED:
        return g.reshape(g.shape[0], ns, g.shape[1] // ns, g.shape[2])
    return g


def _small_grads(g, n_fox_heads):
    return {
        "norm_mix_g": jnp.concatenate(g["norm_mix_g"], axis=0),
        "norm_mlp_g": jnp.concatenate(g["norm_mlp_g"], axis=0),
        "final_norm_g": g["final_norm_g"][0],
        "fox_b_f": jnp.concatenate(g["fox_b"], axis=0)[:, :n_fox_heads],
        "mla_q_norm_g": jnp.concatenate(g["mla_gq"], axis=0),
        "mla_kv_norm_g": jnp.concatenate(g["mla_gkv"], axis=0),
    }


def _silu(c):
    return c * (1.0 / (1.0 + jnp.exp(-c)))


def _ada_fwd(c_all, ada_w, ada_b_cols):
    L, D, C = ada_w.shape
    Bg = c_all.shape[0]
    tc = _tile(C, 512)

    def body(c_ref, w_ref, b_ref, o_ref):
        ca = _silu(c_ref[...]).astype(BF16)
        o_ref[...] = jnp.dot(ca, w_ref[...].astype(BF16), preferred_element_type=F32) + b_ref[...]

    return pl.pallas_call(
        body, name="ada_fwd", grid=(L, C // tc),
        in_specs=[pl.BlockSpec((Bg, D), lambda l, j: (0, 0)), pl.BlockSpec((None, D, tc), lambda l, j: (l, 0, j)),
                  pl.BlockSpec((None, 1, tc), lambda l, j: (l, 0, j))],
        out_specs=pl.BlockSpec((None, Bg, tc), lambda l, j: (l, 0, j)),
        out_shape=jax.ShapeDtypeStruct((L, Bg, C), F32),
        compiler_params=_cparams(("parallel", "parallel")),
    )(c_all, ada_w, ada_b_cols)


def _ada_bwd(c_all, dmod_cols):
    L, Bg, C = dmod_cols.shape
    D = c_all.shape[1]
    tc = _tile(C, 512)

    def body(c_ref, d_ref, o_ref):
        ca = _silu(c_ref[...]).astype(BF16)
        o_ref[...] = _dot_tn(ca, d_ref[...].astype(BF16))

    return pl.pallas_call(
        body, name="ada_bwd", grid=(L, C // tc),
        in_specs=[pl.BlockSpec((Bg, D), lambda l, j: (0, 0)), pl.BlockSpec((None, Bg, tc), lambda l, j: (l, 0, j))],
        out_specs=pl.BlockSpec((None, D, tc), lambda l, j: (l, 0, j)),
        out_shape=jax.ShapeDtypeStruct((L, D, C), F32),
        compiler_params=_cparams(("parallel", "parallel")),
    )(c_all, dmod_cols)


def _adamw_update(w, gv, m, v):
    mn = ADAM_B1 * m + (1.0 - ADAM_B1) * gv
    vn = ADAM_B2 * v + (1.0 - ADAM_B2) * jnp.square(gv)
    m_hat = mn / (1.0 - ADAM_B1 ** ADAM_STEP)
    v_hat = vn / (1.0 - ADAM_B2 ** ADAM_STEP)
    return -ADAM_LR * (m_hat / (jnp.sqrt(v_hat) + ADAM_EPS) + ADAM_WD * w), mn, vn


def _adamw(w, g, m, v, *, name):
    shape = w.shape
    C = shape[-1]
    R = int(np.prod(shape[:-1])) if len(shape) > 1 else 1
    w2, g2, m2, v2 = (a.reshape(R, C) for a in (w, g, m, v))
    tr = _row_tile(R, C)

    def body(w_ref, g_ref, m_ref, v_ref, d_ref, nm_ref, nv_ref):
        d_ref[...], nm_ref[...], nv_ref[...] = _adamw_update(w_ref[...], g_ref[...], m_ref[...], v_ref[...])

    spec = pl.BlockSpec((tr, C), lambda i: (i, 0))
    out = pl.pallas_call(
        body, name=name, grid=(R // tr,), in_specs=[spec] * 4, out_specs=[spec] * 3,
        out_shape=[jax.ShapeDtypeStruct((R, C), F32)] * 3, compiler_params=_cparams(("parallel",)),
    )(w2, g2, m2, v2)
    return tuple(a.reshape(shape) for a in out)


def _adamw_halves(w, g_own, g_peer, m, v, c_idx, *, name):
    L, rows, C = w.shape
    R = rows // 2
    tr = _row_tile(R, C)

    def body(c_ref, w_ref, go_ref, gp_ref, m_ref, v_ref, g_ref, d_ref, nm_ref, nv_ref):
        gv = jnp.where(pl.program_id(1) == c_ref[0], go_ref[...], gp_ref[...])
        g_ref[...] = gv
        d_ref[...], nm_ref[...], nv_ref[...] = _adamw_update(w_ref[...], gv, m_ref[...], v_ref[...])

    full = pl.BlockSpec((None, None, tr, C), lambda l, hh, i, c_ref: (l, hh, i, 0))
    half = pl.BlockSpec((None, tr, C), lambda l, hh, i, c_ref: (l, i, 0))
    grid_spec = pltpu.PrefetchScalarGridSpec(
        num_scalar_prefetch=1, grid=(L, 2, R // tr), in_specs=[full, half, half, full, full], out_specs=[full] * 4)
    split = lambda a: a.reshape(L, 2, R, C)
    out = pl.pallas_call(
        body, name=name, grid_spec=grid_spec, out_shape=[jax.ShapeDtypeStruct((L, 2, R, C), F32)] * 4,
        compiler_params=_cparams(("parallel", "parallel", "parallel")),
    )(c_idx, split(w), g_own, g_peer, split(m), split(v))
    return tuple(a.reshape(w.shape) for a in out)


def _sum_gathered(dm8, sm8):
    n_dev, Bl, R, D = dm8.shape
    Rs = sm8.shape[1]

    def body(dm_ref, sm_ref, ob_ref, os_ref):
        acc_b = jnp.zeros((R, D), F32)
        acc_s = jnp.zeros((Rs, D), F32)
        for d in range(n_dev):
            for b in range(Bl):
                acc_b = acc_b + dm_ref[d, b]
            acc_s = acc_s + sm_ref[d]
        ob_ref[...] = acc_b
        os_ref[...] = acc_s

    return pl.pallas_call(
        body, name="sum_gathered",
        out_shape=[jax.ShapeDtypeStruct((R, D), F32), jax.ShapeDtypeStruct((Rs, D), F32)],
        compiler_params=_cparams(None),
    )(dm8, sm8)


N_DEV = 8
N_CHIP = 4
ANY = pl.BlockSpec(memory_space=pl.ANY)
HBM = pl.BlockSpec(memory_space=pltpu.HBM)
SEM = pl.BlockSpec(memory_space=pltpu.SEMAPHORE)
DATAFLOW = pltpu.SideEffectType.DATAFLOW_SIDE_EFFECTING


def _mesh_pos():
    return lax.axis_index("x"), lax.axis_index("y"), lax.axis_index("c")


def _all_gather8(block, *, name, in_vmem):
    R, W = block.shape

    def body(x_ref, out_ref, send_sems, recv_sems, local_sem):
        x, y, c = _mesh_pos()
        me, sibling = (x, y, c), (x, y, 1 - c)
        chips = [(1 - x, y), (x, 1 - y), (1 - x, 1 - y)]

        def slot(px, py, pc):
            return out_ref.at[4 * px + 2 * py + pc]

        def copy(k, blk, to, src=None):
            return pltpu.make_async_remote_copy(
                src_ref=slot(*blk) if src is None else src, dst_ref=slot(*blk),
                send_sem=send_sems.at[k], recv_sem=recv_sems.at[k], device_id=to, device_id_type=MESH_ID)

        mine = pltpu.make_async_copy(x_ref, slot(*me), local_sem)
        mine.start()
        first = [copy(0, me, sibling, src=x_ref)]
        first += [copy(1 + j, me, (*chip, c), src=x_ref) for j, chip in enumerate(chips)]
        for cp in first:
            cp.start()
        passed = [copy(4 + j, (*chip, c), sibling) for j, chip in enumerate(chips)]
        for j, chip in enumerate(chips):
            copy(1 + j, (*chip, c), me).wait_recv()
            passed[j].start()
        copy(0, sibling, me).wait_recv()
        for j, chip in enumerate(chips):
            copy(4 + j, (*chip, 1 - c), me).wait_recv()
        for cp in first + passed:
            cp.wait_send()
        mine.wait()

    space = pl.BlockSpec(memory_space=pltpu.VMEM) if in_vmem else ANY
    return pl.pallas_call(
        body, name=name, out_shape=jax.ShapeDtypeStruct((N_DEV, R, W), block.dtype),
        in_specs=[space], out_specs=space,
        scratch_shapes=[pltpu.SemaphoreType.DMA((7,)), pltpu.SemaphoreType.DMA((7,)), pltpu.SemaphoreType.DMA],
        compiler_params=pltpu.CompilerParams(vmem_limit_bytes=VMEM_LIMIT_V7X),
    )(block)


def _comm_call(body, arrays, out_shapes, n_sems, *, name):
    return pl.pallas_call(
        body, name=name, out_shape=out_shapes, in_specs=[ANY] * len(arrays), out_specs=[ANY] * len(out_shapes),
        scratch_shapes=[pltpu.SemaphoreType.DMA((n_sems,)), pltpu.SemaphoreType.DMA((n_sems,)),
                        pltpu.SemaphoreType.DMA((len(arrays),))],
    )(*arrays)


def _gather_weights(shards, *, name):
    n = len(shards)

    def body(*refs):
        xs, outs = refs[:n], refs[n:2 * n]
        send_sems, recv_sems, local_sems = refs[2 * n:]
        x, y, c = _mesh_pos()
        me, sibling = (x, y, c), (x, y, 1 - c)
        chips = [(1 - x, y), (x, 1 - y), (1 - x, 1 - y)]
        waits = []
        for i in range(n):
            nl = shards[i].shape[0]
            own = xs[i].at[pl.ds(0, nl), c]

            def slot(px, py, pc, i=i, nl=nl):
                return outs[i].at[pl.ds(0, nl), 2 * px + py, pc]

            def copy(k, blk, to, src=None, i=i, slot=slot):
                return pltpu.make_async_remote_copy(
                    src_ref=slot(*blk) if src is None else src, dst_ref=slot(*blk),
                    send_sem=send_sems.at[7 * i + k], recv_sem=recv_sems.at[7 * i + k], device_id=to,
                    device_id_type=MESH_ID)

            mine = pltpu.make_async_copy(own, slot(*me), local_sems.at[i])
            mine.start()
            first = [copy(0, me, sibling, src=own)]
            first += [copy(1 + j, me, (*chip, c), src=own) for j, chip in enumerate(chips)]
            for cp in first:
                cp.start()
            waits.append((copy, mine, first))
        for copy, mine, first in waits:
            passed = [copy(4 + j, (*chip, c), sibling) for j, chip in enumerate(chips)]
            for j, chip in enumerate(chips):
                copy(1 + j, (*chip, c), me).wait_recv()
                passed[j].start()
            copy(0, sibling, me).wait_recv()
            for j, chip in enumerate(chips):
                copy(4 + j, (*chip, 1 - c), me).wait_recv()
            for cp in first + passed:
                cp.wait_send()
            mine.wait()

    out_shapes = [jax.ShapeDtypeStruct((s.shape[0], N_CHIP) + s.shape[1:], s.dtype) for s in shards]
    return _comm_call(body, shards, out_shapes, 7 * n, name=name)


def _place_own(shard, chip_idx, c_idx, *, name):
    n, _, rows, cols = shard.shape
    tr = _row_tile(rows, cols)

    def body(k_ref, c_ref, x_ref, o_ref):
        o_ref[...] = x_ref[...]

    grid_spec = pltpu.PrefetchScalarGridSpec(
        num_scalar_prefetch=2, grid=(n, rows // tr),
        in_specs=[pl.BlockSpec((None, None, tr, cols), lambda l, i, k_ref, c_ref: (l, c_ref[0], i, 0))],
        out_specs=pl.BlockSpec((None, None, None, tr, cols), lambda l, i, k_ref, c_ref: (l, k_ref[0], c_ref[0], i, 0)))
    return pl.pallas_call(
        body, name=name, grid_spec=grid_spec,
        out_shape=jax.ShapeDtypeStruct((n, N_CHIP, 2, rows, cols), shard.dtype),
        compiler_params=_cparams(("parallel", "parallel")),
    )(chip_idx, c_idx, shard)


def _gather_copies(x_refs, land_refs, send_sems, recv_sems):
    x, y, c = _mesh_pos()
    k_me = 2 * x + y
    targets = [(x, y, 1 - c), (1 - x, y, c), (x, 1 - y, c), (1 - x, 1 - y, c)]
    copies = []
    for i, (x_ref, land_ref) in enumerate(zip(x_refs, land_refs)):
        nl = x_ref.shape[0]
        for j, to in enumerate(targets):
            copies.append(pltpu.make_async_remote_copy(
                src_ref=x_ref.at[pl.ds(0, nl), c], dst_ref=land_ref.at[pl.ds(0, nl), k_me, c],
                send_sem=send_sems.at[4 * i + j], recv_sem=recv_sems.at[4 * i + j], device_id=to,
                device_id_type=MESH_ID))
    return copies


def _split_start(copies_fn, srcs, lands, after, *, name, sems_per_array):
    n = len(srcs)

    def body(*refs):
        send_sems, recv_sems = refs[2 * n + 1], refs[2 * n + 2]
        for cp in copies_fn(refs[:n], refs[n:2 * n], send_sems, recv_sems):
            cp.start()
        refs[-1][...] = jnp.zeros_like(refs[-1])

    operands = [pltpu.with_memory_space_constraint(a, pltpu.HBM) for a in list(srcs) + list(lands)]
    n_sems = sems_per_array * n
    out_shape = ([pltpu.SemaphoreType.DMA((n_sems,)), pltpu.SemaphoreType.DMA((n_sems,))]
                 + [pltpu.HBM(a.shape, a.dtype) for a in operands] + [jax.ShapeDtypeStruct((8, LANES), F32)])
    res = pl.pallas_call(
        body, name=name, out_shape=out_shape, in_specs=[HBM] * (2 * n) + [ANY],
        out_specs=[SEM, SEM] + [HBM] * (2 * n) + [pl.BlockSpec(memory_space=pltpu.VMEM)],
        input_output_aliases={i: 2 + i for i in range(2 * n)},
        compiler_params=pltpu.CompilerParams(has_side_effects=DATAFLOW),
    )(*operands, after)
    return res[0], res[1], list(res[2:2 + n]), list(res[2 + n:2 + 2 * n]), res[-1]


def _split_wait(copies_fn, send_sems, recv_sems, srcs, lands, after, *, name):
    n = len(srcs)

    def body(*refs):
        for cp in copies_fn(refs[:n], refs[n:2 * n], refs[2 * n], refs[2 * n + 1]):
            cp.wait_send()
            cp.wait_recv()

    res = pl.pallas_call(
        body, name=name, out_shape=[pltpu.HBM(a.shape, a.dtype) for a in list(srcs) + list(lands)],
        in_specs=[HBM] * (2 * n) + [SEM, SEM, ANY], out_specs=[HBM] * (2 * n),
        input_output_aliases={i: i for i in range(2 * n)},
        compiler_params=pltpu.CompilerParams(has_side_effects=DATAFLOW),
    )(*srcs, *lands, send_sems, recv_sems, after)
    return list(res[:n]), list(res[n:])


def _gather_forward(lands, *, name):
    n = len(lands)

    def body(*refs):
        xs = refs[:n]
        send_sems, recv_sems, _ = refs[2 * n:]
        x, y, c = _mesh_pos()
        chips = [(1 - x, y), (x, 1 - y), (1 - x, 1 - y)]
        copies = []
        for i in range(n):
            nl = lands[i].shape[0]
            for j, (cx, cy) in enumerate(chips):
                here = xs[i].at[pl.ds(0, nl), 2 * cx + cy, c]
                cp = pltpu.make_async_remote_copy(
                    src_ref=here, dst_ref=here, send_sem=send_sems.at[3 * i + j], recv_sem=recv_sems.at[3 * i + j],
                    device_id=(x, y, 1 - c), device_id_type=MESH_ID)
                cp.start()
                copies.append(cp)
        for cp in copies:
            cp.wait()

    return pl.pallas_call(
        body, name=name, out_shape=[jax.ShapeDtypeStruct(a.shape, a.dtype) for a in lands],
        in_specs=[ANY] * n, out_specs=[ANY] * n, input_output_aliases={i: i for i in range(n)},
        scratch_shapes=[pltpu.SemaphoreType.DMA((3 * n,)), pltpu.SemaphoreType.DMA((3 * n,)),
                        pltpu.SemaphoreType.DMA((1,))],
    )(*lands)


def _pair_copies(g_refs, land_refs, send_sems, recv_sems):
    x, y, c = _mesh_pos()
    copies = []
    for i, (g_ref, land_ref) in enumerate(zip(g_refs, land_refs)):
        nl, ns = g_ref.shape[:2]
        copies.append(pltpu.make_async_remote_copy(
            src_ref=g_ref.at[pl.ds(0, nl), pl.ds(0, ns), 1 - c], dst_ref=land_ref, send_sem=send_sems.at[i],
            recv_sem=recv_sems.at[i], device_id=(x, y, 1 - c), device_id_type=MESH_ID))
    return copies


def _pair_exchange(gs, *, name):
    n = len(gs)

    def body(*refs):
        send_sems, recv_sems, _ = refs[2 * n:]
        copies = _pair_copies(refs[:n], refs[n:2 * n], send_sems, recv_sems)
        for cp in copies:
            cp.start()
        for cp in copies:
            cp.wait()

    out_shapes = [jax.ShapeDtypeStruct(g.shape[:2] + g.shape[3:], g.dtype) for g in gs]
    return _comm_call(body, gs, out_shapes, n, name=name)


def _chip_copies(p_refs, land_refs, send_sems, recv_sems):
    x, y, c = _mesh_pos()
    k_me = 2 * x + y
    chips = [(1 - x, y), (x, 1 - y), (1 - x, 1 - y)]
    copies = []
    for i, (p_ref, land_ref) in enumerate(zip(p_refs, land_refs)):
        nl = p_ref.shape[0]
        for j, (cx, cy) in enumerate(chips):
            copies.append(pltpu.make_async_remote_copy(
                src_ref=p_ref.at[pl.ds(0, nl), 2 * cx + cy], dst_ref=land_ref.at[k_me],
                send_sem=send_sems.at[3 * i + j], recv_sem=recv_sems.at[3 * i + j],
                device_id=(cx, cy, c), device_id_type=MESH_ID))
    return copies


def _chip_landing(ps):
    return [lax.empty((p.shape[1], p.shape[0]) + p.shape[2:], p.dtype) for p in ps]


def _pair_swap(ss, *, name):
    n = len(ss)

    def body(*refs):
        xs, outs = refs[:n], refs[n:2 * n]
        send_sems, recv_sems, _ = refs[2 * n:]
        x, y, c = _mesh_pos()
        copies = []
        for i in range(n):
            cp = pltpu.make_async_remote_copy(src_ref=xs[i], dst_ref=outs[i], send_sem=send_sems.at[i],
                                              recv_sem=recv_sems.at[i], device_id=(x, y, 1 - c),
                                              device_id_type=MESH_ID)
            cp.start()
            copies.append(cp)
        for cp in copies:
            cp.wait()

    out_shapes = [jax.ShapeDtypeStruct(s.shape, s.dtype) for s in ss]
    return _comm_call(body, ss, out_shapes, n, name=name)


def _row_tile(rows, cols):
    tr = rows
    while tr * cols > 256 * 1024 and tr % 16 == 0:
        tr //= 2
    return tr


def _pair_add(g, recv, c_idx, *, name):
    n, ns, _, rows, W = g.shape
    tr = _row_tile(rows, W)

    def body(c_ref, g_ref, r_ref, o_ref):
        o_ref[...] = (g_ref[...] + r_ref[...]).astype(BF16)

    piece = pl.BlockSpec((None, tr, W), lambda p, i, c_ref: (p, i, 0))
    grid_spec = pltpu.PrefetchScalarGridSpec(
        num_scalar_prefetch=1, grid=(n * ns, rows // tr),
        in_specs=[pl.BlockSpec((None, None, tr, W), lambda p, i, c_ref: (p, c_ref[0], i, 0)), piece],
        out_specs=piece)
    out = pl.pallas_call(
        body, name=name, grid_spec=grid_spec, out_shape=jax.ShapeDtypeStruct((n * ns, rows, W), BF16),
        compiler_params=_cparams(("parallel", "parallel")),
    )(c_idx, g.reshape(n * ns, 2, rows, W), recv.reshape(n * ns, rows, W))
    return out.reshape(n, ns, rows, W)


def _sum_pieces(land, own, chip_idx, *, name):
    n, nl, A, W = land.shape
    tr = _row_tile(A, W)

    def body(k_ref, l_ref, o_ref, out_ref):
        acc = jnp.zeros(out_ref.shape, F32)
        for k in range(n):
            acc = acc + jnp.where(k == k_ref[0], o_ref[...], l_ref[k]).astype(F32)
        out_ref[...] = acc

    grid_spec = pltpu.PrefetchScalarGridSpec(
        num_scalar_prefetch=1, grid=(nl, A // tr),
        in_specs=[pl.BlockSpec((n, None, tr, W), lambda l, i, k_ref: (0, l, i, 0)),
                  pl.BlockSpec((None, None, tr, W), lambda l, i, k_ref: (l, k_ref[0], i, 0))],
        out_specs=pl.BlockSpec((None, tr, W), lambda l, i, k_ref: (l, i, 0)))
    return pl.pallas_call(
        body, name=name, grid_spec=grid_spec, out_shape=jax.ShapeDtypeStruct((nl, A, W), F32),
        compiler_params=_cparams(("parallel", "parallel")),
    )(chip_idx, land, own)


SMALL = ("norm_mix_g", "norm_mlp_g", "final_norm_g", "fox_b_f", "mla_q_norm_g", "mla_kv_norm_g")
WEIGHT_ORDER = ("ada_w", "ada_b", "norm_mix_g", "norm_mlp_g", "fox_w_in", "fox_b_f", "fox_w_out", "mla_w_dq",
                "mla_q_norm_g", "mla_w_uq", "mla_w_dkv", "mla_kv_norm_g", "mla_w_ukv", "mla_w_out", "mlp_w1",
                "mlp_w2", "final_norm_g")


def _small_rows(vals, D):
    rows = [vals["norm_mix_g"], vals["norm_mlp_g"], vals["final_norm_g"][None, :]]
    for n in ("fox_b_f", "mla_q_norm_g", "mla_kv_norm_g"):
        flat = vals[n].reshape(-1)
        assert flat.shape[0] <= D
        rows.append(jnp.pad(flat, (0, D - flat.shape[0]))[None, :])
    return jnp.concatenate(rows, axis=0)


def _small_unrows(rows, shapes):
    L = shapes["norm_mix_g"][0]
    out = {"norm_mix_g": rows[0:L], "norm_mlp_g": rows[L:2 * L], "final_norm_g": rows[2 * L]}
    for k, n in enumerate(("fox_b_f", "mla_q_norm_g", "mla_kv_norm_g")):
        size = int(np.prod(shapes[n]))
        out[n] = rows[2 * L + 1 + k, :size].reshape(shapes[n])
    return out


def kernel(x, c, positions, ada_w, ada_b, norm_mix_g, norm_mlp_g, fox_w_in, fox_b_f, fox_w_out, mla_w_dq, mla_q_norm_g, mla_w_uq, mla_w_dkv, mla_kv_norm_g, mla_w_ukv, mla_w_out, mlp_w1, mlp_w2, final_norm_g, loss_target, m_ada_w, m_ada_b, m_norm_mix_g, m_norm_mlp_g, m_fox_w_in, m_fox_b_f, m_fox_w_out, m_mla_w_dq, m_mla_q_norm_g, m_mla_w_uq, m_mla_w_dkv, m_mla_kv_norm_g, m_mla_w_ukv, m_mla_w_out, m_mlp_w1, m_mlp_w2, m_final_norm_g, v_ada_w, v_ada_b, v_norm_mix_g, v_norm_mlp_g, v_fox_w_in, v_fox_b_f, v_fox_w_out, v_mla_w_dq, v_mla_q_norm_g, v_mla_w_uq, v_mla_w_dkv, v_mla_kv_norm_g, v_mla_w_ukv, v_mla_w_out, v_mlp_w1, v_mlp_w2, v_final_norm_g):
    args = dict(locals())
    wts = {n: args[n] for n in WEIGHT_ORDER}
    mom = {n: args["m_" + n] for n in WEIGHT_ORDER}
    var = {n: args["v_" + n] for n in WEIGHT_ORDER}
    Bl, S, D = x.shape
    T = Bl * S
    L = ada_w.shape[0]
    C = ada_w.shape[2]
    mx, my, mc = _mesh_pos()
    chip = 2 * mx + my
    dev = 4 * mx + 2 * my + mc
    c_idx = jnp.reshape(mc, (1,)).astype(jnp.int32)
    chip_idx = jnp.reshape(chip, (1,)).astype(jnp.int32)
    small = {n: wts[n] for n in SMALL}
    L2, q_cols = mla_q_norm_g.shape
    n_fox_heads = fox_b_f.shape[1]

    shards = _shard_layouts(wts)
    groups = _comm_groups(L, L2)
    slots = _layer_slots(groups)

    def row_halves(a):
        return a.reshape(a.shape[:-2] + (2, a.shape[-2] // 2, a.shape[-1]))

    def whole_rows(a):
        return a.reshape(a.shape[:2] + (a.shape[2] * a.shape[3], a.shape[4]))

    part = {g: [row_halves(shards[n][s:s + cnt]) for n, s, cnt in entries] for g, entries in groups.items()}
    mix0 = _gather_weights(part["mix0"], name="gather_mix0")
    gather_sems, after = {}, mix0[0]
    for group in ("mlp0", "rest"):
        placed = [_place_own(a, chip_idx, c_idx, name=f"gather_place_{group}_{n}")
                  for a, (n, _, _) in zip(part[group], groups[group])]
        gather_sems[group] = _split_start(_gather_copies, part[group], placed, after, name=f"gather_{group}_start",
                                          sems_per_array=4)
        after = gather_sems[group][4]

    def layer_weights(w, group, arrays):
        for (n, s, cnt), a in zip(groups[group], arrays):
            for key, view in _weight_views(n, whole_rows(a), D, n_fox_heads).items():
                for l in range(cnt):
                    w[key][s + l] = (view, l)

    w = {key: [None] * L2
         for key in ("fox_qkv", "fox_f", "fox_in", "fox_out", "mla_down", "mla_uq", "mla_ukv", "mla_out")}
    w.update({key: [None] * L for key in ("mlp_w1", "mlp_w2")})
    layer_weights(w, "mix0", mix0)

    def gathered_now(group):
        def hook(x_now, w):
            _, landed = _split_wait(_gather_copies, *gather_sems[group][:4], x_now, name=f"gather_{group}_wait")
            layer_weights(w, group, _gather_forward(landed, name=f"gather_{group}_forward"))
            return w
        return hook

    c_pad = jnp.concatenate([c, jnp.pad(mla_q_norm_g, ((0, 8 - Bl - L2), (0, D - q_cols)))], axis=0)
    c8 = _all_gather8(c_pad, name="gather_c", in_vmem=True)
    c_all = c8[:, :Bl].reshape(N_DEV * Bl, D)
    qg4 = c8.reshape(N_CHIP, 2, 8, D)[:, 0, Bl:Bl + L2, :q_cols]
    small["mla_q_norm_g"] = jnp.transpose(qg4, (1, 0, 2)).reshape(L2, N_CHIP * q_cols)
    ada_b_cols = lax.dynamic_slice_in_dim(ada_b, chip * C, C, axis=1)[:, None, :]
    mod_cols = _ada_fwd(c_all, ada_w, ada_b_cols)
    mod8 = _all_gather8(mod_cols.reshape(L * N_DEV * Bl, C), name="gather_mod", in_vmem=True)
    mod4 = mod8.reshape(N_CHIP, 2, L, N_DEV * Bl, C)[:, 0]
    mod_me = lax.dynamic_slice_in_dim(mod4, dev * Bl, Bl, axis=2)
    mod = jnp.transpose(mod_me, (1, 2, 0, 3)).reshape(L, Bl, 6, D)
    mod = jnp.transpose(mod, (0, 2, 1, 3))[:, :, :, None, :]

    w.update(_small_layouts(small))
    mod = mod + after[0, 0]
    pending = {}

    def grad_pieces(group, g_now):
        out = []
        for n, s, cnt in groups[group]:
            qkv_f = [g_now["fox_in"][j] for j in range(s, s + cnt)] if n == "fox_in" else None
            stacked_g = None if n == "fox_in" else g_now[n][group]
            out.append(row_halves(_grad_pieces(n, stacked_g, qkv_f, n_fox_heads, N_CHIP)))
        return out

    def pair_added(group, big, sibling):
        return [_pair_add(a, r, c_idx, name=f"grad_pair_add_{group}_{n}")
                for (n, _, _), a, r in zip(groups[group], big, sibling)]

    def exchange_start(group, ps, after=None):
        pending[group] = _split_start(_chip_copies, ps, _chip_landing(ps), chip_idx if after is None else after,
                                      name=f"grad_exchange_{group}_start", sems_per_array=3)
        return pending[group][4]

    def bwd_layer0(g_now):
        big = grad_pieces("rest", g_now)
        landing = [lax.empty(a.shape[:2] + a.shape[3:], a.dtype) for a in big]
        pending["rest_pair"] = _split_start(_pair_copies, big, landing, chip_idx, name="grad_pair_rest_start",
                                            sems_per_array=1)
        return pending["rest_pair"][4]

    def bwd_mix0(g_now):
        send_sems, recv_sems, big, landed, _ = pending["rest_pair"]
        big, landed = _split_wait(_pair_copies, send_sems, recv_sems, big, landed, g_now["mlp_w1"]["mlp0"],
                                  name="grad_pair_rest_wait")
        started = exchange_start("rest", pair_added("rest", big, landed))
        big = grad_pieces("mlp0", g_now)
        return exchange_start("mlp0", pair_added("mlp0", big, _pair_exchange(big, name="grad_pair_exchange_mlp0")),
                              after=started)

    half = ROPE_DIM // 2
    inv_freq = ROPE_THETA ** (-jnp.arange(0, ROPE_DIM, 2, dtype=F32) / ROPE_DIM)
    lane = np.arange(LANES)
    inv_freq_row = jnp.tile(inv_freq, LANES // half)[None, :]
    sign_row = jnp.asarray(np.where(lane < 2 * ROPE_DIM, np.where(lane % ROPE_DIM < half, -1.0, 1.0), 0.0), F32)[None, :]
    pos_f = positions.astype(F32).reshape(T, 1)
    loss_row, grad_x, dmod, g = _local_step(x.reshape(T, D), loss_target.reshape(T, D), pos_f, inv_freq_row, sign_row,
                                            mod, w, slots, S=S,
                                            hooks={"fwd_mlp0": gathered_now("mlp0"), "fwd_layer1": gathered_now("rest"),
                                                   "bwd_layer0": bwd_layer0, "bwd_mix0": bwd_mix0})
    g_small = _small_grads(g, n_fox_heads)
    big = grad_pieces("mix0", g)
    exchange_start("mix0", pair_added("mix0", big, _pair_exchange(big, name="grad_pair_exchange_mix0")))

    Rs = -(-(2 * L + 5) // 8) * 8
    srows = jnp.concatenate([_small_rows(g_small, D), jnp.pad(loss_row, ((0, 0), (0, D - LANES)))], axis=0)
    srows = jnp.pad(srows, ((0, Rs - srows.shape[0]), (0, 0)))
    drows = jnp.transpose(dmod[:, :, :, 0, :], (2, 0, 1, 3)).reshape(Bl * L * 6, D)
    both8 = _all_gather8(jnp.concatenate([drows, srows], axis=0), name="gather_small", in_vmem=True)
    dm8 = both8[:, :Bl * L * 6].reshape(N_DEV, Bl, L * 6, D)
    sm8 = both8[:, Bl * L * 6:]
    adb_rows, small_sum = _sum_gathered(dm8, sm8)
    grad_ada_b = adb_rows.reshape(L, 6 * D)
    loss = small_sum[2 * L + 4, 0]
    small_shapes = {n: (wts[n].shape if n != "mla_q_norm_g" else (wts[n].shape[0], N_CHIP * q_cols)) for n in SMALL}
    gs = _small_unrows(small_sum, small_shapes)
    gs["mla_q_norm_g"] = lax.dynamic_slice_in_dim(gs["mla_q_norm_g"], chip * q_cols, q_cols, axis=1)

    dmod16 = jnp.transpose(dm8.reshape(N_DEV, Bl, L, 6 * D), (2, 0, 1, 3)).reshape(L, N_DEV * Bl, 6 * D)
    dmod_cols = lax.dynamic_slice_in_dim(dmod16, chip * C, C, axis=2)
    grad_ada_w = _ada_bwd(c_all, dmod_cols)

    grads = dict(gs)
    grads["ada_w"] = grad_ada_w
    grads["ada_b"] = grad_ada_b
    delta, new_m, new_v = {}, {}, {}
    for n in ("ada_w", "ada_b"):
        delta[n], new_m[n], new_v[n] = _adamw(wts[n], grads[n], mom[n], var[n], name=f"adamw_{n}")
    shard_small_shapes = {n: wts[n].shape for n in SMALL}
    packs = [jnp.pad(_small_rows({n: src[n] for n in SMALL}, D), ((0, Rs - 2 * L - 4), (0, 0)))
             for src in (wts, grads, mom, var)]
    for dst, rows in zip((delta, new_m, new_v), _adamw(*packs, name="adamw_small")):
        dst.update(_small_unrows(rows, shard_small_shapes))

    halves = {}
    for group, after in (("rest", grad_x), ("mlp0", grad_x), ("mix0", delta["ada_w"])):
        send_sems, recv_sems, ps, lands, _ = pending[group]
        ps, lands = _split_wait(_chip_copies, send_sems, recv_sems, ps, lands, after, name=f"grad_exchange_{group}_wait")
        sums = [_sum_pieces(ld, p, chip_idx, name=f"grad_sum_{group}_{n}")
                for (n, _, _), ld, p in zip(groups[group], lands, ps)]
        swapped = _pair_swap(sums, name=f"grad_pair_swap_{group}")
        for (n, _, _), a, b in zip(groups[group], sums, swapped):
            halves[(n, group)] = (a, b)

    def all_layers(n, which):
        return jnp.concatenate([halves[(n, grp)][which] for grp in groups if (n, grp) in halves], axis=0)

    own = {n: all_layers(n, 0) for n in GATHERED}
    peer = {n: all_layers(n, 1) for n in GATHERED}
    for nat, n in (("fox_w_in", "fox_in"), ("fox_w_out", "fox_out"), ("mla_w_out", "mla_out"), ("mlp_w1", "mlp_w1"),
                   ("mlp_w2", "mlp_w2")):
        cols = wts[nat].shape[-1]
        res = _adamw_halves(_pad_lanes(wts[nat]), own[n], peer[n], _pad_lanes(mom[nat]), _pad_lanes(var[nat]), c_idx,
                            name=f"adamw_{nat}")
        grads[nat], delta[nat], new_m[nat], new_v[nat] = (a[..., :cols] for a in res)
    joined = {n: jnp.concatenate([jnp.where(mc == 0, own[n], peer[n]), jnp.where(mc == 0, peer[n], own[n])], axis=1)
              for n in ("mla_down", "mla_uq", "mla_ukv")}
    rq = mla_w_dq.shape[-1]
    grads["mla_w_dq"] = joined["mla_down"][:, :, :rq]
    grads["mla_w_dkv"] = joined["mla_down"][:, :, rq:rq + KV_RANK + ROPE_DIM]
    grads["mla_w_uq"] = jax.vmap(_uq_from_pairs)(joined["mla_uq"])
    grads["mla_w_ukv"] = jax.vmap(_ukv_from_pairs)(joined["mla_ukv"])
    for n in ("mla_w_dq", "mla_w_dkv", "mla_w_uq", "mla_w_ukv"):
        delta[n], new_m[n], new_v[n] = _adamw(wts[n], grads[n], mom[n], var[n], name=f"adamw_{n}")

    return (loss, grad_x.reshape(Bl, S, D), *[grads[n] for n in WEIGHT_ORDER], *[delta[n] for n in WEIGHT_ORDER],
            *[new_m[n] for n in WEIGHT_ORDER], *[new_v[n] for n in WEIGHT_ORDER])
```
